```python
import jax, jax.numpy as jnp
from jax import lax
import numpy as np

D_MODEL = 1024
BATCH = 8
SEQ = 8192
DEPTH = 1

HEAD_DIM = 64
N_Q_HEADS = 8
N_KV_HEADS = 2
Q_PER_KV = N_Q_HEADS // N_KV_HEADS
ATTN_WIDTH = N_Q_HEADS * HEAD_DIM
KV_WIDTH = N_KV_HEADS * HEAD_DIM
WINDOW = 128
BLOCK = 128
N_BUCKETS = 32
MAX_DISTANCE = 128

SSM_HEAD_DIM = 64
SSM_HEADS = 8
SSM_GROUPS = 2
HEADS_PER_GROUP = SSM_HEADS // SSM_GROUPS
SSM_WIDTH = SSM_HEADS * SSM_HEAD_DIM
D_STATE = 128
CONV_K = 4
CHUNK = 128
XBC_WIDTH = SSM_WIDTH + 2 * SSM_GROUPS * D_STATE

MIX_WIDTH = ATTN_WIDTH + SSM_WIDTH
IN_WIDTH = ATTN_WIDTH + 2 * KV_WIDTH + SSM_WIDTH + XBC_WIDTH + SSM_HEADS
D_FF = -(-8 * D_MODEL // (3 * 256)) * 256
EPS = 1e-6

kernel_name = "hymba_swa_sink_ssd_adaln_block"


def rmsnorm(x, g):
    xf = x.astype(jnp.float32)
    y = xf * lax.rsqrt(jnp.mean(xf * xf, axis=-1, keepdims=True) + EPS)
    return (y * g.astype(jnp.float32)).astype(x.dtype)


def t5_buckets(dist):
    n = np.maximum(dist, 0)
    max_exact = N_BUCKETS // 2
    large = max_exact + (np.log(np.maximum(n, 1) / max_exact) / np.log(MAX_DISTANCE / max_exact)
                         * (N_BUCKETS - max_exact)).astype(np.int32)
    large = np.minimum(large, N_BUCKETS - 1)
    return np.where(n < max_exact, n, large).astype(np.int32)


def sliding_window_attention(q, k, v, sinks, rel_bias):
    b, s, _ = q.shape
    nb = s // BLOCK
    qb = q.reshape(b, nb, BLOCK, N_KV_HEADS, Q_PER_KV, HEAD_DIM)

    def band(t):
        t = t.reshape(b, s, N_KV_HEADS, HEAD_DIM)
        t = jnp.pad(t, ((0, 0), (BLOCK, 0), (0, 0), (0, 0)))
        t = t.reshape(b, nb + 1, BLOCK, N_KV_HEADS, HEAD_DIM)
        return jnp.concatenate([t[:, :-1], t[:, 1:]], axis=2)

    kb, vb = band(k), band(v)
    dist = np.arange(BLOCK)[:, None] + BLOCK - np.arange(2 * BLOCK)[None, :]
    key_pos = np.arange(nb)[:, None] * BLOCK - BLOCK + np.arange(2 * BLOCK)[None, :]
    mask = ((dist >= 0) & (dist < WINDOW))[None] & (key_pos >= 0)[:, None, :]
    mask = mask.reshape(nb, 1, 1, BLOCK, 2 * BLOCK)
    bias = rel_bias.astype(jnp.float32)[t5_buckets(dist)]
    bias = jnp.transpose(bias, (2, 0, 1)).reshape(N_KV_HEADS, Q_PER_KV, BLOCK, 2 * BLOCK)

    scores = jnp.einsum("bnqkgd,bnskd->bnkgqs", qb, kb).astype(jnp.float32)
    scores = scores * (HEAD_DIM ** -0.5) + bias
    scores = jnp.where(mask, scores, -jnp.inf)
    sink = sinks.astype(jnp.float32).reshape(N_KV_HEADS, Q_PER_KV, 1, 1)
    m = jnp.maximum(jnp.max(scores, axis=-1, keepdims=True), sink)
    p = jnp.exp(scores - m)
    denom = jnp.sum(p, axis=-1, keepdims=True) + jnp.exp(sink - m)
    out = jnp.einsum("bnkgqs,bnskd->bnkgqd", p, vb.astype(jnp.float32)) / denom
    out = jnp.transpose(out, (0, 1, 4, 2, 3, 5)).reshape(b, s, ATTN_WIDTH)
    return out.astype(q.dtype)


def ssd_scan(xs, dt, A, Bm, Cm, D_skip):
    b, s = xs.shape[:2]
    nc = s // CHUNK
    xs = xs.astype(jnp.float32)
    xdt = xs * dt[..., None]
    xc = xdt.reshape(b, nc, CHUNK, SSM_GROUPS, HEADS_PER_GROUP, SSM_HEAD_DIM)
    Bc = Bm.astype(jnp.float32).reshape(b, nc, CHUNK, SSM_GROUPS, D_STATE)
    Cc = Cm.astype(jnp.float32).reshape(b, nc, CHUNK, SSM_GROUPS, D_STATE)
    dtA = (dt * A).reshape(b, nc, CHUNK, SSM_GROUPS, HEADS_PER_GROUP)
    Acs = jnp.cumsum(jnp.moveaxis(dtA, 2, -1), axis=-1)

    causal = np.tril(np.ones((CHUNK, CHUNK), dtype=bool))
    seg = Acs[..., :, None] - Acs[..., None, :]
    Lmat = jnp.exp(jnp.where(causal, seg, -jnp.inf))
    CB = jnp.einsum("bclgn,bcsgn->bcgls", Cc, Bc)
    W = CB[:, :, :, None] * Lmat
    y_diag = jnp.einsum("bcgrls,bcsgrp->bclgrp", W, xc)

    decay_states = jnp.exp(Acs[..., -1:] - Acs)
    states = jnp.einsum("bclgn,bcgrl,bclgrp->bcgrpn", Bc, decay_states, xc)
    chunk_decay = jnp.exp(Acs[..., -1])

    def step(h, inp):
        s_c, d_c = inp
        return h * d_c[..., None, None] + s_c, h

    h0 = jnp.zeros((b, SSM_GROUPS, HEADS_PER_GROUP, SSM_HEAD_DIM, D_STATE), jnp.float32)
    _, prev = lax.scan(step, h0, (jnp.moveaxis(states, 1, 0), jnp.moveaxis(chunk_decay, 1, 0)))
    prev = jnp.moveaxis(prev, 0, 1)
    y_off = jnp.einsum("bclgn,bcgrpn,bcgrl->bclgrp", Cc, prev, jnp.exp(Acs))

    y = (y_diag + y_off).reshape(b, s, SSM_GROUPS, HEADS_PER_GROUP, SSM_HEAD_DIM)
    y = y + D_skip.astype(jnp.float32)[:, :, None] * xs
    return y


def hybrid_mixer(h, w_in, conv_w, conv_b, dt_bias, A_log, D_skip, sinks,
                 attn_out_norm, ssm_out_norm, w_o, rel_bias):
    b, s, _ = h.shape
    proj = h @ w_in
    o1 = ATTN_WIDTH
    o2 = o1 + KV_WIDTH
    o3 = o2 + KV_WIDTH
    o4 = o3 + SSM_WIDTH
    o5 = o4 + XBC_WIDTH
    q, k, v = proj[..., :o1], proj[..., o1:o2], proj[..., o2:o3]
    z, xbc, dt_raw = proj[..., o3:o4], proj[..., o4:o5], proj[..., o5:]

    y_attn = sliding_window_attention(q, k, v, sinks, rel_bias)
    y_attn = rmsnorm(y_attn, attn_out_norm)

    xbc = lax.conv_general_dilated(xbc, conv_w[:, None, :], window_strides=(1,),
                                   padding=[(CONV_K - 1, 0)],
                                   dimension_numbers=("NWC", "WIO", "NWC"),
                                   feature_group_count=XBC_WIDTH)
    xbc = jax.nn.silu(xbc + conv_b)
    xs = xbc[..., :SSM_WIDTH].reshape(b, s, SSM_GROUPS, HEADS_PER_GROUP, SSM_HEAD_DIM)
    Bm = xbc[..., SSM_WIDTH:SSM_WIDTH + SSM_GROUPS * D_STATE].reshape(b, s, SSM_GROUPS, D_STATE)
    Cm = xbc[..., SSM_WIDTH + SSM_GROUPS * D_STATE:].reshape(b, s, SSM_GROUPS, D_STATE)
    dt = jax.nn.softplus(dt_raw.astype(jnp.float32) + dt_bias.astype(jnp.float32))
    dt = dt.reshape(b, s, SSM_GROUPS, HEADS_PER_GROUP)
    A = -jnp.exp(A_log.astype(jnp.float32)).reshape(SSM_GROUPS, HEADS_PER_GROUP)
    y_ssm = ssd_scan(xs, dt, A, Bm, Cm, D_skip.reshape(SSM_GROUPS, HEADS_PER_GROUP))
    y_ssm = y_ssm.reshape(b, s, SSM_GROUPS, SSM_WIDTH // SSM_GROUPS)
    gz = jax.nn.silu(z.astype(jnp.float32)).reshape(b, s, SSM_GROUPS, SSM_WIDTH // SSM_GROUPS)
    y_ssm = rmsnorm(y_ssm * gz, ssm_out_norm.reshape(SSM_GROUPS, SSM_WIDTH // SSM_GROUPS))
    y_ssm = y_ssm.reshape(b, s, SSM_WIDTH).astype(h.dtype)

    return jnp.concatenate([y_attn, y_ssm], axis=-1) @ w_o


def swiglu(h, w_gate_up, w_down):
    gu = h @ w_gate_up
    g, u = gu[..., :D_FF], gu[..., D_FF:]
    return (jax.nn.silu(g) * u) @ w_down


def _fwd_setup_inputs(seed: int = 0) -> dict:
    key = jax.random.key(seed)
    ks = jax.random.split(key, 24)
    f32 = jnp.float32
    nrm = lambda k, shape, sc: jax.random.normal(k, shape, f32) * sc
    dt = jnp.exp(jax.random.uniform(ks[8], (DEPTH, SSM_HEADS), f32)
                 * (jnp.log(0.1) - jnp.log(0.001)) + jnp.log(0.001))
    return {
        "x": nrm(ks[0], (BATCH, SEQ, D_MODEL), 1.0),
        "c": nrm(ks[1], (BATCH, D_MODEL), 1.0),
        "ada_w": nrm(ks[2], (DEPTH, D_MODEL, 6 * D_MODEL), D_MODEL ** -0.5),
        "ada_b": nrm(ks[3], (DEPTH, 6 * D_MODEL), 0.01),
        "norm1": 1.0 + nrm(ks[4], (DEPTH, D_MODEL), 0.01),
        "w_in": nrm(ks[5], (DEPTH, D_MODEL, IN_WIDTH), D_MODEL ** -0.5),
        "conv_w": nrm(ks[6], (DEPTH, CONV_K, XBC_WIDTH), CONV_K ** -0.5),
        "conv_b": nrm(ks[7], (DEPTH, XBC_WIDTH), 0.01),
        "dt_bias": dt + jnp.log(-jnp.expm1(-dt)),
        "A_log": jnp.log(jax.random.uniform(ks[9], (DEPTH, SSM_HEADS), f32, 1.0, 16.0)),
        "D_skip": 1.0 + nrm(ks[10], (DEPTH, SSM_HEADS), 0.1),
        "sinks": nrm(ks[11], (DEPTH, N_Q_HEADS), 0.5),
        "attn_out_norm": 1.0 + nrm(ks[12], (DEPTH, ATTN_WIDTH), 0.01),
        "ssm_out_norm": 1.0 + nrm(ks[13], (DEPTH, SSM_WIDTH), 0.01),
        "w_o": nrm(ks[14], (DEPTH, MIX_WIDTH, D_MODEL), MIX_WIDTH ** -0.5),
        "norm2": 1.0 + nrm(ks[15], (DEPTH, D_MODEL), 0.01),
        "w_gate_up": nrm(ks[16], (DEPTH, D_MODEL, 2 * D_FF), D_MODEL ** -0.5),
        "w_down": nrm(ks[17], (DEPTH, D_FF, D_MODEL), D_FF ** -0.5),
        "rel_bias": nrm(ks[18], (N_BUCKETS, N_Q_HEADS), 0.5),
        "final_norm": 1.0 + nrm(ks[19], (D_MODEL,), 0.01),
    }


def _fwd_reference(x, c, ada_w, ada_b, norm1, w_in, conv_w, conv_b, dt_bias, A_log, D_skip,
              sinks, attn_out_norm, ssm_out_norm, w_o, norm2, w_gate_up, w_down,
              rel_bias, final_norm):
    cond = jax.nn.silu(c)
    for l in range(DEPTH):
        mod = (cond @ ada_w[l] + ada_b[l])[:, None, :]
        shift1, scale1, gate1, shift2, scale2, gate2 = jnp.split(mod, 6, axis=-1)
        h = rmsnorm(x, norm1[l]) * (1.0 + scale1) + shift1
        x = x + gate1 * hybrid_mixer(h, w_in[l], conv_w[l], conv_b[l], dt_bias[l], A_log[l],
                                     D_skip[l], sinks[l], attn_out_norm[l], ssm_out_norm[l],
                                     w_o[l], rel_bias)
        h = rmsnorm(x, norm2[l]) * (1.0 + scale2) + shift2
        x = x + gate2 * swiglu(h, w_gate_up[l], w_down[l])
    return rmsnorm(x, final_norm)


import jax as _jax
import jax.numpy as _jnp

TWIN_FORMAT = 'train_step'
FWD_PARAMS = ['x', 'c', 'ada_w', 'ada_b', 'norm1', 'w_in', 'conv_w', 'conv_b', 'dt_bias', 'A_log', 'D_skip', 'sinks', 'attn_out_norm', 'ssm_out_norm', 'w_o', 'norm2', 'w_gate_up', 'w_down', 'rel_bias', 'final_norm']
TWIN_WEIGHTS = ['ada_w', 'ada_b', 'norm1', 'w_in', 'conv_w', 'conv_b', 'dt_bias', 'A_log', 'D_skip', 'sinks', 'attn_out_norm', 'ssm_out_norm', 'w_o', 'norm2', 'w_gate_up', 'w_down', 'rel_bias', 'final_norm']
TWIN_DIFF_INPUT = 'x'
TWIN_INPUTS = ['x', 'c', 'ada_w', 'ada_b', 'norm1', 'w_in', 'conv_w', 'conv_b', 'dt_bias', 'A_log', 'D_skip', 'sinks', 'attn_out_norm', 'ssm_out_norm', 'w_o', 'norm2', 'w_gate_up', 'w_down', 'rel_bias', 'final_norm', 'loss_target', 'm_ada_w', 'm_ada_b', 'm_norm1', 'm_w_in', 'm_conv_w', 'm_conv_b', 'm_dt_bias', 'm_A_log', 'm_D_skip', 'm_sinks', 'm_attn_out_norm', 'm_ssm_out_norm', 'm_w_o', 'm_norm2', 'm_w_gate_up', 'm_w_down', 'm_rel_bias', 'm_final_norm', 'v_ada_w', 'v_ada_b', 'v_norm1', 'v_w_in', 'v_conv_w', 'v_conv_b', 'v_dt_bias', 'v_A_log', 'v_D_skip', 'v_sinks', 'v_attn_out_norm', 'v_ssm_out_norm', 'v_w_o', 'v_norm2', 'v_w_gate_up', 'v_w_down', 'v_rel_bias', 'v_final_norm']
TWIN_OUTPUTS = ['loss', 'grad_x', 'grad_ada_w', 'grad_ada_b', 'grad_norm1', 'grad_w_in', 'grad_conv_w', 'grad_conv_b', 'grad_dt_bias', 'grad_A_log', 'grad_D_skip', 'grad_sinks', 'grad_attn_out_norm', 'grad_ssm_out_norm', 'grad_w_o', 'grad_norm2', 'grad_w_gate_up', 'grad_w_down', 'grad_rel_bias', 'grad_final_norm', 'delta_ada_w', 'delta_ada_b', 'delta_norm1', 'delta_w_in', 'delta_conv_w', 'delta_conv_b', 'delta_dt_bias', 'delta_A_log', 'delta_D_skip', 'delta_sinks', 'delta_attn_out_norm', 'delta_ssm_out_norm', 'delta_w_o', 'delta_norm2', 'delta_w_gate_up', 'delta_w_down', 'delta_rel_bias', 'delta_final_norm', 'new_m_ada_w', 'new_m_ada_b', 'new_m_norm1', 'new_m_w_in', 'new_m_conv_w', 'new_m_conv_b', 'new_m_dt_bias', 'new_m_A_log', 'new_m_D_skip', 'new_m_sinks', 'new_m_attn_out_norm', 'new_m_ssm_out_norm', 'new_m_w_o', 'new_m_norm2', 'new_m_w_gate_up', 'new_m_w_down', 'new_m_rel_bias', 'new_m_final_norm', 'new_v_ada_w', 'new_v_ada_b', 'new_v_norm1', 'new_v_w_in', 'new_v_conv_w', 'new_v_conv_b', 'new_v_dt_bias', 'new_v_A_log', 'new_v_D_skip', 'new_v_sinks', 'new_v_attn_out_norm', 'new_v_ssm_out_norm', 'new_v_w_o', 'new_v_norm2', 'new_v_w_gate_up', 'new_v_w_down', 'new_v_rel_bias', 'new_v_final_norm']
TWIN_LEAF_KINDS = {'loss': 'loss', 'grad_x': 'grad_x', 'grad_ada_w': 'grad_w', 'grad_ada_b': 'grad_w', 'grad_norm1': 'grad_w', 'grad_w_in': 'grad_w', 'grad_conv_w': 'grad_w', 'grad_conv_b': 'grad_w', 'grad_dt_bias': 'grad_w', 'grad_A_log': 'grad_w', 'grad_D_skip': 'grad_w', 'grad_sinks': 'grad_w', 'grad_attn_out_norm': 'grad_w', 'grad_ssm_out_norm': 'grad_w', 'grad_w_o': 'grad_w', 'grad_norm2': 'grad_w', 'grad_w_gate_up': 'grad_w', 'grad_w_down': 'grad_w', 'grad_rel_bias': 'grad_w', 'grad_final_norm': 'grad_w', 'delta_ada_w': 'delta_w', 'delta_ada_b': 'delta_w', 'delta_norm1': 'delta_w', 'delta_w_in': 'delta_w', 'delta_conv_w': 'delta_w', 'delta_conv_b': 'delta_w', 'delta_dt_bias': 'delta_w', 'delta_A_log': 'delta_w', 'delta_D_skip': 'delta_w', 'delta_sinks': 'delta_w', 'delta_attn_out_norm': 'delta_w', 'delta_ssm_out_norm': 'delta_w', 'delta_w_o': 'delta_w', 'delta_norm2': 'delta_w', 'delta_w_gate_up': 'delta_w', 'delta_w_down': 'delta_w', 'delta_rel_bias': 'delta_w', 'delta_final_norm': 'delta_w', 'new_m_ada_w': 'new_m', 'new_m_ada_b': 'new_m', 'new_m_norm1': 'new_m', 'new_m_w_in': 'new_m', 'new_m_conv_w': 'new_m', 'new_m_conv_b': 'new_m', 'new_m_dt_bias': 'new_m', 'new_m_A_log': 'new_m', 'new_m_D_skip': 'new_m', 'new_m_sinks': 'new_m', 'new_m_attn_out_norm': 'new_m', 'new_m_ssm_out_norm': 'new_m', 'new_m_w_o': 'new_m', 'new_m_norm2': 'new_m', 'new_m_w_gate_up': 'new_m', 'new_m_w_down': 'new_m', 'new_m_rel_bias': 'new_m', 'new_m_final_norm': 'new_m', 'new_v_ada_w': 'new_v', 'new_v_ada_b': 'new_v', 'new_v_norm1': 'new_v', 'new_v_w_in': 'new_v', 'new_v_conv_w': 'new_v', 'new_v_conv_b': 'new_v', 'new_v_dt_bias': 'new_v', 'new_v_A_log': 'new_v', 'new_v_D_skip': 'new_v', 'new_v_sinks': 'new_v', 'new_v_attn_out_norm': 'new_v', 'new_v_ssm_out_norm': 'new_v', 'new_v_w_o': 'new_v', 'new_v_norm2': 'new_v', 'new_v_w_gate_up': 'new_v', 'new_v_w_down': 'new_v', 'new_v_rel_bias': 'new_v', 'new_v_final_norm': 'new_v'}


def _forward(args):
    return _fwd_reference(*[args[k] for k in FWD_PARAMS])


def _output_shape():
    def fwd():
        inp = _fwd_setup_inputs(0)
        return _fwd_reference(*[inp[k] for k in FWD_PARAMS])
    out = _jax.eval_shape(fwd)
    return out.shape, out.dtype

N_MICROBATCH = 1
ADAM_LR = 0.001
ADAM_B1 = 0.9
ADAM_B2 = 0.999
ADAM_EPS = 1e-08
ADAM_WD = 0.01
ADAM_STEP = 10
PER_EXAMPLE_BATCH_AXIS = {'x': 0, 'c': 0, 'loss_target': 0}
SHARED_INPUTS = []
_WEIGHT_DTYPES = {'ada_w': _jnp.float32, 'ada_b': _jnp.float32, 'norm1': _jnp.float32, 'w_in': _jnp.float32, 'conv_w': _jnp.float32, 'conv_b': _jnp.float32, 'dt_bias': _jnp.float32, 'A_log': _jnp.float32, 'D_skip': _jnp.float32, 'sinks': _jnp.float32, 'attn_out_norm': _jnp.float32, 'ssm_out_norm': _jnp.float32, 'w_o': _jnp.float32, 'norm2': _jnp.float32, 'w_gate_up': _jnp.float32, 'w_down': _jnp.float32, 'rel_bias': _jnp.float32, 'final_norm': _jnp.float32}
MOMENT_SCALE = {'ada_w': 1.154847e-01, 'ada_b': 1.985076e-01, 'norm1': 1.645043e-01, 'w_in': 1.319986e-01, 'conv_w': 1.027328e-01, 'conv_b': 9.435971e-02, 'dt_bias': 2.228592e-01, 'A_log': 3.862165e-01, 'D_skip': 6.525035e-01, 'sinks': 3.959654e-02, 'attn_out_norm': 1.443879e-01, 'ssm_out_norm': 1.346260e-01, 'w_o': 1.441731e-01, 'norm2': 1.618102e-01, 'w_gate_up': 7.625506e-02, 'w_down': 1.245055e-01, 'rel_bias': 8.108249e-02, 'final_norm': 6.521534e+01}


def _to_microbatches(a, axis):
    t = _jnp.moveaxis(a, axis, 0)
    t = t.reshape((N_MICROBATCH, t.shape[0] // N_MICROBATCH) + t.shape[1:])
    return _jnp.moveaxis(t, 1, axis + 1)


def setup_inputs(seed: int = 0) -> dict:
    inp = _fwd_setup_inputs(seed)
    key = _jax.random.fold_in(_jax.random.key(seed), 7919)
    shape, _ = _output_shape()
    out = dict(inp)
    out["loss_target"] = _jax.random.normal(_jax.random.fold_in(key, 0), shape, _jnp.float32)
    for i, name in enumerate(TWIN_WEIGHTS):
        w = inp[name].astype(_jnp.float32)
        if MOMENT_SCALE is None:
            s = _jnp.sqrt(_jnp.mean(_jnp.square(w)) + 1e-30)
        else:
            s = MOMENT_SCALE[name]
        km, kv = _jax.random.split(_jax.random.fold_in(key, i + 1))
        out[name] = w
        out["m_" + name] = s * _jax.random.normal(km, w.shape, _jnp.float32)
        out["v_" + name] = (s * s) * _jax.random.uniform(kv, w.shape, _jnp.float32, 0.5, 1.5)
    if N_MICROBATCH > 1:
        for name, axis in PER_EXAMPLE_BATCH_AXIS.items():
            out[name] = _to_microbatches(out[name], axis)
    return {'x': out['x'], 'c': out['c'], 'ada_w': out['ada_w'], 'ada_b': out['ada_b'], 'norm1': out['norm1'], 'w_in': out['w_in'], 'conv_w': out['conv_w'], 'conv_b': out['conv_b'], 'dt_bias': out['dt_bias'], 'A_log': out['A_log'], 'D_skip': out['D_skip'], 'sinks': out['sinks'], 'attn_out_norm': out['attn_out_norm'], 'ssm_out_norm': out['ssm_out_norm'], 'w_o': out['w_o'], 'norm2': out['norm2'], 'w_gate_up': out['w_gate_up'], 'w_down': out['w_down'], 'rel_bias': out['rel_bias'], 'final_norm': out['final_norm'], 'loss_target': out['loss_target'], 'm_ada_w': out['m_ada_w'], 'm_ada_b': out['m_ada_b'], 'm_norm1': out['m_norm1'], 'm_w_in': out['m_w_in'], 'm_conv_w': out['m_conv_w'], 'm_conv_b': out['m_conv_b'], 'm_dt_bias': out['m_dt_bias'], 'm_A_log': out['m_A_log'], 'm_D_skip': out['m_D_skip'], 'm_sinks': out['m_sinks'], 'm_attn_out_norm': out['m_attn_out_norm'], 'm_ssm_out_norm': out['m_ssm_out_norm'], 'm_w_o': out['m_w_o'], 'm_norm2': out['m_norm2'], 'm_w_gate_up': out['m_w_gate_up'], 'm_w_down': out['m_w_down'], 'm_rel_bias': out['m_rel_bias'], 'm_final_norm': out['m_final_norm'], 'v_ada_w': out['v_ada_w'], 'v_ada_b': out['v_ada_b'], 'v_norm1': out['v_norm1'], 'v_w_in': out['v_w_in'], 'v_conv_w': out['v_conv_w'], 'v_conv_b': out['v_conv_b'], 'v_dt_bias': out['v_dt_bias'], 'v_A_log': out['v_A_log'], 'v_D_skip': out['v_D_skip'], 'v_sinks': out['v_sinks'], 'v_attn_out_norm': out['v_attn_out_norm'], 'v_ssm_out_norm': out['v_ssm_out_norm'], 'v_w_o': out['v_w_o'], 'v_norm2': out['v_norm2'], 'v_w_gate_up': out['v_w_gate_up'], 'v_w_down': out['v_w_down'], 'v_rel_bias': out['v_rel_bias'], 'v_final_norm': out['v_final_norm']}


def _loss(weights, diff, rest, loss_target):
    with _jax.named_scope("forward"):
        args = {**rest, TWIN_DIFF_INPUT: diff, **{k: w.astype(_WEIGHT_DTYPES[k]) for k, w in weights.items()}}
        y = _forward(args)
    with _jax.named_scope("loss_head"):
        err = _jnp.square(y.astype(_jnp.float32) - loss_target)
        return 0.5 * _jnp.sum(_jnp.mean(err, axis=-1)) if err.ndim else 0.5 * err


def _adamw(w, g, m, v):
    m = ADAM_B1 * m + (1.0 - ADAM_B1) * g
    v = ADAM_B2 * v + (1.0 - ADAM_B2) * _jnp.square(g)
    m_hat = m / (1.0 - ADAM_B1 ** ADAM_STEP)
    v_hat = v / (1.0 - ADAM_B2 ** ADAM_STEP)
    delta = -ADAM_LR * (m_hat / (_jnp.sqrt(v_hat) + ADAM_EPS) + ADAM_WD * w)
    return delta, m, v


def reference(x, c, ada_w, ada_b, norm1, w_in, conv_w, conv_b, dt_bias, A_log, D_skip, sinks, attn_out_norm, ssm_out_norm, w_o, norm2, w_gate_up, w_down, rel_bias, final_norm, loss_target, m_ada_w, m_ada_b, m_norm1, m_w_in, m_conv_w, m_conv_b, m_dt_bias, m_A_log, m_D_skip, m_sinks, m_attn_out_norm, m_ssm_out_norm, m_w_o, m_norm2, m_w_gate_up, m_w_down, m_rel_bias, m_final_norm, v_ada_w, v_ada_b, v_norm1, v_w_in, v_conv_w, v_conv_b, v_dt_bias, v_A_log, v_D_skip, v_sinks, v_attn_out_norm, v_ssm_out_norm, v_w_o, v_norm2, v_w_gate_up, v_w_down, v_rel_bias, v_final_norm):
    given = dict(x=x, c=c, ada_w=ada_w, ada_b=ada_b, norm1=norm1, w_in=w_in, conv_w=conv_w, conv_b=conv_b, dt_bias=dt_bias, A_log=A_log, D_skip=D_skip, sinks=sinks, attn_out_norm=attn_out_norm, ssm_out_norm=ssm_out_norm, w_o=w_o, norm2=norm2, w_gate_up=w_gate_up, w_down=w_down, rel_bias=rel_bias, final_norm=final_norm, loss_target=loss_target, m_ada_w=m_ada_w, m_ada_b=m_ada_b, m_norm1=m_norm1, m_w_in=m_w_in, m_conv_w=m_conv_w, m_conv_b=m_conv_b, m_dt_bias=m_dt_bias, m_A_log=m_A_log, m_D_skip=m_D_skip, m_sinks=m_sinks, m_attn_out_norm=m_attn_out_norm, m_ssm_out_norm=m_ssm_out_norm, m_w_o=m_w_o, m_norm2=m_norm2, m_w_gate_up=m_w_gate_up, m_w_down=m_w_down, m_rel_bias=m_rel_bias, m_final_norm=m_final_norm, v_ada_w=v_ada_w, v_ada_b=v_ada_b, v_norm1=v_norm1, v_w_in=v_w_in, v_conv_w=v_conv_w, v_conv_b=v_conv_b, v_dt_bias=v_dt_bias, v_A_log=v_A_log, v_D_skip=v_D_skip, v_sinks=v_sinks, v_attn_out_norm=v_attn_out_norm, v_ssm_out_norm=v_ssm_out_norm, v_w_o=v_w_o, v_norm2=v_norm2, v_w_gate_up=v_w_gate_up, v_w_down=v_w_down, v_rel_bias=v_rel_bias, v_final_norm=v_final_norm)
    weights = {n: given[n] for n in TWIN_WEIGHTS}
    shared = {n: given[n] for n in SHARED_INPUTS}
    per_example = {n: given[n] for n in ['x', 'c']}
    grad_fn = _jax.value_and_grad(_loss, argnums=(0, 1))

    def one_microbatch(ex, loss_target):
        ex = dict(ex)
        diff = ex.pop(TWIN_DIFF_INPUT)
        return grad_fn(weights, diff, {**shared, **ex}, loss_target)

    if N_MICROBATCH == 1:
        loss, (grad_w, grad_x) = one_microbatch(per_example, given["loss_target"])
    else:
        def body(carry, xs):
            loss_sum, grad_sum = carry
            l_k, (gw_k, gx_k) = one_microbatch(xs[0], xs[1])
            with _jax.named_scope("update"):
                return (loss_sum + l_k, _jax.tree.map(_jnp.add, grad_sum, gw_k)), gx_k

        init = (_jnp.zeros((), _jnp.float32), _jax.tree.map(_jnp.zeros_like, weights))
        (loss, grad_w), grad_x = _jax.lax.scan(body, init, (per_example, given["loss_target"]))
    with _jax.named_scope("update"):
        delta_w, new_m, new_v = {}, {}, {}
        for n in TWIN_WEIGHTS:
            delta_w[n], new_m[n], new_v[n] = _adamw(weights[n], grad_w[n], given["m_" + n], given["v_" + n])
    return (loss, grad_x, *[grad_w[n] for n in TWIN_WEIGHTS], *[delta_w[n] for n in TWIN_WEIGHTS],
            *[new_m[n] for n in TWIN_WEIGHTS], *[new_v[n] for n in TWIN_WEIGHTS])
```

```python
import numpy as np
import jax
import jax.numpy as jnp
from jax import lax
from jax.experimental import pallas as pl
from jax.experimental.pallas import tpu as pltpu

F32, BF16 = jnp.float32, jnp.bfloat16
HI = lax.Precision.HIGHEST

D = 1024
QW, KVW = 512, 128
NH, HD, NKV = 8, 64, 2
SW = 512
NST = 128
XBCW = 1024
CK = 4
BLK = 128
DFF = 2816
IN_W = 2312
PROJ_W = 2432
EPS = 1e-6
NEG = -1e30
NBUCKET = 32

B1, B2, LR, AEPS, WD, STEP = 0.9, 0.999, 0.001, 1e-08, 0.01, 10

VMEM_LIMIT = 56 * 1024 * 1024

_NT = (((1,), (1,)), ((), ()))
_TN = (((0,), (0,)), ((), ()))


def _mm(a, b):
    return jnp.dot(a, b, preferred_element_type=F32)


def _mm_nt(a, b):
    return lax.dot_general(a, b, _NT, preferred_element_type=F32)


def _mm_tn(a, b):
    return lax.dot_general(a, b, _TN, preferred_element_type=F32)


def _mm_hi(a, b):
    return jnp.dot(a, b, preferred_element_type=F32, precision=HI)


def _sig(x):
    return 1.0 / (1.0 + jnp.exp(-x))


def _cp(sem):
    return pltpu.CompilerParams(dimension_semantics=sem, vmem_limit_bytes=VMEM_LIMIT)


def _row(shape):
    nd = len(shape)
    return pl.BlockSpec(shape, lambda *_: (0,) * nd)


def _adamw(w, g, m, v):
    m = B1 * m + (1.0 - B1) * g
    v = B2 * v + (1.0 - B2) * (g * g)
    m_hat = m / (1.0 - B1 ** STEP)
    v_hat = v / (1.0 - B2 ** STEP)
    delta = -LR * (m_hat / (jnp.sqrt(v_hat) + AEPS) + WD * w)
    return delta, m, v


def _exchange(name, flips, inps, out_shapes, plan, own=None):
    nplan = len(plan(0, 0, 0, 0))
    nown = len(own(0, 0, 0)) if own is not None else 0
    nsem = len(flips) * nplan
    nin = len(inps)

    def body(*refs):
        in_refs, out_refs = refs[:nin], refs[nin:nin + len(out_shapes)]
        send_sems, recv_sems, own_sems = refs[nin + len(out_shapes):]
        x, y, c = lax.axis_index("x"), lax.axis_index("y"), lax.axis_index("c")
        copies = []
        if own is not None:
            for j, (a, si, o, di) in enumerate(own(x, y, c)):
                cp = pltpu.make_async_copy(in_refs[a].at[si], out_refs[o].at[di], own_sems.at[j])
                cp.start()
                copies.append(cp)
        for k, (fx, fy, fc) in enumerate(flips):
            peer = (1 - x if fx else x, 1 - y if fy else y, 1 - c if fc else c)
            for j, (a, si, o, di) in enumerate(plan(k, x, y, c)):
                cp = pltpu.make_async_remote_copy(
                    src_ref=in_refs[a].at[si], dst_ref=out_refs[o].at[di],
                    send_sem=send_sems.at[k * nplan + j], recv_sem=recv_sems.at[k * nplan + j],
                    device_id=peer, device_id_type=pl.DeviceIdType.MESH)
                cp.start()
                copies.append(cp)
        for cp in copies:
            cp.wait()

    hbm = pl.BlockSpec(memory_space=pl.ANY)
    return pl.pallas_call(
        body, name=name,
        out_shape=out_shapes,
        in_specs=[hbm] * nin,
        out_specs=[hbm] * len(out_shapes),
        scratch_shapes=[pltpu.SemaphoreType.DMA((nsem,)), pltpu.SemaphoreType.DMA((nsem,)),
                        pltpu.SemaphoreType.DMA((max(nown, 1),))],
    )(*inps)


_ALL7 = [(f >> 2 & 1, f >> 1 & 1, f & 1) for f in range(1, 8)]
_CHIPS3 = [(0, 1, 0), (1, 0, 0), (1, 1, 0)]
_SIBLING = [(0, 0, 1)]


def _gather8(name, blk):
    def plan(k, x, y, c):
        return [(0, 0, 0, 4 * x + 2 * y + c)]

    def own(x, y, c):
        return [(0, 0, 0, 4 * x + 2 * y + c)]

    return _exchange(name, _ALL7, [blk[None]], [jax.ShapeDtypeStruct((8,) + blk.shape, blk.dtype)], plan, own)[0]


def _gather_chips(name, blks):
    def plan(k, x, y, c):
        return [(a, 0, a, 2 * x + y) for a in range(len(blks))]

    def own(x, y, c):
        return [(a, 0, a, 2 * x + y) for a in range(len(blks))]

    outs = [jax.ShapeDtypeStruct((4,) + b.shape, b.dtype) for b in blks]
    return _exchange(name, _CHIPS3, [b[None] for b in blks], outs, plan, own)


def _ada_fwd(c_all, w_loc, b_loc):
    n = w_loc.shape[1]
    tn = 512

    def body(c_ref, w_ref, b_ref, o_ref):
        cv = c_ref[...]
        cond = cv * _sig(cv)
        o_ref[...] = _mm_hi(cond, w_ref[...]) + b_ref[...]

    return pl.pallas_call(
        body, name="ada_fwd", grid=(n // tn,),
        out_shape=jax.ShapeDtypeStruct((8, n), F32),
        in_specs=[_row((8, D)), pl.BlockSpec((D, tn), lambda j: (0, j)), pl.BlockSpec((1, tn), lambda j: (0, j))],
        out_specs=pl.BlockSpec((8, tn), lambda j: (0, j)),
        compiler_params=_cp(("parallel",)),
    )(c_all, w_loc, b_loc)


def _ada_bwd_adamw(c_all_t, dmod_loc, w, m, v):
    n = w.shape[1]
    tn = 512

    def body(ct_ref, dm_ref, w_ref, m_ref, v_ref, g_ref, d_ref, mo_ref, vo_ref):
        ct = ct_ref[...]
        cond = ct * _sig(ct)
        dm = dm_ref[...]
        g = cond[:, 0:1] * dm[0:1, :]
        for b in range(1, 8):
            g = g + cond[:, b:b + 1] * dm[b:b + 1, :]
        g_ref[...] = g
        d_ref[...], mo_ref[...], vo_ref[...] = _adamw(w_ref[...], g, m_ref[...], v_ref[...])

    wspec = pl.BlockSpec((D, tn), lambda j: (0, j))
    return pl.pallas_call(
        body, name="ada_bwd_adamw", grid=(n // tn,),
        out_shape=[jax.ShapeDtypeStruct((D, n), F32)] * 4,
        in_specs=[_row((D, 8)), pl.BlockSpec((8, tn), lambda j: (0, j)), wspec, wspec, wspec],
        out_specs=[wspec] * 4,
        compiler_params=_cp(("parallel",)),
    )(c_all_t, dmod_loc, w, m, v)


def _in_proj_fwd(x, a1, sh1, w_in):
    s = x.shape[0]
    tm = 512

    def body(x_ref, a_ref, s_ref, w_ref, q_ref, kv_ref, z_ref, xbc_ref, dt_ref):
        xv = x_ref[...]
        r = lax.rsqrt(jnp.mean(xv * xv, axis=-1, keepdims=True) + EPS)
        h = (xv * r * a_ref[...] + s_ref[...]).astype(BF16)
        p = _mm(h, w_ref[...])
        q_ref[...] = p[:, 0:512].astype(BF16)
        kv_ref[...] = p[:, 512:768].astype(BF16)
        z_ref[...] = p[:, 768:1280]
        xbc_ref[...] = p[:, 1280:2304]
        dt_ref[...] = p[:, 2304:2432]

    def tok(w):
        return pl.BlockSpec((tm, w), lambda i: (i, 0))

    return pl.pallas_call(
        body, name="in_proj_fwd", grid=(s // tm,),
        out_shape=[jax.ShapeDtypeStruct((s, QW), BF16), jax.ShapeDtypeStruct((s, 2 * KVW), BF16),
                   jax.ShapeDtypeStruct((s, SW), F32), jax.ShapeDtypeStruct((s, XBCW), F32),
                   jax.ShapeDtypeStruct((s, 128), F32)],
        in_specs=[tok(D), _row((1, D)), _row((1, D)), _row((D, PROJ_W))],
        out_specs=[tok(QW), tok(2 * KVW), tok(SW), tok(XBCW), tok(128)],
        compiler_params=_cp(("parallel",)),
    )(x, a1, sh1, w_in)


def _in_proj_bwd(x, dx1, a1, sh1, w_in, dq, dkv, dz, dxbc, ddt):
    s = x.shape[0]
    tm = 512

    def body(x_ref, dx1_ref, a_ref, s_ref, w_ref, dq_ref, dkv_ref, dz_ref, dxbc_ref, ddt_ref,
             gx_ref, dproj_ref, h_ref, dsh_ref, p_ref):
        i = pl.program_id(0)

        @pl.when(i == 0)
        def _():
            dsh_ref[...] = jnp.zeros_like(dsh_ref)
            p_ref[...] = jnp.zeros_like(p_ref)

        dproj = jnp.concatenate([dq_ref[...], dkv_ref[...], dz_ref[...], dxbc_ref[...], ddt_ref[...]], axis=1)
        dproj_ref[...] = dproj
        dh = _mm_nt(dproj, w_ref[...])
        xv = x_ref[...]
        r = lax.rsqrt(jnp.mean(xv * xv, axis=-1, keepdims=True) + EPS)
        xn = xv * r
        a = a_ref[...]
        h_ref[...] = (xn * a + s_ref[...]).astype(BF16)
        dsh_ref[0:1, :] += jnp.sum(dh, axis=0, keepdims=True)
        p_ref[0:1, :] += jnp.sum(dh * xn, axis=0, keepdims=True)
        u = dh * a
        gx_ref[...] = dx1_ref[...] + r * u - xn * (r * jnp.mean(u * xn, axis=-1, keepdims=True))

    def tok(w):
        return pl.BlockSpec((tm, w), lambda i: (i, 0))

    return pl.pallas_call(
        body, name="in_proj_bwd", grid=(s // tm,),
        out_shape=[jax.ShapeDtypeStruct((s, D), F32), jax.ShapeDtypeStruct((s, PROJ_W), BF16),
                   jax.ShapeDtypeStruct((s, D), BF16), jax.ShapeDtypeStruct((8, D), F32),
                   jax.ShapeDtypeStruct((8, D), F32)],
        in_specs=[tok(D), tok(D), _row((1, D)), _row((1, D)), _row((D, PROJ_W)),
                  tok(QW), tok(2 * KVW), tok(SW), tok(XBCW), tok(128)],
        out_specs=[tok(D), tok(PROJ_W), tok(D), _row((8, D)), _row((8, D))],
        compiler_params=_cp(("arbitrary",)),
    )(x, dx1, a1, sh1, w_in, dq, dkv, dz, dxbc, ddt)


def _attn_geometry():
    dist = np.arange(BLK)[:, None] + BLK - np.arange(2 * BLK)[None, :]
    n = np.maximum(dist, 0)
    max_exact = NBUCKET // 2
    large = max_exact + (np.log(np.maximum(n, 1) / max_exact) / np.log(128 / max_exact)
                         * (NBUCKET - max_exact)).astype(np.int32)
    large = np.minimum(large, NBUCKET - 1)
    bucket = np.where(n < max_exact, n, large).astype(np.int32)
    mask = (dist >= 0) & (dist < 128)
    return bucket, mask


def _attn_heads(n, q_ref, kvp_ref, kvc_ref, bias_ref, sinks_ref):
    qv = q_ref[...] * 0.125
    kvw = jnp.concatenate([kvp_ref[...], kvc_ref[...]], axis=0)
    col = lax.broadcasted_iota(jnp.int32, (BLK, 2 * BLK), 1)
    first = jnp.where(jnp.logical_and(n == 0, col < BLK), NEG, 0.0)
    groups = []
    for g in range(NKV):
        qs = jnp.concatenate([qv[:, (4 * g + r) * HD:(4 * g + r + 1) * HD] for r in range(4)], axis=0)
        kw = kvw[:, g * HD:(g + 1) * HD]
        vw = kvw[:, KVW + g * HD:KVW + (g + 1) * HD]
        sc = _mm_nt(qs, kw)
        pn, ps = [], []
        for r in range(4):
            h = 4 * g + r
            sr = sc[r * BLK:(r + 1) * BLK] + bias_ref[h] + first
            sink = sinks_ref[h]
            m = jnp.maximum(jnp.max(sr, axis=-1, keepdims=True), sink)
            p = jnp.exp(sr - m)
            es = jnp.exp(sink - m)
            inv = 1.0 / (jnp.sum(p, axis=-1, keepdims=True) + es)
            pn.append(p * inv)
            ps.append(es * inv)
        pn = jnp.concatenate(pn, axis=0)
        ps = jnp.concatenate(ps, axis=0)
        o = _mm(pn.astype(BF16), vw)
        groups.append((qs, kw, vw, pn, ps, o))
    return groups


def _unstack_heads(parts):
    return jnp.concatenate([p[r * BLK:(r + 1) * BLK] for p in parts for r in range(4)], axis=1)


def _attn_fwd(q, kv, bias, sinks, nw):
    s = q.shape[0]

    def body(q_ref, kvp_ref, kvc_ref, bias_ref, sinks_ref, nw_ref, y_ref):
        n = pl.program_id(0)
        groups = _attn_heads(n, q_ref, kvp_ref, kvc_ref, bias_ref, sinks_ref)
        o = _unstack_heads([g[5] for g in groups])
        r = lax.rsqrt(jnp.mean(o * o, axis=-1, keepdims=True) + EPS)
        y_ref[...] = (o * r * nw_ref[...]).astype(BF16)

    return pl.pallas_call(
        body, name="attn_fwd", grid=(s // BLK,),
        out_shape=jax.ShapeDtypeStruct((s, QW), BF16),
        in_specs=[pl.BlockSpec((BLK, QW), lambda n: (n, 0)),
                  pl.BlockSpec((BLK, 2 * KVW), lambda n: (jnp.maximum(n - 1, 0), 0)),
                  pl.BlockSpec((BLK, 2 * KVW), lambda n: (n, 0)),
                  _row((NH, BLK, 2 * BLK)),
                  pl.BlockSpec(memory_space=pltpu.SMEM),
                  _row((1, QW))],
        out_specs=pl.BlockSpec((BLK, QW), lambda n: (n, 0)),
        compiler_params=_cp(("parallel",)),
    )(q, kv, kv, bias, sinks, nw)


def _attn_bwd(q, kv, dya, bias, sinks, nw):
    s = q.shape[0]
    nb = s // BLK

    def body(q_ref, kvp_ref, kvc_ref, dy_ref, bias_ref, sinks_ref, nw_ref,
             dq_ref, dkv_ref, dbias_ref, dsink_ref, dnw_ref, carry_ref):
        n = pl.program_id(0)

        @pl.when(n == 0)
        def _():
            carry_ref[...] = jnp.zeros_like(carry_ref)
            dbias_ref[...] = jnp.zeros_like(dbias_ref)
            dsink_ref[...] = jnp.zeros_like(dsink_ref)
            dnw_ref[...] = jnp.zeros_like(dnw_ref)

        @pl.when(n < nb)
        def _():
            groups = _attn_heads(n, q_ref, kvp_ref, kvc_ref, bias_ref, sinks_ref)
            o = _unstack_heads([g[5] for g in groups])
            r = lax.rsqrt(jnp.mean(o * o, axis=-1, keepdims=True) + EPS)
            dy = dy_ref[...]
            on = o * r
            dnw_ref[0:1, :] += jnp.sum(dy * on, axis=0, keepdims=True)
            u = dy * nw_ref[...]
            do = r * u - on * (r * jnp.mean(u * on, axis=-1, keepdims=True))
            dq_parts, dk_parts, dv_parts = [], [], []
            for g, (qs, kw, vw, pn, ps, og) in enumerate(groups):
                dos = jnp.concatenate([do[:, (4 * g + r_) * HD:(4 * g + r_ + 1) * HD] for r_ in range(4)], axis=0)
                delta = jnp.sum(dos * og, axis=-1, keepdims=True)
                dp = _mm_nt(dos.astype(BF16), vw)
                ds = pn * (dp - delta)
                dsk = ps * delta
                lane = lax.broadcasted_iota(jnp.int32, (1, 128), 1)
                for r_ in range(4):
                    h = 4 * g + r_
                    dbias_ref[h] += ds[r_ * BLK:(r_ + 1) * BLK]
                    dsink_ref[0:1, :] -= jnp.where(lane == h, jnp.sum(dsk[r_ * BLK:(r_ + 1) * BLK]), 0.0)
                dsb = ds.astype(BF16)
                dq_parts.append(_mm(dsb, kw) * 0.125)
                dk_parts.append(_mm_tn(dsb, qs))
                dv_parts.append(_mm_tn(pn.astype(BF16), dos.astype(BF16)))
            dq_ref[...] = _unstack_heads(dq_parts).astype(BF16)
            dkvw = jnp.concatenate(dk_parts + dv_parts, axis=1)
            dkv_ref[...] = (carry_ref[...] + dkvw[0:BLK]).astype(BF16)
            carry_ref[...] = dkvw[BLK:2 * BLK]

        @pl.when(n == nb)
        def _():
            dkv_ref[...] = carry_ref[...].astype(BF16)

    last = nb - 1
    return pl.pallas_call(
        body, name="attn_bwd", grid=(nb + 1,),
        out_shape=[jax.ShapeDtypeStruct((s, QW), BF16), jax.ShapeDtypeStruct((s, 2 * KVW), BF16),
                   jax.ShapeDtypeStruct((NH, BLK, 2 * BLK), F32), jax.ShapeDtypeStruct((NH, 128), F32),
                   jax.ShapeDtypeStruct((8, QW), F32)],
        in_specs=[pl.BlockSpec((BLK, QW), lambda n: (jnp.minimum(n, last), 0)),
                  pl.BlockSpec((BLK, 2 * KVW), lambda n: (jnp.clip(n - 1, 0, last), 0)),
                  pl.BlockSpec((BLK, 2 * KVW), lambda n: (jnp.minimum(n, last), 0)),
                  pl.BlockSpec((BLK, QW), lambda n: (jnp.minimum(n, last), 0)),
                  _row((NH, BLK, 2 * BLK)),
                  pl.BlockSpec(memory_space=pltpu.SMEM),
                  _row((1, QW))],
        out_specs=[pl.BlockSpec((BLK, QW), lambda n: (jnp.minimum(n, last), 0)),
                   pl.BlockSpec((BLK, 2 * KVW), lambda n: (jnp.maximum(n - 1, 0), 0)),
                   _row((NH, BLK, 2 * BLK)), _row((NH, 128)), _row((8, QW))],
        scratch_shapes=[pltpu.VMEM((BLK, 2 * KVW), F32)],
        compiler_params=_cp(("arbitrary",)),
    )(q, kv, kv, dya, bias, sinks, nw)


def _rel_bias_grad(dbias, bucket):
    def body(db_ref, bk_ref, o_ref):
        bk = bk_ref[...]
        lane = lax.broadcasted_iota(jnp.int32, (1, 128), 1)
        for b in range(NBUCKET):
            sel = bk == b
            row = jnp.zeros((1, 128), F32)
            for h in range(NH):
                row = row + jnp.where(lane == h, jnp.sum(jnp.where(sel, db_ref[h], 0.0)), 0.0)
            o_ref[b:b + 1, :] = row

    return pl.pallas_call(
        body, name="rel_bias_grad",
        out_shape=jax.ShapeDtypeStruct((NBUCKET, 128), F32),
    )(dbias, bucket)


def _ssd_consts():
    head_of_lane = np.arange(SW) // HD
    expand = (np.arange(128)[:, None] == head_of_lane[None, :]).astype(np.float32)
    tril = np.tril(np.ones((BLK, BLK), np.float32))
    return jnp.asarray(expand), jnp.asarray(expand.T.copy()), jnp.asarray(tril), jnp.asarray(tril.T.copy())


def _ssd_chunk_fwd(c, z_ref, xc_ref, xp_ref, dtr_ref, cw_ref, cb_ref, dtb_ref, a_ref, dk_ref, ex_ref, tril_ref, h_in):
    halo = jnp.where(c == 0, 0.0, xp_ref[BLK - 8:BLK, :])
    ext = jnp.concatenate([halo, xc_ref[...]], axis=0)
    cw = cw_ref[...]
    taps = [ext[8:8 + BLK] if k == CK - 1 else pltpu.roll(ext, CK - 1 - k, 0)[8:8 + BLK] for k in range(CK)]
    pre = cb_ref[...] + sum(cw[k:k + 1, :] * taps[k] for k in range(CK))
    sp = _sig(pre)
    xbc = pre * sp
    xs, bm, cm = xbc[:, 0:SW], xbc[:, SW:SW + 2 * NST], xbc[:, SW + 2 * NST:]
    dtin = dtr_ref[...] + dtb_ref[...]
    dt = jnp.maximum(dtin, 0.0) + jnp.log1p(jnp.exp(-jnp.abs(dtin)))
    av = a_ref[...]
    cs = _mm_hi(tril_ref[...], dt * av)
    cst = cs.T
    ex = ex_ref[...]
    dtx = _mm_hi(dt, ex)
    csx = _mm_hi(cs, ex)
    xdt = xs * dtx
    csl = csx[BLK - 1:BLK, :]
    decx = jnp.exp(csl - csx)
    ecsx = jnp.exp(csx)
    ecl = jnp.exp(csl)
    causal = tril_ref[...] > 0.5
    ydiag, yoff, cbs, lms = [], [], [], []
    for g in range(2):
        bg = bm[:, g * NST:(g + 1) * NST].astype(BF16)
        cg = cm[:, g * NST:(g + 1) * NST].astype(BF16)
        cb = _mm_nt(cg, bg)
        cbs.append(cb)
        yoff.append(_mm(cg, h_in[:, g * 256:(g + 1) * 256].astype(BF16)))
        for r in range(4):
            h = 4 * g + r
            seg = cs[:, h:h + 1] - cst[h:h + 1, :]
            lm = jnp.where(causal, jnp.exp(jnp.minimum(seg, 0.0)), 0.0)
            lms.append(lm)
            ydiag.append(_mm((cb * lm).astype(BF16), xdt[:, h * HD:(h + 1) * HD].astype(BF16)))
    yoff = jnp.concatenate(yoff, axis=1) * ecsx
    y = jnp.concatenate(ydiag, axis=1) + yoff + dk_ref[...] * xs
    return dict(ext=ext, taps=taps, pre=pre, sp=sp, xs=xs, bm=bm, cm=cm, dtin=dtin, dt=dt, av=av, cs=cs, cst=cst,
                dtx=dtx, csx=csx, xdt=xdt, decx=decx, ecsx=ecsx, ecl=ecl, causal=causal, cbs=cbs, lms=lms,
                yoff=yoff, y=y)


def _group_mean(t):
    m0 = jnp.mean(t[:, 0:256], axis=-1, keepdims=True)
    m1 = jnp.mean(t[:, 256:512], axis=-1, keepdims=True)
    return jnp.concatenate([jnp.broadcast_to(m0, (t.shape[0], 256)), jnp.broadcast_to(m1, (t.shape[0], 256))], axis=1)


def _ssd_specs(nc, rev):
    def cur(w):
        return pl.BlockSpec((BLK, w), (lambda i: (nc - 1 - i, 0)) if rev else (lambda i: (i, 0)))
    prev = pl.BlockSpec((BLK, XBCW), (lambda i: (jnp.maximum(nc - 2 - i, 0), 0)) if rev
                        else (lambda i: (jnp.maximum(i - 1, 0), 0)))
    return cur, prev


def _ssd_fwd(z, xbc, dtr, cw, cb, dtb, av, dk, nw):
    s = z.shape[0]
    nc = s // BLK
    ex, _, tril, _ = _ssd_consts()

    def body(z_ref, xc_ref, xp_ref, dtr_ref, cw_ref, cb_ref, dtb_ref, a_ref, dk_ref, nw_ref, ex_ref, tril_ref,
             y_ref, hs_ref, h_ref):
        c = pl.program_id(0)

        @pl.when(c == 0)
        def _():
            h_ref[...] = jnp.zeros_like(h_ref)

        h_in = h_ref[...]
        hs_ref[0] = h_in
        f = _ssd_chunk_fwd(c, z_ref, xc_ref, xp_ref, dtr_ref, cw_ref, cb_ref, dtb_ref, a_ref, dk_ref, ex_ref,
                           tril_ref, h_in)
        dx = (f["decx"] * f["xdt"]).astype(BF16)
        st = [_mm_tn(f["bm"][:, g * NST:(g + 1) * NST].astype(BF16), dx[:, g * 256:(g + 1) * 256]) for g in range(2)]
        h_ref[...] = h_in * f["ecl"] + jnp.concatenate(st, axis=1)
        zv = z_ref[...]
        t = f["y"] * (zv * _sig(zv))
        r = lax.rsqrt(_group_mean(t * t) + EPS)
        y_ref[...] = (t * r * nw_ref[...]).astype(BF16)

    cur, prev = _ssd_specs(nc, False)
    return pl.pallas_call(
        body, name="ssd_fwd", grid=(nc,),
        out_shape=[jax.ShapeDtypeStruct((s, SW), BF16), jax.ShapeDtypeStruct((nc, NST, SW), F32)],
        in_specs=[cur(SW), cur(XBCW), prev, cur(128), _row((8, XBCW)), _row((1, XBCW)), _row((1, 128)),
                  _row((1, 128)), _row((1, SW)), _row((1, SW)), _row((128, SW)), _row((BLK, BLK))],
        out_specs=[cur(SW), pl.BlockSpec((1, NST, SW), lambda i: (i, 0, 0))],
        scratch_shapes=[pltpu.VMEM((NST, SW), F32)],
        compiler_params=_cp(("arbitrary",)),
    )(z, xbc, xbc, dtr, cw, cb, dtb, av, dk, nw, ex, tril)


def _ssd_bwd(z, xbc, dtr, dys, hs, cw, cb, dtb, av, dk, nw):
    s = z.shape[0]
    nc = s // BLK
    ex, ext_t, tril, triu = _ssd_consts()

    def body(z_ref, xc_ref, xp_ref, dtr_ref, dy_ref, hs_ref, cw_ref, cb_ref, dtb_ref, a_ref, dk_ref, nw_ref,
             ex_ref, ext_ref, tril_ref, triu_ref,
             dz_ref, dxbc_ref, ddt_ref, dcw_ref, dcb_ref, dnw_ref, dhd_ref, dh_ref, nxt_ref, dd_ref):
        i = pl.program_id(0)
        c = nc - 1 - i

        @pl.when(i == 0)
        def _():
            dh_ref[...] = jnp.zeros_like(dh_ref)
            nxt_ref[...] = jnp.zeros_like(nxt_ref)
            dd_ref[...] = jnp.zeros_like(dd_ref)
            dcw_ref[...] = jnp.zeros_like(dcw_ref)
            dcb_ref[...] = jnp.zeros_like(dcb_ref)
            dnw_ref[...] = jnp.zeros_like(dnw_ref)
            dhd_ref[...] = jnp.zeros_like(dhd_ref)

        h_in = hs_ref[0]
        f = _ssd_chunk_fwd(c, z_ref, xc_ref, xp_ref, dtr_ref, cw_ref, cb_ref, dtb_ref, a_ref, dk_ref, ex_ref,
                           tril_ref, h_in)
        xs, xdt, decx, ecsx, ecl, dtx = f["xs"], f["xdt"], f["decx"], f["ecsx"], f["ecl"], f["dtx"]
        cs, cst, causal = f["cs"], f["cst"], f["causal"]
        causal_t = triu_ref[...] > 0.5

        zv = z_ref[...]
        sz = _sig(zv)
        gz = zv * sz
        t = f["y"] * gz
        r = lax.rsqrt(_group_mean(t * t) + EPS)
        tn_ = t * r
        dyn = dy_ref[...]
        dnw_ref[0:1, :] += jnp.sum(dyn * tn_, axis=0, keepdims=True)
        u = dyn * nw_ref[...]
        dt_ = r * u - tn_ * (r * _group_mean(u * tn_))
        dy = dt_ * gz
        dz_ref[...] = (dt_ * f["y"] * (sz * (1.0 + zv * (1.0 - sz)))).astype(BF16)

        dd_ref[0:1, :] += jnp.sum(dy * xs, axis=0, keepdims=True)
        dxs = dk_ref[...] * dy

        gst = dh_ref[...]
        edy = ecsx * dy
        dxdt, dbs, dcs_, dcsx_parts, dh_new = [], [], [], [], []
        lane = lax.broadcasted_iota(jnp.int32, (1, 128), 1)
        dcs_intra = jnp.zeros((BLK, 128), F32)
        for g in range(2):
            sl = slice(g * 256, (g + 1) * 256)
            bgf, cgf = f["bm"][:, g * NST:(g + 1) * NST], f["cm"][:, g * NST:(g + 1) * NST]
            bg, cg = bgf.astype(BF16), cgf.astype(BF16)
            gg = gst[:, sl].astype(BF16)
            hg = h_in[:, sl].astype(BF16)
            edyg = edy[:, sl].astype(BF16)
            dc = _mm_nt(edyg, hg)
            dh_new.append(gst[:, sl] * ecl[:, sl] + _mm_tn(cg, edyg))
            bgm = _mm(bg, gg)
            dxdt_g = decx[:, sl] * bgm
            dxg = (decx[:, sl] * xdt[:, sl]).astype(BF16)
            db = _mm_nt(dxg, gg)
            qd = bgm * xdt[:, sl] * decx[:, sl]
            last = jnp.sum(qd, axis=0, keepdims=True) + ecl[:, sl] * jnp.sum(gst[:, sl] * h_in[:, sl], axis=0, keepdims=True)
            rowid = lax.broadcasted_iota(jnp.int32, (BLK, 256), 0)
            dcsx_parts.append(f["yoff"][:, sl] * dy[:, sl] - qd + jnp.where(rowid == BLK - 1, last, 0.0))
            cb_ = f["cbs"][g]
            cbt = _mm_nt(bg, cg)
            dcb_ = jnp.zeros((BLK, BLK), F32)
            dcbt = jnp.zeros((BLK, BLK), F32)
            dxd = []
            for r_ in range(4):
                h = 4 * g + r_
                hl = slice(h * HD, (h + 1) * HD)
                lm = f["lms"][h]
                segt = cst[h:h + 1, :] - cs[:, h:h + 1]
                lmt = jnp.where(causal_t, jnp.exp(jnp.minimum(segt, 0.0)), 0.0)
                dyh = dy[:, hl].astype(BF16)
                xdh = xdt[:, hl].astype(BF16)
                dw = _mm_nt(dyh, xdh)
                dwt = _mm_nt(xdh, dyh)
                wt = cbt * lmt
                dxd.append(_mm(wt.astype(BF16), dyh))
                dcb_ = dcb_ + dw * lm
                dcbt = dcbt + dwt * lmt
                col = jnp.sum(dw * (cb_ * lm), axis=-1, keepdims=True) - jnp.sum(dwt * wt, axis=-1, keepdims=True)
                dcs_intra = dcs_intra + jnp.where(lane == h, col, 0.0)
            dxdt.append(dxdt_g + jnp.concatenate(dxd, axis=1))
            dcs_.append(dc + _mm(dcb_.astype(BF16), bg))
            dbs.append(db + _mm(dcbt.astype(BF16), cg))
        dh_ref[...] = jnp.concatenate(dh_new, axis=1)
        dxdt = jnp.concatenate(dxdt, axis=1)
        dxs = dxs + dxdt * dtx
        ext_t_ = ext_ref[...]
        dcs = dcs_intra + _mm_hi(jnp.concatenate(dcsx_parts, axis=1), ext_t_)
        da = _mm_hi(triu_ref[...], dcs)
        ddt = da * f["av"] + _mm_hi(dxdt * xs, ext_t_)
        dhd_ref[1:2, :] += jnp.sum(da * f["dt"], axis=0, keepdims=True)
        ddtr = ddt * _sig(f["dtin"])
        dhd_ref[0:1, :] += jnp.sum(ddtr, axis=0, keepdims=True)
        ddt_ref[...] = ddtr.astype(BF16)

        sp, pre = f["sp"], f["pre"]
        dact = jnp.concatenate([dxs] + dbs + dcs_, axis=1)
        dpre = dact * (sp * (1.0 + pre * (1.0 - sp)))
        dcb_ref[0:1, :] += jnp.sum(dpre, axis=0, keepdims=True)
        for k in range(CK):
            dcw_ref[k:k + 1, :] += jnp.sum(dpre * f["taps"][k], axis=0, keepdims=True)
        ext2 = jnp.concatenate([dpre, nxt_ref[...]], axis=0)
        cw = cw_ref[...]
        dxr = cw[CK - 1:CK, :] * dpre
        for k in range(CK - 1):
            dxr = dxr + cw[k:k + 1, :] * pltpu.roll(ext2, BLK + 8 - (CK - 1 - k), 0)[0:BLK]
        dxbc_ref[...] = dxr.astype(BF16)
        nxt_ref[...] = dpre[0:8]

        @pl.when(i == nc - 1)
        def _():
            dhd_ref[2:3, :] = _mm_hi(dd_ref[...], ext_t_)[0:1, :]

    cur, prev = _ssd_specs(nc, True)
    return pl.pallas_call(
        body, name="ssd_bwd", grid=(nc,),
        out_shape=[jax.ShapeDtypeStruct((s, SW), BF16), jax.ShapeDtypeStruct((s, XBCW), BF16),
                   jax.ShapeDtypeStruct((s, 128), BF16), jax.ShapeDtypeStruct((8, XBCW), F32),
                   jax.ShapeDtypeStruct((8, XBCW), F32), jax.ShapeDtypeStruct((8, SW), F32),
                   jax.ShapeDtypeStruct((8, 128), F32)],
        in_specs=[cur(SW), cur(XBCW), prev, cur(128), cur(SW),
                  pl.BlockSpec((1, NST, SW), lambda i: (nc - 1 - i, 0, 0)),
                  _row((8, XBCW)), _row((1, XBCW)), _row((1, 128)), _row((1, 128)), _row((1, SW)), _row((1, SW)),
                  _row((128, SW)), _row((SW, 128)), _row((BLK, BLK)), _row((BLK, BLK))],
        out_specs=[cur(SW), cur(XBCW), cur(128), _row((8, XBCW)), _row((8, XBCW)), _row((8, SW)), _row((8, 128))],
        scratch_shapes=[pltpu.VMEM((NST, SW), F32), pltpu.VMEM((8, XBCW), F32), pltpu.VMEM((8, SW), F32)],
        compiler_params=_cp(("arbitrary",)),
    )(z, xbc, xbc, dtr, dys, hs, cw, cb, dtb, av, dk, nw, ex, ext_t, tril, triu)


def _load_once(i, pairs, sem):
    @pl.when(i == 0)
    def _():
        cps = [pltpu.make_async_copy(src, dst, sem.at[k]) for k, (src, dst) in enumerate(pairs)]
        for cp in cps:
            cp.start()
        for cp in cps:
            cp.wait()


def _mlp_fwd(x, ya, ys, tgt, w_o, w_gu, w_dn, gate1, a2, sh2, gate2, fn):
    s = x.shape[0]
    tm = 256

    def body(x_ref, ya_ref, ys_ref, t_ref, wo_hbm, wgu_hbm, wdn_hbm, g1_ref, a2_ref, s2_ref, g2_ref, fn_ref,
             x1_ref, gu_ref, dx2_ref, loss_ref, dfn_ref, wo, wgu, wdn, sem):
        i = pl.program_id(0)
        _load_once(i, [(wo_hbm, wo), (wgu_hbm, wgu), (wdn_hbm, wdn)], sem)

        @pl.when(i == 0)
        def _():
            loss_ref[...] = jnp.zeros_like(loss_ref)
            dfn_ref[...] = jnp.zeros_like(dfn_ref)

        mix = _mm(ya_ref[...], wo[0:QW, :]) + _mm(ys_ref[...], wo[QW:D, :])
        x1 = x_ref[...] + g1_ref[...] * mix
        x1_ref[...] = x1
        r2 = lax.rsqrt(jnp.mean(x1 * x1, axis=-1, keepdims=True) + EPS)
        h2 = (x1 * r2 * a2_ref[...] + s2_ref[...]).astype(BF16)
        gu = _mm(h2, wgu[...])
        gub = gu.astype(BF16)
        gu_ref[...] = gub
        gv, uv = gub[:, 0:DFF].astype(F32), gub[:, DFF:].astype(F32)
        act = (gv * _sig(gv) * uv).astype(BF16)
        x2 = x1 + g2_ref[...] * _mm(act, wdn[...])
        r3 = lax.rsqrt(jnp.mean(x2 * x2, axis=-1, keepdims=True) + EPS)
        xn = x2 * r3
        fnv = fn_ref[...]
        err = xn * fnv - t_ref[...]
        loss_ref[...] += jnp.sum(err * err) * (0.5 / D)
        dy = err * (1.0 / D)
        dfn_ref[0:1, :] += jnp.sum(dy * xn, axis=0, keepdims=True)
        u = dy * fnv
        dx2_ref[...] = r3 * u - xn * (r3 * jnp.mean(u * xn, axis=-1, keepdims=True))

    def tok(w):
        return pl.BlockSpec((tm, w), lambda i: (i, 0))

    hbm = pl.BlockSpec(memory_space=pl.ANY)
    return pl.pallas_call(
        body, name="mlp_fwd", grid=(s // tm,),
        out_shape=[jax.ShapeDtypeStruct((s, D), F32), jax.ShapeDtypeStruct((s, 2 * DFF), BF16),
                   jax.ShapeDtypeStruct((s, D), F32), jax.ShapeDtypeStruct((8, 128), F32),
                   jax.ShapeDtypeStruct((8, D), F32)],
        in_specs=[tok(D), tok(QW), tok(SW), tok(D), hbm, hbm, hbm,
                  _row((1, D)), _row((1, D)), _row((1, D)), _row((1, D)), _row((1, D))],
        out_specs=[tok(D), tok(2 * DFF), tok(D), _row((8, 128)), _row((8, D))],
        scratch_shapes=[pltpu.VMEM((D, D), BF16), pltpu.VMEM((D, 2 * DFF), BF16), pltpu.VMEM((DFF, D), BF16),
                        pltpu.SemaphoreType.DMA((3,))],
        compiler_params=_cp(("arbitrary",)),
    )(x, ya, ys, tgt, w_o, w_gu, w_dn, gate1, a2, sh2, gate2, fn)


def _mlp_bwd(x1, gu, dx2, w_o, w_gu, w_dn, gate1, a2, sh2, gate2):
    s = x1.shape[0]
    tm = 256

    def body(x1_ref, gu_ref, dx2_ref, wo_hbm, wgu_hbm, wdn_hbm, g1_ref, a2_ref, s2_ref, g2_ref,
             dx1_ref, dya_ref, dys_ref, act_ref, dgu_ref, h2_ref, dsh_ref, p_ref, wo, wgu, wdn, sem):
        i = pl.program_id(0)
        _load_once(i, [(wo_hbm, wo), (wgu_hbm, wgu), (wdn_hbm, wdn)], sem)

        @pl.when(i == 0)
        def _():
            dsh_ref[...] = jnp.zeros_like(dsh_ref)
            p_ref[...] = jnp.zeros_like(p_ref)

        dx2 = dx2_ref[...]
        dact = _mm_nt((dx2 * g2_ref[...]).astype(BF16), wdn[...])
        gub = gu_ref[...]
        gv, uv = gub[:, 0:DFF].astype(F32), gub[:, DFF:].astype(F32)
        sg = _sig(gv)
        sl = gv * sg
        act_ref[...] = (sl * uv).astype(BF16)
        dgu = jnp.concatenate([dact * uv * (sg * (1.0 + gv * (1.0 - sg))), dact * sl], axis=1).astype(BF16)
        dgu_ref[...] = dgu
        dh = _mm_nt(dgu, wgu[...])
        x1 = x1_ref[...]
        r2 = lax.rsqrt(jnp.mean(x1 * x1, axis=-1, keepdims=True) + EPS)
        xn = x1 * r2
        a2 = a2_ref[...]
        h2_ref[...] = (xn * a2 + s2_ref[...]).astype(BF16)
        dsh_ref[0:1, :] += jnp.sum(dh, axis=0, keepdims=True)
        p_ref[0:1, :] += jnp.sum(dh * xn, axis=0, keepdims=True)
        u = dh * a2
        dx1 = dx2 + r2 * u - xn * (r2 * jnp.mean(u * xn, axis=-1, keepdims=True))
        dx1_ref[...] = dx1
        dcat = _mm_nt((dx1 * g1_ref[...]).astype(BF16), wo[...])
        dya_ref[...] = dcat[:, 0:QW]
        dys_ref[...] = dcat[:, QW:D]

    def tok(w):
        return pl.BlockSpec((tm, w), lambda i: (i, 0))

    hbm = pl.BlockSpec(memory_space=pl.ANY)
    return pl.pallas_call(
        body, name="mlp_bwd", grid=(s // tm,),
        out_shape=[jax.ShapeDtypeStruct((s, D), F32), jax.ShapeDtypeStruct((s, QW), F32),
                   jax.ShapeDtypeStruct((s, SW), F32), jax.ShapeDtypeStruct((s, DFF), BF16),
                   jax.ShapeDtypeStruct((s, 2 * DFF), BF16), jax.ShapeDtypeStruct((s, D), BF16),
                   jax.ShapeDtypeStruct((8, D), F32), jax.ShapeDtypeStruct((8, D), F32)],
        in_specs=[tok(D), tok(2 * DFF), tok(D), hbm, hbm, hbm, _row((1, D)), _row((1, D)), _row((1, D)), _row((1, D))],
        out_specs=[tok(D), tok(QW), tok(SW), tok(DFF), tok(2 * DFF), tok(D), _row((8, D)), _row((8, D))],
        scratch_shapes=[pltpu.VMEM((D, D), BF16), pltpu.VMEM((D, 2 * DFF), BF16), pltpu.VMEM((DFF, D), BF16),
                        pltpu.SemaphoreType.DMA((3,))],
        compiler_params=_cp(("arbitrary",)),
    )(x1, gu, dx2, w_o, w_gu, w_dn, gate1, a2, sh2, gate2)


def _wgrad(name, a, b, tn, gate=None, w=None):
    s, m = a.shape
    n = b.shape[1]
    tk = 512
    nk = s // tk

    def body(*refs):
        if gate is None:
            a_ref, b_ref, o_ref = refs
        else:
            a_ref, b_ref, g_ref, w_ref, o_ref, dg_ref = refs
        k = pl.program_id(1)

        @pl.when(k == 0)
        def _():
            o_ref[...] = jnp.zeros_like(o_ref)

        o_ref[...] += _mm_tn(a_ref[...], b_ref[...].astype(BF16))

        if gate is not None:
            @pl.when(k == nk - 1)
            def _():
                acc = o_ref[...]
                dg_ref[...] = jnp.zeros_like(dg_ref)
                dg_ref[0:1, :] = jnp.sum(acc * w_ref[...].astype(F32), axis=0, keepdims=True)
                o_ref[...] = acc * g_ref[...]

    in_specs = [pl.BlockSpec((tk, m), lambda j, k: (k, 0)), pl.BlockSpec((tk, tn), lambda j, k: (k, j))]
    out_shape = [jax.ShapeDtypeStruct((m, n), F32)]
    out_specs = [pl.BlockSpec((m, tn), lambda j, k: (0, j))]
    args = [a, b]
    if gate is not None:
        in_specs += [pl.BlockSpec((1, tn), lambda j, k: (0, j)), pl.BlockSpec((m, tn), lambda j, k: (0, j))]
        out_shape.append(jax.ShapeDtypeStruct((8, n), F32))
        out_specs.append(pl.BlockSpec((8, tn), lambda j, k: (0, j)))
        args += [gate, w]
    return pl.pallas_call(
        body, name=name, grid=(n // tn, nk),
        out_shape=out_shape, in_specs=in_specs, out_specs=out_specs,
        compiler_params=_cp(("parallel", "arbitrary")),
    )(*args)


def _add2(name, a, b):
    r = a.shape[0]
    tr = r // 5

    def body(a_ref, b_ref, o_ref):
        o_ref[...] = a_ref[...] + b_ref[...]

    spec = pl.BlockSpec((tr, D), lambda i: (i, 0))
    return pl.pallas_call(body, name=name, grid=(5,), out_shape=jax.ShapeDtypeStruct(a.shape, F32),
                          in_specs=[spec, spec], out_specs=spec, compiler_params=_cp(("parallel",)))(a, b)


def _add4(name, a, b3):
    r = a.shape[0]
    tr = r // 5

    def body(a_ref, b_ref, o_ref):
        o_ref[...] = ((a_ref[...] + b_ref[0]) + b_ref[1]) + b_ref[2]

    spec = pl.BlockSpec((tr, D), lambda i: (i, 0))
    return pl.pallas_call(body, name=name, grid=(5,), out_shape=jax.ShapeDtypeStruct(a.shape, F32),
                          in_specs=[spec, pl.BlockSpec((3, tr, D), lambda i: (0, i, 0))], out_specs=spec,
                          compiler_params=_cp(("parallel",)))(a, b3)


def _adamw_rows(name, w, g, m, v, tr):
    r = w.shape[0]

    def body(w_ref, g_ref, m_ref, v_ref, d_ref, mo_ref, vo_ref):
        d_ref[...], mo_ref[...], vo_ref[...] = _adamw(w_ref[...], g_ref[...], m_ref[...], v_ref[...])

    spec = pl.BlockSpec((tr, D), lambda i: (i, 0))
    return pl.pallas_call(body, name=name, grid=(r // tr,), out_shape=[jax.ShapeDtypeStruct(w.shape, F32)] * 3,
                          in_specs=[spec] * 4, out_specs=[spec] * 3, compiler_params=_cp(("parallel",)))(w, g, m, v)


def _sum8(parts):
    r = parts.shape[1]

    def body(p_ref, o_ref):
        acc = p_ref[0]
        for b in range(1, 8):
            acc = acc + p_ref[b]
        o_ref[...] = acc

    return pl.pallas_call(body, name="sum8", out_shape=jax.ShapeDtypeStruct((r, D), F32))(parts)


def _scalar_grid(grid, in_specs, out_specs):
    return pltpu.PrefetchScalarGridSpec(num_scalar_prefetch=1, grid=grid, in_specs=in_specs, out_specs=out_specs)


def _add_half(name, g, got, core):
    rr, cc = g.shape[2:]

    def body(c_ref, g_ref, r_ref, o_ref):
        o_ref[...] = g_ref[...] + r_ref[...]

    spec = pl.BlockSpec((None, rr, cc), lambda i, c_ref: (i, 0, 0))
    return pl.pallas_call(
        body, name=name, out_shape=jax.ShapeDtypeStruct(got.shape, F32),
        grid_spec=_scalar_grid((4,), [pl.BlockSpec((None, None, rr, cc), lambda i, c_ref: (i, c_ref[0], 0, 0)), spec], spec),
        compiler_params=_cp(("arbitrary",)),
    )(core, g, got)


def _add_chips(name, s4, got, chip):
    rr, cc = s4.shape[1:]
    tr = rr // 2

    def body(q_ref, s_ref, r_ref, o_ref):
        o_ref[...] = ((s_ref[...] + r_ref[0]) + r_ref[1]) + r_ref[2]

    return pl.pallas_call(
        body, name=name, out_shape=jax.ShapeDtypeStruct((rr, cc), F32),
        grid_spec=_scalar_grid((2,), [pl.BlockSpec((None, tr, cc), lambda i, q_ref: (q_ref[0], i, 0)),
                                      pl.BlockSpec((3, tr, cc), lambda i, q_ref: (0, i, 0))],
                               pl.BlockSpec((tr, cc), lambda i, q_ref: (i, 0))),
        compiler_params=_cp(("arbitrary",)),
    )(chip, s4, got)


def _adamw_halves(name, mine, got, w, m, v, core):
    rr, cc = mine.shape
    tr = rr // 2

    def body(c_ref, t_ref, r_ref, w_ref, m_ref, v_ref, g_ref, d_ref, mo_ref, vo_ref):
        g = jnp.where(pl.program_id(0) == c_ref[0], t_ref[...], r_ref[...])
        g_ref[...] = g
        d_ref[...], mo_ref[...], vo_ref[...] = _adamw(w_ref[...], g, m_ref[...], v_ref[...])

    half = pl.BlockSpec((tr, cc), lambda h, i, c_ref: (i, 0))
    full = pl.BlockSpec((None, tr, cc), lambda h, i, c_ref: (h, i, 0))
    return pl.pallas_call(
        body, name=name, out_shape=[jax.ShapeDtypeStruct(w.shape, F32)] * 4,
        grid_spec=_scalar_grid((2, 2), [half, half, full, full, full], [full] * 4),
        compiler_params=_cp(("arbitrary", "arbitrary")),
    )(core, mine, got, w, m, v)


def _pack_small(dsh1, p1, dsh2, p2, dg1a, dg1b, dg2, norm1, norm2, scale1, scale2, dcw, dcb, dfn,
                dnw_attn, dnw_ssm, dhd, av, dsink, drel):
    def body(dsh1_ref, p1_ref, dsh2_ref, p2_ref, dg1a_ref, dg1b_ref, dg2_ref, n1_ref, n2_ref, s1_ref, s2_ref,
             dcw_ref, dcb_ref, dfn_ref, da_ref, ds_ref, dhd_ref, av_ref, dsink_ref, drel_ref, o_ref):
        o_ref[...] = jnp.zeros_like(o_ref)
        p1v, p2v = p1_ref[0:1, :], p2_ref[0:1, :]
        o_ref[0:1, :] = dsh1_ref[0:1, :]
        o_ref[1:2, :] = p1v * n1_ref[...]
        o_ref[2:3, :] = dg1a_ref[0:1, :] + dg1b_ref[0:1, :]
        o_ref[3:4, :] = dsh2_ref[0:1, :]
        o_ref[4:5, :] = p2v * n2_ref[...]
        o_ref[5:6, :] = dg2_ref[0:1, :]
        o_ref[6:7, :] = p1v * (1.0 + s1_ref[...])
        o_ref[7:11, :] = dcw_ref[0:4, :]
        o_ref[11:12, :] = dcb_ref[0:1, :]
        o_ref[12:13, :] = p2v * (1.0 + s2_ref[...])
        o_ref[13:14, :] = dfn_ref[0:1, :]
        o_ref[14:15, 0:QW] = da_ref[0:1, :]
        o_ref[15:16, 0:SW] = ds_ref[0:1, :]
        o_ref[16:17, 0:128] = dhd_ref[0:1, :]
        o_ref[17:18, 0:128] = dhd_ref[1:2, :] * av_ref[...]
        o_ref[18:19, 0:128] = dhd_ref[2:3, :]
        o_ref[19:20, 0:128] = dsink_ref[0:1, :]
        o_ref[24:56, 0:128] = drel_ref[...]

    return pl.pallas_call(body, name="pack_small", out_shape=jax.ShapeDtypeStruct((56, D), F32))(
        dsh1, p1, dsh2, p2, dg1a, dg1b, dg2, norm1, norm2, scale1, scale2, dcw, dcb, dfn,
        dnw_attn, dnw_ssm, dhd, av, dsink, drel)


def _pad_row(a, rows=1):
    return jnp.pad(a.reshape(rows, -1), ((0, 0), (0, D - a.size // rows)))


def kernel(x, c, ada_w, ada_b, norm1, w_in, conv_w, conv_b, dt_bias, A_log, D_skip, sinks, attn_out_norm, ssm_out_norm, w_o, norm2, w_gate_up, w_down, rel_bias, final_norm, loss_target, m_ada_w, m_ada_b, m_norm1, m_w_in, m_conv_w, m_conv_b, m_dt_bias, m_A_log, m_D_skip, m_sinks, m_attn_out_norm, m_ssm_out_norm, m_w_o, m_norm2, m_w_gate_up, m_w_down, m_rel_bias, m_final_norm, v_ada_w, v_ada_b, v_norm1, v_w_in, v_conv_w, v_conv_b, v_dt_bias, v_A_log, v_D_skip, v_sinks, v_attn_out_norm, v_ssm_out_norm, v_w_o, v_norm2, v_w_gate_up, v_w_down, v_rel_bias, v_final_norm):
    xi, yi, ci = lax.axis_index("x"), lax.axis_index("y"), lax.axis_index("c")
    chip = 2 * xi + yi
    me = 4 * xi + 2 * yi + ci
    chip_arr = jnp.reshape(chip, (1,)).astype(jnp.int32)
    core_arr = jnp.reshape(ci, (1,)).astype(jnp.int32)
    xs2, tgt = x[0], loss_target[0]

    first = jnp.concatenate([c, _pad_row(conv_w[0], CK), jnp.zeros((3, D), F32)], axis=0)
    first_all = _gather8("gather_cond", first)
    c_all = first_all[:, 0, :]
    cw_full = jnp.concatenate([first_all[2 * j, 1:1 + CK, 0:256] for j in range(4)], axis=1)
    w_in_g, w_o_g, w_gu_g, w_dn_g = _gather_chips(
        "gather_weights", [w_in[0].astype(BF16), w_o[0].astype(BF16), w_gate_up[0].astype(BF16), w_down[0].astype(BF16)])
    w_in_f = jnp.pad(jnp.transpose(w_in_g, (1, 0, 2)).reshape(D, IN_W), ((0, 0), (0, PROJ_W - IN_W)))
    w_o_f = w_o_g.reshape(D, D)
    w_gu_f = jnp.transpose(w_gu_g, (1, 0, 2)).reshape(D, 2 * DFF)
    w_dn_f = w_dn_g.reshape(DFF, D)

    ncol = ada_w.shape[2]
    mod_cols = _ada_fwd(c_all, ada_w[0], lax.dynamic_slice(ada_b, (0, chip * ncol), (1, ncol)))
    mod_all = _gather_chips("gather_mod", [mod_cols])[0]
    mod = lax.dynamic_slice(jnp.transpose(mod_all, (1, 0, 2)).reshape(8, 4 * ncol), (me, 0), (1, 4 * ncol))
    shift1, scale1, gate1, shift2, scale2, gate2 = [mod[:, j * D:(j + 1) * D] for j in range(6)]
    a1 = norm1 * (1.0 + scale1)
    a2 = norm2 * (1.0 + scale2)

    q, kv, z, xbc, dtr = _in_proj_fwd(xs2, a1, shift1, w_in_f)
    bucket, mask = _attn_geometry()
    bias = jnp.where(mask[None], jnp.transpose(rel_bias[bucket], (2, 0, 1)), NEG)
    sinks1 = sinks[0]
    ya = _attn_fwd(q, kv, bias, sinks1, attn_out_norm)
    cw8 = jnp.concatenate([cw_full, jnp.zeros((4, XBCW), F32)], axis=0)
    dtb = _pad_row(dt_bias)[:, 0:128]
    av = _pad_row(-jnp.exp(A_log))[:, 0:128]
    dk = jnp.repeat(D_skip, HD, axis=1)
    ys, hs = _ssd_fwd(z, xbc, dtr, cw8, conv_b, dtb, av, dk, ssm_out_norm)
    fn = final_norm[None, :]
    x1, gu, dx2, loss_acc, dfn = _mlp_fwd(xs2, ya, ys, tgt, w_o_f, w_gu_f, w_dn_f, gate1, a2, shift2, gate2, fn)
    loss = lax.psum(loss_acc[0, 0], ("x", "y", "c"))

    dx1, dya, dys, act, dgu, h2, dsh2, p2 = _mlp_bwd(x1, gu, dx2, w_o_f, w_gu_f, w_dn_f, gate1, a2, shift2, gate2)
    g_dn, dg2 = _wgrad("wgrad_down", act, dx2, D // 2, gate2, w_dn_f)
    g_gu = _wgrad("wgrad_gate_up", h2, dgu, DFF // 2)[0]
    g_oa, dg1a = _wgrad("wgrad_o_attn", ya, dx1, D, gate1, w_o_f[0:QW])
    g_os, dg1b = _wgrad("wgrad_o_ssm", ys, dx1, D, gate1, w_o_f[QW:D])
    dq, dkv, dbias, dsink, dnw_attn = _attn_bwd(q, kv, dya, bias, sinks1, attn_out_norm)
    drel = _rel_bias_grad(dbias, jnp.asarray(bucket))
    dz, dxbc, ddt, dcw, dcb, dnw_ssm, dhd = _ssd_bwd(z, xbc, dtr, dys, hs, cw8, conv_b, dtb, av, dk, ssm_out_norm)
    grad_x, dproj, h1, dsh1, p1 = _in_proj_bwd(xs2, dx1, a1, shift1, w_in_f, dq, dkv, dz, dxbc, ddt)
    g_in = _wgrad("wgrad_in", h1, dproj, PROJ_W)[0]

    small = _pack_small(dsh1, p1, dsh2, p2, dg1a, dg1b, dg2, norm1, norm2, scale1, scale2, dcw, dcb, dfn,
                        dnw_attn, dnw_ssm, dhd, av, dsink, drel)
    small_all = _gather8("gather_small", small)
    gsum = _sum8(small_all)

    def conv_rows(a):
        return lax.dynamic_update_slice(jnp.zeros((CK, D), F32), a[0], (0, chip * 256))

    def pack(p):
        rows = [jnp.zeros((6, D), F32), p["norm1"], conv_rows(p["conv_w"]), p["conv_b"], p["norm2"],
                p["final_norm"][None, :], _pad_row(p["attn_out_norm"]), _pad_row(p["ssm_out_norm"]),
                _pad_row(p["dt_bias"]), _pad_row(p["A_log"]), _pad_row(p["D_skip"]), _pad_row(p["sinks"]),
                jnp.zeros((4, D), F32), _pad_row(p["rel_bias"], NBUCKET)]
        return jnp.concatenate(rows, axis=0)

    names = ["norm1", "conv_w", "conv_b", "norm2", "final_norm", "attn_out_norm", "ssm_out_norm", "dt_bias",
             "A_log", "D_skip", "sinks", "rel_bias"]
    ws = dict(zip(names, [norm1, conv_w, conv_b, norm2, final_norm, attn_out_norm, ssm_out_norm, dt_bias, A_log,
                          D_skip, sinks, rel_bias]))
    ms = dict(zip(names, [m_norm1, m_conv_w, m_conv_b, m_norm2, m_final_norm, m_attn_out_norm, m_ssm_out_norm,
                          m_dt_bias, m_A_log, m_D_skip, m_sinks, m_rel_bias]))
    vs = dict(zip(names, [v_norm1, v_conv_w, v_conv_b, v_norm2, v_final_norm, v_attn_out_norm, v_ssm_out_norm,
                          v_dt_bias, v_A_log, v_D_skip, v_sinks, v_rel_bias]))
    wp = pack(ws).at[0:6].set(ada_b.reshape(6, D))
    mp = pack(ms).at[0:6].set(m_ada_b.reshape(6, D))
    vp = pack(vs).at[0:6].set(v_ada_b.reshape(6, D))
    sd, sm, sv = _adamw_rows("adamw_small", wp, gsum, mp, vp, 56)

    def unpack(p):
        conv = lax.dynamic_slice(p[7:11], (0, chip * 256), (CK, 256))[None]
        return dict(ada_b=p[0:6].reshape(1, 6 * D), norm1=p[6:7], conv_w=conv, conv_b=p[11:12], norm2=p[12:13],
                    final_norm=p[13], attn_out_norm=p[14:15, 0:QW], ssm_out_norm=p[15:16, 0:SW],
                    dt_bias=p[16:17, 0:NH], A_log=p[17:18, 0:NH], D_skip=p[18:19, 0:NH], sinks=p[19:20, 0:NH],
                    rel_bias=p[24:56, 0:NH])

    small_out = [unpack(p) for p in (gsum, sd, sm, sv)]

    dmod_all = small_all[:, 0:6, :].reshape(8, 6 * D)
    dmod_loc = lax.dynamic_slice(dmod_all, (0, chip * ncol), (8, ncol))
    ada_out = _ada_bwd_adamw(c_all.T, dmod_loc, ada_w[0], m_ada_w[0], v_ada_w[0])

    def by_cols(g, n):
        return jnp.transpose(g.reshape(D, 4, n), (1, 0, 2)).reshape(4, 2, D // 2, n)

    g_o = jnp.concatenate([g_oa, g_os], axis=0)
    parts = [by_cols(g_in[:, 0:IN_W], IN_W // 4), g_o.reshape(4, 2, D // 8, D), by_cols(g_gu, 2 * DFF // 4),
             g_dn.reshape(4, 2, DFF // 8, D)]
    nw_ = len(parts)

    def plan1(k, x_, y_, c_):
        return [(a, (j, 1 - c_), a, j) for a in range(nw_) for j in range(4)]

    got1 = _exchange("rs_sibling", _SIBLING, parts, [jax.ShapeDtypeStruct((4,) + p.shape[2:], F32) for p in parts], plan1)
    s4 = [_add_half("rs_add_half_%d" % a, parts[a], got1[a], core_arr) for a in range(nw_)]

    def plan2(k, x_, y_, c_):
        return [(a, jnp.bitwise_xor(2 * x_ + y_, k + 1), a, k) for a in range(nw_)]

    got2 = _exchange("rs_chips", _CHIPS3, s4, [jax.ShapeDtypeStruct((3,) + p.shape[1:], F32) for p in s4], plan2)
    mine = [_add_chips("rs_add_chips_%d" % a, s4[a], got2[a], chip_arr) for a in range(nw_)]

    def plan3(k, x_, y_, c_):
        return [(a, 0, a, 0) for a in range(nw_)]

    got3 = _exchange("rs_back", _SIBLING, [t[None] for t in mine], [jax.ShapeDtypeStruct((1,) + t.shape, F32) for t in mine], plan3)
    big = []
    for a, (w_, m_, v_) in enumerate([(w_in, m_w_in, v_w_in), (w_o, m_w_o, v_w_o), (w_gate_up, m_w_gate_up, v_w_gate_up),
                                      (w_down, m_w_down, v_w_down)]):
        shp = (2,) + mine[a].shape
        outs = _adamw_halves("adamw_big_%d" % a, mine[a], got3[a][0], w_[0].reshape(shp), m_[0].reshape(shp),
                             v_[0].reshape(shp), core_arr)
        big.append([o.reshape(w_.shape) for o in outs])

    order = ["ada_w", "ada_b", "norm1", "w_in", "conv_w", "conv_b", "dt_bias", "A_log", "D_skip", "sinks",
             "attn_out_norm", "ssm_out_norm", "w_o", "norm2", "w_gate_up", "w_down", "rel_bias", "final_norm"]
    bigname = {"w_in": 0, "w_o": 1, "w_gate_up": 2, "w_down": 3}
    res = [loss, grad_x[None]]
    for kind in range(4):
        for nm in order:
            if nm == "ada_w":
                res.append(ada_out[kind][None])
            elif nm in bigname:
                res.append(big[bigname[nm]][kind])
            else:
                res.append(small_out[kind][nm])
    return tuple(res)
```

```python
import numpy as np
import jax
import jax.numpy as jnp
from jax import lax
from jax.experimental import pallas as pl
from jax.experimental.pallas import tpu as pltpu

F32, BF16 = jnp.float32, jnp.bfloat16
HI = lax.Precision.HIGHEST

D = 1024
QW, KVW = 512, 128
NH, HD, NKV = 8, 64, 2
SW = 512
NST = 128
XBCW = 1024
CK = 4
BLK = 128
DFF = 2816
IN_W = 2312
PROJ_W = 2432
EPS = 1e-6
NEG = -1e30
NBUCKET = 32

B1, B2, LR, AEPS, WD, STEP = 0.9, 0.999, 0.001, 1e-08, 0.01, 10

VMEM_LIMIT = 56 * 1024 * 1024

_NT = (((1,), (1,)), ((), ()))
_TN = (((0,), (0,)), ((), ()))


def _mm(a, b):
    return jnp.dot(a, b, preferred_element_type=F32)


def _mm_nt(a, b):
    return lax.dot_general(a, b, _NT, preferred_element_type=F32)


def _mm_tn(a, b):
    return lax.dot_general(a, b, _TN, preferred_element_type=F32)


def _mm_hi(a, b):
    return jnp.dot(a, b, preferred_element_type=F32, precision=HI)


def _sig(x):
    return 1.0 / (1.0 + jnp.exp(-x))


def _cp(sem):
    return pltpu.CompilerParams(dimension_semantics=sem, vmem_limit_bytes=VMEM_LIMIT)


def _row(shape):
    nd = len(shape)
    return pl.BlockSpec(shape, lambda *_: (0,) * nd)


def _adamw(w, g, m, v):
    m = B1 * m + (1.0 - B1) * g
    v = B2 * v + (1.0 - B2) * (g * g)
    m_hat = m / (1.0 - B1 ** STEP)
    v_hat = v / (1.0 - B2 ** STEP)
    delta = -LR * (m_hat / (jnp.sqrt(v_hat) + AEPS) + WD * w)
    return delta, m, v


class _Carry:
    def __init__(self, inps, outs, copies):
        self.inps, self.outs, self.copies = list(inps), list(outs), copies
        self.n = len(copies(0, 0, 0))

    def descriptors(self, in_refs, out_refs, send_sems, recv_sems):
        x, y, c = lax.axis_index("x"), lax.axis_index("y"), lax.axis_index("c")
        out = []
        for j, (flip, a, si, o, di) in enumerate(self.copies(x, y, c)):
            if flip is None:
                out.append(pltpu.make_async_copy(in_refs[a].at[si], out_refs[o].at[di], send_sems.at[j]))
            else:
                fx, fy, fc = flip
                peer = (1 - x if fx else x, 1 - y if fy else y, 1 - c if fc else c)
                out.append(pltpu.make_async_remote_copy(
                    src_ref=in_refs[a].at[si], dst_ref=out_refs[o].at[di],
                    send_sem=send_sems.at[j], recv_sem=recv_sems.at[j],
                    device_id=peer, device_id_type=pl.DeviceIdType.MESH))
        return out


def _pcall(body, args, *, name, grid, in_specs, out_specs, out_shape, scratch_shapes=(), sem=None, nprefetch=0,
           carry=None):
    out_shape, out_specs = list(out_shape), list(out_specs)
    in_specs, scratch_shapes = list(in_specs), list(scratch_shapes)
    nin, nout, nscr = len(in_specs), len(out_shape), len(scratch_shapes)
    run = body
    if carry is not None:
        ncin, ncout = len(carry.inps), len(carry.outs)
        hbm = pl.BlockSpec(memory_space=pl.ANY)

        def run(*refs):
            pre, r = refs[:nprefetch], refs[nprefetch:]
            ins, cins = r[:nin], r[nin:nin + ncin]
            r = r[nin + ncin:]
            outs, couts = r[:nout], r[nout:nout + ncout]
            r = r[nout + ncout:]
            scr, (send_sems, recv_sems) = r[:nscr], r[nscr:]
            first = pl.program_id(0) == 0
            last = pl.program_id(0) == grid[0] - 1
            for ax in range(1, len(grid)):
                first = jnp.logical_and(first, pl.program_id(ax) == 0)
                last = jnp.logical_and(last, pl.program_id(ax) == grid[ax] - 1)

            @pl.when(first)
            def _():
                for d in carry.descriptors(cins, couts, send_sems, recv_sems):
                    d.start()

            body(*pre, *ins, *outs, *scr)

            @pl.when(last)
            def _():
                for d in carry.descriptors(cins, couts, send_sems, recv_sems):
                    d.wait()

        in_specs = in_specs + [hbm] * ncin
        out_specs = out_specs + [hbm] * ncout
        out_shape = out_shape + carry.outs
        scratch_shapes = scratch_shapes + [pltpu.SemaphoreType.DMA((carry.n,)), pltpu.SemaphoreType.DMA((carry.n,))]
        args = list(args) + carry.inps
    if sem is None:
        sem = ("arbitrary",) * len(grid)
    if nprefetch:
        kw = dict(grid_spec=pltpu.PrefetchScalarGridSpec(num_scalar_prefetch=nprefetch, grid=grid, in_specs=in_specs,
                                                         out_specs=out_specs, scratch_shapes=scratch_shapes))
    else:
        kw = dict(grid=grid, in_specs=in_specs, out_specs=out_specs, scratch_shapes=scratch_shapes)
    res = pl.pallas_call(run, name=name, out_shape=out_shape, compiler_params=_cp(sem), **kw)(*args)
    return list(res)


def _merge(*carries):
    inps, outs, offs = [], [], []
    for cr in carries:
        offs.append((len(inps), len(outs)))
        inps += cr.inps
        outs += cr.outs

    def copies(x, y, c):
        return [(f, a + io, si, o + oo, di) for cr, (io, oo) in zip(carries, offs) for f, a, si, o, di in cr.copies(x, y, c)]

    return _Carry(inps, outs, copies)


def _exchange(name, carry):
    return _pcall(lambda: None, [], name=name, grid=(1,), in_specs=[], out_specs=[], out_shape=[], carry=carry)


_ALL7 = [(f >> 2 & 1, f >> 1 & 1, f & 1) for f in range(1, 8)]
_CHIPS3 = [(0, 1, 0), (1, 0, 0), (1, 1, 0)]
_SIBLING = (0, 0, 1)


def _gather8_carry(blk):
    def copies(x, y, c):
        me = 4 * x + 2 * y + c
        return [(None, 0, 0, 0, me)] + [(f, 0, 0, 0, me) for f in _ALL7]

    return _Carry([blk[None]], [jax.ShapeDtypeStruct((8,) + blk.shape, blk.dtype)], copies)


def _gather_chips_carry(blks):
    def copies(x, y, c):
        chip = 2 * x + y
        return [(f, a, 0, a, chip) for a in range(len(blks)) for f in [None] + _CHIPS3]

    return _Carry([b[None] for b in blks], [jax.ShapeDtypeStruct((4,) + b.shape, b.dtype) for b in blks], copies)


def _ada_fwd(c_all, w_loc, b_loc):
    n = w_loc.shape[1]
    tn = 512

    def body(c_ref, w_ref, b_ref, o_ref):
        cv = c_ref[...]
        cond = cv * _sig(cv)
        o_ref[...] = _mm_hi(cond, w_ref[...]) + b_ref[...]

    return pl.pallas_call(
        body, name="ada_fwd", grid=(n // tn,),
        out_shape=jax.ShapeDtypeStruct((8, n), F32),
        in_specs=[_row((8, D)), pl.BlockSpec((D, tn), lambda j: (0, j)), pl.BlockSpec((1, tn), lambda j: (0, j))],
        out_specs=pl.BlockSpec((8, tn), lambda j: (0, j)),
        compiler_params=_cp(("parallel",)),
    )(c_all, w_loc, b_loc)


def _ada_bwd_adamw(c_all_t, dmod_loc, w, m, v, carry=None):
    n = w.shape[1]
    tn = 512

    def body(ct_ref, dm_ref, w_ref, m_ref, v_ref, g_ref, d_ref, mo_ref, vo_ref):
        ct = ct_ref[...]
        cond = ct * _sig(ct)
        dm = dm_ref[...]
        g = cond[:, 0:1] * dm[0:1, :]
        for b in range(1, 8):
            g = g + cond[:, b:b + 1] * dm[b:b + 1, :]
        g_ref[...] = g
        d_ref[...], mo_ref[...], vo_ref[...] = _adamw(w_ref[...], g, m_ref[...], v_ref[...])

    wspec = pl.BlockSpec((D, tn), lambda j: (0, j))
    return _pcall(
        body, [c_all_t, dmod_loc, w, m, v], name="ada_bwd_adamw", grid=(n // tn,),
        out_shape=[jax.ShapeDtypeStruct((D, n), F32)] * 4,
        in_specs=[_row((D, 8)), pl.BlockSpec((8, tn), lambda j: (0, j)), wspec, wspec, wspec],
        out_specs=[wspec] * 4, carry=carry)


def _in_proj_fwd(x, a1, sh1, w_in, carry=None):
    s = x.shape[0]
    tm = 512

    def body(x_ref, a_ref, s_ref, w_ref, q_ref, kv_ref, z_ref, xbc_ref, dt_ref):
        xv = x_ref[...]
        r = lax.rsqrt(jnp.mean(xv * xv, axis=-1, keepdims=True) + EPS)
        h = (xv * r * a_ref[...] + s_ref[...]).astype(BF16)
        p = _mm(h, w_ref[...])
        q_ref[...] = p[:, 0:512].astype(BF16)
        kv_ref[...] = p[:, 512:768].astype(BF16)
        z_ref[...] = p[:, 768:1280]
        xbc_ref[...] = p[:, 1280:2304]
        dt_ref[...] = p[:, 2304:2432]

    def tok(w):
        return pl.BlockSpec((tm, w), lambda i: (i, 0))

    return _pcall(
        body, [x, a1, sh1, w_in], name="in_proj_fwd", grid=(s // tm,),
        out_shape=[jax.ShapeDtypeStruct((s, QW), BF16), jax.ShapeDtypeStruct((s, 2 * KVW), BF16),
                   jax.ShapeDtypeStruct((s, SW), F32), jax.ShapeDtypeStruct((s, XBCW), F32),
                   jax.ShapeDtypeStruct((s, 128), F32)],
        in_specs=[tok(D), _row((1, D)), _row((1, D)), _row((D, PROJ_W))],
        out_specs=[tok(QW), tok(2 * KVW), tok(SW), tok(XBCW), tok(128)], carry=carry)


def _in_proj_bwd(x, dx1, a1, sh1, w_in, dq, dkv, dz, dxbc, ddt, carry=None):
    s = x.shape[0]
    tm = 512

    def body(x_ref, dx1_ref, a_ref, s_ref, w_ref, dq_ref, dkv_ref, dz_ref, dxbc_ref, ddt_ref,
             gx_ref, dproj_ref, h_ref, dsh_ref, p_ref):
        i = pl.program_id(0)

        @pl.when(i == 0)
        def _():
            dsh_ref[...] = jnp.zeros_like(dsh_ref)
            p_ref[...] = jnp.zeros_like(p_ref)

        dproj = jnp.concatenate([dq_ref[...], dkv_ref[...], dz_ref[...], dxbc_ref[...], ddt_ref[...]], axis=1)
        dproj_ref[...] = dproj
        dh = _mm_nt(dproj, w_ref[...])
        xv = x_ref[...]
        r = lax.rsqrt(jnp.mean(xv * xv, axis=-1, keepdims=True) + EPS)
        xn = xv * r
        a = a_ref[...]
        h_ref[...] = (xn * a + s_ref[...]).astype(BF16)
        dsh_ref[0:1, :] += jnp.sum(dh, axis=0, keepdims=True)
        p_ref[0:1, :] += jnp.sum(dh * xn, axis=0, keepdims=True)
        u = dh * a
        gx_ref[...] = dx1_ref[...] + r * u - xn * (r * jnp.mean(u * xn, axis=-1, keepdims=True))

    def tok(w):
        return pl.BlockSpec((tm, w), lambda i: (i, 0))

    return _pcall(
        body, [x, dx1, a1, sh1, w_in, dq, dkv, dz, dxbc, ddt], name="in_proj_bwd", grid=(s // tm,),
        out_shape=[jax.ShapeDtypeStruct((s, D), F32), jax.ShapeDtypeStruct((s, PROJ_W), BF16),
                   jax.ShapeDtypeStruct((s, D), BF16), jax.ShapeDtypeStruct((8, D), F32),
                   jax.ShapeDtypeStruct((8, D), F32)],
        in_specs=[tok(D), tok(D), _row((1, D)), _row((1, D)), _row((D, PROJ_W)),
                  tok(QW), tok(2 * KVW), tok(SW), tok(XBCW), tok(128)],
        out_specs=[tok(D), tok(PROJ_W), tok(D), _row((8, D)), _row((8, D))], carry=carry)


def _attn_geometry():
    dist = np.arange(BLK)[:, None] + BLK - np.arange(2 * BLK)[None, :]
    n = np.maximum(dist, 0)
    max_exact = NBUCKET // 2
    large = max_exact + (np.log(np.maximum(n, 1) / max_exact) / np.log(128 / max_exact)
                         * (NBUCKET - max_exact)).astype(np.int32)
    large = np.minimum(large, NBUCKET - 1)
    bucket = np.where(n < max_exact, n, large).astype(np.int32)
    mask = (dist >= 0) & (dist < 128)
    return bucket, mask


def _attn_heads(n, q_ref, kvp_ref, kvc_ref, bias_ref, sinks_ref):
    qv = q_ref[...] * 0.125
    kvw = jnp.concatenate([kvp_ref[...], kvc_ref[...]], axis=0)
    col = lax.broadcasted_iota(jnp.int32, (BLK, 2 * BLK), 1)
    first = jnp.where(jnp.logical_and(n == 0, col < BLK), NEG, 0.0)
    groups = []
    for g in range(NKV):
        qs = jnp.concatenate([qv[:, (4 * g + r) * HD:(4 * g + r + 1) * HD] for r in range(4)], axis=0)
        kw = kvw[:, g * HD:(g + 1) * HD]
        vw = kvw[:, KVW + g * HD:KVW + (g + 1) * HD]
        sc = _mm_nt(qs, kw)
        pn, ps = [], []
        for r in range(4):
            h = 4 * g + r
            sr = sc[r * BLK:(r + 1) * BLK] + bias_ref[h] + first
            sink = sinks_ref[h]
            m = jnp.maximum(jnp.max(sr, axis=-1, keepdims=True), sink)
            p = jnp.exp(sr - m)
            es = jnp.exp(sink - m)
            inv = 1.0 / (jnp.sum(p, axis=-1, keepdims=True) + es)
            pn.append(p * inv)
            ps.append(es * inv)
        pn = jnp.concatenate(pn, axis=0)
        ps = jnp.concatenate(ps, axis=0)
        o = _mm(pn.astype(BF16), vw)
        groups.append((qs, kw, vw, pn, ps, o))
    return groups


def _unstack_heads(parts):
    return jnp.concatenate([p[r * BLK:(r + 1) * BLK] for p in parts for r in range(4)], axis=1)


def _attn_fwd(q, kv, bias, sinks, nw, carry=None):
    s = q.shape[0]

    def body(q_ref, kvp_ref, kvc_ref, bias_ref, sinks_ref, nw_ref, y_ref):
        n = pl.program_id(0)
        groups = _attn_heads(n, q_ref, kvp_ref, kvc_ref, bias_ref, sinks_ref)
        o = _unstack_heads([g[5] for g in groups])
        r = lax.rsqrt(jnp.mean(o * o, axis=-1, keepdims=True) + EPS)
        y_ref[...] = (o * r * nw_ref[...]).astype(BF16)

    return _pcall(
        body, [q, kv, kv, bias, sinks, nw], name="attn_fwd", grid=(s // BLK,),
        out_shape=[jax.ShapeDtypeStruct((s, QW), BF16)],
        in_specs=[pl.BlockSpec((BLK, QW), lambda n: (n, 0)),
                  pl.BlockSpec((BLK, 2 * KVW), lambda n: (jnp.maximum(n - 1, 0), 0)),
                  pl.BlockSpec((BLK, 2 * KVW), lambda n: (n, 0)),
                  _row((NH, BLK, 2 * BLK)),
                  pl.BlockSpec(memory_space=pltpu.SMEM),
                  _row((1, QW))],
        out_specs=[pl.BlockSpec((BLK, QW), lambda n: (n, 0))], carry=carry)


def _attn_bwd(q, kv, dya, bias, sinks, nw, carry=None):
    s = q.shape[0]
    nb = s // BLK

    def body(q_ref, kvp_ref, kvc_ref, dy_ref, bias_ref, sinks_ref, nw_ref,
             dq_ref, dkv_ref, dbias_ref, dsink_ref, dnw_ref, carry_ref):
        n = pl.program_id(0)

        @pl.when(n == 0)
        def _():
            carry_ref[...] = jnp.zeros_like(carry_ref)
            dbias_ref[...] = jnp.zeros_like(dbias_ref)
            dsink_ref[...] = jnp.zeros_like(dsink_ref)
            dnw_ref[...] = jnp.zeros_like(dnw_ref)

        @pl.when(n < nb)
        def _():
            groups = _attn_heads(n, q_ref, kvp_ref, kvc_ref, bias_ref, sinks_ref)
            o = _unstack_heads([g[5] for g in groups])
            r = lax.rsqrt(jnp.mean(o * o, axis=-1, keepdims=True) + EPS)
            dy = dy_ref[...]
            on = o * r
            dnw_ref[0:1, :] += jnp.sum(dy * on, axis=0, keepdims=True)
            u = dy * nw_ref[...]
            do = r * u - on * (r * jnp.mean(u * on, axis=-1, keepdims=True))
            dq_parts, dk_parts, dv_parts = [], [], []
            for g, (qs, kw, vw, pn, ps, og) in enumerate(groups):
                dos = jnp.concatenate([do[:, (4 * g + r_) * HD:(4 * g + r_ + 1) * HD] for r_ in range(4)], axis=0)
                delta = jnp.sum(dos * og, axis=-1, keepdims=True)
                dp = _mm_nt(dos.astype(BF16), vw)
                ds = pn * (dp - delta)
                dsk = ps * delta
                lane = lax.broadcasted_iota(jnp.int32, (1, 128), 1)
                for r_ in range(4):
                    h = 4 * g + r_
                    dbias_ref[h] += ds[r_ * BLK:(r_ + 1) * BLK]
                    dsink_ref[0:1, :] -= jnp.where(lane == h, jnp.sum(dsk[r_ * BLK:(r_ + 1) * BLK]), 0.0)
                dsb = ds.astype(BF16)
                dq_parts.append(_mm(dsb, kw) * 0.125)
                dk_parts.append(_mm_tn(dsb, qs))
                dv_parts.append(_mm_tn(pn.astype(BF16), dos.astype(BF16)))
            dq_ref[...] = _unstack_heads(dq_parts).astype(BF16)
            dkvw = jnp.concatenate(dk_parts + dv_parts, axis=1)
            dkv_ref[...] = (carry_ref[...] + dkvw[0:BLK]).astype(BF16)
            carry_ref[...] = dkvw[BLK:2 * BLK]

        @pl.when(n == nb)
        def _():
            dkv_ref[...] = carry_ref[...].astype(BF16)

    last = nb - 1
    return _pcall(
        body, [q, kv, kv, dya, bias, sinks, nw], name="attn_bwd", grid=(nb + 1,),
        out_shape=[jax.ShapeDtypeStruct((s, QW), BF16), jax.ShapeDtypeStruct((s, 2 * KVW), BF16),
                   jax.ShapeDtypeStruct((NH, BLK, 2 * BLK), F32), jax.ShapeDtypeStruct((NH, 128), F32),
                   jax.ShapeDtypeStruct((8, QW), F32)],
        in_specs=[pl.BlockSpec((BLK, QW), lambda n: (jnp.minimum(n, last), 0)),
                  pl.BlockSpec((BLK, 2 * KVW), lambda n: (jnp.clip(n - 1, 0, last), 0)),
                  pl.BlockSpec((BLK, 2 * KVW), lambda n: (jnp.minimum(n, last), 0)),
                  pl.BlockSpec((BLK, QW), lambda n: (jnp.minimum(n, last), 0)),
                  _row((NH, BLK, 2 * BLK)),
                  pl.BlockSpec(memory_space=pltpu.SMEM),
                  _row((1, QW))],
        out_specs=[pl.BlockSpec((BLK, QW), lambda n: (jnp.minimum(n, last), 0)),
                   pl.BlockSpec((BLK, 2 * KVW), lambda n: (jnp.maximum(n - 1, 0), 0)),
                   _row((NH, BLK, 2 * BLK)), _row((NH, 128)), _row((8, QW))],
        scratch_shapes=[pltpu.VMEM((BLK, 2 * KVW), F32)], carry=carry)


def _rel_bias_grad(dbias, bucket):
    def body(db_ref, bk_ref, o_ref):
        bk = bk_ref[...]
        lane = lax.broadcasted_iota(jnp.int32, (1, 128), 1)
        for b in range(NBUCKET):
            sel = bk == b
            row = jnp.zeros((1, 128), F32)
            for h in range(NH):
                row = row + jnp.where(lane == h, jnp.sum(jnp.where(sel, db_ref[h], 0.0)), 0.0)
            o_ref[b:b + 1, :] = row

    return pl.pallas_call(
        body, name="rel_bias_grad",
        out_shape=jax.ShapeDtypeStruct((NBUCKET, 128), F32),
    )(dbias, bucket)


def _ssd_consts():
    head_of_lane = np.arange(SW) // HD
    expand = (np.arange(128)[:, None] == head_of_lane[None, :]).astype(np.float32)
    tril = np.tril(np.ones((BLK, BLK), np.float32))
    return jnp.asarray(expand), jnp.asarray(expand.T.copy()), jnp.asarray(tril), jnp.asarray(tril.T.copy())


def _ssd_chunk_fwd(c, z_ref, xc_ref, xp_ref, dtr_ref, cw_ref, cb_ref, dtb_ref, a_ref, dk_ref, ex_ref, tril_ref, h_in):
    halo = jnp.where(c == 0, 0.0, xp_ref[BLK - 8:BLK, :])
    ext = jnp.concatenate([halo, xc_ref[...]], axis=0)
    cw = cw_ref[...]
    taps = [ext[8:8 + BLK] if k == CK - 1 else pltpu.roll(ext, CK - 1 - k, 0)[8:8 + BLK] for k in range(CK)]
    pre = cb_ref[...] + sum(cw[k:k + 1, :] * taps[k] for k in range(CK))
    sp = _sig(pre)
    xbc = pre * sp
    xs, bm, cm = xbc[:, 0:SW], xbc[:, SW:SW + 2 * NST], xbc[:, SW + 2 * NST:]
    dtin = dtr_ref[...] + dtb_ref[...]
    dt = jnp.maximum(dtin, 0.0) + jnp.log1p(jnp.exp(-jnp.abs(dtin)))
    av = a_ref[...]
    cs = _mm_hi(tril_ref[...], dt * av)
    cst = cs.T
    ex = ex_ref[...]
    dtx = _mm_hi(dt, ex)
    csx = _mm_hi(cs, ex)
    xdt = xs * dtx
    csl = csx[BLK - 1:BLK, :]
    decx = jnp.exp(csl - csx)
    ecsx = jnp.exp(csx)
    ecl = jnp.exp(csl)
    causal = tril_ref[...] > 0.5
    ydiag, yoff, cbs, lms = [], [], [], []
    for g in range(2):
        bg = bm[:, g * NST:(g + 1) * NST].astype(BF16)
        cg = cm[:, g * NST:(g + 1) * NST].astype(BF16)
        cb = _mm_nt(cg, bg)
        cbs.append(cb)
        yoff.append(_mm(cg, h_in[:, g * 256:(g + 1) * 256].astype(BF16)))
        for r in range(4):
            h = 4 * g + r
            seg = cs[:, h:h + 1] - cst[h:h + 1, :]
            lm = jnp.where(causal, jnp.exp(jnp.minimum(seg, 0.0)), 0.0)
            lms.append(lm)
            ydiag.append(_mm((cb * lm).astype(BF16), xdt[:, h * HD:(h + 1) * HD].astype(BF16)))
    yoff = jnp.concatenate(yoff, axis=1) * ecsx
    y = jnp.concatenate(ydiag, axis=1) + yoff + dk_ref[...] * xs
    return dict(ext=ext, taps=taps, pre=pre, sp=sp, xs=xs, bm=bm, cm=cm, dtin=dtin, dt=dt, av=av, cs=cs, cst=cst,
                dtx=dtx, csx=csx, xdt=xdt, decx=decx, ecsx=ecsx, ecl=ecl, causal=causal, cbs=cbs, lms=lms,
                yoff=yoff, y=y)


def _group_mean(t):
    m0 = jnp.mean(t[:, 0:256], axis=-1, keepdims=True)
    m1 = jnp.mean(t[:, 256:512], axis=-1, keepdims=True)
    return jnp.concatenate([jnp.broadcast_to(m0, (t.shape[0], 256)), jnp.broadcast_to(m1, (t.shape[0], 256))], axis=1)


def _ssd_specs(nc, rev):
    def cur(w):
        return pl.BlockSpec((BLK, w), (lambda i: (nc - 1 - i, 0)) if rev else (lambda i: (i, 0)))
    prev = pl.BlockSpec((BLK, XBCW), (lambda i: (jnp.maximum(nc - 2 - i, 0), 0)) if rev
                        else (lambda i: (jnp.maximum(i - 1, 0), 0)))
    return cur, prev


def _ssd_fwd(z, xbc, dtr, cw, cb, dtb, av, dk, nw, carry=None):
    s = z.shape[0]
    nc = s // BLK
    ex, _, tril, _ = _ssd_consts()

    def body(z_ref, xc_ref, xp_ref, dtr_ref, cw_ref, cb_ref, dtb_ref, a_ref, dk_ref, nw_ref, ex_ref, tril_ref,
             y_ref, hs_ref, h_ref):
        c = pl.program_id(0)

        @pl.when(c == 0)
        def _():
            h_ref[...] = jnp.zeros_like(h_ref)

        h_in = h_ref[...]
        hs_ref[0] = h_in
        f = _ssd_chunk_fwd(c, z_ref, xc_ref, xp_ref, dtr_ref, cw_ref, cb_ref, dtb_ref, a_ref, dk_ref, ex_ref,
                           tril_ref, h_in)
        dx = (f["decx"] * f["xdt"]).astype(BF16)
        st = [_mm_tn(f["bm"][:, g * NST:(g + 1) * NST].astype(BF16), dx[:, g * 256:(g + 1) * 256]) for g in range(2)]
        h_ref[...] = h_in * f["ecl"] + jnp.concatenate(st, axis=1)
        zv = z_ref[...]
        t = f["y"] * (zv * _sig(zv))
        r = lax.rsqrt(_group_mean(t * t) + EPS)
        y_ref[...] = (t * r * nw_ref[...]).astype(BF16)

    cur, prev = _ssd_specs(nc, False)
    return _pcall(
        body, [z, xbc, xbc, dtr, cw, cb, dtb, av, dk, nw, ex, tril], name="ssd_fwd", grid=(nc,),
        out_shape=[jax.ShapeDtypeStruct((s, SW), BF16), jax.ShapeDtypeStruct((nc, NST, SW), F32)],
        in_specs=[cur(SW), cur(XBCW), prev, cur(128), _row((8, XBCW)), _row((1, XBCW)), _row((1, 128)),
                  _row((1, 128)), _row((1, SW)), _row((1, SW)), _row((128, SW)), _row((BLK, BLK))],
        out_specs=[cur(SW), pl.BlockSpec((1, NST, SW), lambda i: (i, 0, 0))],
        scratch_shapes=[pltpu.VMEM((NST, SW), F32)], carry=carry)


def _ssd_bwd(z, xbc, dtr, dys, hs, cw, cb, dtb, av, dk, nw, carry=None):
    s = z.shape[0]
    nc = s // BLK
    ex, ext_t, tril, triu = _ssd_consts()

    def body(z_ref, xc_ref, xp_ref, dtr_ref, dy_ref, hs_ref, cw_ref, cb_ref, dtb_ref, a_ref, dk_ref, nw_ref,
             ex_ref, ext_ref, tril_ref, triu_ref,
             dz_ref, dxbc_ref, ddt_ref, dcw_ref, dcb_ref, dnw_ref, dhd_ref, dh_ref, nxt_ref, dd_ref):
        i = pl.program_id(0)
        c = nc - 1 - i

        @pl.when(i == 0)
        def _():
            dh_ref[...] = jnp.zeros_like(dh_ref)
            nxt_ref[...] = jnp.zeros_like(nxt_ref)
            dd_ref[...] = jnp.zeros_like(dd_ref)
            dcw_ref[...] = jnp.zeros_like(dcw_ref)
            dcb_ref[...] = jnp.zeros_like(dcb_ref)
            dnw_ref[...] = jnp.zeros_like(dnw_ref)
            dhd_ref[...] = jnp.zeros_like(dhd_ref)

        h_in = hs_ref[0]
        f = _ssd_chunk_fwd(c, z_ref, xc_ref, xp_ref, dtr_ref, cw_ref, cb_ref, dtb_ref, a_ref, dk_ref, ex_ref,
                           tril_ref, h_in)
        xs, xdt, decx, ecsx, ecl, dtx = f["xs"], f["xdt"], f["decx"], f["ecsx"], f["ecl"], f["dtx"]
        cs, cst, causal = f["cs"], f["cst"], f["causal"]
        causal_t = triu_ref[...] > 0.5

        zv = z_ref[...]
        sz = _sig(zv)
        gz = zv * sz
        t = f["y"] * gz
        r = lax.rsqrt(_group_mean(t * t) + EPS)
        tn_ = t * r
        dyn = dy_ref[...]
        dnw_ref[0:1, :] += jnp.sum(dyn * tn_, axis=0, keepdims=True)
        u = dyn * nw_ref[...]
        dt_ = r * u - tn_ * (r * _group_mean(u * tn_))
        dy = dt_ * gz
        dz_ref[...] = (dt_ * f["y"] * (sz * (1.0 + zv * (1.0 - sz)))).astype(BF16)

        dd_ref[0:1, :] += jnp.sum(dy * xs, axis=0, keepdims=True)
        dxs = dk_ref[...] * dy

        gst = dh_ref[...]
        edy = ecsx * dy
        dxdt, dbs, dcs_, dcsx_parts, dh_new = [], [], [], [], []
        lane = lax.broadcasted_iota(jnp.int32, (1, 128), 1)
        dcs_intra = jnp.zeros((BLK, 128), F32)
        for g in range(2):
            sl = slice(g * 256, (g + 1) * 256)
            bgf, cgf = f["bm"][:, g * NST:(g + 1) * NST], f["cm"][:, g * NST:(g + 1) * NST]
            bg, cg = bgf.astype(BF16), cgf.astype(BF16)
            gg = gst[:, sl].astype(BF16)
            hg = h_in[:, sl].astype(BF16)
            edyg = edy[:, sl].astype(BF16)
            dc = _mm_nt(edyg, hg)
            dh_new.append(gst[:, sl] * ecl[:, sl] + _mm_tn(cg, edyg))
            bgm = _mm(bg, gg)
            dxdt_g = decx[:, sl] * bgm
            dxg = (decx[:, sl] * xdt[:, sl]).astype(BF16)
            db = _mm_nt(dxg, gg)
            qd = bgm * xdt[:, sl] * decx[:, sl]
            last = jnp.sum(qd, axis=0, keepdims=True) + ecl[:, sl] * jnp.sum(gst[:, sl] * h_in[:, sl], axis=0, keepdims=True)
            rowid = lax.broadcasted_iota(jnp.int32, (BLK, 256), 0)
            dcsx_parts.append(f["yoff"][:, sl] * dy[:, sl] - qd + jnp.where(rowid == BLK - 1, last, 0.0))
            cb_ = f["cbs"][g]
            cbt = _mm_nt(bg, cg)
            dcb_ = jnp.zeros((BLK, BLK), F32)
            dcbt = jnp.zeros((BLK, BLK), F32)
            dxd = []
            for r_ in range(4):
                h = 4 * g + r_
                hl = slice(h * HD, (h + 1) * HD)
                lm = f["lms"][h]
                segt = cst[h:h + 1, :] - cs[:, h:h + 1]
                lmt = jnp.where(causal_t, jnp.exp(jnp.minimum(segt, 0.0)), 0.0)
                dyh = dy[:, hl].astype(BF16)
                xdh = xdt[:, hl].astype(BF16)
                dw = _mm_nt(dyh, xdh)
                dwt = _mm_nt(xdh, dyh)
                wt = cbt * lmt
                dxd.append(_mm(wt.astype(BF16), dyh))
                dcb_ = dcb_ + dw * lm
                dcbt = dcbt + dwt * lmt
                col = jnp.sum(dw * (cb_ * lm), axis=-1, keepdims=True) - jnp.sum(dwt * wt, axis=-1, keepdims=True)
                dcs_intra = dcs_intra + jnp.where(lane == h, col, 0.0)
            dxdt.append(dxdt_g + jnp.concatenate(dxd, axis=1))
            dcs_.append(dc + _mm(dcb_.astype(BF16), bg))
            dbs.append(db + _mm(dcbt.astype(BF16), cg))
        dh_ref[...] = jnp.concatenate(dh_new, axis=1)
        dxdt = jnp.concatenate(dxdt, axis=1)
        dxs = dxs + dxdt * dtx
        ext_t_ = ext_ref[...]
        dcs = dcs_intra + _mm_hi(jnp.concatenate(dcsx_parts, axis=1), ext_t_)
        da = _mm_hi(triu_ref[...], dcs)
        ddt = da * f["av"] + _mm_hi(dxdt * xs, ext_t_)
        dhd_ref[1:2, :] += jnp.sum(da * f["dt"], axis=0, keepdims=True)
        ddtr = ddt * _sig(f["dtin"])
        dhd_ref[0:1, :] += jnp.sum(ddtr, axis=0, keepdims=True)
        ddt_ref[...] = ddtr.astype(BF16)

        sp, pre = f["sp"], f["pre"]
        dact = jnp.concatenate([dxs] + dbs + dcs_, axis=1)
        dpre = dact * (sp * (1.0 + pre * (1.0 - sp)))
        dcb_ref[0:1, :] += jnp.sum(dpre, axis=0, keepdims=True)
        for k in range(CK):
            dcw_ref[k:k + 1, :] += jnp.sum(dpre * f["taps"][k], axis=0, keepdims=True)
        ext2 = jnp.concatenate([dpre, nxt_ref[...]], axis=0)
        cw = cw_ref[...]
        dxr = cw[CK - 1:CK, :] * dpre
        for k in range(CK - 1):
            dxr = dxr + cw[k:k + 1, :] * pltpu.roll(ext2, BLK + 8 - (CK - 1 - k), 0)[0:BLK]
        dxbc_ref[...] = dxr.astype(BF16)
        nxt_ref[...] = dpre[0:8]

        @pl.when(i == nc - 1)
        def _():
            dhd_ref[2:3, :] = _mm_hi(dd_ref[...], ext_t_)[0:1, :]

    cur, prev = _ssd_specs(nc, True)
    return _pcall(
        body, [z, xbc, xbc, dtr, dys, hs, cw, cb, dtb, av, dk, nw, ex, ext_t, tril, triu], name="ssd_bwd", grid=(nc,),
        out_shape=[jax.ShapeDtypeStruct((s, SW), BF16), jax.ShapeDtypeStruct((s, XBCW), BF16),
                   jax.ShapeDtypeStruct((s, 128), BF16), jax.ShapeDtypeStruct((8, XBCW), F32),
                   jax.ShapeDtypeStruct((8, XBCW), F32), jax.ShapeDtypeStruct((8, SW), F32),
                   jax.ShapeDtypeStruct((8, 128), F32)],
        in_specs=[cur(SW), cur(XBCW), prev, cur(128), cur(SW),
                  pl.BlockSpec((1, NST, SW), lambda i: (nc - 1 - i, 0, 0)),
                  _row((8, XBCW)), _row((1, XBCW)), _row((1, 128)), _row((1, 128)), _row((1, SW)), _row((1, SW)),
                  _row((128, SW)), _row((SW, 128)), _row((BLK, BLK)), _row((BLK, BLK))],
        out_specs=[cur(SW), cur(XBCW), cur(128), _row((8, XBCW)), _row((8, XBCW)), _row((8, SW)), _row((8, 128))],
        scratch_shapes=[pltpu.VMEM((NST, SW), F32), pltpu.VMEM((8, XBCW), F32), pltpu.VMEM((8, SW), F32)], carry=carry)


def _load_once(i, pairs, sem):
    @pl.when(i == 0)
    def _():
        cps = [pltpu.make_async_copy(src, dst, sem.at[k]) for k, (src, dst) in enumerate(pairs)]
        for cp in cps:
            cp.start()
        for cp in cps:
            cp.wait()


def _mlp_fwd(x, ya, ys, tgt, w_o, w_ga, w_gb, w_dn, gate1, a2, sh2, gate2, fn):
    s = x.shape[0]
    tm = 256

    def body(x_ref, ya_ref, ys_ref, t_ref, wo_hbm, wga_hbm, wgb_hbm, wdn_hbm, g1_ref, a2_ref, s2_ref, g2_ref, fn_ref,
             x1_ref, gu_ref, dx2_ref, loss_ref, dfn_ref, wo, wga, wgb, wdn, sem):
        i = pl.program_id(0)
        _load_once(i, [(wo_hbm, wo), (wga_hbm, wga), (wgb_hbm, wgb), (wdn_hbm, wdn)], sem)

        @pl.when(i == 0)
        def _():
            loss_ref[...] = jnp.zeros_like(loss_ref)
            dfn_ref[...] = jnp.zeros_like(dfn_ref)

        mix = _mm(ya_ref[...], wo[0:QW, :]) + _mm(ys_ref[...], wo[QW:D, :])
        x1 = x_ref[...] + g1_ref[...] * mix
        x1_ref[...] = x1
        r2 = lax.rsqrt(jnp.mean(x1 * x1, axis=-1, keepdims=True) + EPS)
        h2 = (x1 * r2 * a2_ref[...] + s2_ref[...]).astype(BF16)
        ha, hb = h2[:, 0:D // 2], h2[:, D // 2:D]
        gub = jnp.concatenate([(_mm(ha, wga[j]) + _mm(hb, wgb[j])).astype(BF16) for j in range(4)], axis=1)
        gu_ref[...] = gub
        gv, uv = gub[:, 0:DFF].astype(F32), gub[:, DFF:].astype(F32)
        act = (gv * _sig(gv) * uv).astype(BF16)
        x2 = x1 + g2_ref[...] * _mm(act, wdn[...])
        r3 = lax.rsqrt(jnp.mean(x2 * x2, axis=-1, keepdims=True) + EPS)
        xn = x2 * r3
        fnv = fn_ref[...]
        err = xn * fnv - t_ref[...]
        loss_ref[...] += jnp.sum(err * err) * (0.5 / D)
        dy = err * (1.0 / D)
        dfn_ref[0:1, :] += jnp.sum(dy * xn, axis=0, keepdims=True)
        u = dy * fnv
        dx2_ref[...] = r3 * u - xn * (r3 * jnp.mean(u * xn, axis=-1, keepdims=True))

    def tok(w):
        return pl.BlockSpec((tm, w), lambda i: (i, 0))

    hbm = pl.BlockSpec(memory_space=pl.ANY)
    return pl.pallas_call(
        body, name="mlp_fwd", grid=(s // tm,),
        out_shape=[jax.ShapeDtypeStruct((s, D), F32), jax.ShapeDtypeStruct((s, 2 * DFF), BF16),
                   jax.ShapeDtypeStruct((s, D), F32), jax.ShapeDtypeStruct((8, 128), F32),
                   jax.ShapeDtypeStruct((8, D), F32)],
        in_specs=[tok(D), tok(QW), tok(SW), tok(D), hbm, hbm, hbm, hbm,
                  _row((1, D)), _row((1, D)), _row((1, D)), _row((1, D)), _row((1, D))],
        out_specs=[tok(D), tok(2 * DFF), tok(D), _row((8, 128)), _row((8, D))],
        scratch_shapes=[pltpu.VMEM((D, D), BF16), pltpu.VMEM(w_ga.shape, BF16), pltpu.VMEM(w_gb.shape, BF16),
                        pltpu.VMEM((DFF, D), BF16), pltpu.SemaphoreType.DMA((4,))],
        compiler_params=_cp(("arbitrary",)),
    )(x, ya, ys, tgt, w_o, w_ga, w_gb, w_dn, gate1, a2, sh2, gate2, fn)


def _mlp_bwd(x1, gu, dx2, w_o, w_ga, w_gb, w_dn, gate1, a2, sh2, gate2):
    s = x1.shape[0]
    tm = 256
    nj = 2 * DFF // 4

    def body(x1_ref, gu_ref, dx2_ref, wo_hbm, wga_hbm, wgb_hbm, wdn_hbm, g1_ref, a2_ref, s2_ref, g2_ref,
             dx1_ref, dya_ref, dys_ref, act_ref, dgu_ref, h2_ref, dsh_ref, p_ref, wo, wga, wgb, wdn, sem):
        i = pl.program_id(0)
        _load_once(i, [(wo_hbm, wo), (wga_hbm, wga), (wgb_hbm, wgb), (wdn_hbm, wdn)], sem)

        @pl.when(i == 0)
        def _():
            dsh_ref[...] = jnp.zeros_like(dsh_ref)
            p_ref[...] = jnp.zeros_like(p_ref)

        dx2 = dx2_ref[...]
        dact = _mm_nt((dx2 * g2_ref[...]).astype(BF16), wdn[...])
        gub = gu_ref[...]
        gv, uv = gub[:, 0:DFF].astype(F32), gub[:, DFF:].astype(F32)
        sg = _sig(gv)
        sl = gv * sg
        act_ref[...] = (sl * uv).astype(BF16)
        dgu = jnp.concatenate([dact * uv * (sg * (1.0 + gv * (1.0 - sg))), dact * sl], axis=1).astype(BF16)
        dgu_ref[...] = dgu
        dha = sum(_mm_nt(dgu[:, j * nj:(j + 1) * nj], wga[j]) for j in range(4))
        dhb = sum(_mm_nt(dgu[:, j * nj:(j + 1) * nj], wgb[j]) for j in range(4))
        dh = jnp.concatenate([dha, dhb], axis=1)
        x1 = x1_ref[...]
        r2 = lax.rsqrt(jnp.mean(x1 * x1, axis=-1, keepdims=True) + EPS)
        xn = x1 * r2
        a2 = a2_ref[...]
        h2_ref[...] = (xn * a2 + s2_ref[...]).astype(BF16)
        dsh_ref[0:1, :] += jnp.sum(dh, axis=0, keepdims=True)
        p_ref[0:1, :] += jnp.sum(dh * xn, axis=0, keepdims=True)
        u = dh * a2
        dx1 = dx2 + r2 * u - xn * (r2 * jnp.mean(u * xn, axis=-1, keepdims=True))
        dx1_ref[...] = dx1
        dcat = _mm_nt((dx1 * g1_ref[...]).astype(BF16), wo[...])
        dya_ref[...] = dcat[:, 0:QW]
        dys_ref[...] = dcat[:, QW:D]

    def tok(w):
        return pl.BlockSpec((tm, w), lambda i: (i, 0))

    hbm = pl.BlockSpec(memory_space=pl.ANY)
    return pl.pallas_call(
        body, name="mlp_bwd", grid=(s // tm,),
        out_shape=[jax.ShapeDtypeStruct((s, D), F32), jax.ShapeDtypeStruct((s, QW), F32),
                   jax.ShapeDtypeStruct((s, SW), F32), jax.ShapeDtypeStruct((s, DFF), BF16),
                   jax.ShapeDtypeStruct((s, 2 * DFF), BF16), jax.ShapeDtypeStruct((s, D), BF16),
                   jax.ShapeDtypeStruct((8, D), F32), jax.ShapeDtypeStruct((8, D), F32)],
        in_specs=[tok(D), tok(2 * DFF), tok(D), hbm, hbm, hbm, hbm, _row((1, D)), _row((1, D)), _row((1, D)), _row((1, D))],
        out_specs=[tok(D), tok(QW), tok(SW), tok(DFF), tok(2 * DFF), tok(D), _row((8, D)), _row((8, D))],
        scratch_shapes=[pltpu.VMEM((D, D), BF16), pltpu.VMEM(w_ga.shape, BF16), pltpu.VMEM(w_gb.shape, BF16),
                        pltpu.VMEM((DFF, D), BF16), pltpu.SemaphoreType.DMA((4,))],
        compiler_params=_cp(("arbitrary",)),
    )(x1, gu, dx2, w_o, w_ga, w_gb, w_dn, gate1, a2, sh2, gate2)


def _wgrad(name, a, b, tn, gate=None, w=None, stacked=False, carry=None):
    s, m = a.shape
    n = b.shape[1]
    tk = 512
    nk = s // tk

    def body(*refs):
        if gate is None:
            a_ref, b_ref, o_ref = refs
        else:
            a_ref, b_ref, g_ref, w_ref, o_ref, dg_ref = refs
        k = pl.program_id(1)

        @pl.when(k == 0)
        def _():
            o_ref[...] = jnp.zeros_like(o_ref)

        o_ref[...] += _mm_tn(a_ref[...], b_ref[...].astype(BF16))

        if gate is not None:
            @pl.when(k == nk - 1)
            def _():
                acc = o_ref[...]
                dg_ref[...] = jnp.zeros_like(dg_ref)
                dg_ref[0:1, :] = jnp.sum(acc * w_ref[...].astype(F32), axis=0, keepdims=True)
                o_ref[...] = acc * g_ref[...]

    in_specs = [pl.BlockSpec((tk, m), lambda j, k: (k, 0)), pl.BlockSpec((tk, tn), lambda j, k: (k, j))]
    if stacked:
        out_shape = [jax.ShapeDtypeStruct((n // tn, m, tn), F32)]
        out_specs = [pl.BlockSpec((None, m, tn), lambda j, k: (j, 0, 0))]
    else:
        out_shape = [jax.ShapeDtypeStruct((m, n), F32)]
        out_specs = [pl.BlockSpec((m, tn), lambda j, k: (0, j))]
    args = [a, b]
    if gate is not None:
        in_specs += [pl.BlockSpec((1, tn), lambda j, k: (0, j)), pl.BlockSpec((m, tn), lambda j, k: (0, j))]
        out_shape.append(jax.ShapeDtypeStruct((8, n), F32))
        out_specs.append(pl.BlockSpec((8, tn), lambda j, k: (0, j)))
        args += [gate, w]
    return _pcall(body, args, name=name, grid=(n // tn, nk), out_shape=out_shape, in_specs=in_specs,
                  out_specs=out_specs, carry=carry)


def _add2(name, a, b):
    r = a.shape[0]
    tr = r // 5

    def body(a_ref, b_ref, o_ref):
        o_ref[...] = a_ref[...] + b_ref[...]

    spec = pl.BlockSpec((tr, D), lambda i: (i, 0))
    return pl.pallas_call(body, name=name, grid=(5,), out_shape=jax.ShapeDtypeStruct(a.shape, F32),
                          in_specs=[spec, spec], out_specs=spec, compiler_params=_cp(("parallel",)))(a, b)


def _add4(name, a, b3):
    r = a.shape[0]
    tr = r // 5

    def body(a_ref, b_ref, o_ref):
        o_ref[...] = ((a_ref[...] + b_ref[0]) + b_ref[1]) + b_ref[2]

    spec = pl.BlockSpec((tr, D), lambda i: (i, 0))
    return pl.pallas_call(body, name=name, grid=(5,), out_shape=jax.ShapeDtypeStruct(a.shape, F32),
                          in_specs=[spec, pl.BlockSpec((3, tr, D), lambda i: (0, i, 0))], out_specs=spec,
                          compiler_params=_cp(("parallel",)))(a, b3)


def _adamw_rows(name, w, g, m, v, tr):
    r = w.shape[0]

    def body(w_ref, g_ref, m_ref, v_ref, d_ref, mo_ref, vo_ref):
        d_ref[...], mo_ref[...], vo_ref[...] = _adamw(w_ref[...], g_ref[...], m_ref[...], v_ref[...])

    spec = pl.BlockSpec((tr, D), lambda i: (i, 0))
    return pl.pallas_call(body, name=name, grid=(r // tr,), out_shape=[jax.ShapeDtypeStruct(w.shape, F32)] * 3,
                          in_specs=[spec] * 4, out_specs=[spec] * 3, compiler_params=_cp(("parallel",)))(w, g, m, v)


def _sum8(parts):
    r = parts.shape[1]

    def body(p_ref, o_ref):
        acc = p_ref[0]
        for b in range(1, 8):
            acc = acc + p_ref[b]
        o_ref[...] = acc

    return pl.pallas_call(body, name="sum8", out_shape=jax.ShapeDtypeStruct((r, D), F32))(parts)


def _scalar_grid(grid, in_specs, out_specs):
    return pltpu.PrefetchScalarGridSpec(num_scalar_prefetch=1, grid=grid, in_specs=in_specs, out_specs=out_specs)


def _add_half(name, g, got, core):
    rr, cc = g.shape[2:]

    def body(c_ref, g_ref, r_ref, o_ref):
        o_ref[...] = g_ref[...] + r_ref[...]

    spec = pl.BlockSpec((None, rr, cc), lambda i, c_ref: (i, 0, 0))
    return pl.pallas_call(
        body, name=name, out_shape=jax.ShapeDtypeStruct(got.shape, F32),
        grid_spec=_scalar_grid((4,), [pl.BlockSpec((None, None, rr, cc), lambda i, c_ref: (i, c_ref[0], 0, 0)), spec], spec),
        compiler_params=_cp(("arbitrary",)),
    )(core, g, got)


def _add_chips(name, s4, got, chip):
    rr, cc = s4.shape[1:]
    tr = rr // 2

    def body(q_ref, s_ref, r_ref, o_ref):
        o_ref[...] = ((s_ref[...] + r_ref[0]) + r_ref[1]) + r_ref[2]

    return pl.pallas_call(
        body, name=name, out_shape=jax.ShapeDtypeStruct((rr, cc), F32),
        grid_spec=_scalar_grid((2,), [pl.BlockSpec((None, tr, cc), lambda i, q_ref: (q_ref[0], i, 0)),
                                      pl.BlockSpec((3, tr, cc), lambda i, q_ref: (0, i, 0))],
                               pl.BlockSpec((tr, cc), lambda i, q_ref: (i, 0))),
        compiler_params=_cp(("arbitrary",)),
    )(chip, s4, got)


def _adamw_halves(name, mine, got, w, m, v, core, carry=None):
    rr, cc = mine.shape
    tr = rr // 2

    def body(c_ref, t_ref, r_ref, w_ref, m_ref, v_ref, g_ref, d_ref, mo_ref, vo_ref):
        g = jnp.where(pl.program_id(0) == c_ref[0], t_ref[...], r_ref[...])
        g_ref[...] = g
        d_ref[...], mo_ref[...], vo_ref[...] = _adamw(w_ref[...], g, m_ref[...], v_ref[...])

    half = pl.BlockSpec((tr, cc), lambda h, i, c_ref: (i, 0))
    full = pl.BlockSpec((None, tr, cc), lambda h, i, c_ref: (h, i, 0))
    return _pcall(body, [core, mine, got, w, m, v], name=name, grid=(2, 2), nprefetch=1,
                  out_shape=[jax.ShapeDtypeStruct(w.shape, F32)] * 4,
                  in_specs=[half, half, full, full, full], out_specs=[full] * 4, carry=carry)


def _bias_table(rel_bias, bucket, mask):
    def body(rb_ref, bk_ref, mk_ref, o_ref):
        bk = bk_ref[...]
        valid = mk_ref[...] > 0
        for h in range(NH):
            acc = jnp.zeros((BLK, 2 * BLK), F32)
            for b in range(NBUCKET):
                acc = jnp.where(bk == b, rb_ref[b, h], acc)
            o_ref[h] = jnp.where(valid, acc, NEG)

    vmem = pl.BlockSpec(memory_space=pltpu.VMEM)
    return pl.pallas_call(
        body, name="bias_table", out_shape=jax.ShapeDtypeStruct((NH, BLK, 2 * BLK), F32),
        in_specs=[pl.BlockSpec(memory_space=pltpu.SMEM), vmem, vmem], out_specs=vmem,
    )(rel_bias, bucket, mask)


def _pack_small(dsh1, p1, dsh2, p2, dg1a, dg1b, dg2, norm1, norm2, scale1, scale2, dcw, dcb, dfn,
                dnw_attn, dnw_ssm, dhd, av, dsink, drel):
    def body(dsh1_ref, p1_ref, dsh2_ref, p2_ref, dg1a_ref, dg1b_ref, dg2_ref, n1_ref, n2_ref, s1_ref, s2_ref,
             dcw_ref, dcb_ref, dfn_ref, da_ref, ds_ref, dhd_ref, av_ref, dsink_ref, drel_ref, o_ref):
        o_ref[...] = jnp.zeros_like(o_ref)
        p1v, p2v = p1_ref[0:1, :], p2_ref[0:1, :]
        o_ref[0:1, :] = dsh1_ref[0:1, :]
        o_ref[1:2, :] = p1v * n1_ref[...]
        o_ref[2:3, :] = dg1a_ref[0:1, :] + dg1b_ref[0:1, :]
        o_ref[3:4, :] = dsh2_ref[0:1, :]
        o_ref[4:5, :] = p2v * n2_ref[...]
        o_ref[5:6, :] = dg2_ref[0:1, :]
        o_ref[6:7, :] = p1v * (1.0 + s1_ref[...])
        o_ref[7:11, :] = dcw_ref[0:4, :]
        o_ref[11:12, :] = dcb_ref[0:1, :]
        o_ref[12:13, :] = p2v * (1.0 + s2_ref[...])
        o_ref[13:14, :] = dfn_ref[0:1, :]
        o_ref[14:15, 0:QW] = da_ref[0:1, :]
        o_ref[15:16, 0:SW] = ds_ref[0:1, :]
        o_ref[16:17, 0:128] = dhd_ref[0:1, :]
        o_ref[17:18, 0:128] = dhd_ref[1:2, :] * av_ref[...]
        o_ref[18:19, 0:128] = dhd_ref[2:3, :]
        o_ref[19:20, 0:128] = dsink_ref[0:1, :]
        o_ref[24:56, 0:128] = drel_ref[...]

    return pl.pallas_call(body, name="pack_small", out_shape=jax.ShapeDtypeStruct((56, D), F32))(
        dsh1, p1, dsh2, p2, dg1a, dg1b, dg2, norm1, norm2, scale1, scale2, dcw, dcb, dfn,
        dnw_attn, dnw_ssm, dhd, av, dsink, drel)


def _pad_row(a, rows=1):
    return jnp.pad(a.reshape(rows, -1), ((0, 0), (0, D - a.size // rows)))


def kernel(x, c, ada_w, ada_b, norm1, w_in, conv_w, conv_b, dt_bias, A_log, D_skip, sinks, attn_out_norm, ssm_out_norm, w_o, norm2, w_gate_up, w_down, rel_bias, final_norm, loss_target, m_ada_w, m_ada_b, m_norm1, m_w_in, m_conv_w, m_conv_b, m_dt_bias, m_A_log, m_D_skip, m_sinks, m_attn_out_norm, m_ssm_out_norm, m_w_o, m_norm2, m_w_gate_up, m_w_down, m_rel_bias, m_final_norm, v_ada_w, v_ada_b, v_norm1, v_w_in, v_conv_w, v_conv_b, v_dt_bias, v_A_log, v_D_skip, v_sinks, v_attn_out_norm, v_ssm_out_norm, v_w_o, v_norm2, v_w_gate_up, v_w_down, v_rel_bias, v_final_norm):
    xi, yi, ci = lax.axis_index("x"), lax.axis_index("y"), lax.axis_index("c")
    chip = 2 * xi + yi
    me = 4 * xi + 2 * yi + ci
    chip_arr = jnp.reshape(chip, (1,)).astype(jnp.int32)
    core_arr = jnp.reshape(ci, (1,)).astype(jnp.int32)
    xs2, tgt = x[0], loss_target[0]

    first = jnp.concatenate([c, _pad_row(conv_w[0], CK), jnp.zeros((3, D), F32)], axis=0)
    w_in_b, w_o_b, w_dn_b = w_in[0].astype(BF16), w_o[0].astype(BF16), w_down[0].astype(BF16)
    w_gu_b = w_gate_up[0].astype(BF16)
    first_all, w_in_g = _exchange("gather_first", _merge(_gather8_carry(first), _gather_chips_carry([w_in_b])))
    c_all = first_all[:, 0, :]
    cw_full = jnp.concatenate([first_all[2 * j, 1:1 + CK, 0:256] for j in range(4)], axis=1)
    w_in_f = jnp.pad(jnp.transpose(w_in_g, (1, 0, 2)).reshape(D, IN_W), ((0, 0), (0, PROJ_W - IN_W)))

    ncol = ada_w.shape[2]
    mod_cols = _ada_fwd(c_all, ada_w[0], lax.dynamic_slice(ada_b, (0, chip * ncol), (1, ncol)))
    mod_all = _exchange("gather_mod", _gather_chips_carry([mod_cols]))[0]
    mod = lax.dynamic_slice(jnp.transpose(mod_all, (1, 0, 2)).reshape(8, 4 * ncol), (me, 0), (1, 4 * ncol))
    shift1, scale1, gate1, shift2, scale2, gate2 = [mod[:, j * D:(j + 1) * D] for j in range(6)]
    a1 = norm1 * (1.0 + scale1)
    a2 = norm2 * (1.0 + scale2)

    q, kv, z, xbc, dtr, w_o_g, w_dn_g = _in_proj_fwd(xs2, a1, shift1, w_in_f,
                                                     carry=_gather_chips_carry([w_o_b, w_dn_b]))
    w_o_f = w_o_g.reshape(D, D)
    w_dn_f = w_dn_g.reshape(DFF, D)
    bucket, mask = _attn_geometry()
    bucket = jnp.asarray(bucket)
    bias = _bias_table(rel_bias, bucket, jnp.asarray(mask.astype(np.int32)))
    sinks1 = sinks[0]
    ya, w_ga_g = _attn_fwd(q, kv, bias, sinks1, attn_out_norm, carry=_gather_chips_carry([w_gu_b[0:D // 2]]))
    cw8 = jnp.concatenate([cw_full, jnp.zeros((4, XBCW), F32)], axis=0)
    dtb = _pad_row(dt_bias)[:, 0:128]
    av = _pad_row(-jnp.exp(A_log))[:, 0:128]
    dk = jnp.repeat(D_skip, HD, axis=1)
    ys, hs, w_gb_g = _ssd_fwd(z, xbc, dtr, cw8, conv_b, dtb, av, dk, ssm_out_norm,
                              carry=_gather_chips_carry([w_gu_b[D // 2:D]]))
    fn = final_norm[None, :]
    x1, gu, dx2, loss_acc, dfn = _mlp_fwd(xs2, ya, ys, tgt, w_o_f, w_ga_g, w_gb_g, w_dn_f, gate1, a2, shift2, gate2, fn)
    loss = lax.psum(loss_acc[0, 0], ("x", "y", "c"))

    def to_sibling(p):
        return _Carry([p], [jax.ShapeDtypeStruct((4,) + p.shape[2:], F32)],
                      lambda x_, y_, c_: [(_SIBLING, 0, (j, 1 - c_), 0, j) for j in range(4)])

    def to_chips(s4):
        return _Carry([s4], [jax.ShapeDtypeStruct((3,) + s4.shape[1:], F32)],
                      lambda x_, y_, c_: [(f, 0, jnp.bitwise_xor(2 * x_ + y_, k + 1), 0, k) for k, f in enumerate(_CHIPS3)])

    def back(t):
        return _Carry([t[None]], [jax.ShapeDtypeStruct((1,) + t.shape, F32)], lambda x_, y_, c_: [(_SIBLING, 0, 0, 0, 0)])

    dx1, dya, dys, act, dgu, h2, dsh2, p2 = _mlp_bwd(x1, gu, dx2, w_o_f, w_ga_g, w_gb_g, w_dn_f, gate1, a2, shift2, gate2)
    p_gu = _wgrad("wgrad_gate_up", h2, dgu, 2 * DFF // 4, stacked=True)[0].reshape(4, 2, D // 2, 2 * DFF // 4)
    g_dn, dg2, got1_gu = _wgrad("wgrad_down", act, dx2, D // 2, gate2, w_dn_f, carry=to_sibling(p_gu))
    p_dn = g_dn.reshape(4, 2, DFF // 8, D)
    s4_gu = _add_half("rs_add_half_gu", p_gu, got1_gu, core_arr)
    g_oa, dg1a, got1_dn = _wgrad("wgrad_o_attn", ya, dx1, D, gate1, w_o_f[0:QW], carry=to_sibling(p_dn))
    g_os, dg1b = _wgrad("wgrad_o_ssm", ys, dx1, D, gate1, w_o_f[QW:D])
    s4_dn = _add_half("rs_add_half_dn", p_dn, got1_dn, core_arr)
    p_o = jnp.concatenate([g_oa, g_os], axis=0).reshape(4, 2, D // 8, D)
    dq, dkv, dbias, dsink, dnw_attn, got2_gu, got2_dn, got1_o = _attn_bwd(
        q, kv, dya, bias, sinks1, attn_out_norm, carry=_merge(to_chips(s4_gu), to_chips(s4_dn), to_sibling(p_o)))
    drel = _rel_bias_grad(dbias, bucket)
    mine_gu = _add_chips("rs_add_chips_gu", s4_gu, got2_gu, chip_arr)
    mine_dn = _add_chips("rs_add_chips_dn", s4_dn, got2_dn, chip_arr)
    s4_o = _add_half("rs_add_half_o", p_o, got1_o, core_arr)
    dz, dxbc, ddt, dcw, dcb, dnw_ssm, dhd, got2_o, got3_gu, got3_dn = _ssd_bwd(
        z, xbc, dtr, dys, hs, cw8, conv_b, dtb, av, dk, ssm_out_norm,
        carry=_merge(to_chips(s4_o), back(mine_gu), back(mine_dn)))
    mine_o = _add_chips("rs_add_chips_o", s4_o, got2_o, chip_arr)
    grad_x, dproj, h1, dsh1, p1, got3_o = _in_proj_bwd(xs2, dx1, a1, shift1, w_in_f, dq, dkv, dz, dxbc, ddt,
                                                       carry=back(mine_o))
    g_in = _wgrad("wgrad_in", h1, dproj, PROJ_W)[0]
    p_in = jnp.transpose(g_in[:, 0:IN_W].reshape(D, 4, IN_W // 4), (1, 0, 2)).reshape(4, 2, D // 2, IN_W // 4)

    small = _pack_small(dsh1, p1, dsh2, p2, dg1a, dg1b, dg2, norm1, norm2, scale1, scale2, dcw, dcb, dfn,
                        dnw_attn, dnw_ssm, dhd, av, dsink, drel)
    small_all, got1_in = _exchange("gather_small", _merge(_gather8_carry(small), to_sibling(p_in)))
    gsum = _sum8(small_all)
    s4_in = _add_half("rs_add_half_in", p_in, got1_in, core_arr)

    def conv_rows(a):
        return lax.dynamic_update_slice(jnp.zeros((CK, D), F32), a[0], (0, chip * 256))

    def pack(p):
        rows = [jnp.zeros((6, D), F32), p["norm1"], conv_rows(p["conv_w"]), p["conv_b"], p["norm2"],
                p["final_norm"][None, :], _pad_row(p["attn_out_norm"]), _pad_row(p["ssm_out_norm"]),
                _pad_row(p["dt_bias"]), _pad_row(p["A_log"]), _pad_row(p["D_skip"]), _pad_row(p["sinks"]),
                jnp.zeros((4, D), F32), _pad_row(p["rel_bias"], NBUCKET)]
        return jnp.concatenate(rows, axis=0)

    names = ["norm1", "conv_w", "conv_b", "norm2", "final_norm", "attn_out_norm", "ssm_out_norm", "dt_bias",
             "A_log", "D_skip", "sinks", "rel_bias"]
    ws = dict(zip(names, [norm1, conv_w, conv_b, norm2, final_norm, attn_out_norm, ssm_out_norm, dt_bias, A_log,
                          D_skip, sinks, rel_bias]))
    ms = dict(zip(names, [m_norm1, m_conv_w, m_conv_b, m_norm2, m_final_norm, m_attn_out_norm, m_ssm_out_norm,
                          m_dt_bias, m_A_log, m_D_skip, m_sinks, m_rel_bias]))
    vs = dict(zip(names, [v_norm1, v_conv_w, v_conv_b, v_norm2, v_final_norm, v_attn_out_norm, v_ssm_out_norm,
                          v_dt_bias, v_A_log, v_D_skip, v_sinks, v_rel_bias]))
    wp = pack(ws).at[0:6].set(ada_b.reshape(6, D))
    mp = pack(ms).at[0:6].set(m_ada_b.reshape(6, D))
    vp = pack(vs).at[0:6].set(v_ada_b.reshape(6, D))
    sd, sm, sv = _adamw_rows("adamw_small", wp, gsum, mp, vp, 56)

    def unpack(p):
        conv = lax.dynamic_slice(p[7:11], (0, chip * 256), (CK, 256))[None]
        return dict(ada_b=p[0:6].reshape(1, 6 * D), norm1=p[6:7], conv_w=conv, conv_b=p[11:12], norm2=p[12:13],
                    final_norm=p[13], attn_out_norm=p[14:15, 0:QW], ssm_out_norm=p[15:16, 0:SW],
                    dt_bias=p[16:17, 0:NH], A_log=p[17:18, 0:NH], D_skip=p[18:19, 0:NH], sinks=p[19:20, 0:NH],
                    rel_bias=p[24:56, 0:NH])

    small_out = [unpack(p) for p in (gsum, sd, sm, sv)]

    dmod_all = small_all[:, 0:6, :].reshape(8, 6 * D)
    dmod_loc = lax.dynamic_slice(dmod_all, (0, chip * ncol), (8, ncol))
    *ada_out, got2_in = _ada_bwd_adamw(c_all.T, dmod_loc, ada_w[0], m_ada_w[0], v_ada_w[0], carry=to_chips(s4_in))
    mine_in = _add_chips("rs_add_chips_in", s4_in, got2_in, chip_arr)

    def big_adamw(name, mine, got, w_, m_, v_, carry=None):
        shp = (2,) + mine.shape
        outs = _adamw_halves(name, mine, got[0], w_[0].reshape(shp), m_[0].reshape(shp), v_[0].reshape(shp), core_arr,
                             carry=carry)
        return [o.reshape(w_.shape) for o in outs[:4]], outs[4:]

    big_gu, (got3_in,) = big_adamw("adamw_gate_up", mine_gu, got3_gu, w_gate_up, m_w_gate_up, v_w_gate_up,
                                   carry=back(mine_in))
    big_dn, _ = big_adamw("adamw_down", mine_dn, got3_dn, w_down, m_w_down, v_w_down)
    big_o, _ = big_adamw("adamw_o", mine_o, got3_o, w_o, m_w_o, v_w_o)
    big_in, _ = big_adamw("adamw_in", mine_in, got3_in, w_in, m_w_in, v_w_in)
    big = [big_in, big_o, big_gu, big_dn]

    order = ["ada_w", "ada_b", "norm1", "w_in", "conv_w", "conv_b", "dt_bias", "A_log", "D_skip", "sinks",
             "attn_out_norm", "ssm_out_norm", "w_o", "norm2", "w_gate_up", "w_down", "rel_bias", "final_norm"]
    bigname = {"w_in": 0, "w_o": 1, "w_gate_up": 2, "w_down": 3}
    res = [loss, grad_x[None]]
    for kind in range(4):
        for nm in order:
            if nm == "ada_w":
                res.append(ada_out[kind][None])
            elif nm in bigname:
                res.append(big[bigname[nm]][kind])
            else:
                res.append(small_out[kind][nm])
    return tuple(res)
```

```python
import numpy as np
import jax
import jax.numpy as jnp
from jax import lax
from jax.experimental import pallas as pl
from jax.experimental.pallas import tpu as pltpu

F32, BF16 = jnp.float32, jnp.bfloat16
HI = lax.Precision.HIGHEST

D = 1024
QW, KVW = 512, 128
NH, HD, NKV = 8, 64, 2
SW = 512
NST = 128
XBCW = 1024
CK = 4
BLK = 128
DFF = 2816
IN_W = 2312
PROJ_W = 2432
EPS = 1e-6
NEG = -1e30
NBUCKET = 32

B1, B2, LR, AEPS, WD, STEP = 0.9, 0.999, 0.001, 1e-08, 0.01, 10

VMEM_LIMIT = 56 * 1024 * 1024

_NT = (((1,), (1,)), ((), ()))
_TN = (((0,), (0,)), ((), ()))


def _mm(a, b):
    return jnp.dot(a, b, preferred_element_type=F32)


def _mm_nt(a, b):
    return lax.dot_general(a, b, _NT, preferred_element_type=F32)


def _mm_tn(a, b):
    return lax.dot_general(a, b, _TN, preferred_element_type=F32)


def _mm_hi(a, b):
    return jnp.dot(a, b, preferred_element_type=F32, precision=HI)


def _sig(x):
    return 1.0 / (1.0 + jnp.exp(-x))


def _cp(sem):
    return pltpu.CompilerParams(dimension_semantics=sem, vmem_limit_bytes=VMEM_LIMIT)


def _row(shape):
    nd = len(shape)
    return pl.BlockSpec(shape, lambda *_: (0,) * nd)


def _adamw(w, g, m, v):
    m = B1 * m + (1.0 - B1) * g
    v = B2 * v + (1.0 - B2) * (g * g)
    m_hat = m / (1.0 - B1 ** STEP)
    v_hat = v / (1.0 - B2 ** STEP)
    delta = -LR * (m_hat / (jnp.sqrt(v_hat) + AEPS) + WD * w)
    return delta, m, v


class _Carry:
    def __init__(self, inps, outs, copies):
        self.inps, self.outs, self.copies = list(inps), list(outs), copies
        self.n = len(copies(0, 0, 0))

    def descriptors(self, in_refs, out_refs, send_sems, recv_sems):
        x, y, c = lax.axis_index("x"), lax.axis_index("y"), lax.axis_index("c")
        out = []
        for j, (flip, a, si, o, di) in enumerate(self.copies(x, y, c)):
            if flip is None:
                out.append(pltpu.make_async_copy(in_refs[a].at[si], out_refs[o].at[di], send_sems.at[j]))
            else:
                fx, fy, fc = flip
                peer = (1 - x if fx else x, 1 - y if fy else y, 1 - c if fc else c)
                out.append(pltpu.make_async_remote_copy(
                    src_ref=in_refs[a].at[si], dst_ref=out_refs[o].at[di],
                    send_sem=send_sems.at[j], recv_sem=recv_sems.at[j],
                    device_id=peer, device_id_type=pl.DeviceIdType.MESH))
        return out


def _pcall(body, args, *, name, grid, in_specs, out_specs, out_shape, scratch_shapes=(), sem=None, nprefetch=0,
           carry=None):
    out_shape, out_specs = list(out_shape), list(out_specs)
    in_specs, scratch_shapes = list(in_specs), list(scratch_shapes)
    nin, nout, nscr = len(in_specs), len(out_shape), len(scratch_shapes)
    run = body
    if carry is not None:
        ncin, ncout = len(carry.inps), len(carry.outs)
        hbm = pl.BlockSpec(memory_space=pl.ANY)

        def run(*refs):
            pre, r = refs[:nprefetch], refs[nprefetch:]
            ins, cins = r[:nin], r[nin:nin + ncin]
            r = r[nin + ncin:]
            outs, couts = r[:nout], r[nout:nout + ncout]
            r = r[nout + ncout:]
            scr, (send_sems, recv_sems) = r[:nscr], r[nscr:]
            first = pl.program_id(0) == 0
            last = pl.program_id(0) == grid[0] - 1
            for ax in range(1, len(grid)):
                first = jnp.logical_and(first, pl.program_id(ax) == 0)
                last = jnp.logical_and(last, pl.program_id(ax) == grid[ax] - 1)

            @pl.when(first)
            def _():
                for d in carry.descriptors(cins, couts, send_sems, recv_sems):
                    d.start()

            body(*pre, *ins, *outs, *scr)

            @pl.when(last)
            def _():
                for d in carry.descriptors(cins, couts, send_sems, recv_sems):
                    d.wait()

        in_specs = in_specs + [hbm] * ncin
        out_specs = out_specs + [hbm] * ncout
        out_shape = out_shape + carry.outs
        scratch_shapes = scratch_shapes + [pltpu.SemaphoreType.DMA((carry.n,)), pltpu.SemaphoreType.DMA((carry.n,))]
        args = list(args) + carry.inps
    if sem is None:
        sem = ("arbitrary",) * len(grid)
    if nprefetch:
        kw = dict(grid_spec=pltpu.PrefetchScalarGridSpec(num_scalar_prefetch=nprefetch, grid=grid, in_specs=in_specs,
                                                         out_specs=out_specs, scratch_shapes=scratch_shapes))
    else:
        kw = dict(grid=grid, in_specs=in_specs, out_specs=out_specs, scratch_shapes=scratch_shapes)
    res = pl.pallas_call(run, name=name, out_shape=out_shape, compiler_params=_cp(sem), **kw)(*args)
    return list(res)


def _merge(*carries):
    inps, outs, offs = [], [], []
    for cr in carries:
        offs.append((len(inps), len(outs)))
        inps += cr.inps
        outs += cr.outs

    def copies(x, y, c):
        return [(f, a + io, si, o + oo, di) for cr, (io, oo) in zip(carries, offs) for f, a, si, o, di in cr.copies(x, y, c)]

    return _Carry(inps, outs, copies)


def _exchange(name, carry):
    return _pcall(lambda: None, [], name=name, grid=(1,), in_specs=[], out_specs=[], out_shape=[], carry=carry)


_ALL7 = [(f >> 2 & 1, f >> 1 & 1, f & 1) for f in range(1, 8)]
_CHIPS3 = [(0, 1, 0), (1, 0, 0), (1, 1, 0)]
_SIBLING = (0, 0, 1)


def _gather8_carry(blk):
    def copies(x, y, c):
        me = 4 * x + 2 * y + c
        return [(None, 0, 0, 0, me)] + [(f, 0, 0, 0, me) for f in _ALL7]

    return _Carry([blk[None]], [jax.ShapeDtypeStruct((8,) + blk.shape, blk.dtype)], copies)


def _gather_chips_carry(blks):
    def copies(x, y, c):
        chip = 2 * x + y
        return [(f, a, 0, a, chip) for a in range(len(blks)) for f in [None] + _CHIPS3]

    return _Carry([b[None] for b in blks], [jax.ShapeDtypeStruct((4,) + b.shape, b.dtype) for b in blks], copies)


def _ada_fwd(c_all, w_loc, b_loc):
    n = w_loc.shape[1]
    tn = 512

    def body(c_ref, w_ref, b_ref, o_ref):
        cv = c_ref[...]
        cond = cv * _sig(cv)
        o_ref[...] = _mm_hi(cond, w_ref[...]) + b_ref[...]

    return pl.pallas_call(
        body, name="ada_fwd", grid=(n // tn,),
        out_shape=jax.ShapeDtypeStruct((8, n), F32),
        in_specs=[_row((8, D)), pl.BlockSpec((D, tn), lambda j: (0, j)), pl.BlockSpec((1, tn), lambda j: (0, j))],
        out_specs=pl.BlockSpec((8, tn), lambda j: (0, j)),
        compiler_params=_cp(("parallel",)),
    )(c_all, w_loc, b_loc)


def _ada_bwd_adamw(c_all_t, dmod_loc, w, m, v, carry=None):
    n = w.shape[1]
    tn = 512

    def body(ct_ref, dm_ref, w_ref, m_ref, v_ref, g_ref, d_ref, mo_ref, vo_ref):
        ct = ct_ref[...]
        cond = ct * _sig(ct)
        dm = dm_ref[...]
        g = cond[:, 0:1] * dm[0:1, :]
        for b in range(1, 8):
            g = g + cond[:, b:b + 1] * dm[b:b + 1, :]
        g_ref[...] = g
        d_ref[...], mo_ref[...], vo_ref[...] = _adamw(w_ref[...], g, m_ref[...], v_ref[...])

    wspec = pl.BlockSpec((D, tn), lambda j: (0, j))
    return _pcall(
        body, [c_all_t, dmod_loc, w, m, v], name="ada_bwd_adamw", grid=(n // tn,),
        out_shape=[jax.ShapeDtypeStruct((D, n), F32)] * 4,
        in_specs=[_row((D, 8)), pl.BlockSpec((8, tn), lambda j: (0, j)), wspec, wspec, wspec],
        out_specs=[wspec] * 4, carry=carry)


def _in_proj_fwd(x, a1, sh1, w_in, carry=None):
    s = x.shape[0]
    tm = 512

    def body(x_ref, a_ref, s_ref, w_ref, q_ref, kv_ref, z_ref, xbc_ref, dt_ref):
        xv = x_ref[...]
        r = lax.rsqrt(jnp.mean(xv * xv, axis=-1, keepdims=True) + EPS)
        h = (xv * r * a_ref[...] + s_ref[...]).astype(BF16)
        p = _mm(h, w_ref[...])
        q_ref[...] = p[:, 0:512].astype(BF16)
        kv_ref[...] = p[:, 512:768].astype(BF16)
        z_ref[...] = p[:, 768:1280]
        xbc_ref[...] = p[:, 1280:2304]
        dt_ref[...] = p[:, 2304:2432]

    def tok(w):
        return pl.BlockSpec((tm, w), lambda i: (i, 0))

    return _pcall(
        body, [x, a1, sh1, w_in], name="in_proj_fwd", grid=(s // tm,),
        out_shape=[jax.ShapeDtypeStruct((s, QW), BF16), jax.ShapeDtypeStruct((s, 2 * KVW), BF16),
                   jax.ShapeDtypeStruct((s, SW), F32), jax.ShapeDtypeStruct((s, XBCW), F32),
                   jax.ShapeDtypeStruct((s, 128), F32)],
        in_specs=[tok(D), _row((1, D)), _row((1, D)), _row((D, PROJ_W))],
        out_specs=[tok(QW), tok(2 * KVW), tok(SW), tok(XBCW), tok(128)], carry=carry)


def _in_proj_bwd(x, dx1, a1, sh1, w_in, dq, dkv, dz, dxbc, ddt, carry=None):
    s = x.shape[0]
    tm = 512

    def body(x_ref, dx1_ref, a_ref, s_ref, w_ref, dq_ref, dkv_ref, dz_ref, dxbc_ref, ddt_ref,
             gx_ref, dproj_ref, h_ref, dsh_ref, p_ref):
        i = pl.program_id(0)

        @pl.when(i == 0)
        def _():
            dsh_ref[...] = jnp.zeros_like(dsh_ref)
            p_ref[...] = jnp.zeros_like(p_ref)

        dproj = jnp.concatenate([dq_ref[...], dkv_ref[...], dz_ref[...], dxbc_ref[...], ddt_ref[...]], axis=1)
        dproj_ref[...] = dproj
        dh = _mm_nt(dproj, w_ref[...])
        xv = x_ref[...]
        r = lax.rsqrt(jnp.mean(xv * xv, axis=-1, keepdims=True) + EPS)
        xn = xv * r
        a = a_ref[...]
        h_ref[...] = (xn * a + s_ref[...]).astype(BF16)
        dsh_ref[0:1, :] += jnp.sum(dh, axis=0, keepdims=True)
        p_ref[0:1, :] += jnp.sum(dh * xn, axis=0, keepdims=True)
        u = dh * a
        gx_ref[...] = dx1_ref[...] + r * u - xn * (r * jnp.mean(u * xn, axis=-1, keepdims=True))

    def tok(w):
        return pl.BlockSpec((tm, w), lambda i: (i, 0))

    return _pcall(
        body, [x, dx1, a1, sh1, w_in, dq, dkv, dz, dxbc, ddt], name="in_proj_bwd", grid=(s // tm,),
        out_shape=[jax.ShapeDtypeStruct((s, D), F32), jax.ShapeDtypeStruct((s, PROJ_W), BF16),
                   jax.ShapeDtypeStruct((s, D), BF16), jax.ShapeDtypeStruct((8, D), F32),
                   jax.ShapeDtypeStruct((8, D), F32)],
        in_specs=[tok(D), tok(D), _row((1, D)), _row((1, D)), _row((D, PROJ_W)),
                  tok(QW), tok(2 * KVW), tok(SW), tok(XBCW), tok(128)],
        out_specs=[tok(D), tok(PROJ_W), tok(D), _row((8, D)), _row((8, D))], carry=carry)


def _attn_geometry():
    dist = np.arange(BLK)[:, None] + BLK - np.arange(2 * BLK)[None, :]
    n = np.maximum(dist, 0)
    max_exact = NBUCKET // 2
    large = max_exact + (np.log(np.maximum(n, 1) / max_exact) / np.log(128 / max_exact)
                         * (NBUCKET - max_exact)).astype(np.int32)
    large = np.minimum(large, NBUCKET - 1)
    bucket = np.where(n < max_exact, n, large).astype(np.int32)
    mask = (dist >= 0) & (dist < 128)
    return bucket, mask


def _attn_heads(n, q_ref, kvp_ref, kvc_ref, bias_ref, sinks_ref):
    qv = q_ref[...] * 0.125
    kvw = jnp.concatenate([kvp_ref[...], kvc_ref[...]], axis=0)
    col = lax.broadcasted_iota(jnp.int32, (BLK, 2 * BLK), 1)
    first = jnp.where(jnp.logical_and(n == 0, col < BLK), NEG, 0.0)
    groups = []
    for g in range(NKV):
        qs = jnp.concatenate([qv[:, (4 * g + r) * HD:(4 * g + r + 1) * HD] for r in range(4)], axis=0)
        kw = kvw[:, g * HD:(g + 1) * HD]
        vw = kvw[:, KVW + g * HD:KVW + (g + 1) * HD]
        sc = _mm_nt(qs, kw)
        pn, ps = [], []
        for r in range(4):
            h = 4 * g + r
            sr = sc[r * BLK:(r + 1) * BLK] + bias_ref[h] + first
            sink = sinks_ref[h]
            m = jnp.maximum(jnp.max(sr, axis=-1, keepdims=True), sink)
            p = jnp.exp(sr - m)
            es = jnp.exp(sink - m)
            inv = 1.0 / (jnp.sum(p, axis=-1, keepdims=True) + es)
            pn.append(p * inv)
            ps.append(es * inv)
        pn = jnp.concatenate(pn, axis=0)
        ps = jnp.concatenate(ps, axis=0)
        o = _mm(pn.astype(BF16), vw)
        groups.append((qs, kw, vw, pn, ps, o))
    return groups


def _unstack_heads(parts):
    return jnp.concatenate([p[r * BLK:(r + 1) * BLK] for p in parts for r in range(4)], axis=1)


def _attn_fwd(q, kv, bias, sinks, nw, carry=None):
    s = q.shape[0]

    def body(q_ref, kvp_ref, kvc_ref, bias_ref, sinks_ref, nw_ref, y_ref):
        n = pl.program_id(0)
        groups = _attn_heads(n, q_ref, kvp_ref, kvc_ref, bias_ref, sinks_ref)
        o = _unstack_heads([g[5] for g in groups])
        r = lax.rsqrt(jnp.mean(o * o, axis=-1, keepdims=True) + EPS)
        y_ref[...] = (o * r * nw_ref[...]).astype(BF16)

    return _pcall(
        body, [q, kv, kv, bias, sinks, nw], name="attn_fwd", grid=(s // BLK,),
        out_shape=[jax.ShapeDtypeStruct((s, QW), BF16)],
        in_specs=[pl.BlockSpec((BLK, QW), lambda n: (n, 0)),
                  pl.BlockSpec((BLK, 2 * KVW), lambda n: (jnp.maximum(n - 1, 0), 0)),
                  pl.BlockSpec((BLK, 2 * KVW), lambda n: (n, 0)),
                  _row((NH, BLK, 2 * BLK)),
                  pl.BlockSpec(memory_space=pltpu.SMEM),
                  _row((1, QW))],
        out_specs=[pl.BlockSpec((BLK, QW), lambda n: (n, 0))], carry=carry)


def _attn_bwd(q, kv, dya, bias, sinks, nw, carry=None):
    s = q.shape[0]
    nb = s // BLK

    def body(q_ref, kvp_ref, kvc_ref, dy_ref, bias_ref, sinks_ref, nw_ref,
             dq_ref, dkv_ref, dbias_ref, dsink_ref, dnw_ref, carry_ref):
        n = pl.program_id(0)

        @pl.when(n == 0)
        def _():
            carry_ref[...] = jnp.zeros_like(carry_ref)
            dbias_ref[...] = jnp.zeros_like(dbias_ref)
            dsink_ref[...] = jnp.zeros_like(dsink_ref)
            dnw_ref[...] = jnp.zeros_like(dnw_ref)

        @pl.when(n < nb)
        def _():
            groups = _attn_heads(n, q_ref, kvp_ref, kvc_ref, bias_ref, sinks_ref)
            o = _unstack_heads([g[5] for g in groups])
            r = lax.rsqrt(jnp.mean(o * o, axis=-1, keepdims=True) + EPS)
            dy = dy_ref[...]
            on = o * r
            dnw_ref[0:1, :] += jnp.sum(dy * on, axis=0, keepdims=True)
            u = dy * nw_ref[...]
            do = r * u - on * (r * jnp.mean(u * on, axis=-1, keepdims=True))
            dq_parts, dk_parts, dv_parts = [], [], []
            for g, (qs, kw, vw, pn, ps, og) in enumerate(groups):
                dos = jnp.concatenate([do[:, (4 * g + r_) * HD:(4 * g + r_ + 1) * HD] for r_ in range(4)], axis=0)
                delta = jnp.sum(dos * og, axis=-1, keepdims=True)
                dp = _mm_nt(dos.astype(BF16), vw)
                ds = pn * (dp - delta)
                dsk = ps * delta
                lane = lax.broadcasted_iota(jnp.int32, (1, 128), 1)
                for r_ in range(4):
                    h = 4 * g + r_
                    dbias_ref[h] += ds[r_ * BLK:(r_ + 1) * BLK]
                    dsink_ref[0:1, :] -= jnp.where(lane == h, jnp.sum(dsk[r_ * BLK:(r_ + 1) * BLK]), 0.0)
                dsb = ds.astype(BF16)
                dq_parts.append(_mm(dsb, kw) * 0.125)
                dk_parts.append(_mm_tn(dsb, qs))
                dv_parts.append(_mm_tn(pn.astype(BF16), dos.astype(BF16)))
            dq_ref[...] = _unstack_heads(dq_parts).astype(BF16)
            dkvw = jnp.concatenate(dk_parts + dv_parts, axis=1)
            dkv_ref[...] = (carry_ref[...] + dkvw[0:BLK]).astype(BF16)
            carry_ref[...] = dkvw[BLK:2 * BLK]

        @pl.when(n == nb)
        def _():
            dkv_ref[...] = carry_ref[...].astype(BF16)

    last = nb - 1
    return _pcall(
        body, [q, kv, kv, dya, bias, sinks, nw], name="attn_bwd", grid=(nb + 1,),
        out_shape=[jax.ShapeDtypeStruct((s, QW), BF16), jax.ShapeDtypeStruct((s, 2 * KVW), BF16),
                   jax.ShapeDtypeStruct((NH, BLK, 2 * BLK), F32), jax.ShapeDtypeStruct((NH, 128), F32),
                   jax.ShapeDtypeStruct((8, QW), F32)],
        in_specs=[pl.BlockSpec((BLK, QW), lambda n: (jnp.minimum(n, last), 0)),
                  pl.BlockSpec((BLK, 2 * KVW), lambda n: (jnp.clip(n - 1, 0, last), 0)),
                  pl.BlockSpec((BLK, 2 * KVW), lambda n: (jnp.minimum(n, last), 0)),
                  pl.BlockSpec((BLK, QW), lambda n: (jnp.minimum(n, last), 0)),
                  _row((NH, BLK, 2 * BLK)),
                  pl.BlockSpec(memory_space=pltpu.SMEM),
                  _row((1, QW))],
        out_specs=[pl.BlockSpec((BLK, QW), lambda n: (jnp.minimum(n, last), 0)),
                   pl.BlockSpec((BLK, 2 * KVW), lambda n: (jnp.maximum(n - 1, 0), 0)),
                   _row((NH, BLK, 2 * BLK)), _row((NH, 128)), _row((8, QW))],
        scratch_shapes=[pltpu.VMEM((BLK, 2 * KVW), F32)], carry=carry)


def _rel_bias_grad(dbias, bucket):
    def body(db_ref, bk_ref, o_ref):
        bk = bk_ref[...]
        lane = lax.broadcasted_iota(jnp.int32, (1, 128), 1)
        for b in range(NBUCKET):
            sel = bk == b
            row = jnp.zeros((1, 128), F32)
            for h in range(NH):
                row = row + jnp.where(lane == h, jnp.sum(jnp.where(sel, db_ref[h], 0.0)), 0.0)
            o_ref[b:b + 1, :] = row

    return pl.pallas_call(
        body, name="rel_bias_grad",
        out_shape=jax.ShapeDtypeStruct((NBUCKET, 128), F32),
    )(dbias, bucket)


def _ssd_consts():
    head_of_lane = np.arange(SW) // HD
    expand = (np.arange(128)[:, None] == head_of_lane[None, :]).astype(np.float32)
    tril = np.tril(np.ones((BLK, BLK), np.float32))
    return jnp.asarray(expand), jnp.asarray(expand.T.copy()), jnp.asarray(tril), jnp.asarray(tril.T.copy())


def _ssd_chunk_fwd(c, z_ref, xc_ref, xp_ref, dtr_ref, cw_ref, cb_ref, dtb_ref, a_ref, dk_ref, ex_ref, tril_ref, h_in):
    halo = jnp.where(c == 0, 0.0, xp_ref[BLK - 8:BLK, :])
    ext = jnp.concatenate([halo, xc_ref[...]], axis=0)
    cw = cw_ref[...]
    taps = [ext[8:8 + BLK] if k == CK - 1 else pltpu.roll(ext, CK - 1 - k, 0)[8:8 + BLK] for k in range(CK)]
    pre = cb_ref[...] + sum(cw[k:k + 1, :] * taps[k] for k in range(CK))
    sp = _sig(pre)
    xbc = pre * sp
    xs, bm, cm = xbc[:, 0:SW], xbc[:, SW:SW + 2 * NST], xbc[:, SW + 2 * NST:]
    dtin = dtr_ref[...] + dtb_ref[...]
    dt = jnp.maximum(dtin, 0.0) + jnp.log1p(jnp.exp(-jnp.abs(dtin)))
    av = a_ref[...]
    cs = _mm_hi(tril_ref[...], dt * av)
    cst = cs.T
    ex = ex_ref[...]
    dtx = _mm_hi(dt, ex)
    csx = _mm_hi(cs, ex)
    xdt = xs * dtx
    csl = csx[BLK - 1:BLK, :]
    decx = jnp.exp(csl - csx)
    ecsx = jnp.exp(csx)
    ecl = jnp.exp(csl)
    causal = tril_ref[...] > 0.5
    ydiag, yoff, cbs, lms = [], [], [], []
    for g in range(2):
        bg = bm[:, g * NST:(g + 1) * NST].astype(BF16)
        cg = cm[:, g * NST:(g + 1) * NST].astype(BF16)
        cb = _mm_nt(cg, bg)
        cbs.append(cb)
        yoff.append(_mm(cg, h_in[:, g * 256:(g + 1) * 256].astype(BF16)))
        for r in range(4):
            h = 4 * g + r
            seg = cs[:, h:h + 1] - cst[h:h + 1, :]
            lm = jnp.where(causal, jnp.exp(jnp.minimum(seg, 0.0)), 0.0)
            lms.append(lm)
            ydiag.append(_mm((cb * lm).astype(BF16), xdt[:, h * HD:(h + 1) * HD].astype(BF16)))
    yoff = jnp.concatenate(yoff, axis=1) * ecsx
    y = jnp.concatenate(ydiag, axis=1) + yoff + dk_ref[...] * xs
    return dict(ext=ext, taps=taps, pre=pre, sp=sp, xs=xs, bm=bm, cm=cm, dtin=dtin, dt=dt, av=av, cs=cs, cst=cst,
                dtx=dtx, csx=csx, xdt=xdt, decx=decx, ecsx=ecsx, ecl=ecl, causal=causal, cbs=cbs, lms=lms,
                yoff=yoff, y=y)


def _group_mean(t):
    m0 = jnp.mean(t[:, 0:256], axis=-1, keepdims=True)
    m1 = jnp.mean(t[:, 256:512], axis=-1, keepdims=True)
    return jnp.concatenate([jnp.broadcast_to(m0, (t.shape[0], 256)), jnp.broadcast_to(m1, (t.shape[0], 256))], axis=1)


def _ssd_specs(nc, rev):
    def cur(w):
        return pl.BlockSpec((BLK, w), (lambda i: (nc - 1 - i, 0)) if rev else (lambda i: (i, 0)))
    prev = pl.BlockSpec((BLK, XBCW), (lambda i: (jnp.maximum(nc - 2 - i, 0), 0)) if rev
                        else (lambda i: (jnp.maximum(i - 1, 0), 0)))
    return cur, prev


def _ssd_fwd(z, xbc, dtr, cw, cb, dtb, av, dk, nw, carry=None):
    s = z.shape[0]
    nc = s // BLK
    ex, _, tril, _ = _ssd_consts()

    def body(z_ref, xc_ref, xp_ref, dtr_ref, cw_ref, cb_ref, dtb_ref, a_ref, dk_ref, nw_ref, ex_ref, tril_ref,
             y_ref, hs_ref, h_ref):
        c = pl.program_id(0)

        @pl.when(c == 0)
        def _():
            h_ref[...] = jnp.zeros_like(h_ref)

        h_in = h_ref[...]
        hs_ref[0] = h_in
        f = _ssd_chunk_fwd(c, z_ref, xc_ref, xp_ref, dtr_ref, cw_ref, cb_ref, dtb_ref, a_ref, dk_ref, ex_ref,
                           tril_ref, h_in)
        dx = (f["decx"] * f["xdt"]).astype(BF16)
        st = [_mm_tn(f["bm"][:, g * NST:(g + 1) * NST].astype(BF16), dx[:, g * 256:(g + 1) * 256]) for g in range(2)]
        h_ref[...] = h_in * f["ecl"] + jnp.concatenate(st, axis=1)
        zv = z_ref[...]
        t = f["y"] * (zv * _sig(zv))
        r = lax.rsqrt(_group_mean(t * t) + EPS)
        y_ref[...] = (t * r * nw_ref[...]).astype(BF16)

    cur, prev = _ssd_specs(nc, False)
    return _pcall(
        body, [z, xbc, xbc, dtr, cw, cb, dtb, av, dk, nw, ex, tril], name="ssd_fwd", grid=(nc,),
        out_shape=[jax.ShapeDtypeStruct((s, SW), BF16), jax.ShapeDtypeStruct((nc, NST, SW), F32)],
        in_specs=[cur(SW), cur(XBCW), prev, cur(128), _row((8, XBCW)), _row((1, XBCW)), _row((1, 128)),
                  _row((1, 128)), _row((1, SW)), _row((1, SW)), _row((128, SW)), _row((BLK, BLK))],
        out_specs=[cur(SW), pl.BlockSpec((1, NST, SW), lambda i: (i, 0, 0))],
        scratch_shapes=[pltpu.VMEM((NST, SW), F32)], carry=carry)


def _ssd_bwd(z, xbc, dtr, dys, hs, cw, cb, dtb, av, dk, nw, carry=None):
    s = z.shape[0]
    nc = s // BLK
    ex, ext_t, tril, triu = _ssd_consts()

    def body(z_ref, xc_ref, xp_ref, dtr_ref, dy_ref, hs_ref, cw_ref, cb_ref, dtb_ref, a_ref, dk_ref, nw_ref,
             ex_ref, ext_ref, tril_ref, triu_ref,
             dz_ref, dxbc_ref, ddt_ref, dcw_ref, dcb_ref, dnw_ref, dhd_ref, dh_ref, nxt_ref, dd_ref):
        i = pl.program_id(0)
        c = nc - 1 - i

        @pl.when(i == 0)
        def _():
            dh_ref[...] = jnp.zeros_like(dh_ref)
            nxt_ref[...] = jnp.zeros_like(nxt_ref)
            dd_ref[...] = jnp.zeros_like(dd_ref)
            dcw_ref[...] = jnp.zeros_like(dcw_ref)
            dcb_ref[...] = jnp.zeros_like(dcb_ref)
            dnw_ref[...] = jnp.zeros_like(dnw_ref)
            dhd_ref[...] = jnp.zeros_like(dhd_ref)

        h_in = hs_ref[0]
        f = _ssd_chunk_fwd(c, z_ref, xc_ref, xp_ref, dtr_ref, cw_ref, cb_ref, dtb_ref, a_ref, dk_ref, ex_ref,
                           tril_ref, h_in)
        xs, xdt, decx, ecsx, ecl, dtx = f["xs"], f["xdt"], f["decx"], f["ecsx"], f["ecl"], f["dtx"]
        cs, cst, causal = f["cs"], f["cst"], f["causal"]
        causal_t = triu_ref[...] > 0.5

        zv = z_ref[...]
        sz = _sig(zv)
        gz = zv * sz
        t = f["y"] * gz
        r = lax.rsqrt(_group_mean(t * t) + EPS)
        tn_ = t * r
        dyn = dy_ref[...]
        dnw_ref[0:1, :] += jnp.sum(dyn * tn_, axis=0, keepdims=True)
        u = dyn * nw_ref[...]
        dt_ = r * u - tn_ * (r * _group_mean(u * tn_))
        dy = dt_ * gz
        dz_ref[...] = (dt_ * f["y"] * (sz * (1.0 + zv * (1.0 - sz)))).astype(BF16)

        dd_ref[0:1, :] += jnp.sum(dy * xs, axis=0, keepdims=True)
        dxs = dk_ref[...] * dy

        gst = dh_ref[...]
        edy = ecsx * dy
        dxdt, dbs, dcs_, dcsx_parts, dh_new = [], [], [], [], []
        lane = lax.broadcasted_iota(jnp.int32, (1, 128), 1)
        dcs_intra = jnp.zeros((BLK, 128), F32)
        for g in range(2):
            sl = slice(g * 256, (g + 1) * 256)
            bgf, cgf = f["bm"][:, g * NST:(g + 1) * NST], f["cm"][:, g * NST:(g + 1) * NST]
            bg, cg = bgf.astype(BF16), cgf.astype(BF16)
            gg = gst[:, sl].astype(BF16)
            hg = h_in[:, sl].astype(BF16)
            edyg = edy[:, sl].astype(BF16)
            dc = _mm_nt(edyg, hg)
            dh_new.append(gst[:, sl] * ecl[:, sl] + _mm_tn(cg, edyg))
            bgm = _mm(bg, gg)
            dxdt_g = decx[:, sl] * bgm
            dxg = (decx[:, sl] * xdt[:, sl]).astype(BF16)
            db = _mm_nt(dxg, gg)
            qd = bgm * xdt[:, sl] * decx[:, sl]
            last = jnp.sum(qd, axis=0, keepdims=True) + ecl[:, sl] * jnp.sum(gst[:, sl] * h_in[:, sl], axis=0, keepdims=True)
            rowid = lax.broadcasted_iota(jnp.int32, (BLK, 256), 0)
            dcsx_parts.append(f["yoff"][:, sl] * dy[:, sl] - qd + jnp.where(rowid == BLK - 1, last, 0.0))
            cb_ = f["cbs"][g]
            cbt = _mm_nt(bg, cg)
            dcb_ = jnp.zeros((BLK, BLK), F32)
            dcbt = jnp.zeros((BLK, BLK), F32)
            dxd = []
            for r_ in range(4):
                h = 4 * g + r_
                hl = slice(h * HD, (h + 1) * HD)
                lm = f["lms"][h]
                segt = cst[h:h + 1, :] - cs[:, h:h + 1]
                lmt = jnp.where(causal_t, jnp.exp(jnp.minimum(segt, 0.0)), 0.0)
                dyh = dy[:, hl].astype(BF16)
                xdh = xdt[:, hl].astype(BF16)
                dw = _mm_nt(dyh, xdh)
                dwt = _mm_nt(xdh, dyh)
                wt = cbt * lmt
                dxd.append(_mm(wt.astype(BF16), dyh))
                dcb_ = dcb_ + dw * lm
                dcbt = dcbt + dwt * lmt
                col = jnp.sum(dw * (cb_ * lm), axis=-1, keepdims=True) - jnp.sum(dwt * wt, axis=-1, keepdims=True)
                dcs_intra = dcs_intra + jnp.where(lane == h, col, 0.0)
            dxdt.append(dxdt_g + jnp.concatenate(dxd, axis=1))
            dcs_.append(dc + _mm(dcb_.astype(BF16), bg))
            dbs.append(db + _mm(dcbt.astype(BF16), cg))
        dh_ref[...] = jnp.concatenate(dh_new, axis=1)
        dxdt = jnp.concatenate(dxdt, axis=1)
        dxs = dxs + dxdt * dtx
        ext_t_ = ext_ref[...]
        dcs = dcs_intra + _mm_hi(jnp.concatenate(dcsx_parts, axis=1), ext_t_)
        da = _mm_hi(triu_ref[...], dcs)
        ddt = da * f["av"] + _mm_hi(dxdt * xs, ext_t_)
        dhd_ref[1:2, :] += jnp.sum(da * f["dt"], axis=0, keepdims=True)
        ddtr = ddt * _sig(f["dtin"])
        dhd_ref[0:1, :] += jnp.sum(ddtr, axis=0, keepdims=True)
        ddt_ref[...] = ddtr.astype(BF16)

        sp, pre = f["sp"], f["pre"]
        dact = jnp.concatenate([dxs] + dbs + dcs_, axis=1)
        dpre = dact * (sp * (1.0 + pre * (1.0 - sp)))
        dcb_ref[0:1, :] += jnp.sum(dpre, axis=0, keepdims=True)
        for k in range(CK):
            dcw_ref[k:k + 1, :] += jnp.sum(dpre * f["taps"][k], axis=0, keepdims=True)
        ext2 = jnp.concatenate([dpre, nxt_ref[...]], axis=0)
        cw = cw_ref[...]
        dxr = cw[CK - 1:CK, :] * dpre
        for k in range(CK - 1):
            dxr = dxr + cw[k:k + 1, :] * pltpu.roll(ext2, BLK + 8 - (CK - 1 - k), 0)[0:BLK]
        dxbc_ref[...] = dxr.astype(BF16)
        nxt_ref[...] = dpre[0:8]

        @pl.when(i == nc - 1)
        def _():
            dhd_ref[2:3, :] = _mm_hi(dd_ref[...], ext_t_)[0:1, :]

    cur, prev = _ssd_specs(nc, True)
    return _pcall(
        body, [z, xbc, xbc, dtr, dys, hs, cw, cb, dtb, av, dk, nw, ex, ext_t, tril, triu], name="ssd_bwd", grid=(nc,),
        out_shape=[jax.ShapeDtypeStruct((s, SW), BF16), jax.ShapeDtypeStruct((s, XBCW), BF16),
                   jax.ShapeDtypeStruct((s, 128), BF16), jax.ShapeDtypeStruct((8, XBCW), F32),
                   jax.ShapeDtypeStruct((8, XBCW), F32), jax.ShapeDtypeStruct((8, SW), F32),
                   jax.ShapeDtypeStruct((8, 128), F32)],
        in_specs=[cur(SW), cur(XBCW), prev, cur(128), cur(SW),
                  pl.BlockSpec((1, NST, SW), lambda i: (nc - 1 - i, 0, 0)),
                  _row((8, XBCW)), _row((1, XBCW)), _row((1, 128)), _row((1, 128)), _row((1, SW)), _row((1, SW)),
                  _row((128, SW)), _row((SW, 128)), _row((BLK, BLK)), _row((BLK, BLK))],
        out_specs=[cur(SW), cur(XBCW), cur(128), _row((8, XBCW)), _row((8, XBCW)), _row((8, SW)), _row((8, 128))],
        scratch_shapes=[pltpu.VMEM((NST, SW), F32), pltpu.VMEM((8, XBCW), F32), pltpu.VMEM((8, SW), F32)], carry=carry)


def _load_once(i, pairs, sem):
    @pl.when(i == 0)
    def _():
        cps = [pltpu.make_async_copy(src, dst, sem.at[k]) for k, (src, dst) in enumerate(pairs)]
        for cp in cps:
            cp.start()
        for cp in cps:
            cp.wait()


def _mlp_fwd(x, ya, ys, tgt, w_o, w_ga, w_gb, w_dn, gate1, a2, sh2, gate2, fn):
    s = x.shape[0]
    tm = 256

    def body(x_ref, ya_ref, ys_ref, t_ref, wo_hbm, wga_hbm, wgb_hbm, wdn_hbm, g1_ref, a2_ref, s2_ref, g2_ref, fn_ref,
             x1_ref, gu_ref, dx2_ref, loss_ref, dfn_ref, wo, wga, wgb, wdn, sem):
        i = pl.program_id(0)
        _load_once(i, [(wo_hbm, wo), (wga_hbm, wga), (wgb_hbm, wgb), (wdn_hbm, wdn)], sem)

        @pl.when(i == 0)
        def _():
            loss_ref[...] = jnp.zeros_like(loss_ref)
            dfn_ref[...] = jnp.zeros_like(dfn_ref)

        mix = _mm(ya_ref[...], wo[0:QW, :]) + _mm(ys_ref[...], wo[QW:D, :])
        x1 = x_ref[...] + g1_ref[...] * mix
        x1_ref[...] = x1
        r2 = lax.rsqrt(jnp.mean(x1 * x1, axis=-1, keepdims=True) + EPS)
        h2 = (x1 * r2 * a2_ref[...] + s2_ref[...]).astype(BF16)
        ha, hb = h2[:, 0:D // 2], h2[:, D // 2:D]
        gub = jnp.concatenate([(_mm(ha, wga[j]) + _mm(hb, wgb[j])).astype(BF16) for j in range(4)], axis=1)
        gu_ref[...] = gub
        gv, uv = gub[:, 0:DFF].astype(F32), gub[:, DFF:].astype(F32)
        act = (gv * _sig(gv) * uv).astype(BF16)
        x2 = x1 + g2_ref[...] * _mm(act, wdn[...])
        r3 = lax.rsqrt(jnp.mean(x2 * x2, axis=-1, keepdims=True) + EPS)
        xn = x2 * r3
        fnv = fn_ref[...]
        err = xn * fnv - t_ref[...]
        loss_ref[...] += jnp.sum(err * err) * (0.5 / D)
        dy = err * (1.0 / D)
        dfn_ref[0:1, :] += jnp.sum(dy * xn, axis=0, keepdims=True)
        u = dy * fnv
        dx2_ref[...] = r3 * u - xn * (r3 * jnp.mean(u * xn, axis=-1, keepdims=True))

    def tok(w):
        return pl.BlockSpec((tm, w), lambda i: (i, 0))

    hbm = pl.BlockSpec(memory_space=pl.ANY)
    return pl.pallas_call(
        body, name="mlp_fwd", grid=(s // tm,),
        out_shape=[jax.ShapeDtypeStruct((s, D), F32), jax.ShapeDtypeStruct((s, 2 * DFF), BF16),
                   jax.ShapeDtypeStruct((s, D), F32), jax.ShapeDtypeStruct((8, 128), F32),
                   jax.ShapeDtypeStruct((8, D), F32)],
        in_specs=[tok(D), tok(QW), tok(SW), tok(D), hbm, hbm, hbm, hbm,
                  _row((1, D)), _row((1, D)), _row((1, D)), _row((1, D)), _row((1, D))],
        out_specs=[tok(D), tok(2 * DFF), tok(D), _row((8, 128)), _row((8, D))],
        scratch_shapes=[pltpu.VMEM((D, D), BF16), pltpu.VMEM(w_ga.shape, BF16), pltpu.VMEM(w_gb.shape, BF16),
                        pltpu.VMEM((DFF, D), BF16), pltpu.SemaphoreType.DMA((4,))],
        compiler_params=_cp(("arbitrary",)),
    )(x, ya, ys, tgt, w_o, w_ga, w_gb, w_dn, gate1, a2, sh2, gate2, fn)


def _mlp_bwd(x1, gu, dx2, w_o, w_ga, w_gb, w_dn, gate1, a2, sh2, gate2):
    s = x1.shape[0]
    tm = 256
    nj = 2 * DFF // 4

    def body(x1_ref, gu_ref, dx2_ref, wo_hbm, wga_hbm, wgb_hbm, wdn_hbm, g1_ref, a2_ref, s2_ref, g2_ref,
             dx1_ref, dya_ref, dys_ref, act_ref, dgu_ref, h2_ref, dsh_ref, p_ref, wo, wga, wgb, wdn, sem):
        i = pl.program_id(0)
        _load_once(i, [(wo_hbm, wo), (wga_hbm, wga), (wgb_hbm, wgb), (wdn_hbm, wdn)], sem)

        @pl.when(i == 0)
        def _():
            dsh_ref[...] = jnp.zeros_like(dsh_ref)
            p_ref[...] = jnp.zeros_like(p_ref)

        dx2 = dx2_ref[...]
        dact = _mm_nt((dx2 * g2_ref[...]).astype(BF16), wdn[...])
        gub = gu_ref[...]
        gv, uv = gub[:, 0:DFF].astype(F32), gub[:, DFF:].astype(F32)
        sg = _sig(gv)
        sl = gv * sg
        act_ref[...] = (sl * uv).astype(BF16)
        dgu = jnp.concatenate([dact * uv * (sg * (1.0 + gv * (1.0 - sg))), dact * sl], axis=1).astype(BF16)
        dgu_ref[...] = dgu
        dha = sum(_mm_nt(dgu[:, j * nj:(j + 1) * nj], wga[j]) for j in range(4))
        dhb = sum(_mm_nt(dgu[:, j * nj:(j + 1) * nj], wgb[j]) for j in range(4))
        dh = jnp.concatenate([dha, dhb], axis=1)
        x1 = x1_ref[...]
        r2 = lax.rsqrt(jnp.mean(x1 * x1, axis=-1, keepdims=True) + EPS)
        xn = x1 * r2
        a2 = a2_ref[...]
        h2_ref[...] = (xn * a2 + s2_ref[...]).astype(BF16)
        dsh_ref[0:1, :] += jnp.sum(dh, axis=0, keepdims=True)
        p_ref[0:1, :] += jnp.sum(dh * xn, axis=0, keepdims=True)
        u = dh * a2
        dx1 = dx2 + r2 * u - xn * (r2 * jnp.mean(u * xn, axis=-1, keepdims=True))
        dx1_ref[...] = dx1
        dcat = _mm_nt((dx1 * g1_ref[...]).astype(BF16), wo[...])
        dya_ref[...] = dcat[:, 0:QW]
        dys_ref[...] = dcat[:, QW:D]

    def tok(w):
        return pl.BlockSpec((tm, w), lambda i: (i, 0))

    hbm = pl.BlockSpec(memory_space=pl.ANY)
    return pl.pallas_call(
        body, name="mlp_bwd", grid=(s // tm,),
        out_shape=[jax.ShapeDtypeStruct((s, D), F32), jax.ShapeDtypeStruct((s, QW), F32),
                   jax.ShapeDtypeStruct((s, SW), F32), jax.ShapeDtypeStruct((s, DFF), BF16),
                   jax.ShapeDtypeStruct((s, 2 * DFF), BF16), jax.ShapeDtypeStruct((s, D), BF16),
                   jax.ShapeDtypeStruct((8, D), F32), jax.ShapeDtypeStruct((8, D), F32)],
        in_specs=[tok(D), tok(2 * DFF), tok(D), hbm, hbm, hbm, hbm, _row((1, D)), _row((1, D)), _row((1, D)), _row((1, D))],
        out_specs=[tok(D), tok(QW), tok(SW), tok(DFF), tok(2 * DFF), tok(D), _row((8, D)), _row((8, D))],
        scratch_shapes=[pltpu.VMEM((D, D), BF16), pltpu.VMEM(w_ga.shape, BF16), pltpu.VMEM(w_gb.shape, BF16),
                        pltpu.VMEM((DFF, D), BF16), pltpu.SemaphoreType.DMA((4,))],
        compiler_params=_cp(("arbitrary",)),
    )(x1, gu, dx2, w_o, w_ga, w_gb, w_dn, gate1, a2, sh2, gate2)


def _wgrad(name, a, b, tn, gate=None, w=None, stacked=False, carry=None):
    s, m = a.shape
    n = b.shape[1]
    tk = 512
    nk = s // tk

    def body(*refs):
        if gate is None:
            a_ref, b_ref, o_ref = refs
        else:
            a_ref, b_ref, g_ref, w_ref, o_ref, dg_ref = refs
        k = pl.program_id(1)

        @pl.when(k == 0)
        def _():
            o_ref[...] = jnp.zeros_like(o_ref)

        o_ref[...] += _mm_tn(a_ref[...], b_ref[...].astype(BF16))

        if gate is not None:
            @pl.when(k == nk - 1)
            def _():
                acc = o_ref[...]
                dg_ref[...] = jnp.zeros_like(dg_ref)
                dg_ref[0:1, :] = jnp.sum(acc * w_ref[...].astype(F32), axis=0, keepdims=True)
                o_ref[...] = acc * g_ref[...]

    in_specs = [pl.BlockSpec((tk, m), lambda j, k: (k, 0)), pl.BlockSpec((tk, tn), lambda j, k: (k, j))]
    if stacked:
        out_shape = [jax.ShapeDtypeStruct((n // tn, m, tn), F32)]
        out_specs = [pl.BlockSpec((None, m, tn), lambda j, k: (j, 0, 0))]
    else:
        out_shape = [jax.ShapeDtypeStruct((m, n), F32)]
        out_specs = [pl.BlockSpec((m, tn), lambda j, k: (0, j))]
    args = [a, b]
    if gate is not None:
        in_specs += [pl.BlockSpec((1, tn), lambda j, k: (0, j)), pl.BlockSpec((m, tn), lambda j, k: (0, j))]
        out_shape.append(jax.ShapeDtypeStruct((8, n), F32))
        out_specs.append(pl.BlockSpec((8, tn), lambda j, k: (0, j)))
        args += [gate, w]
    return _pcall(body, args, name=name, grid=(n // tn, nk), out_shape=out_shape, in_specs=in_specs,
                  out_specs=out_specs, carry=carry)


def _adamw_rows(name, w, g, m, v, tr):
    r = w.shape[0]

    def body(w_ref, g_ref, m_ref, v_ref, d_ref, mo_ref, vo_ref):
        d_ref[...], mo_ref[...], vo_ref[...] = _adamw(w_ref[...], g_ref[...], m_ref[...], v_ref[...])

    spec = pl.BlockSpec((tr, D), lambda i: (i, 0))
    return pl.pallas_call(body, name=name, grid=(r // tr,), out_shape=[jax.ShapeDtypeStruct(w.shape, F32)] * 3,
                          in_specs=[spec] * 4, out_specs=[spec] * 3, compiler_params=_cp(("parallel",)))(w, g, m, v)


def _sum8(parts):
    r = parts.shape[1]

    def body(p_ref, o_ref):
        acc = p_ref[0]
        for b in range(1, 8):
            acc = acc + p_ref[b]
        o_ref[...] = acc

    return pl.pallas_call(body, name="sum8", out_shape=jax.ShapeDtypeStruct((r, D), F32))(parts)


def _add_half(name, g, got, where):
    rr, cc = g.shape[2:]

    def body(w_ref, g_ref, r_ref, o_ref, own_ref):
        s = g_ref[...] + r_ref[...]
        o_ref[...] = s.astype(BF16)

        @pl.when(pl.program_id(0) == w_ref[1])
        def _():
            own_ref[...] = s

    spec = pl.BlockSpec((None, rr, cc), lambda i, w_ref: (i, 0, 0))
    return _pcall(body, [where, g, got], name=name, grid=(4,), nprefetch=1,
                  out_shape=[jax.ShapeDtypeStruct(got.shape, BF16), jax.ShapeDtypeStruct((rr, cc), F32)],
                  in_specs=[pl.BlockSpec((None, None, rr, cc), lambda i, w_ref: (i, w_ref[0], 0, 0)), spec],
                  out_specs=[spec, pl.BlockSpec((rr, cc), lambda i, w_ref: (0, 0))])


def _add_chips(name, own, got):
    rr, cc = own.shape
    tr = rr // 2

    def body(s_ref, r_ref, o_ref):
        o_ref[...] = ((s_ref[...] + r_ref[0].astype(F32)) + r_ref[1].astype(F32)) + r_ref[2].astype(F32)

    spec = pl.BlockSpec((tr, cc), lambda i: (i, 0))
    return _pcall(body, [own, got], name=name, grid=(2,), out_shape=[jax.ShapeDtypeStruct((rr, cc), F32)],
                  in_specs=[spec, pl.BlockSpec((3, tr, cc), lambda i: (0, i, 0))], out_specs=[spec])[0]


def _adamw_halves(name, mine, got, w, m, v, where):
    rr, cc = mine.shape
    tr = rr // 2

    def body(w_ref_, t_ref, r_ref, w_ref, m_ref, v_ref, g_ref, d_ref, mo_ref, vo_ref):
        g = jnp.where(pl.program_id(0) == w_ref_[0], t_ref[...], r_ref[...])
        g_ref[...] = g
        d_ref[...], mo_ref[...], vo_ref[...] = _adamw(w_ref[...], g, m_ref[...], v_ref[...])

    half = pl.BlockSpec((tr, cc), lambda h, i, w_ref_: (i, 0))
    full = pl.BlockSpec((None, tr, cc), lambda h, i, w_ref_: (0, 2 * h + i, 0))
    return _pcall(body, [where, mine, got, w, m, v], name=name, grid=(2, 2), nprefetch=1,
                  out_shape=[jax.ShapeDtypeStruct(w.shape, F32)] * 4,
                  in_specs=[half, half, full, full, full], out_specs=[full] * 4)


def _bias_table(rel_bias, bucket, mask):
    def body(rb_ref, bk_ref, mk_ref, o_ref):
        bk = bk_ref[...]
        valid = mk_ref[...] > 0
        for h in range(NH):
            acc = jnp.zeros((BLK, 2 * BLK), F32)
            for b in range(NBUCKET):
                acc = jnp.where(bk == b, rb_ref[b, h], acc)
            o_ref[h] = jnp.where(valid, acc, NEG)

    vmem = pl.BlockSpec(memory_space=pltpu.VMEM)
    return pl.pallas_call(
        body, name="bias_table", out_shape=jax.ShapeDtypeStruct((NH, BLK, 2 * BLK), F32),
        in_specs=[pl.BlockSpec(memory_space=pltpu.SMEM), vmem, vmem], out_specs=vmem,
    )(rel_bias, bucket, mask)


def _pack_small(dsh1, p1, dsh2, p2, dg1a, dg1b, dg2, norm1, norm2, scale1, scale2, dcw, dcb, dfn,
                dnw_attn, dnw_ssm, dhd, av, dsink, drel):
    def body(dsh1_ref, p1_ref, dsh2_ref, p2_ref, dg1a_ref, dg1b_ref, dg2_ref, n1_ref, n2_ref, s1_ref, s2_ref,
             dcw_ref, dcb_ref, dfn_ref, da_ref, ds_ref, dhd_ref, av_ref, dsink_ref, drel_ref, o_ref):
        o_ref[...] = jnp.zeros_like(o_ref)
        p1v, p2v = p1_ref[0:1, :], p2_ref[0:1, :]
        o_ref[0:1, :] = dsh1_ref[0:1, :]
        o_ref[1:2, :] = p1v * n1_ref[...]
        o_ref[2:3, :] = dg1a_ref[0:1, :] + dg1b_ref[0:1, :]
        o_ref[3:4, :] = dsh2_ref[0:1, :]
        o_ref[4:5, :] = p2v * n2_ref[...]
        o_ref[5:6, :] = dg2_ref[0:1, :]
        o_ref[6:7, :] = p1v * (1.0 + s1_ref[...])
        o_ref[7:11, :] = dcw_ref[0:4, :]
        o_ref[11:12, :] = dcb_ref[0:1, :]
        o_ref[12:13, :] = p2v * (1.0 + s2_ref[...])
        o_ref[13:14, :] = dfn_ref[0:1, :]
        o_ref[14:15, 0:QW] = da_ref[0:1, :]
        o_ref[15:16, 0:SW] = ds_ref[0:1, :]
        o_ref[16:17, 0:128] = dhd_ref[0:1, :]
        o_ref[17:18, 0:128] = dhd_ref[1:2, :] * av_ref[...]
        o_ref[18:19, 0:128] = dhd_ref[2:3, :]
        o_ref[19:20, 0:128] = dsink_ref[0:1, :]
        o_ref[24:56, 0:128] = drel_ref[...]

    return pl.pallas_call(body, name="pack_small", out_shape=jax.ShapeDtypeStruct((56, D), F32))(
        dsh1, p1, dsh2, p2, dg1a, dg1b, dg2, norm1, norm2, scale1, scale2, dcw, dcb, dfn,
        dnw_attn, dnw_ssm, dhd, av, dsink, drel)


def _pad_row(a, rows=1):
    return jnp.pad(a.reshape(rows, -1), ((0, 0), (0, D - a.size // rows)))


def kernel(x, c, ada_w, ada_b, norm1, w_in, conv_w, conv_b, dt_bias, A_log, D_skip, sinks, attn_out_norm, ssm_out_norm, w_o, norm2, w_gate_up, w_down, rel_bias, final_norm, loss_target, m_ada_w, m_ada_b, m_norm1, m_w_in, m_conv_w, m_conv_b, m_dt_bias, m_A_log, m_D_skip, m_sinks, m_attn_out_norm, m_ssm_out_norm, m_w_o, m_norm2, m_w_gate_up, m_w_down, m_rel_bias, m_final_norm, v_ada_w, v_ada_b, v_norm1, v_w_in, v_conv_w, v_conv_b, v_dt_bias, v_A_log, v_D_skip, v_sinks, v_attn_out_norm, v_ssm_out_norm, v_w_o, v_norm2, v_w_gate_up, v_w_down, v_rel_bias, v_final_norm):
    xi, yi, ci = lax.axis_index("x"), lax.axis_index("y"), lax.axis_index("c")
    chip = 2 * xi + yi
    me = 4 * xi + 2 * yi + ci
    where = jnp.stack([ci, chip]).astype(jnp.int32)
    xs2, tgt = x[0], loss_target[0]

    first = jnp.concatenate([c, _pad_row(conv_w[0], CK), jnp.zeros((3, D), F32)], axis=0)
    w_in_b, w_o_b, w_dn_b = w_in[0].astype(BF16), w_o[0].astype(BF16), w_down[0].astype(BF16)
    w_gu_b = w_gate_up[0].astype(BF16)
    first_all, w_in_g = _exchange("gather_first", _merge(_gather8_carry(first), _gather_chips_carry([w_in_b])))
    c_all = first_all[:, 0, :]
    cw_full = jnp.concatenate([first_all[2 * j, 1:1 + CK, 0:256] for j in range(4)], axis=1)
    w_in_f = jnp.pad(jnp.transpose(w_in_g, (1, 0, 2)).reshape(D, IN_W), ((0, 0), (0, PROJ_W - IN_W)))

    ncol = ada_w.shape[2]
    mod_cols = _ada_fwd(c_all, ada_w[0], lax.dynamic_slice(ada_b, (0, chip * ncol), (1, ncol)))
    mod_all = _exchange("gather_mod", _gather_chips_carry([mod_cols]))[0]
    mod = lax.dynamic_slice(jnp.transpose(mod_all, (1, 0, 2)).reshape(8, 4 * ncol), (me, 0), (1, 4 * ncol))
    shift1, scale1, gate1, shift2, scale2, gate2 = [mod[:, j * D:(j + 1) * D] for j in range(6)]
    a1 = norm1 * (1.0 + scale1)
    a2 = norm2 * (1.0 + scale2)

    hdn = DFF // 8
    q, kv, z, xbc, dtr, w_o_g, w_dna_g = _in_proj_fwd(xs2, a1, shift1, w_in_f,
                                                      carry=_gather_chips_carry([w_o_b, w_dn_b[0:hdn]]))
    w_o_f = w_o_g.reshape(D, D)
    bucket, mask = _attn_geometry()
    bucket = jnp.asarray(bucket)
    bias = _bias_table(rel_bias, bucket, jnp.asarray(mask.astype(np.int32)))
    sinks1 = sinks[0]
    ya, w_ga_g = _attn_fwd(q, kv, bias, sinks1, attn_out_norm, carry=_gather_chips_carry([w_gu_b[0:D // 2]]))
    cw8 = jnp.concatenate([cw_full, jnp.zeros((4, XBCW), F32)], axis=0)
    dtb = _pad_row(dt_bias)[:, 0:128]
    av = _pad_row(-jnp.exp(A_log))[:, 0:128]
    dk = jnp.repeat(D_skip, HD, axis=1)
    ys, hs, w_gb_g, w_dnb_g = _ssd_fwd(z, xbc, dtr, cw8, conv_b, dtb, av, dk, ssm_out_norm,
                                       carry=_gather_chips_carry([w_gu_b[D // 2:D], w_dn_b[hdn:2 * hdn]]))
    w_dn_f = jnp.stack([w_dna_g, w_dnb_g], axis=1).reshape(DFF, D)
    fn = final_norm[None, :]
    x1, gu, dx2, loss_acc, dfn = _mlp_fwd(xs2, ya, ys, tgt, w_o_f, w_ga_g, w_gb_g, w_dn_f, gate1, a2, shift2, gate2, fn)
    loss = lax.psum(loss_acc[0, 0], ("x", "y", "c"))

    def to_sibling(p):
        return _Carry([p], [jax.ShapeDtypeStruct((4,) + p.shape[2:], F32)],
                      lambda x_, y_, c_: [(_SIBLING, 0, (j, 1 - c_), 0, j) for j in range(4)])

    def to_chips(s4):
        return _Carry([s4], [jax.ShapeDtypeStruct((3,) + s4.shape[1:], s4.dtype)],
                      lambda x_, y_, c_: [(f, 0, jnp.bitwise_xor(2 * x_ + y_, k + 1), 0, k) for k, f in enumerate(_CHIPS3)])

    def back(t):
        return _Carry([t[None]], [jax.ShapeDtypeStruct((1,) + t.shape, F32)], lambda x_, y_, c_: [(_SIBLING, 0, 0, 0, 0)])

    dx1, dya, dys, act, dgu, h2, dsh2, p2 = _mlp_bwd(x1, gu, dx2, w_o_f, w_ga_g, w_gb_g, w_dn_f, gate1, a2, shift2, gate2)
    p_gu = _wgrad("wgrad_gate_up", h2, dgu, 2 * DFF // 4, stacked=True)[0].reshape(4, 2, D // 2, 2 * DFF // 4)
    g_dn, dg2, got1_gu = _wgrad("wgrad_down", act, dx2, D // 2, gate2, w_dn_f, carry=to_sibling(p_gu))
    p_dn = g_dn.reshape(4, 2, DFF // 8, D)
    s4_gu, own_gu = _add_half("rs_add_half_gu", p_gu, got1_gu, where)
    dq, dkv, dbias, dsink, dnw_attn, got2_gu, got1_dn = _attn_bwd(
        q, kv, dya, bias, sinks1, attn_out_norm, carry=_merge(to_chips(s4_gu), to_sibling(p_dn)))
    drel = _rel_bias_grad(dbias, bucket)
    mine_gu = _add_chips("rs_add_chips_gu", own_gu, got2_gu)
    s4_dn, own_dn = _add_half("rs_add_half_dn", p_dn, got1_dn, where)
    dz, dxbc, ddt, dcw, dcb, dnw_ssm, dhd, got2_dn, got3_gu = _ssd_bwd(
        z, xbc, dtr, dys, hs, cw8, conv_b, dtb, av, dk, ssm_out_norm, carry=_merge(to_chips(s4_dn), back(mine_gu)))
    mine_dn = _add_chips("rs_add_chips_dn", own_dn, got2_dn)
    grad_x, dproj, h1, dsh1, p1 = _in_proj_bwd(xs2, dx1, a1, shift1, w_in_f, dq, dkv, dz, dxbc, ddt)
    g_in, got3_dn = _wgrad("wgrad_in", h1, dproj, PROJ_W, carry=back(mine_dn))
    p_in = jnp.transpose(g_in[:, 0:IN_W].reshape(D, 4, IN_W // 4), (1, 0, 2)).reshape(4, 2, D // 2, IN_W // 4)
    g_oa, dg1a, got1_in = _wgrad("wgrad_o_attn", ya, dx1, D, gate1, w_o_f[0:QW], carry=to_sibling(p_in))
    s4_in, own_in = _add_half("rs_add_half_in", p_in, got1_in, where)
    g_os, dg1b, got2_in = _wgrad("wgrad_o_ssm", ys, dx1, D, gate1, w_o_f[QW:D], carry=to_chips(s4_in))
    mine_in = _add_chips("rs_add_chips_in", own_in, got2_in)
    p_o = jnp.concatenate([g_oa, g_os], axis=0).reshape(4, 2, D // 8, D)

    small = _pack_small(dsh1, p1, dsh2, p2, dg1a, dg1b, dg2, norm1, norm2, scale1, scale2, dcw, dcb, dfn,
                        dnw_attn, dnw_ssm, dhd, av, dsink, drel)
    small_all, got1_o, got3_in = _exchange(
        "gather_small", _merge(_gather8_carry(small), to_sibling(p_o), back(mine_in)))
    gsum = _sum8(small_all)
    s4_o, own_o = _add_half("rs_add_half_o", p_o, got1_o, where)
    mine_o = _add_chips("rs_add_chips_o", own_o, _exchange("rs_chips_o", to_chips(s4_o))[0])
    got3_o = _exchange("rs_back_o", back(mine_o))[0]

    def conv_rows(a):
        return lax.dynamic_update_slice(jnp.zeros((CK, D), F32), a[0], (0, chip * 256))

    def pack(p):
        rows = [jnp.zeros((6, D), F32), p["norm1"], conv_rows(p["conv_w"]), p["conv_b"], p["norm2"],
                p["final_norm"][None, :], _pad_row(p["attn_out_norm"]), _pad_row(p["ssm_out_norm"]),
                _pad_row(p["dt_bias"]), _pad_row(p["A_log"]), _pad_row(p["D_skip"]), _pad_row(p["sinks"]),
                jnp.zeros((4, D), F32), _pad_row(p["rel_bias"], NBUCKET)]
        return jnp.concatenate(rows, axis=0)

    names = ["norm1", "conv_w", "conv_b", "norm2", "final_norm", "attn_out_norm", "ssm_out_norm", "dt_bias",
             "A_log", "D_skip", "sinks", "rel_bias"]
    ws = dict(zip(names, [norm1, conv_w, conv_b, norm2, final_norm, attn_out_norm, ssm_out_norm, dt_bias, A_log,
                          D_skip, sinks, rel_bias]))
    ms = dict(zip(names, [m_norm1, m_conv_w, m_conv_b, m_norm2, m_final_norm, m_attn_out_norm, m_ssm_out_norm,
                          m_dt_bias, m_A_log, m_D_skip, m_sinks, m_rel_bias]))
    vs = dict(zip(names, [v_norm1, v_conv_w, v_conv_b, v_norm2, v_final_norm, v_attn_out_norm, v_ssm_out_norm,
                          v_dt_bias, v_A_log, v_D_skip, v_sinks, v_rel_bias]))
    wp = pack(ws).at[0:6].set(ada_b.reshape(6, D))
    mp = pack(ms).at[0:6].set(m_ada_b.reshape(6, D))
    vp = pack(vs).at[0:6].set(v_ada_b.reshape(6, D))
    sd, sm, sv = _adamw_rows("adamw_small", wp, gsum, mp, vp, 56)

    def unpack(p):
        conv = lax.dynamic_slice(p[7:11], (0, chip * 256), (CK, 256))[None]
        return dict(ada_b=p[0:6].reshape(1, 6 * D), norm1=p[6:7], conv_w=conv, conv_b=p[11:12], norm2=p[12:13],
                    final_norm=p[13], attn_out_norm=p[14:15, 0:QW], ssm_out_norm=p[15:16, 0:SW],
                    dt_bias=p[16:17, 0:NH], A_log=p[17:18, 0:NH], D_skip=p[18:19, 0:NH], sinks=p[19:20, 0:NH],
                    rel_bias=p[24:56, 0:NH])

    small_out = [unpack(p) for p in (gsum, sd, sm, sv)]

    dmod_all = small_all[:, 0:6, :].reshape(8, 6 * D)
    dmod_loc = lax.dynamic_slice(dmod_all, (0, chip * ncol), (8, ncol))
    ada_out = _ada_bwd_adamw(c_all.T, dmod_loc, ada_w[0], m_ada_w[0], v_ada_w[0])

    big_gu = _adamw_halves("adamw_gate_up", mine_gu, got3_gu[0], w_gate_up, m_w_gate_up, v_w_gate_up, where)
    big_dn = _adamw_halves("adamw_down", mine_dn, got3_dn[0], w_down, m_w_down, v_w_down, where)
    big_o = _adamw_halves("adamw_o", mine_o, got3_o[0], w_o, m_w_o, v_w_o, where)
    big_in = _adamw_halves("adamw_in", mine_in, got3_in[0], w_in, m_w_in, v_w_in, where)
    big = [big_in, big_o, big_gu, big_dn]

    order = ["ada_w", "ada_b", "norm1", "w_in", "conv_w", "conv_b", "dt_bias", "A_log", "D_skip", "sinks",
             "attn_out_norm", "ssm_out_norm", "w_o", "norm2", "w_gate_up", "w_down", "rel_bias", "final_norm"]
    bigname = {"w_in": 0, "w_o": 1, "w_gate_up": 2, "w_down": 3}
    res = [loss, grad_x[None]]
    for kind in range(4):
        for nm in order:
            if nm == "ada_w":
                res.append(ada_out[kind][None])
            elif nm in bigname:
                res.append(big[bigname[nm]][kind])
            else:
                res.append(small_out[kind][nm])
    return tuple(res)
```

```python
import numpy as np
import jax
import jax.numpy as jnp
from jax import lax
from jax.experimental import pallas as pl
from jax.experimental.pallas import tpu as pltpu

F32, BF16 = jnp.float32, jnp.bfloat16
HI = lax.Precision.HIGHEST

D = 1024
QW, KVW = 512, 128
NH, HD, NKV = 8, 64, 2
SW = 512
NST = 128
XBCW = 1024
CK = 4
BLK = 128
DFF = 2816
IN_W = 2312
PROJ_W = 2432
EPS = 1e-6
NEG = -1e30
NBUCKET = 32

B1, B2, LR, AEPS, WD, STEP = 0.9, 0.999, 0.001, 1e-08, 0.01, 10

VMEM_LIMIT = 56 * 1024 * 1024

_NT = (((1,), (1,)), ((), ()))
_TN = (((0,), (0,)), ((), ()))


def _mm(a, b):
    return jnp.dot(a, b, preferred_element_type=F32)


def _mm_nt(a, b):
    return lax.dot_general(a, b, _NT, preferred_element_type=F32)


def _mm_tn(a, b):
    return lax.dot_general(a, b, _TN, preferred_element_type=F32)


def _mm_hi(a, b):
    return jnp.dot(a, b, preferred_element_type=F32, precision=HI)


def _split3(x):
    hi = x.astype(BF16)
    r = x - hi.astype(F32)
    mid = r.astype(BF16)
    lo = (r - mid.astype(F32)).astype(BF16)
    return hi, mid, lo


def _sel_r(x, e):
    hi, mid, lo = _split3(x)
    return (_mm(hi, e) + _mm(mid, e)) + _mm(lo, e)


def _sel_l(e, x):
    hi, mid, lo = _split3(x)
    return (_mm(e, hi) + _mm(e, mid)) + _mm(e, lo)


def _sig(x):
    return 1.0 / (1.0 + jnp.exp(-x))


def _cp(sem):
    return pltpu.CompilerParams(dimension_semantics=sem, vmem_limit_bytes=VMEM_LIMIT)


def _row(shape):
    nd = len(shape)
    return pl.BlockSpec(shape, lambda *_: (0,) * nd)


def _adamw(w, g, m, v):
    m = B1 * m + (1.0 - B1) * g
    v = B2 * v + (1.0 - B2) * (g * g)
    m_hat = m / (1.0 - B1 ** STEP)
    v_hat = v / (1.0 - B2 ** STEP)
    delta = -LR * (m_hat / (jnp.sqrt(v_hat) + AEPS) + WD * w)
    return delta, m, v


class _Carry:
    def __init__(self, inps, outs, copies):
        self.inps, self.outs, self.copies = list(inps), list(outs), copies
        self.n = len(copies(0, 0, 0))

    def descriptors(self, in_refs, out_refs, send_sems, recv_sems):
        x, y, c = lax.axis_index("x"), lax.axis_index("y"), lax.axis_index("c")
        out = []
        for j, (flip, a, si, o, di) in enumerate(self.copies(x, y, c)):
            if flip is None:
                out.append(pltpu.make_async_copy(in_refs[a].at[si], out_refs[o].at[di], send_sems.at[j]))
            else:
                fx, fy, fc = flip
                peer = (1 - x if fx else x, 1 - y if fy else y, 1 - c if fc else c)
                out.append(pltpu.make_async_remote_copy(
                    src_ref=in_refs[a].at[si], dst_ref=out_refs[o].at[di],
                    send_sem=send_sems.at[j], recv_sem=recv_sems.at[j],
                    device_id=peer, device_id_type=pl.DeviceIdType.MESH))
        return out


def _pcall(body, args, *, name, grid, in_specs, out_specs, out_shape, scratch_shapes=(), sem=None, nprefetch=0,
           carry=None):
    out_shape, out_specs = list(out_shape), list(out_specs)
    in_specs, scratch_shapes = list(in_specs), list(scratch_shapes)
    nin, nout, nscr = len(in_specs), len(out_shape), len(scratch_shapes)
    run = body
    if carry is not None:
        ncin, ncout = len(carry.inps), len(carry.outs)
        hbm = pl.BlockSpec(memory_space=pl.ANY)

        def run(*refs):
            pre, r = refs[:nprefetch], refs[nprefetch:]
            ins, cins = r[:nin], r[nin:nin + ncin]
            r = r[nin + ncin:]
            outs, couts = r[:nout], r[nout:nout + ncout]
            r = r[nout + ncout:]
            scr, (send_sems, recv_sems) = r[:nscr], r[nscr:]
            first = pl.program_id(0) == 0
            last = pl.program_id(0) == grid[0] - 1
            for ax in range(1, len(grid)):
                first = jnp.logical_and(first, pl.program_id(ax) == 0)
                last = jnp.logical_and(last, pl.program_id(ax) == grid[ax] - 1)

            @pl.when(first)
            def _():
                for d in carry.descriptors(cins, couts, send_sems, recv_sems):
                    d.start()

            body(*pre, *ins, *outs, *scr)

            @pl.when(last)
            def _():
                for d in carry.descriptors(cins, couts, send_sems, recv_sems):
                    d.wait()

        in_specs = in_specs + [hbm] * ncin
        out_specs = out_specs + [hbm] * ncout
        out_shape = out_shape + carry.outs
        scratch_shapes = scratch_shapes + [pltpu.SemaphoreType.DMA((carry.n,)), pltpu.SemaphoreType.DMA((carry.n,))]
        args = list(args) + carry.inps
    if sem is None:
        sem = ("arbitrary",) * len(grid)
    if nprefetch:
        kw = dict(grid_spec=pltpu.PrefetchScalarGridSpec(num_scalar_prefetch=nprefetch, grid=grid, in_specs=in_specs,
                                                         out_specs=out_specs, scratch_shapes=scratch_shapes))
    else:
        kw = dict(grid=grid, in_specs=in_specs, out_specs=out_specs, scratch_shapes=scratch_shapes)
    res = pl.pallas_call(run, name=name, out_shape=out_shape, compiler_params=_cp(sem), **kw)(*args)
    return list(res)


def _merge(*carries):
    inps, outs, offs = [], [], []
    for cr in carries:
        offs.append((len(inps), len(outs)))
        inps += cr.inps
        outs += cr.outs

    def copies(x, y, c):
        return [(f, a + io, si, o + oo, di) for cr, (io, oo) in zip(carries, offs) for f, a, si, o, di in cr.copies(x, y, c)]

    return _Carry(inps, outs, copies)


def _exchange(name, carry):
    return _pcall(lambda: None, [], name=name, grid=(1,), in_specs=[], out_specs=[], out_shape=[], carry=carry)


_ALL7 = [(f >> 2 & 1, f >> 1 & 1, f & 1) for f in range(1, 8)]
_CHIPS3 = [(0, 1, 0), (1, 0, 0), (1, 1, 0)]
_SIBLING = (0, 0, 1)


def _gather8_carry(blk):
    def copies(x, y, c):
        me = 4 * x + 2 * y + c
        return [(None, 0, 0, 0, me)] + [(f, 0, 0, 0, me) for f in _ALL7]

    return _Carry([blk[None]], [jax.ShapeDtypeStruct((8,) + blk.shape, blk.dtype)], copies)


def _gather_chips_carry(blks):
    def copies(x, y, c):
        chip = 2 * x + y
        return [(f, a, 0, a, chip) for a in range(len(blks)) for f in [None] + _CHIPS3]

    return _Carry([b[None] for b in blks], [jax.ShapeDtypeStruct((4,) + b.shape, b.dtype) for b in blks], copies)


def _ada_fwd(c_all, w_loc, b_loc):
    n = w_loc.shape[1]
    tn = 512

    def body(c_ref, w_ref, b_ref, o_ref):
        cv = c_ref[...]
        cond = cv * _sig(cv)
        o_ref[...] = _mm_hi(cond, w_ref[...]) + b_ref[...]

    return pl.pallas_call(
        body, name="ada_fwd", grid=(n // tn,),
        out_shape=jax.ShapeDtypeStruct((8, n), F32),
        in_specs=[_row((8, D)), pl.BlockSpec((D, tn), lambda j: (0, j)), pl.BlockSpec((1, tn), lambda j: (0, j))],
        out_specs=pl.BlockSpec((8, tn), lambda j: (0, j)),
        compiler_params=_cp(("parallel",)),
    )(c_all, w_loc, b_loc)


def _ada_bwd_adamw(c_all_t, dmod_loc, w, m, v, carry=None):
    n = w.shape[1]
    tn = 512

    def body(ct_ref, dm_ref, w_ref, m_ref, v_ref, g_ref, d_ref, mo_ref, vo_ref):
        ct = ct_ref[...]
        cond = ct * _sig(ct)
        dm = dm_ref[...]
        g = cond[:, 0:1] * dm[0:1, :]
        for b in range(1, 8):
            g = g + cond[:, b:b + 1] * dm[b:b + 1, :]
        g_ref[...] = g
        d_ref[...], mo_ref[...], vo_ref[...] = _adamw(w_ref[...], g, m_ref[...], v_ref[...])

    wspec = pl.BlockSpec((D, tn), lambda j: (0, j))
    return _pcall(
        body, [c_all_t, dmod_loc, w, m, v], name="ada_bwd_adamw", grid=(n // tn,),
        out_shape=[jax.ShapeDtypeStruct((D, n), F32)] * 4,
        in_specs=[_row((D, 8)), pl.BlockSpec((8, tn), lambda j: (0, j)), wspec, wspec, wspec],
        out_specs=[wspec] * 4, carry=carry)


def _in_proj_fwd(x, a1, sh1, w_in, carry=None):
    s = x.shape[0]
    tm = 512

    def body(x_ref, a_ref, s_ref, w_ref, q_ref, kv_ref, z_ref, xbc_ref, dt_ref):
        xv = x_ref[...]
        r = lax.rsqrt(jnp.mean(xv * xv, axis=-1, keepdims=True) + EPS)
        h = (xv * r * a_ref[...] + s_ref[...]).astype(BF16)
        p = _mm(h, w_ref[...])
        q_ref[...] = p[:, 0:512].astype(BF16)
        kv_ref[...] = p[:, 512:768].astype(BF16)
        z_ref[...] = p[:, 768:1280]
        xbc_ref[...] = p[:, 1280:2304]
        dt_ref[...] = p[:, 2304:2432]

    def tok(w):
        return pl.BlockSpec((tm, w), lambda i: (i, 0))

    return _pcall(
        body, [x, a1, sh1, w_in], name="in_proj_fwd", grid=(s // tm,),
        out_shape=[jax.ShapeDtypeStruct((s, QW), BF16), jax.ShapeDtypeStruct((s, 2 * KVW), BF16),
                   jax.ShapeDtypeStruct((s, SW), F32), jax.ShapeDtypeStruct((s, XBCW), F32),
                   jax.ShapeDtypeStruct((s, 128), F32)],
        in_specs=[tok(D), _row((1, D)), _row((1, D)), _row((D, PROJ_W))],
        out_specs=[tok(QW), tok(2 * KVW), tok(SW), tok(XBCW), tok(128)], carry=carry)


def _in_proj_bwd(x, dx1, a1, sh1, w_in, dq, dkv, dz, dxbc, ddt, carry=None):
    s = x.shape[0]
    tm = 512

    def body(x_ref, dx1_ref, a_ref, s_ref, w_ref, dq_ref, dkv_ref, dz_ref, dxbc_ref, ddt_ref,
             gx_ref, dproj_ref, h_ref, dsh_ref, p_ref):
        i = pl.program_id(0)

        @pl.when(i == 0)
        def _():
            dsh_ref[...] = jnp.zeros_like(dsh_ref)
            p_ref[...] = jnp.zeros_like(p_ref)

        dproj = jnp.concatenate([dq_ref[...], dkv_ref[...], dz_ref[...], dxbc_ref[...], ddt_ref[...]], axis=1)
        dproj_ref[...] = dproj
        dh = _mm_nt(dproj, w_ref[...])
        xv = x_ref[...]
        r = lax.rsqrt(jnp.mean(xv * xv, axis=-1, keepdims=True) + EPS)
        xn = xv * r
        a = a_ref[...]
        h_ref[...] = (xn * a + s_ref[...]).astype(BF16)
        dsh_ref[0:1, :] += jnp.sum(dh, axis=0, keepdims=True)
        p_ref[0:1, :] += jnp.sum(dh * xn, axis=0, keepdims=True)
        u = dh * a
        gx_ref[...] = dx1_ref[...] + r * u - xn * (r * jnp.mean(u * xn, axis=-1, keepdims=True))

    def tok(w):
        return pl.BlockSpec((tm, w), lambda i: (i, 0))

    return _pcall(
        body, [x, dx1, a1, sh1, w_in, dq, dkv, dz, dxbc, ddt], name="in_proj_bwd", grid=(s // tm,),
        out_shape=[jax.ShapeDtypeStruct((s, D), F32), jax.ShapeDtypeStruct((s, PROJ_W), BF16),
                   jax.ShapeDtypeStruct((s, D), BF16), jax.ShapeDtypeStruct((8, D), F32),
                   jax.ShapeDtypeStruct((8, D), F32)],
        in_specs=[tok(D), tok(D), _row((1, D)), _row((1, D)), _row((D, PROJ_W)),
                  tok(QW), tok(2 * KVW), tok(SW), tok(XBCW), tok(128)],
        out_specs=[tok(D), tok(PROJ_W), tok(D), _row((8, D)), _row((8, D))], carry=carry)


def _attn_geometry():
    dist = np.arange(BLK)[:, None] + BLK - np.arange(2 * BLK)[None, :]
    n = np.maximum(dist, 0)
    max_exact = NBUCKET // 2
    large = max_exact + (np.log(np.maximum(n, 1) / max_exact) / np.log(128 / max_exact)
                         * (NBUCKET - max_exact)).astype(np.int32)
    large = np.minimum(large, NBUCKET - 1)
    bucket = np.where(n < max_exact, n, large).astype(np.int32)
    mask = (dist >= 0) & (dist < 128)
    return bucket, mask


def _attn_heads(n, q_ref, kvp_ref, kvc_ref, bias_ref, sinks_ref):
    qv = q_ref[...] * 0.125
    kvw = jnp.concatenate([kvp_ref[...], kvc_ref[...]], axis=0)
    col = lax.broadcasted_iota(jnp.int32, (BLK, 2 * BLK), 1)
    first = jnp.where(jnp.logical_and(n == 0, col < BLK), NEG, 0.0)
    groups = []
    for g in range(NKV):
        qs = jnp.concatenate([qv[:, (4 * g + r) * HD:(4 * g + r + 1) * HD] for r in range(4)], axis=0)
        kw = kvw[:, g * HD:(g + 1) * HD]
        vw = kvw[:, KVW + g * HD:KVW + (g + 1) * HD]
        sc = _mm_nt(qs, kw)
        pn, ps = [], []
        for r in range(4):
            h = 4 * g + r
            sr = sc[r * BLK:(r + 1) * BLK] + bias_ref[h] + first
            sink = sinks_ref[h]
            m = jnp.maximum(jnp.max(sr, axis=-1, keepdims=True), sink)
            p = jnp.exp(sr - m)
            es = jnp.exp(sink - m)
            inv = 1.0 / (jnp.sum(p, axis=-1, keepdims=True) + es)
            pn.append(p * inv)
            ps.append(es * inv)
        pn = jnp.concatenate(pn, axis=0)
        ps = jnp.concatenate(ps, axis=0)
        o = _mm(pn.astype(BF16), vw)
        groups.append((qs, kw, vw, pn, ps, o))
    return groups


def _unstack_heads(parts):
    return jnp.concatenate([p[r * BLK:(r + 1) * BLK] for p in parts for r in range(4)], axis=1)


def _attn_fwd(q, kv, bias, sinks, nw, carry=None):
    s = q.shape[0]

    def body(q_ref, kvp_ref, kvc_ref, bias_ref, sinks_ref, nw_ref, y_ref):
        n = pl.program_id(0)
        groups = _attn_heads(n, q_ref, kvp_ref, kvc_ref, bias_ref, sinks_ref)
        o = _unstack_heads([g[5] for g in groups])
        r = lax.rsqrt(jnp.mean(o * o, axis=-1, keepdims=True) + EPS)
        y_ref[...] = (o * r * nw_ref[...]).astype(BF16)

    return _pcall(
        body, [q, kv, kv, bias, sinks, nw], name="attn_fwd", grid=(s // BLK,),
        out_shape=[jax.ShapeDtypeStruct((s, QW), BF16)],
        in_specs=[pl.BlockSpec((BLK, QW), lambda n: (n, 0)),
                  pl.BlockSpec((BLK, 2 * KVW), lambda n: (jnp.maximum(n - 1, 0), 0)),
                  pl.BlockSpec((BLK, 2 * KVW), lambda n: (n, 0)),
                  _row((NH, BLK, 2 * BLK)),
                  pl.BlockSpec(memory_space=pltpu.SMEM),
                  _row((1, QW))],
        out_specs=[pl.BlockSpec((BLK, QW), lambda n: (n, 0))], carry=carry)


def _attn_bwd(q, kv, dya, bias, sinks, nw, carry=None):
    s = q.shape[0]
    nb = s // BLK

    def body(q_ref, kvp_ref, kvc_ref, dy_ref, bias_ref, sinks_ref, nw_ref,
             dq_ref, dkv_ref, dbias_ref, dsink_ref, dnw_ref, carry_ref):
        n = pl.program_id(0)

        @pl.when(n == 0)
        def _():
            carry_ref[...] = jnp.zeros_like(carry_ref)
            dbias_ref[...] = jnp.zeros_like(dbias_ref)
            dsink_ref[...] = jnp.zeros_like(dsink_ref)
            dnw_ref[...] = jnp.zeros_like(dnw_ref)

        @pl.when(n < nb)
        def _():
            groups = _attn_heads(n, q_ref, kvp_ref, kvc_ref, bias_ref, sinks_ref)
            o = _unstack_heads([g[5] for g in groups])
            r = lax.rsqrt(jnp.mean(o * o, axis=-1, keepdims=True) + EPS)
            dy = dy_ref[...]
            on = o * r
            dnw_ref[0:1, :] += jnp.sum(dy * on, axis=0, keepdims=True)
            u = dy * nw_ref[...]
            do = r * u - on * (r * jnp.mean(u * on, axis=-1, keepdims=True))
            dq_parts, dk_parts, dv_parts = [], [], []
            for g, (qs, kw, vw, pn, ps, og) in enumerate(groups):
                dos = jnp.concatenate([do[:, (4 * g + r_) * HD:(4 * g + r_ + 1) * HD] for r_ in range(4)], axis=0)
                delta = jnp.sum(dos * og, axis=-1, keepdims=True)
                dp = _mm_nt(dos.astype(BF16), vw)
                ds = pn * (dp - delta)
                dsk = ps * delta
                lane = lax.broadcasted_iota(jnp.int32, (1, 128), 1)
                for r_ in range(4):
                    h = 4 * g + r_
                    dbias_ref[h] += ds[r_ * BLK:(r_ + 1) * BLK]
                    dsink_ref[0:1, :] -= jnp.where(lane == h, jnp.sum(dsk[r_ * BLK:(r_ + 1) * BLK]), 0.0)
                dsb = ds.astype(BF16)
                dq_parts.append(_mm(dsb, kw) * 0.125)
                dk_parts.append(_mm_tn(dsb, qs))
                dv_parts.append(_mm_tn(pn.astype(BF16), dos.astype(BF16)))
            dq_ref[...] = _unstack_heads(dq_parts).astype(BF16)
            dkvw = jnp.concatenate(dk_parts + dv_parts, axis=1)
            dkv_ref[...] = (carry_ref[...] + dkvw[0:BLK]).astype(BF16)
            carry_ref[...] = dkvw[BLK:2 * BLK]

        @pl.when(n == nb)
        def _():
            dkv_ref[...] = carry_ref[...].astype(BF16)

    last = nb - 1
    return _pcall(
        body, [q, kv, kv, dya, bias, sinks, nw], name="attn_bwd", grid=(nb + 1,),
        out_shape=[jax.ShapeDtypeStruct((s, QW), BF16), jax.ShapeDtypeStruct((s, 2 * KVW), BF16),
                   jax.ShapeDtypeStruct((NH, BLK, 2 * BLK), F32), jax.ShapeDtypeStruct((NH, 128), F32),
                   jax.ShapeDtypeStruct((8, QW), F32)],
        in_specs=[pl.BlockSpec((BLK, QW), lambda n: (jnp.minimum(n, last), 0)),
                  pl.BlockSpec((BLK, 2 * KVW), lambda n: (jnp.clip(n - 1, 0, last), 0)),
                  pl.BlockSpec((BLK, 2 * KVW), lambda n: (jnp.minimum(n, last), 0)),
                  pl.BlockSpec((BLK, QW), lambda n: (jnp.minimum(n, last), 0)),
                  _row((NH, BLK, 2 * BLK)),
                  pl.BlockSpec(memory_space=pltpu.SMEM),
                  _row((1, QW))],
        out_specs=[pl.BlockSpec((BLK, QW), lambda n: (jnp.minimum(n, last), 0)),
                   pl.BlockSpec((BLK, 2 * KVW), lambda n: (jnp.maximum(n - 1, 0), 0)),
                   _row((NH, BLK, 2 * BLK)), _row((NH, 128)), _row((8, QW))],
        scratch_shapes=[pltpu.VMEM((BLK, 2 * KVW), F32)], carry=carry)


def _rel_bias_grad(dbias, bucket):
    def body(db_ref, bk_ref, o_ref):
        bk = bk_ref[...]
        lane = lax.broadcasted_iota(jnp.int32, (1, 128), 1)
        for b in range(NBUCKET):
            sel = bk == b
            row = jnp.zeros((1, 128), F32)
            for h in range(NH):
                row = row + jnp.where(lane == h, jnp.sum(jnp.where(sel, db_ref[h], 0.0)), 0.0)
            o_ref[b:b + 1, :] = row

    return pl.pallas_call(
        body, name="rel_bias_grad",
        out_shape=jax.ShapeDtypeStruct((NBUCKET, 128), F32),
    )(dbias, bucket)


def _ssd_consts():
    head_of_lane = np.arange(SW) // HD
    expand = (np.arange(128)[:, None] == head_of_lane[None, :]).astype(np.float32)
    tril = np.tril(np.ones((BLK, BLK), np.float32))
    return (jnp.asarray(expand, BF16), jnp.asarray(expand.T.copy(), BF16), jnp.asarray(tril, BF16),
            jnp.asarray(tril.T.copy(), BF16))


def _ssd_chunk_fwd(c, z_ref, xc_ref, xp_ref, dtr_ref, cw_ref, cb_ref, dtb_ref, a_ref, dk_ref, ex_ref, tril_ref, h_in):
    halo = jnp.where(c == 0, 0.0, xp_ref[BLK - 8:BLK, :])
    ext = jnp.concatenate([halo, xc_ref[...]], axis=0)
    cw = cw_ref[...]
    taps = [ext[8:8 + BLK] if k == CK - 1 else pltpu.roll(ext, CK - 1 - k, 0)[8:8 + BLK] for k in range(CK)]
    pre = cb_ref[...] + sum(cw[k:k + 1, :] * taps[k] for k in range(CK))
    sp = _sig(pre)
    xbc = pre * sp
    xs, bm, cm = xbc[:, 0:SW], xbc[:, SW:SW + 2 * NST], xbc[:, SW + 2 * NST:]
    dtin = dtr_ref[...] + dtb_ref[...]
    dt = jnp.maximum(dtin, 0.0) + jnp.log1p(jnp.exp(-jnp.abs(dtin)))
    av = a_ref[...]
    cs = _sel_l(tril_ref[...], dt * av)
    cst = cs.T
    ex = ex_ref[...]
    dtx = _sel_r(dt, ex)
    csx = _sel_r(cs, ex)
    xdt = xs * dtx
    csl = csx[BLK - 1:BLK, :]
    decx = jnp.exp(csl - csx)
    ecsx = jnp.exp(csx)
    ecl = jnp.exp(csl)
    causal = tril_ref[...].astype(F32) > 0.5
    ydiag, yoff, cbs, lms = [], [], [], []
    for g in range(2):
        bg = bm[:, g * NST:(g + 1) * NST].astype(BF16)
        cg = cm[:, g * NST:(g + 1) * NST].astype(BF16)
        cb = _mm_nt(cg, bg)
        cbs.append(cb)
        yoff.append(_mm(cg, h_in[:, g * 256:(g + 1) * 256].astype(BF16)))
        for r in range(4):
            h = 4 * g + r
            seg = cs[:, h:h + 1] - cst[h:h + 1, :]
            lm = jnp.where(causal, jnp.exp(jnp.minimum(seg, 0.0)), 0.0)
            lms.append(lm)
            ydiag.append(_mm((cb * lm).astype(BF16), xdt[:, h * HD:(h + 1) * HD].astype(BF16)))
    yoff = jnp.concatenate(yoff, axis=1) * ecsx
    y = jnp.concatenate(ydiag, axis=1) + yoff + dk_ref[...] * xs
    return dict(ext=ext, taps=taps, pre=pre, sp=sp, xs=xs, bm=bm, cm=cm, dtin=dtin, dt=dt, av=av, cs=cs, cst=cst,
                dtx=dtx, csx=csx, xdt=xdt, decx=decx, ecsx=ecsx, ecl=ecl, causal=causal, cbs=cbs, lms=lms,
                yoff=yoff, y=y)


def _group_mean(t):
    m0 = jnp.mean(t[:, 0:256], axis=-1, keepdims=True)
    m1 = jnp.mean(t[:, 256:512], axis=-1, keepdims=True)
    return jnp.concatenate([jnp.broadcast_to(m0, (t.shape[0], 256)), jnp.broadcast_to(m1, (t.shape[0], 256))], axis=1)


def _ssd_specs(nc, rev):
    def cur(w):
        return pl.BlockSpec((BLK, w), (lambda i: (nc - 1 - i, 0)) if rev else (lambda i: (i, 0)))
    prev = pl.BlockSpec((BLK, XBCW), (lambda i: (jnp.maximum(nc - 2 - i, 0), 0)) if rev
                        else (lambda i: (jnp.maximum(i - 1, 0), 0)))
    return cur, prev


def _ssd_fwd(z, xbc, dtr, cw, cb, dtb, av, dk, nw, carry=None):
    s = z.shape[0]
    nc = s // BLK
    ex, _, tril, _ = _ssd_consts()

    def body(z_ref, xc_ref, xp_ref, dtr_ref, cw_ref, cb_ref, dtb_ref, a_ref, dk_ref, nw_ref, ex_ref, tril_ref,
             y_ref, hs_ref, h_ref):
        c = pl.program_id(0)

        @pl.when(c == 0)
        def _():
            h_ref[...] = jnp.zeros_like(h_ref)

        h_in = h_ref[...]
        hs_ref[0] = h_in
        f = _ssd_chunk_fwd(c, z_ref, xc_ref, xp_ref, dtr_ref, cw_ref, cb_ref, dtb_ref, a_ref, dk_ref, ex_ref,
                           tril_ref, h_in)
        dx = (f["decx"] * f["xdt"]).astype(BF16)
        st = [_mm_tn(f["bm"][:, g * NST:(g + 1) * NST].astype(BF16), dx[:, g * 256:(g + 1) * 256]) for g in range(2)]
        h_ref[...] = h_in * f["ecl"] + jnp.concatenate(st, axis=1)
        zv = z_ref[...]
        t = f["y"] * (zv * _sig(zv))
        r = lax.rsqrt(_group_mean(t * t) + EPS)
        y_ref[...] = (t * r * nw_ref[...]).astype(BF16)

    cur, prev = _ssd_specs(nc, False)
    return _pcall(
        body, [z, xbc, xbc, dtr, cw, cb, dtb, av, dk, nw, ex, tril], name="ssd_fwd", grid=(nc,),
        out_shape=[jax.ShapeDtypeStruct((s, SW), BF16), jax.ShapeDtypeStruct((nc, NST, SW), F32)],
        in_specs=[cur(SW), cur(XBCW), prev, cur(128), _row((8, XBCW)), _row((1, XBCW)), _row((1, 128)),
                  _row((1, 128)), _row((1, SW)), _row((1, SW)), _row((128, SW)), _row((BLK, BLK))],
        out_specs=[cur(SW), pl.BlockSpec((1, NST, SW), lambda i: (i, 0, 0))],
        scratch_shapes=[pltpu.VMEM((NST, SW), F32)], carry=carry)


def _ssd_bwd(z, xbc, dtr, dys, hs, cw, cb, dtb, av, dk, nw, carry=None):
    s = z.shape[0]
    nc = s // BLK
    ex, ext_t, tril, triu = _ssd_consts()

    def body(z_ref, xc_ref, xp_ref, dtr_ref, dy_ref, hs_ref, cw_ref, cb_ref, dtb_ref, a_ref, dk_ref, nw_ref,
             ex_ref, ext_ref, tril_ref, triu_ref,
             dz_ref, dxbc_ref, ddt_ref, dcw_ref, dcb_ref, dnw_ref, dhd_ref, dh_ref, nxt_ref, dd_ref):
        i = pl.program_id(0)
        c = nc - 1 - i

        @pl.when(i == 0)
        def _():
            dh_ref[...] = jnp.zeros_like(dh_ref)
            nxt_ref[...] = jnp.zeros_like(nxt_ref)
            dd_ref[...] = jnp.zeros_like(dd_ref)
            dcw_ref[...] = jnp.zeros_like(dcw_ref)
            dcb_ref[...] = jnp.zeros_like(dcb_ref)
            dnw_ref[...] = jnp.zeros_like(dnw_ref)
            dhd_ref[...] = jnp.zeros_like(dhd_ref)

        h_in = hs_ref[0]
        f = _ssd_chunk_fwd(c, z_ref, xc_ref, xp_ref, dtr_ref, cw_ref, cb_ref, dtb_ref, a_ref, dk_ref, ex_ref,
                           tril_ref, h_in)
        xs, xdt, decx, ecsx, ecl, dtx = f["xs"], f["xdt"], f["decx"], f["ecsx"], f["ecl"], f["dtx"]
        cs, cst, causal = f["cs"], f["cst"], f["causal"]
        causal_t = triu_ref[...].astype(F32) > 0.5

        zv = z_ref[...]
        sz = _sig(zv)
        gz = zv * sz
        t = f["y"] * gz
        r = lax.rsqrt(_group_mean(t * t) + EPS)
        tn_ = t * r
        dyn = dy_ref[...]
        dnw_ref[0:1, :] += jnp.sum(dyn * tn_, axis=0, keepdims=True)
        u = dyn * nw_ref[...]
        dt_ = r * u - tn_ * (r * _group_mean(u * tn_))
        dy = dt_ * gz
        dz_ref[...] = (dt_ * f["y"] * (sz * (1.0 + zv * (1.0 - sz)))).astype(BF16)

        dd_ref[0:1, :] += jnp.sum(dy * xs, axis=0, keepdims=True)
        dxs = dk_ref[...] * dy

        gst = dh_ref[...]
        edy = ecsx * dy
        dxdt, dbs, dcs_, dcsx_parts, dh_new = [], [], [], [], []
        lane = lax.broadcasted_iota(jnp.int32, (1, 128), 1)
        dcs_intra = jnp.zeros((BLK, 128), F32)
        for g in range(2):
            sl = slice(g * 256, (g + 1) * 256)
            bgf, cgf = f["bm"][:, g * NST:(g + 1) * NST], f["cm"][:, g * NST:(g + 1) * NST]
            bg, cg = bgf.astype(BF16), cgf.astype(BF16)
            gg = gst[:, sl].astype(BF16)
            hg = h_in[:, sl].astype(BF16)
            edyg = edy[:, sl].astype(BF16)
            dc = _mm_nt(edyg, hg)
            dh_new.append(gst[:, sl] * ecl[:, sl] + _mm_tn(cg, edyg))
            bgm = _mm(bg, gg)
            dxdt_g = decx[:, sl] * bgm
            dxg = (decx[:, sl] * xdt[:, sl]).astype(BF16)
            db = _mm_nt(dxg, gg)
            qd = bgm * xdt[:, sl] * decx[:, sl]
            last = jnp.sum(qd, axis=0, keepdims=True) + ecl[:, sl] * jnp.sum(gst[:, sl] * h_in[:, sl], axis=0, keepdims=True)
            rowid = lax.broadcasted_iota(jnp.int32, (BLK, 256), 0)
            dcsx_parts.append(f["yoff"][:, sl] * dy[:, sl] - qd + jnp.where(rowid == BLK - 1, last, 0.0))
            cb_ = f["cbs"][g]
            cbt = _mm_nt(bg, cg)
            dcb_ = jnp.zeros((BLK, BLK), F32)
            dcbt = jnp.zeros((BLK, BLK), F32)
            dxd = []
            for r_ in range(4):
                h = 4 * g + r_
                hl = slice(h * HD, (h + 1) * HD)
                lm = f["lms"][h]
                segt = cst[h:h + 1, :] - cs[:, h:h + 1]
                lmt = jnp.where(causal_t, jnp.exp(jnp.minimum(segt, 0.0)), 0.0)
                dyh = dy[:, hl].astype(BF16)
                xdh = xdt[:, hl].astype(BF16)
                dw = _mm_nt(dyh, xdh)
                dwt = _mm_nt(xdh, dyh)
                wt = cbt * lmt
                dxd.append(_mm(wt.astype(BF16), dyh))
                dcb_ = dcb_ + dw * lm
                dcbt = dcbt + dwt * lmt
                col = jnp.sum(dw * (cb_ * lm), axis=-1, keepdims=True) - jnp.sum(dwt * wt, axis=-1, keepdims=True)
                dcs_intra = dcs_intra + jnp.where(lane == h, col, 0.0)
            dxdt.append(dxdt_g + jnp.concatenate(dxd, axis=1))
            dcs_.append(dc + _mm(dcb_.astype(BF16), bg))
            dbs.append(db + _mm(dcbt.astype(BF16), cg))
        dh_ref[...] = jnp.concatenate(dh_new, axis=1)
        dxdt = jnp.concatenate(dxdt, axis=1)
        dxs = dxs + dxdt * dtx
        ext_t_ = ext_ref[...]
        dcs = dcs_intra + _sel_r(jnp.concatenate(dcsx_parts, axis=1), ext_t_)
        da = _sel_l(triu_ref[...], dcs)
        ddt = da * f["av"] + _sel_r(dxdt * xs, ext_t_)
        dhd_ref[1:2, :] += jnp.sum(da * f["dt"], axis=0, keepdims=True)
        ddtr = ddt * _sig(f["dtin"])
        dhd_ref[0:1, :] += jnp.sum(ddtr, axis=0, keepdims=True)
        ddt_ref[...] = ddtr.astype(BF16)

        sp, pre = f["sp"], f["pre"]
        dact = jnp.concatenate([dxs] + dbs + dcs_, axis=1)
        dpre = dact * (sp * (1.0 + pre * (1.0 - sp)))
        dcb_ref[0:1, :] += jnp.sum(dpre, axis=0, keepdims=True)
        for k in range(CK):
            dcw_ref[k:k + 1, :] += jnp.sum(dpre * f["taps"][k], axis=0, keepdims=True)
        ext2 = jnp.concatenate([dpre, nxt_ref[...]], axis=0)
        cw = cw_ref[...]
        dxr = cw[CK - 1:CK, :] * dpre
        for k in range(CK - 1):
            dxr = dxr + cw[k:k + 1, :] * pltpu.roll(ext2, BLK + 8 - (CK - 1 - k), 0)[0:BLK]
        dxbc_ref[...] = dxr.astype(BF16)
        nxt_ref[...] = dpre[0:8]

        @pl.when(i == nc - 1)
        def _():
            dhd_ref[2:3, :] = _sel_r(dd_ref[...], ext_t_)[0:1, :]

    cur, prev = _ssd_specs(nc, True)
    return _pcall(
        body, [z, xbc, xbc, dtr, dys, hs, cw, cb, dtb, av, dk, nw, ex, ext_t, tril, triu], name="ssd_bwd", grid=(nc,),
        out_shape=[jax.ShapeDtypeStruct((s, SW), BF16), jax.ShapeDtypeStruct((s, XBCW), BF16),
                   jax.ShapeDtypeStruct((s, 128), BF16), jax.ShapeDtypeStruct((8, XBCW), F32),
                   jax.ShapeDtypeStruct((8, XBCW), F32), jax.ShapeDtypeStruct((8, SW), F32),
                   jax.ShapeDtypeStruct((8, 128), F32)],
        in_specs=[cur(SW), cur(XBCW), prev, cur(128), cur(SW),
                  pl.BlockSpec((1, NST, SW), lambda i: (nc - 1 - i, 0, 0)),
                  _row((8, XBCW)), _row((1, XBCW)), _row((1, 128)), _row((1, 128)), _row((1, SW)), _row((1, SW)),
                  _row((128, SW)), _row((SW, 128)), _row((BLK, BLK)), _row((BLK, BLK))],
        out_specs=[cur(SW), cur(XBCW), cur(128), _row((8, XBCW)), _row((8, XBCW)), _row((8, SW)), _row((8, 128))],
        scratch_shapes=[pltpu.VMEM((NST, SW), F32), pltpu.VMEM((8, XBCW), F32), pltpu.VMEM((8, SW), F32)], carry=carry)


def _load_once(i, pairs, sem):
    @pl.when(i == 0)
    def _():
        cps = [pltpu.make_async_copy(src, dst, sem.at[k]) for k, (src, dst) in enumerate(pairs)]
        for cp in cps:
            cp.start()
        for cp in cps:
            cp.wait()


def _mlp_fwd(x, ya, ys, tgt, w_o, w_ga, w_gb, w_dn, gate1, a2, sh2, gate2, fn):
    s = x.shape[0]
    tm = 256

    def body(x_ref, ya_ref, ys_ref, t_ref, wo_hbm, wga_hbm, wgb_hbm, wdn_hbm, g1_ref, a2_ref, s2_ref, g2_ref, fn_ref,
             x1_ref, gu_ref, dx2_ref, loss_ref, dfn_ref, wo, wga, wgb, wdn, sem):
        i = pl.program_id(0)
        _load_once(i, [(wo_hbm, wo), (wga_hbm, wga), (wgb_hbm, wgb), (wdn_hbm, wdn)], sem)

        @pl.when(i == 0)
        def _():
            loss_ref[...] = jnp.zeros_like(loss_ref)
            dfn_ref[...] = jnp.zeros_like(dfn_ref)

        mix = _mm(ya_ref[...], wo[0:QW, :]) + _mm(ys_ref[...], wo[QW:D, :])
        x1 = x_ref[...] + g1_ref[...] * mix
        x1_ref[...] = x1
        r2 = lax.rsqrt(jnp.mean(x1 * x1, axis=-1, keepdims=True) + EPS)
        h2 = (x1 * r2 * a2_ref[...] + s2_ref[...]).astype(BF16)
        ha, hb = h2[:, 0:D // 2], h2[:, D // 2:D]
        gub = jnp.concatenate([(_mm(ha, wga[j]) + _mm(hb, wgb[j])).astype(BF16) for j in range(4)], axis=1)
        gu_ref[...] = gub
        gv, uv = gub[:, 0:DFF].astype(F32), gub[:, DFF:].astype(F32)
        act = (gv * _sig(gv) * uv).astype(BF16)
        x2 = x1 + g2_ref[...] * _mm(act, wdn[...])
        r3 = lax.rsqrt(jnp.mean(x2 * x2, axis=-1, keepdims=True) + EPS)
        xn = x2 * r3
        fnv = fn_ref[...]
        err = xn * fnv - t_ref[...]
        loss_ref[...] += jnp.sum(err * err) * (0.5 / D)
        dy = err * (1.0 / D)
        dfn_ref[0:1, :] += jnp.sum(dy * xn, axis=0, keepdims=True)
        u = dy * fnv
        dx2_ref[...] = r3 * u - xn * (r3 * jnp.mean(u * xn, axis=-1, keepdims=True))

    def tok(w):
        return pl.BlockSpec((tm, w), lambda i: (i, 0))

    hbm = pl.BlockSpec(memory_space=pl.ANY)
    return pl.pallas_call(
        body, name="mlp_fwd", grid=(s // tm,),
        out_shape=[jax.ShapeDtypeStruct((s, D), F32), jax.ShapeDtypeStruct((s, 2 * DFF), BF16),
                   jax.ShapeDtypeStruct((s, D), F32), jax.ShapeDtypeStruct((8, 128), F32),
                   jax.ShapeDtypeStruct((8, D), F32)],
        in_specs=[tok(D), tok(QW), tok(SW), tok(D), hbm, hbm, hbm, hbm,
                  _row((1, D)), _row((1, D)), _row((1, D)), _row((1, D)), _row((1, D))],
        out_specs=[tok(D), tok(2 * DFF), tok(D), _row((8, 128)), _row((8, D))],
        scratch_shapes=[pltpu.VMEM((D, D), BF16), pltpu.VMEM(w_ga.shape, BF16), pltpu.VMEM(w_gb.shape, BF16),
                        pltpu.VMEM((DFF, D), BF16), pltpu.SemaphoreType.DMA((4,))],
        compiler_params=_cp(("arbitrary",)),
    )(x, ya, ys, tgt, w_o, w_ga, w_gb, w_dn, gate1, a2, sh2, gate2, fn)


def _mlp_bwd(x1, gu, dx2, w_o, w_ga, w_gb, w_dn, gate1, a2, sh2, gate2):
    s = x1.shape[0]
    tm = 256
    nj = 2 * DFF // 4

    def body(x1_ref, gu_ref, dx2_ref, wo_hbm, wga_hbm, wgb_hbm, wdn_hbm, g1_ref, a2_ref, s2_ref, g2_ref,
             dx1_ref, dya_ref, dys_ref, act_ref, dgu_ref, h2_ref, dsh_ref, p_ref, wo, wga, wgb, wdn, sem):
        i = pl.program_id(0)
        _load_once(i, [(wo_hbm, wo), (wga_hbm, wga), (wgb_hbm, wgb), (wdn_hbm, wdn)], sem)

        @pl.when(i == 0)
        def _():
            dsh_ref[...] = jnp.zeros_like(dsh_ref)
            p_ref[...] = jnp.zeros_like(p_ref)

        dx2 = dx2_ref[...]
        dact = _mm_nt((dx2 * g2_ref[...]).astype(BF16), wdn[...])
        gub = gu_ref[...]
        gv, uv = gub[:, 0:DFF].astype(F32), gub[:, DFF:].astype(F32)
        sg = _sig(gv)
        sl = gv * sg
        act_ref[...] = (sl * uv).astype(BF16)
        dgu = jnp.concatenate([dact * uv * (sg * (1.0 + gv * (1.0 - sg))), dact * sl], axis=1).astype(BF16)
        dgu_ref[...] = dgu
        dha = sum(_mm_nt(dgu[:, j * nj:(j + 1) * nj], wga[j]) for j in range(4))
        dhb = sum(_mm_nt(dgu[:, j * nj:(j + 1) * nj], wgb[j]) for j in range(4))
        dh = jnp.concatenate([dha, dhb], axis=1)
        x1 = x1_ref[...]
        r2 = lax.rsqrt(jnp.mean(x1 * x1, axis=-1, keepdims=True) + EPS)
        xn = x1 * r2
        a2 = a2_ref[...]
        h2_ref[...] = (xn * a2 + s2_ref[...]).astype(BF16)
        dsh_ref[0:1, :] += jnp.sum(dh, axis=0, keepdims=True)
        p_ref[0:1, :] += jnp.sum(dh * xn, axis=0, keepdims=True)
        u = dh * a2
        dx1 = dx2 + r2 * u - xn * (r2 * jnp.mean(u * xn, axis=-1, keepdims=True))
        dx1_ref[...] = dx1
        dcat = _mm_nt((dx1 * g1_ref[...]).astype(BF16), wo[...])
        dya_ref[...] = dcat[:, 0:QW]
        dys_ref[...] = dcat[:, QW:D]

    def tok(w):
        return pl.BlockSpec((tm, w), lambda i: (i, 0))

    hbm = pl.BlockSpec(memory_space=pl.ANY)
    return pl.pallas_call(
        body, name="mlp_bwd", grid=(s // tm,),
        out_shape=[jax.ShapeDtypeStruct((s, D), F32), jax.ShapeDtypeStruct((s, QW), F32),
                   jax.ShapeDtypeStruct((s, SW), F32), jax.ShapeDtypeStruct((s, DFF), BF16),
                   jax.ShapeDtypeStruct((s, 2 * DFF), BF16), jax.ShapeDtypeStruct((s, D), BF16),
                   jax.ShapeDtypeStruct((8, D), F32), jax.ShapeDtypeStruct((8, D), F32)],
        in_specs=[tok(D), tok(2 * DFF), tok(D), hbm, hbm, hbm, hbm, _row((1, D)), _row((1, D)), _row((1, D)), _row((1, D))],
        out_specs=[tok(D), tok(QW), tok(SW), tok(DFF), tok(2 * DFF), tok(D), _row((8, D)), _row((8, D))],
        scratch_shapes=[pltpu.VMEM((D, D), BF16), pltpu.VMEM(w_ga.shape, BF16), pltpu.VMEM(w_gb.shape, BF16),
                        pltpu.VMEM((DFF, D), BF16), pltpu.SemaphoreType.DMA((4,))],
        compiler_params=_cp(("arbitrary",)),
    )(x1, gu, dx2, w_o, w_ga, w_gb, w_dn, gate1, a2, sh2, gate2)


def _wgrad(name, a, b, tn, gate=None, w=None, stacked=False, carry=None):
    s, m = a.shape
    n = b.shape[1]
    tk = min(1024, s)
    nk = s // tk

    def body(*refs):
        if gate is None:
            a_ref, b_ref, o_ref = refs
        else:
            a_ref, b_ref, g_ref, w_ref, o_ref, dg_ref = refs
        k = pl.program_id(1)

        @pl.when(k == 0)
        def _():
            o_ref[...] = jnp.zeros_like(o_ref)

        o_ref[...] += _mm_tn(a_ref[...], b_ref[...].astype(BF16))

        if gate is not None:
            @pl.when(k == nk - 1)
            def _():
                acc = o_ref[...]
                dg_ref[...] = jnp.zeros_like(dg_ref)
                dg_ref[0:1, :] = jnp.sum(acc * w_ref[...].astype(F32), axis=0, keepdims=True)
                o_ref[...] = acc * g_ref[...]

    in_specs = [pl.BlockSpec((tk, m), lambda j, k: (k, 0)), pl.BlockSpec((tk, tn), lambda j, k: (k, j))]
    if stacked:
        out_shape = [jax.ShapeDtypeStruct((n // tn, m, tn), F32)]
        out_specs = [pl.BlockSpec((None, m, tn), lambda j, k: (j, 0, 0))]
    else:
        out_shape = [jax.ShapeDtypeStruct((m, n), F32)]
        out_specs = [pl.BlockSpec((m, tn), lambda j, k: (0, j))]
    args = [a, b]
    if gate is not None:
        in_specs += [pl.BlockSpec((1, tn), lambda j, k: (0, j)), pl.BlockSpec((m, tn), lambda j, k: (0, j))]
        out_shape.append(jax.ShapeDtypeStruct((8, n), F32))
        out_specs.append(pl.BlockSpec((8, tn), lambda j, k: (0, j)))
        args += [gate, w]
    return _pcall(body, args, name=name, grid=(n // tn, nk), out_shape=out_shape, in_specs=in_specs,
                  out_specs=out_specs, carry=carry)


def _adamw_rows(name, w, g, m, v, tr):
    r = w.shape[0]

    def body(w_ref, g_ref, m_ref, v_ref, d_ref, mo_ref, vo_ref):
        d_ref[...], mo_ref[...], vo_ref[...] = _adamw(w_ref[...], g_ref[...], m_ref[...], v_ref[...])

    spec = pl.BlockSpec((tr, D), lambda i: (i, 0))
    return pl.pallas_call(body, name=name, grid=(r // tr,), out_shape=[jax.ShapeDtypeStruct(w.shape, F32)] * 3,
                          in_specs=[spec] * 4, out_specs=[spec] * 3, compiler_params=_cp(("parallel",)))(w, g, m, v)


def _sum8(parts):
    r = parts.shape[1]

    def body(p_ref, o_ref):
        acc = p_ref[0]
        for b in range(1, 8):
            acc = acc + p_ref[b]
        o_ref[...] = acc

    return pl.pallas_call(body, name="sum8", out_shape=jax.ShapeDtypeStruct((r, D), F32))(parts)


def _add_half(name, g, got, where):
    rr, cc = g.shape[2:]

    def body(w_ref, g_ref, r_ref, o_ref, own_ref):
        s = g_ref[...] + r_ref[...]
        o_ref[...] = s.astype(BF16)

        @pl.when(pl.program_id(0) == w_ref[1])
        def _():
            own_ref[...] = s

    spec = pl.BlockSpec((None, rr, cc), lambda i, w_ref: (i, 0, 0))
    return _pcall(body, [where, g, got], name=name, grid=(4,), nprefetch=1,
                  out_shape=[jax.ShapeDtypeStruct(got.shape, BF16), jax.ShapeDtypeStruct((rr, cc), F32)],
                  in_specs=[pl.BlockSpec((None, None, rr, cc), lambda i, w_ref: (i, w_ref[0], 0, 0)), spec],
                  out_specs=[spec, pl.BlockSpec((rr, cc), lambda i, w_ref: (0, 0))])


def _add_chips(name, own, got):
    rr, cc = own.shape
    tr = rr // 2

    def body(s_ref, r_ref, o_ref):
        o_ref[...] = ((s_ref[...] + r_ref[0].astype(F32)) + r_ref[1].astype(F32)) + r_ref[2].astype(F32)

    spec = pl.BlockSpec((tr, cc), lambda i: (i, 0))
    return _pcall(body, [own, got], name=name, grid=(2,), out_shape=[jax.ShapeDtypeStruct((rr, cc), F32)],
                  in_specs=[spec, pl.BlockSpec((3, tr, cc), lambda i: (0, i, 0))], out_specs=[spec])[0]


def _adamw_halves(name, mine, got, w, m, v, where):
    rr, cc = mine.shape
    tr = rr // 2

    def body(w_ref_, t_ref, r_ref, w_ref, m_ref, v_ref, g_ref, d_ref, mo_ref, vo_ref):
        g = jnp.where(pl.program_id(0) == w_ref_[0], t_ref[...], r_ref[...])
        g_ref[...] = g
        d_ref[...], mo_ref[...], vo_ref[...] = _adamw(w_ref[...], g, m_ref[...], v_ref[...])

    half = pl.BlockSpec((tr, cc), lambda h, i, w_ref_: (i, 0))
    full = pl.BlockSpec((None, tr, cc), lambda h, i, w_ref_: (0, 2 * h + i, 0))
    return _pcall(body, [where, mine, got, w, m, v], name=name, grid=(2, 2), nprefetch=1,
                  out_shape=[jax.ShapeDtypeStruct(w.shape, F32)] * 4,
                  in_specs=[half, half, full, full, full], out_specs=[full] * 4)


def _bias_table(rel_bias, bucket, mask):
    def body(rb_ref, bk_ref, mk_ref, o_ref):
        bk = bk_ref[...]
        valid = mk_ref[...] > 0
        for h in range(NH):
            acc = jnp.zeros((BLK, 2 * BLK), F32)
            for b in range(NBUCKET):
                acc = jnp.where(bk == b, rb_ref[b, h], acc)
            o_ref[h] = jnp.where(valid, acc, NEG)

    vmem = pl.BlockSpec(memory_space=pltpu.VMEM)
    return pl.pallas_call(
        body, name="bias_table", out_shape=jax.ShapeDtypeStruct((NH, BLK, 2 * BLK), F32),
        in_specs=[pl.BlockSpec(memory_space=pltpu.SMEM), vmem, vmem], out_specs=vmem,
    )(rel_bias, bucket, mask)


def _pack_small(dsh1, p1, dsh2, p2, dg1a, dg1b, dg2, norm1, norm2, scale1, scale2, dcw, dcb, dfn,
                dnw_attn, dnw_ssm, dhd, av, dsink, drel):
    def body(dsh1_ref, p1_ref, dsh2_ref, p2_ref, dg1a_ref, dg1b_ref, dg2_ref, n1_ref, n2_ref, s1_ref, s2_ref,
             dcw_ref, dcb_ref, dfn_ref, da_ref, ds_ref, dhd_ref, av_ref, dsink_ref, drel_ref, o_ref):
        o_ref[...] = jnp.zeros_like(o_ref)
        p1v, p2v = p1_ref[0:1, :], p2_ref[0:1, :]
        o_ref[0:1, :] = dsh1_ref[0:1, :]
        o_ref[1:2, :] = p1v * n1_ref[...]
        o_ref[2:3, :] = dg1a_ref[0:1, :] + dg1b_ref[0:1, :]
        o_ref[3:4, :] = dsh2_ref[0:1, :]
        o_ref[4:5, :] = p2v * n2_ref[...]
        o_ref[5:6, :] = dg2_ref[0:1, :]
        o_ref[6:7, :] = p1v * (1.0 + s1_ref[...])
        o_ref[7:11, :] = dcw_ref[0:4, :]
        o_ref[11:12, :] = dcb_ref[0:1, :]
        o_ref[12:13, :] = p2v * (1.0 + s2_ref[...])
        o_ref[13:14, :] = dfn_ref[0:1, :]
        o_ref[14:15, 0:QW] = da_ref[0:1, :]
        o_ref[15:16, 0:SW] = ds_ref[0:1, :]
        o_ref[16:17, 0:128] = dhd_ref[0:1, :]
        o_ref[17:18, 0:128] = dhd_ref[1:2, :] * av_ref[...]
        o_ref[18:19, 0:128] = dhd_ref[2:3, :]
        o_ref[19:20, 0:128] = dsink_ref[0:1, :]
        o_ref[24:56, 0:128] = drel_ref[...]

    return pl.pallas_call(body, name="pack_small", out_shape=jax.ShapeDtypeStruct((56, D), F32))(
        dsh1, p1, dsh2, p2, dg1a, dg1b, dg2, norm1, norm2, scale1, scale2, dcw, dcb, dfn,
        dnw_attn, dnw_ssm, dhd, av, dsink, drel)


def _pad_row(a, rows=1):
    return jnp.pad(a.reshape(rows, -1), ((0, 0), (0, D - a.size // rows)))


def kernel(x, c, ada_w, ada_b, norm1, w_in, conv_w, conv_b, dt_bias, A_log, D_skip, sinks, attn_out_norm, ssm_out_norm, w_o, norm2, w_gate_up, w_down, rel_bias, final_norm, loss_target, m_ada_w, m_ada_b, m_norm1, m_w_in, m_conv_w, m_conv_b, m_dt_bias, m_A_log, m_D_skip, m_sinks, m_attn_out_norm, m_ssm_out_norm, m_w_o, m_norm2, m_w_gate_up, m_w_down, m_rel_bias, m_final_norm, v_ada_w, v_ada_b, v_norm1, v_w_in, v_conv_w, v_conv_b, v_dt_bias, v_A_log, v_D_skip, v_sinks, v_attn_out_norm, v_ssm_out_norm, v_w_o, v_norm2, v_w_gate_up, v_w_down, v_rel_bias, v_final_norm):
    xi, yi, ci = lax.axis_index("x"), lax.axis_index("y"), lax.axis_index("c")
    chip = 2 * xi + yi
    me = 4 * xi + 2 * yi + ci
    where = jnp.stack([ci, chip]).astype(jnp.int32)
    xs2, tgt = x[0], loss_target[0]

    first = jnp.concatenate([c, _pad_row(conv_w[0], CK), jnp.zeros((3, D), F32)], axis=0)
    w_in_b, w_o_b, w_dn_b = w_in[0].astype(BF16), w_o[0].astype(BF16), w_down[0].astype(BF16)
    w_gu_b = w_gate_up[0].astype(BF16)
    first_all, w_in_g = _exchange("gather_first", _merge(_gather8_carry(first), _gather_chips_carry([w_in_b])))
    c_all = first_all[:, 0, :]
    cw_full = jnp.concatenate([first_all[2 * j, 1:1 + CK, 0:256] for j in range(4)], axis=1)
    w_in_f = jnp.pad(jnp.transpose(w_in_g, (1, 0, 2)).reshape(D, IN_W), ((0, 0), (0, PROJ_W - IN_W)))

    ncol = ada_w.shape[2]
    mod_cols = _ada_fwd(c_all, ada_w[0], lax.dynamic_slice(ada_b, (0, chip * ncol), (1, ncol)))
    mod_all = _exchange("gather_mod", _gather_chips_carry([mod_cols]))[0]
    mod = lax.dynamic_slice(jnp.transpose(mod_all, (1, 0, 2)).reshape(8, 4 * ncol), (me, 0), (1, 4 * ncol))
    shift1, scale1, gate1, shift2, scale2, gate2 = [mod[:, j * D:(j + 1) * D] for j in range(6)]
    a1 = norm1 * (1.0 + scale1)
    a2 = norm2 * (1.0 + scale2)

    hdn = DFF // 8
    q, kv, z, xbc, dtr, w_o_g, w_dna_g = _in_proj_fwd(xs2, a1, shift1, w_in_f,
                                                      carry=_gather_chips_carry([w_o_b, w_dn_b[0:hdn]]))
    w_o_f = w_o_g.reshape(D, D)
    bucket, mask = _attn_geometry()
    bucket = jnp.asarray(bucket)
    bias = _bias_table(rel_bias, bucket, jnp.asarray(mask.astype(np.int32)))
    sinks1 = sinks[0]
    ya, w_ga_g = _attn_fwd(q, kv, bias, sinks1, attn_out_norm, carry=_gather_chips_carry([w_gu_b[0:D // 2]]))
    cw8 = jnp.concatenate([cw_full, jnp.zeros((4, XBCW), F32)], axis=0)
    dtb = _pad_row(dt_bias)[:, 0:128]
    av = _pad_row(-jnp.exp(A_log))[:, 0:128]
    dk = jnp.repeat(D_skip, HD, axis=1)
    ys, hs, w_gb_g, w_dnb_g = _ssd_fwd(z, xbc, dtr, cw8, conv_b, dtb, av, dk, ssm_out_norm,
                                       carry=_gather_chips_carry([w_gu_b[D // 2:D], w_dn_b[hdn:2 * hdn]]))
    w_dn_f = jnp.stack([w_dna_g, w_dnb_g], axis=1).reshape(DFF, D)
    fn = final_norm[None, :]
    x1, gu, dx2, loss_acc, dfn = _mlp_fwd(xs2, ya, ys, tgt, w_o_f, w_ga_g, w_gb_g, w_dn_f, gate1, a2, shift2, gate2, fn)
    loss = lax.psum(loss_acc[0, 0], ("x", "y", "c"))

    def to_sibling(p):
        return _Carry([p], [jax.ShapeDtypeStruct((4,) + p.shape[2:], F32)],
                      lambda x_, y_, c_: [(_SIBLING, 0, (j, 1 - c_), 0, j) for j in range(4)])

    def to_chips(s4):
        return _Carry([s4], [jax.ShapeDtypeStruct((3,) + s4.shape[1:], s4.dtype)],
                      lambda x_, y_, c_: [(f, 0, jnp.bitwise_xor(2 * x_ + y_, k + 1), 0, k) for k, f in enumerate(_CHIPS3)])

    def back(t):
        return _Carry([t[None]], [jax.ShapeDtypeStruct((1,) + t.shape, F32)], lambda x_, y_, c_: [(_SIBLING, 0, 0, 0, 0)])

    dx1, dya, dys, act, dgu, h2, dsh2, p2 = _mlp_bwd(x1, gu, dx2, w_o_f, w_ga_g, w_gb_g, w_dn_f, gate1, a2, shift2, gate2)
    p_gu = _wgrad("wgrad_gate_up", h2, dgu, 2 * DFF // 4, stacked=True)[0].reshape(4, 2, D // 2, 2 * DFF // 4)
    g_dn, dg2, got1_gu = _wgrad("wgrad_down", act, dx2, D // 2, gate2, w_dn_f, carry=to_sibling(p_gu))
    p_dn = g_dn.reshape(4, 2, DFF // 8, D)
    s4_gu, own_gu = _add_half("rs_add_half_gu", p_gu, got1_gu, where)
    dq, dkv, dbias, dsink, dnw_attn, got2_gu, got1_dn = _attn_bwd(
        q, kv, dya, bias, sinks1, attn_out_norm, carry=_merge(to_chips(s4_gu), to_sibling(p_dn)))
    drel = _rel_bias_grad(dbias, bucket)
    mine_gu = _add_chips("rs_add_chips_gu", own_gu, got2_gu)
    s4_dn, own_dn = _add_half("rs_add_half_dn", p_dn, got1_dn, where)
    dz, dxbc, ddt, dcw, dcb, dnw_ssm, dhd, got2_dn, got3_gu = _ssd_bwd(
        z, xbc, dtr, dys, hs, cw8, conv_b, dtb, av, dk, ssm_out_norm, carry=_merge(to_chips(s4_dn), back(mine_gu)))
    mine_dn = _add_chips("rs_add_chips_dn", own_dn, got2_dn)
    grad_x, dproj, h1, dsh1, p1 = _in_proj_bwd(xs2, dx1, a1, shift1, w_in_f, dq, dkv, dz, dxbc, ddt)
    g_in, got3_dn = _wgrad("wgrad_in", h1, dproj, PROJ_W, carry=back(mine_dn))
    p_in = jnp.transpose(g_in[:, 0:IN_W].reshape(D, 4, IN_W // 4), (1, 0, 2)).reshape(4, 2, D // 2, IN_W // 4)
    g_oa, dg1a, got1_in = _wgrad("wgrad_o_attn", ya, dx1, D, gate1, w_o_f[0:QW], carry=to_sibling(p_in))
    s4_in, own_in = _add_half("rs_add_half_in", p_in, got1_in, where)
    g_os, dg1b, got2_in = _wgrad("wgrad_o_ssm", ys, dx1, D, gate1, w_o_f[QW:D], carry=to_chips(s4_in))
    mine_in = _add_chips("rs_add_chips_in", own_in, got2_in)
    p_o = jnp.concatenate([g_oa, g_os], axis=0).reshape(4, 2, D // 8, D)

    small = _pack_small(dsh1, p1, dsh2, p2, dg1a, dg1b, dg2, norm1, norm2, scale1, scale2, dcw, dcb, dfn,
                        dnw_attn, dnw_ssm, dhd, av, dsink, drel)
    small_all, got1_o, got3_in = _exchange(
        "gather_small", _merge(_gather8_carry(small), to_sibling(p_o), back(mine_in)))
    gsum = _sum8(small_all)
    s4_o, own_o = _add_half("rs_add_half_o", p_o, got1_o, where)
    mine_o = _add_chips("rs_add_chips_o", own_o, _exchange("rs_chips_o", to_chips(s4_o))[0])
    got3_o = _exchange("rs_back_o", back(mine_o))[0]

    def conv_rows(a):
        return lax.dynamic_update_slice(jnp.zeros((CK, D), F32), a[0], (0, chip * 256))

    def pack(p):
        rows = [jnp.zeros((6, D), F32), p["norm1"], conv_rows(p["conv_w"]), p["conv_b"], p["norm2"],
                p["final_norm"][None, :], _pad_row(p["attn_out_norm"]), _pad_row(p["ssm_out_norm"]),
                _pad_row(p["dt_bias"]), _pad_row(p["A_log"]), _pad_row(p["D_skip"]), _pad_row(p["sinks"]),
                jnp.zeros((4, D), F32), _pad_row(p["rel_bias"], NBUCKET)]
        return jnp.concatenate(rows, axis=0)

    names = ["norm1", "conv_w", "conv_b", "norm2", "final_norm", "attn_out_norm", "ssm_out_norm", "dt_bias",
             "A_log", "D_skip", "sinks", "rel_bias"]
    ws = dict(zip(names, [norm1, conv_w, conv_b, norm2, final_norm, attn_out_norm, ssm_out_norm, dt_bias, A_log,
                          D_skip, sinks, rel_bias]))
    ms = dict(zip(names, [m_norm1, m_conv_w, m_conv_b, m_norm2, m_final_norm, m_attn_out_norm, m_ssm_out_norm,
                          m_dt_bias, m_A_log, m_D_skip, m_sinks, m_rel_bias]))
    vs = dict(zip(names, [v_norm1, v_conv_w, v_conv_b, v_norm2, v_final_norm, v_attn_out_norm, v_ssm_out_norm,
                          v_dt_bias, v_A_log, v_D_skip, v_sinks, v_rel_bias]))
    wp = pack(ws).at[0:6].set(ada_b.reshape(6, D))
    mp = pack(ms).at[0:6].set(m_ada_b.reshape(6, D))
    vp = pack(vs).at[0:6].set(v_ada_b.reshape(6, D))
    sd, sm, sv = _adamw_rows("adamw_small", wp, gsum, mp, vp, 56)

    def unpack(p):
        conv = lax.dynamic_slice(p[7:11], (0, chip * 256), (CK, 256))[None]
        return dict(ada_b=p[0:6].reshape(1, 6 * D), norm1=p[6:7], conv_w=conv, conv_b=p[11:12], norm2=p[12:13],
                    final_norm=p[13], attn_out_norm=p[14:15, 0:QW], ssm_out_norm=p[15:16, 0:SW],
                    dt_bias=p[16:17, 0:NH], A_log=p[17:18, 0:NH], D_skip=p[18:19, 0:NH], sinks=p[19:20, 0:NH],
                    rel_bias=p[24:56, 0:NH])

    small_out = [unpack(p) for p in (gsum, sd, sm, sv)]

    dmod_all = small_all[:, 0:6, :].reshape(8, 6 * D)
    dmod_loc = lax.dynamic_slice(dmod_all, (0, chip * ncol), (8, ncol))
    ada_out = _ada_bwd_adamw(c_all.T, dmod_loc, ada_w[0], m_ada_w[0], v_ada_w[0])

    big_gu = _adamw_halves("adamw_gate_up", mine_gu, got3_gu[0], w_gate_up, m_w_gate_up, v_w_gate_up, where)
    big_dn = _adamw_halves("adamw_down", mine_dn, got3_dn[0], w_down, m_w_down, v_w_down, where)
    big_o = _adamw_halves("adamw_o", mine_o, got3_o[0], w_o, m_w_o, v_w_o, where)
    big_in = _adamw_halves("adamw_in", mine_in, got3_in[0], w_in, m_w_in, v_w_in, where)
    big = [big_in, big_o, big_gu, big_dn]

    order = ["ada_w", "ada_b", "norm1", "w_in", "conv_w", "conv_b", "dt_bias", "A_log", "D_skip", "sinks",
             "attn_out_norm", "ssm_out_norm", "w_o", "norm2", "w_gate_up", "w_down", "rel_bias", "final_norm"]
    bigname = {"w_in": 0, "w_o": 1, "w_gate_up": 2, "w_down": 3}
    res = [loss, grad_x[None]]
    for kind in range(4):
        for nm in order:
            if nm == "ada_w":
                res.append(ada_out[kind][None])
            elif nm in bigname:
                res.append(big[bigname[nm]][kind])
            else:
                res.append(small_out[kind][nm])
    return tuple(res)
```

```python
import numpy as np
import jax
import jax.numpy as jnp
from jax import lax
from jax.experimental import pallas as pl
from jax.experimental.pallas import tpu as pltpu

F32, BF16 = jnp.float32, jnp.bfloat16
HI = lax.Precision.HIGHEST

D = 1024
QW, KVW = 512, 128
NH, HD, NKV = 8, 64, 2
SW = 512
NST = 128
XBCW = 1024
CK = 4
BLK = 128
DFF = 2816
IN_W = 2312
PROJ_W = 2432
EPS = 1e-6
NEG = -1e30
NBUCKET = 32

B1, B2, LR, AEPS, WD, STEP = 0.9, 0.999, 0.001, 1e-08, 0.01, 10

VMEM_LIMIT = 56 * 1024 * 1024

_NT = (((1,), (1,)), ((), ()))
_TN = (((0,), (0,)), ((), ()))


def _mm(a, b):
    return jnp.dot(a, b, preferred_element_type=F32)


def _mm_nt(a, b):
    return lax.dot_general(a, b, _NT, preferred_element_type=F32)


def _mm_tn(a, b):
    return lax.dot_general(a, b, _TN, preferred_element_type=F32)


def _mm_hi(a, b):
    return jnp.dot(a, b, preferred_element_type=F32, precision=HI)


def _split3(x):
    hi = x.astype(BF16)
    r = x - hi.astype(F32)
    mid = r.astype(BF16)
    lo = (r - mid.astype(F32)).astype(BF16)
    return hi, mid, lo


def _sel_r(x, e):
    hi, mid, lo = _split3(x)
    return (_mm(hi, e) + _mm(mid, e)) + _mm(lo, e)


def _sel_l(e, x):
    hi, mid, lo = _split3(x)
    return (_mm(e, hi) + _mm(e, mid)) + _mm(e, lo)


def _sig(x):
    return 1.0 / (1.0 + jnp.exp(-x))


def _cp(sem):
    return pltpu.CompilerParams(dimension_semantics=sem, vmem_limit_bytes=VMEM_LIMIT)


def _row(shape):
    nd = len(shape)
    return pl.BlockSpec(shape, lambda *_: (0,) * nd)


def _adamw(w, g, m, v):
    m = B1 * m + (1.0 - B1) * g
    v = B2 * v + (1.0 - B2) * (g * g)
    m_hat = m / (1.0 - B1 ** STEP)
    v_hat = v / (1.0 - B2 ** STEP)
    delta = -LR * (m_hat / (jnp.sqrt(v_hat) + AEPS) + WD * w)
    return delta, m, v


class _Carry:
    def __init__(self, inps, outs, copies):
        self.inps, self.outs, self.copies = list(inps), list(outs), copies
        self.n = len(copies(0, 0, 0))

    def descriptors(self, in_refs, out_refs, send_sems, recv_sems):
        x, y, c = lax.axis_index("x"), lax.axis_index("y"), lax.axis_index("c")
        out = []
        for j, (flip, a, si, o, di) in enumerate(self.copies(x, y, c)):
            if flip is None:
                out.append(pltpu.make_async_copy(in_refs[a].at[si], out_refs[o].at[di], send_sems.at[j]))
            else:
                fx, fy, fc = flip
                peer = (1 - x if fx else x, 1 - y if fy else y, 1 - c if fc else c)
                out.append(pltpu.make_async_remote_copy(
                    src_ref=in_refs[a].at[si], dst_ref=out_refs[o].at[di],
                    send_sem=send_sems.at[j], recv_sem=recv_sems.at[j],
                    device_id=peer, device_id_type=pl.DeviceIdType.MESH))
        return out


def _pcall(body, args, *, name, grid, in_specs, out_specs, out_shape, scratch_shapes=(), sem=None, nprefetch=0,
           carry=None):
    out_shape, out_specs = list(out_shape), list(out_specs)
    in_specs, scratch_shapes = list(in_specs), list(scratch_shapes)
    nin, nout, nscr = len(in_specs), len(out_shape), len(scratch_shapes)
    run = body
    if carry is not None:
        ncin, ncout = len(carry.inps), len(carry.outs)
        hbm = pl.BlockSpec(memory_space=pl.ANY)

        def run(*refs):
            pre, r = refs[:nprefetch], refs[nprefetch:]
            ins, cins = r[:nin], r[nin:nin + ncin]
            r = r[nin + ncin:]
            outs, couts = r[:nout], r[nout:nout + ncout]
            r = r[nout + ncout:]
            scr, (send_sems, recv_sems) = r[:nscr], r[nscr:]
            first = pl.program_id(0) == 0
            last = pl.program_id(0) == grid[0] - 1
            for ax in range(1, len(grid)):
                first = jnp.logical_and(first, pl.program_id(ax) == 0)
                last = jnp.logical_and(last, pl.program_id(ax) == grid[ax] - 1)

            @pl.when(first)
            def _():
                for d in carry.descriptors(cins, couts, send_sems, recv_sems):
                    d.start()

            body(*pre, *ins, *outs, *scr)

            @pl.when(last)
            def _():
                for d in carry.descriptors(cins, couts, send_sems, recv_sems):
                    d.wait()

        in_specs = in_specs + [hbm] * ncin
        out_specs = out_specs + [hbm] * ncout
        out_shape = out_shape + carry.outs
        scratch_shapes = scratch_shapes + [pltpu.SemaphoreType.DMA((carry.n,)), pltpu.SemaphoreType.DMA((carry.n,))]
        args = list(args) + carry.inps
    if sem is None:
        sem = ("arbitrary",) * len(grid)
    if nprefetch:
        kw = dict(grid_spec=pltpu.PrefetchScalarGridSpec(num_scalar_prefetch=nprefetch, grid=grid, in_specs=in_specs,
                                                         out_specs=out_specs, scratch_shapes=scratch_shapes))
    else:
        kw = dict(grid=grid, in_specs=in_specs, out_specs=out_specs, scratch_shapes=scratch_shapes)
    res = pl.pallas_call(run, name=name, out_shape=out_shape, compiler_params=_cp(sem), **kw)(*args)
    return list(res)


def _merge(*carries):
    inps, outs, offs = [], [], []
    for cr in carries:
        offs.append((len(inps), len(outs)))
        inps += cr.inps
        outs += cr.outs

    def copies(x, y, c):
        return [(f, a + io, si, o + oo, di) for cr, (io, oo) in zip(carries, offs) for f, a, si, o, di in cr.copies(x, y, c)]

    return _Carry(inps, outs, copies)


def _exchange(name, carry):
    return _pcall(lambda: None, [], name=name, grid=(1,), in_specs=[], out_specs=[], out_shape=[], carry=carry)


_ALL7 = [(f >> 2 & 1, f >> 1 & 1, f & 1) for f in range(1, 8)]
_CHIPS3 = [(0, 1, 0), (1, 0, 0), (1, 1, 0)]
_SIBLING = (0, 0, 1)


def _gather8_carry(blk):
    def copies(x, y, c):
        me = 4 * x + 2 * y + c
        return [(None, 0, 0, 0, me)] + [(f, 0, 0, 0, me) for f in _ALL7]

    return _Carry([blk[None]], [jax.ShapeDtypeStruct((8,) + blk.shape, blk.dtype)], copies)


def _gather_chips_carry(blks):
    def copies(x, y, c):
        chip = 2 * x + y
        return [(f, a, 0, a, chip) for a in range(len(blks)) for f in [None] + _CHIPS3]

    return _Carry([b[None] for b in blks], [jax.ShapeDtypeStruct((4,) + b.shape, b.dtype) for b in blks], copies)


def _ada_fwd(c_all, w_loc, b_loc):
    n = w_loc.shape[1]
    tn = 512

    def body(c_ref, w_ref, b_ref, o_ref):
        cv = c_ref[...]
        cond = cv * _sig(cv)
        o_ref[...] = _mm_hi(cond, w_ref[...]) + b_ref[...]

    return pl.pallas_call(
        body, name="ada_fwd", grid=(n // tn,),
        out_shape=jax.ShapeDtypeStruct((8, n), F32),
        in_specs=[_row((8, D)), pl.BlockSpec((D, tn), lambda j: (0, j)), pl.BlockSpec((1, tn), lambda j: (0, j))],
        out_specs=pl.BlockSpec((8, tn), lambda j: (0, j)),
        compiler_params=_cp(("parallel",)),
    )(c_all, w_loc, b_loc)


def _ada_bwd_adamw(c_all_t, dmod_loc, w, m, v, carry=None):
    n = w.shape[1]
    tn = 512

    def body(ct_ref, dm_ref, w_ref, m_ref, v_ref, g_ref, d_ref, mo_ref, vo_ref):
        ct = ct_ref[...]
        cond = ct * _sig(ct)
        dm = dm_ref[...]
        g = cond[:, 0:1] * dm[0:1, :]
        for b in range(1, 8):
            g = g + cond[:, b:b + 1] * dm[b:b + 1, :]
        g_ref[...] = g
        d_ref[...], mo_ref[...], vo_ref[...] = _adamw(w_ref[...], g, m_ref[...], v_ref[...])

    wspec = pl.BlockSpec((D, tn), lambda j: (0, j))
    return _pcall(
        body, [c_all_t, dmod_loc, w, m, v], name="ada_bwd_adamw", grid=(n // tn,),
        out_shape=[jax.ShapeDtypeStruct((D, n), F32)] * 4,
        in_specs=[_row((D, 8)), pl.BlockSpec((8, tn), lambda j: (0, j)), wspec, wspec, wspec],
        out_specs=[wspec] * 4, carry=carry)


def _in_proj_fwd(x, a1, sh1, w_in, carry=None):
    s = x.shape[0]
    tm = 512

    def body(x_ref, a_ref, s_ref, w_ref, q_ref, kv_ref, z_ref, xbc_ref, dt_ref):
        xv = x_ref[...]
        r = lax.rsqrt(jnp.mean(xv * xv, axis=-1, keepdims=True) + EPS)
        h = (xv * r * a_ref[...] + s_ref[...]).astype(BF16)
        p = _mm(h, w_ref[...])
        q_ref[...] = p[:, 0:512].astype(BF16)
        kv_ref[...] = p[:, 512:768].astype(BF16)
        z_ref[...] = p[:, 768:1280]
        xbc_ref[...] = p[:, 1280:2304]
        dt_ref[...] = p[:, 2304:2432]

    def tok(w):
        return pl.BlockSpec((tm, w), lambda i: (i, 0))

    return _pcall(
        body, [x, a1, sh1, w_in], name="in_proj_fwd", grid=(s // tm,),
        out_shape=[jax.ShapeDtypeStruct((s, QW), BF16), jax.ShapeDtypeStruct((s, 2 * KVW), BF16),
                   jax.ShapeDtypeStruct((s, SW), F32), jax.ShapeDtypeStruct((s, XBCW), F32),
                   jax.ShapeDtypeStruct((s, 128), F32)],
        in_specs=[tok(D), _row((1, D)), _row((1, D)), _row((D, PROJ_W))],
        out_specs=[tok(QW), tok(2 * KVW), tok(SW), tok(XBCW), tok(128)], carry=carry)


def _in_proj_bwd(x, dx1, a1, sh1, w_in, dq, dkv, dz, dxbc, ddt, carry=None):
    s = x.shape[0]
    tm = 512

    def body(x_ref, dx1_ref, a_ref, s_ref, w_ref, dq_ref, dkv_ref, dz_ref, dxbc_ref, ddt_ref,
             gx_ref, dproj_ref, h_ref, dsh_ref, p_ref):
        i = pl.program_id(0)

        @pl.when(i == 0)
        def _():
            dsh_ref[...] = jnp.zeros_like(dsh_ref)
            p_ref[...] = jnp.zeros_like(p_ref)

        dproj = jnp.concatenate([dq_ref[...], dkv_ref[...], dz_ref[...], dxbc_ref[...], ddt_ref[...]], axis=1)
        dproj_ref[...] = dproj
        dh = _mm_nt(dproj, w_ref[...])
        xv = x_ref[...]
        r = lax.rsqrt(jnp.mean(xv * xv, axis=-1, keepdims=True) + EPS)
        xn = xv * r
        a = a_ref[...]
        h_ref[...] = (xn * a + s_ref[...]).astype(BF16)
        dsh_ref[0:1, :] += jnp.sum(dh, axis=0, keepdims=True)
        p_ref[0:1, :] += jnp.sum(dh * xn, axis=0, keepdims=True)
        u = dh * a
        gx_ref[...] = dx1_ref[...] + r * u - xn * (r * jnp.mean(u * xn, axis=-1, keepdims=True))

    def tok(w):
        return pl.BlockSpec((tm, w), lambda i: (i, 0))

    return _pcall(
        body, [x, dx1, a1, sh1, w_in, dq, dkv, dz, dxbc, ddt], name="in_proj_bwd", grid=(s // tm,),
        out_shape=[jax.ShapeDtypeStruct((s, D), F32), jax.ShapeDtypeStruct((s, PROJ_W), BF16),
                   jax.ShapeDtypeStruct((s, D), BF16), jax.ShapeDtypeStruct((8, D), F32),
                   jax.ShapeDtypeStruct((8, D), F32)],
        in_specs=[tok(D), tok(D), _row((1, D)), _row((1, D)), _row((D, PROJ_W)),
                  tok(QW), tok(2 * KVW), tok(SW), tok(XBCW), tok(128)],
        out_specs=[tok(D), tok(PROJ_W), tok(D), _row((8, D)), _row((8, D))], carry=carry)


def _attn_geometry():
    dist = np.arange(BLK)[:, None] + BLK - np.arange(2 * BLK)[None, :]
    n = np.maximum(dist, 0)
    max_exact = NBUCKET // 2
    large = max_exact + (np.log(np.maximum(n, 1) / max_exact) / np.log(128 / max_exact)
                         * (NBUCKET - max_exact)).astype(np.int32)
    large = np.minimum(large, NBUCKET - 1)
    bucket = np.where(n < max_exact, n, large).astype(np.int32)
    mask = (dist >= 0) & (dist < 128)
    return bucket, mask


def _attn_heads(n, q_ref, kvp_ref, kvc_ref, bias_ref, sinks_ref):
    qv = q_ref[...] * 0.125
    kvw = jnp.concatenate([kvp_ref[...], kvc_ref[...]], axis=0)
    col = lax.broadcasted_iota(jnp.int32, (BLK, 2 * BLK), 1)
    first = jnp.where(jnp.logical_and(n == 0, col < BLK), NEG, 0.0)
    groups = []
    for g in range(NKV):
        qs = jnp.concatenate([qv[:, (4 * g + r) * HD:(4 * g + r + 1) * HD] for r in range(4)], axis=0)
        kw = kvw[:, g * HD:(g + 1) * HD]
        vw = kvw[:, KVW + g * HD:KVW + (g + 1) * HD]
        sc = _mm_nt(qs, kw)
        pn, ps = [], []
        for r in range(4):
            h = 4 * g + r
            sr = sc[r * BLK:(r + 1) * BLK] + bias_ref[h] + first
            sink = sinks_ref[h]
            m = jnp.maximum(jnp.max(sr, axis=-1, keepdims=True), sink)
            p = jnp.exp(sr - m)
            es = jnp.exp(sink - m)
            inv = 1.0 / (jnp.sum(p, axis=-1, keepdims=True) + es)
            pn.append(p * inv)
            ps.append(es * inv)
        pn = jnp.concatenate(pn, axis=0)
        ps = jnp.concatenate(ps, axis=0)
        o = _mm(pn.astype(BF16), vw)
        groups.append((qs, kw, vw, pn, ps, o))
    return groups


def _unstack_heads(parts):
    return jnp.concatenate([p[r * BLK:(r + 1) * BLK] for p in parts for r in range(4)], axis=1)


def _attn_fwd(q, kv, bias, sinks, nw, carry=None):
    s = q.shape[0]

    def body(q_ref, kvp_ref, kvc_ref, bias_ref, sinks_ref, nw_ref, y_ref):
        n = pl.program_id(0)
        groups = _attn_heads(n, q_ref, kvp_ref, kvc_ref, bias_ref, sinks_ref)
        o = _unstack_heads([g[5] for g in groups])
        r = lax.rsqrt(jnp.mean(o * o, axis=-1, keepdims=True) + EPS)
        y_ref[...] = (o * r * nw_ref[...]).astype(BF16)

    return _pcall(
        body, [q, kv, kv, bias, sinks, nw], name="attn_fwd", grid=(s // BLK,),
        out_shape=[jax.ShapeDtypeStruct((s, QW), BF16)],
        in_specs=[pl.BlockSpec((BLK, QW), lambda n: (n, 0)),
                  pl.BlockSpec((BLK, 2 * KVW), lambda n: (jnp.maximum(n - 1, 0), 0)),
                  pl.BlockSpec((BLK, 2 * KVW), lambda n: (n, 0)),
                  _row((NH, BLK, 2 * BLK)),
                  pl.BlockSpec(memory_space=pltpu.SMEM),
                  _row((1, QW))],
        out_specs=[pl.BlockSpec((BLK, QW), lambda n: (n, 0))], carry=carry)


def _attn_bwd(q, kv, dya, bias, sinks, nw, carry=None):
    s = q.shape[0]
    nb = s // BLK

    def body(q_ref, kvp_ref, kvc_ref, dy_ref, bias_ref, sinks_ref, nw_ref,
             dq_ref, dkv_ref, dbias_ref, dsink_ref, dnw_ref, carry_ref):
        n = pl.program_id(0)

        @pl.when(n == 0)
        def _():
            carry_ref[...] = jnp.zeros_like(carry_ref)
            dbias_ref[...] = jnp.zeros_like(dbias_ref)
            dsink_ref[...] = jnp.zeros_like(dsink_ref)
            dnw_ref[...] = jnp.zeros_like(dnw_ref)

        @pl.when(n < nb)
        def _():
            groups = _attn_heads(n, q_ref, kvp_ref, kvc_ref, bias_ref, sinks_ref)
            o = _unstack_heads([g[5] for g in groups])
            r = lax.rsqrt(jnp.mean(o * o, axis=-1, keepdims=True) + EPS)
            dy = dy_ref[...]
            on = o * r
            dnw_ref[0:1, :] += jnp.sum(dy * on, axis=0, keepdims=True)
            u = dy * nw_ref[...]
            do = r * u - on * (r * jnp.mean(u * on, axis=-1, keepdims=True))
            dq_parts, dk_parts, dv_parts = [], [], []
            for g, (qs, kw, vw, pn, ps, og) in enumerate(groups):
                dos = jnp.concatenate([do[:, (4 * g + r_) * HD:(4 * g + r_ + 1) * HD] for r_ in range(4)], axis=0)
                delta = jnp.sum(dos * og, axis=-1, keepdims=True)
                dp = _mm_nt(dos.astype(BF16), vw)
                ds = pn * (dp - delta)
                dsk = ps * delta
                lane = lax.broadcasted_iota(jnp.int32, (1, 128), 1)
                for r_ in range(4):
                    h = 4 * g + r_
                    dbias_ref[h] += ds[r_ * BLK:(r_ + 1) * BLK]
                    dsink_ref[0:1, :] -= jnp.where(lane == h, jnp.sum(dsk[r_ * BLK:(r_ + 1) * BLK]), 0.0)
                dsb = ds.astype(BF16)
                dq_parts.append(_mm(dsb, kw) * 0.125)
                dk_parts.append(_mm_tn(dsb, qs))
                dv_parts.append(_mm_tn(pn.astype(BF16), dos.astype(BF16)))
            dq_ref[...] = _unstack_heads(dq_parts).astype(BF16)
            dkvw = jnp.concatenate(dk_parts + dv_parts, axis=1)
            dkv_ref[...] = (carry_ref[...] + dkvw[0:BLK]).astype(BF16)
            carry_ref[...] = dkvw[BLK:2 * BLK]

        @pl.when(n == nb)
        def _():
            dkv_ref[...] = carry_ref[...].astype(BF16)

    last = nb - 1
    return _pcall(
        body, [q, kv, kv, dya, bias, sinks, nw], name="attn_bwd", grid=(nb + 1,),
        out_shape=[jax.ShapeDtypeStruct((s, QW), BF16), jax.ShapeDtypeStruct((s, 2 * KVW), BF16),
                   jax.ShapeDtypeStruct((NH, BLK, 2 * BLK), F32), jax.ShapeDtypeStruct((NH, 128), F32),
                   jax.ShapeDtypeStruct((8, QW), F32)],
        in_specs=[pl.BlockSpec((BLK, QW), lambda n: (jnp.minimum(n, last), 0)),
                  pl.BlockSpec((BLK, 2 * KVW), lambda n: (jnp.clip(n - 1, 0, last), 0)),
                  pl.BlockSpec((BLK, 2 * KVW), lambda n: (jnp.minimum(n, last), 0)),
                  pl.BlockSpec((BLK, QW), lambda n: (jnp.minimum(n, last), 0)),
                  _row((NH, BLK, 2 * BLK)),
                  pl.BlockSpec(memory_space=pltpu.SMEM),
                  _row((1, QW))],
        out_specs=[pl.BlockSpec((BLK, QW), lambda n: (jnp.minimum(n, last), 0)),
                   pl.BlockSpec((BLK, 2 * KVW), lambda n: (jnp.maximum(n - 1, 0), 0)),
                   _row((NH, BLK, 2 * BLK)), _row((NH, 128)), _row((8, QW))],
        scratch_shapes=[pltpu.VMEM((BLK, 2 * KVW), F32)], carry=carry)


def _rel_bias_grad(dbias, bucket):
    def body(db_ref, bk_ref, o_ref):
        bk = bk_ref[...]
        lane = lax.broadcasted_iota(jnp.int32, (1, 128), 1)
        for b in range(NBUCKET):
            sel = bk == b
            row = jnp.zeros((1, 128), F32)
            for h in range(NH):
                row = row + jnp.where(lane == h, jnp.sum(jnp.where(sel, db_ref[h], 0.0)), 0.0)
            o_ref[b:b + 1, :] = row

    return pl.pallas_call(
        body, name="rel_bias_grad",
        out_shape=jax.ShapeDtypeStruct((NBUCKET, 128), F32),
    )(dbias, bucket)


def _ssd_consts():
    head_of_lane = np.arange(SW) // HD
    expand = (np.arange(128)[:, None] == head_of_lane[None, :]).astype(np.float32)
    tril = np.tril(np.ones((BLK, BLK), np.float32))
    return (jnp.asarray(expand, BF16), jnp.asarray(expand.T.copy(), BF16), jnp.asarray(tril, BF16),
            jnp.asarray(tril.T.copy(), BF16))


def _ssd_chunk_fwd(c, z_ref, xc_ref, xp_ref, dtr_ref, cw_ref, cb_ref, dtb_ref, a_ref, dk_ref, ex_ref, tril_ref, h_in):
    halo = jnp.where(c == 0, 0.0, xp_ref[BLK - 8:BLK, :])
    ext = jnp.concatenate([halo, xc_ref[...]], axis=0)
    cw = cw_ref[...]
    taps = [ext[8:8 + BLK] if k == CK - 1 else pltpu.roll(ext, CK - 1 - k, 0)[8:8 + BLK] for k in range(CK)]
    pre = cb_ref[...] + sum(cw[k:k + 1, :] * taps[k] for k in range(CK))
    sp = _sig(pre)
    xbc = pre * sp
    xs, bm, cm = xbc[:, 0:SW], xbc[:, SW:SW + 2 * NST], xbc[:, SW + 2 * NST:]
    dtin = dtr_ref[...] + dtb_ref[...]
    dt = jnp.maximum(dtin, 0.0) + jnp.log1p(jnp.exp(-jnp.abs(dtin)))
    av = a_ref[...]
    cs = _sel_l(tril_ref[...], dt * av)
    cst = cs.T
    ex = ex_ref[...]
    dtx = _sel_r(dt, ex)
    csx = _sel_r(cs, ex)
    xdt = xs * dtx
    csl = csx[BLK - 1:BLK, :]
    decx = jnp.exp(csl - csx)
    ecsx = jnp.exp(csx)
    ecl = jnp.exp(csl)
    causal = tril_ref[...].astype(F32) > 0.5
    ydiag, yoff, cbs, lms = [], [], [], []
    for g in range(2):
        bg = bm[:, g * NST:(g + 1) * NST].astype(BF16)
        cg = cm[:, g * NST:(g + 1) * NST].astype(BF16)
        cb = _mm_nt(cg, bg)
        cbs.append(cb)
        yoff.append(_mm(cg, h_in[:, g * 256:(g + 1) * 256].astype(BF16)))
        for r in range(4):
            h = 4 * g + r
            seg = cs[:, h:h + 1] - cst[h:h + 1, :]
            lm = jnp.where(causal, jnp.exp(jnp.minimum(seg, 0.0)), 0.0)
            lms.append(lm)
            ydiag.append(_mm((cb * lm).astype(BF16), xdt[:, h * HD:(h + 1) * HD].astype(BF16)))
    yoff = jnp.concatenate(yoff, axis=1) * ecsx
    y = jnp.concatenate(ydiag, axis=1) + yoff + dk_ref[...] * xs
    return dict(ext=ext, taps=taps, pre=pre, sp=sp, xs=xs, bm=bm, cm=cm, dtin=dtin, dt=dt, av=av, cs=cs, cst=cst,
                dtx=dtx, csx=csx, xdt=xdt, decx=decx, ecsx=ecsx, ecl=ecl, causal=causal, cbs=cbs, lms=lms,
                yoff=yoff, y=y)


def _group_mean(t):
    m0 = jnp.mean(t[:, 0:256], axis=-1, keepdims=True)
    m1 = jnp.mean(t[:, 256:512], axis=-1, keepdims=True)
    return jnp.concatenate([jnp.broadcast_to(m0, (t.shape[0], 256)), jnp.broadcast_to(m1, (t.shape[0], 256))], axis=1)


def _ssd_specs(nc, rev):
    def cur(w):
        return pl.BlockSpec((BLK, w), (lambda i: (nc - 1 - i, 0)) if rev else (lambda i: (i, 0)))
    prev = pl.BlockSpec((BLK, XBCW), (lambda i: (jnp.maximum(nc - 2 - i, 0), 0)) if rev
                        else (lambda i: (jnp.maximum(i - 1, 0), 0)))
    return cur, prev


def _ssd_fwd(z, xbc, dtr, cw, cb, dtb, av, dk, nw, carry=None):
    s = z.shape[0]
    nc = s // BLK
    ex, _, tril, _ = _ssd_consts()

    def body(z_ref, xc_ref, xp_ref, dtr_ref, cw_ref, cb_ref, dtb_ref, a_ref, dk_ref, nw_ref, ex_ref, tril_ref,
             y_ref, hs_ref, h_ref):
        c = pl.program_id(0)

        @pl.when(c == 0)
        def _():
            h_ref[...] = jnp.zeros_like(h_ref)

        h_in = h_ref[...]
        hs_ref[0] = h_in
        f = _ssd_chunk_fwd(c, z_ref, xc_ref, xp_ref, dtr_ref, cw_ref, cb_ref, dtb_ref, a_ref, dk_ref, ex_ref,
                           tril_ref, h_in)
        dx = (f["decx"] * f["xdt"]).astype(BF16)
        st = [_mm_tn(f["bm"][:, g * NST:(g + 1) * NST].astype(BF16), dx[:, g * 256:(g + 1) * 256]) for g in range(2)]
        h_ref[...] = h_in * f["ecl"] + jnp.concatenate(st, axis=1)
        zv = z_ref[...]
        t = f["y"] * (zv * _sig(zv))
        r = lax.rsqrt(_group_mean(t * t) + EPS)
        y_ref[...] = (t * r * nw_ref[...]).astype(BF16)

    cur, prev = _ssd_specs(nc, False)
    return _pcall(
        body, [z, xbc, xbc, dtr, cw, cb, dtb, av, dk, nw, ex, tril], name="ssd_fwd", grid=(nc,),
        out_shape=[jax.ShapeDtypeStruct((s, SW), BF16), jax.ShapeDtypeStruct((nc, NST, SW), F32)],
        in_specs=[cur(SW), cur(XBCW), prev, cur(128), _row((8, XBCW)), _row((1, XBCW)), _row((1, 128)),
                  _row((1, 128)), _row((1, SW)), _row((1, SW)), _row((128, SW)), _row((BLK, BLK))],
        out_specs=[cur(SW), pl.BlockSpec((1, NST, SW), lambda i: (i, 0, 0))],
        scratch_shapes=[pltpu.VMEM((NST, SW), F32)], carry=carry)


def _ssd_bwd(z, xbc, dtr, dys, hs, cw, cb, dtb, av, dk, nw, carry=None):
    s = z.shape[0]
    nc = s // BLK
    ex, ext_t, tril, triu = _ssd_consts()

    def body(z_ref, xc_ref, xp_ref, dtr_ref, dy_ref, hs_ref, cw_ref, cb_ref, dtb_ref, a_ref, dk_ref, nw_ref,
             ex_ref, ext_ref, tril_ref, triu_ref,
             dz_ref, dxbc_ref, ddt_ref, dcw_ref, dcb_ref, dnw_ref, dhd_ref, dh_ref, nxt_ref, dd_ref):
        i = pl.program_id(0)
        c = nc - 1 - i

        @pl.when(i == 0)
        def _():
            dh_ref[...] = jnp.zeros_like(dh_ref)
            nxt_ref[...] = jnp.zeros_like(nxt_ref)
            dd_ref[...] = jnp.zeros_like(dd_ref)
            dcw_ref[...] = jnp.zeros_like(dcw_ref)
            dcb_ref[...] = jnp.zeros_like(dcb_ref)
            dnw_ref[...] = jnp.zeros_like(dnw_ref)
            dhd_ref[...] = jnp.zeros_like(dhd_ref)

        h_in = hs_ref[0]
        f = _ssd_chunk_fwd(c, z_ref, xc_ref, xp_ref, dtr_ref, cw_ref, cb_ref, dtb_ref, a_ref, dk_ref, ex_ref,
                           tril_ref, h_in)
        xs, xdt, decx, ecsx, ecl, dtx = f["xs"], f["xdt"], f["decx"], f["ecsx"], f["ecl"], f["dtx"]
        cs, cst, causal = f["cs"], f["cst"], f["causal"]
        causal_t = triu_ref[...].astype(F32) > 0.5

        zv = z_ref[...]
        sz = _sig(zv)
        gz = zv * sz
        t = f["y"] * gz
        r = lax.rsqrt(_group_mean(t * t) + EPS)
        tn_ = t * r
        dyn = dy_ref[...]
        dnw_ref[0:1, :] += jnp.sum(dyn * tn_, axis=0, keepdims=True)
        u = dyn * nw_ref[...]
        dt_ = r * u - tn_ * (r * _group_mean(u * tn_))
        dy = dt_ * gz
        dz_ref[...] = (dt_ * f["y"] * (sz * (1.0 + zv * (1.0 - sz)))).astype(BF16)

        dd_ref[0:1, :] += jnp.sum(dy * xs, axis=0, keepdims=True)
        dxs = dk_ref[...] * dy

        gst = dh_ref[...]
        edy = ecsx * dy
        dxdt, dbs, dcs_, dcsx_parts, dh_new = [], [], [], [], []
        lane = lax.broadcasted_iota(jnp.int32, (1, 128), 1)
        dcs_intra = jnp.zeros((BLK, 128), F32)
        for g in range(2):
            sl = slice(g * 256, (g + 1) * 256)
            bgf, cgf = f["bm"][:, g * NST:(g + 1) * NST], f["cm"][:, g * NST:(g + 1) * NST]
            bg, cg = bgf.astype(BF16), cgf.astype(BF16)
            gg = gst[:, sl].astype(BF16)
            hg = h_in[:, sl].astype(BF16)
            edyg = edy[:, sl].astype(BF16)
            dc = _mm_nt(edyg, hg)
            dh_new.append(gst[:, sl] * ecl[:, sl] + _mm_tn(cg, edyg))
            bgm = _mm(bg, gg)
            dxdt_g = decx[:, sl] * bgm
            dxg = (decx[:, sl] * xdt[:, sl]).astype(BF16)
            db = _mm_nt(dxg, gg)
            qd = bgm * xdt[:, sl] * decx[:, sl]
            last = jnp.sum(qd, axis=0, keepdims=True) + ecl[:, sl] * jnp.sum(gst[:, sl] * h_in[:, sl], axis=0, keepdims=True)
            rowid = lax.broadcasted_iota(jnp.int32, (BLK, 256), 0)
            dcsx_parts.append(f["yoff"][:, sl] * dy[:, sl] - qd + jnp.where(rowid == BLK - 1, last, 0.0))
            cb_ = f["cbs"][g]
            cbt = _mm_nt(bg, cg)
            dcb_ = jnp.zeros((BLK, BLK), F32)
            dcbt = jnp.zeros((BLK, BLK), F32)
            dxd = []
            for r_ in range(4):
                h = 4 * g + r_
                hl = slice(h * HD, (h + 1) * HD)
                lm = f["lms"][h]
                segt = cst[h:h + 1, :] - cs[:, h:h + 1]
                lmt = jnp.where(causal_t, jnp.exp(jnp.minimum(segt, 0.0)), 0.0)
                dyh = dy[:, hl].astype(BF16)
                xdh = xdt[:, hl].astype(BF16)
                dw = _mm_nt(dyh, xdh)
                dwt = _mm_nt(xdh, dyh)
                wt = cbt * lmt
                dxd.append(_mm(wt.astype(BF16), dyh))
                dcb_ = dcb_ + dw * lm
                dcbt = dcbt + dwt * lmt
                col = jnp.sum(dw * (cb_ * lm), axis=-1, keepdims=True) - jnp.sum(dwt * wt, axis=-1, keepdims=True)
                dcs_intra = dcs_intra + jnp.where(lane == h, col, 0.0)
            dxdt.append(dxdt_g + jnp.concatenate(dxd, axis=1))
            dcs_.append(dc + _mm(dcb_.astype(BF16), bg))
            dbs.append(db + _mm(dcbt.astype(BF16), cg))
        dh_ref[...] = jnp.concatenate(dh_new, axis=1)
        dxdt = jnp.concatenate(dxdt, axis=1)
        dxs = dxs + dxdt * dtx
        ext_t_ = ext_ref[...]
        dcs = dcs_intra + _sel_r(jnp.concatenate(dcsx_parts, axis=1), ext_t_)
        da = _sel_l(triu_ref[...], dcs)
        ddt = da * f["av"] + _sel_r(dxdt * xs, ext_t_)
        dhd_ref[1:2, :] += jnp.sum(da * f["dt"], axis=0, keepdims=True)
        ddtr = ddt * _sig(f["dtin"])
        dhd_ref[0:1, :] += jnp.sum(ddtr, axis=0, keepdims=True)
        ddt_ref[...] = ddtr.astype(BF16)

        sp, pre = f["sp"], f["pre"]
        dact = jnp.concatenate([dxs] + dbs + dcs_, axis=1)
        dpre = dact * (sp * (1.0 + pre * (1.0 - sp)))
        dcb_ref[0:1, :] += jnp.sum(dpre, axis=0, keepdims=True)
        for k in range(CK):
            dcw_ref[k:k + 1, :] += jnp.sum(dpre * f["taps"][k], axis=0, keepdims=True)
        ext2 = jnp.concatenate([dpre, nxt_ref[...]], axis=0)
        cw = cw_ref[...]
        dxr = cw[CK - 1:CK, :] * dpre
        for k in range(CK - 1):
            dxr = dxr + cw[k:k + 1, :] * pltpu.roll(ext2, BLK + 8 - (CK - 1 - k), 0)[0:BLK]
        dxbc_ref[...] = dxr.astype(BF16)
        nxt_ref[...] = dpre[0:8]

        @pl.when(i == nc - 1)
        def _():
            dhd_ref[2:3, :] = _sel_r(dd_ref[...], ext_t_)[0:1, :]

    cur, prev = _ssd_specs(nc, True)
    return _pcall(
        body, [z, xbc, xbc, dtr, dys, hs, cw, cb, dtb, av, dk, nw, ex, ext_t, tril, triu], name="ssd_bwd", grid=(nc,),
        out_shape=[jax.ShapeDtypeStruct((s, SW), BF16), jax.ShapeDtypeStruct((s, XBCW), BF16),
                   jax.ShapeDtypeStruct((s, 128), BF16), jax.ShapeDtypeStruct((8, XBCW), F32),
                   jax.ShapeDtypeStruct((8, XBCW), F32), jax.ShapeDtypeStruct((8, SW), F32),
                   jax.ShapeDtypeStruct((8, 128), F32)],
        in_specs=[cur(SW), cur(XBCW), prev, cur(128), cur(SW),
                  pl.BlockSpec((1, NST, SW), lambda i: (nc - 1 - i, 0, 0)),
                  _row((8, XBCW)), _row((1, XBCW)), _row((1, 128)), _row((1, 128)), _row((1, SW)), _row((1, SW)),
                  _row((128, SW)), _row((SW, 128)), _row((BLK, BLK)), _row((BLK, BLK))],
        out_specs=[cur(SW), cur(XBCW), cur(128), _row((8, XBCW)), _row((8, XBCW)), _row((8, SW)), _row((8, 128))],
        scratch_shapes=[pltpu.VMEM((NST, SW), F32), pltpu.VMEM((8, XBCW), F32), pltpu.VMEM((8, SW), F32)], carry=carry)


def _load_once(i, pairs, sem):
    @pl.when(i == 0)
    def _():
        cps = [pltpu.make_async_copy(src, dst, sem.at[k]) for k, (src, dst) in enumerate(pairs)]
        for cp in cps:
            cp.start()
        for cp in cps:
            cp.wait()


def _mlp_fwd(x, ya, ys, tgt, w_o, w_ga, w_gb, w_dn, gate1, a2, sh2, gate2, fn):
    s = x.shape[0]
    tm = 256

    def body(x_ref, ya_ref, ys_ref, t_ref, wo_hbm, wga_hbm, wgb_hbm, wdn_hbm, g1_ref, a2_ref, s2_ref, g2_ref, fn_ref,
             x1_ref, gu_ref, dx2_ref, loss_ref, dfn_ref, wo, wga, wgb, wdn, sem):
        i = pl.program_id(0)
        _load_once(i, [(wo_hbm, wo), (wga_hbm, wga), (wgb_hbm, wgb), (wdn_hbm, wdn)], sem)

        @pl.when(i == 0)
        def _():
            loss_ref[...] = jnp.zeros_like(loss_ref)
            dfn_ref[...] = jnp.zeros_like(dfn_ref)

        mix = _mm(ya_ref[...], wo[0:QW, :]) + _mm(ys_ref[...], wo[QW:D, :])
        x1 = x_ref[...] + g1_ref[...] * mix
        x1_ref[...] = x1
        r2 = lax.rsqrt(jnp.mean(x1 * x1, axis=-1, keepdims=True) + EPS)
        h2 = (x1 * r2 * a2_ref[...] + s2_ref[...]).astype(BF16)
        ha, hb = h2[:, 0:D // 2], h2[:, D // 2:D]
        nj = DFF // 2
        ff = None
        for j in range(2):
            gj = (_mm(ha, wga[j]) + _mm(hb, wgb[j])).astype(BF16)
            uj = (_mm(ha, wga[j + 2]) + _mm(hb, wgb[j + 2])).astype(BF16)
            gu_ref[:, j * nj:(j + 1) * nj] = gj
            gu_ref[:, DFF + j * nj:DFF + (j + 1) * nj] = uj
            act = gj * _sig(gj) * uj
            part = _mm(act, wdn[j * nj:(j + 1) * nj, :])
            ff = part if ff is None else ff + part
        x2 = x1 + g2_ref[...] * ff
        r3 = lax.rsqrt(jnp.mean(x2 * x2, axis=-1, keepdims=True) + EPS)
        xn = x2 * r3
        fnv = fn_ref[...]
        err = xn * fnv - t_ref[...]
        loss_ref[...] += jnp.sum(err * err) * (0.5 / D)
        dy = err * (1.0 / D)
        dfn_ref[0:1, :] += jnp.sum(dy * xn, axis=0, keepdims=True)
        u = dy * fnv
        dx2_ref[...] = r3 * u - xn * (r3 * jnp.mean(u * xn, axis=-1, keepdims=True))

    def tok(w):
        return pl.BlockSpec((tm, w), lambda i: (i, 0))

    hbm = pl.BlockSpec(memory_space=pl.ANY)
    return pl.pallas_call(
        body, name="mlp_fwd", grid=(s // tm,),
        out_shape=[jax.ShapeDtypeStruct((s, D), F32), jax.ShapeDtypeStruct((s, 2 * DFF), BF16),
                   jax.ShapeDtypeStruct((s, D), F32), jax.ShapeDtypeStruct((8, 128), F32),
                   jax.ShapeDtypeStruct((8, D), F32)],
        in_specs=[tok(D), tok(QW), tok(SW), tok(D), hbm, hbm, hbm, hbm,
                  _row((1, D)), _row((1, D)), _row((1, D)), _row((1, D)), _row((1, D))],
        out_specs=[tok(D), tok(2 * DFF), tok(D), _row((8, 128)), _row((8, D))],
        scratch_shapes=[pltpu.VMEM((D, D), BF16), pltpu.VMEM(w_ga.shape, BF16), pltpu.VMEM(w_gb.shape, BF16),
                        pltpu.VMEM((DFF, D), BF16), pltpu.SemaphoreType.DMA((4,))],
        compiler_params=_cp(("arbitrary",)),
    )(x, ya, ys, tgt, w_o, w_ga, w_gb, w_dn, gate1, a2, sh2, gate2, fn)


def _mlp_bwd(x1, gu, dx2, w_o, w_ga, w_gb, w_dn, gate1, a2, sh2, gate2):
    s = x1.shape[0]
    tm = 256
    nj = 2 * DFF // 4

    def body(x1_ref, gu_ref, dx2_ref, wo_hbm, wga_hbm, wgb_hbm, wdn_hbm, g1_ref, a2_ref, s2_ref, g2_ref,
             dx1_ref, dya_ref, dys_ref, act_ref, dgu_ref, h2_ref, dsh_ref, p_ref, wo, wga, wgb, wdn, sem):
        i = pl.program_id(0)
        _load_once(i, [(wo_hbm, wo), (wga_hbm, wga), (wgb_hbm, wgb), (wdn_hbm, wdn)], sem)

        @pl.when(i == 0)
        def _():
            dsh_ref[...] = jnp.zeros_like(dsh_ref)
            p_ref[...] = jnp.zeros_like(p_ref)

        dx2 = dx2_ref[...]
        dx2g = (dx2 * g2_ref[...]).astype(BF16)
        dha = dhb = None
        for j in range(2):
            dact = _mm_nt(dx2g, wdn[j * nj:(j + 1) * nj, :])
            gv = gu_ref[:, j * nj:(j + 1) * nj].astype(F32)
            uv = gu_ref[:, DFF + j * nj:DFF + (j + 1) * nj].astype(F32)
            sg = _sig(gv)
            sl = gv * sg
            act_ref[:, j * nj:(j + 1) * nj] = (sl * uv).astype(BF16)
            dg = (dact * uv * (sg * (1.0 + gv * (1.0 - sg)))).astype(BF16)
            du = (dact * sl).astype(BF16)
            dgu_ref[:, j * nj:(j + 1) * nj] = dg
            dgu_ref[:, DFF + j * nj:DFF + (j + 1) * nj] = du
            pa = _mm_nt(dg, wga[j]) + _mm_nt(du, wga[j + 2])
            pb = _mm_nt(dg, wgb[j]) + _mm_nt(du, wgb[j + 2])
            dha = pa if dha is None else dha + pa
            dhb = pb if dhb is None else dhb + pb
        dh = jnp.concatenate([dha, dhb], axis=1)
        x1 = x1_ref[...]
        r2 = lax.rsqrt(jnp.mean(x1 * x1, axis=-1, keepdims=True) + EPS)
        xn = x1 * r2
        a2 = a2_ref[...]
        h2_ref[...] = (xn * a2 + s2_ref[...]).astype(BF16)
        dsh_ref[0:1, :] += jnp.sum(dh, axis=0, keepdims=True)
        p_ref[0:1, :] += jnp.sum(dh * xn, axis=0, keepdims=True)
        u = dh * a2
        dx1 = dx2 + r2 * u - xn * (r2 * jnp.mean(u * xn, axis=-1, keepdims=True))
        dx1_ref[...] = dx1
        dcat = _mm_nt((dx1 * g1_ref[...]).astype(BF16), wo[...])
        dya_ref[...] = dcat[:, 0:QW]
        dys_ref[...] = dcat[:, QW:D]

    def tok(w):
        return pl.BlockSpec((tm, w), lambda i: (i, 0))

    hbm = pl.BlockSpec(memory_space=pl.ANY)
    return pl.pallas_call(
        body, name="mlp_bwd", grid=(s // tm,),
        out_shape=[jax.ShapeDtypeStruct((s, D), F32), jax.ShapeDtypeStruct((s, QW), F32),
                   jax.ShapeDtypeStruct((s, SW), F32), jax.ShapeDtypeStruct((s, DFF), BF16),
                   jax.ShapeDtypeStruct((s, 2 * DFF), BF16), jax.ShapeDtypeStruct((s, D), BF16),
                   jax.ShapeDtypeStruct((8, D), F32), jax.ShapeDtypeStruct((8, D), F32)],
        in_specs=[tok(D), tok(2 * DFF), tok(D), hbm, hbm, hbm, hbm, _row((1, D)), _row((1, D)), _row((1, D)), _row((1, D))],
        out_specs=[tok(D), tok(QW), tok(SW), tok(DFF), tok(2 * DFF), tok(D), _row((8, D)), _row((8, D))],
        scratch_shapes=[pltpu.VMEM((D, D), BF16), pltpu.VMEM(w_ga.shape, BF16), pltpu.VMEM(w_gb.shape, BF16),
                        pltpu.VMEM((DFF, D), BF16), pltpu.SemaphoreType.DMA((4,))],
        compiler_params=_cp(("arbitrary",)),
    )(x1, gu, dx2, w_o, w_ga, w_gb, w_dn, gate1, a2, sh2, gate2)


def _wgrad(name, a, b, tn, gate=None, w=None, stacked=False, carry=None):
    s, m = a.shape
    n = b.shape[1]
    tk = min(1024, s)
    nk = s // tk

    def body(*refs):
        if gate is None:
            a_ref, b_ref, o_ref = refs
        else:
            a_ref, b_ref, g_ref, w_ref, o_ref, dg_ref = refs
        k = pl.program_id(1)

        @pl.when(k == 0)
        def _():
            o_ref[...] = jnp.zeros_like(o_ref)

        o_ref[...] += _mm_tn(a_ref[...], b_ref[...].astype(BF16))

        if gate is not None:
            @pl.when(k == nk - 1)
            def _():
                acc = o_ref[...]
                dg_ref[...] = jnp.zeros_like(dg_ref)
                dg_ref[0:1, :] = jnp.sum(acc * w_ref[...].astype(F32), axis=0, keepdims=True)
                o_ref[...] = acc * g_ref[...]

    in_specs = [pl.BlockSpec((tk, m), lambda j, k: (k, 0)), pl.BlockSpec((tk, tn), lambda j, k: (k, j))]
    if stacked:
        out_shape = [jax.ShapeDtypeStruct((n // tn, m, tn), F32)]
        out_specs = [pl.BlockSpec((None, m, tn), lambda j, k: (j, 0, 0))]
    else:
        out_shape = [jax.ShapeDtypeStruct((m, n), F32)]
        out_specs = [pl.BlockSpec((m, tn), lambda j, k: (0, j))]
    args = [a, b]
    if gate is not None:
        in_specs += [pl.BlockSpec((1, tn), lambda j, k: (0, j)), pl.BlockSpec((m, tn), lambda j, k: (0, j))]
        out_shape.append(jax.ShapeDtypeStruct((8, n), F32))
        out_specs.append(pl.BlockSpec((8, tn), lambda j, k: (0, j)))
        args += [gate, w]
    return _pcall(body, args, name=name, grid=(n // tn, nk), out_shape=out_shape, in_specs=in_specs,
                  out_specs=out_specs, carry=carry)


_SMALL = ["ada_b", "norm1", "conv_w", "conv_b", "dt_bias", "A_log", "D_skip", "sinks", "attn_out_norm",
          "ssm_out_norm", "norm2", "rel_bias", "final_norm"]


def _small_grad(name, gs, chip):
    if name == "ada_b":
        return jnp.concatenate([gs[j:j + 1, :] for j in range(6)], axis=1)
    if name == "conv_w":
        full = gs[7:11, :]
        out = full[:, 0:256]
        for j in range(1, 4):
            out = jnp.where(chip == j, full[:, j * 256:(j + 1) * 256], out)
        return out
    row, width = {"norm1": (6, D), "conv_b": (11, D), "norm2": (12, D), "final_norm": (13, D),
                  "attn_out_norm": (14, QW), "ssm_out_norm": (15, SW), "dt_bias": (16, NH), "A_log": (17, NH),
                  "D_skip": (18, NH), "sinks": (19, NH), "rel_bias": (24, NH)}[name]
    rows = NBUCKET if name == "rel_bias" else 1
    return gs[row:row + rows, 0:width]


def _small_update(small_all, where, ws, ms, vs):
    n = len(_SMALL)

    def body(where_ref, sa_ref, *refs):
        w_refs, m_refs, v_refs, outs = refs[:n], refs[n:2 * n], refs[2 * n:3 * n], refs[3 * n:]
        gs = sa_ref[0]
        for b in range(1, 8):
            gs = gs + sa_ref[b]
        chip = where_ref[1]
        for i, name in enumerate(_SMALL):
            g = _small_grad(name, gs, chip)
            lead = (0,) if name == "conv_w" else ()
            d, mo, vo = _adamw(w_refs[i][lead + (...,)], g, m_refs[i][lead + (...,)], v_refs[i][lead + (...,)])
            for k, val in enumerate((g, d, mo, vo)):
                outs[k * n + i][lead + (...,)] = val

    shapes = [jax.ShapeDtypeStruct(w.shape, F32) for w in ws]
    vmem = pl.BlockSpec(memory_space=pltpu.VMEM)
    res = pl.pallas_call(
        body, name="small_update", out_shape=shapes * 4,
        in_specs=[pl.BlockSpec(memory_space=pltpu.SMEM)] + [vmem] * (1 + 3 * n), out_specs=[vmem] * (4 * n),
    )(where, small_all, *ws, *ms, *vs)
    return [res[k * n:(k + 1) * n] for k in range(4)]


def _add_half(name, g, got, where):
    rr, cc = g.shape[2:]

    def body(w_ref, g_ref, r_ref, o_ref, own_ref):
        s = g_ref[...] + r_ref[...]
        o_ref[...] = s.astype(BF16)

        @pl.when(pl.program_id(0) == w_ref[1])
        def _():
            own_ref[...] = s

    spec = pl.BlockSpec((None, rr, cc), lambda i, w_ref: (i, 0, 0))
    return _pcall(body, [where, g, got], name=name, grid=(4,), nprefetch=1,
                  out_shape=[jax.ShapeDtypeStruct(got.shape, BF16), jax.ShapeDtypeStruct((rr, cc), F32)],
                  in_specs=[pl.BlockSpec((None, None, rr, cc), lambda i, w_ref: (i, w_ref[0], 0, 0)), spec],
                  out_specs=[spec, pl.BlockSpec((rr, cc), lambda i, w_ref: (0, 0))])


def _add_chips(name, own, got):
    rr, cc = own.shape
    tr = rr // 2

    def body(s_ref, r_ref, o_ref):
        o_ref[...] = ((s_ref[...] + r_ref[0].astype(F32)) + r_ref[1].astype(F32)) + r_ref[2].astype(F32)

    spec = pl.BlockSpec((tr, cc), lambda i: (i, 0))
    return _pcall(body, [own, got], name=name, grid=(2,), out_shape=[jax.ShapeDtypeStruct((rr, cc), F32)],
                  in_specs=[spec, pl.BlockSpec((3, tr, cc), lambda i: (0, i, 0))], out_specs=[spec])[0]


def _adamw_halves(name, mine, got, w, m, v, where):
    rr, cc = mine.shape
    tr = rr // 2

    def body(w_ref_, t_ref, r_ref, w_ref, m_ref, v_ref, g_ref, d_ref, mo_ref, vo_ref):
        g = jnp.where(pl.program_id(0) == w_ref_[0], t_ref[...], r_ref[...])
        g_ref[...] = g
        d_ref[...], mo_ref[...], vo_ref[...] = _adamw(w_ref[...], g, m_ref[...], v_ref[...])

    half = pl.BlockSpec((tr, cc), lambda h, i, w_ref_: (i, 0))
    full = pl.BlockSpec((None, tr, cc), lambda h, i, w_ref_: (0, 2 * h + i, 0))
    return _pcall(body, [where, mine, got, w, m, v], name=name, grid=(2, 2), nprefetch=1,
                  out_shape=[jax.ShapeDtypeStruct(w.shape, F32)] * 4,
                  in_specs=[half, half, full, full, full], out_specs=[full] * 4)


def _bias_table(rel_bias, bucket, mask):
    def body(rb_ref, bk_ref, mk_ref, o_ref):
        bk = bk_ref[...]
        valid = mk_ref[...] > 0
        for h in range(NH):
            acc = jnp.zeros((BLK, 2 * BLK), F32)
            for b in range(NBUCKET):
                acc = jnp.where(bk == b, rb_ref[b, h], acc)
            o_ref[h] = jnp.where(valid, acc, NEG)

    vmem = pl.BlockSpec(memory_space=pltpu.VMEM)
    return pl.pallas_call(
        body, name="bias_table", out_shape=jax.ShapeDtypeStruct((NH, BLK, 2 * BLK), F32),
        in_specs=[pl.BlockSpec(memory_space=pltpu.SMEM), vmem, vmem], out_specs=vmem,
    )(rel_bias, bucket, mask)


def _pack_small(dsh1, p1, dsh2, p2, dg1a, dg1b, dg2, norm1, norm2, scale1, scale2, dcw, dcb, dfn,
                dnw_attn, dnw_ssm, dhd, av, dsink, drel):
    def body(dsh1_ref, p1_ref, dsh2_ref, p2_ref, dg1a_ref, dg1b_ref, dg2_ref, n1_ref, n2_ref, s1_ref, s2_ref,
             dcw_ref, dcb_ref, dfn_ref, da_ref, ds_ref, dhd_ref, av_ref, dsink_ref, drel_ref, o_ref):
        o_ref[...] = jnp.zeros_like(o_ref)
        p1v, p2v = p1_ref[0:1, :], p2_ref[0:1, :]
        o_ref[0:1, :] = dsh1_ref[0:1, :]
        o_ref[1:2, :] = p1v * n1_ref[...]
        o_ref[2:3, :] = dg1a_ref[0:1, :] + dg1b_ref[0:1, :]
        o_ref[3:4, :] = dsh2_ref[0:1, :]
        o_ref[4:5, :] = p2v * n2_ref[...]
        o_ref[5:6, :] = dg2_ref[0:1, :]
        o_ref[6:7, :] = p1v * (1.0 + s1_ref[...])
        o_ref[7:11, :] = dcw_ref[0:4, :]
        o_ref[11:12, :] = dcb_ref[0:1, :]
        o_ref[12:13, :] = p2v * (1.0 + s2_ref[...])
        o_ref[13:14, :] = dfn_ref[0:1, :]
        o_ref[14:15, 0:QW] = da_ref[0:1, :]
        o_ref[15:16, 0:SW] = ds_ref[0:1, :]
        o_ref[16:17, 0:128] = dhd_ref[0:1, :]
        o_ref[17:18, 0:128] = dhd_ref[1:2, :] * av_ref[...]
        o_ref[18:19, 0:128] = dhd_ref[2:3, :]
        o_ref[19:20, 0:128] = dsink_ref[0:1, :]
        o_ref[24:56, 0:128] = drel_ref[...]

    return pl.pallas_call(body, name="pack_small", out_shape=jax.ShapeDtypeStruct((56, D), F32))(
        dsh1, p1, dsh2, p2, dg1a, dg1b, dg2, norm1, norm2, scale1, scale2, dcw, dcb, dfn,
        dnw_attn, dnw_ssm, dhd, av, dsink, drel)


def _pad_row(a, rows=1):
    return jnp.pad(a.reshape(rows, -1), ((0, 0), (0, D - a.size // rows)))


def kernel(x, c, ada_w, ada_b, norm1, w_in, conv_w, conv_b, dt_bias, A_log, D_skip, sinks, attn_out_norm, ssm_out_norm, w_o, norm2, w_gate_up, w_down, rel_bias, final_norm, loss_target, m_ada_w, m_ada_b, m_norm1, m_w_in, m_conv_w, m_conv_b, m_dt_bias, m_A_log, m_D_skip, m_sinks, m_attn_out_norm, m_ssm_out_norm, m_w_o, m_norm2, m_w_gate_up, m_w_down, m_rel_bias, m_final_norm, v_ada_w, v_ada_b, v_norm1, v_w_in, v_conv_w, v_conv_b, v_dt_bias, v_A_log, v_D_skip, v_sinks, v_attn_out_norm, v_ssm_out_norm, v_w_o, v_norm2, v_w_gate_up, v_w_down, v_rel_bias, v_final_norm):
    xi, yi, ci = lax.axis_index("x"), lax.axis_index("y"), lax.axis_index("c")
    chip = 2 * xi + yi
    me = 4 * xi + 2 * yi + ci
    where = jnp.stack([ci, chip]).astype(jnp.int32)
    xs2, tgt = x[0], loss_target[0]

    first = jnp.concatenate([c, _pad_row(conv_w[0], CK), jnp.zeros((3, D), F32)], axis=0)
    w_in_b, w_o_b, w_dn_b = w_in[0].astype(BF16), w_o[0].astype(BF16), w_down[0].astype(BF16)
    w_gu_b = w_gate_up[0].astype(BF16)
    first_all, w_in_g = _exchange("gather_first", _merge(_gather8_carry(first), _gather_chips_carry([w_in_b])))
    c_all = first_all[:, 0, :]
    cw_full = jnp.concatenate([first_all[2 * j, 1:1 + CK, 0:256] for j in range(4)], axis=1)
    w_in_f = jnp.pad(jnp.transpose(w_in_g, (1, 0, 2)).reshape(D, IN_W), ((0, 0), (0, PROJ_W - IN_W)))

    ncol = ada_w.shape[2]
    mod_cols = _ada_fwd(c_all, ada_w[0], lax.dynamic_slice(ada_b, (0, chip * ncol), (1, ncol)))
    mod_all = _exchange("gather_mod", _gather_chips_carry([mod_cols]))[0]
    mod = lax.dynamic_slice(jnp.transpose(mod_all, (1, 0, 2)).reshape(8, 4 * ncol), (me, 0), (1, 4 * ncol))
    shift1, scale1, gate1, shift2, scale2, gate2 = [mod[:, j * D:(j + 1) * D] for j in range(6)]
    a1 = norm1 * (1.0 + scale1)
    a2 = norm2 * (1.0 + scale2)

    hdn = DFF // 8
    q, kv, z, xbc, dtr, w_o_g, w_dna_g = _in_proj_fwd(xs2, a1, shift1, w_in_f,
                                                      carry=_gather_chips_carry([w_o_b, w_dn_b[0:hdn]]))
    w_o_f = w_o_g.reshape(D, D)
    bucket, mask = _attn_geometry()
    bucket = jnp.asarray(bucket)
    bias = _bias_table(rel_bias, bucket, jnp.asarray(mask.astype(np.int32)))
    sinks1 = sinks[0]
    ya, w_ga_g = _attn_fwd(q, kv, bias, sinks1, attn_out_norm, carry=_gather_chips_carry([w_gu_b[0:D // 2]]))
    cw8 = jnp.concatenate([cw_full, jnp.zeros((4, XBCW), F32)], axis=0)
    dtb = _pad_row(dt_bias)[:, 0:128]
    av = _pad_row(-jnp.exp(A_log))[:, 0:128]
    dk = jnp.repeat(D_skip, HD, axis=1)
    ys, hs, w_gb_g, w_dnb_g = _ssd_fwd(z, xbc, dtr, cw8, conv_b, dtb, av, dk, ssm_out_norm,
                                       carry=_gather_chips_carry([w_gu_b[D // 2:D], w_dn_b[hdn:2 * hdn]]))
    w_dn_f = jnp.stack([w_dna_g, w_dnb_g], axis=1).reshape(DFF, D)
    fn = final_norm[None, :]
    x1, gu, dx2, loss_acc, dfn = _mlp_fwd(xs2, ya, ys, tgt, w_o_f, w_ga_g, w_gb_g, w_dn_f, gate1, a2, shift2, gate2, fn)
    loss = lax.psum(loss_acc[0, 0], ("x", "y", "c"))

    def to_sibling(p):
        return _Carry([p], [jax.ShapeDtypeStruct((4,) + p.shape[2:], F32)],
                      lambda x_, y_, c_: [(_SIBLING, 0, (j, 1 - c_), 0, j) for j in range(4)])

    def to_chips(s4):
        return _Carry([s4], [jax.ShapeDtypeStruct((3,) + s4.shape[1:], s4.dtype)],
                      lambda x_, y_, c_: [(f, 0, jnp.bitwise_xor(2 * x_ + y_, k + 1), 0, k) for k, f in enumerate(_CHIPS3)])

    def back(t):
        return _Carry([t[None]], [jax.ShapeDtypeStruct((1,) + t.shape, F32)], lambda x_, y_, c_: [(_SIBLING, 0, 0, 0, 0)])

    dx1, dya, dys, act, dgu, h2, dsh2, p2 = _mlp_bwd(x1, gu, dx2, w_o_f, w_ga_g, w_gb_g, w_dn_f, gate1, a2, shift2, gate2)
    p_gu = _wgrad("wgrad_gate_up", h2, dgu, 2 * DFF // 4, stacked=True)[0].reshape(4, 2, D // 2, 2 * DFF // 4)
    g_dn, dg2, got1_gu = _wgrad("wgrad_down", act, dx2, D // 2, gate2, w_dn_f, carry=to_sibling(p_gu))
    p_dn = g_dn.reshape(4, 2, DFF // 8, D)
    s4_gu, own_gu = _add_half("rs_add_half_gu", p_gu, got1_gu, where)
    dq, dkv, dbias, dsink, dnw_attn, got2_gu, got1_dn = _attn_bwd(
        q, kv, dya, bias, sinks1, attn_out_norm, carry=_merge(to_chips(s4_gu), to_sibling(p_dn)))
    drel = _rel_bias_grad(dbias, bucket)
    mine_gu = _add_chips("rs_add_chips_gu", own_gu, got2_gu)
    s4_dn, own_dn = _add_half("rs_add_half_dn", p_dn, got1_dn, where)
    dz, dxbc, ddt, dcw, dcb, dnw_ssm, dhd, got2_dn, got3_gu = _ssd_bwd(
        z, xbc, dtr, dys, hs, cw8, conv_b, dtb, av, dk, ssm_out_norm, carry=_merge(to_chips(s4_dn), back(mine_gu)))
    mine_dn = _add_chips("rs_add_chips_dn", own_dn, got2_dn)
    grad_x, dproj, h1, dsh1, p1 = _in_proj_bwd(xs2, dx1, a1, shift1, w_in_f, dq, dkv, dz, dxbc, ddt)
    g_in, got3_dn = _wgrad("wgrad_in", h1, dproj, PROJ_W, carry=back(mine_dn))
    p_in = jnp.transpose(g_in[:, 0:IN_W].reshape(D, 4, IN_W // 4), (1, 0, 2)).reshape(4, 2, D // 2, IN_W // 4)
    g_oa, dg1a, got1_in = _wgrad("wgrad_o_attn", ya, dx1, D, gate1, w_o_f[0:QW], carry=to_sibling(p_in))
    s4_in, own_in = _add_half("rs_add_half_in", p_in, got1_in, where)
    g_os, dg1b, got2_in = _wgrad("wgrad_o_ssm", ys, dx1, D, gate1, w_o_f[QW:D], carry=to_chips(s4_in))
    mine_in = _add_chips("rs_add_chips_in", own_in, got2_in)
    p_o = jnp.concatenate([g_oa, g_os], axis=0).reshape(4, 2, D // 8, D)

    small = _pack_small(dsh1, p1, dsh2, p2, dg1a, dg1b, dg2, norm1, norm2, scale1, scale2, dcw, dcb, dfn,
                        dnw_attn, dnw_ssm, dhd, av, dsink, drel)
    small_all, got1_o, got3_in = _exchange(
        "gather_small", _merge(_gather8_carry(small), to_sibling(p_o), back(mine_in)))
    s4_o, own_o = _add_half("rs_add_half_o", p_o, got1_o, where)
    mine_o = _add_chips("rs_add_chips_o", own_o, _exchange("rs_chips_o", to_chips(s4_o))[0])
    got3_o = _exchange("rs_back_o", back(mine_o))[0]
    small_res = _small_update(
        small_all, where,
        [ada_b, norm1, conv_w, conv_b, dt_bias, A_log, D_skip, sinks, attn_out_norm, ssm_out_norm, norm2, rel_bias,
         final_norm[None, :]],
        [m_ada_b, m_norm1, m_conv_w, m_conv_b, m_dt_bias, m_A_log, m_D_skip, m_sinks, m_attn_out_norm,
         m_ssm_out_norm, m_norm2, m_rel_bias, m_final_norm[None, :]],
        [v_ada_b, v_norm1, v_conv_w, v_conv_b, v_dt_bias, v_A_log, v_D_skip, v_sinks, v_attn_out_norm,
         v_ssm_out_norm, v_norm2, v_rel_bias, v_final_norm[None, :]])
    small_out = [dict(zip(_SMALL, r)) for r in small_res]
    for r in small_out:
        r["final_norm"] = r["final_norm"][0]

    dmod_all = small_all[:, 0:6, :].reshape(8, 6 * D)
    dmod_loc = lax.dynamic_slice(dmod_all, (0, chip * ncol), (8, ncol))
    ada_out = _ada_bwd_adamw(c_all.T, dmod_loc, ada_w[0], m_ada_w[0], v_ada_w[0])

    big_gu = _adamw_halves("adamw_gate_up", mine_gu, got3_gu[0], w_gate_up, m_w_gate_up, v_w_gate_up, where)
    big_dn = _adamw_halves("adamw_down", mine_dn, got3_dn[0], w_down, m_w_down, v_w_down, where)
    big_o = _adamw_halves("adamw_o", mine_o, got3_o[0], w_o, m_w_o, v_w_o, where)
    big_in = _adamw_halves("adamw_in", mine_in, got3_in[0], w_in, m_w_in, v_w_in, where)
    big = [big_in, big_o, big_gu, big_dn]

    order = ["ada_w", "ada_b", "norm1", "w_in", "conv_w", "conv_b", "dt_bias", "A_log", "D_skip", "sinks",
             "attn_out_norm", "ssm_out_norm", "w_o", "norm2", "w_gate_up", "w_down", "rel_bias", "final_norm"]
    bigname = {"w_in": 0, "w_o": 1, "w_gate_up": 2, "w_down": 3}
    res = [loss, grad_x[None]]
    for kind in range(4):
        for nm in order:
            if nm == "ada_w":
                res.append(ada_out[kind][None])
            elif nm in bigname:
                res.append(big[bigname[nm]][kind])
            else:
                res.append(small_out[kind][nm])
    return tuple(res)
```

```python
import numpy as np
import jax
import jax.numpy as jnp
from jax import lax
from jax.experimental import pallas as pl
from jax.experimental.pallas import tpu as pltpu

F32, BF16 = jnp.float32, jnp.bfloat16
HI = lax.Precision.HIGHEST

D = 1024
QW, KVW = 512, 128
NH, HD, NKV = 8, 64, 2
SW = 512
NST = 128
XBCW = 1024
CK = 4
BLK = 128
DFF = 2816
IN_W = 2312
PROJ_W = 2432
EPS = 1e-6
NEG = -1e30
NBUCKET = 32

B1, B2, LR, AEPS, WD, STEP = 0.9, 0.999, 0.001, 1e-08, 0.01, 10

VMEM_LIMIT = 56 * 1024 * 1024

_NT = (((1,), (1,)), ((), ()))
_TN = (((0,), (0,)), ((), ()))


def _mm(a, b):
    return jnp.dot(a, b, preferred_element_type=F32)


def _mm_nt(a, b):
    return lax.dot_general(a, b, _NT, preferred_element_type=F32)


def _mm_tn(a, b):
    return lax.dot_general(a, b, _TN, preferred_element_type=F32)


def _mm_hi(a, b):
    return jnp.dot(a, b, preferred_element_type=F32, precision=HI)


def _split3(x):
    hi = x.astype(BF16)
    r = x - hi.astype(F32)
    mid = r.astype(BF16)
    lo = (r - mid.astype(F32)).astype(BF16)
    return hi, mid, lo


def _sel_r(x, e):
    hi, mid, lo = _split3(x)
    return (_mm(hi, e) + _mm(mid, e)) + _mm(lo, e)


def _sel_l(e, x):
    hi, mid, lo = _split3(x)
    return (_mm(e, hi) + _mm(e, mid)) + _mm(e, lo)


def _sig(x):
    return 1.0 / (1.0 + jnp.exp(-x))


def _cp(sem):
    return pltpu.CompilerParams(dimension_semantics=sem, vmem_limit_bytes=VMEM_LIMIT)


def _row(shape):
    nd = len(shape)
    return pl.BlockSpec(shape, lambda *_: (0,) * nd)


def _adamw(w, g, m, v):
    m = B1 * m + (1.0 - B1) * g
    v = B2 * v + (1.0 - B2) * (g * g)
    m_hat = m / (1.0 - B1 ** STEP)
    v_hat = v / (1.0 - B2 ** STEP)
    delta = -LR * (m_hat / (jnp.sqrt(v_hat) + AEPS) + WD * w)
    return delta, m, v


class _Carry:
    def __init__(self, inps, outs, copies):
        self.inps, self.outs, self.copies = list(inps), list(outs), copies
        self.n = len(copies(0, 0, 0))

    def descriptors(self, in_refs, out_refs, send_sems, recv_sems):
        x, y, c = lax.axis_index("x"), lax.axis_index("y"), lax.axis_index("c")
        out = []
        for j, (flip, a, si, o, di) in enumerate(self.copies(x, y, c)):
            if flip is None:
                out.append(pltpu.make_async_copy(in_refs[a].at[si], out_refs[o].at[di], send_sems.at[j]))
            else:
                fx, fy, fc = flip
                peer = (1 - x if fx else x, 1 - y if fy else y, 1 - c if fc else c)
                out.append(pltpu.make_async_remote_copy(
                    src_ref=in_refs[a].at[si], dst_ref=out_refs[o].at[di],
                    send_sem=send_sems.at[j], recv_sem=recv_sems.at[j],
                    device_id=peer, device_id_type=pl.DeviceIdType.MESH))
        return out


def _pcall(body, args, *, name, grid, in_specs, out_specs, out_shape, scratch_shapes=(), sem=None, nprefetch=0,
           carry=None):
    out_shape, out_specs = list(out_shape), list(out_specs)
    in_specs, scratch_shapes = list(in_specs), list(scratch_shapes)
    nin, nout, nscr = len(in_specs), len(out_shape), len(scratch_shapes)
    run = body
    if carry is not None:
        ncin, ncout = len(carry.inps), len(carry.outs)
        hbm = pl.BlockSpec(memory_space=pl.ANY)

        def run(*refs):
            pre, r = refs[:nprefetch], refs[nprefetch:]
            ins, cins = r[:nin], r[nin:nin + ncin]
            r = r[nin + ncin:]
            outs, couts = r[:nout], r[nout:nout + ncout]
            r = r[nout + ncout:]
            scr, (send_sems, recv_sems) = r[:nscr], r[nscr:]
            first = pl.program_id(0) == 0
            last = pl.program_id(0) == grid[0] - 1
            for ax in range(1, len(grid)):
                first = jnp.logical_and(first, pl.program_id(ax) == 0)
                last = jnp.logical_and(last, pl.program_id(ax) == grid[ax] - 1)

            @pl.when(first)
            def _():
                for d in carry.descriptors(cins, couts, send_sems, recv_sems):
                    d.start()

            body(*pre, *ins, *outs, *scr)

            @pl.when(last)
            def _():
                for d in carry.descriptors(cins, couts, send_sems, recv_sems):
                    d.wait()

        in_specs = in_specs + [hbm] * ncin
        out_specs = out_specs + [hbm] * ncout
        out_shape = out_shape + carry.outs
        scratch_shapes = scratch_shapes + [pltpu.SemaphoreType.DMA((carry.n,)), pltpu.SemaphoreType.DMA((carry.n,))]
        args = list(args) + carry.inps
    if sem is None:
        sem = ("arbitrary",) * len(grid)
    if nprefetch:
        kw = dict(grid_spec=pltpu.PrefetchScalarGridSpec(num_scalar_prefetch=nprefetch, grid=grid, in_specs=in_specs,
                                                         out_specs=out_specs, scratch_shapes=scratch_shapes))
    else:
        kw = dict(grid=grid, in_specs=in_specs, out_specs=out_specs, scratch_shapes=scratch_shapes)
    res = pl.pallas_call(run, name=name, out_shape=out_shape, compiler_params=_cp(sem), **kw)(*args)
    return list(res)


def _merge(*carries):
    inps, outs, offs = [], [], []
    for cr in carries:
        offs.append((len(inps), len(outs)))
        inps += cr.inps
        outs += cr.outs

    def copies(x, y, c):
        return [(f, a + io, si, o + oo, di) for cr, (io, oo) in zip(carries, offs) for f, a, si, o, di in cr.copies(x, y, c)]

    return _Carry(inps, outs, copies)


def _exchange(name, carry):
    return _pcall(lambda: None, [], name=name, grid=(1,), in_specs=[], out_specs=[], out_shape=[], carry=carry)


_ALL7 = [(f >> 2 & 1, f >> 1 & 1, f & 1) for f in range(1, 8)]
_CHIPS3 = [(0, 1, 0), (1, 0, 0), (1, 1, 0)]
_SIBLING = (0, 0, 1)


def _gather8_carry(blk):
    def copies(x, y, c):
        me = 4 * x + 2 * y + c
        return [(None, 0, 0, 0, me)] + [(f, 0, 0, 0, me) for f in _ALL7]

    return _Carry([blk[None]], [jax.ShapeDtypeStruct((8,) + blk.shape, blk.dtype)], copies)


def _gather_chips_carry(blks):
    def copies(x, y, c):
        chip = 2 * x + y
        return [(f, a, 0, a, chip) for a in range(len(blks)) for f in [None] + _CHIPS3]

    return _Carry([b[None] for b in blks], [jax.ShapeDtypeStruct((4,) + b.shape, b.dtype) for b in blks], copies)


def _ada_fwd(c_all, w_loc, b_loc):
    n = w_loc.shape[1]
    tn = 512

    def body(c_ref, w_ref, b_ref, o_ref):
        cv = c_ref[...]
        cond = cv * _sig(cv)
        o_ref[...] = _mm_hi(cond, w_ref[...]) + b_ref[...]

    return pl.pallas_call(
        body, name="ada_fwd", grid=(n // tn,),
        out_shape=jax.ShapeDtypeStruct((8, n), F32),
        in_specs=[_row((8, D)), pl.BlockSpec((D, tn), lambda j: (0, j)), pl.BlockSpec((1, tn), lambda j: (0, j))],
        out_specs=pl.BlockSpec((8, tn), lambda j: (0, j)),
        compiler_params=_cp(("parallel",)),
    )(c_all, w_loc, b_loc)


def _ada_bwd_adamw(c_all_t, dmod_loc, w, m, v, carry=None):
    n = w.shape[1]
    tn = 512

    def body(ct_ref, dm_ref, w_ref, m_ref, v_ref, g_ref, d_ref, mo_ref, vo_ref):
        ct = ct_ref[...]
        cond = ct * _sig(ct)
        dm = dm_ref[...]
        g = cond[:, 0:1] * dm[0:1, :]
        for b in range(1, 8):
            g = g + cond[:, b:b + 1] * dm[b:b + 1, :]
        g_ref[...] = g
        d_ref[...], mo_ref[...], vo_ref[...] = _adamw(w_ref[...], g, m_ref[...], v_ref[...])

    wspec = pl.BlockSpec((D, tn), lambda j: (0, j))
    return _pcall(
        body, [c_all_t, dmod_loc, w, m, v], name="ada_bwd_adamw", grid=(n // tn,),
        out_shape=[jax.ShapeDtypeStruct((D, n), F32)] * 4,
        in_specs=[_row((D, 8)), pl.BlockSpec((8, tn), lambda j: (0, j)), wspec, wspec, wspec],
        out_specs=[wspec] * 4, carry=carry)


def _in_proj_fwd(x, a1, sh1, w_in, carry=None):
    s = x.shape[0]
    tm = 512

    def body(x_ref, a_ref, s_ref, w_ref, q_ref, kv_ref, z_ref, xbc_ref, dt_ref):
        xv = x_ref[...]
        r = lax.rsqrt(jnp.mean(xv * xv, axis=-1, keepdims=True) + EPS)
        h = (xv * r * a_ref[...] + s_ref[...]).astype(BF16)
        p = _mm_nt(h, w_ref[...])
        q_ref[...] = p[:, 0:512].astype(BF16)
        kv_ref[...] = p[:, 512:768].astype(BF16)
        z_ref[...] = p[:, 768:1280]
        xbc_ref[...] = p[:, 1280:2304]
        dt_ref[...] = p[:, 2304:2432]

    def tok(w):
        return pl.BlockSpec((tm, w), lambda i: (i, 0))

    return _pcall(
        body, [x, a1, sh1, w_in], name="in_proj_fwd", grid=(s // tm,),
        out_shape=[jax.ShapeDtypeStruct((s, QW), BF16), jax.ShapeDtypeStruct((s, 2 * KVW), BF16),
                   jax.ShapeDtypeStruct((s, SW), F32), jax.ShapeDtypeStruct((s, XBCW), F32),
                   jax.ShapeDtypeStruct((s, 128), F32)],
        in_specs=[tok(D), _row((1, D)), _row((1, D)), _row((PROJ_W, D))],
        out_specs=[tok(QW), tok(2 * KVW), tok(SW), tok(XBCW), tok(128)], carry=carry)


def _in_proj_bwd(x, dx1, a1, sh1, w_in, dq, dkv, dz, dxbc, ddt, carry=None):
    s = x.shape[0]
    tm = 512

    def body(x_ref, dx1_ref, a_ref, s_ref, w_ref, dq_ref, dkv_ref, dz_ref, dxbc_ref, ddt_ref,
             gx_ref, dproj_ref, h_ref, dsh_ref, p_ref):
        i = pl.program_id(0)

        @pl.when(i == 0)
        def _():
            dsh_ref[...] = jnp.zeros_like(dsh_ref)
            p_ref[...] = jnp.zeros_like(p_ref)

        dproj = jnp.concatenate([dq_ref[...], dkv_ref[...], dz_ref[...], dxbc_ref[...], ddt_ref[...]], axis=1)
        dproj_ref[...] = dproj
        dh = _mm(dproj, w_ref[...])
        xv = x_ref[...]
        r = lax.rsqrt(jnp.mean(xv * xv, axis=-1, keepdims=True) + EPS)
        xn = xv * r
        a = a_ref[...]
        h_ref[...] = (xn * a + s_ref[...]).astype(BF16)
        dsh_ref[0:1, :] += jnp.sum(dh, axis=0, keepdims=True)
        p_ref[0:1, :] += jnp.sum(dh * xn, axis=0, keepdims=True)
        u = dh * a
        gx_ref[...] = dx1_ref[...] + r * u - xn * (r * jnp.mean(u * xn, axis=-1, keepdims=True))

    def tok(w):
        return pl.BlockSpec((tm, w), lambda i: (i, 0))

    return _pcall(
        body, [x, dx1, a1, sh1, w_in, dq, dkv, dz, dxbc, ddt], name="in_proj_bwd", grid=(s // tm,),
        out_shape=[jax.ShapeDtypeStruct((s, D), F32), jax.ShapeDtypeStruct((s, PROJ_W), BF16),
                   jax.ShapeDtypeStruct((s, D), BF16), jax.ShapeDtypeStruct((8, D), F32),
                   jax.ShapeDtypeStruct((8, D), F32)],
        in_specs=[tok(D), tok(D), _row((1, D)), _row((1, D)), _row((PROJ_W, D)),
                  tok(QW), tok(2 * KVW), tok(SW), tok(XBCW), tok(128)],
        out_specs=[tok(D), tok(PROJ_W), tok(D), _row((8, D)), _row((8, D))], carry=carry)


def _attn_geometry():
    dist = np.arange(BLK)[:, None] + BLK - np.arange(2 * BLK)[None, :]
    n = np.maximum(dist, 0)
    max_exact = NBUCKET // 2
    large = max_exact + (np.log(np.maximum(n, 1) / max_exact) / np.log(128 / max_exact)
                         * (NBUCKET - max_exact)).astype(np.int32)
    large = np.minimum(large, NBUCKET - 1)
    bucket = np.where(n < max_exact, n, large).astype(np.int32)
    mask = (dist >= 0) & (dist < 128)
    return bucket, mask


def _attn_heads(is_first, q_blk, kvw, bias_ref, sinks_ref):
    qv = q_blk * 0.125
    col = lax.broadcasted_iota(jnp.int32, (BLK, 2 * BLK), 1)
    first = jnp.where(jnp.logical_and(is_first, col < BLK), NEG, 0.0)
    groups = []
    for g in range(NKV):
        qs = jnp.concatenate([qv[:, (4 * g + r) * HD:(4 * g + r + 1) * HD] for r in range(4)], axis=0)
        kw = kvw[:, g * HD:(g + 1) * HD]
        vw = kvw[:, KVW + g * HD:KVW + (g + 1) * HD]
        sc = _mm_nt(qs, kw)
        pn, ps = [], []
        for r in range(4):
            h = 4 * g + r
            sr = sc[r * BLK:(r + 1) * BLK] + bias_ref[h] + first
            sink = sinks_ref[h]
            m = jnp.maximum(jnp.max(sr, axis=-1, keepdims=True), sink)
            p = jnp.exp(sr - m)
            es = jnp.exp(sink - m)
            inv = 1.0 / (jnp.sum(p, axis=-1, keepdims=True) + es)
            pn.append(p * inv)
            ps.append(es * inv)
        pn = jnp.concatenate(pn, axis=0)
        ps = jnp.concatenate(ps, axis=0)
        o = _mm(pn.astype(BF16), vw)
        groups.append((qs, kw, vw, pn, ps, o))
    return groups


def _unstack_heads(parts):
    return jnp.concatenate([p[r * BLK:(r + 1) * BLK] for p in parts for r in range(4)], axis=1)


def _attn_fwd(q, kv, bias, sinks, nw, carry=None):
    s = q.shape[0]

    def body(q_ref, kvp_ref, kvc_ref, bias_ref, sinks_ref, nw_ref, y_ref):
        t = pl.program_id(0)
        kv3 = jnp.concatenate([kvp_ref[...], kvc_ref[...]], axis=0)
        for sub in range(2):
            rows = slice(sub * BLK, (sub + 1) * BLK)
            groups = _attn_heads(jnp.logical_and(t == 0, sub == 0), q_ref[rows, :], kv3[sub * BLK:(sub + 2) * BLK],
                                 bias_ref, sinks_ref)
            o = _unstack_heads([g[5] for g in groups])
            r = lax.rsqrt(jnp.mean(o * o, axis=-1, keepdims=True) + EPS)
            y_ref[rows, :] = (o * r * nw_ref[...]).astype(BF16)

    return _pcall(
        body, [q, kv, kv, bias, sinks, nw], name="attn_fwd", grid=(s // (2 * BLK),),
        out_shape=[jax.ShapeDtypeStruct((s, QW), BF16)],
        in_specs=[pl.BlockSpec((2 * BLK, QW), lambda t: (t, 0)),
                  pl.BlockSpec((BLK, 2 * KVW), lambda t: (jnp.maximum(2 * t - 1, 0), 0)),
                  pl.BlockSpec((2 * BLK, 2 * KVW), lambda t: (t, 0)),
                  _row((NH, BLK, 2 * BLK)),
                  pl.BlockSpec(memory_space=pltpu.SMEM),
                  _row((1, QW))],
        out_specs=[pl.BlockSpec((2 * BLK, QW), lambda t: (t, 0))], carry=carry)


def _attn_bwd(q, kv, dya, bias, sinks, nw, carry=None):
    s = q.shape[0]
    nt = s // (2 * BLK)

    def body(q_ref, kvp_ref, kvc_ref, dy_ref, bias_ref, sinks_ref, nw_ref,
             dq_ref, dkv_ref, dbias_ref, dsink_ref, dnw_ref, carry_ref, held_ref):
        t = pl.program_id(0)

        @pl.when(t == 0)
        def _():
            carry_ref[...] = jnp.zeros_like(carry_ref)
            held_ref[...] = jnp.zeros_like(held_ref)
            dbias_ref[...] = jnp.zeros_like(dbias_ref)
            dsink_ref[...] = jnp.zeros_like(dsink_ref)
            dnw_ref[...] = jnp.zeros_like(dnw_ref)

        def block(sub, kv3):
            rows = slice(sub * BLK, (sub + 1) * BLK)
            groups = _attn_heads(jnp.logical_and(t == 0, sub == 0), q_ref[rows, :], kv3[sub * BLK:(sub + 2) * BLK],
                                 bias_ref, sinks_ref)
            o = _unstack_heads([g[5] for g in groups])
            r = lax.rsqrt(jnp.mean(o * o, axis=-1, keepdims=True) + EPS)
            dy = dy_ref[rows, :]
            on = o * r
            dnw_ref[0:1, :] += jnp.sum(dy * on, axis=0, keepdims=True)
            u = dy * nw_ref[...]
            do = r * u - on * (r * jnp.mean(u * on, axis=-1, keepdims=True))
            dq_parts, dk_parts, dv_parts = [], [], []
            for g, (qs, kw, vw, pn, ps, og) in enumerate(groups):
                dos = jnp.concatenate([do[:, (4 * g + r_) * HD:(4 * g + r_ + 1) * HD] for r_ in range(4)], axis=0)
                delta = jnp.sum(dos * og, axis=-1, keepdims=True)
                dp = _mm_nt(dos.astype(BF16), vw)
                ds = pn * (dp - delta)
                dsk = ps * delta
                lane = lax.broadcasted_iota(jnp.int32, (1, 128), 1)
                for r_ in range(4):
                    h = 4 * g + r_
                    dbias_ref[h] += ds[r_ * BLK:(r_ + 1) * BLK]
                    dsink_ref[0:1, :] -= jnp.where(lane == h, jnp.sum(dsk[r_ * BLK:(r_ + 1) * BLK]), 0.0)
                dsb = ds.astype(BF16)
                dq_parts.append(_mm(dsb, kw) * 0.125)
                dk_parts.append(_mm_tn(dsb, qs))
                dv_parts.append(_mm_tn(pn.astype(BF16), dos.astype(BF16)))
            dq_ref[rows, :] = _unstack_heads(dq_parts).astype(BF16)
            return jnp.concatenate(dk_parts + dv_parts, axis=1)

        @pl.when(t < nt)
        def _():
            kv3 = jnp.concatenate([kvp_ref[...], kvc_ref[...]], axis=0)
            d0 = block(0, kv3)
            dkv_ref[0:BLK, :] = held_ref[...].astype(BF16)
            dkv_ref[BLK:2 * BLK, :] = (carry_ref[...] + d0[0:BLK]).astype(BF16)
            d1 = block(1, kv3)
            held_ref[...] = d0[BLK:2 * BLK] + d1[0:BLK]
            carry_ref[...] = d1[BLK:2 * BLK]

        @pl.when(t == nt)
        def _():
            dkv_ref[0:BLK, :] = held_ref[...].astype(BF16)
            dkv_ref[BLK:2 * BLK, :] = carry_ref[...].astype(BF16)

    last = nt - 1
    tile = lambda w: pl.BlockSpec((2 * BLK, w), lambda t: (jnp.minimum(t, last), 0))
    return _pcall(
        body, [q, kv, kv, dya, bias, sinks, nw], name="attn_bwd", grid=(nt + 1,),
        out_shape=[jax.ShapeDtypeStruct((s, QW), BF16), jax.ShapeDtypeStruct((s, 2 * KVW), BF16),
                   jax.ShapeDtypeStruct((NH, BLK, 2 * BLK), F32), jax.ShapeDtypeStruct((NH, 128), F32),
                   jax.ShapeDtypeStruct((8, QW), F32)],
        in_specs=[tile(QW),
                  pl.BlockSpec((BLK, 2 * KVW), lambda t: (jnp.clip(2 * t - 1, 0, 2 * nt - 1), 0)),
                  tile(2 * KVW), tile(QW),
                  _row((NH, BLK, 2 * BLK)),
                  pl.BlockSpec(memory_space=pltpu.SMEM),
                  _row((1, QW))],
        out_specs=[tile(QW),
                   pl.BlockSpec((2 * BLK, 2 * KVW), lambda t: (jnp.maximum(t - 1, 0), 0)),
                   _row((NH, BLK, 2 * BLK)), _row((NH, 128)), _row((8, QW))],
        scratch_shapes=[pltpu.VMEM((BLK, 2 * KVW), F32), pltpu.VMEM((BLK, 2 * KVW), F32)], carry=carry)


def _rel_bias_grad(dbias, bucket):
    def body(db_ref, bk_ref, o_ref):
        bk = bk_ref[...]
        lane = lax.broadcasted_iota(jnp.int32, (1, 128), 1)
        for b in range(NBUCKET):
            sel = bk == b
            row = jnp.zeros((1, 128), F32)
            for h in range(NH):
                row = row + jnp.where(lane == h, jnp.sum(jnp.where(sel, db_ref[h], 0.0)), 0.0)
            o_ref[b:b + 1, :] = row

    return pl.pallas_call(
        body, name="rel_bias_grad",
        out_shape=jax.ShapeDtypeStruct((NBUCKET, 128), F32),
    )(dbias, bucket)


def _ssd_consts():
    head_of_lane = np.arange(SW) // HD
    expand = (np.arange(128)[:, None] == head_of_lane[None, :]).astype(np.float32)
    tril = np.tril(np.ones((BLK, BLK), np.float32))
    return (jnp.asarray(expand, BF16), jnp.asarray(expand.T.copy(), BF16), jnp.asarray(tril, BF16),
            jnp.asarray(tril.T.copy(), BF16))


def _ssd_chunk_fwd(c, z_ref, xc_ref, xp_ref, dtr_ref, cw_ref, cb_ref, dtb_ref, a_ref, dk_ref, ex_ref, tril_ref, h_in):
    halo = jnp.where(c == 0, 0.0, xp_ref[BLK - 8:BLK, :])
    ext = jnp.concatenate([halo, xc_ref[...]], axis=0)
    cw = cw_ref[...]
    taps = [ext[8:8 + BLK] if k == CK - 1 else pltpu.roll(ext, CK - 1 - k, 0)[8:8 + BLK] for k in range(CK)]
    pre = cb_ref[...] + sum(cw[k:k + 1, :] * taps[k] for k in range(CK))
    sp = _sig(pre)
    xbc = pre * sp
    xs, bm, cm = xbc[:, 0:SW], xbc[:, SW:SW + 2 * NST], xbc[:, SW + 2 * NST:]
    dtin = dtr_ref[...] + dtb_ref[...]
    dt = jnp.maximum(dtin, 0.0) + jnp.log1p(jnp.exp(-jnp.abs(dtin)))
    av = a_ref[...]
    cs = _sel_l(tril_ref[...], dt * av)
    cst = cs.T
    ex = ex_ref[...]
    dtx = _sel_r(dt, ex)
    csx = _sel_r(cs, ex)
    xdt = xs * dtx
    csl = csx[BLK - 1:BLK, :]
    decx = jnp.exp(csl - csx)
    ecsx = jnp.exp(csx)
    ecl = jnp.exp(csl)
    causal = tril_ref[...].astype(F32) > 0.5
    ydiag, yoff, cbs, lms = [], [], [], []
    for g in range(2):
        bg = bm[:, g * NST:(g + 1) * NST].astype(BF16)
        cg = cm[:, g * NST:(g + 1) * NST].astype(BF16)
        cb = _mm_nt(cg, bg)
        cbs.append(cb)
        yoff.append(_mm(cg, h_in[:, g * 256:(g + 1) * 256].astype(BF16)))
        for r in range(4):
            h = 4 * g + r
            seg = cs[:, h:h + 1] - cst[h:h + 1, :]
            lm = jnp.where(causal, jnp.exp(jnp.minimum(seg, 0.0)), 0.0)
            lms.append(lm)
            ydiag.append(_mm((cb * lm).astype(BF16), xdt[:, h * HD:(h + 1) * HD].astype(BF16)))
    yoff = jnp.concatenate(yoff, axis=1) * ecsx
    y = jnp.concatenate(ydiag, axis=1) + yoff + dk_ref[...] * xs
    return dict(ext=ext, taps=taps, pre=pre, sp=sp, xs=xs, bm=bm, cm=cm, dtin=dtin, dt=dt, av=av, cs=cs, cst=cst,
                dtx=dtx, csx=csx, xdt=xdt, decx=decx, ecsx=ecsx, ecl=ecl, causal=causal, cbs=cbs, lms=lms,
                yoff=yoff, y=y)


def _group_mean(t):
    m0 = jnp.mean(t[:, 0:256], axis=-1, keepdims=True)
    m1 = jnp.mean(t[:, 256:512], axis=-1, keepdims=True)
    return jnp.concatenate([jnp.broadcast_to(m0, (t.shape[0], 256)), jnp.broadcast_to(m1, (t.shape[0], 256))], axis=1)


def _ssd_specs(nc, rev):
    def cur(w):
        return pl.BlockSpec((BLK, w), (lambda i: (nc - 1 - i, 0)) if rev else (lambda i: (i, 0)))
    prev = pl.BlockSpec((BLK, XBCW), (lambda i: (jnp.maximum(nc - 2 - i, 0), 0)) if rev
                        else (lambda i: (jnp.maximum(i - 1, 0), 0)))
    return cur, prev


def _ssd_fwd(z, xbc, dtr, cw, cb, dtb, av, dk, nw, carry=None):
    s = z.shape[0]
    nc = s // BLK
    ex, _, tril, _ = _ssd_consts()

    def body(z_ref, xc_ref, xp_ref, dtr_ref, cw_ref, cb_ref, dtb_ref, a_ref, dk_ref, nw_ref, ex_ref, tril_ref,
             y_ref, hs_ref, h_ref):
        c = pl.program_id(0)

        @pl.when(c == 0)
        def _():
            h_ref[...] = jnp.zeros_like(h_ref)

        h_in = h_ref[...]
        hs_ref[0] = h_in
        f = _ssd_chunk_fwd(c, z_ref, xc_ref, xp_ref, dtr_ref, cw_ref, cb_ref, dtb_ref, a_ref, dk_ref, ex_ref,
                           tril_ref, h_in)
        dx = (f["decx"] * f["xdt"]).astype(BF16)
        st = [_mm_tn(f["bm"][:, g * NST:(g + 1) * NST].astype(BF16), dx[:, g * 256:(g + 1) * 256]) for g in range(2)]
        h_ref[...] = h_in * f["ecl"] + jnp.concatenate(st, axis=1)
        zv = z_ref[...]
        t = f["y"] * (zv * _sig(zv))
        r = lax.rsqrt(_group_mean(t * t) + EPS)
        y_ref[...] = (t * r * nw_ref[...]).astype(BF16)

    cur, prev = _ssd_specs(nc, False)
    return _pcall(
        body, [z, xbc, xbc, dtr, cw, cb, dtb, av, dk, nw, ex, tril], name="ssd_fwd", grid=(nc,),
        out_shape=[jax.ShapeDtypeStruct((s, SW), BF16), jax.ShapeDtypeStruct((nc, NST, SW), F32)],
        in_specs=[cur(SW), cur(XBCW), prev, cur(128), _row((8, XBCW)), _row((1, XBCW)), _row((1, 128)),
                  _row((1, 128)), _row((1, SW)), _row((1, SW)), _row((128, SW)), _row((BLK, BLK))],
        out_specs=[cur(SW), pl.BlockSpec((1, NST, SW), lambda i: (i, 0, 0))],
        scratch_shapes=[pltpu.VMEM((NST, SW), F32)], carry=carry)


def _ssd_bwd(z, xbc, dtr, dys, hs, cw, cb, dtb, av, dk, nw, carry=None):
    s = z.shape[0]
    nc = s // BLK
    ex, ext_t, tril, triu = _ssd_consts()

    def body(z_ref, xc_ref, xp_ref, dtr_ref, dy_ref, hs_ref, cw_ref, cb_ref, dtb_ref, a_ref, dk_ref, nw_ref,
             ex_ref, ext_ref, tril_ref, triu_ref,
             dz_ref, dxbc_ref, ddt_ref, dcw_ref, dcb_ref, dnw_ref, dhd_ref, dh_ref, nxt_ref, dd_ref):
        i = pl.program_id(0)
        c = nc - 1 - i

        @pl.when(i == 0)
        def _():
            dh_ref[...] = jnp.zeros_like(dh_ref)
            nxt_ref[...] = jnp.zeros_like(nxt_ref)
            dd_ref[...] = jnp.zeros_like(dd_ref)
            dcw_ref[...] = jnp.zeros_like(dcw_ref)
            dcb_ref[...] = jnp.zeros_like(dcb_ref)
            dnw_ref[...] = jnp.zeros_like(dnw_ref)
            dhd_ref[...] = jnp.zeros_like(dhd_ref)

        h_in = hs_ref[0]
        f = _ssd_chunk_fwd(c, z_ref, xc_ref, xp_ref, dtr_ref, cw_ref, cb_ref, dtb_ref, a_ref, dk_ref, ex_ref,
                           tril_ref, h_in)
        xs, xdt, decx, ecsx, ecl, dtx = f["xs"], f["xdt"], f["decx"], f["ecsx"], f["ecl"], f["dtx"]
        cs, cst, causal = f["cs"], f["cst"], f["causal"]
        causal_t = triu_ref[...].astype(F32) > 0.5

        zv = z_ref[...]
        sz = _sig(zv)
        gz = zv * sz
        t = f["y"] * gz
        r = lax.rsqrt(_group_mean(t * t) + EPS)
        tn_ = t * r
        dyn = dy_ref[...]
        dnw_ref[0:1, :] += jnp.sum(dyn * tn_, axis=0, keepdims=True)
        u = dyn * nw_ref[...]
        dt_ = r * u - tn_ * (r * _group_mean(u * tn_))
        dy = dt_ * gz
        dz_ref[...] = (dt_ * f["y"] * (sz * (1.0 + zv * (1.0 - sz)))).astype(BF16)

        dd_ref[0:1, :] += jnp.sum(dy * xs, axis=0, keepdims=True)
        dxs = dk_ref[...] * dy

        gst = dh_ref[...]
        edy = ecsx * dy
        dxdt, dbs, dcs_, dcsx_parts, dh_new = [], [], [], [], []
        lane = lax.broadcasted_iota(jnp.int32, (1, 128), 1)
        dcs_intra = jnp.zeros((BLK, 128), F32)
        for g in range(2):
            sl = slice(g * 256, (g + 1) * 256)
            bgf, cgf = f["bm"][:, g * NST:(g + 1) * NST], f["cm"][:, g * NST:(g + 1) * NST]
            bg, cg = bgf.astype(BF16), cgf.astype(BF16)
            gg = gst[:, sl].astype(BF16)
            hg = h_in[:, sl].astype(BF16)
            edyg = edy[:, sl].astype(BF16)
            dc = _mm_nt(edyg, hg)
            dh_new.append(gst[:, sl] * ecl[:, sl] + _mm_tn(cg, edyg))
            bgm = _mm(bg, gg)
            dxdt_g = decx[:, sl] * bgm
            dxg = (decx[:, sl] * xdt[:, sl]).astype(BF16)
            db = _mm_nt(dxg, gg)
            qd = bgm * xdt[:, sl] * decx[:, sl]
            last = jnp.sum(qd, axis=0, keepdims=True) + ecl[:, sl] * jnp.sum(gst[:, sl] * h_in[:, sl], axis=0, keepdims=True)
            rowid = lax.broadcasted_iota(jnp.int32, (BLK, 256), 0)
            dcsx_parts.append(f["yoff"][:, sl] * dy[:, sl] - qd + jnp.where(rowid == BLK - 1, last, 0.0))
            cb_ = f["cbs"][g]
            cbt = _mm_nt(bg, cg)
            dcb_ = jnp.zeros((BLK, BLK), F32)
            dcbt = jnp.zeros((BLK, BLK), F32)
            dxd = []
            for r_ in range(4):
                h = 4 * g + r_
                hl = slice(h * HD, (h + 1) * HD)
                lm = f["lms"][h]
                segt = cst[h:h + 1, :] - cs[:, h:h + 1]
                lmt = jnp.where(causal_t, jnp.exp(jnp.minimum(segt, 0.0)), 0.0)
                dyh = dy[:, hl].astype(BF16)
                xdh = xdt[:, hl].astype(BF16)
                dw = _mm_nt(dyh, xdh)
                dwt = _mm_nt(xdh, dyh)
                wt = cbt * lmt
                dxd.append(_mm(wt.astype(BF16), dyh))
                dcb_ = dcb_ + dw * lm
                dcbt = dcbt + dwt * lmt
                col = jnp.sum(dw * (cb_ * lm), axis=-1, keepdims=True) - jnp.sum(dwt * wt, axis=-1, keepdims=True)
                dcs_intra = dcs_intra + jnp.where(lane == h, col, 0.0)
            dxdt.append(dxdt_g + jnp.concatenate(dxd, axis=1))
            dcs_.append(dc + _mm(dcb_.astype(BF16), bg))
            dbs.append(db + _mm(dcbt.astype(BF16), cg))
        dh_ref[...] = jnp.concatenate(dh_new, axis=1)
        dxdt = jnp.concatenate(dxdt, axis=1)
        dxs = dxs + dxdt * dtx
        ext_t_ = ext_ref[...]
        dcs = dcs_intra + _sel_r(jnp.concatenate(dcsx_parts, axis=1), ext_t_)
        da = _sel_l(triu_ref[...], dcs)
        ddt = da * f["av"] + _sel_r(dxdt * xs, ext_t_)
        dhd_ref[1:2, :] += jnp.sum(da * f["dt"], axis=0, keepdims=True)
        ddtr = ddt * _sig(f["dtin"])
        dhd_ref[0:1, :] += jnp.sum(ddtr, axis=0, keepdims=True)
        ddt_ref[...] = ddtr.astype(BF16)

        sp, pre = f["sp"], f["pre"]
        dact = jnp.concatenate([dxs] + dbs + dcs_, axis=1)
        dpre = dact * (sp * (1.0 + pre * (1.0 - sp)))
        dcb_ref[0:1, :] += jnp.sum(dpre, axis=0, keepdims=True)
        for k in range(CK):
            dcw_ref[k:k + 1, :] += jnp.sum(dpre * f["taps"][k], axis=0, keepdims=True)
        ext2 = jnp.concatenate([dpre, nxt_ref[...]], axis=0)
        cw = cw_ref[...]
        dxr = cw[CK - 1:CK, :] * dpre
        for k in range(CK - 1):
            dxr = dxr + cw[k:k + 1, :] * pltpu.roll(ext2, BLK + 8 - (CK - 1 - k), 0)[0:BLK]
        dxbc_ref[...] = dxr.astype(BF16)
        nxt_ref[...] = dpre[0:8]

        @pl.when(i == nc - 1)
        def _():
            dhd_ref[2:3, :] = _sel_r(dd_ref[...], ext_t_)[0:1, :]

    cur, prev = _ssd_specs(nc, True)
    return _pcall(
        body, [z, xbc, xbc, dtr, dys, hs, cw, cb, dtb, av, dk, nw, ex, ext_t, tril, triu], name="ssd_bwd", grid=(nc,),
        out_shape=[jax.ShapeDtypeStruct((s, SW), BF16), jax.ShapeDtypeStruct((s, XBCW), BF16),
                   jax.ShapeDtypeStruct((s, 128), BF16), jax.ShapeDtypeStruct((8, XBCW), F32),
                   jax.ShapeDtypeStruct((8, XBCW), F32), jax.ShapeDtypeStruct((8, SW), F32),
                   jax.ShapeDtypeStruct((8, 128), F32)],
        in_specs=[cur(SW), cur(XBCW), prev, cur(128), cur(SW),
                  pl.BlockSpec((1, NST, SW), lambda i: (nc - 1 - i, 0, 0)),
                  _row((8, XBCW)), _row((1, XBCW)), _row((1, 128)), _row((1, 128)), _row((1, SW)), _row((1, SW)),
                  _row((128, SW)), _row((SW, 128)), _row((BLK, BLK)), _row((BLK, BLK))],
        out_specs=[cur(SW), cur(XBCW), cur(128), _row((8, XBCW)), _row((8, XBCW)), _row((8, SW)), _row((8, 128))],
        scratch_shapes=[pltpu.VMEM((NST, SW), F32), pltpu.VMEM((8, XBCW), F32), pltpu.VMEM((8, SW), F32)], carry=carry)


def _load_once(i, pairs, sem):
    @pl.when(i == 0)
    def _():
        cps = [pltpu.make_async_copy(src, dst, sem.at[k]) for k, (src, dst) in enumerate(pairs)]
        for cp in cps:
            cp.start()
        for cp in cps:
            cp.wait()


def _mlp_fwd(x, ya, ys, tgt, w_o, w_ga, w_gb, w_dn, gate1, a2, sh2, gate2, fn):
    s = x.shape[0]
    tm = 256

    def body(x_ref, ya_ref, ys_ref, t_ref, wo_hbm, wga_hbm, wgb_hbm, wdn_hbm, g1_ref, a2_ref, s2_ref, g2_ref, fn_ref,
             x1_ref, gu_ref, dx2_ref, loss_ref, dfn_ref, wo, wga, wgb, wdn, sem):
        i = pl.program_id(0)
        _load_once(i, [(wo_hbm, wo), (wga_hbm, wga), (wgb_hbm, wgb), (wdn_hbm, wdn)], sem)

        @pl.when(i == 0)
        def _():
            loss_ref[...] = jnp.zeros_like(loss_ref)
            dfn_ref[...] = jnp.zeros_like(dfn_ref)

        mix = _mm(ya_ref[...], wo[0:QW, :]) + _mm(ys_ref[...], wo[QW:D, :])
        x1 = x_ref[...] + g1_ref[...] * mix
        x1_ref[...] = x1
        r2 = lax.rsqrt(jnp.mean(x1 * x1, axis=-1, keepdims=True) + EPS)
        h2 = (x1 * r2 * a2_ref[...] + s2_ref[...]).astype(BF16)
        ha, hb = h2[:, 0:D // 2], h2[:, D // 2:D]
        gub = jnp.concatenate([(_mm(ha, wga[j]) + _mm(hb, wgb[j])).astype(BF16) for j in range(4)], axis=1)
        gu_ref[...] = gub
        gv, uv = gub[:, 0:DFF].astype(F32), gub[:, DFF:].astype(F32)
        act = (gv * _sig(gv) * uv).astype(BF16)
        x2 = x1 + g2_ref[...] * _mm(act, wdn[...])
        r3 = lax.rsqrt(jnp.mean(x2 * x2, axis=-1, keepdims=True) + EPS)
        xn = x2 * r3
        fnv = fn_ref[...]
        err = xn * fnv - t_ref[...]
        loss_ref[...] += jnp.sum(err * err) * (0.5 / D)
        dy = err * (1.0 / D)
        dfn_ref[0:1, :] += jnp.sum(dy * xn, axis=0, keepdims=True)
        u = dy * fnv
        dx2_ref[...] = r3 * u - xn * (r3 * jnp.mean(u * xn, axis=-1, keepdims=True))

    def tok(w):
        return pl.BlockSpec((tm, w), lambda i: (i, 0))

    hbm = pl.BlockSpec(memory_space=pl.ANY)
    return pl.pallas_call(
        body, name="mlp_fwd", grid=(s // tm,),
        out_shape=[jax.ShapeDtypeStruct((s, D), F32), jax.ShapeDtypeStruct((s, 2 * DFF), BF16),
                   jax.ShapeDtypeStruct((s, D), F32), jax.ShapeDtypeStruct((8, 128), F32),
                   jax.ShapeDtypeStruct((8, D), F32)],
        in_specs=[tok(D), tok(QW), tok(SW), tok(D), hbm, hbm, hbm, hbm,
                  _row((1, D)), _row((1, D)), _row((1, D)), _row((1, D)), _row((1, D))],
        out_specs=[tok(D), tok(2 * DFF), tok(D), _row((8, 128)), _row((8, D))],
        scratch_shapes=[pltpu.VMEM((D, D), BF16), pltpu.VMEM(w_ga.shape, BF16), pltpu.VMEM(w_gb.shape, BF16),
                        pltpu.VMEM((DFF, D), BF16), pltpu.SemaphoreType.DMA((4,))],
        compiler_params=_cp(("arbitrary",)),
    )(x, ya, ys, tgt, w_o, w_ga, w_gb, w_dn, gate1, a2, sh2, gate2, fn)


def _mlp_bwd(x1, gu, dx2, w_o, w_ga, w_gb, w_dn, gate1, a2, sh2, gate2):
    s = x1.shape[0]
    tm = 256
    nj = 2 * DFF // 4

    def body(x1_ref, gu_ref, dx2_ref, wo_hbm, wga_hbm, wgb_hbm, wdn_hbm, g1_ref, a2_ref, s2_ref, g2_ref,
             dx1_ref, dya_ref, dys_ref, act_ref, dgu_ref, h2_ref, dsh_ref, p_ref, wo, wga, wgb, wdn, sem):
        i = pl.program_id(0)
        _load_once(i, [(wo_hbm, wo), (wga_hbm, wga), (wgb_hbm, wgb), (wdn_hbm, wdn)], sem)

        @pl.when(i == 0)
        def _():
            dsh_ref[...] = jnp.zeros_like(dsh_ref)
            p_ref[...] = jnp.zeros_like(p_ref)

        dx2 = dx2_ref[...]
        dact = _mm_nt((dx2 * g2_ref[...]).astype(BF16), wdn[...])
        gub = gu_ref[...]
        gv, uv = gub[:, 0:DFF].astype(F32), gub[:, DFF:].astype(F32)
        sg = _sig(gv)
        sl = gv * sg
        act_ref[...] = (sl * uv).astype(BF16)
        dgu = jnp.concatenate([dact * uv * (sg * (1.0 + gv * (1.0 - sg))), dact * sl], axis=1).astype(BF16)
        dgu_ref[...] = dgu
        dha = sum(_mm_nt(dgu[:, j * nj:(j + 1) * nj], wga[j]) for j in range(4))
        dhb = sum(_mm_nt(dgu[:, j * nj:(j + 1) * nj], wgb[j]) for j in range(4))
        dh = jnp.concatenate([dha, dhb], axis=1)
        x1 = x1_ref[...]
        r2 = lax.rsqrt(jnp.mean(x1 * x1, axis=-1, keepdims=True) + EPS)
        xn = x1 * r2
        a2 = a2_ref[...]
        h2_ref[...] = (xn * a2 + s2_ref[...]).astype(BF16)
        dsh_ref[0:1, :] += jnp.sum(dh, axis=0, keepdims=True)
        p_ref[0:1, :] += jnp.sum(dh * xn, axis=0, keepdims=True)
        u = dh * a2
        dx1 = dx2 + r2 * u - xn * (r2 * jnp.mean(u * xn, axis=-1, keepdims=True))
        dx1_ref[...] = dx1
        dcat = _mm_nt((dx1 * g1_ref[...]).astype(BF16), wo[...])
        dya_ref[...] = dcat[:, 0:QW]
        dys_ref[...] = dcat[:, QW:D]

    def tok(w):
        return pl.BlockSpec((tm, w), lambda i: (i, 0))

    hbm = pl.BlockSpec(memory_space=pl.ANY)
    return pl.pallas_call(
        body, name="mlp_bwd", grid=(s // tm,),
        out_shape=[jax.ShapeDtypeStruct((s, D), F32), jax.ShapeDtypeStruct((s, QW), F32),
                   jax.ShapeDtypeStruct((s, SW), F32), jax.ShapeDtypeStruct((s, DFF), BF16),
                   jax.ShapeDtypeStruct((s, 2 * DFF), BF16), jax.ShapeDtypeStruct((s, D), BF16),
                   jax.ShapeDtypeStruct((8, D), F32), jax.ShapeDtypeStruct((8, D), F32)],
        in_specs=[tok(D), tok(2 * DFF), tok(D), hbm, hbm, hbm, hbm, _row((1, D)), _row((1, D)), _row((1, D)), _row((1, D))],
        out_specs=[tok(D), tok(QW), tok(SW), tok(DFF), tok(2 * DFF), tok(D), _row((8, D)), _row((8, D))],
        scratch_shapes=[pltpu.VMEM((D, D), BF16), pltpu.VMEM(w_ga.shape, BF16), pltpu.VMEM(w_gb.shape, BF16),
                        pltpu.VMEM((DFF, D), BF16), pltpu.SemaphoreType.DMA((4,))],
        compiler_params=_cp(("arbitrary",)),
    )(x1, gu, dx2, w_o, w_ga, w_gb, w_dn, gate1, a2, sh2, gate2)


def _wgrad(name, a, b, tn, gate=None, w=None, stacked=False, carry=None):
    s, m = a.shape
    n = b.shape[1]
    tk = min(1024, s)
    nk = s // tk

    def body(*refs):
        if gate is None:
            a_ref, b_ref, o_ref = refs
        else:
            a_ref, b_ref, g_ref, w_ref, o_ref, dg_ref = refs
        k = pl.program_id(1)

        @pl.when(k == 0)
        def _():
            o_ref[...] = jnp.zeros_like(o_ref)

        o_ref[...] += _mm_tn(a_ref[...], b_ref[...].astype(BF16))

        if gate is not None:
            @pl.when(k == nk - 1)
            def _():
                acc = o_ref[...]
                dg_ref[...] = jnp.zeros_like(dg_ref)
                dg_ref[0:1, :] = jnp.sum(acc * w_ref[...].astype(F32), axis=0, keepdims=True)
                o_ref[...] = acc * g_ref[...]

    in_specs = [pl.BlockSpec((tk, m), lambda j, k: (k, 0)), pl.BlockSpec((tk, tn), lambda j, k: (k, j))]
    if stacked:
        out_shape = [jax.ShapeDtypeStruct((n // tn, m, tn), F32)]
        out_specs = [pl.BlockSpec((None, m, tn), lambda j, k: (j, 0, 0))]
    else:
        out_shape = [jax.ShapeDtypeStruct((m, n), F32)]
        out_specs = [pl.BlockSpec((m, tn), lambda j, k: (0, j))]
    args = [a, b]
    if gate is not None:
        in_specs += [pl.BlockSpec((1, tn), lambda j, k: (0, j)), pl.BlockSpec((m, tn), lambda j, k: (0, j))]
        out_shape.append(jax.ShapeDtypeStruct((8, n), F32))
        out_specs.append(pl.BlockSpec((8, tn), lambda j, k: (0, j)))
        args += [gate, w]
    return _pcall(body, args, name=name, grid=(n // tn, nk), out_shape=out_shape, in_specs=in_specs,
                  out_specs=out_specs, carry=carry)


def _wgrad_in_t(h1, dproj, carry=None):
    s = h1.shape[0]
    tk = min(1024, s)
    nk = s // tk

    def body(a_ref, b_ref, o_hbm, acc_ref, tr_ref, sem):
        k = pl.program_id(0)

        @pl.when(k == 0)
        def _():
            acc_ref[...] = jnp.zeros_like(acc_ref)

        acc_ref[...] += _mm_tn(a_ref[...], b_ref[...])

        @pl.when(k == nk - 1)
        def _():
            for j in range(PROJ_W // 128):
                tr_ref[j * 128:(j + 1) * 128, :] = acc_ref[:, j * 128:(j + 1) * 128].T
            cp = pltpu.make_async_copy(tr_ref, o_hbm, sem)
            cp.start()
            cp.wait()

    return _pcall(body, [h1, dproj], name="wgrad_in", grid=(nk,),
                  out_shape=[jax.ShapeDtypeStruct((PROJ_W, D), F32)],
                  in_specs=[pl.BlockSpec((tk, D), lambda k: (k, 0)), pl.BlockSpec((tk, PROJ_W), lambda k: (k, 0))],
                  out_specs=[pl.BlockSpec(memory_space=pl.ANY)],
                  scratch_shapes=[pltpu.VMEM((D, PROJ_W), F32), pltpu.VMEM((PROJ_W, D), F32), pltpu.SemaphoreType.DMA],
                  carry=carry)


_SMALL = ["ada_b", "norm1", "conv_w", "conv_b", "dt_bias", "A_log", "D_skip", "sinks", "attn_out_norm",
          "ssm_out_norm", "norm2", "rel_bias", "final_norm"]


def _small_grad(name, gs, chip):
    if name == "ada_b":
        return jnp.concatenate([gs[j:j + 1, :] for j in range(6)], axis=1)
    if name == "conv_w":
        full = gs[7:11, :]
        out = full[:, 0:256]
        for j in range(1, 4):
            out = jnp.where(chip == j, full[:, j * 256:(j + 1) * 256], out)
        return out
    row, width = {"norm1": (6, D), "conv_b": (11, D), "norm2": (12, D), "final_norm": (13, D),
                  "attn_out_norm": (14, QW), "ssm_out_norm": (15, SW), "dt_bias": (16, NH), "A_log": (17, NH),
                  "D_skip": (18, NH), "sinks": (19, NH), "rel_bias": (24, NH)}[name]
    rows = NBUCKET if name == "rel_bias" else 1
    return gs[row:row + rows, 0:width]


def _small_update(small_all, where, ws, ms, vs):
    n = len(_SMALL)

    def body(where_ref, sa_ref, *refs):
        w_refs, m_refs, v_refs, outs = refs[:n], refs[n:2 * n], refs[2 * n:3 * n], refs[3 * n:]
        gs = sa_ref[0]
        for b in range(1, 8):
            gs = gs + sa_ref[b]
        chip = where_ref[1]
        for i, name in enumerate(_SMALL):
            g = _small_grad(name, gs, chip)
            lead = (0,) if name == "conv_w" else ()
            d, mo, vo = _adamw(w_refs[i][lead + (...,)], g, m_refs[i][lead + (...,)], v_refs[i][lead + (...,)])
            for k, val in enumerate((g, d, mo, vo)):
                outs[k * n + i][lead + (...,)] = val

    shapes = [jax.ShapeDtypeStruct(w.shape, F32) for w in ws]
    vmem = pl.BlockSpec(memory_space=pltpu.VMEM)
    res = pl.pallas_call(
        body, name="small_update", out_shape=shapes * 4,
        in_specs=[pl.BlockSpec(memory_space=pltpu.SMEM)] + [vmem] * (1 + 3 * n), out_specs=[vmem] * (4 * n),
    )(where, small_all, *ws, *ms, *vs)
    return [res[k * n:(k + 1) * n] for k in range(4)]


def _add_half(name, g, got, where, by_cols=False):
    rr, cc = got.shape[1:]
    if by_cols:
        mine = pl.BlockSpec((None, rr, cc), lambda i, w_ref: (i, 0, w_ref[0]))
    else:
        mine = pl.BlockSpec((None, None, rr, cc), lambda i, w_ref: (i, w_ref[0], 0, 0))

    def body(w_ref, g_ref, r_ref, o_ref, own_ref):
        s = g_ref[...] + r_ref[...]
        o_ref[...] = s.astype(BF16)

        @pl.when(pl.program_id(0) == w_ref[1])
        def _():
            own_ref[...] = s

    spec = pl.BlockSpec((None, rr, cc), lambda i, w_ref: (i, 0, 0))
    return _pcall(body, [where, g, got], name=name, grid=(4,), nprefetch=1,
                  out_shape=[jax.ShapeDtypeStruct(got.shape, BF16), jax.ShapeDtypeStruct((rr, cc), F32)],
                  in_specs=[mine, spec],
                  out_specs=[spec, pl.BlockSpec((rr, cc), lambda i, w_ref: (0, 0))])


def _add_chips(name, own, got):
    rr, cc = own.shape
    tr = rr // 2 if rr % 32 == 0 else rr

    def body(s_ref, r_ref, o_ref):
        o_ref[...] = ((s_ref[...] + r_ref[0].astype(F32)) + r_ref[1].astype(F32)) + r_ref[2].astype(F32)

    spec = pl.BlockSpec((tr, cc), lambda i: (i, 0))
    return _pcall(body, [own, got], name=name, grid=(rr // tr,), out_shape=[jax.ShapeDtypeStruct((rr, cc), F32)],
                  in_specs=[spec, pl.BlockSpec((3, tr, cc), lambda i: (0, i, 0))], out_specs=[spec])[0]


def _adamw_halves(name, mine, got, w, m, v, where, by_cols=False):
    rr, cc = mine.shape

    def body(w_ref_, t_ref, r_ref, w_ref, m_ref, v_ref, g_ref, d_ref, mo_ref, vo_ref):
        g = jnp.where(pl.program_id(0) == w_ref_[0], t_ref[...], r_ref[...])
        g_ref[...] = g
        d_ref[...], mo_ref[...], vo_ref[...] = _adamw(w_ref[...], g, m_ref[...], v_ref[...])

    if by_cols:
        grid = (2, 1)
        half = pl.BlockSpec((rr, cc), lambda h, i, w_ref_: (0, 0))
        full = pl.BlockSpec((rr, cc), lambda h, i, w_ref_: (0, h))
    else:
        tr = rr // 2
        grid = (2, 2)
        half = pl.BlockSpec((tr, cc), lambda h, i, w_ref_: (i, 0))
        full = pl.BlockSpec((None, tr, cc), lambda h, i, w_ref_: (0, 2 * h + i, 0))
    return _pcall(body, [where, mine, got, w, m, v], name=name, grid=grid, nprefetch=1,
                  out_shape=[jax.ShapeDtypeStruct(w.shape, F32)] * 4,
                  in_specs=[half, half, full, full, full], out_specs=[full] * 4)


def _bias_table(rel_bias, bucket, mask):
    def body(rb_ref, bk_ref, mk_ref, o_ref):
        bk = bk_ref[...]
        valid = mk_ref[...] > 0
        for h in range(NH):
            acc = jnp.zeros((BLK, 2 * BLK), F32)
            for b in range(NBUCKET):
                acc = jnp.where(bk == b, rb_ref[b, h], acc)
            o_ref[h] = jnp.where(valid, acc, NEG)

    vmem = pl.BlockSpec(memory_space=pltpu.VMEM)
    return pl.pallas_call(
        body, name="bias_table", out_shape=jax.ShapeDtypeStruct((NH, BLK, 2 * BLK), F32),
        in_specs=[pl.BlockSpec(memory_space=pltpu.SMEM), vmem, vmem], out_specs=vmem,
    )(rel_bias, bucket, mask)


def _pack_small(dsh1, p1, dsh2, p2, dg1a, dg1b, dg2, norm1, norm2, scale1, scale2, dcw, dcb, dfn,
                dnw_attn, dnw_ssm, dhd, av, dsink, drel):
    def body(dsh1_ref, p1_ref, dsh2_ref, p2_ref, dg1a_ref, dg1b_ref, dg2_ref, n1_ref, n2_ref, s1_ref, s2_ref,
             dcw_ref, dcb_ref, dfn_ref, da_ref, ds_ref, dhd_ref, av_ref, dsink_ref, drel_ref, o_ref):
        o_ref[...] = jnp.zeros_like(o_ref)
        p1v, p2v = p1_ref[0:1, :], p2_ref[0:1, :]
        o_ref[0:1, :] = dsh1_ref[0:1, :]
        o_ref[1:2, :] = p1v * n1_ref[...]
        o_ref[2:3, :] = dg1a_ref[0:1, :] + dg1b_ref[0:1, :]
        o_ref[3:4, :] = dsh2_ref[0:1, :]
        o_ref[4:5, :] = p2v * n2_ref[...]
        o_ref[5:6, :] = dg2_ref[0:1, :]
        o_ref[6:7, :] = p1v * (1.0 + s1_ref[...])
        o_ref[7:11, :] = dcw_ref[0:4, :]
        o_ref[11:12, :] = dcb_ref[0:1, :]
        o_ref[12:13, :] = p2v * (1.0 + s2_ref[...])
        o_ref[13:14, :] = dfn_ref[0:1, :]
        o_ref[14:15, 0:QW] = da_ref[0:1, :]
        o_ref[15:16, 0:SW] = ds_ref[0:1, :]
        o_ref[16:17, 0:128] = dhd_ref[0:1, :]
        o_ref[17:18, 0:128] = dhd_ref[1:2, :] * av_ref[...]
        o_ref[18:19, 0:128] = dhd_ref[2:3, :]
        o_ref[19:20, 0:128] = dsink_ref[0:1, :]
        o_ref[24:56, 0:128] = drel_ref[...]

    return pl.pallas_call(body, name="pack_small", out_shape=jax.ShapeDtypeStruct((56, D), F32))(
        dsh1, p1, dsh2, p2, dg1a, dg1b, dg2, norm1, norm2, scale1, scale2, dcw, dcb, dfn,
        dnw_attn, dnw_ssm, dhd, av, dsink, drel)


def _pad_row(a, rows=1):
    return jnp.pad(a.reshape(rows, -1), ((0, 0), (0, D - a.size // rows)))


def kernel(x, c, ada_w, ada_b, norm1, w_in, conv_w, conv_b, dt_bias, A_log, D_skip, sinks, attn_out_norm, ssm_out_norm, w_o, norm2, w_gate_up, w_down, rel_bias, final_norm, loss_target, m_ada_w, m_ada_b, m_norm1, m_w_in, m_conv_w, m_conv_b, m_dt_bias, m_A_log, m_D_skip, m_sinks, m_attn_out_norm, m_ssm_out_norm, m_w_o, m_norm2, m_w_gate_up, m_w_down, m_rel_bias, m_final_norm, v_ada_w, v_ada_b, v_norm1, v_w_in, v_conv_w, v_conv_b, v_dt_bias, v_A_log, v_D_skip, v_sinks, v_attn_out_norm, v_ssm_out_norm, v_w_o, v_norm2, v_w_gate_up, v_w_down, v_rel_bias, v_final_norm):
    xi, yi, ci = lax.axis_index("x"), lax.axis_index("y"), lax.axis_index("c")
    chip = 2 * xi + yi
    me = 4 * xi + 2 * yi + ci
    where = jnp.stack([ci, chip]).astype(jnp.int32)
    xs2, tgt = x[0], loss_target[0]

    first = jnp.concatenate([c, _pad_row(conv_w[0], CK), jnp.zeros((3, D), F32)], axis=0)
    w_in_t, m_w_in_t, v_w_in_t = w_in[0].T, m_w_in[0].T, v_w_in[0].T
    w_in_b, w_o_b, w_dn_b = w_in_t.astype(BF16), w_o[0].astype(BF16), w_down[0].astype(BF16)
    w_gu_b = w_gate_up[0].astype(BF16)
    first_all, w_in_g = _exchange("gather_first", _merge(_gather8_carry(first), _gather_chips_carry([w_in_b])))
    c_all = first_all[:, 0, :]
    cw_full = jnp.concatenate([first_all[2 * j, 1:1 + CK, 0:256] for j in range(4)], axis=1)
    w_in_f = jnp.pad(w_in_g.reshape(IN_W, D), ((0, PROJ_W - IN_W), (0, 0)))

    ncol = ada_w.shape[2]
    mod_cols = _ada_fwd(c_all, ada_w[0], lax.dynamic_slice(ada_b, (0, chip * ncol), (1, ncol)))
    mod_all = _exchange("gather_mod", _gather_chips_carry([mod_cols]))[0]
    mod = lax.dynamic_slice(jnp.transpose(mod_all, (1, 0, 2)).reshape(8, 4 * ncol), (me, 0), (1, 4 * ncol))
    shift1, scale1, gate1, shift2, scale2, gate2 = [mod[:, j * D:(j + 1) * D] for j in range(6)]
    a1 = norm1 * (1.0 + scale1)
    a2 = norm2 * (1.0 + scale2)

    hdn = DFF // 8
    q, kv, z, xbc, dtr, w_o_g, w_dna_g = _in_proj_fwd(xs2, a1, shift1, w_in_f,
                                                      carry=_gather_chips_carry([w_o_b, w_dn_b[0:hdn]]))
    w_o_f = w_o_g.reshape(D, D)
    bucket, mask = _attn_geometry()
    bucket = jnp.asarray(bucket)
    bias = _bias_table(rel_bias, bucket, jnp.asarray(mask.astype(np.int32)))
    sinks1 = sinks[0]
    ya, w_ga_g = _attn_fwd(q, kv, bias, sinks1, attn_out_norm, carry=_gather_chips_carry([w_gu_b[0:D // 2]]))
    cw8 = jnp.concatenate([cw_full, jnp.zeros((4, XBCW), F32)], axis=0)
    dtb = _pad_row(dt_bias)[:, 0:128]
    av = _pad_row(-jnp.exp(A_log))[:, 0:128]
    dk = jnp.repeat(D_skip, HD, axis=1)
    ys, hs, w_gb_g, w_dnb_g = _ssd_fwd(z, xbc, dtr, cw8, conv_b, dtb, av, dk, ssm_out_norm,
                                       carry=_gather_chips_carry([w_gu_b[D // 2:D], w_dn_b[hdn:2 * hdn]]))
    w_dn_f = jnp.stack([w_dna_g, w_dnb_g], axis=1).reshape(DFF, D)
    fn = final_norm[None, :]
    x1, gu, dx2, loss_acc, dfn = _mlp_fwd(xs2, ya, ys, tgt, w_o_f, w_ga_g, w_gb_g, w_dn_f, gate1, a2, shift2, gate2, fn)
    loss = lax.psum(loss_acc[0, 0], ("x", "y", "c"))

    def to_sibling(p):
        return _Carry([p], [jax.ShapeDtypeStruct((4,) + p.shape[2:], F32)],
                      lambda x_, y_, c_: [(_SIBLING, 0, (j, 1 - c_), 0, j) for j in range(4)])

    def to_chips(s4):
        return _Carry([s4], [jax.ShapeDtypeStruct((3,) + s4.shape[1:], s4.dtype)],
                      lambda x_, y_, c_: [(f, 0, jnp.bitwise_xor(2 * x_ + y_, k + 1), 0, k) for k, f in enumerate(_CHIPS3)])

    def back(t):
        return _Carry([t[None]], [jax.ShapeDtypeStruct((1,) + t.shape, F32)], lambda x_, y_, c_: [(_SIBLING, 0, 0, 0, 0)])

    dx1, dya, dys, act, dgu, h2, dsh2, p2 = _mlp_bwd(x1, gu, dx2, w_o_f, w_ga_g, w_gb_g, w_dn_f, gate1, a2, shift2, gate2)
    p_gu = _wgrad("wgrad_gate_up", h2, dgu, 2 * DFF // 4, stacked=True)[0].reshape(4, 2, D // 2, 2 * DFF // 4)
    g_dn, dg2, got1_gu = _wgrad("wgrad_down", act, dx2, D // 2, gate2, w_dn_f, carry=to_sibling(p_gu))
    p_dn = g_dn.reshape(4, 2, DFF // 8, D)
    s4_gu, own_gu = _add_half("rs_add_half_gu", p_gu, got1_gu, where)
    dq, dkv, dbias, dsink, dnw_attn, got2_gu, got1_dn = _attn_bwd(
        q, kv, dya, bias, sinks1, attn_out_norm, carry=_merge(to_chips(s4_gu), to_sibling(p_dn)))
    drel = _rel_bias_grad(dbias, bucket)
    mine_gu = _add_chips("rs_add_chips_gu", own_gu, got2_gu)
    s4_dn, own_dn = _add_half("rs_add_half_dn", p_dn, got1_dn, where)
    dz, dxbc, ddt, dcw, dcb, dnw_ssm, dhd, got2_dn, got3_gu = _ssd_bwd(
        z, xbc, dtr, dys, hs, cw8, conv_b, dtb, av, dk, ssm_out_norm, carry=_merge(to_chips(s4_dn), back(mine_gu)))
    mine_dn = _add_chips("rs_add_chips_dn", own_dn, got2_dn)
    grad_x, dproj, h1, dsh1, p1 = _in_proj_bwd(xs2, dx1, a1, shift1, w_in_f, dq, dkv, dz, dxbc, ddt)
    g_in_t, got3_dn = _wgrad_in_t(h1, dproj, carry=back(mine_dn))
    p_in = g_in_t[0:IN_W].reshape(4, IN_W // 4, D)

    def to_sibling_cols(p):
        return _Carry([p], [jax.ShapeDtypeStruct(p.shape[:2] + (D // 2,), F32)],
                      lambda x_, y_, c_: [(_SIBLING, 0, (j, slice(None), pl.ds((1 - c_) * (D // 2), D // 2)), 0, j)
                                          for j in range(4)])

    g_oa, dg1a, got1_in = _wgrad("wgrad_o_attn", ya, dx1, D, gate1, w_o_f[0:QW], carry=to_sibling_cols(p_in))
    s4_in, own_in = _add_half("rs_add_half_in", p_in, got1_in, where, by_cols=True)
    g_os, dg1b, got2_in = _wgrad("wgrad_o_ssm", ys, dx1, D, gate1, w_o_f[QW:D], carry=to_chips(s4_in))
    mine_in = _add_chips("rs_add_chips_in", own_in, got2_in)
    p_o = jnp.concatenate([g_oa, g_os], axis=0).reshape(4, 2, D // 8, D)

    small = _pack_small(dsh1, p1, dsh2, p2, dg1a, dg1b, dg2, norm1, norm2, scale1, scale2, dcw, dcb, dfn,
                        dnw_attn, dnw_ssm, dhd, av, dsink, drel)
    small_all, got1_o, got3_in = _exchange(
        "gather_small", _merge(_gather8_carry(small), to_sibling(p_o), back(mine_in)))
    s4_o, own_o = _add_half("rs_add_half_o", p_o, got1_o, where)
    mine_o = _add_chips("rs_add_chips_o", own_o, _exchange("rs_chips_o", to_chips(s4_o))[0])
    got3_o = _exchange("rs_back_o", back(mine_o))[0]
    small_res = _small_update(
        small_all, where,
        [ada_b, norm1, conv_w, conv_b, dt_bias, A_log, D_skip, sinks, attn_out_norm, ssm_out_norm, norm2, rel_bias,
         final_norm[None, :]],
        [m_ada_b, m_norm1, m_conv_w, m_conv_b, m_dt_bias, m_A_log, m_D_skip, m_sinks, m_attn_out_norm,
         m_ssm_out_norm, m_norm2, m_rel_bias, m_final_norm[None, :]],
        [v_ada_b, v_norm1, v_conv_w, v_conv_b, v_dt_bias, v_A_log, v_D_skip, v_sinks, v_attn_out_norm,
         v_ssm_out_norm, v_norm2, v_rel_bias, v_final_norm[None, :]])
    small_out = [dict(zip(_SMALL, r)) for r in small_res]
    for r in small_out:
        r["final_norm"] = r["final_norm"][0]

    dmod_all = small_all[:, 0:6, :].reshape(8, 6 * D)
    dmod_loc = lax.dynamic_slice(dmod_all, (0, chip * ncol), (8, ncol))
    ada_out = _ada_bwd_adamw(c_all.T, dmod_loc, ada_w[0], m_ada_w[0], v_ada_w[0])

    big_gu = _adamw_halves("adamw_gate_up", mine_gu, got3_gu[0], w_gate_up, m_w_gate_up, v_w_gate_up, where)
    big_dn = _adamw_halves("adamw_down", mine_dn, got3_dn[0], w_down, m_w_down, v_w_down, where)
    big_o = _adamw_halves("adamw_o", mine_o, got3_o[0], w_o, m_w_o, v_w_o, where)
    big_in = [o.T[None] for o in _adamw_halves("adamw_in", mine_in, got3_in[0], w_in_t, m_w_in_t, v_w_in_t, where,
                                               by_cols=True)]
    big = [big_in, big_o, big_gu, big_dn]

    order = ["ada_w", "ada_b", "norm1", "w_in", "conv_w", "conv_b", "dt_bias", "A_log", "D_skip", "sinks",
             "attn_out_norm", "ssm_out_norm", "w_o", "norm2", "w_gate_up", "w_down", "rel_bias", "final_norm"]
    bigname = {"w_in": 0, "w_o": 1, "w_gate_up": 2, "w_down": 3}
    res = [loss, grad_x[None]]
    for kind in range(4):
        for nm in order:
            if nm == "ada_w":
                res.append(ada_out[kind][None])
            elif nm in bigname:
                res.append(big[bigname[nm]][kind])
            else:
                res.append(small_out[kind][nm])
    return tuple(res)
```

```python
import numpy as np
import jax
import jax.numpy as jnp
from jax import lax
from jax.experimental import pallas as pl
from jax.experimental.pallas import tpu as pltpu

F32, BF16 = jnp.float32, jnp.bfloat16
HI = lax.Precision.HIGHEST

D = 1024
QW, KVW = 512, 128
NH, HD, NKV = 8, 64, 2
SW = 512
NST = 128
XBCW = 1024
CK = 4
BLK = 128
DFF = 2816
IN_W = 2312
PROJ_W = 2432
EPS = 1e-6
NEG = -1e30
NBUCKET = 32

B1, B2, LR, AEPS, WD, STEP = 0.9, 0.999, 0.001, 1e-08, 0.01, 10

VMEM_LIMIT = 56 * 1024 * 1024

_NT = (((1,), (1,)), ((), ()))
_TN = (((0,), (0,)), ((), ()))


def _mm(a, b):
    return jnp.dot(a, b, preferred_element_type=F32)


def _mm_nt(a, b):
    return lax.dot_general(a, b, _NT, preferred_element_type=F32)


def _mm_tn(a, b):
    return lax.dot_general(a, b, _TN, preferred_element_type=F32)


def _mm_hi(a, b):
    return jnp.dot(a, b, preferred_element_type=F32, precision=HI)


def _split3(x):
    hi = x.astype(BF16)
    r = x - hi.astype(F32)
    mid = r.astype(BF16)
    lo = (r - mid.astype(F32)).astype(BF16)
    return hi, mid, lo


def _sel_r(x, e):
    hi, mid, lo = _split3(x)
    return (_mm(hi, e) + _mm(mid, e)) + _mm(lo, e)


def _sel_l(e, x):
    hi, mid, lo = _split3(x)
    return (_mm(e, hi) + _mm(e, mid)) + _mm(e, lo)


def _sig(x):
    return 1.0 / (1.0 + jnp.exp(-x))


def _cp(sem):
    return pltpu.CompilerParams(dimension_semantics=sem, vmem_limit_bytes=VMEM_LIMIT)


def _row(shape):
    nd = len(shape)
    return pl.BlockSpec(shape, lambda *_: (0,) * nd)


def _adamw(w, g, m, v):
    m = B1 * m + (1.0 - B1) * g
    v = B2 * v + (1.0 - B2) * (g * g)
    m_hat = m / (1.0 - B1 ** STEP)
    v_hat = v / (1.0 - B2 ** STEP)
    delta = -LR * (m_hat / (jnp.sqrt(v_hat) + AEPS) + WD * w)
    return delta, m, v


class _Carry:
    def __init__(self, inps, outs, copies):
        self.inps, self.outs, self.copies = list(inps), list(outs), copies
        self.n = len(copies(0, 0, 0))

    def descriptors(self, in_refs, out_refs, send_sems, recv_sems):
        x, y, c = lax.axis_index("x"), lax.axis_index("y"), lax.axis_index("c")
        out = []
        for j, (flip, a, si, o, di) in enumerate(self.copies(x, y, c)):
            if flip is None:
                out.append(pltpu.make_async_copy(in_refs[a].at[si], out_refs[o].at[di], send_sems.at[j]))
            else:
                fx, fy, fc = flip
                peer = (1 - x if fx else x, 1 - y if fy else y, 1 - c if fc else c)
                out.append(pltpu.make_async_remote_copy(
                    src_ref=in_refs[a].at[si], dst_ref=out_refs[o].at[di],
                    send_sem=send_sems.at[j], recv_sem=recv_sems.at[j],
                    device_id=peer, device_id_type=pl.DeviceIdType.MESH))
        return out


def _pcall(body, args, *, name, grid, in_specs, out_specs, out_shape, scratch_shapes=(), sem=None, nprefetch=0,
           carry=None):
    out_shape, out_specs = list(out_shape), list(out_specs)
    in_specs, scratch_shapes = list(in_specs), list(scratch_shapes)
    nin, nout, nscr = len(in_specs), len(out_shape), len(scratch_shapes)
    run = body
    if carry is not None:
        ncin, ncout = len(carry.inps), len(carry.outs)
        hbm = pl.BlockSpec(memory_space=pl.ANY)

        def run(*refs):
            pre, r = refs[:nprefetch], refs[nprefetch:]
            ins, cins = r[:nin], r[nin:nin + ncin]
            r = r[nin + ncin:]
            outs, couts = r[:nout], r[nout:nout + ncout]
            r = r[nout + ncout:]
            scr, (send_sems, recv_sems) = r[:nscr], r[nscr:]
            first = pl.program_id(0) == 0
            last = pl.program_id(0) == grid[0] - 1
            for ax in range(1, len(grid)):
                first = jnp.logical_and(first, pl.program_id(ax) == 0)
                last = jnp.logical_and(last, pl.program_id(ax) == grid[ax] - 1)

            @pl.when(first)
            def _():
                for d in carry.descriptors(cins, couts, send_sems, recv_sems):
                    d.start()

            body(*pre, *ins, *outs, *scr)

            @pl.when(last)
            def _():
                for d in carry.descriptors(cins, couts, send_sems, recv_sems):
                    d.wait()

        in_specs = in_specs + [hbm] * ncin
        out_specs = out_specs + [hbm] * ncout
        out_shape = out_shape + carry.outs
        scratch_shapes = scratch_shapes + [pltpu.SemaphoreType.DMA((carry.n,)), pltpu.SemaphoreType.DMA((carry.n,))]
        args = list(args) + carry.inps
    if sem is None:
        sem = ("arbitrary",) * len(grid)
    if nprefetch:
        kw = dict(grid_spec=pltpu.PrefetchScalarGridSpec(num_scalar_prefetch=nprefetch, grid=grid, in_specs=in_specs,
                                                         out_specs=out_specs, scratch_shapes=scratch_shapes))
    else:
        kw = dict(grid=grid, in_specs=in_specs, out_specs=out_specs, scratch_shapes=scratch_shapes)
    res = pl.pallas_call(run, name=name, out_shape=out_shape, compiler_params=_cp(sem), **kw)(*args)
    return list(res)


def _merge(*carries):
    inps, outs, offs = [], [], []
    for cr in carries:
        offs.append((len(inps), len(outs)))
        inps += cr.inps
        outs += cr.outs

    def copies(x, y, c):
        return [(f, a + io, si, o + oo, di) for cr, (io, oo) in zip(carries, offs) for f, a, si, o, di in cr.copies(x, y, c)]

    return _Carry(inps, outs, copies)


def _exchange(name, carry):
    return _pcall(lambda: None, [], name=name, grid=(1,), in_specs=[], out_specs=[], out_shape=[], carry=carry)


def _exchange_two(name, carry, then):
    second = _Carry([], [], then)
    nin, nout = len(carry.inps), len(carry.outs)

    def body(*refs):
        ins, outs = refs[:nin], refs[nin:nin + nout]
        send_a, recv_a, send_b, recv_b = refs[nin + nout:]
        for descs in (carry.descriptors(ins, outs, send_a, recv_a), second.descriptors(outs, outs, send_b, recv_b)):
            for d in descs:
                d.start()
            for d in descs:
                d.wait()

    hbm = pl.BlockSpec(memory_space=pl.ANY)
    return list(pl.pallas_call(
        body, name=name, out_shape=carry.outs, in_specs=[hbm] * nin, out_specs=[hbm] * nout,
        scratch_shapes=[pltpu.SemaphoreType.DMA((carry.n,)), pltpu.SemaphoreType.DMA((carry.n,)),
                        pltpu.SemaphoreType.DMA((second.n,)), pltpu.SemaphoreType.DMA((second.n,))],
    )(*carry.inps))


_ALL7 = [(f >> 2 & 1, f >> 1 & 1, f & 1) for f in range(1, 8)]
_CHIPS3 = [(0, 1, 0), (1, 0, 0), (1, 1, 0)]
_SIBLING = (0, 0, 1)


def _gather8_carry(blk):
    def copies(x, y, c):
        me = 4 * x + 2 * y + c
        return [(None, 0, 0, 0, me)] + [(f, 0, 0, 0, me) for f in _ALL7]

    return _Carry([blk[None]], [jax.ShapeDtypeStruct((8,) + blk.shape, blk.dtype)], copies)


def _gather_chips_carry(blks):
    def copies(x, y, c):
        chip = 2 * x + y
        return [(f, a, 0, a, chip) for a in range(len(blks)) for f in [None] + _CHIPS3]

    return _Carry([b[None] for b in blks], [jax.ShapeDtypeStruct((4,) + b.shape, b.dtype) for b in blks], copies)


def _ada_fwd(c_all, w_loc, b_loc):
    n = w_loc.shape[1]
    tn = 512

    def body(c_ref, w_ref, b_ref, o_ref):
        cv = c_ref[...]
        cond = cv * _sig(cv)
        o_ref[...] = _mm_hi(cond, w_ref[...]) + b_ref[...]

    return pl.pallas_call(
        body, name="ada_fwd", grid=(n // tn,),
        out_shape=jax.ShapeDtypeStruct((8, n), F32),
        in_specs=[_row((8, D)), pl.BlockSpec((D, tn), lambda j: (0, j)), pl.BlockSpec((1, tn), lambda j: (0, j))],
        out_specs=pl.BlockSpec((8, tn), lambda j: (0, j)),
        compiler_params=_cp(("parallel",)),
    )(c_all, w_loc, b_loc)


def _ada_bwd_adamw(c_all_t, dmod_loc, w, m, v, carry=None):
    n = w.shape[1]
    tn = 512

    def body(ct_ref, dm_ref, w_ref, m_ref, v_ref, g_ref, d_ref, mo_ref, vo_ref):
        ct = ct_ref[...]
        cond = ct * _sig(ct)
        dm = dm_ref[...]
        g = cond[:, 0:1] * dm[0:1, :]
        for b in range(1, 8):
            g = g + cond[:, b:b + 1] * dm[b:b + 1, :]
        g_ref[...] = g
        d_ref[...], mo_ref[...], vo_ref[...] = _adamw(w_ref[...], g, m_ref[...], v_ref[...])

    wspec = pl.BlockSpec((D, tn), lambda j: (0, j))
    return _pcall(
        body, [c_all_t, dmod_loc, w, m, v], name="ada_bwd_adamw", grid=(n // tn,),
        out_shape=[jax.ShapeDtypeStruct((D, n), F32)] * 4,
        in_specs=[_row((D, 8)), pl.BlockSpec((8, tn), lambda j: (0, j)), wspec, wspec, wspec],
        out_specs=[wspec] * 4, carry=carry)


def _in_proj_fwd(x, a1, sh1, w_in, carry=None):
    s = x.shape[0]
    tm = 512

    def body(x_ref, a_ref, s_ref, w_ref, q_ref, kv_ref, z_ref, xbc_ref, dt_ref):
        xv = x_ref[...]
        r = lax.rsqrt(jnp.mean(xv * xv, axis=-1, keepdims=True) + EPS)
        h = (xv * r * a_ref[...] + s_ref[...]).astype(BF16)
        p = _mm_nt(h, w_ref[...])
        q_ref[...] = p[:, 0:512].astype(BF16)
        kv_ref[...] = p[:, 512:768].astype(BF16)
        z_ref[...] = p[:, 768:1280]
        xbc_ref[...] = p[:, 1280:2304]
        dt_ref[...] = p[:, 2304:2432]

    def tok(w):
        return pl.BlockSpec((tm, w), lambda i: (i, 0))

    return _pcall(
        body, [x, a1, sh1, w_in], name="in_proj_fwd", grid=(s // tm,),
        out_shape=[jax.ShapeDtypeStruct((s, QW), BF16), jax.ShapeDtypeStruct((s, 2 * KVW), BF16),
                   jax.ShapeDtypeStruct((s, SW), F32), jax.ShapeDtypeStruct((s, XBCW), F32),
                   jax.ShapeDtypeStruct((s, 128), F32)],
        in_specs=[tok(D), _row((1, D)), _row((1, D)), _row((PROJ_W, D))],
        out_specs=[tok(QW), tok(2 * KVW), tok(SW), tok(XBCW), tok(128)], carry=carry)


def _in_proj_bwd(x, dx1, a1, sh1, w_in, dq, dkv, dz, dxbc, ddt, carry=None):
    s = x.shape[0]
    tm = 512

    def body(x_ref, dx1_ref, a_ref, s_ref, w_ref, dq_ref, dkv_ref, dz_ref, dxbc_ref, ddt_ref,
             gx_ref, dproj_ref, h_ref, dsh_ref, p_ref):
        i = pl.program_id(0)

        @pl.when(i == 0)
        def _():
            dsh_ref[...] = jnp.zeros_like(dsh_ref)
            p_ref[...] = jnp.zeros_like(p_ref)

        dproj = jnp.concatenate([dq_ref[...], dkv_ref[...], dz_ref[...], dxbc_ref[...], ddt_ref[...]], axis=1)
        dproj_ref[...] = dproj
        dh = _mm(dproj, w_ref[...])
        xv = x_ref[...]
        r = lax.rsqrt(jnp.mean(xv * xv, axis=-1, keepdims=True) + EPS)
        xn = xv * r
        a = a_ref[...]
        h_ref[...] = (xn * a + s_ref[...]).astype(BF16)
        dsh_ref[0:1, :] += jnp.sum(dh, axis=0, keepdims=True)
        p_ref[0:1, :] += jnp.sum(dh * xn, axis=0, keepdims=True)
        u = dh * a
        gx_ref[...] = dx1_ref[...] + r * u - xn * (r * jnp.mean(u * xn, axis=-1, keepdims=True))

    def tok(w):
        return pl.BlockSpec((tm, w), lambda i: (i, 0))

    return _pcall(
        body, [x, dx1, a1, sh1, w_in, dq, dkv, dz, dxbc, ddt], name="in_proj_bwd", grid=(s // tm,),
        out_shape=[jax.ShapeDtypeStruct((s, D), F32), jax.ShapeDtypeStruct((s, PROJ_W), BF16),
                   jax.ShapeDtypeStruct((s, D), BF16), jax.ShapeDtypeStruct((8, D), F32),
                   jax.ShapeDtypeStruct((8, D), F32)],
        in_specs=[tok(D), tok(D), _row((1, D)), _row((1, D)), _row((PROJ_W, D)),
                  tok(QW), tok(2 * KVW), tok(SW), tok(XBCW), tok(128)],
        out_specs=[tok(D), tok(PROJ_W), tok(D), _row((8, D)), _row((8, D))], carry=carry)


def _attn_geometry():
    dist = np.arange(BLK)[:, None] + BLK - np.arange(2 * BLK)[None, :]
    n = np.maximum(dist, 0)
    max_exact = NBUCKET // 2
    large = max_exact + (np.log(np.maximum(n, 1) / max_exact) / np.log(128 / max_exact)
                         * (NBUCKET - max_exact)).astype(np.int32)
    large = np.minimum(large, NBUCKET - 1)
    bucket = np.where(n < max_exact, n, large).astype(np.int32)
    mask = (dist >= 0) & (dist < 128)
    return bucket, mask


def _attn_heads(is_first, q_blk, kvw, bias_ref, sinks_ref):
    qv = q_blk * 0.125
    col = lax.broadcasted_iota(jnp.int32, (BLK, 2 * BLK), 1)
    first = jnp.where(jnp.logical_and(is_first, col < BLK), NEG, 0.0)
    groups = []
    for g in range(NKV):
        qs = jnp.concatenate([qv[:, (4 * g + r) * HD:(4 * g + r + 1) * HD] for r in range(4)], axis=0)
        kw = kvw[:, g * HD:(g + 1) * HD]
        vw = kvw[:, KVW + g * HD:KVW + (g + 1) * HD]
        sc = _mm_nt(qs, kw)
        pn, ps = [], []
        for r in range(4):
            h = 4 * g + r
            sr = sc[r * BLK:(r + 1) * BLK] + bias_ref[h] + first
            sink = sinks_ref[h]
            m = jnp.maximum(jnp.max(sr, axis=-1, keepdims=True), sink)
            p = jnp.exp(sr - m)
            es = jnp.exp(sink - m)
            inv = 1.0 / (jnp.sum(p, axis=-1, keepdims=True) + es)
            pn.append(p * inv)
            ps.append(es * inv)
        pn = jnp.concatenate(pn, axis=0)
        ps = jnp.concatenate(ps, axis=0)
        o = _mm(pn.astype(BF16), vw)
        groups.append((qs, kw, vw, pn, ps, o))
    return groups


def _unstack_heads(parts):
    return jnp.concatenate([p[r * BLK:(r + 1) * BLK] for p in parts for r in range(4)], axis=1)


def _attn_fwd(q, kv, bias, sinks, nw, carry=None):
    s = q.shape[0]

    def body(q_ref, kvp_ref, kvc_ref, bias_ref, sinks_ref, nw_ref, y_ref):
        t = pl.program_id(0)
        kv3 = jnp.concatenate([kvp_ref[...], kvc_ref[...]], axis=0)
        for sub in range(2):
            rows = slice(sub * BLK, (sub + 1) * BLK)
            groups = _attn_heads(jnp.logical_and(t == 0, sub == 0), q_ref[rows, :], kv3[sub * BLK:(sub + 2) * BLK],
                                 bias_ref, sinks_ref)
            o = _unstack_heads([g[5] for g in groups])
            r = lax.rsqrt(jnp.mean(o * o, axis=-1, keepdims=True) + EPS)
            y_ref[rows, :] = (o * r * nw_ref[...]).astype(BF16)

    return _pcall(
        body, [q, kv, kv, bias, sinks, nw], name="attn_fwd", grid=(s // (2 * BLK),),
        out_shape=[jax.ShapeDtypeStruct((s, QW), BF16)],
        in_specs=[pl.BlockSpec((2 * BLK, QW), lambda t: (t, 0)),
                  pl.BlockSpec((BLK, 2 * KVW), lambda t: (jnp.maximum(2 * t - 1, 0), 0)),
                  pl.BlockSpec((2 * BLK, 2 * KVW), lambda t: (t, 0)),
                  _row((NH, BLK, 2 * BLK)),
                  pl.BlockSpec(memory_space=pltpu.SMEM),
                  _row((1, QW))],
        out_specs=[pl.BlockSpec((2 * BLK, QW), lambda t: (t, 0))], carry=carry)


def _attn_bwd(q, kv, dya, bias, sinks, nw, carry=None):
    s = q.shape[0]
    nt = s // (2 * BLK)

    def body(q_ref, kvp_ref, kvc_ref, dy_ref, bias_ref, sinks_ref, nw_ref,
             dq_ref, dkv_ref, dbias_ref, dsink_ref, dnw_ref, carry_ref, held_ref):
        t = pl.program_id(0)

        @pl.when(t == 0)
        def _():
            carry_ref[...] = jnp.zeros_like(carry_ref)
            held_ref[...] = jnp.zeros_like(held_ref)
            dbias_ref[...] = jnp.zeros_like(dbias_ref)
            dsink_ref[...] = jnp.zeros_like(dsink_ref)
            dnw_ref[...] = jnp.zeros_like(dnw_ref)

        def block(sub, kv3):
            rows = slice(sub * BLK, (sub + 1) * BLK)
            groups = _attn_heads(jnp.logical_and(t == 0, sub == 0), q_ref[rows, :], kv3[sub * BLK:(sub + 2) * BLK],
                                 bias_ref, sinks_ref)
            o = _unstack_heads([g[5] for g in groups])
            r = lax.rsqrt(jnp.mean(o * o, axis=-1, keepdims=True) + EPS)
            dy = dy_ref[rows, :]
            on = o * r
            dnw_ref[0:1, :] += jnp.sum(dy * on, axis=0, keepdims=True)
            u = dy * nw_ref[...]
            do = r * u - on * (r * jnp.mean(u * on, axis=-1, keepdims=True))
            dq_parts, dk_parts, dv_parts = [], [], []
            for g, (qs, kw, vw, pn, ps, og) in enumerate(groups):
                dos = jnp.concatenate([do[:, (4 * g + r_) * HD:(4 * g + r_ + 1) * HD] for r_ in range(4)], axis=0)
                delta = jnp.sum(dos * og, axis=-1, keepdims=True)
                dp = _mm_nt(dos.astype(BF16), vw)
                ds = pn * (dp - delta)
                dsk = ps * delta
                lane = lax.broadcasted_iota(jnp.int32, (1, 128), 1)
                for r_ in range(4):
                    h = 4 * g + r_
                    dbias_ref[h] += ds[r_ * BLK:(r_ + 1) * BLK]
                    dsink_ref[0:1, :] -= jnp.where(lane == h, jnp.sum(dsk[r_ * BLK:(r_ + 1) * BLK]), 0.0)
                dsb = ds.astype(BF16)
                dq_parts.append(_mm(dsb, kw) * 0.125)
                dk_parts.append(_mm_tn(dsb, qs))
                dv_parts.append(_mm_tn(pn.astype(BF16), dos.astype(BF16)))
            dq_ref[rows, :] = _unstack_heads(dq_parts).astype(BF16)
            return jnp.concatenate(dk_parts + dv_parts, axis=1)

        @pl.when(t < nt)
        def _():
            kv3 = jnp.concatenate([kvp_ref[...], kvc_ref[...]], axis=0)
            d0 = block(0, kv3)
            dkv_ref[0:BLK, :] = held_ref[...].astype(BF16)
            dkv_ref[BLK:2 * BLK, :] = (carry_ref[...] + d0[0:BLK]).astype(BF16)
            d1 = block(1, kv3)
            held_ref[...] = d0[BLK:2 * BLK] + d1[0:BLK]
            carry_ref[...] = d1[BLK:2 * BLK]

        @pl.when(t == nt)
        def _():
            dkv_ref[0:BLK, :] = held_ref[...].astype(BF16)
            dkv_ref[BLK:2 * BLK, :] = carry_ref[...].astype(BF16)

    last = nt - 1
    tile = lambda w: pl.BlockSpec((2 * BLK, w), lambda t: (jnp.minimum(t, last), 0))
    return _pcall(
        body, [q, kv, kv, dya, bias, sinks, nw], name="attn_bwd", grid=(nt + 1,),
        out_shape=[jax.ShapeDtypeStruct((s, QW), BF16), jax.ShapeDtypeStruct((s, 2 * KVW), BF16),
                   jax.ShapeDtypeStruct((NH, BLK, 2 * BLK), F32), jax.ShapeDtypeStruct((NH, 128), F32),
                   jax.ShapeDtypeStruct((8, QW), F32)],
        in_specs=[tile(QW),
                  pl.BlockSpec((BLK, 2 * KVW), lambda t: (jnp.clip(2 * t - 1, 0, 2 * nt - 1), 0)),
                  tile(2 * KVW), tile(QW),
                  _row((NH, BLK, 2 * BLK)),
                  pl.BlockSpec(memory_space=pltpu.SMEM),
                  _row((1, QW))],
        out_specs=[tile(QW),
                   pl.BlockSpec((2 * BLK, 2 * KVW), lambda t: (jnp.maximum(t - 1, 0), 0)),
                   _row((NH, BLK, 2 * BLK)), _row((NH, 128)), _row((8, QW))],
        scratch_shapes=[pltpu.VMEM((BLK, 2 * KVW), F32), pltpu.VMEM((BLK, 2 * KVW), F32)], carry=carry)


def _rel_bias_grad(dbias, bucket):
    def body(db_ref, bk_ref, o_ref):
        bk = bk_ref[...]
        lane = lax.broadcasted_iota(jnp.int32, (1, 128), 1)
        for b in range(NBUCKET):
            sel = bk == b
            row = jnp.zeros((1, 128), F32)
            for h in range(NH):
                row = row + jnp.where(lane == h, jnp.sum(jnp.where(sel, db_ref[h], 0.0)), 0.0)
            o_ref[b:b + 1, :] = row

    return pl.pallas_call(
        body, name="rel_bias_grad",
        out_shape=jax.ShapeDtypeStruct((NBUCKET, 128), F32),
    )(dbias, bucket)


def _ssd_consts():
    head_of_lane = np.arange(SW) // HD
    expand = (np.arange(128)[:, None] == head_of_lane[None, :]).astype(np.float32)
    tril = np.tril(np.ones((BLK, BLK), np.float32))
    return (jnp.asarray(expand, BF16), jnp.asarray(expand.T.copy(), BF16), jnp.asarray(tril, BF16),
            jnp.asarray(tril.T.copy(), BF16))


def _ssd_chunk_fwd(c, z_ref, xc_ref, xp_ref, dtr_ref, cw_ref, cb_ref, dtb_ref, a_ref, dk_ref, ex_ref, tril_ref, h_in):
    halo = jnp.where(c == 0, 0.0, xp_ref[BLK - 8:BLK, :])
    ext = jnp.concatenate([halo, xc_ref[...]], axis=0)
    cw = cw_ref[...]
    taps = [ext[8:8 + BLK] if k == CK - 1 else pltpu.roll(ext, CK - 1 - k, 0)[8:8 + BLK] for k in range(CK)]
    pre = cb_ref[...] + sum(cw[k:k + 1, :] * taps[k] for k in range(CK))
    sp = _sig(pre)
    xbc = pre * sp
    xs, bm, cm = xbc[:, 0:SW], xbc[:, SW:SW + 2 * NST], xbc[:, SW + 2 * NST:]
    dtin = dtr_ref[...] + dtb_ref[...]
    dt = jnp.maximum(dtin, 0.0) + jnp.log1p(jnp.exp(-jnp.abs(dtin)))
    av = a_ref[...]
    cs = _sel_l(tril_ref[...], dt * av)
    cst = cs.T
    ex = ex_ref[...]
    dtx = _sel_r(dt, ex)
    csx = _sel_r(cs, ex)
    xdt = xs * dtx
    csl = csx[BLK - 1:BLK, :]
    decx = jnp.exp(csl - csx)
    ecsx = jnp.exp(csx)
    ecl = jnp.exp(csl)
    causal = tril_ref[...].astype(F32) > 0.5
    ydiag, yoff, cbs, lms = [], [], [], []
    for g in range(2):
        bg = bm[:, g * NST:(g + 1) * NST].astype(BF16)
        cg = cm[:, g * NST:(g + 1) * NST].astype(BF16)
        cb = _mm_nt(cg, bg)
        cbs.append(cb)
        yoff.append(_mm(cg, h_in[:, g * 256:(g + 1) * 256].astype(BF16)))
        for r in range(4):
            h = 4 * g + r
            seg = cs[:, h:h + 1] - cst[h:h + 1, :]
            lm = jnp.where(causal, jnp.exp(jnp.minimum(seg, 0.0)), 0.0)
            lms.append(lm)
            ydiag.append(_mm((cb * lm).astype(BF16), xdt[:, h * HD:(h + 1) * HD].astype(BF16)))
    yoff = jnp.concatenate(yoff, axis=1) * ecsx
    y = jnp.concatenate(ydiag, axis=1) + yoff + dk_ref[...] * xs
    return dict(ext=ext, taps=taps, pre=pre, sp=sp, xs=xs, bm=bm, cm=cm, dtin=dtin, dt=dt, av=av, cs=cs, cst=cst,
                dtx=dtx, csx=csx, xdt=xdt, decx=decx, ecsx=ecsx, ecl=ecl, causal=causal, cbs=cbs, lms=lms,
                yoff=yoff, y=y)


def _group_mean(t):
    m0 = jnp.mean(t[:, 0:256], axis=-1, keepdims=True)
    m1 = jnp.mean(t[:, 256:512], axis=-1, keepdims=True)
    return jnp.concatenate([jnp.broadcast_to(m0, (t.shape[0], 256)), jnp.broadcast_to(m1, (t.shape[0], 256))], axis=1)


def _ssd_specs(nc, rev):
    def cur(w):
        return pl.BlockSpec((BLK, w), (lambda i: (nc - 1 - i, 0)) if rev else (lambda i: (i, 0)))
    prev = pl.BlockSpec((BLK, XBCW), (lambda i: (jnp.maximum(nc - 2 - i, 0), 0)) if rev
                        else (lambda i: (jnp.maximum(i - 1, 0), 0)))
    return cur, prev


def _ssd_fwd(z, xbc, dtr, cw, cb, dtb, av, dk, nw, carry=None):
    s = z.shape[0]
    nc = s // BLK
    ex, _, tril, _ = _ssd_consts()

    def body(z_ref, xc_ref, xp_ref, dtr_ref, cw_ref, cb_ref, dtb_ref, a_ref, dk_ref, nw_ref, ex_ref, tril_ref,
             y_ref, hs_ref, h_ref):
        c = pl.program_id(0)

        @pl.when(c == 0)
        def _():
            h_ref[...] = jnp.zeros_like(h_ref)

        h_in = h_ref[...]
        hs_ref[0] = h_in
        f = _ssd_chunk_fwd(c, z_ref, xc_ref, xp_ref, dtr_ref, cw_ref, cb_ref, dtb_ref, a_ref, dk_ref, ex_ref,
                           tril_ref, h_in)
        dx = (f["decx"] * f["xdt"]).astype(BF16)
        st = [_mm_tn(f["bm"][:, g * NST:(g + 1) * NST].astype(BF16), dx[:, g * 256:(g + 1) * 256]) for g in range(2)]
        h_ref[...] = h_in * f["ecl"] + jnp.concatenate(st, axis=1)
        zv = z_ref[...]
        t = f["y"] * (zv * _sig(zv))
        r = lax.rsqrt(_group_mean(t * t) + EPS)
        y_ref[...] = (t * r * nw_ref[...]).astype(BF16)

    cur, prev = _ssd_specs(nc, False)
    return _pcall(
        body, [z, xbc, xbc, dtr, cw, cb, dtb, av, dk, nw, ex, tril], name="ssd_fwd", grid=(nc,),
        out_shape=[jax.ShapeDtypeStruct((s, SW), BF16), jax.ShapeDtypeStruct((nc, NST, SW), F32)],
        in_specs=[cur(SW), cur(XBCW), prev, cur(128), _row((8, XBCW)), _row((1, XBCW)), _row((1, 128)),
                  _row((1, 128)), _row((1, SW)), _row((1, SW)), _row((128, SW)), _row((BLK, BLK))],
        out_specs=[cur(SW), pl.BlockSpec((1, NST, SW), lambda i: (i, 0, 0))],
        scratch_shapes=[pltpu.VMEM((NST, SW), F32)], carry=carry)


def _ssd_bwd(z, xbc, dtr, dys, hs, cw, cb, dtb, av, dk, nw, carry=None):
    s = z.shape[0]
    nc = s // BLK
    ex, ext_t, tril, triu = _ssd_consts()

    def body(z_ref, xc_ref, xp_ref, dtr_ref, dy_ref, hs_ref, cw_ref, cb_ref, dtb_ref, a_ref, dk_ref, nw_ref,
             ex_ref, ext_ref, tril_ref, triu_ref,
             dz_ref, dxbc_ref, ddt_ref, dcw_ref, dcb_ref, dnw_ref, dhd_ref, dh_ref, nxt_ref, dd_ref):
        i = pl.program_id(0)
        c = nc - 1 - i

        @pl.when(i == 0)
        def _():
            dh_ref[...] = jnp.zeros_like(dh_ref)
            nxt_ref[...] = jnp.zeros_like(nxt_ref)
            dd_ref[...] = jnp.zeros_like(dd_ref)
            dcw_ref[...] = jnp.zeros_like(dcw_ref)
            dcb_ref[...] = jnp.zeros_like(dcb_ref)
            dnw_ref[...] = jnp.zeros_like(dnw_ref)
            dhd_ref[...] = jnp.zeros_like(dhd_ref)

        h_in = hs_ref[0]
        f = _ssd_chunk_fwd(c, z_ref, xc_ref, xp_ref, dtr_ref, cw_ref, cb_ref, dtb_ref, a_ref, dk_ref, ex_ref,
                           tril_ref, h_in)
        xs, xdt, decx, ecsx, ecl, dtx = f["xs"], f["xdt"], f["decx"], f["ecsx"], f["ecl"], f["dtx"]
        cs, cst, causal = f["cs"], f["cst"], f["causal"]
        causal_t = triu_ref[...].astype(F32) > 0.5

        zv = z_ref[...]
        sz = _sig(zv)
        gz = zv * sz
        t = f["y"] * gz
        r = lax.rsqrt(_group_mean(t * t) + EPS)
        tn_ = t * r
        dyn = dy_ref[...]
        dnw_ref[0:1, :] += jnp.sum(dyn * tn_, axis=0, keepdims=True)
        u = dyn * nw_ref[...]
        dt_ = r * u - tn_ * (r * _group_mean(u * tn_))
        dy = dt_ * gz
        dz_ref[...] = (dt_ * f["y"] * (sz * (1.0 + zv * (1.0 - sz)))).astype(BF16)

        dd_ref[0:1, :] += jnp.sum(dy * xs, axis=0, keepdims=True)
        dxs = dk_ref[...] * dy

        gst = dh_ref[...]
        edy = ecsx * dy
        dxdt, dbs, dcs_, dcsx_parts, dh_new = [], [], [], [], []
        lane = lax.broadcasted_iota(jnp.int32, (1, 128), 1)
        dcs_intra = jnp.zeros((BLK, 128), F32)
        for g in range(2):
            sl = slice(g * 256, (g + 1) * 256)
            bgf, cgf = f["bm"][:, g * NST:(g + 1) * NST], f["cm"][:, g * NST:(g + 1) * NST]
            bg, cg = bgf.astype(BF16), cgf.astype(BF16)
            gg = gst[:, sl].astype(BF16)
            hg = h_in[:, sl].astype(BF16)
            edyg = edy[:, sl].astype(BF16)
            dc = _mm_nt(edyg, hg)
            dh_new.append(gst[:, sl] * ecl[:, sl] + _mm_tn(cg, edyg))
            bgm = _mm(bg, gg)
            dxdt_g = decx[:, sl] * bgm
            dxg = (decx[:, sl] * xdt[:, sl]).astype(BF16)
            db = _mm_nt(dxg, gg)
            qd = bgm * xdt[:, sl] * decx[:, sl]
            last = jnp.sum(qd, axis=0, keepdims=True) + ecl[:, sl] * jnp.sum(gst[:, sl] * h_in[:, sl], axis=0, keepdims=True)
            rowid = lax.broadcasted_iota(jnp.int32, (BLK, 256), 0)
            dcsx_parts.append(f["yoff"][:, sl] * dy[:, sl] - qd + jnp.where(rowid == BLK - 1, last, 0.0))
            cb_ = f["cbs"][g]
            cbt = _mm_nt(bg, cg)
            dcb_ = jnp.zeros((BLK, BLK), F32)
            dcbt = jnp.zeros((BLK, BLK), F32)
            dxd = []
            for r_ in range(4):
                h = 4 * g + r_
                hl = slice(h * HD, (h + 1) * HD)
                lm = f["lms"][h]
                segt = cst[h:h + 1, :] - cs[:, h:h + 1]
                lmt = jnp.where(causal_t, jnp.exp(jnp.minimum(segt, 0.0)), 0.0)
                dyh = dy[:, hl].astype(BF16)
                xdh = xdt[:, hl].astype(BF16)
                dw = _mm_nt(dyh, xdh)
                dwt = _mm_nt(xdh, dyh)
                wt = cbt * lmt
                dxd.append(_mm(wt.astype(BF16), dyh))
                dcb_ = dcb_ + dw * lm
                dcbt = dcbt + dwt * lmt
                col = jnp.sum(dw * (cb_ * lm), axis=-1, keepdims=True) - jnp.sum(dwt * wt, axis=-1, keepdims=True)
                dcs_intra = dcs_intra + jnp.where(lane == h, col, 0.0)
            dxdt.append(dxdt_g + jnp.concatenate(dxd, axis=1))
            dcs_.append(dc + _mm(dcb_.astype(BF16), bg))
            dbs.append(db + _mm(dcbt.astype(BF16), cg))
        dh_ref[...] = jnp.concatenate(dh_new, axis=1)
        dxdt = jnp.concatenate(dxdt, axis=1)
        dxs = dxs + dxdt * dtx
        ext_t_ = ext_ref[...]
        dcs = dcs_intra + _sel_r(jnp.concatenate(dcsx_parts, axis=1), ext_t_)
        da = _sel_l(triu_ref[...], dcs)
        ddt = da * f["av"] + _sel_r(dxdt * xs, ext_t_)
        dhd_ref[1:2, :] += jnp.sum(da * f["dt"], axis=0, keepdims=True)
        ddtr = ddt * _sig(f["dtin"])
        dhd_ref[0:1, :] += jnp.sum(ddtr, axis=0, keepdims=True)
        ddt_ref[...] = ddtr.astype(BF16)

        sp, pre = f["sp"], f["pre"]
        dact = jnp.concatenate([dxs] + dbs + dcs_, axis=1)
        dpre = dact * (sp * (1.0 + pre * (1.0 - sp)))
        dcb_ref[0:1, :] += jnp.sum(dpre, axis=0, keepdims=True)
        for k in range(CK):
            dcw_ref[k:k + 1, :] += jnp.sum(dpre * f["taps"][k], axis=0, keepdims=True)
        ext2 = jnp.concatenate([dpre, nxt_ref[...]], axis=0)
        cw = cw_ref[...]
        dxr = cw[CK - 1:CK, :] * dpre
        for k in range(CK - 1):
            dxr = dxr + cw[k:k + 1, :] * pltpu.roll(ext2, BLK + 8 - (CK - 1 - k), 0)[0:BLK]
        dxbc_ref[...] = dxr.astype(BF16)
        nxt_ref[...] = dpre[0:8]

        @pl.when(i == nc - 1)
        def _():
            dhd_ref[2:3, :] = _sel_r(dd_ref[...], ext_t_)[0:1, :]

    cur, prev = _ssd_specs(nc, True)
    return _pcall(
        body, [z, xbc, xbc, dtr, dys, hs, cw, cb, dtb, av, dk, nw, ex, ext_t, tril, triu], name="ssd_bwd", grid=(nc,),
        out_shape=[jax.ShapeDtypeStruct((s, SW), BF16), jax.ShapeDtypeStruct((s, XBCW), BF16),
                   jax.ShapeDtypeStruct((s, 128), BF16), jax.ShapeDtypeStruct((8, XBCW), F32),
                   jax.ShapeDtypeStruct((8, XBCW), F32), jax.ShapeDtypeStruct((8, SW), F32),
                   jax.ShapeDtypeStruct((8, 128), F32)],
        in_specs=[cur(SW), cur(XBCW), prev, cur(128), cur(SW),
                  pl.BlockSpec((1, NST, SW), lambda i: (nc - 1 - i, 0, 0)),
                  _row((8, XBCW)), _row((1, XBCW)), _row((1, 128)), _row((1, 128)), _row((1, SW)), _row((1, SW)),
                  _row((128, SW)), _row((SW, 128)), _row((BLK, BLK)), _row((BLK, BLK))],
        out_specs=[cur(SW), cur(XBCW), cur(128), _row((8, XBCW)), _row((8, XBCW)), _row((8, SW)), _row((8, 128))],
        scratch_shapes=[pltpu.VMEM((NST, SW), F32), pltpu.VMEM((8, XBCW), F32), pltpu.VMEM((8, SW), F32)], carry=carry)


def _load_once(i, pairs, sem):
    @pl.when(i == 0)
    def _():
        cps = [pltpu.make_async_copy(src, dst, sem.at[k]) for k, (src, dst) in enumerate(pairs)]
        for cp in cps:
            cp.start()
        for cp in cps:
            cp.wait()


def _mlp_fwd(x, ya, ys, tgt, w_o, w_ga, w_gb, w_dn, gate1, a2, sh2, gate2, fn):
    s = x.shape[0]
    tm = 256

    def body(x_ref, ya_ref, ys_ref, t_ref, wo_hbm, wga_hbm, wgb_hbm, wdn_hbm, g1_ref, a2_ref, s2_ref, g2_ref, fn_ref,
             x1_ref, gu_ref, dx2_ref, loss_ref, dfn_ref, wo, wga, wgb, wdn, sem):
        i = pl.program_id(0)
        _load_once(i, [(wo_hbm, wo), (wga_hbm, wga), (wgb_hbm, wgb), (wdn_hbm, wdn)], sem)

        @pl.when(i == 0)
        def _():
            loss_ref[...] = jnp.zeros_like(loss_ref)
            dfn_ref[...] = jnp.zeros_like(dfn_ref)

        mix = _mm(ya_ref[...], wo[0:QW, :]) + _mm(ys_ref[...], wo[QW:D, :])
        x1 = x_ref[...] + g1_ref[...] * mix
        x1_ref[...] = x1
        r2 = lax.rsqrt(jnp.mean(x1 * x1, axis=-1, keepdims=True) + EPS)
        h2 = (x1 * r2 * a2_ref[...] + s2_ref[...]).astype(BF16)
        ha, hb = h2[:, 0:D // 2], h2[:, D // 2:D]
        gub = jnp.concatenate([(_mm(ha, wga[j]) + _mm(hb, wgb[j])).astype(BF16) for j in range(4)], axis=1)
        gu_ref[...] = gub
        gv, uv = gub[:, 0:DFF].astype(F32), gub[:, DFF:].astype(F32)
        act = (gv * _sig(gv) * uv).astype(BF16)
        x2 = x1 + g2_ref[...] * _mm(act, wdn[...])
        r3 = lax.rsqrt(jnp.mean(x2 * x2, axis=-1, keepdims=True) + EPS)
        xn = x2 * r3
        fnv = fn_ref[...]
        err = xn * fnv - t_ref[...]
        loss_ref[...] += jnp.sum(err * err) * (0.5 / D)
        dy = err * (1.0 / D)
        dfn_ref[0:1, :] += jnp.sum(dy * xn, axis=0, keepdims=True)
        u = dy * fnv
        dx2_ref[...] = r3 * u - xn * (r3 * jnp.mean(u * xn, axis=-1, keepdims=True))

    def tok(w):
        return pl.BlockSpec((tm, w), lambda i: (i, 0))

    hbm = pl.BlockSpec(memory_space=pl.ANY)
    return pl.pallas_call(
        body, name="mlp_fwd", grid=(s // tm,),
        out_shape=[jax.ShapeDtypeStruct((s, D), F32), jax.ShapeDtypeStruct((s, 2 * DFF), BF16),
                   jax.ShapeDtypeStruct((s, D), F32), jax.ShapeDtypeStruct((8, 128), F32),
                   jax.ShapeDtypeStruct((8, D), F32)],
        in_specs=[tok(D), tok(QW), tok(SW), tok(D), hbm, hbm, hbm, hbm,
                  _row((1, D)), _row((1, D)), _row((1, D)), _row((1, D)), _row((1, D))],
        out_specs=[tok(D), tok(2 * DFF), tok(D), _row((8, 128)), _row((8, D))],
        scratch_shapes=[pltpu.VMEM((D, D), BF16), pltpu.VMEM(w_ga.shape, BF16), pltpu.VMEM(w_gb.shape, BF16),
                        pltpu.VMEM((DFF, D), BF16), pltpu.SemaphoreType.DMA((4,))],
        compiler_params=_cp(("arbitrary",)),
    )(x, ya, ys, tgt, w_o, w_ga, w_gb, w_dn, gate1, a2, sh2, gate2, fn)


def _mlp_bwd(x1, gu, dx2, w_o, w_ga, w_gb, w_dn, gate1, a2, sh2, gate2):
    s = x1.shape[0]
    tm = 256
    nj = 2 * DFF // 4

    def body(x1_ref, gu_ref, dx2_ref, wo_hbm, wga_hbm, wgb_hbm, wdn_hbm, g1_ref, a2_ref, s2_ref, g2_ref,
             dx1_ref, dya_ref, dys_ref, act_ref, dgu_ref, h2_ref, dsh_ref, p_ref, wo, wga, wgb, wdn, sem):
        i = pl.program_id(0)
        _load_once(i, [(wo_hbm, wo), (wga_hbm, wga), (wgb_hbm, wgb), (wdn_hbm, wdn)], sem)

        @pl.when(i == 0)
        def _():
            dsh_ref[...] = jnp.zeros_like(dsh_ref)
            p_ref[...] = jnp.zeros_like(p_ref)

        dx2 = dx2_ref[...]
        dact = _mm_nt((dx2 * g2_ref[...]).astype(BF16), wdn[...])
        gub = gu_ref[...]
        gv, uv = gub[:, 0:DFF].astype(F32), gub[:, DFF:].astype(F32)
        sg = _sig(gv)
        sl = gv * sg
        act_ref[...] = (sl * uv).astype(BF16)
        dgu = jnp.concatenate([dact * uv * (sg * (1.0 + gv * (1.0 - sg))), dact * sl], axis=1).astype(BF16)
        dgu_ref[...] = dgu
        dha = sum(_mm_nt(dgu[:, j * nj:(j + 1) * nj], wga[j]) for j in range(4))
        dhb = sum(_mm_nt(dgu[:, j * nj:(j + 1) * nj], wgb[j]) for j in range(4))
        dh = jnp.concatenate([dha, dhb], axis=1)
        x1 = x1_ref[...]
        r2 = lax.rsqrt(jnp.mean(x1 * x1, axis=-1, keepdims=True) + EPS)
        xn = x1 * r2
        a2 = a2_ref[...]
        h2_ref[...] = (xn * a2 + s2_ref[...]).astype(BF16)
        dsh_ref[0:1, :] += jnp.sum(dh, axis=0, keepdims=True)
        p_ref[0:1, :] += jnp.sum(dh * xn, axis=0, keepdims=True)
        u = dh * a2
        dx1 = dx2 + r2 * u - xn * (r2 * jnp.mean(u * xn, axis=-1, keepdims=True))
        dx1_ref[...] = dx1
        dcat = _mm_nt((dx1 * g1_ref[...]).astype(BF16), wo[...])
        dya_ref[...] = dcat[:, 0:QW]
        dys_ref[...] = dcat[:, QW:D]

    def tok(w):
        return pl.BlockSpec((tm, w), lambda i: (i, 0))

    hbm = pl.BlockSpec(memory_space=pl.ANY)
    return pl.pallas_call(
        body, name="mlp_bwd", grid=(s // tm,),
        out_shape=[jax.ShapeDtypeStruct((s, D), F32), jax.ShapeDtypeStruct((s, QW), F32),
                   jax.ShapeDtypeStruct((s, SW), F32), jax.ShapeDtypeStruct((s, DFF), BF16),
                   jax.ShapeDtypeStruct((s, 2 * DFF), BF16), jax.ShapeDtypeStruct((s, D), BF16),
                   jax.ShapeDtypeStruct((8, D), F32), jax.ShapeDtypeStruct((8, D), F32)],
        in_specs=[tok(D), tok(2 * DFF), tok(D), hbm, hbm, hbm, hbm, _row((1, D)), _row((1, D)), _row((1, D)), _row((1, D))],
        out_specs=[tok(D), tok(QW), tok(SW), tok(DFF), tok(2 * DFF), tok(D), _row((8, D)), _row((8, D))],
        scratch_shapes=[pltpu.VMEM((D, D), BF16), pltpu.VMEM(w_ga.shape, BF16), pltpu.VMEM(w_gb.shape, BF16),
                        pltpu.VMEM((DFF, D), BF16), pltpu.SemaphoreType.DMA((4,))],
        compiler_params=_cp(("arbitrary",)),
    )(x1, gu, dx2, w_o, w_ga, w_gb, w_dn, gate1, a2, sh2, gate2)


def _wgrad(name, a, b, tn, gate=None, w=None, stacked=False, carry=None):
    s, m = a.shape
    n = b.shape[1]
    tk = min(1024, s)
    nk = s // tk

    def body(*refs):
        if gate is None:
            a_ref, b_ref, o_ref = refs
        else:
            a_ref, b_ref, g_ref, w_ref, o_ref, dg_ref = refs
        k = pl.program_id(1)

        @pl.when(k == 0)
        def _():
            o_ref[...] = jnp.zeros_like(o_ref)

        o_ref[...] += _mm_tn(a_ref[...], b_ref[...].astype(BF16))

        if gate is not None:
            @pl.when(k == nk - 1)
            def _():
                acc = o_ref[...]
                dg_ref[...] = jnp.zeros_like(dg_ref)
                dg_ref[0:1, :] = jnp.sum(acc * w_ref[...].astype(F32), axis=0, keepdims=True)
                o_ref[...] = acc * g_ref[...]

    in_specs = [pl.BlockSpec((tk, m), lambda j, k: (k, 0)), pl.BlockSpec((tk, tn), lambda j, k: (k, j))]
    if stacked:
        out_shape = [jax.ShapeDtypeStruct((n // tn, m, tn), F32)]
        out_specs = [pl.BlockSpec((None, m, tn), lambda j, k: (j, 0, 0))]
    else:
        out_shape = [jax.ShapeDtypeStruct((m, n), F32)]
        out_specs = [pl.BlockSpec((m, tn), lambda j, k: (0, j))]
    args = [a, b]
    if gate is not None:
        in_specs += [pl.BlockSpec((1, tn), lambda j, k: (0, j)), pl.BlockSpec((m, tn), lambda j, k: (0, j))]
        out_shape.append(jax.ShapeDtypeStruct((8, n), F32))
        out_specs.append(pl.BlockSpec((8, tn), lambda j, k: (0, j)))
        args += [gate, w]
    return _pcall(body, args, name=name, grid=(n // tn, nk), out_shape=out_shape, in_specs=in_specs,
                  out_specs=out_specs, carry=carry)


def _wgrad_in_t(h1, dproj, carry=None):
    s = h1.shape[0]
    tk = min(1024, s)
    nk = s // tk

    def body(a_ref, b_ref, o_hbm, acc_ref, tr_ref, sem):
        k = pl.program_id(0)

        @pl.when(k == 0)
        def _():
            acc_ref[...] = jnp.zeros_like(acc_ref)

        acc_ref[...] += _mm_tn(a_ref[...], b_ref[...])

        @pl.when(k == nk - 1)
        def _():
            for j in range(PROJ_W // 128):
                tr_ref[j * 128:(j + 1) * 128, :] = acc_ref[:, j * 128:(j + 1) * 128].T
            cp = pltpu.make_async_copy(tr_ref, o_hbm, sem)
            cp.start()
            cp.wait()

    return _pcall(body, [h1, dproj], name="wgrad_in", grid=(nk,),
                  out_shape=[jax.ShapeDtypeStruct((PROJ_W, D), F32)],
                  in_specs=[pl.BlockSpec((tk, D), lambda k: (k, 0)), pl.BlockSpec((tk, PROJ_W), lambda k: (k, 0))],
                  out_specs=[pl.BlockSpec(memory_space=pl.ANY)],
                  scratch_shapes=[pltpu.VMEM((D, PROJ_W), F32), pltpu.VMEM((PROJ_W, D), F32), pltpu.SemaphoreType.DMA],
                  carry=carry)


_SMALL = ["ada_b", "norm1", "conv_w", "conv_b", "dt_bias", "A_log", "D_skip", "sinks", "attn_out_norm",
          "ssm_out_norm", "norm2", "rel_bias", "final_norm"]


def _small_grad(name, gs, chip):
    if name == "ada_b":
        return jnp.concatenate([gs[j:j + 1, :] for j in range(6)], axis=1)
    if name == "conv_w":
        full = gs[7:11, :]
        out = full[:, 0:256]
        for j in range(1, 4):
            out = jnp.where(chip == j, full[:, j * 256:(j + 1) * 256], out)
        return out
    row, width = {"norm1": (6, D), "conv_b": (11, D), "norm2": (12, D), "final_norm": (13, D),
                  "attn_out_norm": (14, QW), "ssm_out_norm": (15, SW), "dt_bias": (16, NH), "A_log": (17, NH),
                  "D_skip": (18, NH), "sinks": (19, NH), "rel_bias": (24, NH)}[name]
    rows = NBUCKET if name == "rel_bias" else 1
    return gs[row:row + rows, 0:width]


def _small_update(small_all, where, ws, ms, vs):
    n = len(_SMALL)

    def body(where_ref, sa_ref, *refs):
        w_refs, m_refs, v_refs, outs = refs[:n], refs[n:2 * n], refs[2 * n:3 * n], refs[3 * n:]
        gs = sa_ref[0]
        for b in range(1, 8):
            gs = gs + sa_ref[b]
        chip = where_ref[1]
        for i, name in enumerate(_SMALL):
            g = _small_grad(name, gs, chip)
            lead = (0,) if name == "conv_w" else ()
            d, mo, vo = _adamw(w_refs[i][lead + (...,)], g, m_refs[i][lead + (...,)], v_refs[i][lead + (...,)])
            for k, val in enumerate((g, d, mo, vo)):
                outs[k * n + i][lead + (...,)] = val
        outs[4 * n][...] = gs[20:21, 0:128]

    shapes = [jax.ShapeDtypeStruct(w.shape, F32) for w in ws]
    vmem = pl.BlockSpec(memory_space=pltpu.VMEM)
    res = pl.pallas_call(
        body, name="small_update", out_shape=shapes * 4 + [jax.ShapeDtypeStruct((1, 128), F32)],
        in_specs=[pl.BlockSpec(memory_space=pltpu.SMEM)] + [vmem] * (1 + 3 * n), out_specs=[vmem] * (4 * n + 1),
    )(where, small_all, *ws, *ms, *vs)
    return [res[k * n:(k + 1) * n] for k in range(4)], res[4 * n][0, 0]


def _add_half(name, g, got, where, by_cols=False):
    rr, cc = got.shape[1:]
    if by_cols:
        mine = pl.BlockSpec((None, rr, cc), lambda i, w_ref: (i, 0, w_ref[0]))
    else:
        mine = pl.BlockSpec((None, None, rr, cc), lambda i, w_ref: (i, w_ref[0], 0, 0))

    def body(w_ref, g_ref, r_ref, o_ref, own_ref):
        s = g_ref[...] + r_ref[...]
        o_ref[...] = s.astype(BF16)

        @pl.when(pl.program_id(0) == w_ref[1])
        def _():
            own_ref[...] = s

    spec = pl.BlockSpec((None, rr, cc), lambda i, w_ref: (i, 0, 0))
    return _pcall(body, [where, g, got], name=name, grid=(4,), nprefetch=1,
                  out_shape=[jax.ShapeDtypeStruct(got.shape, BF16), jax.ShapeDtypeStruct((rr, cc), F32)],
                  in_specs=[mine, spec],
                  out_specs=[spec, pl.BlockSpec((rr, cc), lambda i, w_ref: (0, 0))])


def _add_chips(name, own, got):
    rr, cc = own.shape
    tr = rr // 2 if rr % 32 == 0 else rr

    def body(s_ref, r_ref, o_ref):
        o_ref[...] = ((s_ref[...] + r_ref[0].astype(F32)) + r_ref[1].astype(F32)) + r_ref[2].astype(F32)

    spec = pl.BlockSpec((tr, cc), lambda i: (i, 0))
    return _pcall(body, [own, got], name=name, grid=(rr // tr,), out_shape=[jax.ShapeDtypeStruct((rr, cc), F32)],
                  in_specs=[spec, pl.BlockSpec((3, tr, cc), lambda i: (0, i, 0))], out_specs=[spec])[0]


def _adamw_halves(name, mine, got, w, m, v, where, by_cols=False):
    rr, cc = mine.shape

    def body(w_ref_, t_ref, r_ref, w_ref, m_ref, v_ref, g_ref, d_ref, mo_ref, vo_ref):
        g = jnp.where(pl.program_id(0) == w_ref_[0], t_ref[...], r_ref[...])
        g_ref[...] = g
        d_ref[...], mo_ref[...], vo_ref[...] = _adamw(w_ref[...], g, m_ref[...], v_ref[...])

    if by_cols:
        grid = (2, 1)
        half = pl.BlockSpec((rr, cc), lambda h, i, w_ref_: (0, 0))
        full = pl.BlockSpec((rr, cc), lambda h, i, w_ref_: (0, h))
    else:
        tr = rr // 2
        grid = (2, 2)
        half = pl.BlockSpec((tr, cc), lambda h, i, w_ref_: (i, 0))
        full = pl.BlockSpec((None, tr, cc), lambda h, i, w_ref_: (0, 2 * h + i, 0))
    return _pcall(body, [where, mine, got, w, m, v], name=name, grid=grid, nprefetch=1,
                  out_shape=[jax.ShapeDtypeStruct(w.shape, F32)] * 4,
                  in_specs=[half, half, full, full, full], out_specs=[full] * 4)


def _bias_table(rel_bias, bucket, mask):
    def body(rb_ref, bk_ref, mk_ref, o_ref):
        bk = bk_ref[...]
        valid = mk_ref[...] > 0
        for h in range(NH):
            acc = jnp.zeros((BLK, 2 * BLK), F32)
            for b in range(NBUCKET):
                acc = jnp.where(bk == b, rb_ref[b, h], acc)
            o_ref[h] = jnp.where(valid, acc, NEG)

    vmem = pl.BlockSpec(memory_space=pltpu.VMEM)
    return pl.pallas_call(
        body, name="bias_table", out_shape=jax.ShapeDtypeStruct((NH, BLK, 2 * BLK), F32),
        in_specs=[pl.BlockSpec(memory_space=pltpu.SMEM), vmem, vmem], out_specs=vmem,
    )(rel_bias, bucket, mask)


def _pack_small(dsh1, p1, dsh2, p2, dg1a, dg1b, dg2, norm1, norm2, scale1, scale2, dcw, dcb, dfn,
                dnw_attn, dnw_ssm, dhd, av, dsink, drel, loss_acc):
    def body(dsh1_ref, p1_ref, dsh2_ref, p2_ref, dg1a_ref, dg1b_ref, dg2_ref, n1_ref, n2_ref, s1_ref, s2_ref,
             dcw_ref, dcb_ref, dfn_ref, da_ref, ds_ref, dhd_ref, av_ref, dsink_ref, drel_ref, loss_ref, o_ref):
        o_ref[...] = jnp.zeros_like(o_ref)
        p1v, p2v = p1_ref[0:1, :], p2_ref[0:1, :]
        o_ref[0:1, :] = dsh1_ref[0:1, :]
        o_ref[1:2, :] = p1v * n1_ref[...]
        o_ref[2:3, :] = dg1a_ref[0:1, :] + dg1b_ref[0:1, :]
        o_ref[3:4, :] = dsh2_ref[0:1, :]
        o_ref[4:5, :] = p2v * n2_ref[...]
        o_ref[5:6, :] = dg2_ref[0:1, :]
        o_ref[6:7, :] = p1v * (1.0 + s1_ref[...])
        o_ref[7:11, :] = dcw_ref[0:4, :]
        o_ref[11:12, :] = dcb_ref[0:1, :]
        o_ref[12:13, :] = p2v * (1.0 + s2_ref[...])
        o_ref[13:14, :] = dfn_ref[0:1, :]
        o_ref[14:15, 0:QW] = da_ref[0:1, :]
        o_ref[15:16, 0:SW] = ds_ref[0:1, :]
        o_ref[16:17, 0:128] = dhd_ref[0:1, :]
        o_ref[17:18, 0:128] = dhd_ref[1:2, :] * av_ref[...]
        o_ref[18:19, 0:128] = dhd_ref[2:3, :]
        o_ref[19:20, 0:128] = dsink_ref[0:1, :]
        o_ref[20:21, 0:128] = loss_ref[0:1, :]
        o_ref[24:56, 0:128] = drel_ref[...]

    return pl.pallas_call(body, name="pack_small", out_shape=jax.ShapeDtypeStruct((56, D), F32))(
        dsh1, p1, dsh2, p2, dg1a, dg1b, dg2, norm1, norm2, scale1, scale2, dcw, dcb, dfn,
        dnw_attn, dnw_ssm, dhd, av, dsink, drel, loss_acc)


def _pad_row(a, rows=1):
    return jnp.pad(a.reshape(rows, -1), ((0, 0), (0, D - a.size // rows)))


def kernel(x, c, ada_w, ada_b, norm1, w_in, conv_w, conv_b, dt_bias, A_log, D_skip, sinks, attn_out_norm, ssm_out_norm, w_o, norm2, w_gate_up, w_down, rel_bias, final_norm, loss_target, m_ada_w, m_ada_b, m_norm1, m_w_in, m_conv_w, m_conv_b, m_dt_bias, m_A_log, m_D_skip, m_sinks, m_attn_out_norm, m_ssm_out_norm, m_w_o, m_norm2, m_w_gate_up, m_w_down, m_rel_bias, m_final_norm, v_ada_w, v_ada_b, v_norm1, v_w_in, v_conv_w, v_conv_b, v_dt_bias, v_A_log, v_D_skip, v_sinks, v_attn_out_norm, v_ssm_out_norm, v_w_o, v_norm2, v_w_gate_up, v_w_down, v_rel_bias, v_final_norm):
    xi, yi, ci = lax.axis_index("x"), lax.axis_index("y"), lax.axis_index("c")
    chip = 2 * xi + yi
    me = 4 * xi + 2 * yi + ci
    where = jnp.stack([ci, chip]).astype(jnp.int32)
    xs2, tgt = x[0], loss_target[0]

    first = jnp.concatenate([c, _pad_row(conv_w[0], CK), jnp.zeros((3, D), F32)], axis=0)
    w_in_t, m_w_in_t, v_w_in_t = w_in[0].T, m_w_in[0].T, v_w_in[0].T
    w_in_b, w_o_b, w_dn_b = w_in_t.astype(BF16), w_o[0].astype(BF16), w_down[0].astype(BF16)
    w_gu_b = w_gate_up[0].astype(BF16)
    hw = D // 2
    fetch_half = _Carry(
        [w_in_b], [jax.ShapeDtypeStruct((4,) + w_in_b.shape, BF16)],
        lambda x_, y_, c_: [(None, 0, slice(None), 0, 2 * x_ + y_)] + [
            (f, 0, (slice(None), pl.ds(c_ * hw, hw)), 0, (2 * x_ + y_, slice(None), pl.ds(c_ * hw, hw))) for f in _CHIPS3])

    def swap_halves(x_, y_, c_):
        there = [(jnp.bitwise_xor(2 * x_ + y_, k + 1), slice(None), pl.ds(c_ * hw, hw)) for k in range(3)]
        return [(_SIBLING, 1, at, 1, at) for at in there]

    first_all, w_in_g = _exchange_two("gather_first", _merge(_gather8_carry(first), fetch_half), swap_halves)
    c_all = first_all[:, 0, :]
    cw_full = jnp.concatenate([first_all[2 * j, 1:1 + CK, 0:256] for j in range(4)], axis=1)
    w_in_f = jnp.pad(w_in_g.reshape(IN_W, D), ((0, PROJ_W - IN_W), (0, 0)))

    ncol = ada_w.shape[2]
    mod_cols = _ada_fwd(c_all, ada_w[0], lax.dynamic_slice(ada_b, (0, chip * ncol), (1, ncol)))
    mod_all = _exchange("gather_mod", _gather_chips_carry([mod_cols]))[0]
    mod = lax.dynamic_slice(jnp.transpose(mod_all, (1, 0, 2)).reshape(8, 4 * ncol), (me, 0), (1, 4 * ncol))
    shift1, scale1, gate1, shift2, scale2, gate2 = [mod[:, j * D:(j + 1) * D] for j in range(6)]
    a1 = norm1 * (1.0 + scale1)
    a2 = norm2 * (1.0 + scale2)

    hdn = DFF // 8
    q, kv, z, xbc, dtr, w_o_g, w_dna_g = _in_proj_fwd(xs2, a1, shift1, w_in_f,
                                                      carry=_gather_chips_carry([w_o_b, w_dn_b[0:hdn]]))
    w_o_f = w_o_g.reshape(D, D)
    bucket, mask = _attn_geometry()
    bucket = jnp.asarray(bucket)
    bias = _bias_table(rel_bias, bucket, jnp.asarray(mask.astype(np.int32)))
    sinks1 = sinks[0]
    ya, w_ga_g = _attn_fwd(q, kv, bias, sinks1, attn_out_norm, carry=_gather_chips_carry([w_gu_b[0:D // 2]]))
    cw8 = jnp.concatenate([cw_full, jnp.zeros((4, XBCW), F32)], axis=0)
    dtb = _pad_row(dt_bias)[:, 0:128]
    av = _pad_row(-jnp.exp(A_log))[:, 0:128]
    dk = jnp.repeat(D_skip, HD, axis=1)
    ys, hs, w_gb_g, w_dnb_g = _ssd_fwd(z, xbc, dtr, cw8, conv_b, dtb, av, dk, ssm_out_norm,
                                       carry=_gather_chips_carry([w_gu_b[D // 2:D], w_dn_b[hdn:2 * hdn]]))
    w_dn_f = jnp.stack([w_dna_g, w_dnb_g], axis=1).reshape(DFF, D)
    fn = final_norm[None, :]
    x1, gu, dx2, loss_acc, dfn = _mlp_fwd(xs2, ya, ys, tgt, w_o_f, w_ga_g, w_gb_g, w_dn_f, gate1, a2, shift2, gate2, fn)

    def to_sibling(p):
        return _Carry([p], [jax.ShapeDtypeStruct((4,) + p.shape[2:], F32)],
                      lambda x_, y_, c_: [(_SIBLING, 0, (j, 1 - c_), 0, j) for j in range(4)])

    def to_chips(s4):
        return _Carry([s4], [jax.ShapeDtypeStruct((3,) + s4.shape[1:], s4.dtype)],
                      lambda x_, y_, c_: [(f, 0, jnp.bitwise_xor(2 * x_ + y_, k + 1), 0, k) for k, f in enumerate(_CHIPS3)])

    def back(t):
        return _Carry([t[None]], [jax.ShapeDtypeStruct((1,) + t.shape, F32)], lambda x_, y_, c_: [(_SIBLING, 0, 0, 0, 0)])

    dx1, dya, dys, act, dgu, h2, dsh2, p2 = _mlp_bwd(x1, gu, dx2, w_o_f, w_ga_g, w_gb_g, w_dn_f, gate1, a2, shift2, gate2)
    p_gu = _wgrad("wgrad_gate_up", h2, dgu, 2 * DFF // 4, stacked=True)[0].reshape(4, 2, D // 2, 2 * DFF // 4)
    g_dn, dg2, got1_gu = _wgrad("wgrad_down", act, dx2, D // 2, gate2, w_dn_f, carry=to_sibling(p_gu))
    p_dn = g_dn.reshape(4, 2, DFF // 8, D)
    s4_gu, own_gu = _add_half("rs_add_half_gu", p_gu, got1_gu, where)
    dq, dkv, dbias, dsink, dnw_attn, got2_gu, got1_dn = _attn_bwd(
        q, kv, dya, bias, sinks1, attn_out_norm, carry=_merge(to_chips(s4_gu), to_sibling(p_dn)))
    drel = _rel_bias_grad(dbias, bucket)
    mine_gu = _add_chips("rs_add_chips_gu", own_gu, got2_gu)
    s4_dn, own_dn = _add_half("rs_add_half_dn", p_dn, got1_dn, where)
    dz, dxbc, ddt, dcw, dcb, dnw_ssm, dhd, got2_dn, got3_gu = _ssd_bwd(
        z, xbc, dtr, dys, hs, cw8, conv_b, dtb, av, dk, ssm_out_norm, carry=_merge(to_chips(s4_dn), back(mine_gu)))
    mine_dn = _add_chips("rs_add_chips_dn", own_dn, got2_dn)
    grad_x, dproj, h1, dsh1, p1 = _in_proj_bwd(xs2, dx1, a1, shift1, w_in_f, dq, dkv, dz, dxbc, ddt)
    g_in_t, got3_dn = _wgrad_in_t(h1, dproj, carry=back(mine_dn))
    p_in = g_in_t[0:IN_W].reshape(4, IN_W // 4, D)

    def to_sibling_cols(p):
        return _Carry([p], [jax.ShapeDtypeStruct(p.shape[:2] + (D // 2,), F32)],
                      lambda x_, y_, c_: [(_SIBLING, 0, (j, slice(None), pl.ds((1 - c_) * (D // 2), D // 2)), 0, j)
                                          for j in range(4)])

    g_oa, dg1a, got1_in = _wgrad("wgrad_o_attn", ya, dx1, D, gate1, w_o_f[0:QW], carry=to_sibling_cols(p_in))
    s4_in, own_in = _add_half("rs_add_half_in", p_in, got1_in, where, by_cols=True)
    g_os, dg1b, got2_in = _wgrad("wgrad_o_ssm", ys, dx1, D, gate1, w_o_f[QW:D], carry=to_chips(s4_in))
    mine_in = _add_chips("rs_add_chips_in", own_in, got2_in)
    p_o = jnp.concatenate([g_oa, g_os], axis=0).reshape(4, 2, D // 8, D)

    small = _pack_small(dsh1, p1, dsh2, p2, dg1a, dg1b, dg2, norm1, norm2, scale1, scale2, dcw, dcb, dfn,
                        dnw_attn, dnw_ssm, dhd, av, dsink, drel, loss_acc)
    small_all, got1_o, got3_in = _exchange(
        "gather_small", _merge(_gather8_carry(small), to_sibling(p_o), back(mine_in)))
    s4_o, own_o = _add_half("rs_add_half_o", p_o, got1_o, where)
    mine_o = _add_chips("rs_add_chips_o", own_o, _exchange("rs_chips_o", to_chips(s4_o))[0])
    got3_o = _exchange("rs_back_o", back(mine_o))[0]
    small_res, loss = _small_update(
        small_all, where,
        [ada_b, norm1, conv_w, conv_b, dt_bias, A_log, D_skip, sinks, attn_out_norm, ssm_out_norm, norm2, rel_bias,
         final_norm[None, :]],
        [m_ada_b, m_norm1, m_conv_w, m_conv_b, m_dt_bias, m_A_log, m_D_skip, m_sinks, m_attn_out_norm,
         m_ssm_out_norm, m_norm2, m_rel_bias, m_final_norm[None, :]],
        [v_ada_b, v_norm1, v_conv_w, v_conv_b, v_dt_bias, v_A_log, v_D_skip, v_sinks, v_attn_out_norm,
         v_ssm_out_norm, v_norm2, v_rel_bias, v_final_norm[None, :]])
    small_out = [dict(zip(_SMALL, r)) for r in small_res]
    for r in small_out:
        r["final_norm"] = r["final_norm"][0]

    dmod_all = small_all[:, 0:6, :].reshape(8, 6 * D)
    dmod_loc = lax.dynamic_slice(dmod_all, (0, chip * ncol), (8, ncol))
    ada_out = _ada_bwd_adamw(c_all.T, dmod_loc, ada_w[0], m_ada_w[0], v_ada_w[0])

    big_gu = _adamw_halves("adamw_gate_up", mine_gu, got3_gu[0], w_gate_up, m_w_gate_up, v_w_gate_up, where)
    big_dn = _adamw_halves("adamw_down", mine_dn, got3_dn[0], w_down, m_w_down, v_w_down, where)
    big_o = _adamw_halves("adamw_o", mine_o, got3_o[0], w_o, m_w_o, v_w_o, where)
    big_in = [o.T[None] for o in _adamw_halves("adamw_in", mine_in, got3_in[0], w_in_t, m_w_in_t, v_w_in_t, where,
                                               by_cols=True)]
    big = [big_in, big_o, big_gu, big_dn]

    order = ["ada_w", "ada_b", "norm1", "w_in", "conv_w", "conv_b", "dt_bias", "A_log", "D_skip", "sinks",
             "attn_out_norm", "ssm_out_norm", "w_o", "norm2", "w_gate_up", "w_down", "rel_bias", "final_norm"]
    bigname = {"w_in": 0, "w_o": 1, "w_gate_up": 2, "w_down": 3}
    res = [loss, grad_x[None]]
    for kind in range(4):
        for nm in order:
            if nm == "ada_w":
                res.append(ada_out[kind][None])
            elif nm in bigname:
                res.append(big[bigname[nm]][kind])
            else:
                res.append(small_out[kind][nm])
    return tuple(res)
```

```python
import numpy as np
import jax
import jax.numpy as jnp
from jax import lax
from jax.experimental import pallas as pl
from jax.experimental.pallas import tpu as pltpu

F32, BF16 = jnp.float32, jnp.bfloat16
HI = lax.Precision.HIGHEST

D = 1024
QW, KVW = 512, 128
NH, HD, NKV = 8, 64, 2
SW = 512
NST = 128
XBCW = 1024
CK = 4
BLK = 128
DFF = 2816
IN_W = 2312
PROJ_W = 2432
EPS = 1e-6
NEG = -1e30
NBUCKET = 32

B1, B2, LR, AEPS, WD, STEP = 0.9, 0.999, 0.001, 1e-08, 0.01, 10

VMEM_LIMIT = 56 * 1024 * 1024

_NT = (((1,), (1,)), ((), ()))
_TN = (((0,), (0,)), ((), ()))


def _mm(a, b):
    return jnp.dot(a, b, preferred_element_type=F32)


def _mm_nt(a, b):
    return lax.dot_general(a, b, _NT, preferred_element_type=F32)


def _mm_tn(a, b):
    return lax.dot_general(a, b, _TN, preferred_element_type=F32)


def _mm_hi(a, b):
    return jnp.dot(a, b, preferred_element_type=F32, precision=HI)


def _split3(x):
    hi = x.astype(BF16)
    r = x - hi.astype(F32)
    mid = r.astype(BF16)
    lo = (r - mid.astype(F32)).astype(BF16)
    return hi, mid, lo


def _sel_r(x, e):
    hi, mid, lo = _split3(x)
    return (_mm(hi, e) + _mm(mid, e)) + _mm(lo, e)


def _sel_l(e, x):
    hi, mid, lo = _split3(x)
    return (_mm(e, hi) + _mm(e, mid)) + _mm(e, lo)


def _sig(x):
    return 1.0 / (1.0 + jnp.exp(-x))


def _cp(sem):
    return pltpu.CompilerParams(dimension_semantics=sem, vmem_limit_bytes=VMEM_LIMIT)


def _row(shape):
    nd = len(shape)
    return pl.BlockSpec(shape, lambda *_: (0,) * nd)


def _adamw(w, g, m, v):
    m = B1 * m + (1.0 - B1) * g
    v = B2 * v + (1.0 - B2) * (g * g)
    m_hat = m / (1.0 - B1 ** STEP)
    v_hat = v / (1.0 - B2 ** STEP)
    delta = -LR * (m_hat / (jnp.sqrt(v_hat) + AEPS) + WD * w)
    return delta, m, v


class _Carry:
    def __init__(self, inps, outs, copies):
        self.inps, self.outs, self.copies = list(inps), list(outs), copies
        self.n = len(copies(0, 0, 0))

    def descriptors(self, in_refs, out_refs, send_sems, recv_sems):
        x, y, c = lax.axis_index("x"), lax.axis_index("y"), lax.axis_index("c")
        out = []
        for j, (flip, a, si, o, di) in enumerate(self.copies(x, y, c)):
            if flip is None:
                out.append(pltpu.make_async_copy(in_refs[a].at[si], out_refs[o].at[di], send_sems.at[j]))
            else:
                fx, fy, fc = flip
                peer = (1 - x if fx else x, 1 - y if fy else y, 1 - c if fc else c)
                out.append(pltpu.make_async_remote_copy(
                    src_ref=in_refs[a].at[si], dst_ref=out_refs[o].at[di],
                    send_sem=send_sems.at[j], recv_sem=recv_sems.at[j],
                    device_id=peer, device_id_type=pl.DeviceIdType.MESH))
        return out


def _pcall(body, args, *, name, grid, in_specs, out_specs, out_shape, scratch_shapes=(), sem=None, nprefetch=0,
           carry=None):
    out_shape, out_specs = list(out_shape), list(out_specs)
    in_specs, scratch_shapes = list(in_specs), list(scratch_shapes)
    nin, nout, nscr = len(in_specs), len(out_shape), len(scratch_shapes)
    run = body
    if carry is not None:
        ncin, ncout = len(carry.inps), len(carry.outs)
        hbm = pl.BlockSpec(memory_space=pl.ANY)

        def run(*refs):
            pre, r = refs[:nprefetch], refs[nprefetch:]
            ins, cins = r[:nin], r[nin:nin + ncin]
            r = r[nin + ncin:]
            outs, couts = r[:nout], r[nout:nout + ncout]
            r = r[nout + ncout:]
            scr, (send_sems, recv_sems) = r[:nscr], r[nscr:]
            first = pl.program_id(0) == 0
            last = pl.program_id(0) == grid[0] - 1
            for ax in range(1, len(grid)):
                first = jnp.logical_and(first, pl.program_id(ax) == 0)
                last = jnp.logical_and(last, pl.program_id(ax) == grid[ax] - 1)

            @pl.when(first)
            def _():
                for d in carry.descriptors(cins, couts, send_sems, recv_sems):
                    d.start()

            body(*pre, *ins, *outs, *scr)

            @pl.when(last)
            def _():
                for d in carry.descriptors(cins, couts, send_sems, recv_sems):
                    d.wait()

        in_specs = in_specs + [hbm] * ncin
        out_specs = out_specs + [hbm] * ncout
        out_shape = out_shape + carry.outs
        scratch_shapes = scratch_shapes + [pltpu.SemaphoreType.DMA((carry.n,)), pltpu.SemaphoreType.DMA((carry.n,))]
        args = list(args) + carry.inps
    if sem is None:
        sem = ("arbitrary",) * len(grid)
    if nprefetch:
        kw = dict(grid_spec=pltpu.PrefetchScalarGridSpec(num_scalar_prefetch=nprefetch, grid=grid, in_specs=in_specs,
                                                         out_specs=out_specs, scratch_shapes=scratch_shapes))
    else:
        kw = dict(grid=grid, in_specs=in_specs, out_specs=out_specs, scratch_shapes=scratch_shapes)
    res = pl.pallas_call(run, name=name, out_shape=out_shape, compiler_params=_cp(sem), **kw)(*args)
    return list(res)


def _merge(*carries):
    inps, outs, offs = [], [], []
    for cr in carries:
        offs.append((len(inps), len(outs)))
        inps += cr.inps
        outs += cr.outs

    def copies(x, y, c):
        return [(f, a + io, si, o + oo, di) for cr, (io, oo) in zip(carries, offs) for f, a, si, o, di in cr.copies(x, y, c)]

    return _Carry(inps, outs, copies)


def _exchange(name, carry):
    return _pcall(lambda: None, [], name=name, grid=(1,), in_specs=[], out_specs=[], out_shape=[], carry=carry)


def _exchange_two(name, carry, then):
    second = _Carry([], [], then)
    nin, nout = len(carry.inps), len(carry.outs)

    def body(*refs):
        ins, outs = refs[:nin], refs[nin:nin + nout]
        send_a, recv_a, send_b, recv_b = refs[nin + nout:]
        for descs in (carry.descriptors(ins, outs, send_a, recv_a), second.descriptors(outs, outs, send_b, recv_b)):
            for d in descs:
                d.start()
            for d in descs:
                d.wait()

    hbm = pl.BlockSpec(memory_space=pl.ANY)
    return list(pl.pallas_call(
        body, name=name, out_shape=carry.outs, in_specs=[hbm] * nin, out_specs=[hbm] * nout,
        scratch_shapes=[pltpu.SemaphoreType.DMA((carry.n,)), pltpu.SemaphoreType.DMA((carry.n,)),
                        pltpu.SemaphoreType.DMA((second.n,)), pltpu.SemaphoreType.DMA((second.n,))],
    )(*carry.inps))


_ALL7 = [(f >> 2 & 1, f >> 1 & 1, f & 1) for f in range(1, 8)]
_CHIPS3 = [(0, 1, 0), (1, 0, 0), (1, 1, 0)]
_SIBLING = (0, 0, 1)


def _gather8_carry(blk):
    def copies(x, y, c):
        me = 4 * x + 2 * y + c
        return [(None, 0, 0, 0, me)] + [(f, 0, 0, 0, me) for f in _ALL7]

    return _Carry([blk[None]], [jax.ShapeDtypeStruct((8,) + blk.shape, blk.dtype)], copies)


def _gather_chips_carry(blks):
    def copies(x, y, c):
        chip = 2 * x + y
        return [(f, a, 0, a, chip) for a in range(len(blks)) for f in [None] + _CHIPS3]

    return _Carry([b[None] for b in blks], [jax.ShapeDtypeStruct((4,) + b.shape, b.dtype) for b in blks], copies)


def _ada_fwd(c_all, w_loc, b_loc):
    n = w_loc.shape[1]
    tn = 512

    def body(c_ref, w_ref, b_ref, o_ref):
        cv = c_ref[...]
        cond = cv * _sig(cv)
        o_ref[...] = _mm_hi(cond, w_ref[...]) + b_ref[...]

    return pl.pallas_call(
        body, name="ada_fwd", grid=(n // tn,),
        out_shape=jax.ShapeDtypeStruct((8, n), F32),
        in_specs=[_row((8, D)), pl.BlockSpec((D, tn), lambda j: (0, j)), pl.BlockSpec((1, tn), lambda j: (0, j))],
        out_specs=pl.BlockSpec((8, tn), lambda j: (0, j)),
        compiler_params=_cp(("parallel",)),
    )(c_all, w_loc, b_loc)


def _ada_bwd_adamw(c_all_t, dmod_loc, w, m, v, carry=None):
    n = w.shape[1]
    tn = 512

    def body(ct_ref, dm_ref, w_ref, m_ref, v_ref, g_ref, d_ref, mo_ref, vo_ref):
        ct = ct_ref[...]
        cond = ct * _sig(ct)
        dm = dm_ref[...]
        g = cond[:, 0:1] * dm[0:1, :]
        for b in range(1, 8):
            g = g + cond[:, b:b + 1] * dm[b:b + 1, :]
        g_ref[...] = g
        d_ref[...], mo_ref[...], vo_ref[...] = _adamw(w_ref[...], g, m_ref[...], v_ref[...])

    wspec = pl.BlockSpec((D, tn), lambda j: (0, j))
    return _pcall(
        body, [c_all_t, dmod_loc, w, m, v], name="ada_bwd_adamw", grid=(n // tn,),
        out_shape=[jax.ShapeDtypeStruct((D, n), F32)] * 4,
        in_specs=[_row((D, 8)), pl.BlockSpec((8, tn), lambda j: (0, j)), wspec, wspec, wspec],
        out_specs=[wspec] * 4, carry=carry)


def _in_proj_fwd(x, a1, sh1, w_in, carry=None):
    s = x.shape[0]
    tm = 512

    def body(x_ref, a_ref, s_ref, w_ref, q_ref, kv_ref, z_ref, xbc_ref, dt_ref):
        xv = x_ref[...]
        r = lax.rsqrt(jnp.mean(xv * xv, axis=-1, keepdims=True) + EPS)
        h = (xv * r * a_ref[...] + s_ref[...]).astype(BF16)
        p = _mm_nt(h, w_ref[...])
        q_ref[...] = p[:, 0:512].astype(BF16)
        kv_ref[...] = p[:, 512:768].astype(BF16)
        z_ref[...] = p[:, 768:1280]
        xbc_ref[...] = p[:, 1280:2304]
        dt_ref[...] = p[:, 2304:2432]

    def tok(w):
        return pl.BlockSpec((tm, w), lambda i: (i, 0))

    return _pcall(
        body, [x, a1, sh1, w_in], name="in_proj_fwd", grid=(s // tm,),
        out_shape=[jax.ShapeDtypeStruct((s, QW), BF16), jax.ShapeDtypeStruct((s, 2 * KVW), BF16),
                   jax.ShapeDtypeStruct((s, SW), F32), jax.ShapeDtypeStruct((s, XBCW), F32),
                   jax.ShapeDtypeStruct((s, 128), F32)],
        in_specs=[tok(D), _row((1, D)), _row((1, D)), _row((PROJ_W, D))],
        out_specs=[tok(QW), tok(2 * KVW), tok(SW), tok(XBCW), tok(128)], carry=carry)


def _in_proj_bwd(x, dx1, a1, sh1, w_in, dq, dkv, dz, dxbc, ddt, carry=None):
    s = x.shape[0]
    tm = 512

    def body(x_ref, dx1_ref, a_ref, s_ref, w_ref, dq_ref, dkv_ref, dz_ref, dxbc_ref, ddt_ref,
             gx_ref, dproj_ref, h_ref, dsh_ref, p_ref):
        i = pl.program_id(0)

        @pl.when(i == 0)
        def _():
            dsh_ref[...] = jnp.zeros_like(dsh_ref)
            p_ref[...] = jnp.zeros_like(p_ref)

        dproj = jnp.concatenate([dq_ref[...], dkv_ref[...], dz_ref[...], dxbc_ref[...], ddt_ref[...]], axis=1)
        dproj_ref[...] = dproj
        dh = _mm(dproj, w_ref[...])
        xv = x_ref[...]
        r = lax.rsqrt(jnp.mean(xv * xv, axis=-1, keepdims=True) + EPS)
        xn = xv * r
        a = a_ref[...]
        h_ref[...] = (xn * a + s_ref[...]).astype(BF16)
        dsh_ref[0:1, :] += jnp.sum(dh, axis=0, keepdims=True)
        p_ref[0:1, :] += jnp.sum(dh * xn, axis=0, keepdims=True)
        u = dh * a
        gx_ref[...] = dx1_ref[...] + r * u - xn * (r * jnp.mean(u * xn, axis=-1, keepdims=True))

    def tok(w):
        return pl.BlockSpec((tm, w), lambda i: (i, 0))

    return _pcall(
        body, [x, dx1, a1, sh1, w_in, dq, dkv, dz, dxbc, ddt], name="in_proj_bwd", grid=(s // tm,),
        out_shape=[jax.ShapeDtypeStruct((s, D), F32), jax.ShapeDtypeStruct((s, PROJ_W), BF16),
                   jax.ShapeDtypeStruct((s, D), BF16), jax.ShapeDtypeStruct((8, D), F32),
                   jax.ShapeDtypeStruct((8, D), F32)],
        in_specs=[tok(D), tok(D), _row((1, D)), _row((1, D)), _row((PROJ_W, D)),
                  tok(QW), tok(2 * KVW), tok(SW), tok(XBCW), tok(128)],
        out_specs=[tok(D), tok(PROJ_W), tok(D), _row((8, D)), _row((8, D))], carry=carry)


def _attn_geometry():
    dist = np.arange(BLK)[:, None] + BLK - np.arange(2 * BLK)[None, :]
    n = np.maximum(dist, 0)
    max_exact = NBUCKET // 2
    large = max_exact + (np.log(np.maximum(n, 1) / max_exact) / np.log(128 / max_exact)
                         * (NBUCKET - max_exact)).astype(np.int32)
    large = np.minimum(large, NBUCKET - 1)
    bucket = np.where(n < max_exact, n, large).astype(np.int32)
    mask = (dist >= 0) & (dist < 128)
    return bucket, mask


def _attn_heads(is_first, q_blk, kvw, bias_ref, sinks_ref):
    qv = q_blk * 0.125
    col = lax.broadcasted_iota(jnp.int32, (BLK, 2 * BLK), 1)
    first = jnp.where(jnp.logical_and(is_first, col < BLK), NEG, 0.0)
    groups = []
    for g in range(NKV):
        qs = jnp.concatenate([qv[:, (4 * g + r) * HD:(4 * g + r + 1) * HD] for r in range(4)], axis=0)
        kw = kvw[:, g * HD:(g + 1) * HD]
        vw = kvw[:, KVW + g * HD:KVW + (g + 1) * HD]
        sc = _mm_nt(qs, kw)
        pn, ps = [], []
        for r in range(4):
            h = 4 * g + r
            sr = sc[r * BLK:(r + 1) * BLK] + bias_ref[h] + first
            sink = sinks_ref[h]
            m = jnp.maximum(jnp.max(sr, axis=-1, keepdims=True), sink)
            p = jnp.exp(sr - m)
            es = jnp.exp(sink - m)
            inv = 1.0 / (jnp.sum(p, axis=-1, keepdims=True) + es)
            pn.append(p * inv)
            ps.append(es * inv)
        pn = jnp.concatenate(pn, axis=0)
        ps = jnp.concatenate(ps, axis=0)
        o = _mm(pn.astype(BF16), vw)
        groups.append((qs, kw, vw, pn, ps, o))
    return groups


def _unstack_heads(parts):
    return jnp.concatenate([p[r * BLK:(r + 1) * BLK] for p in parts for r in range(4)], axis=1)


def _attn_fwd(q, kv, bias, sinks, nw, carry=None):
    s = q.shape[0]

    def body(q_ref, kvp_ref, kvc_ref, bias_ref, sinks_ref, nw_ref, y_ref):
        t = pl.program_id(0)
        kv3 = jnp.concatenate([kvp_ref[...], kvc_ref[...]], axis=0)
        for sub in range(2):
            rows = slice(sub * BLK, (sub + 1) * BLK)
            groups = _attn_heads(jnp.logical_and(t == 0, sub == 0), q_ref[rows, :], kv3[sub * BLK:(sub + 2) * BLK],
                                 bias_ref, sinks_ref)
            o = _unstack_heads([g[5] for g in groups])
            r = lax.rsqrt(jnp.mean(o * o, axis=-1, keepdims=True) + EPS)
            y_ref[rows, :] = (o * r * nw_ref[...]).astype(BF16)

    return _pcall(
        body, [q, kv, kv, bias, sinks, nw], name="attn_fwd", grid=(s // (2 * BLK),),
        out_shape=[jax.ShapeDtypeStruct((s, QW), BF16)],
        in_specs=[pl.BlockSpec((2 * BLK, QW), lambda t: (t, 0)),
                  pl.BlockSpec((BLK, 2 * KVW), lambda t: (jnp.maximum(2 * t - 1, 0), 0)),
                  pl.BlockSpec((2 * BLK, 2 * KVW), lambda t: (t, 0)),
                  _row((NH, BLK, 2 * BLK)),
                  pl.BlockSpec(memory_space=pltpu.SMEM),
                  _row((1, QW))],
        out_specs=[pl.BlockSpec((2 * BLK, QW), lambda t: (t, 0))], carry=carry)


def _attn_bwd(q, kv, dya, bias, sinks, nw, carry=None):
    s = q.shape[0]
    nt = s // (2 * BLK)

    def body(q_ref, kvp_ref, kvc_ref, dy_ref, bias_ref, sinks_ref, nw_ref,
             dq_ref, dkv_ref, dbias_ref, dsink_ref, dnw_ref, carry_ref, held_ref):
        t = pl.program_id(0)

        @pl.when(t == 0)
        def _():
            carry_ref[...] = jnp.zeros_like(carry_ref)
            held_ref[...] = jnp.zeros_like(held_ref)
            dbias_ref[...] = jnp.zeros_like(dbias_ref)
            dsink_ref[...] = jnp.zeros_like(dsink_ref)
            dnw_ref[...] = jnp.zeros_like(dnw_ref)

        def block(sub, kv3):
            rows = slice(sub * BLK, (sub + 1) * BLK)
            groups = _attn_heads(jnp.logical_and(t == 0, sub == 0), q_ref[rows, :], kv3[sub * BLK:(sub + 2) * BLK],
                                 bias_ref, sinks_ref)
            o = _unstack_heads([g[5] for g in groups])
            r = lax.rsqrt(jnp.mean(o * o, axis=-1, keepdims=True) + EPS)
            dy = dy_ref[rows, :]
            on = o * r
            dnw_ref[0:1, :] += jnp.sum(dy * on, axis=0, keepdims=True)
            u = dy * nw_ref[...]
            do = r * u - on * (r * jnp.mean(u * on, axis=-1, keepdims=True))
            dq_parts, dk_parts, dv_parts = [], [], []
            for g, (qs, kw, vw, pn, ps, og) in enumerate(groups):
                dos = jnp.concatenate([do[:, (4 * g + r_) * HD:(4 * g + r_ + 1) * HD] for r_ in range(4)], axis=0)
                delta = jnp.sum(dos * og, axis=-1, keepdims=True)
                dp = _mm_nt(dos.astype(BF16), vw)
                ds = pn * (dp - delta)
                dsk = ps * delta
                lane = lax.broadcasted_iota(jnp.int32, (1, 128), 1)
                for r_ in range(4):
                    h = 4 * g + r_
                    dbias_ref[h] += ds[r_ * BLK:(r_ + 1) * BLK]
                    dsink_ref[0:1, :] -= jnp.where(lane == h, jnp.sum(dsk[r_ * BLK:(r_ + 1) * BLK]), 0.0)
                dsb = ds.astype(BF16)
                dq_parts.append(_mm(dsb, kw) * 0.125)
                dk_parts.append(_mm_tn(dsb, qs))
                dv_parts.append(_mm_tn(pn.astype(BF16), dos.astype(BF16)))
            dq_ref[rows, :] = _unstack_heads(dq_parts).astype(BF16)
            return jnp.concatenate(dk_parts + dv_parts, axis=1)

        @pl.when(t < nt)
        def _():
            kv3 = jnp.concatenate([kvp_ref[...], kvc_ref[...]], axis=0)
            d0 = block(0, kv3)
            dkv_ref[0:BLK, :] = held_ref[...].astype(BF16)
            dkv_ref[BLK:2 * BLK, :] = (carry_ref[...] + d0[0:BLK]).astype(BF16)
            d1 = block(1, kv3)
            held_ref[...] = d0[BLK:2 * BLK] + d1[0:BLK]
            carry_ref[...] = d1[BLK:2 * BLK]

        @pl.when(t == nt)
        def _():
            dkv_ref[0:BLK, :] = held_ref[...].astype(BF16)
            dkv_ref[BLK:2 * BLK, :] = carry_ref[...].astype(BF16)

    last = nt - 1
    tile = lambda w: pl.BlockSpec((2 * BLK, w), lambda t: (jnp.minimum(t, last), 0))
    return _pcall(
        body, [q, kv, kv, dya, bias, sinks, nw], name="attn_bwd", grid=(nt + 1,),
        out_shape=[jax.ShapeDtypeStruct((s, QW), BF16), jax.ShapeDtypeStruct((s, 2 * KVW), BF16),
                   jax.ShapeDtypeStruct((NH, BLK, 2 * BLK), F32), jax.ShapeDtypeStruct((NH, 128), F32),
                   jax.ShapeDtypeStruct((8, QW), F32)],
        in_specs=[tile(QW),
                  pl.BlockSpec((BLK, 2 * KVW), lambda t: (jnp.clip(2 * t - 1, 0, 2 * nt - 1), 0)),
                  tile(2 * KVW), tile(QW),
                  _row((NH, BLK, 2 * BLK)),
                  pl.BlockSpec(memory_space=pltpu.SMEM),
                  _row((1, QW))],
        out_specs=[tile(QW),
                   pl.BlockSpec((2 * BLK, 2 * KVW), lambda t: (jnp.maximum(t - 1, 0), 0)),
                   _row((NH, BLK, 2 * BLK)), _row((NH, 128)), _row((8, QW))],
        scratch_shapes=[pltpu.VMEM((BLK, 2 * KVW), F32), pltpu.VMEM((BLK, 2 * KVW), F32)], carry=carry)


def _rel_bias_grad(dbias, bucket):
    def body(db_ref, bk_ref, o_ref):
        bk = bk_ref[...]
        lane = lax.broadcasted_iota(jnp.int32, (1, 128), 1)
        for b in range(NBUCKET):
            sel = bk == b
            row = jnp.zeros((1, 128), F32)
            for h in range(NH):
                row = row + jnp.where(lane == h, jnp.sum(jnp.where(sel, db_ref[h], 0.0)), 0.0)
            o_ref[b:b + 1, :] = row

    return pl.pallas_call(
        body, name="rel_bias_grad",
        out_shape=jax.ShapeDtypeStruct((NBUCKET, 128), F32),
    )(dbias, bucket)


def _ssd_consts():
    head_of_lane = np.arange(SW) // HD
    expand = (np.arange(128)[:, None] == head_of_lane[None, :]).astype(np.float32)
    tril = np.tril(np.ones((BLK, BLK), np.float32))
    return (jnp.asarray(expand, BF16), jnp.asarray(expand.T.copy(), BF16), jnp.asarray(tril, BF16),
            jnp.asarray(tril.T.copy(), BF16))


def _conv_pre(xc, halo, cw, cb):
    ext = jnp.concatenate([halo, xc], axis=0)
    taps = [xc if k == CK - 1 else pltpu.roll(ext, CK - 1 - k, 0)[8:8 + BLK] for k in range(CK)]
    return cb + sum(cw[k:k + 1, :] * taps[k] for k in range(CK))


def _ssd_chunk(pre, dtr, dtb, av, dkv, ex, tril, h_in):
    sp = _sig(pre)
    xbc = pre * sp
    xs, bm, cm = xbc[:, 0:SW], xbc[:, SW:SW + 2 * NST], xbc[:, SW + 2 * NST:]
    dtin = dtr + dtb
    dt = jnp.maximum(dtin, 0.0) + jnp.log1p(jnp.exp(-jnp.abs(dtin)))
    cs = _sel_l(tril, dt * av)
    cst = cs.T
    dtx = _sel_r(dt, ex)
    csx = _sel_r(cs, ex)
    xdt = xs * dtx
    csl = csx[BLK - 1:BLK, :]
    decx = jnp.exp(csl - csx)
    ecsx = jnp.exp(csx)
    ecl = jnp.exp(csl)
    causal = tril.astype(F32) > 0.5
    ydiag, yoff, cbs, lms = [], [], [], []
    for g in range(2):
        bg = bm[:, g * NST:(g + 1) * NST].astype(BF16)
        cg = cm[:, g * NST:(g + 1) * NST].astype(BF16)
        cb = _mm_nt(cg, bg)
        cbs.append(cb)
        yoff.append(_mm(cg, h_in[:, g * 256:(g + 1) * 256].astype(BF16)))
        for r in range(4):
            h = 4 * g + r
            seg = cs[:, h:h + 1] - cst[h:h + 1, :]
            lm = jnp.where(causal, jnp.exp(jnp.minimum(seg, 0.0)), 0.0)
            lms.append(lm)
            ydiag.append(_mm((cb * lm).astype(BF16), xdt[:, h * HD:(h + 1) * HD].astype(BF16)))
    yoff = jnp.concatenate(yoff, axis=1) * ecsx
    y = jnp.concatenate(ydiag, axis=1) + yoff + dkv * xs
    return dict(pre=pre, sp=sp, xs=xs, bm=bm, cm=cm, dtin=dtin, dt=dt, av=av, cs=cs, cst=cst,
                dtx=dtx, csx=csx, xdt=xdt, decx=decx, ecsx=ecsx, ecl=ecl, causal=causal, cbs=cbs, lms=lms,
                yoff=yoff, y=y)


def _group_mean(t):
    m0 = jnp.mean(t[:, 0:256], axis=-1, keepdims=True)
    m1 = jnp.mean(t[:, 256:512], axis=-1, keepdims=True)
    return jnp.concatenate([jnp.broadcast_to(m0, (t.shape[0], 256)), jnp.broadcast_to(m1, (t.shape[0], 256))], axis=1)


SUBS = 2


def _ssd_fwd(z, xbc, dtr, cw, cb, dtb, av, dk, nw, carry=None):
    s = z.shape[0]
    nc = s // BLK
    tile = SUBS * BLK
    ex, _, tril, _ = _ssd_consts()

    def body(z_ref, xc_ref, xh_ref, dtr_ref, cw_ref, cb_ref, dtb_ref, a_ref, dk_ref, nw_ref, ex_ref, tril_ref,
             y_ref, hs_ref, pre_ref, h_ref):
        t = pl.program_id(0)

        @pl.when(t == 0)
        def _():
            h_ref[...] = jnp.zeros_like(h_ref)

        h_in = h_ref[...]
        for sub in range(SUBS):
            rows = slice(sub * BLK, (sub + 1) * BLK)
            xc = xc_ref[rows, :]
            halo = jnp.where(t == 0, 0.0, xh_ref[...]) if sub == 0 else xc_ref[sub * BLK - 8:sub * BLK, :]
            pre = _conv_pre(xc, halo, cw_ref[...], cb_ref[...])
            pre_ref[rows, :] = pre
            hs_ref[sub] = h_in
            f = _ssd_chunk(pre, dtr_ref[rows, :], dtb_ref[...], a_ref[...], dk_ref[...], ex_ref[...], tril_ref[...], h_in)
            dx = (f["decx"] * f["xdt"]).astype(BF16)
            st = [_mm_tn(f["bm"][:, g * NST:(g + 1) * NST].astype(BF16), dx[:, g * 256:(g + 1) * 256]) for g in range(2)]
            h_in = h_in * f["ecl"] + jnp.concatenate(st, axis=1)
            zv = z_ref[rows, :]
            tg = f["y"] * (zv * _sig(zv))
            r = lax.rsqrt(_group_mean(tg * tg) + EPS)
            y_ref[rows, :] = (tg * r * nw_ref[...]).astype(BF16)
        h_ref[...] = h_in

    cur = lambda w: pl.BlockSpec((tile, w), lambda t: (t, 0))
    return _pcall(
        body, [z, xbc, xbc, dtr, cw, cb, dtb, av, dk, nw, ex, tril], name="ssd_fwd", grid=(s // tile,),
        out_shape=[jax.ShapeDtypeStruct((s, SW), BF16), jax.ShapeDtypeStruct((nc, NST, SW), F32),
                   jax.ShapeDtypeStruct((s, XBCW), F32)],
        in_specs=[cur(SW), cur(XBCW), pl.BlockSpec((8, XBCW), lambda t: (jnp.maximum(t * (tile // 8) - 1, 0), 0)),
                  cur(128), _row((8, XBCW)), _row((1, XBCW)), _row((1, 128)),
                  _row((1, 128)), _row((1, SW)), _row((1, SW)), _row((128, SW)), _row((BLK, BLK))],
        out_specs=[cur(SW), pl.BlockSpec((SUBS, NST, SW), lambda t: (t, 0, 0)), cur(XBCW)],
        scratch_shapes=[pltpu.VMEM((NST, SW), F32)], carry=carry)


def _ssd_bwd(z, xbc, pre_all, dtr, dys, hs, cw, dtb, av, dk, nw, carry=None):
    s = z.shape[0]
    tile = SUBS * BLK
    nt = s // tile
    ex, ext_t, tril, triu = _ssd_consts()

    def body(z_ref, xc_ref, pre_ref, dtr_ref, dy_ref, hs_ref, cw_ref, dtb_ref, a_ref, dk_ref, nw_ref,
             ex_ref, ext_ref, tril_ref, triu_ref,
             dz_ref, dxbc_ref, ddt_ref, dcw_ref, dcb_ref, dnw_ref, dhd_ref, dh_ref, nxt_ref, dd_ref):
        i = pl.program_id(0)

        @pl.when(i == 0)
        def _():
            dh_ref[...] = jnp.zeros_like(dh_ref)
            nxt_ref[...] = jnp.zeros_like(nxt_ref)
            dd_ref[...] = jnp.zeros_like(dd_ref)
            dcw_ref[...] = jnp.zeros_like(dcw_ref)
            dcb_ref[...] = jnp.zeros_like(dcb_ref)
            dnw_ref[...] = jnp.zeros_like(dnw_ref)
            dhd_ref[...] = jnp.zeros_like(dhd_ref)

        gst, nxt = dh_ref[...], nxt_ref[...]
        for sub in reversed(range(SUBS)):
            rows = slice(sub * BLK, (sub + 1) * BLK)
            gst, nxt = chunk(sub, rows, gst, nxt, z_ref, xc_ref, pre_ref, dtr_ref, dy_ref, hs_ref, cw_ref, dtb_ref,
                             a_ref, dk_ref, nw_ref, ex_ref, ext_ref, tril_ref, triu_ref,
                             dz_ref, dxbc_ref, ddt_ref, dcw_ref, dcb_ref, dnw_ref, dhd_ref, dd_ref)
        dh_ref[...] = gst
        nxt_ref[...] = nxt

        @pl.when(i == nt - 1)
        def _():
            dhd_ref[2:3, :] = _sel_r(dd_ref[...], ext_ref[...])[0:1, :]

    def chunk(sub, rows, gst, nxt, z_ref, xc_ref, pre_ref, dtr_ref, dy_ref, hs_ref, cw_ref, dtb_ref,
              a_ref, dk_ref, nw_ref, ex_ref, ext_ref, tril_ref, triu_ref,
              dz_ref, dxbc_ref, ddt_ref, dcw_ref, dcb_ref, dnw_ref, dhd_ref, dd_ref):
        h_in = hs_ref[sub]
        f = _ssd_chunk(pre_ref[rows, :], dtr_ref[rows, :], dtb_ref[...], a_ref[...], dk_ref[...], ex_ref[...],
                       tril_ref[...], h_in)
        xs, xdt, decx, ecsx, ecl, dtx = f["xs"], f["xdt"], f["decx"], f["ecsx"], f["ecl"], f["dtx"]
        cs, cst, causal = f["cs"], f["cst"], f["causal"]
        causal_t = triu_ref[...].astype(F32) > 0.5

        zv = z_ref[rows, :]
        sz = _sig(zv)
        gz = zv * sz
        t = f["y"] * gz
        r = lax.rsqrt(_group_mean(t * t) + EPS)
        tn_ = t * r
        dyn = dy_ref[rows, :]
        dnw_ref[0:1, :] += jnp.sum(dyn * tn_, axis=0, keepdims=True)
        u = dyn * nw_ref[...]
        dt_ = r * u - tn_ * (r * _group_mean(u * tn_))
        dy = dt_ * gz
        dz_ref[rows, :] = (dt_ * f["y"] * (sz * (1.0 + zv * (1.0 - sz)))).astype(BF16)

        dd_ref[0:1, :] += jnp.sum(dy * xs, axis=0, keepdims=True)
        dxs = dk_ref[...] * dy

        edy = ecsx * dy
        dxdt, dbs, dcs_, dcsx_parts, dh_new = [], [], [], [], []
        lane = lax.broadcasted_iota(jnp.int32, (1, 128), 1)
        dcs_intra = jnp.zeros((BLK, 128), F32)
        for g in range(2):
            sl = slice(g * 256, (g + 1) * 256)
            bgf, cgf = f["bm"][:, g * NST:(g + 1) * NST], f["cm"][:, g * NST:(g + 1) * NST]
            bg, cg = bgf.astype(BF16), cgf.astype(BF16)
            gg = gst[:, sl].astype(BF16)
            hg = h_in[:, sl].astype(BF16)
            edyg = edy[:, sl].astype(BF16)
            dc = _mm_nt(edyg, hg)
            dh_new.append(gst[:, sl] * ecl[:, sl] + _mm_tn(cg, edyg))
            bgm = _mm(bg, gg)
            dxdt_g = decx[:, sl] * bgm
            dxg = (decx[:, sl] * xdt[:, sl]).astype(BF16)
            db = _mm_nt(dxg, gg)
            qd = bgm * xdt[:, sl] * decx[:, sl]
            last = jnp.sum(qd, axis=0, keepdims=True) + ecl[:, sl] * jnp.sum(gst[:, sl] * h_in[:, sl], axis=0, keepdims=True)
            rowid = lax.broadcasted_iota(jnp.int32, (BLK, 256), 0)
            dcsx_parts.append(f["yoff"][:, sl] * dy[:, sl] - qd + jnp.where(rowid == BLK - 1, last, 0.0))
            cb_ = f["cbs"][g]
            cbt = _mm_nt(bg, cg)
            dcb_ = jnp.zeros((BLK, BLK), F32)
            dcbt = jnp.zeros((BLK, BLK), F32)
            dxd = []
            for r_ in range(4):
                h = 4 * g + r_
                hl = slice(h * HD, (h + 1) * HD)
                lm = f["lms"][h]
                segt = cst[h:h + 1, :] - cs[:, h:h + 1]
                lmt = jnp.where(causal_t, jnp.exp(jnp.minimum(segt, 0.0)), 0.0)
                dyh = dy[:, hl].astype(BF16)
                xdh = xdt[:, hl].astype(BF16)
                dw = _mm_nt(dyh, xdh)
                dwt = _mm_nt(xdh, dyh)
                wt = cbt * lmt
                dxd.append(_mm(wt.astype(BF16), dyh))
                dcb_ = dcb_ + dw * lm
                dcbt = dcbt + dwt * lmt
                col = jnp.sum(dw * (cb_ * lm), axis=-1, keepdims=True) - jnp.sum(dwt * wt, axis=-1, keepdims=True)
                dcs_intra = dcs_intra + jnp.where(lane == h, col, 0.0)
            dxdt.append(dxdt_g + jnp.concatenate(dxd, axis=1))
            dcs_.append(dc + _mm(dcb_.astype(BF16), bg))
            dbs.append(db + _mm(dcbt.astype(BF16), cg))
        dxdt = jnp.concatenate(dxdt, axis=1)
        dxs = dxs + dxdt * dtx
        ext_t_ = ext_ref[...]
        dcs = dcs_intra + _sel_r(jnp.concatenate(dcsx_parts, axis=1), ext_t_)
        da = _sel_l(triu_ref[...], dcs)
        ddt = da * f["av"] + _sel_r(dxdt * xs, ext_t_)
        dhd_ref[1:2, :] += jnp.sum(da * f["dt"], axis=0, keepdims=True)
        ddtr = ddt * _sig(f["dtin"])
        dhd_ref[0:1, :] += jnp.sum(ddtr, axis=0, keepdims=True)
        ddt_ref[rows, :] = ddtr.astype(BF16)

        sp, pre = f["sp"], f["pre"]
        dact = jnp.concatenate([dxs] + dbs + dcs_, axis=1)
        dpre = dact * (sp * (1.0 + pre * (1.0 - sp)))
        dcb_ref[0:1, :] += jnp.sum(dpre, axis=0, keepdims=True)
        ext2 = jnp.concatenate([dpre, nxt], axis=0)
        shifted = [pltpu.roll(ext2, BLK + 8 - (CK - 1 - k), 0)[0:BLK] for k in range(CK - 1)] + [dpre]
        cw = cw_ref[...]
        xc = xc_ref[rows, :]
        dxr = cw[CK - 1:CK, :] * dpre
        for k in range(CK):
            dcw_ref[k:k + 1, :] += jnp.sum(shifted[k] * xc, axis=0, keepdims=True)
            if k < CK - 1:
                dxr = dxr + cw[k:k + 1, :] * shifted[k]
        dxbc_ref[rows, :] = dxr.astype(BF16)
        return jnp.concatenate(dh_new, axis=1), dpre[0:8]

    cur = lambda w: pl.BlockSpec((tile, w), lambda i: (nt - 1 - i, 0))
    return _pcall(
        body, [z, xbc, pre_all, dtr, dys, hs, cw, dtb, av, dk, nw, ex, ext_t, tril, triu], name="ssd_bwd", grid=(nt,),
        out_shape=[jax.ShapeDtypeStruct((s, SW), BF16), jax.ShapeDtypeStruct((s, XBCW), BF16),
                   jax.ShapeDtypeStruct((s, 128), BF16), jax.ShapeDtypeStruct((8, XBCW), F32),
                   jax.ShapeDtypeStruct((8, XBCW), F32), jax.ShapeDtypeStruct((8, SW), F32),
                   jax.ShapeDtypeStruct((8, 128), F32)],
        in_specs=[cur(SW), cur(XBCW), cur(XBCW), cur(128), cur(SW),
                  pl.BlockSpec((SUBS, NST, SW), lambda i: (nt - 1 - i, 0, 0)),
                  _row((8, XBCW)), _row((1, 128)), _row((1, 128)), _row((1, SW)), _row((1, SW)),
                  _row((128, SW)), _row((SW, 128)), _row((BLK, BLK)), _row((BLK, BLK))],
        out_specs=[cur(SW), cur(XBCW), cur(128), _row((8, XBCW)), _row((8, XBCW)), _row((8, SW)), _row((8, 128))],
        scratch_shapes=[pltpu.VMEM((NST, SW), F32), pltpu.VMEM((8, XBCW), F32), pltpu.VMEM((8, SW), F32)], carry=carry)


def _load_once(i, pairs, sem):
    @pl.when(i == 0)
    def _():
        cps = [pltpu.make_async_copy(src, dst, sem.at[k]) for k, (src, dst) in enumerate(pairs)]
        for cp in cps:
            cp.start()
        for cp in cps:
            cp.wait()


def _mlp_fwd(x, ya, ys, tgt, w_o, w_ga, w_gb, w_dn, gate1, a2, sh2, gate2, fn):
    s = x.shape[0]
    tm = 256

    def body(x_ref, ya_ref, ys_ref, t_ref, wo_hbm, wga_hbm, wgb_hbm, wdn_hbm, g1_ref, a2_ref, s2_ref, g2_ref, fn_ref,
             x1_ref, gu_ref, dx2_ref, loss_ref, dfn_ref, wo, wga, wgb, wdn, sem):
        i = pl.program_id(0)
        _load_once(i, [(wo_hbm, wo), (wga_hbm, wga), (wgb_hbm, wgb), (wdn_hbm, wdn)], sem)

        @pl.when(i == 0)
        def _():
            loss_ref[...] = jnp.zeros_like(loss_ref)
            dfn_ref[...] = jnp.zeros_like(dfn_ref)

        mix = _mm(ya_ref[...], wo[0:QW, :]) + _mm(ys_ref[...], wo[QW:D, :])
        x1 = x_ref[...] + g1_ref[...] * mix
        x1_ref[...] = x1
        r2 = lax.rsqrt(jnp.mean(x1 * x1, axis=-1, keepdims=True) + EPS)
        h2 = (x1 * r2 * a2_ref[...] + s2_ref[...]).astype(BF16)
        ha, hb = h2[:, 0:D // 2], h2[:, D // 2:D]
        gub = jnp.concatenate([(_mm(ha, wga[j]) + _mm(hb, wgb[j])).astype(BF16) for j in range(4)], axis=1)
        gu_ref[...] = gub
        gb, ub = gub[:, 0:DFF], gub[:, DFF:]
        act = gb * _sig(gb) * ub
        x2 = x1 + g2_ref[...] * _mm(act, wdn[...])
        r3 = lax.rsqrt(jnp.mean(x2 * x2, axis=-1, keepdims=True) + EPS)
        xn = x2 * r3
        fnv = fn_ref[...]
        err = xn * fnv - t_ref[...]
        loss_ref[...] += jnp.sum(err * err) * (0.5 / D)
        dy = err * (1.0 / D)
        dfn_ref[0:1, :] += jnp.sum(dy * xn, axis=0, keepdims=True)
        u = dy * fnv
        dx2_ref[...] = r3 * u - xn * (r3 * jnp.mean(u * xn, axis=-1, keepdims=True))

    def tok(w):
        return pl.BlockSpec((tm, w), lambda i: (i, 0))

    hbm = pl.BlockSpec(memory_space=pl.ANY)
    return pl.pallas_call(
        body, name="mlp_fwd", grid=(s // tm,),
        out_shape=[jax.ShapeDtypeStruct((s, D), F32), jax.ShapeDtypeStruct((s, 2 * DFF), BF16),
                   jax.ShapeDtypeStruct((s, D), F32), jax.ShapeDtypeStruct((8, 128), F32),
                   jax.ShapeDtypeStruct((8, D), F32)],
        in_specs=[tok(D), tok(QW), tok(SW), tok(D), hbm, hbm, hbm, hbm,
                  _row((1, D)), _row((1, D)), _row((1, D)), _row((1, D)), _row((1, D))],
        out_specs=[tok(D), tok(2 * DFF), tok(D), _row((8, 128)), _row((8, D))],
        scratch_shapes=[pltpu.VMEM((D, D), BF16), pltpu.VMEM(w_ga.shape, BF16), pltpu.VMEM(w_gb.shape, BF16),
                        pltpu.VMEM((DFF, D), BF16), pltpu.SemaphoreType.DMA((4,))],
        compiler_params=_cp(("arbitrary",)),
    )(x, ya, ys, tgt, w_o, w_ga, w_gb, w_dn, gate1, a2, sh2, gate2, fn)


def _mlp_bwd(x1, gu, dx2, w_o, w_ga, w_gb, w_dn, gate1, a2, sh2, gate2):
    s = x1.shape[0]
    tm = 256
    nj = 2 * DFF // 4

    def body(x1_ref, gu_ref, dx2_ref, wo_hbm, wga_hbm, wgb_hbm, wdn_hbm, g1_ref, a2_ref, s2_ref, g2_ref,
             dx1_ref, dya_ref, dys_ref, act_ref, dgu_ref, h2_ref, dsh_ref, p_ref, wo, wga, wgb, wdn, sem):
        i = pl.program_id(0)
        _load_once(i, [(wo_hbm, wo), (wga_hbm, wga), (wgb_hbm, wgb), (wdn_hbm, wdn)], sem)

        @pl.when(i == 0)
        def _():
            dsh_ref[...] = jnp.zeros_like(dsh_ref)
            p_ref[...] = jnp.zeros_like(p_ref)

        dx2 = dx2_ref[...]
        dact = _mm_nt((dx2 * g2_ref[...]).astype(BF16), wdn[...])
        gub = gu_ref[...]
        gv, uv = gub[:, 0:DFF].astype(F32), gub[:, DFF:].astype(F32)
        sg = _sig(gv)
        sl = gv * sg
        act_ref[...] = (sl * uv).astype(BF16)
        dgu = jnp.concatenate([dact * uv * (sg * (1.0 + gv * (1.0 - sg))), dact * sl], axis=1).astype(BF16)
        dgu_ref[...] = dgu
        dha = sum(_mm_nt(dgu[:, j * nj:(j + 1) * nj], wga[j]) for j in range(4))
        dhb = sum(_mm_nt(dgu[:, j * nj:(j + 1) * nj], wgb[j]) for j in range(4))
        dh = jnp.concatenate([dha, dhb], axis=1)
        x1 = x1_ref[...]
        r2 = lax.rsqrt(jnp.mean(x1 * x1, axis=-1, keepdims=True) + EPS)
        xn = x1 * r2
        a2 = a2_ref[...]
        h2_ref[...] = (xn * a2 + s2_ref[...]).astype(BF16)
        dsh_ref[0:1, :] += jnp.sum(dh, axis=0, keepdims=True)
        p_ref[0:1, :] += jnp.sum(dh * xn, axis=0, keepdims=True)
        u = dh * a2
        dx1 = dx2 + r2 * u - xn * (r2 * jnp.mean(u * xn, axis=-1, keepdims=True))
        dx1_ref[...] = dx1
        dcat = _mm_nt((dx1 * g1_ref[...]).astype(BF16), wo[...])
        dya_ref[...] = dcat[:, 0:QW]
        dys_ref[...] = dcat[:, QW:D]

    def tok(w):
        return pl.BlockSpec((tm, w), lambda i: (i, 0))

    hbm = pl.BlockSpec(memory_space=pl.ANY)
    return pl.pallas_call(
        body, name="mlp_bwd", grid=(s // tm,),
        out_shape=[jax.ShapeDtypeStruct((s, D), F32), jax.ShapeDtypeStruct((s, QW), F32),
                   jax.ShapeDtypeStruct((s, SW), F32), jax.ShapeDtypeStruct((s, DFF), BF16),
                   jax.ShapeDtypeStruct((s, 2 * DFF), BF16), jax.ShapeDtypeStruct((s, D), BF16),
                   jax.ShapeDtypeStruct((8, D), F32), jax.ShapeDtypeStruct((8, D), F32)],
        in_specs=[tok(D), tok(2 * DFF), tok(D), hbm, hbm, hbm, hbm, _row((1, D)), _row((1, D)), _row((1, D)), _row((1, D))],
        out_specs=[tok(D), tok(QW), tok(SW), tok(DFF), tok(2 * DFF), tok(D), _row((8, D)), _row((8, D))],
        scratch_shapes=[pltpu.VMEM((D, D), BF16), pltpu.VMEM(w_ga.shape, BF16), pltpu.VMEM(w_gb.shape, BF16),
                        pltpu.VMEM((DFF, D), BF16), pltpu.SemaphoreType.DMA((4,))],
        compiler_params=_cp(("arbitrary",)),
    )(x1, gu, dx2, w_o, w_ga, w_gb, w_dn, gate1, a2, sh2, gate2)


def _wgrad(name, a, b, tn, gate=None, w=None, stacked=False, carry=None):
    s, m = a.shape
    n = b.shape[1]
    tk = min(1024, s)
    nk = s // tk

    def body(*refs):
        if gate is None:
            a_ref, b_ref, o_ref = refs
        else:
            a_ref, b_ref, g_ref, w_ref, o_ref, dg_ref = refs
        k = pl.program_id(1)

        @pl.when(k == 0)
        def _():
            o_ref[...] = jnp.zeros_like(o_ref)

        o_ref[...] += _mm_tn(a_ref[...], b_ref[...].astype(BF16))

        if gate is not None:
            @pl.when(k == nk - 1)
            def _():
                acc = o_ref[...]
                dg_ref[...] = jnp.zeros_like(dg_ref)
                dg_ref[0:1, :] = jnp.sum(acc * w_ref[...].astype(F32), axis=0, keepdims=True)
                o_ref[...] = acc * g_ref[...]

    in_specs = [pl.BlockSpec((tk, m), lambda j, k: (k, 0)), pl.BlockSpec((tk, tn), lambda j, k: (k, j))]
    if stacked:
        out_shape = [jax.ShapeDtypeStruct((n // tn, m, tn), F32)]
        out_specs = [pl.BlockSpec((None, m, tn), lambda j, k: (j, 0, 0))]
    else:
        out_shape = [jax.ShapeDtypeStruct((m, n), F32)]
        out_specs = [pl.BlockSpec((m, tn), lambda j, k: (0, j))]
    args = [a, b]
    if gate is not None:
        in_specs += [pl.BlockSpec((1, tn), lambda j, k: (0, j)), pl.BlockSpec((m, tn), lambda j, k: (0, j))]
        out_shape.append(jax.ShapeDtypeStruct((8, n), F32))
        out_specs.append(pl.BlockSpec((8, tn), lambda j, k: (0, j)))
        args += [gate, w]
    return _pcall(body, args, name=name, grid=(n // tn, nk), out_shape=out_shape, in_specs=in_specs,
                  out_specs=out_specs, carry=carry)


def _wgrad_in_t(h1, dproj, carry=None):
    s = h1.shape[0]
    tk = min(1024, s)
    nk = s // tk

    def body(a_ref, b_ref, o_hbm, acc_ref, tr_ref, sem):
        k = pl.program_id(0)

        @pl.when(k == 0)
        def _():
            acc_ref[...] = jnp.zeros_like(acc_ref)

        acc_ref[...] += _mm_tn(a_ref[...], b_ref[...])

        @pl.when(k == nk - 1)
        def _():
            for j in range(PROJ_W // 128):
                tr_ref[j * 128:(j + 1) * 128, :] = acc_ref[:, j * 128:(j + 1) * 128].T
            cp = pltpu.make_async_copy(tr_ref, o_hbm, sem)
            cp.start()
            cp.wait()

    return _pcall(body, [h1, dproj], name="wgrad_in", grid=(nk,),
                  out_shape=[jax.ShapeDtypeStruct((PROJ_W, D), F32)],
                  in_specs=[pl.BlockSpec((tk, D), lambda k: (k, 0)), pl.BlockSpec((tk, PROJ_W), lambda k: (k, 0))],
                  out_specs=[pl.BlockSpec(memory_space=pl.ANY)],
                  scratch_shapes=[pltpu.VMEM((D, PROJ_W), F32), pltpu.VMEM((PROJ_W, D), F32), pltpu.SemaphoreType.DMA],
                  carry=carry)


_SMALL = ["ada_b", "norm1", "conv_w", "conv_b", "dt_bias", "A_log", "D_skip", "sinks", "attn_out_norm",
          "ssm_out_norm", "norm2", "rel_bias", "final_norm"]


def _small_grad(name, gs, chip):
    if name == "ada_b":
        return jnp.concatenate([gs[j:j + 1, :] for j in range(6)], axis=1)
    if name == "conv_w":
        full = gs[7:11, :]
        out = full[:, 0:256]
        for j in range(1, 4):
            out = jnp.where(chip == j, full[:, j * 256:(j + 1) * 256], out)
        return out
    row, width = {"norm1": (6, D), "conv_b": (11, D), "norm2": (12, D), "final_norm": (13, D),
                  "attn_out_norm": (14, QW), "ssm_out_norm": (15, SW), "dt_bias": (16, NH), "A_log": (17, NH),
                  "D_skip": (18, NH), "sinks": (19, NH), "rel_bias": (24, NH)}[name]
    rows = NBUCKET if name == "rel_bias" else 1
    return gs[row:row + rows, 0:width]


def _small_update(small_all, where, ws, ms, vs):
    n = len(_SMALL)

    def body(where_ref, sa_ref, *refs):
        w_refs, m_refs, v_refs, outs = refs[:n], refs[n:2 * n], refs[2 * n:3 * n], refs[3 * n:]
        gs = sa_ref[0]
        for b in range(1, 8):
            gs = gs + sa_ref[b]
        chip = where_ref[1]
        for i, name in enumerate(_SMALL):
            g = _small_grad(name, gs, chip)
            lead = (0,) if name == "conv_w" else ()
            d, mo, vo = _adamw(w_refs[i][lead + (...,)], g, m_refs[i][lead + (...,)], v_refs[i][lead + (...,)])
            for k, val in enumerate((g, d, mo, vo)):
                outs[k * n + i][lead + (...,)] = val
        outs[4 * n][...] = gs[20:21, 0:128]

    shapes = [jax.ShapeDtypeStruct(w.shape, F32) for w in ws]
    vmem = pl.BlockSpec(memory_space=pltpu.VMEM)
    res = pl.pallas_call(
        body, name="small_update", out_shape=shapes * 4 + [jax.ShapeDtypeStruct((1, 128), F32)],
        in_specs=[pl.BlockSpec(memory_space=pltpu.SMEM)] + [vmem] * (1 + 3 * n), out_specs=[vmem] * (4 * n + 1),
    )(where, small_all, *ws, *ms, *vs)
    return [res[k * n:(k + 1) * n] for k in range(4)], res[4 * n][0, 0]


def _add_half(name, g, got, where, by_cols=False):
    rr, cc = got.shape[1:]
    if by_cols:
        mine = pl.BlockSpec((None, rr, cc), lambda i, w_ref: (i, 0, w_ref[0]))
    else:
        mine = pl.BlockSpec((None, None, rr, cc), lambda i, w_ref: (i, w_ref[0], 0, 0))

    def body(w_ref, g_ref, r_ref, o_ref, own_ref):
        s = g_ref[...] + r_ref[...]
        o_ref[...] = s.astype(BF16)

        @pl.when(pl.program_id(0) == w_ref[1])
        def _():
            own_ref[...] = s

    spec = pl.BlockSpec((None, rr, cc), lambda i, w_ref: (i, 0, 0))
    return _pcall(body, [where, g, got], name=name, grid=(4,), nprefetch=1,
                  out_shape=[jax.ShapeDtypeStruct(got.shape, BF16), jax.ShapeDtypeStruct((rr, cc), F32)],
                  in_specs=[mine, spec],
                  out_specs=[spec, pl.BlockSpec((rr, cc), lambda i, w_ref: (0, 0))])


def _add_chips(name, own, got):
    rr, cc = own.shape
    tr = rr // 2 if rr % 32 == 0 else rr

    def body(s_ref, r_ref, o_ref):
        o_ref[...] = ((s_ref[...] + r_ref[0].astype(F32)) + r_ref[1].astype(F32)) + r_ref[2].astype(F32)

    spec = pl.BlockSpec((tr, cc), lambda i: (i, 0))
    return _pcall(body, [own, got], name=name, grid=(rr // tr,), out_shape=[jax.ShapeDtypeStruct((rr, cc), F32)],
                  in_specs=[spec, pl.BlockSpec((3, tr, cc), lambda i: (0, i, 0))], out_specs=[spec])[0]


def _adamw_halves(name, mine, got, w, m, v, where, by_cols=False):
    rr, cc = mine.shape

    def body(w_ref_, t_ref, r_ref, w_ref, m_ref, v_ref, g_ref, d_ref, mo_ref, vo_ref):
        g = jnp.where(pl.program_id(0) == w_ref_[0], t_ref[...], r_ref[...])
        g_ref[...] = g
        d_ref[...], mo_ref[...], vo_ref[...] = _adamw(w_ref[...], g, m_ref[...], v_ref[...])

    if by_cols:
        grid = (2, 1)
        half = pl.BlockSpec((rr, cc), lambda h, i, w_ref_: (0, 0))
        full = pl.BlockSpec((rr, cc), lambda h, i, w_ref_: (0, h))
    else:
        tr = rr // 2
        grid = (2, 2)
        half = pl.BlockSpec((tr, cc), lambda h, i, w_ref_: (i, 0))
        full = pl.BlockSpec((None, tr, cc), lambda h, i, w_ref_: (0, 2 * h + i, 0))
    return _pcall(body, [where, mine, got, w, m, v], name=name, grid=grid, nprefetch=1,
                  out_shape=[jax.ShapeDtypeStruct(w.shape, F32)] * 4,
                  in_specs=[half, half, full, full, full], out_specs=[full] * 4)


def _bias_table(rel_bias, bucket, mask):
    def body(rb_ref, bk_ref, mk_ref, o_ref):
        bk = bk_ref[...]
        valid = mk_ref[...] > 0
        for h in range(NH):
            acc = jnp.zeros((BLK, 2 * BLK), F32)
            for b in range(NBUCKET):
                acc = jnp.where(bk == b, rb_ref[b, h], acc)
            o_ref[h] = jnp.where(valid, acc, NEG)

    vmem = pl.BlockSpec(memory_space=pltpu.VMEM)
    return pl.pallas_call(
        body, name="bias_table", out_shape=jax.ShapeDtypeStruct((NH, BLK, 2 * BLK), F32),
        in_specs=[pl.BlockSpec(memory_space=pltpu.SMEM), vmem, vmem], out_specs=vmem,
    )(rel_bias, bucket, mask)


def _pack_small(dsh1, p1, dsh2, p2, dg1a, dg1b, dg2, norm1, norm2, scale1, scale2, dcw, dcb, dfn,
                dnw_attn, dnw_ssm, dhd, av, dsink, drel, loss_acc):
    def body(dsh1_ref, p1_ref, dsh2_ref, p2_ref, dg1a_ref, dg1b_ref, dg2_ref, n1_ref, n2_ref, s1_ref, s2_ref,
             dcw_ref, dcb_ref, dfn_ref, da_ref, ds_ref, dhd_ref, av_ref, dsink_ref, drel_ref, loss_ref, o_ref):
        o_ref[...] = jnp.zeros_like(o_ref)
        p1v, p2v = p1_ref[0:1, :], p2_ref[0:1, :]
        o_ref[0:1, :] = dsh1_ref[0:1, :]
        o_ref[1:2, :] = p1v * n1_ref[...]
        o_ref[2:3, :] = dg1a_ref[0:1, :] + dg1b_ref[0:1, :]
        o_ref[3:4, :] = dsh2_ref[0:1, :]
        o_ref[4:5, :] = p2v * n2_ref[...]
        o_ref[5:6, :] = dg2_ref[0:1, :]
        o_ref[6:7, :] = p1v * (1.0 + s1_ref[...])
        o_ref[7:11, :] = dcw_ref[0:4, :]
        o_ref[11:12, :] = dcb_ref[0:1, :]
        o_ref[12:13, :] = p2v * (1.0 + s2_ref[...])
        o_ref[13:14, :] = dfn_ref[0:1, :]
        o_ref[14:15, 0:QW] = da_ref[0:1, :]
        o_ref[15:16, 0:SW] = ds_ref[0:1, :]
        o_ref[16:17, 0:128] = dhd_ref[0:1, :]
        o_ref[17:18, 0:128] = dhd_ref[1:2, :] * av_ref[...]
        o_ref[18:19, 0:128] = dhd_ref[2:3, :]
        o_ref[19:20, 0:128] = dsink_ref[0:1, :]
        o_ref[20:21, 0:128] = loss_ref[0:1, :]
        o_ref[24:56, 0:128] = drel_ref[...]

    return pl.pallas_call(body, name="pack_small", out_shape=jax.ShapeDtypeStruct((56, D), F32))(
        dsh1, p1, dsh2, p2, dg1a, dg1b, dg2, norm1, norm2, scale1, scale2, dcw, dcb, dfn,
        dnw_attn, dnw_ssm, dhd, av, dsink, drel, loss_acc)


def _pad_row(a, rows=1):
    return jnp.pad(a.reshape(rows, -1), ((0, 0), (0, D - a.size // rows)))


def kernel(x, c, ada_w, ada_b, norm1, w_in, conv_w, conv_b, dt_bias, A_log, D_skip, sinks, attn_out_norm, ssm_out_norm, w_o, norm2, w_gate_up, w_down, rel_bias, final_norm, loss_target, m_ada_w, m_ada_b, m_norm1, m_w_in, m_conv_w, m_conv_b, m_dt_bias, m_A_log, m_D_skip, m_sinks, m_attn_out_norm, m_ssm_out_norm, m_w_o, m_norm2, m_w_gate_up, m_w_down, m_rel_bias, m_final_norm, v_ada_w, v_ada_b, v_norm1, v_w_in, v_conv_w, v_conv_b, v_dt_bias, v_A_log, v_D_skip, v_sinks, v_attn_out_norm, v_ssm_out_norm, v_w_o, v_norm2, v_w_gate_up, v_w_down, v_rel_bias, v_final_norm):
    xi, yi, ci = lax.axis_index("x"), lax.axis_index("y"), lax.axis_index("c")
    chip = 2 * xi + yi
    me = 4 * xi + 2 * yi + ci
    where = jnp.stack([ci, chip]).astype(jnp.int32)
    xs2, tgt = x[0], loss_target[0]

    first = jnp.concatenate([c, _pad_row(conv_w[0], CK), jnp.zeros((3, D), F32)], axis=0)
    w_in_t, m_w_in_t, v_w_in_t = w_in[0].T, m_w_in[0].T, v_w_in[0].T
    w_in_b, w_o_b, w_dn_b = w_in_t.astype(BF16), w_o[0].astype(BF16), w_down[0].astype(BF16)
    w_gu_b = w_gate_up[0].astype(BF16)
    hw = D // 2
    fetch_half = _Carry(
        [w_in_b], [jax.ShapeDtypeStruct((4,) + w_in_b.shape, BF16)],
        lambda x_, y_, c_: [(None, 0, slice(None), 0, 2 * x_ + y_)] + [
            (f, 0, (slice(None), pl.ds(c_ * hw, hw)), 0, (2 * x_ + y_, slice(None), pl.ds(c_ * hw, hw))) for f in _CHIPS3])

    def swap_halves(x_, y_, c_):
        there = [(jnp.bitwise_xor(2 * x_ + y_, k + 1), slice(None), pl.ds(c_ * hw, hw)) for k in range(3)]
        return [(_SIBLING, 1, at, 1, at) for at in there]

    first_all, w_in_g = _exchange_two("gather_first", _merge(_gather8_carry(first), fetch_half), swap_halves)
    c_all = first_all[:, 0, :]
    cw_full = jnp.concatenate([first_all[2 * j, 1:1 + CK, 0:256] for j in range(4)], axis=1)
    w_in_f = jnp.pad(w_in_g.reshape(IN_W, D), ((0, PROJ_W - IN_W), (0, 0)))

    ncol = ada_w.shape[2]
    mod_cols = _ada_fwd(c_all, ada_w[0], lax.dynamic_slice(ada_b, (0, chip * ncol), (1, ncol)))
    mod_all = _exchange("gather_mod", _gather_chips_carry([mod_cols]))[0]
    mod = lax.dynamic_slice(jnp.transpose(mod_all, (1, 0, 2)).reshape(8, 4 * ncol), (me, 0), (1, 4 * ncol))
    shift1, scale1, gate1, shift2, scale2, gate2 = [mod[:, j * D:(j + 1) * D] for j in range(6)]
    a1 = norm1 * (1.0 + scale1)
    a2 = norm2 * (1.0 + scale2)

    hdn = DFF // 8
    q, kv, z, xbc, dtr, w_o_g, w_dna_g = _in_proj_fwd(xs2, a1, shift1, w_in_f,
                                                      carry=_gather_chips_carry([w_o_b, w_dn_b[0:hdn]]))
    w_o_f = w_o_g.reshape(D, D)
    bucket, mask = _attn_geometry()
    bucket = jnp.asarray(bucket)
    bias = _bias_table(rel_bias, bucket, jnp.asarray(mask.astype(np.int32)))
    sinks1 = sinks[0]
    ya, w_ga_g = _attn_fwd(q, kv, bias, sinks1, attn_out_norm, carry=_gather_chips_carry([w_gu_b[0:D // 2]]))
    cw8 = jnp.concatenate([cw_full, jnp.zeros((4, XBCW), F32)], axis=0)
    dtb = _pad_row(dt_bias)[:, 0:128]
    av = _pad_row(-jnp.exp(A_log))[:, 0:128]
    dk = jnp.repeat(D_skip, HD, axis=1)
    ys, hs, pre, w_gb_g, w_dnb_g = _ssd_fwd(z, xbc, dtr, cw8, conv_b, dtb, av, dk, ssm_out_norm,
                                            carry=_gather_chips_carry([w_gu_b[D // 2:D], w_dn_b[hdn:2 * hdn]]))
    w_dn_f = jnp.stack([w_dna_g, w_dnb_g], axis=1).reshape(DFF, D)
    fn = final_norm[None, :]
    x1, gu, dx2, loss_acc, dfn = _mlp_fwd(xs2, ya, ys, tgt, w_o_f, w_ga_g, w_gb_g, w_dn_f, gate1, a2, shift2, gate2, fn)

    def to_sibling(p):
        return _Carry([p], [jax.ShapeDtypeStruct((4,) + p.shape[2:], F32)],
                      lambda x_, y_, c_: [(_SIBLING, 0, (j, 1 - c_), 0, j) for j in range(4)])

    def to_chips(s4):
        return _Carry([s4], [jax.ShapeDtypeStruct((3,) + s4.shape[1:], s4.dtype)],
                      lambda x_, y_, c_: [(f, 0, jnp.bitwise_xor(2 * x_ + y_, k + 1), 0, k) for k, f in enumerate(_CHIPS3)])

    def back(t):
        return _Carry([t[None]], [jax.ShapeDtypeStruct((1,) + t.shape, F32)], lambda x_, y_, c_: [(_SIBLING, 0, 0, 0, 0)])

    dx1, dya, dys, act, dgu, h2, dsh2, p2 = _mlp_bwd(x1, gu, dx2, w_o_f, w_ga_g, w_gb_g, w_dn_f, gate1, a2, shift2, gate2)
    p_gu = _wgrad("wgrad_gate_up", h2, dgu, 2 * DFF // 4, stacked=True)[0].reshape(4, 2, D // 2, 2 * DFF // 4)
    g_dn, dg2, got1_gu = _wgrad("wgrad_down", act, dx2, D // 2, gate2, w_dn_f, carry=to_sibling(p_gu))
    p_dn = g_dn.reshape(4, 2, DFF // 8, D)
    s4_gu, own_gu = _add_half("rs_add_half_gu", p_gu, got1_gu, where)
    dq, dkv, dbias, dsink, dnw_attn, got2_gu, got1_dn = _attn_bwd(
        q, kv, dya, bias, sinks1, attn_out_norm, carry=_merge(to_chips(s4_gu), to_sibling(p_dn)))
    drel = _rel_bias_grad(dbias, bucket)
    mine_gu = _add_chips("rs_add_chips_gu", own_gu, got2_gu)
    s4_dn, own_dn = _add_half("rs_add_half_dn", p_dn, got1_dn, where)
    dz, dxbc, ddt, dcw, dcb, dnw_ssm, dhd, got2_dn, got3_gu = _ssd_bwd(
        z, xbc, pre, dtr, dys, hs, cw8, dtb, av, dk, ssm_out_norm, carry=_merge(to_chips(s4_dn), back(mine_gu)))
    mine_dn = _add_chips("rs_add_chips_dn", own_dn, got2_dn)
    grad_x, dproj, h1, dsh1, p1 = _in_proj_bwd(xs2, dx1, a1, shift1, w_in_f, dq, dkv, dz, dxbc, ddt)
    g_in_t, got3_dn = _wgrad_in_t(h1, dproj, carry=back(mine_dn))
    p_in = g_in_t[0:IN_W].reshape(4, IN_W // 4, D)

    def to_sibling_cols(p):
        return _Carry([p], [jax.ShapeDtypeStruct(p.shape[:2] + (D // 2,), F32)],
                      lambda x_, y_, c_: [(_SIBLING, 0, (j, slice(None), pl.ds((1 - c_) * (D // 2), D // 2)), 0, j)
                                          for j in range(4)])

    g_oa, dg1a, got1_in = _wgrad("wgrad_o_attn", ya, dx1, D, gate1, w_o_f[0:QW], carry=to_sibling_cols(p_in))
    s4_in, own_in = _add_half("rs_add_half_in", p_in, got1_in, where, by_cols=True)
    g_os, dg1b, got2_in = _wgrad("wgrad_o_ssm", ys, dx1, D, gate1, w_o_f[QW:D], carry=to_chips(s4_in))
    mine_in = _add_chips("rs_add_chips_in", own_in, got2_in)
    p_o = jnp.concatenate([g_oa, g_os], axis=0).reshape(4, 2, D // 8, D)

    small = _pack_small(dsh1, p1, dsh2, p2, dg1a, dg1b, dg2, norm1, norm2, scale1, scale2, dcw, dcb, dfn,
                        dnw_attn, dnw_ssm, dhd, av, dsink, drel, loss_acc)
    small_all, got1_o, got3_in = _exchange(
        "gather_small", _merge(_gather8_carry(small), to_sibling(p_o), back(mine_in)))
    s4_o, own_o = _add_half("rs_add_half_o", p_o, got1_o, where)
    mine_o = _add_chips("rs_add_chips_o", own_o, _exchange("rs_chips_o", to_chips(s4_o))[0])
    got3_o = _exchange("rs_back_o", back(mine_o))[0]
    small_res, loss = _small_update(
        small_all, where,
        [ada_b, norm1, conv_w, conv_b, dt_bias, A_log, D_skip, sinks, attn_out_norm, ssm_out_norm, norm2, rel_bias,
         final_norm[None, :]],
        [m_ada_b, m_norm1, m_conv_w, m_conv_b, m_dt_bias, m_A_log, m_D_skip, m_sinks, m_attn_out_norm,
         m_ssm_out_norm, m_norm2, m_rel_bias, m_final_norm[None, :]],
        [v_ada_b, v_norm1, v_conv_w, v_conv_b, v_dt_bias, v_A_log, v_D_skip, v_sinks, v_attn_out_norm,
         v_ssm_out_norm, v_norm2, v_rel_bias, v_final_norm[None, :]])
    small_out = [dict(zip(_SMALL, r)) for r in small_res]
    for r in small_out:
        r["final_norm"] = r["final_norm"][0]

    dmod_all = small_all[:, 0:6, :].reshape(8, 6 * D)
    dmod_loc = lax.dynamic_slice(dmod_all, (0, chip * ncol), (8, ncol))
    ada_out = _ada_bwd_adamw(c_all.T, dmod_loc, ada_w[0], m_ada_w[0], v_ada_w[0])

    big_gu = _adamw_halves("adamw_gate_up", mine_gu, got3_gu[0], w_gate_up, m_w_gate_up, v_w_gate_up, where)
    big_dn = _adamw_halves("adamw_down", mine_dn, got3_dn[0], w_down, m_w_down, v_w_down, where)
    big_o = _adamw_halves("adamw_o", mine_o, got3_o[0], w_o, m_w_o, v_w_o, where)
    big_in = [o.T[None] for o in _adamw_halves("adamw_in", mine_in, got3_in[0], w_in_t, m_w_in_t, v_w_in_t, where,
                                               by_cols=True)]
    big = [big_in, big_o, big_gu, big_dn]

    order = ["ada_w", "ada_b", "norm1", "w_in", "conv_w", "conv_b", "dt_bias", "A_log", "D_skip", "sinks",
             "attn_out_norm", "ssm_out_norm", "w_o", "norm2", "w_gate_up", "w_down", "rel_bias", "final_norm"]
    bigname = {"w_in": 0, "w_o": 1, "w_gate_up": 2, "w_down": 3}
    res = [loss, grad_x[None]]
    for kind in range(4):
        for nm in order:
            if nm == "ada_w":
                res.append(ada_out[kind][None])
            elif nm in bigname:
                res.append(big[bigname[nm]][kind])
            else:
                res.append(small_out[kind][nm])
    return tuple(res)
```

```python
import numpy as np
import jax
import jax.numpy as jnp
from jax import lax
from jax.experimental import pallas as pl
from jax.experimental.pallas import tpu as pltpu

F32, BF16 = jnp.float32, jnp.bfloat16
HI = lax.Precision.HIGHEST

D = 1024
QW, KVW = 512, 128
NH, HD, NKV = 8, 64, 2
SW = 512
NST = 128
XBCW = 1024
CK = 4
BLK = 128
DFF = 2816
IN_W = 2312
PROJ_W = 2432
EPS = 1e-6
NEG = -1e30
NBUCKET = 32

B1, B2, LR, AEPS, WD, STEP = 0.9, 0.999, 0.001, 1e-08, 0.01, 10

VMEM_LIMIT = 56 * 1024 * 1024

_NT = (((1,), (1,)), ((), ()))
_TN = (((0,), (0,)), ((), ()))


def _mm(a, b):
    return jnp.dot(a, b, preferred_element_type=F32)


def _mm_nt(a, b):
    return lax.dot_general(a, b, _NT, preferred_element_type=F32)


def _mm_tn(a, b):
    return lax.dot_general(a, b, _TN, preferred_element_type=F32)


def _mm_hi(a, b):
    return jnp.dot(a, b, preferred_element_type=F32, precision=HI)


def _split3(x):
    hi = x.astype(BF16)
    r = x - hi.astype(F32)
    mid = r.astype(BF16)
    lo = (r - mid.astype(F32)).astype(BF16)
    return hi, mid, lo


def _sel_r(x, e):
    hi, mid, lo = _split3(x)
    return (_mm(hi, e) + _mm(mid, e)) + _mm(lo, e)


def _sel_l(e, x):
    hi, mid, lo = _split3(x)
    return (_mm(e, hi) + _mm(e, mid)) + _mm(e, lo)


def _sig(x):
    return 1.0 / (1.0 + jnp.exp(-x))


def _cp(sem):
    return pltpu.CompilerParams(dimension_semantics=sem, vmem_limit_bytes=VMEM_LIMIT)


def _row(shape):
    nd = len(shape)
    return pl.BlockSpec(shape, lambda *_: (0,) * nd)


def _adamw(w, g, m, v):
    m = B1 * m + (1.0 - B1) * g
    v = B2 * v + (1.0 - B2) * (g * g)
    m_hat = m / (1.0 - B1 ** STEP)
    v_hat = v / (1.0 - B2 ** STEP)
    delta = -LR * (m_hat / (jnp.sqrt(v_hat) + AEPS) + WD * w)
    return delta, m, v


class _Carry:
    def __init__(self, inps, outs, copies):
        self.inps, self.outs, self.copies = list(inps), list(outs), copies
        self.n = len(copies(0, 0, 0))

    def descriptors(self, in_refs, out_refs, send_sems, recv_sems):
        x, y, c = lax.axis_index("x"), lax.axis_index("y"), lax.axis_index("c")
        out = []
        for j, (flip, a, si, o, di) in enumerate(self.copies(x, y, c)):
            if flip is None:
                out.append(pltpu.make_async_copy(in_refs[a].at[si], out_refs[o].at[di], send_sems.at[j]))
            else:
                fx, fy, fc = flip
                peer = (1 - x if fx else x, 1 - y if fy else y, 1 - c if fc else c)
                out.append(pltpu.make_async_remote_copy(
                    src_ref=in_refs[a].at[si], dst_ref=out_refs[o].at[di],
                    send_sem=send_sems.at[j], recv_sem=recv_sems.at[j],
                    device_id=peer, device_id_type=pl.DeviceIdType.MESH))
        return out


def _pcall(body, args, *, name, grid, in_specs, out_specs, out_shape, scratch_shapes=(), sem=None, nprefetch=0,
           carry=None):
    out_shape, out_specs = list(out_shape), list(out_specs)
    in_specs, scratch_shapes = list(in_specs), list(scratch_shapes)
    nin, nout, nscr = len(in_specs), len(out_shape), len(scratch_shapes)
    run = body
    if carry is not None:
        ncin, ncout = len(carry.inps), len(carry.outs)
        hbm = pl.BlockSpec(memory_space=pl.ANY)

        def run(*refs):
            pre, r = refs[:nprefetch], refs[nprefetch:]
            ins, cins = r[:nin], r[nin:nin + ncin]
            r = r[nin + ncin:]
            outs, couts = r[:nout], r[nout:nout + ncout]
            r = r[nout + ncout:]
            scr, (send_sems, recv_sems) = r[:nscr], r[nscr:]
            first = pl.program_id(0) == 0
            last = pl.program_id(0) == grid[0] - 1
            for ax in range(1, len(grid)):
                first = jnp.logical_and(first, pl.program_id(ax) == 0)
                last = jnp.logical_and(last, pl.program_id(ax) == grid[ax] - 1)

            @pl.when(first)
            def _():
                for d in carry.descriptors(cins, couts, send_sems, recv_sems):
                    d.start()

            body(*pre, *ins, *outs, *scr)

            @pl.when(last)
            def _():
                for d in carry.descriptors(cins, couts, send_sems, recv_sems):
                    d.wait()

        in_specs = in_specs + [hbm] * ncin
        out_specs = out_specs + [hbm] * ncout
        out_shape = out_shape + carry.outs
        scratch_shapes = scratch_shapes + [pltpu.SemaphoreType.DMA((carry.n,)), pltpu.SemaphoreType.DMA((carry.n,))]
        args = list(args) + carry.inps
    if sem is None:
        sem = ("arbitrary",) * len(grid)
    if nprefetch:
        kw = dict(grid_spec=pltpu.PrefetchScalarGridSpec(num_scalar_prefetch=nprefetch, grid=grid, in_specs=in_specs,
                                                         out_specs=out_specs, scratch_shapes=scratch_shapes))
    else:
        kw = dict(grid=grid, in_specs=in_specs, out_specs=out_specs, scratch_shapes=scratch_shapes)
    res = pl.pallas_call(run, name=name, out_shape=out_shape, compiler_params=_cp(sem), **kw)(*args)
    return list(res)


def _merge(*carries):
    inps, outs, offs = [], [], []
    for cr in carries:
        offs.append((len(inps), len(outs)))
        inps += cr.inps
        outs += cr.outs

    def copies(x, y, c):
        return [(f, a + io, si, o + oo, di) for cr, (io, oo) in zip(carries, offs) for f, a, si, o, di in cr.copies(x, y, c)]

    return _Carry(inps, outs, copies)


def _exchange(name, carry):
    return _pcall(lambda: None, [], name=name, grid=(1,), in_specs=[], out_specs=[], out_shape=[], carry=carry)


def _exchange_two(name, carry, then):
    second = _Carry([], [], then)
    nin, nout = len(carry.inps), len(carry.outs)

    def body(*refs):
        ins, outs = refs[:nin], refs[nin:nin + nout]
        send_a, recv_a, send_b, recv_b = refs[nin + nout:]
        for descs in (carry.descriptors(ins, outs, send_a, recv_a), second.descriptors(outs, outs, send_b, recv_b)):
            for d in descs:
                d.start()
            for d in descs:
                d.wait()

    hbm = pl.BlockSpec(memory_space=pl.ANY)
    return list(pl.pallas_call(
        body, name=name, out_shape=carry.outs, in_specs=[hbm] * nin, out_specs=[hbm] * nout,
        scratch_shapes=[pltpu.SemaphoreType.DMA((carry.n,)), pltpu.SemaphoreType.DMA((carry.n,)),
                        pltpu.SemaphoreType.DMA((second.n,)), pltpu.SemaphoreType.DMA((second.n,))],
    )(*carry.inps))


_ALL7 = [(f >> 2 & 1, f >> 1 & 1, f & 1) for f in range(1, 8)]
_CHIPS3 = [(0, 1, 0), (1, 0, 0), (1, 1, 0)]
_SIBLING = (0, 0, 1)


def _gather8_carry(blk):
    def copies(x, y, c):
        me = 4 * x + 2 * y + c
        return [(None, 0, 0, 0, me)] + [(f, 0, 0, 0, me) for f in _ALL7]

    return _Carry([blk[None]], [jax.ShapeDtypeStruct((8,) + blk.shape, blk.dtype)], copies)


def _gather_chips_carry(blks):
    def copies(x, y, c):
        chip = 2 * x + y
        return [(f, a, 0, a, chip) for a in range(len(blks)) for f in [None] + _CHIPS3]

    return _Carry([b[None] for b in blks], [jax.ShapeDtypeStruct((4,) + b.shape, b.dtype) for b in blks], copies)


def _ada_fwd(c_all, w_loc, b_loc):
    n = w_loc.shape[1]
    tn = 512

    def body(c_ref, w_ref, b_ref, o_ref):
        cv = c_ref[...]
        cond = cv * _sig(cv)
        o_ref[...] = _mm_hi(cond, w_ref[...]) + b_ref[...]

    return pl.pallas_call(
        body, name="ada_fwd", grid=(n // tn,),
        out_shape=jax.ShapeDtypeStruct((8, n), F32),
        in_specs=[_row((8, D)), pl.BlockSpec((D, tn), lambda j: (0, j)), pl.BlockSpec((1, tn), lambda j: (0, j))],
        out_specs=pl.BlockSpec((8, tn), lambda j: (0, j)),
        compiler_params=_cp(("parallel",)),
    )(c_all, w_loc, b_loc)


def _ada_bwd_adamw(c_all_t, dmod_loc, w, m, v, carry=None):
    n = w.shape[1]
    tn = 512

    def body(ct_ref, dm_ref, w_ref, m_ref, v_ref, g_ref, d_ref, mo_ref, vo_ref):
        ct = ct_ref[...]
        cond = ct * _sig(ct)
        dm = dm_ref[...]
        g = cond[:, 0:1] * dm[0:1, :]
        for b in range(1, 8):
            g = g + cond[:, b:b + 1] * dm[b:b + 1, :]
        g_ref[...] = g
        d_ref[...], mo_ref[...], vo_ref[...] = _adamw(w_ref[...], g, m_ref[...], v_ref[...])

    wspec = pl.BlockSpec((D, tn), lambda j: (0, j))
    return _pcall(
        body, [c_all_t, dmod_loc, w, m, v], name="ada_bwd_adamw", grid=(n // tn,),
        out_shape=[jax.ShapeDtypeStruct((D, n), F32)] * 4,
        in_specs=[_row((D, 8)), pl.BlockSpec((8, tn), lambda j: (0, j)), wspec, wspec, wspec],
        out_specs=[wspec] * 4, carry=carry)


def _in_proj_fwd(x, a1, sh1, w_in, carry=None):
    s = x.shape[0]
    tm = 512

    def body(x_ref, a_ref, s_ref, w_ref, q_ref, kv_ref, z_ref, xbc_ref, dt_ref):
        xv = x_ref[...]
        r = lax.rsqrt(jnp.mean(xv * xv, axis=-1, keepdims=True) + EPS)
        h = (xv * r * a_ref[...] + s_ref[...]).astype(BF16)
        p = _mm_nt(h, w_ref[...])
        q_ref[...] = p[:, 0:512].astype(BF16)
        kv_ref[...] = p[:, 512:768].astype(BF16)
        z_ref[...] = p[:, 768:1280]
        xbc_ref[...] = p[:, 1280:2304]
        dt_ref[...] = p[:, 2304:2432]

    def tok(w):
        return pl.BlockSpec((tm, w), lambda i: (i, 0))

    return _pcall(
        body, [x, a1, sh1, w_in], name="in_proj_fwd", grid=(s // tm,),
        out_shape=[jax.ShapeDtypeStruct((s, QW), BF16), jax.ShapeDtypeStruct((s, 2 * KVW), BF16),
                   jax.ShapeDtypeStruct((s, SW), F32), jax.ShapeDtypeStruct((s, XBCW), F32),
                   jax.ShapeDtypeStruct((s, 128), F32)],
        in_specs=[tok(D), _row((1, D)), _row((1, D)), _row((PROJ_W, D))],
        out_specs=[tok(QW), tok(2 * KVW), tok(SW), tok(XBCW), tok(128)], carry=carry)


def _in_proj_bwd(x, dx1, a1, sh1, w_in, dq, dkv, dz, dxbc, ddt, carry=None):
    s = x.shape[0]
    tm = 512

    def body(x_ref, dx1_ref, a_ref, s_ref, w_ref, dq_ref, dkv_ref, dz_ref, dxbc_ref, ddt_ref,
             gx_ref, h_ref, dsh_ref, p_ref):
        i = pl.program_id(0)

        @pl.when(i == 0)
        def _():
            dsh_ref[...] = jnp.zeros_like(dsh_ref)
            p_ref[...] = jnp.zeros_like(p_ref)

        def gather(st):
            rows = st["rows"]
            st["dproj"] = jnp.concatenate([dq_ref[rows, :], dkv_ref[rows, :], dz_ref[rows, :], dxbc_ref[rows, :],
                                           ddt_ref[rows, :]], axis=1)

        def back(st):
            st["dh"] = _mm(st.pop("dproj"), w_ref[...])

        def norm(st):
            rows, dh = st["rows"], st.pop("dh")
            xv = x_ref[rows, :]
            r = lax.rsqrt(jnp.mean(xv * xv, axis=-1, keepdims=True) + EPS)
            xn = xv * r
            a = a_ref[...]
            h_ref[rows, :] = (xn * a + s_ref[...]).astype(BF16)
            st["dsh"] = jnp.sum(dh, axis=0, keepdims=True)
            st["p"] = jnp.sum(dh * xn, axis=0, keepdims=True)
            u = dh * a
            gx_ref[rows, :] = dx1_ref[rows, :] + r * u - xn * (r * jnp.mean(u * xn, axis=-1, keepdims=True))

        g0, g1 = [dict(rows=slice(k * (tm // 2), (k + 1) * (tm // 2))) for k in range(2)]
        for stage, st in [(gather, g0), (back, g0), (gather, g1), (norm, g0), (back, g1), (norm, g1)]:
            stage(st)
        dsh_ref[0:1, :] += g0["dsh"] + g1["dsh"]
        p_ref[0:1, :] += g0["p"] + g1["p"]

    def tok(w):
        return pl.BlockSpec((tm, w), lambda i: (i, 0))

    return _pcall(
        body, [x, dx1, a1, sh1, w_in, dq, dkv, dz, dxbc, ddt], name="in_proj_bwd", grid=(s // tm,),
        out_shape=[jax.ShapeDtypeStruct((s, D), F32), jax.ShapeDtypeStruct((s, D), BF16),
                   jax.ShapeDtypeStruct((8, D), F32), jax.ShapeDtypeStruct((8, D), F32)],
        in_specs=[tok(D), tok(D), _row((1, D)), _row((1, D)), _row((PROJ_W, D)),
                  tok(QW), tok(2 * KVW), tok(SW), tok(XBCW), tok(128)],
        out_specs=[tok(D), tok(D), _row((8, D)), _row((8, D))], carry=carry)


def _attn_geometry():
    dist = np.arange(BLK)[:, None] + BLK - np.arange(2 * BLK)[None, :]
    n = np.maximum(dist, 0)
    max_exact = NBUCKET // 2
    large = max_exact + (np.log(np.maximum(n, 1) / max_exact) / np.log(128 / max_exact)
                         * (NBUCKET - max_exact)).astype(np.int32)
    large = np.minimum(large, NBUCKET - 1)
    bucket = np.where(n < max_exact, n, large).astype(np.int32)
    mask = (dist >= 0) & (dist < 128)
    return bucket, mask


def _attn_heads(is_first, q_blk, kvw, bias_ref, sinks_ref):
    qv = q_blk * 0.125
    col = lax.broadcasted_iota(jnp.int32, (BLK, 2 * BLK), 1)
    first = jnp.where(jnp.logical_and(is_first, col < BLK), NEG, 0.0)
    groups = []
    for g in range(NKV):
        qs = jnp.concatenate([qv[:, (4 * g + r) * HD:(4 * g + r + 1) * HD] for r in range(4)], axis=0)
        kw = kvw[:, g * HD:(g + 1) * HD]
        vw = kvw[:, KVW + g * HD:KVW + (g + 1) * HD]
        sc = _mm_nt(qs, kw)
        pn, ps = [], []
        for r in range(4):
            h = 4 * g + r
            sr = sc[r * BLK:(r + 1) * BLK] + bias_ref[h] + first
            sink = sinks_ref[h]
            m = jnp.maximum(jnp.max(sr, axis=-1, keepdims=True), sink)
            p = jnp.exp(sr - m)
            es = jnp.exp(sink - m)
            inv = 1.0 / (jnp.sum(p, axis=-1, keepdims=True) + es)
            pn.append(p * inv)
            ps.append(es * inv)
        pn = jnp.concatenate(pn, axis=0)
        ps = jnp.concatenate(ps, axis=0)
        o = _mm(pn.astype(BF16), vw)
        groups.append((qs, kw, vw, pn, ps, o))
    return groups


def _unstack_heads(parts):
    return jnp.concatenate([p[r * BLK:(r + 1) * BLK] for p in parts for r in range(4)], axis=1)


def _attn_fwd(q, kv, bias, sinks, nw, carry=None):
    s = q.shape[0]

    def body(q_ref, kvp_ref, kvc_ref, bias_ref, sinks_ref, nw_ref, y_ref):
        t = pl.program_id(0)
        kv3 = jnp.concatenate([kvp_ref[...], kvc_ref[...]], axis=0)
        for sub in range(2):
            rows = slice(sub * BLK, (sub + 1) * BLK)
            groups = _attn_heads(jnp.logical_and(t == 0, sub == 0), q_ref[rows, :], kv3[sub * BLK:(sub + 2) * BLK],
                                 bias_ref, sinks_ref)
            o = _unstack_heads([g[5] for g in groups])
            r = lax.rsqrt(jnp.mean(o * o, axis=-1, keepdims=True) + EPS)
            y_ref[rows, :] = (o * r * nw_ref[...]).astype(BF16)

    return _pcall(
        body, [q, kv, kv, bias, sinks, nw], name="attn_fwd", grid=(s // (2 * BLK),),
        out_shape=[jax.ShapeDtypeStruct((s, QW), BF16)],
        in_specs=[pl.BlockSpec((2 * BLK, QW), lambda t: (t, 0)),
                  pl.BlockSpec((BLK, 2 * KVW), lambda t: (jnp.maximum(2 * t - 1, 0), 0)),
                  pl.BlockSpec((2 * BLK, 2 * KVW), lambda t: (t, 0)),
                  _row((NH, BLK, 2 * BLK)),
                  pl.BlockSpec(memory_space=pltpu.SMEM),
                  _row((1, QW))],
        out_specs=[pl.BlockSpec((2 * BLK, QW), lambda t: (t, 0))], carry=carry)


def _attn_bwd(q, kv, dya, bias, sinks, nw, carry=None):
    s = q.shape[0]
    nt = s // (2 * BLK)

    def body(q_ref, kvp_ref, kvc_ref, dy_ref, bias_ref, sinks_ref, nw_ref,
             dq_ref, dkv_ref, dbias_ref, dsink_ref, dnw_ref, carry_ref, held_ref):
        t = pl.program_id(0)

        @pl.when(t == 0)
        def _():
            carry_ref[...] = jnp.zeros_like(carry_ref)
            held_ref[...] = jnp.zeros_like(held_ref)
            dbias_ref[...] = jnp.zeros_like(dbias_ref)
            dsink_ref[...] = jnp.zeros_like(dsink_ref)
            dnw_ref[...] = jnp.zeros_like(dnw_ref)

        def block(sub, kv3):
            rows = slice(sub * BLK, (sub + 1) * BLK)
            groups = _attn_heads(jnp.logical_and(t == 0, sub == 0), q_ref[rows, :], kv3[sub * BLK:(sub + 2) * BLK],
                                 bias_ref, sinks_ref)
            o = _unstack_heads([g[5] for g in groups])
            r = lax.rsqrt(jnp.mean(o * o, axis=-1, keepdims=True) + EPS)
            dy = dy_ref[rows, :]
            on = o * r
            dnw_ref[0:1, :] += jnp.sum(dy * on, axis=0, keepdims=True)
            u = dy * nw_ref[...]
            do = r * u - on * (r * jnp.mean(u * on, axis=-1, keepdims=True))
            dq_parts, dk_parts, dv_parts = [], [], []
            for g, (qs, kw, vw, pn, ps, og) in enumerate(groups):
                dos = jnp.concatenate([do[:, (4 * g + r_) * HD:(4 * g + r_ + 1) * HD] for r_ in range(4)], axis=0)
                delta = jnp.sum(dos * og, axis=-1, keepdims=True)
                dp = _mm_nt(dos.astype(BF16), vw)
                ds = pn * (dp - delta)
                dsk = ps * delta
                lane = lax.broadcasted_iota(jnp.int32, (1, 128), 1)
                for r_ in range(4):
                    h = 4 * g + r_
                    dbias_ref[h] += ds[r_ * BLK:(r_ + 1) * BLK]
                    dsink_ref[0:1, :] -= jnp.where(lane == h, jnp.sum(dsk[r_ * BLK:(r_ + 1) * BLK]), 0.0)
                dsb = ds.astype(BF16)
                dq_parts.append(_mm(dsb, kw) * 0.125)
                dk_parts.append(_mm_tn(dsb, qs))
                dv_parts.append(_mm_tn(pn.astype(BF16), dos.astype(BF16)))
            dq_ref[rows, :] = _unstack_heads(dq_parts).astype(BF16)
            return jnp.concatenate(dk_parts + dv_parts, axis=1)

        @pl.when(t < nt)
        def _():
            kv3 = jnp.concatenate([kvp_ref[...], kvc_ref[...]], axis=0)
            d0 = block(0, kv3)
            dkv_ref[0:BLK, :] = held_ref[...].astype(BF16)
            dkv_ref[BLK:2 * BLK, :] = (carry_ref[...] + d0[0:BLK]).astype(BF16)
            d1 = block(1, kv3)
            held_ref[...] = d0[BLK:2 * BLK] + d1[0:BLK]
            carry_ref[...] = d1[BLK:2 * BLK]

        @pl.when(t == nt)
        def _():
            dkv_ref[0:BLK, :] = held_ref[...].astype(BF16)
            dkv_ref[BLK:2 * BLK, :] = carry_ref[...].astype(BF16)

    last = nt - 1
    tile = lambda w: pl.BlockSpec((2 * BLK, w), lambda t: (jnp.minimum(t, last), 0))
    return _pcall(
        body, [q, kv, kv, dya, bias, sinks, nw], name="attn_bwd", grid=(nt + 1,),
        out_shape=[jax.ShapeDtypeStruct((s, QW), BF16), jax.ShapeDtypeStruct((s, 2 * KVW), BF16),
                   jax.ShapeDtypeStruct((NH, BLK, 2 * BLK), F32), jax.ShapeDtypeStruct((NH, 128), F32),
                   jax.ShapeDtypeStruct((8, QW), F32)],
        in_specs=[tile(QW),
                  pl.BlockSpec((BLK, 2 * KVW), lambda t: (jnp.clip(2 * t - 1, 0, 2 * nt - 1), 0)),
                  tile(2 * KVW), tile(QW),
                  _row((NH, BLK, 2 * BLK)),
                  pl.BlockSpec(memory_space=pltpu.SMEM),
                  _row((1, QW))],
        out_specs=[tile(QW),
                   pl.BlockSpec((2 * BLK, 2 * KVW), lambda t: (jnp.maximum(t - 1, 0), 0)),
                   _row((NH, BLK, 2 * BLK)), _row((NH, 128)), _row((8, QW))],
        scratch_shapes=[pltpu.VMEM((BLK, 2 * KVW), F32), pltpu.VMEM((BLK, 2 * KVW), F32)], carry=carry)


def _rel_bias_grad(dbias, bucket):
    def body(db_ref, bk_ref, o_ref):
        bk = bk_ref[...]
        lane = lax.broadcasted_iota(jnp.int32, (1, 128), 1)
        for b in range(NBUCKET):
            sel = bk == b
            row = jnp.zeros((1, 128), F32)
            for h in range(NH):
                row = row + jnp.where(lane == h, jnp.sum(jnp.where(sel, db_ref[h], 0.0)), 0.0)
            o_ref[b:b + 1, :] = row

    return pl.pallas_call(
        body, name="rel_bias_grad",
        out_shape=jax.ShapeDtypeStruct((NBUCKET, 128), F32),
    )(dbias, bucket)


def _ssd_consts():
    head_of_lane = np.arange(SW) // HD
    expand = (np.arange(128)[:, None] == head_of_lane[None, :]).astype(np.float32)
    tril = np.tril(np.ones((BLK, BLK), np.float32))
    return (jnp.asarray(expand, BF16), jnp.asarray(expand.T.copy(), BF16), jnp.asarray(tril, BF16),
            jnp.asarray(tril.T.copy(), BF16))


def _conv_pre(xc, halo, cw, cb):
    ext = jnp.concatenate([halo, xc], axis=0)
    taps = [xc if k == CK - 1 else pltpu.roll(ext, CK - 1 - k, 0)[8:8 + BLK] for k in range(CK)]
    return cb + sum(cw[k:k + 1, :] * taps[k] for k in range(CK))


def _ssd_chunk(pre, dtr, dtb, av, dkv, ex, tril, h_in):
    sp = _sig(pre)
    xbc = pre * sp
    xs, bm, cm = xbc[:, 0:SW], xbc[:, SW:SW + 2 * NST], xbc[:, SW + 2 * NST:]
    dtin = dtr + dtb
    dt = jnp.maximum(dtin, 0.0) + jnp.log1p(jnp.exp(-jnp.abs(dtin)))
    cs = _sel_l(tril, dt * av)
    cst = cs.T
    dtx = _sel_r(dt, ex)
    csx = _sel_r(cs, ex)
    xdt = xs * dtx
    csl = csx[BLK - 1:BLK, :]
    decx = jnp.exp(csl - csx)
    ecsx = jnp.exp(csx)
    ecl = jnp.exp(csl)
    causal = tril.astype(F32) > 0.5
    ydiag, yoff, cbs, lms = [], [], [], []
    for g in range(2):
        bg = bm[:, g * NST:(g + 1) * NST].astype(BF16)
        cg = cm[:, g * NST:(g + 1) * NST].astype(BF16)
        cb = _mm_nt(cg, bg)
        cbs.append(cb)
        yoff.append(_mm(cg, h_in[:, g * 256:(g + 1) * 256].astype(BF16)))
        for r in range(4):
            h = 4 * g + r
            seg = cs[:, h:h + 1] - cst[h:h + 1, :]
            lm = jnp.where(causal, jnp.exp(jnp.minimum(seg, 0.0)), 0.0)
            lms.append(lm)
            ydiag.append(_mm((cb * lm).astype(BF16), xdt[:, h * HD:(h + 1) * HD].astype(BF16)))
    yoff = jnp.concatenate(yoff, axis=1) * ecsx
    y = jnp.concatenate(ydiag, axis=1) + yoff + dkv * xs
    return dict(pre=pre, sp=sp, xs=xs, bm=bm, cm=cm, dtin=dtin, dt=dt, av=av, cs=cs, cst=cst,
                dtx=dtx, csx=csx, xdt=xdt, decx=decx, ecsx=ecsx, ecl=ecl, causal=causal, cbs=cbs, lms=lms,
                yoff=yoff, y=y)


def _group_mean(t):
    m0 = jnp.mean(t[:, 0:256], axis=-1, keepdims=True)
    m1 = jnp.mean(t[:, 256:512], axis=-1, keepdims=True)
    return jnp.concatenate([jnp.broadcast_to(m0, (t.shape[0], 256)), jnp.broadcast_to(m1, (t.shape[0], 256))], axis=1)


SUBS = 2


def _ssd_fwd(z, xbc, dtr, cw, cb, dtb, av, dk, nw, carry=None):
    s = z.shape[0]
    nc = s // BLK
    tile = SUBS * BLK
    ex, _, tril, _ = _ssd_consts()

    def body(z_ref, xc_ref, xh_ref, dtr_ref, cw_ref, cb_ref, dtb_ref, a_ref, dk_ref, nw_ref, ex_ref, tril_ref,
             y_ref, hs_ref, pre_ref, h_ref):
        t = pl.program_id(0)

        @pl.when(t == 0)
        def _():
            h_ref[...] = jnp.zeros_like(h_ref)

        h_in = h_ref[...]
        for sub in range(SUBS):
            rows = slice(sub * BLK, (sub + 1) * BLK)
            xc = xc_ref[rows, :]
            halo = jnp.where(t == 0, 0.0, xh_ref[...]) if sub == 0 else xc_ref[sub * BLK - 8:sub * BLK, :]
            pre = _conv_pre(xc, halo, cw_ref[...], cb_ref[...])
            pre_ref[rows, :] = pre
            hs_ref[sub] = h_in
            f = _ssd_chunk(pre, dtr_ref[rows, :], dtb_ref[...], a_ref[...], dk_ref[...], ex_ref[...], tril_ref[...], h_in)
            dx = (f["decx"] * f["xdt"]).astype(BF16)
            st = [_mm_tn(f["bm"][:, g * NST:(g + 1) * NST].astype(BF16), dx[:, g * 256:(g + 1) * 256]) for g in range(2)]
            h_in = h_in * f["ecl"] + jnp.concatenate(st, axis=1)
            zv = z_ref[rows, :]
            tg = f["y"] * (zv * _sig(zv))
            r = lax.rsqrt(_group_mean(tg * tg) + EPS)
            y_ref[rows, :] = (tg * r * nw_ref[...]).astype(BF16)
        h_ref[...] = h_in

    cur = lambda w: pl.BlockSpec((tile, w), lambda t: (t, 0))
    return _pcall(
        body, [z, xbc, xbc, dtr, cw, cb, dtb, av, dk, nw, ex, tril], name="ssd_fwd", grid=(s // tile,),
        out_shape=[jax.ShapeDtypeStruct((s, SW), BF16), jax.ShapeDtypeStruct((nc, NST, SW), F32),
                   jax.ShapeDtypeStruct((s, XBCW), F32)],
        in_specs=[cur(SW), cur(XBCW), pl.BlockSpec((8, XBCW), lambda t: (jnp.maximum(t * (tile // 8) - 1, 0), 0)),
                  cur(128), _row((8, XBCW)), _row((1, XBCW)), _row((1, 128)),
                  _row((1, 128)), _row((1, SW)), _row((1, SW)), _row((128, SW)), _row((BLK, BLK))],
        out_specs=[cur(SW), pl.BlockSpec((SUBS, NST, SW), lambda t: (t, 0, 0)), cur(XBCW)],
        scratch_shapes=[pltpu.VMEM((NST, SW), F32)], carry=carry)


def _ssd_bwd(z, xbc, pre_all, dtr, dys, hs, cw, dtb, av, dk, nw, carry=None):
    s = z.shape[0]
    tile = SUBS * BLK
    nt = s // tile
    ex, ext_t, tril, triu = _ssd_consts()

    def body(z_ref, xc_ref, pre_ref, dtr_ref, dy_ref, hs_ref, cw_ref, dtb_ref, a_ref, dk_ref, nw_ref,
             ex_ref, ext_ref, tril_ref, triu_ref,
             dz_ref, dxbc_ref, ddt_ref, dcw_ref, dcb_ref, dnw_ref, dhd_ref, dh_ref, nxt_ref, dd_ref):
        i = pl.program_id(0)

        @pl.when(i == 0)
        def _():
            dh_ref[...] = jnp.zeros_like(dh_ref)
            nxt_ref[...] = jnp.zeros_like(nxt_ref)
            dd_ref[...] = jnp.zeros_like(dd_ref)
            dcw_ref[...] = jnp.zeros_like(dcw_ref)
            dcb_ref[...] = jnp.zeros_like(dcb_ref)
            dnw_ref[...] = jnp.zeros_like(dnw_ref)
            dhd_ref[...] = jnp.zeros_like(dhd_ref)

        gst, nxt = dh_ref[...], nxt_ref[...]
        for sub in reversed(range(SUBS)):
            rows = slice(sub * BLK, (sub + 1) * BLK)
            gst, nxt = chunk(sub, rows, gst, nxt, z_ref, xc_ref, pre_ref, dtr_ref, dy_ref, hs_ref, cw_ref, dtb_ref,
                             a_ref, dk_ref, nw_ref, ex_ref, ext_ref, tril_ref, triu_ref,
                             dz_ref, dxbc_ref, ddt_ref, dcw_ref, dcb_ref, dnw_ref, dhd_ref, dd_ref)
        dh_ref[...] = gst
        nxt_ref[...] = nxt

        @pl.when(i == nt - 1)
        def _():
            dhd_ref[2:3, :] = _sel_r(dd_ref[...], ext_ref[...])[0:1, :]

    def chunk(sub, rows, gst, nxt, z_ref, xc_ref, pre_ref, dtr_ref, dy_ref, hs_ref, cw_ref, dtb_ref,
              a_ref, dk_ref, nw_ref, ex_ref, ext_ref, tril_ref, triu_ref,
              dz_ref, dxbc_ref, ddt_ref, dcw_ref, dcb_ref, dnw_ref, dhd_ref, dd_ref):
        h_in = hs_ref[sub]
        f = _ssd_chunk(pre_ref[rows, :], dtr_ref[rows, :], dtb_ref[...], a_ref[...], dk_ref[...], ex_ref[...],
                       tril_ref[...], h_in)
        xs, xdt, decx, ecsx, ecl, dtx = f["xs"], f["xdt"], f["decx"], f["ecsx"], f["ecl"], f["dtx"]
        cs, cst, causal = f["cs"], f["cst"], f["causal"]
        causal_t = triu_ref[...].astype(F32) > 0.5

        zv = z_ref[rows, :]
        sz = _sig(zv)
        gz = zv * sz
        t = f["y"] * gz
        r = lax.rsqrt(_group_mean(t * t) + EPS)
        tn_ = t * r
        dyn = dy_ref[rows, :]
        dnw_ref[0:1, :] += jnp.sum(dyn * tn_, axis=0, keepdims=True)
        u = dyn * nw_ref[...]
        dt_ = r * u - tn_ * (r * _group_mean(u * tn_))
        dy = dt_ * gz
        dz_ref[rows, :] = (dt_ * f["y"] * (sz * (1.0 + zv * (1.0 - sz)))).astype(BF16)

        dd_ref[0:1, :] += jnp.sum(dy * xs, axis=0, keepdims=True)
        dxs = dk_ref[...] * dy

        edy = ecsx * dy
        dxdt, dbs, dcs_, dcsx_parts, dh_new = [], [], [], [], []
        lane = lax.broadcasted_iota(jnp.int32, (1, 128), 1)
        dcs_intra = jnp.zeros((BLK, 128), F32)
        for g in range(2):
            sl = slice(g * 256, (g + 1) * 256)
            bgf, cgf = f["bm"][:, g * NST:(g + 1) * NST], f["cm"][:, g * NST:(g + 1) * NST]
            bg, cg = bgf.astype(BF16), cgf.astype(BF16)
            gg = gst[:, sl].astype(BF16)
            hg = h_in[:, sl].astype(BF16)
            edyg = edy[:, sl].astype(BF16)
            dc = _mm_nt(edyg, hg)
            dh_new.append(gst[:, sl] * ecl[:, sl] + _mm_tn(cg, edyg))
            bgm = _mm(bg, gg)
            dxdt_g = decx[:, sl] * bgm
            dxg = (decx[:, sl] * xdt[:, sl]).astype(BF16)
            db = _mm_nt(dxg, gg)
            qd = bgm * xdt[:, sl] * decx[:, sl]
            last = jnp.sum(qd, axis=0, keepdims=True) + ecl[:, sl] * jnp.sum(gst[:, sl] * h_in[:, sl], axis=0, keepdims=True)
            rowid = lax.broadcasted_iota(jnp.int32, (BLK, 256), 0)
            dcsx_parts.append(f["yoff"][:, sl] * dy[:, sl] - qd + jnp.where(rowid == BLK - 1, last, 0.0))
            cb_ = f["cbs"][g]
            cbt = _mm_nt(bg, cg)
            dcb_ = jnp.zeros((BLK, BLK), F32)
            dcbt = jnp.zeros((BLK, BLK), F32)
            dxd = []
            for r_ in range(4):
                h = 4 * g + r_
                hl = slice(h * HD, (h + 1) * HD)
                lm = f["lms"][h]
                segt = cst[h:h + 1, :] - cs[:, h:h + 1]
                lmt = jnp.where(causal_t, jnp.exp(jnp.minimum(segt, 0.0)), 0.0)
                dyh = dy[:, hl].astype(BF16)
                xdh = xdt[:, hl].astype(BF16)
                dw = _mm_nt(dyh, xdh)
                dwt = _mm_nt(xdh, dyh)
                wt = cbt * lmt
                dxd.append(_mm(wt.astype(BF16), dyh))
                dcb_ = dcb_ + dw * lm
                dcbt = dcbt + dwt * lmt
                col = jnp.sum(dw * (cb_ * lm), axis=-1, keepdims=True) - jnp.sum(dwt * wt, axis=-1, keepdims=True)
                dcs_intra = dcs_intra + jnp.where(lane == h, col, 0.0)
            dxdt.append(dxdt_g + jnp.concatenate(dxd, axis=1))
            dcs_.append(dc + _mm(dcb_.astype(BF16), bg))
            dbs.append(db + _mm(dcbt.astype(BF16), cg))
        dxdt = jnp.concatenate(dxdt, axis=1)
        dxs = dxs + dxdt * dtx
        ext_t_ = ext_ref[...]
        dcs = dcs_intra + _sel_r(jnp.concatenate(dcsx_parts, axis=1), ext_t_)
        da = _sel_l(triu_ref[...], dcs)
        ddt = da * f["av"] + _sel_r(dxdt * xs, ext_t_)
        dhd_ref[1:2, :] += jnp.sum(da * f["dt"], axis=0, keepdims=True)
        ddtr = ddt * _sig(f["dtin"])
        dhd_ref[0:1, :] += jnp.sum(ddtr, axis=0, keepdims=True)
        ddt_ref[rows, :] = ddtr.astype(BF16)

        sp, pre = f["sp"], f["pre"]
        dact = jnp.concatenate([dxs] + dbs + dcs_, axis=1)
        dpre = dact * (sp * (1.0 + pre * (1.0 - sp)))
        dcb_ref[0:1, :] += jnp.sum(dpre, axis=0, keepdims=True)
        ext2 = jnp.concatenate([dpre, nxt], axis=0)
        shifted = [pltpu.roll(ext2, BLK + 8 - (CK - 1 - k), 0)[0:BLK] for k in range(CK - 1)] + [dpre]
        cw = cw_ref[...]
        xc = xc_ref[rows, :]
        dxr = cw[CK - 1:CK, :] * dpre
        for k in range(CK):
            dcw_ref[k:k + 1, :] += jnp.sum(shifted[k] * xc, axis=0, keepdims=True)
            if k < CK - 1:
                dxr = dxr + cw[k:k + 1, :] * shifted[k]
        dxbc_ref[rows, :] = dxr.astype(BF16)
        return jnp.concatenate(dh_new, axis=1), dpre[0:8]

    cur = lambda w: pl.BlockSpec((tile, w), lambda i: (nt - 1 - i, 0))
    return _pcall(
        body, [z, xbc, pre_all, dtr, dys, hs, cw, dtb, av, dk, nw, ex, ext_t, tril, triu], name="ssd_bwd", grid=(nt,),
        out_shape=[jax.ShapeDtypeStruct((s, SW), BF16), jax.ShapeDtypeStruct((s, XBCW), BF16),
                   jax.ShapeDtypeStruct((s, 128), BF16), jax.ShapeDtypeStruct((8, XBCW), F32),
                   jax.ShapeDtypeStruct((8, XBCW), F32), jax.ShapeDtypeStruct((8, SW), F32),
                   jax.ShapeDtypeStruct((8, 128), F32)],
        in_specs=[cur(SW), cur(XBCW), cur(XBCW), cur(128), cur(SW),
                  pl.BlockSpec((SUBS, NST, SW), lambda i: (nt - 1 - i, 0, 0)),
                  _row((8, XBCW)), _row((1, 128)), _row((1, 128)), _row((1, SW)), _row((1, SW)),
                  _row((128, SW)), _row((SW, 128)), _row((BLK, BLK)), _row((BLK, BLK))],
        out_specs=[cur(SW), cur(XBCW), cur(128), _row((8, XBCW)), _row((8, XBCW)), _row((8, SW)), _row((8, 128))],
        scratch_shapes=[pltpu.VMEM((NST, SW), F32), pltpu.VMEM((8, XBCW), F32), pltpu.VMEM((8, SW), F32)], carry=carry)


def _load_once(i, pairs, sem):
    @pl.when(i == 0)
    def _():
        cps = [pltpu.make_async_copy(src, dst, sem.at[k]) for k, (src, dst) in enumerate(pairs)]
        for cp in cps:
            cp.start()
        for cp in cps:
            cp.wait()


def _mlp_fwd(x, ya, ys, tgt, w_o, w_ga, w_gb, w_dn, gate1, a2, sh2, gate2, fn):
    s = x.shape[0]
    sub_m, subs = 256, 2
    tm = sub_m * subs

    def body(x_ref, ya_ref, ys_ref, t_ref, wo_hbm, wga_hbm, wgb_hbm, wdn_hbm, g1_ref, a2_ref, s2_ref, g2_ref, fn_ref,
             x1_ref, gu_ref, dx2_ref, loss_ref, dfn_ref, wo, wga, wgb, wdn, sem):
        i = pl.program_id(0)
        _load_once(i, [(wo_hbm, wo), (wga_hbm, wga), (wgb_hbm, wgb), (wdn_hbm, wdn)], sem)

        @pl.when(i == 0)
        def _():
            loss_ref[...] = jnp.zeros_like(loss_ref)
            dfn_ref[...] = jnp.zeros_like(dfn_ref)

        def proj(st):
            st["mix"] = _mm(ya_ref[st["rows"], :], wo[0:QW, :]) + _mm(ys_ref[st["rows"], :], wo[QW:D, :])

        def norm(st):
            x1 = x_ref[st["rows"], :] + g1_ref[...] * st.pop("mix")
            x1_ref[st["rows"], :] = x1
            r2 = lax.rsqrt(jnp.mean(x1 * x1, axis=-1, keepdims=True) + EPS)
            st["x1"] = x1
            st["h2"] = (x1 * r2 * a2_ref[...] + s2_ref[...]).astype(BF16)

        def gate_up(st):
            h2 = st.pop("h2")
            ha, hb = h2[:, 0:D // 2], h2[:, D // 2:D]
            gub = jnp.concatenate([(_mm(ha, wga[j]) + _mm(hb, wgb[j])).astype(BF16) for j in range(4)], axis=1)
            gu_ref[st["rows"], :] = gub
            st["gub"] = gub

        def activate(st):
            gub = st.pop("gub")
            gv, uv = gub[:, 0:DFF].astype(F32), gub[:, DFF:].astype(F32)
            st["act"] = (gv * _sig(gv) * uv).astype(BF16)

        def down(st):
            st["ff"] = _mm(st.pop("act"), wdn[...])

        def head(st):
            x2 = st.pop("x1") + g2_ref[...] * st.pop("ff")
            r3 = lax.rsqrt(jnp.mean(x2 * x2, axis=-1, keepdims=True) + EPS)
            xn = x2 * r3
            fnv = fn_ref[...]
            err = xn * fnv - t_ref[st["rows"], :]
            st["loss"] = jnp.sum(err * err) * (0.5 / D)
            dy = err * (1.0 / D)
            st["dfn"] = jnp.sum(dy * xn, axis=0, keepdims=True)
            u = dy * fnv
            dx2_ref[st["rows"], :] = r3 * u - xn * (r3 * jnp.mean(u * xn, axis=-1, keepdims=True))

        a, b = [dict(rows=slice(k * sub_m, (k + 1) * sub_m)) for k in range(subs)]
        for stage, st in [(proj, a), (norm, a), (proj, b), (gate_up, a), (norm, b), (activate, a), (gate_up, b),
                          (down, a), (activate, b), (head, a), (down, b), (head, b)]:
            stage(st)
        loss_ref[...] += a["loss"] + b["loss"]
        dfn_ref[0:1, :] += a["dfn"] + b["dfn"]

    def tok(w):
        return pl.BlockSpec((tm, w), lambda i: (i, 0))

    hbm = pl.BlockSpec(memory_space=pl.ANY)
    return pl.pallas_call(
        body, name="mlp_fwd", grid=(s // tm,),
        out_shape=[jax.ShapeDtypeStruct((s, D), F32), jax.ShapeDtypeStruct((s, 2 * DFF), BF16),
                   jax.ShapeDtypeStruct((s, D), F32), jax.ShapeDtypeStruct((8, 128), F32),
                   jax.ShapeDtypeStruct((8, D), F32)],
        in_specs=[tok(D), tok(QW), tok(SW), tok(D), hbm, hbm, hbm, hbm,
                  _row((1, D)), _row((1, D)), _row((1, D)), _row((1, D)), _row((1, D))],
        out_specs=[tok(D), tok(2 * DFF), tok(D), _row((8, 128)), _row((8, D))],
        scratch_shapes=[pltpu.VMEM((D, D), BF16), pltpu.VMEM(w_ga.shape, BF16), pltpu.VMEM(w_gb.shape, BF16),
                        pltpu.VMEM((DFF, D), BF16), pltpu.SemaphoreType.DMA((4,))],
        compiler_params=_cp(("arbitrary",)),
    )(x, ya, ys, tgt, w_o, w_ga, w_gb, w_dn, gate1, a2, sh2, gate2, fn)


def _mlp_bwd(x1, gu, dx2, w_o, w_ga, w_gb, w_dn, gate1, a2, sh2, gate2):
    s = x1.shape[0]
    tm = 256
    nj = 2 * DFF // 4

    def body(x1_ref, gu_ref, dx2_ref, wo_hbm, wga_hbm, wgb_hbm, wdn_hbm, g1_ref, a2_ref, s2_ref, g2_ref,
             dx1_ref, dya_ref, dys_ref, act_ref, dgu_ref, h2_ref, dsh_ref, p_ref, wo, wga, wgb, wdn, sem):
        i = pl.program_id(0)
        _load_once(i, [(wo_hbm, wo), (wga_hbm, wga), (wgb_hbm, wgb), (wdn_hbm, wdn)], sem)

        @pl.when(i == 0)
        def _():
            dsh_ref[...] = jnp.zeros_like(dsh_ref)
            p_ref[...] = jnp.zeros_like(p_ref)

        dx2 = dx2_ref[...]
        dact = _mm_nt((dx2 * g2_ref[...]).astype(BF16), wdn[...])
        gub = gu_ref[...]
        gv, uv = gub[:, 0:DFF].astype(F32), gub[:, DFF:].astype(F32)
        sg = _sig(gv)
        sl = gv * sg
        act_ref[...] = (sl * uv).astype(BF16)
        dgu = jnp.concatenate([dact * uv * (sg * (1.0 + gv * (1.0 - sg))), dact * sl], axis=1).astype(BF16)
        dgu_ref[...] = dgu
        dha = sum(_mm_nt(dgu[:, j * nj:(j + 1) * nj], wga[j]) for j in range(4))
        dhb = sum(_mm_nt(dgu[:, j * nj:(j + 1) * nj], wgb[j]) for j in range(4))
        dh = jnp.concatenate([dha, dhb], axis=1)
        x1 = x1_ref[...]
        r2 = lax.rsqrt(jnp.mean(x1 * x1, axis=-1, keepdims=True) + EPS)
        xn = x1 * r2
        a2 = a2_ref[...]
        h2_ref[...] = (xn * a2 + s2_ref[...]).astype(BF16)
        dsh_ref[0:1, :] += jnp.sum(dh, axis=0, keepdims=True)
        p_ref[0:1, :] += jnp.sum(dh * xn, axis=0, keepdims=True)
        u = dh * a2
        dx1 = dx2 + r2 * u - xn * (r2 * jnp.mean(u * xn, axis=-1, keepdims=True))
        dx1_ref[...] = dx1
        dcat = _mm_nt((dx1 * g1_ref[...]).astype(BF16), wo[...])
        dya_ref[...] = dcat[:, 0:QW]
        dys_ref[...] = dcat[:, QW:D]

    def tok(w):
        return pl.BlockSpec((tm, w), lambda i: (i, 0))

    hbm = pl.BlockSpec(memory_space=pl.ANY)
    return pl.pallas_call(
        body, name="mlp_bwd", grid=(s // tm,),
        out_shape=[jax.ShapeDtypeStruct((s, D), F32), jax.ShapeDtypeStruct((s, QW), F32),
                   jax.ShapeDtypeStruct((s, SW), F32), jax.ShapeDtypeStruct((s, DFF), BF16),
                   jax.ShapeDtypeStruct((s, 2 * DFF), BF16), jax.ShapeDtypeStruct((s, D), BF16),
                   jax.ShapeDtypeStruct((8, D), F32), jax.ShapeDtypeStruct((8, D), F32)],
        in_specs=[tok(D), tok(2 * DFF), tok(D), hbm, hbm, hbm, hbm, _row((1, D)), _row((1, D)), _row((1, D)), _row((1, D))],
        out_specs=[tok(D), tok(QW), tok(SW), tok(DFF), tok(2 * DFF), tok(D), _row((8, D)), _row((8, D))],
        scratch_shapes=[pltpu.VMEM((D, D), BF16), pltpu.VMEM(w_ga.shape, BF16), pltpu.VMEM(w_gb.shape, BF16),
                        pltpu.VMEM((DFF, D), BF16), pltpu.SemaphoreType.DMA((4,))],
        compiler_params=_cp(("arbitrary",)),
    )(x1, gu, dx2, w_o, w_ga, w_gb, w_dn, gate1, a2, sh2, gate2)


def _wgrad(name, a, b, tn, gate=None, w=None, stacked=False, carry=None):
    s, m = a.shape
    n = b.shape[1]
    tk = min(1024, s)
    nk = s // tk

    def body(*refs):
        if gate is None:
            a_ref, b_ref, o_ref = refs
        else:
            a_ref, b_ref, g_ref, w_ref, o_ref, dg_ref = refs
        k = pl.program_id(1)

        @pl.when(k == 0)
        def _():
            o_ref[...] = jnp.zeros_like(o_ref)

        o_ref[...] += _mm_tn(a_ref[...], b_ref[...].astype(BF16))

        if gate is not None:
            @pl.when(k == nk - 1)
            def _():
                acc = o_ref[...]
                dg_ref[...] = jnp.zeros_like(dg_ref)
                dg_ref[0:1, :] = jnp.sum(acc * w_ref[...].astype(F32), axis=0, keepdims=True)
                o_ref[...] = acc * g_ref[...]

    in_specs = [pl.BlockSpec((tk, m), lambda j, k: (k, 0)), pl.BlockSpec((tk, tn), lambda j, k: (k, j))]
    if stacked:
        out_shape = [jax.ShapeDtypeStruct((n // tn, m, tn), F32)]
        out_specs = [pl.BlockSpec((None, m, tn), lambda j, k: (j, 0, 0))]
    else:
        out_shape = [jax.ShapeDtypeStruct((m, n), F32)]
        out_specs = [pl.BlockSpec((m, tn), lambda j, k: (0, j))]
    args = [a, b]
    if gate is not None:
        in_specs += [pl.BlockSpec((1, tn), lambda j, k: (0, j)), pl.BlockSpec((m, tn), lambda j, k: (0, j))]
        out_shape.append(jax.ShapeDtypeStruct((8, n), F32))
        out_specs.append(pl.BlockSpec((8, tn), lambda j, k: (0, j)))
        args += [gate, w]
    return _pcall(body, args, name=name, grid=(n // tn, nk), out_shape=out_shape, in_specs=in_specs,
                  out_specs=out_specs, carry=carry)


def _wgrad_gate_up(h2, dgu, carry=None):
    s = h2.shape[0]
    tk = min(1024, s)
    nk = s // tk
    n = dgu.shape[1]
    nj = n // 4

    def body(a_ref, b_ref, o_hbm, acc_ref, sems):
        k = pl.program_id(0)

        @pl.when(k == 0)
        def _():
            acc_ref[...] = jnp.zeros_like(acc_ref)

        acc_ref[...] += _mm_tn(a_ref[...], b_ref[...])

        @pl.when(k == nk - 1)
        def _():
            cps = [pltpu.make_async_copy(acc_ref.at[:, pl.ds(j * nj, nj)], o_hbm.at[j], sems.at[j]) for j in range(4)]
            for cp in cps:
                cp.start()
            for cp in cps:
                cp.wait()

    return _pcall(body, [h2, dgu], name="wgrad_gate_up", grid=(nk,),
                  out_shape=[jax.ShapeDtypeStruct((4, D, nj), F32)],
                  in_specs=[pl.BlockSpec((tk, D), lambda k: (k, 0)), pl.BlockSpec((tk, n), lambda k: (k, 0))],
                  out_specs=[pl.BlockSpec(memory_space=pl.ANY)],
                  scratch_shapes=[pltpu.VMEM((D, n), F32), pltpu.SemaphoreType.DMA((4,))], carry=carry)


def _wgrad_in_t(h1, pieces, carry=None):
    s = h1.shape[0]
    tk = min(1024, s)
    nk = s // tk

    def body(a_ref, dq_ref, dkv_ref, dz_ref, dxbc_ref, ddt_ref, o_hbm, acc_ref, tr_ref, sem):
        k = pl.program_id(0)

        @pl.when(k == 0)
        def _():
            acc_ref[...] = jnp.zeros_like(acc_ref)

        dproj = jnp.concatenate([dq_ref[...], dkv_ref[...], dz_ref[...], dxbc_ref[...], ddt_ref[...]], axis=1)
        acc_ref[...] += _mm_tn(a_ref[...], dproj)

        @pl.when(k == nk - 1)
        def _():
            for j in range(PROJ_W // 128):
                tr_ref[j * 128:(j + 1) * 128, :] = acc_ref[:, j * 128:(j + 1) * 128].T
            cp = pltpu.make_async_copy(tr_ref, o_hbm, sem)
            cp.start()
            cp.wait()

    return _pcall(body, [h1] + list(pieces), name="wgrad_in", grid=(nk,),
                  out_shape=[jax.ShapeDtypeStruct((PROJ_W, D), F32)],
                  in_specs=[pl.BlockSpec((tk, p.shape[1]), lambda k: (k, 0)) for p in [h1] + list(pieces)],
                  out_specs=[pl.BlockSpec(memory_space=pl.ANY)],
                  scratch_shapes=[pltpu.VMEM((D, PROJ_W), F32), pltpu.VMEM((PROJ_W, D), F32), pltpu.SemaphoreType.DMA],
                  carry=carry)


_SMALL = ["ada_b", "norm1", "conv_w", "conv_b", "dt_bias", "A_log", "D_skip", "sinks", "attn_out_norm",
          "ssm_out_norm", "norm2", "rel_bias", "final_norm"]


def _small_grad(name, gs, chip):
    if name == "ada_b":
        return jnp.concatenate([gs[j:j + 1, :] for j in range(6)], axis=1)
    if name == "conv_w":
        full = gs[7:11, :]
        out = full[:, 0:256]
        for j in range(1, 4):
            out = jnp.where(chip == j, full[:, j * 256:(j + 1) * 256], out)
        return out
    row, width = {"norm1": (6, D), "conv_b": (11, D), "norm2": (12, D), "final_norm": (13, D),
                  "attn_out_norm": (14, QW), "ssm_out_norm": (15, SW), "dt_bias": (16, NH), "A_log": (17, NH),
                  "D_skip": (18, NH), "sinks": (19, NH), "rel_bias": (24, NH)}[name]
    rows = NBUCKET if name == "rel_bias" else 1
    return gs[row:row + rows, 0:width]


def _small_update(small_all, where, ws, ms, vs):
    n = len(_SMALL)

    def body(where_ref, sa_ref, *refs):
        w_refs, m_refs, v_refs, outs = refs[:n], refs[n:2 * n], refs[2 * n:3 * n], refs[3 * n:]
        gs = sa_ref[0]
        for b in range(1, 8):
            gs = gs + sa_ref[b]
        chip = where_ref[1]
        for i, name in enumerate(_SMALL):
            g = _small_grad(name, gs, chip)
            lead = (0,) if name == "conv_w" else ()
            d, mo, vo = _adamw(w_refs[i][lead + (...,)], g, m_refs[i][lead + (...,)], v_refs[i][lead + (...,)])
            for k, val in enumerate((g, d, mo, vo)):
                outs[k * n + i][lead + (...,)] = val
        outs[4 * n][...] = gs[20:21, 0:128]

    shapes = [jax.ShapeDtypeStruct(w.shape, F32) for w in ws]
    vmem = pl.BlockSpec(memory_space=pltpu.VMEM)
    res = pl.pallas_call(
        body, name="small_update", out_shape=shapes * 4 + [jax.ShapeDtypeStruct((1, 128), F32)],
        in_specs=[pl.BlockSpec(memory_space=pltpu.SMEM)] + [vmem] * (1 + 3 * n), out_specs=[vmem] * (4 * n + 1),
    )(where, small_all, *ws, *ms, *vs)
    return [res[k * n:(k + 1) * n] for k in range(4)], res[4 * n][0, 0]


def _add_half(name, g, got, where, by_cols=False):
    rr, cc = got.shape[1:]
    if by_cols:
        mine = pl.BlockSpec((None, rr, cc), lambda i, w_ref: (i, 0, w_ref[0]))
    else:
        mine = pl.BlockSpec((None, None, rr, cc), lambda i, w_ref: (i, w_ref[0], 0, 0))

    def body(w_ref, g_ref, r_ref, o_ref, own_ref):
        s = g_ref[...] + r_ref[...]
        o_ref[...] = s.astype(BF16)

        @pl.when(pl.program_id(0) == w_ref[1])
        def _():
            own_ref[...] = s

    spec = pl.BlockSpec((None, rr, cc), lambda i, w_ref: (i, 0, 0))
    return _pcall(body, [where, g, got], name=name, grid=(4,), nprefetch=1,
                  out_shape=[jax.ShapeDtypeStruct(got.shape, BF16), jax.ShapeDtypeStruct((rr, cc), F32)],
                  in_specs=[mine, spec],
                  out_specs=[spec, pl.BlockSpec((rr, cc), lambda i, w_ref: (0, 0))])


def _add_chips(name, own, got):
    rr, cc = own.shape
    tr = rr // 2 if rr % 32 == 0 else rr

    def body(s_ref, r_ref, o_ref):
        o_ref[...] = ((s_ref[...] + r_ref[0].astype(F32)) + r_ref[1].astype(F32)) + r_ref[2].astype(F32)

    spec = pl.BlockSpec((tr, cc), lambda i: (i, 0))
    return _pcall(body, [own, got], name=name, grid=(rr // tr,), out_shape=[jax.ShapeDtypeStruct((rr, cc), F32)],
                  in_specs=[spec, pl.BlockSpec((3, tr, cc), lambda i: (0, i, 0))], out_specs=[spec])[0]


def _adamw_halves(name, mine, got, w, m, v, where, by_cols=False):
    rr, cc = mine.shape

    def body(w_ref_, t_ref, r_ref, w_ref, m_ref, v_ref, g_ref, d_ref, mo_ref, vo_ref):
        g = jnp.where(pl.program_id(0) == w_ref_[0], t_ref[...], r_ref[...])
        g_ref[...] = g
        d_ref[...], mo_ref[...], vo_ref[...] = _adamw(w_ref[...], g, m_ref[...], v_ref[...])

    if by_cols:
        grid = (2, 1)
        half = pl.BlockSpec((rr, cc), lambda h, i, w_ref_: (0, 0))
        full = pl.BlockSpec((rr, cc), lambda h, i, w_ref_: (0, h))
    else:
        tr = rr // 2
        grid = (2, 2)
        half = pl.BlockSpec((tr, cc), lambda h, i, w_ref_: (i, 0))
        full = pl.BlockSpec((None, tr, cc), lambda h, i, w_ref_: (0, 2 * h + i, 0))
    return _pcall(body, [where, mine, got, w, m, v], name=name, grid=grid, nprefetch=1,
                  out_shape=[jax.ShapeDtypeStruct(w.shape, F32)] * 4,
                  in_specs=[half, half, full, full, full], out_specs=[full] * 4)


def _bias_table(rel_bias, bucket, mask):
    def body(rb_ref, bk_ref, mk_ref, o_ref):
        bk = bk_ref[...]
        valid = mk_ref[...] > 0
        for h in range(NH):
            acc = jnp.zeros((BLK, 2 * BLK), F32)
            for b in range(NBUCKET):
                acc = jnp.where(bk == b, rb_ref[b, h], acc)
            o_ref[h] = jnp.where(valid, acc, NEG)

    vmem = pl.BlockSpec(memory_space=pltpu.VMEM)
    return pl.pallas_call(
        body, name="bias_table", out_shape=jax.ShapeDtypeStruct((NH, BLK, 2 * BLK), F32),
        in_specs=[pl.BlockSpec(memory_space=pltpu.SMEM), vmem, vmem], out_specs=vmem,
    )(rel_bias, bucket, mask)


def _pack_small(dsh1, p1, dsh2, p2, dg1a, dg1b, dg2, norm1, norm2, scale1, scale2, dcw, dcb, dfn,
                dnw_attn, dnw_ssm, dhd, av, dsink, drel, loss_acc):
    def body(dsh1_ref, p1_ref, dsh2_ref, p2_ref, dg1a_ref, dg1b_ref, dg2_ref, n1_ref, n2_ref, s1_ref, s2_ref,
             dcw_ref, dcb_ref, dfn_ref, da_ref, ds_ref, dhd_ref, av_ref, dsink_ref, drel_ref, loss_ref, o_ref):
        o_ref[...] = jnp.zeros_like(o_ref)
        p1v, p2v = p1_ref[0:1, :], p2_ref[0:1, :]
        o_ref[0:1, :] = dsh1_ref[0:1, :]
        o_ref[1:2, :] = p1v * n1_ref[...]
        o_ref[2:3, :] = dg1a_ref[0:1, :] + dg1b_ref[0:1, :]
        o_ref[3:4, :] = dsh2_ref[0:1, :]
        o_ref[4:5, :] = p2v * n2_ref[...]
        o_ref[5:6, :] = dg2_ref[0:1, :]
        o_ref[6:7, :] = p1v * (1.0 + s1_ref[...])
        o_ref[7:11, :] = dcw_ref[0:4, :]
        o_ref[11:12, :] = dcb_ref[0:1, :]
        o_ref[12:13, :] = p2v * (1.0 + s2_ref[...])
        o_ref[13:14, :] = dfn_ref[0:1, :]
        o_ref[14:15, 0:QW] = da_ref[0:1, :]
        o_ref[15:16, 0:SW] = ds_ref[0:1, :]
        o_ref[16:17, 0:128] = dhd_ref[0:1, :]
        o_ref[17:18, 0:128] = dhd_ref[1:2, :] * av_ref[...]
        o_ref[18:19, 0:128] = dhd_ref[2:3, :]
        o_ref[19:20, 0:128] = dsink_ref[0:1, :]
        o_ref[20:21, 0:128] = loss_ref[0:1, :]
        o_ref[24:56, 0:128] = drel_ref[...]

    return pl.pallas_call(body, name="pack_small", out_shape=jax.ShapeDtypeStruct((56, D), F32))(
        dsh1, p1, dsh2, p2, dg1a, dg1b, dg2, norm1, norm2, scale1, scale2, dcw, dcb, dfn,
        dnw_attn, dnw_ssm, dhd, av, dsink, drel, loss_acc)


def _pad_row(a, rows=1):
    return jnp.pad(a.reshape(rows, -1), ((0, 0), (0, D - a.size // rows)))


def kernel(x, c, ada_w, ada_b, norm1, w_in, conv_w, conv_b, dt_bias, A_log, D_skip, sinks, attn_out_norm, ssm_out_norm, w_o, norm2, w_gate_up, w_down, rel_bias, final_norm, loss_target, m_ada_w, m_ada_b, m_norm1, m_w_in, m_conv_w, m_conv_b, m_dt_bias, m_A_log, m_D_skip, m_sinks, m_attn_out_norm, m_ssm_out_norm, m_w_o, m_norm2, m_w_gate_up, m_w_down, m_rel_bias, m_final_norm, v_ada_w, v_ada_b, v_norm1, v_w_in, v_conv_w, v_conv_b, v_dt_bias, v_A_log, v_D_skip, v_sinks, v_attn_out_norm, v_ssm_out_norm, v_w_o, v_norm2, v_w_gate_up, v_w_down, v_rel_bias, v_final_norm):
    xi, yi, ci = lax.axis_index("x"), lax.axis_index("y"), lax.axis_index("c")
    chip = 2 * xi + yi
    me = 4 * xi + 2 * yi + ci
    where = jnp.stack([ci, chip]).astype(jnp.int32)
    xs2, tgt = x[0], loss_target[0]

    first = jnp.concatenate([c, _pad_row(conv_w[0], CK), jnp.zeros((3, D), F32)], axis=0)
    w_in_t, m_w_in_t, v_w_in_t = w_in[0].T, m_w_in[0].T, v_w_in[0].T
    w_in_b, w_o_b, w_dn_b = w_in_t.astype(BF16), w_o[0].astype(BF16), w_down[0].astype(BF16)
    w_gu_b = w_gate_up[0].astype(BF16)
    hw = D // 2
    fetch_half = _Carry(
        [w_in_b], [jax.ShapeDtypeStruct((4,) + w_in_b.shape, BF16)],
        lambda x_, y_, c_: [(None, 0, slice(None), 0, 2 * x_ + y_)] + [
            (f, 0, (slice(None), pl.ds(c_ * hw, hw)), 0, (2 * x_ + y_, slice(None), pl.ds(c_ * hw, hw))) for f in _CHIPS3])

    def swap_halves(x_, y_, c_):
        there = [(jnp.bitwise_xor(2 * x_ + y_, k + 1), slice(None), pl.ds(c_ * hw, hw)) for k in range(3)]
        return [(_SIBLING, 1, at, 1, at) for at in there]

    first_all, w_in_g = _exchange_two("gather_first", _merge(_gather8_carry(first), fetch_half), swap_halves)
    c_all = first_all[:, 0, :]
    cw_full = jnp.concatenate([first_all[2 * j, 1:1 + CK, 0:256] for j in range(4)], axis=1)
    w_in_f = jnp.pad(w_in_g.reshape(IN_W, D), ((0, PROJ_W - IN_W), (0, 0)))

    ncol = ada_w.shape[2]
    mod_cols = _ada_fwd(c_all, ada_w[0], lax.dynamic_slice(ada_b, (0, chip * ncol), (1, ncol)))
    mod_all = _exchange("gather_mod", _gather_chips_carry([mod_cols]))[0]
    mod = lax.dynamic_slice(jnp.transpose(mod_all, (1, 0, 2)).reshape(8, 4 * ncol), (me, 0), (1, 4 * ncol))
    shift1, scale1, gate1, shift2, scale2, gate2 = [mod[:, j * D:(j + 1) * D] for j in range(6)]
    a1 = norm1 * (1.0 + scale1)
    a2 = norm2 * (1.0 + scale2)

    hdn = DFF // 8
    q, kv, z, xbc, dtr, w_o_g, w_dna_g = _in_proj_fwd(xs2, a1, shift1, w_in_f,
                                                      carry=_gather_chips_carry([w_o_b, w_dn_b[0:hdn]]))
    w_o_f = w_o_g.reshape(D, D)
    bucket, mask = _attn_geometry()
    bucket = jnp.asarray(bucket)
    bias = _bias_table(rel_bias, bucket, jnp.asarray(mask.astype(np.int32)))
    sinks1 = sinks[0]
    ya, w_ga_g = _attn_fwd(q, kv, bias, sinks1, attn_out_norm, carry=_gather_chips_carry([w_gu_b[0:D // 2]]))
    cw8 = jnp.concatenate([cw_full, jnp.zeros((4, XBCW), F32)], axis=0)
    dtb = _pad_row(dt_bias)[:, 0:128]
    av = _pad_row(-jnp.exp(A_log))[:, 0:128]
    dk = jnp.repeat(D_skip, HD, axis=1)
    ys, hs, pre, w_gb_g, w_dnb_g = _ssd_fwd(z, xbc, dtr, cw8, conv_b, dtb, av, dk, ssm_out_norm,
                                            carry=_gather_chips_carry([w_gu_b[D // 2:D], w_dn_b[hdn:2 * hdn]]))
    w_dn_f = jnp.stack([w_dna_g, w_dnb_g], axis=1).reshape(DFF, D)
    fn = final_norm[None, :]
    x1, gu, dx2, loss_acc, dfn = _mlp_fwd(xs2, ya, ys, tgt, w_o_f, w_ga_g, w_gb_g, w_dn_f, gate1, a2, shift2, gate2, fn)

    def to_sibling(p):
        return _Carry([p], [jax.ShapeDtypeStruct((4,) + p.shape[2:], F32)],
                      lambda x_, y_, c_: [(_SIBLING, 0, (j, 1 - c_), 0, j) for j in range(4)])

    def to_chips(s4):
        return _Carry([s4], [jax.ShapeDtypeStruct((3,) + s4.shape[1:], s4.dtype)],
                      lambda x_, y_, c_: [(f, 0, jnp.bitwise_xor(2 * x_ + y_, k + 1), 0, k) for k, f in enumerate(_CHIPS3)])

    def back(t):
        return _Carry([t[None]], [jax.ShapeDtypeStruct((1,) + t.shape, F32)], lambda x_, y_, c_: [(_SIBLING, 0, 0, 0, 0)])

    dx1, dya, dys, act, dgu, h2, dsh2, p2 = _mlp_bwd(x1, gu, dx2, w_o_f, w_ga_g, w_gb_g, w_dn_f, gate1, a2, shift2, gate2)
    p_gu = _wgrad_gate_up(h2, dgu)[0].reshape(4, 2, D // 2, 2 * DFF // 4)
    g_dn, dg2, got1_gu = _wgrad("wgrad_down", act, dx2, D // 2, gate2, w_dn_f, carry=to_sibling(p_gu))
    p_dn = g_dn.reshape(4, 2, DFF // 8, D)
    s4_gu, own_gu = _add_half("rs_add_half_gu", p_gu, got1_gu, where)
    dq, dkv, dbias, dsink, dnw_attn, got2_gu, got1_dn = _attn_bwd(
        q, kv, dya, bias, sinks1, attn_out_norm, carry=_merge(to_chips(s4_gu), to_sibling(p_dn)))
    drel = _rel_bias_grad(dbias, bucket)
    mine_gu = _add_chips("rs_add_chips_gu", own_gu, got2_gu)
    s4_dn, own_dn = _add_half("rs_add_half_dn", p_dn, got1_dn, where)
    dz, dxbc, ddt, dcw, dcb, dnw_ssm, dhd, got2_dn, got3_gu = _ssd_bwd(
        z, xbc, pre, dtr, dys, hs, cw8, dtb, av, dk, ssm_out_norm, carry=_merge(to_chips(s4_dn), back(mine_gu)))
    mine_dn = _add_chips("rs_add_chips_dn", own_dn, got2_dn)
    grad_x, h1, dsh1, p1 = _in_proj_bwd(xs2, dx1, a1, shift1, w_in_f, dq, dkv, dz, dxbc, ddt)
    g_in_t, got3_dn = _wgrad_in_t(h1, [dq, dkv, dz, dxbc, ddt], carry=back(mine_dn))
    p_in = g_in_t[0:IN_W].reshape(4, IN_W // 4, D)

    def to_sibling_cols(p):
        return _Carry([p], [jax.ShapeDtypeStruct(p.shape[:2] + (D // 2,), F32)],
                      lambda x_, y_, c_: [(_SIBLING, 0, (j, slice(None), pl.ds((1 - c_) * (D // 2), D // 2)), 0, j)
                                          for j in range(4)])

    g_oa, dg1a, got1_in = _wgrad("wgrad_o_attn", ya, dx1, D, gate1, w_o_f[0:QW], carry=to_sibling_cols(p_in))
    s4_in, own_in = _add_half("rs_add_half_in", p_in, got1_in, where, by_cols=True)
    g_os, dg1b, got2_in = _wgrad("wgrad_o_ssm", ys, dx1, D, gate1, w_o_f[QW:D], carry=to_chips(s4_in))
    mine_in = _add_chips("rs_add_chips_in", own_in, got2_in)
    p_o = jnp.concatenate([g_oa, g_os], axis=0).reshape(4, 2, D // 8, D)

    small = _pack_small(dsh1, p1, dsh2, p2, dg1a, dg1b, dg2, norm1, norm2, scale1, scale2, dcw, dcb, dfn,
                        dnw_attn, dnw_ssm, dhd, av, dsink, drel, loss_acc)
    small_all, got1_o, got3_in = _exchange(
        "gather_small", _merge(_gather8_carry(small), to_sibling(p_o), back(mine_in)))
    s4_o, own_o = _add_half("rs_add_half_o", p_o, got1_o, where)
    mine_o = _add_chips("rs_add_chips_o", own_o, _exchange("rs_chips_o", to_chips(s4_o))[0])
    got3_o = _exchange("rs_back_o", back(mine_o))[0]
    small_res, loss = _small_update(
        small_all, where,
        [ada_b, norm1, conv_w, conv_b, dt_bias, A_log, D_skip, sinks, attn_out_norm, ssm_out_norm, norm2, rel_bias,
         final_norm[None, :]],
        [m_ada_b, m_norm1, m_conv_w, m_conv_b, m_dt_bias, m_A_log, m_D_skip, m_sinks, m_attn_out_norm,
         m_ssm_out_norm, m_norm2, m_rel_bias, m_final_norm[None, :]],
        [v_ada_b, v_norm1, v_conv_w, v_conv_b, v_dt_bias, v_A_log, v_D_skip, v_sinks, v_attn_out_norm,
         v_ssm_out_norm, v_norm2, v_rel_bias, v_final_norm[None, :]])
    small_out = [dict(zip(_SMALL, r)) for r in small_res]
    for r in small_out:
        r["final_norm"] = r["final_norm"][0]

    dmod_all = small_all[:, 0:6, :].reshape(8, 6 * D)
    dmod_loc = lax.dynamic_slice(dmod_all, (0, chip * ncol), (8, ncol))
    ada_out = _ada_bwd_adamw(c_all.T, dmod_loc, ada_w[0], m_ada_w[0], v_ada_w[0])

    big_gu = _adamw_halves("adamw_gate_up", mine_gu, got3_gu[0], w_gate_up, m_w_gate_up, v_w_gate_up, where)
    big_dn = _adamw_halves("adamw_down", mine_dn, got3_dn[0], w_down, m_w_down, v_w_down, where)
    big_o = _adamw_halves("adamw_o", mine_o, got3_o[0], w_o, m_w_o, v_w_o, where)
    big_in = [o.T[None] for o in _adamw_halves("adamw_in", mine_in, got3_in[0], w_in_t, m_w_in_t, v_w_in_t, where,
                                               by_cols=True)]
    big = [big_in, big_o, big_gu, big_dn]

    order = ["ada_w", "ada_b", "norm1", "w_in", "conv_w", "conv_b", "dt_bias", "A_log", "D_skip", "sinks",
             "attn_out_norm", "ssm_out_norm", "w_o", "norm2", "w_gate_up", "w_down", "rel_bias", "final_norm"]
    bigname = {"w_in": 0, "w_o": 1, "w_gate_up": 2, "w_down": 3}
    res = [loss, grad_x[None]]
    for kind in range(4):
        for nm in order:
            if nm == "ada_w":
                res.append(ada_out[kind][None])
            elif nm in bigname:
                res.append(big[bigname[nm]][kind])
            else:
                res.append(small_out[kind][nm])
    return tuple(res)
```

```python
import numpy as np
import jax
import jax.numpy as jnp
from jax import lax
from jax.experimental import pallas as pl
from jax.experimental.pallas import tpu as pltpu

F32, BF16 = jnp.float32, jnp.bfloat16
HI = lax.Precision.HIGHEST

D = 1024
QW, KVW = 512, 128
NH, HD, NKV = 8, 64, 2
SW = 512
NST = 128
XBCW = 1024
CK = 4
BLK = 128
DFF = 2816
IN_W = 2312
PROJ_W = 2432
EPS = 1e-6
NEG = -1e30
NBUCKET = 32

B1, B2, LR, AEPS, WD, STEP = 0.9, 0.999, 0.001, 1e-08, 0.01, 10

VMEM_LIMIT = 56 * 1024 * 1024

_NT = (((1,), (1,)), ((), ()))
_TN = (((0,), (0,)), ((), ()))


def _mm(a, b):
    return jnp.dot(a, b, preferred_element_type=F32)


def _mm_nt(a, b):
    return lax.dot_general(a, b, _NT, preferred_element_type=F32)


def _mm_tn(a, b):
    return lax.dot_general(a, b, _TN, preferred_element_type=F32)


def _mm_hi(a, b):
    return jnp.dot(a, b, preferred_element_type=F32, precision=HI)


def _split3(x):
    hi = x.astype(BF16)
    r = x - hi.astype(F32)
    mid = r.astype(BF16)
    lo = (r - mid.astype(F32)).astype(BF16)
    return hi, mid, lo


def _sel_r(x, e):
    hi, mid, lo = _split3(x)
    return (_mm(hi, e) + _mm(mid, e)) + _mm(lo, e)


def _sel_l(e, x):
    hi, mid, lo = _split3(x)
    return (_mm(e, hi) + _mm(e, mid)) + _mm(e, lo)


def _sig(x):
    return 1.0 / (1.0 + jnp.exp(-x))


def _cp(sem):
    return pltpu.CompilerParams(dimension_semantics=sem, vmem_limit_bytes=VMEM_LIMIT)


def _row(shape):
    nd = len(shape)
    return pl.BlockSpec(shape, lambda *_: (0,) * nd)


def _adamw(w, g, m, v):
    m = B1 * m + (1.0 - B1) * g
    v = B2 * v + (1.0 - B2) * (g * g)
    m_hat = m / (1.0 - B1 ** STEP)
    v_hat = v / (1.0 - B2 ** STEP)
    delta = -LR * (m_hat / (jnp.sqrt(v_hat) + AEPS) + WD * w)
    return delta, m, v


class _Carry:
    def __init__(self, inps, outs, copies):
        self.inps, self.outs, self.copies = list(inps), list(outs), copies
        self.n = len(copies(0, 0, 0))

    def descriptors(self, in_refs, out_refs, send_sems, recv_sems):
        x, y, c = lax.axis_index("x"), lax.axis_index("y"), lax.axis_index("c")
        out = []
        for j, (flip, a, si, o, di) in enumerate(self.copies(x, y, c)):
            if flip is None:
                out.append(pltpu.make_async_copy(in_refs[a].at[si], out_refs[o].at[di], send_sems.at[j]))
            else:
                fx, fy, fc = flip
                peer = (1 - x if fx else x, 1 - y if fy else y, 1 - c if fc else c)
                out.append(pltpu.make_async_remote_copy(
                    src_ref=in_refs[a].at[si], dst_ref=out_refs[o].at[di],
                    send_sem=send_sems.at[j], recv_sem=recv_sems.at[j],
                    device_id=peer, device_id_type=pl.DeviceIdType.MESH))
        return out


def _pcall(body, args, *, name, grid, in_specs, out_specs, out_shape, scratch_shapes=(), sem=None, nprefetch=0,
           carry=None):
    out_shape, out_specs = list(out_shape), list(out_specs)
    in_specs, scratch_shapes = list(in_specs), list(scratch_shapes)
    nin, nout, nscr = len(in_specs), len(out_shape), len(scratch_shapes)
    run = body
    if carry is not None:
        ncin, ncout = len(carry.inps), len(carry.outs)
        hbm = pl.BlockSpec(memory_space=pl.ANY)

        def run(*refs):
            pre, r = refs[:nprefetch], refs[nprefetch:]
            ins, cins = r[:nin], r[nin:nin + ncin]
            r = r[nin + ncin:]
            outs, couts = r[:nout], r[nout:nout + ncout]
            r = r[nout + ncout:]
            scr, (send_sems, recv_sems) = r[:nscr], r[nscr:]
            first = pl.program_id(0) == 0
            last = pl.program_id(0) == grid[0] - 1
            for ax in range(1, len(grid)):
                first = jnp.logical_and(first, pl.program_id(ax) == 0)
                last = jnp.logical_and(last, pl.program_id(ax) == grid[ax] - 1)

            @pl.when(first)
            def _():
                for d in carry.descriptors(cins, couts, send_sems, recv_sems):
                    d.start()

            body(*pre, *ins, *outs, *scr)

            @pl.when(last)
            def _():
                for d in carry.descriptors(cins, couts, send_sems, recv_sems):
                    d.wait()

        in_specs = in_specs + [hbm] * ncin
        out_specs = out_specs + [hbm] * ncout
        out_shape = out_shape + carry.outs
        scratch_shapes = scratch_shapes + [pltpu.SemaphoreType.DMA((carry.n,)), pltpu.SemaphoreType.DMA((carry.n,))]
        args = list(args) + carry.inps
    if sem is None:
        sem = ("arbitrary",) * len(grid)
    if nprefetch:
        kw = dict(grid_spec=pltpu.PrefetchScalarGridSpec(num_scalar_prefetch=nprefetch, grid=grid, in_specs=in_specs,
                                                         out_specs=out_specs, scratch_shapes=scratch_shapes))
    else:
        kw = dict(grid=grid, in_specs=in_specs, out_specs=out_specs, scratch_shapes=scratch_shapes)
    res = pl.pallas_call(run, name=name, out_shape=out_shape, compiler_params=_cp(sem), **kw)(*args)
    return list(res)


def _merge(*carries):
    inps, outs, offs = [], [], []
    for cr in carries:
        offs.append((len(inps), len(outs)))
        inps += cr.inps
        outs += cr.outs

    def copies(x, y, c):
        return [(f, a + io, si, o + oo, di) for cr, (io, oo) in zip(carries, offs) for f, a, si, o, di in cr.copies(x, y, c)]

    return _Carry(inps, outs, copies)


def _exchange(name, carry):
    return _pcall(lambda: None, [], name=name, grid=(1,), in_specs=[], out_specs=[], out_shape=[], carry=carry)


def _exchange_two(name, carry, then):
    second = _Carry([], [], then)
    nin, nout = len(carry.inps), len(carry.outs)

    def body(*refs):
        ins, outs = refs[:nin], refs[nin:nin + nout]
        send_a, recv_a, send_b, recv_b = refs[nin + nout:]
        for descs in (carry.descriptors(ins, outs, send_a, recv_a), second.descriptors(outs, outs, send_b, recv_b)):
            for d in descs:
                d.start()
            for d in descs:
                d.wait()

    hbm = pl.BlockSpec(memory_space=pl.ANY)
    return list(pl.pallas_call(
        body, name=name, out_shape=carry.outs, in_specs=[hbm] * nin, out_specs=[hbm] * nout,
        scratch_shapes=[pltpu.SemaphoreType.DMA((carry.n,)), pltpu.SemaphoreType.DMA((carry.n,)),
                        pltpu.SemaphoreType.DMA((second.n,)), pltpu.SemaphoreType.DMA((second.n,))],
    )(*carry.inps))


_ALL7 = [(f >> 2 & 1, f >> 1 & 1, f & 1) for f in range(1, 8)]
_CHIPS3 = [(0, 1, 0), (1, 0, 0), (1, 1, 0)]
_SIBLING = (0, 0, 1)


def _gather8_carry(blk):
    def copies(x, y, c):
        me = 4 * x + 2 * y + c
        return [(None, 0, 0, 0, me)] + [(f, 0, 0, 0, me) for f in _ALL7]

    return _Carry([blk[None]], [jax.ShapeDtypeStruct((8,) + blk.shape, blk.dtype)], copies)


def _gather_chips_carry(blks):
    def copies(x, y, c):
        chip = 2 * x + y
        return [(f, a, 0, a, chip) for a in range(len(blks)) for f in [None] + _CHIPS3]

    return _Carry([b[None] for b in blks], [jax.ShapeDtypeStruct((4,) + b.shape, b.dtype) for b in blks], copies)


def _ada_fwd(c_all, w_loc, b_loc):
    n = w_loc.shape[1]
    tn = 512

    def body(c_ref, w_ref, b_ref, o_ref):
        cv = c_ref[...]
        cond = cv * _sig(cv)
        o_ref[...] = _mm_hi(cond, w_ref[...]) + b_ref[...]

    return pl.pallas_call(
        body, name="ada_fwd", grid=(n // tn,),
        out_shape=jax.ShapeDtypeStruct((8, n), F32),
        in_specs=[_row((8, D)), pl.BlockSpec((D, tn), lambda j: (0, j)), pl.BlockSpec((1, tn), lambda j: (0, j))],
        out_specs=pl.BlockSpec((8, tn), lambda j: (0, j)),
        compiler_params=_cp(("parallel",)),
    )(c_all, w_loc, b_loc)


def _ada_bwd_adamw(c_all_t, dmod_loc, w, m, v, carry=None):
    n = w.shape[1]
    tn = 512

    def body(ct_ref, dm_ref, w_ref, m_ref, v_ref, g_ref, d_ref, mo_ref, vo_ref):
        ct = ct_ref[...]
        cond = ct * _sig(ct)
        dm = dm_ref[...]
        g = cond[:, 0:1] * dm[0:1, :]
        for b in range(1, 8):
            g = g + cond[:, b:b + 1] * dm[b:b + 1, :]
        g_ref[...] = g
        d_ref[...], mo_ref[...], vo_ref[...] = _adamw(w_ref[...], g, m_ref[...], v_ref[...])

    wspec = pl.BlockSpec((D, tn), lambda j: (0, j))
    return _pcall(
        body, [c_all_t, dmod_loc, w, m, v], name="ada_bwd_adamw", grid=(n // tn,),
        out_shape=[jax.ShapeDtypeStruct((D, n), F32)] * 4,
        in_specs=[_row((D, 8)), pl.BlockSpec((8, tn), lambda j: (0, j)), wspec, wspec, wspec],
        out_specs=[wspec] * 4, carry=carry)


def _in_proj_fwd(x, a1, sh1, w_in, carry=None):
    s = x.shape[0]
    tm = 512

    def body(x_ref, a_ref, s_ref, w_ref, q_ref, kv_ref, z_ref, xbc_ref, dt_ref):
        xv = x_ref[...]
        r = lax.rsqrt(jnp.mean(xv * xv, axis=-1, keepdims=True) + EPS)
        h = (xv * r * a_ref[...] + s_ref[...]).astype(BF16)
        p = _mm_nt(h, w_ref[...])
        q_ref[...] = p[:, 0:512].astype(BF16)
        kv_ref[...] = p[:, 512:768].astype(BF16)
        z_ref[...] = p[:, 768:1280]
        xbc_ref[...] = p[:, 1280:2304]
        dt_ref[...] = p[:, 2304:2432]

    def tok(w):
        return pl.BlockSpec((tm, w), lambda i: (i, 0))

    return _pcall(
        body, [x, a1, sh1, w_in], name="in_proj_fwd", grid=(s // tm,),
        out_shape=[jax.ShapeDtypeStruct((s, QW), BF16), jax.ShapeDtypeStruct((s, 2 * KVW), BF16),
                   jax.ShapeDtypeStruct((s, SW), F32), jax.ShapeDtypeStruct((s, XBCW), F32),
                   jax.ShapeDtypeStruct((s, 128), F32)],
        in_specs=[tok(D), _row((1, D)), _row((1, D)), _row((PROJ_W, D))],
        out_specs=[tok(QW), tok(2 * KVW), tok(SW), tok(XBCW), tok(128)], carry=carry)


def _in_proj_bwd(x, dx1, a1, sh1, w_in, dq, dkv, dz, dxbc, ddt, carry=None):
    s = x.shape[0]
    tm = 512

    def body(x_ref, dx1_ref, a_ref, s_ref, w_ref, dq_ref, dkv_ref, dz_ref, dxbc_ref, ddt_ref,
             gx_ref, h_ref, dsh_ref, p_ref):
        i = pl.program_id(0)

        @pl.when(i == 0)
        def _():
            dsh_ref[...] = jnp.zeros_like(dsh_ref)
            p_ref[...] = jnp.zeros_like(p_ref)

        def gather(st):
            rows = st["rows"]
            st["dproj"] = jnp.concatenate([dq_ref[rows, :], dkv_ref[rows, :], dz_ref[rows, :], dxbc_ref[rows, :],
                                           ddt_ref[rows, :]], axis=1)

        def back(st):
            st["dh"] = _mm(st.pop("dproj"), w_ref[...])

        def norm(st):
            rows, dh = st["rows"], st.pop("dh")
            xv = x_ref[rows, :]
            r = lax.rsqrt(jnp.mean(xv * xv, axis=-1, keepdims=True) + EPS)
            xn = xv * r
            a = a_ref[...]
            h_ref[rows, :] = (xn * a + s_ref[...]).astype(BF16)
            st["dsh"] = jnp.sum(dh, axis=0, keepdims=True)
            st["p"] = jnp.sum(dh * xn, axis=0, keepdims=True)
            u = dh * a
            gx_ref[rows, :] = dx1_ref[rows, :] + r * u - xn * (r * jnp.mean(u * xn, axis=-1, keepdims=True))

        g0, g1 = [dict(rows=slice(k * (tm // 2), (k + 1) * (tm // 2))) for k in range(2)]
        for stage, st in [(gather, g0), (back, g0), (gather, g1), (norm, g0), (back, g1), (norm, g1)]:
            stage(st)
        dsh_ref[0:1, :] += g0["dsh"] + g1["dsh"]
        p_ref[0:1, :] += g0["p"] + g1["p"]

    def tok(w):
        return pl.BlockSpec((tm, w), lambda i: (i, 0))

    return _pcall(
        body, [x, dx1, a1, sh1, w_in, dq, dkv, dz, dxbc, ddt], name="in_proj_bwd", grid=(s // tm,),
        out_shape=[jax.ShapeDtypeStruct((s, D), F32), jax.ShapeDtypeStruct((s, D), BF16),
                   jax.ShapeDtypeStruct((8, D), F32), jax.ShapeDtypeStruct((8, D), F32)],
        in_specs=[tok(D), tok(D), _row((1, D)), _row((1, D)), _row((PROJ_W, D)),
                  tok(QW), tok(2 * KVW), tok(SW), tok(XBCW), tok(128)],
        out_specs=[tok(D), tok(D), _row((8, D)), _row((8, D))], carry=carry)


def _attn_geometry():
    dist = np.arange(BLK)[:, None] + BLK - np.arange(2 * BLK)[None, :]
    n = np.maximum(dist, 0)
    max_exact = NBUCKET // 2
    large = max_exact + (np.log(np.maximum(n, 1) / max_exact) / np.log(128 / max_exact)
                         * (NBUCKET - max_exact)).astype(np.int32)
    large = np.minimum(large, NBUCKET - 1)
    bucket = np.where(n < max_exact, n, large).astype(np.int32)
    mask = (dist >= 0) & (dist < 128)
    return bucket, mask


def _attn_heads(is_first, q_blk, kvw, bias_ref, sinks_ref):
    qv = q_blk * 0.125
    col = lax.broadcasted_iota(jnp.int32, (BLK, 2 * BLK), 1)
    first = jnp.where(jnp.logical_and(is_first, col < BLK), NEG, 0.0)
    groups = []
    for g in range(NKV):
        qs = jnp.concatenate([qv[:, (4 * g + r) * HD:(4 * g + r + 1) * HD] for r in range(4)], axis=0)
        kw = kvw[:, g * HD:(g + 1) * HD]
        vw = kvw[:, KVW + g * HD:KVW + (g + 1) * HD]
        sc = _mm_nt(qs, kw)
        pn, ps = [], []
        for r in range(4):
            h = 4 * g + r
            sr = sc[r * BLK:(r + 1) * BLK] + bias_ref[h] + first
            sink = sinks_ref[h]
            m = jnp.maximum(jnp.max(sr, axis=-1, keepdims=True), sink)
            p = jnp.exp(sr - m)
            es = jnp.exp(sink - m)
            inv = 1.0 / (jnp.sum(p, axis=-1, keepdims=True) + es)
            pn.append(p * inv)
            ps.append(es * inv)
        pn = jnp.concatenate(pn, axis=0)
        ps = jnp.concatenate(ps, axis=0)
        o = _mm(pn.astype(BF16), vw)
        groups.append((qs, kw, vw, pn, ps, o))
    return groups


def _unstack_heads(parts):
    return jnp.concatenate([p[r * BLK:(r + 1) * BLK] for p in parts for r in range(4)], axis=1)


def _attn_fwd(q, kv, bias, sinks, nw, carry=None):
    s = q.shape[0]

    def body(q_ref, kvp_ref, kvc_ref, bias_ref, sinks_ref, nw_ref, y_ref):
        t = pl.program_id(0)
        kv3 = jnp.concatenate([kvp_ref[...], kvc_ref[...]], axis=0)
        for sub in range(2):
            rows = slice(sub * BLK, (sub + 1) * BLK)
            groups = _attn_heads(jnp.logical_and(t == 0, sub == 0), q_ref[rows, :], kv3[sub * BLK:(sub + 2) * BLK],
                                 bias_ref, sinks_ref)
            o = _unstack_heads([g[5] for g in groups])
            r = lax.rsqrt(jnp.mean(o * o, axis=-1, keepdims=True) + EPS)
            y_ref[rows, :] = (o * r * nw_ref[...]).astype(BF16)

    return _pcall(
        body, [q, kv, kv, bias, sinks, nw], name="attn_fwd", grid=(s // (2 * BLK),),
        out_shape=[jax.ShapeDtypeStruct((s, QW), BF16)],
        in_specs=[pl.BlockSpec((2 * BLK, QW), lambda t: (t, 0)),
                  pl.BlockSpec((BLK, 2 * KVW), lambda t: (jnp.maximum(2 * t - 1, 0), 0)),
                  pl.BlockSpec((2 * BLK, 2 * KVW), lambda t: (t, 0)),
                  _row((NH, BLK, 2 * BLK)),
                  pl.BlockSpec(memory_space=pltpu.SMEM),
                  _row((1, QW))],
        out_specs=[pl.BlockSpec((2 * BLK, QW), lambda t: (t, 0))], carry=carry)


def _attn_bwd(q, kv, dya, bias, sinks, nw, carry=None):
    s = q.shape[0]
    nt = s // (2 * BLK)

    def body(q_ref, kvp_ref, kvc_ref, dy_ref, bias_ref, sinks_ref, nw_ref,
             dq_ref, dkv_ref, dbias_ref, dsink_ref, dnw_ref, carry_ref, held_ref):
        t = pl.program_id(0)

        @pl.when(t == 0)
        def _():
            carry_ref[...] = jnp.zeros_like(carry_ref)
            held_ref[...] = jnp.zeros_like(held_ref)
            dbias_ref[...] = jnp.zeros_like(dbias_ref)
            dsink_ref[...] = jnp.zeros_like(dsink_ref)
            dnw_ref[...] = jnp.zeros_like(dnw_ref)

        def block(sub, kv3):
            rows = slice(sub * BLK, (sub + 1) * BLK)
            groups = _attn_heads(jnp.logical_and(t == 0, sub == 0), q_ref[rows, :], kv3[sub * BLK:(sub + 2) * BLK],
                                 bias_ref, sinks_ref)
            o = _unstack_heads([g[5] for g in groups])
            r = lax.rsqrt(jnp.mean(o * o, axis=-1, keepdims=True) + EPS)
            dy = dy_ref[rows, :]
            on = o * r
            dnw_ref[0:1, :] += jnp.sum(dy * on, axis=0, keepdims=True)
            u = dy * nw_ref[...]
            do = r * u - on * (r * jnp.mean(u * on, axis=-1, keepdims=True))
            dq_parts, dk_parts, dv_parts = [], [], []
            for g, (qs, kw, vw, pn, ps, og) in enumerate(groups):
                dos = jnp.concatenate([do[:, (4 * g + r_) * HD:(4 * g + r_ + 1) * HD] for r_ in range(4)], axis=0)
                delta = jnp.sum(dos * og, axis=-1, keepdims=True)
                dp = _mm_nt(dos.astype(BF16), vw)
                ds = pn * (dp - delta)
                dsk = ps * delta
                lane = lax.broadcasted_iota(jnp.int32, (1, 128), 1)
                for r_ in range(4):
                    h = 4 * g + r_
                    dbias_ref[h] += ds[r_ * BLK:(r_ + 1) * BLK]
                    dsink_ref[0:1, :] -= jnp.where(lane == h, jnp.sum(dsk[r_ * BLK:(r_ + 1) * BLK]), 0.0)
                dsb = ds.astype(BF16)
                dq_parts.append(_mm(dsb, kw) * 0.125)
                dk_parts.append(_mm_tn(dsb, qs))
                dv_parts.append(_mm_tn(pn.astype(BF16), dos.astype(BF16)))
            dq_ref[rows, :] = _unstack_heads(dq_parts).astype(BF16)
            return jnp.concatenate(dk_parts + dv_parts, axis=1)

        @pl.when(t < nt)
        def _():
            kv3 = jnp.concatenate([kvp_ref[...], kvc_ref[...]], axis=0)
            d0 = block(0, kv3)
            dkv_ref[0:BLK, :] = held_ref[...].astype(BF16)
            dkv_ref[BLK:2 * BLK, :] = (carry_ref[...] + d0[0:BLK]).astype(BF16)
            d1 = block(1, kv3)
            held_ref[...] = d0[BLK:2 * BLK] + d1[0:BLK]
            carry_ref[...] = d1[BLK:2 * BLK]

        @pl.when(t == nt)
        def _():
            dkv_ref[0:BLK, :] = held_ref[...].astype(BF16)
            dkv_ref[BLK:2 * BLK, :] = carry_ref[...].astype(BF16)

    last = nt - 1
    tile = lambda w: pl.BlockSpec((2 * BLK, w), lambda t: (jnp.minimum(t, last), 0))
    return _pcall(
        body, [q, kv, kv, dya, bias, sinks, nw], name="attn_bwd", grid=(nt + 1,),
        out_shape=[jax.ShapeDtypeStruct((s, QW), BF16), jax.ShapeDtypeStruct((s, 2 * KVW), BF16),
                   jax.ShapeDtypeStruct((NH, BLK, 2 * BLK), F32), jax.ShapeDtypeStruct((NH, 128), F32),
                   jax.ShapeDtypeStruct((8, QW), F32)],
        in_specs=[tile(QW),
                  pl.BlockSpec((BLK, 2 * KVW), lambda t: (jnp.clip(2 * t - 1, 0, 2 * nt - 1), 0)),
                  tile(2 * KVW), tile(QW),
                  _row((NH, BLK, 2 * BLK)),
                  pl.BlockSpec(memory_space=pltpu.SMEM),
                  _row((1, QW))],
        out_specs=[tile(QW),
                   pl.BlockSpec((2 * BLK, 2 * KVW), lambda t: (jnp.maximum(t - 1, 0), 0)),
                   _row((NH, BLK, 2 * BLK)), _row((NH, 128)), _row((8, QW))],
        scratch_shapes=[pltpu.VMEM((BLK, 2 * KVW), F32), pltpu.VMEM((BLK, 2 * KVW), F32)], carry=carry)


def _rel_bias_grad(dbias, bucket):
    def body(db_ref, bk_ref, o_ref):
        bk = bk_ref[...]
        lane = lax.broadcasted_iota(jnp.int32, (1, 128), 1)
        for b in range(NBUCKET):
            sel = bk == b
            row = jnp.zeros((1, 128), F32)
            for h in range(NH):
                row = row + jnp.where(lane == h, jnp.sum(jnp.where(sel, db_ref[h], 0.0)), 0.0)
            o_ref[b:b + 1, :] = row

    return pl.pallas_call(
        body, name="rel_bias_grad",
        out_shape=jax.ShapeDtypeStruct((NBUCKET, 128), F32),
    )(dbias, bucket)


def _ssd_consts():
    head_of_lane = np.arange(SW) // HD
    expand = (np.arange(128)[:, None] == head_of_lane[None, :]).astype(np.float32)
    tril = np.tril(np.ones((BLK, BLK), np.float32))
    return (jnp.asarray(expand, BF16), jnp.asarray(expand.T.copy(), BF16), jnp.asarray(tril, BF16),
            jnp.asarray(tril.T.copy(), BF16))


def _conv_pre(xc, halo, cw, cb):
    ext = jnp.concatenate([halo, xc], axis=0)
    taps = [xc if k == CK - 1 else pltpu.roll(ext, CK - 1 - k, 0)[8:8 + BLK] for k in range(CK)]
    return cb + sum(cw[k:k + 1, :] * taps[k] for k in range(CK))


def _ssd_chunk(pre, dtr, dtb, av, dkv, ex, tril, h_in):
    sp = _sig(pre)
    xbc = pre * sp
    xs, bm, cm = xbc[:, 0:SW], xbc[:, SW:SW + 2 * NST], xbc[:, SW + 2 * NST:]
    dtin = dtr + dtb
    dt = jnp.maximum(dtin, 0.0) + jnp.log1p(jnp.exp(-jnp.abs(dtin)))
    cs = _sel_l(tril, dt * av)
    cst = cs.T
    dtx = _sel_r(dt, ex)
    csx = _sel_r(cs, ex)
    xdt = xs * dtx
    csl = csx[BLK - 1:BLK, :]
    decx = jnp.exp(csl - csx)
    ecsx = jnp.exp(csx)
    ecl = jnp.exp(csl)
    causal = tril.astype(F32) > 0.5
    ydiag, yoff, cbs, lms = [], [], [], []
    for g in range(2):
        bg = bm[:, g * NST:(g + 1) * NST].astype(BF16)
        cg = cm[:, g * NST:(g + 1) * NST].astype(BF16)
        cb = _mm_nt(cg, bg)
        cbs.append(cb)
        yoff.append(_mm(cg, h_in[:, g * 256:(g + 1) * 256].astype(BF16)))
        for r in range(4):
            h = 4 * g + r
            seg = cs[:, h:h + 1] - cst[h:h + 1, :]
            lm = jnp.where(causal, jnp.exp(jnp.minimum(seg, 0.0)), 0.0)
            lms.append(lm)
            ydiag.append(_mm((cb * lm).astype(BF16), xdt[:, h * HD:(h + 1) * HD].astype(BF16)))
    yoff = jnp.concatenate(yoff, axis=1) * ecsx
    y = jnp.concatenate(ydiag, axis=1) + yoff + dkv * xs
    return dict(pre=pre, sp=sp, xs=xs, bm=bm, cm=cm, dtin=dtin, dt=dt, av=av, cs=cs, cst=cst,
                dtx=dtx, csx=csx, xdt=xdt, decx=decx, ecsx=ecsx, ecl=ecl, causal=causal, cbs=cbs, lms=lms,
                yoff=yoff, y=y)


def _group_mean(t):
    m0 = jnp.mean(t[:, 0:256], axis=-1, keepdims=True)
    m1 = jnp.mean(t[:, 256:512], axis=-1, keepdims=True)
    return jnp.concatenate([jnp.broadcast_to(m0, (t.shape[0], 256)), jnp.broadcast_to(m1, (t.shape[0], 256))], axis=1)


SUBS = 2


def _ssd_fwd(z, xbc, dtr, cw, cb, dtb, av, dk, nw, carry=None):
    s = z.shape[0]
    nc = s // BLK
    tile = SUBS * BLK
    ex, _, tril, _ = _ssd_consts()

    def body(z_ref, xc_ref, xh_ref, dtr_ref, cw_ref, cb_ref, dtb_ref, a_ref, dk_ref, nw_ref, ex_ref, tril_ref,
             y_ref, hs_ref, pre_ref, h_ref):
        t = pl.program_id(0)

        @pl.when(t == 0)
        def _():
            h_ref[...] = jnp.zeros_like(h_ref)

        h_in = h_ref[...]
        for sub in range(SUBS):
            rows = slice(sub * BLK, (sub + 1) * BLK)
            xc = xc_ref[rows, :]
            halo = jnp.where(t == 0, 0.0, xh_ref[...]) if sub == 0 else xc_ref[sub * BLK - 8:sub * BLK, :]
            pre = _conv_pre(xc, halo, cw_ref[...], cb_ref[...])
            pre_ref[rows, :] = pre
            hs_ref[sub] = h_in
            f = _ssd_chunk(pre, dtr_ref[rows, :], dtb_ref[...], a_ref[...], dk_ref[...], ex_ref[...], tril_ref[...], h_in)
            dx = (f["decx"] * f["xdt"]).astype(BF16)
            st = [_mm_tn(f["bm"][:, g * NST:(g + 1) * NST].astype(BF16), dx[:, g * 256:(g + 1) * 256]) for g in range(2)]
            h_in = h_in * f["ecl"] + jnp.concatenate(st, axis=1)
            zv = z_ref[rows, :]
            tg = f["y"] * (zv * _sig(zv))
            r = lax.rsqrt(_group_mean(tg * tg) + EPS)
            y_ref[rows, :] = (tg * r * nw_ref[...]).astype(BF16)
        h_ref[...] = h_in

    cur = lambda w: pl.BlockSpec((tile, w), lambda t: (t, 0))
    return _pcall(
        body, [z, xbc, xbc, dtr, cw, cb, dtb, av, dk, nw, ex, tril], name="ssd_fwd", grid=(s // tile,),
        out_shape=[jax.ShapeDtypeStruct((s, SW), BF16), jax.ShapeDtypeStruct((nc, NST, SW), F32),
                   jax.ShapeDtypeStruct((s, XBCW), F32)],
        in_specs=[cur(SW), cur(XBCW), pl.BlockSpec((8, XBCW), lambda t: (jnp.maximum(t * (tile // 8) - 1, 0), 0)),
                  cur(128), _row((8, XBCW)), _row((1, XBCW)), _row((1, 128)),
                  _row((1, 128)), _row((1, SW)), _row((1, SW)), _row((128, SW)), _row((BLK, BLK))],
        out_specs=[cur(SW), pl.BlockSpec((SUBS, NST, SW), lambda t: (t, 0, 0)), cur(XBCW)],
        scratch_shapes=[pltpu.VMEM((NST, SW), F32)], carry=carry)


def _ssd_bwd(z, xbc, pre_all, dtr, dys, hs, cw, dtb, av, dk, nw, carry=None):
    s = z.shape[0]
    tile = SUBS * BLK
    nt = s // tile
    ex, ext_t, tril, triu = _ssd_consts()

    def body(z_ref, xc_ref, pre_ref, dtr_ref, dy_ref, hs_ref, cw_ref, dtb_ref, a_ref, dk_ref, nw_ref,
             ex_ref, ext_ref, tril_ref, triu_ref,
             dz_ref, dxbc_ref, ddt_ref, dcw_ref, dcb_ref, dnw_ref, dhd_ref, dh_ref, nxt_ref, dd_ref):
        i = pl.program_id(0)

        @pl.when(i == 0)
        def _():
            dh_ref[...] = jnp.zeros_like(dh_ref)
            nxt_ref[...] = jnp.zeros_like(nxt_ref)
            dd_ref[...] = jnp.zeros_like(dd_ref)
            dcw_ref[...] = jnp.zeros_like(dcw_ref)
            dcb_ref[...] = jnp.zeros_like(dcb_ref)
            dnw_ref[...] = jnp.zeros_like(dnw_ref)
            dhd_ref[...] = jnp.zeros_like(dhd_ref)

        gst, nxt = dh_ref[...], nxt_ref[...]
        for sub in reversed(range(SUBS)):
            rows = slice(sub * BLK, (sub + 1) * BLK)
            gst, nxt = chunk(sub, rows, gst, nxt, z_ref, xc_ref, pre_ref, dtr_ref, dy_ref, hs_ref, cw_ref, dtb_ref,
                             a_ref, dk_ref, nw_ref, ex_ref, ext_ref, tril_ref, triu_ref,
                             dz_ref, dxbc_ref, ddt_ref, dcw_ref, dcb_ref, dnw_ref, dhd_ref, dd_ref)
        dh_ref[...] = gst
        nxt_ref[...] = nxt

        @pl.when(i == nt - 1)
        def _():
            dhd_ref[2:3, :] = _sel_r(dd_ref[...], ext_ref[...])[0:1, :]

    def chunk(sub, rows, gst, nxt, z_ref, xc_ref, pre_ref, dtr_ref, dy_ref, hs_ref, cw_ref, dtb_ref,
              a_ref, dk_ref, nw_ref, ex_ref, ext_ref, tril_ref, triu_ref,
              dz_ref, dxbc_ref, ddt_ref, dcw_ref, dcb_ref, dnw_ref, dhd_ref, dd_ref):
        h_in = hs_ref[sub]
        f = _ssd_chunk(pre_ref[rows, :], dtr_ref[rows, :], dtb_ref[...], a_ref[...], dk_ref[...], ex_ref[...],
                       tril_ref[...], h_in)
        xs, xdt, decx, ecsx, ecl, dtx = f["xs"], f["xdt"], f["decx"], f["ecsx"], f["ecl"], f["dtx"]
        cs, cst, causal = f["cs"], f["cst"], f["causal"]
        causal_t = triu_ref[...].astype(F32) > 0.5

        zv = z_ref[rows, :]
        sz = _sig(zv)
        gz = zv * sz
        t = f["y"] * gz
        r = lax.rsqrt(_group_mean(t * t) + EPS)
        tn_ = t * r
        dyn = dy_ref[rows, :]
        dnw_ref[0:1, :] += jnp.sum(dyn * tn_, axis=0, keepdims=True)
        u = dyn * nw_ref[...]
        dt_ = r * u - tn_ * (r * _group_mean(u * tn_))
        dy = dt_ * gz
        dz_ref[rows, :] = (dt_ * f["y"] * (sz * (1.0 + zv * (1.0 - sz)))).astype(BF16)

        dd_ref[0:1, :] += jnp.sum(dy * xs, axis=0, keepdims=True)
        dxs = dk_ref[...] * dy

        edy = ecsx * dy
        dxdt, dbs, dcs_, dcsx_parts, dh_new = [], [], [], [], []
        lane = lax.broadcasted_iota(jnp.int32, (1, 128), 1)
        dcs_intra = jnp.zeros((BLK, 128), F32)
        for g in range(2):
            sl = slice(g * 256, (g + 1) * 256)
            bgf, cgf = f["bm"][:, g * NST:(g + 1) * NST], f["cm"][:, g * NST:(g + 1) * NST]
            bg, cg = bgf.astype(BF16), cgf.astype(BF16)
            gg = gst[:, sl].astype(BF16)
            hg = h_in[:, sl].astype(BF16)
            edyg = edy[:, sl].astype(BF16)
            dc = _mm_nt(edyg, hg)
            dh_new.append(gst[:, sl] * ecl[:, sl] + _mm_tn(cg, edyg))
            bgm = _mm(bg, gg)
            dxdt_g = decx[:, sl] * bgm
            dxg = (decx[:, sl] * xdt[:, sl]).astype(BF16)
            db = _mm_nt(dxg, gg)
            qd = bgm * xdt[:, sl] * decx[:, sl]
            last = jnp.sum(qd, axis=0, keepdims=True) + ecl[:, sl] * jnp.sum(gst[:, sl] * h_in[:, sl], axis=0, keepdims=True)
            rowid = lax.broadcasted_iota(jnp.int32, (BLK, 256), 0)
            dcsx_parts.append(f["yoff"][:, sl] * dy[:, sl] - qd + jnp.where(rowid == BLK - 1, last, 0.0))
            cb_ = f["cbs"][g]
            cbt = _mm_nt(bg, cg)
            dcb_ = jnp.zeros((BLK, BLK), F32)
            dcbt = jnp.zeros((BLK, BLK), F32)
            dxd = []
            for r_ in range(4):
                h = 4 * g + r_
                hl = slice(h * HD, (h + 1) * HD)
                lm = f["lms"][h]
                segt = cst[h:h + 1, :] - cs[:, h:h + 1]
                lmt = jnp.where(causal_t, jnp.exp(jnp.minimum(segt, 0.0)), 0.0)
                dyh = dy[:, hl].astype(BF16)
                xdh = xdt[:, hl].astype(BF16)
                dw = _mm_nt(dyh, xdh)
                dwt = _mm_nt(xdh, dyh)
                wt = cbt * lmt
                dxd.append(_mm(wt.astype(BF16), dyh))
                dcb_ = dcb_ + dw * lm
                dcbt = dcbt + dwt * lmt
                col = jnp.sum(dw * (cb_ * lm), axis=-1, keepdims=True) - jnp.sum(dwt * wt, axis=-1, keepdims=True)
                dcs_intra = dcs_intra + jnp.where(lane == h, col, 0.0)
            dxdt.append(dxdt_g + jnp.concatenate(dxd, axis=1))
            dcs_.append(dc + _mm(dcb_.astype(BF16), bg))
            dbs.append(db + _mm(dcbt.astype(BF16), cg))
        dxdt = jnp.concatenate(dxdt, axis=1)
        dxs = dxs + dxdt * dtx
        ext_t_ = ext_ref[...]
        dcs = dcs_intra + _sel_r(jnp.concatenate(dcsx_parts, axis=1), ext_t_)
        da = _sel_l(triu_ref[...], dcs)
        ddt = da * f["av"] + _sel_r(dxdt * xs, ext_t_)
        dhd_ref[1:2, :] += jnp.sum(da * f["dt"], axis=0, keepdims=True)
        ddtr = ddt * _sig(f["dtin"])
        dhd_ref[0:1, :] += jnp.sum(ddtr, axis=0, keepdims=True)
        ddt_ref[rows, :] = ddtr.astype(BF16)

        sp, pre = f["sp"], f["pre"]
        dact = jnp.concatenate([dxs] + dbs + dcs_, axis=1)
        dpre = dact * (sp * (1.0 + pre * (1.0 - sp)))
        dcb_ref[0:1, :] += jnp.sum(dpre, axis=0, keepdims=True)
        ext2 = jnp.concatenate([dpre, nxt], axis=0)
        shifted = [pltpu.roll(ext2, BLK + 8 - (CK - 1 - k), 0)[0:BLK] for k in range(CK - 1)] + [dpre]
        cw = cw_ref[...]
        xc = xc_ref[rows, :]
        dxr = cw[CK - 1:CK, :] * dpre
        for k in range(CK):
            dcw_ref[k:k + 1, :] += jnp.sum(shifted[k] * xc, axis=0, keepdims=True)
            if k < CK - 1:
                dxr = dxr + cw[k:k + 1, :] * shifted[k]
        dxbc_ref[rows, :] = dxr.astype(BF16)
        return jnp.concatenate(dh_new, axis=1), dpre[0:8]

    cur = lambda w: pl.BlockSpec((tile, w), lambda i: (nt - 1 - i, 0))
    return _pcall(
        body, [z, xbc, pre_all, dtr, dys, hs, cw, dtb, av, dk, nw, ex, ext_t, tril, triu], name="ssd_bwd", grid=(nt,),
        out_shape=[jax.ShapeDtypeStruct((s, SW), BF16), jax.ShapeDtypeStruct((s, XBCW), BF16),
                   jax.ShapeDtypeStruct((s, 128), BF16), jax.ShapeDtypeStruct((8, XBCW), F32),
                   jax.ShapeDtypeStruct((8, XBCW), F32), jax.ShapeDtypeStruct((8, SW), F32),
                   jax.ShapeDtypeStruct((8, 128), F32)],
        in_specs=[cur(SW), cur(XBCW), cur(XBCW), cur(128), cur(SW),
                  pl.BlockSpec((SUBS, NST, SW), lambda i: (nt - 1 - i, 0, 0)),
                  _row((8, XBCW)), _row((1, 128)), _row((1, 128)), _row((1, SW)), _row((1, SW)),
                  _row((128, SW)), _row((SW, 128)), _row((BLK, BLK)), _row((BLK, BLK))],
        out_specs=[cur(SW), cur(XBCW), cur(128), _row((8, XBCW)), _row((8, XBCW)), _row((8, SW)), _row((8, 128))],
        scratch_shapes=[pltpu.VMEM((NST, SW), F32), pltpu.VMEM((8, XBCW), F32), pltpu.VMEM((8, SW), F32)], carry=carry)


def _load_once(i, pairs, sem):
    @pl.when(i == 0)
    def _():
        cps = [pltpu.make_async_copy(src, dst, sem.at[k]) for k, (src, dst) in enumerate(pairs)]
        for cp in cps:
            cp.start()
        for cp in cps:
            cp.wait()


def _mlp_fwd(x, ya, ys, tgt, w_o, w_ga, w_gb, w_dn, gate1, a2, sh2, gate2, fn):
    s = x.shape[0]
    sub_m, subs = 256, 2
    tm = sub_m * subs

    def body(x_ref, ya_ref, ys_ref, t_ref, wo_hbm, wga_hbm, wgb_hbm, wdn_hbm, g1_ref, a2_ref, s2_ref, g2_ref, fn_ref,
             x1_ref, gu_ref, dx2_ref, loss_ref, dfn_ref, wo, wga, wgb, wdn, sem):
        i = pl.program_id(0)
        _load_once(i, [(wo_hbm, wo), (wga_hbm, wga), (wgb_hbm, wgb), (wdn_hbm, wdn)], sem)

        @pl.when(i == 0)
        def _():
            loss_ref[...] = jnp.zeros_like(loss_ref)
            dfn_ref[...] = jnp.zeros_like(dfn_ref)

        def proj(st):
            st["mix"] = _mm(ya_ref[st["rows"], :], wo[0:QW, :]) + _mm(ys_ref[st["rows"], :], wo[QW:D, :])

        def norm(st):
            x1 = x_ref[st["rows"], :] + g1_ref[...] * st.pop("mix")
            x1_ref[st["rows"], :] = x1
            r2 = lax.rsqrt(jnp.mean(x1 * x1, axis=-1, keepdims=True) + EPS)
            st["x1"] = x1
            st["h2"] = (x1 * r2 * a2_ref[...] + s2_ref[...]).astype(BF16)

        def gate_up(st):
            h2 = st.pop("h2")
            ha, hb = h2[:, 0:D // 2], h2[:, D // 2:D]
            gub = jnp.concatenate([(_mm(ha, wga[j]) + _mm(hb, wgb[j])).astype(BF16) for j in range(4)], axis=1)
            gu_ref[st["rows"], :] = gub
            st["gub"] = gub

        def activate(st):
            gub = st.pop("gub")
            gv, uv = gub[:, 0:DFF].astype(F32), gub[:, DFF:].astype(F32)
            st["act"] = (gv * _sig(gv) * uv).astype(BF16)

        def down(st):
            st["ff"] = _mm(st.pop("act"), wdn[...])

        def head(st):
            x2 = st.pop("x1") + g2_ref[...] * st.pop("ff")
            r3 = lax.rsqrt(jnp.mean(x2 * x2, axis=-1, keepdims=True) + EPS)
            xn = x2 * r3
            fnv = fn_ref[...]
            err = xn * fnv - t_ref[st["rows"], :]
            st["loss"] = jnp.sum(err * err) * (0.5 / D)
            dy = err * (1.0 / D)
            st["dfn"] = jnp.sum(dy * xn, axis=0, keepdims=True)
            u = dy * fnv
            dx2_ref[st["rows"], :] = r3 * u - xn * (r3 * jnp.mean(u * xn, axis=-1, keepdims=True))

        a, b = [dict(rows=slice(k * sub_m, (k + 1) * sub_m)) for k in range(subs)]
        for stage, st in [(proj, a), (norm, a), (proj, b), (gate_up, a), (norm, b), (activate, a), (gate_up, b),
                          (down, a), (activate, b), (head, a), (down, b), (head, b)]:
            stage(st)
        loss_ref[...] += a["loss"] + b["loss"]
        dfn_ref[0:1, :] += a["dfn"] + b["dfn"]

    def tok(w):
        return pl.BlockSpec((tm, w), lambda i: (i, 0))

    hbm = pl.BlockSpec(memory_space=pl.ANY)
    return pl.pallas_call(
        body, name="mlp_fwd", grid=(s // tm,),
        out_shape=[jax.ShapeDtypeStruct((s, D), F32), jax.ShapeDtypeStruct((s, 2 * DFF), BF16),
                   jax.ShapeDtypeStruct((s, D), F32), jax.ShapeDtypeStruct((8, 128), F32),
                   jax.ShapeDtypeStruct((8, D), F32)],
        in_specs=[tok(D), tok(QW), tok(SW), tok(D), hbm, hbm, hbm, hbm,
                  _row((1, D)), _row((1, D)), _row((1, D)), _row((1, D)), _row((1, D))],
        out_specs=[tok(D), tok(2 * DFF), tok(D), _row((8, 128)), _row((8, D))],
        scratch_shapes=[pltpu.VMEM((D, D), BF16), pltpu.VMEM(w_ga.shape, BF16), pltpu.VMEM(w_gb.shape, BF16),
                        pltpu.VMEM((DFF, D), BF16), pltpu.SemaphoreType.DMA((4,))],
        compiler_params=_cp(("arbitrary",)),
    )(x, ya, ys, tgt, w_o, w_ga, w_gb, w_dn, gate1, a2, sh2, gate2, fn)


def _mlp_bwd(x1, gu, dx2, w_o, w_ga, w_gb, w_dn, gate1, a2, sh2, gate2):
    s = x1.shape[0]
    tm = 256
    nj = 2 * DFF // 4

    def body(x1_ref, gu_ref, dx2_ref, wo_hbm, wga_hbm, wgb_hbm, wdn_hbm, g1_ref, a2_ref, s2_ref, g2_ref,
             dx1_ref, dya_ref, dys_ref, act_ref, dgu_ref, h2_ref, dsh_ref, p_ref, wo, wga, wgb, wdn, sem):
        i = pl.program_id(0)
        _load_once(i, [(wo_hbm, wo), (wga_hbm, wga), (wgb_hbm, wgb), (wdn_hbm, wdn)], sem)

        @pl.when(i == 0)
        def _():
            dsh_ref[...] = jnp.zeros_like(dsh_ref)
            p_ref[...] = jnp.zeros_like(p_ref)

        dx2 = dx2_ref[...]
        dact = _mm_nt((dx2 * g2_ref[...]).astype(BF16), wdn[...])
        gub = gu_ref[...]
        gv, uv = gub[:, 0:DFF].astype(F32), gub[:, DFF:].astype(F32)
        sg = _sig(gv)
        sl = gv * sg
        act_ref[...] = (sl * uv).astype(BF16)
        dgu = jnp.concatenate([dact * uv * (sg * (1.0 + gv * (1.0 - sg))), dact * sl], axis=1).astype(BF16)
        dgu_ref[...] = dgu
        dha = sum(_mm_nt(dgu[:, j * nj:(j + 1) * nj], wga[j]) for j in range(4))
        dhb = sum(_mm_nt(dgu[:, j * nj:(j + 1) * nj], wgb[j]) for j in range(4))
        dh = jnp.concatenate([dha, dhb], axis=1)
        x1 = x1_ref[...]
        r2 = lax.rsqrt(jnp.mean(x1 * x1, axis=-1, keepdims=True) + EPS)
        xn = x1 * r2
        a2 = a2_ref[...]
        h2_ref[...] = (xn * a2 + s2_ref[...]).astype(BF16)
        dsh_ref[0:1, :] += jnp.sum(dh, axis=0, keepdims=True)
        p_ref[0:1, :] += jnp.sum(dh * xn, axis=0, keepdims=True)
        u = dh * a2
        dx1 = dx2 + r2 * u - xn * (r2 * jnp.mean(u * xn, axis=-1, keepdims=True))
        dx1_ref[...] = dx1
        dcat = _mm_nt((dx1 * g1_ref[...]).astype(BF16), wo[...])
        dya_ref[...] = dcat[:, 0:QW]
        dys_ref[...] = dcat[:, QW:D]

    def tok(w):
        return pl.BlockSpec((tm, w), lambda i: (i, 0))

    hbm = pl.BlockSpec(memory_space=pl.ANY)
    return pl.pallas_call(
        body, name="mlp_bwd", grid=(s // tm,),
        out_shape=[jax.ShapeDtypeStruct((s, D), F32), jax.ShapeDtypeStruct((s, QW), F32),
                   jax.ShapeDtypeStruct((s, SW), F32), jax.ShapeDtypeStruct((s, DFF), BF16),
                   jax.ShapeDtypeStruct((s, 2 * DFF), BF16), jax.ShapeDtypeStruct((s, D), BF16),
                   jax.ShapeDtypeStruct((8, D), F32), jax.ShapeDtypeStruct((8, D), F32)],
        in_specs=[tok(D), tok(2 * DFF), tok(D), hbm, hbm, hbm, hbm, _row((1, D)), _row((1, D)), _row((1, D)), _row((1, D))],
        out_specs=[tok(D), tok(QW), tok(SW), tok(DFF), tok(2 * DFF), tok(D), _row((8, D)), _row((8, D))],
        scratch_shapes=[pltpu.VMEM((D, D), BF16), pltpu.VMEM(w_ga.shape, BF16), pltpu.VMEM(w_gb.shape, BF16),
                        pltpu.VMEM((DFF, D), BF16), pltpu.SemaphoreType.DMA((4,))],
        compiler_params=_cp(("arbitrary",)),
    )(x1, gu, dx2, w_o, w_ga, w_gb, w_dn, gate1, a2, sh2, gate2)


def _wgrad(name, a, b, gate, w, carry=None):
    s, m = a.shape
    n = b.shape[1]
    tk = min(1024, s)
    nk = s // tk

    def body(a_ref, b_ref, g_ref, w_ref, o_hbm, dg_ref, acc_ref, sem):
        k = pl.program_id(0)

        @pl.when(k == 0)
        def _():
            acc_ref[...] = jnp.zeros_like(acc_ref)

        acc_ref[...] += _mm_tn(a_ref[...], b_ref[...].astype(BF16))

        @pl.when(k == nk - 1)
        def _():
            acc = acc_ref[...]
            dg_ref[...] = jnp.zeros_like(dg_ref)
            dg_ref[0:1, :] = jnp.sum(acc * w_ref[...].astype(F32), axis=0, keepdims=True)
            acc_ref[...] = acc * g_ref[...]
            cp = pltpu.make_async_copy(acc_ref, o_hbm, sem)
            cp.start()
            cp.wait()

    return _pcall(body, [a, b, gate, w], name=name, grid=(nk,),
                  out_shape=[jax.ShapeDtypeStruct((m, n), F32), jax.ShapeDtypeStruct((8, n), F32)],
                  in_specs=[pl.BlockSpec((tk, m), lambda k: (k, 0)), pl.BlockSpec((tk, n), lambda k: (k, 0)),
                            _row((1, n)), _row((m, n))],
                  out_specs=[pl.BlockSpec(memory_space=pl.ANY), _row((8, n))],
                  scratch_shapes=[pltpu.VMEM((m, n), F32), pltpu.SemaphoreType.DMA], carry=carry)


def _wgrad_gate_up(h2, dgu, carry=None):
    s = h2.shape[0]
    tk = min(1024, s)
    nk = s // tk
    n = dgu.shape[1]
    nj = n // 4

    def body(a_ref, b_ref, o_hbm, acc_ref, sems):
        k = pl.program_id(0)

        @pl.when(k == 0)
        def _():
            acc_ref[...] = jnp.zeros_like(acc_ref)

        acc_ref[...] += _mm_tn(a_ref[...], b_ref[...])

        @pl.when(k == nk - 1)
        def _():
            cps = [pltpu.make_async_copy(acc_ref.at[:, pl.ds(j * nj, nj)], o_hbm.at[j], sems.at[j]) for j in range(4)]
            for cp in cps:
                cp.start()
            for cp in cps:
                cp.wait()

    return _pcall(body, [h2, dgu], name="wgrad_gate_up", grid=(nk,),
                  out_shape=[jax.ShapeDtypeStruct((4, D, nj), F32)],
                  in_specs=[pl.BlockSpec((tk, D), lambda k: (k, 0)), pl.BlockSpec((tk, n), lambda k: (k, 0))],
                  out_specs=[pl.BlockSpec(memory_space=pl.ANY)],
                  scratch_shapes=[pltpu.VMEM((D, n), F32), pltpu.SemaphoreType.DMA((4,))], carry=carry)


def _wgrad_in_t(h1, pieces, carry=None):
    s = h1.shape[0]
    tk = min(1024, s)
    nk = s // tk

    def body(a_ref, dq_ref, dkv_ref, dz_ref, dxbc_ref, ddt_ref, o_hbm, acc_ref, tr_ref, sem):
        k = pl.program_id(0)

        @pl.when(k == 0)
        def _():
            acc_ref[...] = jnp.zeros_like(acc_ref)

        dproj = jnp.concatenate([dq_ref[...], dkv_ref[...], dz_ref[...], dxbc_ref[...], ddt_ref[...]], axis=1)
        acc_ref[...] += _mm_tn(a_ref[...], dproj)

        @pl.when(k == nk - 1)
        def _():
            for j in range(PROJ_W // 128):
                tr_ref[j * 128:(j + 1) * 128, :] = acc_ref[:, j * 128:(j + 1) * 128].T
            cp = pltpu.make_async_copy(tr_ref, o_hbm, sem)
            cp.start()
            cp.wait()

    return _pcall(body, [h1] + list(pieces), name="wgrad_in", grid=(nk,),
                  out_shape=[jax.ShapeDtypeStruct((PROJ_W, D), F32)],
                  in_specs=[pl.BlockSpec((tk, p.shape[1]), lambda k: (k, 0)) for p in [h1] + list(pieces)],
                  out_specs=[pl.BlockSpec(memory_space=pl.ANY)],
                  scratch_shapes=[pltpu.VMEM((D, PROJ_W), F32), pltpu.VMEM((PROJ_W, D), F32), pltpu.SemaphoreType.DMA],
                  carry=carry)


_SMALL = ["ada_b", "norm1", "conv_w", "conv_b", "dt_bias", "A_log", "D_skip", "sinks", "attn_out_norm",
          "ssm_out_norm", "norm2", "rel_bias", "final_norm"]


def _small_grad(name, gs, chip):
    if name == "ada_b":
        return jnp.concatenate([gs[j:j + 1, :] for j in range(6)], axis=1)
    if name == "conv_w":
        full = gs[7:11, :]
        out = full[:, 0:256]
        for j in range(1, 4):
            out = jnp.where(chip == j, full[:, j * 256:(j + 1) * 256], out)
        return out
    row, width = {"norm1": (6, D), "conv_b": (11, D), "norm2": (12, D), "final_norm": (13, D),
                  "attn_out_norm": (14, QW), "ssm_out_norm": (15, SW), "dt_bias": (16, NH), "A_log": (17, NH),
                  "D_skip": (18, NH), "sinks": (19, NH), "rel_bias": (24, NH)}[name]
    rows = NBUCKET if name == "rel_bias" else 1
    return gs[row:row + rows, 0:width]


def _small_update(small_all, where, ws, ms, vs):
    n = len(_SMALL)

    def body(where_ref, sa_ref, *refs):
        w_refs, m_refs, v_refs, outs = refs[:n], refs[n:2 * n], refs[2 * n:3 * n], refs[3 * n:]
        gs = sa_ref[0]
        for b in range(1, 8):
            gs = gs + sa_ref[b]
        chip = where_ref[1]
        for i, name in enumerate(_SMALL):
            g = _small_grad(name, gs, chip)
            lead = (0,) if name == "conv_w" else ()
            d, mo, vo = _adamw(w_refs[i][lead + (...,)], g, m_refs[i][lead + (...,)], v_refs[i][lead + (...,)])
            for k, val in enumerate((g, d, mo, vo)):
                outs[k * n + i][lead + (...,)] = val
        outs[4 * n][...] = gs[20:21, 0:128]

    shapes = [jax.ShapeDtypeStruct(w.shape, F32) for w in ws]
    vmem = pl.BlockSpec(memory_space=pltpu.VMEM)
    res = pl.pallas_call(
        body, name="small_update", out_shape=shapes * 4 + [jax.ShapeDtypeStruct((1, 128), F32)],
        in_specs=[pl.BlockSpec(memory_space=pltpu.SMEM)] + [vmem] * (1 + 3 * n), out_specs=[vmem] * (4 * n + 1),
    )(where, small_all, *ws, *ms, *vs)
    return [res[k * n:(k + 1) * n] for k in range(4)], res[4 * n][0, 0]


def _add_half(name, g, got, where, by_cols=False):
    rr, cc = got.shape[1:]
    if by_cols:
        mine = pl.BlockSpec((None, rr, cc), lambda i, w_ref: (i, 0, w_ref[0]))
    else:
        mine = pl.BlockSpec((None, None, rr, cc), lambda i, w_ref: (i, w_ref[0], 0, 0))

    def body(w_ref, g_ref, r_ref, o_ref, own_ref):
        s = g_ref[...] + r_ref[...]
        o_ref[...] = s.astype(BF16)

        @pl.when(pl.program_id(0) == w_ref[1])
        def _():
            own_ref[...] = s

    spec = pl.BlockSpec((None, rr, cc), lambda i, w_ref: (i, 0, 0))
    return _pcall(body, [where, g, got], name=name, grid=(4,), nprefetch=1,
                  out_shape=[jax.ShapeDtypeStruct(got.shape, BF16), jax.ShapeDtypeStruct((rr, cc), F32)],
                  in_specs=[mine, spec],
                  out_specs=[spec, pl.BlockSpec((rr, cc), lambda i, w_ref: (0, 0))])


def _add_chips(name, own, got):
    rr, cc = own.shape
    tr = rr // 2 if rr % 32 == 0 else rr

    def body(s_ref, r_ref, o_ref):
        o_ref[...] = ((s_ref[...] + r_ref[0].astype(F32)) + r_ref[1].astype(F32)) + r_ref[2].astype(F32)

    spec = pl.BlockSpec((tr, cc), lambda i: (i, 0))
    return _pcall(body, [own, got], name=name, grid=(rr // tr,), out_shape=[jax.ShapeDtypeStruct((rr, cc), F32)],
                  in_specs=[spec, pl.BlockSpec((3, tr, cc), lambda i: (0, i, 0))], out_specs=[spec])[0]


def _adamw_halves(name, mine, got, w, m, v, where, by_cols=False):
    rr, cc = mine.shape

    def body(w_ref_, t_ref, r_ref, w_ref, m_ref, v_ref, g_ref, d_ref, mo_ref, vo_ref):
        g = jnp.where(pl.program_id(0) == w_ref_[0], t_ref[...], r_ref[...])
        g_ref[...] = g
        d_ref[...], mo_ref[...], vo_ref[...] = _adamw(w_ref[...], g, m_ref[...], v_ref[...])

    if by_cols:
        grid = (2, 1)
        half = pl.BlockSpec((rr, cc), lambda h, i, w_ref_: (0, 0))
        full = pl.BlockSpec((rr, cc), lambda h, i, w_ref_: (0, h))
    else:
        tr = rr // 2
        grid = (2, 2)
        half = pl.BlockSpec((tr, cc), lambda h, i, w_ref_: (i, 0))
        full = pl.BlockSpec((None, tr, cc), lambda h, i, w_ref_: (0, 2 * h + i, 0))
    return _pcall(body, [where, mine, got, w, m, v], name=name, grid=grid, nprefetch=1,
                  out_shape=[jax.ShapeDtypeStruct(w.shape, F32)] * 4,
                  in_specs=[half, half, full, full, full], out_specs=[full] * 4)


def _bias_table(rel_bias, bucket, mask):
    def body(rb_ref, bk_ref, mk_ref, o_ref):
        bk = bk_ref[...]
        valid = mk_ref[...] > 0
        for h in range(NH):
            acc = jnp.zeros((BLK, 2 * BLK), F32)
            for b in range(NBUCKET):
                acc = jnp.where(bk == b, rb_ref[b, h], acc)
            o_ref[h] = jnp.where(valid, acc, NEG)

    vmem = pl.BlockSpec(memory_space=pltpu.VMEM)
    return pl.pallas_call(
        body, name="bias_table", out_shape=jax.ShapeDtypeStruct((NH, BLK, 2 * BLK), F32),
        in_specs=[pl.BlockSpec(memory_space=pltpu.SMEM), vmem, vmem], out_specs=vmem,
    )(rel_bias, bucket, mask)


def _pack_small(dsh1, p1, dsh2, p2, dg1a, dg1b, dg2, norm1, norm2, scale1, scale2, dcw, dcb, dfn,
                dnw_attn, dnw_ssm, dhd, av, dsink, drel, loss_acc):
    def body(dsh1_ref, p1_ref, dsh2_ref, p2_ref, dg1a_ref, dg1b_ref, dg2_ref, n1_ref, n2_ref, s1_ref, s2_ref,
             dcw_ref, dcb_ref, dfn_ref, da_ref, ds_ref, dhd_ref, av_ref, dsink_ref, drel_ref, loss_ref, o_ref):
        o_ref[...] = jnp.zeros_like(o_ref)
        p1v, p2v = p1_ref[0:1, :], p2_ref[0:1, :]
        o_ref[0:1, :] = dsh1_ref[0:1, :]
        o_ref[1:2, :] = p1v * n1_ref[...]
        o_ref[2:3, :] = dg1a_ref[0:1, :] + dg1b_ref[0:1, :]
        o_ref[3:4, :] = dsh2_ref[0:1, :]
        o_ref[4:5, :] = p2v * n2_ref[...]
        o_ref[5:6, :] = dg2_ref[0:1, :]
        o_ref[6:7, :] = p1v * (1.0 + s1_ref[...])
        o_ref[7:11, :] = dcw_ref[0:4, :]
        o_ref[11:12, :] = dcb_ref[0:1, :]
        o_ref[12:13, :] = p2v * (1.0 + s2_ref[...])
        o_ref[13:14, :] = dfn_ref[0:1, :]
        o_ref[14:15, 0:QW] = da_ref[0:1, :]
        o_ref[15:16, 0:SW] = ds_ref[0:1, :]
        o_ref[16:17, 0:128] = dhd_ref[0:1, :]
        o_ref[17:18, 0:128] = dhd_ref[1:2, :] * av_ref[...]
        o_ref[18:19, 0:128] = dhd_ref[2:3, :]
        o_ref[19:20, 0:128] = dsink_ref[0:1, :]
        o_ref[20:21, 0:128] = loss_ref[0:1, :]
        o_ref[24:56, 0:128] = drel_ref[...]

    return pl.pallas_call(body, name="pack_small", out_shape=jax.ShapeDtypeStruct((56, D), F32))(
        dsh1, p1, dsh2, p2, dg1a, dg1b, dg2, norm1, norm2, scale1, scale2, dcw, dcb, dfn,
        dnw_attn, dnw_ssm, dhd, av, dsink, drel, loss_acc)


def _pad_row(a, rows=1):
    return jnp.pad(a.reshape(rows, -1), ((0, 0), (0, D - a.size // rows)))


def kernel(x, c, ada_w, ada_b, norm1, w_in, conv_w, conv_b, dt_bias, A_log, D_skip, sinks, attn_out_norm, ssm_out_norm, w_o, norm2, w_gate_up, w_down, rel_bias, final_norm, loss_target, m_ada_w, m_ada_b, m_norm1, m_w_in, m_conv_w, m_conv_b, m_dt_bias, m_A_log, m_D_skip, m_sinks, m_attn_out_norm, m_ssm_out_norm, m_w_o, m_norm2, m_w_gate_up, m_w_down, m_rel_bias, m_final_norm, v_ada_w, v_ada_b, v_norm1, v_w_in, v_conv_w, v_conv_b, v_dt_bias, v_A_log, v_D_skip, v_sinks, v_attn_out_norm, v_ssm_out_norm, v_w_o, v_norm2, v_w_gate_up, v_w_down, v_rel_bias, v_final_norm):
    xi, yi, ci = lax.axis_index("x"), lax.axis_index("y"), lax.axis_index("c")
    chip = 2 * xi + yi
    me = 4 * xi + 2 * yi + ci
    where = jnp.stack([ci, chip]).astype(jnp.int32)
    xs2, tgt = x[0], loss_target[0]

    first = jnp.concatenate([c, _pad_row(conv_w[0], CK), jnp.zeros((3, D), F32)], axis=0)
    w_in_t, m_w_in_t, v_w_in_t = w_in[0].T, m_w_in[0].T, v_w_in[0].T
    w_in_b, w_o_b, w_dn_b = w_in_t.astype(BF16), w_o[0].astype(BF16), w_down[0].astype(BF16)
    w_gu_b = w_gate_up[0].astype(BF16)
    hw = D // 2
    fetch_half = _Carry(
        [w_in_b], [jax.ShapeDtypeStruct((4,) + w_in_b.shape, BF16)],
        lambda x_, y_, c_: [(None, 0, slice(None), 0, 2 * x_ + y_)] + [
            (f, 0, (slice(None), pl.ds(c_ * hw, hw)), 0, (2 * x_ + y_, slice(None), pl.ds(c_ * hw, hw))) for f in _CHIPS3])

    def swap_halves(x_, y_, c_):
        there = [(jnp.bitwise_xor(2 * x_ + y_, k + 1), slice(None), pl.ds(c_ * hw, hw)) for k in range(3)]
        return [(_SIBLING, 1, at, 1, at) for at in there]

    first_all, w_in_g = _exchange_two("gather_first", _merge(_gather8_carry(first), fetch_half), swap_halves)
    c_all = first_all[:, 0, :]
    cw_full = jnp.concatenate([first_all[2 * j, 1:1 + CK, 0:256] for j in range(4)], axis=1)
    w_in_f = jnp.pad(w_in_g.reshape(IN_W, D), ((0, PROJ_W - IN_W), (0, 0)))

    ncol = ada_w.shape[2]
    mod_cols = _ada_fwd(c_all, ada_w[0], lax.dynamic_slice(ada_b, (0, chip * ncol), (1, ncol)))
    mod_all = _exchange("gather_mod", _gather_chips_carry([mod_cols]))[0]
    mod = lax.dynamic_slice(jnp.transpose(mod_all, (1, 0, 2)).reshape(8, 4 * ncol), (me, 0), (1, 4 * ncol))
    shift1, scale1, gate1, shift2, scale2, gate2 = [mod[:, j * D:(j + 1) * D] for j in range(6)]
    a1 = norm1 * (1.0 + scale1)
    a2 = norm2 * (1.0 + scale2)

    hdn = DFF // 8
    q, kv, z, xbc, dtr, w_o_g, w_dna_g = _in_proj_fwd(xs2, a1, shift1, w_in_f,
                                                      carry=_gather_chips_carry([w_o_b, w_dn_b[0:hdn]]))
    w_o_f = w_o_g.reshape(D, D)
    bucket, mask = _attn_geometry()
    bucket = jnp.asarray(bucket)
    bias = _bias_table(rel_bias, bucket, jnp.asarray(mask.astype(np.int32)))
    sinks1 = sinks[0]
    ya, w_ga_g = _attn_fwd(q, kv, bias, sinks1, attn_out_norm, carry=_gather_chips_carry([w_gu_b[0:D // 2]]))
    cw8 = jnp.concatenate([cw_full, jnp.zeros((4, XBCW), F32)], axis=0)
    dtb = _pad_row(dt_bias)[:, 0:128]
    av = _pad_row(-jnp.exp(A_log))[:, 0:128]
    dk = jnp.repeat(D_skip, HD, axis=1)
    ys, hs, pre, w_gb_g, w_dnb_g = _ssd_fwd(z, xbc, dtr, cw8, conv_b, dtb, av, dk, ssm_out_norm,
                                            carry=_gather_chips_carry([w_gu_b[D // 2:D], w_dn_b[hdn:2 * hdn]]))
    w_dn_f = jnp.stack([w_dna_g, w_dnb_g], axis=1).reshape(DFF, D)
    fn = final_norm[None, :]
    x1, gu, dx2, loss_acc, dfn = _mlp_fwd(xs2, ya, ys, tgt, w_o_f, w_ga_g, w_gb_g, w_dn_f, gate1, a2, shift2, gate2, fn)

    def to_sibling(p):
        return _Carry([p], [jax.ShapeDtypeStruct((4,) + p.shape[2:], F32)],
                      lambda x_, y_, c_: [(_SIBLING, 0, (j, 1 - c_), 0, j) for j in range(4)])

    def to_chips(s4):
        return _Carry([s4], [jax.ShapeDtypeStruct((3,) + s4.shape[1:], s4.dtype)],
                      lambda x_, y_, c_: [(f, 0, jnp.bitwise_xor(2 * x_ + y_, k + 1), 0, k) for k, f in enumerate(_CHIPS3)])

    def back(t):
        return _Carry([t[None]], [jax.ShapeDtypeStruct((1,) + t.shape, F32)], lambda x_, y_, c_: [(_SIBLING, 0, 0, 0, 0)])

    dx1, dya, dys, act, dgu, h2, dsh2, p2 = _mlp_bwd(x1, gu, dx2, w_o_f, w_ga_g, w_gb_g, w_dn_f, gate1, a2, shift2, gate2)
    p_gu = _wgrad_gate_up(h2, dgu)[0].reshape(4, 2, D // 2, 2 * DFF // 4)
    g_dn, dg2, got1_gu = _wgrad("wgrad_down", act, dx2, gate2, w_dn_f, carry=to_sibling(p_gu))
    p_dn = g_dn.reshape(4, 2, DFF // 8, D)
    s4_gu, own_gu = _add_half("rs_add_half_gu", p_gu, got1_gu, where)
    dq, dkv, dbias, dsink, dnw_attn, got2_gu, got1_dn = _attn_bwd(
        q, kv, dya, bias, sinks1, attn_out_norm, carry=_merge(to_chips(s4_gu), to_sibling(p_dn)))
    drel = _rel_bias_grad(dbias, bucket)
    mine_gu = _add_chips("rs_add_chips_gu", own_gu, got2_gu)
    s4_dn, own_dn = _add_half("rs_add_half_dn", p_dn, got1_dn, where)
    dz, dxbc, ddt, dcw, dcb, dnw_ssm, dhd, got2_dn, got3_gu = _ssd_bwd(
        z, xbc, pre, dtr, dys, hs, cw8, dtb, av, dk, ssm_out_norm, carry=_merge(to_chips(s4_dn), back(mine_gu)))
    mine_dn = _add_chips("rs_add_chips_dn", own_dn, got2_dn)
    grad_x, h1, dsh1, p1 = _in_proj_bwd(xs2, dx1, a1, shift1, w_in_f, dq, dkv, dz, dxbc, ddt)
    g_in_t, got3_dn = _wgrad_in_t(h1, [dq, dkv, dz, dxbc, ddt], carry=back(mine_dn))
    p_in = g_in_t[0:IN_W].reshape(4, IN_W // 4, D)

    def to_sibling_cols(p):
        return _Carry([p], [jax.ShapeDtypeStruct(p.shape[:2] + (D // 2,), F32)],
                      lambda x_, y_, c_: [(_SIBLING, 0, (j, slice(None), pl.ds((1 - c_) * (D // 2), D // 2)), 0, j)
                                          for j in range(4)])

    g_oa, dg1a, got1_in = _wgrad("wgrad_o_attn", ya, dx1, gate1, w_o_f[0:QW], carry=to_sibling_cols(p_in))
    s4_in, own_in = _add_half("rs_add_half_in", p_in, got1_in, where, by_cols=True)
    g_os, dg1b, got2_in = _wgrad("wgrad_o_ssm", ys, dx1, gate1, w_o_f[QW:D], carry=to_chips(s4_in))
    mine_in = _add_chips("rs_add_chips_in", own_in, got2_in)
    p_o = jnp.concatenate([g_oa, g_os], axis=0).reshape(4, 2, D // 8, D)

    small = _pack_small(dsh1, p1, dsh2, p2, dg1a, dg1b, dg2, norm1, norm2, scale1, scale2, dcw, dcb, dfn,
                        dnw_attn, dnw_ssm, dhd, av, dsink, drel, loss_acc)
    small_all, got1_o, got3_in = _exchange(
        "gather_small", _merge(_gather8_carry(small), to_sibling(p_o), back(mine_in)))
    s4_o, own_o = _add_half("rs_add_half_o", p_o, got1_o, where)
    mine_o = _add_chips("rs_add_chips_o", own_o, _exchange("rs_chips_o", to_chips(s4_o))[0])
    got3_o = _exchange("rs_back_o", back(mine_o))[0]
    small_res, loss = _small_update(
        small_all, where,
        [ada_b, norm1, conv_w, conv_b, dt_bias, A_log, D_skip, sinks, attn_out_norm, ssm_out_norm, norm2, rel_bias,
         final_norm[None, :]],
        [m_ada_b, m_norm1, m_conv_w, m_conv_b, m_dt_bias, m_A_log, m_D_skip, m_sinks, m_attn_out_norm,
         m_ssm_out_norm, m_norm2, m_rel_bias, m_final_norm[None, :]],
        [v_ada_b, v_norm1, v_conv_w, v_conv_b, v_dt_bias, v_A_log, v_D_skip, v_sinks, v_attn_out_norm,
         v_ssm_out_norm, v_norm2, v_rel_bias, v_final_norm[None, :]])
    small_out = [dict(zip(_SMALL, r)) for r in small_res]
    for r in small_out:
        r["final_norm"] = r["final_norm"][0]

    dmod_all = small_all[:, 0:6, :].reshape(8, 6 * D)
    dmod_loc = lax.dynamic_slice(dmod_all, (0, chip * ncol), (8, ncol))
    ada_out = _ada_bwd_adamw(c_all.T, dmod_loc, ada_w[0], m_ada_w[0], v_ada_w[0])

    big_gu = _adamw_halves("adamw_gate_up", mine_gu, got3_gu[0], w_gate_up, m_w_gate_up, v_w_gate_up, where)
    big_dn = _adamw_halves("adamw_down", mine_dn, got3_dn[0], w_down, m_w_down, v_w_down, where)
    big_o = _adamw_halves("adamw_o", mine_o, got3_o[0], w_o, m_w_o, v_w_o, where)
    big_in = [o.T[None] for o in _adamw_halves("adamw_in", mine_in, got3_in[0], w_in_t, m_w_in_t, v_w_in_t, where,
                                               by_cols=True)]
    big = [big_in, big_o, big_gu, big_dn]

    order = ["ada_w", "ada_b", "norm1", "w_in", "conv_w", "conv_b", "dt_bias", "A_log", "D_skip", "sinks",
             "attn_out_norm", "ssm_out_norm", "w_o", "norm2", "w_gate_up", "w_down", "rel_bias", "final_norm"]
    bigname = {"w_in": 0, "w_o": 1, "w_gate_up": 2, "w_down": 3}
    res = [loss, grad_x[None]]
    for kind in range(4):
        for nm in order:
            if nm == "ada_w":
                res.append(ada_out[kind][None])
            elif nm in bigname:
                res.append(big[bigname[nm]][kind])
            else:
                res.append(small_out[kind][nm])
    return tuple(res)
```

```python
import numpy as np
import jax
import jax.numpy as jnp
from jax import lax
from jax.experimental import pallas as pl
from jax.experimental.pallas import tpu as pltpu

F32, BF16 = jnp.float32, jnp.bfloat16
HI = lax.Precision.HIGHEST

D = 1024
QW, KVW = 512, 128
NH, HD, NKV = 8, 64, 2
SW = 512
NST = 128
XBCW = 1024
CK = 4
BLK = 128
DFF = 2816
IN_W = 2312
PROJ_W = 2432
EPS = 1e-6
NEG = -1e30
NBUCKET = 32

B1, B2, LR, AEPS, WD, STEP = 0.9, 0.999, 0.001, 1e-08, 0.01, 10

VMEM_LIMIT = 56 * 1024 * 1024

_NT = (((1,), (1,)), ((), ()))
_TN = (((0,), (0,)), ((), ()))


def _mm(a, b):
    return jnp.dot(a, b, preferred_element_type=F32)


def _mm_nt(a, b):
    return lax.dot_general(a, b, _NT, preferred_element_type=F32)


def _mm_tn(a, b):
    return lax.dot_general(a, b, _TN, preferred_element_type=F32)


def _mm_hi(a, b):
    return jnp.dot(a, b, preferred_element_type=F32, precision=HI)


def _split3(x):
    hi = x.astype(BF16)
    r = x - hi.astype(F32)
    mid = r.astype(BF16)
    lo = (r - mid.astype(F32)).astype(BF16)
    return hi, mid, lo


def _sel_r(x, e):
    hi, mid, lo = _split3(x)
    return (_mm(hi, e) + _mm(mid, e)) + _mm(lo, e)


def _sel_l(e, x):
    hi, mid, lo = _split3(x)
    return (_mm(e, hi) + _mm(e, mid)) + _mm(e, lo)


def _sig(x):
    return 1.0 / (1.0 + jnp.exp(-x))


def _cp(sem):
    return pltpu.CompilerParams(dimension_semantics=sem, vmem_limit_bytes=VMEM_LIMIT)


def _row(shape):
    nd = len(shape)
    return pl.BlockSpec(shape, lambda *_: (0,) * nd)


def _adamw(w, g, m, v):
    m = B1 * m + (1.0 - B1) * g
    v = B2 * v + (1.0 - B2) * (g * g)
    m_hat = m / (1.0 - B1 ** STEP)
    v_hat = v / (1.0 - B2 ** STEP)
    delta = -LR * (m_hat / (jnp.sqrt(v_hat) + AEPS) + WD * w)
    return delta, m, v


class _Carry:
    def __init__(self, inps, outs, copies):
        self.inps, self.outs, self.copies = list(inps), list(outs), copies
        self.n = len(copies(0, 0, 0))

    def descriptors(self, in_refs, out_refs, send_sems, recv_sems):
        x, y, c = lax.axis_index("x"), lax.axis_index("y"), lax.axis_index("c")
        out = []
        for j, (flip, a, si, o, di) in enumerate(self.copies(x, y, c)):
            if flip is None:
                out.append(pltpu.make_async_copy(in_refs[a].at[si], out_refs[o].at[di], send_sems.at[j]))
            else:
                fx, fy, fc = flip
                peer = (1 - x if fx else x, 1 - y if fy else y, 1 - c if fc else c)
                out.append(pltpu.make_async_remote_copy(
                    src_ref=in_refs[a].at[si], dst_ref=out_refs[o].at[di],
                    send_sem=send_sems.at[j], recv_sem=recv_sems.at[j],
                    device_id=peer, device_id_type=pl.DeviceIdType.MESH))
        return out


def _pcall(body, args, *, name, grid, in_specs, out_specs, out_shape, scratch_shapes=(), sem=None, nprefetch=0,
           carry=None):
    out_shape, out_specs = list(out_shape), list(out_specs)
    in_specs, scratch_shapes = list(in_specs), list(scratch_shapes)
    nin, nout, nscr = len(in_specs), len(out_shape), len(scratch_shapes)
    run = body
    if carry is not None:
        ncin, ncout = len(carry.inps), len(carry.outs)
        hbm = pl.BlockSpec(memory_space=pl.ANY)

        def run(*refs):
            pre, r = refs[:nprefetch], refs[nprefetch:]
            ins, cins = r[:nin], r[nin:nin + ncin]
            r = r[nin + ncin:]
            outs, couts = r[:nout], r[nout:nout + ncout]
            r = r[nout + ncout:]
            scr, (send_sems, recv_sems) = r[:nscr], r[nscr:]
            first = pl.program_id(0) == 0
            last = pl.program_id(0) == grid[0] - 1
            for ax in range(1, len(grid)):
                first = jnp.logical_and(first, pl.program_id(ax) == 0)
                last = jnp.logical_and(last, pl.program_id(ax) == grid[ax] - 1)

            @pl.when(first)
            def _():
                for d in carry.descriptors(cins, couts, send_sems, recv_sems):
                    d.start()

            body(*pre, *ins, *outs, *scr)

            @pl.when(last)
            def _():
                for d in carry.descriptors(cins, couts, send_sems, recv_sems):
                    d.wait()

        in_specs = in_specs + [hbm] * ncin
        out_specs = out_specs + [hbm] * ncout
        out_shape = out_shape + carry.outs
        scratch_shapes = scratch_shapes + [pltpu.SemaphoreType.DMA((carry.n,)), pltpu.SemaphoreType.DMA((carry.n,))]
        args = list(args) + carry.inps
    if sem is None:
        sem = ("arbitrary",) * len(grid)
    if nprefetch:
        kw = dict(grid_spec=pltpu.PrefetchScalarGridSpec(num_scalar_prefetch=nprefetch, grid=grid, in_specs=in_specs,
                                                         out_specs=out_specs, scratch_shapes=scratch_shapes))
    else:
        kw = dict(grid=grid, in_specs=in_specs, out_specs=out_specs, scratch_shapes=scratch_shapes)
    res = pl.pallas_call(run, name=name, out_shape=out_shape, compiler_params=_cp(sem), **kw)(*args)
    return list(res)


def _merge(*carries):
    inps, outs, offs = [], [], []
    for cr in carries:
        offs.append((len(inps), len(outs)))
        inps += cr.inps
        outs += cr.outs

    def copies(x, y, c):
        return [(f, a + io, si, o + oo, di) for cr, (io, oo) in zip(carries, offs) for f, a, si, o, di in cr.copies(x, y, c)]

    return _Carry(inps, outs, copies)


def _exchange(name, carry):
    return _pcall(lambda: None, [], name=name, grid=(1,), in_specs=[], out_specs=[], out_shape=[], carry=carry)


def _exchange_two(name, carry, then):
    second = _Carry([], [], then)
    nin, nout = len(carry.inps), len(carry.outs)

    def body(*refs):
        ins, outs = refs[:nin], refs[nin:nin + nout]
        send_a, recv_a, send_b, recv_b = refs[nin + nout:]
        for descs in (carry.descriptors(ins, outs, send_a, recv_a), second.descriptors(outs, outs, send_b, recv_b)):
            for d in descs:
                d.start()
            for d in descs:
                d.wait()

    hbm = pl.BlockSpec(memory_space=pl.ANY)
    return list(pl.pallas_call(
        body, name=name, out_shape=carry.outs, in_specs=[hbm] * nin, out_specs=[hbm] * nout,
        scratch_shapes=[pltpu.SemaphoreType.DMA((carry.n,)), pltpu.SemaphoreType.DMA((carry.n,)),
                        pltpu.SemaphoreType.DMA((second.n,)), pltpu.SemaphoreType.DMA((second.n,))],
    )(*carry.inps))


_ALL7 = [(f >> 2 & 1, f >> 1 & 1, f & 1) for f in range(1, 8)]
_CHIPS3 = [(0, 1, 0), (1, 0, 0), (1, 1, 0)]
_SIBLING = (0, 0, 1)


def _gather8_carry(blk):
    def copies(x, y, c):
        me = 4 * x + 2 * y + c
        return [(None, 0, 0, 0, me)] + [(f, 0, 0, 0, me) for f in _ALL7]

    return _Carry([blk[None]], [jax.ShapeDtypeStruct((8,) + blk.shape, blk.dtype)], copies)


def _gather_chips_carry(blks):
    def copies(x, y, c):
        chip = 2 * x + y
        return [(f, a, 0, a, chip) for a in range(len(blks)) for f in [None] + _CHIPS3]

    return _Carry([b[None] for b in blks], [jax.ShapeDtypeStruct((4,) + b.shape, b.dtype) for b in blks], copies)


def _ada_fwd(c_all, w_loc, b_loc):
    n = w_loc.shape[1]
    tn = 512

    def body(c_ref, w_ref, b_ref, o_ref):
        cv = c_ref[...]
        cond = cv * _sig(cv)
        o_ref[...] = _mm_hi(cond, w_ref[...]) + b_ref[...]

    return pl.pallas_call(
        body, name="ada_fwd", grid=(n // tn,),
        out_shape=jax.ShapeDtypeStruct((8, n), F32),
        in_specs=[_row((8, D)), pl.BlockSpec((D, tn), lambda j: (0, j)), pl.BlockSpec((1, tn), lambda j: (0, j))],
        out_specs=pl.BlockSpec((8, tn), lambda j: (0, j)),
        compiler_params=_cp(("parallel",)),
    )(c_all, w_loc, b_loc)


def _ada_bwd_adamw(c_all_t, dmod_loc, w, m, v, carry=None):
    n = w.shape[1]
    tn = 512

    def body(ct_ref, dm_ref, w_ref, m_ref, v_ref, g_ref, d_ref, mo_ref, vo_ref):
        ct = ct_ref[...]
        cond = ct * _sig(ct)
        dm = dm_ref[...]
        g = cond[:, 0:1] * dm[0:1, :]
        for b in range(1, 8):
            g = g + cond[:, b:b + 1] * dm[b:b + 1, :]
        g_ref[...] = g
        d_ref[...], mo_ref[...], vo_ref[...] = _adamw(w_ref[...], g, m_ref[...], v_ref[...])

    wspec = pl.BlockSpec((D, tn), lambda j: (0, j))
    return _pcall(
        body, [c_all_t, dmod_loc, w, m, v], name="ada_bwd_adamw", grid=(n // tn,),
        out_shape=[jax.ShapeDtypeStruct((D, n), F32)] * 4,
        in_specs=[_row((D, 8)), pl.BlockSpec((8, tn), lambda j: (0, j)), wspec, wspec, wspec],
        out_specs=[wspec] * 4, carry=carry)


def _in_proj_fwd(x, a1, sh1, w_in, carry=None):
    s = x.shape[0]
    tm = 512

    def body(x_ref, a_ref, s_ref, w_ref, q_ref, kv_ref, z_ref, xbc_ref, dt_ref):
        def norm(rows):
            xv = x_ref[rows, :]
            r = lax.rsqrt(jnp.mean(xv * xv, axis=-1, keepdims=True) + EPS)
            return (xv * r * a_ref[...] + s_ref[...]).astype(BF16)

        def project(rows, h):
            p = _mm_nt(h, w_ref[...])
            q_ref[rows, :] = p[:, 0:512].astype(BF16)
            kv_ref[rows, :] = p[:, 512:768].astype(BF16)
            z_ref[rows, :] = p[:, 768:1280]
            xbc_ref[rows, :] = p[:, 1280:2304]
            dt_ref[rows, :] = p[:, 2304:2432]

        r0, r1 = slice(0, tm // 2), slice(tm // 2, tm)
        h0 = norm(r0)
        project(r0, h0)
        project(r1, norm(r1))

    def tok(w):
        return pl.BlockSpec((tm, w), lambda i: (i, 0))

    return _pcall(
        body, [x, a1, sh1, w_in], name="in_proj_fwd", grid=(s // tm,),
        out_shape=[jax.ShapeDtypeStruct((s, QW), BF16), jax.ShapeDtypeStruct((s, 2 * KVW), BF16),
                   jax.ShapeDtypeStruct((s, SW), F32), jax.ShapeDtypeStruct((s, XBCW), F32),
                   jax.ShapeDtypeStruct((s, 128), F32)],
        in_specs=[tok(D), _row((1, D)), _row((1, D)), _row((PROJ_W, D))],
        out_specs=[tok(QW), tok(2 * KVW), tok(SW), tok(XBCW), tok(128)], carry=carry)


def _in_proj_bwd(x, dx1, a1, sh1, w_in, dq, dkv, dz, dxbc, ddt, carry=None):
    s = x.shape[0]
    tm = 512

    def body(x_ref, dx1_ref, a_ref, s_ref, w_ref, dq_ref, dkv_ref, dz_ref, dxbc_ref, ddt_ref,
             gx_ref, h_ref, dsh_ref, p_ref):
        i = pl.program_id(0)

        @pl.when(i == 0)
        def _():
            dsh_ref[...] = jnp.zeros_like(dsh_ref)
            p_ref[...] = jnp.zeros_like(p_ref)

        def gather(st):
            rows = st["rows"]
            st["dproj"] = jnp.concatenate([dq_ref[rows, :], dkv_ref[rows, :], dz_ref[rows, :], dxbc_ref[rows, :],
                                           ddt_ref[rows, :]], axis=1)

        def back(st):
            st["dh"] = _mm(st.pop("dproj"), w_ref[...])

        def norm(st):
            rows, dh = st["rows"], st.pop("dh")
            xv = x_ref[rows, :]
            r = lax.rsqrt(jnp.mean(xv * xv, axis=-1, keepdims=True) + EPS)
            xn = xv * r
            a = a_ref[...]
            h_ref[rows, :] = (xn * a + s_ref[...]).astype(BF16)
            st["dsh"] = jnp.sum(dh, axis=0, keepdims=True)
            st["p"] = jnp.sum(dh * xn, axis=0, keepdims=True)
            u = dh * a
            gx_ref[rows, :] = dx1_ref[rows, :] + r * u - xn * (r * jnp.mean(u * xn, axis=-1, keepdims=True))

        g0, g1 = [dict(rows=slice(k * (tm // 2), (k + 1) * (tm // 2))) for k in range(2)]
        for stage, st in [(gather, g0), (back, g0), (gather, g1), (norm, g0), (back, g1), (norm, g1)]:
            stage(st)
        dsh_ref[0:1, :] += g0["dsh"] + g1["dsh"]
        p_ref[0:1, :] += g0["p"] + g1["p"]

    def tok(w):
        return pl.BlockSpec((tm, w), lambda i: (i, 0))

    return _pcall(
        body, [x, dx1, a1, sh1, w_in, dq, dkv, dz, dxbc, ddt], name="in_proj_bwd", grid=(s // tm,),
        out_shape=[jax.ShapeDtypeStruct((s, D), F32), jax.ShapeDtypeStruct((s, D), BF16),
                   jax.ShapeDtypeStruct((8, D), F32), jax.ShapeDtypeStruct((8, D), F32)],
        in_specs=[tok(D), tok(D), _row((1, D)), _row((1, D)), _row((PROJ_W, D)),
                  tok(QW), tok(2 * KVW), tok(SW), tok(XBCW), tok(128)],
        out_specs=[tok(D), tok(D), _row((8, D)), _row((8, D))], carry=carry)


def _attn_geometry():
    dist = np.arange(BLK)[:, None] + BLK - np.arange(2 * BLK)[None, :]
    n = np.maximum(dist, 0)
    max_exact = NBUCKET // 2
    large = max_exact + (np.log(np.maximum(n, 1) / max_exact) / np.log(128 / max_exact)
                         * (NBUCKET - max_exact)).astype(np.int32)
    large = np.minimum(large, NBUCKET - 1)
    bucket = np.where(n < max_exact, n, large).astype(np.int32)
    mask = (dist >= 0) & (dist < 128)
    return bucket, mask


def _attn_heads(is_first, q_blk, kvw, bias_ref, sinks_ref):
    qv = q_blk * 0.125
    col = lax.broadcasted_iota(jnp.int32, (BLK, 2 * BLK), 1)
    first = jnp.where(jnp.logical_and(is_first, col < BLK), NEG, 0.0)
    groups = []
    for g in range(NKV):
        qs = jnp.concatenate([qv[:, (4 * g + r) * HD:(4 * g + r + 1) * HD] for r in range(4)], axis=0)
        kw = kvw[:, g * HD:(g + 1) * HD]
        vw = kvw[:, KVW + g * HD:KVW + (g + 1) * HD]
        sc = _mm_nt(qs, kw)
        pn, ps = [], []
        for r in range(4):
            h = 4 * g + r
            sr = sc[r * BLK:(r + 1) * BLK] + bias_ref[h] + first
            sink = sinks_ref[h]
            m = jnp.maximum(jnp.max(sr, axis=-1, keepdims=True), sink)
            p = jnp.exp(sr - m)
            es = jnp.exp(sink - m)
            inv = 1.0 / (jnp.sum(p, axis=-1, keepdims=True) + es)
            pn.append(p * inv)
            ps.append(es * inv)
        pn = jnp.concatenate(pn, axis=0)
        ps = jnp.concatenate(ps, axis=0)
        o = _mm(pn.astype(BF16), vw)
        groups.append((qs, kw, vw, pn, ps, o))
    return groups


def _unstack_heads(parts):
    return jnp.concatenate([p[r * BLK:(r + 1) * BLK] for p in parts for r in range(4)], axis=1)


NB = 4


def _attn_fwd(q, kv, bias, sinks, nw, carry=None):
    s = q.shape[0]

    def body(q_ref, kvp_ref, kvc_ref, bias_ref, sinks_ref, nw_ref, y_ref):
        t = pl.program_id(0)
        kv3 = jnp.concatenate([kvp_ref[...], kvc_ref[...]], axis=0)
        for sub in range(NB):
            rows = slice(sub * BLK, (sub + 1) * BLK)
            groups = _attn_heads(jnp.logical_and(t == 0, sub == 0), q_ref[rows, :], kv3[sub * BLK:(sub + 2) * BLK],
                                 bias_ref, sinks_ref)
            o = _unstack_heads([g[5] for g in groups])
            r = lax.rsqrt(jnp.mean(o * o, axis=-1, keepdims=True) + EPS)
            y_ref[rows, :] = (o * r * nw_ref[...]).astype(BF16)

    return _pcall(
        body, [q, kv, kv, bias, sinks, nw], name="attn_fwd", grid=(s // (NB * BLK),),
        out_shape=[jax.ShapeDtypeStruct((s, QW), BF16)],
        in_specs=[pl.BlockSpec((NB * BLK, QW), lambda t: (t, 0)),
                  pl.BlockSpec((BLK, 2 * KVW), lambda t: (jnp.maximum(NB * t - 1, 0), 0)),
                  pl.BlockSpec((NB * BLK, 2 * KVW), lambda t: (t, 0)),
                  _row((NH, BLK, 2 * BLK)),
                  pl.BlockSpec(memory_space=pltpu.SMEM),
                  _row((1, QW))],
        out_specs=[pl.BlockSpec((NB * BLK, QW), lambda t: (t, 0))], carry=carry)


def _attn_bwd(q, kv, dya, bias, sinks, nw, carry=None):
    s = q.shape[0]
    nt = s // (NB * BLK)

    def body(q_ref, kvp_ref, kvc_ref, dy_ref, bias_ref, sinks_ref, nw_ref,
             dq_ref, dkv_ref, dbias_ref, dsink_ref, dnw_ref, carry_ref, held_ref):
        t = pl.program_id(0)

        @pl.when(t == 0)
        def _():
            carry_ref[...] = jnp.zeros_like(carry_ref)
            held_ref[...] = jnp.zeros_like(held_ref)
            dbias_ref[...] = jnp.zeros_like(dbias_ref)
            dsink_ref[...] = jnp.zeros_like(dsink_ref)
            dnw_ref[...] = jnp.zeros_like(dnw_ref)

        def block(sub, kv3):
            rows = slice(sub * BLK, (sub + 1) * BLK)
            groups = _attn_heads(jnp.logical_and(t == 0, sub == 0), q_ref[rows, :], kv3[sub * BLK:(sub + 2) * BLK],
                                 bias_ref, sinks_ref)
            o = _unstack_heads([g[5] for g in groups])
            r = lax.rsqrt(jnp.mean(o * o, axis=-1, keepdims=True) + EPS)
            dy = dy_ref[rows, :]
            on = o * r
            dnw_ref[0:1, :] += jnp.sum(dy * on, axis=0, keepdims=True)
            u = dy * nw_ref[...]
            do = r * u - on * (r * jnp.mean(u * on, axis=-1, keepdims=True))
            dq_parts, dk_parts, dv_parts = [], [], []
            for g, (qs, kw, vw, pn, ps, og) in enumerate(groups):
                dos = jnp.concatenate([do[:, (4 * g + r_) * HD:(4 * g + r_ + 1) * HD] for r_ in range(4)], axis=0)
                delta = jnp.sum(dos * og, axis=-1, keepdims=True)
                dp = _mm_nt(dos.astype(BF16), vw)
                ds = pn * (dp - delta)
                dsk = ps * delta
                lane = lax.broadcasted_iota(jnp.int32, (1, 128), 1)
                for r_ in range(4):
                    h = 4 * g + r_
                    dbias_ref[h] += ds[r_ * BLK:(r_ + 1) * BLK]
                    dsink_ref[0:1, :] -= jnp.where(lane == h, jnp.sum(dsk[r_ * BLK:(r_ + 1) * BLK]), 0.0)
                dsb = ds.astype(BF16)
                dq_parts.append(_mm(dsb, kw) * 0.125)
                dk_parts.append(_mm_tn(dsb, qs))
                dv_parts.append(_mm_tn(pn.astype(BF16), dos.astype(BF16)))
            dq_ref[rows, :] = _unstack_heads(dq_parts).astype(BF16)
            return jnp.concatenate(dk_parts + dv_parts, axis=1)

        @pl.when(t < nt)
        def _():
            kv3 = jnp.concatenate([kvp_ref[...], kvc_ref[...]], axis=0)
            tail = carry_ref[...]
            for sub in range(NB):
                d = block(sub, kv3)
                done = tail + d[0:BLK]
                if sub == 0:
                    dkv_ref[0:(NB - 1) * BLK, :] = held_ref[...].astype(BF16)
                    dkv_ref[(NB - 1) * BLK:NB * BLK, :] = done.astype(BF16)
                else:
                    held_ref[(sub - 1) * BLK:sub * BLK, :] = done
                tail = d[BLK:2 * BLK]
            carry_ref[...] = tail

        @pl.when(t == nt)
        def _():
            dkv_ref[0:(NB - 1) * BLK, :] = held_ref[...].astype(BF16)
            dkv_ref[(NB - 1) * BLK:NB * BLK, :] = carry_ref[...].astype(BF16)

    last = nt - 1
    tile = lambda w: pl.BlockSpec((NB * BLK, w), lambda t: (jnp.minimum(t, last), 0))
    return _pcall(
        body, [q, kv, kv, dya, bias, sinks, nw], name="attn_bwd", grid=(nt + 1,),
        out_shape=[jax.ShapeDtypeStruct((s, QW), BF16), jax.ShapeDtypeStruct((s, 2 * KVW), BF16),
                   jax.ShapeDtypeStruct((NH, BLK, 2 * BLK), F32), jax.ShapeDtypeStruct((NH, 128), F32),
                   jax.ShapeDtypeStruct((8, QW), F32)],
        in_specs=[tile(QW),
                  pl.BlockSpec((BLK, 2 * KVW), lambda t: (jnp.clip(NB * t - 1, 0, NB * nt - 1), 0)),
                  tile(2 * KVW), tile(QW),
                  _row((NH, BLK, 2 * BLK)),
                  pl.BlockSpec(memory_space=pltpu.SMEM),
                  _row((1, QW))],
        out_specs=[tile(QW),
                   pl.BlockSpec((NB * BLK, 2 * KVW), lambda t: (jnp.maximum(t - 1, 0), 0)),
                   _row((NH, BLK, 2 * BLK)), _row((NH, 128)), _row((8, QW))],
        scratch_shapes=[pltpu.VMEM((BLK, 2 * KVW), F32), pltpu.VMEM(((NB - 1) * BLK, 2 * KVW), F32)], carry=carry)


def _rel_bias_grad(dbias, bucket):
    def body(db_ref, bk_ref, o_ref):
        bk = bk_ref[...]
        lane = lax.broadcasted_iota(jnp.int32, (1, 128), 1)
        for b in range(NBUCKET):
            sel = bk == b
            row = jnp.zeros((1, 128), F32)
            for h in range(NH):
                row = row + jnp.where(lane == h, jnp.sum(jnp.where(sel, db_ref[h], 0.0)), 0.0)
            o_ref[b:b + 1, :] = row

    return pl.pallas_call(
        body, name="rel_bias_grad",
        out_shape=jax.ShapeDtypeStruct((NBUCKET, 128), F32),
    )(dbias, bucket)


def _ssd_consts():
    head_of_lane = np.arange(SW) // HD
    expand = (np.arange(128)[:, None] == head_of_lane[None, :]).astype(np.float32)
    tril = np.tril(np.ones((BLK, BLK), np.float32))
    return (jnp.asarray(expand, BF16), jnp.asarray(expand.T.copy(), BF16), jnp.asarray(tril, BF16),
            jnp.asarray(tril.T.copy(), BF16))


def _conv_pre(xc, halo, cw, cb):
    ext = jnp.concatenate([halo, xc], axis=0)
    taps = [xc if k == CK - 1 else pltpu.roll(ext, CK - 1 - k, 0)[8:8 + BLK] for k in range(CK)]
    return cb + sum(cw[k:k + 1, :] * taps[k] for k in range(CK))


def _ssd_chunk(pre, dtr, dtb, av, dkv, ex, tril, h_in):
    sp = _sig(pre)
    xbc = pre * sp
    xs, bm, cm = xbc[:, 0:SW], xbc[:, SW:SW + 2 * NST], xbc[:, SW + 2 * NST:]
    dtin = dtr + dtb
    dt = jnp.maximum(dtin, 0.0) + jnp.log1p(jnp.exp(-jnp.abs(dtin)))
    cs = _sel_l(tril, dt * av)
    cst = cs.T
    dtx = _sel_r(dt, ex)
    csx = _sel_r(cs, ex)
    xdt = xs * dtx
    csl = csx[BLK - 1:BLK, :]
    decx = jnp.exp(csl - csx)
    ecsx = jnp.exp(csx)
    ecl = jnp.exp(csl)
    causal = tril.astype(F32) > 0.5
    ydiag, yoff, cbs, lms = [], [], [], []
    for g in range(2):
        bg = bm[:, g * NST:(g + 1) * NST].astype(BF16)
        cg = cm[:, g * NST:(g + 1) * NST].astype(BF16)
        cb = _mm_nt(cg, bg)
        cbs.append(cb)
        yoff.append(_mm(cg, h_in[:, g * 256:(g + 1) * 256].astype(BF16)))
        for r in range(4):
            h = 4 * g + r
            seg = cs[:, h:h + 1] - cst[h:h + 1, :]
            lm = jnp.where(causal, jnp.exp(jnp.minimum(seg, 0.0)), 0.0)
            lms.append(lm)
            ydiag.append(_mm((cb * lm).astype(BF16), xdt[:, h * HD:(h + 1) * HD].astype(BF16)))
    yoff = jnp.concatenate(yoff, axis=1) * ecsx
    y = jnp.concatenate(ydiag, axis=1) + yoff + dkv * xs
    return dict(pre=pre, sp=sp, xs=xs, bm=bm, cm=cm, dtin=dtin, dt=dt, av=av, cs=cs, cst=cst,
                dtx=dtx, csx=csx, xdt=xdt, decx=decx, ecsx=ecsx, ecl=ecl, causal=causal, cbs=cbs, lms=lms,
                yoff=yoff, y=y)


def _group_mean(t):
    m0 = jnp.mean(t[:, 0:256], axis=-1, keepdims=True)
    m1 = jnp.mean(t[:, 256:512], axis=-1, keepdims=True)
    return jnp.concatenate([jnp.broadcast_to(m0, (t.shape[0], 256)), jnp.broadcast_to(m1, (t.shape[0], 256))], axis=1)


SUBS = 4


def _ssd_fwd(z, xbc, dtr, cw, cb, dtb, av, dk, nw, carry=None):
    s = z.shape[0]
    nc = s // BLK
    tile = SUBS * BLK
    ex, _, tril, _ = _ssd_consts()

    def body(z_ref, xc_ref, xh_ref, dtr_ref, cw_ref, cb_ref, dtb_ref, a_ref, dk_ref, nw_ref, ex_ref, tril_ref,
             y_ref, hs_ref, pre_ref, h_ref):
        t = pl.program_id(0)

        @pl.when(t == 0)
        def _():
            h_ref[...] = jnp.zeros_like(h_ref)

        h_in = h_ref[...]
        for sub in range(SUBS):
            rows = slice(sub * BLK, (sub + 1) * BLK)
            xc = xc_ref[rows, :]
            halo = jnp.where(t == 0, 0.0, xh_ref[...]) if sub == 0 else xc_ref[sub * BLK - 8:sub * BLK, :]
            pre = _conv_pre(xc, halo, cw_ref[...], cb_ref[...])
            pre_ref[rows, :] = pre
            hs_ref[sub] = h_in
            f = _ssd_chunk(pre, dtr_ref[rows, :], dtb_ref[...], a_ref[...], dk_ref[...], ex_ref[...], tril_ref[...], h_in)
            dx = (f["decx"] * f["xdt"]).astype(BF16)
            st = [_mm_tn(f["bm"][:, g * NST:(g + 1) * NST].astype(BF16), dx[:, g * 256:(g + 1) * 256]) for g in range(2)]
            h_in = h_in * f["ecl"] + jnp.concatenate(st, axis=1)
            zv = z_ref[rows, :]
            tg = f["y"] * (zv * _sig(zv))
            r = lax.rsqrt(_group_mean(tg * tg) + EPS)
            y_ref[rows, :] = (tg * r * nw_ref[...]).astype(BF16)
        h_ref[...] = h_in

    cur = lambda w: pl.BlockSpec((tile, w), lambda t: (t, 0))
    return _pcall(
        body, [z, xbc, xbc, dtr, cw, cb, dtb, av, dk, nw, ex, tril], name="ssd_fwd", grid=(s // tile,),
        out_shape=[jax.ShapeDtypeStruct((s, SW), BF16), jax.ShapeDtypeStruct((nc, NST, SW), F32),
                   jax.ShapeDtypeStruct((s, XBCW), F32)],
        in_specs=[cur(SW), cur(XBCW), pl.BlockSpec((8, XBCW), lambda t: (jnp.maximum(t * (tile // 8) - 1, 0), 0)),
                  cur(128), _row((8, XBCW)), _row((1, XBCW)), _row((1, 128)),
                  _row((1, 128)), _row((1, SW)), _row((1, SW)), _row((128, SW)), _row((BLK, BLK))],
        out_specs=[cur(SW), pl.BlockSpec((SUBS, NST, SW), lambda t: (t, 0, 0)), cur(XBCW)],
        scratch_shapes=[pltpu.VMEM((NST, SW), F32)], carry=carry)


def _ssd_bwd(z, xbc, pre_all, dtr, dys, hs, cw, dtb, av, dk, nw, carry=None):
    s = z.shape[0]
    tile = SUBS * BLK
    nt = s // tile
    ex, ext_t, tril, triu = _ssd_consts()

    def body(z_ref, xc_ref, pre_ref, dtr_ref, dy_ref, hs_ref, cw_ref, dtb_ref, a_ref, dk_ref, nw_ref,
             ex_ref, ext_ref, tril_ref, triu_ref,
             dz_ref, dxbc_ref, ddt_ref, dcw_ref, dcb_ref, dnw_ref, dhd_ref, dh_ref, nxt_ref, dd_ref):
        i = pl.program_id(0)

        @pl.when(i == 0)
        def _():
            dh_ref[...] = jnp.zeros_like(dh_ref)
            nxt_ref[...] = jnp.zeros_like(nxt_ref)
            dd_ref[...] = jnp.zeros_like(dd_ref)
            dcw_ref[...] = jnp.zeros_like(dcw_ref)
            dcb_ref[...] = jnp.zeros_like(dcb_ref)
            dnw_ref[...] = jnp.zeros_like(dnw_ref)
            dhd_ref[...] = jnp.zeros_like(dhd_ref)

        gst, nxt = dh_ref[...], nxt_ref[...]
        for sub in reversed(range(SUBS)):
            rows = slice(sub * BLK, (sub + 1) * BLK)
            gst, nxt = chunk(sub, rows, gst, nxt, z_ref, xc_ref, pre_ref, dtr_ref, dy_ref, hs_ref, cw_ref, dtb_ref,
                             a_ref, dk_ref, nw_ref, ex_ref, ext_ref, tril_ref, triu_ref,
                             dz_ref, dxbc_ref, ddt_ref, dcw_ref, dcb_ref, dnw_ref, dhd_ref, dd_ref)
        dh_ref[...] = gst
        nxt_ref[...] = nxt

        @pl.when(i == nt - 1)
        def _():
            dhd_ref[2:3, :] = _sel_r(dd_ref[...], ext_ref[...])[0:1, :]

    def chunk(sub, rows, gst, nxt, z_ref, xc_ref, pre_ref, dtr_ref, dy_ref, hs_ref, cw_ref, dtb_ref,
              a_ref, dk_ref, nw_ref, ex_ref, ext_ref, tril_ref, triu_ref,
              dz_ref, dxbc_ref, ddt_ref, dcw_ref, dcb_ref, dnw_ref, dhd_ref, dd_ref):
        h_in = hs_ref[sub]
        f = _ssd_chunk(pre_ref[rows, :], dtr_ref[rows, :], dtb_ref[...], a_ref[...], dk_ref[...], ex_ref[...],
                       tril_ref[...], h_in)
        xs, xdt, decx, ecsx, ecl, dtx = f["xs"], f["xdt"], f["decx"], f["ecsx"], f["ecl"], f["dtx"]
        cs, cst, causal = f["cs"], f["cst"], f["causal"]
        causal_t = triu_ref[...].astype(F32) > 0.5

        zv = z_ref[rows, :]
        sz = _sig(zv)
        gz = zv * sz
        t = f["y"] * gz
        r = lax.rsqrt(_group_mean(t * t) + EPS)
        tn_ = t * r
        dyn = dy_ref[rows, :]
        dnw_ref[0:1, :] += jnp.sum(dyn * tn_, axis=0, keepdims=True)
        u = dyn * nw_ref[...]
        dt_ = r * u - tn_ * (r * _group_mean(u * tn_))
        dy = dt_ * gz
        dz_ref[rows, :] = (dt_ * f["y"] * (sz * (1.0 + zv * (1.0 - sz)))).astype(BF16)

        dd_ref[0:1, :] += jnp.sum(dy * xs, axis=0, keepdims=True)
        dxs = dk_ref[...] * dy

        edy = ecsx * dy
        dxdt, dbs, dcs_, dcsx_parts, dh_new = [], [], [], [], []
        lane = lax.broadcasted_iota(jnp.int32, (1, 128), 1)
        dcs_intra = jnp.zeros((BLK, 128), F32)
        for g in range(2):
            sl = slice(g * 256, (g + 1) * 256)
            bgf, cgf = f["bm"][:, g * NST:(g + 1) * NST], f["cm"][:, g * NST:(g + 1) * NST]
            bg, cg = bgf.astype(BF16), cgf.astype(BF16)
            gg = gst[:, sl].astype(BF16)
            hg = h_in[:, sl].astype(BF16)
            edyg = edy[:, sl].astype(BF16)
            dc = _mm_nt(edyg, hg)
            dh_new.append(gst[:, sl] * ecl[:, sl] + _mm_tn(cg, edyg))
            bgm = _mm(bg, gg)
            dxdt_g = decx[:, sl] * bgm
            dxg = (decx[:, sl] * xdt[:, sl]).astype(BF16)
            db = _mm_nt(dxg, gg)
            qd = bgm * xdt[:, sl] * decx[:, sl]
            last = jnp.sum(qd, axis=0, keepdims=True) + ecl[:, sl] * jnp.sum(gst[:, sl] * h_in[:, sl], axis=0, keepdims=True)
            rowid = lax.broadcasted_iota(jnp.int32, (BLK, 256), 0)
            dcsx_parts.append(f["yoff"][:, sl] * dy[:, sl] - qd + jnp.where(rowid == BLK - 1, last, 0.0))
            cb_ = f["cbs"][g]
            cbt = _mm_nt(bg, cg)
            dcb_ = jnp.zeros((BLK, BLK), F32)
            dcbt = jnp.zeros((BLK, BLK), F32)
            dxd = []
            for r_ in range(4):
                h = 4 * g + r_
                hl = slice(h * HD, (h + 1) * HD)
                lm = f["lms"][h]
                segt = cst[h:h + 1, :] - cs[:, h:h + 1]
                lmt = jnp.where(causal_t, jnp.exp(jnp.minimum(segt, 0.0)), 0.0)
                dyh = dy[:, hl].astype(BF16)
                xdh = xdt[:, hl].astype(BF16)
                dw = _mm_nt(dyh, xdh)
                dwt = _mm_nt(xdh, dyh)
                wt = cbt * lmt
                dxd.append(_mm(wt.astype(BF16), dyh))
                dcb_ = dcb_ + dw * lm
                dcbt = dcbt + dwt * lmt
                col = jnp.sum(dw * (cb_ * lm), axis=-1, keepdims=True) - jnp.sum(dwt * wt, axis=-1, keepdims=True)
                dcs_intra = dcs_intra + jnp.where(lane == h, col, 0.0)
            dxdt.append(dxdt_g + jnp.concatenate(dxd, axis=1))
            dcs_.append(dc + _mm(dcb_.astype(BF16), bg))
            dbs.append(db + _mm(dcbt.astype(BF16), cg))
        dxdt = jnp.concatenate(dxdt, axis=1)
        dxs = dxs + dxdt * dtx
        ext_t_ = ext_ref[...]
        dcs = dcs_intra + _sel_r(jnp.concatenate(dcsx_parts, axis=1), ext_t_)
        da = _sel_l(triu_ref[...], dcs)
        ddt = da * f["av"] + _sel_r(dxdt * xs, ext_t_)
        dhd_ref[1:2, :] += jnp.sum(da * f["dt"], axis=0, keepdims=True)
        ddtr = ddt * _sig(f["dtin"])
        dhd_ref[0:1, :] += jnp.sum(ddtr, axis=0, keepdims=True)
        ddt_ref[rows, :] = ddtr.astype(BF16)

        sp, pre = f["sp"], f["pre"]
        dact = jnp.concatenate([dxs] + dbs + dcs_, axis=1)
        dpre = dact * (sp * (1.0 + pre * (1.0 - sp)))
        dcb_ref[0:1, :] += jnp.sum(dpre, axis=0, keepdims=True)
        ext2 = jnp.concatenate([dpre, nxt], axis=0)
        shifted = [pltpu.roll(ext2, BLK + 8 - (CK - 1 - k), 0)[0:BLK] for k in range(CK - 1)] + [dpre]
        cw = cw_ref[...]
        xc = xc_ref[rows, :]
        dxr = cw[CK - 1:CK, :] * dpre
        for k in range(CK):
            dcw_ref[k:k + 1, :] += jnp.sum(shifted[k] * xc, axis=0, keepdims=True)
            if k < CK - 1:
                dxr = dxr + cw[k:k + 1, :] * shifted[k]
        dxbc_ref[rows, :] = dxr.astype(BF16)
        return jnp.concatenate(dh_new, axis=1), dpre[0:8]

    cur = lambda w: pl.BlockSpec((tile, w), lambda i: (nt - 1 - i, 0))
    return _pcall(
        body, [z, xbc, pre_all, dtr, dys, hs, cw, dtb, av, dk, nw, ex, ext_t, tril, triu], name="ssd_bwd", grid=(nt,),
        out_shape=[jax.ShapeDtypeStruct((s, SW), BF16), jax.ShapeDtypeStruct((s, XBCW), BF16),
                   jax.ShapeDtypeStruct((s, 128), BF16), jax.ShapeDtypeStruct((8, XBCW), F32),
                   jax.ShapeDtypeStruct((8, XBCW), F32), jax.ShapeDtypeStruct((8, SW), F32),
                   jax.ShapeDtypeStruct((8, 128), F32)],
        in_specs=[cur(SW), cur(XBCW), cur(XBCW), cur(128), cur(SW),
                  pl.BlockSpec((SUBS, NST, SW), lambda i: (nt - 1 - i, 0, 0)),
                  _row((8, XBCW)), _row((1, 128)), _row((1, 128)), _row((1, SW)), _row((1, SW)),
                  _row((128, SW)), _row((SW, 128)), _row((BLK, BLK)), _row((BLK, BLK))],
        out_specs=[cur(SW), cur(XBCW), cur(128), _row((8, XBCW)), _row((8, XBCW)), _row((8, SW)), _row((8, 128))],
        scratch_shapes=[pltpu.VMEM((NST, SW), F32), pltpu.VMEM((8, XBCW), F32), pltpu.VMEM((8, SW), F32)], carry=carry)


def _load_once(i, pairs, sem):
    @pl.when(i == 0)
    def _():
        cps = [pltpu.make_async_copy(src, dst, sem.at[k]) for k, (src, dst) in enumerate(pairs)]
        for cp in cps:
            cp.start()
        for cp in cps:
            cp.wait()


def _mlp_fwd(x, ya, ys, tgt, w_o, w_ga, w_gb, w_dn, gate1, a2, sh2, gate2, fn):
    s = x.shape[0]
    sub_m, subs = 256, 2
    tm = sub_m * subs

    def body(x_ref, ya_ref, ys_ref, t_ref, wo_hbm, wga_hbm, wgb_hbm, wdn_hbm, g1_ref, a2_ref, s2_ref, g2_ref, fn_ref,
             x1_ref, gu_ref, dx2_ref, loss_ref, dfn_ref, wo, wga, wgb, wdn, sem):
        i = pl.program_id(0)
        _load_once(i, [(wo_hbm, wo), (wga_hbm, wga), (wgb_hbm, wgb), (wdn_hbm, wdn)], sem)

        @pl.when(i == 0)
        def _():
            loss_ref[...] = jnp.zeros_like(loss_ref)
            dfn_ref[...] = jnp.zeros_like(dfn_ref)

        def proj(st):
            st["mix"] = _mm(ya_ref[st["rows"], :], wo[0:QW, :]) + _mm(ys_ref[st["rows"], :], wo[QW:D, :])

        def norm(st):
            x1 = x_ref[st["rows"], :] + g1_ref[...] * st.pop("mix")
            x1_ref[st["rows"], :] = x1
            r2 = lax.rsqrt(jnp.mean(x1 * x1, axis=-1, keepdims=True) + EPS)
            st["x1"] = x1
            st["h2"] = (x1 * r2 * a2_ref[...] + s2_ref[...]).astype(BF16)

        def gate_up(st):
            h2 = st.pop("h2")
            ha, hb = h2[:, 0:D // 2], h2[:, D // 2:D]
            gub = jnp.concatenate([(_mm(ha, wga[j]) + _mm(hb, wgb[j])).astype(BF16) for j in range(4)], axis=1)
            gu_ref[st["rows"], :] = gub
            st["gub"] = gub

        def activate(st):
            gub = st.pop("gub")
            gv, uv = gub[:, 0:DFF].astype(F32), gub[:, DFF:].astype(F32)
            st["act"] = (gv * _sig(gv) * uv).astype(BF16)

        def down(st):
            st["ff"] = _mm(st.pop("act"), wdn[...])

        def head(st):
            x2 = st.pop("x1") + g2_ref[...] * st.pop("ff")
            r3 = lax.rsqrt(jnp.mean(x2 * x2, axis=-1, keepdims=True) + EPS)
            xn = x2 * r3
            fnv = fn_ref[...]
            err = xn * fnv - t_ref[st["rows"], :]
            st["loss"] = jnp.sum(err * err) * (0.5 / D)
            dy = err * (1.0 / D)
            st["dfn"] = jnp.sum(dy * xn, axis=0, keepdims=True)
            u = dy * fnv
            dx2_ref[st["rows"], :] = r3 * u - xn * (r3 * jnp.mean(u * xn, axis=-1, keepdims=True))

        a, b = [dict(rows=slice(k * sub_m, (k + 1) * sub_m)) for k in range(subs)]
        for stage, st in [(proj, a), (norm, a), (proj, b), (gate_up, a), (norm, b), (activate, a), (gate_up, b),
                          (down, a), (activate, b), (head, a), (down, b), (head, b)]:
            stage(st)
        loss_ref[...] += a["loss"] + b["loss"]
        dfn_ref[0:1, :] += a["dfn"] + b["dfn"]

    def tok(w):
        return pl.BlockSpec((tm, w), lambda i: (i, 0))

    hbm = pl.BlockSpec(memory_space=pl.ANY)
    return pl.pallas_call(
        body, name="mlp_fwd", grid=(s // tm,),
        out_shape=[jax.ShapeDtypeStruct((s, D), F32), jax.ShapeDtypeStruct((s, 2 * DFF), BF16),
                   jax.ShapeDtypeStruct((s, D), F32), jax.ShapeDtypeStruct((8, 128), F32),
                   jax.ShapeDtypeStruct((8, D), F32)],
        in_specs=[tok(D), tok(QW), tok(SW), tok(D), hbm, hbm, hbm, hbm,
                  _row((1, D)), _row((1, D)), _row((1, D)), _row((1, D)), _row((1, D))],
        out_specs=[tok(D), tok(2 * DFF), tok(D), _row((8, 128)), _row((8, D))],
        scratch_shapes=[pltpu.VMEM((D, D), BF16), pltpu.VMEM(w_ga.shape, BF16), pltpu.VMEM(w_gb.shape, BF16),
                        pltpu.VMEM((DFF, D), BF16), pltpu.SemaphoreType.DMA((4,))],
        compiler_params=_cp(("arbitrary",)),
    )(x, ya, ys, tgt, w_o, w_ga, w_gb, w_dn, gate1, a2, sh2, gate2, fn)


def _mlp_bwd(x1, gu, dx2, w_o, w_ga, w_gb, w_dn, gate1, a2, sh2, gate2):
    s = x1.shape[0]
    tm = 256
    nj = 2 * DFF // 4

    def body(x1_ref, gu_ref, dx2_ref, wo_hbm, wga_hbm, wgb_hbm, wdn_hbm, g1_ref, a2_ref, s2_ref, g2_ref,
             dx1_ref, dya_ref, dys_ref, act_ref, dgu_ref, h2_ref, dsh_ref, p_ref, wo, wga, wgb, wdn, sem):
        i = pl.program_id(0)
        _load_once(i, [(wo_hbm, wo), (wga_hbm, wga), (wgb_hbm, wgb), (wdn_hbm, wdn)], sem)

        @pl.when(i == 0)
        def _():
            dsh_ref[...] = jnp.zeros_like(dsh_ref)
            p_ref[...] = jnp.zeros_like(p_ref)

        dx2 = dx2_ref[...]
        dact = _mm_nt((dx2 * g2_ref[...]).astype(BF16), wdn[...])
        gub = gu_ref[...]
        gv, uv = gub[:, 0:DFF].astype(F32), gub[:, DFF:].astype(F32)
        sg = _sig(gv)
        sl = gv * sg
        act_ref[...] = (sl * uv).astype(BF16)
        dgu = jnp.concatenate([dact * uv * (sg * (1.0 + gv * (1.0 - sg))), dact * sl], axis=1).astype(BF16)
        dgu_ref[...] = dgu
        dha = sum(_mm_nt(dgu[:, j * nj:(j + 1) * nj], wga[j]) for j in range(4))
        dhb = sum(_mm_nt(dgu[:, j * nj:(j + 1) * nj], wgb[j]) for j in range(4))
        dh = jnp.concatenate([dha, dhb], axis=1)
        x1 = x1_ref[...]
        r2 = lax.rsqrt(jnp.mean(x1 * x1, axis=-1, keepdims=True) + EPS)
        xn = x1 * r2
        a2 = a2_ref[...]
        h2_ref[...] = (xn * a2 + s2_ref[...]).astype(BF16)
        dsh_ref[0:1, :] += jnp.sum(dh, axis=0, keepdims=True)
        p_ref[0:1, :] += jnp.sum(dh * xn, axis=0, keepdims=True)
        u = dh * a2
        dx1 = dx2 + r2 * u - xn * (r2 * jnp.mean(u * xn, axis=-1, keepdims=True))
        dx1_ref[...] = dx1
        dcat = _mm_nt((dx1 * g1_ref[...]).astype(BF16), wo[...])
        dya_ref[...] = dcat[:, 0:QW]
        dys_ref[...] = dcat[:, QW:D]

    def tok(w):
        return pl.BlockSpec((tm, w), lambda i: (i, 0))

    hbm = pl.BlockSpec(memory_space=pl.ANY)
    return pl.pallas_call(
        body, name="mlp_bwd", grid=(s // tm,),
        out_shape=[jax.ShapeDtypeStruct((s, D), F32), jax.ShapeDtypeStruct((s, QW), F32),
                   jax.ShapeDtypeStruct((s, SW), F32), jax.ShapeDtypeStruct((s, DFF), BF16),
                   jax.ShapeDtypeStruct((s, 2 * DFF), BF16), jax.ShapeDtypeStruct((s, D), BF16),
                   jax.ShapeDtypeStruct((8, D), F32), jax.ShapeDtypeStruct((8, D), F32)],
        in_specs=[tok(D), tok(2 * DFF), tok(D), hbm, hbm, hbm, hbm, _row((1, D)), _row((1, D)), _row((1, D)), _row((1, D))],
        out_specs=[tok(D), tok(QW), tok(SW), tok(DFF), tok(2 * DFF), tok(D), _row((8, D)), _row((8, D))],
        scratch_shapes=[pltpu.VMEM((D, D), BF16), pltpu.VMEM(w_ga.shape, BF16), pltpu.VMEM(w_gb.shape, BF16),
                        pltpu.VMEM((DFF, D), BF16), pltpu.SemaphoreType.DMA((4,))],
        compiler_params=_cp(("arbitrary",)),
    )(x1, gu, dx2, w_o, w_ga, w_gb, w_dn, gate1, a2, sh2, gate2)


def _wgrad(name, a, b, gate, w, carry=None):
    s, m = a.shape
    n = b.shape[1]
    tk = min(1024, s)
    nk = s // tk

    def body(a_ref, b_ref, g_ref, w_ref, o_hbm, dg_ref, acc_ref, sem):
        k = pl.program_id(0)

        @pl.when(k == 0)
        def _():
            acc_ref[...] = jnp.zeros_like(acc_ref)

        acc_ref[...] += _mm_tn(a_ref[...], b_ref[...].astype(BF16))

        @pl.when(k == nk - 1)
        def _():
            acc = acc_ref[...]
            dg_ref[...] = jnp.zeros_like(dg_ref)
            dg_ref[0:1, :] = jnp.sum(acc * w_ref[...].astype(F32), axis=0, keepdims=True)
            acc_ref[...] = acc * g_ref[...]
            cp = pltpu.make_async_copy(acc_ref, o_hbm, sem)
            cp.start()
            cp.wait()

    return _pcall(body, [a, b, gate, w], name=name, grid=(nk,),
                  out_shape=[jax.ShapeDtypeStruct((m, n), F32), jax.ShapeDtypeStruct((8, n), F32)],
                  in_specs=[pl.BlockSpec((tk, m), lambda k: (k, 0)), pl.BlockSpec((tk, n), lambda k: (k, 0)),
                            _row((1, n)), _row((m, n))],
                  out_specs=[pl.BlockSpec(memory_space=pl.ANY), _row((8, n))],
                  scratch_shapes=[pltpu.VMEM((m, n), F32), pltpu.SemaphoreType.DMA], carry=carry)


def _wgrad_gate_up(h2, dgu, carry=None):
    s = h2.shape[0]
    tk = min(1024, s)
    nk = s // tk
    n = dgu.shape[1]
    nj = n // 4

    def body(a_ref, b_ref, o_hbm, acc_ref, sems):
        k = pl.program_id(0)

        @pl.when(k == 0)
        def _():
            acc_ref[...] = jnp.zeros_like(acc_ref)

        acc_ref[...] += _mm_tn(a_ref[...], b_ref[...])

        @pl.when(k == nk - 1)
        def _():
            cps = [pltpu.make_async_copy(acc_ref.at[:, pl.ds(j * nj, nj)], o_hbm.at[j], sems.at[j]) for j in range(4)]
            for cp in cps:
                cp.start()
            for cp in cps:
                cp.wait()

    return _pcall(body, [h2, dgu], name="wgrad_gate_up", grid=(nk,),
                  out_shape=[jax.ShapeDtypeStruct((4, D, nj), F32)],
                  in_specs=[pl.BlockSpec((tk, D), lambda k: (k, 0)), pl.BlockSpec((tk, n), lambda k: (k, 0))],
                  out_specs=[pl.BlockSpec(memory_space=pl.ANY)],
                  scratch_shapes=[pltpu.VMEM((D, n), F32), pltpu.SemaphoreType.DMA((4,))], carry=carry)


def _wgrad_in_t(h1, pieces, carry=None):
    s = h1.shape[0]
    tk = min(1024, s)
    nk = s // tk

    def body(a_ref, dq_ref, dkv_ref, dz_ref, dxbc_ref, ddt_ref, o_hbm, acc_ref, tr_ref, sem):
        k = pl.program_id(0)

        @pl.when(k == 0)
        def _():
            acc_ref[...] = jnp.zeros_like(acc_ref)

        dproj = jnp.concatenate([dq_ref[...], dkv_ref[...], dz_ref[...], dxbc_ref[...], ddt_ref[...]], axis=1)
        acc_ref[...] += _mm_tn(a_ref[...], dproj)

        @pl.when(k == nk - 1)
        def _():
            for j in range(PROJ_W // 128):
                tr_ref[j * 128:(j + 1) * 128, :] = acc_ref[:, j * 128:(j + 1) * 128].T
            cp = pltpu.make_async_copy(tr_ref, o_hbm, sem)
            cp.start()
            cp.wait()

    return _pcall(body, [h1] + list(pieces), name="wgrad_in", grid=(nk,),
                  out_shape=[jax.ShapeDtypeStruct((PROJ_W, D), F32)],
                  in_specs=[pl.BlockSpec((tk, p.shape[1]), lambda k: (k, 0)) for p in [h1] + list(pieces)],
                  out_specs=[pl.BlockSpec(memory_space=pl.ANY)],
                  scratch_shapes=[pltpu.VMEM((D, PROJ_W), F32), pltpu.VMEM((PROJ_W, D), F32), pltpu.SemaphoreType.DMA],
                  carry=carry)


_SMALL = ["ada_b", "norm1", "conv_w", "conv_b", "dt_bias", "A_log", "D_skip", "sinks", "attn_out_norm",
          "ssm_out_norm", "norm2", "rel_bias", "final_norm"]


def _small_grad(name, gs, chip):
    if name == "ada_b":
        return jnp.concatenate([gs[j:j + 1, :] for j in range(6)], axis=1)
    if name == "conv_w":
        full = gs[7:11, :]
        out = full[:, 0:256]
        for j in range(1, 4):
            out = jnp.where(chip == j, full[:, j * 256:(j + 1) * 256], out)
        return out
    row, width = {"norm1": (6, D), "conv_b": (11, D), "norm2": (12, D), "final_norm": (13, D),
                  "attn_out_norm": (14, QW), "ssm_out_norm": (15, SW), "dt_bias": (16, NH), "A_log": (17, NH),
                  "D_skip": (18, NH), "sinks": (19, NH), "rel_bias": (24, NH)}[name]
    rows = NBUCKET if name == "rel_bias" else 1
    return gs[row:row + rows, 0:width]


def _small_update(small_all, where, ws, ms, vs):
    n = len(_SMALL)

    def body(where_ref, sa_ref, *refs):
        w_refs, m_refs, v_refs, outs = refs[:n], refs[n:2 * n], refs[2 * n:3 * n], refs[3 * n:]
        gs = sa_ref[0]
        for b in range(1, 8):
            gs = gs + sa_ref[b]
        chip = where_ref[1]
        for i, name in enumerate(_SMALL):
            g = _small_grad(name, gs, chip)
            lead = (0,) if name == "conv_w" else ()
            d, mo, vo = _adamw(w_refs[i][lead + (...,)], g, m_refs[i][lead + (...,)], v_refs[i][lead + (...,)])
            for k, val in enumerate((g, d, mo, vo)):
                outs[k * n + i][lead + (...,)] = val
        outs[4 * n][...] = gs[20:21, 0:128]

    shapes = [jax.ShapeDtypeStruct(w.shape, F32) for w in ws]
    vmem = pl.BlockSpec(memory_space=pltpu.VMEM)
    res = pl.pallas_call(
        body, name="small_update", out_shape=shapes * 4 + [jax.ShapeDtypeStruct((1, 128), F32)],
        in_specs=[pl.BlockSpec(memory_space=pltpu.SMEM)] + [vmem] * (1 + 3 * n), out_specs=[vmem] * (4 * n + 1),
    )(where, small_all, *ws, *ms, *vs)
    return [res[k * n:(k + 1) * n] for k in range(4)], res[4 * n][0, 0]


def _add_half(name, g, got, where, by_cols=False):
    rr, cc = got.shape[1:]
    if by_cols:
        mine = pl.BlockSpec((None, rr, cc), lambda i, w_ref: (i, 0, w_ref[0]))
    else:
        mine = pl.BlockSpec((None, None, rr, cc), lambda i, w_ref: (i, w_ref[0], 0, 0))

    def body(w_ref, g_ref, r_ref, o_ref, own_ref):
        s = g_ref[...] + r_ref[...]
        o_ref[...] = s.astype(BF16)

        @pl.when(pl.program_id(0) == w_ref[1])
        def _():
            own_ref[...] = s

    spec = pl.BlockSpec((None, rr, cc), lambda i, w_ref: (i, 0, 0))
    return _pcall(body, [where, g, got], name=name, grid=(4,), nprefetch=1,
                  out_shape=[jax.ShapeDtypeStruct(got.shape, BF16), jax.ShapeDtypeStruct((rr, cc), F32)],
                  in_specs=[mine, spec],
                  out_specs=[spec, pl.BlockSpec((rr, cc), lambda i, w_ref: (0, 0))])


def _add_chips(name, own, got):
    rr, cc = own.shape
    tr = rr // 2 if rr % 32 == 0 else rr

    def body(s_ref, r_ref, o_ref):
        o_ref[...] = ((s_ref[...] + r_ref[0].astype(F32)) + r_ref[1].astype(F32)) + r_ref[2].astype(F32)

    spec = pl.BlockSpec((tr, cc), lambda i: (i, 0))
    return _pcall(body, [own, got], name=name, grid=(rr // tr,), out_shape=[jax.ShapeDtypeStruct((rr, cc), F32)],
                  in_specs=[spec, pl.BlockSpec((3, tr, cc), lambda i: (0, i, 0))], out_specs=[spec])[0]


def _adamw_halves(name, mine, got, w, m, v, where, by_cols=False):
    rr, cc = mine.shape

    def body(w_ref_, t_ref, r_ref, w_ref, m_ref, v_ref, g_ref, d_ref, mo_ref, vo_ref):
        g = jnp.where(pl.program_id(0) == w_ref_[0], t_ref[...], r_ref[...])
        g_ref[...] = g
        d_ref[...], mo_ref[...], vo_ref[...] = _adamw(w_ref[...], g, m_ref[...], v_ref[...])

    if by_cols:
        grid = (2, 1)
        half = pl.BlockSpec((rr, cc), lambda h, i, w_ref_: (0, 0))
        full = pl.BlockSpec((rr, cc), lambda h, i, w_ref_: (0, h))
    else:
        tr = rr // 2
        grid = (2, 2)
        half = pl.BlockSpec((tr, cc), lambda h, i, w_ref_: (i, 0))
        full = pl.BlockSpec((None, tr, cc), lambda h, i, w_ref_: (0, 2 * h + i, 0))
    return _pcall(body, [where, mine, got, w, m, v], name=name, grid=grid, nprefetch=1,
                  out_shape=[jax.ShapeDtypeStruct(w.shape, F32)] * 4,
                  in_specs=[half, half, full, full, full], out_specs=[full] * 4)


def _bias_table(rel_bias, bucket, mask):
    def body(rb_ref, bk_ref, mk_ref, o_ref):
        bk = bk_ref[...]
        valid = mk_ref[...] > 0
        for h in range(NH):
            acc = jnp.zeros((BLK, 2 * BLK), F32)
            for b in range(NBUCKET):
                acc = jnp.where(bk == b, rb_ref[b, h], acc)
            o_ref[h] = jnp.where(valid, acc, NEG)

    vmem = pl.BlockSpec(memory_space=pltpu.VMEM)
    return pl.pallas_call(
        body, name="bias_table", out_shape=jax.ShapeDtypeStruct((NH, BLK, 2 * BLK), F32),
        in_specs=[pl.BlockSpec(memory_space=pltpu.SMEM), vmem, vmem], out_specs=vmem,
    )(rel_bias, bucket, mask)


def _pack_small(dsh1, p1, dsh2, p2, dg1a, dg1b, dg2, norm1, norm2, scale1, scale2, dcw, dcb, dfn,
                dnw_attn, dnw_ssm, dhd, av, dsink, drel, loss_acc):
    def body(dsh1_ref, p1_ref, dsh2_ref, p2_ref, dg1a_ref, dg1b_ref, dg2_ref, n1_ref, n2_ref, s1_ref, s2_ref,
             dcw_ref, dcb_ref, dfn_ref, da_ref, ds_ref, dhd_ref, av_ref, dsink_ref, drel_ref, loss_ref, o_ref):
        o_ref[...] = jnp.zeros_like(o_ref)
        p1v, p2v = p1_ref[0:1, :], p2_ref[0:1, :]
        o_ref[0:1, :] = dsh1_ref[0:1, :]
        o_ref[1:2, :] = p1v * n1_ref[...]
        o_ref[2:3, :] = dg1a_ref[0:1, :] + dg1b_ref[0:1, :]
        o_ref[3:4, :] = dsh2_ref[0:1, :]
        o_ref[4:5, :] = p2v * n2_ref[...]
        o_ref[5:6, :] = dg2_ref[0:1, :]
        o_ref[6:7, :] = p1v * (1.0 + s1_ref[...])
        o_ref[7:11, :] = dcw_ref[0:4, :]
        o_ref[11:12, :] = dcb_ref[0:1, :]
        o_ref[12:13, :] = p2v * (1.0 + s2_ref[...])
        o_ref[13:14, :] = dfn_ref[0:1, :]
        o_ref[14:15, 0:QW] = da_ref[0:1, :]
        o_ref[15:16, 0:SW] = ds_ref[0:1, :]
        o_ref[16:17, 0:128] = dhd_ref[0:1, :]
        o_ref[17:18, 0:128] = dhd_ref[1:2, :] * av_ref[...]
        o_ref[18:19, 0:128] = dhd_ref[2:3, :]
        o_ref[19:20, 0:128] = dsink_ref[0:1, :]
        o_ref[20:21, 0:128] = loss_ref[0:1, :]
        o_ref[24:56, 0:128] = drel_ref[...]

    return pl.pallas_call(body, name="pack_small", out_shape=jax.ShapeDtypeStruct((56, D), F32))(
        dsh1, p1, dsh2, p2, dg1a, dg1b, dg2, norm1, norm2, scale1, scale2, dcw, dcb, dfn,
        dnw_attn, dnw_ssm, dhd, av, dsink, drel, loss_acc)


def _pad_row(a, rows=1):
    return jnp.pad(a.reshape(rows, -1), ((0, 0), (0, D - a.size // rows)))


def kernel(x, c, ada_w, ada_b, norm1, w_in, conv_w, conv_b, dt_bias, A_log, D_skip, sinks, attn_out_norm, ssm_out_norm, w_o, norm2, w_gate_up, w_down, rel_bias, final_norm, loss_target, m_ada_w, m_ada_b, m_norm1, m_w_in, m_conv_w, m_conv_b, m_dt_bias, m_A_log, m_D_skip, m_sinks, m_attn_out_norm, m_ssm_out_norm, m_w_o, m_norm2, m_w_gate_up, m_w_down, m_rel_bias, m_final_norm, v_ada_w, v_ada_b, v_norm1, v_w_in, v_conv_w, v_conv_b, v_dt_bias, v_A_log, v_D_skip, v_sinks, v_attn_out_norm, v_ssm_out_norm, v_w_o, v_norm2, v_w_gate_up, v_w_down, v_rel_bias, v_final_norm):
    xi, yi, ci = lax.axis_index("x"), lax.axis_index("y"), lax.axis_index("c")
    chip = 2 * xi + yi
    me = 4 * xi + 2 * yi + ci
    where = jnp.stack([ci, chip]).astype(jnp.int32)
    xs2, tgt = x[0], loss_target[0]

    first = jnp.concatenate([c, _pad_row(conv_w[0], CK), jnp.zeros((3, D), F32)], axis=0)
    w_in_t, m_w_in_t, v_w_in_t = w_in[0].T, m_w_in[0].T, v_w_in[0].T
    w_in_b, w_o_b, w_dn_b = w_in_t.astype(BF16), w_o[0].astype(BF16), w_down[0].astype(BF16)
    w_gu_b = w_gate_up[0].astype(BF16)
    hw = D // 2
    fetch_half = _Carry(
        [w_in_b], [jax.ShapeDtypeStruct((4,) + w_in_b.shape, BF16)],
        lambda x_, y_, c_: [(None, 0, slice(None), 0, 2 * x_ + y_)] + [
            (f, 0, (slice(None), pl.ds(c_ * hw, hw)), 0, (2 * x_ + y_, slice(None), pl.ds(c_ * hw, hw))) for f in _CHIPS3])

    def swap_halves(x_, y_, c_):
        there = [(jnp.bitwise_xor(2 * x_ + y_, k + 1), slice(None), pl.ds(c_ * hw, hw)) for k in range(3)]
        return [(_SIBLING, 1, at, 1, at) for at in there]

    first_all, w_in_g = _exchange_two("gather_first", _merge(_gather8_carry(first), fetch_half), swap_halves)
    c_all = first_all[:, 0, :]
    cw_full = jnp.concatenate([first_all[2 * j, 1:1 + CK, 0:256] for j in range(4)], axis=1)
    w_in_f = jnp.pad(w_in_g.reshape(IN_W, D), ((0, PROJ_W - IN_W), (0, 0)))

    ncol = ada_w.shape[2]
    mod_cols = _ada_fwd(c_all, ada_w[0], lax.dynamic_slice(ada_b, (0, chip * ncol), (1, ncol)))
    mod_all = _exchange("gather_mod", _gather_chips_carry([mod_cols]))[0]
    mod = lax.dynamic_slice(jnp.transpose(mod_all, (1, 0, 2)).reshape(8, 4 * ncol), (me, 0), (1, 4 * ncol))
    shift1, scale1, gate1, shift2, scale2, gate2 = [mod[:, j * D:(j + 1) * D] for j in range(6)]
    a1 = norm1 * (1.0 + scale1)
    a2 = norm2 * (1.0 + scale2)

    hdn = DFF // 8
    q, kv, z, xbc, dtr, w_o_g, w_dna_g = _in_proj_fwd(xs2, a1, shift1, w_in_f,
                                                      carry=_gather_chips_carry([w_o_b, w_dn_b[0:hdn]]))
    w_o_f = w_o_g.reshape(D, D)
    bucket, mask = _attn_geometry()
    bucket = jnp.asarray(bucket)
    bias = _bias_table(rel_bias, bucket, jnp.asarray(mask.astype(np.int32)))
    sinks1 = sinks[0]
    ya, w_ga_g = _attn_fwd(q, kv, bias, sinks1, attn_out_norm, carry=_gather_chips_carry([w_gu_b[0:D // 2]]))
    cw8 = jnp.concatenate([cw_full, jnp.zeros((4, XBCW), F32)], axis=0)
    dtb = _pad_row(dt_bias)[:, 0:128]
    av = _pad_row(-jnp.exp(A_log))[:, 0:128]
    dk = jnp.repeat(D_skip, HD, axis=1)
    ys, hs, pre, w_gb_g, w_dnb_g = _ssd_fwd(z, xbc, dtr, cw8, conv_b, dtb, av, dk, ssm_out_norm,
                                            carry=_gather_chips_carry([w_gu_b[D // 2:D], w_dn_b[hdn:2 * hdn]]))
    w_dn_f = jnp.stack([w_dna_g, w_dnb_g], axis=1).reshape(DFF, D)
    fn = final_norm[None, :]
    x1, gu, dx2, loss_acc, dfn = _mlp_fwd(xs2, ya, ys, tgt, w_o_f, w_ga_g, w_gb_g, w_dn_f, gate1, a2, shift2, gate2, fn)

    def to_sibling(p):
        return _Carry([p], [jax.ShapeDtypeStruct((4,) + p.shape[2:], F32)],
                      lambda x_, y_, c_: [(_SIBLING, 0, (j, 1 - c_), 0, j) for j in range(4)])

    def to_chips(s4):
        return _Carry([s4], [jax.ShapeDtypeStruct((3,) + s4.shape[1:], s4.dtype)],
                      lambda x_, y_, c_: [(f, 0, jnp.bitwise_xor(2 * x_ + y_, k + 1), 0, k) for k, f in enumerate(_CHIPS3)])

    def back(t):
        return _Carry([t[None]], [jax.ShapeDtypeStruct((1,) + t.shape, F32)], lambda x_, y_, c_: [(_SIBLING, 0, 0, 0, 0)])

    dx1, dya, dys, act, dgu, h2, dsh2, p2 = _mlp_bwd(x1, gu, dx2, w_o_f, w_ga_g, w_gb_g, w_dn_f, gate1, a2, shift2, gate2)
    p_gu = _wgrad_gate_up(h2, dgu)[0].reshape(4, 2, D // 2, 2 * DFF // 4)
    g_dn, dg2, got1_gu = _wgrad("wgrad_down", act, dx2, gate2, w_dn_f, carry=to_sibling(p_gu))
    p_dn = g_dn.reshape(4, 2, DFF // 8, D)
    s4_gu, own_gu = _add_half("rs_add_half_gu", p_gu, got1_gu, where)
    dq, dkv, dbias, dsink, dnw_attn, got2_gu, got1_dn = _attn_bwd(
        q, kv, dya, bias, sinks1, attn_out_norm, carry=_merge(to_chips(s4_gu), to_sibling(p_dn)))
    drel = _rel_bias_grad(dbias, bucket)
    mine_gu = _add_chips("rs_add_chips_gu", own_gu, got2_gu)
    s4_dn, own_dn = _add_half("rs_add_half_dn", p_dn, got1_dn, where)
    dz, dxbc, ddt, dcw, dcb, dnw_ssm, dhd, got2_dn, got3_gu = _ssd_bwd(
        z, xbc, pre, dtr, dys, hs, cw8, dtb, av, dk, ssm_out_norm, carry=_merge(to_chips(s4_dn), back(mine_gu)))
    mine_dn = _add_chips("rs_add_chips_dn", own_dn, got2_dn)
    grad_x, h1, dsh1, p1 = _in_proj_bwd(xs2, dx1, a1, shift1, w_in_f, dq, dkv, dz, dxbc, ddt)
    g_in_t, got3_dn = _wgrad_in_t(h1, [dq, dkv, dz, dxbc, ddt], carry=back(mine_dn))
    p_in = g_in_t[0:IN_W].reshape(4, IN_W // 4, D)

    def to_sibling_cols(p):
        return _Carry([p], [jax.ShapeDtypeStruct(p.shape[:2] + (D // 2,), F32)],
                      lambda x_, y_, c_: [(_SIBLING, 0, (j, slice(None), pl.ds((1 - c_) * (D // 2), D // 2)), 0, j)
                                          for j in range(4)])

    g_oa, dg1a, got1_in = _wgrad("wgrad_o_attn", ya, dx1, gate1, w_o_f[0:QW], carry=to_sibling_cols(p_in))
    s4_in, own_in = _add_half("rs_add_half_in", p_in, got1_in, where, by_cols=True)
    g_os, dg1b, got2_in = _wgrad("wgrad_o_ssm", ys, dx1, gate1, w_o_f[QW:D], carry=to_chips(s4_in))
    mine_in = _add_chips("rs_add_chips_in", own_in, got2_in)
    p_o = jnp.concatenate([g_oa, g_os], axis=0).reshape(4, 2, D // 8, D)

    small = _pack_small(dsh1, p1, dsh2, p2, dg1a, dg1b, dg2, norm1, norm2, scale1, scale2, dcw, dcb, dfn,
                        dnw_attn, dnw_ssm, dhd, av, dsink, drel, loss_acc)
    small_all, got1_o, got3_in = _exchange(
        "gather_small", _merge(_gather8_carry(small), to_sibling(p_o), back(mine_in)))
    s4_o, own_o = _add_half("rs_add_half_o", p_o, got1_o, where)
    mine_o = _add_chips("rs_add_chips_o", own_o, _exchange("rs_chips_o", to_chips(s4_o))[0])
    got3_o = _exchange("rs_back_o", back(mine_o))[0]
    small_res, loss = _small_update(
        small_all, where,
        [ada_b, norm1, conv_w, conv_b, dt_bias, A_log, D_skip, sinks, attn_out_norm, ssm_out_norm, norm2, rel_bias,
         final_norm[None, :]],
        [m_ada_b, m_norm1, m_conv_w, m_conv_b, m_dt_bias, m_A_log, m_D_skip, m_sinks, m_attn_out_norm,
         m_ssm_out_norm, m_norm2, m_rel_bias, m_final_norm[None, :]],
        [v_ada_b, v_norm1, v_conv_w, v_conv_b, v_dt_bias, v_A_log, v_D_skip, v_sinks, v_attn_out_norm,
         v_ssm_out_norm, v_norm2, v_rel_bias, v_final_norm[None, :]])
    small_out = [dict(zip(_SMALL, r)) for r in small_res]
    for r in small_out:
        r["final_norm"] = r["final_norm"][0]

    dmod_all = small_all[:, 0:6, :].reshape(8, 6 * D)
    dmod_loc = lax.dynamic_slice(dmod_all, (0, chip * ncol), (8, ncol))
    ada_out = _ada_bwd_adamw(c_all.T, dmod_loc, ada_w[0], m_ada_w[0], v_ada_w[0])

    big_gu = _adamw_halves("adamw_gate_up", mine_gu, got3_gu[0], w_gate_up, m_w_gate_up, v_w_gate_up, where)
    big_dn = _adamw_halves("adamw_down", mine_dn, got3_dn[0], w_down, m_w_down, v_w_down, where)
    big_o = _adamw_halves("adamw_o", mine_o, got3_o[0], w_o, m_w_o, v_w_o, where)
    big_in = [o.T[None] for o in _adamw_halves("adamw_in", mine_in, got3_in[0], w_in_t, m_w_in_t, v_w_in_t, where,
                                               by_cols=True)]
    big = [big_in, big_o, big_gu, big_dn]

    order = ["ada_w", "ada_b", "norm1", "w_in", "conv_w", "conv_b", "dt_bias", "A_log", "D_skip", "sinks",
             "attn_out_norm", "ssm_out_norm", "w_o", "norm2", "w_gate_up", "w_down", "rel_bias", "final_norm"]
    bigname = {"w_in": 0, "w_o": 1, "w_gate_up": 2, "w_down": 3}
    res = [loss, grad_x[None]]
    for kind in range(4):
        for nm in order:
            if nm == "ada_w":
                res.append(ada_out[kind][None])
            elif nm in bigname:
                res.append(big[bigname[nm]][kind])
            else:
                res.append(small_out[kind][nm])
    return tuple(res)
```

```python
import numpy as np
import jax
import jax.numpy as jnp
from jax import lax
from jax.experimental import pallas as pl
from jax.experimental.pallas import tpu as pltpu

F32, BF16 = jnp.float32, jnp.bfloat16
HI = lax.Precision.HIGHEST

D = 1024
QW, KVW = 512, 128
NH, HD, NKV = 8, 64, 2
SW = 512
NST = 128
XBCW = 1024
CK = 4
BLK = 128
DFF = 2816
IN_W = 2312
PROJ_W = 2432
EPS = 1e-6
NEG = -1e30
NBUCKET = 32

B1, B2, LR, AEPS, WD, STEP = 0.9, 0.999, 0.001, 1e-08, 0.01, 10

VMEM_LIMIT = 56 * 1024 * 1024

_NT = (((1,), (1,)), ((), ()))
_TN = (((0,), (0,)), ((), ()))


def _mm(a, b):
    return jnp.dot(a, b, preferred_element_type=F32)


def _mm_nt(a, b):
    return lax.dot_general(a, b, _NT, preferred_element_type=F32)


def _mm_tn(a, b):
    return lax.dot_general(a, b, _TN, preferred_element_type=F32)


def _mm_hi(a, b):
    return jnp.dot(a, b, preferred_element_type=F32, precision=HI)


def _split3(x):
    hi = x.astype(BF16)
    r = x - hi.astype(F32)
    mid = r.astype(BF16)
    lo = (r - mid.astype(F32)).astype(BF16)
    return hi, mid, lo


def _sel_r(x, e):
    hi, mid, lo = _split3(x)
    return (_mm(hi, e) + _mm(mid, e)) + _mm(lo, e)


def _sel_l(e, x):
    hi, mid, lo = _split3(x)
    return (_mm(e, hi) + _mm(e, mid)) + _mm(e, lo)


def _sig(x):
    return 1.0 / (1.0 + jnp.exp(-x))


def _cp(sem):
    return pltpu.CompilerParams(dimension_semantics=sem, vmem_limit_bytes=VMEM_LIMIT)


def _row(shape):
    nd = len(shape)
    return pl.BlockSpec(shape, lambda *_: (0,) * nd)


def _adamw(w, g, m, v):
    m = B1 * m + (1.0 - B1) * g
    v = B2 * v + (1.0 - B2) * (g * g)
    m_hat = m / (1.0 - B1 ** STEP)
    v_hat = v / (1.0 - B2 ** STEP)
    delta = -LR * (m_hat / (jnp.sqrt(v_hat) + AEPS) + WD * w)
    return delta, m, v


class _Carry:
    def __init__(self, inps, outs, copies):
        self.inps, self.outs, self.copies = list(inps), list(outs), copies
        self.n = len(copies(0, 0, 0))

    def descriptors(self, in_refs, out_refs, send_sems, recv_sems):
        x, y, c = lax.axis_index("x"), lax.axis_index("y"), lax.axis_index("c")
        out = []
        for j, (flip, a, si, o, di) in enumerate(self.copies(x, y, c)):
            if flip is None:
                out.append(pltpu.make_async_copy(in_refs[a].at[si], out_refs[o].at[di], send_sems.at[j]))
            else:
                fx, fy, fc = flip
                peer = (1 - x if fx else x, 1 - y if fy else y, 1 - c if fc else c)
                out.append(pltpu.make_async_remote_copy(
                    src_ref=in_refs[a].at[si], dst_ref=out_refs[o].at[di],
                    send_sem=send_sems.at[j], recv_sem=recv_sems.at[j],
                    device_id=peer, device_id_type=pl.DeviceIdType.MESH))
        return out


def _pcall(body, args, *, name, grid, in_specs, out_specs, out_shape, scratch_shapes=(), sem=None, nprefetch=0,
           carry=None):
    out_shape, out_specs = list(out_shape), list(out_specs)
    in_specs, scratch_shapes = list(in_specs), list(scratch_shapes)
    nin, nout, nscr = len(in_specs), len(out_shape), len(scratch_shapes)
    run = body
    if carry is not None:
        ncin, ncout = len(carry.inps), len(carry.outs)
        hbm = pl.BlockSpec(memory_space=pl.ANY)

        def run(*refs):
            pre, r = refs[:nprefetch], refs[nprefetch:]
            ins, cins = r[:nin], r[nin:nin + ncin]
            r = r[nin + ncin:]
            outs, couts = r[:nout], r[nout:nout + ncout]
            r = r[nout + ncout:]
            scr, (send_sems, recv_sems) = r[:nscr], r[nscr:]
            first = pl.program_id(0) == 0
            last = pl.program_id(0) == grid[0] - 1
            for ax in range(1, len(grid)):
                first = jnp.logical_and(first, pl.program_id(ax) == 0)
                last = jnp.logical_and(last, pl.program_id(ax) == grid[ax] - 1)

            @pl.when(first)
            def _():
                for d in carry.descriptors(cins, couts, send_sems, recv_sems):
                    d.start()

            body(*pre, *ins, *outs, *scr)

            @pl.when(last)
            def _():
                for d in carry.descriptors(cins, couts, send_sems, recv_sems):
                    d.wait()

        in_specs = in_specs + [hbm] * ncin
        out_specs = out_specs + [hbm] * ncout
        out_shape = out_shape + carry.outs
        scratch_shapes = scratch_shapes + [pltpu.SemaphoreType.DMA((carry.n,)), pltpu.SemaphoreType.DMA((carry.n,))]
        args = list(args) + carry.inps
    if sem is None:
        sem = ("arbitrary",) * len(grid)
    if nprefetch:
        kw = dict(grid_spec=pltpu.PrefetchScalarGridSpec(num_scalar_prefetch=nprefetch, grid=grid, in_specs=in_specs,
                                                         out_specs=out_specs, scratch_shapes=scratch_shapes))
    else:
        kw = dict(grid=grid, in_specs=in_specs, out_specs=out_specs, scratch_shapes=scratch_shapes)
    res = pl.pallas_call(run, name=name, out_shape=out_shape, compiler_params=_cp(sem), **kw)(*args)
    return list(res)


def _merge(*carries):
    inps, outs, offs = [], [], []
    for cr in carries:
        offs.append((len(inps), len(outs)))
        inps += cr.inps
        outs += cr.outs

    def copies(x, y, c):
        return [(f, a + io, si, o + oo, di) for cr, (io, oo) in zip(carries, offs) for f, a, si, o, di in cr.copies(x, y, c)]

    return _Carry(inps, outs, copies)


def _exchange(name, carry):
    return _pcall(lambda: None, [], name=name, grid=(1,), in_specs=[], out_specs=[], out_shape=[], carry=carry)


def _exchange_two(name, carry, then):
    second = _Carry([], [], then)
    nin, nout = len(carry.inps), len(carry.outs)

    def body(*refs):
        ins, outs = refs[:nin], refs[nin:nin + nout]
        send_a, recv_a, send_b, recv_b = refs[nin + nout:]
        for descs in (carry.descriptors(ins, outs, send_a, recv_a), second.descriptors(outs, outs, send_b, recv_b)):
            for d in descs:
                d.start()
            for d in descs:
                d.wait()

    hbm = pl.BlockSpec(memory_space=pl.ANY)
    return list(pl.pallas_call(
        body, name=name, out_shape=carry.outs, in_specs=[hbm] * nin, out_specs=[hbm] * nout,
        scratch_shapes=[pltpu.SemaphoreType.DMA((carry.n,)), pltpu.SemaphoreType.DMA((carry.n,)),
                        pltpu.SemaphoreType.DMA((second.n,)), pltpu.SemaphoreType.DMA((second.n,))],
    )(*carry.inps))


_ALL7 = [(f >> 2 & 1, f >> 1 & 1, f & 1) for f in range(1, 8)]
_CHIPS3 = [(0, 1, 0), (1, 0, 0), (1, 1, 0)]
_SIBLING = (0, 0, 1)


def _gather8_carry(blk):
    def copies(x, y, c):
        me = 4 * x + 2 * y + c
        return [(None, 0, 0, 0, me)] + [(f, 0, 0, 0, me) for f in _ALL7]

    return _Carry([blk[None]], [jax.ShapeDtypeStruct((8,) + blk.shape, blk.dtype)], copies)


def _gather_chips_carry(blks):
    def copies(x, y, c):
        chip = 2 * x + y
        return [(f, a, 0, a, chip) for a in range(len(blks)) for f in [None] + _CHIPS3]

    return _Carry([b[None] for b in blks], [jax.ShapeDtypeStruct((4,) + b.shape, b.dtype) for b in blks], copies)


def _ada_fwd(c_all, w_loc, b_loc):
    n = w_loc.shape[1]
    tn = 512

    def body(c_ref, w_ref, b_ref, o_ref):
        cv = c_ref[...]
        cond = cv * _sig(cv)
        o_ref[...] = _mm_hi(cond, w_ref[...]) + b_ref[...]

    return pl.pallas_call(
        body, name="ada_fwd", grid=(n // tn,),
        out_shape=jax.ShapeDtypeStruct((8, n), F32),
        in_specs=[_row((8, D)), pl.BlockSpec((D, tn), lambda j: (0, j)), pl.BlockSpec((1, tn), lambda j: (0, j))],
        out_specs=pl.BlockSpec((8, tn), lambda j: (0, j)),
        compiler_params=_cp(("parallel",)),
    )(c_all, w_loc, b_loc)


def _ada_bwd_adamw(c_all_t, dmod_loc, w, m, v, carry=None):
    n = w.shape[1]
    tn = 512

    def body(ct_ref, dm_ref, w_ref, m_ref, v_ref, g_ref, d_ref, mo_ref, vo_ref):
        ct = ct_ref[...]
        cond = ct * _sig(ct)
        dm = dm_ref[...]
        g = cond[:, 0:1] * dm[0:1, :]
        for b in range(1, 8):
            g = g + cond[:, b:b + 1] * dm[b:b + 1, :]
        g_ref[...] = g
        d_ref[...], mo_ref[...], vo_ref[...] = _adamw(w_ref[...], g, m_ref[...], v_ref[...])

    wspec = pl.BlockSpec((D, tn), lambda j: (0, j))
    return _pcall(
        body, [c_all_t, dmod_loc, w, m, v], name="ada_bwd_adamw", grid=(n // tn,),
        out_shape=[jax.ShapeDtypeStruct((D, n), F32)] * 4,
        in_specs=[_row((D, 8)), pl.BlockSpec((8, tn), lambda j: (0, j)), wspec, wspec, wspec],
        out_specs=[wspec] * 4, carry=carry)


def _in_proj_fwd(x, a1, sh1, w_in, carry=None):
    s = x.shape[0]
    tm = 512

    def body(x_ref, a_ref, s_ref, w_ref, q_ref, kv_ref, z_ref, xbc_ref, dt_ref):
        def norm(rows):
            xv = x_ref[rows, :]
            r = lax.rsqrt(jnp.mean(xv * xv, axis=-1, keepdims=True) + EPS)
            return (xv * r * a_ref[...] + s_ref[...]).astype(BF16)

        def project(rows, h):
            p = _mm_nt(h, w_ref[...])
            q_ref[rows, :] = p[:, 0:512].astype(BF16)
            kv_ref[rows, :] = p[:, 512:768].astype(BF16)
            z_ref[rows, :] = p[:, 768:1280]
            xbc_ref[rows, :] = p[:, 1280:2304]
            dt_ref[rows, :] = p[:, 2304:2432]

        r0, r1 = slice(0, tm // 2), slice(tm // 2, tm)
        h0 = norm(r0)
        project(r0, h0)
        project(r1, norm(r1))

    def tok(w):
        return pl.BlockSpec((tm, w), lambda i: (i, 0))

    return _pcall(
        body, [x, a1, sh1, w_in], name="in_proj_fwd", grid=(s // tm,),
        out_shape=[jax.ShapeDtypeStruct((s, QW), BF16), jax.ShapeDtypeStruct((s, 2 * KVW), BF16),
                   jax.ShapeDtypeStruct((s, SW), F32), jax.ShapeDtypeStruct((s, XBCW), F32),
                   jax.ShapeDtypeStruct((s, 128), F32)],
        in_specs=[tok(D), _row((1, D)), _row((1, D)), _row((PROJ_W, D))],
        out_specs=[tok(QW), tok(2 * KVW), tok(SW), tok(XBCW), tok(128)], carry=carry)


def _in_proj_bwd(x, dx1, a1, sh1, w_in, dq, dkv, dz, dxbc, ddt, carry=None):
    s = x.shape[0]
    tm = 512

    def body(x_ref, dx1_ref, a_ref, s_ref, w_ref, dq_ref, dkv_ref, dz_ref, dxbc_ref, ddt_ref,
             gx_ref, h_ref, dsh_ref, p_ref):
        i = pl.program_id(0)

        @pl.when(i == 0)
        def _():
            dsh_ref[...] = jnp.zeros_like(dsh_ref)
            p_ref[...] = jnp.zeros_like(p_ref)

        def gather(st):
            rows = st["rows"]
            st["dproj"] = jnp.concatenate([dq_ref[rows, :], dkv_ref[rows, :], dz_ref[rows, :], dxbc_ref[rows, :],
                                           ddt_ref[rows, :]], axis=1)

        def back(st):
            st["dh"] = _mm(st.pop("dproj"), w_ref[...])

        def norm(st):
            rows, dh = st["rows"], st.pop("dh")
            xv = x_ref[rows, :]
            r = lax.rsqrt(jnp.mean(xv * xv, axis=-1, keepdims=True) + EPS)
            xn = xv * r
            a = a_ref[...]
            h_ref[rows, :] = (xn * a + s_ref[...]).astype(BF16)
            st["dsh"] = jnp.sum(dh, axis=0, keepdims=True)
            st["p"] = jnp.sum(dh * xn, axis=0, keepdims=True)
            u = dh * a
            gx_ref[rows, :] = dx1_ref[rows, :] + r * u - xn * (r * jnp.mean(u * xn, axis=-1, keepdims=True))

        g0, g1 = [dict(rows=slice(k * (tm // 2), (k + 1) * (tm // 2))) for k in range(2)]
        for stage, st in [(gather, g0), (back, g0), (gather, g1), (norm, g0), (back, g1), (norm, g1)]:
            stage(st)
        dsh_ref[0:1, :] += g0["dsh"] + g1["dsh"]
        p_ref[0:1, :] += g0["p"] + g1["p"]

    def tok(w):
        return pl.BlockSpec((tm, w), lambda i: (i, 0))

    return _pcall(
        body, [x, dx1, a1, sh1, w_in, dq, dkv, dz, dxbc, ddt], name="in_proj_bwd", grid=(s // tm,),
        out_shape=[jax.ShapeDtypeStruct((s, D), F32), jax.ShapeDtypeStruct((s, D), BF16),
                   jax.ShapeDtypeStruct((8, D), F32), jax.ShapeDtypeStruct((8, D), F32)],
        in_specs=[tok(D), tok(D), _row((1, D)), _row((1, D)), _row((PROJ_W, D)),
                  tok(QW), tok(2 * KVW), tok(SW), tok(XBCW), tok(128)],
        out_specs=[tok(D), tok(D), _row((8, D)), _row((8, D))], carry=carry)


def _attn_geometry():
    dist = np.arange(BLK)[:, None] + BLK - np.arange(2 * BLK)[None, :]
    n = np.maximum(dist, 0)
    max_exact = NBUCKET // 2
    large = max_exact + (np.log(np.maximum(n, 1) / max_exact) / np.log(128 / max_exact)
                         * (NBUCKET - max_exact)).astype(np.int32)
    large = np.minimum(large, NBUCKET - 1)
    bucket = np.where(n < max_exact, n, large).astype(np.int32)
    mask = (dist >= 0) & (dist < 128)
    return bucket, mask


def _attn_heads(is_first, q_blk, kvw, bias_ref, sinks_ref):
    qv = q_blk * 0.125
    col = lax.broadcasted_iota(jnp.int32, (BLK, 2 * BLK), 1)
    first = jnp.where(jnp.logical_and(is_first, col < BLK), NEG, 0.0)
    groups = []
    for g in range(NKV):
        qs = jnp.concatenate([qv[:, (4 * g + r) * HD:(4 * g + r + 1) * HD] for r in range(4)], axis=0)
        kw = kvw[:, g * HD:(g + 1) * HD]
        vw = kvw[:, KVW + g * HD:KVW + (g + 1) * HD]
        sc = _mm_nt(qs, kw)
        pn, ps = [], []
        for r in range(4):
            h = 4 * g + r
            sr = sc[r * BLK:(r + 1) * BLK] + bias_ref[h] + first
            sink = sinks_ref[h]
            m = jnp.maximum(jnp.max(sr, axis=-1, keepdims=True), sink)
            p = jnp.exp(sr - m)
            es = jnp.exp(sink - m)
            inv = 1.0 / (jnp.sum(p, axis=-1, keepdims=True) + es)
            pn.append(p * inv)
            ps.append(es * inv)
        pn = jnp.concatenate(pn, axis=0)
        ps = jnp.concatenate(ps, axis=0)
        o = _mm(pn.astype(BF16), vw)
        groups.append((qs, kw, vw, pn, ps, o))
    return groups


def _unstack_heads(parts):
    return jnp.concatenate([p[r * BLK:(r + 1) * BLK] for p in parts for r in range(4)], axis=1)


NB = 2


def _attn_fwd(q, kv, bias, sinks, nw, carry=None):
    s = q.shape[0]

    def body(q_ref, kvp_ref, kvc_ref, bias_ref, sinks_ref, nw_ref, y_ref):
        t = pl.program_id(0)
        kv3 = jnp.concatenate([kvp_ref[...], kvc_ref[...]], axis=0)
        for sub in range(NB):
            rows = slice(sub * BLK, (sub + 1) * BLK)
            groups = _attn_heads(jnp.logical_and(t == 0, sub == 0), q_ref[rows, :], kv3[sub * BLK:(sub + 2) * BLK],
                                 bias_ref, sinks_ref)
            o = _unstack_heads([g[5] for g in groups])
            r = lax.rsqrt(jnp.mean(o * o, axis=-1, keepdims=True) + EPS)
            y_ref[rows, :] = (o * r * nw_ref[...]).astype(BF16)

    return _pcall(
        body, [q, kv, kv, bias, sinks, nw], name="attn_fwd", grid=(s // (NB * BLK),),
        out_shape=[jax.ShapeDtypeStruct((s, QW), BF16)],
        in_specs=[pl.BlockSpec((NB * BLK, QW), lambda t: (t, 0)),
                  pl.BlockSpec((BLK, 2 * KVW), lambda t: (jnp.maximum(NB * t - 1, 0), 0)),
                  pl.BlockSpec((NB * BLK, 2 * KVW), lambda t: (t, 0)),
                  _row((NH, BLK, 2 * BLK)),
                  pl.BlockSpec(memory_space=pltpu.SMEM),
                  _row((1, QW))],
        out_specs=[pl.BlockSpec((NB * BLK, QW), lambda t: (t, 0))], carry=carry)


def _attn_bwd(q, kv, dya, bias, sinks, nw, act, dx2, gate2, w_dn, carry=None):
    s = q.shape[0]
    nt = s // (NB * BLK)
    npiece = DFF // NB

    def body(q_ref, kvp_ref, kvc_ref, dy_ref, bias_ref, sinks_ref, nw_ref, act_ref, dx2_ref, g2_ref, wdn_ref,
             dq_ref, dkv_ref, dbias_ref, dsink_ref, dnw_ref, gdn_hbm, dg2_ref, carry_ref, held_ref, acc_ref, sem):
        t = pl.program_id(0)

        @pl.when(t == 0)
        def _():
            carry_ref[...] = jnp.zeros_like(carry_ref)
            held_ref[...] = jnp.zeros_like(held_ref)
            dbias_ref[...] = jnp.zeros_like(dbias_ref)
            dsink_ref[...] = jnp.zeros_like(dsink_ref)
            dnw_ref[...] = jnp.zeros_like(dnw_ref)
            acc_ref[...] = jnp.zeros_like(acc_ref)

        def wgrad_piece(sub):
            rows = slice(sub * npiece, (sub + 1) * npiece)
            acc_ref[rows, :] += _mm_tn(act_ref[:, rows], dx2_ref[...].astype(BF16))

        def block(sub, kv3):
            rows = slice(sub * BLK, (sub + 1) * BLK)
            groups = _attn_heads(jnp.logical_and(t == 0, sub == 0), q_ref[rows, :], kv3[sub * BLK:(sub + 2) * BLK],
                                 bias_ref, sinks_ref)
            o = _unstack_heads([g[5] for g in groups])
            r = lax.rsqrt(jnp.mean(o * o, axis=-1, keepdims=True) + EPS)
            dy = dy_ref[rows, :]
            on = o * r
            dnw_ref[0:1, :] += jnp.sum(dy * on, axis=0, keepdims=True)
            u = dy * nw_ref[...]
            do = r * u - on * (r * jnp.mean(u * on, axis=-1, keepdims=True))
            dq_parts, dk_parts, dv_parts = [], [], []
            for g, (qs, kw, vw, pn, ps, og) in enumerate(groups):
                dos = jnp.concatenate([do[:, (4 * g + r_) * HD:(4 * g + r_ + 1) * HD] for r_ in range(4)], axis=0)
                delta = jnp.sum(dos * og, axis=-1, keepdims=True)
                dp = _mm_nt(dos.astype(BF16), vw)
                ds = pn * (dp - delta)
                dsk = ps * delta
                lane = lax.broadcasted_iota(jnp.int32, (1, 128), 1)
                for r_ in range(4):
                    h = 4 * g + r_
                    dbias_ref[h] += ds[r_ * BLK:(r_ + 1) * BLK]
                    dsink_ref[0:1, :] -= jnp.where(lane == h, jnp.sum(dsk[r_ * BLK:(r_ + 1) * BLK]), 0.0)
                dsb = ds.astype(BF16)
                dq_parts.append(_mm(dsb, kw) * 0.125)
                dk_parts.append(_mm_tn(dsb, qs))
                dv_parts.append(_mm_tn(pn.astype(BF16), dos.astype(BF16)))
            dq_ref[rows, :] = _unstack_heads(dq_parts).astype(BF16)
            return jnp.concatenate(dk_parts + dv_parts, axis=1)

        @pl.when(t < nt)
        def _():
            kv3 = jnp.concatenate([kvp_ref[...], kvc_ref[...]], axis=0)
            tail = carry_ref[...]
            for sub in range(NB):
                d = block(sub, kv3)
                done = tail + d[0:BLK]
                if sub == 0:
                    dkv_ref[0:(NB - 1) * BLK, :] = held_ref[...].astype(BF16)
                    dkv_ref[(NB - 1) * BLK:NB * BLK, :] = done.astype(BF16)
                else:
                    held_ref[(sub - 1) * BLK:sub * BLK, :] = done
                tail = d[BLK:2 * BLK]
                wgrad_piece(sub)
            carry_ref[...] = tail

        @pl.when(t == nt)
        def _():
            dkv_ref[0:(NB - 1) * BLK, :] = held_ref[...].astype(BF16)
            dkv_ref[(NB - 1) * BLK:NB * BLK, :] = carry_ref[...].astype(BF16)
            acc = acc_ref[...]
            dg2_ref[...] = jnp.zeros_like(dg2_ref)
            dg2_ref[0:1, :] = jnp.sum(acc * wdn_ref[...].astype(F32), axis=0, keepdims=True)
            acc_ref[...] = acc * g2_ref[...]
            cp = pltpu.make_async_copy(acc_ref, gdn_hbm, sem)
            cp.start()
            cp.wait()

    last = nt - 1
    tile = lambda w: pl.BlockSpec((NB * BLK, w), lambda t: (jnp.minimum(t, last), 0))
    return _pcall(
        body, [q, kv, kv, dya, bias, sinks, nw, act, dx2, gate2, w_dn], name="attn_bwd", grid=(nt + 1,),
        out_shape=[jax.ShapeDtypeStruct((s, QW), BF16), jax.ShapeDtypeStruct((s, 2 * KVW), BF16),
                   jax.ShapeDtypeStruct((NH, BLK, 2 * BLK), F32), jax.ShapeDtypeStruct((NH, 128), F32),
                   jax.ShapeDtypeStruct((8, QW), F32), jax.ShapeDtypeStruct((DFF, D), F32),
                   jax.ShapeDtypeStruct((8, D), F32)],
        in_specs=[tile(QW),
                  pl.BlockSpec((BLK, 2 * KVW), lambda t: (jnp.clip(NB * t - 1, 0, NB * nt - 1), 0)),
                  tile(2 * KVW), tile(QW),
                  _row((NH, BLK, 2 * BLK)),
                  pl.BlockSpec(memory_space=pltpu.SMEM),
                  _row((1, QW)), tile(DFF), tile(D), _row((1, D)), _row((DFF, D))],
        out_specs=[tile(QW),
                   pl.BlockSpec((NB * BLK, 2 * KVW), lambda t: (jnp.maximum(t - 1, 0), 0)),
                   _row((NH, BLK, 2 * BLK)), _row((NH, 128)), _row((8, QW)),
                   pl.BlockSpec(memory_space=pl.ANY), _row((8, D))],
        scratch_shapes=[pltpu.VMEM((BLK, 2 * KVW), F32), pltpu.VMEM(((NB - 1) * BLK, 2 * KVW), F32),
                        pltpu.VMEM((DFF, D), F32), pltpu.SemaphoreType.DMA], carry=carry)


def _rel_bias_grad(dbias, bucket):
    def body(db_ref, bk_ref, o_ref):
        bk = bk_ref[...]
        lane = lax.broadcasted_iota(jnp.int32, (1, 128), 1)
        for b in range(NBUCKET):
            sel = bk == b
            row = jnp.zeros((1, 128), F32)
            for h in range(NH):
                row = row + jnp.where(lane == h, jnp.sum(jnp.where(sel, db_ref[h], 0.0)), 0.0)
            o_ref[b:b + 1, :] = row

    return pl.pallas_call(
        body, name="rel_bias_grad",
        out_shape=jax.ShapeDtypeStruct((NBUCKET, 128), F32),
    )(dbias, bucket)


def _ssd_consts():
    head_of_lane = np.arange(SW) // HD
    expand = (np.arange(128)[:, None] == head_of_lane[None, :]).astype(np.float32)
    tril = np.tril(np.ones((BLK, BLK), np.float32))
    return (jnp.asarray(expand, BF16), jnp.asarray(expand.T.copy(), BF16), jnp.asarray(tril, BF16),
            jnp.asarray(tril.T.copy(), BF16))


def _conv_pre(xc, halo, cw, cb):
    ext = jnp.concatenate([halo, xc], axis=0)
    taps = [xc if k == CK - 1 else pltpu.roll(ext, CK - 1 - k, 0)[8:8 + BLK] for k in range(CK)]
    return cb + sum(cw[k:k + 1, :] * taps[k] for k in range(CK))


def _ssd_chunk(pre, dtr, dtb, av, dkv, ex, tril, h_in):
    sp = _sig(pre)
    xbc = pre * sp
    xs, bm, cm = xbc[:, 0:SW], xbc[:, SW:SW + 2 * NST], xbc[:, SW + 2 * NST:]
    dtin = dtr + dtb
    dt = jnp.maximum(dtin, 0.0) + jnp.log1p(jnp.exp(-jnp.abs(dtin)))
    cs = _sel_l(tril, dt * av)
    cst = cs.T
    dtx = _sel_r(dt, ex)
    csx = _sel_r(cs, ex)
    xdt = xs * dtx
    csl = csx[BLK - 1:BLK, :]
    decx = jnp.exp(csl - csx)
    ecsx = jnp.exp(csx)
    ecl = jnp.exp(csl)
    causal = tril.astype(F32) > 0.5
    ydiag, yoff, cbs, lms = [], [], [], []
    for g in range(2):
        bg = bm[:, g * NST:(g + 1) * NST].astype(BF16)
        cg = cm[:, g * NST:(g + 1) * NST].astype(BF16)
        cb = _mm_nt(cg, bg)
        cbs.append(cb)
        yoff.append(_mm(cg, h_in[:, g * 256:(g + 1) * 256].astype(BF16)))
        for r in range(4):
            h = 4 * g + r
            seg = cs[:, h:h + 1] - cst[h:h + 1, :]
            lm = jnp.where(causal, jnp.exp(jnp.minimum(seg, 0.0)), 0.0)
            lms.append(lm)
            ydiag.append(_mm((cb * lm).astype(BF16), xdt[:, h * HD:(h + 1) * HD].astype(BF16)))
    yoff = jnp.concatenate(yoff, axis=1) * ecsx
    y = jnp.concatenate(ydiag, axis=1) + yoff + dkv * xs
    return dict(pre=pre, sp=sp, xs=xs, bm=bm, cm=cm, dtin=dtin, dt=dt, av=av, cs=cs, cst=cst,
                dtx=dtx, csx=csx, xdt=xdt, decx=decx, ecsx=ecsx, ecl=ecl, causal=causal, cbs=cbs, lms=lms,
                yoff=yoff, y=y)


def _group_mean(t):
    m0 = jnp.mean(t[:, 0:256], axis=-1, keepdims=True)
    m1 = jnp.mean(t[:, 256:512], axis=-1, keepdims=True)
    return jnp.concatenate([jnp.broadcast_to(m0, (t.shape[0], 256)), jnp.broadcast_to(m1, (t.shape[0], 256))], axis=1)


SUBS = 4


def _ssd_fwd(z, xbc, dtr, cw, cb, dtb, av, dk, nw, carry=None):
    s = z.shape[0]
    nc = s // BLK
    tile = SUBS * BLK
    ex, _, tril, _ = _ssd_consts()

    def body(z_ref, xc_ref, xh_ref, dtr_ref, cw_ref, cb_ref, dtb_ref, a_ref, dk_ref, nw_ref, ex_ref, tril_ref,
             y_ref, hs_ref, pre_ref, h_ref):
        t = pl.program_id(0)

        @pl.when(t == 0)
        def _():
            h_ref[...] = jnp.zeros_like(h_ref)

        h_in = h_ref[...]
        for sub in range(SUBS):
            rows = slice(sub * BLK, (sub + 1) * BLK)
            xc = xc_ref[rows, :]
            halo = jnp.where(t == 0, 0.0, xh_ref[...]) if sub == 0 else xc_ref[sub * BLK - 8:sub * BLK, :]
            pre = _conv_pre(xc, halo, cw_ref[...], cb_ref[...])
            pre_ref[rows, :] = pre
            hs_ref[sub] = h_in
            f = _ssd_chunk(pre, dtr_ref[rows, :], dtb_ref[...], a_ref[...], dk_ref[...], ex_ref[...], tril_ref[...], h_in)
            dx = (f["decx"] * f["xdt"]).astype(BF16)
            st = [_mm_tn(f["bm"][:, g * NST:(g + 1) * NST].astype(BF16), dx[:, g * 256:(g + 1) * 256]) for g in range(2)]
            h_in = h_in * f["ecl"] + jnp.concatenate(st, axis=1)
            zv = z_ref[rows, :]
            tg = f["y"] * (zv * _sig(zv))
            r = lax.rsqrt(_group_mean(tg * tg) + EPS)
            y_ref[rows, :] = (tg * r * nw_ref[...]).astype(BF16)
        h_ref[...] = h_in

    cur = lambda w: pl.BlockSpec((tile, w), lambda t: (t, 0))
    return _pcall(
        body, [z, xbc, xbc, dtr, cw, cb, dtb, av, dk, nw, ex, tril], name="ssd_fwd", grid=(s // tile,),
        out_shape=[jax.ShapeDtypeStruct((s, SW), BF16), jax.ShapeDtypeStruct((nc, NST, SW), F32),
                   jax.ShapeDtypeStruct((s, XBCW), F32)],
        in_specs=[cur(SW), cur(XBCW), pl.BlockSpec((8, XBCW), lambda t: (jnp.maximum(t * (tile // 8) - 1, 0), 0)),
                  cur(128), _row((8, XBCW)), _row((1, XBCW)), _row((1, 128)),
                  _row((1, 128)), _row((1, SW)), _row((1, SW)), _row((128, SW)), _row((BLK, BLK))],
        out_specs=[cur(SW), pl.BlockSpec((SUBS, NST, SW), lambda t: (t, 0, 0)), cur(XBCW)],
        scratch_shapes=[pltpu.VMEM((NST, SW), F32)], carry=carry)


def _ssd_bwd(z, xbc, pre_all, dtr, dys, hs, cw, dtb, av, dk, nw, carry=None):
    s = z.shape[0]
    tile = SUBS * BLK
    nt = s // tile
    ex, ext_t, tril, triu = _ssd_consts()

    def body(z_ref, xc_ref, pre_ref, dtr_ref, dy_ref, hs_ref, cw_ref, dtb_ref, a_ref, dk_ref, nw_ref,
             ex_ref, ext_ref, tril_ref, triu_ref,
             dz_ref, dxbc_ref, ddt_ref, dcw_ref, dcb_ref, dnw_ref, dhd_ref, dh_ref, nxt_ref, dd_ref):
        i = pl.program_id(0)

        @pl.when(i == 0)
        def _():
            dh_ref[...] = jnp.zeros_like(dh_ref)
            nxt_ref[...] = jnp.zeros_like(nxt_ref)
            dd_ref[...] = jnp.zeros_like(dd_ref)
            dcw_ref[...] = jnp.zeros_like(dcw_ref)
            dcb_ref[...] = jnp.zeros_like(dcb_ref)
            dnw_ref[...] = jnp.zeros_like(dnw_ref)
            dhd_ref[...] = jnp.zeros_like(dhd_ref)

        gst, nxt = dh_ref[...], nxt_ref[...]
        for sub in reversed(range(SUBS)):
            rows = slice(sub * BLK, (sub + 1) * BLK)
            gst, nxt = chunk(sub, rows, gst, nxt, z_ref, xc_ref, pre_ref, dtr_ref, dy_ref, hs_ref, cw_ref, dtb_ref,
                             a_ref, dk_ref, nw_ref, ex_ref, ext_ref, tril_ref, triu_ref,
                             dz_ref, dxbc_ref, ddt_ref, dcw_ref, dcb_ref, dnw_ref, dhd_ref, dd_ref)
        dh_ref[...] = gst
        nxt_ref[...] = nxt

        @pl.when(i == nt - 1)
        def _():
            dhd_ref[2:3, :] = _sel_r(dd_ref[...], ext_ref[...])[0:1, :]

    def chunk(sub, rows, gst, nxt, z_ref, xc_ref, pre_ref, dtr_ref, dy_ref, hs_ref, cw_ref, dtb_ref,
              a_ref, dk_ref, nw_ref, ex_ref, ext_ref, tril_ref, triu_ref,
              dz_ref, dxbc_ref, ddt_ref, dcw_ref, dcb_ref, dnw_ref, dhd_ref, dd_ref):
        h_in = hs_ref[sub]
        f = _ssd_chunk(pre_ref[rows, :], dtr_ref[rows, :], dtb_ref[...], a_ref[...], dk_ref[...], ex_ref[...],
                       tril_ref[...], h_in)
        xs, xdt, decx, ecsx, ecl, dtx = f["xs"], f["xdt"], f["decx"], f["ecsx"], f["ecl"], f["dtx"]
        cs, cst, causal = f["cs"], f["cst"], f["causal"]
        causal_t = triu_ref[...].astype(F32) > 0.5

        zv = z_ref[rows, :]
        sz = _sig(zv)
        gz = zv * sz
        t = f["y"] * gz
        r = lax.rsqrt(_group_mean(t * t) + EPS)
        tn_ = t * r
        dyn = dy_ref[rows, :]
        dnw_ref[0:1, :] += jnp.sum(dyn * tn_, axis=0, keepdims=True)
        u = dyn * nw_ref[...]
        dt_ = r * u - tn_ * (r * _group_mean(u * tn_))
        dy = dt_ * gz
        dz_ref[rows, :] = (dt_ * f["y"] * (sz * (1.0 + zv * (1.0 - sz)))).astype(BF16)

        dd_ref[0:1, :] += jnp.sum(dy * xs, axis=0, keepdims=True)
        dxs = dk_ref[...] * dy

        edy = ecsx * dy
        dxdt, dbs, dcs_, dcsx_parts, dh_new = [], [], [], [], []
        lane = lax.broadcasted_iota(jnp.int32, (1, 128), 1)
        dcs_intra = jnp.zeros((BLK, 128), F32)
        for g in range(2):
            sl = slice(g * 256, (g + 1) * 256)
            bgf, cgf = f["bm"][:, g * NST:(g + 1) * NST], f["cm"][:, g * NST:(g + 1) * NST]
            bg, cg = bgf.astype(BF16), cgf.astype(BF16)
            gg = gst[:, sl].astype(BF16)
            hg = h_in[:, sl].astype(BF16)
            edyg = edy[:, sl].astype(BF16)
            dc = _mm_nt(edyg, hg)
            dh_new.append(gst[:, sl] * ecl[:, sl] + _mm_tn(cg, edyg))
            bgm = _mm(bg, gg)
            dxdt_g = decx[:, sl] * bgm
            dxg = (decx[:, sl] * xdt[:, sl]).astype(BF16)
            db = _mm_nt(dxg, gg)
            qd = bgm * xdt[:, sl] * decx[:, sl]
            last = jnp.sum(qd, axis=0, keepdims=True) + ecl[:, sl] * jnp.sum(gst[:, sl] * h_in[:, sl], axis=0, keepdims=True)
            rowid = lax.broadcasted_iota(jnp.int32, (BLK, 256), 0)
            dcsx_parts.append(f["yoff"][:, sl] * dy[:, sl] - qd + jnp.where(rowid == BLK - 1, last, 0.0))
            cb_ = f["cbs"][g]
            cbt = _mm_nt(bg, cg)
            dcb_ = jnp.zeros((BLK, BLK), F32)
            dcbt = jnp.zeros((BLK, BLK), F32)
            dxd = []
            for r_ in range(4):
                h = 4 * g + r_
                hl = slice(h * HD, (h + 1) * HD)
                lm = f["lms"][h]
                segt = cst[h:h + 1, :] - cs[:, h:h + 1]
                lmt = jnp.where(causal_t, jnp.exp(jnp.minimum(segt, 0.0)), 0.0)
                dyh = dy[:, hl].astype(BF16)
                xdh = xdt[:, hl].astype(BF16)
                dw = _mm_nt(dyh, xdh)
                dwt = _mm_nt(xdh, dyh)
                wt = cbt * lmt
                dxd.append(_mm(wt.astype(BF16), dyh))
                dcb_ = dcb_ + dw * lm
                dcbt = dcbt + dwt * lmt
                col = jnp.sum(dw * (cb_ * lm), axis=-1, keepdims=True) - jnp.sum(dwt * wt, axis=-1, keepdims=True)
                dcs_intra = dcs_intra + jnp.where(lane == h, col, 0.0)
            dxdt.append(dxdt_g + jnp.concatenate(dxd, axis=1))
            dcs_.append(dc + _mm(dcb_.astype(BF16), bg))
            dbs.append(db + _mm(dcbt.astype(BF16), cg))
        dxdt = jnp.concatenate(dxdt, axis=1)
        dxs = dxs + dxdt * dtx
        ext_t_ = ext_ref[...]
        dcs = dcs_intra + _sel_r(jnp.concatenate(dcsx_parts, axis=1), ext_t_)
        da = _sel_l(triu_ref[...], dcs)
        ddt = da * f["av"] + _sel_r(dxdt * xs, ext_t_)
        dhd_ref[1:2, :] += jnp.sum(da * f["dt"], axis=0, keepdims=True)
        ddtr = ddt * _sig(f["dtin"])
        dhd_ref[0:1, :] += jnp.sum(ddtr, axis=0, keepdims=True)
        ddt_ref[rows, :] = ddtr.astype(BF16)

        sp, pre = f["sp"], f["pre"]
        dact = jnp.concatenate([dxs] + dbs + dcs_, axis=1)
        dpre = dact * (sp * (1.0 + pre * (1.0 - sp)))
        dcb_ref[0:1, :] += jnp.sum(dpre, axis=0, keepdims=True)
        ext2 = jnp.concatenate([dpre, nxt], axis=0)
        shifted = [pltpu.roll(ext2, BLK + 8 - (CK - 1 - k), 0)[0:BLK] for k in range(CK - 1)] + [dpre]
        cw = cw_ref[...]
        xc = xc_ref[rows, :]
        dxr = cw[CK - 1:CK, :] * dpre
        for k in range(CK):
            dcw_ref[k:k + 1, :] += jnp.sum(shifted[k] * xc, axis=0, keepdims=True)
            if k < CK - 1:
                dxr = dxr + cw[k:k + 1, :] * shifted[k]
        dxbc_ref[rows, :] = dxr.astype(BF16)
        return jnp.concatenate(dh_new, axis=1), dpre[0:8]

    cur = lambda w: pl.BlockSpec((tile, w), lambda i: (nt - 1 - i, 0))
    return _pcall(
        body, [z, xbc, pre_all, dtr, dys, hs, cw, dtb, av, dk, nw, ex, ext_t, tril, triu], name="ssd_bwd", grid=(nt,),
        out_shape=[jax.ShapeDtypeStruct((s, SW), BF16), jax.ShapeDtypeStruct((s, XBCW), BF16),
                   jax.ShapeDtypeStruct((s, 128), BF16), jax.ShapeDtypeStruct((8, XBCW), F32),
                   jax.ShapeDtypeStruct((8, XBCW), F32), jax.ShapeDtypeStruct((8, SW), F32),
                   jax.ShapeDtypeStruct((8, 128), F32)],
        in_specs=[cur(SW), cur(XBCW), cur(XBCW), cur(128), cur(SW),
                  pl.BlockSpec((SUBS, NST, SW), lambda i: (nt - 1 - i, 0, 0)),
                  _row((8, XBCW)), _row((1, 128)), _row((1, 128)), _row((1, SW)), _row((1, SW)),
                  _row((128, SW)), _row((SW, 128)), _row((BLK, BLK)), _row((BLK, BLK))],
        out_specs=[cur(SW), cur(XBCW), cur(128), _row((8, XBCW)), _row((8, XBCW)), _row((8, SW)), _row((8, 128))],
        scratch_shapes=[pltpu.VMEM((NST, SW), F32), pltpu.VMEM((8, XBCW), F32), pltpu.VMEM((8, SW), F32)], carry=carry)


def _load_once(i, pairs, sem):
    @pl.when(i == 0)
    def _():
        cps = [pltpu.make_async_copy(src, dst, sem.at[k]) for k, (src, dst) in enumerate(pairs)]
        for cp in cps:
            cp.start()
        for cp in cps:
            cp.wait()


def _mlp_fwd(x, ya, ys, tgt, w_o, w_ga, w_gb, w_dn, gate1, a2, sh2, gate2, fn):
    s = x.shape[0]
    sub_m, subs = 256, 2
    tm = sub_m * subs

    def body(x_ref, ya_ref, ys_ref, t_ref, wo_hbm, wga_hbm, wgb_hbm, wdn_hbm, g1_ref, a2_ref, s2_ref, g2_ref, fn_ref,
             x1_ref, gu_ref, dx2_ref, loss_ref, dfn_ref, wo, wga, wgb, wdn, sem):
        i = pl.program_id(0)
        _load_once(i, [(wo_hbm, wo), (wga_hbm, wga), (wgb_hbm, wgb), (wdn_hbm, wdn)], sem)

        @pl.when(i == 0)
        def _():
            loss_ref[...] = jnp.zeros_like(loss_ref)
            dfn_ref[...] = jnp.zeros_like(dfn_ref)

        def proj(st):
            st["mix"] = _mm(ya_ref[st["rows"], :], wo[0:QW, :]) + _mm(ys_ref[st["rows"], :], wo[QW:D, :])

        def norm(st):
            x1 = x_ref[st["rows"], :] + g1_ref[...] * st.pop("mix")
            x1_ref[st["rows"], :] = x1
            r2 = lax.rsqrt(jnp.mean(x1 * x1, axis=-1, keepdims=True) + EPS)
            st["x1"] = x1
            st["h2"] = (x1 * r2 * a2_ref[...] + s2_ref[...]).astype(BF16)

        def gate_up(st):
            h2 = st.pop("h2")
            ha, hb = h2[:, 0:D // 2], h2[:, D // 2:D]
            gub = jnp.concatenate([(_mm(ha, wga[j]) + _mm(hb, wgb[j])).astype(BF16) for j in range(4)], axis=1)
            gu_ref[st["rows"], :] = gub
            st["gub"] = gub

        def activate(st):
            gub = st.pop("gub")
            gv, uv = gub[:, 0:DFF].astype(F32), gub[:, DFF:].astype(F32)
            st["act"] = (gv * _sig(gv) * uv).astype(BF16)

        def down(st):
            st["ff"] = _mm(st.pop("act"), wdn[...])

        def head(st):
            x2 = st.pop("x1") + g2_ref[...] * st.pop("ff")
            r3 = lax.rsqrt(jnp.mean(x2 * x2, axis=-1, keepdims=True) + EPS)
            xn = x2 * r3
            fnv = fn_ref[...]
            err = xn * fnv - t_ref[st["rows"], :]
            st["loss"] = jnp.sum(err * err) * (0.5 / D)
            dy = err * (1.0 / D)
            st["dfn"] = jnp.sum(dy * xn, axis=0, keepdims=True)
            u = dy * fnv
            dx2_ref[st["rows"], :] = r3 * u - xn * (r3 * jnp.mean(u * xn, axis=-1, keepdims=True))

        a, b = [dict(rows=slice(k * sub_m, (k + 1) * sub_m)) for k in range(subs)]
        for stage, st in [(proj, a), (norm, a), (proj, b), (gate_up, a), (norm, b), (activate, a), (gate_up, b),
                          (down, a), (activate, b), (head, a), (down, b), (head, b)]:
            stage(st)
        loss_ref[...] += a["loss"] + b["loss"]
        dfn_ref[0:1, :] += a["dfn"] + b["dfn"]

    def tok(w):
        return pl.BlockSpec((tm, w), lambda i: (i, 0))

    hbm = pl.BlockSpec(memory_space=pl.ANY)
    return pl.pallas_call(
        body, name="mlp_fwd", grid=(s // tm,),
        out_shape=[jax.ShapeDtypeStruct((s, D), F32), jax.ShapeDtypeStruct((s, 2 * DFF), BF16),
                   jax.ShapeDtypeStruct((s, D), F32), jax.ShapeDtypeStruct((8, 128), F32),
                   jax.ShapeDtypeStruct((8, D), F32)],
        in_specs=[tok(D), tok(QW), tok(SW), tok(D), hbm, hbm, hbm, hbm,
                  _row((1, D)), _row((1, D)), _row((1, D)), _row((1, D)), _row((1, D))],
        out_specs=[tok(D), tok(2 * DFF), tok(D), _row((8, 128)), _row((8, D))],
        scratch_shapes=[pltpu.VMEM((D, D), BF16), pltpu.VMEM(w_ga.shape, BF16), pltpu.VMEM(w_gb.shape, BF16),
                        pltpu.VMEM((DFF, D), BF16), pltpu.SemaphoreType.DMA((4,))],
        compiler_params=_cp(("arbitrary",)),
    )(x, ya, ys, tgt, w_o, w_ga, w_gb, w_dn, gate1, a2, sh2, gate2, fn)


def _mlp_bwd(x1, gu, dx2, w_o, w_ga, w_gb, w_dn, gate1, a2, sh2, gate2):
    s = x1.shape[0]
    tm = 256
    nj = 2 * DFF // 4

    def body(x1_ref, gu_ref, dx2_ref, wo_hbm, wga_hbm, wgb_hbm, wdn_hbm, g1_ref, a2_ref, s2_ref, g2_ref,
             dx1_ref, dya_ref, dys_ref, act_ref, dgu_ref, h2_ref, dsh_ref, p_ref, wo, wga, wgb, wdn, sem):
        i = pl.program_id(0)
        _load_once(i, [(wo_hbm, wo), (wga_hbm, wga), (wgb_hbm, wgb), (wdn_hbm, wdn)], sem)

        @pl.when(i == 0)
        def _():
            dsh_ref[...] = jnp.zeros_like(dsh_ref)
            p_ref[...] = jnp.zeros_like(p_ref)

        dx2 = dx2_ref[...]
        dact = _mm_nt((dx2 * g2_ref[...]).astype(BF16), wdn[...])
        gub = gu_ref[...]
        gv, uv = gub[:, 0:DFF].astype(F32), gub[:, DFF:].astype(F32)
        sg = _sig(gv)
        sl = gv * sg
        act_ref[...] = (sl * uv).astype(BF16)
        dgu = jnp.concatenate([dact * uv * (sg * (1.0 + gv * (1.0 - sg))), dact * sl], axis=1).astype(BF16)
        dgu_ref[...] = dgu
        dha = sum(_mm_nt(dgu[:, j * nj:(j + 1) * nj], wga[j]) for j in range(4))
        dhb = sum(_mm_nt(dgu[:, j * nj:(j + 1) * nj], wgb[j]) for j in range(4))
        dh = jnp.concatenate([dha, dhb], axis=1)
        x1 = x1_ref[...]
        r2 = lax.rsqrt(jnp.mean(x1 * x1, axis=-1, keepdims=True) + EPS)
        xn = x1 * r2
        a2 = a2_ref[...]
        h2_ref[...] = (xn * a2 + s2_ref[...]).astype(BF16)
        dsh_ref[0:1, :] += jnp.sum(dh, axis=0, keepdims=True)
        p_ref[0:1, :] += jnp.sum(dh * xn, axis=0, keepdims=True)
        u = dh * a2
        dx1 = dx2 + r2 * u - xn * (r2 * jnp.mean(u * xn, axis=-1, keepdims=True))
        dx1_ref[...] = dx1
        dcat = _mm_nt((dx1 * g1_ref[...]).astype(BF16), wo[...])
        dya_ref[...] = dcat[:, 0:QW]
        dys_ref[...] = dcat[:, QW:D]

    def tok(w):
        return pl.BlockSpec((tm, w), lambda i: (i, 0))

    hbm = pl.BlockSpec(memory_space=pl.ANY)
    return pl.pallas_call(
        body, name="mlp_bwd", grid=(s // tm,),
        out_shape=[jax.ShapeDtypeStruct((s, D), F32), jax.ShapeDtypeStruct((s, QW), F32),
                   jax.ShapeDtypeStruct((s, SW), F32), jax.ShapeDtypeStruct((s, DFF), BF16),
                   jax.ShapeDtypeStruct((s, 2 * DFF), BF16), jax.ShapeDtypeStruct((s, D), BF16),
                   jax.ShapeDtypeStruct((8, D), F32), jax.ShapeDtypeStruct((8, D), F32)],
        in_specs=[tok(D), tok(2 * DFF), tok(D), hbm, hbm, hbm, hbm, _row((1, D)), _row((1, D)), _row((1, D)), _row((1, D))],
        out_specs=[tok(D), tok(QW), tok(SW), tok(DFF), tok(2 * DFF), tok(D), _row((8, D)), _row((8, D))],
        scratch_shapes=[pltpu.VMEM((D, D), BF16), pltpu.VMEM(w_ga.shape, BF16), pltpu.VMEM(w_gb.shape, BF16),
                        pltpu.VMEM((DFF, D), BF16), pltpu.SemaphoreType.DMA((4,))],
        compiler_params=_cp(("arbitrary",)),
    )(x1, gu, dx2, w_o, w_ga, w_gb, w_dn, gate1, a2, sh2, gate2)


def _wgrad(name, a, b, gate, w, carry=None):
    s, m = a.shape
    n = b.shape[1]
    tk = min(1024, s)
    nk = s // tk

    def body(a_ref, b_ref, g_ref, w_ref, o_hbm, dg_ref, acc_ref, sem):
        k = pl.program_id(0)

        @pl.when(k == 0)
        def _():
            acc_ref[...] = jnp.zeros_like(acc_ref)

        acc_ref[...] += _mm_tn(a_ref[...], b_ref[...].astype(BF16))

        @pl.when(k == nk - 1)
        def _():
            acc = acc_ref[...]
            dg_ref[...] = jnp.zeros_like(dg_ref)
            dg_ref[0:1, :] = jnp.sum(acc * w_ref[...].astype(F32), axis=0, keepdims=True)
            acc_ref[...] = acc * g_ref[...]
            cp = pltpu.make_async_copy(acc_ref, o_hbm, sem)
            cp.start()
            cp.wait()

    return _pcall(body, [a, b, gate, w], name=name, grid=(nk,),
                  out_shape=[jax.ShapeDtypeStruct((m, n), F32), jax.ShapeDtypeStruct((8, n), F32)],
                  in_specs=[pl.BlockSpec((tk, m), lambda k: (k, 0)), pl.BlockSpec((tk, n), lambda k: (k, 0)),
                            _row((1, n)), _row((m, n))],
                  out_specs=[pl.BlockSpec(memory_space=pl.ANY), _row((8, n))],
                  scratch_shapes=[pltpu.VMEM((m, n), F32), pltpu.SemaphoreType.DMA], carry=carry)


def _wgrad_gate_up(h2, dgu, carry=None):
    s = h2.shape[0]
    tk = min(1024, s)
    nk = s // tk
    n = dgu.shape[1]
    nj = n // 4

    def body(a_ref, b_ref, o_hbm, acc_ref, sems):
        k = pl.program_id(0)

        @pl.when(k == 0)
        def _():
            acc_ref[...] = jnp.zeros_like(acc_ref)

        acc_ref[...] += _mm_tn(a_ref[...], b_ref[...])

        @pl.when(k == nk - 1)
        def _():
            cps = [pltpu.make_async_copy(acc_ref.at[:, pl.ds(j * nj, nj)], o_hbm.at[j], sems.at[j]) for j in range(4)]
            for cp in cps:
                cp.start()
            for cp in cps:
                cp.wait()

    return _pcall(body, [h2, dgu], name="wgrad_gate_up", grid=(nk,),
                  out_shape=[jax.ShapeDtypeStruct((4, D, nj), F32)],
                  in_specs=[pl.BlockSpec((tk, D), lambda k: (k, 0)), pl.BlockSpec((tk, n), lambda k: (k, 0))],
                  out_specs=[pl.BlockSpec(memory_space=pl.ANY)],
                  scratch_shapes=[pltpu.VMEM((D, n), F32), pltpu.SemaphoreType.DMA((4,))], carry=carry)


def _wgrad_in_t(h1, pieces, carry=None):
    s = h1.shape[0]
    tk = min(1024, s)
    nk = s // tk

    def body(a_ref, dq_ref, dkv_ref, dz_ref, dxbc_ref, ddt_ref, o_hbm, acc_ref, tr_ref, sem):
        k = pl.program_id(0)

        @pl.when(k == 0)
        def _():
            acc_ref[...] = jnp.zeros_like(acc_ref)

        dproj = jnp.concatenate([dq_ref[...], dkv_ref[...], dz_ref[...], dxbc_ref[...], ddt_ref[...]], axis=1)
        acc_ref[...] += _mm_tn(a_ref[...], dproj)

        @pl.when(k == nk - 1)
        def _():
            for j in range(PROJ_W // 128):
                tr_ref[j * 128:(j + 1) * 128, :] = acc_ref[:, j * 128:(j + 1) * 128].T
            cp = pltpu.make_async_copy(tr_ref, o_hbm, sem)
            cp.start()
            cp.wait()

    return _pcall(body, [h1] + list(pieces), name="wgrad_in", grid=(nk,),
                  out_shape=[jax.ShapeDtypeStruct((PROJ_W, D), F32)],
                  in_specs=[pl.BlockSpec((tk, p.shape[1]), lambda k: (k, 0)) for p in [h1] + list(pieces)],
                  out_specs=[pl.BlockSpec(memory_space=pl.ANY)],
                  scratch_shapes=[pltpu.VMEM((D, PROJ_W), F32), pltpu.VMEM((PROJ_W, D), F32), pltpu.SemaphoreType.DMA],
                  carry=carry)


_SMALL = ["ada_b", "norm1", "conv_w", "conv_b", "dt_bias", "A_log", "D_skip", "sinks", "attn_out_norm",
          "ssm_out_norm", "norm2", "rel_bias", "final_norm"]


def _small_grad(name, gs, chip):
    if name == "ada_b":
        return jnp.concatenate([gs[j:j + 1, :] for j in range(6)], axis=1)
    if name == "conv_w":
        full = gs[7:11, :]
        out = full[:, 0:256]
        for j in range(1, 4):
            out = jnp.where(chip == j, full[:, j * 256:(j + 1) * 256], out)
        return out
    row, width = {"norm1": (6, D), "conv_b": (11, D), "norm2": (12, D), "final_norm": (13, D),
                  "attn_out_norm": (14, QW), "ssm_out_norm": (15, SW), "dt_bias": (16, NH), "A_log": (17, NH),
                  "D_skip": (18, NH), "sinks": (19, NH), "rel_bias": (24, NH)}[name]
    rows = NBUCKET if name == "rel_bias" else 1
    return gs[row:row + rows, 0:width]


def _small_update(small_all, where, ws, ms, vs):
    n = len(_SMALL)

    def body(where_ref, sa_ref, *refs):
        w_refs, m_refs, v_refs, outs = refs[:n], refs[n:2 * n], refs[2 * n:3 * n], refs[3 * n:]
        gs = sa_ref[0]
        for b in range(1, 8):
            gs = gs + sa_ref[b]
        chip = where_ref[1]
        for i, name in enumerate(_SMALL):
            g = _small_grad(name, gs, chip)
            lead = (0,) if name == "conv_w" else ()
            d, mo, vo = _adamw(w_refs[i][lead + (...,)], g, m_refs[i][lead + (...,)], v_refs[i][lead + (...,)])
            for k, val in enumerate((g, d, mo, vo)):
                outs[k * n + i][lead + (...,)] = val
        outs[4 * n][...] = gs[20:21, 0:128]

    shapes = [jax.ShapeDtypeStruct(w.shape, F32) for w in ws]
    vmem = pl.BlockSpec(memory_space=pltpu.VMEM)
    res = pl.pallas_call(
        body, name="small_update", out_shape=shapes * 4 + [jax.ShapeDtypeStruct((1, 128), F32)],
        in_specs=[pl.BlockSpec(memory_space=pltpu.SMEM)] + [vmem] * (1 + 3 * n), out_specs=[vmem] * (4 * n + 1),
    )(where, small_all, *ws, *ms, *vs)
    return [res[k * n:(k + 1) * n] for k in range(4)], res[4 * n][0, 0]


def _add_half(name, g, got, where, by_cols=False):
    rr, cc = got.shape[1:]
    if by_cols:
        mine = pl.BlockSpec((None, rr, cc), lambda i, w_ref: (i, 0, w_ref[0]))
    else:
        mine = pl.BlockSpec((None, None, rr, cc), lambda i, w_ref: (i, w_ref[0], 0, 0))

    def body(w_ref, g_ref, r_ref, o_ref, own_ref):
        s = g_ref[...] + r_ref[...]
        o_ref[...] = s.astype(BF16)

        @pl.when(pl.program_id(0) == w_ref[1])
        def _():
            own_ref[...] = s

    spec = pl.BlockSpec((None, rr, cc), lambda i, w_ref: (i, 0, 0))
    return _pcall(body, [where, g, got], name=name, grid=(4,), nprefetch=1,
                  out_shape=[jax.ShapeDtypeStruct(got.shape, BF16), jax.ShapeDtypeStruct((rr, cc), F32)],
                  in_specs=[mine, spec],
                  out_specs=[spec, pl.BlockSpec((rr, cc), lambda i, w_ref: (0, 0))])


def _add_chips(name, own, got):
    rr, cc = own.shape
    tr = rr // 2 if rr % 32 == 0 else rr

    def body(s_ref, r_ref, o_ref):
        o_ref[...] = ((s_ref[...] + r_ref[0].astype(F32)) + r_ref[1].astype(F32)) + r_ref[2].astype(F32)

    spec = pl.BlockSpec((tr, cc), lambda i: (i, 0))
    return _pcall(body, [own, got], name=name, grid=(rr // tr,), out_shape=[jax.ShapeDtypeStruct((rr, cc), F32)],
                  in_specs=[spec, pl.BlockSpec((3, tr, cc), lambda i: (0, i, 0))], out_specs=[spec])[0]


def _adamw_halves(name, mine, got, w, m, v, where, by_cols=False):
    rr, cc = mine.shape

    def body(w_ref_, t_ref, r_ref, w_ref, m_ref, v_ref, g_ref, d_ref, mo_ref, vo_ref):
        g = jnp.where(pl.program_id(0) == w_ref_[0], t_ref[...], r_ref[...])
        g_ref[...] = g
        d_ref[...], mo_ref[...], vo_ref[...] = _adamw(w_ref[...], g, m_ref[...], v_ref[...])

    if by_cols:
        grid = (2, 1)
        half = pl.BlockSpec((rr, cc), lambda h, i, w_ref_: (0, 0))
        full = pl.BlockSpec((rr, cc), lambda h, i, w_ref_: (0, h))
    else:
        tr = rr // 2
        grid = (2, 2)
        half = pl.BlockSpec((tr, cc), lambda h, i, w_ref_: (i, 0))
        full = pl.BlockSpec((None, tr, cc), lambda h, i, w_ref_: (0, 2 * h + i, 0))
    return _pcall(body, [where, mine, got, w, m, v], name=name, grid=grid, nprefetch=1,
                  out_shape=[jax.ShapeDtypeStruct(w.shape, F32)] * 4,
                  in_specs=[half, half, full, full, full], out_specs=[full] * 4)


def _bias_table(rel_bias, bucket, mask):
    def body(rb_ref, bk_ref, mk_ref, o_ref):
        bk = bk_ref[...]
        valid = mk_ref[...] > 0
        for h in range(NH):
            acc = jnp.zeros((BLK, 2 * BLK), F32)
            for b in range(NBUCKET):
                acc = jnp.where(bk == b, rb_ref[b, h], acc)
            o_ref[h] = jnp.where(valid, acc, NEG)

    vmem = pl.BlockSpec(memory_space=pltpu.VMEM)
    return pl.pallas_call(
        body, name="bias_table", out_shape=jax.ShapeDtypeStruct((NH, BLK, 2 * BLK), F32),
        in_specs=[pl.BlockSpec(memory_space=pltpu.SMEM), vmem, vmem], out_specs=vmem,
    )(rel_bias, bucket, mask)


def _pack_small(dsh1, p1, dsh2, p2, dg1a, dg1b, dg2, norm1, norm2, scale1, scale2, dcw, dcb, dfn,
                dnw_attn, dnw_ssm, dhd, av, dsink, drel, loss_acc):
    def body(dsh1_ref, p1_ref, dsh2_ref, p2_ref, dg1a_ref, dg1b_ref, dg2_ref, n1_ref, n2_ref, s1_ref, s2_ref,
             dcw_ref, dcb_ref, dfn_ref, da_ref, ds_ref, dhd_ref, av_ref, dsink_ref, drel_ref, loss_ref, o_ref):
        o_ref[...] = jnp.zeros_like(o_ref)
        p1v, p2v = p1_ref[0:1, :], p2_ref[0:1, :]
        o_ref[0:1, :] = dsh1_ref[0:1, :]
        o_ref[1:2, :] = p1v * n1_ref[...]
        o_ref[2:3, :] = dg1a_ref[0:1, :] + dg1b_ref[0:1, :]
        o_ref[3:4, :] = dsh2_ref[0:1, :]
        o_ref[4:5, :] = p2v * n2_ref[...]
        o_ref[5:6, :] = dg2_ref[0:1, :]
        o_ref[6:7, :] = p1v * (1.0 + s1_ref[...])
        o_ref[7:11, :] = dcw_ref[0:4, :]
        o_ref[11:12, :] = dcb_ref[0:1, :]
        o_ref[12:13, :] = p2v * (1.0 + s2_ref[...])
        o_ref[13:14, :] = dfn_ref[0:1, :]
        o_ref[14:15, 0:QW] = da_ref[0:1, :]
        o_ref[15:16, 0:SW] = ds_ref[0:1, :]
        o_ref[16:17, 0:128] = dhd_ref[0:1, :]
        o_ref[17:18, 0:128] = dhd_ref[1:2, :] * av_ref[...]
        o_ref[18:19, 0:128] = dhd_ref[2:3, :]
        o_ref[19:20, 0:128] = dsink_ref[0:1, :]
        o_ref[20:21, 0:128] = loss_ref[0:1, :]
        o_ref[24:56, 0:128] = drel_ref[...]

    return pl.pallas_call(body, name="pack_small", out_shape=jax.ShapeDtypeStruct((56, D), F32))(
        dsh1, p1, dsh2, p2, dg1a, dg1b, dg2, norm1, norm2, scale1, scale2, dcw, dcb, dfn,
        dnw_attn, dnw_ssm, dhd, av, dsink, drel, loss_acc)


def _pad_row(a, rows=1):
    return jnp.pad(a.reshape(rows, -1), ((0, 0), (0, D - a.size // rows)))


def kernel(x, c, ada_w, ada_b, norm1, w_in, conv_w, conv_b, dt_bias, A_log, D_skip, sinks, attn_out_norm, ssm_out_norm, w_o, norm2, w_gate_up, w_down, rel_bias, final_norm, loss_target, m_ada_w, m_ada_b, m_norm1, m_w_in, m_conv_w, m_conv_b, m_dt_bias, m_A_log, m_D_skip, m_sinks, m_attn_out_norm, m_ssm_out_norm, m_w_o, m_norm2, m_w_gate_up, m_w_down, m_rel_bias, m_final_norm, v_ada_w, v_ada_b, v_norm1, v_w_in, v_conv_w, v_conv_b, v_dt_bias, v_A_log, v_D_skip, v_sinks, v_attn_out_norm, v_ssm_out_norm, v_w_o, v_norm2, v_w_gate_up, v_w_down, v_rel_bias, v_final_norm):
    xi, yi, ci = lax.axis_index("x"), lax.axis_index("y"), lax.axis_index("c")
    chip = 2 * xi + yi
    me = 4 * xi + 2 * yi + ci
    where = jnp.stack([ci, chip]).astype(jnp.int32)
    xs2, tgt = x[0], loss_target[0]

    first = jnp.concatenate([c, _pad_row(conv_w[0], CK), jnp.zeros((3, D), F32)], axis=0)
    w_in_t, m_w_in_t, v_w_in_t = w_in[0].T, m_w_in[0].T, v_w_in[0].T
    w_in_b, w_o_b, w_dn_b = w_in_t.astype(BF16), w_o[0].astype(BF16), w_down[0].astype(BF16)
    w_gu_b = w_gate_up[0].astype(BF16)
    hw = D // 2
    fetch_half = _Carry(
        [w_in_b], [jax.ShapeDtypeStruct((4,) + w_in_b.shape, BF16)],
        lambda x_, y_, c_: [(None, 0, slice(None), 0, 2 * x_ + y_)] + [
            (f, 0, (slice(None), pl.ds(c_ * hw, hw)), 0, (2 * x_ + y_, slice(None), pl.ds(c_ * hw, hw))) for f in _CHIPS3])

    def swap_halves(x_, y_, c_):
        there = [(jnp.bitwise_xor(2 * x_ + y_, k + 1), slice(None), pl.ds(c_ * hw, hw)) for k in range(3)]
        return [(_SIBLING, 1, at, 1, at) for at in there]

    first_all, w_in_g = _exchange_two("gather_first", _merge(_gather8_carry(first), fetch_half), swap_halves)
    c_all = first_all[:, 0, :]
    cw_full = jnp.concatenate([first_all[2 * j, 1:1 + CK, 0:256] for j in range(4)], axis=1)
    w_in_f = jnp.pad(w_in_g.reshape(IN_W, D), ((0, PROJ_W - IN_W), (0, 0)))

    ncol = ada_w.shape[2]
    mod_cols = _ada_fwd(c_all, ada_w[0], lax.dynamic_slice(ada_b, (0, chip * ncol), (1, ncol)))
    mod_all = _exchange("gather_mod", _gather_chips_carry([mod_cols]))[0]
    mod = lax.dynamic_slice(jnp.transpose(mod_all, (1, 0, 2)).reshape(8, 4 * ncol), (me, 0), (1, 4 * ncol))
    shift1, scale1, gate1, shift2, scale2, gate2 = [mod[:, j * D:(j + 1) * D] for j in range(6)]
    a1 = norm1 * (1.0 + scale1)
    a2 = norm2 * (1.0 + scale2)

    hdn = DFF // 8
    q, kv, z, xbc, dtr, w_o_g, w_dna_g = _in_proj_fwd(xs2, a1, shift1, w_in_f,
                                                      carry=_gather_chips_carry([w_o_b, w_dn_b[0:hdn]]))
    w_o_f = w_o_g.reshape(D, D)
    bucket, mask = _attn_geometry()
    bucket = jnp.asarray(bucket)
    bias = _bias_table(rel_bias, bucket, jnp.asarray(mask.astype(np.int32)))
    sinks1 = sinks[0]
    ya, w_ga_g = _attn_fwd(q, kv, bias, sinks1, attn_out_norm, carry=_gather_chips_carry([w_gu_b[0:D // 2]]))
    cw8 = jnp.concatenate([cw_full, jnp.zeros((4, XBCW), F32)], axis=0)
    dtb = _pad_row(dt_bias)[:, 0:128]
    av = _pad_row(-jnp.exp(A_log))[:, 0:128]
    dk = jnp.repeat(D_skip, HD, axis=1)
    ys, hs, pre, w_gb_g, w_dnb_g = _ssd_fwd(z, xbc, dtr, cw8, conv_b, dtb, av, dk, ssm_out_norm,
                                            carry=_gather_chips_carry([w_gu_b[D // 2:D], w_dn_b[hdn:2 * hdn]]))
    w_dn_f = jnp.stack([w_dna_g, w_dnb_g], axis=1).reshape(DFF, D)
    fn = final_norm[None, :]
    x1, gu, dx2, loss_acc, dfn = _mlp_fwd(xs2, ya, ys, tgt, w_o_f, w_ga_g, w_gb_g, w_dn_f, gate1, a2, shift2, gate2, fn)

    def to_sibling(p):
        return _Carry([p], [jax.ShapeDtypeStruct((4,) + p.shape[2:], F32)],
                      lambda x_, y_, c_: [(_SIBLING, 0, (j, 1 - c_), 0, j) for j in range(4)])

    def to_chips(s4):
        return _Carry([s4], [jax.ShapeDtypeStruct((3,) + s4.shape[1:], s4.dtype)],
                      lambda x_, y_, c_: [(f, 0, jnp.bitwise_xor(2 * x_ + y_, k + 1), 0, k) for k, f in enumerate(_CHIPS3)])

    def back(t):
        return _Carry([t[None]], [jax.ShapeDtypeStruct((1,) + t.shape, F32)], lambda x_, y_, c_: [(_SIBLING, 0, 0, 0, 0)])

    dx1, dya, dys, act, dgu, h2, dsh2, p2 = _mlp_bwd(x1, gu, dx2, w_o_f, w_ga_g, w_gb_g, w_dn_f, gate1, a2, shift2, gate2)
    p_gu = _wgrad_gate_up(h2, dgu)[0].reshape(4, 2, D // 2, 2 * DFF // 4)
    dq, dkv, dbias, dsink, dnw_attn, g_dn, dg2, got1_gu = _attn_bwd(
        q, kv, dya, bias, sinks1, attn_out_norm, act, dx2, gate2, w_dn_f, carry=to_sibling(p_gu))
    p_dn = g_dn.reshape(4, 2, DFF // 8, D)
    drel = _rel_bias_grad(dbias, bucket)
    s4_gu, own_gu = _add_half("rs_add_half_gu", p_gu, got1_gu, where)
    dz, dxbc, ddt, dcw, dcb, dnw_ssm, dhd, got2_gu, got1_dn = _ssd_bwd(
        z, xbc, pre, dtr, dys, hs, cw8, dtb, av, dk, ssm_out_norm, carry=_merge(to_chips(s4_gu), to_sibling(p_dn)))
    mine_gu = _add_chips("rs_add_chips_gu", own_gu, got2_gu)
    s4_dn, own_dn = _add_half("rs_add_half_dn", p_dn, got1_dn, where)
    grad_x, h1, dsh1, p1 = _in_proj_bwd(xs2, dx1, a1, shift1, w_in_f, dq, dkv, dz, dxbc, ddt)
    g_in_t, got2_dn, got3_gu = _wgrad_in_t(h1, [dq, dkv, dz, dxbc, ddt],
                                           carry=_merge(to_chips(s4_dn), back(mine_gu)))
    mine_dn = _add_chips("rs_add_chips_dn", own_dn, got2_dn)
    p_in = g_in_t[0:IN_W].reshape(4, IN_W // 4, D)

    def to_sibling_cols(p):
        return _Carry([p], [jax.ShapeDtypeStruct(p.shape[:2] + (D // 2,), F32)],
                      lambda x_, y_, c_: [(_SIBLING, 0, (j, slice(None), pl.ds((1 - c_) * (D // 2), D // 2)), 0, j)
                                          for j in range(4)])

    g_oa, dg1a, got1_in, got3_dn = _wgrad("wgrad_o_attn", ya, dx1, gate1, w_o_f[0:QW],
                                          carry=_merge(to_sibling_cols(p_in), back(mine_dn)))
    s4_in, own_in = _add_half("rs_add_half_in", p_in, got1_in, where, by_cols=True)
    g_os, dg1b, got2_in = _wgrad("wgrad_o_ssm", ys, dx1, gate1, w_o_f[QW:D], carry=to_chips(s4_in))
    mine_in = _add_chips("rs_add_chips_in", own_in, got2_in)
    p_o = jnp.concatenate([g_oa, g_os], axis=0).reshape(4, 2, D // 8, D)

    small = _pack_small(dsh1, p1, dsh2, p2, dg1a, dg1b, dg2, norm1, norm2, scale1, scale2, dcw, dcb, dfn,
                        dnw_attn, dnw_ssm, dhd, av, dsink, drel, loss_acc)
    small_all, got1_o, got3_in = _exchange(
        "gather_small", _merge(_gather8_carry(small), to_sibling(p_o), back(mine_in)))
    s4_o, own_o = _add_half("rs_add_half_o", p_o, got1_o, where)
    mine_o = _add_chips("rs_add_chips_o", own_o, _exchange("rs_chips_o", to_chips(s4_o))[0])
    got3_o = _exchange("rs_back_o", back(mine_o))[0]
    small_res, loss = _small_update(
        small_all, where,
        [ada_b, norm1, conv_w, conv_b, dt_bias, A_log, D_skip, sinks, attn_out_norm, ssm_out_norm, norm2, rel_bias,
         final_norm[None, :]],
        [m_ada_b, m_norm1, m_conv_w, m_conv_b, m_dt_bias, m_A_log, m_D_skip, m_sinks, m_attn_out_norm,
         m_ssm_out_norm, m_norm2, m_rel_bias, m_final_norm[None, :]],
        [v_ada_b, v_norm1, v_conv_w, v_conv_b, v_dt_bias, v_A_log, v_D_skip, v_sinks, v_attn_out_norm,
         v_ssm_out_norm, v_norm2, v_rel_bias, v_final_norm[None, :]])
    small_out = [dict(zip(_SMALL, r)) for r in small_res]
    for r in small_out:
        r["final_norm"] = r["final_norm"][0]

    dmod_all = small_all[:, 0:6, :].reshape(8, 6 * D)
    dmod_loc = lax.dynamic_slice(dmod_all, (0, chip * ncol), (8, ncol))
    ada_out = _ada_bwd_adamw(c_all.T, dmod_loc, ada_w[0], m_ada_w[0], v_ada_w[0])

    big_gu = _adamw_halves("adamw_gate_up", mine_gu, got3_gu[0], w_gate_up, m_w_gate_up, v_w_gate_up, where)
    big_dn = _adamw_halves("adamw_down", mine_dn, got3_dn[0], w_down, m_w_down, v_w_down, where)
    big_o = _adamw_halves("adamw_o", mine_o, got3_o[0], w_o, m_w_o, v_w_o, where)
    big_in = [o.T[None] for o in _adamw_halves("adamw_in", mine_in, got3_in[0], w_in_t, m_w_in_t, v_w_in_t, where,
                                               by_cols=True)]
    big = [big_in, big_o, big_gu, big_dn]

    order = ["ada_w", "ada_b", "norm1", "w_in", "conv_w", "conv_b", "dt_bias", "A_log", "D_skip", "sinks",
             "attn_out_norm", "ssm_out_norm", "w_o", "norm2", "w_gate_up", "w_down", "rel_bias", "final_norm"]
    bigname = {"w_in": 0, "w_o": 1, "w_gate_up": 2, "w_down": 3}
    res = [loss, grad_x[None]]
    for kind in range(4):
        for nm in order:
            if nm == "ada_w":
                res.append(ada_out[kind][None])
            elif nm in bigname:
                res.append(big[bigname[nm]][kind])
            else:
                res.append(small_out[kind][nm])
    return tuple(res)
```

```python
import functools

import numpy as np
import jax
import jax.numpy as jnp
from jax import lax
from jax.experimental import pallas as pl
from jax.experimental.pallas import tpu as pltpu

F32, BF16 = jnp.float32, jnp.bfloat16
HI = lax.Precision.HIGHEST

D = 1024
QW, KVW = 512, 128
NH, HD, NKV = 8, 64, 2
SW = 512
NST = 128
XBCW = 1024
CK = 4
BLK = 128
DFF = 2816
IN_W = 2312
PROJ_W = 2432
EPS = 1e-6
NEG = -1e30
NBUCKET = 32

B1, B2, LR, AEPS, WD, STEP = 0.9, 0.999, 0.001, 1e-08, 0.01, 10

VMEM_LIMIT = 56 * 1024 * 1024

_NT = (((1,), (1,)), ((), ()))
_TN = (((0,), (0,)), ((), ()))


def _mm(a, b):
    return jnp.dot(a, b, preferred_element_type=F32)


def _mm_nt(a, b):
    return lax.dot_general(a, b, _NT, preferred_element_type=F32)


def _mm_tn(a, b):
    return lax.dot_general(a, b, _TN, preferred_element_type=F32)


def _mm_hi(a, b):
    return jnp.dot(a, b, preferred_element_type=F32, precision=HI)


def _split3(x):
    hi = x.astype(BF16)
    r = x - hi.astype(F32)
    mid = r.astype(BF16)
    lo = (r - mid.astype(F32)).astype(BF16)
    return hi, mid, lo


def _sel_r(x, e):
    hi, mid, lo = _split3(x)
    return (_mm(hi, e) + _mm(mid, e)) + _mm(lo, e)


def _sel_l(e, x):
    hi, mid, lo = _split3(x)
    return (_mm(e, hi) + _mm(e, mid)) + _mm(e, lo)


def _sig(x):
    return 1.0 / (1.0 + jnp.exp(-x))


def _cp(sem):
    return pltpu.CompilerParams(dimension_semantics=sem, vmem_limit_bytes=VMEM_LIMIT)


def _row(shape):
    nd = len(shape)
    return pl.BlockSpec(shape, lambda *_: (0,) * nd)


def _adamw(w, g, m, v):
    m = B1 * m + (1.0 - B1) * g
    v = B2 * v + (1.0 - B2) * (g * g)
    m_hat = m / (1.0 - B1 ** STEP)
    v_hat = v / (1.0 - B2 ** STEP)
    delta = -LR * (m_hat / (jnp.sqrt(v_hat) + AEPS) + WD * w)
    return delta, m, v


class _Carry:
    def __init__(self, inps, outs, copies):
        self.inps, self.outs, self.copies = list(inps), list(outs), copies
        self.n = len(copies(0, 0, 0))

    def descriptors(self, in_refs, out_refs, send_sems, recv_sems):
        x, y, c = lax.axis_index("x"), lax.axis_index("y"), lax.axis_index("c")
        out = []
        for j, (flip, a, si, o, di) in enumerate(self.copies(x, y, c)):
            if flip is None:
                out.append(pltpu.make_async_copy(in_refs[a].at[si], out_refs[o].at[di], send_sems.at[j]))
            else:
                fx, fy, fc = flip
                peer = (1 - x if fx else x, 1 - y if fy else y, 1 - c if fc else c)
                out.append(pltpu.make_async_remote_copy(
                    src_ref=in_refs[a].at[si], dst_ref=out_refs[o].at[di],
                    send_sem=send_sems.at[j], recv_sem=recv_sems.at[j],
                    device_id=peer, device_id_type=pl.DeviceIdType.MESH))
        return out


def _pcall(body, args, *, name, grid, in_specs, out_specs, out_shape, scratch_shapes=(), sem=None, nprefetch=0,
           carry=None):
    out_shape, out_specs = list(out_shape), list(out_specs)
    in_specs, scratch_shapes = list(in_specs), list(scratch_shapes)
    nin, nout, nscr = len(in_specs), len(out_shape), len(scratch_shapes)
    run = body
    if carry is not None:
        ncin, ncout = len(carry.inps), len(carry.outs)
        hbm = pl.BlockSpec(memory_space=pl.ANY)

        def run(*refs):
            pre, r = refs[:nprefetch], refs[nprefetch:]
            ins, cins = r[:nin], r[nin:nin + ncin]
            r = r[nin + ncin:]
            outs, couts = r[:nout], r[nout:nout + ncout]
            r = r[nout + ncout:]
            scr, (send_sems, recv_sems) = r[:nscr], r[nscr:]
            first = pl.program_id(0) == 0
            last = pl.program_id(0) == grid[0] - 1
            for ax in range(1, len(grid)):
                first = jnp.logical_and(first, pl.program_id(ax) == 0)
                last = jnp.logical_and(last, pl.program_id(ax) == grid[ax] - 1)

            @pl.when(first)
            def _():
                for d in carry.descriptors(cins, couts, send_sems, recv_sems):
                    d.start()

            body(*pre, *ins, *outs, *scr)

            @pl.when(last)
            def _():
                for d in carry.descriptors(cins, couts, send_sems, recv_sems):
                    d.wait()

        in_specs = in_specs + [hbm] * ncin
        out_specs = out_specs + [hbm] * ncout
        out_shape = out_shape + carry.outs
        scratch_shapes = scratch_shapes + [pltpu.SemaphoreType.DMA((carry.n,)), pltpu.SemaphoreType.DMA((carry.n,))]
        args = list(args) + carry.inps
    if sem is None:
        sem = ("arbitrary",) * len(grid)
    if nprefetch:
        kw = dict(grid_spec=pltpu.PrefetchScalarGridSpec(num_scalar_prefetch=nprefetch, grid=grid, in_specs=in_specs,
                                                         out_specs=out_specs, scratch_shapes=scratch_shapes))
    else:
        kw = dict(grid=grid, in_specs=in_specs, out_specs=out_specs, scratch_shapes=scratch_shapes)
    res = pl.pallas_call(run, name=name, out_shape=out_shape, compiler_params=_cp(sem), **kw)(*args)
    return list(res)


def _merge(*carries):
    inps, outs, offs = [], [], []
    for cr in carries:
        offs.append((len(inps), len(outs)))
        inps += cr.inps
        outs += cr.outs

    def copies(x, y, c):
        return [(f, a + io, si, o + oo, di) for cr, (io, oo) in zip(carries, offs) for f, a, si, o, di in cr.copies(x, y, c)]

    return _Carry(inps, outs, copies)


def _exchange(name, carry):
    return _pcall(lambda: None, [], name=name, grid=(1,), in_specs=[], out_specs=[], out_shape=[], carry=carry)


def _exchange_two(name, carry, then):
    second = _Carry([], [], then)
    nin, nout = len(carry.inps), len(carry.outs)

    def body(*refs):
        ins, outs = refs[:nin], refs[nin:nin + nout]
        send_a, recv_a, send_b, recv_b = refs[nin + nout:]
        for descs in (carry.descriptors(ins, outs, send_a, recv_a), second.descriptors(outs, outs, send_b, recv_b)):
            for d in descs:
                d.start()
            for d in descs:
                d.wait()

    hbm = pl.BlockSpec(memory_space=pl.ANY)
    return list(pl.pallas_call(
        body, name=name, out_shape=carry.outs, in_specs=[hbm] * nin, out_specs=[hbm] * nout,
        scratch_shapes=[pltpu.SemaphoreType.DMA((carry.n,)), pltpu.SemaphoreType.DMA((carry.n,)),
                        pltpu.SemaphoreType.DMA((second.n,)), pltpu.SemaphoreType.DMA((second.n,))],
    )(*carry.inps))


_ALL7 = [(f >> 2 & 1, f >> 1 & 1, f & 1) for f in range(1, 8)]
_CHIPS3 = [(0, 1, 0), (1, 0, 0), (1, 1, 0)]
_SIBLING = (0, 0, 1)


def _gather8_carry(blk):
    def copies(x, y, c):
        me = 4 * x + 2 * y + c
        return [(None, 0, 0, 0, me)] + [(f, 0, 0, 0, me) for f in _ALL7]

    return _Carry([blk[None]], [jax.ShapeDtypeStruct((8,) + blk.shape, blk.dtype)], copies)


def _gather_chips_carry(blks):
    def copies(x, y, c):
        chip = 2 * x + y
        return [(f, a, 0, a, chip) for a in range(len(blks)) for f in [None] + _CHIPS3]

    return _Carry([b[None] for b in blks], [jax.ShapeDtypeStruct((4,) + b.shape, b.dtype) for b in blks], copies)


def _ada_fwd(c_all, w_loc, b_loc):
    n = w_loc.shape[1]
    tn = 512

    def body(c_ref, w_ref, b_ref, o_ref):
        cv = c_ref[...]
        cond = cv * _sig(cv)
        o_ref[...] = _mm_hi(cond, w_ref[...]) + b_ref[...]

    return pl.pallas_call(
        body, name="ada_fwd", grid=(n // tn,),
        out_shape=jax.ShapeDtypeStruct((8, n), F32),
        in_specs=[_row((8, D)), pl.BlockSpec((D, tn), lambda j: (0, j)), pl.BlockSpec((1, tn), lambda j: (0, j))],
        out_specs=pl.BlockSpec((8, tn), lambda j: (0, j)),
        compiler_params=_cp(("parallel",)),
    )(c_all, w_loc, b_loc)


def _ada_bwd_adamw(c_all_t, dmod_loc, w, m, v, carry=None):
    n = w.shape[1]
    tn = 512

    def body(ct_ref, dm_ref, w_ref, m_ref, v_ref, g_ref, d_ref, mo_ref, vo_ref):
        ct = ct_ref[...]
        cond = ct * _sig(ct)
        dm = dm_ref[...]
        g = cond[:, 0:1] * dm[0:1, :]
        for b in range(1, 8):
            g = g + cond[:, b:b + 1] * dm[b:b + 1, :]
        g_ref[...] = g
        d_ref[...], mo_ref[...], vo_ref[...] = _adamw(w_ref[...], g, m_ref[...], v_ref[...])

    wspec = pl.BlockSpec((D, tn), lambda j: (0, j))
    return _pcall(
        body, [c_all_t, dmod_loc, w, m, v], name="ada_bwd_adamw", grid=(n // tn,),
        out_shape=[jax.ShapeDtypeStruct((D, n), F32)] * 4,
        in_specs=[_row((D, 8)), pl.BlockSpec((8, tn), lambda j: (0, j)), wspec, wspec, wspec],
        out_specs=[wspec] * 4, carry=carry)


def _in_proj_fwd(x, a1, sh1, w_in, carry=None):
    s = x.shape[0]
    tm = 512

    def body(x_ref, a_ref, s_ref, w_ref, q_ref, kv_ref, z_ref, xbc_ref, dt_ref):
        def norm(rows):
            xv = x_ref[rows, :]
            r = lax.rsqrt(jnp.mean(xv * xv, axis=-1, keepdims=True) + EPS)
            return (xv * r * a_ref[...] + s_ref[...]).astype(BF16)

        def project(rows, h):
            p = _mm_nt(h, w_ref[...])
            q_ref[rows, :] = p[:, 0:512].astype(BF16)
            kv_ref[rows, :] = p[:, 512:768].astype(BF16)
            z_ref[rows, :] = p[:, 768:1280]
            xbc_ref[rows, :] = p[:, 1280:2304]
            dt_ref[rows, :] = p[:, 2304:2432]

        r0, r1 = slice(0, tm // 2), slice(tm // 2, tm)
        h0 = norm(r0)
        project(r0, h0)
        project(r1, norm(r1))

    def tok(w):
        return pl.BlockSpec((tm, w), lambda i: (i, 0))

    return _pcall(
        body, [x, a1, sh1, w_in], name="in_proj_fwd", grid=(s // tm,),
        out_shape=[jax.ShapeDtypeStruct((s, QW), BF16), jax.ShapeDtypeStruct((s, 2 * KVW), BF16),
                   jax.ShapeDtypeStruct((s, SW), F32), jax.ShapeDtypeStruct((s, XBCW), F32),
                   jax.ShapeDtypeStruct((s, 128), F32)],
        in_specs=[tok(D), _row((1, D)), _row((1, D)), _row((PROJ_W, D))],
        out_specs=[tok(QW), tok(2 * KVW), tok(SW), tok(XBCW), tok(128)], carry=carry)


def _in_proj_bwd(x, dx1, a1, sh1, w_in, dq, dkv, dz, dxbc, ddt, carry=None):
    s = x.shape[0]
    tm = 512

    def body(x_ref, dx1_ref, a_ref, s_ref, w_ref, dq_ref, dkv_ref, dz_ref, dxbc_ref, ddt_ref,
             gx_ref, h_ref, dsh_ref, p_ref):
        i = pl.program_id(0)

        @pl.when(i == 0)
        def _():
            dsh_ref[...] = jnp.zeros_like(dsh_ref)
            p_ref[...] = jnp.zeros_like(p_ref)

        def gather(st):
            rows = st["rows"]
            st["dproj"] = jnp.concatenate([dq_ref[rows, :], dkv_ref[rows, :], dz_ref[rows, :], dxbc_ref[rows, :],
                                           ddt_ref[rows, :]], axis=1)

        def back(st):
            st["dh"] = _mm(st.pop("dproj"), w_ref[...])

        def norm(st):
            rows, dh = st["rows"], st.pop("dh")
            xv = x_ref[rows, :]
            r = lax.rsqrt(jnp.mean(xv * xv, axis=-1, keepdims=True) + EPS)
            xn = xv * r
            a = a_ref[...]
            h_ref[rows, :] = (xn * a + s_ref[...]).astype(BF16)
            st["dsh"] = jnp.sum(dh, axis=0, keepdims=True)
            st["p"] = jnp.sum(dh * xn, axis=0, keepdims=True)
            u = dh * a
            gx_ref[rows, :] = dx1_ref[rows, :] + r * u - xn * (r * jnp.mean(u * xn, axis=-1, keepdims=True))

        g0, g1 = [dict(rows=slice(k * (tm // 2), (k + 1) * (tm // 2))) for k in range(2)]
        for stage, st in [(gather, g0), (back, g0), (gather, g1), (norm, g0), (back, g1), (norm, g1)]:
            stage(st)
        dsh_ref[0:1, :] += g0["dsh"] + g1["dsh"]
        p_ref[0:1, :] += g0["p"] + g1["p"]

    def tok(w):
        return pl.BlockSpec((tm, w), lambda i: (i, 0))

    return _pcall(
        body, [x, dx1, a1, sh1, w_in, dq, dkv, dz, dxbc, ddt], name="in_proj_bwd", grid=(s // tm,),
        out_shape=[jax.ShapeDtypeStruct((s, D), F32), jax.ShapeDtypeStruct((s, D), BF16),
                   jax.ShapeDtypeStruct((8, D), F32), jax.ShapeDtypeStruct((8, D), F32)],
        in_specs=[tok(D), tok(D), _row((1, D)), _row((1, D)), _row((PROJ_W, D)),
                  tok(QW), tok(2 * KVW), tok(SW), tok(XBCW), tok(128)],
        out_specs=[tok(D), tok(D), _row((8, D)), _row((8, D))], carry=carry)


def _attn_geometry():
    dist = np.arange(BLK)[:, None] + BLK - np.arange(2 * BLK)[None, :]
    n = np.maximum(dist, 0)
    max_exact = NBUCKET // 2
    large = max_exact + (np.log(np.maximum(n, 1) / max_exact) / np.log(128 / max_exact)
                         * (NBUCKET - max_exact)).astype(np.int32)
    large = np.minimum(large, NBUCKET - 1)
    bucket = np.where(n < max_exact, n, large).astype(np.int32)
    mask = (dist >= 0) & (dist < 128)
    return bucket, mask


def _attn_heads(is_first, q_blk, kvw, bias_ref, sinks_ref):
    qv = q_blk * 0.125
    col = lax.broadcasted_iota(jnp.int32, (BLK, 2 * BLK), 1)
    first = jnp.where(jnp.logical_and(is_first, col < BLK), NEG, 0.0)
    groups = []
    for g in range(NKV):
        qs = jnp.concatenate([qv[:, (4 * g + r) * HD:(4 * g + r + 1) * HD] for r in range(4)], axis=0)
        kw = kvw[:, g * HD:(g + 1) * HD]
        vw = kvw[:, KVW + g * HD:KVW + (g + 1) * HD]
        sc = _mm_nt(qs, kw)
        pn, ps = [], []
        for r in range(4):
            h = 4 * g + r
            sr = sc[r * BLK:(r + 1) * BLK] + bias_ref[h] + first
            sink = sinks_ref[h]
            m = jnp.maximum(jnp.max(sr, axis=-1, keepdims=True), sink)
            p = jnp.exp(sr - m)
            es = jnp.exp(sink - m)
            inv = 1.0 / (jnp.sum(p, axis=-1, keepdims=True) + es)
            pn.append(p * inv)
            ps.append(es * inv)
        pn = jnp.concatenate(pn, axis=0)
        ps = jnp.concatenate(ps, axis=0)
        o = _mm(pn.astype(BF16), vw)
        groups.append((qs, kw, vw, pn, ps, o))
    return groups


def _unstack_heads(parts):
    return jnp.concatenate([p[r * BLK:(r + 1) * BLK] for p in parts for r in range(4)], axis=1)


NB = 2


def _attn_fwd(q, kv, bias, sinks, nw, carry=None):
    s = q.shape[0]

    def body(q_ref, kvp_ref, kvc_ref, bias_ref, sinks_ref, nw_ref, y_ref):
        t = pl.program_id(0)
        kv3 = jnp.concatenate([kvp_ref[...], kvc_ref[...]], axis=0)
        for sub in range(NB):
            rows = slice(sub * BLK, (sub + 1) * BLK)
            groups = _attn_heads(jnp.logical_and(t == 0, sub == 0), q_ref[rows, :], kv3[sub * BLK:(sub + 2) * BLK],
                                 bias_ref, sinks_ref)
            o = _unstack_heads([g[5] for g in groups])
            r = lax.rsqrt(jnp.mean(o * o, axis=-1, keepdims=True) + EPS)
            y_ref[rows, :] = (o * r * nw_ref[...]).astype(BF16)

    return _pcall(
        body, [q, kv, kv, bias, sinks, nw], name="attn_fwd", grid=(s // (NB * BLK),),
        out_shape=[jax.ShapeDtypeStruct((s, QW), BF16)],
        in_specs=[pl.BlockSpec((NB * BLK, QW), lambda t: (t, 0)),
                  pl.BlockSpec((BLK, 2 * KVW), lambda t: (jnp.maximum(NB * t - 1, 0), 0)),
                  pl.BlockSpec((NB * BLK, 2 * KVW), lambda t: (t, 0)),
                  _row((NH, BLK, 2 * BLK)),
                  pl.BlockSpec(memory_space=pltpu.SMEM),
                  _row((1, QW))],
        out_specs=[pl.BlockSpec((NB * BLK, QW), lambda t: (t, 0))], carry=carry)


def _attn_bwd(q, kv, dya, bias, sinks, nw, act, dx2, gate2, w_dn, carry=None):
    s = q.shape[0]
    nt = s // (NB * BLK)
    npiece = DFF // NB

    def body(q_ref, kvp_ref, kvc_ref, dy_ref, bias_ref, sinks_ref, nw_ref, act_ref, dx2_ref, g2_ref, wdn_ref,
             dq_ref, dkv_ref, dbias_ref, dsink_ref, dnw_ref, gdn_hbm, dg2_ref, carry_ref, held_ref, acc_ref, sem):
        t = pl.program_id(0)

        @pl.when(t == 0)
        def _():
            carry_ref[...] = jnp.zeros_like(carry_ref)
            held_ref[...] = jnp.zeros_like(held_ref)
            dbias_ref[...] = jnp.zeros_like(dbias_ref)
            dsink_ref[...] = jnp.zeros_like(dsink_ref)
            dnw_ref[...] = jnp.zeros_like(dnw_ref)
            acc_ref[...] = jnp.zeros_like(acc_ref)

        def wgrad_piece(sub):
            rows = slice(sub * npiece, (sub + 1) * npiece)
            acc_ref[rows, :] += _mm_tn(act_ref[:, rows], dx2_ref[...].astype(BF16))

        def block(sub, kv3):
            rows = slice(sub * BLK, (sub + 1) * BLK)
            groups = _attn_heads(jnp.logical_and(t == 0, sub == 0), q_ref[rows, :], kv3[sub * BLK:(sub + 2) * BLK],
                                 bias_ref, sinks_ref)
            o = _unstack_heads([g[5] for g in groups])
            r = lax.rsqrt(jnp.mean(o * o, axis=-1, keepdims=True) + EPS)
            dy = dy_ref[rows, :]
            on = o * r
            dnw_ref[0:1, :] += jnp.sum(dy * on, axis=0, keepdims=True)
            u = dy * nw_ref[...]
            do = r * u - on * (r * jnp.mean(u * on, axis=-1, keepdims=True))
            dq_parts, dk_parts, dv_parts = [], [], []
            for g, (qs, kw, vw, pn, ps, og) in enumerate(groups):
                dos = jnp.concatenate([do[:, (4 * g + r_) * HD:(4 * g + r_ + 1) * HD] for r_ in range(4)], axis=0)
                delta = jnp.sum(dos * og, axis=-1, keepdims=True)
                dp = _mm_nt(dos.astype(BF16), vw)
                ds = pn * (dp - delta)
                dsk = ps * delta
                lane = lax.broadcasted_iota(jnp.int32, (1, 128), 1)
                for r_ in range(4):
                    h = 4 * g + r_
                    dbias_ref[h] += ds[r_ * BLK:(r_ + 1) * BLK]
                    dsink_ref[0:1, :] -= jnp.where(lane == h, jnp.sum(dsk[r_ * BLK:(r_ + 1) * BLK]), 0.0)
                dsb = ds.astype(BF16)
                dq_parts.append(_mm(dsb, kw) * 0.125)
                dk_parts.append(_mm_tn(dsb, qs))
                dv_parts.append(_mm_tn(pn.astype(BF16), dos.astype(BF16)))
            dq_ref[rows, :] = _unstack_heads(dq_parts).astype(BF16)
            return jnp.concatenate(dk_parts + dv_parts, axis=1)

        @pl.when(t < nt)
        def _():
            kv3 = jnp.concatenate([kvp_ref[...], kvc_ref[...]], axis=0)
            tail = carry_ref[...]
            for sub in range(NB):
                d = block(sub, kv3)
                done = tail + d[0:BLK]
                if sub == 0:
                    dkv_ref[0:(NB - 1) * BLK, :] = held_ref[...].astype(BF16)
                    dkv_ref[(NB - 1) * BLK:NB * BLK, :] = done.astype(BF16)
                else:
                    held_ref[(sub - 1) * BLK:sub * BLK, :] = done
                tail = d[BLK:2 * BLK]
                wgrad_piece(sub)
            carry_ref[...] = tail

        @pl.when(t == nt)
        def _():
            dkv_ref[0:(NB - 1) * BLK, :] = held_ref[...].astype(BF16)
            dkv_ref[(NB - 1) * BLK:NB * BLK, :] = carry_ref[...].astype(BF16)
            acc = acc_ref[...]
            dg2_ref[...] = jnp.zeros_like(dg2_ref)
            dg2_ref[0:1, :] = jnp.sum(acc * wdn_ref[...].astype(F32), axis=0, keepdims=True)
            acc_ref[...] = acc * g2_ref[...]
            cp = pltpu.make_async_copy(acc_ref, gdn_hbm, sem)
            cp.start()
            cp.wait()

    last = nt - 1
    tile = lambda w: pl.BlockSpec((NB * BLK, w), lambda t: (jnp.minimum(t, last), 0))
    return _pcall(
        body, [q, kv, kv, dya, bias, sinks, nw, act, dx2, gate2, w_dn], name="attn_bwd", grid=(nt + 1,),
        out_shape=[jax.ShapeDtypeStruct((s, QW), BF16), jax.ShapeDtypeStruct((s, 2 * KVW), BF16),
                   jax.ShapeDtypeStruct((NH, BLK, 2 * BLK), F32), jax.ShapeDtypeStruct((NH, 128), F32),
                   jax.ShapeDtypeStruct((8, QW), F32), jax.ShapeDtypeStruct((DFF, D), F32),
                   jax.ShapeDtypeStruct((8, D), F32)],
        in_specs=[tile(QW),
                  pl.BlockSpec((BLK, 2 * KVW), lambda t: (jnp.clip(NB * t - 1, 0, NB * nt - 1), 0)),
                  tile(2 * KVW), tile(QW),
                  _row((NH, BLK, 2 * BLK)),
                  pl.BlockSpec(memory_space=pltpu.SMEM),
                  _row((1, QW)), tile(DFF), tile(D), _row((1, D)), _row((DFF, D))],
        out_specs=[tile(QW),
                   pl.BlockSpec((NB * BLK, 2 * KVW), lambda t: (jnp.maximum(t - 1, 0), 0)),
                   _row((NH, BLK, 2 * BLK)), _row((NH, 128)), _row((8, QW)),
                   pl.BlockSpec(memory_space=pl.ANY), _row((8, D))],
        scratch_shapes=[pltpu.VMEM((BLK, 2 * KVW), F32), pltpu.VMEM(((NB - 1) * BLK, 2 * KVW), F32),
                        pltpu.VMEM((DFF, D), F32), pltpu.SemaphoreType.DMA], carry=carry)


def _rel_bias_grad(dbias, bucket):
    def body(db_ref, bk_ref, o_ref):
        bk = bk_ref[...]
        lane = lax.broadcasted_iota(jnp.int32, (1, 128), 1)
        for b in range(NBUCKET):
            sel = bk == b
            row = jnp.zeros((1, 128), F32)
            for h in range(NH):
                row = row + jnp.where(lane == h, jnp.sum(jnp.where(sel, db_ref[h], 0.0)), 0.0)
            o_ref[b:b + 1, :] = row

    return pl.pallas_call(
        body, name="rel_bias_grad",
        out_shape=jax.ShapeDtypeStruct((NBUCKET, 128), F32),
    )(dbias, bucket)


def _ssd_consts():
    head_of_lane = np.arange(SW) // HD
    expand = (np.arange(128)[:, None] == head_of_lane[None, :]).astype(np.float32)
    tril = np.tril(np.ones((BLK, BLK), np.float32))
    return (jnp.asarray(expand, BF16), jnp.asarray(expand.T.copy(), BF16), jnp.asarray(tril, BF16),
            jnp.asarray(tril.T.copy(), BF16))


def _conv_pre(xc, halo, cw, cb):
    ext = jnp.concatenate([halo, xc], axis=0)
    taps = [xc if k == CK - 1 else pltpu.roll(ext, CK - 1 - k, 0)[8:8 + BLK] for k in range(CK)]
    return cb + sum(cw[k:k + 1, :] * taps[k] for k in range(CK))


def _ssd_chunk(pre, dtr, dtb, av, dkv, ex, tril, h_in):
    sp = _sig(pre)
    xbc = pre * sp
    xs, bm, cm = xbc[:, 0:SW], xbc[:, SW:SW + 2 * NST], xbc[:, SW + 2 * NST:]
    dtin = dtr + dtb
    dt = jnp.maximum(dtin, 0.0) + jnp.log1p(jnp.exp(-jnp.abs(dtin)))
    cs = _sel_l(tril, dt * av)
    cst = cs.T
    dtx = _sel_r(dt, ex)
    csx = _sel_r(cs, ex)
    xdt = xs * dtx
    csl = csx[BLK - 1:BLK, :]
    decx = jnp.exp(csl - csx)
    ecsx = jnp.exp(csx)
    ecl = jnp.exp(csl)
    causal = tril.astype(F32) > 0.5
    ydiag, yoff, cbs, lms = [], [], [], []
    for g in range(2):
        bg = bm[:, g * NST:(g + 1) * NST].astype(BF16)
        cg = cm[:, g * NST:(g + 1) * NST].astype(BF16)
        cb = _mm_nt(cg, bg)
        cbs.append(cb)
        yoff.append(_mm(cg, h_in[:, g * 256:(g + 1) * 256].astype(BF16)))
        for r in range(4):
            h = 4 * g + r
            seg = cs[:, h:h + 1] - cst[h:h + 1, :]
            lm = jnp.where(causal, jnp.exp(jnp.minimum(seg, 0.0)), 0.0)
            lms.append(lm)
            ydiag.append(_mm((cb * lm).astype(BF16), xdt[:, h * HD:(h + 1) * HD].astype(BF16)))
    yoff = jnp.concatenate(yoff, axis=1) * ecsx
    y = jnp.concatenate(ydiag, axis=1) + yoff + dkv * xs
    return dict(pre=pre, sp=sp, xs=xs, bm=bm, cm=cm, dtin=dtin, dt=dt, av=av, cs=cs, cst=cst,
                dtx=dtx, csx=csx, xdt=xdt, decx=decx, ecsx=ecsx, ecl=ecl, causal=causal, cbs=cbs, lms=lms,
                yoff=yoff, y=y)


def _group_mean(t):
    m0 = jnp.mean(t[:, 0:256], axis=-1, keepdims=True)
    m1 = jnp.mean(t[:, 256:512], axis=-1, keepdims=True)
    return jnp.concatenate([jnp.broadcast_to(m0, (t.shape[0], 256)), jnp.broadcast_to(m1, (t.shape[0], 256))], axis=1)


SUBS = 2


def _ssd_fwd(z, xbc, dtr, cw, cb, dtb, av, dk, nw, carry=None):
    s = z.shape[0]
    nc = s // BLK
    tile = SUBS * BLK
    ex, _, tril, _ = _ssd_consts()

    def body(z_ref, xc_ref, xh_ref, dtr_ref, cw_ref, cb_ref, dtb_ref, a_ref, dk_ref, nw_ref, ex_ref, tril_ref,
             y_ref, hs_ref, pre_ref, h_ref):
        t = pl.program_id(0)

        @pl.when(t == 0)
        def _():
            h_ref[...] = jnp.zeros_like(h_ref)

        h_in = h_ref[...]
        for sub in range(SUBS):
            rows = slice(sub * BLK, (sub + 1) * BLK)
            xc = xc_ref[rows, :]
            halo = jnp.where(t == 0, 0.0, xh_ref[...]) if sub == 0 else xc_ref[sub * BLK - 8:sub * BLK, :]
            pre = _conv_pre(xc, halo, cw_ref[...], cb_ref[...])
            pre_ref[rows, :] = pre
            hs_ref[sub] = h_in
            f = _ssd_chunk(pre, dtr_ref[rows, :], dtb_ref[...], a_ref[...], dk_ref[...], ex_ref[...], tril_ref[...], h_in)
            dx = (f["decx"] * f["xdt"]).astype(BF16)
            st = [_mm_tn(f["bm"][:, g * NST:(g + 1) * NST].astype(BF16), dx[:, g * 256:(g + 1) * 256]) for g in range(2)]
            h_in = h_in * f["ecl"] + jnp.concatenate(st, axis=1)
            zv = z_ref[rows, :]
            tg = f["y"] * (zv * _sig(zv))
            r = lax.rsqrt(_group_mean(tg * tg) + EPS)
            y_ref[rows, :] = (tg * r * nw_ref[...]).astype(BF16)
        h_ref[...] = h_in

    cur = lambda w: pl.BlockSpec((tile, w), lambda t: (t, 0))
    return _pcall(
        body, [z, xbc, xbc, dtr, cw, cb, dtb, av, dk, nw, ex, tril], name="ssd_fwd", grid=(s // tile,),
        out_shape=[jax.ShapeDtypeStruct((s, SW), BF16), jax.ShapeDtypeStruct((nc, NST, SW), F32),
                   jax.ShapeDtypeStruct((s, XBCW), F32)],
        in_specs=[cur(SW), cur(XBCW), pl.BlockSpec((8, XBCW), lambda t: (jnp.maximum(t * (tile // 8) - 1, 0), 0)),
                  cur(128), _row((8, XBCW)), _row((1, XBCW)), _row((1, 128)),
                  _row((1, 128)), _row((1, SW)), _row((1, SW)), _row((128, SW)), _row((BLK, BLK))],
        out_specs=[cur(SW), pl.BlockSpec((SUBS, NST, SW), lambda t: (t, 0, 0)), cur(XBCW)],
        scratch_shapes=[pltpu.VMEM((NST, SW), F32)], carry=carry)


def _ssd_bwd(z, xbc, pre_all, dtr, dys, hs, cw, dtb, av, dk, nw, h2, dgu, carry=None):
    s = z.shape[0]
    tile = SUBS * BLK
    nt = s // tile
    ex, ext_t, tril, triu = _ssd_consts()
    ngu = dgu.shape[1]
    npiece = D // SUBS

    def body(z_ref, xc_ref, pre_ref, dtr_ref, dy_ref, hs_ref, cw_ref, dtb_ref, a_ref, dk_ref, nw_ref,
             ex_ref, ext_ref, tril_ref, triu_ref, h2_ref, dgu_ref,
             dz_ref, dxbc_ref, ddt_ref, dcw_ref, dcb_ref, dnw_ref, dhd_ref, ggu_hbm, dh_ref, nxt_ref, dd_ref,
             acc_ref, sems):
        i = pl.program_id(0)

        @pl.when(i == 0)
        def _():
            dh_ref[...] = jnp.zeros_like(dh_ref)
            nxt_ref[...] = jnp.zeros_like(nxt_ref)
            dd_ref[...] = jnp.zeros_like(dd_ref)
            dcw_ref[...] = jnp.zeros_like(dcw_ref)
            dcb_ref[...] = jnp.zeros_like(dcb_ref)
            dnw_ref[...] = jnp.zeros_like(dnw_ref)
            dhd_ref[...] = jnp.zeros_like(dhd_ref)
            acc_ref[...] = jnp.zeros_like(acc_ref)

        gst, nxt = dh_ref[...], nxt_ref[...]
        for sub in reversed(range(SUBS)):
            rows = slice(sub * BLK, (sub + 1) * BLK)
            gst, nxt = chunk(sub, rows, gst, nxt, z_ref, xc_ref, pre_ref, dtr_ref, dy_ref, hs_ref, cw_ref, dtb_ref,
                             a_ref, dk_ref, nw_ref, ex_ref, ext_ref, tril_ref, triu_ref,
                             dz_ref, dxbc_ref, ddt_ref, dcw_ref, dcb_ref, dnw_ref, dhd_ref, dd_ref,
                             functools.partial(wgrad_part, sub, h2_ref[:, sub * npiece:(sub + 1) * npiece].T,
                                               dgu_ref, acc_ref))
        dh_ref[...] = gst
        nxt_ref[...] = nxt

        @pl.when(i == nt - 1)
        def _():
            dhd_ref[2:3, :] = _sel_r(dd_ref[...], ext_ref[...])[0:1, :]
            nj = ngu // 4
            cps = [pltpu.make_async_copy(acc_ref.at[:, pl.ds(j * nj, nj)], ggu_hbm.at[j], sems.at[j]) for j in range(4)]
            for cp in cps:
                cp.start()
            for cp in cps:
                cp.wait()

    col_tiles = [0, 3, 6, 9, 12, 15, 18, 20, 22]

    def wgrad_part(sub, h2t, dgu_ref, acc_ref, h):
        piece = slice(sub * npiece, (sub + 1) * npiece)
        cols = slice(col_tiles[h] * 256, col_tiles[h + 1] * 256)
        acc_ref[piece, cols] += _mm(h2t, dgu_ref[:, cols])

    def chunk(sub, rows, gst, nxt, z_ref, xc_ref, pre_ref, dtr_ref, dy_ref, hs_ref, cw_ref, dtb_ref,
              a_ref, dk_ref, nw_ref, ex_ref, ext_ref, tril_ref, triu_ref,
              dz_ref, dxbc_ref, ddt_ref, dcw_ref, dcb_ref, dnw_ref, dhd_ref, dd_ref, after_head):
        h_in = hs_ref[sub]
        f = _ssd_chunk(pre_ref[rows, :], dtr_ref[rows, :], dtb_ref[...], a_ref[...], dk_ref[...], ex_ref[...],
                       tril_ref[...], h_in)
        xs, xdt, decx, ecsx, ecl, dtx = f["xs"], f["xdt"], f["decx"], f["ecsx"], f["ecl"], f["dtx"]
        cs, cst, causal = f["cs"], f["cst"], f["causal"]
        causal_t = triu_ref[...].astype(F32) > 0.5

        zv = z_ref[rows, :]
        sz = _sig(zv)
        gz = zv * sz
        t = f["y"] * gz
        r = lax.rsqrt(_group_mean(t * t) + EPS)
        tn_ = t * r
        dyn = dy_ref[rows, :]
        dnw_ref[0:1, :] += jnp.sum(dyn * tn_, axis=0, keepdims=True)
        u = dyn * nw_ref[...]
        dt_ = r * u - tn_ * (r * _group_mean(u * tn_))
        dy = dt_ * gz
        dz_ref[rows, :] = (dt_ * f["y"] * (sz * (1.0 + zv * (1.0 - sz)))).astype(BF16)

        dd_ref[0:1, :] += jnp.sum(dy * xs, axis=0, keepdims=True)
        dxs = dk_ref[...] * dy

        edy = ecsx * dy
        dxdt, dbs, dcs_, dcsx_parts, dh_new = [], [], [], [], []
        lane = lax.broadcasted_iota(jnp.int32, (1, 128), 1)
        dcs_intra = jnp.zeros((BLK, 128), F32)
        for g in range(2):
            sl = slice(g * 256, (g + 1) * 256)
            bgf, cgf = f["bm"][:, g * NST:(g + 1) * NST], f["cm"][:, g * NST:(g + 1) * NST]
            bg, cg = bgf.astype(BF16), cgf.astype(BF16)
            gg = gst[:, sl].astype(BF16)
            hg = h_in[:, sl].astype(BF16)
            edyg = edy[:, sl].astype(BF16)
            dc = _mm_nt(edyg, hg)
            dh_new.append(gst[:, sl] * ecl[:, sl] + _mm_tn(cg, edyg))
            bgm = _mm(bg, gg)
            dxdt_g = decx[:, sl] * bgm
            dxg = (decx[:, sl] * xdt[:, sl]).astype(BF16)
            db = _mm_nt(dxg, gg)
            qd = bgm * xdt[:, sl] * decx[:, sl]
            last = jnp.sum(qd, axis=0, keepdims=True) + ecl[:, sl] * jnp.sum(gst[:, sl] * h_in[:, sl], axis=0, keepdims=True)
            rowid = lax.broadcasted_iota(jnp.int32, (BLK, 256), 0)
            dcsx_parts.append(f["yoff"][:, sl] * dy[:, sl] - qd + jnp.where(rowid == BLK - 1, last, 0.0))
            cb_ = f["cbs"][g]
            cbt = _mm_nt(bg, cg)
            dcb_ = jnp.zeros((BLK, BLK), F32)
            dcbt = jnp.zeros((BLK, BLK), F32)
            dxd = []
            for r_ in range(4):
                h = 4 * g + r_
                hl = slice(h * HD, (h + 1) * HD)
                lm = f["lms"][h]
                segt = cst[h:h + 1, :] - cs[:, h:h + 1]
                lmt = jnp.where(causal_t, jnp.exp(jnp.minimum(segt, 0.0)), 0.0)
                dyh = dy[:, hl].astype(BF16)
                xdh = xdt[:, hl].astype(BF16)
                dw = _mm_nt(dyh, xdh)
                dwt = _mm_nt(xdh, dyh)
                wt = cbt * lmt
                dxd.append(_mm(wt.astype(BF16), dyh))
                dcb_ = dcb_ + dw * lm
                dcbt = dcbt + dwt * lmt
                col = jnp.sum(dw * (cb_ * lm), axis=-1, keepdims=True) - jnp.sum(dwt * wt, axis=-1, keepdims=True)
                dcs_intra = dcs_intra + jnp.where(lane == h, col, 0.0)
                after_head(h)
            dxdt.append(dxdt_g + jnp.concatenate(dxd, axis=1))
            dcs_.append(dc + _mm(dcb_.astype(BF16), bg))
            dbs.append(db + _mm(dcbt.astype(BF16), cg))
        dxdt = jnp.concatenate(dxdt, axis=1)
        dxs = dxs + dxdt * dtx
        ext_t_ = ext_ref[...]
        dcs = dcs_intra + _sel_r(jnp.concatenate(dcsx_parts, axis=1), ext_t_)
        da = _sel_l(triu_ref[...], dcs)
        ddt = da * f["av"] + _sel_r(dxdt * xs, ext_t_)
        dhd_ref[1:2, :] += jnp.sum(da * f["dt"], axis=0, keepdims=True)
        ddtr = ddt * _sig(f["dtin"])
        dhd_ref[0:1, :] += jnp.sum(ddtr, axis=0, keepdims=True)
        ddt_ref[rows, :] = ddtr.astype(BF16)

        sp, pre = f["sp"], f["pre"]
        dact = jnp.concatenate([dxs] + dbs + dcs_, axis=1)
        dpre = dact * (sp * (1.0 + pre * (1.0 - sp)))
        dcb_ref[0:1, :] += jnp.sum(dpre, axis=0, keepdims=True)
        ext2 = jnp.concatenate([dpre, nxt], axis=0)
        shifted = [pltpu.roll(ext2, BLK + 8 - (CK - 1 - k), 0)[0:BLK] for k in range(CK - 1)] + [dpre]
        cw = cw_ref[...]
        xc = xc_ref[rows, :]
        dxr = cw[CK - 1:CK, :] * dpre
        for k in range(CK):
            dcw_ref[k:k + 1, :] += jnp.sum(shifted[k] * xc, axis=0, keepdims=True)
            if k < CK - 1:
                dxr = dxr + cw[k:k + 1, :] * shifted[k]
        dxbc_ref[rows, :] = dxr.astype(BF16)
        return jnp.concatenate(dh_new, axis=1), dpre[0:8]

    cur = lambda w: pl.BlockSpec((tile, w), lambda i: (nt - 1 - i, 0))
    return _pcall(
        body, [z, xbc, pre_all, dtr, dys, hs, cw, dtb, av, dk, nw, ex, ext_t, tril, triu, h2, dgu], name="ssd_bwd",
        grid=(nt,),
        out_shape=[jax.ShapeDtypeStruct((s, SW), BF16), jax.ShapeDtypeStruct((s, XBCW), BF16),
                   jax.ShapeDtypeStruct((s, 128), BF16), jax.ShapeDtypeStruct((8, XBCW), F32),
                   jax.ShapeDtypeStruct((8, XBCW), F32), jax.ShapeDtypeStruct((8, SW), F32),
                   jax.ShapeDtypeStruct((8, 128), F32), jax.ShapeDtypeStruct((4, D, ngu // 4), F32)],
        in_specs=[cur(SW), cur(XBCW), cur(XBCW), cur(128), cur(SW),
                  pl.BlockSpec((SUBS, NST, SW), lambda i: (nt - 1 - i, 0, 0)),
                  _row((8, XBCW)), _row((1, 128)), _row((1, 128)), _row((1, SW)), _row((1, SW)),
                  _row((128, SW)), _row((SW, 128)), _row((BLK, BLK)), _row((BLK, BLK)), cur(D), cur(ngu)],
        out_specs=[cur(SW), cur(XBCW), cur(128), _row((8, XBCW)), _row((8, XBCW)), _row((8, SW)), _row((8, 128)),
                   pl.BlockSpec(memory_space=pl.ANY)],
        scratch_shapes=[pltpu.VMEM((NST, SW), F32), pltpu.VMEM((8, XBCW), F32), pltpu.VMEM((8, SW), F32),
                        pltpu.VMEM((D, ngu), F32), pltpu.SemaphoreType.DMA((4,))], carry=carry)


def _load_once(i, pairs, sem):
    @pl.when(i == 0)
    def _():
        cps = [pltpu.make_async_copy(src, dst, sem.at[k]) for k, (src, dst) in enumerate(pairs)]
        for cp in cps:
            cp.start()
        for cp in cps:
            cp.wait()


def _mlp_fwd(x, ya, ys, tgt, w_o, w_ga, w_gb, w_dn, gate1, a2, sh2, gate2, fn):
    s = x.shape[0]
    sub_m, subs = 256, 2
    tm = sub_m * subs

    def body(x_ref, ya_ref, ys_ref, t_ref, wo_hbm, wga_hbm, wgb_hbm, wdn_hbm, g1_ref, a2_ref, s2_ref, g2_ref, fn_ref,
             x1_ref, gu_ref, dx2_ref, loss_ref, dfn_ref, wo, wga, wgb, wdn, sem):
        i = pl.program_id(0)
        _load_once(i, [(wo_hbm, wo), (wga_hbm, wga), (wgb_hbm, wgb), (wdn_hbm, wdn)], sem)

        @pl.when(i == 0)
        def _():
            loss_ref[...] = jnp.zeros_like(loss_ref)
            dfn_ref[...] = jnp.zeros_like(dfn_ref)

        def proj(st):
            st["mix"] = _mm(ya_ref[st["rows"], :], wo[0:QW, :]) + _mm(ys_ref[st["rows"], :], wo[QW:D, :])

        def norm(st):
            x1 = x_ref[st["rows"], :] + g1_ref[...] * st.pop("mix")
            x1_ref[st["rows"], :] = x1
            r2 = lax.rsqrt(jnp.mean(x1 * x1, axis=-1, keepdims=True) + EPS)
            st["x1"] = x1
            st["h2"] = (x1 * r2 * a2_ref[...] + s2_ref[...]).astype(BF16)

        def gate_up(st):
            h2 = st.pop("h2")
            ha, hb = h2[:, 0:D // 2], h2[:, D // 2:D]
            gub = jnp.concatenate([(_mm(ha, wga[j]) + _mm(hb, wgb[j])).astype(BF16) for j in range(4)], axis=1)
            gu_ref[st["rows"], :] = gub
            st["gub"] = gub

        def activate(st):
            gub = st.pop("gub")
            gv, uv = gub[:, 0:DFF].astype(F32), gub[:, DFF:].astype(F32)
            st["act"] = (gv * _sig(gv) * uv).astype(BF16)

        def down(st):
            st["ff"] = _mm(st.pop("act"), wdn[...])

        def head(st):
            x2 = st.pop("x1") + g2_ref[...] * st.pop("ff")
            r3 = lax.rsqrt(jnp.mean(x2 * x2, axis=-1, keepdims=True) + EPS)
            xn = x2 * r3
            fnv = fn_ref[...]
            err = xn * fnv - t_ref[st["rows"], :]
            st["loss"] = jnp.sum(err * err) * (0.5 / D)
            dy = err * (1.0 / D)
            st["dfn"] = jnp.sum(dy * xn, axis=0, keepdims=True)
            u = dy * fnv
            dx2_ref[st["rows"], :] = r3 * u - xn * (r3 * jnp.mean(u * xn, axis=-1, keepdims=True))

        a, b = [dict(rows=slice(k * sub_m, (k + 1) * sub_m)) for k in range(subs)]
        for stage, st in [(proj, a), (norm, a), (proj, b), (gate_up, a), (norm, b), (activate, a), (gate_up, b),
                          (down, a), (activate, b), (head, a), (down, b), (head, b)]:
            stage(st)
        loss_ref[...] += a["loss"] + b["loss"]
        dfn_ref[0:1, :] += a["dfn"] + b["dfn"]

    def tok(w):
        return pl.BlockSpec((tm, w), lambda i: (i, 0))

    hbm = pl.BlockSpec(memory_space=pl.ANY)
    return pl.pallas_call(
        body, name="mlp_fwd", grid=(s // tm,),
        out_shape=[jax.ShapeDtypeStruct((s, D), F32), jax.ShapeDtypeStruct((s, 2 * DFF), BF16),
                   jax.ShapeDtypeStruct((s, D), F32), jax.ShapeDtypeStruct((8, 128), F32),
                   jax.ShapeDtypeStruct((8, D), F32)],
        in_specs=[tok(D), tok(QW), tok(SW), tok(D), hbm, hbm, hbm, hbm,
                  _row((1, D)), _row((1, D)), _row((1, D)), _row((1, D)), _row((1, D))],
        out_specs=[tok(D), tok(2 * DFF), tok(D), _row((8, 128)), _row((8, D))],
        scratch_shapes=[pltpu.VMEM((D, D), BF16), pltpu.VMEM(w_ga.shape, BF16), pltpu.VMEM(w_gb.shape, BF16),
                        pltpu.VMEM((DFF, D), BF16), pltpu.SemaphoreType.DMA((4,))],
        compiler_params=_cp(("arbitrary",)),
    )(x, ya, ys, tgt, w_o, w_ga, w_gb, w_dn, gate1, a2, sh2, gate2, fn)


def _mlp_bwd(x1, gu, dx2, w_o, w_ga, w_gb, w_dn, gate1, a2, sh2, gate2):
    s = x1.shape[0]
    tm = 256
    nj = 2 * DFF // 4

    def body(x1_ref, gu_ref, dx2_ref, wo_hbm, wga_hbm, wgb_hbm, wdn_hbm, g1_ref, a2_ref, s2_ref, g2_ref,
             dx1_ref, dya_ref, dys_ref, act_ref, dgu_ref, h2_ref, dsh_ref, p_ref, wo, wga, wgb, wdn, sem):
        i = pl.program_id(0)
        _load_once(i, [(wo_hbm, wo), (wga_hbm, wga), (wgb_hbm, wgb), (wdn_hbm, wdn)], sem)

        @pl.when(i == 0)
        def _():
            dsh_ref[...] = jnp.zeros_like(dsh_ref)
            p_ref[...] = jnp.zeros_like(p_ref)

        dx2 = dx2_ref[...]
        dact = _mm_nt((dx2 * g2_ref[...]).astype(BF16), wdn[...])
        gub = gu_ref[...]
        gv, uv = gub[:, 0:DFF].astype(F32), gub[:, DFF:].astype(F32)
        sg = _sig(gv)
        sl = gv * sg
        act_ref[...] = (sl * uv).astype(BF16)
        dgu = jnp.concatenate([dact * uv * (sg * (1.0 + gv * (1.0 - sg))), dact * sl], axis=1).astype(BF16)
        dgu_ref[...] = dgu
        dha = sum(_mm_nt(dgu[:, j * nj:(j + 1) * nj], wga[j]) for j in range(4))
        dhb = sum(_mm_nt(dgu[:, j * nj:(j + 1) * nj], wgb[j]) for j in range(4))
        dh = jnp.concatenate([dha, dhb], axis=1)
        x1 = x1_ref[...]
        r2 = lax.rsqrt(jnp.mean(x1 * x1, axis=-1, keepdims=True) + EPS)
        xn = x1 * r2
        a2 = a2_ref[...]
        h2_ref[...] = (xn * a2 + s2_ref[...]).astype(BF16)
        dsh_ref[0:1, :] += jnp.sum(dh, axis=0, keepdims=True)
        p_ref[0:1, :] += jnp.sum(dh * xn, axis=0, keepdims=True)
        u = dh * a2
        dx1 = dx2 + r2 * u - xn * (r2 * jnp.mean(u * xn, axis=-1, keepdims=True))
        dx1_ref[...] = dx1
        dcat = _mm_nt((dx1 * g1_ref[...]).astype(BF16), wo[...])
        dya_ref[...] = dcat[:, 0:QW]
        dys_ref[...] = dcat[:, QW:D]

    def tok(w):
        return pl.BlockSpec((tm, w), lambda i: (i, 0))

    hbm = pl.BlockSpec(memory_space=pl.ANY)
    return pl.pallas_call(
        body, name="mlp_bwd", grid=(s // tm,),
        out_shape=[jax.ShapeDtypeStruct((s, D), F32), jax.ShapeDtypeStruct((s, QW), F32),
                   jax.ShapeDtypeStruct((s, SW), F32), jax.ShapeDtypeStruct((s, DFF), BF16),
                   jax.ShapeDtypeStruct((s, 2 * DFF), BF16), jax.ShapeDtypeStruct((s, D), BF16),
                   jax.ShapeDtypeStruct((8, D), F32), jax.ShapeDtypeStruct((8, D), F32)],
        in_specs=[tok(D), tok(2 * DFF), tok(D), hbm, hbm, hbm, hbm, _row((1, D)), _row((1, D)), _row((1, D)), _row((1, D))],
        out_specs=[tok(D), tok(QW), tok(SW), tok(DFF), tok(2 * DFF), tok(D), _row((8, D)), _row((8, D))],
        scratch_shapes=[pltpu.VMEM((D, D), BF16), pltpu.VMEM(w_ga.shape, BF16), pltpu.VMEM(w_gb.shape, BF16),
                        pltpu.VMEM((DFF, D), BF16), pltpu.SemaphoreType.DMA((4,))],
        compiler_params=_cp(("arbitrary",)),
    )(x1, gu, dx2, w_o, w_ga, w_gb, w_dn, gate1, a2, sh2, gate2)


def _wgrad(name, a, b, gate, w, carry=None):
    s, m = a.shape
    n = b.shape[1]
    tk = min(1024, s)
    nk = s // tk

    def body(a_ref, b_ref, g_ref, w_ref, o_hbm, dg_ref, acc_ref, sem):
        k = pl.program_id(0)

        @pl.when(k == 0)
        def _():
            acc_ref[...] = jnp.zeros_like(acc_ref)

        acc_ref[...] += _mm_tn(a_ref[...], b_ref[...].astype(BF16))

        @pl.when(k == nk - 1)
        def _():
            acc = acc_ref[...]
            dg_ref[...] = jnp.zeros_like(dg_ref)
            dg_ref[0:1, :] = jnp.sum(acc * w_ref[...].astype(F32), axis=0, keepdims=True)
            acc_ref[...] = acc * g_ref[...]
            cp = pltpu.make_async_copy(acc_ref, o_hbm, sem)
            cp.start()
            cp.wait()

    return _pcall(body, [a, b, gate, w], name=name, grid=(nk,),
                  out_shape=[jax.ShapeDtypeStruct((m, n), F32), jax.ShapeDtypeStruct((8, n), F32)],
                  in_specs=[pl.BlockSpec((tk, m), lambda k: (k, 0)), pl.BlockSpec((tk, n), lambda k: (k, 0)),
                            _row((1, n)), _row((m, n))],
                  out_specs=[pl.BlockSpec(memory_space=pl.ANY), _row((8, n))],
                  scratch_shapes=[pltpu.VMEM((m, n), F32), pltpu.SemaphoreType.DMA], carry=carry)


def _wgrad_gate_up(h2, dgu, carry=None):
    s = h2.shape[0]
    tk = min(1024, s)
    nk = s // tk
    n = dgu.shape[1]
    nj = n // 4

    def body(a_ref, b_ref, o_hbm, acc_ref, sems):
        k = pl.program_id(0)

        @pl.when(k == 0)
        def _():
            acc_ref[...] = jnp.zeros_like(acc_ref)

        acc_ref[...] += _mm_tn(a_ref[...], b_ref[...])

        @pl.when(k == nk - 1)
        def _():
            cps = [pltpu.make_async_copy(acc_ref.at[:, pl.ds(j * nj, nj)], o_hbm.at[j], sems.at[j]) for j in range(4)]
            for cp in cps:
                cp.start()
            for cp in cps:
                cp.wait()

    return _pcall(body, [h2, dgu], name="wgrad_gate_up", grid=(nk,),
                  out_shape=[jax.ShapeDtypeStruct((4, D, nj), F32)],
                  in_specs=[pl.BlockSpec((tk, D), lambda k: (k, 0)), pl.BlockSpec((tk, n), lambda k: (k, 0))],
                  out_specs=[pl.BlockSpec(memory_space=pl.ANY)],
                  scratch_shapes=[pltpu.VMEM((D, n), F32), pltpu.SemaphoreType.DMA((4,))], carry=carry)


def _wgrad_in_t(h1, pieces, carry=None):
    s = h1.shape[0]
    tk = min(1024, s)
    nk = s // tk

    def body(a_ref, dq_ref, dkv_ref, dz_ref, dxbc_ref, ddt_ref, o_hbm, acc_ref, tr_ref, sem):
        k = pl.program_id(0)

        @pl.when(k == 0)
        def _():
            acc_ref[...] = jnp.zeros_like(acc_ref)

        dproj = jnp.concatenate([dq_ref[...], dkv_ref[...], dz_ref[...], dxbc_ref[...], ddt_ref[...]], axis=1)
        acc_ref[...] += _mm_tn(a_ref[...], dproj)

        @pl.when(k == nk - 1)
        def _():
            for j in range(PROJ_W // 128):
                tr_ref[j * 128:(j + 1) * 128, :] = acc_ref[:, j * 128:(j + 1) * 128].T
            cp = pltpu.make_async_copy(tr_ref, o_hbm, sem)
            cp.start()
            cp.wait()

    return _pcall(body, [h1] + list(pieces), name="wgrad_in", grid=(nk,),
                  out_shape=[jax.ShapeDtypeStruct((PROJ_W, D), F32)],
                  in_specs=[pl.BlockSpec((tk, p.shape[1]), lambda k: (k, 0)) for p in [h1] + list(pieces)],
                  out_specs=[pl.BlockSpec(memory_space=pl.ANY)],
                  scratch_shapes=[pltpu.VMEM((D, PROJ_W), F32), pltpu.VMEM((PROJ_W, D), F32), pltpu.SemaphoreType.DMA],
                  carry=carry)


_SMALL = ["ada_b", "norm1", "conv_w", "conv_b", "dt_bias", "A_log", "D_skip", "sinks", "attn_out_norm",
          "ssm_out_norm", "norm2", "rel_bias", "final_norm"]


def _small_grad(name, gs, chip):
    if name == "ada_b":
        return jnp.concatenate([gs[j:j + 1, :] for j in range(6)], axis=1)
    if name == "conv_w":
        full = gs[7:11, :]
        out = full[:, 0:256]
        for j in range(1, 4):
            out = jnp.where(chip == j, full[:, j * 256:(j + 1) * 256], out)
        return out
    row, width = {"norm1": (6, D), "conv_b": (11, D), "norm2": (12, D), "final_norm": (13, D),
                  "attn_out_norm": (14, QW), "ssm_out_norm": (15, SW), "dt_bias": (16, NH), "A_log": (17, NH),
                  "D_skip": (18, NH), "sinks": (19, NH), "rel_bias": (24, NH)}[name]
    rows = NBUCKET if name == "rel_bias" else 1
    return gs[row:row + rows, 0:width]


def _small_update(small_all, where, ws, ms, vs):
    n = len(_SMALL)

    def body(where_ref, sa_ref, *refs):
        w_refs, m_refs, v_refs, outs = refs[:n], refs[n:2 * n], refs[2 * n:3 * n], refs[3 * n:]
        gs = sa_ref[0]
        for b in range(1, 8):
            gs = gs + sa_ref[b]
        chip = where_ref[1]
        for i, name in enumerate(_SMALL):
            g = _small_grad(name, gs, chip)
            lead = (0,) if name == "conv_w" else ()
            d, mo, vo = _adamw(w_refs[i][lead + (...,)], g, m_refs[i][lead + (...,)], v_refs[i][lead + (...,)])
            for k, val in enumerate((g, d, mo, vo)):
                outs[k * n + i][lead + (...,)] = val
        outs[4 * n][...] = gs[20:21, 0:128]

    shapes = [jax.ShapeDtypeStruct(w.shape, F32) for w in ws]
    vmem = pl.BlockSpec(memory_space=pltpu.VMEM)
    res = pl.pallas_call(
        body, name="small_update", out_shape=shapes * 4 + [jax.ShapeDtypeStruct((1, 128), F32)],
        in_specs=[pl.BlockSpec(memory_space=pltpu.SMEM)] + [vmem] * (1 + 3 * n), out_specs=[vmem] * (4 * n + 1),
    )(where, small_all, *ws, *ms, *vs)
    return [res[k * n:(k + 1) * n] for k in range(4)], res[4 * n][0, 0]


def _add_half(name, g, got, where, by_cols=False):
    rr, cc = got.shape[1:]
    if by_cols:
        mine = pl.BlockSpec((None, rr, cc), lambda i, w_ref: (i, 0, w_ref[0]))
    else:
        mine = pl.BlockSpec((None, None, rr, cc), lambda i, w_ref: (i, w_ref[0], 0, 0))

    def body(w_ref, g_ref, r_ref, o_ref, own_ref):
        s = g_ref[...] + r_ref[...]
        o_ref[...] = s.astype(BF16)

        @pl.when(pl.program_id(0) == w_ref[1])
        def _():
            own_ref[...] = s

    spec = pl.BlockSpec((None, rr, cc), lambda i, w_ref: (i, 0, 0))
    return _pcall(body, [where, g, got], name=name, grid=(4,), nprefetch=1,
                  out_shape=[jax.ShapeDtypeStruct(got.shape, BF16), jax.ShapeDtypeStruct((rr, cc), F32)],
                  in_specs=[mine, spec],
                  out_specs=[spec, pl.BlockSpec((rr, cc), lambda i, w_ref: (0, 0))])


def _add_chips(name, own, got):
    rr, cc = own.shape
    tr = rr // 2 if rr % 32 == 0 else rr

    def body(s_ref, r_ref, o_ref):
        o_ref[...] = ((s_ref[...] + r_ref[0].astype(F32)) + r_ref[1].astype(F32)) + r_ref[2].astype(F32)

    spec = pl.BlockSpec((tr, cc), lambda i: (i, 0))
    return _pcall(body, [own, got], name=name, grid=(rr // tr,), out_shape=[jax.ShapeDtypeStruct((rr, cc), F32)],
                  in_specs=[spec, pl.BlockSpec((3, tr, cc), lambda i: (0, i, 0))], out_specs=[spec])[0]


def _adamw_halves(name, mine, got, w, m, v, where, by_cols=False):
    rr, cc = mine.shape

    def body(w_ref_, t_ref, r_ref, w_ref, m_ref, v_ref, g_ref, d_ref, mo_ref, vo_ref):
        g = jnp.where(pl.program_id(0) == w_ref_[0], t_ref[...], r_ref[...])
        g_ref[...] = g
        d_ref[...], mo_ref[...], vo_ref[...] = _adamw(w_ref[...], g, m_ref[...], v_ref[...])

    if by_cols:
        grid = (2, 1)
        half = pl.BlockSpec((rr, cc), lambda h, i, w_ref_: (0, 0))
        full = pl.BlockSpec((rr, cc), lambda h, i, w_ref_: (0, h))
    else:
        tr = rr // 2
        grid = (2, 2)
        half = pl.BlockSpec((tr, cc), lambda h, i, w_ref_: (i, 0))
        full = pl.BlockSpec((None, tr, cc), lambda h, i, w_ref_: (0, 2 * h + i, 0))
    return _pcall(body, [where, mine, got, w, m, v], name=name, grid=grid, nprefetch=1,
                  out_shape=[jax.ShapeDtypeStruct(w.shape, F32)] * 4,
                  in_specs=[half, half, full, full, full], out_specs=[full] * 4)


def _bias_table(rel_bias, bucket, mask):
    def body(rb_ref, bk_ref, mk_ref, o_ref):
        bk = bk_ref[...]
        valid = mk_ref[...] > 0
        for h in range(NH):
            acc = jnp.zeros((BLK, 2 * BLK), F32)
            for b in range(NBUCKET):
                acc = jnp.where(bk == b, rb_ref[b, h], acc)
            o_ref[h] = jnp.where(valid, acc, NEG)

    vmem = pl.BlockSpec(memory_space=pltpu.VMEM)
    return pl.pallas_call(
        body, name="bias_table", out_shape=jax.ShapeDtypeStruct((NH, BLK, 2 * BLK), F32),
        in_specs=[pl.BlockSpec(memory_space=pltpu.SMEM), vmem, vmem], out_specs=vmem,
    )(rel_bias, bucket, mask)


def _pack_small(dsh1, p1, dsh2, p2, dg1a, dg1b, dg2, norm1, norm2, scale1, scale2, dcw, dcb, dfn,
                dnw_attn, dnw_ssm, dhd, av, dsink, drel, loss_acc):
    def body(dsh1_ref, p1_ref, dsh2_ref, p2_ref, dg1a_ref, dg1b_ref, dg2_ref, n1_ref, n2_ref, s1_ref, s2_ref,
             dcw_ref, dcb_ref, dfn_ref, da_ref, ds_ref, dhd_ref, av_ref, dsink_ref, drel_ref, loss_ref, o_ref):
        o_ref[...] = jnp.zeros_like(o_ref)
        p1v, p2v = p1_ref[0:1, :], p2_ref[0:1, :]
        o_ref[0:1, :] = dsh1_ref[0:1, :]
        o_ref[1:2, :] = p1v * n1_ref[...]
        o_ref[2:3, :] = dg1a_ref[0:1, :] + dg1b_ref[0:1, :]
        o_ref[3:4, :] = dsh2_ref[0:1, :]
        o_ref[4:5, :] = p2v * n2_ref[...]
        o_ref[5:6, :] = dg2_ref[0:1, :]
        o_ref[6:7, :] = p1v * (1.0 + s1_ref[...])
        o_ref[7:11, :] = dcw_ref[0:4, :]
        o_ref[11:12, :] = dcb_ref[0:1, :]
        o_ref[12:13, :] = p2v * (1.0 + s2_ref[...])
        o_ref[13:14, :] = dfn_ref[0:1, :]
        o_ref[14:15, 0:QW] = da_ref[0:1, :]
        o_ref[15:16, 0:SW] = ds_ref[0:1, :]
        o_ref[16:17, 0:128] = dhd_ref[0:1, :]
        o_ref[17:18, 0:128] = dhd_ref[1:2, :] * av_ref[...]
        o_ref[18:19, 0:128] = dhd_ref[2:3, :]
        o_ref[19:20, 0:128] = dsink_ref[0:1, :]
        o_ref[20:21, 0:128] = loss_ref[0:1, :]
        o_ref[24:56, 0:128] = drel_ref[...]

    return pl.pallas_call(body, name="pack_small", out_shape=jax.ShapeDtypeStruct((56, D), F32))(
        dsh1, p1, dsh2, p2, dg1a, dg1b, dg2, norm1, norm2, scale1, scale2, dcw, dcb, dfn,
        dnw_attn, dnw_ssm, dhd, av, dsink, drel, loss_acc)


def _pad_row(a, rows=1):
    return jnp.pad(a.reshape(rows, -1), ((0, 0), (0, D - a.size // rows)))


def kernel(x, c, ada_w, ada_b, norm1, w_in, conv_w, conv_b, dt_bias, A_log, D_skip, sinks, attn_out_norm, ssm_out_norm, w_o, norm2, w_gate_up, w_down, rel_bias, final_norm, loss_target, m_ada_w, m_ada_b, m_norm1, m_w_in, m_conv_w, m_conv_b, m_dt_bias, m_A_log, m_D_skip, m_sinks, m_attn_out_norm, m_ssm_out_norm, m_w_o, m_norm2, m_w_gate_up, m_w_down, m_rel_bias, m_final_norm, v_ada_w, v_ada_b, v_norm1, v_w_in, v_conv_w, v_conv_b, v_dt_bias, v_A_log, v_D_skip, v_sinks, v_attn_out_norm, v_ssm_out_norm, v_w_o, v_norm2, v_w_gate_up, v_w_down, v_rel_bias, v_final_norm):
    xi, yi, ci = lax.axis_index("x"), lax.axis_index("y"), lax.axis_index("c")
    chip = 2 * xi + yi
    me = 4 * xi + 2 * yi + ci
    where = jnp.stack([ci, chip]).astype(jnp.int32)
    xs2, tgt = x[0], loss_target[0]

    first = jnp.concatenate([c, _pad_row(conv_w[0], CK), jnp.zeros((3, D), F32)], axis=0)
    w_in_t, m_w_in_t, v_w_in_t = w_in[0].T, m_w_in[0].T, v_w_in[0].T
    w_in_b, w_o_b, w_dn_b = w_in_t.astype(BF16), w_o[0].astype(BF16), w_down[0].astype(BF16)
    w_gu_b = w_gate_up[0].astype(BF16)
    hw = D // 2
    fetch_half = _Carry(
        [w_in_b], [jax.ShapeDtypeStruct((4,) + w_in_b.shape, BF16)],
        lambda x_, y_, c_: [(None, 0, slice(None), 0, 2 * x_ + y_)] + [
            (f, 0, (slice(None), pl.ds(c_ * hw, hw)), 0, (2 * x_ + y_, slice(None), pl.ds(c_ * hw, hw))) for f in _CHIPS3])

    def swap_halves(x_, y_, c_):
        there = [(jnp.bitwise_xor(2 * x_ + y_, k + 1), slice(None), pl.ds(c_ * hw, hw)) for k in range(3)]
        return [(_SIBLING, 1, at, 1, at) for at in there]

    first_all, w_in_g = _exchange_two("gather_first", _merge(_gather8_carry(first), fetch_half), swap_halves)
    c_all = first_all[:, 0, :]
    cw_full = jnp.concatenate([first_all[2 * j, 1:1 + CK, 0:256] for j in range(4)], axis=1)
    w_in_f = jnp.pad(w_in_g.reshape(IN_W, D), ((0, PROJ_W - IN_W), (0, 0)))

    ncol = ada_w.shape[2]
    mod_cols = _ada_fwd(c_all, ada_w[0], lax.dynamic_slice(ada_b, (0, chip * ncol), (1, ncol)))
    mod_all = _exchange("gather_mod", _gather_chips_carry([mod_cols]))[0]
    mod = lax.dynamic_slice(jnp.transpose(mod_all, (1, 0, 2)).reshape(8, 4 * ncol), (me, 0), (1, 4 * ncol))
    shift1, scale1, gate1, shift2, scale2, gate2 = [mod[:, j * D:(j + 1) * D] for j in range(6)]
    a1 = norm1 * (1.0 + scale1)
    a2 = norm2 * (1.0 + scale2)

    hdn = DFF // 8
    q, kv, z, xbc, dtr, w_o_g, w_dna_g = _in_proj_fwd(xs2, a1, shift1, w_in_f,
                                                      carry=_gather_chips_carry([w_o_b, w_dn_b[0:hdn]]))
    w_o_f = w_o_g.reshape(D, D)
    bucket, mask = _attn_geometry()
    bucket = jnp.asarray(bucket)
    bias = _bias_table(rel_bias, bucket, jnp.asarray(mask.astype(np.int32)))
    sinks1 = sinks[0]
    ya, w_ga_g = _attn_fwd(q, kv, bias, sinks1, attn_out_norm, carry=_gather_chips_carry([w_gu_b[0:D // 2]]))
    cw8 = jnp.concatenate([cw_full, jnp.zeros((4, XBCW), F32)], axis=0)
    dtb = _pad_row(dt_bias)[:, 0:128]
    av = _pad_row(-jnp.exp(A_log))[:, 0:128]
    dk = jnp.repeat(D_skip, HD, axis=1)
    ys, hs, pre, w_gb_g, w_dnb_g = _ssd_fwd(z, xbc, dtr, cw8, conv_b, dtb, av, dk, ssm_out_norm,
                                            carry=_gather_chips_carry([w_gu_b[D // 2:D], w_dn_b[hdn:2 * hdn]]))
    w_dn_f = jnp.stack([w_dna_g, w_dnb_g], axis=1).reshape(DFF, D)
    fn = final_norm[None, :]
    x1, gu, dx2, loss_acc, dfn = _mlp_fwd(xs2, ya, ys, tgt, w_o_f, w_ga_g, w_gb_g, w_dn_f, gate1, a2, shift2, gate2, fn)

    def to_sibling(p):
        return _Carry([p], [jax.ShapeDtypeStruct((4,) + p.shape[2:], F32)],
                      lambda x_, y_, c_: [(_SIBLING, 0, (j, 1 - c_), 0, j) for j in range(4)])

    def to_chips(s4):
        return _Carry([s4], [jax.ShapeDtypeStruct((3,) + s4.shape[1:], s4.dtype)],
                      lambda x_, y_, c_: [(f, 0, jnp.bitwise_xor(2 * x_ + y_, k + 1), 0, k) for k, f in enumerate(_CHIPS3)])

    def back(t):
        return _Carry([t[None]], [jax.ShapeDtypeStruct((1,) + t.shape, F32)], lambda x_, y_, c_: [(_SIBLING, 0, 0, 0, 0)])

    dx1, dya, dys, act, dgu, h2, dsh2, p2 = _mlp_bwd(x1, gu, dx2, w_o_f, w_ga_g, w_gb_g, w_dn_f, gate1, a2, shift2, gate2)
    dz, dxbc, ddt, dcw, dcb, dnw_ssm, dhd, g_gu = _ssd_bwd(
        z, xbc, pre, dtr, dys, hs, cw8, dtb, av, dk, ssm_out_norm, h2, dgu)
    p_gu = g_gu.reshape(4, 2, D // 2, 2 * DFF // 4)
    dq, dkv, dbias, dsink, dnw_attn, g_dn, dg2, got1_gu = _attn_bwd(
        q, kv, dya, bias, sinks1, attn_out_norm, act, dx2, gate2, w_dn_f, carry=to_sibling(p_gu))
    p_dn = g_dn.reshape(4, 2, DFF // 8, D)
    drel = _rel_bias_grad(dbias, bucket)
    s4_gu, own_gu = _add_half("rs_add_half_gu", p_gu, got1_gu, where)
    grad_x, h1, dsh1, p1 = _in_proj_bwd(xs2, dx1, a1, shift1, w_in_f, dq, dkv, dz, dxbc, ddt)
    g_in_t, got2_gu, got1_dn = _wgrad_in_t(h1, [dq, dkv, dz, dxbc, ddt],
                                           carry=_merge(to_chips(s4_gu), to_sibling(p_dn)))
    mine_gu = _add_chips("rs_add_chips_gu", own_gu, got2_gu)
    s4_dn, own_dn = _add_half("rs_add_half_dn", p_dn, got1_dn, where)
    p_in = g_in_t[0:IN_W].reshape(4, IN_W // 4, D)

    def to_sibling_cols(p):
        return _Carry([p], [jax.ShapeDtypeStruct(p.shape[:2] + (D // 2,), F32)],
                      lambda x_, y_, c_: [(_SIBLING, 0, (j, slice(None), pl.ds((1 - c_) * (D // 2), D // 2)), 0, j)
                                          for j in range(4)])

    g_oa, dg1a, got1_in, got2_dn, got3_gu = _wgrad(
        "wgrad_o_attn", ya, dx1, gate1, w_o_f[0:QW],
        carry=_merge(to_sibling_cols(p_in), to_chips(s4_dn), back(mine_gu)))
    mine_dn = _add_chips("rs_add_chips_dn", own_dn, got2_dn)
    s4_in, own_in = _add_half("rs_add_half_in", p_in, got1_in, where, by_cols=True)
    g_os, dg1b, got2_in, got3_dn = _wgrad("wgrad_o_ssm", ys, dx1, gate1, w_o_f[QW:D],
                                          carry=_merge(to_chips(s4_in), back(mine_dn)))
    mine_in = _add_chips("rs_add_chips_in", own_in, got2_in)
    p_o = jnp.concatenate([g_oa, g_os], axis=0).reshape(4, 2, D // 8, D)

    small = _pack_small(dsh1, p1, dsh2, p2, dg1a, dg1b, dg2, norm1, norm2, scale1, scale2, dcw, dcb, dfn,
                        dnw_attn, dnw_ssm, dhd, av, dsink, drel, loss_acc)
    small_all, got1_o, got3_in = _exchange(
        "gather_small", _merge(_gather8_carry(small), to_sibling(p_o), back(mine_in)))
    s4_o, own_o = _add_half("rs_add_half_o", p_o, got1_o, where)
    mine_o = _add_chips("rs_add_chips_o", own_o, _exchange("rs_chips_o", to_chips(s4_o))[0])
    got3_o = _exchange("rs_back_o", back(mine_o))[0]
    small_res, loss = _small_update(
        small_all, where,
        [ada_b, norm1, conv_w, conv_b, dt_bias, A_log, D_skip, sinks, attn_out_norm, ssm_out_norm, norm2, rel_bias,
         final_norm[None, :]],
        [m_ada_b, m_norm1, m_conv_w, m_conv_b, m_dt_bias, m_A_log, m_D_skip, m_sinks, m_attn_out_norm,
         m_ssm_out_norm, m_norm2, m_rel_bias, m_final_norm[None, :]],
        [v_ada_b, v_norm1, v_conv_w, v_conv_b, v_dt_bias, v_A_log, v_D_skip, v_sinks, v_attn_out_norm,
         v_ssm_out_norm, v_norm2, v_rel_bias, v_final_norm[None, :]])
    small_out = [dict(zip(_SMALL, r)) for r in small_res]
    for r in small_out:
        r["final_norm"] = r["final_norm"][0]

    dmod_all = small_all[:, 0:6, :].reshape(8, 6 * D)
    dmod_loc = lax.dynamic_slice(dmod_all, (0, chip * ncol), (8, ncol))
    ada_out = _ada_bwd_adamw(c_all.T, dmod_loc, ada_w[0], m_ada_w[0], v_ada_w[0])

    big_gu = _adamw_halves("adamw_gate_up", mine_gu, got3_gu[0], w_gate_up, m_w_gate_up, v_w_gate_up, where)
    big_dn = _adamw_halves("adamw_down", mine_dn, got3_dn[0], w_down, m_w_down, v_w_down, where)
    big_o = _adamw_halves("adamw_o", mine_o, got3_o[0], w_o, m_w_o, v_w_o, where)
    big_in = [o.T[None] for o in _adamw_halves("adamw_in", mine_in, got3_in[0], w_in_t, m_w_in_t, v_w_in_t, where,
                                               by_cols=True)]
    big = [big_in, big_o, big_gu, big_dn]

    order = ["ada_w", "ada_b", "norm1", "w_in", "conv_w", "conv_b", "dt_bias", "A_log", "D_skip", "sinks",
             "attn_out_norm", "ssm_out_norm", "w_o", "norm2", "w_gate_up", "w_down", "rel_bias", "final_norm"]
    bigname = {"w_in": 0, "w_o": 1, "w_gate_up": 2, "w_down": 3}
    res = [loss, grad_x[None]]
    for kind in range(4):
        for nm in order:
            if nm == "ada_w":
                res.append(ada_out[kind][None])
            elif nm in bigname:
                res.append(big[bigname[nm]][kind])
            else:
                res.append(small_out[kind][nm])
    return tuple(res)
```

```python
import functools

import numpy as np
import jax
import jax.numpy as jnp
from jax import lax
from jax.experimental import pallas as pl
from jax.experimental.pallas import tpu as pltpu

F32, BF16 = jnp.float32, jnp.bfloat16
HI = lax.Precision.HIGHEST

D = 1024
QW, KVW = 512, 128
NH, HD, NKV = 8, 64, 2
SW = 512
NST = 128
XBCW = 1024
CK = 4
BLK = 128
DFF = 2816
IN_W = 2312
PROJ_W = 2432
EPS = 1e-6
NEG = -1e30
NBUCKET = 32

B1, B2, LR, AEPS, WD, STEP = 0.9, 0.999, 0.001, 1e-08, 0.01, 10

VMEM_LIMIT = 56 * 1024 * 1024

_NT = (((1,), (1,)), ((), ()))
_TN = (((0,), (0,)), ((), ()))


def _mm(a, b):
    return jnp.dot(a, b, preferred_element_type=F32)


def _mm_nt(a, b):
    return lax.dot_general(a, b, _NT, preferred_element_type=F32)


def _mm_tn(a, b):
    return lax.dot_general(a, b, _TN, preferred_element_type=F32)


def _mm_hi(a, b):
    return jnp.dot(a, b, preferred_element_type=F32, precision=HI)


def _split3(x):
    hi = x.astype(BF16)
    r = x - hi.astype(F32)
    mid = r.astype(BF16)
    lo = (r - mid.astype(F32)).astype(BF16)
    return hi, mid, lo


def _sel_r(x, e):
    hi, mid, lo = _split3(x)
    return (_mm(hi, e) + _mm(mid, e)) + _mm(lo, e)


def _sel_l(e, x):
    hi, mid, lo = _split3(x)
    return (_mm(e, hi) + _mm(e, mid)) + _mm(e, lo)


def _sig(x):
    return 1.0 / (1.0 + jnp.exp(-x))


def _cp(sem):
    return pltpu.CompilerParams(dimension_semantics=sem, vmem_limit_bytes=VMEM_LIMIT)


def _row(shape):
    nd = len(shape)
    return pl.BlockSpec(shape, lambda *_: (0,) * nd)


def _adamw(w, g, m, v):
    m = B1 * m + (1.0 - B1) * g
    v = B2 * v + (1.0 - B2) * (g * g)
    m_hat = m / (1.0 - B1 ** STEP)
    v_hat = v / (1.0 - B2 ** STEP)
    delta = -LR * (m_hat / (jnp.sqrt(v_hat) + AEPS) + WD * w)
    return delta, m, v


class _Carry:
    def __init__(self, inps, outs, copies):
        self.inps, self.outs, self.copies = list(inps), list(outs), copies
        self.n = len(copies(0, 0, 0))

    def descriptors(self, in_refs, out_refs, send_sems, recv_sems):
        x, y, c = lax.axis_index("x"), lax.axis_index("y"), lax.axis_index("c")
        out = []
        for j, (flip, a, si, o, di) in enumerate(self.copies(x, y, c)):
            if flip is None:
                out.append(pltpu.make_async_copy(in_refs[a].at[si], out_refs[o].at[di], send_sems.at[j]))
            else:
                fx, fy, fc = flip
                peer = (1 - x if fx else x, 1 - y if fy else y, 1 - c if fc else c)
                out.append(pltpu.make_async_remote_copy(
                    src_ref=in_refs[a].at[si], dst_ref=out_refs[o].at[di],
                    send_sem=send_sems.at[j], recv_sem=recv_sems.at[j],
                    device_id=peer, device_id_type=pl.DeviceIdType.MESH))
        return out


def _pcall(body, args, *, name, grid, in_specs, out_specs, out_shape, scratch_shapes=(), sem=None, nprefetch=0,
           carry=None):
    out_shape, out_specs = list(out_shape), list(out_specs)
    in_specs, scratch_shapes = list(in_specs), list(scratch_shapes)
    nin, nout, nscr = len(in_specs), len(out_shape), len(scratch_shapes)
    run = body
    if carry is not None:
        ncin, ncout = len(carry.inps), len(carry.outs)
        hbm = pl.BlockSpec(memory_space=pl.ANY)

        def run(*refs):
            pre, r = refs[:nprefetch], refs[nprefetch:]
            ins, cins = r[:nin], r[nin:nin + ncin]
            r = r[nin + ncin:]
            outs, couts = r[:nout], r[nout:nout + ncout]
            r = r[nout + ncout:]
            scr, (send_sems, recv_sems) = r[:nscr], r[nscr:]
            first = pl.program_id(0) == 0
            last = pl.program_id(0) == grid[0] - 1
            for ax in range(1, len(grid)):
                first = jnp.logical_and(first, pl.program_id(ax) == 0)
                last = jnp.logical_and(last, pl.program_id(ax) == grid[ax] - 1)

            @pl.when(first)
            def _():
                for d in carry.descriptors(cins, couts, send_sems, recv_sems):
                    d.start()

            body(*pre, *ins, *outs, *scr)

            @pl.when(last)
            def _():
                for d in carry.descriptors(cins, couts, send_sems, recv_sems):
                    d.wait()

        in_specs = in_specs + [hbm] * ncin
        out_specs = out_specs + [hbm] * ncout
        out_shape = out_shape + carry.outs
        scratch_shapes = scratch_shapes + [pltpu.SemaphoreType.DMA((carry.n,)), pltpu.SemaphoreType.DMA((carry.n,))]
        args = list(args) + carry.inps
    if sem is None:
        sem = ("arbitrary",) * len(grid)
    if nprefetch:
        kw = dict(grid_spec=pltpu.PrefetchScalarGridSpec(num_scalar_prefetch=nprefetch, grid=grid, in_specs=in_specs,
                                                         out_specs=out_specs, scratch_shapes=scratch_shapes))
    else:
        kw = dict(grid=grid, in_specs=in_specs, out_specs=out_specs, scratch_shapes=scratch_shapes)
    res = pl.pallas_call(run, name=name, out_shape=out_shape, compiler_params=_cp(sem), **kw)(*args)
    return list(res)


def _merge(*carries):
    inps, outs, offs = [], [], []
    for cr in carries:
        offs.append((len(inps), len(outs)))
        inps += cr.inps
        outs += cr.outs

    def copies(x, y, c):
        return [(f, a + io, si, o + oo, di) for cr, (io, oo) in zip(carries, offs) for f, a, si, o, di in cr.copies(x, y, c)]

    return _Carry(inps, outs, copies)


def _exchange(name, carry):
    return _pcall(lambda: None, [], name=name, grid=(1,), in_specs=[], out_specs=[], out_shape=[], carry=carry)


def _exchange_two(name, carry, then):
    second = _Carry([], [], then)
    nin, nout = len(carry.inps), len(carry.outs)

    def body(*refs):
        ins, outs = refs[:nin], refs[nin:nin + nout]
        send_a, recv_a, send_b, recv_b = refs[nin + nout:]
        for descs in (carry.descriptors(ins, outs, send_a, recv_a), second.descriptors(outs, outs, send_b, recv_b)):
            for d in descs:
                d.start()
            for d in descs:
                d.wait()

    hbm = pl.BlockSpec(memory_space=pl.ANY)
    return list(pl.pallas_call(
        body, name=name, out_shape=carry.outs, in_specs=[hbm] * nin, out_specs=[hbm] * nout,
        scratch_shapes=[pltpu.SemaphoreType.DMA((carry.n,)), pltpu.SemaphoreType.DMA((carry.n,)),
                        pltpu.SemaphoreType.DMA((second.n,)), pltpu.SemaphoreType.DMA((second.n,))],
    )(*carry.inps))


_ALL7 = [(f >> 2 & 1, f >> 1 & 1, f & 1) for f in range(1, 8)]
_CHIPS3 = [(0, 1, 0), (1, 0, 0), (1, 1, 0)]
_SIBLING = (0, 0, 1)


def _gather8_carry(blk):
    def copies(x, y, c):
        me = 4 * x + 2 * y + c
        return [(None, 0, 0, 0, me)] + [(f, 0, 0, 0, me) for f in _ALL7]

    return _Carry([blk[None]], [jax.ShapeDtypeStruct((8,) + blk.shape, blk.dtype)], copies)


def _gather_chips_carry(blks):
    def copies(x, y, c):
        chip = 2 * x + y
        return [(f, a, 0, a, chip) for a in range(len(blks)) for f in [None] + _CHIPS3]

    return _Carry([b[None] for b in blks], [jax.ShapeDtypeStruct((4,) + b.shape, b.dtype) for b in blks], copies)


def _ada_fwd(c_all, w_loc, b_loc):
    n = w_loc.shape[1]
    tn = 512

    def body(c_ref, w_ref, b_ref, o_ref):
        cv = c_ref[...]
        cond = cv * _sig(cv)
        o_ref[...] = _mm_hi(cond, w_ref[...]) + b_ref[...]

    return pl.pallas_call(
        body, name="ada_fwd", grid=(n // tn,),
        out_shape=jax.ShapeDtypeStruct((8, n), F32),
        in_specs=[_row((8, D)), pl.BlockSpec((D, tn), lambda j: (0, j)), pl.BlockSpec((1, tn), lambda j: (0, j))],
        out_specs=pl.BlockSpec((8, tn), lambda j: (0, j)),
        compiler_params=_cp(("parallel",)),
    )(c_all, w_loc, b_loc)


def _ada_bwd_adamw(c_all_t, dmod_loc, w, m, v, carry=None):
    n = w.shape[1]
    tn = 512

    def body(ct_ref, dm_ref, w_ref, m_ref, v_ref, g_ref, d_ref, mo_ref, vo_ref):
        ct = ct_ref[...]
        cond = ct * _sig(ct)
        dm = dm_ref[...]
        g = cond[:, 0:1] * dm[0:1, :]
        for b in range(1, 8):
            g = g + cond[:, b:b + 1] * dm[b:b + 1, :]
        g_ref[...] = g
        d_ref[...], mo_ref[...], vo_ref[...] = _adamw(w_ref[...], g, m_ref[...], v_ref[...])

    wspec = pl.BlockSpec((D, tn), lambda j: (0, j))
    return _pcall(
        body, [c_all_t, dmod_loc, w, m, v], name="ada_bwd_adamw", grid=(n // tn,),
        out_shape=[jax.ShapeDtypeStruct((D, n), F32)] * 4,
        in_specs=[_row((D, 8)), pl.BlockSpec((8, tn), lambda j: (0, j)), wspec, wspec, wspec],
        out_specs=[wspec] * 4, carry=carry)


def _in_proj_fwd(x, a1, sh1, w_in, carry=None):
    s = x.shape[0]
    tm = 512

    def body(x_ref, a_ref, s_ref, w_ref, q_ref, kv_ref, z_ref, xbc_ref, dt_ref):
        def norm(rows):
            xv = x_ref[rows, :]
            r = lax.rsqrt(jnp.mean(xv * xv, axis=-1, keepdims=True) + EPS)
            return (xv * r * a_ref[...] + s_ref[...]).astype(BF16)

        def project(rows, h):
            p = _mm_nt(h, w_ref[...])
            q_ref[rows, :] = p[:, 0:512].astype(BF16)
            kv_ref[rows, :] = p[:, 512:768].astype(BF16)
            z_ref[rows, :] = p[:, 768:1280]
            xbc_ref[rows, :] = p[:, 1280:2304]
            dt_ref[rows, :] = p[:, 2304:2432]

        r0, r1 = slice(0, tm // 2), slice(tm // 2, tm)
        h0 = norm(r0)
        project(r0, h0)
        project(r1, norm(r1))

    def tok(w):
        return pl.BlockSpec((tm, w), lambda i: (i, 0))

    return _pcall(
        body, [x, a1, sh1, w_in], name="in_proj_fwd", grid=(s // tm,),
        out_shape=[jax.ShapeDtypeStruct((s, QW), BF16), jax.ShapeDtypeStruct((s, 2 * KVW), BF16),
                   jax.ShapeDtypeStruct((s, SW), F32), jax.ShapeDtypeStruct((s, XBCW), F32),
                   jax.ShapeDtypeStruct((s, 128), F32)],
        in_specs=[tok(D), _row((1, D)), _row((1, D)), _row((PROJ_W, D))],
        out_specs=[tok(QW), tok(2 * KVW), tok(SW), tok(XBCW), tok(128)], carry=carry)


def _in_proj_bwd(x, dx1, a1, sh1, w_in, dq, dkv, dz, dxbc, ddt, carry=None):
    s = x.shape[0]
    tm = 512

    def body(x_ref, dx1_ref, a_ref, s_ref, w_ref, dq_ref, dkv_ref, dz_ref, dxbc_ref, ddt_ref,
             gx_ref, h_ref, dsh_ref, p_ref):
        i = pl.program_id(0)

        @pl.when(i == 0)
        def _():
            dsh_ref[...] = jnp.zeros_like(dsh_ref)
            p_ref[...] = jnp.zeros_like(p_ref)

        def gather(st):
            rows = st["rows"]
            st["dproj"] = jnp.concatenate([dq_ref[rows, :], dkv_ref[rows, :], dz_ref[rows, :], dxbc_ref[rows, :],
                                           ddt_ref[rows, :]], axis=1)

        def back(st):
            st["dh"] = _mm(st.pop("dproj"), w_ref[...])

        def norm(st):
            rows, dh = st["rows"], st.pop("dh")
            xv = x_ref[rows, :]
            r = lax.rsqrt(jnp.mean(xv * xv, axis=-1, keepdims=True) + EPS)
            xn = xv * r
            a = a_ref[...]
            h_ref[rows, :] = (xn * a + s_ref[...]).astype(BF16)
            st["dsh"] = jnp.sum(dh, axis=0, keepdims=True)
            st["p"] = jnp.sum(dh * xn, axis=0, keepdims=True)
            u = dh * a
            gx_ref[rows, :] = dx1_ref[rows, :] + r * u - xn * (r * jnp.mean(u * xn, axis=-1, keepdims=True))

        g0, g1 = [dict(rows=slice(k * (tm // 2), (k + 1) * (tm // 2))) for k in range(2)]
        for stage, st in [(gather, g0), (back, g0), (gather, g1), (norm, g0), (back, g1), (norm, g1)]:
            stage(st)
        dsh_ref[0:1, :] += g0["dsh"] + g1["dsh"]
        p_ref[0:1, :] += g0["p"] + g1["p"]

    def tok(w):
        return pl.BlockSpec((tm, w), lambda i: (i, 0))

    return _pcall(
        body, [x, dx1, a1, sh1, w_in, dq, dkv, dz, dxbc, ddt], name="in_proj_bwd", grid=(s // tm,),
        out_shape=[jax.ShapeDtypeStruct((s, D), F32), jax.ShapeDtypeStruct((s, D), BF16),
                   jax.ShapeDtypeStruct((8, D), F32), jax.ShapeDtypeStruct((8, D), F32)],
        in_specs=[tok(D), tok(D), _row((1, D)), _row((1, D)), _row((PROJ_W, D)),
                  tok(QW), tok(2 * KVW), tok(SW), tok(XBCW), tok(128)],
        out_specs=[tok(D), tok(D), _row((8, D)), _row((8, D))], carry=carry)


def _attn_geometry():
    dist = np.arange(BLK)[:, None] + BLK - np.arange(2 * BLK)[None, :]
    n = np.maximum(dist, 0)
    max_exact = NBUCKET // 2
    large = max_exact + (np.log(np.maximum(n, 1) / max_exact) / np.log(128 / max_exact)
                         * (NBUCKET - max_exact)).astype(np.int32)
    large = np.minimum(large, NBUCKET - 1)
    bucket = np.where(n < max_exact, n, large).astype(np.int32)
    mask = (dist >= 0) & (dist < 128)
    return bucket, mask


def _attn_heads(is_first, q_blk, kvw, bias_ref, sinks_ref):
    qv = q_blk * 0.125
    col = lax.broadcasted_iota(jnp.int32, (BLK, 2 * BLK), 1)
    first = jnp.where(jnp.logical_and(is_first, col < BLK), NEG, 0.0)
    groups = []
    for g in range(NKV):
        qs = jnp.concatenate([qv[:, (4 * g + r) * HD:(4 * g + r + 1) * HD] for r in range(4)], axis=0)
        kw = kvw[:, g * HD:(g + 1) * HD]
        vw = kvw[:, KVW + g * HD:KVW + (g + 1) * HD]
        sc = _mm_nt(qs, kw)
        pn, ps = [], []
        for r in range(4):
            h = 4 * g + r
            sr = sc[r * BLK:(r + 1) * BLK] + bias_ref[h] + first
            sink = sinks_ref[h]
            m = jnp.maximum(jnp.max(sr, axis=-1, keepdims=True), sink)
            p = jnp.exp(sr - m)
            es = jnp.exp(sink - m)
            inv = 1.0 / (jnp.sum(p, axis=-1, keepdims=True) + es)
            pn.append(p * inv)
            ps.append(es * inv)
        pn = jnp.concatenate(pn, axis=0)
        ps = jnp.concatenate(ps, axis=0)
        o = _mm(pn.astype(BF16), vw)
        groups.append((qs, kw, vw, pn, ps, o))
    return groups


def _unstack_heads(parts):
    return jnp.concatenate([p[r * BLK:(r + 1) * BLK] for p in parts for r in range(4)], axis=1)


NB = 2


def _attn_fwd(q, kv, bias, sinks, nw, carry=None):
    s = q.shape[0]

    def body(q_ref, kvp_ref, kvc_ref, bias_ref, sinks_ref, nw_ref, y_ref):
        t = pl.program_id(0)
        kv3 = jnp.concatenate([kvp_ref[...], kvc_ref[...]], axis=0)
        for sub in range(NB):
            rows = slice(sub * BLK, (sub + 1) * BLK)
            groups = _attn_heads(jnp.logical_and(t == 0, sub == 0), q_ref[rows, :], kv3[sub * BLK:(sub + 2) * BLK],
                                 bias_ref, sinks_ref)
            o = _unstack_heads([g[5] for g in groups])
            r = lax.rsqrt(jnp.mean(o * o, axis=-1, keepdims=True) + EPS)
            y_ref[rows, :] = (o * r * nw_ref[...]).astype(BF16)

    return _pcall(
        body, [q, kv, kv, bias, sinks, nw], name="attn_fwd", grid=(s // (NB * BLK),),
        out_shape=[jax.ShapeDtypeStruct((s, QW), BF16)],
        in_specs=[pl.BlockSpec((NB * BLK, QW), lambda t: (t, 0)),
                  pl.BlockSpec((BLK, 2 * KVW), lambda t: (jnp.maximum(NB * t - 1, 0), 0)),
                  pl.BlockSpec((NB * BLK, 2 * KVW), lambda t: (t, 0)),
                  _row((NH, BLK, 2 * BLK)),
                  pl.BlockSpec(memory_space=pltpu.SMEM),
                  _row((1, QW))],
        out_specs=[pl.BlockSpec((NB * BLK, QW), lambda t: (t, 0))], carry=carry)


def _attn_bwd(q, kv, dya, bias, sinks, nw, h2, dgu, carry=None):
    s = q.shape[0]
    nt = s // (NB * BLK)
    ngu = dgu.shape[1]
    npiece = D // NB

    def body(q_ref, kvp_ref, kvc_ref, dy_ref, bias_ref, sinks_ref, nw_ref, h2_ref, dgu_ref,
             dq_ref, dkv_ref, dbias_ref, dsink_ref, dnw_ref, ggu_hbm, carry_ref, held_ref, acc_ref, sems):
        t = pl.program_id(0)

        @pl.when(t == 0)
        def _():
            carry_ref[...] = jnp.zeros_like(carry_ref)
            held_ref[...] = jnp.zeros_like(held_ref)
            dbias_ref[...] = jnp.zeros_like(dbias_ref)
            dsink_ref[...] = jnp.zeros_like(dsink_ref)
            dnw_ref[...] = jnp.zeros_like(dnw_ref)
            acc_ref[...] = jnp.zeros_like(acc_ref)

        def wgrad_piece(sub):
            rows = slice(sub * npiece, (sub + 1) * npiece)
            acc_ref[rows, :] += _mm_tn(h2_ref[:, rows], dgu_ref[...])

        def block(sub, kv3):
            rows = slice(sub * BLK, (sub + 1) * BLK)
            groups = _attn_heads(jnp.logical_and(t == 0, sub == 0), q_ref[rows, :], kv3[sub * BLK:(sub + 2) * BLK],
                                 bias_ref, sinks_ref)
            o = _unstack_heads([g[5] for g in groups])
            r = lax.rsqrt(jnp.mean(o * o, axis=-1, keepdims=True) + EPS)
            dy = dy_ref[rows, :]
            on = o * r
            dnw_ref[0:1, :] += jnp.sum(dy * on, axis=0, keepdims=True)
            u = dy * nw_ref[...]
            do = r * u - on * (r * jnp.mean(u * on, axis=-1, keepdims=True))
            dq_parts, dk_parts, dv_parts = [], [], []
            for g, (qs, kw, vw, pn, ps, og) in enumerate(groups):
                dos = jnp.concatenate([do[:, (4 * g + r_) * HD:(4 * g + r_ + 1) * HD] for r_ in range(4)], axis=0)
                delta = jnp.sum(dos * og, axis=-1, keepdims=True)
                dp = _mm_nt(dos.astype(BF16), vw)
                ds = pn * (dp - delta)
                dsk = ps * delta
                lane = lax.broadcasted_iota(jnp.int32, (1, 128), 1)
                for r_ in range(4):
                    h = 4 * g + r_
                    dbias_ref[h] += ds[r_ * BLK:(r_ + 1) * BLK]
                    dsink_ref[0:1, :] -= jnp.where(lane == h, jnp.sum(dsk[r_ * BLK:(r_ + 1) * BLK]), 0.0)
                dsb = ds.astype(BF16)
                dq_parts.append(_mm(dsb, kw) * 0.125)
                dk_parts.append(_mm_tn(dsb, qs))
                dv_parts.append(_mm_tn(pn.astype(BF16), dos.astype(BF16)))
            dq_ref[rows, :] = _unstack_heads(dq_parts).astype(BF16)
            return jnp.concatenate(dk_parts + dv_parts, axis=1)

        @pl.when(t < nt)
        def _():
            kv3 = jnp.concatenate([kvp_ref[...], kvc_ref[...]], axis=0)
            tail = carry_ref[...]
            for sub in range(NB):
                d = block(sub, kv3)
                done = tail + d[0:BLK]
                if sub == 0:
                    dkv_ref[0:(NB - 1) * BLK, :] = held_ref[...].astype(BF16)
                    dkv_ref[(NB - 1) * BLK:NB * BLK, :] = done.astype(BF16)
                else:
                    held_ref[(sub - 1) * BLK:sub * BLK, :] = done
                tail = d[BLK:2 * BLK]
                wgrad_piece(sub)
            carry_ref[...] = tail

        @pl.when(t == nt)
        def _():
            dkv_ref[0:(NB - 1) * BLK, :] = held_ref[...].astype(BF16)
            dkv_ref[(NB - 1) * BLK:NB * BLK, :] = carry_ref[...].astype(BF16)
            nj = ngu // 4
            cps = [pltpu.make_async_copy(acc_ref.at[:, pl.ds(j * nj, nj)], ggu_hbm.at[j], sems.at[j]) for j in range(4)]
            for cp in cps:
                cp.start()
            for cp in cps:
                cp.wait()

    last = nt - 1
    tile = lambda w: pl.BlockSpec((NB * BLK, w), lambda t: (jnp.minimum(t, last), 0))
    return _pcall(
        body, [q, kv, kv, dya, bias, sinks, nw, h2, dgu], name="attn_bwd", grid=(nt + 1,),
        out_shape=[jax.ShapeDtypeStruct((s, QW), BF16), jax.ShapeDtypeStruct((s, 2 * KVW), BF16),
                   jax.ShapeDtypeStruct((NH, BLK, 2 * BLK), F32), jax.ShapeDtypeStruct((NH, 128), F32),
                   jax.ShapeDtypeStruct((8, QW), F32), jax.ShapeDtypeStruct((4, D, ngu // 4), F32)],
        in_specs=[tile(QW),
                  pl.BlockSpec((BLK, 2 * KVW), lambda t: (jnp.clip(NB * t - 1, 0, NB * nt - 1), 0)),
                  tile(2 * KVW), tile(QW),
                  _row((NH, BLK, 2 * BLK)),
                  pl.BlockSpec(memory_space=pltpu.SMEM),
                  _row((1, QW)), tile(D), tile(ngu)],
        out_specs=[tile(QW),
                   pl.BlockSpec((NB * BLK, 2 * KVW), lambda t: (jnp.maximum(t - 1, 0), 0)),
                   _row((NH, BLK, 2 * BLK)), _row((NH, 128)), _row((8, QW)),
                   pl.BlockSpec(memory_space=pl.ANY)],
        scratch_shapes=[pltpu.VMEM((BLK, 2 * KVW), F32), pltpu.VMEM(((NB - 1) * BLK, 2 * KVW), F32),
                        pltpu.VMEM((D, ngu), F32), pltpu.SemaphoreType.DMA((4,))], carry=carry)


def _rel_bias_grad(dbias, bucket):
    def body(db_ref, bk_ref, o_ref):
        bk = bk_ref[...]
        lane = lax.broadcasted_iota(jnp.int32, (1, 128), 1)
        for b in range(NBUCKET):
            sel = bk == b
            row = jnp.zeros((1, 128), F32)
            for h in range(NH):
                row = row + jnp.where(lane == h, jnp.sum(jnp.where(sel, db_ref[h], 0.0)), 0.0)
            o_ref[b:b + 1, :] = row

    return pl.pallas_call(
        body, name="rel_bias_grad",
        out_shape=jax.ShapeDtypeStruct((NBUCKET, 128), F32),
    )(dbias, bucket)


def _ssd_consts():
    head_of_lane = np.arange(SW) // HD
    expand = (np.arange(128)[:, None] == head_of_lane[None, :]).astype(np.float32)
    tril = np.tril(np.ones((BLK, BLK), np.float32))
    return (jnp.asarray(expand, BF16), jnp.asarray(expand.T.copy(), BF16), jnp.asarray(tril, BF16),
            jnp.asarray(tril.T.copy(), BF16))


def _conv_pre(xc, halo, cw, cb):
    ext = jnp.concatenate([halo, xc], axis=0)
    taps = [xc if k == CK - 1 else pltpu.roll(ext, CK - 1 - k, 0)[8:8 + BLK] for k in range(CK)]
    return cb + sum(cw[k:k + 1, :] * taps[k] for k in range(CK))


def _ssd_chunk(pre, dtr, dtb, av, dkv, ex, tril, h_in):
    sp = _sig(pre)
    xbc = pre * sp
    xs, bm, cm = xbc[:, 0:SW], xbc[:, SW:SW + 2 * NST], xbc[:, SW + 2 * NST:]
    dtin = dtr + dtb
    dt = jnp.maximum(dtin, 0.0) + jnp.log1p(jnp.exp(-jnp.abs(dtin)))
    cs = _sel_l(tril, dt * av)
    cst = cs.T
    dtx = _sel_r(dt, ex)
    csx = _sel_r(cs, ex)
    xdt = xs * dtx
    csl = csx[BLK - 1:BLK, :]
    decx = jnp.exp(csl - csx)
    ecsx = jnp.exp(csx)
    ecl = jnp.exp(csl)
    causal = tril.astype(F32) > 0.5
    ydiag, yoff, cbs, lms = [], [], [], []
    for g in range(2):
        bg = bm[:, g * NST:(g + 1) * NST].astype(BF16)
        cg = cm[:, g * NST:(g + 1) * NST].astype(BF16)
        cb = _mm_nt(cg, bg)
        cbs.append(cb)
        yoff.append(_mm(cg, h_in[:, g * 256:(g + 1) * 256].astype(BF16)))
        for r in range(4):
            h = 4 * g + r
            seg = cs[:, h:h + 1] - cst[h:h + 1, :]
            lm = jnp.where(causal, jnp.exp(jnp.minimum(seg, 0.0)), 0.0)
            lms.append(lm)
            ydiag.append(_mm((cb * lm).astype(BF16), xdt[:, h * HD:(h + 1) * HD].astype(BF16)))
    yoff = jnp.concatenate(yoff, axis=1) * ecsx
    y = jnp.concatenate(ydiag, axis=1) + yoff + dkv * xs
    return dict(pre=pre, sp=sp, xs=xs, bm=bm, cm=cm, dtin=dtin, dt=dt, av=av, cs=cs, cst=cst,
                dtx=dtx, csx=csx, xdt=xdt, decx=decx, ecsx=ecsx, ecl=ecl, causal=causal, cbs=cbs, lms=lms,
                yoff=yoff, y=y)


def _group_mean(t):
    m0 = jnp.mean(t[:, 0:256], axis=-1, keepdims=True)
    m1 = jnp.mean(t[:, 256:512], axis=-1, keepdims=True)
    return jnp.concatenate([jnp.broadcast_to(m0, (t.shape[0], 256)), jnp.broadcast_to(m1, (t.shape[0], 256))], axis=1)


SUBS = 2


def _ssd_fwd(z, xbc, dtr, cw, cb, dtb, av, dk, nw, carry=None):
    s = z.shape[0]
    nc = s // BLK
    tile = SUBS * BLK
    ex, _, tril, _ = _ssd_consts()

    def body(z_ref, xc_ref, xh_ref, dtr_ref, cw_ref, cb_ref, dtb_ref, a_ref, dk_ref, nw_ref, ex_ref, tril_ref,
             y_ref, hs_ref, pre_ref, h_ref):
        t = pl.program_id(0)

        @pl.when(t == 0)
        def _():
            h_ref[...] = jnp.zeros_like(h_ref)

        h_in = h_ref[...]
        for sub in range(SUBS):
            rows = slice(sub * BLK, (sub + 1) * BLK)
            xc = xc_ref[rows, :]
            halo = jnp.where(t == 0, 0.0, xh_ref[...]) if sub == 0 else xc_ref[sub * BLK - 8:sub * BLK, :]
            pre = _conv_pre(xc, halo, cw_ref[...], cb_ref[...])
            pre_ref[rows, :] = pre
            hs_ref[sub] = h_in
            f = _ssd_chunk(pre, dtr_ref[rows, :], dtb_ref[...], a_ref[...], dk_ref[...], ex_ref[...], tril_ref[...], h_in)
            dx = (f["decx"] * f["xdt"]).astype(BF16)
            st = [_mm_tn(f["bm"][:, g * NST:(g + 1) * NST].astype(BF16), dx[:, g * 256:(g + 1) * 256]) for g in range(2)]
            h_in = h_in * f["ecl"] + jnp.concatenate(st, axis=1)
            zv = z_ref[rows, :]
            tg = f["y"] * (zv * _sig(zv))
            r = lax.rsqrt(_group_mean(tg * tg) + EPS)
            y_ref[rows, :] = (tg * r * nw_ref[...]).astype(BF16)
        h_ref[...] = h_in

    cur = lambda w: pl.BlockSpec((tile, w), lambda t: (t, 0))
    return _pcall(
        body, [z, xbc, xbc, dtr, cw, cb, dtb, av, dk, nw, ex, tril], name="ssd_fwd", grid=(s // tile,),
        out_shape=[jax.ShapeDtypeStruct((s, SW), BF16), jax.ShapeDtypeStruct((nc, NST, SW), F32),
                   jax.ShapeDtypeStruct((s, XBCW), F32)],
        in_specs=[cur(SW), cur(XBCW), pl.BlockSpec((8, XBCW), lambda t: (jnp.maximum(t * (tile // 8) - 1, 0), 0)),
                  cur(128), _row((8, XBCW)), _row((1, XBCW)), _row((1, 128)),
                  _row((1, 128)), _row((1, SW)), _row((1, SW)), _row((128, SW)), _row((BLK, BLK))],
        out_specs=[cur(SW), pl.BlockSpec((SUBS, NST, SW), lambda t: (t, 0, 0)), cur(XBCW)],
        scratch_shapes=[pltpu.VMEM((NST, SW), F32)], carry=carry)


def _ssd_bwd(z, xbc, pre_all, dtr, dys, hs, cw, dtb, av, dk, nw, act, dx2, gate2, w_dn, carry=None):
    s = z.shape[0]
    tile = SUBS * BLK
    nt = s // tile
    ex, ext_t, tril, triu = _ssd_consts()
    npiece = DFF // SUBS

    def body(z_ref, xc_ref, pre_ref, dtr_ref, dy_ref, hs_ref, cw_ref, dtb_ref, a_ref, dk_ref, nw_ref,
             ex_ref, ext_ref, tril_ref, triu_ref, act_ref, dx2_ref, g2_ref, wdn_ref,
             dz_ref, dxbc_ref, ddt_ref, dcw_ref, dcb_ref, dnw_ref, dhd_ref, gdn_hbm, dg2_ref, dh_ref, nxt_ref, dd_ref,
             acc_ref, sem):
        i = pl.program_id(0)

        @pl.when(i == 0)
        def _():
            dh_ref[...] = jnp.zeros_like(dh_ref)
            nxt_ref[...] = jnp.zeros_like(nxt_ref)
            dd_ref[...] = jnp.zeros_like(dd_ref)
            dcw_ref[...] = jnp.zeros_like(dcw_ref)
            dcb_ref[...] = jnp.zeros_like(dcb_ref)
            dnw_ref[...] = jnp.zeros_like(dnw_ref)
            dhd_ref[...] = jnp.zeros_like(dhd_ref)
            acc_ref[...] = jnp.zeros_like(acc_ref)

        gst, nxt = dh_ref[...], nxt_ref[...]
        for sub in reversed(range(SUBS)):
            rows = slice(sub * BLK, (sub + 1) * BLK)
            gst, nxt = chunk(sub, rows, gst, nxt, z_ref, xc_ref, pre_ref, dtr_ref, dy_ref, hs_ref, cw_ref, dtb_ref,
                             a_ref, dk_ref, nw_ref, ex_ref, ext_ref, tril_ref, triu_ref,
                             dz_ref, dxbc_ref, ddt_ref, dcw_ref, dcb_ref, dnw_ref, dhd_ref, dd_ref,
                             functools.partial(wgrad_part, sub, act_ref[:, sub * npiece:(sub + 1) * npiece].T,
                                               dx2_ref, acc_ref))
        dh_ref[...] = gst
        nxt_ref[...] = nxt

        @pl.when(i == nt - 1)
        def _():
            dhd_ref[2:3, :] = _sel_r(dd_ref[...], ext_ref[...])[0:1, :]
            acc = acc_ref[...]
            dg2_ref[...] = jnp.zeros_like(dg2_ref)
            dg2_ref[0:1, :] = jnp.sum(acc * wdn_ref[...].astype(F32), axis=0, keepdims=True)
            acc_ref[...] = acc * g2_ref[...]
            cp = pltpu.make_async_copy(acc_ref, gdn_hbm, sem)
            cp.start()
            cp.wait()

    def wgrad_part(sub, act_t, dx2_ref, acc_ref, h):
        if h % 2:
            return
        piece = slice(sub * npiece, (sub + 1) * npiece)
        cols = slice((h // 2) * 256, (h // 2 + 1) * 256)
        acc_ref[piece, cols] += _mm(act_t, dx2_ref[:, cols].astype(BF16))

    def chunk(sub, rows, gst, nxt, z_ref, xc_ref, pre_ref, dtr_ref, dy_ref, hs_ref, cw_ref, dtb_ref,
              a_ref, dk_ref, nw_ref, ex_ref, ext_ref, tril_ref, triu_ref,
              dz_ref, dxbc_ref, ddt_ref, dcw_ref, dcb_ref, dnw_ref, dhd_ref, dd_ref, after_head):
        h_in = hs_ref[sub]
        f = _ssd_chunk(pre_ref[rows, :], dtr_ref[rows, :], dtb_ref[...], a_ref[...], dk_ref[...], ex_ref[...],
                       tril_ref[...], h_in)
        xs, xdt, decx, ecsx, ecl, dtx = f["xs"], f["xdt"], f["decx"], f["ecsx"], f["ecl"], f["dtx"]
        cs, cst, causal = f["cs"], f["cst"], f["causal"]
        causal_t = triu_ref[...].astype(F32) > 0.5

        zv = z_ref[rows, :]
        sz = _sig(zv)
        gz = zv * sz
        t = f["y"] * gz
        r = lax.rsqrt(_group_mean(t * t) + EPS)
        tn_ = t * r
        dyn = dy_ref[rows, :]
        dnw_ref[0:1, :] += jnp.sum(dyn * tn_, axis=0, keepdims=True)
        u = dyn * nw_ref[...]
        dt_ = r * u - tn_ * (r * _group_mean(u * tn_))
        dy = dt_ * gz
        dz_ref[rows, :] = (dt_ * f["y"] * (sz * (1.0 + zv * (1.0 - sz)))).astype(BF16)

        dd_ref[0:1, :] += jnp.sum(dy * xs, axis=0, keepdims=True)
        dxs = dk_ref[...] * dy

        edy = ecsx * dy
        dxdt, dbs, dcs_, dcsx_parts, dh_new = [], [], [], [], []
        lane = lax.broadcasted_iota(jnp.int32, (1, 128), 1)
        dcs_intra = jnp.zeros((BLK, 128), F32)
        for g in range(2):
            sl = slice(g * 256, (g + 1) * 256)
            bgf, cgf = f["bm"][:, g * NST:(g + 1) * NST], f["cm"][:, g * NST:(g + 1) * NST]
            bg, cg = bgf.astype(BF16), cgf.astype(BF16)
            gg = gst[:, sl].astype(BF16)
            hg = h_in[:, sl].astype(BF16)
            edyg = edy[:, sl].astype(BF16)
            dc = _mm_nt(edyg, hg)
            dh_new.append(gst[:, sl] * ecl[:, sl] + _mm_tn(cg, edyg))
            bgm = _mm(bg, gg)
            dxdt_g = decx[:, sl] * bgm
            dxg = (decx[:, sl] * xdt[:, sl]).astype(BF16)
            db = _mm_nt(dxg, gg)
            qd = bgm * xdt[:, sl] * decx[:, sl]
            last = jnp.sum(qd, axis=0, keepdims=True) + ecl[:, sl] * jnp.sum(gst[:, sl] * h_in[:, sl], axis=0, keepdims=True)
            rowid = lax.broadcasted_iota(jnp.int32, (BLK, 256), 0)
            dcsx_parts.append(f["yoff"][:, sl] * dy[:, sl] - qd + jnp.where(rowid == BLK - 1, last, 0.0))
            cb_ = f["cbs"][g]
            cbt = _mm_nt(bg, cg)
            dcb_ = jnp.zeros((BLK, BLK), F32)
            dcbt = jnp.zeros((BLK, BLK), F32)
            dxd = []
            for r_ in range(4):
                h = 4 * g + r_
                hl = slice(h * HD, (h + 1) * HD)
                lm = f["lms"][h]
                segt = cst[h:h + 1, :] - cs[:, h:h + 1]
                lmt = jnp.where(causal_t, jnp.exp(jnp.minimum(segt, 0.0)), 0.0)
                dyh = dy[:, hl].astype(BF16)
                xdh = xdt[:, hl].astype(BF16)
                dw = _mm_nt(dyh, xdh)
                dwt = _mm_nt(xdh, dyh)
                wt = cbt * lmt
                dxd.append(_mm(wt.astype(BF16), dyh))
                dcb_ = dcb_ + dw * lm
                dcbt = dcbt + dwt * lmt
                col = jnp.sum(dw * (cb_ * lm), axis=-1, keepdims=True) - jnp.sum(dwt * wt, axis=-1, keepdims=True)
                dcs_intra = dcs_intra + jnp.where(lane == h, col, 0.0)
                after_head(h)
            dxdt.append(dxdt_g + jnp.concatenate(dxd, axis=1))
            dcs_.append(dc + _mm(dcb_.astype(BF16), bg))
            dbs.append(db + _mm(dcbt.astype(BF16), cg))
        dxdt = jnp.concatenate(dxdt, axis=1)
        dxs = dxs + dxdt * dtx
        ext_t_ = ext_ref[...]
        dcs = dcs_intra + _sel_r(jnp.concatenate(dcsx_parts, axis=1), ext_t_)
        da = _sel_l(triu_ref[...], dcs)
        ddt = da * f["av"] + _sel_r(dxdt * xs, ext_t_)
        dhd_ref[1:2, :] += jnp.sum(da * f["dt"], axis=0, keepdims=True)
        ddtr = ddt * _sig(f["dtin"])
        dhd_ref[0:1, :] += jnp.sum(ddtr, axis=0, keepdims=True)
        ddt_ref[rows, :] = ddtr.astype(BF16)

        sp, pre = f["sp"], f["pre"]
        dact = jnp.concatenate([dxs] + dbs + dcs_, axis=1)
        dpre = dact * (sp * (1.0 + pre * (1.0 - sp)))
        dcb_ref[0:1, :] += jnp.sum(dpre, axis=0, keepdims=True)
        ext2 = jnp.concatenate([dpre, nxt], axis=0)
        shifted = [pltpu.roll(ext2, BLK + 8 - (CK - 1 - k), 0)[0:BLK] for k in range(CK - 1)] + [dpre]
        cw = cw_ref[...]
        xc = xc_ref[rows, :]
        dxr = cw[CK - 1:CK, :] * dpre
        for k in range(CK):
            dcw_ref[k:k + 1, :] += jnp.sum(shifted[k] * xc, axis=0, keepdims=True)
            if k < CK - 1:
                dxr = dxr + cw[k:k + 1, :] * shifted[k]
        dxbc_ref[rows, :] = dxr.astype(BF16)
        return jnp.concatenate(dh_new, axis=1), dpre[0:8]

    cur = lambda w: pl.BlockSpec((tile, w), lambda i: (nt - 1 - i, 0))
    return _pcall(
        body, [z, xbc, pre_all, dtr, dys, hs, cw, dtb, av, dk, nw, ex, ext_t, tril, triu, act, dx2, gate2, w_dn],
        name="ssd_bwd", grid=(nt,),
        out_shape=[jax.ShapeDtypeStruct((s, SW), BF16), jax.ShapeDtypeStruct((s, XBCW), BF16),
                   jax.ShapeDtypeStruct((s, 128), BF16), jax.ShapeDtypeStruct((8, XBCW), F32),
                   jax.ShapeDtypeStruct((8, XBCW), F32), jax.ShapeDtypeStruct((8, SW), F32),
                   jax.ShapeDtypeStruct((8, 128), F32), jax.ShapeDtypeStruct((DFF, D), F32),
                   jax.ShapeDtypeStruct((8, D), F32)],
        in_specs=[cur(SW), cur(XBCW), cur(XBCW), cur(128), cur(SW),
                  pl.BlockSpec((SUBS, NST, SW), lambda i: (nt - 1 - i, 0, 0)),
                  _row((8, XBCW)), _row((1, 128)), _row((1, 128)), _row((1, SW)), _row((1, SW)),
                  _row((128, SW)), _row((SW, 128)), _row((BLK, BLK)), _row((BLK, BLK)),
                  cur(DFF), cur(D), _row((1, D)), _row((DFF, D))],
        out_specs=[cur(SW), cur(XBCW), cur(128), _row((8, XBCW)), _row((8, XBCW)), _row((8, SW)), _row((8, 128)),
                   pl.BlockSpec(memory_space=pl.ANY), _row((8, D))],
        scratch_shapes=[pltpu.VMEM((NST, SW), F32), pltpu.VMEM((8, XBCW), F32), pltpu.VMEM((8, SW), F32),
                        pltpu.VMEM((DFF, D), F32), pltpu.SemaphoreType.DMA], carry=carry)


def _load_once(i, pairs, sem):
    @pl.when(i == 0)
    def _():
        cps = [pltpu.make_async_copy(src, dst, sem.at[k]) for k, (src, dst) in enumerate(pairs)]
        for cp in cps:
            cp.start()
        for cp in cps:
            cp.wait()


def _mlp_fwd(x, ya, ys, tgt, w_o, w_ga, w_gb, w_dn, gate1, a2, sh2, gate2, fn):
    s = x.shape[0]
    sub_m, subs = 256, 2
    tm = sub_m * subs

    def body(x_ref, ya_ref, ys_ref, t_ref, wo_hbm, wga_hbm, wgb_hbm, wdn_hbm, g1_ref, a2_ref, s2_ref, g2_ref, fn_ref,
             x1_ref, gu_ref, dx2_ref, loss_ref, dfn_ref, wo, wga, wgb, wdn, sem):
        i = pl.program_id(0)
        _load_once(i, [(wo_hbm, wo), (wga_hbm, wga), (wgb_hbm, wgb), (wdn_hbm, wdn)], sem)

        @pl.when(i == 0)
        def _():
            loss_ref[...] = jnp.zeros_like(loss_ref)
            dfn_ref[...] = jnp.zeros_like(dfn_ref)

        def proj(st):
            st["mix"] = _mm(ya_ref[st["rows"], :], wo[0:QW, :]) + _mm(ys_ref[st["rows"], :], wo[QW:D, :])

        def norm(st):
            x1 = x_ref[st["rows"], :] + g1_ref[...] * st.pop("mix")
            x1_ref[st["rows"], :] = x1
            r2 = lax.rsqrt(jnp.mean(x1 * x1, axis=-1, keepdims=True) + EPS)
            st["x1"] = x1
            st["h2"] = (x1 * r2 * a2_ref[...] + s2_ref[...]).astype(BF16)

        def gate_up(st):
            h2 = st.pop("h2")
            ha, hb = h2[:, 0:D // 2], h2[:, D // 2:D]
            gub = jnp.concatenate([(_mm(ha, wga[j]) + _mm(hb, wgb[j])).astype(BF16) for j in range(4)], axis=1)
            gu_ref[st["rows"], :] = gub
            st["gub"] = gub

        def activate(st):
            gub = st.pop("gub")
            gv, uv = gub[:, 0:DFF].astype(F32), gub[:, DFF:].astype(F32)
            st["act"] = (gv * _sig(gv) * uv).astype(BF16)

        def down(st):
            st["ff"] = _mm(st.pop("act"), wdn[...])

        def head(st):
            x2 = st.pop("x1") + g2_ref[...] * st.pop("ff")
            r3 = lax.rsqrt(jnp.mean(x2 * x2, axis=-1, keepdims=True) + EPS)
            xn = x2 * r3
            fnv = fn_ref[...]
            err = xn * fnv - t_ref[st["rows"], :]
            st["loss"] = jnp.sum(err * err) * (0.5 / D)
            dy = err * (1.0 / D)
            st["dfn"] = jnp.sum(dy * xn, axis=0, keepdims=True)
            u = dy * fnv
            dx2_ref[st["rows"], :] = r3 * u - xn * (r3 * jnp.mean(u * xn, axis=-1, keepdims=True))

        a, b = [dict(rows=slice(k * sub_m, (k + 1) * sub_m)) for k in range(subs)]
        for stage, st in [(proj, a), (norm, a), (proj, b), (gate_up, a), (norm, b), (activate, a), (gate_up, b),
                          (down, a), (activate, b), (head, a), (down, b), (head, b)]:
            stage(st)
        loss_ref[...] += a["loss"] + b["loss"]
        dfn_ref[0:1, :] += a["dfn"] + b["dfn"]

    def tok(w):
        return pl.BlockSpec((tm, w), lambda i: (i, 0))

    hbm = pl.BlockSpec(memory_space=pl.ANY)
    return pl.pallas_call(
        body, name="mlp_fwd", grid=(s // tm,),
        out_shape=[jax.ShapeDtypeStruct((s, D), F32), jax.ShapeDtypeStruct((s, 2 * DFF), BF16),
                   jax.ShapeDtypeStruct((s, D), F32), jax.ShapeDtypeStruct((8, 128), F32),
                   jax.ShapeDtypeStruct((8, D), F32)],
        in_specs=[tok(D), tok(QW), tok(SW), tok(D), hbm, hbm, hbm, hbm,
                  _row((1, D)), _row((1, D)), _row((1, D)), _row((1, D)), _row((1, D))],
        out_specs=[tok(D), tok(2 * DFF), tok(D), _row((8, 128)), _row((8, D))],
        scratch_shapes=[pltpu.VMEM((D, D), BF16), pltpu.VMEM(w_ga.shape, BF16), pltpu.VMEM(w_gb.shape, BF16),
                        pltpu.VMEM((DFF, D), BF16), pltpu.SemaphoreType.DMA((4,))],
        compiler_params=_cp(("arbitrary",)),
    )(x, ya, ys, tgt, w_o, w_ga, w_gb, w_dn, gate1, a2, sh2, gate2, fn)


def _mlp_bwd(x1, gu, dx2, w_o, w_ga, w_gb, w_dn, gate1, a2, sh2, gate2):
    s = x1.shape[0]
    tm = 256
    nj = 2 * DFF // 4

    def body(x1_ref, gu_ref, dx2_ref, wo_hbm, wga_hbm, wgb_hbm, wdn_hbm, g1_ref, a2_ref, s2_ref, g2_ref,
             dx1_ref, dya_ref, dys_ref, act_ref, dgu_ref, h2_ref, dsh_ref, p_ref, wo, wga, wgb, wdn, sem):
        i = pl.program_id(0)
        _load_once(i, [(wo_hbm, wo), (wga_hbm, wga), (wgb_hbm, wgb), (wdn_hbm, wdn)], sem)

        @pl.when(i == 0)
        def _():
            dsh_ref[...] = jnp.zeros_like(dsh_ref)
            p_ref[...] = jnp.zeros_like(p_ref)

        dx2 = dx2_ref[...]
        dact = _mm_nt((dx2 * g2_ref[...]).astype(BF16), wdn[...])
        gub = gu_ref[...]
        gv, uv = gub[:, 0:DFF].astype(F32), gub[:, DFF:].astype(F32)
        sg = _sig(gv)
        sl = gv * sg
        act_ref[...] = (sl * uv).astype(BF16)
        dgu = jnp.concatenate([dact * uv * (sg * (1.0 + gv * (1.0 - sg))), dact * sl], axis=1).astype(BF16)
        dgu_ref[...] = dgu
        dha = sum(_mm_nt(dgu[:, j * nj:(j + 1) * nj], wga[j]) for j in range(4))
        dhb = sum(_mm_nt(dgu[:, j * nj:(j + 1) * nj], wgb[j]) for j in range(4))
        dh = jnp.concatenate([dha, dhb], axis=1)
        x1 = x1_ref[...]
        r2 = lax.rsqrt(jnp.mean(x1 * x1, axis=-1, keepdims=True) + EPS)
        xn = x1 * r2
        a2 = a2_ref[...]
        h2_ref[...] = (xn * a2 + s2_ref[...]).astype(BF16)
        dsh_ref[0:1, :] += jnp.sum(dh, axis=0, keepdims=True)
        p_ref[0:1, :] += jnp.sum(dh * xn, axis=0, keepdims=True)
        u = dh * a2
        dx1 = dx2 + r2 * u - xn * (r2 * jnp.mean(u * xn, axis=-1, keepdims=True))
        dx1_ref[...] = dx1
        dcat = _mm_nt((dx1 * g1_ref[...]).astype(BF16), wo[...])
        dya_ref[...] = dcat[:, 0:QW]
        dys_ref[...] = dcat[:, QW:D]

    def tok(w):
        return pl.BlockSpec((tm, w), lambda i: (i, 0))

    hbm = pl.BlockSpec(memory_space=pl.ANY)
    return pl.pallas_call(
        body, name="mlp_bwd", grid=(s // tm,),
        out_shape=[jax.ShapeDtypeStruct((s, D), F32), jax.ShapeDtypeStruct((s, QW), F32),
                   jax.ShapeDtypeStruct((s, SW), F32), jax.ShapeDtypeStruct((s, DFF), BF16),
                   jax.ShapeDtypeStruct((s, 2 * DFF), BF16), jax.ShapeDtypeStruct((s, D), BF16),
                   jax.ShapeDtypeStruct((8, D), F32), jax.ShapeDtypeStruct((8, D), F32)],
        in_specs=[tok(D), tok(2 * DFF), tok(D), hbm, hbm, hbm, hbm, _row((1, D)), _row((1, D)), _row((1, D)), _row((1, D))],
        out_specs=[tok(D), tok(QW), tok(SW), tok(DFF), tok(2 * DFF), tok(D), _row((8, D)), _row((8, D))],
        scratch_shapes=[pltpu.VMEM((D, D), BF16), pltpu.VMEM(w_ga.shape, BF16), pltpu.VMEM(w_gb.shape, BF16),
                        pltpu.VMEM((DFF, D), BF16), pltpu.SemaphoreType.DMA((4,))],
        compiler_params=_cp(("arbitrary",)),
    )(x1, gu, dx2, w_o, w_ga, w_gb, w_dn, gate1, a2, sh2, gate2)


def _wgrad(name, a, b, gate, w, carry=None):
    s, m = a.shape
    n = b.shape[1]
    tk = min(1024, s)
    nk = s // tk

    def body(a_ref, b_ref, g_ref, w_ref, o_hbm, dg_ref, acc_ref, sem):
        k = pl.program_id(0)

        @pl.when(k == 0)
        def _():
            acc_ref[...] = jnp.zeros_like(acc_ref)

        acc_ref[...] += _mm_tn(a_ref[...], b_ref[...].astype(BF16))

        @pl.when(k == nk - 1)
        def _():
            acc = acc_ref[...]
            dg_ref[...] = jnp.zeros_like(dg_ref)
            dg_ref[0:1, :] = jnp.sum(acc * w_ref[...].astype(F32), axis=0, keepdims=True)
            acc_ref[...] = acc * g_ref[...]
            cp = pltpu.make_async_copy(acc_ref, o_hbm, sem)
            cp.start()
            cp.wait()

    return _pcall(body, [a, b, gate, w], name=name, grid=(nk,),
                  out_shape=[jax.ShapeDtypeStruct((m, n), F32), jax.ShapeDtypeStruct((8, n), F32)],
                  in_specs=[pl.BlockSpec((tk, m), lambda k: (k, 0)), pl.BlockSpec((tk, n), lambda k: (k, 0)),
                            _row((1, n)), _row((m, n))],
                  out_specs=[pl.BlockSpec(memory_space=pl.ANY), _row((8, n))],
                  scratch_shapes=[pltpu.VMEM((m, n), F32), pltpu.SemaphoreType.DMA], carry=carry)


def _wgrad_gate_up(h2, dgu, carry=None):
    s = h2.shape[0]
    tk = min(1024, s)
    nk = s // tk
    n = dgu.shape[1]
    nj = n // 4

    def body(a_ref, b_ref, o_hbm, acc_ref, sems):
        k = pl.program_id(0)

        @pl.when(k == 0)
        def _():
            acc_ref[...] = jnp.zeros_like(acc_ref)

        acc_ref[...] += _mm_tn(a_ref[...], b_ref[...])

        @pl.when(k == nk - 1)
        def _():
            cps = [pltpu.make_async_copy(acc_ref.at[:, pl.ds(j * nj, nj)], o_hbm.at[j], sems.at[j]) for j in range(4)]
            for cp in cps:
                cp.start()
            for cp in cps:
                cp.wait()

    return _pcall(body, [h2, dgu], name="wgrad_gate_up", grid=(nk,),
                  out_shape=[jax.ShapeDtypeStruct((4, D, nj), F32)],
                  in_specs=[pl.BlockSpec((tk, D), lambda k: (k, 0)), pl.BlockSpec((tk, n), lambda k: (k, 0))],
                  out_specs=[pl.BlockSpec(memory_space=pl.ANY)],
                  scratch_shapes=[pltpu.VMEM((D, n), F32), pltpu.SemaphoreType.DMA((4,))], carry=carry)


def _wgrad_in_t(h1, pieces, carry=None):
    s = h1.shape[0]
    tk = min(1024, s)
    nk = s // tk

    def body(a_ref, dq_ref, dkv_ref, dz_ref, dxbc_ref, ddt_ref, o_hbm, acc_ref, tr_ref, sem):
        k = pl.program_id(0)

        @pl.when(k == 0)
        def _():
            acc_ref[...] = jnp.zeros_like(acc_ref)

        dproj = jnp.concatenate([dq_ref[...], dkv_ref[...], dz_ref[...], dxbc_ref[...], ddt_ref[...]], axis=1)
        acc_ref[...] += _mm_tn(a_ref[...], dproj)

        @pl.when(k == nk - 1)
        def _():
            for j in range(PROJ_W // 128):
                tr_ref[j * 128:(j + 1) * 128, :] = acc_ref[:, j * 128:(j + 1) * 128].T
            cp = pltpu.make_async_copy(tr_ref, o_hbm, sem)
            cp.start()
            cp.wait()

    return _pcall(body, [h1] + list(pieces), name="wgrad_in", grid=(nk,),
                  out_shape=[jax.ShapeDtypeStruct((PROJ_W, D), F32)],
                  in_specs=[pl.BlockSpec((tk, p.shape[1]), lambda k: (k, 0)) for p in [h1] + list(pieces)],
                  out_specs=[pl.BlockSpec(memory_space=pl.ANY)],
                  scratch_shapes=[pltpu.VMEM((D, PROJ_W), F32), pltpu.VMEM((PROJ_W, D), F32), pltpu.SemaphoreType.DMA],
                  carry=carry)


_SMALL = ["ada_b", "norm1", "conv_w", "conv_b", "dt_bias", "A_log", "D_skip", "sinks", "attn_out_norm",
          "ssm_out_norm", "norm2", "rel_bias", "final_norm"]


def _small_grad(name, gs, chip):
    if name == "ada_b":
        return jnp.concatenate([gs[j:j + 1, :] for j in range(6)], axis=1)
    if name == "conv_w":
        full = gs[7:11, :]
        out = full[:, 0:256]
        for j in range(1, 4):
            out = jnp.where(chip == j, full[:, j * 256:(j + 1) * 256], out)
        return out
    row, width = {"norm1": (6, D), "conv_b": (11, D), "norm2": (12, D), "final_norm": (13, D),
                  "attn_out_norm": (14, QW), "ssm_out_norm": (15, SW), "dt_bias": (16, NH), "A_log": (17, NH),
                  "D_skip": (18, NH), "sinks": (19, NH), "rel_bias": (24, NH)}[name]
    rows = NBUCKET if name == "rel_bias" else 1
    return gs[row:row + rows, 0:width]


def _small_update(small_all, where, ws, ms, vs):
    n = len(_SMALL)

    def body(where_ref, sa_ref, *refs):
        w_refs, m_refs, v_refs, outs = refs[:n], refs[n:2 * n], refs[2 * n:3 * n], refs[3 * n:]
        gs = sa_ref[0]
        for b in range(1, 8):
            gs = gs + sa_ref[b]
        chip = where_ref[1]
        for i, name in enumerate(_SMALL):
            g = _small_grad(name, gs, chip)
            lead = (0,) if name == "conv_w" else ()
            d, mo, vo = _adamw(w_refs[i][lead + (...,)], g, m_refs[i][lead + (...,)], v_refs[i][lead + (...,)])
            for k, val in enumerate((g, d, mo, vo)):
                outs[k * n + i][lead + (...,)] = val
        outs[4 * n][...] = gs[20:21, 0:128]

    shapes = [jax.ShapeDtypeStruct(w.shape, F32) for w in ws]
    vmem = pl.BlockSpec(memory_space=pltpu.VMEM)
    res = pl.pallas_call(
        body, name="small_update", out_shape=shapes * 4 + [jax.ShapeDtypeStruct((1, 128), F32)],
        in_specs=[pl.BlockSpec(memory_space=pltpu.SMEM)] + [vmem] * (1 + 3 * n), out_specs=[vmem] * (4 * n + 1),
    )(where, small_all, *ws, *ms, *vs)
    return [res[k * n:(k + 1) * n] for k in range(4)], res[4 * n][0, 0]


def _add_half(name, g, got, where, by_cols=False):
    rr, cc = got.shape[1:]
    if by_cols:
        mine = pl.BlockSpec((None, rr, cc), lambda i, w_ref: (i, 0, w_ref[0]))
    else:
        mine = pl.BlockSpec((None, None, rr, cc), lambda i, w_ref: (i, w_ref[0], 0, 0))

    def body(w_ref, g_ref, r_ref, o_ref, own_ref):
        s = g_ref[...] + r_ref[...]
        o_ref[...] = s.astype(BF16)

        @pl.when(pl.program_id(0) == w_ref[1])
        def _():
            own_ref[...] = s

    spec = pl.BlockSpec((None, rr, cc), lambda i, w_ref: (i, 0, 0))
    return _pcall(body, [where, g, got], name=name, grid=(4,), nprefetch=1,
                  out_shape=[jax.ShapeDtypeStruct(got.shape, BF16), jax.ShapeDtypeStruct((rr, cc), F32)],
                  in_specs=[mine, spec],
                  out_specs=[spec, pl.BlockSpec((rr, cc), lambda i, w_ref: (0, 0))])


def _add_chips(name, own, got):
    rr, cc = own.shape
    tr = rr // 2 if rr % 32 == 0 else rr

    def body(s_ref, r_ref, o_ref):
        o_ref[...] = ((s_ref[...] + r_ref[0].astype(F32)) + r_ref[1].astype(F32)) + r_ref[2].astype(F32)

    spec = pl.BlockSpec((tr, cc), lambda i: (i, 0))
    return _pcall(body, [own, got], name=name, grid=(rr // tr,), out_shape=[jax.ShapeDtypeStruct((rr, cc), F32)],
                  in_specs=[spec, pl.BlockSpec((3, tr, cc), lambda i: (0, i, 0))], out_specs=[spec])[0]


def _adamw_halves(name, mine, got, w, m, v, where, by_cols=False):
    rr, cc = mine.shape

    def body(w_ref_, t_ref, r_ref, w_ref, m_ref, v_ref, g_ref, d_ref, mo_ref, vo_ref):
        g = jnp.where(pl.program_id(0) == w_ref_[0], t_ref[...], r_ref[...])
        g_ref[...] = g
        d_ref[...], mo_ref[...], vo_ref[...] = _adamw(w_ref[...], g, m_ref[...], v_ref[...])

    if by_cols:
        grid = (2, 1)
        half = pl.BlockSpec((rr, cc), lambda h, i, w_ref_: (0, 0))
        full = pl.BlockSpec((rr, cc), lambda h, i, w_ref_: (0, h))
    else:
        tr = rr // 2
        grid = (2, 2)
        half = pl.BlockSpec((tr, cc), lambda h, i, w_ref_: (i, 0))
        full = pl.BlockSpec((None, tr, cc), lambda h, i, w_ref_: (0, 2 * h + i, 0))
    return _pcall(body, [where, mine, got, w, m, v], name=name, grid=grid, nprefetch=1,
                  out_shape=[jax.ShapeDtypeStruct(w.shape, F32)] * 4,
                  in_specs=[half, half, full, full, full], out_specs=[full] * 4)


def _bias_table(rel_bias, bucket, mask):
    def body(rb_ref, bk_ref, mk_ref, o_ref):
        bk = bk_ref[...]
        valid = mk_ref[...] > 0
        for h in range(NH):
            acc = jnp.zeros((BLK, 2 * BLK), F32)
            for b in range(NBUCKET):
                acc = jnp.where(bk == b, rb_ref[b, h], acc)
            o_ref[h] = jnp.where(valid, acc, NEG)

    vmem = pl.BlockSpec(memory_space=pltpu.VMEM)
    return pl.pallas_call(
        body, name="bias_table", out_shape=jax.ShapeDtypeStruct((NH, BLK, 2 * BLK), F32),
        in_specs=[pl.BlockSpec(memory_space=pltpu.SMEM), vmem, vmem], out_specs=vmem,
    )(rel_bias, bucket, mask)


def _pack_small(dsh1, p1, dsh2, p2, dg1a, dg1b, dg2, norm1, norm2, scale1, scale2, dcw, dcb, dfn,
                dnw_attn, dnw_ssm, dhd, av, dsink, drel, loss_acc):
    def body(dsh1_ref, p1_ref, dsh2_ref, p2_ref, dg1a_ref, dg1b_ref, dg2_ref, n1_ref, n2_ref, s1_ref, s2_ref,
             dcw_ref, dcb_ref, dfn_ref, da_ref, ds_ref, dhd_ref, av_ref, dsink_ref, drel_ref, loss_ref, o_ref):
        o_ref[...] = jnp.zeros_like(o_ref)
        p1v, p2v = p1_ref[0:1, :], p2_ref[0:1, :]
        o_ref[0:1, :] = dsh1_ref[0:1, :]
        o_ref[1:2, :] = p1v * n1_ref[...]
        o_ref[2:3, :] = dg1a_ref[0:1, :] + dg1b_ref[0:1, :]
        o_ref[3:4, :] = dsh2_ref[0:1, :]
        o_ref[4:5, :] = p2v * n2_ref[...]
        o_ref[5:6, :] = dg2_ref[0:1, :]
        o_ref[6:7, :] = p1v * (1.0 + s1_ref[...])
        o_ref[7:11, :] = dcw_ref[0:4, :]
        o_ref[11:12, :] = dcb_ref[0:1, :]
        o_ref[12:13, :] = p2v * (1.0 + s2_ref[...])
        o_ref[13:14, :] = dfn_ref[0:1, :]
        o_ref[14:15, 0:QW] = da_ref[0:1, :]
        o_ref[15:16, 0:SW] = ds_ref[0:1, :]
        o_ref[16:17, 0:128] = dhd_ref[0:1, :]
        o_ref[17:18, 0:128] = dhd_ref[1:2, :] * av_ref[...]
        o_ref[18:19, 0:128] = dhd_ref[2:3, :]
        o_ref[19:20, 0:128] = dsink_ref[0:1, :]
        o_ref[20:21, 0:128] = loss_ref[0:1, :]
        o_ref[24:56, 0:128] = drel_ref[...]

    return pl.pallas_call(body, name="pack_small", out_shape=jax.ShapeDtypeStruct((56, D), F32))(
        dsh1, p1, dsh2, p2, dg1a, dg1b, dg2, norm1, norm2, scale1, scale2, dcw, dcb, dfn,
        dnw_attn, dnw_ssm, dhd, av, dsink, drel, loss_acc)


def _pad_row(a, rows=1):
    return jnp.pad(a.reshape(rows, -1), ((0, 0), (0, D - a.size // rows)))


def kernel(x, c, ada_w, ada_b, norm1, w_in, conv_w, conv_b, dt_bias, A_log, D_skip, sinks, attn_out_norm, ssm_out_norm, w_o, norm2, w_gate_up, w_down, rel_bias, final_norm, loss_target, m_ada_w, m_ada_b, m_norm1, m_w_in, m_conv_w, m_conv_b, m_dt_bias, m_A_log, m_D_skip, m_sinks, m_attn_out_norm, m_ssm_out_norm, m_w_o, m_norm2, m_w_gate_up, m_w_down, m_rel_bias, m_final_norm, v_ada_w, v_ada_b, v_norm1, v_w_in, v_conv_w, v_conv_b, v_dt_bias, v_A_log, v_D_skip, v_sinks, v_attn_out_norm, v_ssm_out_norm, v_w_o, v_norm2, v_w_gate_up, v_w_down, v_rel_bias, v_final_norm):
    xi, yi, ci = lax.axis_index("x"), lax.axis_index("y"), lax.axis_index("c")
    chip = 2 * xi + yi
    me = 4 * xi + 2 * yi + ci
    where = jnp.stack([ci, chip]).astype(jnp.int32)
    xs2, tgt = x[0], loss_target[0]

    first = jnp.concatenate([c, _pad_row(conv_w[0], CK), jnp.zeros((3, D), F32)], axis=0)
    w_in_t, m_w_in_t, v_w_in_t = w_in[0].T, m_w_in[0].T, v_w_in[0].T
    w_in_b, w_o_b, w_dn_b = w_in_t.astype(BF16), w_o[0].astype(BF16), w_down[0].astype(BF16)
    w_gu_b = w_gate_up[0].astype(BF16)
    hw = D // 2
    fetch_half = _Carry(
        [w_in_b], [jax.ShapeDtypeStruct((4,) + w_in_b.shape, BF16)],
        lambda x_, y_, c_: [(None, 0, slice(None), 0, 2 * x_ + y_)] + [
            (f, 0, (slice(None), pl.ds(c_ * hw, hw)), 0, (2 * x_ + y_, slice(None), pl.ds(c_ * hw, hw))) for f in _CHIPS3])

    def swap_halves(x_, y_, c_):
        there = [(jnp.bitwise_xor(2 * x_ + y_, k + 1), slice(None), pl.ds(c_ * hw, hw)) for k in range(3)]
        return [(_SIBLING, 1, at, 1, at) for at in there]

    first_all, w_in_g = _exchange_two("gather_first", _merge(_gather8_carry(first), fetch_half), swap_halves)
    c_all = first_all[:, 0, :]
    cw_full = jnp.concatenate([first_all[2 * j, 1:1 + CK, 0:256] for j in range(4)], axis=1)
    w_in_f = jnp.pad(w_in_g.reshape(IN_W, D), ((0, PROJ_W - IN_W), (0, 0)))

    ncol = ada_w.shape[2]
    mod_cols = _ada_fwd(c_all, ada_w[0], lax.dynamic_slice(ada_b, (0, chip * ncol), (1, ncol)))
    mod_all = _exchange("gather_mod", _gather_chips_carry([mod_cols]))[0]
    mod = lax.dynamic_slice(jnp.transpose(mod_all, (1, 0, 2)).reshape(8, 4 * ncol), (me, 0), (1, 4 * ncol))
    shift1, scale1, gate1, shift2, scale2, gate2 = [mod[:, j * D:(j + 1) * D] for j in range(6)]
    a1 = norm1 * (1.0 + scale1)
    a2 = norm2 * (1.0 + scale2)

    hdn = DFF // 8
    q, kv, z, xbc, dtr, w_o_g, w_dna_g = _in_proj_fwd(xs2, a1, shift1, w_in_f,
                                                      carry=_gather_chips_carry([w_o_b, w_dn_b[0:hdn]]))
    w_o_f = w_o_g.reshape(D, D)
    bucket, mask = _attn_geometry()
    bucket = jnp.asarray(bucket)
    bias = _bias_table(rel_bias, bucket, jnp.asarray(mask.astype(np.int32)))
    sinks1 = sinks[0]
    ya, w_ga_g = _attn_fwd(q, kv, bias, sinks1, attn_out_norm, carry=_gather_chips_carry([w_gu_b[0:D // 2]]))
    cw8 = jnp.concatenate([cw_full, jnp.zeros((4, XBCW), F32)], axis=0)
    dtb = _pad_row(dt_bias)[:, 0:128]
    av = _pad_row(-jnp.exp(A_log))[:, 0:128]
    dk = jnp.repeat(D_skip, HD, axis=1)
    ys, hs, pre, w_gb_g, w_dnb_g = _ssd_fwd(z, xbc, dtr, cw8, conv_b, dtb, av, dk, ssm_out_norm,
                                            carry=_gather_chips_carry([w_gu_b[D // 2:D], w_dn_b[hdn:2 * hdn]]))
    w_dn_f = jnp.stack([w_dna_g, w_dnb_g], axis=1).reshape(DFF, D)
    fn = final_norm[None, :]
    x1, gu, dx2, loss_acc, dfn = _mlp_fwd(xs2, ya, ys, tgt, w_o_f, w_ga_g, w_gb_g, w_dn_f, gate1, a2, shift2, gate2, fn)

    def to_sibling(p):
        return _Carry([p], [jax.ShapeDtypeStruct((4,) + p.shape[2:], F32)],
                      lambda x_, y_, c_: [(_SIBLING, 0, (j, 1 - c_), 0, j) for j in range(4)])

    def to_chips(s4):
        return _Carry([s4], [jax.ShapeDtypeStruct((3,) + s4.shape[1:], s4.dtype)],
                      lambda x_, y_, c_: [(f, 0, jnp.bitwise_xor(2 * x_ + y_, k + 1), 0, k) for k, f in enumerate(_CHIPS3)])

    def back(t):
        return _Carry([t[None]], [jax.ShapeDtypeStruct((1,) + t.shape, F32)], lambda x_, y_, c_: [(_SIBLING, 0, 0, 0, 0)])

    dx1, dya, dys, act, dgu, h2, dsh2, p2 = _mlp_bwd(x1, gu, dx2, w_o_f, w_ga_g, w_gb_g, w_dn_f, gate1, a2, shift2, gate2)
    dq, dkv, dbias, dsink, dnw_attn, g_gu = _attn_bwd(q, kv, dya, bias, sinks1, attn_out_norm, h2, dgu)
    p_gu = g_gu.reshape(4, 2, D // 2, 2 * DFF // 4)
    drel = _rel_bias_grad(dbias, bucket)
    got1_gu = _exchange("rs_sibling_gate_up", to_sibling(p_gu))[0]
    s4_gu, own_gu = _add_half("rs_add_half_gu", p_gu, got1_gu, where)
    dz, dxbc, ddt, dcw, dcb, dnw_ssm, dhd, g_dn, dg2, got2_gu = _ssd_bwd(
        z, xbc, pre, dtr, dys, hs, cw8, dtb, av, dk, ssm_out_norm, act, dx2, gate2, w_dn_f, carry=to_chips(s4_gu))
    mine_gu = _add_chips("rs_add_chips_gu", own_gu, got2_gu)
    p_dn = g_dn.reshape(4, 2, DFF // 8, D)
    grad_x, h1, dsh1, p1 = _in_proj_bwd(xs2, dx1, a1, shift1, w_in_f, dq, dkv, dz, dxbc, ddt)
    g_in_t, got1_dn, got3_gu = _wgrad_in_t(h1, [dq, dkv, dz, dxbc, ddt],
                                           carry=_merge(to_sibling(p_dn), back(mine_gu)))
    s4_dn, own_dn = _add_half("rs_add_half_dn", p_dn, got1_dn, where)
    p_in = g_in_t[0:IN_W].reshape(4, IN_W // 4, D)

    def to_sibling_cols(p):
        return _Carry([p], [jax.ShapeDtypeStruct(p.shape[:2] + (D // 2,), F32)],
                      lambda x_, y_, c_: [(_SIBLING, 0, (j, slice(None), pl.ds((1 - c_) * (D // 2), D // 2)), 0, j)
                                          for j in range(4)])

    g_oa, dg1a, got1_in, got2_dn = _wgrad("wgrad_o_attn", ya, dx1, gate1, w_o_f[0:QW],
                                          carry=_merge(to_sibling_cols(p_in), to_chips(s4_dn)))
    mine_dn = _add_chips("rs_add_chips_dn", own_dn, got2_dn)
    s4_in, own_in = _add_half("rs_add_half_in", p_in, got1_in, where, by_cols=True)
    g_os, dg1b, got2_in, got3_dn = _wgrad("wgrad_o_ssm", ys, dx1, gate1, w_o_f[QW:D],
                                          carry=_merge(to_chips(s4_in), back(mine_dn)))
    mine_in = _add_chips("rs_add_chips_in", own_in, got2_in)
    p_o = jnp.concatenate([g_oa, g_os], axis=0).reshape(4, 2, D // 8, D)

    small = _pack_small(dsh1, p1, dsh2, p2, dg1a, dg1b, dg2, norm1, norm2, scale1, scale2, dcw, dcb, dfn,
                        dnw_attn, dnw_ssm, dhd, av, dsink, drel, loss_acc)
    small_all, got1_o, got3_in = _exchange(
        "gather_small", _merge(_gather8_carry(small), to_sibling(p_o), back(mine_in)))
    s4_o, own_o = _add_half("rs_add_half_o", p_o, got1_o, where)
    mine_o = _add_chips("rs_add_chips_o", own_o, _exchange("rs_chips_o", to_chips(s4_o))[0])
    got3_o = _exchange("rs_back_o", back(mine_o))[0]
    small_res, loss = _small_update(
        small_all, where,
        [ada_b, norm1, conv_w, conv_b, dt_bias, A_log, D_skip, sinks, attn_out_norm, ssm_out_norm, norm2, rel_bias,
         final_norm[None, :]],
        [m_ada_b, m_norm1, m_conv_w, m_conv_b, m_dt_bias, m_A_log, m_D_skip, m_sinks, m_attn_out_norm,
         m_ssm_out_norm, m_norm2, m_rel_bias, m_final_norm[None, :]],
        [v_ada_b, v_norm1, v_conv_w, v_conv_b, v_dt_bias, v_A_log, v_D_skip, v_sinks, v_attn_out_norm,
         v_ssm_out_norm, v_norm2, v_rel_bias, v_final_norm[None, :]])
    small_out = [dict(zip(_SMALL, r)) for r in small_res]
    for r in small_out:
        r["final_norm"] = r["final_norm"][0]

    dmod_all = small_all[:, 0:6, :].reshape(8, 6 * D)
    dmod_loc = lax.dynamic_slice(dmod_all, (0, chip * ncol), (8, ncol))
    ada_out = _ada_bwd_adamw(c_all.T, dmod_loc, ada_w[0], m_ada_w[0], v_ada_w[0])

    big_gu = _adamw_halves("adamw_gate_up", mine_gu, got3_gu[0], w_gate_up, m_w_gate_up, v_w_gate_up, where)
    big_dn = _adamw_halves("adamw_down", mine_dn, got3_dn[0], w_down, m_w_down, v_w_down, where)
    big_o = _adamw_halves("adamw_o", mine_o, got3_o[0], w_o, m_w_o, v_w_o, where)
    big_in = [o.T[None] for o in _adamw_halves("adamw_in", mine_in, got3_in[0], w_in_t, m_w_in_t, v_w_in_t, where,
                                               by_cols=True)]
    big = [big_in, big_o, big_gu, big_dn]

    order = ["ada_w", "ada_b", "norm1", "w_in", "conv_w", "conv_b", "dt_bias", "A_log", "D_skip", "sinks",
             "attn_out_norm", "ssm_out_norm", "w_o", "norm2", "w_gate_up", "w_down", "rel_bias", "final_norm"]
    bigname = {"w_in": 0, "w_o": 1, "w_gate_up": 2, "w_down": 3}
    res = [loss, grad_x[None]]
    for kind in range(4):
        for nm in order:
            if nm == "ada_w":
                res.append(ada_out[kind][None])
            elif nm in bigname:
                res.append(big[bigname[nm]][kind])
            else:
                res.append(small_out[kind][nm])
    return tuple(res)
```

```python
import numpy as np
import jax
import jax.numpy as jnp
from jax import lax
from jax.experimental import pallas as pl
from jax.experimental.pallas import tpu as pltpu

F32, BF16 = jnp.float32, jnp.bfloat16
HI = lax.Precision.HIGHEST

D = 1024
QW, KVW = 512, 128
NH, HD, NKV = 8, 64, 2
SW = 512
NST = 128
XBCW = 1024
CK = 4
BLK = 128
DFF = 2816
IN_W = 2312
PROJ_W = 2432
EPS = 1e-6
NEG = -1e30
NBUCKET = 32

B1, B2, LR, AEPS, WD, STEP = 0.9, 0.999, 0.001, 1e-08, 0.01, 10

VMEM_LIMIT = 56 * 1024 * 1024

_NT = (((1,), (1,)), ((), ()))
_TN = (((0,), (0,)), ((), ()))


def _mm(a, b):
    return jnp.dot(a, b, preferred_element_type=F32)


def _mm_nt(a, b):
    return lax.dot_general(a, b, _NT, preferred_element_type=F32)


def _mm_tn(a, b):
    return lax.dot_general(a, b, _TN, preferred_element_type=F32)


def _mm_hi(a, b):
    return jnp.dot(a, b, preferred_element_type=F32, precision=HI)


def _split3(x):
    hi = x.astype(BF16)
    r = x - hi.astype(F32)
    mid = r.astype(BF16)
    lo = (r - mid.astype(F32)).astype(BF16)
    return hi, mid, lo


def _sel_r(x, e):
    hi, mid, lo = _split3(x)
    return (_mm(hi, e) + _mm(mid, e)) + _mm(lo, e)


def _sel_l(e, x):
    hi, mid, lo = _split3(x)
    return (_mm(e, hi) + _mm(e, mid)) + _mm(e, lo)


def _sig(x):
    return 1.0 / (1.0 + jnp.exp(-x))


def _cp(sem):
    return pltpu.CompilerParams(dimension_semantics=sem, vmem_limit_bytes=VMEM_LIMIT)


def _row(shape):
    nd = len(shape)
    return pl.BlockSpec(shape, lambda *_: (0,) * nd)


def _adamw(w, g, m, v):
    m = B1 * m + (1.0 - B1) * g
    v = B2 * v + (1.0 - B2) * (g * g)
    m_hat = m / (1.0 - B1 ** STEP)
    v_hat = v / (1.0 - B2 ** STEP)
    delta = -LR * (m_hat / (jnp.sqrt(v_hat) + AEPS) + WD * w)
    return delta, m, v


class _Carry:
    def __init__(self, inps, outs, copies):
        self.inps, self.outs, self.copies = list(inps), list(outs), copies
        self.n = len(copies(0, 0, 0))

    def descriptors(self, in_refs, out_refs, send_sems, recv_sems):
        x, y, c = lax.axis_index("x"), lax.axis_index("y"), lax.axis_index("c")
        out = []
        for j, (flip, a, si, o, di) in enumerate(self.copies(x, y, c)):
            if flip is None:
                out.append(pltpu.make_async_copy(in_refs[a].at[si], out_refs[o].at[di], send_sems.at[j]))
            else:
                fx, fy, fc = flip
                peer = (1 - x if fx else x, 1 - y if fy else y, 1 - c if fc else c)
                out.append(pltpu.make_async_remote_copy(
                    src_ref=in_refs[a].at[si], dst_ref=out_refs[o].at[di],
                    send_sem=send_sems.at[j], recv_sem=recv_sems.at[j],
                    device_id=peer, device_id_type=pl.DeviceIdType.MESH))
        return out


def _pcall(body, args, *, name, grid, in_specs, out_specs, out_shape, scratch_shapes=(), sem=None, nprefetch=0,
           carry=None):
    out_shape, out_specs = list(out_shape), list(out_specs)
    in_specs, scratch_shapes = list(in_specs), list(scratch_shapes)
    nin, nout, nscr = len(in_specs), len(out_shape), len(scratch_shapes)
    run = body
    if carry is not None:
        ncin, ncout = len(carry.inps), len(carry.outs)
        hbm = pl.BlockSpec(memory_space=pl.ANY)

        def run(*refs):
            pre, r = refs[:nprefetch], refs[nprefetch:]
            ins, cins = r[:nin], r[nin:nin + ncin]
            r = r[nin + ncin:]
            outs, couts = r[:nout], r[nout:nout + ncout]
            r = r[nout + ncout:]
            scr, (send_sems, recv_sems) = r[:nscr], r[nscr:]
            first = pl.program_id(0) == 0
            last = pl.program_id(0) == grid[0] - 1
            for ax in range(1, len(grid)):
                first = jnp.logical_and(first, pl.program_id(ax) == 0)
                last = jnp.logical_and(last, pl.program_id(ax) == grid[ax] - 1)

            @pl.when(first)
            def _():
                for d in carry.descriptors(cins, couts, send_sems, recv_sems):
                    d.start()

            body(*pre, *ins, *outs, *scr)

            @pl.when(last)
            def _():
                for d in carry.descriptors(cins, couts, send_sems, recv_sems):
                    d.wait()

        in_specs = in_specs + [hbm] * ncin
        out_specs = out_specs + [hbm] * ncout
        out_shape = out_shape + carry.outs
        scratch_shapes = scratch_shapes + [pltpu.SemaphoreType.DMA((carry.n,)), pltpu.SemaphoreType.DMA((carry.n,))]
        args = list(args) + carry.inps
    if sem is None:
        sem = ("arbitrary",) * len(grid)
    if nprefetch:
        kw = dict(grid_spec=pltpu.PrefetchScalarGridSpec(num_scalar_prefetch=nprefetch, grid=grid, in_specs=in_specs,
                                                         out_specs=out_specs, scratch_shapes=scratch_shapes))
    else:
        kw = dict(grid=grid, in_specs=in_specs, out_specs=out_specs, scratch_shapes=scratch_shapes)
    res = pl.pallas_call(run, name=name, out_shape=out_shape, compiler_params=_cp(sem), **kw)(*args)
    return list(res)


def _merge(*carries):
    inps, outs, offs = [], [], []
    for cr in carries:
        offs.append((len(inps), len(outs)))
        inps += cr.inps
        outs += cr.outs

    def copies(x, y, c):
        return [(f, a + io, si, o + oo, di) for cr, (io, oo) in zip(carries, offs) for f, a, si, o, di in cr.copies(x, y, c)]

    return _Carry(inps, outs, copies)


def _exchange(name, carry):
    return _pcall(lambda: None, [], name=name, grid=(1,), in_specs=[], out_specs=[], out_shape=[], carry=carry)


def _exchange_two(name, carry, then):
    second = _Carry([], [], then)
    nin, nout = len(carry.inps), len(carry.outs)

    def body(*refs):
        ins, outs = refs[:nin], refs[nin:nin + nout]
        send_a, recv_a, send_b, recv_b = refs[nin + nout:]
        for descs in (carry.descriptors(ins, outs, send_a, recv_a), second.descriptors(outs, outs, send_b, recv_b)):
            for d in descs:
                d.start()
            for d in descs:
                d.wait()

    hbm = pl.BlockSpec(memory_space=pl.ANY)
    return list(pl.pallas_call(
        body, name=name, out_shape=carry.outs, in_specs=[hbm] * nin, out_specs=[hbm] * nout,
        scratch_shapes=[pltpu.SemaphoreType.DMA((carry.n,)), pltpu.SemaphoreType.DMA((carry.n,)),
                        pltpu.SemaphoreType.DMA((second.n,)), pltpu.SemaphoreType.DMA((second.n,))],
    )(*carry.inps))


_ALL7 = [(f >> 2 & 1, f >> 1 & 1, f & 1) for f in range(1, 8)]
_CHIPS3 = [(0, 1, 0), (1, 0, 0), (1, 1, 0)]
_SIBLING = (0, 0, 1)


def _gather8_carry(blk):
    def copies(x, y, c):
        me = 4 * x + 2 * y + c
        return [(None, 0, 0, 0, me)] + [(f, 0, 0, 0, me) for f in _ALL7]

    return _Carry([blk[None]], [jax.ShapeDtypeStruct((8,) + blk.shape, blk.dtype)], copies)


def _gather_chips_carry(blks):
    def copies(x, y, c):
        chip = 2 * x + y
        return [(f, a, 0, a, chip) for a in range(len(blks)) for f in [None] + _CHIPS3]

    return _Carry([b[None] for b in blks], [jax.ShapeDtypeStruct((4,) + b.shape, b.dtype) for b in blks], copies)


def _ada_fwd(c_all, w_loc, b_loc):
    n = w_loc.shape[1]
    tn = 512

    def body(c_ref, w_ref, b_ref, o_ref):
        cv = c_ref[...]
        cond = cv * _sig(cv)
        o_ref[...] = _mm_hi(cond, w_ref[...]) + b_ref[...]

    return pl.pallas_call(
        body, name="ada_fwd", grid=(n // tn,),
        out_shape=jax.ShapeDtypeStruct((8, n), F32),
        in_specs=[_row((8, D)), pl.BlockSpec((D, tn), lambda j: (0, j)), pl.BlockSpec((1, tn), lambda j: (0, j))],
        out_specs=pl.BlockSpec((8, tn), lambda j: (0, j)),
        compiler_params=_cp(("parallel",)),
    )(c_all, w_loc, b_loc)


def _ada_bwd_adamw(c_all_t, dmod_loc, w, m, v, carry=None):
    n = w.shape[1]
    tn = 512

    def body(ct_ref, dm_ref, w_ref, m_ref, v_ref, g_ref, d_ref, mo_ref, vo_ref):
        ct = ct_ref[...]
        cond = ct * _sig(ct)
        dm = dm_ref[...]
        g = cond[:, 0:1] * dm[0:1, :]
        for b in range(1, 8):
            g = g + cond[:, b:b + 1] * dm[b:b + 1, :]
        g_ref[...] = g
        d_ref[...], mo_ref[...], vo_ref[...] = _adamw(w_ref[...], g, m_ref[...], v_ref[...])

    wspec = pl.BlockSpec((D, tn), lambda j: (0, j))
    return _pcall(
        body, [c_all_t, dmod_loc, w, m, v], name="ada_bwd_adamw", grid=(n // tn,),
        out_shape=[jax.ShapeDtypeStruct((D, n), F32)] * 4,
        in_specs=[_row((D, 8)), pl.BlockSpec((8, tn), lambda j: (0, j)), wspec, wspec, wspec],
        out_specs=[wspec] * 4, carry=carry)


def _in_proj_fwd(x, a1, sh1, w_in, carry=None):
    s = x.shape[0]
    tm = 512

    def body(x_ref, a_ref, s_ref, w_ref, q_ref, kv_ref, z_ref, xbc_ref, dt_ref):
        def norm(rows):
            xv = x_ref[rows, :]
            r = lax.rsqrt(jnp.mean(xv * xv, axis=-1, keepdims=True) + EPS)
            return (xv * r * a_ref[...] + s_ref[...]).astype(BF16)

        def project(rows, h):
            p = _mm_nt(h, w_ref[...])
            q_ref[rows, :] = p[:, 0:512].astype(BF16)
            kv_ref[rows, :] = p[:, 512:768].astype(BF16)
            z_ref[rows, :] = p[:, 768:1280]
            xbc_ref[rows, :] = p[:, 1280:2304]
            dt_ref[rows, :] = p[:, 2304:2432]

        r0, r1 = slice(0, tm // 2), slice(tm // 2, tm)
        h0 = norm(r0)
        project(r0, h0)
        project(r1, norm(r1))

    def tok(w):
        return pl.BlockSpec((tm, w), lambda i: (i, 0))

    return _pcall(
        body, [x, a1, sh1, w_in], name="in_proj_fwd", grid=(s // tm,),
        out_shape=[jax.ShapeDtypeStruct((s, QW), BF16), jax.ShapeDtypeStruct((s, 2 * KVW), BF16),
                   jax.ShapeDtypeStruct((s, SW), F32), jax.ShapeDtypeStruct((s, XBCW), F32),
                   jax.ShapeDtypeStruct((s, 128), F32)],
        in_specs=[tok(D), _row((1, D)), _row((1, D)), _row((PROJ_W, D))],
        out_specs=[tok(QW), tok(2 * KVW), tok(SW), tok(XBCW), tok(128)], carry=carry)


def _in_proj_bwd(x, dx1, a1, sh1, w_in, dq, dkv, dz, dxbc, ddt, carry=None):
    s = x.shape[0]
    tm = 512

    def body(x_ref, dx1_ref, a_ref, s_ref, w_ref, dq_ref, dkv_ref, dz_ref, dxbc_ref, ddt_ref,
             gx_ref, h_ref, dsh_ref, p_ref):
        i = pl.program_id(0)

        @pl.when(i == 0)
        def _():
            dsh_ref[...] = jnp.zeros_like(dsh_ref)
            p_ref[...] = jnp.zeros_like(p_ref)

        def gather(st):
            rows = st["rows"]
            st["dproj"] = jnp.concatenate([dq_ref[rows, :], dkv_ref[rows, :], dz_ref[rows, :], dxbc_ref[rows, :],
                                           ddt_ref[rows, :]], axis=1)

        def back(st):
            st["dh"] = _mm(st.pop("dproj"), w_ref[...])

        def norm(st):
            rows, dh = st["rows"], st.pop("dh")
            xv = x_ref[rows, :]
            r = lax.rsqrt(jnp.mean(xv * xv, axis=-1, keepdims=True) + EPS)
            xn = xv * r
            a = a_ref[...]
            h_ref[rows, :] = (xn * a + s_ref[...]).astype(BF16)
            st["dsh"] = jnp.sum(dh, axis=0, keepdims=True)
            st["p"] = jnp.sum(dh * xn, axis=0, keepdims=True)
            u = dh * a
            gx_ref[rows, :] = dx1_ref[rows, :] + r * u - xn * (r * jnp.mean(u * xn, axis=-1, keepdims=True))

        g0, g1 = [dict(rows=slice(k * (tm // 2), (k + 1) * (tm // 2))) for k in range(2)]
        for stage, st in [(gather, g0), (back, g0), (gather, g1), (norm, g0), (back, g1), (norm, g1)]:
            stage(st)
        dsh_ref[0:1, :] += g0["dsh"] + g1["dsh"]
        p_ref[0:1, :] += g0["p"] + g1["p"]

    def tok(w):
        return pl.BlockSpec((tm, w), lambda i: (i, 0))

    return _pcall(
        body, [x, dx1, a1, sh1, w_in, dq, dkv, dz, dxbc, ddt], name="in_proj_bwd", grid=(s // tm,),
        out_shape=[jax.ShapeDtypeStruct((s, D), F32), jax.ShapeDtypeStruct((s, D), BF16),
                   jax.ShapeDtypeStruct((8, D), F32), jax.ShapeDtypeStruct((8, D), F32)],
        in_specs=[tok(D), tok(D), _row((1, D)), _row((1, D)), _row((PROJ_W, D)),
                  tok(QW), tok(2 * KVW), tok(SW), tok(XBCW), tok(128)],
        out_specs=[tok(D), tok(D), _row((8, D)), _row((8, D))], carry=carry)


def _attn_geometry():
    dist = np.arange(BLK)[:, None] + BLK - np.arange(2 * BLK)[None, :]
    n = np.maximum(dist, 0)
    max_exact = NBUCKET // 2
    large = max_exact + (np.log(np.maximum(n, 1) / max_exact) / np.log(128 / max_exact)
                         * (NBUCKET - max_exact)).astype(np.int32)
    large = np.minimum(large, NBUCKET - 1)
    bucket = np.where(n < max_exact, n, large).astype(np.int32)
    mask = (dist >= 0) & (dist < 128)
    return bucket, mask


def _attn_heads(is_first, q_blk, kvw, bias_ref, sinks_ref):
    qv = q_blk * 0.125
    col = lax.broadcasted_iota(jnp.int32, (BLK, 2 * BLK), 1)
    first = jnp.where(jnp.logical_and(is_first, col < BLK), NEG, 0.0)
    groups = []
    for g in range(NKV):
        qs = jnp.concatenate([qv[:, (4 * g + r) * HD:(4 * g + r + 1) * HD] for r in range(4)], axis=0)
        kw = kvw[:, g * HD:(g + 1) * HD]
        vw = kvw[:, KVW + g * HD:KVW + (g + 1) * HD]
        sc = _mm_nt(qs, kw)
        pn, ps = [], []
        for r in range(4):
            h = 4 * g + r
            sr = sc[r * BLK:(r + 1) * BLK] + bias_ref[h] + first
            sink = sinks_ref[h]
            m = jnp.maximum(jnp.max(sr, axis=-1, keepdims=True), sink)
            p = jnp.exp(sr - m)
            es = jnp.exp(sink - m)
            inv = 1.0 / (jnp.sum(p, axis=-1, keepdims=True) + es)
            pn.append(p * inv)
            ps.append(es * inv)
        pn = jnp.concatenate(pn, axis=0)
        ps = jnp.concatenate(ps, axis=0)
        o = _mm(pn.astype(BF16), vw)
        groups.append((qs, kw, vw, pn, ps, o))
    return groups


def _unstack_heads(parts):
    return jnp.concatenate([p[r * BLK:(r + 1) * BLK] for p in parts for r in range(4)], axis=1)


NB = 2


def _attn_fwd(q, kv, bias, sinks, nw, carry=None):
    s = q.shape[0]

    def body(q_ref, kvp_ref, kvc_ref, bias_ref, sinks_ref, nw_ref, y_ref):
        t = pl.program_id(0)
        kv3 = jnp.concatenate([kvp_ref[...], kvc_ref[...]], axis=0)
        for sub in range(NB):
            rows = slice(sub * BLK, (sub + 1) * BLK)
            groups = _attn_heads(jnp.logical_and(t == 0, sub == 0), q_ref[rows, :], kv3[sub * BLK:(sub + 2) * BLK],
                                 bias_ref, sinks_ref)
            o = _unstack_heads([g[5] for g in groups])
            r = lax.rsqrt(jnp.mean(o * o, axis=-1, keepdims=True) + EPS)
            y_ref[rows, :] = (o * r * nw_ref[...]).astype(BF16)

    return _pcall(
        body, [q, kv, kv, bias, sinks, nw], name="attn_fwd", grid=(s // (NB * BLK),),
        out_shape=[jax.ShapeDtypeStruct((s, QW), BF16)],
        in_specs=[pl.BlockSpec((NB * BLK, QW), lambda t: (t, 0)),
                  pl.BlockSpec((BLK, 2 * KVW), lambda t: (jnp.maximum(NB * t - 1, 0), 0)),
                  pl.BlockSpec((NB * BLK, 2 * KVW), lambda t: (t, 0)),
                  _row((NH, BLK, 2 * BLK)),
                  pl.BlockSpec(memory_space=pltpu.SMEM),
                  _row((1, QW))],
        out_specs=[pl.BlockSpec((NB * BLK, QW), lambda t: (t, 0))], carry=carry)


def _attn_bwd(q, kv, dya, bias, sinks, nw, h2, dgu, act, dx2, carry=None):
    s = q.shape[0]
    nt = s // (NB * BLK)
    ngu = dgu.shape[1]
    npiece = D // NB
    mpiece = DFF // NB

    def body(q_ref, kvp_ref, kvc_ref, dy_ref, bias_ref, sinks_ref, nw_ref, h2_ref, dgu_ref, act_ref, dx2_ref,
             dq_ref, dkv_ref, dbias_ref, dsink_ref, dnw_ref, ggu_hbm, gdn_hbm, carry_ref, held_ref, acc_ref, acc2_ref,
             sems):
        t = pl.program_id(0)

        @pl.when(t == 0)
        def _():
            carry_ref[...] = jnp.zeros_like(carry_ref)
            held_ref[...] = jnp.zeros_like(held_ref)
            dbias_ref[...] = jnp.zeros_like(dbias_ref)
            dsink_ref[...] = jnp.zeros_like(dsink_ref)
            dnw_ref[...] = jnp.zeros_like(dnw_ref)
            acc_ref[...] = jnp.zeros_like(acc_ref)
            acc2_ref[...] = jnp.zeros_like(acc2_ref)

        def wgrad_piece(sub):
            rows = slice(sub * npiece, (sub + 1) * npiece)
            acc_ref[rows, :] += _mm_tn(h2_ref[:, rows], dgu_ref[...])
            rows2 = slice(sub * mpiece, (sub + 1) * mpiece)
            acc2_ref[rows2, :] += _mm_tn(act_ref[:, rows2], dx2_ref[...].astype(BF16))

        def block(sub, kv3):
            rows = slice(sub * BLK, (sub + 1) * BLK)
            groups = _attn_heads(jnp.logical_and(t == 0, sub == 0), q_ref[rows, :], kv3[sub * BLK:(sub + 2) * BLK],
                                 bias_ref, sinks_ref)
            o = _unstack_heads([g[5] for g in groups])
            r = lax.rsqrt(jnp.mean(o * o, axis=-1, keepdims=True) + EPS)
            dy = dy_ref[rows, :]
            on = o * r
            dnw_ref[0:1, :] += jnp.sum(dy * on, axis=0, keepdims=True)
            u = dy * nw_ref[...]
            do = r * u - on * (r * jnp.mean(u * on, axis=-1, keepdims=True))
            dq_parts, dk_parts, dv_parts = [], [], []
            for g, (qs, kw, vw, pn, ps, og) in enumerate(groups):
                dos = jnp.concatenate([do[:, (4 * g + r_) * HD:(4 * g + r_ + 1) * HD] for r_ in range(4)], axis=0)
                delta = jnp.sum(dos * og, axis=-1, keepdims=True)
                dp = _mm_nt(dos.astype(BF16), vw)
                ds = pn * (dp - delta)
                dsk = ps * delta
                lane = lax.broadcasted_iota(jnp.int32, (1, 128), 1)
                for r_ in range(4):
                    h = 4 * g + r_
                    dbias_ref[h] += ds[r_ * BLK:(r_ + 1) * BLK]
                    dsink_ref[0:1, :] -= jnp.where(lane == h, jnp.sum(dsk[r_ * BLK:(r_ + 1) * BLK]), 0.0)
                dsb = ds.astype(BF16)
                dq_parts.append(_mm(dsb, kw) * 0.125)
                dk_parts.append(_mm_tn(dsb, qs))
                dv_parts.append(_mm_tn(pn.astype(BF16), dos.astype(BF16)))
            dq_ref[rows, :] = _unstack_heads(dq_parts).astype(BF16)
            return jnp.concatenate(dk_parts + dv_parts, axis=1)

        @pl.when(t < nt)
        def _():
            kv3 = jnp.concatenate([kvp_ref[...], kvc_ref[...]], axis=0)
            tail = carry_ref[...]
            for sub in range(NB):
                d = block(sub, kv3)
                done = tail + d[0:BLK]
                if sub == 0:
                    dkv_ref[0:(NB - 1) * BLK, :] = held_ref[...].astype(BF16)
                    dkv_ref[(NB - 1) * BLK:NB * BLK, :] = done.astype(BF16)
                else:
                    held_ref[(sub - 1) * BLK:sub * BLK, :] = done
                tail = d[BLK:2 * BLK]
                wgrad_piece(sub)
            carry_ref[...] = tail

        @pl.when(t == nt)
        def _():
            dkv_ref[0:(NB - 1) * BLK, :] = held_ref[...].astype(BF16)
            dkv_ref[(NB - 1) * BLK:NB * BLK, :] = carry_ref[...].astype(BF16)
            nj = ngu // 4
            cps = [pltpu.make_async_copy(acc_ref.at[:, pl.ds(j * nj, nj)], ggu_hbm.at[j], sems.at[j]) for j in range(4)]
            cps.append(pltpu.make_async_copy(acc2_ref, gdn_hbm, sems.at[4]))
            for cp in cps:
                cp.start()
            for cp in cps:
                cp.wait()

    last = nt - 1
    tile = lambda w: pl.BlockSpec((NB * BLK, w), lambda t: (jnp.minimum(t, last), 0))
    return _pcall(
        body, [q, kv, kv, dya, bias, sinks, nw, h2, dgu, act, dx2], name="attn_bwd", grid=(nt + 1,),
        out_shape=[jax.ShapeDtypeStruct((s, QW), BF16), jax.ShapeDtypeStruct((s, 2 * KVW), BF16),
                   jax.ShapeDtypeStruct((NH, BLK, 2 * BLK), F32), jax.ShapeDtypeStruct((NH, 128), F32),
                   jax.ShapeDtypeStruct((8, QW), F32), jax.ShapeDtypeStruct((4, D, ngu // 4), F32),
                   jax.ShapeDtypeStruct((DFF, D), F32)],
        in_specs=[tile(QW),
                  pl.BlockSpec((BLK, 2 * KVW), lambda t: (jnp.clip(NB * t - 1, 0, NB * nt - 1), 0)),
                  tile(2 * KVW), tile(QW),
                  _row((NH, BLK, 2 * BLK)),
                  pl.BlockSpec(memory_space=pltpu.SMEM),
                  _row((1, QW)), tile(D), tile(ngu), tile(DFF), tile(D)],
        out_specs=[tile(QW),
                   pl.BlockSpec((NB * BLK, 2 * KVW), lambda t: (jnp.maximum(t - 1, 0), 0)),
                   _row((NH, BLK, 2 * BLK)), _row((NH, 128)), _row((8, QW)),
                   pl.BlockSpec(memory_space=pl.ANY), pl.BlockSpec(memory_space=pl.ANY)],
        scratch_shapes=[pltpu.VMEM((BLK, 2 * KVW), F32), pltpu.VMEM(((NB - 1) * BLK, 2 * KVW), F32),
                        pltpu.VMEM((D, ngu), F32), pltpu.VMEM((DFF, D), F32), pltpu.SemaphoreType.DMA((5,))],
        carry=carry)


def _gate_scale(name, g, gate, w):
    m, n = g.shape
    tr = m // 4

    def body(g_ref, gate_ref, w_ref, o_ref, dg_ref):
        @pl.when(pl.program_id(0) == 0)
        def _():
            dg_ref[...] = jnp.zeros_like(dg_ref)

        acc = g_ref[...]
        dg_ref[0:1, :] += jnp.sum(acc * w_ref[...].astype(F32), axis=0, keepdims=True)
        o_ref[...] = acc * gate_ref[...]

    blk = pl.BlockSpec((tr, n), lambda i: (i, 0))
    return _pcall(body, [g, gate, w], name=name, grid=(4,),
                  out_shape=[jax.ShapeDtypeStruct((m, n), F32), jax.ShapeDtypeStruct((8, n), F32)],
                  in_specs=[blk, _row((1, n)), blk], out_specs=[blk, _row((8, n))])


def _rel_bias_grad(dbias, bucket):
    def body(db_ref, bk_ref, o_ref):
        bk = bk_ref[...]
        lane = lax.broadcasted_iota(jnp.int32, (1, 128), 1)
        for b in range(NBUCKET):
            sel = bk == b
            row = jnp.zeros((1, 128), F32)
            for h in range(NH):
                row = row + jnp.where(lane == h, jnp.sum(jnp.where(sel, db_ref[h], 0.0)), 0.0)
            o_ref[b:b + 1, :] = row

    return pl.pallas_call(
        body, name="rel_bias_grad",
        out_shape=jax.ShapeDtypeStruct((NBUCKET, 128), F32),
    )(dbias, bucket)


def _ssd_consts():
    head_of_lane = np.arange(SW) // HD
    expand = (np.arange(128)[:, None] == head_of_lane[None, :]).astype(np.float32)
    tril = np.tril(np.ones((BLK, BLK), np.float32))
    return (jnp.asarray(expand, BF16), jnp.asarray(expand.T.copy(), BF16), jnp.asarray(tril, BF16),
            jnp.asarray(tril.T.copy(), BF16))


def _conv_pre(xc, halo, cw, cb):
    ext = jnp.concatenate([halo, xc], axis=0)
    taps = [xc if k == CK - 1 else pltpu.roll(ext, CK - 1 - k, 0)[8:8 + BLK] for k in range(CK)]
    return cb + sum(cw[k:k + 1, :] * taps[k] for k in range(CK))


def _ssd_chunk(pre, dtr, dtb, av, dkv, ex, tril, h_in):
    sp = _sig(pre)
    xbc = pre * sp
    xs, bm, cm = xbc[:, 0:SW], xbc[:, SW:SW + 2 * NST], xbc[:, SW + 2 * NST:]
    dtin = dtr + dtb
    dt = jnp.maximum(dtin, 0.0) + jnp.log1p(jnp.exp(-jnp.abs(dtin)))
    cs = _sel_l(tril, dt * av)
    cst = cs.T
    dtx = _sel_r(dt, ex)
    csx = _sel_r(cs, ex)
    xdt = xs * dtx
    csl = csx[BLK - 1:BLK, :]
    decx = jnp.exp(csl - csx)
    ecsx = jnp.exp(csx)
    ecl = jnp.exp(csl)
    causal = tril.astype(F32) > 0.5
    ydiag, yoff, cbs, lms = [], [], [], []
    for g in range(2):
        bg = bm[:, g * NST:(g + 1) * NST].astype(BF16)
        cg = cm[:, g * NST:(g + 1) * NST].astype(BF16)
        cb = _mm_nt(cg, bg)
        cbs.append(cb)
        yoff.append(_mm(cg, h_in[:, g * 256:(g + 1) * 256].astype(BF16)))
        for r in range(4):
            h = 4 * g + r
            seg = cs[:, h:h + 1] - cst[h:h + 1, :]
            lm = jnp.where(causal, jnp.exp(jnp.minimum(seg, 0.0)), 0.0)
            lms.append(lm)
            ydiag.append(_mm((cb * lm).astype(BF16), xdt[:, h * HD:(h + 1) * HD].astype(BF16)))
    yoff = jnp.concatenate(yoff, axis=1) * ecsx
    y = jnp.concatenate(ydiag, axis=1) + yoff + dkv * xs
    return dict(pre=pre, sp=sp, xs=xs, bm=bm, cm=cm, dtin=dtin, dt=dt, av=av, cs=cs, cst=cst,
                dtx=dtx, csx=csx, xdt=xdt, decx=decx, ecsx=ecsx, ecl=ecl, causal=causal, cbs=cbs, lms=lms,
                yoff=yoff, y=y)


def _group_mean(t):
    m0 = jnp.mean(t[:, 0:256], axis=-1, keepdims=True)
    m1 = jnp.mean(t[:, 256:512], axis=-1, keepdims=True)
    return jnp.concatenate([jnp.broadcast_to(m0, (t.shape[0], 256)), jnp.broadcast_to(m1, (t.shape[0], 256))], axis=1)


SUBS = 4


def _ssd_fwd(z, xbc, dtr, cw, cb, dtb, av, dk, nw, carry=None):
    s = z.shape[0]
    nc = s // BLK
    tile = SUBS * BLK
    ex, _, tril, _ = _ssd_consts()

    def body(z_ref, xc_ref, xh_ref, dtr_ref, cw_ref, cb_ref, dtb_ref, a_ref, dk_ref, nw_ref, ex_ref, tril_ref,
             y_ref, hs_ref, pre_ref, h_ref):
        t = pl.program_id(0)

        @pl.when(t == 0)
        def _():
            h_ref[...] = jnp.zeros_like(h_ref)

        h_in = h_ref[...]
        for sub in range(SUBS):
            rows = slice(sub * BLK, (sub + 1) * BLK)
            xc = xc_ref[rows, :]
            halo = jnp.where(t == 0, 0.0, xh_ref[...]) if sub == 0 else xc_ref[sub * BLK - 8:sub * BLK, :]
            pre = _conv_pre(xc, halo, cw_ref[...], cb_ref[...])
            pre_ref[rows, :] = pre
            hs_ref[sub] = h_in
            f = _ssd_chunk(pre, dtr_ref[rows, :], dtb_ref[...], a_ref[...], dk_ref[...], ex_ref[...], tril_ref[...], h_in)
            dx = (f["decx"] * f["xdt"]).astype(BF16)
            st = [_mm_tn(f["bm"][:, g * NST:(g + 1) * NST].astype(BF16), dx[:, g * 256:(g + 1) * 256]) for g in range(2)]
            h_in = h_in * f["ecl"] + jnp.concatenate(st, axis=1)
            zv = z_ref[rows, :]
            tg = f["y"] * (zv * _sig(zv))
            r = lax.rsqrt(_group_mean(tg * tg) + EPS)
            y_ref[rows, :] = (tg * r * nw_ref[...]).astype(BF16)
        h_ref[...] = h_in

    cur = lambda w: pl.BlockSpec((tile, w), lambda t: (t, 0))
    return _pcall(
        body, [z, xbc, xbc, dtr, cw, cb, dtb, av, dk, nw, ex, tril], name="ssd_fwd", grid=(s // tile,),
        out_shape=[jax.ShapeDtypeStruct((s, SW), BF16), jax.ShapeDtypeStruct((nc, NST, SW), F32),
                   jax.ShapeDtypeStruct((s, XBCW), F32)],
        in_specs=[cur(SW), cur(XBCW), pl.BlockSpec((8, XBCW), lambda t: (jnp.maximum(t * (tile // 8) - 1, 0), 0)),
                  cur(128), _row((8, XBCW)), _row((1, XBCW)), _row((1, 128)),
                  _row((1, 128)), _row((1, SW)), _row((1, SW)), _row((128, SW)), _row((BLK, BLK))],
        out_specs=[cur(SW), pl.BlockSpec((SUBS, NST, SW), lambda t: (t, 0, 0)), cur(XBCW)],
        scratch_shapes=[pltpu.VMEM((NST, SW), F32)], carry=carry)


def _ssd_bwd(z, xbc, pre_all, dtr, dys, hs, cw, dtb, av, dk, nw, carry=None):
    s = z.shape[0]
    tile = SUBS * BLK
    nt = s // tile
    ex, ext_t, tril, triu = _ssd_consts()

    def body(z_ref, xc_ref, pre_ref, dtr_ref, dy_ref, hs_ref, cw_ref, dtb_ref, a_ref, dk_ref, nw_ref,
             ex_ref, ext_ref, tril_ref, triu_ref,
             dz_ref, dxbc_ref, ddt_ref, dcw_ref, dcb_ref, dnw_ref, dhd_ref, dh_ref, nxt_ref, dd_ref):
        i = pl.program_id(0)

        @pl.when(i == 0)
        def _():
            dh_ref[...] = jnp.zeros_like(dh_ref)
            nxt_ref[...] = jnp.zeros_like(nxt_ref)
            dd_ref[...] = jnp.zeros_like(dd_ref)
            dcw_ref[...] = jnp.zeros_like(dcw_ref)
            dcb_ref[...] = jnp.zeros_like(dcb_ref)
            dnw_ref[...] = jnp.zeros_like(dnw_ref)
            dhd_ref[...] = jnp.zeros_like(dhd_ref)

        gst, nxt = dh_ref[...], nxt_ref[...]
        for sub in reversed(range(SUBS)):
            rows = slice(sub * BLK, (sub + 1) * BLK)
            gst, nxt = chunk(sub, rows, gst, nxt, z_ref, xc_ref, pre_ref, dtr_ref, dy_ref, hs_ref, cw_ref, dtb_ref,
                             a_ref, dk_ref, nw_ref, ex_ref, ext_ref, tril_ref, triu_ref,
                             dz_ref, dxbc_ref, ddt_ref, dcw_ref, dcb_ref, dnw_ref, dhd_ref, dd_ref)
        dh_ref[...] = gst
        nxt_ref[...] = nxt

        @pl.when(i == nt - 1)
        def _():
            dhd_ref[2:3, :] = _sel_r(dd_ref[...], ext_ref[...])[0:1, :]

    def chunk(sub, rows, gst, nxt, z_ref, xc_ref, pre_ref, dtr_ref, dy_ref, hs_ref, cw_ref, dtb_ref,
              a_ref, dk_ref, nw_ref, ex_ref, ext_ref, tril_ref, triu_ref,
              dz_ref, dxbc_ref, ddt_ref, dcw_ref, dcb_ref, dnw_ref, dhd_ref, dd_ref):
        h_in = hs_ref[sub]
        f = _ssd_chunk(pre_ref[rows, :], dtr_ref[rows, :], dtb_ref[...], a_ref[...], dk_ref[...], ex_ref[...],
                       tril_ref[...], h_in)
        xs, xdt, decx, ecsx, ecl, dtx = f["xs"], f["xdt"], f["decx"], f["ecsx"], f["ecl"], f["dtx"]
        cs, cst, causal = f["cs"], f["cst"], f["causal"]
        causal_t = triu_ref[...].astype(F32) > 0.5

        zv = z_ref[rows, :]
        sz = _sig(zv)
        gz = zv * sz
        t = f["y"] * gz
        r = lax.rsqrt(_group_mean(t * t) + EPS)
        tn_ = t * r
        dyn = dy_ref[rows, :]
        dnw_ref[0:1, :] += jnp.sum(dyn * tn_, axis=0, keepdims=True)
        u = dyn * nw_ref[...]
        dt_ = r * u - tn_ * (r * _group_mean(u * tn_))
        dy = dt_ * gz
        dz_ref[rows, :] = (dt_ * f["y"] * (sz * (1.0 + zv * (1.0 - sz)))).astype(BF16)

        dd_ref[0:1, :] += jnp.sum(dy * xs, axis=0, keepdims=True)
        dxs = dk_ref[...] * dy

        edy = ecsx * dy
        dxdt, dbs, dcs_, dcsx_parts, dh_new = [], [], [], [], []
        lane = lax.broadcasted_iota(jnp.int32, (1, 128), 1)
        dcs_intra = jnp.zeros((BLK, 128), F32)
        for g in range(2):
            sl = slice(g * 256, (g + 1) * 256)
            bgf, cgf = f["bm"][:, g * NST:(g + 1) * NST], f["cm"][:, g * NST:(g + 1) * NST]
            bg, cg = bgf.astype(BF16), cgf.astype(BF16)
            gg = gst[:, sl].astype(BF16)
            hg = h_in[:, sl].astype(BF16)
            edyg = edy[:, sl].astype(BF16)
            dc = _mm_nt(edyg, hg)
            dh_new.append(gst[:, sl] * ecl[:, sl] + _mm_tn(cg, edyg))
            bgm = _mm(bg, gg)
            dxdt_g = decx[:, sl] * bgm
            dxg = (decx[:, sl] * xdt[:, sl]).astype(BF16)
            db = _mm_nt(dxg, gg)
            qd = bgm * xdt[:, sl] * decx[:, sl]
            last = jnp.sum(qd, axis=0, keepdims=True) + ecl[:, sl] * jnp.sum(gst[:, sl] * h_in[:, sl], axis=0, keepdims=True)
            rowid = lax.broadcasted_iota(jnp.int32, (BLK, 256), 0)
            dcsx_parts.append(f["yoff"][:, sl] * dy[:, sl] - qd + jnp.where(rowid == BLK - 1, last, 0.0))
            cb_ = f["cbs"][g]
            cbt = _mm_nt(bg, cg)
            dcb_ = jnp.zeros((BLK, BLK), F32)
            dcbt = jnp.zeros((BLK, BLK), F32)
            dxd = []
            for r_ in range(4):
                h = 4 * g + r_
                hl = slice(h * HD, (h + 1) * HD)
                lm = f["lms"][h]
                segt = cst[h:h + 1, :] - cs[:, h:h + 1]
                lmt = jnp.where(causal_t, jnp.exp(jnp.minimum(segt, 0.0)), 0.0)
                dyh = dy[:, hl].astype(BF16)
                xdh = xdt[:, hl].astype(BF16)
                dw = _mm_nt(dyh, xdh)
                dwt = _mm_nt(xdh, dyh)
                wt = cbt * lmt
                dxd.append(_mm(wt.astype(BF16), dyh))
                dcb_ = dcb_ + dw * lm
                dcbt = dcbt + dwt * lmt
                col = jnp.sum(dw * (cb_ * lm), axis=-1, keepdims=True) - jnp.sum(dwt * wt, axis=-1, keepdims=True)
                dcs_intra = dcs_intra + jnp.where(lane == h, col, 0.0)
            dxdt.append(dxdt_g + jnp.concatenate(dxd, axis=1))
            dcs_.append(dc + _mm(dcb_.astype(BF16), bg))
            dbs.append(db + _mm(dcbt.astype(BF16), cg))
        dxdt = jnp.concatenate(dxdt, axis=1)
        dxs = dxs + dxdt * dtx
        ext_t_ = ext_ref[...]
        dcs = dcs_intra + _sel_r(jnp.concatenate(dcsx_parts, axis=1), ext_t_)
        da = _sel_l(triu_ref[...], dcs)
        ddt = da * f["av"] + _sel_r(dxdt * xs, ext_t_)
        dhd_ref[1:2, :] += jnp.sum(da * f["dt"], axis=0, keepdims=True)
        ddtr = ddt * _sig(f["dtin"])
        dhd_ref[0:1, :] += jnp.sum(ddtr, axis=0, keepdims=True)
        ddt_ref[rows, :] = ddtr.astype(BF16)

        sp, pre = f["sp"], f["pre"]
        dact = jnp.concatenate([dxs] + dbs + dcs_, axis=1)
        dpre = dact * (sp * (1.0 + pre * (1.0 - sp)))
        dcb_ref[0:1, :] += jnp.sum(dpre, axis=0, keepdims=True)
        ext2 = jnp.concatenate([dpre, nxt], axis=0)
        shifted = [pltpu.roll(ext2, BLK + 8 - (CK - 1 - k), 0)[0:BLK] for k in range(CK - 1)] + [dpre]
        cw = cw_ref[...]
        xc = xc_ref[rows, :]
        dxr = cw[CK - 1:CK, :] * dpre
        for k in range(CK):
            dcw_ref[k:k + 1, :] += jnp.sum(shifted[k] * xc, axis=0, keepdims=True)
            if k < CK - 1:
                dxr = dxr + cw[k:k + 1, :] * shifted[k]
        dxbc_ref[rows, :] = dxr.astype(BF16)
        return jnp.concatenate(dh_new, axis=1), dpre[0:8]

    cur = lambda w: pl.BlockSpec((tile, w), lambda i: (nt - 1 - i, 0))
    return _pcall(
        body, [z, xbc, pre_all, dtr, dys, hs, cw, dtb, av, dk, nw, ex, ext_t, tril, triu], name="ssd_bwd", grid=(nt,),
        out_shape=[jax.ShapeDtypeStruct((s, SW), BF16), jax.ShapeDtypeStruct((s, XBCW), BF16),
                   jax.ShapeDtypeStruct((s, 128), BF16), jax.ShapeDtypeStruct((8, XBCW), F32),
                   jax.ShapeDtypeStruct((8, XBCW), F32), jax.ShapeDtypeStruct((8, SW), F32),
                   jax.ShapeDtypeStruct((8, 128), F32)],
        in_specs=[cur(SW), cur(XBCW), cur(XBCW), cur(128), cur(SW),
                  pl.BlockSpec((SUBS, NST, SW), lambda i: (nt - 1 - i, 0, 0)),
                  _row((8, XBCW)), _row((1, 128)), _row((1, 128)), _row((1, SW)), _row((1, SW)),
                  _row((128, SW)), _row((SW, 128)), _row((BLK, BLK)), _row((BLK, BLK))],
        out_specs=[cur(SW), cur(XBCW), cur(128), _row((8, XBCW)), _row((8, XBCW)), _row((8, SW)), _row((8, 128))],
        scratch_shapes=[pltpu.VMEM((NST, SW), F32), pltpu.VMEM((8, XBCW), F32), pltpu.VMEM((8, SW), F32)], carry=carry)


def _load_once(i, pairs, sem):
    @pl.when(i == 0)
    def _():
        cps = [pltpu.make_async_copy(src, dst, sem.at[k]) for k, (src, dst) in enumerate(pairs)]
        for cp in cps:
            cp.start()
        for cp in cps:
            cp.wait()


def _mlp_fwd(x, ya, ys, tgt, w_o, w_ga, w_gb, w_dn, gate1, a2, sh2, gate2, fn):
    s = x.shape[0]
    sub_m, subs = 256, 2
    tm = sub_m * subs

    def body(x_ref, ya_ref, ys_ref, t_ref, wo_hbm, wga_hbm, wgb_hbm, wdn_hbm, g1_ref, a2_ref, s2_ref, g2_ref, fn_ref,
             x1_ref, gu_ref, dx2_ref, loss_ref, dfn_ref, wo, wga, wgb, wdn, sem):
        i = pl.program_id(0)
        _load_once(i, [(wo_hbm, wo), (wga_hbm, wga), (wgb_hbm, wgb), (wdn_hbm, wdn)], sem)

        @pl.when(i == 0)
        def _():
            loss_ref[...] = jnp.zeros_like(loss_ref)
            dfn_ref[...] = jnp.zeros_like(dfn_ref)

        def proj(st):
            st["mix"] = _mm(ya_ref[st["rows"], :], wo[0:QW, :]) + _mm(ys_ref[st["rows"], :], wo[QW:D, :])

        def norm(st):
            x1 = x_ref[st["rows"], :] + g1_ref[...] * st.pop("mix")
            x1_ref[st["rows"], :] = x1
            r2 = lax.rsqrt(jnp.mean(x1 * x1, axis=-1, keepdims=True) + EPS)
            st["x1"] = x1
            st["h2"] = (x1 * r2 * a2_ref[...] + s2_ref[...]).astype(BF16)

        def gate_up(st):
            h2 = st.pop("h2")
            ha, hb = h2[:, 0:D // 2], h2[:, D // 2:D]
            gub = jnp.concatenate([(_mm(ha, wga[j]) + _mm(hb, wgb[j])).astype(BF16) for j in range(4)], axis=1)
            gu_ref[st["rows"], :] = gub
            st["gub"] = gub

        def activate(st):
            gub = st.pop("gub")
            gv, uv = gub[:, 0:DFF].astype(F32), gub[:, DFF:].astype(F32)
            st["act"] = (gv * _sig(gv) * uv).astype(BF16)

        def down(st):
            st["ff"] = _mm(st.pop("act"), wdn[...])

        def head(st):
            x2 = st.pop("x1") + g2_ref[...] * st.pop("ff")
            r3 = lax.rsqrt(jnp.mean(x2 * x2, axis=-1, keepdims=True) + EPS)
            xn = x2 * r3
            fnv = fn_ref[...]
            err = xn * fnv - t_ref[st["rows"], :]
            st["loss"] = jnp.sum(err * err) * (0.5 / D)
            dy = err * (1.0 / D)
            st["dfn"] = jnp.sum(dy * xn, axis=0, keepdims=True)
            u = dy * fnv
            dx2_ref[st["rows"], :] = r3 * u - xn * (r3 * jnp.mean(u * xn, axis=-1, keepdims=True))

        a, b = [dict(rows=slice(k * sub_m, (k + 1) * sub_m)) for k in range(subs)]
        for stage, st in [(proj, a), (norm, a), (proj, b), (gate_up, a), (norm, b), (activate, a), (gate_up, b),
                          (down, a), (activate, b), (head, a), (down, b), (head, b)]:
            stage(st)
        loss_ref[...] += a["loss"] + b["loss"]
        dfn_ref[0:1, :] += a["dfn"] + b["dfn"]

    def tok(w):
        return pl.BlockSpec((tm, w), lambda i: (i, 0))

    hbm = pl.BlockSpec(memory_space=pl.ANY)
    return pl.pallas_call(
        body, name="mlp_fwd", grid=(s // tm,),
        out_shape=[jax.ShapeDtypeStruct((s, D), F32), jax.ShapeDtypeStruct((s, 2 * DFF), BF16),
                   jax.ShapeDtypeStruct((s, D), F32), jax.ShapeDtypeStruct((8, 128), F32),
                   jax.ShapeDtypeStruct((8, D), F32)],
        in_specs=[tok(D), tok(QW), tok(SW), tok(D), hbm, hbm, hbm, hbm,
                  _row((1, D)), _row((1, D)), _row((1, D)), _row((1, D)), _row((1, D))],
        out_specs=[tok(D), tok(2 * DFF), tok(D), _row((8, 128)), _row((8, D))],
        scratch_shapes=[pltpu.VMEM((D, D), BF16), pltpu.VMEM(w_ga.shape, BF16), pltpu.VMEM(w_gb.shape, BF16),
                        pltpu.VMEM((DFF, D), BF16), pltpu.SemaphoreType.DMA((4,))],
        compiler_params=_cp(("arbitrary",)),
    )(x, ya, ys, tgt, w_o, w_ga, w_gb, w_dn, gate1, a2, sh2, gate2, fn)


def _mlp_bwd(x1, gu, dx2, w_o, w_ga, w_gb, w_dn, gate1, a2, sh2, gate2):
    s = x1.shape[0]
    tm = 256
    nj = 2 * DFF // 4

    def body(x1_ref, gu_ref, dx2_ref, wo_hbm, wga_hbm, wgb_hbm, wdn_hbm, g1_ref, a2_ref, s2_ref, g2_ref,
             dx1_ref, dya_ref, dys_ref, act_ref, dgu_ref, h2_ref, dsh_ref, p_ref, wo, wga, wgb, wdn, sem):
        i = pl.program_id(0)
        _load_once(i, [(wo_hbm, wo), (wga_hbm, wga), (wgb_hbm, wgb), (wdn_hbm, wdn)], sem)

        @pl.when(i == 0)
        def _():
            dsh_ref[...] = jnp.zeros_like(dsh_ref)
            p_ref[...] = jnp.zeros_like(p_ref)

        dx2 = dx2_ref[...]
        dact = _mm_nt((dx2 * g2_ref[...]).astype(BF16), wdn[...])
        gub = gu_ref[...]
        gv, uv = gub[:, 0:DFF].astype(F32), gub[:, DFF:].astype(F32)
        sg = _sig(gv)
        sl = gv * sg
        act_ref[...] = (sl * uv).astype(BF16)
        dgu = jnp.concatenate([dact * uv * (sg * (1.0 + gv * (1.0 - sg))), dact * sl], axis=1).astype(BF16)
        dgu_ref[...] = dgu
        dha = sum(_mm_nt(dgu[:, j * nj:(j + 1) * nj], wga[j]) for j in range(4))
        dhb = sum(_mm_nt(dgu[:, j * nj:(j + 1) * nj], wgb[j]) for j in range(4))
        dh = jnp.concatenate([dha, dhb], axis=1)
        x1 = x1_ref[...]
        r2 = lax.rsqrt(jnp.mean(x1 * x1, axis=-1, keepdims=True) + EPS)
        xn = x1 * r2
        a2 = a2_ref[...]
        h2_ref[...] = (xn * a2 + s2_ref[...]).astype(BF16)
        dsh_ref[0:1, :] += jnp.sum(dh, axis=0, keepdims=True)
        p_ref[0:1, :] += jnp.sum(dh * xn, axis=0, keepdims=True)
        u = dh * a2
        dx1 = dx2 + r2 * u - xn * (r2 * jnp.mean(u * xn, axis=-1, keepdims=True))
        dx1_ref[...] = dx1
        dcat = _mm_nt((dx1 * g1_ref[...]).astype(BF16), wo[...])
        dya_ref[...] = dcat[:, 0:QW]
        dys_ref[...] = dcat[:, QW:D]

    def tok(w):
        return pl.BlockSpec((tm, w), lambda i: (i, 0))

    hbm = pl.BlockSpec(memory_space=pl.ANY)
    return pl.pallas_call(
        body, name="mlp_bwd", grid=(s // tm,),
        out_shape=[jax.ShapeDtypeStruct((s, D), F32), jax.ShapeDtypeStruct((s, QW), F32),
                   jax.ShapeDtypeStruct((s, SW), F32), jax.ShapeDtypeStruct((s, DFF), BF16),
                   jax.ShapeDtypeStruct((s, 2 * DFF), BF16), jax.ShapeDtypeStruct((s, D), BF16),
                   jax.ShapeDtypeStruct((8, D), F32), jax.ShapeDtypeStruct((8, D), F32)],
        in_specs=[tok(D), tok(2 * DFF), tok(D), hbm, hbm, hbm, hbm, _row((1, D)), _row((1, D)), _row((1, D)), _row((1, D))],
        out_specs=[tok(D), tok(QW), tok(SW), tok(DFF), tok(2 * DFF), tok(D), _row((8, D)), _row((8, D))],
        scratch_shapes=[pltpu.VMEM((D, D), BF16), pltpu.VMEM(w_ga.shape, BF16), pltpu.VMEM(w_gb.shape, BF16),
                        pltpu.VMEM((DFF, D), BF16), pltpu.SemaphoreType.DMA((4,))],
        compiler_params=_cp(("arbitrary",)),
    )(x1, gu, dx2, w_o, w_ga, w_gb, w_dn, gate1, a2, sh2, gate2)


def _wgrad(name, a, b, gate, w, carry=None):
    s, m = a.shape
    n = b.shape[1]
    tk = min(1024, s)
    nk = s // tk

    def body(a_ref, b_ref, g_ref, w_ref, o_hbm, dg_ref, acc_ref, sem):
        k = pl.program_id(0)

        @pl.when(k == 0)
        def _():
            acc_ref[...] = jnp.zeros_like(acc_ref)

        acc_ref[...] += _mm_tn(a_ref[...], b_ref[...].astype(BF16))

        @pl.when(k == nk - 1)
        def _():
            acc = acc_ref[...]
            dg_ref[...] = jnp.zeros_like(dg_ref)
            dg_ref[0:1, :] = jnp.sum(acc * w_ref[...].astype(F32), axis=0, keepdims=True)
            acc_ref[...] = acc * g_ref[...]
            cp = pltpu.make_async_copy(acc_ref, o_hbm, sem)
            cp.start()
            cp.wait()

    return _pcall(body, [a, b, gate, w], name=name, grid=(nk,),
                  out_shape=[jax.ShapeDtypeStruct((m, n), F32), jax.ShapeDtypeStruct((8, n), F32)],
                  in_specs=[pl.BlockSpec((tk, m), lambda k: (k, 0)), pl.BlockSpec((tk, n), lambda k: (k, 0)),
                            _row((1, n)), _row((m, n))],
                  out_specs=[pl.BlockSpec(memory_space=pl.ANY), _row((8, n))],
                  scratch_shapes=[pltpu.VMEM((m, n), F32), pltpu.SemaphoreType.DMA], carry=carry)


def _wgrad_gate_up(h2, dgu, carry=None):
    s = h2.shape[0]
    tk = min(1024, s)
    nk = s // tk
    n = dgu.shape[1]
    nj = n // 4

    def body(a_ref, b_ref, o_hbm, acc_ref, sems):
        k = pl.program_id(0)

        @pl.when(k == 0)
        def _():
            acc_ref[...] = jnp.zeros_like(acc_ref)

        acc_ref[...] += _mm_tn(a_ref[...], b_ref[...])

        @pl.when(k == nk - 1)
        def _():
            cps = [pltpu.make_async_copy(acc_ref.at[:, pl.ds(j * nj, nj)], o_hbm.at[j], sems.at[j]) for j in range(4)]
            for cp in cps:
                cp.start()
            for cp in cps:
                cp.wait()

    return _pcall(body, [h2, dgu], name="wgrad_gate_up", grid=(nk,),
                  out_shape=[jax.ShapeDtypeStruct((4, D, nj), F32)],
                  in_specs=[pl.BlockSpec((tk, D), lambda k: (k, 0)), pl.BlockSpec((tk, n), lambda k: (k, 0))],
                  out_specs=[pl.BlockSpec(memory_space=pl.ANY)],
                  scratch_shapes=[pltpu.VMEM((D, n), F32), pltpu.SemaphoreType.DMA((4,))], carry=carry)


def _wgrad_in_t(h1, pieces, carry=None):
    s = h1.shape[0]
    tk = min(1024, s)
    nk = s // tk

    def body(a_ref, dq_ref, dkv_ref, dz_ref, dxbc_ref, ddt_ref, o_hbm, acc_ref, tr_ref, sem):
        k = pl.program_id(0)

        @pl.when(k == 0)
        def _():
            acc_ref[...] = jnp.zeros_like(acc_ref)

        dproj = jnp.concatenate([dq_ref[...], dkv_ref[...], dz_ref[...], dxbc_ref[...], ddt_ref[...]], axis=1)
        acc_ref[...] += _mm_tn(a_ref[...], dproj)

        @pl.when(k == nk - 1)
        def _():
            for j in range(PROJ_W // 128):
                tr_ref[j * 128:(j + 1) * 128, :] = acc_ref[:, j * 128:(j + 1) * 128].T
            cp = pltpu.make_async_copy(tr_ref, o_hbm, sem)
            cp.start()
            cp.wait()

    return _pcall(body, [h1] + list(pieces), name="wgrad_in", grid=(nk,),
                  out_shape=[jax.ShapeDtypeStruct((PROJ_W, D), F32)],
                  in_specs=[pl.BlockSpec((tk, p.shape[1]), lambda k: (k, 0)) for p in [h1] + list(pieces)],
                  out_specs=[pl.BlockSpec(memory_space=pl.ANY)],
                  scratch_shapes=[pltpu.VMEM((D, PROJ_W), F32), pltpu.VMEM((PROJ_W, D), F32), pltpu.SemaphoreType.DMA],
                  carry=carry)


_SMALL = ["ada_b", "norm1", "conv_w", "conv_b", "dt_bias", "A_log", "D_skip", "sinks", "attn_out_norm",
          "ssm_out_norm", "norm2", "rel_bias", "final_norm"]


def _small_grad(name, gs, chip):
    if name == "ada_b":
        return jnp.concatenate([gs[j:j + 1, :] for j in range(6)], axis=1)
    if name == "conv_w":
        full = gs[7:11, :]
        out = full[:, 0:256]
        for j in range(1, 4):
            out = jnp.where(chip == j, full[:, j * 256:(j + 1) * 256], out)
        return out
    row, width = {"norm1": (6, D), "conv_b": (11, D), "norm2": (12, D), "final_norm": (13, D),
                  "attn_out_norm": (14, QW), "ssm_out_norm": (15, SW), "dt_bias": (16, NH), "A_log": (17, NH),
                  "D_skip": (18, NH), "sinks": (19, NH), "rel_bias": (24, NH)}[name]
    rows = NBUCKET if name == "rel_bias" else 1
    return gs[row:row + rows, 0:width]


def _small_update(small_all, where, ws, ms, vs):
    n = len(_SMALL)

    def body(where_ref, sa_ref, *refs):
        w_refs, m_refs, v_refs, outs = refs[:n], refs[n:2 * n], refs[2 * n:3 * n], refs[3 * n:]
        gs = sa_ref[0]
        for b in range(1, 8):
            gs = gs + sa_ref[b]
        chip = where_ref[1]
        for i, name in enumerate(_SMALL):
            g = _small_grad(name, gs, chip)
            lead = (0,) if name == "conv_w" else ()
            d, mo, vo = _adamw(w_refs[i][lead + (...,)], g, m_refs[i][lead + (...,)], v_refs[i][lead + (...,)])
            for k, val in enumerate((g, d, mo, vo)):
                outs[k * n + i][lead + (...,)] = val
        outs[4 * n][...] = gs[20:21, 0:128]

    shapes = [jax.ShapeDtypeStruct(w.shape, F32) for w in ws]
    vmem = pl.BlockSpec(memory_space=pltpu.VMEM)
    res = pl.pallas_call(
        body, name="small_update", out_shape=shapes * 4 + [jax.ShapeDtypeStruct((1, 128), F32)],
        in_specs=[pl.BlockSpec(memory_space=pltpu.SMEM)] + [vmem] * (1 + 3 * n), out_specs=[vmem] * (4 * n + 1),
    )(where, small_all, *ws, *ms, *vs)
    return [res[k * n:(k + 1) * n] for k in range(4)], res[4 * n][0, 0]


def _add_half(name, g, got, where, by_cols=False):
    rr, cc = got.shape[1:]
    if by_cols:
        mine = pl.BlockSpec((None, rr, cc), lambda i, w_ref: (i, 0, w_ref[0]))
    else:
        mine = pl.BlockSpec((None, None, rr, cc), lambda i, w_ref: (i, w_ref[0], 0, 0))

    def body(w_ref, g_ref, r_ref, o_ref, own_ref):
        s = g_ref[...] + r_ref[...]
        o_ref[...] = s.astype(BF16)

        @pl.when(pl.program_id(0) == w_ref[1])
        def _():
            own_ref[...] = s

    spec = pl.BlockSpec((None, rr, cc), lambda i, w_ref: (i, 0, 0))
    return _pcall(body, [where, g, got], name=name, grid=(4,), nprefetch=1,
                  out_shape=[jax.ShapeDtypeStruct(got.shape, BF16), jax.ShapeDtypeStruct((rr, cc), F32)],
                  in_specs=[mine, spec],
                  out_specs=[spec, pl.BlockSpec((rr, cc), lambda i, w_ref: (0, 0))])


def _add_chips(name, own, got):
    rr, cc = own.shape
    tr = rr // 2 if rr % 32 == 0 else rr

    def body(s_ref, r_ref, o_ref):
        o_ref[...] = ((s_ref[...] + r_ref[0].astype(F32)) + r_ref[1].astype(F32)) + r_ref[2].astype(F32)

    spec = pl.BlockSpec((tr, cc), lambda i: (i, 0))
    return _pcall(body, [own, got], name=name, grid=(rr // tr,), out_shape=[jax.ShapeDtypeStruct((rr, cc), F32)],
                  in_specs=[spec, pl.BlockSpec((3, tr, cc), lambda i: (0, i, 0))], out_specs=[spec])[0]


def _adamw_halves(name, mine, got, w, m, v, where, by_cols=False):
    rr, cc = mine.shape

    def body(w_ref_, t_ref, r_ref, w_ref, m_ref, v_ref, g_ref, d_ref, mo_ref, vo_ref):
        g = jnp.where(pl.program_id(0) == w_ref_[0], t_ref[...], r_ref[...])
        g_ref[...] = g
        d_ref[...], mo_ref[...], vo_ref[...] = _adamw(w_ref[...], g, m_ref[...], v_ref[...])

    if by_cols:
        grid = (2, 1)
        half = pl.BlockSpec((rr, cc), lambda h, i, w_ref_: (0, 0))
        full = pl.BlockSpec((rr, cc), lambda h, i, w_ref_: (0, h))
    else:
        tr = rr // 2
        grid = (2, 2)
        half = pl.BlockSpec((tr, cc), lambda h, i, w_ref_: (i, 0))
        full = pl.BlockSpec((None, tr, cc), lambda h, i, w_ref_: (0, 2 * h + i, 0))
    return _pcall(body, [where, mine, got, w, m, v], name=name, grid=grid, nprefetch=1,
                  out_shape=[jax.ShapeDtypeStruct(w.shape, F32)] * 4,
                  in_specs=[half, half, full, full, full], out_specs=[full] * 4)


def _bias_table(rel_bias, bucket, mask):
    def body(rb_ref, bk_ref, mk_ref, o_ref):
        bk = bk_ref[...]
        valid = mk_ref[...] > 0
        for h in range(NH):
            acc = jnp.zeros((BLK, 2 * BLK), F32)
            for b in range(NBUCKET):
                acc = jnp.where(bk == b, rb_ref[b, h], acc)
            o_ref[h] = jnp.where(valid, acc, NEG)

    vmem = pl.BlockSpec(memory_space=pltpu.VMEM)
    return pl.pallas_call(
        body, name="bias_table", out_shape=jax.ShapeDtypeStruct((NH, BLK, 2 * BLK), F32),
        in_specs=[pl.BlockSpec(memory_space=pltpu.SMEM), vmem, vmem], out_specs=vmem,
    )(rel_bias, bucket, mask)


def _pack_small(dsh1, p1, dsh2, p2, dg1a, dg1b, dg2, norm1, norm2, scale1, scale2, dcw, dcb, dfn,
                dnw_attn, dnw_ssm, dhd, av, dsink, drel, loss_acc):
    def body(dsh1_ref, p1_ref, dsh2_ref, p2_ref, dg1a_ref, dg1b_ref, dg2_ref, n1_ref, n2_ref, s1_ref, s2_ref,
             dcw_ref, dcb_ref, dfn_ref, da_ref, ds_ref, dhd_ref, av_ref, dsink_ref, drel_ref, loss_ref, o_ref):
        o_ref[...] = jnp.zeros_like(o_ref)
        p1v, p2v = p1_ref[0:1, :], p2_ref[0:1, :]
        o_ref[0:1, :] = dsh1_ref[0:1, :]
        o_ref[1:2, :] = p1v * n1_ref[...]
        o_ref[2:3, :] = dg1a_ref[0:1, :] + dg1b_ref[0:1, :]
        o_ref[3:4, :] = dsh2_ref[0:1, :]
        o_ref[4:5, :] = p2v * n2_ref[...]
        o_ref[5:6, :] = dg2_ref[0:1, :]
        o_ref[6:7, :] = p1v * (1.0 + s1_ref[...])
        o_ref[7:11, :] = dcw_ref[0:4, :]
        o_ref[11:12, :] = dcb_ref[0:1, :]
        o_ref[12:13, :] = p2v * (1.0 + s2_ref[...])
        o_ref[13:14, :] = dfn_ref[0:1, :]
        o_ref[14:15, 0:QW] = da_ref[0:1, :]
        o_ref[15:16, 0:SW] = ds_ref[0:1, :]
        o_ref[16:17, 0:128] = dhd_ref[0:1, :]
        o_ref[17:18, 0:128] = dhd_ref[1:2, :] * av_ref[...]
        o_ref[18:19, 0:128] = dhd_ref[2:3, :]
        o_ref[19:20, 0:128] = dsink_ref[0:1, :]
        o_ref[20:21, 0:128] = loss_ref[0:1, :]
        o_ref[24:56, 0:128] = drel_ref[...]

    return pl.pallas_call(body, name="pack_small", out_shape=jax.ShapeDtypeStruct((56, D), F32))(
        dsh1, p1, dsh2, p2, dg1a, dg1b, dg2, norm1, norm2, scale1, scale2, dcw, dcb, dfn,
        dnw_attn, dnw_ssm, dhd, av, dsink, drel, loss_acc)


def _pad_row(a, rows=1):
    return jnp.pad(a.reshape(rows, -1), ((0, 0), (0, D - a.size // rows)))


def kernel(x, c, ada_w, ada_b, norm1, w_in, conv_w, conv_b, dt_bias, A_log, D_skip, sinks, attn_out_norm, ssm_out_norm, w_o, norm2, w_gate_up, w_down, rel_bias, final_norm, loss_target, m_ada_w, m_ada_b, m_norm1, m_w_in, m_conv_w, m_conv_b, m_dt_bias, m_A_log, m_D_skip, m_sinks, m_attn_out_norm, m_ssm_out_norm, m_w_o, m_norm2, m_w_gate_up, m_w_down, m_rel_bias, m_final_norm, v_ada_w, v_ada_b, v_norm1, v_w_in, v_conv_w, v_conv_b, v_dt_bias, v_A_log, v_D_skip, v_sinks, v_attn_out_norm, v_ssm_out_norm, v_w_o, v_norm2, v_w_gate_up, v_w_down, v_rel_bias, v_final_norm):
    xi, yi, ci = lax.axis_index("x"), lax.axis_index("y"), lax.axis_index("c")
    chip = 2 * xi + yi
    me = 4 * xi + 2 * yi + ci
    where = jnp.stack([ci, chip]).astype(jnp.int32)
    xs2, tgt = x[0], loss_target[0]

    first = jnp.concatenate([c, _pad_row(conv_w[0], CK), jnp.zeros((3, D), F32)], axis=0)
    w_in_t, m_w_in_t, v_w_in_t = w_in[0].T, m_w_in[0].T, v_w_in[0].T
    w_in_b, w_o_b, w_dn_b = w_in_t.astype(BF16), w_o[0].astype(BF16), w_down[0].astype(BF16)
    w_gu_b = w_gate_up[0].astype(BF16)
    hw = D // 2
    fetch_half = _Carry(
        [w_in_b], [jax.ShapeDtypeStruct((4,) + w_in_b.shape, BF16)],
        lambda x_, y_, c_: [(None, 0, slice(None), 0, 2 * x_ + y_)] + [
            (f, 0, (slice(None), pl.ds(c_ * hw, hw)), 0, (2 * x_ + y_, slice(None), pl.ds(c_ * hw, hw))) for f in _CHIPS3])

    def swap_halves(x_, y_, c_):
        there = [(jnp.bitwise_xor(2 * x_ + y_, k + 1), slice(None), pl.ds(c_ * hw, hw)) for k in range(3)]
        return [(_SIBLING, 1, at, 1, at) for at in there]

    first_all, w_in_g = _exchange_two("gather_first", _merge(_gather8_carry(first), fetch_half), swap_halves)
    c_all = first_all[:, 0, :]
    cw_full = jnp.concatenate([first_all[2 * j, 1:1 + CK, 0:256] for j in range(4)], axis=1)
    w_in_f = jnp.pad(w_in_g.reshape(IN_W, D), ((0, PROJ_W - IN_W), (0, 0)))

    ncol = ada_w.shape[2]
    mod_cols = _ada_fwd(c_all, ada_w[0], lax.dynamic_slice(ada_b, (0, chip * ncol), (1, ncol)))
    mod_all = _exchange("gather_mod", _gather_chips_carry([mod_cols]))[0]
    mod = lax.dynamic_slice(jnp.transpose(mod_all, (1, 0, 2)).reshape(8, 4 * ncol), (me, 0), (1, 4 * ncol))
    shift1, scale1, gate1, shift2, scale2, gate2 = [mod[:, j * D:(j + 1) * D] for j in range(6)]
    a1 = norm1 * (1.0 + scale1)
    a2 = norm2 * (1.0 + scale2)

    hdn = DFF // 8
    q, kv, z, xbc, dtr, w_o_g, w_dna_g = _in_proj_fwd(xs2, a1, shift1, w_in_f,
                                                      carry=_gather_chips_carry([w_o_b, w_dn_b[0:hdn]]))
    w_o_f = w_o_g.reshape(D, D)
    bucket, mask = _attn_geometry()
    bucket = jnp.asarray(bucket)
    bias = _bias_table(rel_bias, bucket, jnp.asarray(mask.astype(np.int32)))
    sinks1 = sinks[0]
    ya, w_ga_g = _attn_fwd(q, kv, bias, sinks1, attn_out_norm, carry=_gather_chips_carry([w_gu_b[0:D // 2]]))
    cw8 = jnp.concatenate([cw_full, jnp.zeros((4, XBCW), F32)], axis=0)
    dtb = _pad_row(dt_bias)[:, 0:128]
    av = _pad_row(-jnp.exp(A_log))[:, 0:128]
    dk = jnp.repeat(D_skip, HD, axis=1)
    ys, hs, pre, w_gb_g, w_dnb_g = _ssd_fwd(z, xbc, dtr, cw8, conv_b, dtb, av, dk, ssm_out_norm,
                                            carry=_gather_chips_carry([w_gu_b[D // 2:D], w_dn_b[hdn:2 * hdn]]))
    w_dn_f = jnp.stack([w_dna_g, w_dnb_g], axis=1).reshape(DFF, D)
    fn = final_norm[None, :]
    x1, gu, dx2, loss_acc, dfn = _mlp_fwd(xs2, ya, ys, tgt, w_o_f, w_ga_g, w_gb_g, w_dn_f, gate1, a2, shift2, gate2, fn)

    def to_sibling(p):
        return _Carry([p], [jax.ShapeDtypeStruct((4,) + p.shape[2:], F32)],
                      lambda x_, y_, c_: [(_SIBLING, 0, (j, 1 - c_), 0, j) for j in range(4)])

    def to_chips(s4):
        return _Carry([s4], [jax.ShapeDtypeStruct((3,) + s4.shape[1:], s4.dtype)],
                      lambda x_, y_, c_: [(f, 0, jnp.bitwise_xor(2 * x_ + y_, k + 1), 0, k) for k, f in enumerate(_CHIPS3)])

    def back(t):
        return _Carry([t[None]], [jax.ShapeDtypeStruct((1,) + t.shape, F32)], lambda x_, y_, c_: [(_SIBLING, 0, 0, 0, 0)])

    dx1, dya, dys, act, dgu, h2, dsh2, p2 = _mlp_bwd(x1, gu, dx2, w_o_f, w_ga_g, w_gb_g, w_dn_f, gate1, a2, shift2, gate2)
    dq, dkv, dbias, dsink, dnw_attn, g_gu, g_dn_raw = _attn_bwd(q, kv, dya, bias, sinks1, attn_out_norm, h2, dgu,
                                                                 act, dx2)
    g_dn, dg2 = _gate_scale("wgrad_down_gate", g_dn_raw, gate2, w_dn_f)
    p_gu = g_gu.reshape(4, 2, D // 2, 2 * DFF // 4)
    p_dn = g_dn.reshape(4, 2, DFF // 8, D)
    drel = _rel_bias_grad(dbias, bucket)
    dz, dxbc, ddt, dcw, dcb, dnw_ssm, dhd, got1_gu, got1_dn = _ssd_bwd(
        z, xbc, pre, dtr, dys, hs, cw8, dtb, av, dk, ssm_out_norm, carry=_merge(to_sibling(p_gu), to_sibling(p_dn)))
    s4_gu, own_gu = _add_half("rs_add_half_gu", p_gu, got1_gu, where)
    s4_dn, own_dn = _add_half("rs_add_half_dn", p_dn, got1_dn, where)
    grad_x, h1, dsh1, p1 = _in_proj_bwd(xs2, dx1, a1, shift1, w_in_f, dq, dkv, dz, dxbc, ddt)
    g_in_t, got2_gu = _wgrad_in_t(h1, [dq, dkv, dz, dxbc, ddt], carry=to_chips(s4_gu))
    mine_gu = _add_chips("rs_add_chips_gu", own_gu, got2_gu)
    p_in = g_in_t[0:IN_W].reshape(4, IN_W // 4, D)

    def to_sibling_cols(p):
        return _Carry([p], [jax.ShapeDtypeStruct(p.shape[:2] + (D // 2,), F32)],
                      lambda x_, y_, c_: [(_SIBLING, 0, (j, slice(None), pl.ds((1 - c_) * (D // 2), D // 2)), 0, j)
                                          for j in range(4)])

    g_oa, dg1a, got1_in, got2_dn, got3_gu = _wgrad(
        "wgrad_o_attn", ya, dx1, gate1, w_o_f[0:QW],
        carry=_merge(to_sibling_cols(p_in), to_chips(s4_dn), back(mine_gu)))
    mine_dn = _add_chips("rs_add_chips_dn", own_dn, got2_dn)
    s4_in, own_in = _add_half("rs_add_half_in", p_in, got1_in, where, by_cols=True)
    g_os, dg1b, got2_in, got3_dn = _wgrad("wgrad_o_ssm", ys, dx1, gate1, w_o_f[QW:D],
                                          carry=_merge(to_chips(s4_in), back(mine_dn)))
    mine_in = _add_chips("rs_add_chips_in", own_in, got2_in)
    p_o = jnp.concatenate([g_oa, g_os], axis=0).reshape(4, 2, D // 8, D)

    small = _pack_small(dsh1, p1, dsh2, p2, dg1a, dg1b, dg2, norm1, norm2, scale1, scale2, dcw, dcb, dfn,
                        dnw_attn, dnw_ssm, dhd, av, dsink, drel, loss_acc)
    small_all, got1_o, got3_in = _exchange(
        "gather_small", _merge(_gather8_carry(small), to_sibling(p_o), back(mine_in)))
    s4_o, own_o = _add_half("rs_add_half_o", p_o, got1_o, where)
    mine_o = _add_chips("rs_add_chips_o", own_o, _exchange("rs_chips_o", to_chips(s4_o))[0])
    got3_o = _exchange("rs_back_o", back(mine_o))[0]
    small_res, loss = _small_update(
        small_all, where,
        [ada_b, norm1, conv_w, conv_b, dt_bias, A_log, D_skip, sinks, attn_out_norm, ssm_out_norm, norm2, rel_bias,
         final_norm[None, :]],
        [m_ada_b, m_norm1, m_conv_w, m_conv_b, m_dt_bias, m_A_log, m_D_skip, m_sinks, m_attn_out_norm,
         m_ssm_out_norm, m_norm2, m_rel_bias, m_final_norm[None, :]],
        [v_ada_b, v_norm1, v_conv_w, v_conv_b, v_dt_bias, v_A_log, v_D_skip, v_sinks, v_attn_out_norm,
         v_ssm_out_norm, v_norm2, v_rel_bias, v_final_norm[None, :]])
    small_out = [dict(zip(_SMALL, r)) for r in small_res]
    for r in small_out:
        r["final_norm"] = r["final_norm"][0]

    dmod_all = small_all[:, 0:6, :].reshape(8, 6 * D)
    dmod_loc = lax.dynamic_slice(dmod_all, (0, chip * ncol), (8, ncol))
    ada_out = _ada_bwd_adamw(c_all.T, dmod_loc, ada_w[0], m_ada_w[0], v_ada_w[0])

    big_gu = _adamw_halves("adamw_gate_up", mine_gu, got3_gu[0], w_gate_up, m_w_gate_up, v_w_gate_up, where)
    big_dn = _adamw_halves("adamw_down", mine_dn, got3_dn[0], w_down, m_w_down, v_w_down, where)
    big_o = _adamw_halves("adamw_o", mine_o, got3_o[0], w_o, m_w_o, v_w_o, where)
    big_in = [o.T[None] for o in _adamw_halves("adamw_in", mine_in, got3_in[0], w_in_t, m_w_in_t, v_w_in_t, where,
                                               by_cols=True)]
    big = [big_in, big_o, big_gu, big_dn]

    order = ["ada_w", "ada_b", "norm1", "w_in", "conv_w", "conv_b", "dt_bias", "A_log", "D_skip", "sinks",
             "attn_out_norm", "ssm_out_norm", "w_o", "norm2", "w_gate_up", "w_down", "rel_bias", "final_norm"]
    bigname = {"w_in": 0, "w_o": 1, "w_gate_up": 2, "w_down": 3}
    res = [loss, grad_x[None]]
    for kind in range(4):
        for nm in order:
            if nm == "ada_w":
                res.append(ada_out[kind][None])
            elif nm in bigname:
                res.append(big[bigname[nm]][kind])
            else:
                res.append(small_out[kind][nm])
    return tuple(res)
```

```python
import numpy as np
import jax
import jax.numpy as jnp
from jax import lax
from jax.experimental import pallas as pl
from jax.experimental.pallas import tpu as pltpu

F32, BF16 = jnp.float32, jnp.bfloat16
HI = lax.Precision.HIGHEST

D = 1024
QW, KVW = 512, 128
NH, HD, NKV = 8, 64, 2
SW = 512
NST = 128
XBCW = 1024
CK = 4
BLK = 128
DFF = 2816
IN_W = 2312
PROJ_W = 2432
EPS = 1e-6
NEG = -1e30
NBUCKET = 32

B1, B2, LR, AEPS, WD, STEP = 0.9, 0.999, 0.001, 1e-08, 0.01, 10

VMEM_LIMIT = 56 * 1024 * 1024

_NT = (((1,), (1,)), ((), ()))
_TN = (((0,), (0,)), ((), ()))


def _mm(a, b):
    return jnp.dot(a, b, preferred_element_type=F32)


def _mm_nt(a, b):
    return lax.dot_general(a, b, _NT, preferred_element_type=F32)


def _mm_tn(a, b):
    return lax.dot_general(a, b, _TN, preferred_element_type=F32)


def _mm_hi(a, b):
    return jnp.dot(a, b, preferred_element_type=F32, precision=HI)


def _split3(x):
    hi = x.astype(BF16)
    r = x - hi.astype(F32)
    mid = r.astype(BF16)
    lo = (r - mid.astype(F32)).astype(BF16)
    return hi, mid, lo


def _sel_r(x, e):
    hi, mid, lo = _split3(x)
    return (_mm(hi, e) + _mm(mid, e)) + _mm(lo, e)


def _sel_l(e, x):
    hi, mid, lo = _split3(x)
    return (_mm(e, hi) + _mm(e, mid)) + _mm(e, lo)


def _sig(x):
    return 1.0 / (1.0 + jnp.exp(-x))


def _cp(sem):
    return pltpu.CompilerParams(dimension_semantics=sem, vmem_limit_bytes=VMEM_LIMIT)


def _row(shape):
    nd = len(shape)
    return pl.BlockSpec(shape, lambda *_: (0,) * nd)


def _adamw(w, g, m, v):
    m = B1 * m + (1.0 - B1) * g
    v = B2 * v + (1.0 - B2) * (g * g)
    m_hat = m / (1.0 - B1 ** STEP)
    v_hat = v / (1.0 - B2 ** STEP)
    delta = -LR * (m_hat / (jnp.sqrt(v_hat) + AEPS) + WD * w)
    return delta, m, v


class _Carry:
    def __init__(self, inps, outs, copies):
        self.inps, self.outs, self.copies = list(inps), list(outs), copies
        self.n = len(copies(0, 0, 0))

    def descriptors(self, in_refs, out_refs, send_sems, recv_sems):
        x, y, c = lax.axis_index("x"), lax.axis_index("y"), lax.axis_index("c")
        out = []
        for j, (flip, a, si, o, di) in enumerate(self.copies(x, y, c)):
            if flip is None:
                out.append(pltpu.make_async_copy(in_refs[a].at[si], out_refs[o].at[di], send_sems.at[j]))
            else:
                fx, fy, fc = flip
                peer = (1 - x if fx else x, 1 - y if fy else y, 1 - c if fc else c)
                out.append(pltpu.make_async_remote_copy(
                    src_ref=in_refs[a].at[si], dst_ref=out_refs[o].at[di],
                    send_sem=send_sems.at[j], recv_sem=recv_sems.at[j],
                    device_id=peer, device_id_type=pl.DeviceIdType.MESH))
        return out


def _pcall(body, args, *, name, grid, in_specs, out_specs, out_shape, scratch_shapes=(), sem=None, nprefetch=0,
           carry=None):
    out_shape, out_specs = list(out_shape), list(out_specs)
    in_specs, scratch_shapes = list(in_specs), list(scratch_shapes)
    nin, nout, nscr = len(in_specs), len(out_shape), len(scratch_shapes)
    run = body
    if carry is not None:
        ncin, ncout = len(carry.inps), len(carry.outs)
        hbm = pl.BlockSpec(memory_space=pl.ANY)

        def run(*refs):
            pre, r = refs[:nprefetch], refs[nprefetch:]
            ins, cins = r[:nin], r[nin:nin + ncin]
            r = r[nin + ncin:]
            outs, couts = r[:nout], r[nout:nout + ncout]
            r = r[nout + ncout:]
            scr, (send_sems, recv_sems) = r[:nscr], r[nscr:]
            first = pl.program_id(0) == 0
            last = pl.program_id(0) == grid[0] - 1
            for ax in range(1, len(grid)):
                first = jnp.logical_and(first, pl.program_id(ax) == 0)
                last = jnp.logical_and(last, pl.program_id(ax) == grid[ax] - 1)

            @pl.when(first)
            def _():
                for d in carry.descriptors(cins, couts, send_sems, recv_sems):
                    d.start()

            body(*pre, *ins, *outs, *scr)

            @pl.when(last)
            def _():
                for d in carry.descriptors(cins, couts, send_sems, recv_sems):
                    d.wait()

        in_specs = in_specs + [hbm] * ncin
        out_specs = out_specs + [hbm] * ncout
        out_shape = out_shape + carry.outs
        scratch_shapes = scratch_shapes + [pltpu.SemaphoreType.DMA((carry.n,)), pltpu.SemaphoreType.DMA((carry.n,))]
        args = list(args) + carry.inps
    if sem is None:
        sem = ("arbitrary",) * len(grid)
    if nprefetch:
        kw = dict(grid_spec=pltpu.PrefetchScalarGridSpec(num_scalar_prefetch=nprefetch, grid=grid, in_specs=in_specs,
                                                         out_specs=out_specs, scratch_shapes=scratch_shapes))
    else:
        kw = dict(grid=grid, in_specs=in_specs, out_specs=out_specs, scratch_shapes=scratch_shapes)
    res = pl.pallas_call(run, name=name, out_shape=out_shape, compiler_params=_cp(sem), **kw)(*args)
    return list(res)


def _merge(*carries):
    inps, outs, offs = [], [], []
    for cr in carries:
        offs.append((len(inps), len(outs)))
        inps += cr.inps
        outs += cr.outs

    def copies(x, y, c):
        return [(f, a + io, si, o + oo, di) for cr, (io, oo) in zip(carries, offs) for f, a, si, o, di in cr.copies(x, y, c)]

    return _Carry(inps, outs, copies)


def _exchange(name, carry):
    return _pcall(lambda: None, [], name=name, grid=(1,), in_specs=[], out_specs=[], out_shape=[], carry=carry)


def _exchange_two(name, carry, then):
    second = _Carry([], [], then)
    nin, nout = len(carry.inps), len(carry.outs)

    def body(*refs):
        ins, outs = refs[:nin], refs[nin:nin + nout]
        send_a, recv_a, send_b, recv_b = refs[nin + nout:]
        for descs in (carry.descriptors(ins, outs, send_a, recv_a), second.descriptors(outs, outs, send_b, recv_b)):
            for d in descs:
                d.start()
            for d in descs:
                d.wait()

    hbm = pl.BlockSpec(memory_space=pl.ANY)
    return list(pl.pallas_call(
        body, name=name, out_shape=carry.outs, in_specs=[hbm] * nin, out_specs=[hbm] * nout,
        scratch_shapes=[pltpu.SemaphoreType.DMA((carry.n,)), pltpu.SemaphoreType.DMA((carry.n,)),
                        pltpu.SemaphoreType.DMA((second.n,)), pltpu.SemaphoreType.DMA((second.n,))],
    )(*carry.inps))


_ALL7 = [(f >> 2 & 1, f >> 1 & 1, f & 1) for f in range(1, 8)]
_CHIPS3 = [(0, 1, 0), (1, 0, 0), (1, 1, 0)]
_SIBLING = (0, 0, 1)


def _gather8_carry(blk):
    def copies(x, y, c):
        me = 4 * x + 2 * y + c
        return [(None, 0, 0, 0, me)] + [(f, 0, 0, 0, me) for f in _ALL7]

    return _Carry([blk[None]], [jax.ShapeDtypeStruct((8,) + blk.shape, blk.dtype)], copies)


def _gather_chips_carry(blks):
    def copies(x, y, c):
        chip = 2 * x + y
        return [(f, a, 0, a, chip) for a in range(len(blks)) for f in [None] + _CHIPS3]

    return _Carry([b[None] for b in blks], [jax.ShapeDtypeStruct((4,) + b.shape, b.dtype) for b in blks], copies)


def _ada_fwd(c_all, w_loc, b_loc):
    n = w_loc.shape[1]
    tn = 512

    def body(c_ref, w_ref, b_ref, o_ref):
        cv = c_ref[...]
        cond = cv * _sig(cv)
        o_ref[...] = _mm_hi(cond, w_ref[...]) + b_ref[...]

    return pl.pallas_call(
        body, name="ada_fwd", grid=(n // tn,),
        out_shape=jax.ShapeDtypeStruct((8, n), F32),
        in_specs=[_row((8, D)), pl.BlockSpec((D, tn), lambda j: (0, j)), pl.BlockSpec((1, tn), lambda j: (0, j))],
        out_specs=pl.BlockSpec((8, tn), lambda j: (0, j)),
        compiler_params=_cp(("parallel",)),
    )(c_all, w_loc, b_loc)


def _ada_bwd_adamw(c_all_t, dmod_loc, w, m, v, carry=None):
    n = w.shape[1]
    tn = 512

    def body(ct_ref, dm_ref, w_ref, m_ref, v_ref, g_ref, d_ref, mo_ref, vo_ref):
        ct = ct_ref[...]
        cond = ct * _sig(ct)
        dm = dm_ref[...]
        g = cond[:, 0:1] * dm[0:1, :]
        for b in range(1, 8):
            g = g + cond[:, b:b + 1] * dm[b:b + 1, :]
        g_ref[...] = g
        d_ref[...], mo_ref[...], vo_ref[...] = _adamw(w_ref[...], g, m_ref[...], v_ref[...])

    wspec = pl.BlockSpec((D, tn), lambda j: (0, j))
    return _pcall(
        body, [c_all_t, dmod_loc, w, m, v], name="ada_bwd_adamw", grid=(n // tn,),
        out_shape=[jax.ShapeDtypeStruct((D, n), F32)] * 4,
        in_specs=[_row((D, 8)), pl.BlockSpec((8, tn), lambda j: (0, j)), wspec, wspec, wspec],
        out_specs=[wspec] * 4, carry=carry)


def _in_proj_fwd(x, a1, sh1, w_in, carry=None):
    s = x.shape[0]
    tm = 512

    def body(x_ref, a_ref, s_ref, w_ref, q_ref, kv_ref, z_ref, xbc_ref, dt_ref):
        def norm(rows):
            xv = x_ref[rows, :]
            r = lax.rsqrt(jnp.mean(xv * xv, axis=-1, keepdims=True) + EPS)
            return (xv * r * a_ref[...] + s_ref[...]).astype(BF16)

        def project(rows, h):
            p = _mm_nt(h, w_ref[...])
            q_ref[rows, :] = p[:, 0:512].astype(BF16)
            kv_ref[rows, :] = p[:, 512:768].astype(BF16)
            z_ref[rows, :] = p[:, 768:1280]
            xbc_ref[rows, :] = p[:, 1280:2304]
            dt_ref[rows, :] = p[:, 2304:2432]

        r0, r1 = slice(0, tm // 2), slice(tm // 2, tm)
        h0 = norm(r0)
        project(r0, h0)
        project(r1, norm(r1))

    def tok(w):
        return pl.BlockSpec((tm, w), lambda i: (i, 0))

    return _pcall(
        body, [x, a1, sh1, w_in], name="in_proj_fwd", grid=(s // tm,),
        out_shape=[jax.ShapeDtypeStruct((s, QW), BF16), jax.ShapeDtypeStruct((s, 2 * KVW), BF16),
                   jax.ShapeDtypeStruct((s, SW), F32), jax.ShapeDtypeStruct((s, XBCW), F32),
                   jax.ShapeDtypeStruct((s, 128), F32)],
        in_specs=[tok(D), _row((1, D)), _row((1, D)), _row((PROJ_W, D))],
        out_specs=[tok(QW), tok(2 * KVW), tok(SW), tok(XBCW), tok(128)], carry=carry)


def _in_proj_bwd(x, dx1, a1, sh1, w_in, dq, dkv, dz, dxbc, ddt, carry=None):
    s = x.shape[0]
    tm = 512
    nsteps = s // tm

    def body(x_ref, dx1_ref, a_ref, s_ref, w_ref, dq_ref, dkv_ref, dz_ref, dxbc_ref, ddt_ref,
             gx_ref, gw_hbm, dsh_ref, p_ref, acc_ref, tr_ref, sem):
        i = pl.program_id(0)

        @pl.when(i == 0)
        def _():
            dsh_ref[...] = jnp.zeros_like(dsh_ref)
            p_ref[...] = jnp.zeros_like(p_ref)
            acc_ref[...] = jnp.zeros_like(acc_ref)

        def gather(st):
            rows = st["rows"]
            st["dproj"] = jnp.concatenate([dq_ref[rows, :], dkv_ref[rows, :], dz_ref[rows, :], dxbc_ref[rows, :],
                                           ddt_ref[rows, :]], axis=1)

        def back(st):
            st["dh"] = _mm(st["dproj"], w_ref[...])

        def norm(st):
            rows, dh = st["rows"], st.pop("dh")
            xv = x_ref[rows, :]
            r = lax.rsqrt(jnp.mean(xv * xv, axis=-1, keepdims=True) + EPS)
            xn = xv * r
            a = a_ref[...]
            st["h"] = (xn * a + s_ref[...]).astype(BF16)
            st["dsh"] = jnp.sum(dh, axis=0, keepdims=True)
            st["p"] = jnp.sum(dh * xn, axis=0, keepdims=True)
            u = dh * a
            gx_ref[rows, :] = dx1_ref[rows, :] + r * u - xn * (r * jnp.mean(u * xn, axis=-1, keepdims=True))

        def wgrad(st):
            acc_ref[...] += _mm_tn(st.pop("h"), st.pop("dproj"))

        g0, g1 = [dict(rows=slice(k * (tm // 2), (k + 1) * (tm // 2))) for k in range(2)]
        for stage, st in [(gather, g0), (back, g0), (gather, g1), (norm, g0), (back, g1), (wgrad, g0), (norm, g1),
                          (wgrad, g1)]:
            stage(st)
        dsh_ref[0:1, :] += g0["dsh"] + g1["dsh"]
        p_ref[0:1, :] += g0["p"] + g1["p"]

        @pl.when(i == nsteps - 1)
        def _():
            for j in range(PROJ_W // 128):
                tr_ref[j * 128:(j + 1) * 128, :] = acc_ref[:, j * 128:(j + 1) * 128].T
            cp = pltpu.make_async_copy(tr_ref, gw_hbm, sem)
            cp.start()
            cp.wait()

    def tok(w):
        return pl.BlockSpec((tm, w), lambda i: (i, 0))

    return _pcall(
        body, [x, dx1, a1, sh1, w_in, dq, dkv, dz, dxbc, ddt], name="in_proj_bwd", grid=(nsteps,),
        out_shape=[jax.ShapeDtypeStruct((s, D), F32), jax.ShapeDtypeStruct((PROJ_W, D), F32),
                   jax.ShapeDtypeStruct((8, D), F32), jax.ShapeDtypeStruct((8, D), F32)],
        in_specs=[tok(D), tok(D), _row((1, D)), _row((1, D)), _row((PROJ_W, D)),
                  tok(QW), tok(2 * KVW), tok(SW), tok(XBCW), tok(128)],
        out_specs=[tok(D), pl.BlockSpec(memory_space=pl.ANY), _row((8, D)), _row((8, D))],
        scratch_shapes=[pltpu.VMEM((D, PROJ_W), F32), pltpu.VMEM((PROJ_W, D), F32), pltpu.SemaphoreType.DMA],
        carry=carry)


def _attn_geometry():
    dist = np.arange(BLK)[:, None] + BLK - np.arange(2 * BLK)[None, :]
    n = np.maximum(dist, 0)
    max_exact = NBUCKET // 2
    large = max_exact + (np.log(np.maximum(n, 1) / max_exact) / np.log(128 / max_exact)
                         * (NBUCKET - max_exact)).astype(np.int32)
    large = np.minimum(large, NBUCKET - 1)
    bucket = np.where(n < max_exact, n, large).astype(np.int32)
    mask = (dist >= 0) & (dist < 128)
    return bucket, mask


def _attn_heads(is_first, q_blk, kvw, bias_ref, sinks_ref):
    qv = q_blk * 0.125
    col = lax.broadcasted_iota(jnp.int32, (BLK, 2 * BLK), 1)
    first = jnp.where(jnp.logical_and(is_first, col < BLK), NEG, 0.0)
    groups = []
    for g in range(NKV):
        qs = jnp.concatenate([qv[:, (4 * g + r) * HD:(4 * g + r + 1) * HD] for r in range(4)], axis=0)
        kw = kvw[:, g * HD:(g + 1) * HD]
        vw = kvw[:, KVW + g * HD:KVW + (g + 1) * HD]
        sc = _mm_nt(qs, kw)
        pn, ps = [], []
        for r in range(4):
            h = 4 * g + r
            sr = sc[r * BLK:(r + 1) * BLK] + bias_ref[h] + first
            sink = sinks_ref[h]
            m = jnp.maximum(jnp.max(sr, axis=-1, keepdims=True), sink)
            p = jnp.exp(sr - m)
            es = jnp.exp(sink - m)
            inv = 1.0 / (jnp.sum(p, axis=-1, keepdims=True) + es)
            pn.append(p * inv)
            ps.append(es * inv)
        pn = jnp.concatenate(pn, axis=0)
        ps = jnp.concatenate(ps, axis=0)
        o = _mm(pn.astype(BF16), vw)
        groups.append((qs, kw, vw, pn, ps, o))
    return groups


def _unstack_heads(parts):
    return jnp.concatenate([p[r * BLK:(r + 1) * BLK] for p in parts for r in range(4)], axis=1)


NB = 2


def _attn_fwd(q, kv, bias, sinks, nw, carry=None):
    s = q.shape[0]

    def body(q_ref, kvp_ref, kvc_ref, bias_ref, sinks_ref, nw_ref, y_ref):
        t = pl.program_id(0)
        kv3 = jnp.concatenate([kvp_ref[...], kvc_ref[...]], axis=0)
        for sub in range(NB):
            rows = slice(sub * BLK, (sub + 1) * BLK)
            groups = _attn_heads(jnp.logical_and(t == 0, sub == 0), q_ref[rows, :], kv3[sub * BLK:(sub + 2) * BLK],
                                 bias_ref, sinks_ref)
            o = _unstack_heads([g[5] for g in groups])
            r = lax.rsqrt(jnp.mean(o * o, axis=-1, keepdims=True) + EPS)
            y_ref[rows, :] = (o * r * nw_ref[...]).astype(BF16)

    return _pcall(
        body, [q, kv, kv, bias, sinks, nw], name="attn_fwd", grid=(s // (NB * BLK),),
        out_shape=[jax.ShapeDtypeStruct((s, QW), BF16)],
        in_specs=[pl.BlockSpec((NB * BLK, QW), lambda t: (t, 0)),
                  pl.BlockSpec((BLK, 2 * KVW), lambda t: (jnp.maximum(NB * t - 1, 0), 0)),
                  pl.BlockSpec((NB * BLK, 2 * KVW), lambda t: (t, 0)),
                  _row((NH, BLK, 2 * BLK)),
                  pl.BlockSpec(memory_space=pltpu.SMEM),
                  _row((1, QW))],
        out_specs=[pl.BlockSpec((NB * BLK, QW), lambda t: (t, 0))], carry=carry)


def _attn_bwd(q, kv, dya, bias, sinks, nw, act, dx2, gate2, w_dn, carry=None):
    s = q.shape[0]
    nt = s // (NB * BLK)
    npiece = DFF // NB

    def body(q_ref, kvp_ref, kvc_ref, dy_ref, bias_ref, sinks_ref, nw_ref, act_ref, dx2_ref, g2_ref, wdn_ref,
             dq_ref, dkv_ref, dbias_ref, dsink_ref, dnw_ref, gdn_hbm, dg2_ref, carry_ref, held_ref, acc_ref, sem):
        t = pl.program_id(0)

        @pl.when(t == 0)
        def _():
            carry_ref[...] = jnp.zeros_like(carry_ref)
            held_ref[...] = jnp.zeros_like(held_ref)
            dbias_ref[...] = jnp.zeros_like(dbias_ref)
            dsink_ref[...] = jnp.zeros_like(dsink_ref)
            dnw_ref[...] = jnp.zeros_like(dnw_ref)
            acc_ref[...] = jnp.zeros_like(acc_ref)

        def wgrad_piece(sub):
            rows = slice(sub * npiece, (sub + 1) * npiece)
            acc_ref[rows, :] += _mm_tn(act_ref[:, rows], dx2_ref[...].astype(BF16))

        def block(sub, kv3):
            rows = slice(sub * BLK, (sub + 1) * BLK)
            groups = _attn_heads(jnp.logical_and(t == 0, sub == 0), q_ref[rows, :], kv3[sub * BLK:(sub + 2) * BLK],
                                 bias_ref, sinks_ref)
            o = _unstack_heads([g[5] for g in groups])
            r = lax.rsqrt(jnp.mean(o * o, axis=-1, keepdims=True) + EPS)
            dy = dy_ref[rows, :]
            on = o * r
            dnw_ref[0:1, :] += jnp.sum(dy * on, axis=0, keepdims=True)
            u = dy * nw_ref[...]
            do = r * u - on * (r * jnp.mean(u * on, axis=-1, keepdims=True))
            dq_parts, dk_parts, dv_parts = [], [], []
            for g, (qs, kw, vw, pn, ps, og) in enumerate(groups):
                dos = jnp.concatenate([do[:, (4 * g + r_) * HD:(4 * g + r_ + 1) * HD] for r_ in range(4)], axis=0)
                delta = jnp.sum(dos * og, axis=-1, keepdims=True)
                dp = _mm_nt(dos.astype(BF16), vw)
                ds = pn * (dp - delta)
                dsk = ps * delta
                lane = lax.broadcasted_iota(jnp.int32, (1, 128), 1)
                for r_ in range(4):
                    h = 4 * g + r_
                    dbias_ref[h] += ds[r_ * BLK:(r_ + 1) * BLK]
                    dsink_ref[0:1, :] -= jnp.where(lane == h, jnp.sum(dsk[r_ * BLK:(r_ + 1) * BLK]), 0.0)
                dsb = ds.astype(BF16)
                dq_parts.append(_mm(dsb, kw) * 0.125)
                dk_parts.append(_mm_tn(dsb, qs))
                dv_parts.append(_mm_tn(pn.astype(BF16), dos.astype(BF16)))
            dq_ref[rows, :] = _unstack_heads(dq_parts).astype(BF16)
            return jnp.concatenate(dk_parts + dv_parts, axis=1)

        @pl.when(t < nt)
        def _():
            kv3 = jnp.concatenate([kvp_ref[...], kvc_ref[...]], axis=0)
            tail = carry_ref[...]
            for sub in range(NB):
                d = block(sub, kv3)
                done = tail + d[0:BLK]
                if sub == 0:
                    dkv_ref[0:(NB - 1) * BLK, :] = held_ref[...].astype(BF16)
                    dkv_ref[(NB - 1) * BLK:NB * BLK, :] = done.astype(BF16)
                else:
                    held_ref[(sub - 1) * BLK:sub * BLK, :] = done
                tail = d[BLK:2 * BLK]
                wgrad_piece(sub)
            carry_ref[...] = tail

        @pl.when(t == nt)
        def _():
            dkv_ref[0:(NB - 1) * BLK, :] = held_ref[...].astype(BF16)
            dkv_ref[(NB - 1) * BLK:NB * BLK, :] = carry_ref[...].astype(BF16)
            acc = acc_ref[...]
            dg2_ref[...] = jnp.zeros_like(dg2_ref)
            dg2_ref[0:1, :] = jnp.sum(acc * wdn_ref[...].astype(F32), axis=0, keepdims=True)
            acc_ref[...] = acc * g2_ref[...]
            cp = pltpu.make_async_copy(acc_ref, gdn_hbm, sem)
            cp.start()
            cp.wait()

    last = nt - 1
    tile = lambda w: pl.BlockSpec((NB * BLK, w), lambda t: (jnp.minimum(t, last), 0))
    return _pcall(
        body, [q, kv, kv, dya, bias, sinks, nw, act, dx2, gate2, w_dn], name="attn_bwd", grid=(nt + 1,),
        out_shape=[jax.ShapeDtypeStruct((s, QW), BF16), jax.ShapeDtypeStruct((s, 2 * KVW), BF16),
                   jax.ShapeDtypeStruct((NH, BLK, 2 * BLK), F32), jax.ShapeDtypeStruct((NH, 128), F32),
                   jax.ShapeDtypeStruct((8, QW), F32), jax.ShapeDtypeStruct((DFF, D), F32),
                   jax.ShapeDtypeStruct((8, D), F32)],
        in_specs=[tile(QW),
                  pl.BlockSpec((BLK, 2 * KVW), lambda t: (jnp.clip(NB * t - 1, 0, NB * nt - 1), 0)),
                  tile(2 * KVW), tile(QW),
                  _row((NH, BLK, 2 * BLK)),
                  pl.BlockSpec(memory_space=pltpu.SMEM),
                  _row((1, QW)), tile(DFF), tile(D), _row((1, D)), _row((DFF, D))],
        out_specs=[tile(QW),
                   pl.BlockSpec((NB * BLK, 2 * KVW), lambda t: (jnp.maximum(t - 1, 0), 0)),
                   _row((NH, BLK, 2 * BLK)), _row((NH, 128)), _row((8, QW)),
                   pl.BlockSpec(memory_space=pl.ANY), _row((8, D))],
        scratch_shapes=[pltpu.VMEM((BLK, 2 * KVW), F32), pltpu.VMEM(((NB - 1) * BLK, 2 * KVW), F32),
                        pltpu.VMEM((DFF, D), F32), pltpu.SemaphoreType.DMA], carry=carry)


def _rel_bias_grad(dbias, bucket):
    def body(db_ref, bk_ref, o_ref):
        bk = bk_ref[...]
        lane = lax.broadcasted_iota(jnp.int32, (1, 128), 1)
        for b in range(NBUCKET):
            sel = bk == b
            row = jnp.zeros((1, 128), F32)
            for h in range(NH):
                row = row + jnp.where(lane == h, jnp.sum(jnp.where(sel, db_ref[h], 0.0)), 0.0)
            o_ref[b:b + 1, :] = row

    return pl.pallas_call(
        body, name="rel_bias_grad",
        out_shape=jax.ShapeDtypeStruct((NBUCKET, 128), F32),
    )(dbias, bucket)


def _ssd_consts():
    head_of_lane = np.arange(SW) // HD
    expand = (np.arange(128)[:, None] == head_of_lane[None, :]).astype(np.float32)
    tril = np.tril(np.ones((BLK, BLK), np.float32))
    return (jnp.asarray(expand, BF16), jnp.asarray(expand.T.copy(), BF16), jnp.asarray(tril, BF16),
            jnp.asarray(tril.T.copy(), BF16))


def _conv_pre(xc, halo, cw, cb):
    ext = jnp.concatenate([halo, xc], axis=0)
    taps = [xc if k == CK - 1 else pltpu.roll(ext, CK - 1 - k, 0)[8:8 + BLK] for k in range(CK)]
    return cb + sum(cw[k:k + 1, :] * taps[k] for k in range(CK))


def _ssd_chunk(pre, dtr, dtb, av, dkv, ex, tril, h_in):
    sp = _sig(pre)
    xbc = pre * sp
    xs, bm, cm = xbc[:, 0:SW], xbc[:, SW:SW + 2 * NST], xbc[:, SW + 2 * NST:]
    dtin = dtr + dtb
    dt = jnp.maximum(dtin, 0.0) + jnp.log1p(jnp.exp(-jnp.abs(dtin)))
    cs = _sel_l(tril, dt * av)
    cst = cs.T
    dtx = _sel_r(dt, ex)
    csx = _sel_r(cs, ex)
    xdt = xs * dtx
    csl = csx[BLK - 1:BLK, :]
    decx = jnp.exp(csl - csx)
    ecsx = jnp.exp(csx)
    ecl = jnp.exp(csl)
    causal = tril.astype(F32) > 0.5
    ydiag, yoff, cbs, lms = [], [], [], []
    for g in range(2):
        bg = bm[:, g * NST:(g + 1) * NST].astype(BF16)
        cg = cm[:, g * NST:(g + 1) * NST].astype(BF16)
        cb = _mm_nt(cg, bg)
        cbs.append(cb)
        yoff.append(_mm(cg, h_in[:, g * 256:(g + 1) * 256].astype(BF16)))
        for r in range(4):
            h = 4 * g + r
            seg = cs[:, h:h + 1] - cst[h:h + 1, :]
            lm = jnp.where(causal, jnp.exp(jnp.minimum(seg, 0.0)), 0.0)
            lms.append(lm)
            ydiag.append(_mm((cb * lm).astype(BF16), xdt[:, h * HD:(h + 1) * HD].astype(BF16)))
    yoff = jnp.concatenate(yoff, axis=1) * ecsx
    y = jnp.concatenate(ydiag, axis=1) + yoff + dkv * xs
    return dict(pre=pre, sp=sp, xs=xs, bm=bm, cm=cm, dtin=dtin, dt=dt, av=av, cs=cs, cst=cst,
                dtx=dtx, csx=csx, xdt=xdt, decx=decx, ecsx=ecsx, ecl=ecl, causal=causal, cbs=cbs, lms=lms,
                yoff=yoff, y=y)


def _group_mean(t):
    m0 = jnp.mean(t[:, 0:256], axis=-1, keepdims=True)
    m1 = jnp.mean(t[:, 256:512], axis=-1, keepdims=True)
    return jnp.concatenate([jnp.broadcast_to(m0, (t.shape[0], 256)), jnp.broadcast_to(m1, (t.shape[0], 256))], axis=1)


SUBS = 4


def _ssd_fwd(z, xbc, dtr, cw, cb, dtb, av, dk, nw, carry=None):
    s = z.shape[0]
    nc = s // BLK
    tile = SUBS * BLK
    ex, _, tril, _ = _ssd_consts()

    def body(z_ref, xc_ref, xh_ref, dtr_ref, cw_ref, cb_ref, dtb_ref, a_ref, dk_ref, nw_ref, ex_ref, tril_ref,
             y_ref, hs_ref, pre_ref, h_ref):
        t = pl.program_id(0)

        @pl.when(t == 0)
        def _():
            h_ref[...] = jnp.zeros_like(h_ref)

        h_in = h_ref[...]
        for sub in range(SUBS):
            rows = slice(sub * BLK, (sub + 1) * BLK)
            xc = xc_ref[rows, :]
            halo = jnp.where(t == 0, 0.0, xh_ref[...]) if sub == 0 else xc_ref[sub * BLK - 8:sub * BLK, :]
            pre = _conv_pre(xc, halo, cw_ref[...], cb_ref[...])
            pre_ref[rows, :] = pre
            hs_ref[sub] = h_in
            f = _ssd_chunk(pre, dtr_ref[rows, :], dtb_ref[...], a_ref[...], dk_ref[...], ex_ref[...], tril_ref[...], h_in)
            dx = (f["decx"] * f["xdt"]).astype(BF16)
            st = [_mm_tn(f["bm"][:, g * NST:(g + 1) * NST].astype(BF16), dx[:, g * 256:(g + 1) * 256]) for g in range(2)]
            h_in = h_in * f["ecl"] + jnp.concatenate(st, axis=1)
            zv = z_ref[rows, :]
            tg = f["y"] * (zv * _sig(zv))
            r = lax.rsqrt(_group_mean(tg * tg) + EPS)
            y_ref[rows, :] = (tg * r * nw_ref[...]).astype(BF16)
        h_ref[...] = h_in

    cur = lambda w: pl.BlockSpec((tile, w), lambda t: (t, 0))
    return _pcall(
        body, [z, xbc, xbc, dtr, cw, cb, dtb, av, dk, nw, ex, tril], name="ssd_fwd", grid=(s // tile,),
        out_shape=[jax.ShapeDtypeStruct((s, SW), BF16), jax.ShapeDtypeStruct((nc, NST, SW), F32),
                   jax.ShapeDtypeStruct((s, XBCW), F32)],
        in_specs=[cur(SW), cur(XBCW), pl.BlockSpec((8, XBCW), lambda t: (jnp.maximum(t * (tile // 8) - 1, 0), 0)),
                  cur(128), _row((8, XBCW)), _row((1, XBCW)), _row((1, 128)),
                  _row((1, 128)), _row((1, SW)), _row((1, SW)), _row((128, SW)), _row((BLK, BLK))],
        out_specs=[cur(SW), pl.BlockSpec((SUBS, NST, SW), lambda t: (t, 0, 0)), cur(XBCW)],
        scratch_shapes=[pltpu.VMEM((NST, SW), F32)], carry=carry)


def _ssd_bwd(z, xbc, pre_all, dtr, dys, hs, cw, dtb, av, dk, nw, carry=None):
    s = z.shape[0]
    tile = SUBS * BLK
    nt = s // tile
    ex, ext_t, tril, triu = _ssd_consts()

    def body(z_ref, xc_ref, pre_ref, dtr_ref, dy_ref, hs_ref, cw_ref, dtb_ref, a_ref, dk_ref, nw_ref,
             ex_ref, ext_ref, tril_ref, triu_ref,
             dz_ref, dxbc_ref, ddt_ref, dcw_ref, dcb_ref, dnw_ref, dhd_ref, dh_ref, nxt_ref, dd_ref):
        i = pl.program_id(0)

        @pl.when(i == 0)
        def _():
            dh_ref[...] = jnp.zeros_like(dh_ref)
            nxt_ref[...] = jnp.zeros_like(nxt_ref)
            dd_ref[...] = jnp.zeros_like(dd_ref)
            dcw_ref[...] = jnp.zeros_like(dcw_ref)
            dcb_ref[...] = jnp.zeros_like(dcb_ref)
            dnw_ref[...] = jnp.zeros_like(dnw_ref)
            dhd_ref[...] = jnp.zeros_like(dhd_ref)

        gst, nxt = dh_ref[...], nxt_ref[...]
        for sub in reversed(range(SUBS)):
            rows = slice(sub * BLK, (sub + 1) * BLK)
            gst, nxt = chunk(sub, rows, gst, nxt, z_ref, xc_ref, pre_ref, dtr_ref, dy_ref, hs_ref, cw_ref, dtb_ref,
                             a_ref, dk_ref, nw_ref, ex_ref, ext_ref, tril_ref, triu_ref,
                             dz_ref, dxbc_ref, ddt_ref, dcw_ref, dcb_ref, dnw_ref, dhd_ref, dd_ref)
        dh_ref[...] = gst
        nxt_ref[...] = nxt

        @pl.when(i == nt - 1)
        def _():
            dhd_ref[2:3, :] = _sel_r(dd_ref[...], ext_ref[...])[0:1, :]

    def chunk(sub, rows, gst, nxt, z_ref, xc_ref, pre_ref, dtr_ref, dy_ref, hs_ref, cw_ref, dtb_ref,
              a_ref, dk_ref, nw_ref, ex_ref, ext_ref, tril_ref, triu_ref,
              dz_ref, dxbc_ref, ddt_ref, dcw_ref, dcb_ref, dnw_ref, dhd_ref, dd_ref):
        h_in = hs_ref[sub]
        f = _ssd_chunk(pre_ref[rows, :], dtr_ref[rows, :], dtb_ref[...], a_ref[...], dk_ref[...], ex_ref[...],
                       tril_ref[...], h_in)
        xs, xdt, decx, ecsx, ecl, dtx = f["xs"], f["xdt"], f["decx"], f["ecsx"], f["ecl"], f["dtx"]
        cs, cst, causal = f["cs"], f["cst"], f["causal"]
        causal_t = triu_ref[...].astype(F32) > 0.5

        zv = z_ref[rows, :]
        sz = _sig(zv)
        gz = zv * sz
        t = f["y"] * gz
        r = lax.rsqrt(_group_mean(t * t) + EPS)
        tn_ = t * r
        dyn = dy_ref[rows, :]
        dnw_ref[0:1, :] += jnp.sum(dyn * tn_, axis=0, keepdims=True)
        u = dyn * nw_ref[...]
        dt_ = r * u - tn_ * (r * _group_mean(u * tn_))
        dy = dt_ * gz
        dz_ref[rows, :] = (dt_ * f["y"] * (sz * (1.0 + zv * (1.0 - sz)))).astype(BF16)

        dd_ref[0:1, :] += jnp.sum(dy * xs, axis=0, keepdims=True)
        dxs = dk_ref[...] * dy

        edy = ecsx * dy
        dxdt, dbs, dcs_, dcsx_parts, dh_new = [], [], [], [], []
        lane = lax.broadcasted_iota(jnp.int32, (1, 128), 1)
        dcs_intra = jnp.zeros((BLK, 128), F32)
        for g in range(2):
            sl = slice(g * 256, (g + 1) * 256)
            bgf, cgf = f["bm"][:, g * NST:(g + 1) * NST], f["cm"][:, g * NST:(g + 1) * NST]
            bg, cg = bgf.astype(BF16), cgf.astype(BF16)
            gg = gst[:, sl].astype(BF16)
            hg = h_in[:, sl].astype(BF16)
            edyg = edy[:, sl].astype(BF16)
            dc = _mm_nt(edyg, hg)
            dh_new.append(gst[:, sl] * ecl[:, sl] + _mm_tn(cg, edyg))
            bgm = _mm(bg, gg)
            dxdt_g = decx[:, sl] * bgm
            dxg = (decx[:, sl] * xdt[:, sl]).astype(BF16)
            db = _mm_nt(dxg, gg)
            qd = bgm * xdt[:, sl] * decx[:, sl]
            last = jnp.sum(qd, axis=0, keepdims=True) + ecl[:, sl] * jnp.sum(gst[:, sl] * h_in[:, sl], axis=0, keepdims=True)
            rowid = lax.broadcasted_iota(jnp.int32, (BLK, 256), 0)
            dcsx_parts.append(f["yoff"][:, sl] * dy[:, sl] - qd + jnp.where(rowid == BLK - 1, last, 0.0))
            cb_ = f["cbs"][g]
            cbt = _mm_nt(bg, cg)
            dcb_ = jnp.zeros((BLK, BLK), F32)
            dcbt = jnp.zeros((BLK, BLK), F32)
            dxd = []
            for r_ in range(4):
                h = 4 * g + r_
                hl = slice(h * HD, (h + 1) * HD)
                lm = f["lms"][h]
                segt = cst[h:h + 1, :] - cs[:, h:h + 1]
                lmt = jnp.where(causal_t, jnp.exp(jnp.minimum(segt, 0.0)), 0.0)
                dyh = dy[:, hl].astype(BF16)
                xdh = xdt[:, hl].astype(BF16)
                dw = _mm_nt(dyh, xdh)
                dwt = _mm_nt(xdh, dyh)
                wt = cbt * lmt
                dxd.append(_mm(wt.astype(BF16), dyh))
                dcb_ = dcb_ + dw * lm
                dcbt = dcbt + dwt * lmt
                col = jnp.sum(dw * (cb_ * lm), axis=-1, keepdims=True) - jnp.sum(dwt * wt, axis=-1, keepdims=True)
                dcs_intra = dcs_intra + jnp.where(lane == h, col, 0.0)
            dxdt.append(dxdt_g + jnp.concatenate(dxd, axis=1))
            dcs_.append(dc + _mm(dcb_.astype(BF16), bg))
            dbs.append(db + _mm(dcbt.astype(BF16), cg))
        dxdt = jnp.concatenate(dxdt, axis=1)
        dxs = dxs + dxdt * dtx
        ext_t_ = ext_ref[...]
        dcs = dcs_intra + _sel_r(jnp.concatenate(dcsx_parts, axis=1), ext_t_)
        da = _sel_l(triu_ref[...], dcs)
        ddt = da * f["av"] + _sel_r(dxdt * xs, ext_t_)
        dhd_ref[1:2, :] += jnp.sum(da * f["dt"], axis=0, keepdims=True)
        ddtr = ddt * _sig(f["dtin"])
        dhd_ref[0:1, :] += jnp.sum(ddtr, axis=0, keepdims=True)
        ddt_ref[rows, :] = ddtr.astype(BF16)

        sp, pre = f["sp"], f["pre"]
        dact = jnp.concatenate([dxs] + dbs + dcs_, axis=1)
        dpre = dact * (sp * (1.0 + pre * (1.0 - sp)))
        dcb_ref[0:1, :] += jnp.sum(dpre, axis=0, keepdims=True)
        ext2 = jnp.concatenate([dpre, nxt], axis=0)
        shifted = [pltpu.roll(ext2, BLK + 8 - (CK - 1 - k), 0)[0:BLK] for k in range(CK - 1)] + [dpre]
        cw = cw_ref[...]
        xc = xc_ref[rows, :]
        dxr = cw[CK - 1:CK, :] * dpre
        for k in range(CK):
            dcw_ref[k:k + 1, :] += jnp.sum(shifted[k] * xc, axis=0, keepdims=True)
            if k < CK - 1:
                dxr = dxr + cw[k:k + 1, :] * shifted[k]
        dxbc_ref[rows, :] = dxr.astype(BF16)
        return jnp.concatenate(dh_new, axis=1), dpre[0:8]

    cur = lambda w: pl.BlockSpec((tile, w), lambda i: (nt - 1 - i, 0))
    return _pcall(
        body, [z, xbc, pre_all, dtr, dys, hs, cw, dtb, av, dk, nw, ex, ext_t, tril, triu], name="ssd_bwd", grid=(nt,),
        out_shape=[jax.ShapeDtypeStruct((s, SW), BF16), jax.ShapeDtypeStruct((s, XBCW), BF16),
                   jax.ShapeDtypeStruct((s, 128), BF16), jax.ShapeDtypeStruct((8, XBCW), F32),
                   jax.ShapeDtypeStruct((8, XBCW), F32), jax.ShapeDtypeStruct((8, SW), F32),
                   jax.ShapeDtypeStruct((8, 128), F32)],
        in_specs=[cur(SW), cur(XBCW), cur(XBCW), cur(128), cur(SW),
                  pl.BlockSpec((SUBS, NST, SW), lambda i: (nt - 1 - i, 0, 0)),
                  _row((8, XBCW)), _row((1, 128)), _row((1, 128)), _row((1, SW)), _row((1, SW)),
                  _row((128, SW)), _row((SW, 128)), _row((BLK, BLK)), _row((BLK, BLK))],
        out_specs=[cur(SW), cur(XBCW), cur(128), _row((8, XBCW)), _row((8, XBCW)), _row((8, SW)), _row((8, 128))],
        scratch_shapes=[pltpu.VMEM((NST, SW), F32), pltpu.VMEM((8, XBCW), F32), pltpu.VMEM((8, SW), F32)], carry=carry)


def _load_once(i, pairs, sem):
    @pl.when(i == 0)
    def _():
        cps = [pltpu.make_async_copy(src, dst, sem.at[k]) for k, (src, dst) in enumerate(pairs)]
        for cp in cps:
            cp.start()
        for cp in cps:
            cp.wait()


def _mlp_fwd(x, ya, ys, tgt, w_o, w_ga, w_gb, w_dn, gate1, a2, sh2, gate2, fn):
    s = x.shape[0]
    sub_m, subs = 256, 2
    tm = sub_m * subs

    def body(x_ref, ya_ref, ys_ref, t_ref, wo_hbm, wga_hbm, wgb_hbm, wdn_hbm, g1_ref, a2_ref, s2_ref, g2_ref, fn_ref,
             x1_ref, gu_ref, dx2_ref, loss_ref, dfn_ref, wo, wga, wgb, wdn, sem):
        i = pl.program_id(0)
        _load_once(i, [(wo_hbm, wo), (wga_hbm, wga), (wgb_hbm, wgb), (wdn_hbm, wdn)], sem)

        @pl.when(i == 0)
        def _():
            loss_ref[...] = jnp.zeros_like(loss_ref)
            dfn_ref[...] = jnp.zeros_like(dfn_ref)

        def proj(st):
            st["mix"] = _mm(ya_ref[st["rows"], :], wo[0:QW, :]) + _mm(ys_ref[st["rows"], :], wo[QW:D, :])

        def norm(st):
            x1 = x_ref[st["rows"], :] + g1_ref[...] * st.pop("mix")
            x1_ref[st["rows"], :] = x1
            r2 = lax.rsqrt(jnp.mean(x1 * x1, axis=-1, keepdims=True) + EPS)
            st["x1"] = x1
            st["h2"] = (x1 * r2 * a2_ref[...] + s2_ref[...]).astype(BF16)

        def gate_up(st):
            h2 = st.pop("h2")
            ha, hb = h2[:, 0:D // 2], h2[:, D // 2:D]
            gub = jnp.concatenate([(_mm(ha, wga[j]) + _mm(hb, wgb[j])).astype(BF16) for j in range(4)], axis=1)
            gu_ref[st["rows"], :] = gub
            st["gub"] = gub

        def activate(st):
            gub = st.pop("gub")
            gv, uv = gub[:, 0:DFF].astype(F32), gub[:, DFF:].astype(F32)
            st["act"] = (gv * _sig(gv) * uv).astype(BF16)

        def down(st):
            st["ff"] = _mm(st.pop("act"), wdn[...])

        def head(st):
            x2 = st.pop("x1") + g2_ref[...] * st.pop("ff")
            r3 = lax.rsqrt(jnp.mean(x2 * x2, axis=-1, keepdims=True) + EPS)
            xn = x2 * r3
            fnv = fn_ref[...]
            err = xn * fnv - t_ref[st["rows"], :]
            st["loss"] = jnp.sum(err * err) * (0.5 / D)
            dy = err * (1.0 / D)
            st["dfn"] = jnp.sum(dy * xn, axis=0, keepdims=True)
            u = dy * fnv
            dx2_ref[st["rows"], :] = r3 * u - xn * (r3 * jnp.mean(u * xn, axis=-1, keepdims=True))

        a, b = [dict(rows=slice(k * sub_m, (k + 1) * sub_m)) for k in range(subs)]
        for stage, st in [(proj, a), (norm, a), (proj, b), (gate_up, a), (norm, b), (activate, a), (gate_up, b),
                          (down, a), (activate, b), (head, a), (down, b), (head, b)]:
            stage(st)
        loss_ref[...] += a["loss"] + b["loss"]
        dfn_ref[0:1, :] += a["dfn"] + b["dfn"]

    def tok(w):
        return pl.BlockSpec((tm, w), lambda i: (i, 0))

    hbm = pl.BlockSpec(memory_space=pl.ANY)
    return pl.pallas_call(
        body, name="mlp_fwd", grid=(s // tm,),
        out_shape=[jax.ShapeDtypeStruct((s, D), F32), jax.ShapeDtypeStruct((s, 2 * DFF), BF16),
                   jax.ShapeDtypeStruct((s, D), F32), jax.ShapeDtypeStruct((8, 128), F32),
                   jax.ShapeDtypeStruct((8, D), F32)],
        in_specs=[tok(D), tok(QW), tok(SW), tok(D), hbm, hbm, hbm, hbm,
                  _row((1, D)), _row((1, D)), _row((1, D)), _row((1, D)), _row((1, D))],
        out_specs=[tok(D), tok(2 * DFF), tok(D), _row((8, 128)), _row((8, D))],
        scratch_shapes=[pltpu.VMEM((D, D), BF16), pltpu.VMEM(w_ga.shape, BF16), pltpu.VMEM(w_gb.shape, BF16),
                        pltpu.VMEM((DFF, D), BF16), pltpu.SemaphoreType.DMA((4,))],
        compiler_params=_cp(("arbitrary",)),
    )(x, ya, ys, tgt, w_o, w_ga, w_gb, w_dn, gate1, a2, sh2, gate2, fn)


def _mlp_bwd(x1, gu, dx2, w_o, w_ga, w_gb, w_dn, gate1, a2, sh2, gate2):
    s = x1.shape[0]
    tm = 256
    nj = 2 * DFF // 4

    def body(x1_ref, gu_ref, dx2_ref, wo_hbm, wga_hbm, wgb_hbm, wdn_hbm, g1_ref, a2_ref, s2_ref, g2_ref,
             dx1_ref, dya_ref, dys_ref, act_ref, dgu_ref, h2_ref, dsh_ref, p_ref, wo, wga, wgb, wdn, sem):
        i = pl.program_id(0)
        _load_once(i, [(wo_hbm, wo), (wga_hbm, wga), (wgb_hbm, wgb), (wdn_hbm, wdn)], sem)

        @pl.when(i == 0)
        def _():
            dsh_ref[...] = jnp.zeros_like(dsh_ref)
            p_ref[...] = jnp.zeros_like(p_ref)

        dx2 = dx2_ref[...]
        dact = _mm_nt((dx2 * g2_ref[...]).astype(BF16), wdn[...])
        gub = gu_ref[...]
        gv, uv = gub[:, 0:DFF].astype(F32), gub[:, DFF:].astype(F32)
        sg = _sig(gv)
        sl = gv * sg
        act_ref[...] = (sl * uv).astype(BF16)
        dgu = jnp.concatenate([dact * uv * (sg * (1.0 + gv * (1.0 - sg))), dact * sl], axis=1).astype(BF16)
        dgu_ref[...] = dgu
        dha = sum(_mm_nt(dgu[:, j * nj:(j + 1) * nj], wga[j]) for j in range(4))
        dhb = sum(_mm_nt(dgu[:, j * nj:(j + 1) * nj], wgb[j]) for j in range(4))
        dh = jnp.concatenate([dha, dhb], axis=1)
        x1 = x1_ref[...]
        r2 = lax.rsqrt(jnp.mean(x1 * x1, axis=-1, keepdims=True) + EPS)
        xn = x1 * r2
        a2 = a2_ref[...]
        h2_ref[...] = (xn * a2 + s2_ref[...]).astype(BF16)
        dsh_ref[0:1, :] += jnp.sum(dh, axis=0, keepdims=True)
        p_ref[0:1, :] += jnp.sum(dh * xn, axis=0, keepdims=True)
        u = dh * a2
        dx1 = dx2 + r2 * u - xn * (r2 * jnp.mean(u * xn, axis=-1, keepdims=True))
        dx1_ref[...] = dx1
        dcat = _mm_nt((dx1 * g1_ref[...]).astype(BF16), wo[...])
        dya_ref[...] = dcat[:, 0:QW]
        dys_ref[...] = dcat[:, QW:D]

    def tok(w):
        return pl.BlockSpec((tm, w), lambda i: (i, 0))

    hbm = pl.BlockSpec(memory_space=pl.ANY)
    return pl.pallas_call(
        body, name="mlp_bwd", grid=(s // tm,),
        out_shape=[jax.ShapeDtypeStruct((s, D), F32), jax.ShapeDtypeStruct((s, QW), F32),
                   jax.ShapeDtypeStruct((s, SW), F32), jax.ShapeDtypeStruct((s, DFF), BF16),
                   jax.ShapeDtypeStruct((s, 2 * DFF), BF16), jax.ShapeDtypeStruct((s, D), BF16),
                   jax.ShapeDtypeStruct((8, D), F32), jax.ShapeDtypeStruct((8, D), F32)],
        in_specs=[tok(D), tok(2 * DFF), tok(D), hbm, hbm, hbm, hbm, _row((1, D)), _row((1, D)), _row((1, D)), _row((1, D))],
        out_specs=[tok(D), tok(QW), tok(SW), tok(DFF), tok(2 * DFF), tok(D), _row((8, D)), _row((8, D))],
        scratch_shapes=[pltpu.VMEM((D, D), BF16), pltpu.VMEM(w_ga.shape, BF16), pltpu.VMEM(w_gb.shape, BF16),
                        pltpu.VMEM((DFF, D), BF16), pltpu.SemaphoreType.DMA((4,))],
        compiler_params=_cp(("arbitrary",)),
    )(x1, gu, dx2, w_o, w_ga, w_gb, w_dn, gate1, a2, sh2, gate2)


def _wgrad(name, a, b, gate, w, carry=None):
    s, m = a.shape
    n = b.shape[1]
    tk = min(1024, s)
    nk = s // tk

    def body(a_ref, b_ref, g_ref, w_ref, o_hbm, dg_ref, acc_ref, sem):
        k = pl.program_id(0)

        @pl.when(k == 0)
        def _():
            acc_ref[...] = jnp.zeros_like(acc_ref)

        acc_ref[...] += _mm_tn(a_ref[...], b_ref[...].astype(BF16))

        @pl.when(k == nk - 1)
        def _():
            acc = acc_ref[...]
            dg_ref[...] = jnp.zeros_like(dg_ref)
            dg_ref[0:1, :] = jnp.sum(acc * w_ref[...].astype(F32), axis=0, keepdims=True)
            acc_ref[...] = acc * g_ref[...]
            cp = pltpu.make_async_copy(acc_ref, o_hbm, sem)
            cp.start()
            cp.wait()

    return _pcall(body, [a, b, gate, w], name=name, grid=(nk,),
                  out_shape=[jax.ShapeDtypeStruct((m, n), F32), jax.ShapeDtypeStruct((8, n), F32)],
                  in_specs=[pl.BlockSpec((tk, m), lambda k: (k, 0)), pl.BlockSpec((tk, n), lambda k: (k, 0)),
                            _row((1, n)), _row((m, n))],
                  out_specs=[pl.BlockSpec(memory_space=pl.ANY), _row((8, n))],
                  scratch_shapes=[pltpu.VMEM((m, n), F32), pltpu.SemaphoreType.DMA], carry=carry)


def _wgrad_gate_up(h2, dgu, carry=None):
    s = h2.shape[0]
    tk = min(1024, s)
    nk = s // tk
    n = dgu.shape[1]
    nj = n // 4

    def body(a_ref, b_ref, o_hbm, acc_ref, sems):
        k = pl.program_id(0)

        @pl.when(k == 0)
        def _():
            acc_ref[...] = jnp.zeros_like(acc_ref)

        acc_ref[...] += _mm_tn(a_ref[...], b_ref[...])

        @pl.when(k == nk - 1)
        def _():
            cps = [pltpu.make_async_copy(acc_ref.at[:, pl.ds(j * nj, nj)], o_hbm.at[j], sems.at[j]) for j in range(4)]
            for cp in cps:
                cp.start()
            for cp in cps:
                cp.wait()

    return _pcall(body, [h2, dgu], name="wgrad_gate_up", grid=(nk,),
                  out_shape=[jax.ShapeDtypeStruct((4, D, nj), F32)],
                  in_specs=[pl.BlockSpec((tk, D), lambda k: (k, 0)), pl.BlockSpec((tk, n), lambda k: (k, 0))],
                  out_specs=[pl.BlockSpec(memory_space=pl.ANY)],
                  scratch_shapes=[pltpu.VMEM((D, n), F32), pltpu.SemaphoreType.DMA((4,))], carry=carry)


def _wgrad_in_t(h1, pieces, carry=None):
    s = h1.shape[0]
    tk = min(1024, s)
    nk = s // tk

    def body(a_ref, dq_ref, dkv_ref, dz_ref, dxbc_ref, ddt_ref, o_hbm, acc_ref, tr_ref, sem):
        k = pl.program_id(0)

        @pl.when(k == 0)
        def _():
            acc_ref[...] = jnp.zeros_like(acc_ref)

        dproj = jnp.concatenate([dq_ref[...], dkv_ref[...], dz_ref[...], dxbc_ref[...], ddt_ref[...]], axis=1)
        acc_ref[...] += _mm_tn(a_ref[...], dproj)

        @pl.when(k == nk - 1)
        def _():
            for j in range(PROJ_W // 128):
                tr_ref[j * 128:(j + 1) * 128, :] = acc_ref[:, j * 128:(j + 1) * 128].T
            cp = pltpu.make_async_copy(tr_ref, o_hbm, sem)
            cp.start()
            cp.wait()

    return _pcall(body, [h1] + list(pieces), name="wgrad_in", grid=(nk,),
                  out_shape=[jax.ShapeDtypeStruct((PROJ_W, D), F32)],
                  in_specs=[pl.BlockSpec((tk, p.shape[1]), lambda k: (k, 0)) for p in [h1] + list(pieces)],
                  out_specs=[pl.BlockSpec(memory_space=pl.ANY)],
                  scratch_shapes=[pltpu.VMEM((D, PROJ_W), F32), pltpu.VMEM((PROJ_W, D), F32), pltpu.SemaphoreType.DMA],
                  carry=carry)


_SMALL = ["ada_b", "norm1", "conv_w", "conv_b", "dt_bias", "A_log", "D_skip", "sinks", "attn_out_norm",
          "ssm_out_norm", "norm2", "rel_bias", "final_norm"]


def _small_grad(name, gs, chip):
    if name == "ada_b":
        return jnp.concatenate([gs[j:j + 1, :] for j in range(6)], axis=1)
    if name == "conv_w":
        full = gs[7:11, :]
        out = full[:, 0:256]
        for j in range(1, 4):
            out = jnp.where(chip == j, full[:, j * 256:(j + 1) * 256], out)
        return out
    row, width = {"norm1": (6, D), "conv_b": (11, D), "norm2": (12, D), "final_norm": (13, D),
                  "attn_out_norm": (14, QW), "ssm_out_norm": (15, SW), "dt_bias": (16, NH), "A_log": (17, NH),
                  "D_skip": (18, NH), "sinks": (19, NH), "rel_bias": (24, NH)}[name]
    rows = NBUCKET if name == "rel_bias" else 1
    return gs[row:row + rows, 0:width]


def _small_update(small_all, where, ws, ms, vs):
    n = len(_SMALL)

    def body(where_ref, sa_ref, *refs):
        w_refs, m_refs, v_refs, outs = refs[:n], refs[n:2 * n], refs[2 * n:3 * n], refs[3 * n:]
        gs = sa_ref[0]
        for b in range(1, 8):
            gs = gs + sa_ref[b]
        chip = where_ref[1]
        for i, name in enumerate(_SMALL):
            g = _small_grad(name, gs, chip)
            lead = (0,) if name == "conv_w" else ()
            d, mo, vo = _adamw(w_refs[i][lead + (...,)], g, m_refs[i][lead + (...,)], v_refs[i][lead + (...,)])
            for k, val in enumerate((g, d, mo, vo)):
                outs[k * n + i][lead + (...,)] = val
        outs[4 * n][...] = gs[20:21, 0:128]

    shapes = [jax.ShapeDtypeStruct(w.shape, F32) for w in ws]
    vmem = pl.BlockSpec(memory_space=pltpu.VMEM)
    res = pl.pallas_call(
        body, name="small_update", out_shape=shapes * 4 + [jax.ShapeDtypeStruct((1, 128), F32)],
        in_specs=[pl.BlockSpec(memory_space=pltpu.SMEM)] + [vmem] * (1 + 3 * n), out_specs=[vmem] * (4 * n + 1),
    )(where, small_all, *ws, *ms, *vs)
    return [res[k * n:(k + 1) * n] for k in range(4)], res[4 * n][0, 0]


def _add_half(name, g, got, where, by_cols=False):
    rr, cc = got.shape[1:]
    if by_cols:
        mine = pl.BlockSpec((None, rr, cc), lambda i, w_ref: (i, 0, w_ref[0]))
    else:
        mine = pl.BlockSpec((None, None, rr, cc), lambda i, w_ref: (i, w_ref[0], 0, 0))

    def body(w_ref, g_ref, r_ref, o_ref, own_ref):
        s = g_ref[...] + r_ref[...]
        o_ref[...] = s.astype(BF16)

        @pl.when(pl.program_id(0) == w_ref[1])
        def _():
            own_ref[...] = s

    spec = pl.BlockSpec((None, rr, cc), lambda i, w_ref: (i, 0, 0))
    return _pcall(body, [where, g, got], name=name, grid=(4,), nprefetch=1,
                  out_shape=[jax.ShapeDtypeStruct(got.shape, BF16), jax.ShapeDtypeStruct((rr, cc), F32)],
                  in_specs=[mine, spec],
                  out_specs=[spec, pl.BlockSpec((rr, cc), lambda i, w_ref: (0, 0))])


def _add_chips(name, own, got):
    rr, cc = own.shape
    tr = rr // 2 if rr % 32 == 0 else rr

    def body(s_ref, r_ref, o_ref):
        o_ref[...] = ((s_ref[...] + r_ref[0].astype(F32)) + r_ref[1].astype(F32)) + r_ref[2].astype(F32)

    spec = pl.BlockSpec((tr, cc), lambda i: (i, 0))
    return _pcall(body, [own, got], name=name, grid=(rr // tr,), out_shape=[jax.ShapeDtypeStruct((rr, cc), F32)],
                  in_specs=[spec, pl.BlockSpec((3, tr, cc), lambda i: (0, i, 0))], out_specs=[spec])[0]


def _adamw_halves(name, mine, got, w, m, v, where, by_cols=False):
    rr, cc = mine.shape

    def body(w_ref_, t_ref, r_ref, w_ref, m_ref, v_ref, g_ref, d_ref, mo_ref, vo_ref):
        g = jnp.where(pl.program_id(0) == w_ref_[0], t_ref[...], r_ref[...])
        g_ref[...] = g
        d_ref[...], mo_ref[...], vo_ref[...] = _adamw(w_ref[...], g, m_ref[...], v_ref[...])

    if by_cols:
        grid = (2, 1)
        half = pl.BlockSpec((rr, cc), lambda h, i, w_ref_: (0, 0))
        full = pl.BlockSpec((rr, cc), lambda h, i, w_ref_: (0, h))
    else:
        tr = rr // 2
        grid = (2, 2)
        half = pl.BlockSpec((tr, cc), lambda h, i, w_ref_: (i, 0))
        full = pl.BlockSpec((None, tr, cc), lambda h, i, w_ref_: (0, 2 * h + i, 0))
    return _pcall(body, [where, mine, got, w, m, v], name=name, grid=grid, nprefetch=1,
                  out_shape=[jax.ShapeDtypeStruct(w.shape, F32)] * 4,
                  in_specs=[half, half, full, full, full], out_specs=[full] * 4)


def _bias_table(rel_bias, bucket, mask):
    def body(rb_ref, bk_ref, mk_ref, o_ref):
        bk = bk_ref[...]
        valid = mk_ref[...] > 0
        for h in range(NH):
            acc = jnp.zeros((BLK, 2 * BLK), F32)
            for b in range(NBUCKET):
                acc = jnp.where(bk == b, rb_ref[b, h], acc)
            o_ref[h] = jnp.where(valid, acc, NEG)

    vmem = pl.BlockSpec(memory_space=pltpu.VMEM)
    return pl.pallas_call(
        body, name="bias_table", out_shape=jax.ShapeDtypeStruct((NH, BLK, 2 * BLK), F32),
        in_specs=[pl.BlockSpec(memory_space=pltpu.SMEM), vmem, vmem], out_specs=vmem,
    )(rel_bias, bucket, mask)


def _pack_small(dsh1, p1, dsh2, p2, dg1a, dg1b, dg2, norm1, norm2, scale1, scale2, dcw, dcb, dfn,
                dnw_attn, dnw_ssm, dhd, av, dsink, drel, loss_acc):
    def body(dsh1_ref, p1_ref, dsh2_ref, p2_ref, dg1a_ref, dg1b_ref, dg2_ref, n1_ref, n2_ref, s1_ref, s2_ref,
             dcw_ref, dcb_ref, dfn_ref, da_ref, ds_ref, dhd_ref, av_ref, dsink_ref, drel_ref, loss_ref, o_ref):
        o_ref[...] = jnp.zeros_like(o_ref)
        p1v, p2v = p1_ref[0:1, :], p2_ref[0:1, :]
        o_ref[0:1, :] = dsh1_ref[0:1, :]
        o_ref[1:2, :] = p1v * n1_ref[...]
        o_ref[2:3, :] = dg1a_ref[0:1, :] + dg1b_ref[0:1, :]
        o_ref[3:4, :] = dsh2_ref[0:1, :]
        o_ref[4:5, :] = p2v * n2_ref[...]
        o_ref[5:6, :] = dg2_ref[0:1, :]
        o_ref[6:7, :] = p1v * (1.0 + s1_ref[...])
        o_ref[7:11, :] = dcw_ref[0:4, :]
        o_ref[11:12, :] = dcb_ref[0:1, :]
        o_ref[12:13, :] = p2v * (1.0 + s2_ref[...])
        o_ref[13:14, :] = dfn_ref[0:1, :]
        o_ref[14:15, 0:QW] = da_ref[0:1, :]
        o_ref[15:16, 0:SW] = ds_ref[0:1, :]
        o_ref[16:17, 0:128] = dhd_ref[0:1, :]
        o_ref[17:18, 0:128] = dhd_ref[1:2, :] * av_ref[...]
        o_ref[18:19, 0:128] = dhd_ref[2:3, :]
        o_ref[19:20, 0:128] = dsink_ref[0:1, :]
        o_ref[20:21, 0:128] = loss_ref[0:1, :]
        o_ref[24:56, 0:128] = drel_ref[...]

    return pl.pallas_call(body, name="pack_small", out_shape=jax.ShapeDtypeStruct((56, D), F32))(
        dsh1, p1, dsh2, p2, dg1a, dg1b, dg2, norm1, norm2, scale1, scale2, dcw, dcb, dfn,
        dnw_attn, dnw_ssm, dhd, av, dsink, drel, loss_acc)


def _pad_row(a, rows=1):
    return jnp.pad(a.reshape(rows, -1), ((0, 0), (0, D - a.size // rows)))


def kernel(x, c, ada_w, ada_b, norm1, w_in, conv_w, conv_b, dt_bias, A_log, D_skip, sinks, attn_out_norm, ssm_out_norm, w_o, norm2, w_gate_up, w_down, rel_bias, final_norm, loss_target, m_ada_w, m_ada_b, m_norm1, m_w_in, m_conv_w, m_conv_b, m_dt_bias, m_A_log, m_D_skip, m_sinks, m_attn_out_norm, m_ssm_out_norm, m_w_o, m_norm2, m_w_gate_up, m_w_down, m_rel_bias, m_final_norm, v_ada_w, v_ada_b, v_norm1, v_w_in, v_conv_w, v_conv_b, v_dt_bias, v_A_log, v_D_skip, v_sinks, v_attn_out_norm, v_ssm_out_norm, v_w_o, v_norm2, v_w_gate_up, v_w_down, v_rel_bias, v_final_norm):
    xi, yi, ci = lax.axis_index("x"), lax.axis_index("y"), lax.axis_index("c")
    chip = 2 * xi + yi
    me = 4 * xi + 2 * yi + ci
    where = jnp.stack([ci, chip]).astype(jnp.int32)
    xs2, tgt = x[0], loss_target[0]

    first = jnp.concatenate([c, _pad_row(conv_w[0], CK), jnp.zeros((3, D), F32)], axis=0)
    w_in_t, m_w_in_t, v_w_in_t = w_in[0].T, m_w_in[0].T, v_w_in[0].T
    w_in_b, w_o_b, w_dn_b = w_in_t.astype(BF16), w_o[0].astype(BF16), w_down[0].astype(BF16)
    w_gu_b = w_gate_up[0].astype(BF16)
    hw = D // 2
    fetch_half = _Carry(
        [w_in_b], [jax.ShapeDtypeStruct((4,) + w_in_b.shape, BF16)],
        lambda x_, y_, c_: [(None, 0, slice(None), 0, 2 * x_ + y_)] + [
            (f, 0, (slice(None), pl.ds(c_ * hw, hw)), 0, (2 * x_ + y_, slice(None), pl.ds(c_ * hw, hw))) for f in _CHIPS3])

    def swap_halves(x_, y_, c_):
        there = [(jnp.bitwise_xor(2 * x_ + y_, k + 1), slice(None), pl.ds(c_ * hw, hw)) for k in range(3)]
        return [(_SIBLING, 1, at, 1, at) for at in there]

    first_all, w_in_g = _exchange_two("gather_first", _merge(_gather8_carry(first), fetch_half), swap_halves)
    c_all = first_all[:, 0, :]
    cw_full = jnp.concatenate([first_all[2 * j, 1:1 + CK, 0:256] for j in range(4)], axis=1)
    w_in_f = jnp.pad(w_in_g.reshape(IN_W, D), ((0, PROJ_W - IN_W), (0, 0)))

    ncol = ada_w.shape[2]
    mod_cols = _ada_fwd(c_all, ada_w[0], lax.dynamic_slice(ada_b, (0, chip * ncol), (1, ncol)))
    mod_all = _exchange("gather_mod", _gather_chips_carry([mod_cols]))[0]
    mod = lax.dynamic_slice(jnp.transpose(mod_all, (1, 0, 2)).reshape(8, 4 * ncol), (me, 0), (1, 4 * ncol))
    shift1, scale1, gate1, shift2, scale2, gate2 = [mod[:, j * D:(j + 1) * D] for j in range(6)]
    a1 = norm1 * (1.0 + scale1)
    a2 = norm2 * (1.0 + scale2)

    hdn = DFF // 8
    q, kv, z, xbc, dtr, w_o_g, w_dna_g = _in_proj_fwd(xs2, a1, shift1, w_in_f,
                                                      carry=_gather_chips_carry([w_o_b, w_dn_b[0:hdn]]))
    w_o_f = w_o_g.reshape(D, D)
    bucket, mask = _attn_geometry()
    bucket = jnp.asarray(bucket)
    bias = _bias_table(rel_bias, bucket, jnp.asarray(mask.astype(np.int32)))
    sinks1 = sinks[0]
    ya, w_ga_g = _attn_fwd(q, kv, bias, sinks1, attn_out_norm, carry=_gather_chips_carry([w_gu_b[0:D // 2]]))
    cw8 = jnp.concatenate([cw_full, jnp.zeros((4, XBCW), F32)], axis=0)
    dtb = _pad_row(dt_bias)[:, 0:128]
    av = _pad_row(-jnp.exp(A_log))[:, 0:128]
    dk = jnp.repeat(D_skip, HD, axis=1)
    ys, hs, pre, w_gb_g, w_dnb_g = _ssd_fwd(z, xbc, dtr, cw8, conv_b, dtb, av, dk, ssm_out_norm,
                                            carry=_gather_chips_carry([w_gu_b[D // 2:D], w_dn_b[hdn:2 * hdn]]))
    w_dn_f = jnp.stack([w_dna_g, w_dnb_g], axis=1).reshape(DFF, D)
    fn = final_norm[None, :]
    x1, gu, dx2, loss_acc, dfn = _mlp_fwd(xs2, ya, ys, tgt, w_o_f, w_ga_g, w_gb_g, w_dn_f, gate1, a2, shift2, gate2, fn)

    def to_sibling(p):
        return _Carry([p], [jax.ShapeDtypeStruct((4,) + p.shape[2:], F32)],
                      lambda x_, y_, c_: [(_SIBLING, 0, (j, 1 - c_), 0, j) for j in range(4)])

    def to_chips(s4):
        return _Carry([s4], [jax.ShapeDtypeStruct((3,) + s4.shape[1:], s4.dtype)],
                      lambda x_, y_, c_: [(f, 0, jnp.bitwise_xor(2 * x_ + y_, k + 1), 0, k) for k, f in enumerate(_CHIPS3)])

    def back(t):
        return _Carry([t[None]], [jax.ShapeDtypeStruct((1,) + t.shape, F32)], lambda x_, y_, c_: [(_SIBLING, 0, 0, 0, 0)])

    dx1, dya, dys, act, dgu, h2, dsh2, p2 = _mlp_bwd(x1, gu, dx2, w_o_f, w_ga_g, w_gb_g, w_dn_f, gate1, a2, shift2, gate2)
    p_gu = _wgrad_gate_up(h2, dgu)[0].reshape(4, 2, D // 2, 2 * DFF // 4)
    dq, dkv, dbias, dsink, dnw_attn, g_dn, dg2, got1_gu = _attn_bwd(
        q, kv, dya, bias, sinks1, attn_out_norm, act, dx2, gate2, w_dn_f, carry=to_sibling(p_gu))
    p_dn = g_dn.reshape(4, 2, DFF // 8, D)
    drel = _rel_bias_grad(dbias, bucket)
    s4_gu, own_gu = _add_half("rs_add_half_gu", p_gu, got1_gu, where)
    dz, dxbc, ddt, dcw, dcb, dnw_ssm, dhd, got2_gu, got1_dn = _ssd_bwd(
        z, xbc, pre, dtr, dys, hs, cw8, dtb, av, dk, ssm_out_norm, carry=_merge(to_chips(s4_gu), to_sibling(p_dn)))
    mine_gu = _add_chips("rs_add_chips_gu", own_gu, got2_gu)
    s4_dn, own_dn = _add_half("rs_add_half_dn", p_dn, got1_dn, where)
    grad_x, g_in_t, dsh1, p1 = _in_proj_bwd(xs2, dx1, a1, shift1, w_in_f, dq, dkv, dz, dxbc, ddt)
    p_in = g_in_t[0:IN_W].reshape(4, IN_W // 4, D)

    def to_sibling_cols(p):
        return _Carry([p], [jax.ShapeDtypeStruct(p.shape[:2] + (D // 2,), F32)],
                      lambda x_, y_, c_: [(_SIBLING, 0, (j, slice(None), pl.ds((1 - c_) * (D // 2), D // 2)), 0, j)
                                          for j in range(4)])

    g_oa, dg1a, got1_in, got2_dn, got3_gu = _wgrad(
        "wgrad_o_attn", ya, dx1, gate1, w_o_f[0:QW],
        carry=_merge(to_sibling_cols(p_in), to_chips(s4_dn), back(mine_gu)))
    mine_dn = _add_chips("rs_add_chips_dn", own_dn, got2_dn)
    s4_in, own_in = _add_half("rs_add_half_in", p_in, got1_in, where, by_cols=True)
    g_os, dg1b, got2_in, got3_dn = _wgrad("wgrad_o_ssm", ys, dx1, gate1, w_o_f[QW:D],
                                          carry=_merge(to_chips(s4_in), back(mine_dn)))
    mine_in = _add_chips("rs_add_chips_in", own_in, got2_in)
    p_o = jnp.concatenate([g_oa, g_os], axis=0).reshape(4, 2, D // 8, D)

    small = _pack_small(dsh1, p1, dsh2, p2, dg1a, dg1b, dg2, norm1, norm2, scale1, scale2, dcw, dcb, dfn,
                        dnw_attn, dnw_ssm, dhd, av, dsink, drel, loss_acc)
    small_all, got1_o, got3_in = _exchange(
        "gather_small", _merge(_gather8_carry(small), to_sibling(p_o), back(mine_in)))
    s4_o, own_o = _add_half("rs_add_half_o", p_o, got1_o, where)
    mine_o = _add_chips("rs_add_chips_o", own_o, _exchange("rs_chips_o", to_chips(s4_o))[0])
    got3_o = _exchange("rs_back_o", back(mine_o))[0]
    small_res, loss = _small_update(
        small_all, where,
        [ada_b, norm1, conv_w, conv_b, dt_bias, A_log, D_skip, sinks, attn_out_norm, ssm_out_norm, norm2, rel_bias,
         final_norm[None, :]],
        [m_ada_b, m_norm1, m_conv_w, m_conv_b, m_dt_bias, m_A_log, m_D_skip, m_sinks, m_attn_out_norm,
         m_ssm_out_norm, m_norm2, m_rel_bias, m_final_norm[None, :]],
        [v_ada_b, v_norm1, v_conv_w, v_conv_b, v_dt_bias, v_A_log, v_D_skip, v_sinks, v_attn_out_norm,
         v_ssm_out_norm, v_norm2, v_rel_bias, v_final_norm[None, :]])
    small_out = [dict(zip(_SMALL, r)) for r in small_res]
    for r in small_out:
        r["final_norm"] = r["final_norm"][0]

    dmod_all = small_all[:, 0:6, :].reshape(8, 6 * D)
    dmod_loc = lax.dynamic_slice(dmod_all, (0, chip * ncol), (8, ncol))
    ada_out = _ada_bwd_adamw(c_all.T, dmod_loc, ada_w[0], m_ada_w[0], v_ada_w[0])

    big_gu = _adamw_halves("adamw_gate_up", mine_gu, got3_gu[0], w_gate_up, m_w_gate_up, v_w_gate_up, where)
    big_dn = _adamw_halves("adamw_down", mine_dn, got3_dn[0], w_down, m_w_down, v_w_down, where)
    big_o = _adamw_halves("adamw_o", mine_o, got3_o[0], w_o, m_w_o, v_w_o, where)
    big_in = [o.T[None] for o in _adamw_halves("adamw_in", mine_in, got3_in[0], w_in_t, m_w_in_t, v_w_in_t, where,
                                               by_cols=True)]
    big = [big_in, big_o, big_gu, big_dn]

    order = ["ada_w", "ada_b", "norm1", "w_in", "conv_w", "conv_b", "dt_bias", "A_log", "D_skip", "sinks",
             "attn_out_norm", "ssm_out_norm", "w_o", "norm2", "w_gate_up", "w_down", "rel_bias", "final_norm"]
    bigname = {"w_in": 0, "w_o": 1, "w_gate_up": 2, "w_down": 3}
    res = [loss, grad_x[None]]
    for kind in range(4):
        for nm in order:
            if nm == "ada_w":
                res.append(ada_out[kind][None])
            elif nm in bigname:
                res.append(big[bigname[nm]][kind])
            else:
                res.append(small_out[kind][nm])
    return tuple(res)
```

```python
import numpy as np
import jax
import jax.numpy as jnp
from jax import lax
from jax.experimental import pallas as pl
from jax.experimental.pallas import tpu as pltpu

F32, BF16 = jnp.float32, jnp.bfloat16
HI = lax.Precision.HIGHEST

D = 1024
QW, KVW = 512, 128
NH, HD, NKV = 8, 64, 2
SW = 512
NST = 128
XBCW = 1024
CK = 4
BLK = 128
DFF = 2816
IN_W = 2312
PROJ_W = 2432
EPS = 1e-6
NEG = -1e30
NBUCKET = 32

B1, B2, LR, AEPS, WD, STEP = 0.9, 0.999, 0.001, 1e-08, 0.01, 10

VMEM_LIMIT = 56 * 1024 * 1024

_NT = (((1,), (1,)), ((), ()))
_TN = (((0,), (0,)), ((), ()))


def _mm(a, b):
    return jnp.dot(a, b, preferred_element_type=F32)


def _mm_nt(a, b):
    return lax.dot_general(a, b, _NT, preferred_element_type=F32)


def _mm_tn(a, b):
    return lax.dot_general(a, b, _TN, preferred_element_type=F32)


def _mm_hi(a, b):
    return jnp.dot(a, b, preferred_element_type=F32, precision=HI)


def _split3(x):
    hi = x.astype(BF16)
    r = x - hi.astype(F32)
    mid = r.astype(BF16)
    lo = (r - mid.astype(F32)).astype(BF16)
    return hi, mid, lo


def _sel_r(x, e):
    hi, mid, lo = _split3(x)
    return (_mm(hi, e) + _mm(mid, e)) + _mm(lo, e)


def _sel_l(e, x):
    hi, mid, lo = _split3(x)
    return (_mm(e, hi) + _mm(e, mid)) + _mm(e, lo)


def _sig(x):
    return 1.0 / (1.0 + jnp.exp(-x))


def _cp(sem):
    return pltpu.CompilerParams(dimension_semantics=sem, vmem_limit_bytes=VMEM_LIMIT)


def _row(shape):
    nd = len(shape)
    return pl.BlockSpec(shape, lambda *_: (0,) * nd)


def _adamw(w, g, m, v):
    m = B1 * m + (1.0 - B1) * g
    v = B2 * v + (1.0 - B2) * (g * g)
    m_hat = m / (1.0 - B1 ** STEP)
    v_hat = v / (1.0 - B2 ** STEP)
    delta = -LR * (m_hat / (jnp.sqrt(v_hat) + AEPS) + WD * w)
    return delta, m, v


class _Carry:
    def __init__(self, inps, outs, copies):
        self.inps, self.outs, self.copies = list(inps), list(outs), copies
        self.n = len(copies(0, 0, 0))

    def descriptors(self, in_refs, out_refs, send_sems, recv_sems):
        x, y, c = lax.axis_index("x"), lax.axis_index("y"), lax.axis_index("c")
        out = []
        for j, (flip, a, si, o, di) in enumerate(self.copies(x, y, c)):
            if flip is None:
                out.append(pltpu.make_async_copy(in_refs[a].at[si], out_refs[o].at[di], send_sems.at[j]))
            else:
                fx, fy, fc = flip
                peer = (1 - x if fx else x, 1 - y if fy else y, 1 - c if fc else c)
                out.append(pltpu.make_async_remote_copy(
                    src_ref=in_refs[a].at[si], dst_ref=out_refs[o].at[di],
                    send_sem=send_sems.at[j], recv_sem=recv_sems.at[j],
                    device_id=peer, device_id_type=pl.DeviceIdType.MESH))
        return out


def _pcall(body, args, *, name, grid, in_specs, out_specs, out_shape, scratch_shapes=(), sem=None, nprefetch=0,
           carry=None):
    out_shape, out_specs = list(out_shape), list(out_specs)
    in_specs, scratch_shapes = list(in_specs), list(scratch_shapes)
    nin, nout, nscr = len(in_specs), len(out_shape), len(scratch_shapes)
    run = body
    if carry is not None:
        ncin, ncout = len(carry.inps), len(carry.outs)
        hbm = pl.BlockSpec(memory_space=pl.ANY)

        def run(*refs):
            pre, r = refs[:nprefetch], refs[nprefetch:]
            ins, cins = r[:nin], r[nin:nin + ncin]
            r = r[nin + ncin:]
            outs, couts = r[:nout], r[nout:nout + ncout]
            r = r[nout + ncout:]
            scr, (send_sems, recv_sems) = r[:nscr], r[nscr:]
            first = pl.program_id(0) == 0
            last = pl.program_id(0) == grid[0] - 1
            for ax in range(1, len(grid)):
                first = jnp.logical_and(first, pl.program_id(ax) == 0)
                last = jnp.logical_and(last, pl.program_id(ax) == grid[ax] - 1)

            @pl.when(first)
            def _():
                for d in carry.descriptors(cins, couts, send_sems, recv_sems):
                    d.start()

            body(*pre, *ins, *outs, *scr)

            @pl.when(last)
            def _():
                for d in carry.descriptors(cins, couts, send_sems, recv_sems):
                    d.wait()

        in_specs = in_specs + [hbm] * ncin
        out_specs = out_specs + [hbm] * ncout
        out_shape = out_shape + carry.outs
        scratch_shapes = scratch_shapes + [pltpu.SemaphoreType.DMA((carry.n,)), pltpu.SemaphoreType.DMA((carry.n,))]
        args = list(args) + carry.inps
    if sem is None:
        sem = ("arbitrary",) * len(grid)
    if nprefetch:
        kw = dict(grid_spec=pltpu.PrefetchScalarGridSpec(num_scalar_prefetch=nprefetch, grid=grid, in_specs=in_specs,
                                                         out_specs=out_specs, scratch_shapes=scratch_shapes))
    else:
        kw = dict(grid=grid, in_specs=in_specs, out_specs=out_specs, scratch_shapes=scratch_shapes)
    res = pl.pallas_call(run, name=name, out_shape=out_shape, compiler_params=_cp(sem), **kw)(*args)
    return list(res)


def _merge(*carries):
    inps, outs, offs = [], [], []
    for cr in carries:
        offs.append((len(inps), len(outs)))
        inps += cr.inps
        outs += cr.outs

    def copies(x, y, c):
        return [(f, a + io, si, o + oo, di) for cr, (io, oo) in zip(carries, offs) for f, a, si, o, di in cr.copies(x, y, c)]

    return _Carry(inps, outs, copies)


def _exchange(name, carry):
    return _pcall(lambda: None, [], name=name, grid=(1,), in_specs=[], out_specs=[], out_shape=[], carry=carry)


def _exchange_two(name, carry, then):
    second = _Carry([], [], then)
    nin, nout = len(carry.inps), len(carry.outs)

    def body(*refs):
        ins, outs = refs[:nin], refs[nin:nin + nout]
        send_a, recv_a, send_b, recv_b = refs[nin + nout:]
        for descs in (carry.descriptors(ins, outs, send_a, recv_a), second.descriptors(outs, outs, send_b, recv_b)):
            for d in descs:
                d.start()
            for d in descs:
                d.wait()

    hbm = pl.BlockSpec(memory_space=pl.ANY)
    return list(pl.pallas_call(
        body, name=name, out_shape=carry.outs, in_specs=[hbm] * nin, out_specs=[hbm] * nout,
        scratch_shapes=[pltpu.SemaphoreType.DMA((carry.n,)), pltpu.SemaphoreType.DMA((carry.n,)),
                        pltpu.SemaphoreType.DMA((second.n,)), pltpu.SemaphoreType.DMA((second.n,))],
    )(*carry.inps))


_ALL7 = [(f >> 2 & 1, f >> 1 & 1, f & 1) for f in range(1, 8)]
_CHIPS3 = [(0, 1, 0), (1, 0, 0), (1, 1, 0)]
_SIBLING = (0, 0, 1)


def _gather8_carry(blk):
    def copies(x, y, c):
        me = 4 * x + 2 * y + c
        return [(None, 0, 0, 0, me)] + [(f, 0, 0, 0, me) for f in _ALL7]

    return _Carry([blk[None]], [jax.ShapeDtypeStruct((8,) + blk.shape, blk.dtype)], copies)


def _gather_chips_carry(blks):
    def copies(x, y, c):
        chip = 2 * x + y
        return [(f, a, 0, a, chip) for a in range(len(blks)) for f in [None] + _CHIPS3]

    return _Carry([b[None] for b in blks], [jax.ShapeDtypeStruct((4,) + b.shape, b.dtype) for b in blks], copies)


def _ada_fwd(c_all, w_loc, b_loc, carry=None):
    n = w_loc.shape[1]
    tn = 512

    def body(c_ref, w_ref, b_ref, o_ref):
        cv = c_ref[...]
        cond = cv * _sig(cv)
        o_ref[...] = _mm_hi(cond, w_ref[...]) + b_ref[...]

    return _pcall(
        body, [c_all, w_loc, b_loc], name="ada_fwd", grid=(n // tn,),
        out_shape=[jax.ShapeDtypeStruct((8, n), F32)],
        in_specs=[_row((8, D)), pl.BlockSpec((D, tn), lambda j: (0, j)), pl.BlockSpec((1, tn), lambda j: (0, j))],
        out_specs=[pl.BlockSpec((8, tn), lambda j: (0, j))], carry=carry)


def _ada_bwd_adamw(c_all_t, dmod_loc, w, m, v, carry=None):
    n = w.shape[1]
    tn = 512

    def body(ct_ref, dm_ref, w_ref, m_ref, v_ref, g_ref, d_ref, mo_ref, vo_ref):
        ct = ct_ref[...]
        cond = ct * _sig(ct)
        dm = dm_ref[...]
        g = cond[:, 0:1] * dm[0:1, :]
        for b in range(1, 8):
            g = g + cond[:, b:b + 1] * dm[b:b + 1, :]
        g_ref[...] = g
        d_ref[...], mo_ref[...], vo_ref[...] = _adamw(w_ref[...], g, m_ref[...], v_ref[...])

    wspec = pl.BlockSpec((D, tn), lambda j: (0, j))
    return _pcall(
        body, [c_all_t, dmod_loc, w, m, v], name="ada_bwd_adamw", grid=(n // tn,),
        out_shape=[jax.ShapeDtypeStruct((D, n), F32)] * 4,
        in_specs=[_row((D, 8)), pl.BlockSpec((8, tn), lambda j: (0, j)), wspec, wspec, wspec],
        out_specs=[wspec] * 4, carry=carry)


def _in_proj_fwd(x, a1, sh1, w_in, carry=None):
    s = x.shape[0]
    tm = 512

    def body(x_ref, a_ref, s_ref, w_ref, q_ref, kv_ref, z_ref, xbc_ref, dt_ref):
        def norm(rows):
            xv = x_ref[rows, :]
            r = lax.rsqrt(jnp.mean(xv * xv, axis=-1, keepdims=True) + EPS)
            return (xv * r * a_ref[...] + s_ref[...]).astype(BF16)

        def project(rows, h):
            p = _mm_nt(h, w_ref[...])
            q_ref[rows, :] = p[:, 0:512].astype(BF16)
            kv_ref[rows, :] = p[:, 512:768].astype(BF16)
            z_ref[rows, :] = p[:, 768:1280]
            xbc_ref[rows, :] = p[:, 1280:2304]
            dt_ref[rows, :] = p[:, 2304:2432]

        r0, r1 = slice(0, tm // 2), slice(tm // 2, tm)
        h0 = norm(r0)
        project(r0, h0)
        project(r1, norm(r1))

    def tok(w):
        return pl.BlockSpec((tm, w), lambda i: (i, 0))

    return _pcall(
        body, [x, a1, sh1, w_in], name="in_proj_fwd", grid=(s // tm,),
        out_shape=[jax.ShapeDtypeStruct((s, QW), BF16), jax.ShapeDtypeStruct((s, 2 * KVW), BF16),
                   jax.ShapeDtypeStruct((s, SW), F32), jax.ShapeDtypeStruct((s, XBCW), F32),
                   jax.ShapeDtypeStruct((s, 128), F32)],
        in_specs=[tok(D), _row((1, D)), _row((1, D)), _row((PROJ_W, D))],
        out_specs=[tok(QW), tok(2 * KVW), tok(SW), tok(XBCW), tok(128)], carry=carry)


def _in_proj_bwd(x, dx1, a1, sh1, w_in, dq, dkv, dz, dxbc, ddt, carry=None):
    s = x.shape[0]
    tm = 512

    def body(x_ref, dx1_ref, a_ref, s_ref, w_ref, dq_ref, dkv_ref, dz_ref, dxbc_ref, ddt_ref,
             gx_ref, h_ref, dsh_ref, p_ref):
        i = pl.program_id(0)

        @pl.when(i == 0)
        def _():
            dsh_ref[...] = jnp.zeros_like(dsh_ref)
            p_ref[...] = jnp.zeros_like(p_ref)

        def gather(st):
            rows = st["rows"]
            st["dproj"] = jnp.concatenate([dq_ref[rows, :], dkv_ref[rows, :], dz_ref[rows, :], dxbc_ref[rows, :],
                                           ddt_ref[rows, :]], axis=1)

        def back(st):
            st["dh"] = _mm(st.pop("dproj"), w_ref[...])

        def norm(st):
            rows, dh = st["rows"], st.pop("dh")
            xv = x_ref[rows, :]
            r = lax.rsqrt(jnp.mean(xv * xv, axis=-1, keepdims=True) + EPS)
            xn = xv * r
            a = a_ref[...]
            h_ref[rows, :] = (xn * a + s_ref[...]).astype(BF16)
            st["dsh"] = jnp.sum(dh, axis=0, keepdims=True)
            st["p"] = jnp.sum(dh * xn, axis=0, keepdims=True)
            u = dh * a
            gx_ref[rows, :] = dx1_ref[rows, :] + r * u - xn * (r * jnp.mean(u * xn, axis=-1, keepdims=True))

        g0, g1 = [dict(rows=slice(k * (tm // 2), (k + 1) * (tm // 2))) for k in range(2)]
        for stage, st in [(gather, g0), (back, g0), (gather, g1), (norm, g0), (back, g1), (norm, g1)]:
            stage(st)
        dsh_ref[0:1, :] += g0["dsh"] + g1["dsh"]
        p_ref[0:1, :] += g0["p"] + g1["p"]

    def tok(w):
        return pl.BlockSpec((tm, w), lambda i: (i, 0))

    return _pcall(
        body, [x, dx1, a1, sh1, w_in, dq, dkv, dz, dxbc, ddt], name="in_proj_bwd", grid=(s // tm,),
        out_shape=[jax.ShapeDtypeStruct((s, D), F32), jax.ShapeDtypeStruct((s, D), BF16),
                   jax.ShapeDtypeStruct((8, D), F32), jax.ShapeDtypeStruct((8, D), F32)],
        in_specs=[tok(D), tok(D), _row((1, D)), _row((1, D)), _row((PROJ_W, D)),
                  tok(QW), tok(2 * KVW), tok(SW), tok(XBCW), tok(128)],
        out_specs=[tok(D), tok(D), _row((8, D)), _row((8, D))], carry=carry)


def _attn_geometry():
    dist = np.arange(BLK)[:, None] + BLK - np.arange(2 * BLK)[None, :]
    n = np.maximum(dist, 0)
    max_exact = NBUCKET // 2
    large = max_exact + (np.log(np.maximum(n, 1) / max_exact) / np.log(128 / max_exact)
                         * (NBUCKET - max_exact)).astype(np.int32)
    large = np.minimum(large, NBUCKET - 1)
    bucket = np.where(n < max_exact, n, large).astype(np.int32)
    mask = (dist >= 0) & (dist < 128)
    return bucket, mask


def _attn_heads(is_first, q_blk, kvw, bias_ref, sinks_ref):
    qv = q_blk * 0.125
    col = lax.broadcasted_iota(jnp.int32, (BLK, 2 * BLK), 1)
    first = jnp.where(jnp.logical_and(is_first, col < BLK), NEG, 0.0)
    groups = []
    for g in range(NKV):
        qs = jnp.concatenate([qv[:, (4 * g + r) * HD:(4 * g + r + 1) * HD] for r in range(4)], axis=0)
        kw = kvw[:, g * HD:(g + 1) * HD]
        vw = kvw[:, KVW + g * HD:KVW + (g + 1) * HD]
        sc = _mm_nt(qs, kw)
        pn, ps = [], []
        for r in range(4):
            h = 4 * g + r
            sr = sc[r * BLK:(r + 1) * BLK] + bias_ref[h] + first
            sink = sinks_ref[h]
            m = jnp.maximum(jnp.max(sr, axis=-1, keepdims=True), sink)
            p = jnp.exp(sr - m)
            es = jnp.exp(sink - m)
            inv = 1.0 / (jnp.sum(p, axis=-1, keepdims=True) + es)
            pn.append(p * inv)
            ps.append(es * inv)
        pn = jnp.concatenate(pn, axis=0)
        ps = jnp.concatenate(ps, axis=0)
        o = _mm(pn.astype(BF16), vw)
        groups.append((qs, kw, vw, pn, ps, o))
    return groups


def _unstack_heads(parts):
    return jnp.concatenate([p[r * BLK:(r + 1) * BLK] for p in parts for r in range(4)], axis=1)


NB = 2


def _attn_fwd(q, kv, bias, sinks, nw, carry=None):
    s = q.shape[0]

    def body(q_ref, kvp_ref, kvc_ref, bias_ref, sinks_ref, nw_ref, y_ref):
        t = pl.program_id(0)
        kv3 = jnp.concatenate([kvp_ref[...], kvc_ref[...]], axis=0)
        for sub in range(NB):
            rows = slice(sub * BLK, (sub + 1) * BLK)
            groups = _attn_heads(jnp.logical_and(t == 0, sub == 0), q_ref[rows, :], kv3[sub * BLK:(sub + 2) * BLK],
                                 bias_ref, sinks_ref)
            o = _unstack_heads([g[5] for g in groups])
            r = lax.rsqrt(jnp.mean(o * o, axis=-1, keepdims=True) + EPS)
            y_ref[rows, :] = (o * r * nw_ref[...]).astype(BF16)

    return _pcall(
        body, [q, kv, kv, bias, sinks, nw], name="attn_fwd", grid=(s // (NB * BLK),),
        out_shape=[jax.ShapeDtypeStruct((s, QW), BF16)],
        in_specs=[pl.BlockSpec((NB * BLK, QW), lambda t: (t, 0)),
                  pl.BlockSpec((BLK, 2 * KVW), lambda t: (jnp.maximum(NB * t - 1, 0), 0)),
                  pl.BlockSpec((NB * BLK, 2 * KVW), lambda t: (t, 0)),
                  _row((NH, BLK, 2 * BLK)),
                  pl.BlockSpec(memory_space=pltpu.SMEM),
                  _row((1, QW))],
        out_specs=[pl.BlockSpec((NB * BLK, QW), lambda t: (t, 0))], carry=carry)


def _attn_bwd(q, kv, dya, bias, sinks, nw, act, dx2, gate2, w_dn, carry=None):
    s = q.shape[0]
    nt = s // (NB * BLK)
    npiece = DFF // NB

    def body(q_ref, kvp_ref, kvc_ref, dy_ref, bias_ref, sinks_ref, nw_ref, act_ref, dx2_ref, g2_ref, wdn_ref,
             dq_ref, dkv_ref, dbias_ref, dsink_ref, dnw_ref, gdn_hbm, dg2_ref, carry_ref, held_ref, acc_ref, sem):
        t = pl.program_id(0)

        @pl.when(t == 0)
        def _():
            carry_ref[...] = jnp.zeros_like(carry_ref)
            held_ref[...] = jnp.zeros_like(held_ref)
            dbias_ref[...] = jnp.zeros_like(dbias_ref)
            dsink_ref[...] = jnp.zeros_like(dsink_ref)
            dnw_ref[...] = jnp.zeros_like(dnw_ref)
            acc_ref[...] = jnp.zeros_like(acc_ref)

        def wgrad_piece(sub):
            rows = slice(sub * npiece, (sub + 1) * npiece)
            acc_ref[rows, :] += _mm_tn(act_ref[:, rows], dx2_ref[...].astype(BF16))

        def block(sub, kv3):
            rows = slice(sub * BLK, (sub + 1) * BLK)
            groups = _attn_heads(jnp.logical_and(t == 0, sub == 0), q_ref[rows, :], kv3[sub * BLK:(sub + 2) * BLK],
                                 bias_ref, sinks_ref)
            o = _unstack_heads([g[5] for g in groups])
            r = lax.rsqrt(jnp.mean(o * o, axis=-1, keepdims=True) + EPS)
            dy = dy_ref[rows, :]
            on = o * r
            dnw_ref[0:1, :] += jnp.sum(dy * on, axis=0, keepdims=True)
            u = dy * nw_ref[...]
            do = r * u - on * (r * jnp.mean(u * on, axis=-1, keepdims=True))
            dq_parts, dk_parts, dv_parts = [], [], []
            for g, (qs, kw, vw, pn, ps, og) in enumerate(groups):
                dos = jnp.concatenate([do[:, (4 * g + r_) * HD:(4 * g + r_ + 1) * HD] for r_ in range(4)], axis=0)
                delta = jnp.sum(dos * og, axis=-1, keepdims=True)
                dp = _mm_nt(dos.astype(BF16), vw)
                ds = pn * (dp - delta)
                dsk = ps * delta
                lane = lax.broadcasted_iota(jnp.int32, (1, 128), 1)
                for r_ in range(4):
                    h = 4 * g + r_
                    dbias_ref[h] += ds[r_ * BLK:(r_ + 1) * BLK]
                    dsink_ref[0:1, :] -= jnp.where(lane == h, jnp.sum(dsk[r_ * BLK:(r_ + 1) * BLK]), 0.0)
                dsb = ds.astype(BF16)
                dq_parts.append(_mm(dsb, kw) * 0.125)
                dk_parts.append(_mm_tn(dsb, qs))
                dv_parts.append(_mm_tn(pn.astype(BF16), dos.astype(BF16)))
            dq_ref[rows, :] = _unstack_heads(dq_parts).astype(BF16)
            return jnp.concatenate(dk_parts + dv_parts, axis=1)

        @pl.when(t < nt)
        def _():
            kv3 = jnp.concatenate([kvp_ref[...], kvc_ref[...]], axis=0)
            tail = carry_ref[...]
            for sub in range(NB):
                d = block(sub, kv3)
                done = tail + d[0:BLK]
                if sub == 0:
                    dkv_ref[0:(NB - 1) * BLK, :] = held_ref[...].astype(BF16)
                    dkv_ref[(NB - 1) * BLK:NB * BLK, :] = done.astype(BF16)
                else:
                    held_ref[(sub - 1) * BLK:sub * BLK, :] = done
                tail = d[BLK:2 * BLK]
                wgrad_piece(sub)
            carry_ref[...] = tail

        @pl.when(t == nt)
        def _():
            dkv_ref[0:(NB - 1) * BLK, :] = held_ref[...].astype(BF16)
            dkv_ref[(NB - 1) * BLK:NB * BLK, :] = carry_ref[...].astype(BF16)
            acc = acc_ref[...]
            dg2_ref[...] = jnp.zeros_like(dg2_ref)
            dg2_ref[0:1, :] = jnp.sum(acc * wdn_ref[...].astype(F32), axis=0, keepdims=True)
            acc_ref[...] = acc * g2_ref[...]
            cp = pltpu.make_async_copy(acc_ref, gdn_hbm, sem)
            cp.start()
            cp.wait()

    last = nt - 1
    tile = lambda w: pl.BlockSpec((NB * BLK, w), lambda t: (jnp.minimum(t, last), 0))
    return _pcall(
        body, [q, kv, kv, dya, bias, sinks, nw, act, dx2, gate2, w_dn], name="attn_bwd", grid=(nt + 1,),
        out_shape=[jax.ShapeDtypeStruct((s, QW), BF16), jax.ShapeDtypeStruct((s, 2 * KVW), BF16),
                   jax.ShapeDtypeStruct((NH, BLK, 2 * BLK), F32), jax.ShapeDtypeStruct((NH, 128), F32),
                   jax.ShapeDtypeStruct((8, QW), F32), jax.ShapeDtypeStruct((DFF, D), F32),
                   jax.ShapeDtypeStruct((8, D), F32)],
        in_specs=[tile(QW),
                  pl.BlockSpec((BLK, 2 * KVW), lambda t: (jnp.clip(NB * t - 1, 0, NB * nt - 1), 0)),
                  tile(2 * KVW), tile(QW),
                  _row((NH, BLK, 2 * BLK)),
                  pl.BlockSpec(memory_space=pltpu.SMEM),
                  _row((1, QW)), tile(DFF), tile(D), _row((1, D)), _row((DFF, D))],
        out_specs=[tile(QW),
                   pl.BlockSpec((NB * BLK, 2 * KVW), lambda t: (jnp.maximum(t - 1, 0), 0)),
                   _row((NH, BLK, 2 * BLK)), _row((NH, 128)), _row((8, QW)),
                   pl.BlockSpec(memory_space=pl.ANY), _row((8, D))],
        scratch_shapes=[pltpu.VMEM((BLK, 2 * KVW), F32), pltpu.VMEM(((NB - 1) * BLK, 2 * KVW), F32),
                        pltpu.VMEM((DFF, D), F32), pltpu.SemaphoreType.DMA], carry=carry)


def _rel_bias_grad(dbias, bucket):
    def body(db_ref, bk_ref, o_ref):
        bk = bk_ref[...]
        lane = lax.broadcasted_iota(jnp.int32, (1, 128), 1)
        for b in range(NBUCKET):
            sel = bk == b
            row = jnp.zeros((1, 128), F32)
            for h in range(NH):
                row = row + jnp.where(lane == h, jnp.sum(jnp.where(sel, db_ref[h], 0.0)), 0.0)
            o_ref[b:b + 1, :] = row

    return pl.pallas_call(
        body, name="rel_bias_grad",
        out_shape=jax.ShapeDtypeStruct((NBUCKET, 128), F32),
    )(dbias, bucket)


def _ssd_consts():
    head_of_lane = np.arange(SW) // HD
    expand = (np.arange(128)[:, None] == head_of_lane[None, :]).astype(np.float32)
    tril = np.tril(np.ones((BLK, BLK), np.float32))
    return (jnp.asarray(expand, BF16), jnp.asarray(expand.T.copy(), BF16), jnp.asarray(tril, BF16),
            jnp.asarray(tril.T.copy(), BF16))


def _conv_pre(xc, halo, cw, cb):
    ext = jnp.concatenate([halo, xc], axis=0)
    taps = [xc if k == CK - 1 else pltpu.roll(ext, CK - 1 - k, 0)[8:8 + BLK] for k in range(CK)]
    return cb + sum(cw[k:k + 1, :] * taps[k] for k in range(CK))


def _ssd_chunk(pre, dtr, dtb, av, dkv, ex, tril, h_in):
    sp = _sig(pre)
    xbc = pre * sp
    xs, bm, cm = xbc[:, 0:SW], xbc[:, SW:SW + 2 * NST], xbc[:, SW + 2 * NST:]
    dtin = dtr + dtb
    dt = jnp.maximum(dtin, 0.0) + jnp.log1p(jnp.exp(-jnp.abs(dtin)))
    cs = _sel_l(tril, dt * av)
    cst = cs.T
    dtx = _sel_r(dt, ex)
    csx = _sel_r(cs, ex)
    xdt = xs * dtx
    csl = csx[BLK - 1:BLK, :]
    decx = jnp.exp(csl - csx)
    ecsx = jnp.exp(csx)
    ecl = jnp.exp(csl)
    causal = tril.astype(F32) > 0.5
    ydiag, yoff, cbs, lms = [], [], [], []
    for g in range(2):
        bg = bm[:, g * NST:(g + 1) * NST].astype(BF16)
        cg = cm[:, g * NST:(g + 1) * NST].astype(BF16)
        cb = _mm_nt(cg, bg)
        cbs.append(cb)
        yoff.append(_mm(cg, h_in[:, g * 256:(g + 1) * 256].astype(BF16)))
        for r in range(4):
            h = 4 * g + r
            seg = cs[:, h:h + 1] - cst[h:h + 1, :]
            lm = jnp.where(causal, jnp.exp(jnp.minimum(seg, 0.0)), 0.0)
            lms.append(lm)
            ydiag.append(_mm((cb * lm).astype(BF16), xdt[:, h * HD:(h + 1) * HD].astype(BF16)))
    yoff = jnp.concatenate(yoff, axis=1) * ecsx
    y = jnp.concatenate(ydiag, axis=1) + yoff + dkv * xs
    return dict(pre=pre, sp=sp, xs=xs, bm=bm, cm=cm, dtin=dtin, dt=dt, av=av, cs=cs, cst=cst,
                dtx=dtx, csx=csx, xdt=xdt, decx=decx, ecsx=ecsx, ecl=ecl, causal=causal, cbs=cbs, lms=lms,
                yoff=yoff, y=y)


def _group_mean(t):
    m0 = jnp.mean(t[:, 0:256], axis=-1, keepdims=True)
    m1 = jnp.mean(t[:, 256:512], axis=-1, keepdims=True)
    return jnp.concatenate([jnp.broadcast_to(m0, (t.shape[0], 256)), jnp.broadcast_to(m1, (t.shape[0], 256))], axis=1)


SUBS = 4


def _ssd_fwd(z, xbc, dtr, cw, cb, dtb, av, dk, nw, carry=None):
    s = z.shape[0]
    nc = s // BLK
    tile = SUBS * BLK
    ex, _, tril, _ = _ssd_consts()

    def body(z_ref, xc_ref, xh_ref, dtr_ref, cw_ref, cb_ref, dtb_ref, a_ref, dk_ref, nw_ref, ex_ref, tril_ref,
             y_ref, hs_ref, pre_ref, h_ref):
        t = pl.program_id(0)

        @pl.when(t == 0)
        def _():
            h_ref[...] = jnp.zeros_like(h_ref)

        h_in = h_ref[...]
        for sub in range(SUBS):
            rows = slice(sub * BLK, (sub + 1) * BLK)
            xc = xc_ref[rows, :]
            halo = jnp.where(t == 0, 0.0, xh_ref[...]) if sub == 0 else xc_ref[sub * BLK - 8:sub * BLK, :]
            pre = _conv_pre(xc, halo, cw_ref[...], cb_ref[...])
            pre_ref[rows, :] = pre
            hs_ref[sub] = h_in
            f = _ssd_chunk(pre, dtr_ref[rows, :], dtb_ref[...], a_ref[...], dk_ref[...], ex_ref[...], tril_ref[...], h_in)
            dx = (f["decx"] * f["xdt"]).astype(BF16)
            st = [_mm_tn(f["bm"][:, g * NST:(g + 1) * NST].astype(BF16), dx[:, g * 256:(g + 1) * 256]) for g in range(2)]
            h_in = h_in * f["ecl"] + jnp.concatenate(st, axis=1)
            zv = z_ref[rows, :]
            tg = f["y"] * (zv * _sig(zv))
            r = lax.rsqrt(_group_mean(tg * tg) + EPS)
            y_ref[rows, :] = (tg * r * nw_ref[...]).astype(BF16)
        h_ref[...] = h_in

    cur = lambda w: pl.BlockSpec((tile, w), lambda t: (t, 0))
    return _pcall(
        body, [z, xbc, xbc, dtr, cw, cb, dtb, av, dk, nw, ex, tril], name="ssd_fwd", grid=(s // tile,),
        out_shape=[jax.ShapeDtypeStruct((s, SW), BF16), jax.ShapeDtypeStruct((nc, NST, SW), F32),
                   jax.ShapeDtypeStruct((s, XBCW), F32)],
        in_specs=[cur(SW), cur(XBCW), pl.BlockSpec((8, XBCW), lambda t: (jnp.maximum(t * (tile // 8) - 1, 0), 0)),
                  cur(128), _row((8, XBCW)), _row((1, XBCW)), _row((1, 128)),
                  _row((1, 128)), _row((1, SW)), _row((1, SW)), _row((128, SW)), _row((BLK, BLK))],
        out_specs=[cur(SW), pl.BlockSpec((SUBS, NST, SW), lambda t: (t, 0, 0)), cur(XBCW)],
        scratch_shapes=[pltpu.VMEM((NST, SW), F32)], carry=carry)


def _ssd_bwd(z, xbc, pre_all, dtr, dys, hs, cw, dtb, av, dk, nw, carry=None):
    s = z.shape[0]
    tile = SUBS * BLK
    nt = s // tile
    ex, ext_t, tril, triu = _ssd_consts()

    def body(z_ref, xc_ref, pre_ref, dtr_ref, dy_ref, hs_ref, cw_ref, dtb_ref, a_ref, dk_ref, nw_ref,
             ex_ref, ext_ref, tril_ref, triu_ref,
             dz_ref, dxbc_ref, ddt_ref, dcw_ref, dcb_ref, dnw_ref, dhd_ref, dh_ref, nxt_ref, dd_ref):
        i = pl.program_id(0)

        @pl.when(i == 0)
        def _():
            dh_ref[...] = jnp.zeros_like(dh_ref)
            nxt_ref[...] = jnp.zeros_like(nxt_ref)
            dd_ref[...] = jnp.zeros_like(dd_ref)
            dcw_ref[...] = jnp.zeros_like(dcw_ref)
            dcb_ref[...] = jnp.zeros_like(dcb_ref)
            dnw_ref[...] = jnp.zeros_like(dnw_ref)
            dhd_ref[...] = jnp.zeros_like(dhd_ref)

        gst, nxt = dh_ref[...], nxt_ref[...]
        for sub in reversed(range(SUBS)):
            rows = slice(sub * BLK, (sub + 1) * BLK)
            gst, nxt = chunk(sub, rows, gst, nxt, z_ref, xc_ref, pre_ref, dtr_ref, dy_ref, hs_ref, cw_ref, dtb_ref,
                             a_ref, dk_ref, nw_ref, ex_ref, ext_ref, tril_ref, triu_ref,
                             dz_ref, dxbc_ref, ddt_ref, dcw_ref, dcb_ref, dnw_ref, dhd_ref, dd_ref)
        dh_ref[...] = gst
        nxt_ref[...] = nxt

        @pl.when(i == nt - 1)
        def _():
            dhd_ref[2:3, :] = _sel_r(dd_ref[...], ext_ref[...])[0:1, :]

    def chunk(sub, rows, gst, nxt, z_ref, xc_ref, pre_ref, dtr_ref, dy_ref, hs_ref, cw_ref, dtb_ref,
              a_ref, dk_ref, nw_ref, ex_ref, ext_ref, tril_ref, triu_ref,
              dz_ref, dxbc_ref, ddt_ref, dcw_ref, dcb_ref, dnw_ref, dhd_ref, dd_ref):
        h_in = hs_ref[sub]
        f = _ssd_chunk(pre_ref[rows, :], dtr_ref[rows, :], dtb_ref[...], a_ref[...], dk_ref[...], ex_ref[...],
                       tril_ref[...], h_in)
        xs, xdt, decx, ecsx, ecl, dtx = f["xs"], f["xdt"], f["decx"], f["ecsx"], f["ecl"], f["dtx"]
        cs, cst, causal = f["cs"], f["cst"], f["causal"]
        causal_t = triu_ref[...].astype(F32) > 0.5

        zv = z_ref[rows, :]
        sz = _sig(zv)
        gz = zv * sz
        t = f["y"] * gz
        r = lax.rsqrt(_group_mean(t * t) + EPS)
        tn_ = t * r
        dyn = dy_ref[rows, :]
        dnw_ref[0:1, :] += jnp.sum(dyn * tn_, axis=0, keepdims=True)
        u = dyn * nw_ref[...]
        dt_ = r * u - tn_ * (r * _group_mean(u * tn_))
        dy = dt_ * gz
        dz_ref[rows, :] = (dt_ * f["y"] * (sz * (1.0 + zv * (1.0 - sz)))).astype(BF16)

        dd_ref[0:1, :] += jnp.sum(dy * xs, axis=0, keepdims=True)
        dxs = dk_ref[...] * dy

        edy = ecsx * dy
        dxdt, dbs, dcs_, dcsx_parts, dh_new = [], [], [], [], []
        lane = lax.broadcasted_iota(jnp.int32, (1, 128), 1)
        dcs_intra = jnp.zeros((BLK, 128), F32)
        for g in range(2):
            sl = slice(g * 256, (g + 1) * 256)
            bgf, cgf = f["bm"][:, g * NST:(g + 1) * NST], f["cm"][:, g * NST:(g + 1) * NST]
            bg, cg = bgf.astype(BF16), cgf.astype(BF16)
            gg = gst[:, sl].astype(BF16)
            hg = h_in[:, sl].astype(BF16)
            edyg = edy[:, sl].astype(BF16)
            dc = _mm_nt(edyg, hg)
            dh_new.append(gst[:, sl] * ecl[:, sl] + _mm_tn(cg, edyg))
            bgm = _mm(bg, gg)
            dxdt_g = decx[:, sl] * bgm
            dxg = (decx[:, sl] * xdt[:, sl]).astype(BF16)
            db = _mm_nt(dxg, gg)
            qd = bgm * xdt[:, sl] * decx[:, sl]
            last = jnp.sum(qd, axis=0, keepdims=True) + ecl[:, sl] * jnp.sum(gst[:, sl] * h_in[:, sl], axis=0, keepdims=True)
            rowid = lax.broadcasted_iota(jnp.int32, (BLK, 256), 0)
            dcsx_parts.append(f["yoff"][:, sl] * dy[:, sl] - qd + jnp.where(rowid == BLK - 1, last, 0.0))
            cb_ = f["cbs"][g]
            cbt = _mm_nt(bg, cg)
            dcb_ = jnp.zeros((BLK, BLK), F32)
            dcbt = jnp.zeros((BLK, BLK), F32)
            dxd = []
            for r_ in range(4):
                h = 4 * g + r_
                hl = slice(h * HD, (h + 1) * HD)
                lm = f["lms"][h]
                segt = cst[h:h + 1, :] - cs[:, h:h + 1]
                lmt = jnp.where(causal_t, jnp.exp(jnp.minimum(segt, 0.0)), 0.0)
                dyh = dy[:, hl].astype(BF16)
                xdh = xdt[:, hl].astype(BF16)
                dw = _mm_nt(dyh, xdh)
                dwt = _mm_nt(xdh, dyh)
                wt = cbt * lmt
                dxd.append(_mm(wt.astype(BF16), dyh))
                dcb_ = dcb_ + dw * lm
                dcbt = dcbt + dwt * lmt
                col = jnp.sum(dw * (cb_ * lm), axis=-1, keepdims=True) - jnp.sum(dwt * wt, axis=-1, keepdims=True)
                dcs_intra = dcs_intra + jnp.where(lane == h, col, 0.0)
            dxdt.append(dxdt_g + jnp.concatenate(dxd, axis=1))
            dcs_.append(dc + _mm(dcb_.astype(BF16), bg))
            dbs.append(db + _mm(dcbt.astype(BF16), cg))
        dxdt = jnp.concatenate(dxdt, axis=1)
        dxs = dxs + dxdt * dtx
        ext_t_ = ext_ref[...]
        dcs = dcs_intra + _sel_r(jnp.concatenate(dcsx_parts, axis=1), ext_t_)
        da = _sel_l(triu_ref[...], dcs)
        ddt = da * f["av"] + _sel_r(dxdt * xs, ext_t_)
        dhd_ref[1:2, :] += jnp.sum(da * f["dt"], axis=0, keepdims=True)
        ddtr = ddt * _sig(f["dtin"])
        dhd_ref[0:1, :] += jnp.sum(ddtr, axis=0, keepdims=True)
        ddt_ref[rows, :] = ddtr.astype(BF16)

        sp, pre = f["sp"], f["pre"]
        dact = jnp.concatenate([dxs] + dbs + dcs_, axis=1)
        dpre = dact * (sp * (1.0 + pre * (1.0 - sp)))
        dcb_ref[0:1, :] += jnp.sum(dpre, axis=0, keepdims=True)
        ext2 = jnp.concatenate([dpre, nxt], axis=0)
        shifted = [pltpu.roll(ext2, BLK + 8 - (CK - 1 - k), 0)[0:BLK] for k in range(CK - 1)] + [dpre]
        cw = cw_ref[...]
        xc = xc_ref[rows, :]
        dxr = cw[CK - 1:CK, :] * dpre
        for k in range(CK):
            dcw_ref[k:k + 1, :] += jnp.sum(shifted[k] * xc, axis=0, keepdims=True)
            if k < CK - 1:
                dxr = dxr + cw[k:k + 1, :] * shifted[k]
        dxbc_ref[rows, :] = dxr.astype(BF16)
        return jnp.concatenate(dh_new, axis=1), dpre[0:8]

    cur = lambda w: pl.BlockSpec((tile, w), lambda i: (nt - 1 - i, 0))
    return _pcall(
        body, [z, xbc, pre_all, dtr, dys, hs, cw, dtb, av, dk, nw, ex, ext_t, tril, triu], name="ssd_bwd", grid=(nt,),
        out_shape=[jax.ShapeDtypeStruct((s, SW), BF16), jax.ShapeDtypeStruct((s, XBCW), BF16),
                   jax.ShapeDtypeStruct((s, 128), BF16), jax.ShapeDtypeStruct((8, XBCW), F32),
                   jax.ShapeDtypeStruct((8, XBCW), F32), jax.ShapeDtypeStruct((8, SW), F32),
                   jax.ShapeDtypeStruct((8, 128), F32)],
        in_specs=[cur(SW), cur(XBCW), cur(XBCW), cur(128), cur(SW),
                  pl.BlockSpec((SUBS, NST, SW), lambda i: (nt - 1 - i, 0, 0)),
                  _row((8, XBCW)), _row((1, 128)), _row((1, 128)), _row((1, SW)), _row((1, SW)),
                  _row((128, SW)), _row((SW, 128)), _row((BLK, BLK)), _row((BLK, BLK))],
        out_specs=[cur(SW), cur(XBCW), cur(128), _row((8, XBCW)), _row((8, XBCW)), _row((8, SW)), _row((8, 128))],
        scratch_shapes=[pltpu.VMEM((NST, SW), F32), pltpu.VMEM((8, XBCW), F32), pltpu.VMEM((8, SW), F32)], carry=carry)


def _load_once(i, pairs, sem):
    @pl.when(i == 0)
    def _():
        cps = [pltpu.make_async_copy(src, dst, sem.at[k]) for k, (src, dst) in enumerate(pairs)]
        for cp in cps:
            cp.start()
        for cp in cps:
            cp.wait()


def _mlp_fwd(x, ya, ys, tgt, w_o, w_ga, w_gb, w_dn, gate1, a2, sh2, gate2, fn):
    s = x.shape[0]
    sub_m, subs = 256, 2
    tm = sub_m * subs

    def body(x_ref, ya_ref, ys_ref, t_ref, wo_hbm, wga_hbm, wgb_hbm, wdn_hbm, g1_ref, a2_ref, s2_ref, g2_ref, fn_ref,
             x1_ref, gu_ref, dx2_ref, loss_ref, dfn_ref, wo, wga, wgb, wdn, sem):
        i = pl.program_id(0)
        _load_once(i, [(wo_hbm, wo), (wga_hbm, wga), (wgb_hbm, wgb), (wdn_hbm, wdn)], sem)

        @pl.when(i == 0)
        def _():
            loss_ref[...] = jnp.zeros_like(loss_ref)
            dfn_ref[...] = jnp.zeros_like(dfn_ref)

        def proj(st):
            st["mix"] = _mm(ya_ref[st["rows"], :], wo[0:QW, :]) + _mm(ys_ref[st["rows"], :], wo[QW:D, :])

        def norm(st):
            x1 = x_ref[st["rows"], :] + g1_ref[...] * st.pop("mix")
            x1_ref[st["rows"], :] = x1
            r2 = lax.rsqrt(jnp.mean(x1 * x1, axis=-1, keepdims=True) + EPS)
            st["x1"] = x1
            st["h2"] = (x1 * r2 * a2_ref[...] + s2_ref[...]).astype(BF16)

        def gate_up(st):
            h2 = st.pop("h2")
            ha, hb = h2[:, 0:D // 2], h2[:, D // 2:D]
            gub = jnp.concatenate([(_mm(ha, wga[j]) + _mm(hb, wgb[j])).astype(BF16) for j in range(4)], axis=1)
            gu_ref[st["rows"], :] = gub
            st["gub"] = gub

        def activate(st):
            gub = st.pop("gub")
            gv, uv = gub[:, 0:DFF].astype(F32), gub[:, DFF:].astype(F32)
            st["act"] = (gv * _sig(gv) * uv).astype(BF16)

        def down(st):
            st["ff"] = _mm(st.pop("act"), wdn[...])

        def head(st):
            x2 = st.pop("x1") + g2_ref[...] * st.pop("ff")
            r3 = lax.rsqrt(jnp.mean(x2 * x2, axis=-1, keepdims=True) + EPS)
            xn = x2 * r3
            fnv = fn_ref[...]
            err = xn * fnv - t_ref[st["rows"], :]
            st["loss"] = jnp.sum(err * err) * (0.5 / D)
            dy = err * (1.0 / D)
            st["dfn"] = jnp.sum(dy * xn, axis=0, keepdims=True)
            u = dy * fnv
            dx2_ref[st["rows"], :] = r3 * u - xn * (r3 * jnp.mean(u * xn, axis=-1, keepdims=True))

        a, b = [dict(rows=slice(k * sub_m, (k + 1) * sub_m)) for k in range(subs)]
        for stage, st in [(proj, a), (norm, a), (proj, b), (gate_up, a), (norm, b), (activate, a), (gate_up, b),
                          (down, a), (activate, b), (head, a), (down, b), (head, b)]:
            stage(st)
        loss_ref[...] += a["loss"] + b["loss"]
        dfn_ref[0:1, :] += a["dfn"] + b["dfn"]

    def tok(w):
        return pl.BlockSpec((tm, w), lambda i: (i, 0))

    hbm = pl.BlockSpec(memory_space=pl.ANY)
    return pl.pallas_call(
        body, name="mlp_fwd", grid=(s // tm,),
        out_shape=[jax.ShapeDtypeStruct((s, D), F32), jax.ShapeDtypeStruct((s, 2 * DFF), BF16),
                   jax.ShapeDtypeStruct((s, D), F32), jax.ShapeDtypeStruct((8, 128), F32),
                   jax.ShapeDtypeStruct((8, D), F32)],
        in_specs=[tok(D), tok(QW), tok(SW), tok(D), hbm, hbm, hbm, hbm,
                  _row((1, D)), _row((1, D)), _row((1, D)), _row((1, D)), _row((1, D))],
        out_specs=[tok(D), tok(2 * DFF), tok(D), _row((8, 128)), _row((8, D))],
        scratch_shapes=[pltpu.VMEM((D, D), BF16), pltpu.VMEM(w_ga.shape, BF16), pltpu.VMEM(w_gb.shape, BF16),
                        pltpu.VMEM((DFF, D), BF16), pltpu.SemaphoreType.DMA((4,))],
        compiler_params=_cp(("arbitrary",)),
    )(x, ya, ys, tgt, w_o, w_ga, w_gb, w_dn, gate1, a2, sh2, gate2, fn)


def _mlp_bwd(x1, gu, dx2, w_o, w_ga, w_gb, w_dn, gate1, a2, sh2, gate2):
    s = x1.shape[0]
    tm = 256
    nj = 2 * DFF // 4

    def body(x1_ref, gu_ref, dx2_ref, wo_hbm, wga_hbm, wgb_hbm, wdn_hbm, g1_ref, a2_ref, s2_ref, g2_ref,
             dx1_ref, dya_ref, dys_ref, act_ref, dgu_ref, h2_ref, dsh_ref, p_ref, wo, wga, wgb, wdn, sem):
        i = pl.program_id(0)
        _load_once(i, [(wo_hbm, wo), (wga_hbm, wga), (wgb_hbm, wgb), (wdn_hbm, wdn)], sem)

        @pl.when(i == 0)
        def _():
            dsh_ref[...] = jnp.zeros_like(dsh_ref)
            p_ref[...] = jnp.zeros_like(p_ref)

        dx2 = dx2_ref[...]
        dact = _mm_nt((dx2 * g2_ref[...]).astype(BF16), wdn[...])
        gub = gu_ref[...]
        gv, uv = gub[:, 0:DFF].astype(F32), gub[:, DFF:].astype(F32)
        sg = _sig(gv)
        sl = gv * sg
        act_ref[...] = (sl * uv).astype(BF16)
        dgu = jnp.concatenate([dact * uv * (sg * (1.0 + gv * (1.0 - sg))), dact * sl], axis=1).astype(BF16)
        dgu_ref[...] = dgu
        dha = sum(_mm_nt(dgu[:, j * nj:(j + 1) * nj], wga[j]) for j in range(4))
        dhb = sum(_mm_nt(dgu[:, j * nj:(j + 1) * nj], wgb[j]) for j in range(4))
        dh = jnp.concatenate([dha, dhb], axis=1)
        x1 = x1_ref[...]
        r2 = lax.rsqrt(jnp.mean(x1 * x1, axis=-1, keepdims=True) + EPS)
        xn = x1 * r2
        a2 = a2_ref[...]
        h2_ref[...] = (xn * a2 + s2_ref[...]).astype(BF16)
        dsh_ref[0:1, :] += jnp.sum(dh, axis=0, keepdims=True)
        p_ref[0:1, :] += jnp.sum(dh * xn, axis=0, keepdims=True)
        u = dh * a2
        dx1 = dx2 + r2 * u - xn * (r2 * jnp.mean(u * xn, axis=-1, keepdims=True))
        dx1_ref[...] = dx1
        dcat = _mm_nt((dx1 * g1_ref[...]).astype(BF16), wo[...])
        dya_ref[...] = dcat[:, 0:QW]
        dys_ref[...] = dcat[:, QW:D]

    def tok(w):
        return pl.BlockSpec((tm, w), lambda i: (i, 0))

    hbm = pl.BlockSpec(memory_space=pl.ANY)
    return pl.pallas_call(
        body, name="mlp_bwd", grid=(s // tm,),
        out_shape=[jax.ShapeDtypeStruct((s, D), F32), jax.ShapeDtypeStruct((s, QW), F32),
                   jax.ShapeDtypeStruct((s, SW), F32), jax.ShapeDtypeStruct((s, DFF), BF16),
                   jax.ShapeDtypeStruct((s, 2 * DFF), BF16), jax.ShapeDtypeStruct((s, D), BF16),
                   jax.ShapeDtypeStruct((8, D), F32), jax.ShapeDtypeStruct((8, D), F32)],
        in_specs=[tok(D), tok(2 * DFF), tok(D), hbm, hbm, hbm, hbm, _row((1, D)), _row((1, D)), _row((1, D)), _row((1, D))],
        out_specs=[tok(D), tok(QW), tok(SW), tok(DFF), tok(2 * DFF), tok(D), _row((8, D)), _row((8, D))],
        scratch_shapes=[pltpu.VMEM((D, D), BF16), pltpu.VMEM(w_ga.shape, BF16), pltpu.VMEM(w_gb.shape, BF16),
                        pltpu.VMEM((DFF, D), BF16), pltpu.SemaphoreType.DMA((4,))],
        compiler_params=_cp(("arbitrary",)),
    )(x1, gu, dx2, w_o, w_ga, w_gb, w_dn, gate1, a2, sh2, gate2)


def _wgrad(name, a, b, gate, w, carry=None):
    s, m = a.shape
    n = b.shape[1]
    tk = min(1024, s)
    nk = s // tk

    def body(a_ref, b_ref, g_ref, w_ref, o_hbm, dg_ref, acc_ref, sem):
        k = pl.program_id(0)

        @pl.when(k == 0)
        def _():
            acc_ref[...] = jnp.zeros_like(acc_ref)

        acc_ref[...] += _mm_tn(a_ref[...], b_ref[...].astype(BF16))

        @pl.when(k == nk - 1)
        def _():
            acc = acc_ref[...]
            dg_ref[...] = jnp.zeros_like(dg_ref)
            dg_ref[0:1, :] = jnp.sum(acc * w_ref[...].astype(F32), axis=0, keepdims=True)
            acc_ref[...] = acc * g_ref[...]
            cp = pltpu.make_async_copy(acc_ref, o_hbm, sem)
            cp.start()
            cp.wait()

    return _pcall(body, [a, b, gate, w], name=name, grid=(nk,),
                  out_shape=[jax.ShapeDtypeStruct((m, n), F32), jax.ShapeDtypeStruct((8, n), F32)],
                  in_specs=[pl.BlockSpec((tk, m), lambda k: (k, 0)), pl.BlockSpec((tk, n), lambda k: (k, 0)),
                            _row((1, n)), _row((m, n))],
                  out_specs=[pl.BlockSpec(memory_space=pl.ANY), _row((8, n))],
                  scratch_shapes=[pltpu.VMEM((m, n), F32), pltpu.SemaphoreType.DMA], carry=carry)


def _wgrad_gate_up(h2, dgu, carry=None):
    s = h2.shape[0]
    tk = min(1024, s)
    nk = s // tk
    n = dgu.shape[1]
    nj = n // 4

    def body(a_ref, b_ref, o_hbm, acc_ref, sems):
        k = pl.program_id(0)

        @pl.when(k == 0)
        def _():
            acc_ref[...] = jnp.zeros_like(acc_ref)

        acc_ref[...] += _mm_tn(a_ref[...], b_ref[...])

        @pl.when(k == nk - 1)
        def _():
            cps = [pltpu.make_async_copy(acc_ref.at[:, pl.ds(j * nj, nj)], o_hbm.at[j], sems.at[j]) for j in range(4)]
            for cp in cps:
                cp.start()
            for cp in cps:
                cp.wait()

    return _pcall(body, [h2, dgu], name="wgrad_gate_up", grid=(nk,),
                  out_shape=[jax.ShapeDtypeStruct((4, D, nj), F32)],
                  in_specs=[pl.BlockSpec((tk, D), lambda k: (k, 0)), pl.BlockSpec((tk, n), lambda k: (k, 0))],
                  out_specs=[pl.BlockSpec(memory_space=pl.ANY)],
                  scratch_shapes=[pltpu.VMEM((D, n), F32), pltpu.SemaphoreType.DMA((4,))], carry=carry)


def _wgrad_in_t(h1, pieces, carry=None):
    s = h1.shape[0]
    tk = min(1024, s)
    nk = s // tk

    def body(a_ref, dq_ref, dkv_ref, dz_ref, dxbc_ref, ddt_ref, o_hbm, acc_ref, tr_ref, sem):
        k = pl.program_id(0)

        @pl.when(k == 0)
        def _():
            acc_ref[...] = jnp.zeros_like(acc_ref)

        dproj = jnp.concatenate([dq_ref[...], dkv_ref[...], dz_ref[...], dxbc_ref[...], ddt_ref[...]], axis=1)
        acc_ref[...] += _mm_tn(a_ref[...], dproj)

        @pl.when(k == nk - 1)
        def _():
            for j in range(PROJ_W // 128):
                tr_ref[j * 128:(j + 1) * 128, :] = acc_ref[:, j * 128:(j + 1) * 128].T
            cp = pltpu.make_async_copy(tr_ref, o_hbm, sem)
            cp.start()
            cp.wait()

    return _pcall(body, [h1] + list(pieces), name="wgrad_in", grid=(nk,),
                  out_shape=[jax.ShapeDtypeStruct((PROJ_W, D), F32)],
                  in_specs=[pl.BlockSpec((tk, p.shape[1]), lambda k: (k, 0)) for p in [h1] + list(pieces)],
                  out_specs=[pl.BlockSpec(memory_space=pl.ANY)],
                  scratch_shapes=[pltpu.VMEM((D, PROJ_W), F32), pltpu.VMEM((PROJ_W, D), F32), pltpu.SemaphoreType.DMA],
                  carry=carry)


_SMALL = ["ada_b", "norm1", "conv_w", "conv_b", "dt_bias", "A_log", "D_skip", "sinks", "attn_out_norm",
          "ssm_out_norm", "norm2", "rel_bias", "final_norm"]


def _small_grad(name, gs, chip):
    if name == "ada_b":
        return jnp.concatenate([gs[j:j + 1, :] for j in range(6)], axis=1)
    if name == "conv_w":
        full = gs[7:11, :]
        out = full[:, 0:256]
        for j in range(1, 4):
            out = jnp.where(chip == j, full[:, j * 256:(j + 1) * 256], out)
        return out
    row, width = {"norm1": (6, D), "conv_b": (11, D), "norm2": (12, D), "final_norm": (13, D),
                  "attn_out_norm": (14, QW), "ssm_out_norm": (15, SW), "dt_bias": (16, NH), "A_log": (17, NH),
                  "D_skip": (18, NH), "sinks": (19, NH), "rel_bias": (24, NH)}[name]
    rows = NBUCKET if name == "rel_bias" else 1
    return gs[row:row + rows, 0:width]


def _small_update(small_all, where, ws, ms, vs):
    n = len(_SMALL)

    def body(where_ref, sa_ref, *refs):
        w_refs, m_refs, v_refs, outs = refs[:n], refs[n:2 * n], refs[2 * n:3 * n], refs[3 * n:]
        gs = sa_ref[0]
        for b in range(1, 8):
            gs = gs + sa_ref[b]
        chip = where_ref[1]
        for i, name in enumerate(_SMALL):
            g = _small_grad(name, gs, chip)
            lead = (0,) if name == "conv_w" else ()
            d, mo, vo = _adamw(w_refs[i][lead + (...,)], g, m_refs[i][lead + (...,)], v_refs[i][lead + (...,)])
            for k, val in enumerate((g, d, mo, vo)):
                outs[k * n + i][lead + (...,)] = val
        outs[4 * n][...] = gs[20:21, 0:128]

    shapes = [jax.ShapeDtypeStruct(w.shape, F32) for w in ws]
    vmem = pl.BlockSpec(memory_space=pltpu.VMEM)
    res = pl.pallas_call(
        body, name="small_update", out_shape=shapes * 4 + [jax.ShapeDtypeStruct((1, 128), F32)],
        in_specs=[pl.BlockSpec(memory_space=pltpu.SMEM)] + [vmem] * (1 + 3 * n), out_specs=[vmem] * (4 * n + 1),
    )(where, small_all, *ws, *ms, *vs)
    return [res[k * n:(k + 1) * n] for k in range(4)], res[4 * n][0, 0]


def _add_half(name, g, got, where, by_cols=False):
    rr, cc = got.shape[1:]
    if by_cols:
        mine = pl.BlockSpec((None, rr, cc), lambda i, w_ref: (i, 0, w_ref[0]))
    else:
        mine = pl.BlockSpec((None, None, rr, cc), lambda i, w_ref: (i, w_ref[0], 0, 0))

    def body(w_ref, g_ref, r_ref, o_ref, own_ref):
        s = g_ref[...] + r_ref[...]
        o_ref[...] = s.astype(BF16)

        @pl.when(pl.program_id(0) == w_ref[1])
        def _():
            own_ref[...] = s

    spec = pl.BlockSpec((None, rr, cc), lambda i, w_ref: (i, 0, 0))
    return _pcall(body, [where, g, got], name=name, grid=(4,), nprefetch=1,
                  out_shape=[jax.ShapeDtypeStruct(got.shape, BF16), jax.ShapeDtypeStruct((rr, cc), F32)],
                  in_specs=[mine, spec],
                  out_specs=[spec, pl.BlockSpec((rr, cc), lambda i, w_ref: (0, 0))])


def _add_chips(name, own, got):
    rr, cc = own.shape
    tr = rr // 2 if rr % 32 == 0 else rr

    def body(s_ref, r_ref, o_ref):
        o_ref[...] = ((s_ref[...] + r_ref[0].astype(F32)) + r_ref[1].astype(F32)) + r_ref[2].astype(F32)

    spec = pl.BlockSpec((tr, cc), lambda i: (i, 0))
    return _pcall(body, [own, got], name=name, grid=(rr // tr,), out_shape=[jax.ShapeDtypeStruct((rr, cc), F32)],
                  in_specs=[spec, pl.BlockSpec((3, tr, cc), lambda i: (0, i, 0))], out_specs=[spec])[0]


def _adamw_halves(name, mine, got, w, m, v, where, by_cols=False):
    rr, cc = mine.shape

    def body(w_ref_, t_ref, r_ref, w_ref, m_ref, v_ref, g_ref, d_ref, mo_ref, vo_ref):
        g = jnp.where(pl.program_id(0) == w_ref_[0], t_ref[...], r_ref[...])
        g_ref[...] = g
        d_ref[...], mo_ref[...], vo_ref[...] = _adamw(w_ref[...], g, m_ref[...], v_ref[...])

    if by_cols:
        grid = (2, 1)
        half = pl.BlockSpec((rr, cc), lambda h, i, w_ref_: (0, 0))
        full = pl.BlockSpec((rr, cc), lambda h, i, w_ref_: (0, h))
    else:
        tr = rr // 2
        grid = (2, 2)
        half = pl.BlockSpec((tr, cc), lambda h, i, w_ref_: (i, 0))
        full = pl.BlockSpec((None, tr, cc), lambda h, i, w_ref_: (0, 2 * h + i, 0))
    return _pcall(body, [where, mine, got, w, m, v], name=name, grid=grid, nprefetch=1,
                  out_shape=[jax.ShapeDtypeStruct(w.shape, F32)] * 4,
                  in_specs=[half, half, full, full, full], out_specs=[full] * 4)


def _bias_table(rel_bias, bucket, mask):
    def body(rb_ref, bk_ref, mk_ref, o_ref):
        bk = bk_ref[...]
        valid = mk_ref[...] > 0
        for h in range(NH):
            acc = jnp.zeros((BLK, 2 * BLK), F32)
            for b in range(NBUCKET):
                acc = jnp.where(bk == b, rb_ref[b, h], acc)
            o_ref[h] = jnp.where(valid, acc, NEG)

    vmem = pl.BlockSpec(memory_space=pltpu.VMEM)
    return pl.pallas_call(
        body, name="bias_table", out_shape=jax.ShapeDtypeStruct((NH, BLK, 2 * BLK), F32),
        in_specs=[pl.BlockSpec(memory_space=pltpu.SMEM), vmem, vmem], out_specs=vmem,
    )(rel_bias, bucket, mask)


def _pack_small(dsh1, p1, dsh2, p2, dg1a, dg1b, dg2, norm1, norm2, scale1, scale2, dcw, dcb, dfn,
                dnw_attn, dnw_ssm, dhd, av, dsink, drel, loss_acc):
    def body(dsh1_ref, p1_ref, dsh2_ref, p2_ref, dg1a_ref, dg1b_ref, dg2_ref, n1_ref, n2_ref, s1_ref, s2_ref,
             dcw_ref, dcb_ref, dfn_ref, da_ref, ds_ref, dhd_ref, av_ref, dsink_ref, drel_ref, loss_ref, o_ref):
        o_ref[...] = jnp.zeros_like(o_ref)
        p1v, p2v = p1_ref[0:1, :], p2_ref[0:1, :]
        o_ref[0:1, :] = dsh1_ref[0:1, :]
        o_ref[1:2, :] = p1v * n1_ref[...]
        o_ref[2:3, :] = dg1a_ref[0:1, :] + dg1b_ref[0:1, :]
        o_ref[3:4, :] = dsh2_ref[0:1, :]
        o_ref[4:5, :] = p2v * n2_ref[...]
        o_ref[5:6, :] = dg2_ref[0:1, :]
        o_ref[6:7, :] = p1v * (1.0 + s1_ref[...])
        o_ref[7:11, :] = dcw_ref[0:4, :]
        o_ref[11:12, :] = dcb_ref[0:1, :]
        o_ref[12:13, :] = p2v * (1.0 + s2_ref[...])
        o_ref[13:14, :] = dfn_ref[0:1, :]
        o_ref[14:15, 0:QW] = da_ref[0:1, :]
        o_ref[15:16, 0:SW] = ds_ref[0:1, :]
        o_ref[16:17, 0:128] = dhd_ref[0:1, :]
        o_ref[17:18, 0:128] = dhd_ref[1:2, :] * av_ref[...]
        o_ref[18:19, 0:128] = dhd_ref[2:3, :]
        o_ref[19:20, 0:128] = dsink_ref[0:1, :]
        o_ref[20:21, 0:128] = loss_ref[0:1, :]
        o_ref[24:56, 0:128] = drel_ref[...]

    return pl.pallas_call(body, name="pack_small", out_shape=jax.ShapeDtypeStruct((56, D), F32))(
        dsh1, p1, dsh2, p2, dg1a, dg1b, dg2, norm1, norm2, scale1, scale2, dcw, dcb, dfn,
        dnw_attn, dnw_ssm, dhd, av, dsink, drel, loss_acc)


def _pad_row(a, rows=1):
    return jnp.pad(a.reshape(rows, -1), ((0, 0), (0, D - a.size // rows)))


def kernel(x, c, ada_w, ada_b, norm1, w_in, conv_w, conv_b, dt_bias, A_log, D_skip, sinks, attn_out_norm, ssm_out_norm, w_o, norm2, w_gate_up, w_down, rel_bias, final_norm, loss_target, m_ada_w, m_ada_b, m_norm1, m_w_in, m_conv_w, m_conv_b, m_dt_bias, m_A_log, m_D_skip, m_sinks, m_attn_out_norm, m_ssm_out_norm, m_w_o, m_norm2, m_w_gate_up, m_w_down, m_rel_bias, m_final_norm, v_ada_w, v_ada_b, v_norm1, v_w_in, v_conv_w, v_conv_b, v_dt_bias, v_A_log, v_D_skip, v_sinks, v_attn_out_norm, v_ssm_out_norm, v_w_o, v_norm2, v_w_gate_up, v_w_down, v_rel_bias, v_final_norm):
    xi, yi, ci = lax.axis_index("x"), lax.axis_index("y"), lax.axis_index("c")
    chip = 2 * xi + yi
    me = 4 * xi + 2 * yi + ci
    where = jnp.stack([ci, chip]).astype(jnp.int32)
    xs2, tgt = x[0], loss_target[0]

    first = jnp.concatenate([c, _pad_row(conv_w[0], CK), jnp.zeros((3, D), F32)], axis=0)
    w_in_t, m_w_in_t, v_w_in_t = w_in[0].T, m_w_in[0].T, v_w_in[0].T
    w_in_b, w_o_b, w_dn_b = w_in_t.astype(BF16), w_o[0].astype(BF16), w_down[0].astype(BF16)
    w_gu_b = w_gate_up[0].astype(BF16)
    first_all = _exchange("gather_cond", _gather8_carry(first))[0]
    c_all = first_all[:, 0, :]
    cw_full = jnp.concatenate([first_all[2 * j, 1:1 + CK, 0:256] for j in range(4)], axis=1)

    hw = D // 2
    fetch_half = _Carry(
        [w_in_b], [jax.ShapeDtypeStruct((4, w_in_b.shape[0], hw), BF16)],
        lambda x_, y_, c_: [(f, 0, (slice(None), pl.ds(c_ * hw, hw)), 0, 2 * x_ + y_) for f in [None] + _CHIPS3])
    ncol = ada_w.shape[2]
    mod_cols, w_half = _ada_fwd(c_all, ada_w[0], lax.dynamic_slice(ada_b, (0, chip * ncol), (1, ncol)),
                                carry=fetch_half)
    to_other = _Carry([w_half[None]], [jax.ShapeDtypeStruct((1,) + w_half.shape, BF16)],
                      lambda x_, y_, c_: [(_SIBLING, 0, 0, 0, 0)])
    mod_all, w_other = _exchange("gather_mod", _merge(_gather_chips_carry([mod_cols]), to_other))
    w_lo = jnp.where(ci == 0, w_half, w_other[0])
    w_hi = jnp.where(ci == 0, w_other[0], w_half)
    w_in_f = jnp.pad(jnp.concatenate([w_lo, w_hi], axis=2).reshape(IN_W, D), ((0, PROJ_W - IN_W), (0, 0)))
    mod = lax.dynamic_slice(jnp.transpose(mod_all, (1, 0, 2)).reshape(8, 4 * ncol), (me, 0), (1, 4 * ncol))
    shift1, scale1, gate1, shift2, scale2, gate2 = [mod[:, j * D:(j + 1) * D] for j in range(6)]
    a1 = norm1 * (1.0 + scale1)
    a2 = norm2 * (1.0 + scale2)

    hdn = DFF // 8
    q, kv, z, xbc, dtr, w_o_g, w_dna_g = _in_proj_fwd(xs2, a1, shift1, w_in_f,
                                                      carry=_gather_chips_carry([w_o_b, w_dn_b[0:hdn]]))
    w_o_f = w_o_g.reshape(D, D)
    bucket, mask = _attn_geometry()
    bucket = jnp.asarray(bucket)
    bias = _bias_table(rel_bias, bucket, jnp.asarray(mask.astype(np.int32)))
    sinks1 = sinks[0]
    ya, w_ga_g = _attn_fwd(q, kv, bias, sinks1, attn_out_norm, carry=_gather_chips_carry([w_gu_b[0:D // 2]]))
    cw8 = jnp.concatenate([cw_full, jnp.zeros((4, XBCW), F32)], axis=0)
    dtb = _pad_row(dt_bias)[:, 0:128]
    av = _pad_row(-jnp.exp(A_log))[:, 0:128]
    dk = jnp.repeat(D_skip, HD, axis=1)
    ys, hs, pre, w_gb_g, w_dnb_g = _ssd_fwd(z, xbc, dtr, cw8, conv_b, dtb, av, dk, ssm_out_norm,
                                            carry=_gather_chips_carry([w_gu_b[D // 2:D], w_dn_b[hdn:2 * hdn]]))
    w_dn_f = jnp.stack([w_dna_g, w_dnb_g], axis=1).reshape(DFF, D)
    fn = final_norm[None, :]
    x1, gu, dx2, loss_acc, dfn = _mlp_fwd(xs2, ya, ys, tgt, w_o_f, w_ga_g, w_gb_g, w_dn_f, gate1, a2, shift2, gate2, fn)

    def to_sibling(p):
        return _Carry([p], [jax.ShapeDtypeStruct((4,) + p.shape[2:], F32)],
                      lambda x_, y_, c_: [(_SIBLING, 0, (j, 1 - c_), 0, j) for j in range(4)])

    def to_chips(s4):
        return _Carry([s4], [jax.ShapeDtypeStruct((3,) + s4.shape[1:], s4.dtype)],
                      lambda x_, y_, c_: [(f, 0, jnp.bitwise_xor(2 * x_ + y_, k + 1), 0, k) for k, f in enumerate(_CHIPS3)])

    def back(t):
        return _Carry([t[None]], [jax.ShapeDtypeStruct((1,) + t.shape, F32)], lambda x_, y_, c_: [(_SIBLING, 0, 0, 0, 0)])

    dx1, dya, dys, act, dgu, h2, dsh2, p2 = _mlp_bwd(x1, gu, dx2, w_o_f, w_ga_g, w_gb_g, w_dn_f, gate1, a2, shift2, gate2)
    p_gu = _wgrad_gate_up(h2, dgu)[0].reshape(4, 2, D // 2, 2 * DFF // 4)
    dq, dkv, dbias, dsink, dnw_attn, g_dn, dg2, got1_gu = _attn_bwd(
        q, kv, dya, bias, sinks1, attn_out_norm, act, dx2, gate2, w_dn_f, carry=to_sibling(p_gu))
    p_dn = g_dn.reshape(4, 2, DFF // 8, D)
    drel = _rel_bias_grad(dbias, bucket)
    s4_gu, own_gu = _add_half("rs_add_half_gu", p_gu, got1_gu, where)
    dz, dxbc, ddt, dcw, dcb, dnw_ssm, dhd, got2_gu, got1_dn = _ssd_bwd(
        z, xbc, pre, dtr, dys, hs, cw8, dtb, av, dk, ssm_out_norm, carry=_merge(to_chips(s4_gu), to_sibling(p_dn)))
    mine_gu = _add_chips("rs_add_chips_gu", own_gu, got2_gu)
    s4_dn, own_dn = _add_half("rs_add_half_dn", p_dn, got1_dn, where)
    grad_x, h1, dsh1, p1 = _in_proj_bwd(xs2, dx1, a1, shift1, w_in_f, dq, dkv, dz, dxbc, ddt)
    g_in_t, got2_dn, got3_gu = _wgrad_in_t(h1, [dq, dkv, dz, dxbc, ddt],
                                           carry=_merge(to_chips(s4_dn), back(mine_gu)))
    mine_dn = _add_chips("rs_add_chips_dn", own_dn, got2_dn)
    p_in = g_in_t[0:IN_W].reshape(4, IN_W // 4, D)

    def to_sibling_cols(p):
        return _Carry([p], [jax.ShapeDtypeStruct(p.shape[:2] + (D // 2,), F32)],
                      lambda x_, y_, c_: [(_SIBLING, 0, (j, slice(None), pl.ds((1 - c_) * (D // 2), D // 2)), 0, j)
                                          for j in range(4)])

    g_oa, dg1a, got1_in, got3_dn = _wgrad("wgrad_o_attn", ya, dx1, gate1, w_o_f[0:QW],
                                          carry=_merge(to_sibling_cols(p_in), back(mine_dn)))
    s4_in, own_in = _add_half("rs_add_half_in", p_in, got1_in, where, by_cols=True)
    g_os, dg1b, got2_in = _wgrad("wgrad_o_ssm", ys, dx1, gate1, w_o_f[QW:D], carry=to_chips(s4_in))
    mine_in = _add_chips("rs_add_chips_in", own_in, got2_in)
    p_o = jnp.concatenate([g_oa, g_os], axis=0).reshape(4, 2, D // 8, D)

    small = _pack_small(dsh1, p1, dsh2, p2, dg1a, dg1b, dg2, norm1, norm2, scale1, scale2, dcw, dcb, dfn,
                        dnw_attn, dnw_ssm, dhd, av, dsink, drel, loss_acc)
    small_all, got1_o, got3_in = _exchange(
        "gather_small", _merge(_gather8_carry(small), to_sibling(p_o), back(mine_in)))
    s4_o, own_o = _add_half("rs_add_half_o", p_o, got1_o, where)
    mine_o = _add_chips("rs_add_chips_o", own_o, _exchange("rs_chips_o", to_chips(s4_o))[0])
    got3_o = _exchange("rs_back_o", back(mine_o))[0]
    small_res, loss = _small_update(
        small_all, where,
        [ada_b, norm1, conv_w, conv_b, dt_bias, A_log, D_skip, sinks, attn_out_norm, ssm_out_norm, norm2, rel_bias,
         final_norm[None, :]],
        [m_ada_b, m_norm1, m_conv_w, m_conv_b, m_dt_bias, m_A_log, m_D_skip, m_sinks, m_attn_out_norm,
         m_ssm_out_norm, m_norm2, m_rel_bias, m_final_norm[None, :]],
        [v_ada_b, v_norm1, v_conv_w, v_conv_b, v_dt_bias, v_A_log, v_D_skip, v_sinks, v_attn_out_norm,
         v_ssm_out_norm, v_norm2, v_rel_bias, v_final_norm[None, :]])
    small_out = [dict(zip(_SMALL, r)) for r in small_res]
    for r in small_out:
        r["final_norm"] = r["final_norm"][0]

    dmod_all = small_all[:, 0:6, :].reshape(8, 6 * D)
    dmod_loc = lax.dynamic_slice(dmod_all, (0, chip * ncol), (8, ncol))
    ada_out = _ada_bwd_adamw(c_all.T, dmod_loc, ada_w[0], m_ada_w[0], v_ada_w[0])

    big_gu = _adamw_halves("adamw_gate_up", mine_gu, got3_gu[0], w_gate_up, m_w_gate_up, v_w_gate_up, where)
    big_dn = _adamw_halves("adamw_down", mine_dn, got3_dn[0], w_down, m_w_down, v_w_down, where)
    big_o = _adamw_halves("adamw_o", mine_o, got3_o[0], w_o, m_w_o, v_w_o, where)
    big_in = [o.T[None] for o in _adamw_halves("adamw_in", mine_in, got3_in[0], w_in_t, m_w_in_t, v_w_in_t, where,
                                               by_cols=True)]
    big = [big_in, big_o, big_gu, big_dn]

    order = ["ada_w", "ada_b", "norm1", "w_in", "conv_w", "conv_b", "dt_bias", "A_log", "D_skip", "sinks",
             "attn_out_norm", "ssm_out_norm", "w_o", "norm2", "w_gate_up", "w_down", "rel_bias", "final_norm"]
    bigname = {"w_in": 0, "w_o": 1, "w_gate_up": 2, "w_down": 3}
    res = [loss, grad_x[None]]
    for kind in range(4):
        for nm in order:
            if nm == "ada_w":
                res.append(ada_out[kind][None])
            elif nm in bigname:
                res.append(big[bigname[nm]][kind])
            else:
                res.append(small_out[kind][nm])
    return tuple(res)
```

```python
import numpy as np
import jax
import jax.numpy as jnp
from jax import lax
from jax.experimental import pallas as pl
from jax.experimental.pallas import tpu as pltpu

F32, BF16 = jnp.float32, jnp.bfloat16
HI = lax.Precision.HIGHEST

D = 1024
QW, KVW = 512, 128
NH, HD, NKV = 8, 64, 2
SW = 512
NST = 128
XBCW = 1024
CK = 4
BLK = 128
DFF = 2816
IN_W = 2312
PROJ_W = 2432
EPS = 1e-6
NEG = -1e30
NBUCKET = 32

B1, B2, LR, AEPS, WD, STEP = 0.9, 0.999, 0.001, 1e-08, 0.01, 10

VMEM_LIMIT = 56 * 1024 * 1024

_NT = (((1,), (1,)), ((), ()))
_TN = (((0,), (0,)), ((), ()))


def _mm(a, b):
    return jnp.dot(a, b, preferred_element_type=F32)


def _mm_nt(a, b):
    return lax.dot_general(a, b, _NT, preferred_element_type=F32)


def _mm_tn(a, b):
    return lax.dot_general(a, b, _TN, preferred_element_type=F32)


def _mm_hi(a, b):
    return jnp.dot(a, b, preferred_element_type=F32, precision=HI)


def _split3(x):
    hi = x.astype(BF16)
    r = x - hi.astype(F32)
    mid = r.astype(BF16)
    lo = (r - mid.astype(F32)).astype(BF16)
    return hi, mid, lo


def _sel_r(x, e):
    hi, mid, lo = _split3(x)
    return (_mm(hi, e) + _mm(mid, e)) + _mm(lo, e)


def _sel_l(e, x):
    hi, mid, lo = _split3(x)
    return (_mm(e, hi) + _mm(e, mid)) + _mm(e, lo)


def _sig(x):
    return 1.0 / (1.0 + jnp.exp(-x))


def _cp(sem):
    return pltpu.CompilerParams(dimension_semantics=sem, vmem_limit_bytes=VMEM_LIMIT)


def _row(shape):
    nd = len(shape)
    return pl.BlockSpec(shape, lambda *_: (0,) * nd)


def _adamw(w, g, m, v):
    m = B1 * m + (1.0 - B1) * g
    v = B2 * v + (1.0 - B2) * (g * g)
    m_hat = m / (1.0 - B1 ** STEP)
    v_hat = v / (1.0 - B2 ** STEP)
    delta = -LR * (m_hat / (jnp.sqrt(v_hat) + AEPS) + WD * w)
    return delta, m, v


class _Carry:
    def __init__(self, inps, outs, copies):
        self.inps, self.outs, self.copies = list(inps), list(outs), copies
        self.n = len(copies(0, 0, 0))

    def descriptors(self, in_refs, out_refs, send_sems, recv_sems):
        x, y, c = lax.axis_index("x"), lax.axis_index("y"), lax.axis_index("c")
        out = []
        for j, (flip, a, si, o, di) in enumerate(self.copies(x, y, c)):
            if flip is None:
                out.append(pltpu.make_async_copy(in_refs[a].at[si], out_refs[o].at[di], send_sems.at[j]))
            else:
                fx, fy, fc = flip
                peer = (1 - x if fx else x, 1 - y if fy else y, 1 - c if fc else c)
                out.append(pltpu.make_async_remote_copy(
                    src_ref=in_refs[a].at[si], dst_ref=out_refs[o].at[di],
                    send_sem=send_sems.at[j], recv_sem=recv_sems.at[j],
                    device_id=peer, device_id_type=pl.DeviceIdType.MESH))
        return out


def _pcall(body, args, *, name, grid, in_specs, out_specs, out_shape, scratch_shapes=(), sem=None, nprefetch=0,
           carry=None):
    out_shape, out_specs = list(out_shape), list(out_specs)
    in_specs, scratch_shapes = list(in_specs), list(scratch_shapes)
    nin, nout, nscr = len(in_specs), len(out_shape), len(scratch_shapes)
    run = body
    if carry is not None:
        ncin, ncout = len(carry.inps), len(carry.outs)
        hbm = pl.BlockSpec(memory_space=pl.ANY)

        def run(*refs):
            pre, r = refs[:nprefetch], refs[nprefetch:]
            ins, cins = r[:nin], r[nin:nin + ncin]
            r = r[nin + ncin:]
            outs, couts = r[:nout], r[nout:nout + ncout]
            r = r[nout + ncout:]
            scr, (send_sems, recv_sems) = r[:nscr], r[nscr:]
            first = pl.program_id(0) == 0
            last = pl.program_id(0) == grid[0] - 1
            for ax in range(1, len(grid)):
                first = jnp.logical_and(first, pl.program_id(ax) == 0)
                last = jnp.logical_and(last, pl.program_id(ax) == grid[ax] - 1)

            @pl.when(first)
            def _():
                for d in carry.descriptors(cins, couts, send_sems, recv_sems):
                    d.start()

            body(*pre, *ins, *outs, *scr)

            @pl.when(last)
            def _():
                for d in carry.descriptors(cins, couts, send_sems, recv_sems):
                    d.wait()

        in_specs = in_specs + [hbm] * ncin
        out_specs = out_specs + [hbm] * ncout
        out_shape = out_shape + carry.outs
        scratch_shapes = scratch_shapes + [pltpu.SemaphoreType.DMA((carry.n,)), pltpu.SemaphoreType.DMA((carry.n,))]
        args = list(args) + carry.inps
    if sem is None:
        sem = ("arbitrary",) * len(grid)
    if nprefetch:
        kw = dict(grid_spec=pltpu.PrefetchScalarGridSpec(num_scalar_prefetch=nprefetch, grid=grid, in_specs=in_specs,
                                                         out_specs=out_specs, scratch_shapes=scratch_shapes))
    else:
        kw = dict(grid=grid, in_specs=in_specs, out_specs=out_specs, scratch_shapes=scratch_shapes)
    res = pl.pallas_call(run, name=name, out_shape=out_shape, compiler_params=_cp(sem), **kw)(*args)
    return list(res)


def _merge(*carries):
    inps, outs, offs = [], [], []
    for cr in carries:
        offs.append((len(inps), len(outs)))
        inps += cr.inps
        outs += cr.outs

    def copies(x, y, c):
        return [(f, a + io, si, o + oo, di) for cr, (io, oo) in zip(carries, offs) for f, a, si, o, di in cr.copies(x, y, c)]

    return _Carry(inps, outs, copies)


def _exchange(name, carry):
    return _pcall(lambda: None, [], name=name, grid=(1,), in_specs=[], out_specs=[], out_shape=[], carry=carry)


_ALL7 = [(f >> 2 & 1, f >> 1 & 1, f & 1) for f in range(1, 8)]
_CHIPS3 = [(0, 1, 0), (1, 0, 0), (1, 1, 0)]
_SIBLING = (0, 0, 1)


def _gather8_carry(blk):
    def copies(x, y, c):
        me = 4 * x + 2 * y + c
        return [(None, 0, 0, 0, me)] + [(f, 0, 0, 0, me) for f in _ALL7]

    return _Carry([blk[None]], [jax.ShapeDtypeStruct((8,) + blk.shape, blk.dtype)], copies)


def _gather_chips_carry(blks):
    def copies(x, y, c):
        chip = 2 * x + y
        return [(f, a, 0, a, chip) for a in range(len(blks)) for f in [None] + _CHIPS3]

    return _Carry([b[None] for b in blks], [jax.ShapeDtypeStruct((4,) + b.shape, b.dtype) for b in blks], copies)


def _ada_fwd(c_all, w_loc, b_loc, carry=None):
    n = w_loc.shape[1]
    tn = 512

    def body(c_ref, w_ref, b_ref, o_ref):
        cv = c_ref[...]
        cond = cv * _sig(cv)
        o_ref[...] = _mm_hi(cond, w_ref[...]) + b_ref[...]

    return _pcall(
        body, [c_all, w_loc, b_loc], name="ada_fwd", grid=(n // tn,),
        out_shape=[jax.ShapeDtypeStruct((8, n), F32)],
        in_specs=[_row((8, D)), pl.BlockSpec((D, tn), lambda j: (0, j)), pl.BlockSpec((1, tn), lambda j: (0, j))],
        out_specs=[pl.BlockSpec((8, tn), lambda j: (0, j))], carry=carry)


def _ada_bwd_adamw(c_all_t, dmod_loc, w, m, v, carry=None):
    n = w.shape[1]
    tn = 512

    def body(ct_ref, dm_ref, w_ref, m_ref, v_ref, g_ref, d_ref, mo_ref, vo_ref):
        ct = ct_ref[...]
        cond = ct * _sig(ct)
        dm = dm_ref[...]
        g = cond[:, 0:1] * dm[0:1, :]
        for b in range(1, 8):
            g = g + cond[:, b:b + 1] * dm[b:b + 1, :]
        g_ref[...] = g
        d_ref[...], mo_ref[...], vo_ref[...] = _adamw(w_ref[...], g, m_ref[...], v_ref[...])

    wspec = pl.BlockSpec((D, tn), lambda j: (0, j))
    return _pcall(
        body, [c_all_t, dmod_loc, w, m, v], name="ada_bwd_adamw", grid=(n // tn,),
        out_shape=[jax.ShapeDtypeStruct((D, n), F32)] * 4,
        in_specs=[_row((D, 8)), pl.BlockSpec((8, tn), lambda j: (0, j)), wspec, wspec, wspec],
        out_specs=[wspec] * 4, carry=carry)


def _in_proj_fwd(x, a1, sh1, w_in, carry=None):
    s = x.shape[0]
    tm = 512

    def body(x_ref, a_ref, s_ref, w_ref, q_ref, kv_ref, z_ref, xbc_ref, dt_ref):
        def norm(rows):
            xv = x_ref[rows, :]
            r = lax.rsqrt(jnp.mean(xv * xv, axis=-1, keepdims=True) + EPS)
            return (xv * r * a_ref[...] + s_ref[...]).astype(BF16)

        def project(rows, h):
            p = _mm_nt(h, w_ref[...])
            q_ref[rows, :] = p[:, 0:512].astype(BF16)
            kv_ref[rows, :] = p[:, 512:768].astype(BF16)
            z_ref[rows, :] = p[:, 768:1280]
            xbc_ref[rows, :] = p[:, 1280:2304]
            dt_ref[rows, :] = p[:, 2304:2432]

        r0, r1 = slice(0, tm // 2), slice(tm // 2, tm)
        h0 = norm(r0)
        project(r0, h0)
        project(r1, norm(r1))

    def tok(w):
        return pl.BlockSpec((tm, w), lambda i: (i, 0))

    return _pcall(
        body, [x, a1, sh1, w_in], name="in_proj_fwd", grid=(s // tm,),
        out_shape=[jax.ShapeDtypeStruct((s, QW), BF16), jax.ShapeDtypeStruct((s, 2 * KVW), BF16),
                   jax.ShapeDtypeStruct((s, SW), F32), jax.ShapeDtypeStruct((s, XBCW), F32),
                   jax.ShapeDtypeStruct((s, 128), F32)],
        in_specs=[tok(D), _row((1, D)), _row((1, D)), _row((PROJ_W, D))],
        out_specs=[tok(QW), tok(2 * KVW), tok(SW), tok(XBCW), tok(128)], carry=carry)


def _in_proj_bwd(x, dx1, a1, sh1, w_in, dq, dkv, dz, dxbc, ddt, carry=None):
    s = x.shape[0]
    tm = 512

    def body(x_ref, dx1_ref, a_ref, s_ref, w_ref, dq_ref, dkv_ref, dz_ref, dxbc_ref, ddt_ref,
             gx_ref, h_ref, dsh_ref, p_ref):
        i = pl.program_id(0)

        @pl.when(i == 0)
        def _():
            dsh_ref[...] = jnp.zeros_like(dsh_ref)
            p_ref[...] = jnp.zeros_like(p_ref)

        def gather(st):
            rows = st["rows"]
            st["dproj"] = jnp.concatenate([dq_ref[rows, :], dkv_ref[rows, :], dz_ref[rows, :], dxbc_ref[rows, :],
                                           ddt_ref[rows, :]], axis=1)

        def back(st):
            st["dh"] = _mm(st.pop("dproj"), w_ref[...])

        def norm(st):
            rows, dh = st["rows"], st.pop("dh")
            xv = x_ref[rows, :]
            r = lax.rsqrt(jnp.mean(xv * xv, axis=-1, keepdims=True) + EPS)
            xn = xv * r
            a = a_ref[...]
            h_ref[rows, :] = (xn * a + s_ref[...]).astype(BF16)
            st["dsh"] = jnp.sum(dh, axis=0, keepdims=True)
            st["p"] = jnp.sum(dh * xn, axis=0, keepdims=True)
            u = dh * a
            gx_ref[rows, :] = dx1_ref[rows, :] + r * u - xn * (r * jnp.mean(u * xn, axis=-1, keepdims=True))

        g0, g1 = [dict(rows=slice(k * (tm // 2), (k + 1) * (tm // 2))) for k in range(2)]
        for stage, st in [(gather, g0), (back, g0), (gather, g1), (norm, g0), (back, g1), (norm, g1)]:
            stage(st)
        dsh_ref[0:1, :] += g0["dsh"] + g1["dsh"]
        p_ref[0:1, :] += g0["p"] + g1["p"]

    def tok(w):
        return pl.BlockSpec((tm, w), lambda i: (i, 0))

    return _pcall(
        body, [x, dx1, a1, sh1, w_in, dq, dkv, dz, dxbc, ddt], name="in_proj_bwd", grid=(s // tm,),
        out_shape=[jax.ShapeDtypeStruct((s, D), F32), jax.ShapeDtypeStruct((s, D), BF16),
                   jax.ShapeDtypeStruct((8, D), F32), jax.ShapeDtypeStruct((8, D), F32)],
        in_specs=[tok(D), tok(D), _row((1, D)), _row((1, D)), _row((PROJ_W, D)),
                  tok(QW), tok(2 * KVW), tok(SW), tok(XBCW), tok(128)],
        out_specs=[tok(D), tok(D), _row((8, D)), _row((8, D))], carry=carry)


def _attn_geometry():
    dist = np.arange(BLK)[:, None] + BLK - np.arange(2 * BLK)[None, :]
    n = np.maximum(dist, 0)
    max_exact = NBUCKET // 2
    large = max_exact + (np.log(np.maximum(n, 1) / max_exact) / np.log(128 / max_exact)
                         * (NBUCKET - max_exact)).astype(np.int32)
    large = np.minimum(large, NBUCKET - 1)
    bucket = np.where(n < max_exact, n, large).astype(np.int32)
    mask = (dist >= 0) & (dist < 128)
    return bucket, mask


def _attn_heads(is_first, q_blk, kvw, bias_ref, sinks_ref):
    qv = q_blk * 0.125
    col = lax.broadcasted_iota(jnp.int32, (BLK, 2 * BLK), 1)
    first = jnp.where(jnp.logical_and(is_first, col < BLK), NEG, 0.0)
    groups = []
    for g in range(NKV):
        qs = jnp.concatenate([qv[:, (4 * g + r) * HD:(4 * g + r + 1) * HD] for r in range(4)], axis=0)
        kw = kvw[:, g * HD:(g + 1) * HD]
        vw = kvw[:, KVW + g * HD:KVW + (g + 1) * HD]
        sc = _mm_nt(qs, kw)
        pn, ps = [], []
        for r in range(4):
            h = 4 * g + r
            sr = sc[r * BLK:(r + 1) * BLK] + bias_ref[h] + first
            sink = sinks_ref[h]
            m = jnp.maximum(jnp.max(sr, axis=-1, keepdims=True), sink)
            p = jnp.exp(sr - m)
            es = jnp.exp(sink - m)
            inv = 1.0 / (jnp.sum(p, axis=-1, keepdims=True) + es)
            pn.append(p * inv)
            ps.append(es * inv)
        pn = jnp.concatenate(pn, axis=0)
        ps = jnp.concatenate(ps, axis=0)
        o = _mm(pn.astype(BF16), vw)
        groups.append((qs, kw, vw, pn, ps, o))
    return groups


def _unstack_heads(parts):
    return jnp.concatenate([p[r * BLK:(r + 1) * BLK] for p in parts for r in range(4)], axis=1)


NB = 2


def _attn_fwd(q, kv, bias, sinks, nw, carry=None):
    s = q.shape[0]

    def body(q_ref, kvp_ref, kvc_ref, bias_ref, sinks_ref, nw_ref, y_ref):
        t = pl.program_id(0)
        kv3 = jnp.concatenate([kvp_ref[...], kvc_ref[...]], axis=0)
        for sub in range(NB):
            rows = slice(sub * BLK, (sub + 1) * BLK)
            groups = _attn_heads(jnp.logical_and(t == 0, sub == 0), q_ref[rows, :], kv3[sub * BLK:(sub + 2) * BLK],
                                 bias_ref, sinks_ref)
            o = _unstack_heads([g[5] for g in groups])
            r = lax.rsqrt(jnp.mean(o * o, axis=-1, keepdims=True) + EPS)
            y_ref[rows, :] = (o * r * nw_ref[...]).astype(BF16)

    return _pcall(
        body, [q, kv, kv, bias, sinks, nw], name="attn_fwd", grid=(s // (NB * BLK),),
        out_shape=[jax.ShapeDtypeStruct((s, QW), BF16)],
        in_specs=[pl.BlockSpec((NB * BLK, QW), lambda t: (t, 0)),
                  pl.BlockSpec((BLK, 2 * KVW), lambda t: (jnp.maximum(NB * t - 1, 0), 0)),
                  pl.BlockSpec((NB * BLK, 2 * KVW), lambda t: (t, 0)),
                  _row((NH, BLK, 2 * BLK)),
                  pl.BlockSpec(memory_space=pltpu.SMEM),
                  _row((1, QW))],
        out_specs=[pl.BlockSpec((NB * BLK, QW), lambda t: (t, 0))], carry=carry)


def _attn_bwd(q, kv, dya, bias, sinks, nw, act, dx2, gate2, w_dn, carry=None):
    s = q.shape[0]
    nt = s // (NB * BLK)
    npiece = DFF // NB

    def body(q_ref, kvp_ref, kvc_ref, dy_ref, bias_ref, sinks_ref, nw_ref, act_ref, dx2_ref, g2_ref, wdn_ref,
             dq_ref, dkv_ref, dbias_ref, dsink_ref, dnw_ref, gdn_hbm, dg2_ref, carry_ref, held_ref, acc_ref, sem):
        t = pl.program_id(0)

        @pl.when(t == 0)
        def _():
            carry_ref[...] = jnp.zeros_like(carry_ref)
            held_ref[...] = jnp.zeros_like(held_ref)
            dbias_ref[...] = jnp.zeros_like(dbias_ref)
            dsink_ref[...] = jnp.zeros_like(dsink_ref)
            dnw_ref[...] = jnp.zeros_like(dnw_ref)
            acc_ref[...] = jnp.zeros_like(acc_ref)

        def wgrad_piece(sub):
            rows = slice(sub * npiece, (sub + 1) * npiece)
            acc_ref[rows, :] += _mm_tn(act_ref[:, rows], dx2_ref[...].astype(BF16))

        def block(sub, kv3):
            rows = slice(sub * BLK, (sub + 1) * BLK)
            groups = _attn_heads(jnp.logical_and(t == 0, sub == 0), q_ref[rows, :], kv3[sub * BLK:(sub + 2) * BLK],
                                 bias_ref, sinks_ref)
            o = _unstack_heads([g[5] for g in groups])
            r = lax.rsqrt(jnp.mean(o * o, axis=-1, keepdims=True) + EPS)
            dy = dy_ref[rows, :]
            on = o * r
            dnw_ref[0:1, :] += jnp.sum(dy * on, axis=0, keepdims=True)
            u = dy * nw_ref[...]
            do = r * u - on * (r * jnp.mean(u * on, axis=-1, keepdims=True))
            dq_parts, dk_parts, dv_parts = [], [], []
            for g, (qs, kw, vw, pn, ps, og) in enumerate(groups):
                dos = jnp.concatenate([do[:, (4 * g + r_) * HD:(4 * g + r_ + 1) * HD] for r_ in range(4)], axis=0)
                delta = jnp.sum(dos * og, axis=-1, keepdims=True)
                dp = _mm_nt(dos.astype(BF16), vw)
                ds = pn * (dp - delta)
                dsk = ps * delta
                lane = lax.broadcasted_iota(jnp.int32, (1, 128), 1)
                for r_ in range(4):
                    h = 4 * g + r_
                    dbias_ref[h] += ds[r_ * BLK:(r_ + 1) * BLK]
                    dsink_ref[0:1, :] -= jnp.where(lane == h, jnp.sum(dsk[r_ * BLK:(r_ + 1) * BLK]), 0.0)
                dsb = ds.astype(BF16)
                dq_parts.append(_mm(dsb, kw) * 0.125)
                dk_parts.append(_mm_tn(dsb, qs))
                dv_parts.append(_mm_tn(pn.astype(BF16), dos.astype(BF16)))
            dq_ref[rows, :] = _unstack_heads(dq_parts).astype(BF16)
            return jnp.concatenate(dk_parts + dv_parts, axis=1)

        @pl.when(t < nt)
        def _():
            kv3 = jnp.concatenate([kvp_ref[...], kvc_ref[...]], axis=0)
            tail = carry_ref[...]
            for sub in range(NB):
                d = block(sub, kv3)
                done = tail + d[0:BLK]
                if sub == 0:
                    dkv_ref[0:(NB - 1) * BLK, :] = held_ref[...].astype(BF16)
                    dkv_ref[(NB - 1) * BLK:NB * BLK, :] = done.astype(BF16)
                else:
                    held_ref[(sub - 1) * BLK:sub * BLK, :] = done
                tail = d[BLK:2 * BLK]
                wgrad_piece(sub)
            carry_ref[...] = tail

        @pl.when(t == nt)
        def _():
            dkv_ref[0:(NB - 1) * BLK, :] = held_ref[...].astype(BF16)
            dkv_ref[(NB - 1) * BLK:NB * BLK, :] = carry_ref[...].astype(BF16)
            acc = acc_ref[...]
            dg2_ref[...] = jnp.zeros_like(dg2_ref)
            dg2_ref[0:1, :] = jnp.sum(acc * wdn_ref[...].astype(F32), axis=0, keepdims=True)
            acc_ref[...] = acc * g2_ref[...]
            cp = pltpu.make_async_copy(acc_ref, gdn_hbm, sem)
            cp.start()
            cp.wait()

    last = nt - 1
    tile = lambda w: pl.BlockSpec((NB * BLK, w), lambda t: (jnp.minimum(t, last), 0))
    return _pcall(
        body, [q, kv, kv, dya, bias, sinks, nw, act, dx2, gate2, w_dn], name="attn_bwd", grid=(nt + 1,),
        out_shape=[jax.ShapeDtypeStruct((s, QW), BF16), jax.ShapeDtypeStruct((s, 2 * KVW), BF16),
                   jax.ShapeDtypeStruct((NH, BLK, 2 * BLK), F32), jax.ShapeDtypeStruct((NH, 128), F32),
                   jax.ShapeDtypeStruct((8, QW), F32), jax.ShapeDtypeStruct((DFF, D), F32),
                   jax.ShapeDtypeStruct((8, D), F32)],
        in_specs=[tile(QW),
                  pl.BlockSpec((BLK, 2 * KVW), lambda t: (jnp.clip(NB * t - 1, 0, NB * nt - 1), 0)),
                  tile(2 * KVW), tile(QW),
                  _row((NH, BLK, 2 * BLK)),
                  pl.BlockSpec(memory_space=pltpu.SMEM),
                  _row((1, QW)), tile(DFF), tile(D), _row((1, D)), _row((DFF, D))],
        out_specs=[tile(QW),
                   pl.BlockSpec((NB * BLK, 2 * KVW), lambda t: (jnp.maximum(t - 1, 0), 0)),
                   _row((NH, BLK, 2 * BLK)), _row((NH, 128)), _row((8, QW)),
                   pl.BlockSpec(memory_space=pl.ANY), _row((8, D))],
        scratch_shapes=[pltpu.VMEM((BLK, 2 * KVW), F32), pltpu.VMEM(((NB - 1) * BLK, 2 * KVW), F32),
                        pltpu.VMEM((DFF, D), F32), pltpu.SemaphoreType.DMA], carry=carry)


def _rel_bias_grad(dbias, bucket):
    def body(db_ref, bk_ref, o_ref):
        bk = bk_ref[...]
        lane = lax.broadcasted_iota(jnp.int32, (1, 128), 1)
        for b in range(NBUCKET):
            sel = bk == b
            row = jnp.zeros((1, 128), F32)
            for h in range(NH):
                row = row + jnp.where(lane == h, jnp.sum(jnp.where(sel, db_ref[h], 0.0)), 0.0)
            o_ref[b:b + 1, :] = row

    return pl.pallas_call(
        body, name="rel_bias_grad",
        out_shape=jax.ShapeDtypeStruct((NBUCKET, 128), F32),
    )(dbias, bucket)


def _ssd_consts():
    head_of_lane = np.arange(SW) // HD
    expand = (np.arange(128)[:, None] == head_of_lane[None, :]).astype(np.float32)
    tril = np.tril(np.ones((BLK, BLK), np.float32))
    return (jnp.asarray(expand, BF16), jnp.asarray(expand.T.copy(), BF16), jnp.asarray(tril, BF16),
            jnp.asarray(tril.T.copy(), BF16))


def _conv_pre(xc, halo, cw, cb):
    ext = jnp.concatenate([halo, xc], axis=0)
    taps = [xc if k == CK - 1 else pltpu.roll(ext, CK - 1 - k, 0)[8:8 + BLK] for k in range(CK)]
    return cb + sum(cw[k:k + 1, :] * taps[k] for k in range(CK))


def _ssd_chunk(pre, dtr, dtb, av, dkv, ex, tril, h_in):
    sp = _sig(pre)
    xbc = pre * sp
    xs, bm, cm = xbc[:, 0:SW], xbc[:, SW:SW + 2 * NST], xbc[:, SW + 2 * NST:]
    dtin = dtr + dtb
    dt = jnp.maximum(dtin, 0.0) + jnp.log1p(jnp.exp(-jnp.abs(dtin)))
    cs = _sel_l(tril, dt * av)
    cst = cs.T
    dtx = _sel_r(dt, ex)
    csx = _sel_r(cs, ex)
    xdt = xs * dtx
    csl = csx[BLK - 1:BLK, :]
    decx = jnp.exp(csl - csx)
    ecsx = jnp.exp(csx)
    ecl = jnp.exp(csl)
    causal = tril.astype(F32) > 0.5
    ydiag, yoff, cbs, lms = [], [], [], []
    for g in range(2):
        bg = bm[:, g * NST:(g + 1) * NST].astype(BF16)
        cg = cm[:, g * NST:(g + 1) * NST].astype(BF16)
        cb = _mm_nt(cg, bg)
        cbs.append(cb)
        yoff.append(_mm(cg, h_in[:, g * 256:(g + 1) * 256].astype(BF16)))
        for r in range(4):
            h = 4 * g + r
            seg = cs[:, h:h + 1] - cst[h:h + 1, :]
            lm = jnp.where(causal, jnp.exp(jnp.minimum(seg, 0.0)), 0.0)
            lms.append(lm)
            ydiag.append(_mm((cb * lm).astype(BF16), xdt[:, h * HD:(h + 1) * HD].astype(BF16)))
    yoff = jnp.concatenate(yoff, axis=1) * ecsx
    y = jnp.concatenate(ydiag, axis=1) + yoff + dkv * xs
    return dict(pre=pre, sp=sp, xs=xs, bm=bm, cm=cm, dtin=dtin, dt=dt, av=av, cs=cs, cst=cst,
                dtx=dtx, csx=csx, xdt=xdt, decx=decx, ecsx=ecsx, ecl=ecl, causal=causal, cbs=cbs, lms=lms,
                yoff=yoff, y=y)


def _group_mean(t):
    m0 = jnp.mean(t[:, 0:256], axis=-1, keepdims=True)
    m1 = jnp.mean(t[:, 256:512], axis=-1, keepdims=True)
    return jnp.concatenate([jnp.broadcast_to(m0, (t.shape[0], 256)), jnp.broadcast_to(m1, (t.shape[0], 256))], axis=1)


SUBS = 4


def _ssd_fwd(z, xbc, dtr, cw, cb, dtb, av, dk, nw, carry=None):
    s = z.shape[0]
    nc = s // BLK
    tile = SUBS * BLK
    ex, _, tril, _ = _ssd_consts()

    def body(z_ref, xc_ref, xh_ref, dtr_ref, cw_ref, cb_ref, dtb_ref, a_ref, dk_ref, nw_ref, ex_ref, tril_ref,
             y_ref, hs_ref, pre_ref, h_ref):
        t = pl.program_id(0)

        @pl.when(t == 0)
        def _():
            h_ref[...] = jnp.zeros_like(h_ref)

        h_in = h_ref[...]
        for sub in range(SUBS):
            rows = slice(sub * BLK, (sub + 1) * BLK)
            xc = xc_ref[rows, :]
            halo = jnp.where(t == 0, 0.0, xh_ref[...]) if sub == 0 else xc_ref[sub * BLK - 8:sub * BLK, :]
            pre = _conv_pre(xc, halo, cw_ref[...], cb_ref[...])
            pre_ref[rows, :] = pre
            hs_ref[sub] = h_in
            f = _ssd_chunk(pre, dtr_ref[rows, :], dtb_ref[...], a_ref[...], dk_ref[...], ex_ref[...], tril_ref[...], h_in)
            dx = (f["decx"] * f["xdt"]).astype(BF16)
            st = [_mm_tn(f["bm"][:, g * NST:(g + 1) * NST].astype(BF16), dx[:, g * 256:(g + 1) * 256]) for g in range(2)]
            h_in = h_in * f["ecl"] + jnp.concatenate(st, axis=1)
            zv = z_ref[rows, :]
            tg = f["y"] * (zv * _sig(zv))
            r = lax.rsqrt(_group_mean(tg * tg) + EPS)
            y_ref[rows, :] = (tg * r * nw_ref[...]).astype(BF16)
        h_ref[...] = h_in

    cur = lambda w: pl.BlockSpec((tile, w), lambda t: (t, 0))
    return _pcall(
        body, [z, xbc, xbc, dtr, cw, cb, dtb, av, dk, nw, ex, tril], name="ssd_fwd", grid=(s // tile,),
        out_shape=[jax.ShapeDtypeStruct((s, SW), BF16), jax.ShapeDtypeStruct((nc, NST, SW), F32),
                   jax.ShapeDtypeStruct((s, XBCW), F32)],
        in_specs=[cur(SW), cur(XBCW), pl.BlockSpec((8, XBCW), lambda t: (jnp.maximum(t * (tile // 8) - 1, 0), 0)),
                  cur(128), _row((8, XBCW)), _row((1, XBCW)), _row((1, 128)),
                  _row((1, 128)), _row((1, SW)), _row((1, SW)), _row((128, SW)), _row((BLK, BLK))],
        out_specs=[cur(SW), pl.BlockSpec((SUBS, NST, SW), lambda t: (t, 0, 0)), cur(XBCW)],
        scratch_shapes=[pltpu.VMEM((NST, SW), F32)], carry=carry)


def _ssd_bwd(z, xbc, pre_all, dtr, dys, hs, cw, dtb, av, dk, nw, carry=None):
    s = z.shape[0]
    tile = SUBS * BLK
    nt = s // tile
    ex, ext_t, tril, triu = _ssd_consts()

    def body(z_ref, xc_ref, pre_ref, dtr_ref, dy_ref, hs_ref, cw_ref, dtb_ref, a_ref, dk_ref, nw_ref,
             ex_ref, ext_ref, tril_ref, triu_ref,
             dz_ref, dxbc_ref, ddt_ref, dcw_ref, dcb_ref, dnw_ref, dhd_ref, dh_ref, nxt_ref, dd_ref):
        i = pl.program_id(0)

        @pl.when(i == 0)
        def _():
            dh_ref[...] = jnp.zeros_like(dh_ref)
            nxt_ref[...] = jnp.zeros_like(nxt_ref)
            dd_ref[...] = jnp.zeros_like(dd_ref)
            dcw_ref[...] = jnp.zeros_like(dcw_ref)
            dcb_ref[...] = jnp.zeros_like(dcb_ref)
            dnw_ref[...] = jnp.zeros_like(dnw_ref)
            dhd_ref[...] = jnp.zeros_like(dhd_ref)

        gst, nxt = dh_ref[...], nxt_ref[...]
        for sub in reversed(range(SUBS)):
            rows = slice(sub * BLK, (sub + 1) * BLK)
            gst, nxt = chunk(sub, rows, gst, nxt, z_ref, xc_ref, pre_ref, dtr_ref, dy_ref, hs_ref, cw_ref, dtb_ref,
                             a_ref, dk_ref, nw_ref, ex_ref, ext_ref, tril_ref, triu_ref,
                             dz_ref, dxbc_ref, ddt_ref, dcw_ref, dcb_ref, dnw_ref, dhd_ref, dd_ref)
        dh_ref[...] = gst
        nxt_ref[...] = nxt

        @pl.when(i == nt - 1)
        def _():
            dhd_ref[2:3, :] = _sel_r(dd_ref[...], ext_ref[...])[0:1, :]

    def chunk(sub, rows, gst, nxt, z_ref, xc_ref, pre_ref, dtr_ref, dy_ref, hs_ref, cw_ref, dtb_ref,
              a_ref, dk_ref, nw_ref, ex_ref, ext_ref, tril_ref, triu_ref,
              dz_ref, dxbc_ref, ddt_ref, dcw_ref, dcb_ref, dnw_ref, dhd_ref, dd_ref):
        h_in = hs_ref[sub]
        f = _ssd_chunk(pre_ref[rows, :], dtr_ref[rows, :], dtb_ref[...], a_ref[...], dk_ref[...], ex_ref[...],
                       tril_ref[...], h_in)
        xs, xdt, decx, ecsx, ecl, dtx = f["xs"], f["xdt"], f["decx"], f["ecsx"], f["ecl"], f["dtx"]
        cs, cst, causal = f["cs"], f["cst"], f["causal"]
        causal_t = triu_ref[...].astype(F32) > 0.5

        zv = z_ref[rows, :]
        sz = _sig(zv)
        gz = zv * sz
        t = f["y"] * gz
        r = lax.rsqrt(_group_mean(t * t) + EPS)
        tn_ = t * r
        dyn = dy_ref[rows, :]
        dnw_ref[0:1, :] += jnp.sum(dyn * tn_, axis=0, keepdims=True)
        u = dyn * nw_ref[...]
        dt_ = r * u - tn_ * (r * _group_mean(u * tn_))
        dy = dt_ * gz
        dz_ref[rows, :] = (dt_ * f["y"] * (sz * (1.0 + zv * (1.0 - sz)))).astype(BF16)

        dd_ref[0:1, :] += jnp.sum(dy * xs, axis=0, keepdims=True)
        dxs = dk_ref[...] * dy

        edy = ecsx * dy
        dxdt, dbs, dcs_, dcsx_parts, dh_new = [], [], [], [], []
        lane = lax.broadcasted_iota(jnp.int32, (1, 128), 1)
        dcs_intra = jnp.zeros((BLK, 128), F32)
        for g in range(2):
            sl = slice(g * 256, (g + 1) * 256)
            bgf, cgf = f["bm"][:, g * NST:(g + 1) * NST], f["cm"][:, g * NST:(g + 1) * NST]
            bg, cg = bgf.astype(BF16), cgf.astype(BF16)
            gg = gst[:, sl].astype(BF16)
            hg = h_in[:, sl].astype(BF16)
            edyg = edy[:, sl].astype(BF16)
            dc = _mm_nt(edyg, hg)
            dh_new.append(gst[:, sl] * ecl[:, sl] + _mm_tn(cg, edyg))
            bgm = _mm(bg, gg)
            dxdt_g = decx[:, sl] * bgm
            dxg = (decx[:, sl] * xdt[:, sl]).astype(BF16)
            db = _mm_nt(dxg, gg)
            qd = bgm * xdt[:, sl] * decx[:, sl]
            last = jnp.sum(qd, axis=0, keepdims=True) + ecl[:, sl] * jnp.sum(gst[:, sl] * h_in[:, sl], axis=0, keepdims=True)
            rowid = lax.broadcasted_iota(jnp.int32, (BLK, 256), 0)
            dcsx_parts.append(f["yoff"][:, sl] * dy[:, sl] - qd + jnp.where(rowid == BLK - 1, last, 0.0))
            cb_ = f["cbs"][g]
            cbt = _mm_nt(bg, cg)
            dcb_ = jnp.zeros((BLK, BLK), F32)
            dcbt = jnp.zeros((BLK, BLK), F32)
            dxd = []
            for r_ in range(4):
                h = 4 * g + r_
                hl = slice(h * HD, (h + 1) * HD)
                lm = f["lms"][h]
                segt = cst[h:h + 1, :] - cs[:, h:h + 1]
                lmt = jnp.where(causal_t, jnp.exp(jnp.minimum(segt, 0.0)), 0.0)
                dyh = dy[:, hl].astype(BF16)
                xdh = xdt[:, hl].astype(BF16)
                dw = _mm_nt(dyh, xdh)
                dwt = _mm_nt(xdh, dyh)
                wt = cbt * lmt
                dxd.append(_mm(wt.astype(BF16), dyh))
                dcb_ = dcb_ + dw * lm
                dcbt = dcbt + dwt * lmt
                col = jnp.sum(dw * (cb_ * lm), axis=-1, keepdims=True) - jnp.sum(dwt * wt, axis=-1, keepdims=True)
                dcs_intra = dcs_intra + jnp.where(lane == h, col, 0.0)
            dxdt.append(dxdt_g + jnp.concatenate(dxd, axis=1))
            dcs_.append(dc + _mm(dcb_.astype(BF16), bg))
            dbs.append(db + _mm(dcbt.astype(BF16), cg))
        dxdt = jnp.concatenate(dxdt, axis=1)
        dxs = dxs + dxdt * dtx
        ext_t_ = ext_ref[...]
        dcs = dcs_intra + _sel_r(jnp.concatenate(dcsx_parts, axis=1), ext_t_)
        da = _sel_l(triu_ref[...], dcs)
        ddt = da * f["av"] + _sel_r(dxdt * xs, ext_t_)
        dhd_ref[1:2, :] += jnp.sum(da * f["dt"], axis=0, keepdims=True)
        ddtr = ddt * _sig(f["dtin"])
        dhd_ref[0:1, :] += jnp.sum(ddtr, axis=0, keepdims=True)
        ddt_ref[rows, :] = ddtr.astype(BF16)

        sp, pre = f["sp"], f["pre"]
        dact = jnp.concatenate([dxs] + dbs + dcs_, axis=1)
        dpre = dact * (sp * (1.0 + pre * (1.0 - sp)))
        dcb_ref[0:1, :] += jnp.sum(dpre, axis=0, keepdims=True)
        ext2 = jnp.concatenate([dpre, nxt], axis=0)
        shifted = [pltpu.roll(ext2, BLK + 8 - (CK - 1 - k), 0)[0:BLK] for k in range(CK - 1)] + [dpre]
        cw = cw_ref[...]
        xc = xc_ref[rows, :]
        dxr = cw[CK - 1:CK, :] * dpre
        for k in range(CK):
            dcw_ref[k:k + 1, :] += jnp.sum(shifted[k] * xc, axis=0, keepdims=True)
            if k < CK - 1:
                dxr = dxr + cw[k:k + 1, :] * shifted[k]
        dxbc_ref[rows, :] = dxr.astype(BF16)
        return jnp.concatenate(dh_new, axis=1), dpre[0:8]

    cur = lambda w: pl.BlockSpec((tile, w), lambda i: (nt - 1 - i, 0))
    return _pcall(
        body, [z, xbc, pre_all, dtr, dys, hs, cw, dtb, av, dk, nw, ex, ext_t, tril, triu], name="ssd_bwd", grid=(nt,),
        out_shape=[jax.ShapeDtypeStruct((s, SW), BF16), jax.ShapeDtypeStruct((s, XBCW), BF16),
                   jax.ShapeDtypeStruct((s, 128), BF16), jax.ShapeDtypeStruct((8, XBCW), F32),
                   jax.ShapeDtypeStruct((8, XBCW), F32), jax.ShapeDtypeStruct((8, SW), F32),
                   jax.ShapeDtypeStruct((8, 128), F32)],
        in_specs=[cur(SW), cur(XBCW), cur(XBCW), cur(128), cur(SW),
                  pl.BlockSpec((SUBS, NST, SW), lambda i: (nt - 1 - i, 0, 0)),
                  _row((8, XBCW)), _row((1, 128)), _row((1, 128)), _row((1, SW)), _row((1, SW)),
                  _row((128, SW)), _row((SW, 128)), _row((BLK, BLK)), _row((BLK, BLK))],
        out_specs=[cur(SW), cur(XBCW), cur(128), _row((8, XBCW)), _row((8, XBCW)), _row((8, SW)), _row((8, 128))],
        scratch_shapes=[pltpu.VMEM((NST, SW), F32), pltpu.VMEM((8, XBCW), F32), pltpu.VMEM((8, SW), F32)], carry=carry)


def _load_once(i, pairs, sem):
    @pl.when(i == 0)
    def _():
        cps = [pltpu.make_async_copy(src, dst, sem.at[k]) for k, (src, dst) in enumerate(pairs)]
        for cp in cps:
            cp.start()
        for cp in cps:
            cp.wait()


def _mlp_fwd(x, ya, ys, tgt, w_o, w_ga, w_gb, w_dn, gate1, a2, sh2, gate2, fn):
    s = x.shape[0]
    sub_m, subs = 256, 2
    tm = sub_m * subs

    def body(x_ref, ya_ref, ys_ref, t_ref, wo_hbm, wga_hbm, wgb_hbm, wdn_hbm, g1_ref, a2_ref, s2_ref, g2_ref, fn_ref,
             x1_ref, gu_ref, dx2_ref, loss_ref, dfn_ref, wo, wga, wgb, wdn, sem):
        i = pl.program_id(0)
        _load_once(i, [(wo_hbm, wo), (wga_hbm, wga), (wgb_hbm, wgb), (wdn_hbm, wdn)], sem)

        @pl.when(i == 0)
        def _():
            loss_ref[...] = jnp.zeros_like(loss_ref)
            dfn_ref[...] = jnp.zeros_like(dfn_ref)

        def proj(st):
            st["mix"] = _mm(ya_ref[st["rows"], :], wo[0:QW, :]) + _mm(ys_ref[st["rows"], :], wo[QW:D, :])

        def norm(st):
            x1 = x_ref[st["rows"], :] + g1_ref[...] * st.pop("mix")
            x1_ref[st["rows"], :] = x1
            r2 = lax.rsqrt(jnp.mean(x1 * x1, axis=-1, keepdims=True) + EPS)
            st["x1"] = x1
            st["h2"] = (x1 * r2 * a2_ref[...] + s2_ref[...]).astype(BF16)

        def gate_up(st):
            h2 = st.pop("h2")
            ha, hb = h2[:, 0:D // 2], h2[:, D // 2:D]
            gub = jnp.concatenate([(_mm(ha, wga[j]) + _mm(hb, wgb[j])).astype(BF16) for j in range(4)], axis=1)
            gu_ref[st["rows"], :] = gub
            st["gub"] = gub

        def activate(st):
            gub = st.pop("gub")
            gv, uv = gub[:, 0:DFF].astype(F32), gub[:, DFF:].astype(F32)
            st["act"] = (gv * _sig(gv) * uv).astype(BF16)

        def down(st):
            st["ff"] = _mm(st.pop("act"), wdn[...])

        def head(st):
            x2 = st.pop("x1") + g2_ref[...] * st.pop("ff")
            r3 = lax.rsqrt(jnp.mean(x2 * x2, axis=-1, keepdims=True) + EPS)
            xn = x2 * r3
            fnv = fn_ref[...]
            err = xn * fnv - t_ref[st["rows"], :]
            st["loss"] = jnp.sum(err * err) * (0.5 / D)
            dy = err * (1.0 / D)
            st["dfn"] = jnp.sum(dy * xn, axis=0, keepdims=True)
            u = dy * fnv
            dx2_ref[st["rows"], :] = r3 * u - xn * (r3 * jnp.mean(u * xn, axis=-1, keepdims=True))

        a, b = [dict(rows=slice(k * sub_m, (k + 1) * sub_m)) for k in range(subs)]
        for stage, st in [(proj, a), (norm, a), (proj, b), (gate_up, a), (norm, b), (activate, a), (gate_up, b),
                          (down, a), (activate, b), (head, a), (down, b), (head, b)]:
            stage(st)
        loss_ref[...] += a["loss"] + b["loss"]
        dfn_ref[0:1, :] += a["dfn"] + b["dfn"]

    def tok(w):
        return pl.BlockSpec((tm, w), lambda i: (i, 0))

    hbm = pl.BlockSpec(memory_space=pl.ANY)
    return pl.pallas_call(
        body, name="mlp_fwd", grid=(s // tm,),
        out_shape=[jax.ShapeDtypeStruct((s, D), F32), jax.ShapeDtypeStruct((s, 2 * DFF), BF16),
                   jax.ShapeDtypeStruct((s, D), F32), jax.ShapeDtypeStruct((8, 128), F32),
                   jax.ShapeDtypeStruct((8, D), F32)],
        in_specs=[tok(D), tok(QW), tok(SW), tok(D), hbm, hbm, hbm, hbm,
                  _row((1, D)), _row((1, D)), _row((1, D)), _row((1, D)), _row((1, D))],
        out_specs=[tok(D), tok(2 * DFF), tok(D), _row((8, 128)), _row((8, D))],
        scratch_shapes=[pltpu.VMEM((D, D), BF16), pltpu.VMEM(w_ga.shape, BF16), pltpu.VMEM(w_gb.shape, BF16),
                        pltpu.VMEM((DFF, D), BF16), pltpu.SemaphoreType.DMA((4,))],
        compiler_params=_cp(("arbitrary",)),
    )(x, ya, ys, tgt, w_o, w_ga, w_gb, w_dn, gate1, a2, sh2, gate2, fn)


def _mlp_bwd(x1, gu, dx2, w_o, w_ga, w_gb, w_dn, gate1, a2, sh2, gate2):
    s = x1.shape[0]
    tm = 256
    nj = 2 * DFF // 4

    def body(x1_ref, gu_ref, dx2_ref, wo_hbm, wga_hbm, wgb_hbm, wdn_hbm, g1_ref, a2_ref, s2_ref, g2_ref,
             dx1_ref, dya_ref, dys_ref, act_ref, dgu_ref, h2_ref, dsh_ref, p_ref, wo, wga, wgb, wdn, sem):
        i = pl.program_id(0)
        _load_once(i, [(wo_hbm, wo), (wga_hbm, wga), (wgb_hbm, wgb), (wdn_hbm, wdn)], sem)

        @pl.when(i == 0)
        def _():
            dsh_ref[...] = jnp.zeros_like(dsh_ref)
            p_ref[...] = jnp.zeros_like(p_ref)

        dx2 = dx2_ref[...]
        dact = _mm_nt((dx2 * g2_ref[...]).astype(BF16), wdn[...])
        gub = gu_ref[...]
        gv, uv = gub[:, 0:DFF].astype(F32), gub[:, DFF:].astype(F32)
        sg = _sig(gv)
        sl = gv * sg
        act_ref[...] = (sl * uv).astype(BF16)
        dgu = jnp.concatenate([dact * uv * (sg * (1.0 + gv * (1.0 - sg))), dact * sl], axis=1).astype(BF16)
        dgu_ref[...] = dgu
        dha = sum(_mm_nt(dgu[:, j * nj:(j + 1) * nj], wga[j]) for j in range(4))
        dhb = sum(_mm_nt(dgu[:, j * nj:(j + 1) * nj], wgb[j]) for j in range(4))
        dh = jnp.concatenate([dha, dhb], axis=1)
        x1 = x1_ref[...]
        r2 = lax.rsqrt(jnp.mean(x1 * x1, axis=-1, keepdims=True) + EPS)
        xn = x1 * r2
        a2 = a2_ref[...]
        h2_ref[...] = (xn * a2 + s2_ref[...]).astype(BF16)
        dsh_ref[0:1, :] += jnp.sum(dh, axis=0, keepdims=True)
        p_ref[0:1, :] += jnp.sum(dh * xn, axis=0, keepdims=True)
        u = dh * a2
        dx1 = dx2 + r2 * u - xn * (r2 * jnp.mean(u * xn, axis=-1, keepdims=True))
        dx1_ref[...] = dx1
        dcat = _mm_nt((dx1 * g1_ref[...]).astype(BF16), wo[...])
        dya_ref[...] = dcat[:, 0:QW]
        dys_ref[...] = dcat[:, QW:D]

    def tok(w):
        return pl.BlockSpec((tm, w), lambda i: (i, 0))

    hbm = pl.BlockSpec(memory_space=pl.ANY)
    return pl.pallas_call(
        body, name="mlp_bwd", grid=(s // tm,),
        out_shape=[jax.ShapeDtypeStruct((s, D), F32), jax.ShapeDtypeStruct((s, QW), F32),
                   jax.ShapeDtypeStruct((s, SW), F32), jax.ShapeDtypeStruct((s, DFF), BF16),
                   jax.ShapeDtypeStruct((s, 2 * DFF), BF16), jax.ShapeDtypeStruct((s, D), BF16),
                   jax.ShapeDtypeStruct((8, D), F32), jax.ShapeDtypeStruct((8, D), F32)],
        in_specs=[tok(D), tok(2 * DFF), tok(D), hbm, hbm, hbm, hbm, _row((1, D)), _row((1, D)), _row((1, D)), _row((1, D))],
        out_specs=[tok(D), tok(QW), tok(SW), tok(DFF), tok(2 * DFF), tok(D), _row((8, D)), _row((8, D))],
        scratch_shapes=[pltpu.VMEM((D, D), BF16), pltpu.VMEM(w_ga.shape, BF16), pltpu.VMEM(w_gb.shape, BF16),
                        pltpu.VMEM((DFF, D), BF16), pltpu.SemaphoreType.DMA((4,))],
        compiler_params=_cp(("arbitrary",)),
    )(x1, gu, dx2, w_o, w_ga, w_gb, w_dn, gate1, a2, sh2, gate2)


def _wgrad(name, a, b, gate, w, carry=None):
    s, m = a.shape
    n = b.shape[1]
    tk = min(1024, s)
    nk = s // tk

    def body(a_ref, b_ref, g_ref, w_ref, o_hbm, dg_ref, acc_ref, sem):
        k = pl.program_id(0)

        @pl.when(k == 0)
        def _():
            acc_ref[...] = jnp.zeros_like(acc_ref)

        acc_ref[...] += _mm_tn(a_ref[...], b_ref[...].astype(BF16))

        @pl.when(k == nk - 1)
        def _():
            acc = acc_ref[...]
            dg_ref[...] = jnp.zeros_like(dg_ref)
            dg_ref[0:1, :] = jnp.sum(acc * w_ref[...].astype(F32), axis=0, keepdims=True)
            acc_ref[...] = acc * g_ref[...]
            cp = pltpu.make_async_copy(acc_ref, o_hbm, sem)
            cp.start()
            cp.wait()

    return _pcall(body, [a, b, gate, w], name=name, grid=(nk,),
                  out_shape=[jax.ShapeDtypeStruct((m, n), F32), jax.ShapeDtypeStruct((8, n), F32)],
                  in_specs=[pl.BlockSpec((tk, m), lambda k: (k, 0)), pl.BlockSpec((tk, n), lambda k: (k, 0)),
                            _row((1, n)), _row((m, n))],
                  out_specs=[pl.BlockSpec(memory_space=pl.ANY), _row((8, n))],
                  scratch_shapes=[pltpu.VMEM((m, n), F32), pltpu.SemaphoreType.DMA], carry=carry)


def _wgrad_gate_up(h2, dgu, carry=None):
    s = h2.shape[0]
    tk = min(1024, s)
    nk = s // tk
    n = dgu.shape[1]
    nj = n // 4

    def body(a_ref, b_ref, o_hbm, acc_ref, sems):
        k = pl.program_id(0)

        @pl.when(k == 0)
        def _():
            acc_ref[...] = jnp.zeros_like(acc_ref)

        acc_ref[...] += _mm_tn(a_ref[...], b_ref[...])

        @pl.when(k == nk - 1)
        def _():
            cps = [pltpu.make_async_copy(acc_ref.at[:, pl.ds(j * nj, nj)], o_hbm.at[j], sems.at[j]) for j in range(4)]
            for cp in cps:
                cp.start()
            for cp in cps:
                cp.wait()

    return _pcall(body, [h2, dgu], name="wgrad_gate_up", grid=(nk,),
                  out_shape=[jax.ShapeDtypeStruct((4, D, nj), F32)],
                  in_specs=[pl.BlockSpec((tk, D), lambda k: (k, 0)), pl.BlockSpec((tk, n), lambda k: (k, 0))],
                  out_specs=[pl.BlockSpec(memory_space=pl.ANY)],
                  scratch_shapes=[pltpu.VMEM((D, n), F32), pltpu.SemaphoreType.DMA((4,))], carry=carry)


def _wgrad_in_t(h1, pieces, carry=None):
    s = h1.shape[0]
    tk = min(1024, s)
    nk = s // tk

    def body(a_ref, dq_ref, dkv_ref, dz_ref, dxbc_ref, ddt_ref, o_hbm, acc_ref, tr_ref, sem):
        k = pl.program_id(0)

        @pl.when(k == 0)
        def _():
            acc_ref[...] = jnp.zeros_like(acc_ref)

        dproj = jnp.concatenate([dq_ref[...], dkv_ref[...], dz_ref[...], dxbc_ref[...], ddt_ref[...]], axis=1)
        acc_ref[...] += _mm_tn(a_ref[...], dproj)

        @pl.when(k == nk - 1)
        def _():
            for j in range(PROJ_W // 128):
                tr_ref[j * 128:(j + 1) * 128, :] = acc_ref[:, j * 128:(j + 1) * 128].T
            cp = pltpu.make_async_copy(tr_ref, o_hbm, sem)
            cp.start()
            cp.wait()

    return _pcall(body, [h1] + list(pieces), name="wgrad_in", grid=(nk,),
                  out_shape=[jax.ShapeDtypeStruct((PROJ_W, D), F32)],
                  in_specs=[pl.BlockSpec((tk, p.shape[1]), lambda k: (k, 0)) for p in [h1] + list(pieces)],
                  out_specs=[pl.BlockSpec(memory_space=pl.ANY)],
                  scratch_shapes=[pltpu.VMEM((D, PROJ_W), F32), pltpu.VMEM((PROJ_W, D), F32), pltpu.SemaphoreType.DMA],
                  carry=carry)


_SMALL = ["ada_b", "norm1", "conv_w", "conv_b", "dt_bias", "A_log", "D_skip", "sinks", "attn_out_norm",
          "ssm_out_norm", "norm2", "rel_bias", "final_norm"]


def _small_grad(name, gs, chip):
    if name == "ada_b":
        return jnp.concatenate([gs[j:j + 1, :] for j in range(6)], axis=1)
    if name == "conv_w":
        full = gs[7:11, :]
        out = full[:, 0:256]
        for j in range(1, 4):
            out = jnp.where(chip == j, full[:, j * 256:(j + 1) * 256], out)
        return out
    row, width = {"norm1": (6, D), "conv_b": (11, D), "norm2": (12, D), "final_norm": (13, D),
                  "attn_out_norm": (14, QW), "ssm_out_norm": (15, SW), "dt_bias": (16, NH), "A_log": (17, NH),
                  "D_skip": (18, NH), "sinks": (19, NH), "rel_bias": (24, NH)}[name]
    rows = NBUCKET if name == "rel_bias" else 1
    return gs[row:row + rows, 0:width]


def _small_update(small_all, where, ws, ms, vs):
    n = len(_SMALL)

    def body(where_ref, sa_ref, *refs):
        w_refs, m_refs, v_refs, outs = refs[:n], refs[n:2 * n], refs[2 * n:3 * n], refs[3 * n:]
        gs = sa_ref[0]
        for b in range(1, 8):
            gs = gs + sa_ref[b]
        chip = where_ref[1]
        for i, name in enumerate(_SMALL):
            g = _small_grad(name, gs, chip)
            lead = (0,) if name == "conv_w" else ()
            d, mo, vo = _adamw(w_refs[i][lead + (...,)], g, m_refs[i][lead + (...,)], v_refs[i][lead + (...,)])
            for k, val in enumerate((g, d, mo, vo)):
                outs[k * n + i][lead + (...,)] = val
        outs[4 * n][...] = gs[20:21, 0:128]

    shapes = [jax.ShapeDtypeStruct(w.shape, F32) for w in ws]
    vmem = pl.BlockSpec(memory_space=pltpu.VMEM)
    res = pl.pallas_call(
        body, name="small_update", out_shape=shapes * 4 + [jax.ShapeDtypeStruct((1, 128), F32)],
        in_specs=[pl.BlockSpec(memory_space=pltpu.SMEM)] + [vmem] * (1 + 3 * n), out_specs=[vmem] * (4 * n + 1),
    )(where, small_all, *ws, *ms, *vs)
    return [res[k * n:(k + 1) * n] for k in range(4)], res[4 * n][0, 0]


def _add_half(name, g, got, where, by_cols=False):
    rr, cc = got.shape[1:]
    if by_cols:
        mine = pl.BlockSpec((None, rr, cc), lambda i, w_ref: (i, 0, w_ref[0]))
    else:
        mine = pl.BlockSpec((None, None, rr, cc), lambda i, w_ref: (i, w_ref[0], 0, 0))

    def body(w_ref, g_ref, r_ref, o_ref, own_ref):
        s = g_ref[...] + r_ref[...]
        o_ref[...] = s.astype(BF16)

        @pl.when(pl.program_id(0) == w_ref[1])
        def _():
            own_ref[...] = s

    spec = pl.BlockSpec((None, rr, cc), lambda i, w_ref: (i, 0, 0))
    return _pcall(body, [where, g, got], name=name, grid=(4,), nprefetch=1,
                  out_shape=[jax.ShapeDtypeStruct(got.shape, BF16), jax.ShapeDtypeStruct((rr, cc), F32)],
                  in_specs=[mine, spec],
                  out_specs=[spec, pl.BlockSpec((rr, cc), lambda i, w_ref: (0, 0))])


def _add_chips(name, own, got):
    rr, cc = own.shape
    tr = rr // 2 if rr % 32 == 0 else rr

    def body(s_ref, r_ref, o_ref):
        o_ref[...] = ((s_ref[...] + r_ref[0].astype(F32)) + r_ref[1].astype(F32)) + r_ref[2].astype(F32)

    spec = pl.BlockSpec((tr, cc), lambda i: (i, 0))
    return _pcall(body, [own, got], name=name, grid=(rr // tr,), out_shape=[jax.ShapeDtypeStruct((rr, cc), F32)],
                  in_specs=[spec, pl.BlockSpec((3, tr, cc), lambda i: (0, i, 0))], out_specs=[spec])[0]


def _adamw_halves(name, mine, got, w, m, v, where, by_cols=False):
    rr, cc = mine.shape

    def body(w_ref_, t_ref, r_ref, w_ref, m_ref, v_ref, g_ref, d_ref, mo_ref, vo_ref):
        g = jnp.where(pl.program_id(0) == w_ref_[0], t_ref[...], r_ref[...])
        g_ref[...] = g
        d_ref[...], mo_ref[...], vo_ref[...] = _adamw(w_ref[...], g, m_ref[...], v_ref[...])

    if by_cols:
        grid = (2, 1)
        half = pl.BlockSpec((rr, cc), lambda h, i, w_ref_: (0, 0))
        full = pl.BlockSpec((rr, cc), lambda h, i, w_ref_: (0, h))
    else:
        tr = rr // 2
        grid = (2, 2)
        half = pl.BlockSpec((tr, cc), lambda h, i, w_ref_: (i, 0))
        full = pl.BlockSpec((None, tr, cc), lambda h, i, w_ref_: (0, 2 * h + i, 0))
    return _pcall(body, [where, mine, got, w, m, v], name=name, grid=grid, nprefetch=1,
                  out_shape=[jax.ShapeDtypeStruct(w.shape, F32)] * 4,
                  in_specs=[half, half, full, full, full], out_specs=[full] * 4)


def _bias_table(rel_bias, bucket, mask):
    def body(rb_ref, bk_ref, mk_ref, o_ref):
        bk = bk_ref[...]
        valid = mk_ref[...] > 0
        for h in range(NH):
            acc = jnp.zeros((BLK, 2 * BLK), F32)
            for b in range(NBUCKET):
                acc = jnp.where(bk == b, rb_ref[b, h], acc)
            o_ref[h] = jnp.where(valid, acc, NEG)

    vmem = pl.BlockSpec(memory_space=pltpu.VMEM)
    return pl.pallas_call(
        body, name="bias_table", out_shape=jax.ShapeDtypeStruct((NH, BLK, 2 * BLK), F32),
        in_specs=[pl.BlockSpec(memory_space=pltpu.SMEM), vmem, vmem], out_specs=vmem,
    )(rel_bias, bucket, mask)


def _pack_small(dsh1, p1, dsh2, p2, dg1a, dg1b, dg2, norm1, norm2, scale1, scale2, dcw, dcb, dfn,
                dnw_attn, dnw_ssm, dhd, av, dsink, drel, loss_acc):
    def body(dsh1_ref, p1_ref, dsh2_ref, p2_ref, dg1a_ref, dg1b_ref, dg2_ref, n1_ref, n2_ref, s1_ref, s2_ref,
             dcw_ref, dcb_ref, dfn_ref, da_ref, ds_ref, dhd_ref, av_ref, dsink_ref, drel_ref, loss_ref, o_ref):
        o_ref[...] = jnp.zeros_like(o_ref)
        p1v, p2v = p1_ref[0:1, :], p2_ref[0:1, :]
        o_ref[0:1, :] = dsh1_ref[0:1, :]
        o_ref[1:2, :] = p1v * n1_ref[...]
        o_ref[2:3, :] = dg1a_ref[0:1, :] + dg1b_ref[0:1, :]
        o_ref[3:4, :] = dsh2_ref[0:1, :]
        o_ref[4:5, :] = p2v * n2_ref[...]
        o_ref[5:6, :] = dg2_ref[0:1, :]
        o_ref[6:7, :] = p1v * (1.0 + s1_ref[...])
        o_ref[7:11, :] = dcw_ref[0:4, :]
        o_ref[11:12, :] = dcb_ref[0:1, :]
        o_ref[12:13, :] = p2v * (1.0 + s2_ref[...])
        o_ref[13:14, :] = dfn_ref[0:1, :]
        o_ref[14:15, 0:QW] = da_ref[0:1, :]
        o_ref[15:16, 0:SW] = ds_ref[0:1, :]
        o_ref[16:17, 0:128] = dhd_ref[0:1, :]
        o_ref[17:18, 0:128] = dhd_ref[1:2, :] * av_ref[...]
        o_ref[18:19, 0:128] = dhd_ref[2:3, :]
        o_ref[19:20, 0:128] = dsink_ref[0:1, :]
        o_ref[20:21, 0:128] = loss_ref[0:1, :]
        o_ref[24:56, 0:128] = drel_ref[...]

    return pl.pallas_call(body, name="pack_small", out_shape=jax.ShapeDtypeStruct((56, D), F32))(
        dsh1, p1, dsh2, p2, dg1a, dg1b, dg2, norm1, norm2, scale1, scale2, dcw, dcb, dfn,
        dnw_attn, dnw_ssm, dhd, av, dsink, drel, loss_acc)


def _pad_row(a, rows=1):
    return jnp.pad(a.reshape(rows, -1), ((0, 0), (0, D - a.size // rows)))


def kernel(x, c, ada_w, ada_b, norm1, w_in, conv_w, conv_b, dt_bias, A_log, D_skip, sinks, attn_out_norm, ssm_out_norm, w_o, norm2, w_gate_up, w_down, rel_bias, final_norm, loss_target, m_ada_w, m_ada_b, m_norm1, m_w_in, m_conv_w, m_conv_b, m_dt_bias, m_A_log, m_D_skip, m_sinks, m_attn_out_norm, m_ssm_out_norm, m_w_o, m_norm2, m_w_gate_up, m_w_down, m_rel_bias, m_final_norm, v_ada_w, v_ada_b, v_norm1, v_w_in, v_conv_w, v_conv_b, v_dt_bias, v_A_log, v_D_skip, v_sinks, v_attn_out_norm, v_ssm_out_norm, v_w_o, v_norm2, v_w_gate_up, v_w_down, v_rel_bias, v_final_norm):
    xi, yi, ci = lax.axis_index("x"), lax.axis_index("y"), lax.axis_index("c")
    chip = 2 * xi + yi
    me = 4 * xi + 2 * yi + ci
    where = jnp.stack([ci, chip]).astype(jnp.int32)
    xs2, tgt = x[0], loss_target[0]

    first = jnp.concatenate([c, _pad_row(conv_w[0], CK), jnp.zeros((3, D), F32)], axis=0)
    w_in_t, m_w_in_t, v_w_in_t = w_in[0].T, m_w_in[0].T, v_w_in[0].T
    w_in_b, w_o_b, w_dn_b = w_in_t.astype(BF16), w_o[0].astype(BF16), w_down[0].astype(BF16)
    w_gu_b = w_gate_up[0].astype(BF16)
    first_all = _exchange("gather_cond", _gather8_carry(first))[0]
    c_all = first_all[:, 0, :]
    cw_full = jnp.concatenate([first_all[2 * j, 1:1 + CK, 0:256] for j in range(4)], axis=1)

    hw = D // 2
    fetch_half = _Carry(
        [w_in_b], [jax.ShapeDtypeStruct((4, w_in_b.shape[0], hw), BF16)],
        lambda x_, y_, c_: [(f, 0, (slice(None), pl.ds(c_ * hw, hw)), 0, 2 * x_ + y_) for f in [None] + _CHIPS3])
    ncol = ada_w.shape[2]
    mod_cols, w_half = _ada_fwd(c_all, ada_w[0], lax.dynamic_slice(ada_b, (0, chip * ncol), (1, ncol)),
                                carry=fetch_half)
    to_other = _Carry([w_half[None]], [jax.ShapeDtypeStruct((1,) + w_half.shape, BF16)],
                      lambda x_, y_, c_: [(_SIBLING, 0, 0, 0, 0)])
    mod_all, w_other = _exchange("gather_mod", _merge(_gather_chips_carry([mod_cols]), to_other))
    w_lo = jnp.where(ci == 0, w_half, w_other[0])
    w_hi = jnp.where(ci == 0, w_other[0], w_half)
    w_in_f = jnp.pad(jnp.concatenate([w_lo, w_hi], axis=2).reshape(IN_W, D), ((0, PROJ_W - IN_W), (0, 0)))
    mod = lax.dynamic_slice(jnp.transpose(mod_all, (1, 0, 2)).reshape(8, 4 * ncol), (me, 0), (1, 4 * ncol))
    shift1, scale1, gate1, shift2, scale2, gate2 = [mod[:, j * D:(j + 1) * D] for j in range(6)]
    a1 = norm1 * (1.0 + scale1)
    a2 = norm2 * (1.0 + scale2)

    hdn = DFF // 8
    q, kv, z, xbc, dtr, w_o_g, w_dna_g = _in_proj_fwd(xs2, a1, shift1, w_in_f,
                                                      carry=_gather_chips_carry([w_o_b, w_dn_b[0:hdn]]))
    w_o_f = w_o_g.reshape(D, D)
    bucket, mask = _attn_geometry()
    bucket = jnp.asarray(bucket)
    bias = _bias_table(rel_bias, bucket, jnp.asarray(mask.astype(np.int32)))
    sinks1 = sinks[0]
    ya, w_ga_g = _attn_fwd(q, kv, bias, sinks1, attn_out_norm, carry=_gather_chips_carry([w_gu_b[0:D // 2]]))
    cw8 = jnp.concatenate([cw_full, jnp.zeros((4, XBCW), F32)], axis=0)
    dtb = _pad_row(dt_bias)[:, 0:128]
    av = _pad_row(-jnp.exp(A_log))[:, 0:128]
    dk = jnp.repeat(D_skip, HD, axis=1)
    ys, hs, pre, w_gb_g, w_dnb_g = _ssd_fwd(z, xbc, dtr, cw8, conv_b, dtb, av, dk, ssm_out_norm,
                                            carry=_gather_chips_carry([w_gu_b[D // 2:D], w_dn_b[hdn:2 * hdn]]))
    w_dn_f = jnp.stack([w_dna_g, w_dnb_g], axis=1).reshape(DFF, D)
    fn = final_norm[None, :]
    x1, gu, dx2, loss_acc, dfn = _mlp_fwd(xs2, ya, ys, tgt, w_o_f, w_ga_g, w_gb_g, w_dn_f, gate1, a2, shift2, gate2, fn)

    def to_sibling(p):
        return _Carry([p], [jax.ShapeDtypeStruct((4,) + p.shape[2:], F32)],
                      lambda x_, y_, c_: [(_SIBLING, 0, (j, 1 - c_), 0, j) for j in range(4)])

    def to_chips(s4):
        return _Carry([s4], [jax.ShapeDtypeStruct((3,) + s4.shape[1:], s4.dtype)],
                      lambda x_, y_, c_: [(f, 0, jnp.bitwise_xor(2 * x_ + y_, k + 1), 0, k) for k, f in enumerate(_CHIPS3)])

    def back(t):
        return _Carry([t[None]], [jax.ShapeDtypeStruct((1,) + t.shape, F32)], lambda x_, y_, c_: [(_SIBLING, 0, 0, 0, 0)])

    dx1, dya, dys, act, dgu, h2, dsh2, p2 = _mlp_bwd(x1, gu, dx2, w_o_f, w_ga_g, w_gb_g, w_dn_f, gate1, a2, shift2, gate2)
    p_gu = _wgrad_gate_up(h2, dgu)[0].reshape(4, 2, D // 2, 2 * DFF // 4)
    dq, dkv, dbias, dsink, dnw_attn, g_dn, dg2, got1_gu = _attn_bwd(
        q, kv, dya, bias, sinks1, attn_out_norm, act, dx2, gate2, w_dn_f, carry=to_sibling(p_gu))
    p_dn = g_dn.reshape(4, 2, DFF // 8, D)
    drel = _rel_bias_grad(dbias, bucket)
    s4_gu, own_gu = _add_half("rs_add_half_gu", p_gu, got1_gu, where)
    dz, dxbc, ddt, dcw, dcb, dnw_ssm, dhd, got2_gu, got1_dn = _ssd_bwd(
        z, xbc, pre, dtr, dys, hs, cw8, dtb, av, dk, ssm_out_norm, carry=_merge(to_chips(s4_gu), to_sibling(p_dn)))
    mine_gu = _add_chips("rs_add_chips_gu", own_gu, got2_gu)
    s4_dn, own_dn = _add_half("rs_add_half_dn", p_dn, got1_dn, where)
    grad_x, h1, dsh1, p1 = _in_proj_bwd(xs2, dx1, a1, shift1, w_in_f, dq, dkv, dz, dxbc, ddt)
    g_in_t, got2_dn, got3_gu = _wgrad_in_t(h1, [dq, dkv, dz, dxbc, ddt],
                                           carry=_merge(to_chips(s4_dn), back(mine_gu)))
    mine_dn = _add_chips("rs_add_chips_dn", own_dn, got2_dn)
    p_in = g_in_t[0:IN_W].reshape(4, IN_W // 4, D)

    def to_sibling_cols(p):
        return _Carry([p], [jax.ShapeDtypeStruct(p.shape[:2] + (D // 2,), F32)],
                      lambda x_, y_, c_: [(_SIBLING, 0, (j, slice(None), pl.ds((1 - c_) * (D // 2), D // 2)), 0, j)
                                          for j in range(4)])

    no_dg1 = jnp.zeros((8, D), F32)
    small = _pack_small(dsh1, p1, dsh2, p2, no_dg1, no_dg1, dg2, norm1, norm2, scale1, scale2, dcw, dcb, dfn,
                        dnw_attn, dnw_ssm, dhd, av, dsink, drel, loss_acc)
    g_oa, dg1a, got1_in, got3_dn, small_all = _wgrad(
        "wgrad_o_attn", ya, dx1, gate1, w_o_f[0:QW],
        carry=_merge(to_sibling_cols(p_in), back(mine_dn), _gather8_carry(small)))
    s4_in, own_in = _add_half("rs_add_half_in", p_in, got1_in, where, by_cols=True)
    g_os, dg1b, got2_in = _wgrad("wgrad_o_ssm", ys, dx1, gate1, w_o_f[QW:D], carry=to_chips(s4_in))
    mine_in = _add_chips("rs_add_chips_in", own_in, got2_in)
    p_o = jnp.concatenate([g_oa, g_os], axis=0).reshape(4, 2, D // 8, D)

    dg1_all, got1_o, got3_in = _exchange(
        "gather_tail", _merge(_gather8_carry(dg1a[0:1] + dg1b[0:1]), to_sibling(p_o), back(mine_in)))
    small_all = small_all.at[:, 2, :].set(dg1_all[:, 0, :])
    s4_o, own_o = _add_half("rs_add_half_o", p_o, got1_o, where)
    mine_o = _add_chips("rs_add_chips_o", own_o, _exchange("rs_chips_o", to_chips(s4_o))[0])
    got3_o = _exchange("rs_back_o", back(mine_o))[0]
    small_res, loss = _small_update(
        small_all, where,
        [ada_b, norm1, conv_w, conv_b, dt_bias, A_log, D_skip, sinks, attn_out_norm, ssm_out_norm, norm2, rel_bias,
         final_norm[None, :]],
        [m_ada_b, m_norm1, m_conv_w, m_conv_b, m_dt_bias, m_A_log, m_D_skip, m_sinks, m_attn_out_norm,
         m_ssm_out_norm, m_norm2, m_rel_bias, m_final_norm[None, :]],
        [v_ada_b, v_norm1, v_conv_w, v_conv_b, v_dt_bias, v_A_log, v_D_skip, v_sinks, v_attn_out_norm,
         v_ssm_out_norm, v_norm2, v_rel_bias, v_final_norm[None, :]])
    small_out = [dict(zip(_SMALL, r)) for r in small_res]
    for r in small_out:
        r["final_norm"] = r["final_norm"][0]

    dmod_all = small_all[:, 0:6, :].reshape(8, 6 * D)
    dmod_loc = lax.dynamic_slice(dmod_all, (0, chip * ncol), (8, ncol))
    ada_out = _ada_bwd_adamw(c_all.T, dmod_loc, ada_w[0], m_ada_w[0], v_ada_w[0])

    big_gu = _adamw_halves("adamw_gate_up", mine_gu, got3_gu[0], w_gate_up, m_w_gate_up, v_w_gate_up, where)
    big_dn = _adamw_halves("adamw_down", mine_dn, got3_dn[0], w_down, m_w_down, v_w_down, where)
    big_o = _adamw_halves("adamw_o", mine_o, got3_o[0], w_o, m_w_o, v_w_o, where)
    big_in = [o.T[None] for o in _adamw_halves("adamw_in", mine_in, got3_in[0], w_in_t, m_w_in_t, v_w_in_t, where,
                                               by_cols=True)]
    big = [big_in, big_o, big_gu, big_dn]

    order = ["ada_w", "ada_b", "norm1", "w_in", "conv_w", "conv_b", "dt_bias", "A_log", "D_skip", "sinks",
             "attn_out_norm", "ssm_out_norm", "w_o", "norm2", "w_gate_up", "w_down", "rel_bias", "final_norm"]
    bigname = {"w_in": 0, "w_o": 1, "w_gate_up": 2, "w_down": 3}
    res = [loss, grad_x[None]]
    for kind in range(4):
        for nm in order:
            if nm == "ada_w":
                res.append(ada_out[kind][None])
            elif nm in bigname:
                res.append(big[bigname[nm]][kind])
            else:
                res.append(small_out[kind][nm])
    return tuple(res)
```

```python
import numpy as np
import jax
import jax.numpy as jnp
from jax import lax
from jax.experimental import pallas as pl
from jax.experimental.pallas import tpu as pltpu

F32, BF16 = jnp.float32, jnp.bfloat16
HI = lax.Precision.HIGHEST

D = 1024
QW, KVW = 512, 128
NH, HD, NKV = 8, 64, 2
SW = 512
NST = 128
XBCW = 1024
CK = 4
BLK = 128
DFF = 2816
IN_W = 2312
PROJ_W = 2432
EPS = 1e-6
NEG = -1e30
NBUCKET = 32

B1, B2, LR, AEPS, WD, STEP = 0.9, 0.999, 0.001, 1e-08, 0.01, 10

VMEM_LIMIT = 56 * 1024 * 1024

_NT = (((1,), (1,)), ((), ()))
_TN = (((0,), (0,)), ((), ()))


def _mm(a, b):
    return jnp.dot(a, b, preferred_element_type=F32)


def _mm_nt(a, b):
    return lax.dot_general(a, b, _NT, preferred_element_type=F32)


def _mm_tn(a, b):
    return lax.dot_general(a, b, _TN, preferred_element_type=F32)


def _mm_hi(a, b):
    return jnp.dot(a, b, preferred_element_type=F32, precision=HI)


def _split3(x):
    hi = x.astype(BF16)
    r = x - hi.astype(F32)
    mid = r.astype(BF16)
    lo = (r - mid.astype(F32)).astype(BF16)
    return hi, mid, lo


def _sel_r(x, e):
    hi, mid, lo = _split3(x)
    return (_mm(hi, e) + _mm(mid, e)) + _mm(lo, e)


def _sel_l(e, x):
    hi, mid, lo = _split3(x)
    return (_mm(e, hi) + _mm(e, mid)) + _mm(e, lo)


def _sig(x):
    return 1.0 / (1.0 + jnp.exp(-x))


def _cp(sem):
    return pltpu.CompilerParams(dimension_semantics=sem, vmem_limit_bytes=VMEM_LIMIT)


def _row(shape):
    nd = len(shape)
    return pl.BlockSpec(shape, lambda *_: (0,) * nd)


def _adamw(w, g, m, v):
    m = B1 * m + (1.0 - B1) * g
    v = B2 * v + (1.0 - B2) * (g * g)
    m_hat = m / (1.0 - B1 ** STEP)
    v_hat = v / (1.0 - B2 ** STEP)
    delta = -LR * (m_hat / (jnp.sqrt(v_hat) + AEPS) + WD * w)
    return delta, m, v


class _Carry:
    def __init__(self, inps, outs, copies):
        self.inps, self.outs, self.copies = list(inps), list(outs), copies
        self.n = len(copies(0, 0, 0))

    def descriptors(self, in_refs, out_refs, send_sems, recv_sems):
        x, y, c = lax.axis_index("x"), lax.axis_index("y"), lax.axis_index("c")
        out = []
        for j, (flip, a, si, o, di) in enumerate(self.copies(x, y, c)):
            if flip is None:
                out.append(pltpu.make_async_copy(in_refs[a].at[si], out_refs[o].at[di], send_sems.at[j]))
            else:
                fx, fy, fc = flip
                peer = (1 - x if fx else x, 1 - y if fy else y, 1 - c if fc else c)
                out.append(pltpu.make_async_remote_copy(
                    src_ref=in_refs[a].at[si], dst_ref=out_refs[o].at[di],
                    send_sem=send_sems.at[j], recv_sem=recv_sems.at[j],
                    device_id=peer, device_id_type=pl.DeviceIdType.MESH))
        return out


def _pcall(body, args, *, name, grid, in_specs, out_specs, out_shape, scratch_shapes=(), sem=None, nprefetch=0,
           carry=None):
    out_shape, out_specs = list(out_shape), list(out_specs)
    in_specs, scratch_shapes = list(in_specs), list(scratch_shapes)
    nin, nout, nscr = len(in_specs), len(out_shape), len(scratch_shapes)
    run = body
    if carry is not None:
        ncin, ncout = len(carry.inps), len(carry.outs)
        hbm = pl.BlockSpec(memory_space=pl.ANY)

        def run(*refs):
            pre, r = refs[:nprefetch], refs[nprefetch:]
            ins, cins = r[:nin], r[nin:nin + ncin]
            r = r[nin + ncin:]
            outs, couts = r[:nout], r[nout:nout + ncout]
            r = r[nout + ncout:]
            scr, (send_sems, recv_sems) = r[:nscr], r[nscr:]
            first = pl.program_id(0) == 0
            last = pl.program_id(0) == grid[0] - 1
            for ax in range(1, len(grid)):
                first = jnp.logical_and(first, pl.program_id(ax) == 0)
                last = jnp.logical_and(last, pl.program_id(ax) == grid[ax] - 1)

            @pl.when(first)
            def _():
                for d in carry.descriptors(cins, couts, send_sems, recv_sems):
                    d.start()

            body(*pre, *ins, *outs, *scr)

            @pl.when(last)
            def _():
                for d in carry.descriptors(cins, couts, send_sems, recv_sems):
                    d.wait()

        in_specs = in_specs + [hbm] * ncin
        out_specs = out_specs + [hbm] * ncout
        out_shape = out_shape + carry.outs
        scratch_shapes = scratch_shapes + [pltpu.SemaphoreType.DMA((carry.n,)), pltpu.SemaphoreType.DMA((carry.n,))]
        args = list(args) + carry.inps
    if sem is None:
        sem = ("arbitrary",) * len(grid)
    if nprefetch:
        kw = dict(grid_spec=pltpu.PrefetchScalarGridSpec(num_scalar_prefetch=nprefetch, grid=grid, in_specs=in_specs,
                                                         out_specs=out_specs, scratch_shapes=scratch_shapes))
    else:
        kw = dict(grid=grid, in_specs=in_specs, out_specs=out_specs, scratch_shapes=scratch_shapes)
    res = pl.pallas_call(run, name=name, out_shape=out_shape, compiler_params=_cp(sem), **kw)(*args)
    return list(res)


def _merge(*carries):
    inps, outs, offs = [], [], []
    for cr in carries:
        offs.append((len(inps), len(outs)))
        inps += cr.inps
        outs += cr.outs

    def copies(x, y, c):
        return [(f, a + io, si, o + oo, di) for cr, (io, oo) in zip(carries, offs) for f, a, si, o, di in cr.copies(x, y, c)]

    return _Carry(inps, outs, copies)


def _exchange(name, carry):
    return _pcall(lambda: None, [], name=name, grid=(1,), in_specs=[], out_specs=[], out_shape=[], carry=carry)


_ALL7 = [(f >> 2 & 1, f >> 1 & 1, f & 1) for f in range(1, 8)]
_CHIPS3 = [(0, 1, 0), (1, 0, 0), (1, 1, 0)]
_SIBLING = (0, 0, 1)


def _gather8_carry(blk):
    def copies(x, y, c):
        me = 4 * x + 2 * y + c
        return [(None, 0, 0, 0, me)] + [(f, 0, 0, 0, me) for f in _ALL7]

    return _Carry([blk[None]], [jax.ShapeDtypeStruct((8,) + blk.shape, blk.dtype)], copies)


def _gather_chips_carry(blks):
    def copies(x, y, c):
        chip = 2 * x + y
        return [(f, a, 0, a, chip) for a in range(len(blks)) for f in [None] + _CHIPS3]

    return _Carry([b[None] for b in blks], [jax.ShapeDtypeStruct((4,) + b.shape, b.dtype) for b in blks], copies)


def _ada_fwd(c_all, w_loc, b_loc, carry=None):
    n = w_loc.shape[1]
    tn = 512

    def body(c_ref, w_ref, b_ref, o_ref):
        cv = c_ref[...]
        cond = cv * _sig(cv)
        o_ref[...] = _mm_hi(cond, w_ref[...]) + b_ref[...]

    return _pcall(
        body, [c_all, w_loc, b_loc], name="ada_fwd", grid=(n // tn,),
        out_shape=[jax.ShapeDtypeStruct((8, n), F32)],
        in_specs=[_row((8, D)), pl.BlockSpec((D, tn), lambda j: (0, j)), pl.BlockSpec((1, tn), lambda j: (0, j))],
        out_specs=[pl.BlockSpec((8, tn), lambda j: (0, j))], carry=carry)


def _ada_bwd_adamw(c_all_t, dmod_loc, w, m, v, carry=None):
    n = w.shape[1]
    tn = 512

    def body(ct_ref, dm_ref, w_ref, m_ref, v_ref, g_ref, d_ref, mo_ref, vo_ref):
        ct = ct_ref[...]
        cond = ct * _sig(ct)
        dm = dm_ref[...]
        g = cond[:, 0:1] * dm[0:1, :]
        for b in range(1, 8):
            g = g + cond[:, b:b + 1] * dm[b:b + 1, :]
        g_ref[...] = g
        d_ref[...], mo_ref[...], vo_ref[...] = _adamw(w_ref[...], g, m_ref[...], v_ref[...])

    wspec = pl.BlockSpec((D, tn), lambda j: (0, j))
    return _pcall(
        body, [c_all_t, dmod_loc, w, m, v], name="ada_bwd_adamw", grid=(n // tn,),
        out_shape=[jax.ShapeDtypeStruct((D, n), F32)] * 4,
        in_specs=[_row((D, 8)), pl.BlockSpec((8, tn), lambda j: (0, j)), wspec, wspec, wspec],
        out_specs=[wspec] * 4, carry=carry)


def _in_proj_fwd(x, a1, sh1, w_in, carry=None):
    s = x.shape[0]
    tm = 512

    def body(x_ref, a_ref, s_ref, w_ref, q_ref, kv_ref, z_ref, xbc_ref, dt_ref):
        def norm(rows):
            xv = x_ref[rows, :]
            r = lax.rsqrt(jnp.mean(xv * xv, axis=-1, keepdims=True) + EPS)
            return (xv * r * a_ref[...] + s_ref[...]).astype(BF16)

        def project(rows, h):
            p = _mm_nt(h, w_ref[...])
            q_ref[rows, :] = p[:, 0:512].astype(BF16)
            kv_ref[rows, :] = p[:, 512:768].astype(BF16)
            z_ref[rows, :] = p[:, 768:1280]
            xbc_ref[rows, :] = p[:, 1280:2304]
            dt_ref[rows, :] = p[:, 2304:2432]

        r0, r1 = slice(0, tm // 2), slice(tm // 2, tm)
        h0 = norm(r0)
        project(r0, h0)
        project(r1, norm(r1))

    def tok(w):
        return pl.BlockSpec((tm, w), lambda i: (i, 0))

    return _pcall(
        body, [x, a1, sh1, w_in], name="in_proj_fwd", grid=(s // tm,),
        out_shape=[jax.ShapeDtypeStruct((s, QW), BF16), jax.ShapeDtypeStruct((s, 2 * KVW), BF16),
                   jax.ShapeDtypeStruct((s, SW), F32), jax.ShapeDtypeStruct((s, XBCW), F32),
                   jax.ShapeDtypeStruct((s, 128), F32)],
        in_specs=[tok(D), _row((1, D)), _row((1, D)), _row((PROJ_W, D))],
        out_specs=[tok(QW), tok(2 * KVW), tok(SW), tok(XBCW), tok(128)], carry=carry)


def _in_proj_bwd(x, dx1, a1, sh1, w_in, dq, dkv, dz, dxbc, ddt, carry=None):
    s = x.shape[0]
    tm = 512

    def body(x_ref, dx1_ref, a_ref, s_ref, w_ref, dq_ref, dkv_ref, dz_ref, dxbc_ref, ddt_ref,
             gx_ref, h_ref, dsh_ref, p_ref):
        i = pl.program_id(0)

        @pl.when(i == 0)
        def _():
            dsh_ref[...] = jnp.zeros_like(dsh_ref)
            p_ref[...] = jnp.zeros_like(p_ref)

        def gather(st):
            rows = st["rows"]
            st["dproj"] = jnp.concatenate([dq_ref[rows, :], dkv_ref[rows, :], dz_ref[rows, :], dxbc_ref[rows, :],
                                           ddt_ref[rows, :]], axis=1)

        def back(st):
            st["dh"] = _mm(st.pop("dproj"), w_ref[...])

        def norm(st):
            rows, dh = st["rows"], st.pop("dh")
            xv = x_ref[rows, :]
            r = lax.rsqrt(jnp.mean(xv * xv, axis=-1, keepdims=True) + EPS)
            xn = xv * r
            a = a_ref[...]
            h_ref[rows, :] = (xn * a + s_ref[...]).astype(BF16)
            st["dsh"] = jnp.sum(dh, axis=0, keepdims=True)
            st["p"] = jnp.sum(dh * xn, axis=0, keepdims=True)
            u = dh * a
            gx_ref[rows, :] = dx1_ref[rows, :] + r * u - xn * (r * jnp.mean(u * xn, axis=-1, keepdims=True))

        g0, g1 = [dict(rows=slice(k * (tm // 2), (k + 1) * (tm // 2))) for k in range(2)]
        for stage, st in [(gather, g0), (back, g0), (gather, g1), (norm, g0), (back, g1), (norm, g1)]:
            stage(st)
        dsh_ref[0:1, :] += g0["dsh"] + g1["dsh"]
        p_ref[0:1, :] += g0["p"] + g1["p"]

    def tok(w):
        return pl.BlockSpec((tm, w), lambda i: (i, 0))

    return _pcall(
        body, [x, dx1, a1, sh1, w_in, dq, dkv, dz, dxbc, ddt], name="in_proj_bwd", grid=(s // tm,),
        out_shape=[jax.ShapeDtypeStruct((s, D), F32), jax.ShapeDtypeStruct((s, D), BF16),
                   jax.ShapeDtypeStruct((8, D), F32), jax.ShapeDtypeStruct((8, D), F32)],
        in_specs=[tok(D), tok(D), _row((1, D)), _row((1, D)), _row((PROJ_W, D)),
                  tok(QW), tok(2 * KVW), tok(SW), tok(XBCW), tok(128)],
        out_specs=[tok(D), tok(D), _row((8, D)), _row((8, D))], carry=carry)


def _attn_geometry():
    dist = np.arange(BLK)[:, None] + BLK - np.arange(2 * BLK)[None, :]
    n = np.maximum(dist, 0)
    max_exact = NBUCKET // 2
    large = max_exact + (np.log(np.maximum(n, 1) / max_exact) / np.log(128 / max_exact)
                         * (NBUCKET - max_exact)).astype(np.int32)
    large = np.minimum(large, NBUCKET - 1)
    bucket = np.where(n < max_exact, n, large).astype(np.int32)
    mask = (dist >= 0) & (dist < 128)
    return bucket, mask


def _attn_heads(is_first, q_blk, kvw, bias_ref, sinks_ref):
    qv = q_blk * 0.125
    col = lax.broadcasted_iota(jnp.int32, (BLK, 2 * BLK), 1)
    first = jnp.where(jnp.logical_and(is_first, col < BLK), NEG, 0.0)
    groups = []
    for g in range(NKV):
        qs = jnp.concatenate([qv[:, (4 * g + r) * HD:(4 * g + r + 1) * HD] for r in range(4)], axis=0)
        kw = kvw[:, g * HD:(g + 1) * HD]
        vw = kvw[:, KVW + g * HD:KVW + (g + 1) * HD]
        sc = _mm_nt(qs, kw)
        pn, ps = [], []
        for r in range(4):
            h = 4 * g + r
            sr = sc[r * BLK:(r + 1) * BLK] + bias_ref[h] + first
            sink = sinks_ref[h]
            m = jnp.maximum(jnp.max(sr, axis=-1, keepdims=True), sink)
            p = jnp.exp(sr - m)
            es = jnp.exp(sink - m)
            inv = 1.0 / (jnp.sum(p, axis=-1, keepdims=True) + es)
            pn.append(p * inv)
            ps.append(es * inv)
        pn = jnp.concatenate(pn, axis=0)
        ps = jnp.concatenate(ps, axis=0)
        o = _mm(pn.astype(BF16), vw)
        groups.append((qs, kw, vw, pn, ps, o))
    return groups


def _unstack_heads(parts):
    return jnp.concatenate([p[r * BLK:(r + 1) * BLK] for p in parts for r in range(4)], axis=1)


NB = 2


def _attn_fwd(q, kv, bias, sinks, nw, carry=None):
    s = q.shape[0]

    def body(q_ref, kvp_ref, kvc_ref, bias_ref, sinks_ref, nw_ref, y_ref):
        t = pl.program_id(0)
        kv3 = jnp.concatenate([kvp_ref[...], kvc_ref[...]], axis=0)
        for sub in range(NB):
            rows = slice(sub * BLK, (sub + 1) * BLK)
            groups = _attn_heads(jnp.logical_and(t == 0, sub == 0), q_ref[rows, :], kv3[sub * BLK:(sub + 2) * BLK],
                                 bias_ref, sinks_ref)
            o = _unstack_heads([g[5] for g in groups])
            r = lax.rsqrt(jnp.mean(o * o, axis=-1, keepdims=True) + EPS)
            y_ref[rows, :] = (o * r * nw_ref[...]).astype(BF16)

    return _pcall(
        body, [q, kv, kv, bias, sinks, nw], name="attn_fwd", grid=(s // (NB * BLK),),
        out_shape=[jax.ShapeDtypeStruct((s, QW), BF16)],
        in_specs=[pl.BlockSpec((NB * BLK, QW), lambda t: (t, 0)),
                  pl.BlockSpec((BLK, 2 * KVW), lambda t: (jnp.maximum(NB * t - 1, 0), 0)),
                  pl.BlockSpec((NB * BLK, 2 * KVW), lambda t: (t, 0)),
                  _row((NH, BLK, 2 * BLK)),
                  pl.BlockSpec(memory_space=pltpu.SMEM),
                  _row((1, QW))],
        out_specs=[pl.BlockSpec((NB * BLK, QW), lambda t: (t, 0))], carry=carry)


def _attn_bwd(q, kv, dya, bias, sinks, nw, act, dx2, gate2, w_dn, carry=None):
    s = q.shape[0]
    nt = s // (NB * BLK)
    npiece = DFF // NB

    def body(q_ref, kvp_ref, kvc_ref, dy_ref, bias_ref, sinks_ref, nw_ref, act_ref, dx2_ref, g2_ref, wdn_ref,
             dq_ref, dkv_ref, dbias_ref, dsink_ref, dnw_ref, gdn_hbm, dg2_ref, carry_ref, held_ref, acc_ref, sem):
        t = pl.program_id(0)

        @pl.when(t == 0)
        def _():
            carry_ref[...] = jnp.zeros_like(carry_ref)
            held_ref[...] = jnp.zeros_like(held_ref)
            dbias_ref[...] = jnp.zeros_like(dbias_ref)
            dsink_ref[...] = jnp.zeros_like(dsink_ref)
            dnw_ref[...] = jnp.zeros_like(dnw_ref)
            acc_ref[...] = jnp.zeros_like(acc_ref)

        def wgrad_piece(sub):
            rows = slice(sub * npiece, (sub + 1) * npiece)
            acc_ref[rows, :] += _mm_tn(act_ref[:, rows], dx2_ref[...].astype(BF16))

        def block(sub, kv3):
            rows = slice(sub * BLK, (sub + 1) * BLK)
            groups = _attn_heads(jnp.logical_and(t == 0, sub == 0), q_ref[rows, :], kv3[sub * BLK:(sub + 2) * BLK],
                                 bias_ref, sinks_ref)
            o = _unstack_heads([g[5] for g in groups])
            r = lax.rsqrt(jnp.mean(o * o, axis=-1, keepdims=True) + EPS)
            dy = dy_ref[rows, :].astype(F32)
            on = o * r
            dnw_ref[0:1, :] += jnp.sum(dy * on, axis=0, keepdims=True)
            u = dy * nw_ref[...]
            do = r * u - on * (r * jnp.mean(u * on, axis=-1, keepdims=True))
            dq_parts, dk_parts, dv_parts = [], [], []
            for g, (qs, kw, vw, pn, ps, og) in enumerate(groups):
                dos = jnp.concatenate([do[:, (4 * g + r_) * HD:(4 * g + r_ + 1) * HD] for r_ in range(4)], axis=0)
                delta = jnp.sum(dos * og, axis=-1, keepdims=True)
                dp = _mm_nt(dos.astype(BF16), vw)
                ds = pn * (dp - delta)
                dsk = ps * delta
                lane = lax.broadcasted_iota(jnp.int32, (1, 128), 1)
                for r_ in range(4):
                    h = 4 * g + r_
                    dbias_ref[h] += ds[r_ * BLK:(r_ + 1) * BLK]
                    dsink_ref[0:1, :] -= jnp.where(lane == h, jnp.sum(dsk[r_ * BLK:(r_ + 1) * BLK]), 0.0)
                dsb = ds.astype(BF16)
                dq_parts.append(_mm(dsb, kw) * 0.125)
                dk_parts.append(_mm_tn(dsb, qs))
                dv_parts.append(_mm_tn(pn.astype(BF16), dos.astype(BF16)))
            dq_ref[rows, :] = _unstack_heads(dq_parts).astype(BF16)
            return jnp.concatenate(dk_parts + dv_parts, axis=1)

        @pl.when(t < nt)
        def _():
            kv3 = jnp.concatenate([kvp_ref[...], kvc_ref[...]], axis=0)
            tail = carry_ref[...]
            for sub in range(NB):
                d = block(sub, kv3)
                done = tail + d[0:BLK]
                if sub == 0:
                    dkv_ref[0:(NB - 1) * BLK, :] = held_ref[...].astype(BF16)
                    dkv_ref[(NB - 1) * BLK:NB * BLK, :] = done.astype(BF16)
                else:
                    held_ref[(sub - 1) * BLK:sub * BLK, :] = done
                tail = d[BLK:2 * BLK]
                wgrad_piece(sub)
            carry_ref[...] = tail

        @pl.when(t == nt)
        def _():
            dkv_ref[0:(NB - 1) * BLK, :] = held_ref[...].astype(BF16)
            dkv_ref[(NB - 1) * BLK:NB * BLK, :] = carry_ref[...].astype(BF16)
            acc = acc_ref[...]
            dg2_ref[...] = jnp.zeros_like(dg2_ref)
            dg2_ref[0:1, :] = jnp.sum(acc * wdn_ref[...].astype(F32), axis=0, keepdims=True)
            acc_ref[...] = acc * g2_ref[...]
            cp = pltpu.make_async_copy(acc_ref, gdn_hbm, sem)
            cp.start()
            cp.wait()

    last = nt - 1
    tile = lambda w: pl.BlockSpec((NB * BLK, w), lambda t: (jnp.minimum(t, last), 0))
    return _pcall(
        body, [q, kv, kv, dya, bias, sinks, nw, act, dx2, gate2, w_dn], name="attn_bwd", grid=(nt + 1,),
        out_shape=[jax.ShapeDtypeStruct((s, QW), BF16), jax.ShapeDtypeStruct((s, 2 * KVW), BF16),
                   jax.ShapeDtypeStruct((NH, BLK, 2 * BLK), F32), jax.ShapeDtypeStruct((NH, 128), F32),
                   jax.ShapeDtypeStruct((8, QW), F32), jax.ShapeDtypeStruct((DFF, D), F32),
                   jax.ShapeDtypeStruct((8, D), F32)],
        in_specs=[tile(QW),
                  pl.BlockSpec((BLK, 2 * KVW), lambda t: (jnp.clip(NB * t - 1, 0, NB * nt - 1), 0)),
                  tile(2 * KVW), tile(QW),
                  _row((NH, BLK, 2 * BLK)),
                  pl.BlockSpec(memory_space=pltpu.SMEM),
                  _row((1, QW)), tile(DFF), tile(D), _row((1, D)), _row((DFF, D))],
        out_specs=[tile(QW),
                   pl.BlockSpec((NB * BLK, 2 * KVW), lambda t: (jnp.maximum(t - 1, 0), 0)),
                   _row((NH, BLK, 2 * BLK)), _row((NH, 128)), _row((8, QW)),
                   pl.BlockSpec(memory_space=pl.ANY), _row((8, D))],
        scratch_shapes=[pltpu.VMEM((BLK, 2 * KVW), F32), pltpu.VMEM(((NB - 1) * BLK, 2 * KVW), F32),
                        pltpu.VMEM((DFF, D), F32), pltpu.SemaphoreType.DMA], carry=carry)


def _rel_bias_grad(dbias, bucket):
    def body(db_ref, bk_ref, o_ref):
        bk = bk_ref[...]
        lane = lax.broadcasted_iota(jnp.int32, (1, 128), 1)
        for b in range(NBUCKET):
            sel = bk == b
            row = jnp.zeros((1, 128), F32)
            for h in range(NH):
                row = row + jnp.where(lane == h, jnp.sum(jnp.where(sel, db_ref[h], 0.0)), 0.0)
            o_ref[b:b + 1, :] = row

    return pl.pallas_call(
        body, name="rel_bias_grad",
        out_shape=jax.ShapeDtypeStruct((NBUCKET, 128), F32),
    )(dbias, bucket)


def _ssd_consts():
    head_of_lane = np.arange(SW) // HD
    expand = (np.arange(128)[:, None] == head_of_lane[None, :]).astype(np.float32)
    tril = np.tril(np.ones((BLK, BLK), np.float32))
    return (jnp.asarray(expand, BF16), jnp.asarray(expand.T.copy(), BF16), jnp.asarray(tril, BF16),
            jnp.asarray(tril.T.copy(), BF16))


def _conv_pre(xc, halo, cw, cb):
    ext = jnp.concatenate([halo, xc], axis=0)
    taps = [xc if k == CK - 1 else pltpu.roll(ext, CK - 1 - k, 0)[8:8 + BLK] for k in range(CK)]
    return cb + sum(cw[k:k + 1, :] * taps[k] for k in range(CK))


def _ssd_chunk(pre, dtr, dtb, av, dkv, ex, tril, h_in):
    sp = _sig(pre)
    xbc = pre * sp
    xs, bm, cm = xbc[:, 0:SW], xbc[:, SW:SW + 2 * NST], xbc[:, SW + 2 * NST:]
    dtin = dtr + dtb
    dt = jnp.maximum(dtin, 0.0) + jnp.log1p(jnp.exp(-jnp.abs(dtin)))
    cs = _sel_l(tril, dt * av)
    cst = cs.T
    dtx = _sel_r(dt, ex)
    csx = _sel_r(cs, ex)
    xdt = xs * dtx
    csl = csx[BLK - 1:BLK, :]
    decx = jnp.exp(csl - csx)
    ecsx = jnp.exp(csx)
    ecl = jnp.exp(csl)
    causal = tril.astype(F32) > 0.5
    ydiag, yoff, cbs, lms = [], [], [], []
    for g in range(2):
        bg = bm[:, g * NST:(g + 1) * NST].astype(BF16)
        cg = cm[:, g * NST:(g + 1) * NST].astype(BF16)
        cb = _mm_nt(cg, bg)
        cbs.append(cb)
        yoff.append(_mm(cg, h_in[:, g * 256:(g + 1) * 256].astype(BF16)))
        for r in range(4):
            h = 4 * g + r
            seg = cs[:, h:h + 1] - cst[h:h + 1, :]
            lm = jnp.where(causal, jnp.exp(jnp.minimum(seg, 0.0)), 0.0)
            lms.append(lm)
            ydiag.append(_mm((cb * lm).astype(BF16), xdt[:, h * HD:(h + 1) * HD].astype(BF16)))
    yoff = jnp.concatenate(yoff, axis=1) * ecsx
    y = jnp.concatenate(ydiag, axis=1) + yoff + dkv * xs
    return dict(pre=pre, sp=sp, xs=xs, bm=bm, cm=cm, dtin=dtin, dt=dt, av=av, cs=cs, cst=cst,
                dtx=dtx, csx=csx, xdt=xdt, decx=decx, ecsx=ecsx, ecl=ecl, causal=causal, cbs=cbs, lms=lms,
                yoff=yoff, y=y)


def _group_mean(t):
    m0 = jnp.mean(t[:, 0:256], axis=-1, keepdims=True)
    m1 = jnp.mean(t[:, 256:512], axis=-1, keepdims=True)
    return jnp.concatenate([jnp.broadcast_to(m0, (t.shape[0], 256)), jnp.broadcast_to(m1, (t.shape[0], 256))], axis=1)


SUBS = 4


def _ssd_fwd(z, xbc, dtr, cw, cb, dtb, av, dk, nw, carry=None):
    s = z.shape[0]
    nc = s // BLK
    tile = SUBS * BLK
    ex, _, tril, _ = _ssd_consts()

    def body(z_ref, xc_ref, xh_ref, dtr_ref, cw_ref, cb_ref, dtb_ref, a_ref, dk_ref, nw_ref, ex_ref, tril_ref,
             y_ref, hs_ref, pre_ref, h_ref):
        t = pl.program_id(0)

        @pl.when(t == 0)
        def _():
            h_ref[...] = jnp.zeros_like(h_ref)

        h_in = h_ref[...]
        for sub in range(SUBS):
            rows = slice(sub * BLK, (sub + 1) * BLK)
            xc = xc_ref[rows, :]
            halo = jnp.where(t == 0, 0.0, xh_ref[...]) if sub == 0 else xc_ref[sub * BLK - 8:sub * BLK, :]
            pre = _conv_pre(xc, halo, cw_ref[...], cb_ref[...])
            pre_ref[rows, :] = pre
            hs_ref[sub] = h_in
            f = _ssd_chunk(pre, dtr_ref[rows, :], dtb_ref[...], a_ref[...], dk_ref[...], ex_ref[...], tril_ref[...], h_in)
            dx = (f["decx"] * f["xdt"]).astype(BF16)
            st = [_mm_tn(f["bm"][:, g * NST:(g + 1) * NST].astype(BF16), dx[:, g * 256:(g + 1) * 256]) for g in range(2)]
            h_in = h_in * f["ecl"] + jnp.concatenate(st, axis=1)
            zv = z_ref[rows, :]
            tg = f["y"] * (zv * _sig(zv))
            r = lax.rsqrt(_group_mean(tg * tg) + EPS)
            y_ref[rows, :] = (tg * r * nw_ref[...]).astype(BF16)
        h_ref[...] = h_in

    cur = lambda w: pl.BlockSpec((tile, w), lambda t: (t, 0))
    return _pcall(
        body, [z, xbc, xbc, dtr, cw, cb, dtb, av, dk, nw, ex, tril], name="ssd_fwd", grid=(s // tile,),
        out_shape=[jax.ShapeDtypeStruct((s, SW), BF16), jax.ShapeDtypeStruct((nc, NST, SW), F32),
                   jax.ShapeDtypeStruct((s, XBCW), F32)],
        in_specs=[cur(SW), cur(XBCW), pl.BlockSpec((8, XBCW), lambda t: (jnp.maximum(t * (tile // 8) - 1, 0), 0)),
                  cur(128), _row((8, XBCW)), _row((1, XBCW)), _row((1, 128)),
                  _row((1, 128)), _row((1, SW)), _row((1, SW)), _row((128, SW)), _row((BLK, BLK))],
        out_specs=[cur(SW), pl.BlockSpec((SUBS, NST, SW), lambda t: (t, 0, 0)), cur(XBCW)],
        scratch_shapes=[pltpu.VMEM((NST, SW), F32)], carry=carry)


def _ssd_bwd(z, xbc, pre_all, dtr, dys, hs, cw, dtb, av, dk, nw, carry=None):
    s = z.shape[0]
    tile = SUBS * BLK
    nt = s // tile
    ex, ext_t, tril, triu = _ssd_consts()

    def body(z_ref, xc_ref, pre_ref, dtr_ref, dy_ref, hs_ref, cw_ref, dtb_ref, a_ref, dk_ref, nw_ref,
             ex_ref, ext_ref, tril_ref, triu_ref,
             dz_ref, dxbc_ref, ddt_ref, dcw_ref, dcb_ref, dnw_ref, dhd_ref, dh_ref, nxt_ref, dd_ref):
        i = pl.program_id(0)

        @pl.when(i == 0)
        def _():
            dh_ref[...] = jnp.zeros_like(dh_ref)
            nxt_ref[...] = jnp.zeros_like(nxt_ref)
            dd_ref[...] = jnp.zeros_like(dd_ref)
            dcw_ref[...] = jnp.zeros_like(dcw_ref)
            dcb_ref[...] = jnp.zeros_like(dcb_ref)
            dnw_ref[...] = jnp.zeros_like(dnw_ref)
            dhd_ref[...] = jnp.zeros_like(dhd_ref)

        gst, nxt = dh_ref[...], nxt_ref[...]
        for sub in reversed(range(SUBS)):
            rows = slice(sub * BLK, (sub + 1) * BLK)
            gst, nxt = chunk(sub, rows, gst, nxt, z_ref, xc_ref, pre_ref, dtr_ref, dy_ref, hs_ref, cw_ref, dtb_ref,
                             a_ref, dk_ref, nw_ref, ex_ref, ext_ref, tril_ref, triu_ref,
                             dz_ref, dxbc_ref, ddt_ref, dcw_ref, dcb_ref, dnw_ref, dhd_ref, dd_ref)
        dh_ref[...] = gst
        nxt_ref[...] = nxt

        @pl.when(i == nt - 1)
        def _():
            dhd_ref[2:3, :] = _sel_r(dd_ref[...], ext_ref[...])[0:1, :]

    def chunk(sub, rows, gst, nxt, z_ref, xc_ref, pre_ref, dtr_ref, dy_ref, hs_ref, cw_ref, dtb_ref,
              a_ref, dk_ref, nw_ref, ex_ref, ext_ref, tril_ref, triu_ref,
              dz_ref, dxbc_ref, ddt_ref, dcw_ref, dcb_ref, dnw_ref, dhd_ref, dd_ref):
        h_in = hs_ref[sub]
        f = _ssd_chunk(pre_ref[rows, :], dtr_ref[rows, :], dtb_ref[...], a_ref[...], dk_ref[...], ex_ref[...],
                       tril_ref[...], h_in)
        xs, xdt, decx, ecsx, ecl, dtx = f["xs"], f["xdt"], f["decx"], f["ecsx"], f["ecl"], f["dtx"]
        cs, cst, causal = f["cs"], f["cst"], f["causal"]
        causal_t = triu_ref[...].astype(F32) > 0.5

        zv = z_ref[rows, :]
        sz = _sig(zv)
        gz = zv * sz
        t = f["y"] * gz
        r = lax.rsqrt(_group_mean(t * t) + EPS)
        tn_ = t * r
        dyn = dy_ref[rows, :].astype(F32)
        dnw_ref[0:1, :] += jnp.sum(dyn * tn_, axis=0, keepdims=True)
        u = dyn * nw_ref[...]
        dt_ = r * u - tn_ * (r * _group_mean(u * tn_))
        dy = dt_ * gz
        dz_ref[rows, :] = (dt_ * f["y"] * (sz * (1.0 + zv * (1.0 - sz)))).astype(BF16)

        dd_ref[0:1, :] += jnp.sum(dy * xs, axis=0, keepdims=True)
        dxs = dk_ref[...] * dy

        edy = ecsx * dy
        dxdt, dbs, dcs_, dcsx_parts, dh_new = [], [], [], [], []
        lane = lax.broadcasted_iota(jnp.int32, (1, 128), 1)
        dcs_intra = jnp.zeros((BLK, 128), F32)
        for g in range(2):
            sl = slice(g * 256, (g + 1) * 256)
            bgf, cgf = f["bm"][:, g * NST:(g + 1) * NST], f["cm"][:, g * NST:(g + 1) * NST]
            bg, cg = bgf.astype(BF16), cgf.astype(BF16)
            gg = gst[:, sl].astype(BF16)
            hg = h_in[:, sl].astype(BF16)
            edyg = edy[:, sl].astype(BF16)
            dc = _mm_nt(edyg, hg)
            dh_new.append(gst[:, sl] * ecl[:, sl] + _mm_tn(cg, edyg))
            bgm = _mm(bg, gg)
            dxdt_g = decx[:, sl] * bgm
            dxg = (decx[:, sl] * xdt[:, sl]).astype(BF16)
            db = _mm_nt(dxg, gg)
            qd = bgm * xdt[:, sl] * decx[:, sl]
            last = jnp.sum(qd, axis=0, keepdims=True) + ecl[:, sl] * jnp.sum(gst[:, sl] * h_in[:, sl], axis=0, keepdims=True)
            rowid = lax.broadcasted_iota(jnp.int32, (BLK, 256), 0)
            dcsx_parts.append(f["yoff"][:, sl] * dy[:, sl] - qd + jnp.where(rowid == BLK - 1, last, 0.0))
            cb_ = f["cbs"][g]
            cbt = _mm_nt(bg, cg)
            dcb_ = jnp.zeros((BLK, BLK), F32)
            dcbt = jnp.zeros((BLK, BLK), F32)
            dxd = []
            for r_ in range(4):
                h = 4 * g + r_
                hl = slice(h * HD, (h + 1) * HD)
                lm = f["lms"][h]
                segt = cst[h:h + 1, :] - cs[:, h:h + 1]
                lmt = jnp.where(causal_t, jnp.exp(jnp.minimum(segt, 0.0)), 0.0)
                dyh = dy[:, hl].astype(BF16)
                xdh = xdt[:, hl].astype(BF16)
                dw = _mm_nt(dyh, xdh)
                dwt = _mm_nt(xdh, dyh)
                wt = cbt * lmt
                dxd.append(_mm(wt.astype(BF16), dyh))
                dcb_ = dcb_ + dw * lm
                dcbt = dcbt + dwt * lmt
                col = jnp.sum(dw * (cb_ * lm), axis=-1, keepdims=True) - jnp.sum(dwt * wt, axis=-1, keepdims=True)
                dcs_intra = dcs_intra + jnp.where(lane == h, col, 0.0)
            dxdt.append(dxdt_g + jnp.concatenate(dxd, axis=1))
            dcs_.append(dc + _mm(dcb_.astype(BF16), bg))
            dbs.append(db + _mm(dcbt.astype(BF16), cg))
        dxdt = jnp.concatenate(dxdt, axis=1)
        dxs = dxs + dxdt * dtx
        ext_t_ = ext_ref[...]
        dcs = dcs_intra + _sel_r(jnp.concatenate(dcsx_parts, axis=1), ext_t_)
        da = _sel_l(triu_ref[...], dcs)
        ddt = da * f["av"] + _sel_r(dxdt * xs, ext_t_)
        dhd_ref[1:2, :] += jnp.sum(da * f["dt"], axis=0, keepdims=True)
        ddtr = ddt * _sig(f["dtin"])
        dhd_ref[0:1, :] += jnp.sum(ddtr, axis=0, keepdims=True)
        ddt_ref[rows, :] = ddtr.astype(BF16)

        sp, pre = f["sp"], f["pre"]
        dact = jnp.concatenate([dxs] + dbs + dcs_, axis=1)
        dpre = dact * (sp * (1.0 + pre * (1.0 - sp)))
        dcb_ref[0:1, :] += jnp.sum(dpre, axis=0, keepdims=True)
        ext2 = jnp.concatenate([dpre, nxt], axis=0)
        shifted = [pltpu.roll(ext2, BLK + 8 - (CK - 1 - k), 0)[0:BLK] for k in range(CK - 1)] + [dpre]
        cw = cw_ref[...]
        xc = xc_ref[rows, :]
        dxr = cw[CK - 1:CK, :] * dpre
        for k in range(CK):
            dcw_ref[k:k + 1, :] += jnp.sum(shifted[k] * xc, axis=0, keepdims=True)
            if k < CK - 1:
                dxr = dxr + cw[k:k + 1, :] * shifted[k]
        dxbc_ref[rows, :] = dxr.astype(BF16)
        return jnp.concatenate(dh_new, axis=1), dpre[0:8]

    cur = lambda w: pl.BlockSpec((tile, w), lambda i: (nt - 1 - i, 0))
    return _pcall(
        body, [z, xbc, pre_all, dtr, dys, hs, cw, dtb, av, dk, nw, ex, ext_t, tril, triu], name="ssd_bwd", grid=(nt,),
        out_shape=[jax.ShapeDtypeStruct((s, SW), BF16), jax.ShapeDtypeStruct((s, XBCW), BF16),
                   jax.ShapeDtypeStruct((s, 128), BF16), jax.ShapeDtypeStruct((8, XBCW), F32),
                   jax.ShapeDtypeStruct((8, XBCW), F32), jax.ShapeDtypeStruct((8, SW), F32),
                   jax.ShapeDtypeStruct((8, 128), F32)],
        in_specs=[cur(SW), cur(XBCW), cur(XBCW), cur(128), cur(SW),
                  pl.BlockSpec((SUBS, NST, SW), lambda i: (nt - 1 - i, 0, 0)),
                  _row((8, XBCW)), _row((1, 128)), _row((1, 128)), _row((1, SW)), _row((1, SW)),
                  _row((128, SW)), _row((SW, 128)), _row((BLK, BLK)), _row((BLK, BLK))],
        out_specs=[cur(SW), cur(XBCW), cur(128), _row((8, XBCW)), _row((8, XBCW)), _row((8, SW)), _row((8, 128))],
        scratch_shapes=[pltpu.VMEM((NST, SW), F32), pltpu.VMEM((8, XBCW), F32), pltpu.VMEM((8, SW), F32)], carry=carry)


def _load_once(i, pairs, sem):
    @pl.when(i == 0)
    def _():
        cps = [pltpu.make_async_copy(src, dst, sem.at[k]) for k, (src, dst) in enumerate(pairs)]
        for cp in cps:
            cp.start()
        for cp in cps:
            cp.wait()


def _mlp_fwd(x, ya, ys, tgt, w_o, w_ga, w_gb, w_dn, gate1, a2, sh2, gate2, fn):
    s = x.shape[0]
    sub_m, subs = 256, 2
    tm = sub_m * subs

    def body(x_ref, ya_ref, ys_ref, t_ref, wo_hbm, wga_hbm, wgb_hbm, wdn_hbm, g1_ref, a2_ref, s2_ref, g2_ref, fn_ref,
             x1_ref, gu_ref, dx2_ref, loss_ref, dfn_ref, wo, wga, wgb, wdn, sem):
        i = pl.program_id(0)
        _load_once(i, [(wo_hbm, wo), (wga_hbm, wga), (wgb_hbm, wgb), (wdn_hbm, wdn)], sem)

        @pl.when(i == 0)
        def _():
            loss_ref[...] = jnp.zeros_like(loss_ref)
            dfn_ref[...] = jnp.zeros_like(dfn_ref)

        def proj(st):
            st["mix"] = _mm(ya_ref[st["rows"], :], wo[0:QW, :]) + _mm(ys_ref[st["rows"], :], wo[QW:D, :])

        def norm(st):
            x1 = x_ref[st["rows"], :] + g1_ref[...] * st.pop("mix")
            x1_ref[st["rows"], :] = x1
            r2 = lax.rsqrt(jnp.mean(x1 * x1, axis=-1, keepdims=True) + EPS)
            st["x1"] = x1
            st["h2"] = (x1 * r2 * a2_ref[...] + s2_ref[...]).astype(BF16)

        def gate_up(st):
            h2 = st.pop("h2")
            ha, hb = h2[:, 0:D // 2], h2[:, D // 2:D]
            gub = jnp.concatenate([(_mm(ha, wga[j]) + _mm(hb, wgb[j])).astype(BF16) for j in range(4)], axis=1)
            gu_ref[st["rows"], :] = gub
            st["gub"] = gub

        def activate(st):
            gub = st.pop("gub")
            gv, uv = gub[:, 0:DFF].astype(F32), gub[:, DFF:].astype(F32)
            st["act"] = (gv * _sig(gv) * uv).astype(BF16)

        def down(st):
            st["ff"] = _mm(st.pop("act"), wdn[...])

        def head(st):
            x2 = st.pop("x1") + g2_ref[...] * st.pop("ff")
            r3 = lax.rsqrt(jnp.mean(x2 * x2, axis=-1, keepdims=True) + EPS)
            xn = x2 * r3
            fnv = fn_ref[...]
            err = xn * fnv - t_ref[st["rows"], :]
            st["loss"] = jnp.sum(err * err) * (0.5 / D)
            dy = err * (1.0 / D)
            st["dfn"] = jnp.sum(dy * xn, axis=0, keepdims=True)
            u = dy * fnv
            dx2_ref[st["rows"], :] = r3 * u - xn * (r3 * jnp.mean(u * xn, axis=-1, keepdims=True))

        a, b = [dict(rows=slice(k * sub_m, (k + 1) * sub_m)) for k in range(subs)]
        for stage, st in [(proj, a), (norm, a), (proj, b), (gate_up, a), (norm, b), (activate, a), (gate_up, b),
                          (down, a), (activate, b), (head, a), (down, b), (head, b)]:
            stage(st)
        loss_ref[...] += a["loss"] + b["loss"]
        dfn_ref[0:1, :] += a["dfn"] + b["dfn"]

    def tok(w):
        return pl.BlockSpec((tm, w), lambda i: (i, 0))

    hbm = pl.BlockSpec(memory_space=pl.ANY)
    return pl.pallas_call(
        body, name="mlp_fwd", grid=(s // tm,),
        out_shape=[jax.ShapeDtypeStruct((s, D), F32), jax.ShapeDtypeStruct((s, 2 * DFF), BF16),
                   jax.ShapeDtypeStruct((s, D), F32), jax.ShapeDtypeStruct((8, 128), F32),
                   jax.ShapeDtypeStruct((8, D), F32)],
        in_specs=[tok(D), tok(QW), tok(SW), tok(D), hbm, hbm, hbm, hbm,
                  _row((1, D)), _row((1, D)), _row((1, D)), _row((1, D)), _row((1, D))],
        out_specs=[tok(D), tok(2 * DFF), tok(D), _row((8, 128)), _row((8, D))],
        scratch_shapes=[pltpu.VMEM((D, D), BF16), pltpu.VMEM(w_ga.shape, BF16), pltpu.VMEM(w_gb.shape, BF16),
                        pltpu.VMEM((DFF, D), BF16), pltpu.SemaphoreType.DMA((4,))],
        compiler_params=_cp(("arbitrary",)),
    )(x, ya, ys, tgt, w_o, w_ga, w_gb, w_dn, gate1, a2, sh2, gate2, fn)


def _mlp_bwd(x1, gu, dx2, w_o, w_ga, w_gb, w_dn, gate1, a2, sh2, gate2):
    s = x1.shape[0]
    tm = 256
    nj = 2 * DFF // 4

    def body(x1_ref, gu_ref, dx2_ref, wo_hbm, wga_hbm, wgb_hbm, wdn_hbm, g1_ref, a2_ref, s2_ref, g2_ref,
             dx1_ref, dya_ref, dys_ref, act_ref, dgu_ref, h2_ref, dsh_ref, p_ref, wo, wga, wgb, wdn, sem):
        i = pl.program_id(0)
        _load_once(i, [(wo_hbm, wo), (wga_hbm, wga), (wgb_hbm, wgb), (wdn_hbm, wdn)], sem)

        @pl.when(i == 0)
        def _():
            dsh_ref[...] = jnp.zeros_like(dsh_ref)
            p_ref[...] = jnp.zeros_like(p_ref)

        dx2 = dx2_ref[...]
        dact = _mm_nt((dx2 * g2_ref[...]).astype(BF16), wdn[...])
        gub = gu_ref[...]
        gv, uv = gub[:, 0:DFF].astype(F32), gub[:, DFF:].astype(F32)
        sg = _sig(gv)
        sl = gv * sg
        act_ref[...] = (sl * uv).astype(BF16)
        dgu = jnp.concatenate([dact * uv * (sg * (1.0 + gv * (1.0 - sg))), dact * sl], axis=1).astype(BF16)
        dgu_ref[...] = dgu
        dha = sum(_mm_nt(dgu[:, j * nj:(j + 1) * nj], wga[j]) for j in range(4))
        dhb = sum(_mm_nt(dgu[:, j * nj:(j + 1) * nj], wgb[j]) for j in range(4))
        dh = jnp.concatenate([dha, dhb], axis=1)
        x1 = x1_ref[...]
        r2 = lax.rsqrt(jnp.mean(x1 * x1, axis=-1, keepdims=True) + EPS)
        xn = x1 * r2
        a2 = a2_ref[...]
        h2_ref[...] = (xn * a2 + s2_ref[...]).astype(BF16)
        dsh_ref[0:1, :] += jnp.sum(dh, axis=0, keepdims=True)
        p_ref[0:1, :] += jnp.sum(dh * xn, axis=0, keepdims=True)
        u = dh * a2
        dx1 = dx2 + r2 * u - xn * (r2 * jnp.mean(u * xn, axis=-1, keepdims=True))
        dx1_ref[...] = dx1
        dcat = _mm_nt((dx1 * g1_ref[...]).astype(BF16), wo[...])
        dya_ref[...] = dcat[:, 0:QW].astype(BF16)
        dys_ref[...] = dcat[:, QW:D].astype(BF16)

    def tok(w):
        return pl.BlockSpec((tm, w), lambda i: (i, 0))

    hbm = pl.BlockSpec(memory_space=pl.ANY)
    return pl.pallas_call(
        body, name="mlp_bwd", grid=(s // tm,),
        out_shape=[jax.ShapeDtypeStruct((s, D), F32), jax.ShapeDtypeStruct((s, QW), BF16),
                   jax.ShapeDtypeStruct((s, SW), BF16), jax.ShapeDtypeStruct((s, DFF), BF16),
                   jax.ShapeDtypeStruct((s, 2 * DFF), BF16), jax.ShapeDtypeStruct((s, D), BF16),
                   jax.ShapeDtypeStruct((8, D), F32), jax.ShapeDtypeStruct((8, D), F32)],
        in_specs=[tok(D), tok(2 * DFF), tok(D), hbm, hbm, hbm, hbm, _row((1, D)), _row((1, D)), _row((1, D)), _row((1, D))],
        out_specs=[tok(D), tok(QW), tok(SW), tok(DFF), tok(2 * DFF), tok(D), _row((8, D)), _row((8, D))],
        scratch_shapes=[pltpu.VMEM((D, D), BF16), pltpu.VMEM(w_ga.shape, BF16), pltpu.VMEM(w_gb.shape, BF16),
                        pltpu.VMEM((DFF, D), BF16), pltpu.SemaphoreType.DMA((4,))],
        compiler_params=_cp(("arbitrary",)),
    )(x1, gu, dx2, w_o, w_ga, w_gb, w_dn, gate1, a2, sh2, gate2)


def _wgrad(name, a, b, gate, w, carry=None):
    s, m = a.shape
    n = b.shape[1]
    tk = min(1024, s)
    nk = s // tk

    def body(a_ref, b_ref, g_ref, w_ref, o_hbm, dg_ref, acc_ref, sem):
        k = pl.program_id(0)

        @pl.when(k == 0)
        def _():
            acc_ref[...] = jnp.zeros_like(acc_ref)

        acc_ref[...] += _mm_tn(a_ref[...], b_ref[...].astype(BF16))

        @pl.when(k == nk - 1)
        def _():
            acc = acc_ref[...]
            dg_ref[...] = jnp.zeros_like(dg_ref)
            dg_ref[0:1, :] = jnp.sum(acc * w_ref[...].astype(F32), axis=0, keepdims=True)
            acc_ref[...] = acc * g_ref[...]
            cp = pltpu.make_async_copy(acc_ref, o_hbm, sem)
            cp.start()
            cp.wait()

    return _pcall(body, [a, b, gate, w], name=name, grid=(nk,),
                  out_shape=[jax.ShapeDtypeStruct((m, n), F32), jax.ShapeDtypeStruct((8, n), F32)],
                  in_specs=[pl.BlockSpec((tk, m), lambda k: (k, 0)), pl.BlockSpec((tk, n), lambda k: (k, 0)),
                            _row((1, n)), _row((m, n))],
                  out_specs=[pl.BlockSpec(memory_space=pl.ANY), _row((8, n))],
                  scratch_shapes=[pltpu.VMEM((m, n), F32), pltpu.SemaphoreType.DMA], carry=carry)


def _wgrad_gate_up(h2, dgu, carry=None):
    s = h2.shape[0]
    tk = min(1024, s)
    nk = s // tk
    n = dgu.shape[1]
    nj = n // 4

    def body(a_ref, b_ref, o_hbm, acc_ref, sems):
        k = pl.program_id(0)

        @pl.when(k == 0)
        def _():
            acc_ref[...] = jnp.zeros_like(acc_ref)

        acc_ref[...] += _mm_tn(a_ref[...], b_ref[...])

        @pl.when(k == nk - 1)
        def _():
            cps = [pltpu.make_async_copy(acc_ref.at[:, pl.ds(j * nj, nj)], o_hbm.at[j], sems.at[j]) for j in range(4)]
            for cp in cps:
                cp.start()
            for cp in cps:
                cp.wait()

    return _pcall(body, [h2, dgu], name="wgrad_gate_up", grid=(nk,),
                  out_shape=[jax.ShapeDtypeStruct((4, D, nj), F32)],
                  in_specs=[pl.BlockSpec((tk, D), lambda k: (k, 0)), pl.BlockSpec((tk, n), lambda k: (k, 0))],
                  out_specs=[pl.BlockSpec(memory_space=pl.ANY)],
                  scratch_shapes=[pltpu.VMEM((D, n), F32), pltpu.SemaphoreType.DMA((4,))], carry=carry)


def _wgrad_in_t(h1, pieces, carry=None):
    s = h1.shape[0]
    tk = min(1024, s)
    nk = s // tk

    def body(a_ref, dq_ref, dkv_ref, dz_ref, dxbc_ref, ddt_ref, o_hbm, acc_ref, tr_ref, sem):
        k = pl.program_id(0)

        @pl.when(k == 0)
        def _():
            acc_ref[...] = jnp.zeros_like(acc_ref)

        dproj = jnp.concatenate([dq_ref[...], dkv_ref[...], dz_ref[...], dxbc_ref[...], ddt_ref[...]], axis=1)
        acc_ref[...] += _mm_tn(a_ref[...], dproj)

        @pl.when(k == nk - 1)
        def _():
            for j in range(PROJ_W // 128):
                tr_ref[j * 128:(j + 1) * 128, :] = acc_ref[:, j * 128:(j + 1) * 128].T
            cp = pltpu.make_async_copy(tr_ref, o_hbm, sem)
            cp.start()
            cp.wait()

    return _pcall(body, [h1] + list(pieces), name="wgrad_in", grid=(nk,),
                  out_shape=[jax.ShapeDtypeStruct((PROJ_W, D), F32)],
                  in_specs=[pl.BlockSpec((tk, p.shape[1]), lambda k: (k, 0)) for p in [h1] + list(pieces)],
                  out_specs=[pl.BlockSpec(memory_space=pl.ANY)],
                  scratch_shapes=[pltpu.VMEM((D, PROJ_W), F32), pltpu.VMEM((PROJ_W, D), F32), pltpu.SemaphoreType.DMA],
                  carry=carry)


_SMALL = ["ada_b", "norm1", "conv_w", "conv_b", "dt_bias", "A_log", "D_skip", "sinks", "attn_out_norm",
          "ssm_out_norm", "norm2", "rel_bias", "final_norm"]


def _small_grad(name, gs, chip):
    if name == "ada_b":
        return jnp.concatenate([gs[j:j + 1, :] for j in range(6)], axis=1)
    if name == "conv_w":
        full = gs[7:11, :]
        out = full[:, 0:256]
        for j in range(1, 4):
            out = jnp.where(chip == j, full[:, j * 256:(j + 1) * 256], out)
        return out
    row, width = {"norm1": (6, D), "conv_b": (11, D), "norm2": (12, D), "final_norm": (13, D),
                  "attn_out_norm": (14, QW), "ssm_out_norm": (15, SW), "dt_bias": (16, NH), "A_log": (17, NH),
                  "D_skip": (18, NH), "sinks": (19, NH), "rel_bias": (24, NH)}[name]
    rows = NBUCKET if name == "rel_bias" else 1
    return gs[row:row + rows, 0:width]


def _small_update(small_all, where, ws, ms, vs):
    n = len(_SMALL)

    def body(where_ref, sa_ref, *refs):
        w_refs, m_refs, v_refs, outs = refs[:n], refs[n:2 * n], refs[2 * n:3 * n], refs[3 * n:]
        gs = sa_ref[0]
        for b in range(1, 8):
            gs = gs + sa_ref[b]
        chip = where_ref[1]
        for i, name in enumerate(_SMALL):
            g = _small_grad(name, gs, chip)
            lead = (0,) if name == "conv_w" else ()
            d, mo, vo = _adamw(w_refs[i][lead + (...,)], g, m_refs[i][lead + (...,)], v_refs[i][lead + (...,)])
            for k, val in enumerate((g, d, mo, vo)):
                outs[k * n + i][lead + (...,)] = val
        outs[4 * n][...] = gs[20:21, 0:128]

    shapes = [jax.ShapeDtypeStruct(w.shape, F32) for w in ws]
    vmem = pl.BlockSpec(memory_space=pltpu.VMEM)
    res = pl.pallas_call(
        body, name="small_update", out_shape=shapes * 4 + [jax.ShapeDtypeStruct((1, 128), F32)],
        in_specs=[pl.BlockSpec(memory_space=pltpu.SMEM)] + [vmem] * (1 + 3 * n), out_specs=[vmem] * (4 * n + 1),
    )(where, small_all, *ws, *ms, *vs)
    return [res[k * n:(k + 1) * n] for k in range(4)], res[4 * n][0, 0]


def _add_half(name, g, got, where, by_cols=False):
    rr, cc = got.shape[1:]
    if by_cols:
        mine = pl.BlockSpec((None, rr, cc), lambda i, w_ref: (i, 0, w_ref[0]))
    else:
        mine = pl.BlockSpec((None, None, rr, cc), lambda i, w_ref: (i, w_ref[0], 0, 0))

    def body(w_ref, g_ref, r_ref, o_ref, own_ref):
        s = g_ref[...] + r_ref[...]
        o_ref[...] = s.astype(BF16)

        @pl.when(pl.program_id(0) == w_ref[1])
        def _():
            own_ref[...] = s

    spec = pl.BlockSpec((None, rr, cc), lambda i, w_ref: (i, 0, 0))
    return _pcall(body, [where, g, got], name=name, grid=(4,), nprefetch=1,
                  out_shape=[jax.ShapeDtypeStruct(got.shape, BF16), jax.ShapeDtypeStruct((rr, cc), F32)],
                  in_specs=[mine, spec],
                  out_specs=[spec, pl.BlockSpec((rr, cc), lambda i, w_ref: (0, 0))])


def _add_chips(name, own, got):
    rr, cc = own.shape
    tr = rr // 2 if rr % 32 == 0 else rr

    def body(s_ref, r_ref, o_ref):
        o_ref[...] = ((s_ref[...] + r_ref[0].astype(F32)) + r_ref[1].astype(F32)) + r_ref[2].astype(F32)

    spec = pl.BlockSpec((tr, cc), lambda i: (i, 0))
    return _pcall(body, [own, got], name=name, grid=(rr // tr,), out_shape=[jax.ShapeDtypeStruct((rr, cc), F32)],
                  in_specs=[spec, pl.BlockSpec((3, tr, cc), lambda i: (0, i, 0))], out_specs=[spec])[0]


def _adamw_halves(name, mine, got, w, m, v, where, by_cols=False):
    rr, cc = mine.shape

    def body(w_ref_, t_ref, r_ref, w_ref, m_ref, v_ref, g_ref, d_ref, mo_ref, vo_ref):
        g = jnp.where(pl.program_id(0) == w_ref_[0], t_ref[...], r_ref[...])
        g_ref[...] = g
        d_ref[...], mo_ref[...], vo_ref[...] = _adamw(w_ref[...], g, m_ref[...], v_ref[...])

    if by_cols:
        grid = (2, 1)
        half = pl.BlockSpec((rr, cc), lambda h, i, w_ref_: (0, 0))
        full = pl.BlockSpec((rr, cc), lambda h, i, w_ref_: (0, h))
    else:
        tr = rr // 2
        grid = (2, 2)
        half = pl.BlockSpec((tr, cc), lambda h, i, w_ref_: (i, 0))
        full = pl.BlockSpec((None, tr, cc), lambda h, i, w_ref_: (0, 2 * h + i, 0))
    return _pcall(body, [where, mine, got, w, m, v], name=name, grid=grid, nprefetch=1,
                  out_shape=[jax.ShapeDtypeStruct(w.shape, F32)] * 4,
                  in_specs=[half, half, full, full, full], out_specs=[full] * 4)


def _bias_table(rel_bias, bucket, mask):
    def body(rb_ref, bk_ref, mk_ref, o_ref):
        bk = bk_ref[...]
        valid = mk_ref[...] > 0
        for h in range(NH):
            acc = jnp.zeros((BLK, 2 * BLK), F32)
            for b in range(NBUCKET):
                acc = jnp.where(bk == b, rb_ref[b, h], acc)
            o_ref[h] = jnp.where(valid, acc, NEG)

    vmem = pl.BlockSpec(memory_space=pltpu.VMEM)
    return pl.pallas_call(
        body, name="bias_table", out_shape=jax.ShapeDtypeStruct((NH, BLK, 2 * BLK), F32),
        in_specs=[pl.BlockSpec(memory_space=pltpu.SMEM), vmem, vmem], out_specs=vmem,
    )(rel_bias, bucket, mask)


def _pack_small(dsh1, p1, dsh2, p2, dg1a, dg1b, dg2, norm1, norm2, scale1, scale2, dcw, dcb, dfn,
                dnw_attn, dnw_ssm, dhd, av, dsink, drel, loss_acc):
    def body(dsh1_ref, p1_ref, dsh2_ref, p2_ref, dg1a_ref, dg1b_ref, dg2_ref, n1_ref, n2_ref, s1_ref, s2_ref,
             dcw_ref, dcb_ref, dfn_ref, da_ref, ds_ref, dhd_ref, av_ref, dsink_ref, drel_ref, loss_ref, o_ref):
        o_ref[...] = jnp.zeros_like(o_ref)
        p1v, p2v = p1_ref[0:1, :], p2_ref[0:1, :]
        o_ref[0:1, :] = dsh1_ref[0:1, :]
        o_ref[1:2, :] = p1v * n1_ref[...]
        o_ref[2:3, :] = dg1a_ref[0:1, :] + dg1b_ref[0:1, :]
        o_ref[3:4, :] = dsh2_ref[0:1, :]
        o_ref[4:5, :] = p2v * n2_ref[...]
        o_ref[5:6, :] = dg2_ref[0:1, :]
        o_ref[6:7, :] = p1v * (1.0 + s1_ref[...])
        o_ref[7:11, :] = dcw_ref[0:4, :]
        o_ref[11:12, :] = dcb_ref[0:1, :]
        o_ref[12:13, :] = p2v * (1.0 + s2_ref[...])
        o_ref[13:14, :] = dfn_ref[0:1, :]
        o_ref[14:15, 0:QW] = da_ref[0:1, :]
        o_ref[15:16, 0:SW] = ds_ref[0:1, :]
        o_ref[16:17, 0:128] = dhd_ref[0:1, :]
        o_ref[17:18, 0:128] = dhd_ref[1:2, :] * av_ref[...]
        o_ref[18:19, 0:128] = dhd_ref[2:3, :]
        o_ref[19:20, 0:128] = dsink_ref[0:1, :]
        o_ref[20:21, 0:128] = loss_ref[0:1, :]
        o_ref[24:56, 0:128] = drel_ref[...]

    return pl.pallas_call(body, name="pack_small", out_shape=jax.ShapeDtypeStruct((56, D), F32))(
        dsh1, p1, dsh2, p2, dg1a, dg1b, dg2, norm1, norm2, scale1, scale2, dcw, dcb, dfn,
        dnw_attn, dnw_ssm, dhd, av, dsink, drel, loss_acc)


def _pad_row(a, rows=1):
    return jnp.pad(a.reshape(rows, -1), ((0, 0), (0, D - a.size // rows)))


def kernel(x, c, ada_w, ada_b, norm1, w_in, conv_w, conv_b, dt_bias, A_log, D_skip, sinks, attn_out_norm, ssm_out_norm, w_o, norm2, w_gate_up, w_down, rel_bias, final_norm, loss_target, m_ada_w, m_ada_b, m_norm1, m_w_in, m_conv_w, m_conv_b, m_dt_bias, m_A_log, m_D_skip, m_sinks, m_attn_out_norm, m_ssm_out_norm, m_w_o, m_norm2, m_w_gate_up, m_w_down, m_rel_bias, m_final_norm, v_ada_w, v_ada_b, v_norm1, v_w_in, v_conv_w, v_conv_b, v_dt_bias, v_A_log, v_D_skip, v_sinks, v_attn_out_norm, v_ssm_out_norm, v_w_o, v_norm2, v_w_gate_up, v_w_down, v_rel_bias, v_final_norm):
    xi, yi, ci = lax.axis_index("x"), lax.axis_index("y"), lax.axis_index("c")
    chip = 2 * xi + yi
    me = 4 * xi + 2 * yi + ci
    where = jnp.stack([ci, chip]).astype(jnp.int32)
    xs2, tgt = x[0], loss_target[0]

    first = jnp.concatenate([c, _pad_row(conv_w[0], CK), jnp.zeros((3, D), F32)], axis=0)
    w_in_t, m_w_in_t, v_w_in_t = w_in[0].T, m_w_in[0].T, v_w_in[0].T
    w_in_b, w_o_b, w_dn_b = w_in_t.astype(BF16), w_o[0].astype(BF16), w_down[0].astype(BF16)
    w_gu_b = w_gate_up[0].astype(BF16)
    first_all = _exchange("gather_cond", _gather8_carry(first))[0]
    c_all = first_all[:, 0, :]
    cw_full = jnp.concatenate([first_all[2 * j, 1:1 + CK, 0:256] for j in range(4)], axis=1)

    hw = D // 2
    fetch_half = _Carry(
        [w_in_b], [jax.ShapeDtypeStruct((4, w_in_b.shape[0], hw), BF16)],
        lambda x_, y_, c_: [(f, 0, (slice(None), pl.ds(c_ * hw, hw)), 0, 2 * x_ + y_) for f in [None] + _CHIPS3])
    ncol = ada_w.shape[2]
    mod_cols, w_half = _ada_fwd(c_all, ada_w[0], lax.dynamic_slice(ada_b, (0, chip * ncol), (1, ncol)),
                                carry=fetch_half)
    to_other = _Carry([w_half[None]], [jax.ShapeDtypeStruct((1,) + w_half.shape, BF16)],
                      lambda x_, y_, c_: [(_SIBLING, 0, 0, 0, 0)])
    mod_all, w_other = _exchange("gather_mod", _merge(_gather_chips_carry([mod_cols]), to_other))
    w_lo = jnp.where(ci == 0, w_half, w_other[0])
    w_hi = jnp.where(ci == 0, w_other[0], w_half)
    w_in_f = jnp.pad(jnp.concatenate([w_lo, w_hi], axis=2).reshape(IN_W, D), ((0, PROJ_W - IN_W), (0, 0)))
    mod = lax.dynamic_slice(jnp.transpose(mod_all, (1, 0, 2)).reshape(8, 4 * ncol), (me, 0), (1, 4 * ncol))
    shift1, scale1, gate1, shift2, scale2, gate2 = [mod[:, j * D:(j + 1) * D] for j in range(6)]
    a1 = norm1 * (1.0 + scale1)
    a2 = norm2 * (1.0 + scale2)

    hdn = DFF // 8
    q, kv, z, xbc, dtr, w_o_g, w_dna_g = _in_proj_fwd(xs2, a1, shift1, w_in_f,
                                                      carry=_gather_chips_carry([w_o_b, w_dn_b[0:hdn]]))
    w_o_f = w_o_g.reshape(D, D)
    bucket, mask = _attn_geometry()
    bucket = jnp.asarray(bucket)
    bias = _bias_table(rel_bias, bucket, jnp.asarray(mask.astype(np.int32)))
    sinks1 = sinks[0]
    ya, w_ga_g = _attn_fwd(q, kv, bias, sinks1, attn_out_norm, carry=_gather_chips_carry([w_gu_b[0:D // 2]]))
    cw8 = jnp.concatenate([cw_full, jnp.zeros((4, XBCW), F32)], axis=0)
    dtb = _pad_row(dt_bias)[:, 0:128]
    av = _pad_row(-jnp.exp(A_log))[:, 0:128]
    dk = jnp.repeat(D_skip, HD, axis=1)
    ys, hs, pre, w_gb_g, w_dnb_g = _ssd_fwd(z, xbc, dtr, cw8, conv_b, dtb, av, dk, ssm_out_norm,
                                            carry=_gather_chips_carry([w_gu_b[D // 2:D], w_dn_b[hdn:2 * hdn]]))
    w_dn_f = jnp.stack([w_dna_g, w_dnb_g], axis=1).reshape(DFF, D)
    fn = final_norm[None, :]
    x1, gu, dx2, loss_acc, dfn = _mlp_fwd(xs2, ya, ys, tgt, w_o_f, w_ga_g, w_gb_g, w_dn_f, gate1, a2, shift2, gate2, fn)

    def to_sibling(p):
        return _Carry([p], [jax.ShapeDtypeStruct((4,) + p.shape[2:], F32)],
                      lambda x_, y_, c_: [(_SIBLING, 0, (j, 1 - c_), 0, j) for j in range(4)])

    def to_chips(s4):
        return _Carry([s4], [jax.ShapeDtypeStruct((3,) + s4.shape[1:], s4.dtype)],
                      lambda x_, y_, c_: [(f, 0, jnp.bitwise_xor(2 * x_ + y_, k + 1), 0, k) for k, f in enumerate(_CHIPS3)])

    def back(t):
        return _Carry([t[None]], [jax.ShapeDtypeStruct((1,) + t.shape, F32)], lambda x_, y_, c_: [(_SIBLING, 0, 0, 0, 0)])

    dx1, dya, dys, act, dgu, h2, dsh2, p2 = _mlp_bwd(x1, gu, dx2, w_o_f, w_ga_g, w_gb_g, w_dn_f, gate1, a2, shift2, gate2)
    p_gu = _wgrad_gate_up(h2, dgu)[0].reshape(4, 2, D // 2, 2 * DFF // 4)
    dq, dkv, dbias, dsink, dnw_attn, g_dn, dg2, got1_gu = _attn_bwd(
        q, kv, dya, bias, sinks1, attn_out_norm, act, dx2, gate2, w_dn_f, carry=to_sibling(p_gu))
    p_dn = g_dn.reshape(4, 2, DFF // 8, D)
    drel = _rel_bias_grad(dbias, bucket)
    s4_gu, own_gu = _add_half("rs_add_half_gu", p_gu, got1_gu, where)
    dz, dxbc, ddt, dcw, dcb, dnw_ssm, dhd, got2_gu, got1_dn = _ssd_bwd(
        z, xbc, pre, dtr, dys, hs, cw8, dtb, av, dk, ssm_out_norm, carry=_merge(to_chips(s4_gu), to_sibling(p_dn)))
    mine_gu = _add_chips("rs_add_chips_gu", own_gu, got2_gu)
    s4_dn, own_dn = _add_half("rs_add_half_dn", p_dn, got1_dn, where)
    grad_x, h1, dsh1, p1 = _in_proj_bwd(xs2, dx1, a1, shift1, w_in_f, dq, dkv, dz, dxbc, ddt)
    g_in_t, got2_dn, got3_gu = _wgrad_in_t(h1, [dq, dkv, dz, dxbc, ddt],
                                           carry=_merge(to_chips(s4_dn), back(mine_gu)))
    mine_dn = _add_chips("rs_add_chips_dn", own_dn, got2_dn)
    p_in = g_in_t[0:IN_W].reshape(4, IN_W // 4, D)

    def to_sibling_cols(p):
        return _Carry([p], [jax.ShapeDtypeStruct(p.shape[:2] + (D // 2,), F32)],
                      lambda x_, y_, c_: [(_SIBLING, 0, (j, slice(None), pl.ds((1 - c_) * (D // 2), D // 2)), 0, j)
                                          for j in range(4)])

    no_dg1 = jnp.zeros((8, D), F32)
    small = _pack_small(dsh1, p1, dsh2, p2, no_dg1, no_dg1, dg2, norm1, norm2, scale1, scale2, dcw, dcb, dfn,
                        dnw_attn, dnw_ssm, dhd, av, dsink, drel, loss_acc)
    g_oa, dg1a, got1_in, got3_dn, small_all = _wgrad(
        "wgrad_o_attn", ya, dx1, gate1, w_o_f[0:QW],
        carry=_merge(to_sibling_cols(p_in), back(mine_dn), _gather8_carry(small)))
    s4_in, own_in = _add_half("rs_add_half_in", p_in, got1_in, where, by_cols=True)
    g_os, dg1b, got2_in = _wgrad("wgrad_o_ssm", ys, dx1, gate1, w_o_f[QW:D], carry=to_chips(s4_in))
    mine_in = _add_chips("rs_add_chips_in", own_in, got2_in)
    p_o = jnp.concatenate([g_oa, g_os], axis=0).reshape(4, 2, D // 8, D)

    dg1_all, got1_o, got3_in = _exchange(
        "gather_tail", _merge(_gather8_carry(dg1a[0:1] + dg1b[0:1]), to_sibling(p_o), back(mine_in)))
    small_all = small_all.at[:, 2, :].set(dg1_all[:, 0, :])
    s4_o, own_o = _add_half("rs_add_half_o", p_o, got1_o, where)
    mine_o = _add_chips("rs_add_chips_o", own_o, _exchange("rs_chips_o", to_chips(s4_o))[0])
    got3_o = _exchange("rs_back_o", back(mine_o))[0]
    small_res, loss = _small_update(
        small_all, where,
        [ada_b, norm1, conv_w, conv_b, dt_bias, A_log, D_skip, sinks, attn_out_norm, ssm_out_norm, norm2, rel_bias,
         final_norm[None, :]],
        [m_ada_b, m_norm1, m_conv_w, m_conv_b, m_dt_bias, m_A_log, m_D_skip, m_sinks, m_attn_out_norm,
         m_ssm_out_norm, m_norm2, m_rel_bias, m_final_norm[None, :]],
        [v_ada_b, v_norm1, v_conv_w, v_conv_b, v_dt_bias, v_A_log, v_D_skip, v_sinks, v_attn_out_norm,
         v_ssm_out_norm, v_norm2, v_rel_bias, v_final_norm[None, :]])
    small_out = [dict(zip(_SMALL, r)) for r in small_res]
    for r in small_out:
        r["final_norm"] = r["final_norm"][0]

    dmod_all = small_all[:, 0:6, :].reshape(8, 6 * D)
    dmod_loc = lax.dynamic_slice(dmod_all, (0, chip * ncol), (8, ncol))
    ada_out = _ada_bwd_adamw(c_all.T, dmod_loc, ada_w[0], m_ada_w[0], v_ada_w[0])

    big_gu = _adamw_halves("adamw_gate_up", mine_gu, got3_gu[0], w_gate_up, m_w_gate_up, v_w_gate_up, where)
    big_dn = _adamw_halves("adamw_down", mine_dn, got3_dn[0], w_down, m_w_down, v_w_down, where)
    big_o = _adamw_halves("adamw_o", mine_o, got3_o[0], w_o, m_w_o, v_w_o, where)
    big_in = [o.T[None] for o in _adamw_halves("adamw_in", mine_in, got3_in[0], w_in_t, m_w_in_t, v_w_in_t, where,
                                               by_cols=True)]
    big = [big_in, big_o, big_gu, big_dn]

    order = ["ada_w", "ada_b", "norm1", "w_in", "conv_w", "conv_b", "dt_bias", "A_log", "D_skip", "sinks",
             "attn_out_norm", "ssm_out_norm", "w_o", "norm2", "w_gate_up", "w_down", "rel_bias", "final_norm"]
    bigname = {"w_in": 0, "w_o": 1, "w_gate_up": 2, "w_down": 3}
    res = [loss, grad_x[None]]
    for kind in range(4):
        for nm in order:
            if nm == "ada_w":
                res.append(ada_out[kind][None])
            elif nm in bigname:
                res.append(big[bigname[nm]][kind])
            else:
                res.append(small_out[kind][nm])
    return tuple(res)
```

```python
import numpy as np
import jax
import jax.numpy as jnp
from jax import lax
from jax.experimental import pallas as pl
from jax.experimental.pallas import tpu as pltpu

F32, BF16 = jnp.float32, jnp.bfloat16
HI = lax.Precision.HIGHEST

D = 1024
QW, KVW = 512, 128
NH, HD, NKV = 8, 64, 2
SW = 512
NST = 128
XBCW = 1024
CK = 4
BLK = 128
DFF = 2816
IN_W = 2312
PROJ_W = 2432
EPS = 1e-6
NEG = -1e30
NBUCKET = 32

B1, B2, LR, AEPS, WD, STEP = 0.9, 0.999, 0.001, 1e-08, 0.01, 10

VMEM_LIMIT = 56 * 1024 * 1024

_NT = (((1,), (1,)), ((), ()))
_TN = (((0,), (0,)), ((), ()))


def _mm(a, b):
    return jnp.dot(a, b, preferred_element_type=F32)


def _mm_nt(a, b):
    return lax.dot_general(a, b, _NT, preferred_element_type=F32)


def _mm_tn(a, b):
    return lax.dot_general(a, b, _TN, preferred_element_type=F32)


def _mm_hi(a, b):
    return jnp.dot(a, b, preferred_element_type=F32, precision=HI)


def _split3(x):
    hi = x.astype(BF16)
    r = x - hi.astype(F32)
    mid = r.astype(BF16)
    lo = (r - mid.astype(F32)).astype(BF16)
    return hi, mid, lo


def _sel_r(x, e):
    hi, mid, lo = _split3(x)
    return (_mm(hi, e) + _mm(mid, e)) + _mm(lo, e)


def _sel_l(e, x):
    hi, mid, lo = _split3(x)
    return (_mm(e, hi) + _mm(e, mid)) + _mm(e, lo)


def _sig(x):
    return 1.0 / (1.0 + jnp.exp(-x))


def _cp(sem):
    return pltpu.CompilerParams(dimension_semantics=sem, vmem_limit_bytes=VMEM_LIMIT)


def _row(shape):
    nd = len(shape)
    return pl.BlockSpec(shape, lambda *_: (0,) * nd)


def _adamw(w, g, m, v):
    m = B1 * m + (1.0 - B1) * g
    v = B2 * v + (1.0 - B2) * (g * g)
    m_hat = m / (1.0 - B1 ** STEP)
    v_hat = v / (1.0 - B2 ** STEP)
    delta = -LR * (m_hat / (jnp.sqrt(v_hat) + AEPS) + WD * w)
    return delta, m, v


class _Carry:
    def __init__(self, inps, outs, copies):
        self.inps, self.outs, self.copies = list(inps), list(outs), copies
        self.n = len(copies(0, 0, 0))

    def descriptors(self, in_refs, out_refs, send_sems, recv_sems):
        x, y, c = lax.axis_index("x"), lax.axis_index("y"), lax.axis_index("c")
        out = []
        for j, (flip, a, si, o, di) in enumerate(self.copies(x, y, c)):
            if flip is None:
                out.append(pltpu.make_async_copy(in_refs[a].at[si], out_refs[o].at[di], send_sems.at[j]))
            else:
                fx, fy, fc = flip
                peer = (1 - x if fx else x, 1 - y if fy else y, 1 - c if fc else c)
                out.append(pltpu.make_async_remote_copy(
                    src_ref=in_refs[a].at[si], dst_ref=out_refs[o].at[di],
                    send_sem=send_sems.at[j], recv_sem=recv_sems.at[j],
                    device_id=peer, device_id_type=pl.DeviceIdType.MESH))
        return out


def _pcall(body, args, *, name, grid, in_specs, out_specs, out_shape, scratch_shapes=(), sem=None, nprefetch=0,
           carry=None):
    out_shape, out_specs = list(out_shape), list(out_specs)
    in_specs, scratch_shapes = list(in_specs), list(scratch_shapes)
    nin, nout, nscr = len(in_specs), len(out_shape), len(scratch_shapes)
    run = body
    if carry is not None:
        ncin, ncout = len(carry.inps), len(carry.outs)
        hbm = pl.BlockSpec(memory_space=pl.ANY)

        def run(*refs):
            pre, r = refs[:nprefetch], refs[nprefetch:]
            ins, cins = r[:nin], r[nin:nin + ncin]
            r = r[nin + ncin:]
            outs, couts = r[:nout], r[nout:nout + ncout]
            r = r[nout + ncout:]
            scr, (send_sems, recv_sems) = r[:nscr], r[nscr:]
            first = pl.program_id(0) == 0
            last = pl.program_id(0) == grid[0] - 1
            for ax in range(1, len(grid)):
                first = jnp.logical_and(first, pl.program_id(ax) == 0)
                last = jnp.logical_and(last, pl.program_id(ax) == grid[ax] - 1)

            @pl.when(first)
            def _():
                for d in carry.descriptors(cins, couts, send_sems, recv_sems):
                    d.start()

            body(*pre, *ins, *outs, *scr)

            @pl.when(last)
            def _():
                for d in carry.descriptors(cins, couts, send_sems, recv_sems):
                    d.wait()

        in_specs = in_specs + [hbm] * ncin
        out_specs = out_specs + [hbm] * ncout
        out_shape = out_shape + carry.outs
        scratch_shapes = scratch_shapes + [pltpu.SemaphoreType.DMA((carry.n,)), pltpu.SemaphoreType.DMA((carry.n,))]
        args = list(args) + carry.inps
    if sem is None:
        sem = ("arbitrary",) * len(grid)
    if nprefetch:
        kw = dict(grid_spec=pltpu.PrefetchScalarGridSpec(num_scalar_prefetch=nprefetch, grid=grid, in_specs=in_specs,
                                                         out_specs=out_specs, scratch_shapes=scratch_shapes))
    else:
        kw = dict(grid=grid, in_specs=in_specs, out_specs=out_specs, scratch_shapes=scratch_shapes)
    res = pl.pallas_call(run, name=name, out_shape=out_shape, compiler_params=_cp(sem), **kw)(*args)
    return list(res)


def _merge(*carries):
    inps, outs, offs = [], [], []
    for cr in carries:
        offs.append((len(inps), len(outs)))
        inps += cr.inps
        outs += cr.outs

    def copies(x, y, c):
        return [(f, a + io, si, o + oo, di) for cr, (io, oo) in zip(carries, offs) for f, a, si, o, di in cr.copies(x, y, c)]

    return _Carry(inps, outs, copies)


def _exchange(name, carry):
    return _pcall(lambda: None, [], name=name, grid=(1,), in_specs=[], out_specs=[], out_shape=[], carry=carry)


_ALL7 = [(f >> 2 & 1, f >> 1 & 1, f & 1) for f in range(1, 8)]
_CHIPS3 = [(0, 1, 0), (1, 0, 0), (1, 1, 0)]
_SIBLING = (0, 0, 1)


def _gather8_carry(blk):
    def copies(x, y, c):
        me = 4 * x + 2 * y + c
        return [(None, 0, 0, 0, me)] + [(f, 0, 0, 0, me) for f in _ALL7]

    return _Carry([blk[None]], [jax.ShapeDtypeStruct((8,) + blk.shape, blk.dtype)], copies)


def _gather_chips_carry(blks):
    def copies(x, y, c):
        chip = 2 * x + y
        return [(f, a, 0, a, chip) for a in range(len(blks)) for f in [None] + _CHIPS3]

    return _Carry([b[None] for b in blks], [jax.ShapeDtypeStruct((4,) + b.shape, b.dtype) for b in blks], copies)


def _ada_fwd(c_all, w_loc, b_loc, carry=None):
    n = w_loc.shape[1]
    tn = 512

    def body(c_ref, w_ref, b_ref, o_ref):
        cv = c_ref[...]
        cond = cv * _sig(cv)
        o_ref[...] = _mm_hi(cond, w_ref[...]) + b_ref[...]

    return _pcall(
        body, [c_all, w_loc, b_loc], name="ada_fwd", grid=(n // tn,),
        out_shape=[jax.ShapeDtypeStruct((8, n), F32)],
        in_specs=[_row((8, D)), pl.BlockSpec((D, tn), lambda j: (0, j)), pl.BlockSpec((1, tn), lambda j: (0, j))],
        out_specs=[pl.BlockSpec((8, tn), lambda j: (0, j))], carry=carry)


def _ada_bwd_adamw(c_all_t, dmod_loc, w, m, v, carry=None):
    n = w.shape[1]
    tn = 512

    def body(ct_ref, dm_ref, w_ref, m_ref, v_ref, g_ref, d_ref, mo_ref, vo_ref):
        ct = ct_ref[...]
        cond = ct * _sig(ct)
        dm = dm_ref[...]
        g = cond[:, 0:1] * dm[0:1, :]
        for b in range(1, 8):
            g = g + cond[:, b:b + 1] * dm[b:b + 1, :]
        g_ref[...] = g
        d_ref[...], mo_ref[...], vo_ref[...] = _adamw(w_ref[...], g, m_ref[...], v_ref[...])

    wspec = pl.BlockSpec((D, tn), lambda j: (0, j))
    return _pcall(
        body, [c_all_t, dmod_loc, w, m, v], name="ada_bwd_adamw", grid=(n // tn,),
        out_shape=[jax.ShapeDtypeStruct((D, n), F32)] * 4,
        in_specs=[_row((D, 8)), pl.BlockSpec((8, tn), lambda j: (0, j)), wspec, wspec, wspec],
        out_specs=[wspec] * 4, carry=carry)


def _in_proj_fwd(x, a1, sh1, w_in, carry=None):
    s = x.shape[0]
    tm = 512

    def body(x_ref, a_ref, s_ref, w_ref, q_ref, kv_ref, z_ref, xbc_ref, dt_ref):
        def norm(rows):
            xv = x_ref[rows, :]
            r = lax.rsqrt(jnp.mean(xv * xv, axis=-1, keepdims=True) + EPS)
            return (xv * r * a_ref[...] + s_ref[...]).astype(BF16)

        def project(rows, h):
            p = _mm_nt(h, w_ref[...])
            q_ref[rows, :] = p[:, 0:512].astype(BF16)
            kv_ref[rows, :] = p[:, 512:768].astype(BF16)
            z_ref[rows, :] = p[:, 768:1280]
            xbc_ref[rows, :] = p[:, 1280:2304]
            dt_ref[rows, :] = p[:, 2304:2432]

        r0, r1 = slice(0, tm // 2), slice(tm // 2, tm)
        h0 = norm(r0)
        project(r0, h0)
        project(r1, norm(r1))

    def tok(w):
        return pl.BlockSpec((tm, w), lambda i: (i, 0))

    return _pcall(
        body, [x, a1, sh1, w_in], name="in_proj_fwd", grid=(s // tm,),
        out_shape=[jax.ShapeDtypeStruct((s, QW), BF16), jax.ShapeDtypeStruct((s, 2 * KVW), BF16),
                   jax.ShapeDtypeStruct((s, SW), F32), jax.ShapeDtypeStruct((s, XBCW), F32),
                   jax.ShapeDtypeStruct((s, 128), F32)],
        in_specs=[tok(D), _row((1, D)), _row((1, D)), _row((PROJ_W, D))],
        out_specs=[tok(QW), tok(2 * KVW), tok(SW), tok(XBCW), tok(128)], carry=carry)


def _in_proj_bwd(x, dx1, a1, sh1, w_in, dq, dkv, dz, dxbc, ddt, carry=None):
    s = x.shape[0]
    tm = 512

    def body(x_ref, dx1_ref, a_ref, s_ref, w_ref, dq_ref, dkv_ref, dz_ref, dxbc_ref, ddt_ref,
             gx_ref, h_ref, dsh_ref, p_ref):
        i = pl.program_id(0)

        @pl.when(i == 0)
        def _():
            dsh_ref[...] = jnp.zeros_like(dsh_ref)
            p_ref[...] = jnp.zeros_like(p_ref)

        def gather(st):
            rows = st["rows"]
            st["dproj"] = jnp.concatenate([dq_ref[rows, :], dkv_ref[rows, :], dz_ref[rows, :], dxbc_ref[rows, :],
                                           ddt_ref[rows, :]], axis=1)

        def back(st):
            st["dh"] = _mm(st.pop("dproj"), w_ref[...])

        def norm(st):
            rows, dh = st["rows"], st.pop("dh")
            xv = x_ref[rows, :]
            r = lax.rsqrt(jnp.mean(xv * xv, axis=-1, keepdims=True) + EPS)
            xn = xv * r
            a = a_ref[...]
            h_ref[rows, :] = (xn * a + s_ref[...]).astype(BF16)
            st["dsh"] = jnp.sum(dh, axis=0, keepdims=True)
            st["p"] = jnp.sum(dh * xn, axis=0, keepdims=True)
            u = dh * a
            gx_ref[rows, :] = dx1_ref[rows, :] + r * u - xn * (r * jnp.mean(u * xn, axis=-1, keepdims=True))

        g0, g1 = [dict(rows=slice(k * (tm // 2), (k + 1) * (tm // 2))) for k in range(2)]
        for stage, st in [(gather, g0), (back, g0), (gather, g1), (norm, g0), (back, g1), (norm, g1)]:
            stage(st)
        dsh_ref[0:1, :] += g0["dsh"] + g1["dsh"]
        p_ref[0:1, :] += g0["p"] + g1["p"]

    def tok(w):
        return pl.BlockSpec((tm, w), lambda i: (i, 0))

    return _pcall(
        body, [x, dx1, a1, sh1, w_in, dq, dkv, dz, dxbc, ddt], name="in_proj_bwd", grid=(s // tm,),
        out_shape=[jax.ShapeDtypeStruct((s, D), F32), jax.ShapeDtypeStruct((s, D), BF16),
                   jax.ShapeDtypeStruct((8, D), F32), jax.ShapeDtypeStruct((8, D), F32)],
        in_specs=[tok(D), tok(D), _row((1, D)), _row((1, D)), _row((PROJ_W, D)),
                  tok(QW), tok(2 * KVW), tok(SW), tok(XBCW), tok(128)],
        out_specs=[tok(D), tok(D), _row((8, D)), _row((8, D))], carry=carry)


def _attn_geometry():
    dist = np.arange(BLK)[:, None] + BLK - np.arange(2 * BLK)[None, :]
    n = np.maximum(dist, 0)
    max_exact = NBUCKET // 2
    large = max_exact + (np.log(np.maximum(n, 1) / max_exact) / np.log(128 / max_exact)
                         * (NBUCKET - max_exact)).astype(np.int32)
    large = np.minimum(large, NBUCKET - 1)
    bucket = np.where(n < max_exact, n, large).astype(np.int32)
    mask = (dist >= 0) & (dist < 128)
    return bucket, mask


def _attn_heads(is_first, q_blk, kvw, bias_ref, sinks_ref):
    qv = q_blk * 0.125
    col = lax.broadcasted_iota(jnp.int32, (BLK, 2 * BLK), 1)
    first = jnp.where(jnp.logical_and(is_first, col < BLK), NEG, 0.0)
    groups = []
    for g in range(NKV):
        qs = jnp.concatenate([qv[:, (4 * g + r) * HD:(4 * g + r + 1) * HD] for r in range(4)], axis=0)
        kw = kvw[:, g * HD:(g + 1) * HD]
        vw = kvw[:, KVW + g * HD:KVW + (g + 1) * HD]
        sc = _mm_nt(qs, kw)
        pn, ps = [], []
        for r in range(4):
            h = 4 * g + r
            sr = sc[r * BLK:(r + 1) * BLK] + bias_ref[h] + first
            sink = sinks_ref[h]
            m = jnp.maximum(jnp.max(sr, axis=-1, keepdims=True), sink)
            p = jnp.exp(sr - m)
            es = jnp.exp(sink - m)
            inv = 1.0 / (jnp.sum(p, axis=-1, keepdims=True) + es)
            pn.append(p * inv)
            ps.append(es * inv)
        pn = jnp.concatenate(pn, axis=0)
        ps = jnp.concatenate(ps, axis=0)
        o = _mm(pn.astype(BF16), vw)
        groups.append((qs, kw, vw, pn, ps, o))
    return groups


def _unstack_heads(parts):
    return jnp.concatenate([p[r * BLK:(r + 1) * BLK] for p in parts for r in range(4)], axis=1)


NB = 2


def _attn_fwd(q, kv, bias, sinks, nw, carry=None):
    s = q.shape[0]

    def body(q_ref, kvp_ref, kvc_ref, bias_ref, sinks_ref, nw_ref, y_ref):
        t = pl.program_id(0)
        kv3 = jnp.concatenate([kvp_ref[...], kvc_ref[...]], axis=0)
        for sub in range(NB):
            rows = slice(sub * BLK, (sub + 1) * BLK)
            groups = _attn_heads(jnp.logical_and(t == 0, sub == 0), q_ref[rows, :], kv3[sub * BLK:(sub + 2) * BLK],
                                 bias_ref, sinks_ref)
            o = _unstack_heads([g[5] for g in groups])
            r = lax.rsqrt(jnp.mean(o * o, axis=-1, keepdims=True) + EPS)
            y_ref[rows, :] = (o * r * nw_ref[...]).astype(BF16)

    return _pcall(
        body, [q, kv, kv, bias, sinks, nw], name="attn_fwd", grid=(s // (NB * BLK),),
        out_shape=[jax.ShapeDtypeStruct((s, QW), BF16)],
        in_specs=[pl.BlockSpec((NB * BLK, QW), lambda t: (t, 0)),
                  pl.BlockSpec((BLK, 2 * KVW), lambda t: (jnp.maximum(NB * t - 1, 0), 0)),
                  pl.BlockSpec((NB * BLK, 2 * KVW), lambda t: (t, 0)),
                  _row((NH, BLK, 2 * BLK)),
                  pl.BlockSpec(memory_space=pltpu.SMEM),
                  _row((1, QW))],
        out_specs=[pl.BlockSpec((NB * BLK, QW), lambda t: (t, 0))], carry=carry)


def _attn_bwd(q, kv, dya, bias, sinks, nw, act, dx2, gate2, w_dn, carry=None):
    s = q.shape[0]
    nt = s // (NB * BLK)
    npiece = DFF // NB

    def body(q_ref, kvp_ref, kvc_ref, dy_ref, bias_ref, sinks_ref, nw_ref, act_ref, dx2_ref, g2_ref, wdn_ref,
             dq_ref, dkv_ref, dbias_ref, dsink_ref, dnw_ref, gdn_hbm, dg2_ref, carry_ref, held_ref, acc_ref, sem):
        t = pl.program_id(0)

        @pl.when(t == 0)
        def _():
            carry_ref[...] = jnp.zeros_like(carry_ref)
            held_ref[...] = jnp.zeros_like(held_ref)
            dbias_ref[...] = jnp.zeros_like(dbias_ref)
            dsink_ref[...] = jnp.zeros_like(dsink_ref)
            dnw_ref[...] = jnp.zeros_like(dnw_ref)
            acc_ref[...] = jnp.zeros_like(acc_ref)

        def wgrad_piece(sub):
            rows = slice(sub * npiece, (sub + 1) * npiece)
            acc_ref[rows, :] += _mm_tn(act_ref[:, rows], dx2_ref[...].astype(BF16))

        def block(sub, kv3):
            rows = slice(sub * BLK, (sub + 1) * BLK)
            groups = _attn_heads(jnp.logical_and(t == 0, sub == 0), q_ref[rows, :], kv3[sub * BLK:(sub + 2) * BLK],
                                 bias_ref, sinks_ref)
            o = _unstack_heads([g[5] for g in groups])
            r = lax.rsqrt(jnp.mean(o * o, axis=-1, keepdims=True) + EPS)
            dy = dy_ref[rows, :]
            on = o * r
            dnw_ref[0:1, :] += jnp.sum(dy * on, axis=0, keepdims=True)
            u = dy * nw_ref[...]
            do = r * u - on * (r * jnp.mean(u * on, axis=-1, keepdims=True))
            dq_parts, dk_parts, dv_parts = [], [], []
            for g, (qs, kw, vw, pn, ps, og) in enumerate(groups):
                dos = jnp.concatenate([do[:, (4 * g + r_) * HD:(4 * g + r_ + 1) * HD] for r_ in range(4)], axis=0)
                delta = jnp.sum(dos * og, axis=-1, keepdims=True)
                dp = _mm_nt(dos.astype(BF16), vw)
                ds = pn * (dp - delta)
                dsk = ps * delta
                lane = lax.broadcasted_iota(jnp.int32, (1, 128), 1)
                for r_ in range(4):
                    h = 4 * g + r_
                    dbias_ref[h] += ds[r_ * BLK:(r_ + 1) * BLK]
                    dsink_ref[0:1, :] -= jnp.where(lane == h, jnp.sum(dsk[r_ * BLK:(r_ + 1) * BLK]), 0.0)
                dsb = ds.astype(BF16)
                dq_parts.append(_mm(dsb, kw) * 0.125)
                dk_parts.append(_mm_tn(dsb, qs))
                dv_parts.append(_mm_tn(pn.astype(BF16), dos.astype(BF16)))
            dq_ref[rows, :] = _unstack_heads(dq_parts).astype(BF16)
            return jnp.concatenate(dk_parts + dv_parts, axis=1)

        @pl.when(t < nt)
        def _():
            kv3 = jnp.concatenate([kvp_ref[...], kvc_ref[...]], axis=0)
            tail = carry_ref[...]
            for sub in range(NB):
                d = block(sub, kv3)
                done = tail + d[0:BLK]
                if sub == 0:
                    dkv_ref[0:(NB - 1) * BLK, :] = held_ref[...].astype(BF16)
                    dkv_ref[(NB - 1) * BLK:NB * BLK, :] = done.astype(BF16)
                else:
                    held_ref[(sub - 1) * BLK:sub * BLK, :] = done
                tail = d[BLK:2 * BLK]
                wgrad_piece(sub)
            carry_ref[...] = tail

        @pl.when(t == nt)
        def _():
            dkv_ref[0:(NB - 1) * BLK, :] = held_ref[...].astype(BF16)
            dkv_ref[(NB - 1) * BLK:NB * BLK, :] = carry_ref[...].astype(BF16)
            acc = acc_ref[...]
            dg2_ref[...] = jnp.zeros_like(dg2_ref)
            dg2_ref[0:1, :] = jnp.sum(acc * wdn_ref[...].astype(F32), axis=0, keepdims=True)
            acc_ref[...] = acc * g2_ref[...]
            cp = pltpu.make_async_copy(acc_ref, gdn_hbm, sem)
            cp.start()
            cp.wait()

    last = nt - 1
    tile = lambda w: pl.BlockSpec((NB * BLK, w), lambda t: (jnp.minimum(t, last), 0))
    return _pcall(
        body, [q, kv, kv, dya, bias, sinks, nw, act, dx2, gate2, w_dn], name="attn_bwd", grid=(nt + 1,),
        out_shape=[jax.ShapeDtypeStruct((s, QW), BF16), jax.ShapeDtypeStruct((s, 2 * KVW), BF16),
                   jax.ShapeDtypeStruct((NH, BLK, 2 * BLK), F32), jax.ShapeDtypeStruct((NH, 128), F32),
                   jax.ShapeDtypeStruct((8, QW), F32), jax.ShapeDtypeStruct((DFF, D), F32),
                   jax.ShapeDtypeStruct((8, D), F32)],
        in_specs=[tile(QW),
                  pl.BlockSpec((BLK, 2 * KVW), lambda t: (jnp.clip(NB * t - 1, 0, NB * nt - 1), 0)),
                  tile(2 * KVW), tile(QW),
                  _row((NH, BLK, 2 * BLK)),
                  pl.BlockSpec(memory_space=pltpu.SMEM),
                  _row((1, QW)), tile(DFF), tile(D), _row((1, D)), _row((DFF, D))],
        out_specs=[tile(QW),
                   pl.BlockSpec((NB * BLK, 2 * KVW), lambda t: (jnp.maximum(t - 1, 0), 0)),
                   _row((NH, BLK, 2 * BLK)), _row((NH, 128)), _row((8, QW)),
                   pl.BlockSpec(memory_space=pl.ANY), _row((8, D))],
        scratch_shapes=[pltpu.VMEM((BLK, 2 * KVW), F32), pltpu.VMEM(((NB - 1) * BLK, 2 * KVW), F32),
                        pltpu.VMEM((DFF, D), F32), pltpu.SemaphoreType.DMA], carry=carry)


def _rel_bias_grad(dbias, bucket):
    def body(db_ref, bk_ref, o_ref):
        bk = bk_ref[...]
        lane = lax.broadcasted_iota(jnp.int32, (1, 128), 1)
        for b in range(NBUCKET):
            sel = bk == b
            row = jnp.zeros((1, 128), F32)
            for h in range(NH):
                row = row + jnp.where(lane == h, jnp.sum(jnp.where(sel, db_ref[h], 0.0)), 0.0)
            o_ref[b:b + 1, :] = row

    return pl.pallas_call(
        body, name="rel_bias_grad",
        out_shape=jax.ShapeDtypeStruct((NBUCKET, 128), F32),
    )(dbias, bucket)


def _ssd_consts():
    head_of_lane = np.arange(SW) // HD
    expand = (np.arange(128)[:, None] == head_of_lane[None, :]).astype(np.float32)
    tril = np.tril(np.ones((BLK, BLK), np.float32))
    return (jnp.asarray(expand, BF16), jnp.asarray(expand.T.copy(), BF16), jnp.asarray(tril, BF16),
            jnp.asarray(tril.T.copy(), BF16))


def _conv_pre(xc, halo, cw, cb):
    ext = jnp.concatenate([halo, xc], axis=0)
    taps = [xc if k == CK - 1 else pltpu.roll(ext, CK - 1 - k, 0)[8:8 + BLK] for k in range(CK)]
    return cb + sum(cw[k:k + 1, :] * taps[k] for k in range(CK))


def _ssd_chunk(pre, dtr, dtb, av, dkv, ex, tril, h_in):
    sp = _sig(pre)
    xbc = pre * sp
    xs, bm, cm = xbc[:, 0:SW], xbc[:, SW:SW + 2 * NST], xbc[:, SW + 2 * NST:]
    dtin = dtr + dtb
    dt = jnp.maximum(dtin, 0.0) + jnp.log1p(jnp.exp(-jnp.abs(dtin)))
    cs = _sel_l(tril, dt * av)
    cst = cs.T
    dtx = _sel_r(dt, ex)
    csx = _sel_r(cs, ex)
    xdt = xs * dtx
    csl = csx[BLK - 1:BLK, :]
    decx = jnp.exp(csl - csx)
    ecsx = jnp.exp(csx)
    ecl = jnp.exp(csl)
    causal = tril.astype(F32) > 0.5
    ydiag, yoff, cbs, lms = [], [], [], []
    for g in range(2):
        bg = bm[:, g * NST:(g + 1) * NST].astype(BF16)
        cg = cm[:, g * NST:(g + 1) * NST].astype(BF16)
        cb = _mm_nt(cg, bg)
        cbs.append(cb)
        yoff.append(_mm(cg, h_in[:, g * 256:(g + 1) * 256].astype(BF16)))
        for r in range(4):
            h = 4 * g + r
            seg = cs[:, h:h + 1] - cst[h:h + 1, :]
            lm = jnp.where(causal, jnp.exp(jnp.minimum(seg, 0.0)), 0.0)
            lms.append(lm)
            ydiag.append(_mm((cb * lm).astype(BF16), xdt[:, h * HD:(h + 1) * HD].astype(BF16)))
    yoff = jnp.concatenate(yoff, axis=1) * ecsx
    y = jnp.concatenate(ydiag, axis=1) + yoff + dkv * xs
    return dict(pre=pre, sp=sp, xs=xs, bm=bm, cm=cm, dtin=dtin, dt=dt, av=av, cs=cs, cst=cst,
                dtx=dtx, csx=csx, xdt=xdt, decx=decx, ecsx=ecsx, ecl=ecl, causal=causal, cbs=cbs, lms=lms,
                yoff=yoff, y=y)


def _group_mean(t):
    m0 = jnp.mean(t[:, 0:256], axis=-1, keepdims=True)
    m1 = jnp.mean(t[:, 256:512], axis=-1, keepdims=True)
    return jnp.concatenate([jnp.broadcast_to(m0, (t.shape[0], 256)), jnp.broadcast_to(m1, (t.shape[0], 256))], axis=1)


SUBS = 4


def _ssd_fwd(z, xbc, dtr, cw, cb, dtb, av, dk, nw, carry=None):
    s = z.shape[0]
    nc = s // BLK
    tile = SUBS * BLK
    ex, _, tril, _ = _ssd_consts()

    def body(z_ref, xc_ref, xh_ref, dtr_ref, cw_ref, cb_ref, dtb_ref, a_ref, dk_ref, nw_ref, ex_ref, tril_ref,
             y_ref, hs_ref, pre_ref, h_ref):
        t = pl.program_id(0)

        @pl.when(t == 0)
        def _():
            h_ref[...] = jnp.zeros_like(h_ref)

        h_in = h_ref[...]
        for sub in range(SUBS):
            rows = slice(sub * BLK, (sub + 1) * BLK)
            xc = xc_ref[rows, :]
            halo = jnp.where(t == 0, 0.0, xh_ref[...]) if sub == 0 else xc_ref[sub * BLK - 8:sub * BLK, :]
            pre = _conv_pre(xc, halo, cw_ref[...], cb_ref[...])
            pre_ref[rows, :] = pre
            hs_ref[sub] = h_in
            f = _ssd_chunk(pre, dtr_ref[rows, :], dtb_ref[...], a_ref[...], dk_ref[...], ex_ref[...], tril_ref[...], h_in)
            dx = (f["decx"] * f["xdt"]).astype(BF16)
            st = [_mm_tn(f["bm"][:, g * NST:(g + 1) * NST].astype(BF16), dx[:, g * 256:(g + 1) * 256]) for g in range(2)]
            h_in = h_in * f["ecl"] + jnp.concatenate(st, axis=1)
            zv = z_ref[rows, :]
            tg = f["y"] * (zv * _sig(zv))
            r = lax.rsqrt(_group_mean(tg * tg) + EPS)
            y_ref[rows, :] = (tg * r * nw_ref[...]).astype(BF16)
        h_ref[...] = h_in

    cur = lambda w: pl.BlockSpec((tile, w), lambda t: (t, 0))
    return _pcall(
        body, [z, xbc, xbc, dtr, cw, cb, dtb, av, dk, nw, ex, tril], name="ssd_fwd", grid=(s // tile,),
        out_shape=[jax.ShapeDtypeStruct((s, SW), BF16), jax.ShapeDtypeStruct((nc, NST, SW), F32),
                   jax.ShapeDtypeStruct((s, XBCW), F32)],
        in_specs=[cur(SW), cur(XBCW), pl.BlockSpec((8, XBCW), lambda t: (jnp.maximum(t * (tile // 8) - 1, 0), 0)),
                  cur(128), _row((8, XBCW)), _row((1, XBCW)), _row((1, 128)),
                  _row((1, 128)), _row((1, SW)), _row((1, SW)), _row((128, SW)), _row((BLK, BLK))],
        out_specs=[cur(SW), pl.BlockSpec((SUBS, NST, SW), lambda t: (t, 0, 0)), cur(XBCW)],
        scratch_shapes=[pltpu.VMEM((NST, SW), F32)], carry=carry)


def _ssd_bwd(z, xbc, pre_all, dtr, dys, hs, cw, dtb, av, dk, nw, carry=None):
    s = z.shape[0]
    tile = SUBS * BLK
    nt = s // tile
    ex, ext_t, tril, triu = _ssd_consts()

    def body(z_ref, xc_ref, pre_ref, dtr_ref, dy_ref, hs_ref, cw_ref, dtb_ref, a_ref, dk_ref, nw_ref,
             ex_ref, ext_ref, tril_ref, triu_ref,
             dz_ref, dxbc_ref, ddt_ref, dcw_ref, dcb_ref, dnw_ref, dhd_ref, dh_ref, nxt_ref, dd_ref):
        i = pl.program_id(0)

        @pl.when(i == 0)
        def _():
            dh_ref[...] = jnp.zeros_like(dh_ref)
            nxt_ref[...] = jnp.zeros_like(nxt_ref)
            dd_ref[...] = jnp.zeros_like(dd_ref)
            dcw_ref[...] = jnp.zeros_like(dcw_ref)
            dcb_ref[...] = jnp.zeros_like(dcb_ref)
            dnw_ref[...] = jnp.zeros_like(dnw_ref)
            dhd_ref[...] = jnp.zeros_like(dhd_ref)

        gst, nxt = dh_ref[...], nxt_ref[...]
        for sub in reversed(range(SUBS)):
            rows = slice(sub * BLK, (sub + 1) * BLK)
            gst, nxt = chunk(sub, rows, gst, nxt, z_ref, xc_ref, pre_ref, dtr_ref, dy_ref, hs_ref, cw_ref, dtb_ref,
                             a_ref, dk_ref, nw_ref, ex_ref, ext_ref, tril_ref, triu_ref,
                             dz_ref, dxbc_ref, ddt_ref, dcw_ref, dcb_ref, dnw_ref, dhd_ref, dd_ref)
        dh_ref[...] = gst
        nxt_ref[...] = nxt

        @pl.when(i == nt - 1)
        def _():
            dhd_ref[2:3, :] = _sel_r(dd_ref[...], ext_ref[...])[0:1, :]

    def chunk(sub, rows, gst, nxt, z_ref, xc_ref, pre_ref, dtr_ref, dy_ref, hs_ref, cw_ref, dtb_ref,
              a_ref, dk_ref, nw_ref, ex_ref, ext_ref, tril_ref, triu_ref,
              dz_ref, dxbc_ref, ddt_ref, dcw_ref, dcb_ref, dnw_ref, dhd_ref, dd_ref):
        h_in = hs_ref[sub]
        f = _ssd_chunk(pre_ref[rows, :], dtr_ref[rows, :], dtb_ref[...], a_ref[...], dk_ref[...], ex_ref[...],
                       tril_ref[...], h_in)
        xs, xdt, decx, ecsx, ecl, dtx = f["xs"], f["xdt"], f["decx"], f["ecsx"], f["ecl"], f["dtx"]
        cs, cst, causal = f["cs"], f["cst"], f["causal"]
        causal_t = triu_ref[...].astype(F32) > 0.5

        zv = z_ref[rows, :]
        sz = _sig(zv)
        gz = zv * sz
        t = f["y"] * gz
        r = lax.rsqrt(_group_mean(t * t) + EPS)
        tn_ = t * r
        dyn = dy_ref[rows, :]
        dnw_ref[0:1, :] += jnp.sum(dyn * tn_, axis=0, keepdims=True)
        u = dyn * nw_ref[...]
        dt_ = r * u - tn_ * (r * _group_mean(u * tn_))
        dy = dt_ * gz
        dz_ref[rows, :] = (dt_ * f["y"] * (sz * (1.0 + zv * (1.0 - sz)))).astype(BF16)

        dd_ref[0:1, :] += jnp.sum(dy * xs, axis=0, keepdims=True)
        dxs = dk_ref[...] * dy

        edy = ecsx * dy
        dxdt, dbs, dcs_, dcsx_parts, dh_new = [], [], [], [], []
        lane = lax.broadcasted_iota(jnp.int32, (1, 128), 1)
        dcs_intra = jnp.zeros((BLK, 128), F32)
        for g in range(2):
            sl = slice(g * 256, (g + 1) * 256)
            bgf, cgf = f["bm"][:, g * NST:(g + 1) * NST], f["cm"][:, g * NST:(g + 1) * NST]
            bg, cg = bgf.astype(BF16), cgf.astype(BF16)
            gg = gst[:, sl].astype(BF16)
            hg = h_in[:, sl].astype(BF16)
            edyg = edy[:, sl].astype(BF16)
            dc = _mm_nt(edyg, hg)
            dh_new.append(gst[:, sl] * ecl[:, sl] + _mm_tn(cg, edyg))
            bgm = _mm(bg, gg)
            dxdt_g = decx[:, sl] * bgm
            dxg = (decx[:, sl] * xdt[:, sl]).astype(BF16)
            db = _mm_nt(dxg, gg)
            qd = bgm * xdt[:, sl] * decx[:, sl]
            last = jnp.sum(qd, axis=0, keepdims=True) + ecl[:, sl] * jnp.sum(gst[:, sl] * h_in[:, sl], axis=0, keepdims=True)
            rowid = lax.broadcasted_iota(jnp.int32, (BLK, 256), 0)
            dcsx_parts.append(f["yoff"][:, sl] * dy[:, sl] - qd + jnp.where(rowid == BLK - 1, last, 0.0))
            cb_ = f["cbs"][g]
            cbt = _mm_nt(bg, cg)
            dcb_ = jnp.zeros((BLK, BLK), F32)
            dcbt = jnp.zeros((BLK, BLK), F32)
            dxd = []
            for r_ in range(4):
                h = 4 * g + r_
                hl = slice(h * HD, (h + 1) * HD)
                lm = f["lms"][h]
                segt = cst[h:h + 1, :] - cs[:, h:h + 1]
                lmt = jnp.where(causal_t, jnp.exp(jnp.minimum(segt, 0.0)), 0.0)
                dyh = dy[:, hl].astype(BF16)
                xdh = xdt[:, hl].astype(BF16)
                dw = _mm_nt(dyh, xdh)
                dwt = _mm_nt(xdh, dyh)
                wt = cbt * lmt
                dxd.append(_mm(wt.astype(BF16), dyh))
                dcb_ = dcb_ + dw * lm
                dcbt = dcbt + dwt * lmt
                col = jnp.sum(dw * (cb_ * lm), axis=-1, keepdims=True) - jnp.sum(dwt * wt, axis=-1, keepdims=True)
                dcs_intra = dcs_intra + jnp.where(lane == h, col, 0.0)
            dxdt.append(dxdt_g + jnp.concatenate(dxd, axis=1))
            dcs_.append(dc + _mm(dcb_.astype(BF16), bg))
            dbs.append(db + _mm(dcbt.astype(BF16), cg))
        dxdt = jnp.concatenate(dxdt, axis=1)
        dxs = dxs + dxdt * dtx
        ext_t_ = ext_ref[...]
        dcs = dcs_intra + _sel_r(jnp.concatenate(dcsx_parts, axis=1), ext_t_)
        da = _sel_l(triu_ref[...], dcs)
        ddt = da * f["av"] + _sel_r(dxdt * xs, ext_t_)
        dhd_ref[1:2, :] += jnp.sum(da * f["dt"], axis=0, keepdims=True)
        ddtr = ddt * _sig(f["dtin"])
        dhd_ref[0:1, :] += jnp.sum(ddtr, axis=0, keepdims=True)
        ddt_ref[rows, :] = ddtr.astype(BF16)

        sp, pre = f["sp"], f["pre"]
        dact = jnp.concatenate([dxs] + dbs + dcs_, axis=1)
        dpre = dact * (sp * (1.0 + pre * (1.0 - sp)))
        dcb_ref[0:1, :] += jnp.sum(dpre, axis=0, keepdims=True)
        ext2 = jnp.concatenate([dpre, nxt], axis=0)
        shifted = [pltpu.roll(ext2, BLK + 8 - (CK - 1 - k), 0)[0:BLK] for k in range(CK - 1)] + [dpre]
        cw = cw_ref[...]
        xc = xc_ref[rows, :]
        dxr = cw[CK - 1:CK, :] * dpre
        for k in range(CK):
            dcw_ref[k:k + 1, :] += jnp.sum(shifted[k] * xc, axis=0, keepdims=True)
            if k < CK - 1:
                dxr = dxr + cw[k:k + 1, :] * shifted[k]
        dxbc_ref[rows, :] = dxr.astype(BF16)
        return jnp.concatenate(dh_new, axis=1), dpre[0:8]

    cur = lambda w: pl.BlockSpec((tile, w), lambda i: (nt - 1 - i, 0))
    return _pcall(
        body, [z, xbc, pre_all, dtr, dys, hs, cw, dtb, av, dk, nw, ex, ext_t, tril, triu], name="ssd_bwd", grid=(nt,),
        out_shape=[jax.ShapeDtypeStruct((s, SW), BF16), jax.ShapeDtypeStruct((s, XBCW), BF16),
                   jax.ShapeDtypeStruct((s, 128), BF16), jax.ShapeDtypeStruct((8, XBCW), F32),
                   jax.ShapeDtypeStruct((8, XBCW), F32), jax.ShapeDtypeStruct((8, SW), F32),
                   jax.ShapeDtypeStruct((8, 128), F32)],
        in_specs=[cur(SW), cur(XBCW), cur(XBCW), cur(128), cur(SW),
                  pl.BlockSpec((SUBS, NST, SW), lambda i: (nt - 1 - i, 0, 0)),
                  _row((8, XBCW)), _row((1, 128)), _row((1, 128)), _row((1, SW)), _row((1, SW)),
                  _row((128, SW)), _row((SW, 128)), _row((BLK, BLK)), _row((BLK, BLK))],
        out_specs=[cur(SW), cur(XBCW), cur(128), _row((8, XBCW)), _row((8, XBCW)), _row((8, SW)), _row((8, 128))],
        scratch_shapes=[pltpu.VMEM((NST, SW), F32), pltpu.VMEM((8, XBCW), F32), pltpu.VMEM((8, SW), F32)], carry=carry)


def _load_once(i, pairs, sem):
    @pl.when(i == 0)
    def _():
        cps = [pltpu.make_async_copy(src, dst, sem.at[k]) for k, (src, dst) in enumerate(pairs)]
        for cp in cps:
            cp.start()
        for cp in cps:
            cp.wait()


def _mlp_fwd(x, ya, ys, tgt, w_o, w_ga, w_gb, w_dn, gate1, a2, sh2, gate2, fn):
    s = x.shape[0]
    sub_m, subs = 256, 2
    tm = sub_m * subs

    def body(x_ref, ya_ref, ys_ref, t_ref, wo_hbm, wga_hbm, wgb_hbm, wdn_hbm, g1_ref, a2_ref, s2_ref, g2_ref, fn_ref,
             x1_ref, gu_ref, dx2_ref, loss_ref, dfn_ref, wo, wga, wgb, wdn, sem):
        i = pl.program_id(0)
        _load_once(i, [(wo_hbm, wo), (wga_hbm, wga), (wgb_hbm, wgb), (wdn_hbm, wdn)], sem)

        @pl.when(i == 0)
        def _():
            loss_ref[...] = jnp.zeros_like(loss_ref)
            dfn_ref[...] = jnp.zeros_like(dfn_ref)

        def proj(st):
            st["mix"] = _mm(ya_ref[st["rows"], :], wo[0:QW, :]) + _mm(ys_ref[st["rows"], :], wo[QW:D, :])

        def norm(st):
            x1 = x_ref[st["rows"], :] + g1_ref[...] * st.pop("mix")
            x1_ref[st["rows"], :] = x1
            r2 = lax.rsqrt(jnp.mean(x1 * x1, axis=-1, keepdims=True) + EPS)
            st["x1"] = x1
            st["h2"] = (x1 * r2 * a2_ref[...] + s2_ref[...]).astype(BF16)

        def gate_up(st):
            h2 = st.pop("h2")
            ha, hb = h2[:, 0:D // 2], h2[:, D // 2:D]
            gub = jnp.concatenate([(_mm(ha, wga[j]) + _mm(hb, wgb[j])).astype(BF16) for j in range(4)], axis=1)
            gu_ref[st["rows"], :] = gub
            st["gub"] = gub

        def activate(st):
            gub = st.pop("gub")
            gv, uv = gub[:, 0:DFF].astype(F32), gub[:, DFF:].astype(F32)
            st["act"] = (gv * _sig(gv) * uv).astype(BF16)

        def down(st):
            st["ff"] = _mm(st.pop("act"), wdn[...])

        def head(st):
            x2 = st.pop("x1") + g2_ref[...] * st.pop("ff")
            r3 = lax.rsqrt(jnp.mean(x2 * x2, axis=-1, keepdims=True) + EPS)
            xn = x2 * r3
            fnv = fn_ref[...]
            err = xn * fnv - t_ref[st["rows"], :]
            st["loss"] = jnp.sum(err * err) * (0.5 / D)
            dy = err * (1.0 / D)
            st["dfn"] = jnp.sum(dy * xn, axis=0, keepdims=True)
            u = dy * fnv
            dx2_ref[st["rows"], :] = r3 * u - xn * (r3 * jnp.mean(u * xn, axis=-1, keepdims=True))

        a, b = [dict(rows=slice(k * sub_m, (k + 1) * sub_m)) for k in range(subs)]
        for stage, st in [(proj, a), (norm, a), (proj, b), (gate_up, a), (norm, b), (activate, a), (gate_up, b),
                          (down, a), (activate, b), (head, a), (down, b), (head, b)]:
            stage(st)
        loss_ref[...] += a["loss"] + b["loss"]
        dfn_ref[0:1, :] += a["dfn"] + b["dfn"]

    def tok(w):
        return pl.BlockSpec((tm, w), lambda i: (i, 0))

    hbm = pl.BlockSpec(memory_space=pl.ANY)
    return pl.pallas_call(
        body, name="mlp_fwd", grid=(s // tm,),
        out_shape=[jax.ShapeDtypeStruct((s, D), F32), jax.ShapeDtypeStruct((s, 2 * DFF), BF16),
                   jax.ShapeDtypeStruct((s, D), F32), jax.ShapeDtypeStruct((8, 128), F32),
                   jax.ShapeDtypeStruct((8, D), F32)],
        in_specs=[tok(D), tok(QW), tok(SW), tok(D), hbm, hbm, hbm, hbm,
                  _row((1, D)), _row((1, D)), _row((1, D)), _row((1, D)), _row((1, D))],
        out_specs=[tok(D), tok(2 * DFF), tok(D), _row((8, 128)), _row((8, D))],
        scratch_shapes=[pltpu.VMEM((D, D), BF16), pltpu.VMEM(w_ga.shape, BF16), pltpu.VMEM(w_gb.shape, BF16),
                        pltpu.VMEM((DFF, D), BF16), pltpu.SemaphoreType.DMA((4,))],
        compiler_params=_cp(("arbitrary",)),
    )(x, ya, ys, tgt, w_o, w_ga, w_gb, w_dn, gate1, a2, sh2, gate2, fn)


def _mlp_bwd(x1, gu, dx2, w_o, w_ga, w_gb, w_dn, gate1, a2, sh2, gate2):
    s = x1.shape[0]
    tm = 256
    nj = 2 * DFF // 4

    def body(x1_ref, gu_ref, dx2_ref, wo_hbm, wga_hbm, wgb_hbm, wdn_hbm, g1_ref, a2_ref, s2_ref, g2_ref,
             dx1_ref, dya_ref, dys_ref, act_ref, dgu_ref, h2_ref, dsh_ref, p_ref, wo, wga, wgb, wdn, sem):
        i = pl.program_id(0)
        _load_once(i, [(wo_hbm, wo), (wga_hbm, wga), (wgb_hbm, wgb), (wdn_hbm, wdn)], sem)

        @pl.when(i == 0)
        def _():
            dsh_ref[...] = jnp.zeros_like(dsh_ref)
            p_ref[...] = jnp.zeros_like(p_ref)

        dx2 = dx2_ref[...]
        dact = _mm_nt((dx2 * g2_ref[...]).astype(BF16), wdn[...])
        gub = gu_ref[...]
        gv, uv = gub[:, 0:DFF].astype(F32), gub[:, DFF:].astype(F32)
        sg = _sig(gv)
        sl = gv * sg
        act_ref[...] = (sl * uv).astype(BF16)
        dgu = jnp.concatenate([dact * uv * (sg * (1.0 + gv * (1.0 - sg))), dact * sl], axis=1).astype(BF16)
        dgu_ref[...] = dgu
        dha = sum(_mm_nt(dgu[:, j * nj:(j + 1) * nj], wga[j]) for j in range(4))
        dhb = sum(_mm_nt(dgu[:, j * nj:(j + 1) * nj], wgb[j]) for j in range(4))
        dh = jnp.concatenate([dha, dhb], axis=1)
        x1 = x1_ref[...]
        r2 = lax.rsqrt(jnp.mean(x1 * x1, axis=-1, keepdims=True) + EPS)
        xn = x1 * r2
        a2 = a2_ref[...]
        h2_ref[...] = (xn * a2 + s2_ref[...]).astype(BF16)
        dsh_ref[0:1, :] += jnp.sum(dh, axis=0, keepdims=True)
        p_ref[0:1, :] += jnp.sum(dh * xn, axis=0, keepdims=True)
        u = dh * a2
        dx1 = dx2 + r2 * u - xn * (r2 * jnp.mean(u * xn, axis=-1, keepdims=True))
        dx1_ref[...] = dx1
        dcat = _mm_nt((dx1 * g1_ref[...]).astype(BF16), wo[...])
        dya_ref[...] = dcat[:, 0:QW]
        dys_ref[...] = dcat[:, QW:D]

    def tok(w):
        return pl.BlockSpec((tm, w), lambda i: (i, 0))

    hbm = pl.BlockSpec(memory_space=pl.ANY)
    return pl.pallas_call(
        body, name="mlp_bwd", grid=(s // tm,),
        out_shape=[jax.ShapeDtypeStruct((s, D), F32), jax.ShapeDtypeStruct((s, QW), F32),
                   jax.ShapeDtypeStruct((s, SW), F32), jax.ShapeDtypeStruct((s, DFF), BF16),
                   jax.ShapeDtypeStruct((s, 2 * DFF), BF16), jax.ShapeDtypeStruct((s, D), BF16),
                   jax.ShapeDtypeStruct((8, D), F32), jax.ShapeDtypeStruct((8, D), F32)],
        in_specs=[tok(D), tok(2 * DFF), tok(D), hbm, hbm, hbm, hbm, _row((1, D)), _row((1, D)), _row((1, D)), _row((1, D))],
        out_specs=[tok(D), tok(QW), tok(SW), tok(DFF), tok(2 * DFF), tok(D), _row((8, D)), _row((8, D))],
        scratch_shapes=[pltpu.VMEM((D, D), BF16), pltpu.VMEM(w_ga.shape, BF16), pltpu.VMEM(w_gb.shape, BF16),
                        pltpu.VMEM((DFF, D), BF16), pltpu.SemaphoreType.DMA((4,))],
        compiler_params=_cp(("arbitrary",)),
    )(x1, gu, dx2, w_o, w_ga, w_gb, w_dn, gate1, a2, sh2, gate2)


def _wgrad(name, a, b, gate, w, carry=None):
    s, m = a.shape
    n = b.shape[1]
    tk = min(1024, s)
    nk = s // tk

    def body(a_ref, b_ref, g_ref, w_ref, o_hbm, dg_ref, acc_ref, sem):
        k = pl.program_id(0)

        @pl.when(k == 0)
        def _():
            acc_ref[...] = jnp.zeros_like(acc_ref)

        acc_ref[...] += _mm_tn(a_ref[...], b_ref[...].astype(BF16))

        @pl.when(k == nk - 1)
        def _():
            acc = acc_ref[...]
            dg_ref[...] = jnp.zeros_like(dg_ref)
            dg_ref[0:1, :] = jnp.sum(acc * w_ref[...].astype(F32), axis=0, keepdims=True)
            acc_ref[...] = acc * g_ref[...]
            cp = pltpu.make_async_copy(acc_ref, o_hbm, sem)
            cp.start()
            cp.wait()

    return _pcall(body, [a, b, gate, w], name=name, grid=(nk,),
                  out_shape=[jax.ShapeDtypeStruct((m, n), F32), jax.ShapeDtypeStruct((8, n), F32)],
                  in_specs=[pl.BlockSpec((tk, m), lambda k: (k, 0)), pl.BlockSpec((tk, n), lambda k: (k, 0)),
                            _row((1, n)), _row((m, n))],
                  out_specs=[pl.BlockSpec(memory_space=pl.ANY), _row((8, n))],
                  scratch_shapes=[pltpu.VMEM((m, n), F32), pltpu.SemaphoreType.DMA], carry=carry)


def _wgrad_gate_up(h2, dgu, carry=None):
    s = h2.shape[0]
    tk = min(1024, s)
    nk = s // tk
    n = dgu.shape[1]
    nj = n // 4

    def body(a_ref, b_ref, o_hbm, acc_ref, sems):
        k = pl.program_id(0)

        @pl.when(k == 0)
        def _():
            acc_ref[...] = jnp.zeros_like(acc_ref)

        acc_ref[...] += _mm_tn(a_ref[...], b_ref[...])

        @pl.when(k == nk - 1)
        def _():
            cps = [pltpu.make_async_copy(acc_ref.at[:, pl.ds(j * nj, nj)], o_hbm.at[j], sems.at[j]) for j in range(4)]
            for cp in cps:
                cp.start()
            for cp in cps:
                cp.wait()

    return _pcall(body, [h2, dgu], name="wgrad_gate_up", grid=(nk,),
                  out_shape=[jax.ShapeDtypeStruct((4, D, nj), F32)],
                  in_specs=[pl.BlockSpec((tk, D), lambda k: (k, 0)), pl.BlockSpec((tk, n), lambda k: (k, 0))],
                  out_specs=[pl.BlockSpec(memory_space=pl.ANY)],
                  scratch_shapes=[pltpu.VMEM((D, n), F32), pltpu.SemaphoreType.DMA((4,))], carry=carry)


def _wgrad_in_t(h1, pieces, carry=None):
    s = h1.shape[0]
    tk = min(1024, s)
    nk = s // tk

    def body(a_ref, dq_ref, dkv_ref, dz_ref, dxbc_ref, ddt_ref, o_hbm, acc_ref, tr_ref, sem):
        k = pl.program_id(0)

        @pl.when(k == 0)
        def _():
            acc_ref[...] = jnp.zeros_like(acc_ref)

        dproj = jnp.concatenate([dq_ref[...], dkv_ref[...], dz_ref[...], dxbc_ref[...], ddt_ref[...]], axis=1)
        acc_ref[...] += _mm_tn(a_ref[...], dproj)

        @pl.when(k == nk - 1)
        def _():
            for j in range(PROJ_W // 128):
                tr_ref[j * 128:(j + 1) * 128, :] = acc_ref[:, j * 128:(j + 1) * 128].T
            cp = pltpu.make_async_copy(tr_ref, o_hbm, sem)
            cp.start()
            cp.wait()

    return _pcall(body, [h1] + list(pieces), name="wgrad_in", grid=(nk,),
                  out_shape=[jax.ShapeDtypeStruct((PROJ_W, D), F32)],
                  in_specs=[pl.BlockSpec((tk, p.shape[1]), lambda k: (k, 0)) for p in [h1] + list(pieces)],
                  out_specs=[pl.BlockSpec(memory_space=pl.ANY)],
                  scratch_shapes=[pltpu.VMEM((D, PROJ_W), F32), pltpu.VMEM((PROJ_W, D), F32), pltpu.SemaphoreType.DMA],
                  carry=carry)


_SMALL = ["ada_b", "norm1", "conv_w", "conv_b", "dt_bias", "A_log", "D_skip", "sinks", "attn_out_norm",
          "ssm_out_norm", "norm2", "rel_bias", "final_norm"]


def _small_grad(name, gs, chip):
    if name == "ada_b":
        return jnp.concatenate([gs[j:j + 1, :] for j in range(6)], axis=1)
    if name == "conv_w":
        full = gs[7:11, :]
        out = full[:, 0:256]
        for j in range(1, 4):
            out = jnp.where(chip == j, full[:, j * 256:(j + 1) * 256], out)
        return out
    row, width = {"norm1": (6, D), "conv_b": (11, D), "norm2": (12, D), "final_norm": (13, D),
                  "attn_out_norm": (14, QW), "ssm_out_norm": (15, SW), "dt_bias": (16, NH), "A_log": (17, NH),
                  "D_skip": (18, NH), "sinks": (19, NH), "rel_bias": (24, NH)}[name]
    rows = NBUCKET if name == "rel_bias" else 1
    return gs[row:row + rows, 0:width]


def _small_update(small_all, where, ws, ms, vs):
    n = len(_SMALL)

    def body(where_ref, sa_ref, *refs):
        w_refs, m_refs, v_refs, outs = refs[:n], refs[n:2 * n], refs[2 * n:3 * n], refs[3 * n:]
        gs = sa_ref[0]
        for b in range(1, 8):
            gs = gs + sa_ref[b]
        chip = where_ref[1]
        for i, name in enumerate(_SMALL):
            g = _small_grad(name, gs, chip)
            lead = (0,) if name == "conv_w" else ()
            d, mo, vo = _adamw(w_refs[i][lead + (...,)], g, m_refs[i][lead + (...,)], v_refs[i][lead + (...,)])
            for k, val in enumerate((g, d, mo, vo)):
                outs[k * n + i][lead + (...,)] = val
        outs[4 * n][...] = gs[20:21, 0:128]

    shapes = [jax.ShapeDtypeStruct(w.shape, F32) for w in ws]
    vmem = pl.BlockSpec(memory_space=pltpu.VMEM)
    res = pl.pallas_call(
        body, name="small_update", out_shape=shapes * 4 + [jax.ShapeDtypeStruct((1, 128), F32)],
        in_specs=[pl.BlockSpec(memory_space=pltpu.SMEM)] + [vmem] * (1 + 3 * n), out_specs=[vmem] * (4 * n + 1),
    )(where, small_all, *ws, *ms, *vs)
    return [res[k * n:(k + 1) * n] for k in range(4)], res[4 * n][0, 0]


def _add_half(name, g, got, where, by_cols=False):
    rr, cc = got.shape[1:]
    if by_cols:
        mine = pl.BlockSpec((None, rr, cc), lambda i, w_ref: (i, 0, w_ref[0]))
    else:
        mine = pl.BlockSpec((None, None, rr, cc), lambda i, w_ref: (i, w_ref[0], 0, 0))

    def body(w_ref, g_ref, r_ref, o_ref, own_ref):
        s = g_ref[...] + r_ref[...]
        o_ref[...] = s.astype(BF16)

        @pl.when(pl.program_id(0) == w_ref[1])
        def _():
            own_ref[...] = s

    spec = pl.BlockSpec((None, rr, cc), lambda i, w_ref: (i, 0, 0))
    return _pcall(body, [where, g, got], name=name, grid=(4,), nprefetch=1,
                  out_shape=[jax.ShapeDtypeStruct(got.shape, BF16), jax.ShapeDtypeStruct((rr, cc), F32)],
                  in_specs=[mine, spec],
                  out_specs=[spec, pl.BlockSpec((rr, cc), lambda i, w_ref: (0, 0))])


def _add_chips(name, own, got):
    rr, cc = own.shape
    tr = rr // 2 if rr % 32 == 0 else rr

    def body(s_ref, r_ref, o_ref):
        o_ref[...] = ((s_ref[...] + r_ref[0].astype(F32)) + r_ref[1].astype(F32)) + r_ref[2].astype(F32)

    spec = pl.BlockSpec((tr, cc), lambda i: (i, 0))
    return _pcall(body, [own, got], name=name, grid=(rr // tr,), out_shape=[jax.ShapeDtypeStruct((rr, cc), F32)],
                  in_specs=[spec, pl.BlockSpec((3, tr, cc), lambda i: (0, i, 0))], out_specs=[spec])[0]


def _adamw_halves(name, mine, got, w, m, v, where, by_cols=False):
    rr, cc = mine.shape

    def body(w_ref_, t_ref, r_ref, w_ref, m_ref, v_ref, g_ref, d_ref, mo_ref, vo_ref):
        g = jnp.where(pl.program_id(0) == w_ref_[0], t_ref[...], r_ref[...])
        g_ref[...] = g
        d_ref[...], mo_ref[...], vo_ref[...] = _adamw(w_ref[...], g, m_ref[...], v_ref[...])

    if by_cols:
        grid = (2, 1)
        half = pl.BlockSpec((rr, cc), lambda h, i, w_ref_: (0, 0))
        full = pl.BlockSpec((rr, cc), lambda h, i, w_ref_: (0, h))
    else:
        tr = rr // 2
        grid = (2, 2)
        half = pl.BlockSpec((tr, cc), lambda h, i, w_ref_: (i, 0))
        full = pl.BlockSpec((None, tr, cc), lambda h, i, w_ref_: (0, 2 * h + i, 0))
    return _pcall(body, [where, mine, got, w, m, v], name=name, grid=grid, nprefetch=1,
                  out_shape=[jax.ShapeDtypeStruct(w.shape, F32)] * 4,
                  in_specs=[half, half, full, full, full], out_specs=[full] * 4)


def _bias_table(rel_bias, bucket, mask):
    def body(rb_ref, bk_ref, mk_ref, o_ref):
        bk = bk_ref[...]
        valid = mk_ref[...] > 0
        for h in range(NH):
            acc = jnp.zeros((BLK, 2 * BLK), F32)
            for b in range(NBUCKET):
                acc = jnp.where(bk == b, rb_ref[b, h], acc)
            o_ref[h] = jnp.where(valid, acc, NEG)

    vmem = pl.BlockSpec(memory_space=pltpu.VMEM)
    return pl.pallas_call(
        body, name="bias_table", out_shape=jax.ShapeDtypeStruct((NH, BLK, 2 * BLK), F32),
        in_specs=[pl.BlockSpec(memory_space=pltpu.SMEM), vmem, vmem], out_specs=vmem,
    )(rel_bias, bucket, mask)


def _pack_small(dsh1, p1, dsh2, p2, dg1a, dg1b, dg2, norm1, norm2, scale1, scale2, dcw, dcb, dfn,
                dnw_attn, dnw_ssm, dhd, av, dsink, drel, loss_acc):
    def body(dsh1_ref, p1_ref, dsh2_ref, p2_ref, dg1a_ref, dg1b_ref, dg2_ref, n1_ref, n2_ref, s1_ref, s2_ref,
             dcw_ref, dcb_ref, dfn_ref, da_ref, ds_ref, dhd_ref, av_ref, dsink_ref, drel_ref, loss_ref, o_ref):
        o_ref[...] = jnp.zeros_like(o_ref)
        p1v, p2v = p1_ref[0:1, :], p2_ref[0:1, :]
        o_ref[0:1, :] = dsh1_ref[0:1, :]
        o_ref[1:2, :] = p1v * n1_ref[...]
        o_ref[2:3, :] = dg1a_ref[0:1, :] + dg1b_ref[0:1, :]
        o_ref[3:4, :] = dsh2_ref[0:1, :]
        o_ref[4:5, :] = p2v * n2_ref[...]
        o_ref[5:6, :] = dg2_ref[0:1, :]
        o_ref[6:7, :] = p1v * (1.0 + s1_ref[...])
        o_ref[7:11, :] = dcw_ref[0:4, :]
        o_ref[11:12, :] = dcb_ref[0:1, :]
        o_ref[12:13, :] = p2v * (1.0 + s2_ref[...])
        o_ref[13:14, :] = dfn_ref[0:1, :]
        o_ref[14:15, 0:QW] = da_ref[0:1, :]
        o_ref[15:16, 0:SW] = ds_ref[0:1, :]
        o_ref[16:17, 0:128] = dhd_ref[0:1, :]
        o_ref[17:18, 0:128] = dhd_ref[1:2, :] * av_ref[...]
        o_ref[18:19, 0:128] = dhd_ref[2:3, :]
        o_ref[19:20, 0:128] = dsink_ref[0:1, :]
        o_ref[20:21, 0:128] = loss_ref[0:1, :]
        o_ref[24:56, 0:128] = drel_ref[...]

    return pl.pallas_call(body, name="pack_small", out_shape=jax.ShapeDtypeStruct((56, D), F32))(
        dsh1, p1, dsh2, p2, dg1a, dg1b, dg2, norm1, norm2, scale1, scale2, dcw, dcb, dfn,
        dnw_attn, dnw_ssm, dhd, av, dsink, drel, loss_acc)


def _pad_row(a, rows=1):
    return jnp.pad(a.reshape(rows, -1), ((0, 0), (0, D - a.size // rows)))


def kernel(x, c, ada_w, ada_b, norm1, w_in, conv_w, conv_b, dt_bias, A_log, D_skip, sinks, attn_out_norm, ssm_out_norm, w_o, norm2, w_gate_up, w_down, rel_bias, final_norm, loss_target, m_ada_w, m_ada_b, m_norm1, m_w_in, m_conv_w, m_conv_b, m_dt_bias, m_A_log, m_D_skip, m_sinks, m_attn_out_norm, m_ssm_out_norm, m_w_o, m_norm2, m_w_gate_up, m_w_down, m_rel_bias, m_final_norm, v_ada_w, v_ada_b, v_norm1, v_w_in, v_conv_w, v_conv_b, v_dt_bias, v_A_log, v_D_skip, v_sinks, v_attn_out_norm, v_ssm_out_norm, v_w_o, v_norm2, v_w_gate_up, v_w_down, v_rel_bias, v_final_norm):
    xi, yi, ci = lax.axis_index("x"), lax.axis_index("y"), lax.axis_index("c")
    chip = 2 * xi + yi
    me = 4 * xi + 2 * yi + ci
    where = jnp.stack([ci, chip]).astype(jnp.int32)
    xs2, tgt = x[0], loss_target[0]

    first = jnp.concatenate([c, _pad_row(conv_w[0], CK), jnp.zeros((3, D), F32)], axis=0)
    w_in_t, m_w_in_t, v_w_in_t = w_in[0].T, m_w_in[0].T, v_w_in[0].T
    w_in_b, w_o_b, w_dn_b = w_in_t.astype(BF16), w_o[0].astype(BF16), w_down[0].astype(BF16)
    w_gu_b = w_gate_up[0].astype(BF16)
    hw = D // 2

    def fetch(flips):
        return _Carry([w_in_b], [jax.ShapeDtypeStruct((2, w_in_b.shape[0], hw), BF16)],
                      lambda x_, y_, c_: [(f, 0, (slice(None), pl.ds(c_ * hw, hw)), 0, k) for k, f in enumerate(flips)])

    first_all, w_own_diag = _exchange("gather_cond", _merge(_gather8_carry(first), fetch([None, (1, 1, 0)])))
    c_all = first_all[:, 0, :]
    cw_full = jnp.concatenate([first_all[2 * j, 1:1 + CK, 0:256] for j in range(4)], axis=1)

    ncol = ada_w.shape[2]
    mod_cols, w_nbrs = _ada_fwd(c_all, ada_w[0], lax.dynamic_slice(ada_b, (0, chip * ncol), (1, ncol)),
                                carry=fetch([(0, 1, 0), (1, 0, 0)]))
    by_flip = jnp.concatenate([w_own_diag, w_nbrs], axis=0)
    w_half = jnp.take(by_flip, jnp.asarray([0, 2, 3, 1])[jnp.bitwise_xor(chip, jnp.arange(4))], axis=0)
    to_other = _Carry([w_half[None]], [jax.ShapeDtypeStruct((1,) + w_half.shape, BF16)],
                      lambda x_, y_, c_: [(_SIBLING, 0, 0, 0, 0)])
    mod_all, w_other = _exchange("gather_mod", _merge(_gather_chips_carry([mod_cols]), to_other))
    w_lo = jnp.where(ci == 0, w_half, w_other[0])
    w_hi = jnp.where(ci == 0, w_other[0], w_half)
    w_in_f = jnp.pad(jnp.concatenate([w_lo, w_hi], axis=2).reshape(IN_W, D), ((0, PROJ_W - IN_W), (0, 0)))
    mod = lax.dynamic_slice(jnp.transpose(mod_all, (1, 0, 2)).reshape(8, 4 * ncol), (me, 0), (1, 4 * ncol))
    shift1, scale1, gate1, shift2, scale2, gate2 = [mod[:, j * D:(j + 1) * D] for j in range(6)]
    a1 = norm1 * (1.0 + scale1)
    a2 = norm2 * (1.0 + scale2)

    hdn = DFF // 8
    q, kv, z, xbc, dtr, w_o_g, w_dna_g = _in_proj_fwd(xs2, a1, shift1, w_in_f,
                                                      carry=_gather_chips_carry([w_o_b, w_dn_b[0:hdn]]))
    w_o_f = w_o_g.reshape(D, D)
    bucket, mask = _attn_geometry()
    bucket = jnp.asarray(bucket)
    bias = _bias_table(rel_bias, bucket, jnp.asarray(mask.astype(np.int32)))
    sinks1 = sinks[0]
    ya, w_ga_g = _attn_fwd(q, kv, bias, sinks1, attn_out_norm, carry=_gather_chips_carry([w_gu_b[0:D // 2]]))
    cw8 = jnp.concatenate([cw_full, jnp.zeros((4, XBCW), F32)], axis=0)
    dtb = _pad_row(dt_bias)[:, 0:128]
    av = _pad_row(-jnp.exp(A_log))[:, 0:128]
    dk = jnp.repeat(D_skip, HD, axis=1)
    ys, hs, pre, w_gb_g, w_dnb_g = _ssd_fwd(z, xbc, dtr, cw8, conv_b, dtb, av, dk, ssm_out_norm,
                                            carry=_gather_chips_carry([w_gu_b[D // 2:D], w_dn_b[hdn:2 * hdn]]))
    w_dn_f = jnp.stack([w_dna_g, w_dnb_g], axis=1).reshape(DFF, D)
    fn = final_norm[None, :]
    x1, gu, dx2, loss_acc, dfn = _mlp_fwd(xs2, ya, ys, tgt, w_o_f, w_ga_g, w_gb_g, w_dn_f, gate1, a2, shift2, gate2, fn)

    def to_sibling(p):
        return _Carry([p], [jax.ShapeDtypeStruct((4,) + p.shape[2:], F32)],
                      lambda x_, y_, c_: [(_SIBLING, 0, (j, 1 - c_), 0, j) for j in range(4)])

    def to_chips(s4):
        return _Carry([s4], [jax.ShapeDtypeStruct((3,) + s4.shape[1:], s4.dtype)],
                      lambda x_, y_, c_: [(f, 0, jnp.bitwise_xor(2 * x_ + y_, k + 1), 0, k) for k, f in enumerate(_CHIPS3)])

    def back(t):
        return _Carry([t[None]], [jax.ShapeDtypeStruct((1,) + t.shape, F32)], lambda x_, y_, c_: [(_SIBLING, 0, 0, 0, 0)])

    dx1, dya, dys, act, dgu, h2, dsh2, p2 = _mlp_bwd(x1, gu, dx2, w_o_f, w_ga_g, w_gb_g, w_dn_f, gate1, a2, shift2, gate2)
    p_gu = _wgrad_gate_up(h2, dgu)[0].reshape(4, 2, D // 2, 2 * DFF // 4)
    dq, dkv, dbias, dsink, dnw_attn, g_dn, dg2, got1_gu = _attn_bwd(
        q, kv, dya, bias, sinks1, attn_out_norm, act, dx2, gate2, w_dn_f, carry=to_sibling(p_gu))
    p_dn = g_dn.reshape(4, 2, DFF // 8, D)
    drel = _rel_bias_grad(dbias, bucket)
    s4_gu, own_gu = _add_half("rs_add_half_gu", p_gu, got1_gu, where)
    dz, dxbc, ddt, dcw, dcb, dnw_ssm, dhd, got2_gu, got1_dn = _ssd_bwd(
        z, xbc, pre, dtr, dys, hs, cw8, dtb, av, dk, ssm_out_norm, carry=_merge(to_chips(s4_gu), to_sibling(p_dn)))
    mine_gu = _add_chips("rs_add_chips_gu", own_gu, got2_gu)
    s4_dn, own_dn = _add_half("rs_add_half_dn", p_dn, got1_dn, where)
    grad_x, h1, dsh1, p1 = _in_proj_bwd(xs2, dx1, a1, shift1, w_in_f, dq, dkv, dz, dxbc, ddt)
    g_in_t, got2_dn, got3_gu = _wgrad_in_t(h1, [dq, dkv, dz, dxbc, ddt],
                                           carry=_merge(to_chips(s4_dn), back(mine_gu)))
    mine_dn = _add_chips("rs_add_chips_dn", own_dn, got2_dn)
    p_in = g_in_t[0:IN_W].reshape(4, IN_W // 4, D)

    def to_sibling_cols(p):
        return _Carry([p], [jax.ShapeDtypeStruct(p.shape[:2] + (D // 2,), F32)],
                      lambda x_, y_, c_: [(_SIBLING, 0, (j, slice(None), pl.ds((1 - c_) * (D // 2), D // 2)), 0, j)
                                          for j in range(4)])

    no_dg1 = jnp.zeros((8, D), F32)
    small = _pack_small(dsh1, p1, dsh2, p2, no_dg1, no_dg1, dg2, norm1, norm2, scale1, scale2, dcw, dcb, dfn,
                        dnw_attn, dnw_ssm, dhd, av, dsink, drel, loss_acc)
    g_oa, dg1a, got1_in, got3_dn, small_all = _wgrad(
        "wgrad_o_attn", ya, dx1, gate1, w_o_f[0:QW],
        carry=_merge(to_sibling_cols(p_in), back(mine_dn), _gather8_carry(small)))
    s4_in, own_in = _add_half("rs_add_half_in", p_in, got1_in, where, by_cols=True)
    g_os, dg1b, got2_in = _wgrad("wgrad_o_ssm", ys, dx1, gate1, w_o_f[QW:D], carry=to_chips(s4_in))
    mine_in = _add_chips("rs_add_chips_in", own_in, got2_in)
    p_o = jnp.concatenate([g_oa, g_os], axis=0).reshape(4, 2, D // 8, D)

    dg1_all, got1_o, got3_in = _exchange(
        "gather_tail", _merge(_gather8_carry(dg1a[0:1] + dg1b[0:1]), to_sibling(p_o), back(mine_in)))
    small_all = small_all.at[:, 2, :].set(dg1_all[:, 0, :])
    s4_o, own_o = _add_half("rs_add_half_o", p_o, got1_o, where)
    mine_o = _add_chips("rs_add_chips_o", own_o, _exchange("rs_chips_o", to_chips(s4_o))[0])
    got3_o = _exchange("rs_back_o", back(mine_o))[0]
    small_res, loss = _small_update(
        small_all, where,
        [ada_b, norm1, conv_w, conv_b, dt_bias, A_log, D_skip, sinks, attn_out_norm, ssm_out_norm, norm2, rel_bias,
         final_norm[None, :]],
        [m_ada_b, m_norm1, m_conv_w, m_conv_b, m_dt_bias, m_A_log, m_D_skip, m_sinks, m_attn_out_norm,
         m_ssm_out_norm, m_norm2, m_rel_bias, m_final_norm[None, :]],
        [v_ada_b, v_norm1, v_conv_w, v_conv_b, v_dt_bias, v_A_log, v_D_skip, v_sinks, v_attn_out_norm,
         v_ssm_out_norm, v_norm2, v_rel_bias, v_final_norm[None, :]])
    small_out = [dict(zip(_SMALL, r)) for r in small_res]
    for r in small_out:
        r["final_norm"] = r["final_norm"][0]

    dmod_all = small_all[:, 0:6, :].reshape(8, 6 * D)
    dmod_loc = lax.dynamic_slice(dmod_all, (0, chip * ncol), (8, ncol))
    ada_out = _ada_bwd_adamw(c_all.T, dmod_loc, ada_w[0], m_ada_w[0], v_ada_w[0])

    big_gu = _adamw_halves("adamw_gate_up", mine_gu, got3_gu[0], w_gate_up, m_w_gate_up, v_w_gate_up, where)
    big_dn = _adamw_halves("adamw_down", mine_dn, got3_dn[0], w_down, m_w_down, v_w_down, where)
    big_o = _adamw_halves("adamw_o", mine_o, got3_o[0], w_o, m_w_o, v_w_o, where)
    big_in = [o.T[None] for o in _adamw_halves("adamw_in", mine_in, got3_in[0], w_in_t, m_w_in_t, v_w_in_t, where,
                                               by_cols=True)]
    big = [big_in, big_o, big_gu, big_dn]

    order = ["ada_w", "ada_b", "norm1", "w_in", "conv_w", "conv_b", "dt_bias", "A_log", "D_skip", "sinks",
             "attn_out_norm", "ssm_out_norm", "w_o", "norm2", "w_gate_up", "w_down", "rel_bias", "final_norm"]
    bigname = {"w_in": 0, "w_o": 1, "w_gate_up": 2, "w_down": 3}
    res = [loss, grad_x[None]]
    for kind in range(4):
        for nm in order:
            if nm == "ada_w":
                res.append(ada_out[kind][None])
            elif nm in bigname:
                res.append(big[bigname[nm]][kind])
            else:
                res.append(small_out[kind][nm])
    return tuple(res)
```

```python
import numpy as np
import jax
import jax.numpy as jnp
from jax import lax
from jax.experimental import pallas as pl
from jax.experimental.pallas import tpu as pltpu

F32, BF16 = jnp.float32, jnp.bfloat16
HI = lax.Precision.HIGHEST

D = 1024
QW, KVW = 512, 128
NH, HD, NKV = 8, 64, 2
SW = 512
NST = 128
XBCW = 1024
CK = 4
BLK = 128
DFF = 2816
IN_W = 2312
PROJ_W = 2432
EPS = 1e-6
NEG = -1e30
NBUCKET = 32

B1, B2, LR, AEPS, WD, STEP = 0.9, 0.999, 0.001, 1e-08, 0.01, 10

VMEM_LIMIT = 56 * 1024 * 1024

_NT = (((1,), (1,)), ((), ()))
_TN = (((0,), (0,)), ((), ()))


def _mm(a, b):
    return jnp.dot(a, b, preferred_element_type=F32)


def _mm_nt(a, b):
    return lax.dot_general(a, b, _NT, preferred_element_type=F32)


def _mm_tn(a, b):
    return lax.dot_general(a, b, _TN, preferred_element_type=F32)


def _mm_hi(a, b):
    return jnp.dot(a, b, preferred_element_type=F32, precision=HI)


def _split3(x):
    hi = x.astype(BF16)
    r = x - hi.astype(F32)
    mid = r.astype(BF16)
    lo = (r - mid.astype(F32)).astype(BF16)
    return hi, mid, lo


def _sel_r(x, e):
    hi, mid, lo = _split3(x)
    return (_mm(hi, e) + _mm(mid, e)) + _mm(lo, e)


def _sel_l(e, x):
    hi, mid, lo = _split3(x)
    return (_mm(e, hi) + _mm(e, mid)) + _mm(e, lo)


def _sig(x):
    return 1.0 / (1.0 + jnp.exp(-x))


def _cp(sem):
    return pltpu.CompilerParams(dimension_semantics=sem, vmem_limit_bytes=VMEM_LIMIT)


def _row(shape):
    nd = len(shape)
    return pl.BlockSpec(shape, lambda *_: (0,) * nd)


def _adamw(w, g, m, v):
    m = B1 * m + (1.0 - B1) * g
    v = B2 * v + (1.0 - B2) * (g * g)
    m_hat = m / (1.0 - B1 ** STEP)
    v_hat = v / (1.0 - B2 ** STEP)
    delta = -LR * (m_hat / (jnp.sqrt(v_hat) + AEPS) + WD * w)
    return delta, m, v


class _Carry:
    def __init__(self, inps, outs, copies):
        self.inps, self.outs, self.copies = list(inps), list(outs), copies
        self.n = len(copies(0, 0, 0))

    def descriptors(self, in_refs, out_refs, send_sems, recv_sems):
        x, y, c = lax.axis_index("x"), lax.axis_index("y"), lax.axis_index("c")
        out = []
        for j, (flip, a, si, o, di) in enumerate(self.copies(x, y, c)):
            if flip is None:
                out.append(pltpu.make_async_copy(in_refs[a].at[si], out_refs[o].at[di], send_sems.at[j]))
            else:
                fx, fy, fc = flip
                peer = (1 - x if fx else x, 1 - y if fy else y, 1 - c if fc else c)
                out.append(pltpu.make_async_remote_copy(
                    src_ref=in_refs[a].at[si], dst_ref=out_refs[o].at[di],
                    send_sem=send_sems.at[j], recv_sem=recv_sems.at[j],
                    device_id=peer, device_id_type=pl.DeviceIdType.MESH))
        return out


def _pcall(body, args, *, name, grid, in_specs, out_specs, out_shape, scratch_shapes=(), sem=None, nprefetch=0,
           carry=None):
    out_shape, out_specs = list(out_shape), list(out_specs)
    in_specs, scratch_shapes = list(in_specs), list(scratch_shapes)
    nin, nout, nscr = len(in_specs), len(out_shape), len(scratch_shapes)
    run = body
    if carry is not None:
        ncin, ncout = len(carry.inps), len(carry.outs)
        hbm = pl.BlockSpec(memory_space=pl.ANY)

        def run(*refs):
            pre, r = refs[:nprefetch], refs[nprefetch:]
            ins, cins = r[:nin], r[nin:nin + ncin]
            r = r[nin + ncin:]
            outs, couts = r[:nout], r[nout:nout + ncout]
            r = r[nout + ncout:]
            scr, (send_sems, recv_sems) = r[:nscr], r[nscr:]
            first = pl.program_id(0) == 0
            last = pl.program_id(0) == grid[0] - 1
            for ax in range(1, len(grid)):
                first = jnp.logical_and(first, pl.program_id(ax) == 0)
                last = jnp.logical_and(last, pl.program_id(ax) == grid[ax] - 1)

            @pl.when(first)
            def _():
                for d in carry.descriptors(cins, couts, send_sems, recv_sems):
                    d.start()

            body(*pre, *ins, *outs, *scr)

            @pl.when(last)
            def _():
                for d in carry.descriptors(cins, couts, send_sems, recv_sems):
                    d.wait()

        in_specs = in_specs + [hbm] * ncin
        out_specs = out_specs + [hbm] * ncout
        out_shape = out_shape + carry.outs
        scratch_shapes = scratch_shapes + [pltpu.SemaphoreType.DMA((carry.n,)), pltpu.SemaphoreType.DMA((carry.n,))]
        args = list(args) + carry.inps
    if sem is None:
        sem = ("arbitrary",) * len(grid)
    if nprefetch:
        kw = dict(grid_spec=pltpu.PrefetchScalarGridSpec(num_scalar_prefetch=nprefetch, grid=grid, in_specs=in_specs,
                                                         out_specs=out_specs, scratch_shapes=scratch_shapes))
    else:
        kw = dict(grid=grid, in_specs=in_specs, out_specs=out_specs, scratch_shapes=scratch_shapes)
    res = pl.pallas_call(run, name=name, out_shape=out_shape, compiler_params=_cp(sem), **kw)(*args)
    return list(res)


def _merge(*carries):
    inps, outs, offs = [], [], []
    for cr in carries:
        offs.append((len(inps), len(outs)))
        inps += cr.inps
        outs += cr.outs

    def copies(x, y, c):
        return [(f, a + io, si, o + oo, di) for cr, (io, oo) in zip(carries, offs) for f, a, si, o, di in cr.copies(x, y, c)]

    return _Carry(inps, outs, copies)


def _exchange(name, carry):
    return _pcall(lambda: None, [], name=name, grid=(1,), in_specs=[], out_specs=[], out_shape=[], carry=carry)


_ALL7 = [(f >> 2 & 1, f >> 1 & 1, f & 1) for f in range(1, 8)]
_CHIPS3 = [(0, 1, 0), (1, 0, 0), (1, 1, 0)]
_SIBLING = (0, 0, 1)


def _gather8_carry(blk):
    def copies(x, y, c):
        me = 4 * x + 2 * y + c
        return [(None, 0, 0, 0, me)] + [(f, 0, 0, 0, me) for f in _ALL7]

    return _Carry([blk[None]], [jax.ShapeDtypeStruct((8,) + blk.shape, blk.dtype)], copies)


def _gather_chips_carry(blks):
    def copies(x, y, c):
        chip = 2 * x + y
        return [(f, a, 0, a, chip) for a in range(len(blks)) for f in [None] + _CHIPS3]

    return _Carry([b[None] for b in blks], [jax.ShapeDtypeStruct((4,) + b.shape, b.dtype) for b in blks], copies)


def _front(first, w_in_t, w_loc, b_loc):
    n = w_loc.shape[1]
    rows = w_in_t.shape[0]
    hw = D // 2
    gather = _gather8_carry(first)
    fetch = _Carry([w_in_t], [jax.ShapeDtypeStruct((4, rows, hw), BF16)],
                   lambda x_, y_, c_: [(f, 0, (slice(None), pl.ds(c_ * hw, hw)), 0, 2 * x_ + y_) for f in [None] + _CHIPS3])
    phase_a = _merge(gather, fetch)
    send_mod = _Carry([None], [None], lambda x_, y_, c_: [(f, 0, slice(None), 0, 2 * x_ + y_) for f in [None] + _CHIPS3])
    swap = _Carry([None], [None], lambda x_, y_, c_: [(_SIBLING, 0, slice(None), 0, slice(None))])

    def body(first_hbm, w_hbm, wl_ref, bl_ref, first_all, w_half, mod_all, w_other,
             c_scr, mod_scr, sa, ra, sb, rb, sc, rc, sl):
        da = phase_a.descriptors([first_hbm, w_hbm], [first_all, w_half], sa, ra)
        for d in da:
            d.start()
        for d in da[:gather.n]:
            d.wait()
        cp = pltpu.make_async_copy(first_all, c_scr, sl)
        cp.start()
        cp.wait()
        cv = c_scr[:, 0, :]
        cond = cv * _sig(cv)
        for j in range(n // 512):
            cols = slice(j * 512, (j + 1) * 512)
            mod_scr[:, cols] = _mm_hi(cond, wl_ref[:, cols]) + bl_ref[:, cols]
        db = send_mod.descriptors([mod_scr], [mod_all], sb, rb)
        for d in db:
            d.start()
        for d in da[gather.n:]:
            d.wait()
        dc = swap.descriptors([w_half], [w_other], sc, rc)
        for d in dc:
            d.start()
        for d in db + dc:
            d.wait()

    hbm = pl.BlockSpec(memory_space=pl.ANY)
    vmem = pl.BlockSpec(memory_space=pltpu.VMEM)
    sems = pltpu.SemaphoreType.DMA
    return pl.pallas_call(
        body, name="front",
        out_shape=[jax.ShapeDtypeStruct((8,) + first.shape, F32), jax.ShapeDtypeStruct((4, rows, hw), BF16),
                   jax.ShapeDtypeStruct((4, 8, n), F32), jax.ShapeDtypeStruct((4, rows, hw), BF16)],
        in_specs=[hbm, hbm, vmem, vmem], out_specs=[hbm] * 4,
        scratch_shapes=[pltpu.VMEM((8,) + first.shape, F32), pltpu.VMEM((8, n), F32),
                        sems((phase_a.n,)), sems((phase_a.n,)), sems((4,)), sems((4,)), sems((1,)), sems((1,)), sems],
        compiler_params=pltpu.CompilerParams(vmem_limit_bytes=VMEM_LIMIT),
    )(first[None], w_in_t, w_loc, b_loc)


def _ada_bwd_adamw(c_all_t, dmod_loc, w, m, v, carry=None):
    n = w.shape[1]
    tn = 512

    def body(ct_ref, dm_ref, w_ref, m_ref, v_ref, g_ref, d_ref, mo_ref, vo_ref):
        ct = ct_ref[...]
        cond = ct * _sig(ct)
        dm = dm_ref[...]
        g = cond[:, 0:1] * dm[0:1, :]
        for b in range(1, 8):
            g = g + cond[:, b:b + 1] * dm[b:b + 1, :]
        g_ref[...] = g
        d_ref[...], mo_ref[...], vo_ref[...] = _adamw(w_ref[...], g, m_ref[...], v_ref[...])

    wspec = pl.BlockSpec((D, tn), lambda j: (0, j))
    return _pcall(
        body, [c_all_t, dmod_loc, w, m, v], name="ada_bwd_adamw", grid=(n // tn,),
        out_shape=[jax.ShapeDtypeStruct((D, n), F32)] * 4,
        in_specs=[_row((D, 8)), pl.BlockSpec((8, tn), lambda j: (0, j)), wspec, wspec, wspec],
        out_specs=[wspec] * 4, carry=carry)


def _in_proj_fwd(x, a1, sh1, w_in, carry=None):
    s = x.shape[0]
    tm = 512

    def body(x_ref, a_ref, s_ref, w_ref, q_ref, kv_ref, z_ref, xbc_ref, dt_ref):
        def norm(rows):
            xv = x_ref[rows, :]
            r = lax.rsqrt(jnp.mean(xv * xv, axis=-1, keepdims=True) + EPS)
            return (xv * r * a_ref[...] + s_ref[...]).astype(BF16)

        def project(rows, h):
            p = _mm_nt(h, w_ref[...])
            q_ref[rows, :] = p[:, 0:512].astype(BF16)
            kv_ref[rows, :] = p[:, 512:768].astype(BF16)
            z_ref[rows, :] = p[:, 768:1280]
            xbc_ref[rows, :] = p[:, 1280:2304]
            dt_ref[rows, :] = p[:, 2304:2432]

        r0, r1 = slice(0, tm // 2), slice(tm // 2, tm)
        h0 = norm(r0)
        project(r0, h0)
        project(r1, norm(r1))

    def tok(w):
        return pl.BlockSpec((tm, w), lambda i: (i, 0))

    return _pcall(
        body, [x, a1, sh1, w_in], name="in_proj_fwd", grid=(s // tm,),
        out_shape=[jax.ShapeDtypeStruct((s, QW), BF16), jax.ShapeDtypeStruct((s, 2 * KVW), BF16),
                   jax.ShapeDtypeStruct((s, SW), F32), jax.ShapeDtypeStruct((s, XBCW), F32),
                   jax.ShapeDtypeStruct((s, 128), F32)],
        in_specs=[tok(D), _row((1, D)), _row((1, D)), _row((PROJ_W, D))],
        out_specs=[tok(QW), tok(2 * KVW), tok(SW), tok(XBCW), tok(128)], carry=carry)


def _in_proj_bwd(x, dx1, a1, sh1, w_in, dq, dkv, dz, dxbc, ddt, carry=None):
    s = x.shape[0]
    tm = 512

    def body(x_ref, dx1_ref, a_ref, s_ref, w_ref, dq_ref, dkv_ref, dz_ref, dxbc_ref, ddt_ref,
             gx_ref, h_ref, dsh_ref, p_ref):
        i = pl.program_id(0)

        @pl.when(i == 0)
        def _():
            dsh_ref[...] = jnp.zeros_like(dsh_ref)
            p_ref[...] = jnp.zeros_like(p_ref)

        def gather(st):
            rows = st["rows"]
            st["dproj"] = jnp.concatenate([dq_ref[rows, :], dkv_ref[rows, :], dz_ref[rows, :], dxbc_ref[rows, :],
                                           ddt_ref[rows, :]], axis=1)

        def back(st):
            st["dh"] = _mm(st.pop("dproj"), w_ref[...])

        def norm(st):
            rows, dh = st["rows"], st.pop("dh")
            xv = x_ref[rows, :]
            r = lax.rsqrt(jnp.mean(xv * xv, axis=-1, keepdims=True) + EPS)
            xn = xv * r
            a = a_ref[...]
            h_ref[rows, :] = (xn * a + s_ref[...]).astype(BF16)
            st["dsh"] = jnp.sum(dh, axis=0, keepdims=True)
            st["p"] = jnp.sum(dh * xn, axis=0, keepdims=True)
            u = dh * a
            gx_ref[rows, :] = dx1_ref[rows, :] + r * u - xn * (r * jnp.mean(u * xn, axis=-1, keepdims=True))

        g0, g1 = [dict(rows=slice(k * (tm // 2), (k + 1) * (tm // 2))) for k in range(2)]
        for stage, st in [(gather, g0), (back, g0), (gather, g1), (norm, g0), (back, g1), (norm, g1)]:
            stage(st)
        dsh_ref[0:1, :] += g0["dsh"] + g1["dsh"]
        p_ref[0:1, :] += g0["p"] + g1["p"]

    def tok(w):
        return pl.BlockSpec((tm, w), lambda i: (i, 0))

    return _pcall(
        body, [x, dx1, a1, sh1, w_in, dq, dkv, dz, dxbc, ddt], name="in_proj_bwd", grid=(s // tm,),
        out_shape=[jax.ShapeDtypeStruct((s, D), F32), jax.ShapeDtypeStruct((s, D), BF16),
                   jax.ShapeDtypeStruct((8, D), F32), jax.ShapeDtypeStruct((8, D), F32)],
        in_specs=[tok(D), tok(D), _row((1, D)), _row((1, D)), _row((PROJ_W, D)),
                  tok(QW), tok(2 * KVW), tok(SW), tok(XBCW), tok(128)],
        out_specs=[tok(D), tok(D), _row((8, D)), _row((8, D))], carry=carry)


def _attn_geometry():
    dist = np.arange(BLK)[:, None] + BLK - np.arange(2 * BLK)[None, :]
    n = np.maximum(dist, 0)
    max_exact = NBUCKET // 2
    large = max_exact + (np.log(np.maximum(n, 1) / max_exact) / np.log(128 / max_exact)
                         * (NBUCKET - max_exact)).astype(np.int32)
    large = np.minimum(large, NBUCKET - 1)
    bucket = np.where(n < max_exact, n, large).astype(np.int32)
    mask = (dist >= 0) & (dist < 128)
    return bucket, mask


def _attn_heads(is_first, q_blk, kvw, bias_ref, sinks_ref):
    qv = q_blk * 0.125
    col = lax.broadcasted_iota(jnp.int32, (BLK, 2 * BLK), 1)
    first = jnp.where(jnp.logical_and(is_first, col < BLK), NEG, 0.0)
    groups = []
    for g in range(NKV):
        qs = jnp.concatenate([qv[:, (4 * g + r) * HD:(4 * g + r + 1) * HD] for r in range(4)], axis=0)
        kw = kvw[:, g * HD:(g + 1) * HD]
        vw = kvw[:, KVW + g * HD:KVW + (g + 1) * HD]
        sc = _mm_nt(qs, kw)
        pn, ps = [], []
        for r in range(4):
            h = 4 * g + r
            sr = sc[r * BLK:(r + 1) * BLK] + bias_ref[h] + first
            sink = sinks_ref[h]
            m = jnp.maximum(jnp.max(sr, axis=-1, keepdims=True), sink)
            p = jnp.exp(sr - m)
            es = jnp.exp(sink - m)
            inv = 1.0 / (jnp.sum(p, axis=-1, keepdims=True) + es)
            pn.append(p * inv)
            ps.append(es * inv)
        pn = jnp.concatenate(pn, axis=0)
        ps = jnp.concatenate(ps, axis=0)
        o = _mm(pn.astype(BF16), vw)
        groups.append((qs, kw, vw, pn, ps, o))
    return groups


def _unstack_heads(parts):
    return jnp.concatenate([p[r * BLK:(r + 1) * BLK] for p in parts for r in range(4)], axis=1)


NB = 2


def _attn_fwd(q, kv, bias, sinks, nw, carry=None):
    s = q.shape[0]

    def body(q_ref, kvp_ref, kvc_ref, bias_ref, sinks_ref, nw_ref, y_ref):
        t = pl.program_id(0)
        kv3 = jnp.concatenate([kvp_ref[...], kvc_ref[...]], axis=0)
        for sub in range(NB):
            rows = slice(sub * BLK, (sub + 1) * BLK)
            groups = _attn_heads(jnp.logical_and(t == 0, sub == 0), q_ref[rows, :], kv3[sub * BLK:(sub + 2) * BLK],
                                 bias_ref, sinks_ref)
            o = _unstack_heads([g[5] for g in groups])
            r = lax.rsqrt(jnp.mean(o * o, axis=-1, keepdims=True) + EPS)
            y_ref[rows, :] = (o * r * nw_ref[...]).astype(BF16)

    return _pcall(
        body, [q, kv, kv, bias, sinks, nw], name="attn_fwd", grid=(s // (NB * BLK),),
        out_shape=[jax.ShapeDtypeStruct((s, QW), BF16)],
        in_specs=[pl.BlockSpec((NB * BLK, QW), lambda t: (t, 0)),
                  pl.BlockSpec((BLK, 2 * KVW), lambda t: (jnp.maximum(NB * t - 1, 0), 0)),
                  pl.BlockSpec((NB * BLK, 2 * KVW), lambda t: (t, 0)),
                  _row((NH, BLK, 2 * BLK)),
                  pl.BlockSpec(memory_space=pltpu.SMEM),
                  _row((1, QW))],
        out_specs=[pl.BlockSpec((NB * BLK, QW), lambda t: (t, 0))], carry=carry)


def _attn_bwd(q, kv, dya, bias, sinks, nw, act, dx2, gate2, w_dn, carry=None):
    s = q.shape[0]
    nt = s // (NB * BLK)
    npiece = DFF // NB

    def body(q_ref, kvp_ref, kvc_ref, dy_ref, bias_ref, sinks_ref, nw_ref, act_ref, dx2_ref, g2_ref, wdn_ref,
             dq_ref, dkv_ref, dbias_ref, dsink_ref, dnw_ref, gdn_hbm, dg2_ref, carry_ref, held_ref, acc_ref, sem):
        t = pl.program_id(0)

        @pl.when(t == 0)
        def _():
            carry_ref[...] = jnp.zeros_like(carry_ref)
            held_ref[...] = jnp.zeros_like(held_ref)
            dbias_ref[...] = jnp.zeros_like(dbias_ref)
            dsink_ref[...] = jnp.zeros_like(dsink_ref)
            dnw_ref[...] = jnp.zeros_like(dnw_ref)
            acc_ref[...] = jnp.zeros_like(acc_ref)

        def wgrad_piece(sub):
            rows = slice(sub * npiece, (sub + 1) * npiece)
            acc_ref[rows, :] += _mm_tn(act_ref[:, rows], dx2_ref[...].astype(BF16))

        def block(sub, kv3):
            rows = slice(sub * BLK, (sub + 1) * BLK)
            groups = _attn_heads(jnp.logical_and(t == 0, sub == 0), q_ref[rows, :], kv3[sub * BLK:(sub + 2) * BLK],
                                 bias_ref, sinks_ref)
            o = _unstack_heads([g[5] for g in groups])
            r = lax.rsqrt(jnp.mean(o * o, axis=-1, keepdims=True) + EPS)
            dy = dy_ref[rows, :]
            on = o * r
            dnw_ref[0:1, :] += jnp.sum(dy * on, axis=0, keepdims=True)
            u = dy * nw_ref[...]
            do = r * u - on * (r * jnp.mean(u * on, axis=-1, keepdims=True))
            dq_parts, dk_parts, dv_parts = [], [], []
            for g, (qs, kw, vw, pn, ps, og) in enumerate(groups):
                dos = jnp.concatenate([do[:, (4 * g + r_) * HD:(4 * g + r_ + 1) * HD] for r_ in range(4)], axis=0)
                delta = jnp.sum(dos * og, axis=-1, keepdims=True)
                dp = _mm_nt(dos.astype(BF16), vw)
                ds = pn * (dp - delta)
                dsk = ps * delta
                lane = lax.broadcasted_iota(jnp.int32, (1, 128), 1)
                for r_ in range(4):
                    h = 4 * g + r_
                    dbias_ref[h] += ds[r_ * BLK:(r_ + 1) * BLK]
                    dsink_ref[0:1, :] -= jnp.where(lane == h, jnp.sum(dsk[r_ * BLK:(r_ + 1) * BLK]), 0.0)
                dsb = ds.astype(BF16)
                dq_parts.append(_mm(dsb, kw) * 0.125)
                dk_parts.append(_mm_tn(dsb, qs))
                dv_parts.append(_mm_tn(pn.astype(BF16), dos.astype(BF16)))
            dq_ref[rows, :] = _unstack_heads(dq_parts).astype(BF16)
            return jnp.concatenate(dk_parts + dv_parts, axis=1)

        @pl.when(t < nt)
        def _():
            kv3 = jnp.concatenate([kvp_ref[...], kvc_ref[...]], axis=0)
            tail = carry_ref[...]
            for sub in range(NB):
                d = block(sub, kv3)
                done = tail + d[0:BLK]
                if sub == 0:
                    dkv_ref[0:(NB - 1) * BLK, :] = held_ref[...].astype(BF16)
                    dkv_ref[(NB - 1) * BLK:NB * BLK, :] = done.astype(BF16)
                else:
                    held_ref[(sub - 1) * BLK:sub * BLK, :] = done
                tail = d[BLK:2 * BLK]
                wgrad_piece(sub)
            carry_ref[...] = tail

        @pl.when(t == nt)
        def _():
            dkv_ref[0:(NB - 1) * BLK, :] = held_ref[...].astype(BF16)
            dkv_ref[(NB - 1) * BLK:NB * BLK, :] = carry_ref[...].astype(BF16)
            acc = acc_ref[...]
            dg2_ref[...] = jnp.zeros_like(dg2_ref)
            dg2_ref[0:1, :] = jnp.sum(acc * wdn_ref[...].astype(F32), axis=0, keepdims=True)
            acc_ref[...] = acc * g2_ref[...]
            cp = pltpu.make_async_copy(acc_ref, gdn_hbm, sem)
            cp.start()
            cp.wait()

    last = nt - 1
    tile = lambda w: pl.BlockSpec((NB * BLK, w), lambda t: (jnp.minimum(t, last), 0))
    return _pcall(
        body, [q, kv, kv, dya, bias, sinks, nw, act, dx2, gate2, w_dn], name="attn_bwd", grid=(nt + 1,),
        out_shape=[jax.ShapeDtypeStruct((s, QW), BF16), jax.ShapeDtypeStruct((s, 2 * KVW), BF16),
                   jax.ShapeDtypeStruct((NH, BLK, 2 * BLK), F32), jax.ShapeDtypeStruct((NH, 128), F32),
                   jax.ShapeDtypeStruct((8, QW), F32), jax.ShapeDtypeStruct((DFF, D), F32),
                   jax.ShapeDtypeStruct((8, D), F32)],
        in_specs=[tile(QW),
                  pl.BlockSpec((BLK, 2 * KVW), lambda t: (jnp.clip(NB * t - 1, 0, NB * nt - 1), 0)),
                  tile(2 * KVW), tile(QW),
                  _row((NH, BLK, 2 * BLK)),
                  pl.BlockSpec(memory_space=pltpu.SMEM),
                  _row((1, QW)), tile(DFF), tile(D), _row((1, D)), _row((DFF, D))],
        out_specs=[tile(QW),
                   pl.BlockSpec((NB * BLK, 2 * KVW), lambda t: (jnp.maximum(t - 1, 0), 0)),
                   _row((NH, BLK, 2 * BLK)), _row((NH, 128)), _row((8, QW)),
                   pl.BlockSpec(memory_space=pl.ANY), _row((8, D))],
        scratch_shapes=[pltpu.VMEM((BLK, 2 * KVW), F32), pltpu.VMEM(((NB - 1) * BLK, 2 * KVW), F32),
                        pltpu.VMEM((DFF, D), F32), pltpu.SemaphoreType.DMA], carry=carry)


def _rel_bias_grad(dbias, bucket):
    def body(db_ref, bk_ref, o_ref):
        bk = bk_ref[...]
        lane = lax.broadcasted_iota(jnp.int32, (1, 128), 1)
        for b in range(NBUCKET):
            sel = bk == b
            row = jnp.zeros((1, 128), F32)
            for h in range(NH):
                row = row + jnp.where(lane == h, jnp.sum(jnp.where(sel, db_ref[h], 0.0)), 0.0)
            o_ref[b:b + 1, :] = row

    return pl.pallas_call(
        body, name="rel_bias_grad",
        out_shape=jax.ShapeDtypeStruct((NBUCKET, 128), F32),
    )(dbias, bucket)


def _ssd_consts():
    head_of_lane = np.arange(SW) // HD
    expand = (np.arange(128)[:, None] == head_of_lane[None, :]).astype(np.float32)
    tril = np.tril(np.ones((BLK, BLK), np.float32))
    return (jnp.asarray(expand, BF16), jnp.asarray(expand.T.copy(), BF16), jnp.asarray(tril, BF16),
            jnp.asarray(tril.T.copy(), BF16))


def _conv_pre(xc, halo, cw, cb):
    ext = jnp.concatenate([halo, xc], axis=0)
    taps = [xc if k == CK - 1 else pltpu.roll(ext, CK - 1 - k, 0)[8:8 + BLK] for k in range(CK)]
    return cb + sum(cw[k:k + 1, :] * taps[k] for k in range(CK))


def _ssd_chunk(pre, dtr, dtb, av, dkv, ex, tril, h_in):
    sp = _sig(pre)
    xbc = pre * sp
    xs, bm, cm = xbc[:, 0:SW], xbc[:, SW:SW + 2 * NST], xbc[:, SW + 2 * NST:]
    dtin = dtr + dtb
    dt = jnp.maximum(dtin, 0.0) + jnp.log1p(jnp.exp(-jnp.abs(dtin)))
    cs = _sel_l(tril, dt * av)
    cst = cs.T
    dtx = _sel_r(dt, ex)
    csx = _sel_r(cs, ex)
    xdt = xs * dtx
    csl = csx[BLK - 1:BLK, :]
    decx = jnp.exp(csl - csx)
    ecsx = jnp.exp(csx)
    ecl = jnp.exp(csl)
    causal = tril.astype(F32) > 0.5
    ydiag, yoff, cbs, lms = [], [], [], []
    for g in range(2):
        bg = bm[:, g * NST:(g + 1) * NST].astype(BF16)
        cg = cm[:, g * NST:(g + 1) * NST].astype(BF16)
        cb = _mm_nt(cg, bg)
        cbs.append(cb)
        yoff.append(_mm(cg, h_in[:, g * 256:(g + 1) * 256].astype(BF16)))
        for r in range(4):
            h = 4 * g + r
            seg = cs[:, h:h + 1] - cst[h:h + 1, :]
            lm = jnp.where(causal, jnp.exp(jnp.minimum(seg, 0.0)), 0.0)
            lms.append(lm)
            ydiag.append(_mm((cb * lm).astype(BF16), xdt[:, h * HD:(h + 1) * HD].astype(BF16)))
    yoff = jnp.concatenate(yoff, axis=1) * ecsx
    y = jnp.concatenate(ydiag, axis=1) + yoff + dkv * xs
    return dict(pre=pre, sp=sp, xs=xs, bm=bm, cm=cm, dtin=dtin, dt=dt, av=av, cs=cs, cst=cst,
                dtx=dtx, csx=csx, xdt=xdt, decx=decx, ecsx=ecsx, ecl=ecl, causal=causal, cbs=cbs, lms=lms,
                yoff=yoff, y=y)


def _group_mean(t):
    m0 = jnp.mean(t[:, 0:256], axis=-1, keepdims=True)
    m1 = jnp.mean(t[:, 256:512], axis=-1, keepdims=True)
    return jnp.concatenate([jnp.broadcast_to(m0, (t.shape[0], 256)), jnp.broadcast_to(m1, (t.shape[0], 256))], axis=1)


SUBS = 4


def _ssd_fwd(z, xbc, dtr, cw, cb, dtb, av, dk, nw, carry=None):
    s = z.shape[0]
    nc = s // BLK
    tile = SUBS * BLK
    ex, _, tril, _ = _ssd_consts()

    def body(z_ref, xc_ref, xh_ref, dtr_ref, cw_ref, cb_ref, dtb_ref, a_ref, dk_ref, nw_ref, ex_ref, tril_ref,
             y_ref, hs_ref, pre_ref, h_ref):
        t = pl.program_id(0)

        @pl.when(t == 0)
        def _():
            h_ref[...] = jnp.zeros_like(h_ref)

        h_in = h_ref[...]
        for sub in range(SUBS):
            rows = slice(sub * BLK, (sub + 1) * BLK)
            xc = xc_ref[rows, :]
            halo = jnp.where(t == 0, 0.0, xh_ref[...]) if sub == 0 else xc_ref[sub * BLK - 8:sub * BLK, :]
            pre = _conv_pre(xc, halo, cw_ref[...], cb_ref[...])
            pre_ref[rows, :] = pre
            hs_ref[sub] = h_in
            f = _ssd_chunk(pre, dtr_ref[rows, :], dtb_ref[...], a_ref[...], dk_ref[...], ex_ref[...], tril_ref[...], h_in)
            dx = (f["decx"] * f["xdt"]).astype(BF16)
            st = [_mm_tn(f["bm"][:, g * NST:(g + 1) * NST].astype(BF16), dx[:, g * 256:(g + 1) * 256]) for g in range(2)]
            h_in = h_in * f["ecl"] + jnp.concatenate(st, axis=1)
            zv = z_ref[rows, :]
            tg = f["y"] * (zv * _sig(zv))
            r = lax.rsqrt(_group_mean(tg * tg) + EPS)
            y_ref[rows, :] = (tg * r * nw_ref[...]).astype(BF16)
        h_ref[...] = h_in

    cur = lambda w: pl.BlockSpec((tile, w), lambda t: (t, 0))
    return _pcall(
        body, [z, xbc, xbc, dtr, cw, cb, dtb, av, dk, nw, ex, tril], name="ssd_fwd", grid=(s // tile,),
        out_shape=[jax.ShapeDtypeStruct((s, SW), BF16), jax.ShapeDtypeStruct((nc, NST, SW), F32),
                   jax.ShapeDtypeStruct((s, XBCW), F32)],
        in_specs=[cur(SW), cur(XBCW), pl.BlockSpec((8, XBCW), lambda t: (jnp.maximum(t * (tile // 8) - 1, 0), 0)),
                  cur(128), _row((8, XBCW)), _row((1, XBCW)), _row((1, 128)),
                  _row((1, 128)), _row((1, SW)), _row((1, SW)), _row((128, SW)), _row((BLK, BLK))],
        out_specs=[cur(SW), pl.BlockSpec((SUBS, NST, SW), lambda t: (t, 0, 0)), cur(XBCW)],
        scratch_shapes=[pltpu.VMEM((NST, SW), F32)], carry=carry)


def _ssd_bwd(z, xbc, pre_all, dtr, dys, hs, cw, dtb, av, dk, nw, carry=None):
    s = z.shape[0]
    tile = SUBS * BLK
    nt = s // tile
    ex, ext_t, tril, triu = _ssd_consts()

    def body(z_ref, xc_ref, pre_ref, dtr_ref, dy_ref, hs_ref, cw_ref, dtb_ref, a_ref, dk_ref, nw_ref,
             ex_ref, ext_ref, tril_ref, triu_ref,
             dz_ref, dxbc_ref, ddt_ref, dcw_ref, dcb_ref, dnw_ref, dhd_ref, dh_ref, nxt_ref, dd_ref):
        i = pl.program_id(0)

        @pl.when(i == 0)
        def _():
            dh_ref[...] = jnp.zeros_like(dh_ref)
            nxt_ref[...] = jnp.zeros_like(nxt_ref)
            dd_ref[...] = jnp.zeros_like(dd_ref)
            dcw_ref[...] = jnp.zeros_like(dcw_ref)
            dcb_ref[...] = jnp.zeros_like(dcb_ref)
            dnw_ref[...] = jnp.zeros_like(dnw_ref)
            dhd_ref[...] = jnp.zeros_like(dhd_ref)

        gst, nxt = dh_ref[...], nxt_ref[...]
        for sub in reversed(range(SUBS)):
            rows = slice(sub * BLK, (sub + 1) * BLK)
            gst, nxt = chunk(sub, rows, gst, nxt, z_ref, xc_ref, pre_ref, dtr_ref, dy_ref, hs_ref, cw_ref, dtb_ref,
                             a_ref, dk_ref, nw_ref, ex_ref, ext_ref, tril_ref, triu_ref,
                             dz_ref, dxbc_ref, ddt_ref, dcw_ref, dcb_ref, dnw_ref, dhd_ref, dd_ref)
        dh_ref[...] = gst
        nxt_ref[...] = nxt

        @pl.when(i == nt - 1)
        def _():
            dhd_ref[2:3, :] = _sel_r(dd_ref[...], ext_ref[...])[0:1, :]

    def chunk(sub, rows, gst, nxt, z_ref, xc_ref, pre_ref, dtr_ref, dy_ref, hs_ref, cw_ref, dtb_ref,
              a_ref, dk_ref, nw_ref, ex_ref, ext_ref, tril_ref, triu_ref,
              dz_ref, dxbc_ref, ddt_ref, dcw_ref, dcb_ref, dnw_ref, dhd_ref, dd_ref):
        h_in = hs_ref[sub]
        f = _ssd_chunk(pre_ref[rows, :], dtr_ref[rows, :], dtb_ref[...], a_ref[...], dk_ref[...], ex_ref[...],
                       tril_ref[...], h_in)
        xs, xdt, decx, ecsx, ecl, dtx = f["xs"], f["xdt"], f["decx"], f["ecsx"], f["ecl"], f["dtx"]
        cs, cst, causal = f["cs"], f["cst"], f["causal"]
        causal_t = triu_ref[...].astype(F32) > 0.5

        zv = z_ref[rows, :]
        sz = _sig(zv)
        gz = zv * sz
        t = f["y"] * gz
        r = lax.rsqrt(_group_mean(t * t) + EPS)
        tn_ = t * r
        dyn = dy_ref[rows, :]
        dnw_ref[0:1, :] += jnp.sum(dyn * tn_, axis=0, keepdims=True)
        u = dyn * nw_ref[...]
        dt_ = r * u - tn_ * (r * _group_mean(u * tn_))
        dy = dt_ * gz
        dz_ref[rows, :] = (dt_ * f["y"] * (sz * (1.0 + zv * (1.0 - sz)))).astype(BF16)

        dd_ref[0:1, :] += jnp.sum(dy * xs, axis=0, keepdims=True)
        dxs = dk_ref[...] * dy

        edy = ecsx * dy
        dxdt, dbs, dcs_, dcsx_parts, dh_new = [], [], [], [], []
        lane = lax.broadcasted_iota(jnp.int32, (1, 128), 1)
        dcs_intra = jnp.zeros((BLK, 128), F32)
        for g in range(2):
            sl = slice(g * 256, (g + 1) * 256)
            bgf, cgf = f["bm"][:, g * NST:(g + 1) * NST], f["cm"][:, g * NST:(g + 1) * NST]
            bg, cg = bgf.astype(BF16), cgf.astype(BF16)
            gg = gst[:, sl].astype(BF16)
            hg = h_in[:, sl].astype(BF16)
            edyg = edy[:, sl].astype(BF16)
            dc = _mm_nt(edyg, hg)
            dh_new.append(gst[:, sl] * ecl[:, sl] + _mm_tn(cg, edyg))
            bgm = _mm(bg, gg)
            dxdt_g = decx[:, sl] * bgm
            dxg = (decx[:, sl] * xdt[:, sl]).astype(BF16)
            db = _mm_nt(dxg, gg)
            qd = bgm * xdt[:, sl] * decx[:, sl]
            last = jnp.sum(qd, axis=0, keepdims=True) + ecl[:, sl] * jnp.sum(gst[:, sl] * h_in[:, sl], axis=0, keepdims=True)
            rowid = lax.broadcasted_iota(jnp.int32, (BLK, 256), 0)
            dcsx_parts.append(f["yoff"][:, sl] * dy[:, sl] - qd + jnp.where(rowid == BLK - 1, last, 0.0))
            cb_ = f["cbs"][g]
            cbt = _mm_nt(bg, cg)
            dcb_ = jnp.zeros((BLK, BLK), F32)
            dcbt = jnp.zeros((BLK, BLK), F32)
            dxd = []
            for r_ in range(4):
                h = 4 * g + r_
                hl = slice(h * HD, (h + 1) * HD)
                lm = f["lms"][h]
                segt = cst[h:h + 1, :] - cs[:, h:h + 1]
                lmt = jnp.where(causal_t, jnp.exp(jnp.minimum(segt, 0.0)), 0.0)
                dyh = dy[:, hl].astype(BF16)
                xdh = xdt[:, hl].astype(BF16)
                dw = _mm_nt(dyh, xdh)
                dwt = _mm_nt(xdh, dyh)
                wt = cbt * lmt
                dxd.append(_mm(wt.astype(BF16), dyh))
                dcb_ = dcb_ + dw * lm
                dcbt = dcbt + dwt * lmt
                col = jnp.sum(dw * (cb_ * lm), axis=-1, keepdims=True) - jnp.sum(dwt * wt, axis=-1, keepdims=True)
                dcs_intra = dcs_intra + jnp.where(lane == h, col, 0.0)
            dxdt.append(dxdt_g + jnp.concatenate(dxd, axis=1))
            dcs_.append(dc + _mm(dcb_.astype(BF16), bg))
            dbs.append(db + _mm(dcbt.astype(BF16), cg))
        dxdt = jnp.concatenate(dxdt, axis=1)
        dxs = dxs + dxdt * dtx
        ext_t_ = ext_ref[...]
        dcs = dcs_intra + _sel_r(jnp.concatenate(dcsx_parts, axis=1), ext_t_)
        da = _sel_l(triu_ref[...], dcs)
        ddt = da * f["av"] + _sel_r(dxdt * xs, ext_t_)
        dhd_ref[1:2, :] += jnp.sum(da * f["dt"], axis=0, keepdims=True)
        ddtr = ddt * _sig(f["dtin"])
        dhd_ref[0:1, :] += jnp.sum(ddtr, axis=0, keepdims=True)
        ddt_ref[rows, :] = ddtr.astype(BF16)

        sp, pre = f["sp"], f["pre"]
        dact = jnp.concatenate([dxs] + dbs + dcs_, axis=1)
        dpre = dact * (sp * (1.0 + pre * (1.0 - sp)))
        dcb_ref[0:1, :] += jnp.sum(dpre, axis=0, keepdims=True)
        ext2 = jnp.concatenate([dpre, nxt], axis=0)
        shifted = [pltpu.roll(ext2, BLK + 8 - (CK - 1 - k), 0)[0:BLK] for k in range(CK - 1)] + [dpre]
        cw = cw_ref[...]
        xc = xc_ref[rows, :]
        dxr = cw[CK - 1:CK, :] * dpre
        for k in range(CK):
            dcw_ref[k:k + 1, :] += jnp.sum(shifted[k] * xc, axis=0, keepdims=True)
            if k < CK - 1:
                dxr = dxr + cw[k:k + 1, :] * shifted[k]
        dxbc_ref[rows, :] = dxr.astype(BF16)
        return jnp.concatenate(dh_new, axis=1), dpre[0:8]

    cur = lambda w: pl.BlockSpec((tile, w), lambda i: (nt - 1 - i, 0))
    return _pcall(
        body, [z, xbc, pre_all, dtr, dys, hs, cw, dtb, av, dk, nw, ex, ext_t, tril, triu], name="ssd_bwd", grid=(nt,),
        out_shape=[jax.ShapeDtypeStruct((s, SW), BF16), jax.ShapeDtypeStruct((s, XBCW), BF16),
                   jax.ShapeDtypeStruct((s, 128), BF16), jax.ShapeDtypeStruct((8, XBCW), F32),
                   jax.ShapeDtypeStruct((8, XBCW), F32), jax.ShapeDtypeStruct((8, SW), F32),
                   jax.ShapeDtypeStruct((8, 128), F32)],
        in_specs=[cur(SW), cur(XBCW), cur(XBCW), cur(128), cur(SW),
                  pl.BlockSpec((SUBS, NST, SW), lambda i: (nt - 1 - i, 0, 0)),
                  _row((8, XBCW)), _row((1, 128)), _row((1, 128)), _row((1, SW)), _row((1, SW)),
                  _row((128, SW)), _row((SW, 128)), _row((BLK, BLK)), _row((BLK, BLK))],
        out_specs=[cur(SW), cur(XBCW), cur(128), _row((8, XBCW)), _row((8, XBCW)), _row((8, SW)), _row((8, 128))],
        scratch_shapes=[pltpu.VMEM((NST, SW), F32), pltpu.VMEM((8, XBCW), F32), pltpu.VMEM((8, SW), F32)], carry=carry)


def _load_once(i, pairs, sem):
    @pl.when(i == 0)
    def _():
        cps = [pltpu.make_async_copy(src, dst, sem.at[k]) for k, (src, dst) in enumerate(pairs)]
        for cp in cps:
            cp.start()
        for cp in cps:
            cp.wait()


def _mlp_fwd(x, ya, ys, tgt, w_o, w_ga, w_gb, w_dn, gate1, a2, sh2, gate2, fn):
    s = x.shape[0]
    sub_m, subs = 256, 2
    tm = sub_m * subs

    def body(x_ref, ya_ref, ys_ref, t_ref, wo_hbm, wga_hbm, wgb_hbm, wdn_hbm, g1_ref, a2_ref, s2_ref, g2_ref, fn_ref,
             x1_ref, gu_ref, dx2_ref, loss_ref, dfn_ref, wo, wga, wgb, wdn, sem):
        i = pl.program_id(0)
        _load_once(i, [(wo_hbm, wo), (wga_hbm, wga), (wgb_hbm, wgb), (wdn_hbm, wdn)], sem)

        @pl.when(i == 0)
        def _():
            loss_ref[...] = jnp.zeros_like(loss_ref)
            dfn_ref[...] = jnp.zeros_like(dfn_ref)

        def proj(st):
            st["mix"] = _mm(ya_ref[st["rows"], :], wo[0:QW, :]) + _mm(ys_ref[st["rows"], :], wo[QW:D, :])

        def norm(st):
            x1 = x_ref[st["rows"], :] + g1_ref[...] * st.pop("mix")
            x1_ref[st["rows"], :] = x1
            r2 = lax.rsqrt(jnp.mean(x1 * x1, axis=-1, keepdims=True) + EPS)
            st["x1"] = x1
            st["h2"] = (x1 * r2 * a2_ref[...] + s2_ref[...]).astype(BF16)

        def gate_up(st):
            h2 = st.pop("h2")
            ha, hb = h2[:, 0:D // 2], h2[:, D // 2:D]
            gub = jnp.concatenate([(_mm(ha, wga[j]) + _mm(hb, wgb[j])).astype(BF16) for j in range(4)], axis=1)
            gu_ref[st["rows"], :] = gub
            st["gub"] = gub

        def activate(st):
            gub = st.pop("gub")
            gv, uv = gub[:, 0:DFF].astype(F32), gub[:, DFF:].astype(F32)
            st["act"] = (gv * _sig(gv) * uv).astype(BF16)

        def down(st):
            st["ff"] = _mm(st.pop("act"), wdn[...])

        def head(st):
            x2 = st.pop("x1") + g2_ref[...] * st.pop("ff")
            r3 = lax.rsqrt(jnp.mean(x2 * x2, axis=-1, keepdims=True) + EPS)
            xn = x2 * r3
            fnv = fn_ref[...]
            err = xn * fnv - t_ref[st["rows"], :]
            st["loss"] = jnp.sum(err * err) * (0.5 / D)
            dy = err * (1.0 / D)
            st["dfn"] = jnp.sum(dy * xn, axis=0, keepdims=True)
            u = dy * fnv
            dx2_ref[st["rows"], :] = r3 * u - xn * (r3 * jnp.mean(u * xn, axis=-1, keepdims=True))

        a, b = [dict(rows=slice(k * sub_m, (k + 1) * sub_m)) for k in range(subs)]
        for stage, st in [(proj, a), (norm, a), (proj, b), (gate_up, a), (norm, b), (activate, a), (gate_up, b),
                          (down, a), (activate, b), (head, a), (down, b), (head, b)]:
            stage(st)
        loss_ref[...] += a["loss"] + b["loss"]
        dfn_ref[0:1, :] += a["dfn"] + b["dfn"]

    def tok(w):
        return pl.BlockSpec((tm, w), lambda i: (i, 0))

    hbm = pl.BlockSpec(memory_space=pl.ANY)
    return pl.pallas_call(
        body, name="mlp_fwd", grid=(s // tm,),
        out_shape=[jax.ShapeDtypeStruct((s, D), F32), jax.ShapeDtypeStruct((s, 2 * DFF), BF16),
                   jax.ShapeDtypeStruct((s, D), F32), jax.ShapeDtypeStruct((8, 128), F32),
                   jax.ShapeDtypeStruct((8, D), F32)],
        in_specs=[tok(D), tok(QW), tok(SW), tok(D), hbm, hbm, hbm, hbm,
                  _row((1, D)), _row((1, D)), _row((1, D)), _row((1, D)), _row((1, D))],
        out_specs=[tok(D), tok(2 * DFF), tok(D), _row((8, 128)), _row((8, D))],
        scratch_shapes=[pltpu.VMEM((D, D), BF16), pltpu.VMEM(w_ga.shape, BF16), pltpu.VMEM(w_gb.shape, BF16),
                        pltpu.VMEM((DFF, D), BF16), pltpu.SemaphoreType.DMA((4,))],
        compiler_params=_cp(("arbitrary",)),
    )(x, ya, ys, tgt, w_o, w_ga, w_gb, w_dn, gate1, a2, sh2, gate2, fn)


def _mlp_bwd(x1, gu, dx2, w_o, w_ga, w_gb, w_dn, gate1, a2, sh2, gate2):
    s = x1.shape[0]
    tm = 256
    nj = 2 * DFF // 4

    def body(x1_ref, gu_ref, dx2_ref, wo_hbm, wga_hbm, wgb_hbm, wdn_hbm, g1_ref, a2_ref, s2_ref, g2_ref,
             dx1_ref, dya_ref, dys_ref, act_ref, dgu_ref, h2_ref, dsh_ref, p_ref, wo, wga, wgb, wdn, sem):
        i = pl.program_id(0)
        _load_once(i, [(wo_hbm, wo), (wga_hbm, wga), (wgb_hbm, wgb), (wdn_hbm, wdn)], sem)

        @pl.when(i == 0)
        def _():
            dsh_ref[...] = jnp.zeros_like(dsh_ref)
            p_ref[...] = jnp.zeros_like(p_ref)

        dx2 = dx2_ref[...]
        dact = _mm_nt((dx2 * g2_ref[...]).astype(BF16), wdn[...])
        gub = gu_ref[...]
        gv, uv = gub[:, 0:DFF].astype(F32), gub[:, DFF:].astype(F32)
        sg = _sig(gv)
        sl = gv * sg
        act_ref[...] = (sl * uv).astype(BF16)
        dgu = jnp.concatenate([dact * uv * (sg * (1.0 + gv * (1.0 - sg))), dact * sl], axis=1).astype(BF16)
        dgu_ref[...] = dgu
        dha = sum(_mm_nt(dgu[:, j * nj:(j + 1) * nj], wga[j]) for j in range(4))
        dhb = sum(_mm_nt(dgu[:, j * nj:(j + 1) * nj], wgb[j]) for j in range(4))
        dh = jnp.concatenate([dha, dhb], axis=1)
        x1 = x1_ref[...]
        r2 = lax.rsqrt(jnp.mean(x1 * x1, axis=-1, keepdims=True) + EPS)
        xn = x1 * r2
        a2 = a2_ref[...]
        h2_ref[...] = (xn * a2 + s2_ref[...]).astype(BF16)
        dsh_ref[0:1, :] += jnp.sum(dh, axis=0, keepdims=True)
        p_ref[0:1, :] += jnp.sum(dh * xn, axis=0, keepdims=True)
        u = dh * a2
        dx1 = dx2 + r2 * u - xn * (r2 * jnp.mean(u * xn, axis=-1, keepdims=True))
        dx1_ref[...] = dx1
        dcat = _mm_nt((dx1 * g1_ref[...]).astype(BF16), wo[...])
        dya_ref[...] = dcat[:, 0:QW]
        dys_ref[...] = dcat[:, QW:D]

    def tok(w):
        return pl.BlockSpec((tm, w), lambda i: (i, 0))

    hbm = pl.BlockSpec(memory_space=pl.ANY)
    return pl.pallas_call(
        body, name="mlp_bwd", grid=(s // tm,),
        out_shape=[jax.ShapeDtypeStruct((s, D), F32), jax.ShapeDtypeStruct((s, QW), F32),
                   jax.ShapeDtypeStruct((s, SW), F32), jax.ShapeDtypeStruct((s, DFF), BF16),
                   jax.ShapeDtypeStruct((s, 2 * DFF), BF16), jax.ShapeDtypeStruct((s, D), BF16),
                   jax.ShapeDtypeStruct((8, D), F32), jax.ShapeDtypeStruct((8, D), F32)],
        in_specs=[tok(D), tok(2 * DFF), tok(D), hbm, hbm, hbm, hbm, _row((1, D)), _row((1, D)), _row((1, D)), _row((1, D))],
        out_specs=[tok(D), tok(QW), tok(SW), tok(DFF), tok(2 * DFF), tok(D), _row((8, D)), _row((8, D))],
        scratch_shapes=[pltpu.VMEM((D, D), BF16), pltpu.VMEM(w_ga.shape, BF16), pltpu.VMEM(w_gb.shape, BF16),
                        pltpu.VMEM((DFF, D), BF16), pltpu.SemaphoreType.DMA((4,))],
        compiler_params=_cp(("arbitrary",)),
    )(x1, gu, dx2, w_o, w_ga, w_gb, w_dn, gate1, a2, sh2, gate2)


def _wgrad(name, a, b, gate, w, carry=None):
    s, m = a.shape
    n = b.shape[1]
    tk = min(1024, s)
    nk = s // tk

    def body(a_ref, b_ref, g_ref, w_ref, o_hbm, dg_ref, acc_ref, sem):
        k = pl.program_id(0)

        @pl.when(k == 0)
        def _():
            acc_ref[...] = jnp.zeros_like(acc_ref)

        acc_ref[...] += _mm_tn(a_ref[...], b_ref[...].astype(BF16))

        @pl.when(k == nk - 1)
        def _():
            acc = acc_ref[...]
            dg_ref[...] = jnp.zeros_like(dg_ref)
            dg_ref[0:1, :] = jnp.sum(acc * w_ref[...].astype(F32), axis=0, keepdims=True)
            acc_ref[...] = acc * g_ref[...]
            cp = pltpu.make_async_copy(acc_ref, o_hbm, sem)
            cp.start()
            cp.wait()

    return _pcall(body, [a, b, gate, w], name=name, grid=(nk,),
                  out_shape=[jax.ShapeDtypeStruct((m, n), F32), jax.ShapeDtypeStruct((8, n), F32)],
                  in_specs=[pl.BlockSpec((tk, m), lambda k: (k, 0)), pl.BlockSpec((tk, n), lambda k: (k, 0)),
                            _row((1, n)), _row((m, n))],
                  out_specs=[pl.BlockSpec(memory_space=pl.ANY), _row((8, n))],
                  scratch_shapes=[pltpu.VMEM((m, n), F32), pltpu.SemaphoreType.DMA], carry=carry)


def _wgrad_gate_up(h2, dgu, carry=None):
    s = h2.shape[0]
    tk = min(1024, s)
    nk = s // tk
    n = dgu.shape[1]
    nj = n // 4

    def body(a_ref, b_ref, o_hbm, acc_ref, sems):
        k = pl.program_id(0)

        @pl.when(k == 0)
        def _():
            acc_ref[...] = jnp.zeros_like(acc_ref)

        acc_ref[...] += _mm_tn(a_ref[...], b_ref[...])

        @pl.when(k == nk - 1)
        def _():
            cps = [pltpu.make_async_copy(acc_ref.at[:, pl.ds(j * nj, nj)], o_hbm.at[j], sems.at[j]) for j in range(4)]
            for cp in cps:
                cp.start()
            for cp in cps:
                cp.wait()

    return _pcall(body, [h2, dgu], name="wgrad_gate_up", grid=(nk,),
                  out_shape=[jax.ShapeDtypeStruct((4, D, nj), F32)],
                  in_specs=[pl.BlockSpec((tk, D), lambda k: (k, 0)), pl.BlockSpec((tk, n), lambda k: (k, 0))],
                  out_specs=[pl.BlockSpec(memory_space=pl.ANY)],
                  scratch_shapes=[pltpu.VMEM((D, n), F32), pltpu.SemaphoreType.DMA((4,))], carry=carry)


def _wgrad_in_t(h1, pieces, carry=None):
    s = h1.shape[0]
    tk = min(1024, s)
    nk = s // tk

    def body(a_ref, dq_ref, dkv_ref, dz_ref, dxbc_ref, ddt_ref, o_hbm, acc_ref, tr_ref, sem):
        k = pl.program_id(0)

        @pl.when(k == 0)
        def _():
            acc_ref[...] = jnp.zeros_like(acc_ref)

        dproj = jnp.concatenate([dq_ref[...], dkv_ref[...], dz_ref[...], dxbc_ref[...], ddt_ref[...]], axis=1)
        acc_ref[...] += _mm_tn(a_ref[...], dproj)

        @pl.when(k == nk - 1)
        def _():
            for j in range(PROJ_W // 128):
                tr_ref[j * 128:(j + 1) * 128, :] = acc_ref[:, j * 128:(j + 1) * 128].T
            cp = pltpu.make_async_copy(tr_ref, o_hbm, sem)
            cp.start()
            cp.wait()

    return _pcall(body, [h1] + list(pieces), name="wgrad_in", grid=(nk,),
                  out_shape=[jax.ShapeDtypeStruct((PROJ_W, D), F32)],
                  in_specs=[pl.BlockSpec((tk, p.shape[1]), lambda k: (k, 0)) for p in [h1] + list(pieces)],
                  out_specs=[pl.BlockSpec(memory_space=pl.ANY)],
                  scratch_shapes=[pltpu.VMEM((D, PROJ_W), F32), pltpu.VMEM((PROJ_W, D), F32), pltpu.SemaphoreType.DMA],
                  carry=carry)


_SMALL = ["ada_b", "norm1", "conv_w", "conv_b", "dt_bias", "A_log", "D_skip", "sinks", "attn_out_norm",
          "ssm_out_norm", "norm2", "rel_bias", "final_norm"]


def _small_grad(name, gs, chip):
    if name == "ada_b":
        return jnp.concatenate([gs[j:j + 1, :] for j in range(6)], axis=1)
    if name == "conv_w":
        full = gs[7:11, :]
        out = full[:, 0:256]
        for j in range(1, 4):
            out = jnp.where(chip == j, full[:, j * 256:(j + 1) * 256], out)
        return out
    row, width = {"norm1": (6, D), "conv_b": (11, D), "norm2": (12, D), "final_norm": (13, D),
                  "attn_out_norm": (14, QW), "ssm_out_norm": (15, SW), "dt_bias": (16, NH), "A_log": (17, NH),
                  "D_skip": (18, NH), "sinks": (19, NH), "rel_bias": (24, NH)}[name]
    rows = NBUCKET if name == "rel_bias" else 1
    return gs[row:row + rows, 0:width]


def _small_update(small_all, where, ws, ms, vs):
    n = len(_SMALL)

    def body(where_ref, sa_ref, *refs):
        w_refs, m_refs, v_refs, outs = refs[:n], refs[n:2 * n], refs[2 * n:3 * n], refs[3 * n:]
        gs = sa_ref[0]
        for b in range(1, 8):
            gs = gs + sa_ref[b]
        chip = where_ref[1]
        for i, name in enumerate(_SMALL):
            g = _small_grad(name, gs, chip)
            lead = (0,) if name == "conv_w" else ()
            d, mo, vo = _adamw(w_refs[i][lead + (...,)], g, m_refs[i][lead + (...,)], v_refs[i][lead + (...,)])
            for k, val in enumerate((g, d, mo, vo)):
                outs[k * n + i][lead + (...,)] = val
        outs[4 * n][...] = gs[20:21, 0:128]

    shapes = [jax.ShapeDtypeStruct(w.shape, F32) for w in ws]
    vmem = pl.BlockSpec(memory_space=pltpu.VMEM)
    res = pl.pallas_call(
        body, name="small_update", out_shape=shapes * 4 + [jax.ShapeDtypeStruct((1, 128), F32)],
        in_specs=[pl.BlockSpec(memory_space=pltpu.SMEM)] + [vmem] * (1 + 3 * n), out_specs=[vmem] * (4 * n + 1),
    )(where, small_all, *ws, *ms, *vs)
    return [res[k * n:(k + 1) * n] for k in range(4)], res[4 * n][0, 0]


def _add_half(name, g, got, where, by_cols=False):
    rr, cc = got.shape[1:]
    if by_cols:
        mine = pl.BlockSpec((None, rr, cc), lambda i, w_ref: (i, 0, w_ref[0]))
    else:
        mine = pl.BlockSpec((None, None, rr, cc), lambda i, w_ref: (i, w_ref[0], 0, 0))

    def body(w_ref, g_ref, r_ref, o_ref, own_ref):
        s = g_ref[...] + r_ref[...]
        o_ref[...] = s.astype(BF16)

        @pl.when(pl.program_id(0) == w_ref[1])
        def _():
            own_ref[...] = s

    spec = pl.BlockSpec((None, rr, cc), lambda i, w_ref: (i, 0, 0))
    return _pcall(body, [where, g, got], name=name, grid=(4,), nprefetch=1,
                  out_shape=[jax.ShapeDtypeStruct(got.shape, BF16), jax.ShapeDtypeStruct((rr, cc), F32)],
                  in_specs=[mine, spec],
                  out_specs=[spec, pl.BlockSpec((rr, cc), lambda i, w_ref: (0, 0))])


def _add_chips(name, own, got):
    rr, cc = own.shape
    tr = rr // 2 if rr % 32 == 0 else rr

    def body(s_ref, r_ref, o_ref):
        o_ref[...] = ((s_ref[...] + r_ref[0].astype(F32)) + r_ref[1].astype(F32)) + r_ref[2].astype(F32)

    spec = pl.BlockSpec((tr, cc), lambda i: (i, 0))
    return _pcall(body, [own, got], name=name, grid=(rr // tr,), out_shape=[jax.ShapeDtypeStruct((rr, cc), F32)],
                  in_specs=[spec, pl.BlockSpec((3, tr, cc), lambda i: (0, i, 0))], out_specs=[spec])[0]


def _adamw_halves(name, mine, got, w, m, v, where, by_cols=False):
    rr, cc = mine.shape

    def body(w_ref_, t_ref, r_ref, w_ref, m_ref, v_ref, g_ref, d_ref, mo_ref, vo_ref):
        g = jnp.where(pl.program_id(0) == w_ref_[0], t_ref[...], r_ref[...])
        g_ref[...] = g
        d_ref[...], mo_ref[...], vo_ref[...] = _adamw(w_ref[...], g, m_ref[...], v_ref[...])

    if by_cols:
        grid = (2, 1)
        half = pl.BlockSpec((rr, cc), lambda h, i, w_ref_: (0, 0))
        full = pl.BlockSpec((rr, cc), lambda h, i, w_ref_: (0, h))
    else:
        tr = rr // 2
        grid = (2, 2)
        half = pl.BlockSpec((tr, cc), lambda h, i, w_ref_: (i, 0))
        full = pl.BlockSpec((None, tr, cc), lambda h, i, w_ref_: (0, 2 * h + i, 0))
    return _pcall(body, [where, mine, got, w, m, v], name=name, grid=grid, nprefetch=1,
                  out_shape=[jax.ShapeDtypeStruct(w.shape, F32)] * 4,
                  in_specs=[half, half, full, full, full], out_specs=[full] * 4)


def _bias_table(rel_bias, bucket, mask):
    def body(rb_ref, bk_ref, mk_ref, o_ref):
        bk = bk_ref[...]
        valid = mk_ref[...] > 0
        for h in range(NH):
            acc = jnp.zeros((BLK, 2 * BLK), F32)
            for b in range(NBUCKET):
                acc = jnp.where(bk == b, rb_ref[b, h], acc)
            o_ref[h] = jnp.where(valid, acc, NEG)

    vmem = pl.BlockSpec(memory_space=pltpu.VMEM)
    return pl.pallas_call(
        body, name="bias_table", out_shape=jax.ShapeDtypeStruct((NH, BLK, 2 * BLK), F32),
        in_specs=[pl.BlockSpec(memory_space=pltpu.SMEM), vmem, vmem], out_specs=vmem,
    )(rel_bias, bucket, mask)


def _pack_small(dsh1, p1, dsh2, p2, dg1a, dg1b, dg2, norm1, norm2, scale1, scale2, dcw, dcb, dfn,
                dnw_attn, dnw_ssm, dhd, av, dsink, drel, loss_acc):
    def body(dsh1_ref, p1_ref, dsh2_ref, p2_ref, dg1a_ref, dg1b_ref, dg2_ref, n1_ref, n2_ref, s1_ref, s2_ref,
             dcw_ref, dcb_ref, dfn_ref, da_ref, ds_ref, dhd_ref, av_ref, dsink_ref, drel_ref, loss_ref, o_ref):
        o_ref[...] = jnp.zeros_like(o_ref)
        p1v, p2v = p1_ref[0:1, :], p2_ref[0:1, :]
        o_ref[0:1, :] = dsh1_ref[0:1, :]
        o_ref[1:2, :] = p1v * n1_ref[...]
        o_ref[2:3, :] = dg1a_ref[0:1, :] + dg1b_ref[0:1, :]
        o_ref[3:4, :] = dsh2_ref[0:1, :]
        o_ref[4:5, :] = p2v * n2_ref[...]
        o_ref[5:6, :] = dg2_ref[0:1, :]
        o_ref[6:7, :] = p1v * (1.0 + s1_ref[...])
        o_ref[7:11, :] = dcw_ref[0:4, :]
        o_ref[11:12, :] = dcb_ref[0:1, :]
        o_ref[12:13, :] = p2v * (1.0 + s2_ref[...])
        o_ref[13:14, :] = dfn_ref[0:1, :]
        o_ref[14:15, 0:QW] = da_ref[0:1, :]
        o_ref[15:16, 0:SW] = ds_ref[0:1, :]
        o_ref[16:17, 0:128] = dhd_ref[0:1, :]
        o_ref[17:18, 0:128] = dhd_ref[1:2, :] * av_ref[...]
        o_ref[18:19, 0:128] = dhd_ref[2:3, :]
        o_ref[19:20, 0:128] = dsink_ref[0:1, :]
        o_ref[20:21, 0:128] = loss_ref[0:1, :]
        o_ref[24:56, 0:128] = drel_ref[...]

    return pl.pallas_call(body, name="pack_small", out_shape=jax.ShapeDtypeStruct((56, D), F32))(
        dsh1, p1, dsh2, p2, dg1a, dg1b, dg2, norm1, norm2, scale1, scale2, dcw, dcb, dfn,
        dnw_attn, dnw_ssm, dhd, av, dsink, drel, loss_acc)


def _pad_row(a, rows=1):
    return jnp.pad(a.reshape(rows, -1), ((0, 0), (0, D - a.size // rows)))


def kernel(x, c, ada_w, ada_b, norm1, w_in, conv_w, conv_b, dt_bias, A_log, D_skip, sinks, attn_out_norm, ssm_out_norm, w_o, norm2, w_gate_up, w_down, rel_bias, final_norm, loss_target, m_ada_w, m_ada_b, m_norm1, m_w_in, m_conv_w, m_conv_b, m_dt_bias, m_A_log, m_D_skip, m_sinks, m_attn_out_norm, m_ssm_out_norm, m_w_o, m_norm2, m_w_gate_up, m_w_down, m_rel_bias, m_final_norm, v_ada_w, v_ada_b, v_norm1, v_w_in, v_conv_w, v_conv_b, v_dt_bias, v_A_log, v_D_skip, v_sinks, v_attn_out_norm, v_ssm_out_norm, v_w_o, v_norm2, v_w_gate_up, v_w_down, v_rel_bias, v_final_norm):
    xi, yi, ci = lax.axis_index("x"), lax.axis_index("y"), lax.axis_index("c")
    chip = 2 * xi + yi
    me = 4 * xi + 2 * yi + ci
    where = jnp.stack([ci, chip]).astype(jnp.int32)
    xs2, tgt = x[0], loss_target[0]

    first = jnp.concatenate([c, _pad_row(conv_w[0], CK), jnp.zeros((3, D), F32)], axis=0)
    w_in_t, m_w_in_t, v_w_in_t = w_in[0].T, m_w_in[0].T, v_w_in[0].T
    w_in_b, w_o_b, w_dn_b = w_in_t.astype(BF16), w_o[0].astype(BF16), w_down[0].astype(BF16)
    w_gu_b = w_gate_up[0].astype(BF16)
    ncol = ada_w.shape[2]
    first_all, w_half, mod_all, w_other = _front(first, w_in_b, ada_w[0],
                                                 lax.dynamic_slice(ada_b, (0, chip * ncol), (1, ncol)))
    c_all = first_all[:, 0, :]
    cw_full = jnp.concatenate([first_all[2 * j, 1:1 + CK, 0:256] for j in range(4)], axis=1)
    w_lo = jnp.where(ci == 0, w_half, w_other)
    w_hi = jnp.where(ci == 0, w_other, w_half)
    w_in_f = jnp.pad(jnp.concatenate([w_lo, w_hi], axis=2).reshape(IN_W, D), ((0, PROJ_W - IN_W), (0, 0)))
    mod = lax.dynamic_slice(jnp.transpose(mod_all, (1, 0, 2)).reshape(8, 4 * ncol), (me, 0), (1, 4 * ncol))
    shift1, scale1, gate1, shift2, scale2, gate2 = [mod[:, j * D:(j + 1) * D] for j in range(6)]
    a1 = norm1 * (1.0 + scale1)
    a2 = norm2 * (1.0 + scale2)

    hdn = DFF // 8
    q, kv, z, xbc, dtr, w_o_g, w_dna_g = _in_proj_fwd(xs2, a1, shift1, w_in_f,
                                                      carry=_gather_chips_carry([w_o_b, w_dn_b[0:hdn]]))
    w_o_f = w_o_g.reshape(D, D)
    bucket, mask = _attn_geometry()
    bucket = jnp.asarray(bucket)
    bias = _bias_table(rel_bias, bucket, jnp.asarray(mask.astype(np.int32)))
    sinks1 = sinks[0]
    ya, w_ga_g = _attn_fwd(q, kv, bias, sinks1, attn_out_norm, carry=_gather_chips_carry([w_gu_b[0:D // 2]]))
    cw8 = jnp.concatenate([cw_full, jnp.zeros((4, XBCW), F32)], axis=0)
    dtb = _pad_row(dt_bias)[:, 0:128]
    av = _pad_row(-jnp.exp(A_log))[:, 0:128]
    dk = jnp.repeat(D_skip, HD, axis=1)
    ys, hs, pre, w_gb_g, w_dnb_g = _ssd_fwd(z, xbc, dtr, cw8, conv_b, dtb, av, dk, ssm_out_norm,
                                            carry=_gather_chips_carry([w_gu_b[D // 2:D], w_dn_b[hdn:2 * hdn]]))
    w_dn_f = jnp.stack([w_dna_g, w_dnb_g], axis=1).reshape(DFF, D)
    fn = final_norm[None, :]
    x1, gu, dx2, loss_acc, dfn = _mlp_fwd(xs2, ya, ys, tgt, w_o_f, w_ga_g, w_gb_g, w_dn_f, gate1, a2, shift2, gate2, fn)

    def to_sibling(p):
        return _Carry([p], [jax.ShapeDtypeStruct((4,) + p.shape[2:], F32)],
                      lambda x_, y_, c_: [(_SIBLING, 0, (j, 1 - c_), 0, j) for j in range(4)])

    def to_chips(s4):
        return _Carry([s4], [jax.ShapeDtypeStruct((3,) + s4.shape[1:], s4.dtype)],
                      lambda x_, y_, c_: [(f, 0, jnp.bitwise_xor(2 * x_ + y_, k + 1), 0, k) for k, f in enumerate(_CHIPS3)])

    def back(t):
        return _Carry([t[None]], [jax.ShapeDtypeStruct((1,) + t.shape, F32)], lambda x_, y_, c_: [(_SIBLING, 0, 0, 0, 0)])

    dx1, dya, dys, act, dgu, h2, dsh2, p2 = _mlp_bwd(x1, gu, dx2, w_o_f, w_ga_g, w_gb_g, w_dn_f, gate1, a2, shift2, gate2)
    p_gu = _wgrad_gate_up(h2, dgu)[0].reshape(4, 2, D // 2, 2 * DFF // 4)
    dq, dkv, dbias, dsink, dnw_attn, g_dn, dg2, got1_gu = _attn_bwd(
        q, kv, dya, bias, sinks1, attn_out_norm, act, dx2, gate2, w_dn_f, carry=to_sibling(p_gu))
    p_dn = g_dn.reshape(4, 2, DFF // 8, D)
    drel = _rel_bias_grad(dbias, bucket)
    s4_gu, own_gu = _add_half("rs_add_half_gu", p_gu, got1_gu, where)
    dz, dxbc, ddt, dcw, dcb, dnw_ssm, dhd, got2_gu, got1_dn = _ssd_bwd(
        z, xbc, pre, dtr, dys, hs, cw8, dtb, av, dk, ssm_out_norm, carry=_merge(to_chips(s4_gu), to_sibling(p_dn)))
    mine_gu = _add_chips("rs_add_chips_gu", own_gu, got2_gu)
    s4_dn, own_dn = _add_half("rs_add_half_dn", p_dn, got1_dn, where)
    grad_x, h1, dsh1, p1 = _in_proj_bwd(xs2, dx1, a1, shift1, w_in_f, dq, dkv, dz, dxbc, ddt)
    g_in_t, got2_dn, got3_gu = _wgrad_in_t(h1, [dq, dkv, dz, dxbc, ddt],
                                           carry=_merge(to_chips(s4_dn), back(mine_gu)))
    mine_dn = _add_chips("rs_add_chips_dn", own_dn, got2_dn)
    p_in = g_in_t[0:IN_W].reshape(4, IN_W // 4, D)

    def to_sibling_cols(p):
        return _Carry([p], [jax.ShapeDtypeStruct(p.shape[:2] + (D // 2,), F32)],
                      lambda x_, y_, c_: [(_SIBLING, 0, (j, slice(None), pl.ds((1 - c_) * (D // 2), D // 2)), 0, j)
                                          for j in range(4)])

    no_dg1 = jnp.zeros((8, D), F32)
    small = _pack_small(dsh1, p1, dsh2, p2, no_dg1, no_dg1, dg2, norm1, norm2, scale1, scale2, dcw, dcb, dfn,
                        dnw_attn, dnw_ssm, dhd, av, dsink, drel, loss_acc)
    g_oa, dg1a, got1_in, got3_dn, small_all = _wgrad(
        "wgrad_o_attn", ya, dx1, gate1, w_o_f[0:QW],
        carry=_merge(to_sibling_cols(p_in), back(mine_dn), _gather8_carry(small)))
    s4_in, own_in = _add_half("rs_add_half_in", p_in, got1_in, where, by_cols=True)
    g_os, dg1b, got2_in = _wgrad("wgrad_o_ssm", ys, dx1, gate1, w_o_f[QW:D], carry=to_chips(s4_in))
    mine_in = _add_chips("rs_add_chips_in", own_in, got2_in)
    p_o = jnp.concatenate([g_oa, g_os], axis=0).reshape(4, 2, D // 8, D)

    dg1_all, got1_o, got3_in = _exchange(
        "gather_tail", _merge(_gather8_carry(dg1a[0:1] + dg1b[0:1]), to_sibling(p_o), back(mine_in)))
    small_all = small_all.at[:, 2, :].set(dg1_all[:, 0, :])
    s4_o, own_o = _add_half("rs_add_half_o", p_o, got1_o, where)
    mine_o = _add_chips("rs_add_chips_o", own_o, _exchange("rs_chips_o", to_chips(s4_o))[0])
    got3_o = _exchange("rs_back_o", back(mine_o))[0]
    small_res, loss = _small_update(
        small_all, where,
        [ada_b, norm1, conv_w, conv_b, dt_bias, A_log, D_skip, sinks, attn_out_norm, ssm_out_norm, norm2, rel_bias,
         final_norm[None, :]],
        [m_ada_b, m_norm1, m_conv_w, m_conv_b, m_dt_bias, m_A_log, m_D_skip, m_sinks, m_attn_out_norm,
         m_ssm_out_norm, m_norm2, m_rel_bias, m_final_norm[None, :]],
        [v_ada_b, v_norm1, v_conv_w, v_conv_b, v_dt_bias, v_A_log, v_D_skip, v_sinks, v_attn_out_norm,
         v_ssm_out_norm, v_norm2, v_rel_bias, v_final_norm[None, :]])
    small_out = [dict(zip(_SMALL, r)) for r in small_res]
    for r in small_out:
        r["final_norm"] = r["final_norm"][0]

    dmod_all = small_all[:, 0:6, :].reshape(8, 6 * D)
    dmod_loc = lax.dynamic_slice(dmod_all, (0, chip * ncol), (8, ncol))
    ada_out = _ada_bwd_adamw(c_all.T, dmod_loc, ada_w[0], m_ada_w[0], v_ada_w[0])

    big_gu = _adamw_halves("adamw_gate_up", mine_gu, got3_gu[0], w_gate_up, m_w_gate_up, v_w_gate_up, where)
    big_dn = _adamw_halves("adamw_down", mine_dn, got3_dn[0], w_down, m_w_down, v_w_down, where)
    big_o = _adamw_halves("adamw_o", mine_o, got3_o[0], w_o, m_w_o, v_w_o, where)
    big_in = [o.T[None] for o in _adamw_halves("adamw_in", mine_in, got3_in[0], w_in_t, m_w_in_t, v_w_in_t, where,
                                               by_cols=True)]
    big = [big_in, big_o, big_gu, big_dn]

    order = ["ada_w", "ada_b", "norm1", "w_in", "conv_w", "conv_b", "dt_bias", "A_log", "D_skip", "sinks",
             "attn_out_norm", "ssm_out_norm", "w_o", "norm2", "w_gate_up", "w_down", "rel_bias", "final_norm"]
    bigname = {"w_in": 0, "w_o": 1, "w_gate_up": 2, "w_down": 3}
    res = [loss, grad_x[None]]
    for kind in range(4):
        for nm in order:
            if nm == "ada_w":
                res.append(ada_out[kind][None])
            elif nm in bigname:
                res.append(big[bigname[nm]][kind])
            else:
                res.append(small_out[kind][nm])
    return tuple(res)
```

```python
import numpy as np
import jax
import jax.numpy as jnp
from jax import lax
from jax.experimental import pallas as pl
from jax.experimental.pallas import tpu as pltpu

F32, BF16 = jnp.float32, jnp.bfloat16
HI = lax.Precision.HIGHEST

D = 1024
QW, KVW = 512, 128
NH, HD, NKV = 8, 64, 2
SW = 512
NST = 128
XBCW = 1024
CK = 4
BLK = 128
DFF = 2816
IN_W = 2312
PROJ_W = 2432
EPS = 1e-6
NEG = -1e30
NBUCKET = 32

B1, B2, LR, AEPS, WD, STEP = 0.9, 0.999, 0.001, 1e-08, 0.01, 10

VMEM_LIMIT = 56 * 1024 * 1024

_NT = (((1,), (1,)), ((), ()))
_TN = (((0,), (0,)), ((), ()))


def _mm(a, b):
    return jnp.dot(a, b, preferred_element_type=F32)


def _mm_nt(a, b):
    return lax.dot_general(a, b, _NT, preferred_element_type=F32)


def _mm_tn(a, b):
    return lax.dot_general(a, b, _TN, preferred_element_type=F32)


def _mm_hi(a, b):
    return jnp.dot(a, b, preferred_element_type=F32, precision=HI)


def _split3(x):
    hi = x.astype(BF16)
    r = x - hi.astype(F32)
    mid = r.astype(BF16)
    lo = (r - mid.astype(F32)).astype(BF16)
    return hi, mid, lo


def _sel_r(x, e):
    hi, mid, lo = _split3(x)
    return (_mm(hi, e) + _mm(mid, e)) + _mm(lo, e)


def _sel_l(e, x):
    hi, mid, lo = _split3(x)
    return (_mm(e, hi) + _mm(e, mid)) + _mm(e, lo)


def _sig(x):
    return 1.0 / (1.0 + jnp.exp(-x))


def _cp(sem):
    return pltpu.CompilerParams(dimension_semantics=sem, vmem_limit_bytes=VMEM_LIMIT)


def _row(shape):
    nd = len(shape)
    return pl.BlockSpec(shape, lambda *_: (0,) * nd)


def _adamw(w, g, m, v):
    m = B1 * m + (1.0 - B1) * g
    v = B2 * v + (1.0 - B2) * (g * g)
    m_hat = m / (1.0 - B1 ** STEP)
    v_hat = v / (1.0 - B2 ** STEP)
    delta = -LR * (m_hat / (jnp.sqrt(v_hat) + AEPS) + WD * w)
    return delta, m, v


class _Carry:
    def __init__(self, inps, outs, copies):
        self.inps, self.outs, self.copies = list(inps), list(outs), copies
        self.n = len(copies(0, 0, 0))

    def descriptors(self, in_refs, out_refs, send_sems, recv_sems):
        x, y, c = lax.axis_index("x"), lax.axis_index("y"), lax.axis_index("c")
        out = []
        for j, (flip, a, si, o, di) in enumerate(self.copies(x, y, c)):
            if flip is None:
                out.append(pltpu.make_async_copy(in_refs[a].at[si], out_refs[o].at[di], send_sems.at[j]))
            else:
                fx, fy, fc = flip
                peer = (1 - x if fx else x, 1 - y if fy else y, 1 - c if fc else c)
                out.append(pltpu.make_async_remote_copy(
                    src_ref=in_refs[a].at[si], dst_ref=out_refs[o].at[di],
                    send_sem=send_sems.at[j], recv_sem=recv_sems.at[j],
                    device_id=peer, device_id_type=pl.DeviceIdType.MESH))
        return out


def _pcall(body, args, *, name, grid, in_specs, out_specs, out_shape, scratch_shapes=(), sem=None, nprefetch=0,
           carry=None):
    out_shape, out_specs = list(out_shape), list(out_specs)
    in_specs, scratch_shapes = list(in_specs), list(scratch_shapes)
    nin, nout, nscr = len(in_specs), len(out_shape), len(scratch_shapes)
    run = body
    if carry is not None:
        ncin, ncout = len(carry.inps), len(carry.outs)
        hbm = pl.BlockSpec(memory_space=pl.ANY)

        def run(*refs):
            pre, r = refs[:nprefetch], refs[nprefetch:]
            ins, cins = r[:nin], r[nin:nin + ncin]
            r = r[nin + ncin:]
            outs, couts = r[:nout], r[nout:nout + ncout]
            r = r[nout + ncout:]
            scr, (send_sems, recv_sems) = r[:nscr], r[nscr:]
            first = pl.program_id(0) == 0
            last = pl.program_id(0) == grid[0] - 1
            for ax in range(1, len(grid)):
                first = jnp.logical_and(first, pl.program_id(ax) == 0)
                last = jnp.logical_and(last, pl.program_id(ax) == grid[ax] - 1)

            @pl.when(first)
            def _():
                for d in carry.descriptors(cins, couts, send_sems, recv_sems):
                    d.start()

            body(*pre, *ins, *outs, *scr)

            @pl.when(last)
            def _():
                for d in carry.descriptors(cins, couts, send_sems, recv_sems):
                    d.wait()

        in_specs = in_specs + [hbm] * ncin
        out_specs = out_specs + [hbm] * ncout
        out_shape = out_shape + carry.outs
        scratch_shapes = scratch_shapes + [pltpu.SemaphoreType.DMA((carry.n,)), pltpu.SemaphoreType.DMA((carry.n,))]
        args = list(args) + carry.inps
    if sem is None:
        sem = ("arbitrary",) * len(grid)
    if nprefetch:
        kw = dict(grid_spec=pltpu.PrefetchScalarGridSpec(num_scalar_prefetch=nprefetch, grid=grid, in_specs=in_specs,
                                                         out_specs=out_specs, scratch_shapes=scratch_shapes))
    else:
        kw = dict(grid=grid, in_specs=in_specs, out_specs=out_specs, scratch_shapes=scratch_shapes)
    res = pl.pallas_call(run, name=name, out_shape=out_shape, compiler_params=_cp(sem), **kw)(*args)
    return list(res)


def _merge(*carries):
    inps, outs, offs = [], [], []
    for cr in carries:
        offs.append((len(inps), len(outs)))
        inps += cr.inps
        outs += cr.outs

    def copies(x, y, c):
        return [(f, a + io, si, o + oo, di) for cr, (io, oo) in zip(carries, offs) for f, a, si, o, di in cr.copies(x, y, c)]

    return _Carry(inps, outs, copies)


_ALL7 = [(f >> 2 & 1, f >> 1 & 1, f & 1) for f in range(1, 8)]
_CHIPS3 = [(0, 1, 0), (1, 0, 0), (1, 1, 0)]
_SIBLING = (0, 0, 1)


def _gather8_carry(blk):
    def copies(x, y, c):
        me = 4 * x + 2 * y + c
        return [(None, 0, 0, 0, me)] + [(f, 0, 0, 0, me) for f in _ALL7]

    return _Carry([blk[None]], [jax.ShapeDtypeStruct((8,) + blk.shape, blk.dtype)], copies)


def _gather_chips_carry(blks):
    def copies(x, y, c):
        chip = 2 * x + y
        return [(f, a, 0, a, chip) for a in range(len(blks)) for f in [None] + _CHIPS3]

    return _Carry([b[None] for b in blks], [jax.ShapeDtypeStruct((4,) + b.shape, b.dtype) for b in blks], copies)


def _front(first, w_in_t, w_loc, b_loc):
    n = w_loc.shape[1]
    rows = w_in_t.shape[0]
    hw = D // 2
    gather = _gather8_carry(first)
    fetch = _Carry([w_in_t], [jax.ShapeDtypeStruct((4, rows, hw), BF16)],
                   lambda x_, y_, c_: [(f, 0, (slice(None), pl.ds(c_ * hw, hw)), 0, 2 * x_ + y_) for f in [None] + _CHIPS3])
    phase_a = _merge(gather, fetch)
    send_mod = _Carry([None], [None], lambda x_, y_, c_: [(f, 0, slice(None), 0, 2 * x_ + y_) for f in [None] + _CHIPS3])
    swap = _Carry([None], [None], lambda x_, y_, c_: [(_SIBLING, 0, slice(None), 0, slice(None))])

    def body(first_hbm, w_hbm, wl_ref, bl_ref, first_all, w_half, mod_all, w_other,
             c_scr, mod_scr, sa, ra, sb, rb, sc, rc, sl):
        da = phase_a.descriptors([first_hbm, w_hbm], [first_all, w_half], sa, ra)
        for d in da:
            d.start()
        for d in da[:gather.n]:
            d.wait()
        cp = pltpu.make_async_copy(first_all, c_scr, sl)
        cp.start()
        cp.wait()
        cv = c_scr[:, 0, :]
        cond = cv * _sig(cv)
        for j in range(n // 512):
            cols = slice(j * 512, (j + 1) * 512)
            mod_scr[:, cols] = _mm_hi(cond, wl_ref[:, cols]) + bl_ref[:, cols]
        db = send_mod.descriptors([mod_scr], [mod_all], sb, rb)
        for d in db:
            d.start()
        for d in da[gather.n:]:
            d.wait()
        dc = swap.descriptors([w_half], [w_other], sc, rc)
        for d in dc:
            d.start()
        for d in db + dc:
            d.wait()

    hbm = pl.BlockSpec(memory_space=pl.ANY)
    vmem = pl.BlockSpec(memory_space=pltpu.VMEM)
    sems = pltpu.SemaphoreType.DMA
    return pl.pallas_call(
        body, name="front",
        out_shape=[jax.ShapeDtypeStruct((8,) + first.shape, F32), jax.ShapeDtypeStruct((4, rows, hw), BF16),
                   jax.ShapeDtypeStruct((4, 8, n), F32), jax.ShapeDtypeStruct((4, rows, hw), BF16)],
        in_specs=[hbm, hbm, vmem, vmem], out_specs=[hbm] * 4,
        scratch_shapes=[pltpu.VMEM((8,) + first.shape, F32), pltpu.VMEM((8, n), F32),
                        sems((phase_a.n,)), sems((phase_a.n,)), sems((4,)), sems((4,)), sems((1,)), sems((1,)), sems],
        compiler_params=pltpu.CompilerParams(vmem_limit_bytes=VMEM_LIMIT),
    )(first[None], w_in_t, w_loc, b_loc)


def _tail(g_a, g_b, row_a, row_b, mine_in):
    rr, cc = g_a.shape[0] // 4, g_a.shape[1]

    def copies_a(x_, y_, c_):
        me = 4 * x_ + 2 * y_ + c_
        out = [(None, 0, 0, 0, me)] + [(f, 0, 0, 0, me) for f in _ALL7]
        out += [(_SIBLING, 1 + j // 2, pl.ds((j % 2) * 2 * rr + (1 - c_) * rr, rr), 1, j) for j in range(4)]
        out += [(None, 1 + j // 2, pl.ds((j % 2) * 2 * rr + c_ * rr, rr), 2, j) for j in range(4)]
        return out + [(_SIBLING, 3, slice(None), 3, slice(None))]

    phase_a = _Carry([None] * 4, [None] * 4, copies_a)
    to_chips = _Carry([None], [None], lambda x_, y_, c_: [(f, 0, jnp.bitwise_xor(2 * x_ + y_, k + 1), 0, k)
                                                        for k, f in enumerate(_CHIPS3)])
    back = _Carry([None], [None] * 2, lambda x_, y_, c_: [(None, 0, slice(None), 0, slice(None)),
                                                        (_SIBLING, 0, slice(None), 1, slice(None))])

    def body(ga_hbm, gb_hbm, ra_ref, rb_ref, in_hbm, rows_all, got_in, mine_o, got_o,
             row_scr, got1_scr, mine_scr, s4_scr, got2_scr, red_scr, sa, ra, sb, rb, sc, rc):
        chip = 2 * lax.axis_index("x") + lax.axis_index("y")
        row_scr[0] = ra_ref[0:1, :] + rb_ref[0:1, :]
        da = phase_a.descriptors([row_scr, ga_hbm, gb_hbm, in_hbm], [rows_all, got1_scr, mine_scr, got_in], sa, ra)
        for d in da:
            d.start()
        for d in da[8:16]:
            d.wait()
        for j in range(4):
            s4_scr[j] = (mine_scr[j] + got1_scr[j]).astype(BF16)
        red_scr[...] = mine_scr[chip] + got1_scr[chip]
        db = to_chips.descriptors([s4_scr], [got2_scr], sb, rb)
        for d in db:
            d.start()
        for d in db:
            d.wait()
        red_scr[...] = ((red_scr[...] + got2_scr[0].astype(F32)) + got2_scr[1].astype(F32)) + got2_scr[2].astype(F32)
        dc = back.descriptors([red_scr], [mine_o, got_o], sc, rc)
        for d in dc:
            d.start()
        for d in da[:8] + da[16:] + dc:
            d.wait()

    hbm = pl.BlockSpec(memory_space=pl.ANY)
    vmem = pl.BlockSpec(memory_space=pltpu.VMEM)
    sems = pltpu.SemaphoreType.DMA
    half = jax.ShapeDtypeStruct((rr, cc), F32)
    return pl.pallas_call(
        body, name="tail",
        out_shape=[jax.ShapeDtypeStruct((8, 1, cc), F32), jax.ShapeDtypeStruct(mine_in.shape, F32), half, half],
        in_specs=[hbm, hbm, vmem, vmem, hbm], out_specs=[hbm] * 4,
        scratch_shapes=[pltpu.VMEM((1, 1, cc), F32), pltpu.VMEM((4, rr, cc), F32), pltpu.VMEM((4, rr, cc), F32),
                        pltpu.VMEM((4, rr, cc), BF16), pltpu.VMEM((3, rr, cc), BF16), pltpu.VMEM((rr, cc), F32),
                        sems((phase_a.n,)), sems((phase_a.n,)), sems((3,)), sems((3,)), sems((2,)), sems((2,))],
        compiler_params=pltpu.CompilerParams(vmem_limit_bytes=VMEM_LIMIT),
    )(g_a, g_b, row_a, row_b, mine_in)


def _ada_bwd_adamw(c_all_t, dmod_loc, w, m, v, carry=None):
    n = w.shape[1]
    tn = 512

    def body(ct_ref, dm_ref, w_ref, m_ref, v_ref, g_ref, d_ref, mo_ref, vo_ref):
        ct = ct_ref[...]
        cond = ct * _sig(ct)
        dm = dm_ref[...]
        g = cond[:, 0:1] * dm[0:1, :]
        for b in range(1, 8):
            g = g + cond[:, b:b + 1] * dm[b:b + 1, :]
        g_ref[...] = g
        d_ref[...], mo_ref[...], vo_ref[...] = _adamw(w_ref[...], g, m_ref[...], v_ref[...])

    wspec = pl.BlockSpec((D, tn), lambda j: (0, j))
    return _pcall(
        body, [c_all_t, dmod_loc, w, m, v], name="ada_bwd_adamw", grid=(n // tn,),
        out_shape=[jax.ShapeDtypeStruct((D, n), F32)] * 4,
        in_specs=[_row((D, 8)), pl.BlockSpec((8, tn), lambda j: (0, j)), wspec, wspec, wspec],
        out_specs=[wspec] * 4, carry=carry)


def _in_proj_fwd(x, a1, sh1, w_in, carry=None):
    s = x.shape[0]
    tm = 512

    def body(x_ref, a_ref, s_ref, w_ref, q_ref, kv_ref, z_ref, xbc_ref, dt_ref):
        def norm(rows):
            xv = x_ref[rows, :]
            r = lax.rsqrt(jnp.mean(xv * xv, axis=-1, keepdims=True) + EPS)
            return (xv * r * a_ref[...] + s_ref[...]).astype(BF16)

        def project(rows, h):
            p = _mm_nt(h, w_ref[...])
            q_ref[rows, :] = p[:, 0:512].astype(BF16)
            kv_ref[rows, :] = p[:, 512:768].astype(BF16)
            z_ref[rows, :] = p[:, 768:1280]
            xbc_ref[rows, :] = p[:, 1280:2304]
            dt_ref[rows, :] = p[:, 2304:2432]

        r0, r1 = slice(0, tm // 2), slice(tm // 2, tm)
        h0 = norm(r0)
        project(r0, h0)
        project(r1, norm(r1))

    def tok(w):
        return pl.BlockSpec((tm, w), lambda i: (i, 0))

    return _pcall(
        body, [x, a1, sh1, w_in], name="in_proj_fwd", grid=(s // tm,),
        out_shape=[jax.ShapeDtypeStruct((s, QW), BF16), jax.ShapeDtypeStruct((s, 2 * KVW), BF16),
                   jax.ShapeDtypeStruct((s, SW), F32), jax.ShapeDtypeStruct((s, XBCW), F32),
                   jax.ShapeDtypeStruct((s, 128), F32)],
        in_specs=[tok(D), _row((1, D)), _row((1, D)), _row((PROJ_W, D))],
        out_specs=[tok(QW), tok(2 * KVW), tok(SW), tok(XBCW), tok(128)], carry=carry)


def _in_proj_bwd(x, dx1, a1, sh1, w_in, dq, dkv, dz, dxbc, ddt, carry=None):
    s = x.shape[0]
    tm = 512

    def body(x_ref, dx1_ref, a_ref, s_ref, w_ref, dq_ref, dkv_ref, dz_ref, dxbc_ref, ddt_ref,
             gx_ref, h_ref, dsh_ref, p_ref):
        i = pl.program_id(0)

        @pl.when(i == 0)
        def _():
            dsh_ref[...] = jnp.zeros_like(dsh_ref)
            p_ref[...] = jnp.zeros_like(p_ref)

        def gather(st):
            rows = st["rows"]
            st["dproj"] = jnp.concatenate([dq_ref[rows, :], dkv_ref[rows, :], dz_ref[rows, :], dxbc_ref[rows, :],
                                           ddt_ref[rows, :]], axis=1)

        def back(st):
            st["dh"] = _mm(st.pop("dproj"), w_ref[...])

        def norm(st):
            rows, dh = st["rows"], st.pop("dh")
            xv = x_ref[rows, :]
            r = lax.rsqrt(jnp.mean(xv * xv, axis=-1, keepdims=True) + EPS)
            xn = xv * r
            a = a_ref[...]
            h_ref[rows, :] = (xn * a + s_ref[...]).astype(BF16)
            st["dsh"] = jnp.sum(dh, axis=0, keepdims=True)
            st["p"] = jnp.sum(dh * xn, axis=0, keepdims=True)
            u = dh * a
            gx_ref[rows, :] = dx1_ref[rows, :] + r * u - xn * (r * jnp.mean(u * xn, axis=-1, keepdims=True))

        g0, g1 = [dict(rows=slice(k * (tm // 2), (k + 1) * (tm // 2))) for k in range(2)]
        for stage, st in [(gather, g0), (back, g0), (gather, g1), (norm, g0), (back, g1), (norm, g1)]:
            stage(st)
        dsh_ref[0:1, :] += g0["dsh"] + g1["dsh"]
        p_ref[0:1, :] += g0["p"] + g1["p"]

    def tok(w):
        return pl.BlockSpec((tm, w), lambda i: (i, 0))

    return _pcall(
        body, [x, dx1, a1, sh1, w_in, dq, dkv, dz, dxbc, ddt], name="in_proj_bwd", grid=(s // tm,),
        out_shape=[jax.ShapeDtypeStruct((s, D), F32), jax.ShapeDtypeStruct((s, D), BF16),
                   jax.ShapeDtypeStruct((8, D), F32), jax.ShapeDtypeStruct((8, D), F32)],
        in_specs=[tok(D), tok(D), _row((1, D)), _row((1, D)), _row((PROJ_W, D)),
                  tok(QW), tok(2 * KVW), tok(SW), tok(XBCW), tok(128)],
        out_specs=[tok(D), tok(D), _row((8, D)), _row((8, D))], carry=carry)


def _attn_geometry():
    dist = np.arange(BLK)[:, None] + BLK - np.arange(2 * BLK)[None, :]
    n = np.maximum(dist, 0)
    max_exact = NBUCKET // 2
    large = max_exact + (np.log(np.maximum(n, 1) / max_exact) / np.log(128 / max_exact)
                         * (NBUCKET - max_exact)).astype(np.int32)
    large = np.minimum(large, NBUCKET - 1)
    bucket = np.where(n < max_exact, n, large).astype(np.int32)
    mask = (dist >= 0) & (dist < 128)
    return bucket, mask


def _attn_heads(is_first, q_blk, kvw, bias_ref, sinks_ref):
    qv = q_blk * 0.125
    col = lax.broadcasted_iota(jnp.int32, (BLK, 2 * BLK), 1)
    first = jnp.where(jnp.logical_and(is_first, col < BLK), NEG, 0.0)
    groups = []
    for g in range(NKV):
        qs = jnp.concatenate([qv[:, (4 * g + r) * HD:(4 * g + r + 1) * HD] for r in range(4)], axis=0)
        kw = kvw[:, g * HD:(g + 1) * HD]
        vw = kvw[:, KVW + g * HD:KVW + (g + 1) * HD]
        sc = _mm_nt(qs, kw)
        pn, ps = [], []
        for r in range(4):
            h = 4 * g + r
            sr = sc[r * BLK:(r + 1) * BLK] + bias_ref[h] + first
            sink = sinks_ref[h]
            m = jnp.maximum(jnp.max(sr, axis=-1, keepdims=True), sink)
            p = jnp.exp(sr - m)
            es = jnp.exp(sink - m)
            inv = 1.0 / (jnp.sum(p, axis=-1, keepdims=True) + es)
            pn.append(p * inv)
            ps.append(es * inv)
        pn = jnp.concatenate(pn, axis=0)
        ps = jnp.concatenate(ps, axis=0)
        o = _mm(pn.astype(BF16), vw)
        groups.append((qs, kw, vw, pn, ps, o))
    return groups


def _unstack_heads(parts):
    return jnp.concatenate([p[r * BLK:(r + 1) * BLK] for p in parts for r in range(4)], axis=1)


NB = 2


def _attn_fwd(q, kv, bias, sinks, nw, carry=None):
    s = q.shape[0]

    def body(q_ref, kvp_ref, kvc_ref, bias_ref, sinks_ref, nw_ref, y_ref):
        t = pl.program_id(0)
        kv3 = jnp.concatenate([kvp_ref[...], kvc_ref[...]], axis=0)
        for sub in range(NB):
            rows = slice(sub * BLK, (sub + 1) * BLK)
            groups = _attn_heads(jnp.logical_and(t == 0, sub == 0), q_ref[rows, :], kv3[sub * BLK:(sub + 2) * BLK],
                                 bias_ref, sinks_ref)
            o = _unstack_heads([g[5] for g in groups])
            r = lax.rsqrt(jnp.mean(o * o, axis=-1, keepdims=True) + EPS)
            y_ref[rows, :] = (o * r * nw_ref[...]).astype(BF16)

    return _pcall(
        body, [q, kv, kv, bias, sinks, nw], name="attn_fwd", grid=(s // (NB * BLK),),
        out_shape=[jax.ShapeDtypeStruct((s, QW), BF16)],
        in_specs=[pl.BlockSpec((NB * BLK, QW), lambda t: (t, 0)),
                  pl.BlockSpec((BLK, 2 * KVW), lambda t: (jnp.maximum(NB * t - 1, 0), 0)),
                  pl.BlockSpec((NB * BLK, 2 * KVW), lambda t: (t, 0)),
                  _row((NH, BLK, 2 * BLK)),
                  pl.BlockSpec(memory_space=pltpu.SMEM),
                  _row((1, QW))],
        out_specs=[pl.BlockSpec((NB * BLK, QW), lambda t: (t, 0))], carry=carry)


def _attn_bwd(q, kv, dya, bias, sinks, nw, act, dx2, gate2, w_dn, carry=None):
    s = q.shape[0]
    nt = s // (NB * BLK)
    npiece = DFF // NB

    def body(q_ref, kvp_ref, kvc_ref, dy_ref, bias_ref, sinks_ref, nw_ref, act_ref, dx2_ref, g2_ref, wdn_ref,
             dq_ref, dkv_ref, dbias_ref, dsink_ref, dnw_ref, gdn_hbm, dg2_ref, carry_ref, held_ref, acc_ref, sem):
        t = pl.program_id(0)

        @pl.when(t == 0)
        def _():
            carry_ref[...] = jnp.zeros_like(carry_ref)
            held_ref[...] = jnp.zeros_like(held_ref)
            dbias_ref[...] = jnp.zeros_like(dbias_ref)
            dsink_ref[...] = jnp.zeros_like(dsink_ref)
            dnw_ref[...] = jnp.zeros_like(dnw_ref)
            acc_ref[...] = jnp.zeros_like(acc_ref)

        def wgrad_piece(sub):
            rows = slice(sub * npiece, (sub + 1) * npiece)
            acc_ref[rows, :] += _mm_tn(act_ref[:, rows], dx2_ref[...].astype(BF16))

        def block(sub, kv3):
            rows = slice(sub * BLK, (sub + 1) * BLK)
            groups = _attn_heads(jnp.logical_and(t == 0, sub == 0), q_ref[rows, :], kv3[sub * BLK:(sub + 2) * BLK],
                                 bias_ref, sinks_ref)
            o = _unstack_heads([g[5] for g in groups])
            r = lax.rsqrt(jnp.mean(o * o, axis=-1, keepdims=True) + EPS)
            dy = dy_ref[rows, :]
            on = o * r
            dnw_ref[0:1, :] += jnp.sum(dy * on, axis=0, keepdims=True)
            u = dy * nw_ref[...]
            do = r * u - on * (r * jnp.mean(u * on, axis=-1, keepdims=True))
            dq_parts, dk_parts, dv_parts = [], [], []
            for g, (qs, kw, vw, pn, ps, og) in enumerate(groups):
                dos = jnp.concatenate([do[:, (4 * g + r_) * HD:(4 * g + r_ + 1) * HD] for r_ in range(4)], axis=0)
                delta = jnp.sum(dos * og, axis=-1, keepdims=True)
                dp = _mm_nt(dos.astype(BF16), vw)
                ds = pn * (dp - delta)
                dsk = ps * delta
                lane = lax.broadcasted_iota(jnp.int32, (1, 128), 1)
                for r_ in range(4):
                    h = 4 * g + r_
                    dbias_ref[h] += ds[r_ * BLK:(r_ + 1) * BLK]
                    dsink_ref[0:1, :] -= jnp.where(lane == h, jnp.sum(dsk[r_ * BLK:(r_ + 1) * BLK]), 0.0)
                dsb = ds.astype(BF16)
                dq_parts.append(_mm(dsb, kw) * 0.125)
                dk_parts.append(_mm_tn(dsb, qs))
                dv_parts.append(_mm_tn(pn.astype(BF16), dos.astype(BF16)))
            dq_ref[rows, :] = _unstack_heads(dq_parts).astype(BF16)
            return jnp.concatenate(dk_parts + dv_parts, axis=1)

        @pl.when(t < nt)
        def _():
            kv3 = jnp.concatenate([kvp_ref[...], kvc_ref[...]], axis=0)
            tail = carry_ref[...]
            for sub in range(NB):
                d = block(sub, kv3)
                done = tail + d[0:BLK]
                if sub == 0:
                    dkv_ref[0:(NB - 1) * BLK, :] = held_ref[...].astype(BF16)
                    dkv_ref[(NB - 1) * BLK:NB * BLK, :] = done.astype(BF16)
                else:
                    held_ref[(sub - 1) * BLK:sub * BLK, :] = done
                tail = d[BLK:2 * BLK]
                wgrad_piece(sub)
            carry_ref[...] = tail

        @pl.when(t == nt)
        def _():
            dkv_ref[0:(NB - 1) * BLK, :] = held_ref[...].astype(BF16)
            dkv_ref[(NB - 1) * BLK:NB * BLK, :] = carry_ref[...].astype(BF16)
            acc = acc_ref[...]
            dg2_ref[...] = jnp.zeros_like(dg2_ref)
            dg2_ref[0:1, :] = jnp.sum(acc * wdn_ref[...].astype(F32), axis=0, keepdims=True)
            acc_ref[...] = acc * g2_ref[...]
            cp = pltpu.make_async_copy(acc_ref, gdn_hbm, sem)
            cp.start()
            cp.wait()

    last = nt - 1
    tile = lambda w: pl.BlockSpec((NB * BLK, w), lambda t: (jnp.minimum(t, last), 0))
    return _pcall(
        body, [q, kv, kv, dya, bias, sinks, nw, act, dx2, gate2, w_dn], name="attn_bwd", grid=(nt + 1,),
        out_shape=[jax.ShapeDtypeStruct((s, QW), BF16), jax.ShapeDtypeStruct((s, 2 * KVW), BF16),
                   jax.ShapeDtypeStruct((NH, BLK, 2 * BLK), F32), jax.ShapeDtypeStruct((NH, 128), F32),
                   jax.ShapeDtypeStruct((8, QW), F32), jax.ShapeDtypeStruct((DFF, D), F32),
                   jax.ShapeDtypeStruct((8, D), F32)],
        in_specs=[tile(QW),
                  pl.BlockSpec((BLK, 2 * KVW), lambda t: (jnp.clip(NB * t - 1, 0, NB * nt - 1), 0)),
                  tile(2 * KVW), tile(QW),
                  _row((NH, BLK, 2 * BLK)),
                  pl.BlockSpec(memory_space=pltpu.SMEM),
                  _row((1, QW)), tile(DFF), tile(D), _row((1, D)), _row((DFF, D))],
        out_specs=[tile(QW),
                   pl.BlockSpec((NB * BLK, 2 * KVW), lambda t: (jnp.maximum(t - 1, 0), 0)),
                   _row((NH, BLK, 2 * BLK)), _row((NH, 128)), _row((8, QW)),
                   pl.BlockSpec(memory_space=pl.ANY), _row((8, D))],
        scratch_shapes=[pltpu.VMEM((BLK, 2 * KVW), F32), pltpu.VMEM(((NB - 1) * BLK, 2 * KVW), F32),
                        pltpu.VMEM((DFF, D), F32), pltpu.SemaphoreType.DMA], carry=carry)


def _rel_bias_grad(dbias, bucket):
    def body(db_ref, bk_ref, o_ref):
        bk = bk_ref[...]
        lane = lax.broadcasted_iota(jnp.int32, (1, 128), 1)
        for b in range(NBUCKET):
            sel = bk == b
            row = jnp.zeros((1, 128), F32)
            for h in range(NH):
                row = row + jnp.where(lane == h, jnp.sum(jnp.where(sel, db_ref[h], 0.0)), 0.0)
            o_ref[b:b + 1, :] = row

    return pl.pallas_call(
        body, name="rel_bias_grad",
        out_shape=jax.ShapeDtypeStruct((NBUCKET, 128), F32),
    )(dbias, bucket)


def _ssd_consts():
    head_of_lane = np.arange(SW) // HD
    expand = (np.arange(128)[:, None] == head_of_lane[None, :]).astype(np.float32)
    tril = np.tril(np.ones((BLK, BLK), np.float32))
    return (jnp.asarray(expand, BF16), jnp.asarray(expand.T.copy(), BF16), jnp.asarray(tril, BF16),
            jnp.asarray(tril.T.copy(), BF16))


def _conv_pre(xc, halo, cw, cb):
    ext = jnp.concatenate([halo, xc], axis=0)
    taps = [xc if k == CK - 1 else pltpu.roll(ext, CK - 1 - k, 0)[8:8 + BLK] for k in range(CK)]
    return cb + sum(cw[k:k + 1, :] * taps[k] for k in range(CK))


def _ssd_chunk(pre, dtr, dtb, av, dkv, ex, tril, h_in):
    sp = _sig(pre)
    xbc = pre * sp
    xs, bm, cm = xbc[:, 0:SW], xbc[:, SW:SW + 2 * NST], xbc[:, SW + 2 * NST:]
    dtin = dtr + dtb
    dt = jnp.maximum(dtin, 0.0) + jnp.log1p(jnp.exp(-jnp.abs(dtin)))
    cs = _sel_l(tril, dt * av)
    cst = cs.T
    dtx = _sel_r(dt, ex)
    csx = _sel_r(cs, ex)
    xdt = xs * dtx
    csl = csx[BLK - 1:BLK, :]
    decx = jnp.exp(csl - csx)
    ecsx = jnp.exp(csx)
    ecl = jnp.exp(csl)
    causal = tril.astype(F32) > 0.5
    ydiag, yoff, cbs, lms = [], [], [], []
    for g in range(2):
        bg = bm[:, g * NST:(g + 1) * NST].astype(BF16)
        cg = cm[:, g * NST:(g + 1) * NST].astype(BF16)
        cb = _mm_nt(cg, bg)
        cbs.append(cb)
        yoff.append(_mm(cg, h_in[:, g * 256:(g + 1) * 256].astype(BF16)))
        for r in range(4):
            h = 4 * g + r
            seg = cs[:, h:h + 1] - cst[h:h + 1, :]
            lm = jnp.where(causal, jnp.exp(jnp.minimum(seg, 0.0)), 0.0)
            lms.append(lm)
            ydiag.append(_mm((cb * lm).astype(BF16), xdt[:, h * HD:(h + 1) * HD].astype(BF16)))
    yoff = jnp.concatenate(yoff, axis=1) * ecsx
    y = jnp.concatenate(ydiag, axis=1) + yoff + dkv * xs
    return dict(pre=pre, sp=sp, xs=xs, bm=bm, cm=cm, dtin=dtin, dt=dt, av=av, cs=cs, cst=cst,
                dtx=dtx, csx=csx, xdt=xdt, decx=decx, ecsx=ecsx, ecl=ecl, causal=causal, cbs=cbs, lms=lms,
                yoff=yoff, y=y)


def _group_mean(t):
    m0 = jnp.mean(t[:, 0:256], axis=-1, keepdims=True)
    m1 = jnp.mean(t[:, 256:512], axis=-1, keepdims=True)
    return jnp.concatenate([jnp.broadcast_to(m0, (t.shape[0], 256)), jnp.broadcast_to(m1, (t.shape[0], 256))], axis=1)


SUBS = 4


def _ssd_fwd(z, xbc, dtr, cw, cb, dtb, av, dk, nw, carry=None):
    s = z.shape[0]
    nc = s // BLK
    tile = SUBS * BLK
    ex, _, tril, _ = _ssd_consts()

    def body(z_ref, xc_ref, xh_ref, dtr_ref, cw_ref, cb_ref, dtb_ref, a_ref, dk_ref, nw_ref, ex_ref, tril_ref,
             y_ref, hs_ref, pre_ref, h_ref):
        t = pl.program_id(0)

        @pl.when(t == 0)
        def _():
            h_ref[...] = jnp.zeros_like(h_ref)

        h_in = h_ref[...]
        for sub in range(SUBS):
            rows = slice(sub * BLK, (sub + 1) * BLK)
            xc = xc_ref[rows, :]
            halo = jnp.where(t == 0, 0.0, xh_ref[...]) if sub == 0 else xc_ref[sub * BLK - 8:sub * BLK, :]
            pre = _conv_pre(xc, halo, cw_ref[...], cb_ref[...])
            pre_ref[rows, :] = pre
            hs_ref[sub] = h_in
            f = _ssd_chunk(pre, dtr_ref[rows, :], dtb_ref[...], a_ref[...], dk_ref[...], ex_ref[...], tril_ref[...], h_in)
            dx = (f["decx"] * f["xdt"]).astype(BF16)
            st = [_mm_tn(f["bm"][:, g * NST:(g + 1) * NST].astype(BF16), dx[:, g * 256:(g + 1) * 256]) for g in range(2)]
            h_in = h_in * f["ecl"] + jnp.concatenate(st, axis=1)
            zv = z_ref[rows, :]
            tg = f["y"] * (zv * _sig(zv))
            r = lax.rsqrt(_group_mean(tg * tg) + EPS)
            y_ref[rows, :] = (tg * r * nw_ref[...]).astype(BF16)
        h_ref[...] = h_in

    cur = lambda w: pl.BlockSpec((tile, w), lambda t: (t, 0))
    return _pcall(
        body, [z, xbc, xbc, dtr, cw, cb, dtb, av, dk, nw, ex, tril], name="ssd_fwd", grid=(s // tile,),
        out_shape=[jax.ShapeDtypeStruct((s, SW), BF16), jax.ShapeDtypeStruct((nc, NST, SW), F32),
                   jax.ShapeDtypeStruct((s, XBCW), F32)],
        in_specs=[cur(SW), cur(XBCW), pl.BlockSpec((8, XBCW), lambda t: (jnp.maximum(t * (tile // 8) - 1, 0), 0)),
                  cur(128), _row((8, XBCW)), _row((1, XBCW)), _row((1, 128)),
                  _row((1, 128)), _row((1, SW)), _row((1, SW)), _row((128, SW)), _row((BLK, BLK))],
        out_specs=[cur(SW), pl.BlockSpec((SUBS, NST, SW), lambda t: (t, 0, 0)), cur(XBCW)],
        scratch_shapes=[pltpu.VMEM((NST, SW), F32)], carry=carry)


def _ssd_bwd(z, xbc, pre_all, dtr, dys, hs, cw, dtb, av, dk, nw, carry=None):
    s = z.shape[0]
    tile = SUBS * BLK
    nt = s // tile
    ex, ext_t, tril, triu = _ssd_consts()

    def body(z_ref, xc_ref, pre_ref, dtr_ref, dy_ref, hs_ref, cw_ref, dtb_ref, a_ref, dk_ref, nw_ref,
             ex_ref, ext_ref, tril_ref, triu_ref,
             dz_ref, dxbc_ref, ddt_ref, dcw_ref, dcb_ref, dnw_ref, dhd_ref, dh_ref, nxt_ref, dd_ref):
        i = pl.program_id(0)

        @pl.when(i == 0)
        def _():
            dh_ref[...] = jnp.zeros_like(dh_ref)
            nxt_ref[...] = jnp.zeros_like(nxt_ref)
            dd_ref[...] = jnp.zeros_like(dd_ref)
            dcw_ref[...] = jnp.zeros_like(dcw_ref)
            dcb_ref[...] = jnp.zeros_like(dcb_ref)
            dnw_ref[...] = jnp.zeros_like(dnw_ref)
            dhd_ref[...] = jnp.zeros_like(dhd_ref)

        gst, nxt = dh_ref[...], nxt_ref[...]
        for sub in reversed(range(SUBS)):
            rows = slice(sub * BLK, (sub + 1) * BLK)
            gst, nxt = chunk(sub, rows, gst, nxt, z_ref, xc_ref, pre_ref, dtr_ref, dy_ref, hs_ref, cw_ref, dtb_ref,
                             a_ref, dk_ref, nw_ref, ex_ref, ext_ref, tril_ref, triu_ref,
                             dz_ref, dxbc_ref, ddt_ref, dcw_ref, dcb_ref, dnw_ref, dhd_ref, dd_ref)
        dh_ref[...] = gst
        nxt_ref[...] = nxt

        @pl.when(i == nt - 1)
        def _():
            dhd_ref[2:3, :] = _sel_r(dd_ref[...], ext_ref[...])[0:1, :]

    def chunk(sub, rows, gst, nxt, z_ref, xc_ref, pre_ref, dtr_ref, dy_ref, hs_ref, cw_ref, dtb_ref,
              a_ref, dk_ref, nw_ref, ex_ref, ext_ref, tril_ref, triu_ref,
              dz_ref, dxbc_ref, ddt_ref, dcw_ref, dcb_ref, dnw_ref, dhd_ref, dd_ref):
        h_in = hs_ref[sub]
        f = _ssd_chunk(pre_ref[rows, :], dtr_ref[rows, :], dtb_ref[...], a_ref[...], dk_ref[...], ex_ref[...],
                       tril_ref[...], h_in)
        xs, xdt, decx, ecsx, ecl, dtx = f["xs"], f["xdt"], f["decx"], f["ecsx"], f["ecl"], f["dtx"]
        cs, cst, causal = f["cs"], f["cst"], f["causal"]
        causal_t = triu_ref[...].astype(F32) > 0.5

        zv = z_ref[rows, :]
        sz = _sig(zv)
        gz = zv * sz
        t = f["y"] * gz
        r = lax.rsqrt(_group_mean(t * t) + EPS)
        tn_ = t * r
        dyn = dy_ref[rows, :]
        dnw_ref[0:1, :] += jnp.sum(dyn * tn_, axis=0, keepdims=True)
        u = dyn * nw_ref[...]
        dt_ = r * u - tn_ * (r * _group_mean(u * tn_))
        dy = dt_ * gz
        dz_ref[rows, :] = (dt_ * f["y"] * (sz * (1.0 + zv * (1.0 - sz)))).astype(BF16)

        dd_ref[0:1, :] += jnp.sum(dy * xs, axis=0, keepdims=True)
        dxs = dk_ref[...] * dy

        edy = ecsx * dy
        dxdt, dbs, dcs_, dcsx_parts, dh_new = [], [], [], [], []
        lane = lax.broadcasted_iota(jnp.int32, (1, 128), 1)
        dcs_intra = jnp.zeros((BLK, 128), F32)
        for g in range(2):
            sl = slice(g * 256, (g + 1) * 256)
            bgf, cgf = f["bm"][:, g * NST:(g + 1) * NST], f["cm"][:, g * NST:(g + 1) * NST]
            bg, cg = bgf.astype(BF16), cgf.astype(BF16)
            gg = gst[:, sl].astype(BF16)
            hg = h_in[:, sl].astype(BF16)
            edyg = edy[:, sl].astype(BF16)
            dc = _mm_nt(edyg, hg)
            dh_new.append(gst[:, sl] * ecl[:, sl] + _mm_tn(cg, edyg))
            bgm = _mm(bg, gg)
            dxdt_g = decx[:, sl] * bgm
            dxg = (decx[:, sl] * xdt[:, sl]).astype(BF16)
            db = _mm_nt(dxg, gg)
            qd = bgm * xdt[:, sl] * decx[:, sl]
            last = jnp.sum(qd, axis=0, keepdims=True) + ecl[:, sl] * jnp.sum(gst[:, sl] * h_in[:, sl], axis=0, keepdims=True)
            rowid = lax.broadcasted_iota(jnp.int32, (BLK, 256), 0)
            dcsx_parts.append(f["yoff"][:, sl] * dy[:, sl] - qd + jnp.where(rowid == BLK - 1, last, 0.0))
            cb_ = f["cbs"][g]
            cbt = _mm_nt(bg, cg)
            dcb_ = jnp.zeros((BLK, BLK), F32)
            dcbt = jnp.zeros((BLK, BLK), F32)
            dxd = []
            for r_ in range(4):
                h = 4 * g + r_
                hl = slice(h * HD, (h + 1) * HD)
                lm = f["lms"][h]
                segt = cst[h:h + 1, :] - cs[:, h:h + 1]
                lmt = jnp.where(causal_t, jnp.exp(jnp.minimum(segt, 0.0)), 0.0)
                dyh = dy[:, hl].astype(BF16)
                xdh = xdt[:, hl].astype(BF16)
                dw = _mm_nt(dyh, xdh)
                dwt = _mm_nt(xdh, dyh)
                wt = cbt * lmt
                dxd.append(_mm(wt.astype(BF16), dyh))
                dcb_ = dcb_ + dw * lm
                dcbt = dcbt + dwt * lmt
                col = jnp.sum(dw * (cb_ * lm), axis=-1, keepdims=True) - jnp.sum(dwt * wt, axis=-1, keepdims=True)
                dcs_intra = dcs_intra + jnp.where(lane == h, col, 0.0)
            dxdt.append(dxdt_g + jnp.concatenate(dxd, axis=1))
            dcs_.append(dc + _mm(dcb_.astype(BF16), bg))
            dbs.append(db + _mm(dcbt.astype(BF16), cg))
        dxdt = jnp.concatenate(dxdt, axis=1)
        dxs = dxs + dxdt * dtx
        ext_t_ = ext_ref[...]
        dcs = dcs_intra + _sel_r(jnp.concatenate(dcsx_parts, axis=1), ext_t_)
        da = _sel_l(triu_ref[...], dcs)
        ddt = da * f["av"] + _sel_r(dxdt * xs, ext_t_)
        dhd_ref[1:2, :] += jnp.sum(da * f["dt"], axis=0, keepdims=True)
        ddtr = ddt * _sig(f["dtin"])
        dhd_ref[0:1, :] += jnp.sum(ddtr, axis=0, keepdims=True)
        ddt_ref[rows, :] = ddtr.astype(BF16)

        sp, pre = f["sp"], f["pre"]
        dact = jnp.concatenate([dxs] + dbs + dcs_, axis=1)
        dpre = dact * (sp * (1.0 + pre * (1.0 - sp)))
        dcb_ref[0:1, :] += jnp.sum(dpre, axis=0, keepdims=True)
        ext2 = jnp.concatenate([dpre, nxt], axis=0)
        shifted = [pltpu.roll(ext2, BLK + 8 - (CK - 1 - k), 0)[0:BLK] for k in range(CK - 1)] + [dpre]
        cw = cw_ref[...]
        xc = xc_ref[rows, :]
        dxr = cw[CK - 1:CK, :] * dpre
        for k in range(CK):
            dcw_ref[k:k + 1, :] += jnp.sum(shifted[k] * xc, axis=0, keepdims=True)
            if k < CK - 1:
                dxr = dxr + cw[k:k + 1, :] * shifted[k]
        dxbc_ref[rows, :] = dxr.astype(BF16)
        return jnp.concatenate(dh_new, axis=1), dpre[0:8]

    cur = lambda w: pl.BlockSpec((tile, w), lambda i: (nt - 1 - i, 0))
    return _pcall(
        body, [z, xbc, pre_all, dtr, dys, hs, cw, dtb, av, dk, nw, ex, ext_t, tril, triu], name="ssd_bwd", grid=(nt,),
        out_shape=[jax.ShapeDtypeStruct((s, SW), BF16), jax.ShapeDtypeStruct((s, XBCW), BF16),
                   jax.ShapeDtypeStruct((s, 128), BF16), jax.ShapeDtypeStruct((8, XBCW), F32),
                   jax.ShapeDtypeStruct((8, XBCW), F32), jax.ShapeDtypeStruct((8, SW), F32),
                   jax.ShapeDtypeStruct((8, 128), F32)],
        in_specs=[cur(SW), cur(XBCW), cur(XBCW), cur(128), cur(SW),
                  pl.BlockSpec((SUBS, NST, SW), lambda i: (nt - 1 - i, 0, 0)),
                  _row((8, XBCW)), _row((1, 128)), _row((1, 128)), _row((1, SW)), _row((1, SW)),
                  _row((128, SW)), _row((SW, 128)), _row((BLK, BLK)), _row((BLK, BLK))],
        out_specs=[cur(SW), cur(XBCW), cur(128), _row((8, XBCW)), _row((8, XBCW)), _row((8, SW)), _row((8, 128))],
        scratch_shapes=[pltpu.VMEM((NST, SW), F32), pltpu.VMEM((8, XBCW), F32), pltpu.VMEM((8, SW), F32)], carry=carry)


def _load_once(i, pairs, sem):
    @pl.when(i == 0)
    def _():
        cps = [pltpu.make_async_copy(src, dst, sem.at[k]) for k, (src, dst) in enumerate(pairs)]
        for cp in cps:
            cp.start()
        for cp in cps:
            cp.wait()


def _mlp_fwd(x, ya, ys, tgt, w_o, w_ga, w_gb, w_dn, gate1, a2, sh2, gate2, fn):
    s = x.shape[0]
    sub_m, subs = 256, 2
    tm = sub_m * subs

    def body(x_ref, ya_ref, ys_ref, t_ref, wo_hbm, wga_hbm, wgb_hbm, wdn_hbm, g1_ref, a2_ref, s2_ref, g2_ref, fn_ref,
             x1_ref, gu_ref, dx2_ref, loss_ref, dfn_ref, wo, wga, wgb, wdn, sem):
        i = pl.program_id(0)
        _load_once(i, [(wo_hbm, wo), (wga_hbm, wga), (wgb_hbm, wgb), (wdn_hbm, wdn)], sem)

        @pl.when(i == 0)
        def _():
            loss_ref[...] = jnp.zeros_like(loss_ref)
            dfn_ref[...] = jnp.zeros_like(dfn_ref)

        def proj(st):
            st["mix"] = _mm(ya_ref[st["rows"], :], wo[0:QW, :]) + _mm(ys_ref[st["rows"], :], wo[QW:D, :])

        def norm(st):
            x1 = x_ref[st["rows"], :] + g1_ref[...] * st.pop("mix")
            x1_ref[st["rows"], :] = x1
            r2 = lax.rsqrt(jnp.mean(x1 * x1, axis=-1, keepdims=True) + EPS)
            st["x1"] = x1
            st["h2"] = (x1 * r2 * a2_ref[...] + s2_ref[...]).astype(BF16)

        def gate_up(st):
            h2 = st.pop("h2")
            ha, hb = h2[:, 0:D // 2], h2[:, D // 2:D]
            gub = jnp.concatenate([(_mm(ha, wga[j]) + _mm(hb, wgb[j])).astype(BF16) for j in range(4)], axis=1)
            gu_ref[st["rows"], :] = gub
            st["gub"] = gub

        def activate(st):
            gub = st.pop("gub")
            gv, uv = gub[:, 0:DFF].astype(F32), gub[:, DFF:].astype(F32)
            st["act"] = (gv * _sig(gv) * uv).astype(BF16)

        def down(st):
            st["ff"] = _mm(st.pop("act"), wdn[...])

        def head(st):
            x2 = st.pop("x1") + g2_ref[...] * st.pop("ff")
            r3 = lax.rsqrt(jnp.mean(x2 * x2, axis=-1, keepdims=True) + EPS)
            xn = x2 * r3
            fnv = fn_ref[...]
            err = xn * fnv - t_ref[st["rows"], :]
            st["loss"] = jnp.sum(err * err) * (0.5 / D)
            dy = err * (1.0 / D)
            st["dfn"] = jnp.sum(dy * xn, axis=0, keepdims=True)
            u = dy * fnv
            dx2_ref[st["rows"], :] = r3 * u - xn * (r3 * jnp.mean(u * xn, axis=-1, keepdims=True))

        a, b = [dict(rows=slice(k * sub_m, (k + 1) * sub_m)) for k in range(subs)]
        for stage, st in [(proj, a), (norm, a), (proj, b), (gate_up, a), (norm, b), (activate, a), (gate_up, b),
                          (down, a), (activate, b), (head, a), (down, b), (head, b)]:
            stage(st)
        loss_ref[...] += a["loss"] + b["loss"]
        dfn_ref[0:1, :] += a["dfn"] + b["dfn"]

    def tok(w):
        return pl.BlockSpec((tm, w), lambda i: (i, 0))

    hbm = pl.BlockSpec(memory_space=pl.ANY)
    return pl.pallas_call(
        body, name="mlp_fwd", grid=(s // tm,),
        out_shape=[jax.ShapeDtypeStruct((s, D), F32), jax.ShapeDtypeStruct((s, 2 * DFF), BF16),
                   jax.ShapeDtypeStruct((s, D), F32), jax.ShapeDtypeStruct((8, 128), F32),
                   jax.ShapeDtypeStruct((8, D), F32)],
        in_specs=[tok(D), tok(QW), tok(SW), tok(D), hbm, hbm, hbm, hbm,
                  _row((1, D)), _row((1, D)), _row((1, D)), _row((1, D)), _row((1, D))],
        out_specs=[tok(D), tok(2 * DFF), tok(D), _row((8, 128)), _row((8, D))],
        scratch_shapes=[pltpu.VMEM((D, D), BF16), pltpu.VMEM(w_ga.shape, BF16), pltpu.VMEM(w_gb.shape, BF16),
                        pltpu.VMEM((DFF, D), BF16), pltpu.SemaphoreType.DMA((4,))],
        compiler_params=_cp(("arbitrary",)),
    )(x, ya, ys, tgt, w_o, w_ga, w_gb, w_dn, gate1, a2, sh2, gate2, fn)


def _mlp_bwd(x1, gu, dx2, w_o, w_ga, w_gb, w_dn, gate1, a2, sh2, gate2):
    s = x1.shape[0]
    tm = 256
    nj = 2 * DFF // 4

    def body(x1_ref, gu_ref, dx2_ref, wo_hbm, wga_hbm, wgb_hbm, wdn_hbm, g1_ref, a2_ref, s2_ref, g2_ref,
             dx1_ref, dya_ref, dys_ref, act_ref, dgu_ref, h2_ref, dsh_ref, p_ref, wo, wga, wgb, wdn, sem):
        i = pl.program_id(0)
        _load_once(i, [(wo_hbm, wo), (wga_hbm, wga), (wgb_hbm, wgb), (wdn_hbm, wdn)], sem)

        @pl.when(i == 0)
        def _():
            dsh_ref[...] = jnp.zeros_like(dsh_ref)
            p_ref[...] = jnp.zeros_like(p_ref)

        dx2 = dx2_ref[...]
        dact = _mm_nt((dx2 * g2_ref[...]).astype(BF16), wdn[...])
        gub = gu_ref[...]
        gv, uv = gub[:, 0:DFF].astype(F32), gub[:, DFF:].astype(F32)
        sg = _sig(gv)
        sl = gv * sg
        act_ref[...] = (sl * uv).astype(BF16)
        dgu = jnp.concatenate([dact * uv * (sg * (1.0 + gv * (1.0 - sg))), dact * sl], axis=1).astype(BF16)
        dgu_ref[...] = dgu
        dha = sum(_mm_nt(dgu[:, j * nj:(j + 1) * nj], wga[j]) for j in range(4))
        dhb = sum(_mm_nt(dgu[:, j * nj:(j + 1) * nj], wgb[j]) for j in range(4))
        dh = jnp.concatenate([dha, dhb], axis=1)
        x1 = x1_ref[...]
        r2 = lax.rsqrt(jnp.mean(x1 * x1, axis=-1, keepdims=True) + EPS)
        xn = x1 * r2
        a2 = a2_ref[...]
        h2_ref[...] = (xn * a2 + s2_ref[...]).astype(BF16)
        dsh_ref[0:1, :] += jnp.sum(dh, axis=0, keepdims=True)
        p_ref[0:1, :] += jnp.sum(dh * xn, axis=0, keepdims=True)
        u = dh * a2
        dx1 = dx2 + r2 * u - xn * (r2 * jnp.mean(u * xn, axis=-1, keepdims=True))
        dx1_ref[...] = dx1
        dcat = _mm_nt((dx1 * g1_ref[...]).astype(BF16), wo[...])
        dya_ref[...] = dcat[:, 0:QW]
        dys_ref[...] = dcat[:, QW:D]

    def tok(w):
        return pl.BlockSpec((tm, w), lambda i: (i, 0))

    hbm = pl.BlockSpec(memory_space=pl.ANY)
    return pl.pallas_call(
        body, name="mlp_bwd", grid=(s // tm,),
        out_shape=[jax.ShapeDtypeStruct((s, D), F32), jax.ShapeDtypeStruct((s, QW), F32),
                   jax.ShapeDtypeStruct((s, SW), F32), jax.ShapeDtypeStruct((s, DFF), BF16),
                   jax.ShapeDtypeStruct((s, 2 * DFF), BF16), jax.ShapeDtypeStruct((s, D), BF16),
                   jax.ShapeDtypeStruct((8, D), F32), jax.ShapeDtypeStruct((8, D), F32)],
        in_specs=[tok(D), tok(2 * DFF), tok(D), hbm, hbm, hbm, hbm, _row((1, D)), _row((1, D)), _row((1, D)), _row((1, D))],
        out_specs=[tok(D), tok(QW), tok(SW), tok(DFF), tok(2 * DFF), tok(D), _row((8, D)), _row((8, D))],
        scratch_shapes=[pltpu.VMEM((D, D), BF16), pltpu.VMEM(w_ga.shape, BF16), pltpu.VMEM(w_gb.shape, BF16),
                        pltpu.VMEM((DFF, D), BF16), pltpu.SemaphoreType.DMA((4,))],
        compiler_params=_cp(("arbitrary",)),
    )(x1, gu, dx2, w_o, w_ga, w_gb, w_dn, gate1, a2, sh2, gate2)


def _wgrad(name, a, b, gate, w, carry=None):
    s, m = a.shape
    n = b.shape[1]
    tk = min(1024, s)
    nk = s // tk

    def body(a_ref, b_ref, g_ref, w_ref, o_hbm, dg_ref, acc_ref, sem):
        k = pl.program_id(0)

        @pl.when(k == 0)
        def _():
            acc_ref[...] = jnp.zeros_like(acc_ref)

        acc_ref[...] += _mm_tn(a_ref[...], b_ref[...].astype(BF16))

        @pl.when(k == nk - 1)
        def _():
            acc = acc_ref[...]
            dg_ref[...] = jnp.zeros_like(dg_ref)
            dg_ref[0:1, :] = jnp.sum(acc * w_ref[...].astype(F32), axis=0, keepdims=True)
            acc_ref[...] = acc * g_ref[...]
            cp = pltpu.make_async_copy(acc_ref, o_hbm, sem)
            cp.start()
            cp.wait()

    return _pcall(body, [a, b, gate, w], name=name, grid=(nk,),
                  out_shape=[jax.ShapeDtypeStruct((m, n), F32), jax.ShapeDtypeStruct((8, n), F32)],
                  in_specs=[pl.BlockSpec((tk, m), lambda k: (k, 0)), pl.BlockSpec((tk, n), lambda k: (k, 0)),
                            _row((1, n)), _row((m, n))],
                  out_specs=[pl.BlockSpec(memory_space=pl.ANY), _row((8, n))],
                  scratch_shapes=[pltpu.VMEM((m, n), F32), pltpu.SemaphoreType.DMA], carry=carry)


def _wgrad_gate_up(h2, dgu, carry=None):
    s = h2.shape[0]
    tk = min(1024, s)
    nk = s // tk
    n = dgu.shape[1]
    nj = n // 4

    def body(a_ref, b_ref, o_hbm, acc_ref, sems):
        k = pl.program_id(0)

        @pl.when(k == 0)
        def _():
            acc_ref[...] = jnp.zeros_like(acc_ref)

        acc_ref[...] += _mm_tn(a_ref[...], b_ref[...])

        @pl.when(k == nk - 1)
        def _():
            cps = [pltpu.make_async_copy(acc_ref.at[:, pl.ds(j * nj, nj)], o_hbm.at[j], sems.at[j]) for j in range(4)]
            for cp in cps:
                cp.start()
            for cp in cps:
                cp.wait()

    return _pcall(body, [h2, dgu], name="wgrad_gate_up", grid=(nk,),
                  out_shape=[jax.ShapeDtypeStruct((4, D, nj), F32)],
                  in_specs=[pl.BlockSpec((tk, D), lambda k: (k, 0)), pl.BlockSpec((tk, n), lambda k: (k, 0))],
                  out_specs=[pl.BlockSpec(memory_space=pl.ANY)],
                  scratch_shapes=[pltpu.VMEM((D, n), F32), pltpu.SemaphoreType.DMA((4,))], carry=carry)


def _wgrad_in_t(h1, pieces, carry=None):
    s = h1.shape[0]
    tk = min(1024, s)
    nk = s // tk

    def body(a_ref, dq_ref, dkv_ref, dz_ref, dxbc_ref, ddt_ref, o_hbm, acc_ref, tr_ref, sem):
        k = pl.program_id(0)

        @pl.when(k == 0)
        def _():
            acc_ref[...] = jnp.zeros_like(acc_ref)

        dproj = jnp.concatenate([dq_ref[...], dkv_ref[...], dz_ref[...], dxbc_ref[...], ddt_ref[...]], axis=1)
        acc_ref[...] += _mm_tn(a_ref[...], dproj)

        @pl.when(k == nk - 1)
        def _():
            for j in range(PROJ_W // 128):
                tr_ref[j * 128:(j + 1) * 128, :] = acc_ref[:, j * 128:(j + 1) * 128].T
            cp = pltpu.make_async_copy(tr_ref, o_hbm, sem)
            cp.start()
            cp.wait()

    return _pcall(body, [h1] + list(pieces), name="wgrad_in", grid=(nk,),
                  out_shape=[jax.ShapeDtypeStruct((PROJ_W, D), F32)],
                  in_specs=[pl.BlockSpec((tk, p.shape[1]), lambda k: (k, 0)) for p in [h1] + list(pieces)],
                  out_specs=[pl.BlockSpec(memory_space=pl.ANY)],
                  scratch_shapes=[pltpu.VMEM((D, PROJ_W), F32), pltpu.VMEM((PROJ_W, D), F32), pltpu.SemaphoreType.DMA],
                  carry=carry)


_SMALL = ["ada_b", "norm1", "conv_w", "conv_b", "dt_bias", "A_log", "D_skip", "sinks", "attn_out_norm",
          "ssm_out_norm", "norm2", "rel_bias", "final_norm"]


def _small_grad(name, gs, chip):
    if name == "ada_b":
        return jnp.concatenate([gs[j:j + 1, :] for j in range(6)], axis=1)
    if name == "conv_w":
        full = gs[7:11, :]
        out = full[:, 0:256]
        for j in range(1, 4):
            out = jnp.where(chip == j, full[:, j * 256:(j + 1) * 256], out)
        return out
    row, width = {"norm1": (6, D), "conv_b": (11, D), "norm2": (12, D), "final_norm": (13, D),
                  "attn_out_norm": (14, QW), "ssm_out_norm": (15, SW), "dt_bias": (16, NH), "A_log": (17, NH),
                  "D_skip": (18, NH), "sinks": (19, NH), "rel_bias": (24, NH)}[name]
    rows = NBUCKET if name == "rel_bias" else 1
    return gs[row:row + rows, 0:width]


def _small_update(small_all, where, ws, ms, vs):
    n = len(_SMALL)

    def body(where_ref, sa_ref, *refs):
        w_refs, m_refs, v_refs, outs = refs[:n], refs[n:2 * n], refs[2 * n:3 * n], refs[3 * n:]
        gs = sa_ref[0]
        for b in range(1, 8):
            gs = gs + sa_ref[b]
        chip = where_ref[1]
        for i, name in enumerate(_SMALL):
            g = _small_grad(name, gs, chip)
            lead = (0,) if name == "conv_w" else ()
            d, mo, vo = _adamw(w_refs[i][lead + (...,)], g, m_refs[i][lead + (...,)], v_refs[i][lead + (...,)])
            for k, val in enumerate((g, d, mo, vo)):
                outs[k * n + i][lead + (...,)] = val
        outs[4 * n][...] = gs[20:21, 0:128]

    shapes = [jax.ShapeDtypeStruct(w.shape, F32) for w in ws]
    vmem = pl.BlockSpec(memory_space=pltpu.VMEM)
    res = pl.pallas_call(
        body, name="small_update", out_shape=shapes * 4 + [jax.ShapeDtypeStruct((1, 128), F32)],
        in_specs=[pl.BlockSpec(memory_space=pltpu.SMEM)] + [vmem] * (1 + 3 * n), out_specs=[vmem] * (4 * n + 1),
    )(where, small_all, *ws, *ms, *vs)
    return [res[k * n:(k + 1) * n] for k in range(4)], res[4 * n][0, 0]


def _add_half(name, g, got, where, by_cols=False):
    rr, cc = got.shape[1:]
    if by_cols:
        mine = pl.BlockSpec((None, rr, cc), lambda i, w_ref: (i, 0, w_ref[0]))
    else:
        mine = pl.BlockSpec((None, None, rr, cc), lambda i, w_ref: (i, w_ref[0], 0, 0))

    def body(w_ref, g_ref, r_ref, o_ref, own_ref):
        s = g_ref[...] + r_ref[...]
        o_ref[...] = s.astype(BF16)

        @pl.when(pl.program_id(0) == w_ref[1])
        def _():
            own_ref[...] = s

    spec = pl.BlockSpec((None, rr, cc), lambda i, w_ref: (i, 0, 0))
    return _pcall(body, [where, g, got], name=name, grid=(4,), nprefetch=1,
                  out_shape=[jax.ShapeDtypeStruct(got.shape, BF16), jax.ShapeDtypeStruct((rr, cc), F32)],
                  in_specs=[mine, spec],
                  out_specs=[spec, pl.BlockSpec((rr, cc), lambda i, w_ref: (0, 0))])


def _add_chips(name, own, got):
    rr, cc = own.shape
    tr = rr // 2 if rr % 32 == 0 else rr

    def body(s_ref, r_ref, o_ref):
        o_ref[...] = ((s_ref[...] + r_ref[0].astype(F32)) + r_ref[1].astype(F32)) + r_ref[2].astype(F32)

    spec = pl.BlockSpec((tr, cc), lambda i: (i, 0))
    return _pcall(body, [own, got], name=name, grid=(rr // tr,), out_shape=[jax.ShapeDtypeStruct((rr, cc), F32)],
                  in_specs=[spec, pl.BlockSpec((3, tr, cc), lambda i: (0, i, 0))], out_specs=[spec])[0]


def _adamw_halves(name, mine, got, w, m, v, where, by_cols=False):
    rr, cc = mine.shape

    def body(w_ref_, t_ref, r_ref, w_ref, m_ref, v_ref, g_ref, d_ref, mo_ref, vo_ref):
        g = jnp.where(pl.program_id(0) == w_ref_[0], t_ref[...], r_ref[...])
        g_ref[...] = g
        d_ref[...], mo_ref[...], vo_ref[...] = _adamw(w_ref[...], g, m_ref[...], v_ref[...])

    if by_cols:
        grid = (2, 1)
        half = pl.BlockSpec((rr, cc), lambda h, i, w_ref_: (0, 0))
        full = pl.BlockSpec((rr, cc), lambda h, i, w_ref_: (0, h))
    else:
        tr = rr // 2
        grid = (2, 2)
        half = pl.BlockSpec((tr, cc), lambda h, i, w_ref_: (i, 0))
        full = pl.BlockSpec((None, tr, cc), lambda h, i, w_ref_: (0, 2 * h + i, 0))
    return _pcall(body, [where, mine, got, w, m, v], name=name, grid=grid, nprefetch=1,
                  out_shape=[jax.ShapeDtypeStruct(w.shape, F32)] * 4,
                  in_specs=[half, half, full, full, full], out_specs=[full] * 4)


def _bias_table(rel_bias, bucket, mask):
    def body(rb_ref, bk_ref, mk_ref, o_ref):
        bk = bk_ref[...]
        valid = mk_ref[...] > 0
        for h in range(NH):
            acc = jnp.zeros((BLK, 2 * BLK), F32)
            for b in range(NBUCKET):
                acc = jnp.where(bk == b, rb_ref[b, h], acc)
            o_ref[h] = jnp.where(valid, acc, NEG)

    vmem = pl.BlockSpec(memory_space=pltpu.VMEM)
    return pl.pallas_call(
        body, name="bias_table", out_shape=jax.ShapeDtypeStruct((NH, BLK, 2 * BLK), F32),
        in_specs=[pl.BlockSpec(memory_space=pltpu.SMEM), vmem, vmem], out_specs=vmem,
    )(rel_bias, bucket, mask)


def _pack_small(dsh1, p1, dsh2, p2, dg1a, dg1b, dg2, norm1, norm2, scale1, scale2, dcw, dcb, dfn,
                dnw_attn, dnw_ssm, dhd, av, dsink, drel, loss_acc):
    def body(dsh1_ref, p1_ref, dsh2_ref, p2_ref, dg1a_ref, dg1b_ref, dg2_ref, n1_ref, n2_ref, s1_ref, s2_ref,
             dcw_ref, dcb_ref, dfn_ref, da_ref, ds_ref, dhd_ref, av_ref, dsink_ref, drel_ref, loss_ref, o_ref):
        o_ref[...] = jnp.zeros_like(o_ref)
        p1v, p2v = p1_ref[0:1, :], p2_ref[0:1, :]
        o_ref[0:1, :] = dsh1_ref[0:1, :]
        o_ref[1:2, :] = p1v * n1_ref[...]
        o_ref[2:3, :] = dg1a_ref[0:1, :] + dg1b_ref[0:1, :]
        o_ref[3:4, :] = dsh2_ref[0:1, :]
        o_ref[4:5, :] = p2v * n2_ref[...]
        o_ref[5:6, :] = dg2_ref[0:1, :]
        o_ref[6:7, :] = p1v * (1.0 + s1_ref[...])
        o_ref[7:11, :] = dcw_ref[0:4, :]
        o_ref[11:12, :] = dcb_ref[0:1, :]
        o_ref[12:13, :] = p2v * (1.0 + s2_ref[...])
        o_ref[13:14, :] = dfn_ref[0:1, :]
        o_ref[14:15, 0:QW] = da_ref[0:1, :]
        o_ref[15:16, 0:SW] = ds_ref[0:1, :]
        o_ref[16:17, 0:128] = dhd_ref[0:1, :]
        o_ref[17:18, 0:128] = dhd_ref[1:2, :] * av_ref[...]
        o_ref[18:19, 0:128] = dhd_ref[2:3, :]
        o_ref[19:20, 0:128] = dsink_ref[0:1, :]
        o_ref[20:21, 0:128] = loss_ref[0:1, :]
        o_ref[24:56, 0:128] = drel_ref[...]

    return pl.pallas_call(body, name="pack_small", out_shape=jax.ShapeDtypeStruct((56, D), F32))(
        dsh1, p1, dsh2, p2, dg1a, dg1b, dg2, norm1, norm2, scale1, scale2, dcw, dcb, dfn,
        dnw_attn, dnw_ssm, dhd, av, dsink, drel, loss_acc)


def _pad_row(a, rows=1):
    return jnp.pad(a.reshape(rows, -1), ((0, 0), (0, D - a.size // rows)))


def kernel(x, c, ada_w, ada_b, norm1, w_in, conv_w, conv_b, dt_bias, A_log, D_skip, sinks, attn_out_norm, ssm_out_norm, w_o, norm2, w_gate_up, w_down, rel_bias, final_norm, loss_target, m_ada_w, m_ada_b, m_norm1, m_w_in, m_conv_w, m_conv_b, m_dt_bias, m_A_log, m_D_skip, m_sinks, m_attn_out_norm, m_ssm_out_norm, m_w_o, m_norm2, m_w_gate_up, m_w_down, m_rel_bias, m_final_norm, v_ada_w, v_ada_b, v_norm1, v_w_in, v_conv_w, v_conv_b, v_dt_bias, v_A_log, v_D_skip, v_sinks, v_attn_out_norm, v_ssm_out_norm, v_w_o, v_norm2, v_w_gate_up, v_w_down, v_rel_bias, v_final_norm):
    xi, yi, ci = lax.axis_index("x"), lax.axis_index("y"), lax.axis_index("c")
    chip = 2 * xi + yi
    me = 4 * xi + 2 * yi + ci
    where = jnp.stack([ci, chip]).astype(jnp.int32)
    xs2, tgt = x[0], loss_target[0]

    first = jnp.concatenate([c, _pad_row(conv_w[0], CK), jnp.zeros((3, D), F32)], axis=0)
    w_in_t, m_w_in_t, v_w_in_t = w_in[0].T, m_w_in[0].T, v_w_in[0].T
    w_in_b, w_o_b, w_dn_b = w_in_t.astype(BF16), w_o[0].astype(BF16), w_down[0].astype(BF16)
    w_gu_b = w_gate_up[0].astype(BF16)
    ncol = ada_w.shape[2]
    first_all, w_half, mod_all, w_other = _front(first, w_in_b, ada_w[0],
                                                 lax.dynamic_slice(ada_b, (0, chip * ncol), (1, ncol)))
    c_all = first_all[:, 0, :]
    cw_full = jnp.concatenate([first_all[2 * j, 1:1 + CK, 0:256] for j in range(4)], axis=1)
    w_lo = jnp.where(ci == 0, w_half, w_other)
    w_hi = jnp.where(ci == 0, w_other, w_half)
    w_in_f = jnp.pad(jnp.concatenate([w_lo, w_hi], axis=2).reshape(IN_W, D), ((0, PROJ_W - IN_W), (0, 0)))
    mod = lax.dynamic_slice(jnp.transpose(mod_all, (1, 0, 2)).reshape(8, 4 * ncol), (me, 0), (1, 4 * ncol))
    shift1, scale1, gate1, shift2, scale2, gate2 = [mod[:, j * D:(j + 1) * D] for j in range(6)]
    a1 = norm1 * (1.0 + scale1)
    a2 = norm2 * (1.0 + scale2)

    hdn = DFF // 8
    q, kv, z, xbc, dtr, w_o_g, w_dna_g = _in_proj_fwd(xs2, a1, shift1, w_in_f,
                                                      carry=_gather_chips_carry([w_o_b, w_dn_b[0:hdn]]))
    w_o_f = w_o_g.reshape(D, D)
    bucket, mask = _attn_geometry()
    bucket = jnp.asarray(bucket)
    bias = _bias_table(rel_bias, bucket, jnp.asarray(mask.astype(np.int32)))
    sinks1 = sinks[0]
    ya, w_ga_g = _attn_fwd(q, kv, bias, sinks1, attn_out_norm, carry=_gather_chips_carry([w_gu_b[0:D // 2]]))
    cw8 = jnp.concatenate([cw_full, jnp.zeros((4, XBCW), F32)], axis=0)
    dtb = _pad_row(dt_bias)[:, 0:128]
    av = _pad_row(-jnp.exp(A_log))[:, 0:128]
    dk = jnp.repeat(D_skip, HD, axis=1)
    ys, hs, pre, w_gb_g, w_dnb_g = _ssd_fwd(z, xbc, dtr, cw8, conv_b, dtb, av, dk, ssm_out_norm,
                                            carry=_gather_chips_carry([w_gu_b[D // 2:D], w_dn_b[hdn:2 * hdn]]))
    w_dn_f = jnp.stack([w_dna_g, w_dnb_g], axis=1).reshape(DFF, D)
    fn = final_norm[None, :]
    x1, gu, dx2, loss_acc, dfn = _mlp_fwd(xs2, ya, ys, tgt, w_o_f, w_ga_g, w_gb_g, w_dn_f, gate1, a2, shift2, gate2, fn)

    def to_sibling(p):
        return _Carry([p], [jax.ShapeDtypeStruct((4,) + p.shape[2:], F32)],
                      lambda x_, y_, c_: [(_SIBLING, 0, (j, 1 - c_), 0, j) for j in range(4)])

    def to_chips(s4):
        return _Carry([s4], [jax.ShapeDtypeStruct((3,) + s4.shape[1:], s4.dtype)],
                      lambda x_, y_, c_: [(f, 0, jnp.bitwise_xor(2 * x_ + y_, k + 1), 0, k) for k, f in enumerate(_CHIPS3)])

    def back(t):
        return _Carry([t[None]], [jax.ShapeDtypeStruct((1,) + t.shape, F32)], lambda x_, y_, c_: [(_SIBLING, 0, 0, 0, 0)])

    dx1, dya, dys, act, dgu, h2, dsh2, p2 = _mlp_bwd(x1, gu, dx2, w_o_f, w_ga_g, w_gb_g, w_dn_f, gate1, a2, shift2, gate2)
    p_gu = _wgrad_gate_up(h2, dgu)[0].reshape(4, 2, D // 2, 2 * DFF // 4)
    dq, dkv, dbias, dsink, dnw_attn, g_dn, dg2, got1_gu = _attn_bwd(
        q, kv, dya, bias, sinks1, attn_out_norm, act, dx2, gate2, w_dn_f, carry=to_sibling(p_gu))
    p_dn = g_dn.reshape(4, 2, DFF // 8, D)
    drel = _rel_bias_grad(dbias, bucket)
    s4_gu, own_gu = _add_half("rs_add_half_gu", p_gu, got1_gu, where)
    dz, dxbc, ddt, dcw, dcb, dnw_ssm, dhd, got2_gu, got1_dn = _ssd_bwd(
        z, xbc, pre, dtr, dys, hs, cw8, dtb, av, dk, ssm_out_norm, carry=_merge(to_chips(s4_gu), to_sibling(p_dn)))
    mine_gu = _add_chips("rs_add_chips_gu", own_gu, got2_gu)
    s4_dn, own_dn = _add_half("rs_add_half_dn", p_dn, got1_dn, where)
    grad_x, h1, dsh1, p1 = _in_proj_bwd(xs2, dx1, a1, shift1, w_in_f, dq, dkv, dz, dxbc, ddt)
    g_in_t, got2_dn, got3_gu = _wgrad_in_t(h1, [dq, dkv, dz, dxbc, ddt],
                                           carry=_merge(to_chips(s4_dn), back(mine_gu)))
    mine_dn = _add_chips("rs_add_chips_dn", own_dn, got2_dn)
    p_in = g_in_t[0:IN_W].reshape(4, IN_W // 4, D)

    def to_sibling_cols(p):
        return _Carry([p], [jax.ShapeDtypeStruct(p.shape[:2] + (D // 2,), F32)],
                      lambda x_, y_, c_: [(_SIBLING, 0, (j, slice(None), pl.ds((1 - c_) * (D // 2), D // 2)), 0, j)
                                          for j in range(4)])

    no_dg1 = jnp.zeros((8, D), F32)
    small = _pack_small(dsh1, p1, dsh2, p2, no_dg1, no_dg1, dg2, norm1, norm2, scale1, scale2, dcw, dcb, dfn,
                        dnw_attn, dnw_ssm, dhd, av, dsink, drel, loss_acc)
    g_oa, dg1a, got1_in, got3_dn, small_all = _wgrad(
        "wgrad_o_attn", ya, dx1, gate1, w_o_f[0:QW],
        carry=_merge(to_sibling_cols(p_in), back(mine_dn), _gather8_carry(small)))
    s4_in, own_in = _add_half("rs_add_half_in", p_in, got1_in, where, by_cols=True)
    g_os, dg1b, got2_in = _wgrad("wgrad_o_ssm", ys, dx1, gate1, w_o_f[QW:D], carry=to_chips(s4_in))
    mine_in = _add_chips("rs_add_chips_in", own_in, got2_in)

    dg1_all, got3_in, mine_o, got3_o = _tail(g_oa, g_os, dg1a, dg1b, mine_in)
    small_all = small_all.at[:, 2, :].set(dg1_all[:, 0, :])
    small_res, loss = _small_update(
        small_all, where,
        [ada_b, norm1, conv_w, conv_b, dt_bias, A_log, D_skip, sinks, attn_out_norm, ssm_out_norm, norm2, rel_bias,
         final_norm[None, :]],
        [m_ada_b, m_norm1, m_conv_w, m_conv_b, m_dt_bias, m_A_log, m_D_skip, m_sinks, m_attn_out_norm,
         m_ssm_out_norm, m_norm2, m_rel_bias, m_final_norm[None, :]],
        [v_ada_b, v_norm1, v_conv_w, v_conv_b, v_dt_bias, v_A_log, v_D_skip, v_sinks, v_attn_out_norm,
         v_ssm_out_norm, v_norm2, v_rel_bias, v_final_norm[None, :]])
    small_out = [dict(zip(_SMALL, r)) for r in small_res]
    for r in small_out:
        r["final_norm"] = r["final_norm"][0]

    dmod_all = small_all[:, 0:6, :].reshape(8, 6 * D)
    dmod_loc = lax.dynamic_slice(dmod_all, (0, chip * ncol), (8, ncol))
    ada_out = _ada_bwd_adamw(c_all.T, dmod_loc, ada_w[0], m_ada_w[0], v_ada_w[0])

    big_gu = _adamw_halves("adamw_gate_up", mine_gu, got3_gu[0], w_gate_up, m_w_gate_up, v_w_gate_up, where)
    big_dn = _adamw_halves("adamw_down", mine_dn, got3_dn[0], w_down, m_w_down, v_w_down, where)
    big_o = _adamw_halves("adamw_o", mine_o, got3_o, w_o, m_w_o, v_w_o, where)
    big_in = [o.T[None] for o in _adamw_halves("adamw_in", mine_in, got3_in, w_in_t, m_w_in_t, v_w_in_t, where,
                                               by_cols=True)]
    big = [big_in, big_o, big_gu, big_dn]

    order = ["ada_w", "ada_b", "norm1", "w_in", "conv_w", "conv_b", "dt_bias", "A_log", "D_skip", "sinks",
             "attn_out_norm", "ssm_out_norm", "w_o", "norm2", "w_gate_up", "w_down", "rel_bias", "final_norm"]
    bigname = {"w_in": 0, "w_o": 1, "w_gate_up": 2, "w_down": 3}
    res = [loss, grad_x[None]]
    for kind in range(4):
        for nm in order:
            if nm == "ada_w":
                res.append(ada_out[kind][None])
            elif nm in bigname:
                res.append(big[bigname[nm]][kind])
            else:
                res.append(small_out[kind][nm])
    return tuple(res)
```

```python
import numpy as np
import jax
import jax.numpy as jnp
from jax import lax
from jax.experimental import pallas as pl
from jax.experimental.pallas import tpu as pltpu

F32, BF16 = jnp.float32, jnp.bfloat16
HI = lax.Precision.HIGHEST

D = 1024
QW, KVW = 512, 128
NH, HD, NKV = 8, 64, 2
SW = 512
NST = 128
XBCW = 1024
CK = 4
BLK = 128
DFF = 2816
IN_W = 2312
PROJ_W = 2432
EPS = 1e-6
NEG = -1e30
NBUCKET = 32

B1, B2, LR, AEPS, WD, STEP = 0.9, 0.999, 0.001, 1e-08, 0.01, 10

VMEM_LIMIT = 56 * 1024 * 1024

_NT = (((1,), (1,)), ((), ()))
_TN = (((0,), (0,)), ((), ()))


def _mm(a, b):
    return jnp.dot(a, b, preferred_element_type=F32)


def _mm_nt(a, b):
    return lax.dot_general(a, b, _NT, preferred_element_type=F32)


def _mm_tn(a, b):
    return lax.dot_general(a, b, _TN, preferred_element_type=F32)


def _mm_hi(a, b):
    return jnp.dot(a, b, preferred_element_type=F32, precision=HI)


def _split3(x):
    hi = x.astype(BF16)
    r = x - hi.astype(F32)
    mid = r.astype(BF16)
    lo = (r - mid.astype(F32)).astype(BF16)
    return hi, mid, lo


def _sel_r(x, e):
    hi, mid, lo = _split3(x)
    return (_mm(hi, e) + _mm(mid, e)) + _mm(lo, e)


def _sel_l(e, x):
    hi, mid, lo = _split3(x)
    return (_mm(e, hi) + _mm(e, mid)) + _mm(e, lo)


def _sig(x):
    return 1.0 / (1.0 + jnp.exp(-x))


def _cp(sem):
    return pltpu.CompilerParams(dimension_semantics=sem, vmem_limit_bytes=VMEM_LIMIT)


def _row(shape):
    nd = len(shape)
    return pl.BlockSpec(shape, lambda *_: (0,) * nd)


def _adamw(w, g, m, v):
    m = B1 * m + (1.0 - B1) * g
    v = B2 * v + (1.0 - B2) * (g * g)
    m_hat = m / (1.0 - B1 ** STEP)
    v_hat = v / (1.0 - B2 ** STEP)
    delta = -LR * (m_hat / (jnp.sqrt(v_hat) + AEPS) + WD * w)
    return delta, m, v


class _Carry:
    def __init__(self, inps, outs, copies):
        self.inps, self.outs, self.copies = list(inps), list(outs), copies
        self.n = len(copies(0, 0, 0))

    def descriptors(self, in_refs, out_refs, send_sems, recv_sems):
        x, y, c = lax.axis_index("x"), lax.axis_index("y"), lax.axis_index("c")
        out = []
        for j, (flip, a, si, o, di) in enumerate(self.copies(x, y, c)):
            if flip is None:
                out.append(pltpu.make_async_copy(in_refs[a].at[si], out_refs[o].at[di], send_sems.at[j]))
            else:
                fx, fy, fc = flip
                peer = (1 - x if fx else x, 1 - y if fy else y, 1 - c if fc else c)
                out.append(pltpu.make_async_remote_copy(
                    src_ref=in_refs[a].at[si], dst_ref=out_refs[o].at[di],
                    send_sem=send_sems.at[j], recv_sem=recv_sems.at[j],
                    device_id=peer, device_id_type=pl.DeviceIdType.MESH))
        return out


def _pcall(body, args, *, name, grid, in_specs, out_specs, out_shape, scratch_shapes=(), sem=None, nprefetch=0,
           carry=None):
    out_shape, out_specs = list(out_shape), list(out_specs)
    in_specs, scratch_shapes = list(in_specs), list(scratch_shapes)
    nin, nout, nscr = len(in_specs), len(out_shape), len(scratch_shapes)
    run = body
    if carry is not None:
        ncin, ncout = len(carry.inps), len(carry.outs)
        hbm = pl.BlockSpec(memory_space=pl.ANY)

        def run(*refs):
            pre, r = refs[:nprefetch], refs[nprefetch:]
            ins, cins = r[:nin], r[nin:nin + ncin]
            r = r[nin + ncin:]
            outs, couts = r[:nout], r[nout:nout + ncout]
            r = r[nout + ncout:]
            scr, (send_sems, recv_sems) = r[:nscr], r[nscr:]
            first = pl.program_id(0) == 0
            last = pl.program_id(0) == grid[0] - 1
            for ax in range(1, len(grid)):
                first = jnp.logical_and(first, pl.program_id(ax) == 0)
                last = jnp.logical_and(last, pl.program_id(ax) == grid[ax] - 1)

            @pl.when(first)
            def _():
                for d in carry.descriptors(cins, couts, send_sems, recv_sems):
                    d.start()

            body(*pre, *ins, *outs, *scr)

            @pl.when(last)
            def _():
                for d in carry.descriptors(cins, couts, send_sems, recv_sems):
                    d.wait()

        in_specs = in_specs + [hbm] * ncin
        out_specs = out_specs + [hbm] * ncout
        out_shape = out_shape + carry.outs
        scratch_shapes = scratch_shapes + [pltpu.SemaphoreType.DMA((carry.n,)), pltpu.SemaphoreType.DMA((carry.n,))]
        args = list(args) + carry.inps
    if sem is None:
        sem = ("arbitrary",) * len(grid)
    if nprefetch:
        kw = dict(grid_spec=pltpu.PrefetchScalarGridSpec(num_scalar_prefetch=nprefetch, grid=grid, in_specs=in_specs,
                                                         out_specs=out_specs, scratch_shapes=scratch_shapes))
    else:
        kw = dict(grid=grid, in_specs=in_specs, out_specs=out_specs, scratch_shapes=scratch_shapes)
    res = pl.pallas_call(run, name=name, out_shape=out_shape, compiler_params=_cp(sem), **kw)(*args)
    return list(res)


def _merge(*carries):
    inps, outs, offs = [], [], []
    for cr in carries:
        offs.append((len(inps), len(outs)))
        inps += cr.inps
        outs += cr.outs

    def copies(x, y, c):
        return [(f, a + io, si, o + oo, di) for cr, (io, oo) in zip(carries, offs) for f, a, si, o, di in cr.copies(x, y, c)]

    return _Carry(inps, outs, copies)


_ALL7 = [(f >> 2 & 1, f >> 1 & 1, f & 1) for f in range(1, 8)]
_CHIPS3 = [(0, 1, 0), (1, 0, 0), (1, 1, 0)]
_SIBLING = (0, 0, 1)


def _gather8_carry(blk):
    def copies(x, y, c):
        me = 4 * x + 2 * y + c
        return [(None, 0, 0, 0, me)] + [(f, 0, 0, 0, me) for f in _ALL7]

    return _Carry([blk[None]], [jax.ShapeDtypeStruct((8,) + blk.shape, blk.dtype)], copies)


def _gather_chips_carry(blks):
    def copies(x, y, c):
        chip = 2 * x + y
        return [(f, a, 0, a, chip) for a in range(len(blks)) for f in [None] + _CHIPS3]

    return _Carry([b[None] for b in blks], [jax.ShapeDtypeStruct((4,) + b.shape, b.dtype) for b in blks], copies)


def _front(first, w_in_t, w_loc, b_loc):
    n = w_loc.shape[1]
    rows = w_in_t.shape[0]
    hw = D // 2
    gather = _gather8_carry(first)
    fetch = _Carry([w_in_t], [jax.ShapeDtypeStruct((4, rows, hw), BF16)],
                   lambda x_, y_, c_: [(f, 0, (slice(None), pl.ds(c_ * hw, hw)), 0, 2 * x_ + y_) for f in [None] + _CHIPS3])
    phase_a = _merge(gather, fetch)
    send_mod = _Carry([None], [None], lambda x_, y_, c_: [(f, 0, slice(None), 0, 2 * x_ + y_) for f in [None] + _CHIPS3])
    swap = _Carry([None], [None], lambda x_, y_, c_: [(_SIBLING, 0, slice(None), 0, slice(None))])

    def body(first_hbm, w_hbm, wl_ref, bl_ref, first_all, w_half, mod_all, w_other,
             c_scr, mod_scr, sa, ra, sb, rb, sc, rc, sl):
        da = phase_a.descriptors([first_hbm, w_hbm], [first_all, w_half], sa, ra)
        for d in da:
            d.start()
        for d in da[:gather.n]:
            d.wait()
        cp = pltpu.make_async_copy(first_all, c_scr, sl)
        cp.start()
        cp.wait()
        cv = c_scr[:, 0, :]
        cond = cv * _sig(cv)
        for j in range(n // 512):
            cols = slice(j * 512, (j + 1) * 512)
            mod_scr[:, cols] = _mm_hi(cond, wl_ref[:, cols]) + bl_ref[:, cols]
        db = send_mod.descriptors([mod_scr], [mod_all], sb, rb)
        for d in db:
            d.start()
        for d in da[gather.n:]:
            d.wait()
        dc = swap.descriptors([w_half], [w_other], sc, rc)
        for d in dc:
            d.start()
        for d in db + dc:
            d.wait()

    hbm = pl.BlockSpec(memory_space=pl.ANY)
    vmem = pl.BlockSpec(memory_space=pltpu.VMEM)
    sems = pltpu.SemaphoreType.DMA
    return pl.pallas_call(
        body, name="front",
        out_shape=[jax.ShapeDtypeStruct((8,) + first.shape, F32), jax.ShapeDtypeStruct((4, rows, hw), BF16),
                   jax.ShapeDtypeStruct((4, 8, n), F32), jax.ShapeDtypeStruct((4, rows, hw), BF16)],
        in_specs=[hbm, hbm, vmem, vmem], out_specs=[hbm] * 4,
        scratch_shapes=[pltpu.VMEM((8,) + first.shape, F32), pltpu.VMEM((8, n), F32),
                        sems((phase_a.n,)), sems((phase_a.n,)), sems((4,)), sems((4,)), sems((1,)), sems((1,)), sems],
        compiler_params=pltpu.CompilerParams(vmem_limit_bytes=VMEM_LIMIT),
    )(first[None], w_in_t, w_loc, b_loc)


def _tail(g_a, g_b, row_a, row_b, mine_in):
    rr, cc = g_a.shape[0] // 4, g_a.shape[1]

    def copies_a(x_, y_, c_):
        me = 4 * x_ + 2 * y_ + c_
        out = [(None, 0, 0, 0, me)] + [(f, 0, 0, 0, me) for f in _ALL7]
        out += [(_SIBLING, 1 + j // 2, pl.ds((j % 2) * 2 * rr + (1 - c_) * rr, rr), 1, j) for j in range(4)]
        out += [(None, 1 + j // 2, pl.ds((j % 2) * 2 * rr + c_ * rr, rr), 2, j) for j in range(4)]
        return out + [(_SIBLING, 3, slice(None), 3, slice(None))]

    phase_a = _Carry([None] * 4, [None] * 4, copies_a)
    to_chips = _Carry([None], [None], lambda x_, y_, c_: [(f, 0, jnp.bitwise_xor(2 * x_ + y_, k + 1), 0, k)
                                                        for k, f in enumerate(_CHIPS3)])
    back = _Carry([None], [None] * 2, lambda x_, y_, c_: [(None, 0, slice(None), 0, slice(None)),
                                                        (_SIBLING, 0, slice(None), 1, slice(None))])

    def body(ga_hbm, gb_hbm, ra_ref, rb_ref, in_hbm, rows_all, got_in, mine_o, got_o,
             row_scr, got1_scr, mine_scr, s4_scr, got2_scr, red_scr, sa, ra, sb, rb, sc, rc):
        chip = 2 * lax.axis_index("x") + lax.axis_index("y")
        row_scr[0] = ra_ref[0:1, :] + rb_ref[0:1, :]
        da = phase_a.descriptors([row_scr, ga_hbm, gb_hbm, in_hbm], [rows_all, got1_scr, mine_scr, got_in], sa, ra)
        for d in da:
            d.start()
        for d in da[8:16]:
            d.wait()
        for j in range(4):
            s4_scr[j] = (mine_scr[j] + got1_scr[j]).astype(BF16)
        red_scr[...] = mine_scr[chip] + got1_scr[chip]
        db = to_chips.descriptors([s4_scr], [got2_scr], sb, rb)
        for d in db:
            d.start()
        for d in db:
            d.wait()
        red_scr[...] = ((red_scr[...] + got2_scr[0].astype(F32)) + got2_scr[1].astype(F32)) + got2_scr[2].astype(F32)
        dc = back.descriptors([red_scr], [mine_o, got_o], sc, rc)
        for d in dc:
            d.start()
        for d in da[:8] + da[16:] + dc:
            d.wait()

    hbm = pl.BlockSpec(memory_space=pl.ANY)
    vmem = pl.BlockSpec(memory_space=pltpu.VMEM)
    sems = pltpu.SemaphoreType.DMA
    half = jax.ShapeDtypeStruct((rr, cc), F32)
    return pl.pallas_call(
        body, name="tail",
        out_shape=[jax.ShapeDtypeStruct((8, 1, cc), F32), jax.ShapeDtypeStruct(mine_in.shape, F32), half, half],
        in_specs=[hbm, hbm, vmem, vmem, hbm], out_specs=[hbm] * 4,
        scratch_shapes=[pltpu.VMEM((1, 1, cc), F32), pltpu.VMEM((4, rr, cc), F32), pltpu.VMEM((4, rr, cc), F32),
                        pltpu.VMEM((4, rr, cc), BF16), pltpu.VMEM((3, rr, cc), BF16), pltpu.VMEM((rr, cc), F32),
                        sems((phase_a.n,)), sems((phase_a.n,)), sems((3,)), sems((3,)), sems((2,)), sems((2,))],
        compiler_params=pltpu.CompilerParams(vmem_limit_bytes=VMEM_LIMIT),
    )(g_a, g_b, row_a, row_b, mine_in)


def _ada_bwd_adamw(c_all_t, dmod_loc, w, m, v, carry=None):
    n = w.shape[1]
    tn = 512

    def body(ct_ref, dm_ref, w_ref, m_ref, v_ref, g_ref, d_ref, mo_ref, vo_ref):
        ct = ct_ref[...]
        cond = ct * _sig(ct)
        dm = dm_ref[...]
        g = cond[:, 0:1] * dm[0:1, :]
        for b in range(1, 8):
            g = g + cond[:, b:b + 1] * dm[b:b + 1, :]
        g_ref[...] = g
        d_ref[...], mo_ref[...], vo_ref[...] = _adamw(w_ref[...], g, m_ref[...], v_ref[...])

    wspec = pl.BlockSpec((D, tn), lambda j: (0, j))
    return _pcall(
        body, [c_all_t, dmod_loc, w, m, v], name="ada_bwd_adamw", grid=(n // tn,),
        out_shape=[jax.ShapeDtypeStruct((D, n), F32)] * 4,
        in_specs=[_row((D, 8)), pl.BlockSpec((8, tn), lambda j: (0, j)), wspec, wspec, wspec],
        out_specs=[wspec] * 4, carry=carry)


def _in_proj_fwd(x, a1, sh1, w_in, carry=None):
    s = x.shape[0]
    tm = 512

    def body(x_ref, a_ref, s_ref, w_ref, q_ref, kv_ref, z_ref, xbc_ref, dt_ref):
        def norm(rows):
            xv = x_ref[rows, :]
            r = lax.rsqrt(jnp.mean(xv * xv, axis=-1, keepdims=True) + EPS)
            return (xv * r * a_ref[...] + s_ref[...]).astype(BF16)

        def project(rows, h):
            p = _mm_nt(h, w_ref[...])
            q_ref[rows, :] = p[:, 0:512].astype(BF16)
            kv_ref[rows, :] = p[:, 512:768].astype(BF16)
            z_ref[rows, :] = p[:, 768:1280]
            xbc_ref[rows, :] = p[:, 1280:2304]
            dt_ref[rows, :] = p[:, 2304:2432]

        r0, r1 = slice(0, tm // 2), slice(tm // 2, tm)
        h0 = norm(r0)
        project(r0, h0)
        project(r1, norm(r1))

    def tok(w):
        return pl.BlockSpec((tm, w), lambda i: (i, 0))

    return _pcall(
        body, [x, a1, sh1, w_in], name="in_proj_fwd", grid=(s // tm,),
        out_shape=[jax.ShapeDtypeStruct((s, QW), BF16), jax.ShapeDtypeStruct((s, 2 * KVW), BF16),
                   jax.ShapeDtypeStruct((s, SW), F32), jax.ShapeDtypeStruct((s, XBCW), F32),
                   jax.ShapeDtypeStruct((s, 128), F32)],
        in_specs=[tok(D), _row((1, D)), _row((1, D)), _row((PROJ_W, D))],
        out_specs=[tok(QW), tok(2 * KVW), tok(SW), tok(XBCW), tok(128)], carry=carry)


def _in_proj_bwd(x, dx1, a1, sh1, w_in, dq, dkv, dz, dxbc, ddt, carry=None):
    s = x.shape[0]
    tm = 512

    def body(x_ref, dx1_ref, a_ref, s_ref, w_ref, dq_ref, dkv_ref, dz_ref, dxbc_ref, ddt_ref,
             gx_ref, h_ref, dsh_ref, p_ref):
        i = pl.program_id(0)

        @pl.when(i == 0)
        def _():
            dsh_ref[...] = jnp.zeros_like(dsh_ref)
            p_ref[...] = jnp.zeros_like(p_ref)

        def gather(st):
            rows = st["rows"]
            st["dproj"] = jnp.concatenate([dq_ref[rows, :], dkv_ref[rows, :], dz_ref[rows, :], dxbc_ref[rows, :],
                                           ddt_ref[rows, :]], axis=1)

        def back(st):
            st["dh"] = _mm(st.pop("dproj"), w_ref[...])

        def norm(st):
            rows, dh = st["rows"], st.pop("dh")
            xv = x_ref[rows, :]
            r = lax.rsqrt(jnp.mean(xv * xv, axis=-1, keepdims=True) + EPS)
            xn = xv * r
            a = a_ref[...]
            h_ref[rows, :] = (xn * a + s_ref[...]).astype(BF16)
            st["dsh"] = jnp.sum(dh, axis=0, keepdims=True)
            st["p"] = jnp.sum(dh * xn, axis=0, keepdims=True)
            u = dh * a
            gx_ref[rows, :] = dx1_ref[rows, :] + r * u - xn * (r * jnp.mean(u * xn, axis=-1, keepdims=True))

        g0, g1 = [dict(rows=slice(k * (tm // 2), (k + 1) * (tm // 2))) for k in range(2)]
        for stage, st in [(gather, g0), (back, g0), (gather, g1), (norm, g0), (back, g1), (norm, g1)]:
            stage(st)
        dsh_ref[0:1, :] += g0["dsh"] + g1["dsh"]
        p_ref[0:1, :] += g0["p"] + g1["p"]

    def tok(w):
        return pl.BlockSpec((tm, w), lambda i: (i, 0))

    return _pcall(
        body, [x, dx1, a1, sh1, w_in, dq, dkv, dz, dxbc, ddt], name="in_proj_bwd", grid=(s // tm,),
        out_shape=[jax.ShapeDtypeStruct((s, D), F32), jax.ShapeDtypeStruct((s, D), BF16),
                   jax.ShapeDtypeStruct((8, D), F32), jax.ShapeDtypeStruct((8, D), F32)],
        in_specs=[tok(D), tok(D), _row((1, D)), _row((1, D)), _row((PROJ_W, D)),
                  tok(QW), tok(2 * KVW), tok(SW), tok(XBCW), tok(128)],
        out_specs=[tok(D), tok(D), _row((8, D)), _row((8, D))], carry=carry)


def _attn_geometry():
    dist = np.arange(BLK)[:, None] + BLK - np.arange(2 * BLK)[None, :]
    n = np.maximum(dist, 0)
    max_exact = NBUCKET // 2
    large = max_exact + (np.log(np.maximum(n, 1) / max_exact) / np.log(128 / max_exact)
                         * (NBUCKET - max_exact)).astype(np.int32)
    large = np.minimum(large, NBUCKET - 1)
    bucket = np.where(n < max_exact, n, large).astype(np.int32)
    mask = (dist >= 0) & (dist < 128)
    return bucket, mask


def _attn_heads(is_first, q_blk, kvw, bias_ref, sinks_ref):
    qv = q_blk * 0.125
    col = lax.broadcasted_iota(jnp.int32, (BLK, 2 * BLK), 1)
    first = jnp.where(jnp.logical_and(is_first, col < BLK), NEG, 0.0)
    groups = []
    for g in range(NKV):
        qs = jnp.concatenate([qv[:, (4 * g + r) * HD:(4 * g + r + 1) * HD] for r in range(4)], axis=0)
        kw = kvw[:, g * HD:(g + 1) * HD]
        vw = kvw[:, KVW + g * HD:KVW + (g + 1) * HD]
        sc = _mm_nt(qs, kw)
        pn, ps = [], []
        for r in range(4):
            h = 4 * g + r
            sr = sc[r * BLK:(r + 1) * BLK] + bias_ref[h] + first
            sink = sinks_ref[h]
            m = jnp.maximum(jnp.max(sr, axis=-1, keepdims=True), sink)
            p = jnp.exp(sr - m)
            es = jnp.exp(sink - m)
            inv = 1.0 / (jnp.sum(p, axis=-1, keepdims=True) + es)
            pn.append(p * inv)
            ps.append(es * inv)
        pn = jnp.concatenate(pn, axis=0)
        ps = jnp.concatenate(ps, axis=0)
        o = _mm(pn.astype(BF16), vw)
        groups.append((qs, kw, vw, pn, ps, o))
    return groups


def _unstack_heads(parts):
    return jnp.concatenate([p[r * BLK:(r + 1) * BLK] for p in parts for r in range(4)], axis=1)


NB = 2


def _attn_fwd(q, kv, bias, sinks, nw, carry=None):
    s = q.shape[0]

    def body(q_ref, kvp_ref, kvc_ref, bias_ref, sinks_ref, nw_ref, y_ref):
        t = pl.program_id(0)
        kv3 = jnp.concatenate([kvp_ref[...], kvc_ref[...]], axis=0)
        for sub in range(NB):
            rows = slice(sub * BLK, (sub + 1) * BLK)
            groups = _attn_heads(jnp.logical_and(t == 0, sub == 0), q_ref[rows, :], kv3[sub * BLK:(sub + 2) * BLK],
                                 bias_ref, sinks_ref)
            o = _unstack_heads([g[5] for g in groups])
            r = lax.rsqrt(jnp.mean(o * o, axis=-1, keepdims=True) + EPS)
            y_ref[rows, :] = (o * r * nw_ref[...]).astype(BF16)

    return _pcall(
        body, [q, kv, kv, bias, sinks, nw], name="attn_fwd", grid=(s // (NB * BLK),),
        out_shape=[jax.ShapeDtypeStruct((s, QW), BF16)],
        in_specs=[pl.BlockSpec((NB * BLK, QW), lambda t: (t, 0)),
                  pl.BlockSpec((BLK, 2 * KVW), lambda t: (jnp.maximum(NB * t - 1, 0), 0)),
                  pl.BlockSpec((NB * BLK, 2 * KVW), lambda t: (t, 0)),
                  _row((NH, BLK, 2 * BLK)),
                  pl.BlockSpec(memory_space=pltpu.SMEM),
                  _row((1, QW))],
        out_specs=[pl.BlockSpec((NB * BLK, QW), lambda t: (t, 0))], carry=carry)


def _attn_bwd(q, kv, dya, bias, sinks, nw, act, dx2, gate2, w_dn, carry=None):
    s = q.shape[0]
    nt = s // (NB * BLK)
    npiece = DFF // NB

    def body(q_ref, kvp_ref, kvc_ref, dy_ref, bias_ref, sinks_ref, nw_ref, act_ref, dx2_ref, g2_ref, wdn_ref,
             dq_ref, dkv_ref, dbias_ref, dsink_ref, dnw_ref, gdn_hbm, dg2_ref, carry_ref, held_ref, acc_ref, sem):
        t = pl.program_id(0)

        @pl.when(t == 0)
        def _():
            carry_ref[...] = jnp.zeros_like(carry_ref)
            held_ref[...] = jnp.zeros_like(held_ref)
            dbias_ref[...] = jnp.zeros_like(dbias_ref)
            dsink_ref[...] = jnp.zeros_like(dsink_ref)
            dnw_ref[...] = jnp.zeros_like(dnw_ref)
            acc_ref[...] = jnp.zeros_like(acc_ref)

        def wgrad_piece(sub):
            rows = slice(sub * npiece, (sub + 1) * npiece)
            acc_ref[rows, :] += _mm_tn(act_ref[:, rows], dx2_ref[...].astype(BF16))

        def block(sub, kv3):
            rows = slice(sub * BLK, (sub + 1) * BLK)
            groups = _attn_heads(jnp.logical_and(t == 0, sub == 0), q_ref[rows, :], kv3[sub * BLK:(sub + 2) * BLK],
                                 bias_ref, sinks_ref)
            o = _unstack_heads([g[5] for g in groups])
            r = lax.rsqrt(jnp.mean(o * o, axis=-1, keepdims=True) + EPS)
            dy = dy_ref[rows, :]
            on = o * r
            dnw_ref[0:1, :] += jnp.sum(dy * on, axis=0, keepdims=True)
            u = dy * nw_ref[...]
            do = r * u - on * (r * jnp.mean(u * on, axis=-1, keepdims=True))
            dq_parts, dk_parts, dv_parts = [], [], []
            for g, (qs, kw, vw, pn, ps, og) in enumerate(groups):
                dos = jnp.concatenate([do[:, (4 * g + r_) * HD:(4 * g + r_ + 1) * HD] for r_ in range(4)], axis=0)
                delta = jnp.sum(dos * og, axis=-1, keepdims=True)
                dp = _mm_nt(dos.astype(BF16), vw)
                ds = pn * (dp - delta)
                dsk = ps * delta
                lane = lax.broadcasted_iota(jnp.int32, (1, 128), 1)
                for r_ in range(4):
                    h = 4 * g + r_
                    dbias_ref[h] += ds[r_ * BLK:(r_ + 1) * BLK]
                    dsink_ref[0:1, :] -= jnp.where(lane == h, jnp.sum(dsk[r_ * BLK:(r_ + 1) * BLK]), 0.0)
                dsb = ds.astype(BF16)
                dq_parts.append(_mm(dsb, kw) * 0.125)
                dk_parts.append(_mm_tn(dsb, qs))
                dv_parts.append(_mm_tn(pn.astype(BF16), dos.astype(BF16)))
            dq_ref[rows, :] = _unstack_heads(dq_parts).astype(BF16)
            return jnp.concatenate(dk_parts + dv_parts, axis=1)

        @pl.when(t < nt)
        def _():
            kv3 = jnp.concatenate([kvp_ref[...], kvc_ref[...]], axis=0)
            tail = carry_ref[...]
            for sub in range(NB):
                d = block(sub, kv3)
                done = tail + d[0:BLK]
                if sub == 0:
                    dkv_ref[0:(NB - 1) * BLK, :] = held_ref[...].astype(BF16)
                    dkv_ref[(NB - 1) * BLK:NB * BLK, :] = done.astype(BF16)
                else:
                    held_ref[(sub - 1) * BLK:sub * BLK, :] = done
                tail = d[BLK:2 * BLK]
                wgrad_piece(sub)
            carry_ref[...] = tail

        @pl.when(t == nt)
        def _():
            dkv_ref[0:(NB - 1) * BLK, :] = held_ref[...].astype(BF16)
            dkv_ref[(NB - 1) * BLK:NB * BLK, :] = carry_ref[...].astype(BF16)
            acc = acc_ref[...]
            dg2_ref[...] = jnp.zeros_like(dg2_ref)
            dg2_ref[0:1, :] = jnp.sum(acc * wdn_ref[...].astype(F32), axis=0, keepdims=True)
            acc_ref[...] = acc * g2_ref[...]
            cp = pltpu.make_async_copy(acc_ref, gdn_hbm, sem)
            cp.start()
            cp.wait()

    last = nt - 1
    tile = lambda w: pl.BlockSpec((NB * BLK, w), lambda t: (jnp.minimum(t, last), 0))
    return _pcall(
        body, [q, kv, kv, dya, bias, sinks, nw, act, dx2, gate2, w_dn], name="attn_bwd", grid=(nt + 1,),
        out_shape=[jax.ShapeDtypeStruct((s, QW), BF16), jax.ShapeDtypeStruct((s, 2 * KVW), BF16),
                   jax.ShapeDtypeStruct((NH, BLK, 2 * BLK), F32), jax.ShapeDtypeStruct((NH, 128), F32),
                   jax.ShapeDtypeStruct((8, QW), F32), jax.ShapeDtypeStruct((DFF, D), F32),
                   jax.ShapeDtypeStruct((8, D), F32)],
        in_specs=[tile(QW),
                  pl.BlockSpec((BLK, 2 * KVW), lambda t: (jnp.clip(NB * t - 1, 0, NB * nt - 1), 0)),
                  tile(2 * KVW), tile(QW),
                  _row((NH, BLK, 2 * BLK)),
                  pl.BlockSpec(memory_space=pltpu.SMEM),
                  _row((1, QW)), tile(DFF), tile(D), _row((1, D)), _row((DFF, D))],
        out_specs=[tile(QW),
                   pl.BlockSpec((NB * BLK, 2 * KVW), lambda t: (jnp.maximum(t - 1, 0), 0)),
                   _row((NH, BLK, 2 * BLK)), _row((NH, 128)), _row((8, QW)),
                   pl.BlockSpec(memory_space=pl.ANY), _row((8, D))],
        scratch_shapes=[pltpu.VMEM((BLK, 2 * KVW), F32), pltpu.VMEM(((NB - 1) * BLK, 2 * KVW), F32),
                        pltpu.VMEM((DFF, D), F32), pltpu.SemaphoreType.DMA], carry=carry)


def _rel_bias_grad(dbias, bucket):
    def body(db_ref, bk_ref, o_ref):
        bk = bk_ref[...]
        lane = lax.broadcasted_iota(jnp.int32, (1, 128), 1)
        for b in range(NBUCKET):
            sel = bk == b
            row = jnp.zeros((1, 128), F32)
            for h in range(NH):
                row = row + jnp.where(lane == h, jnp.sum(jnp.where(sel, db_ref[h], 0.0)), 0.0)
            o_ref[b:b + 1, :] = row

    return pl.pallas_call(
        body, name="rel_bias_grad",
        out_shape=jax.ShapeDtypeStruct((NBUCKET, 128), F32),
    )(dbias, bucket)


def _ssd_consts():
    head_of_lane = np.arange(SW) // HD
    expand = (np.arange(128)[:, None] == head_of_lane[None, :]).astype(np.float32)
    tril = np.tril(np.ones((BLK, BLK), np.float32))
    return (jnp.asarray(expand, BF16), jnp.asarray(expand.T.copy(), BF16), jnp.asarray(tril, BF16),
            jnp.asarray(tril.T.copy(), BF16))


def _conv_pre(xc, halo, cw, cb):
    ext = jnp.concatenate([halo, xc], axis=0)
    taps = [xc if k == CK - 1 else pltpu.roll(ext, CK - 1 - k, 0)[8:8 + BLK] for k in range(CK)]
    return cb + sum(cw[k:k + 1, :] * taps[k] for k in range(CK))


def _ssd_chunk(pre, dtr, dtb, av, dkv, ex, tril, h_in):
    sp = _sig(pre)
    xbc = pre * sp
    xs, bm, cm = xbc[:, 0:SW], xbc[:, SW:SW + 2 * NST], xbc[:, SW + 2 * NST:]
    dtin = dtr + dtb
    dt = jnp.maximum(dtin, 0.0) + jnp.log1p(jnp.exp(-jnp.abs(dtin)))
    cs = _sel_l(tril, dt * av)
    cst = cs.T
    dtx = _sel_r(dt, ex)
    csx = _sel_r(cs, ex)
    xdt = xs * dtx
    csl = csx[BLK - 1:BLK, :]
    decx = jnp.exp(csl - csx)
    ecsx = jnp.exp(csx)
    ecl = jnp.exp(csl)
    causal = tril.astype(F32) > 0.5
    ydiag, yoff, cbs, lms = [], [], [], []
    for g in range(2):
        bg = bm[:, g * NST:(g + 1) * NST].astype(BF16)
        cg = cm[:, g * NST:(g + 1) * NST].astype(BF16)
        cb = _mm_nt(cg, bg)
        cbs.append(cb)
        yoff.append(_mm(cg, h_in[:, g * 256:(g + 1) * 256].astype(BF16)))
        for r in range(4):
            h = 4 * g + r
            seg = cs[:, h:h + 1] - cst[h:h + 1, :]
            lm = jnp.where(causal, jnp.exp(jnp.minimum(seg, 0.0)), 0.0)
            lms.append(lm)
            ydiag.append(_mm((cb * lm).astype(BF16), xdt[:, h * HD:(h + 1) * HD].astype(BF16)))
    yoff = jnp.concatenate(yoff, axis=1) * ecsx
    y = jnp.concatenate(ydiag, axis=1) + yoff + dkv * xs
    return dict(pre=pre, sp=sp, xs=xs, bm=bm, cm=cm, dtin=dtin, dt=dt, av=av, cs=cs, cst=cst,
                dtx=dtx, csx=csx, xdt=xdt, decx=decx, ecsx=ecsx, ecl=ecl, causal=causal, cbs=cbs, lms=lms,
                yoff=yoff, y=y)


def _group_mean(t):
    m0 = jnp.mean(t[:, 0:256], axis=-1, keepdims=True)
    m1 = jnp.mean(t[:, 256:512], axis=-1, keepdims=True)
    return jnp.concatenate([jnp.broadcast_to(m0, (t.shape[0], 256)), jnp.broadcast_to(m1, (t.shape[0], 256))], axis=1)


SUBS = 4


def _ssd_fwd(z, xbc, dtr, cw, cb, dtb, av, dk, nw, carry=None):
    s = z.shape[0]
    nc = s // BLK
    tile = SUBS * BLK
    ex, _, tril, _ = _ssd_consts()

    def body(z_ref, xc_ref, xh_ref, dtr_ref, cw_ref, cb_ref, dtb_ref, a_ref, dk_ref, nw_ref, ex_ref, tril_ref,
             y_ref, hs_ref, pre_ref, h_ref):
        t = pl.program_id(0)

        @pl.when(t == 0)
        def _():
            h_ref[...] = jnp.zeros_like(h_ref)

        h_in = h_ref[...]
        for sub in range(SUBS):
            rows = slice(sub * BLK, (sub + 1) * BLK)
            xc = xc_ref[rows, :]
            halo = jnp.where(t == 0, 0.0, xh_ref[...]) if sub == 0 else xc_ref[sub * BLK - 8:sub * BLK, :]
            pre = _conv_pre(xc, halo, cw_ref[...], cb_ref[...])
            pre_ref[rows, :] = pre
            hs_ref[sub] = h_in
            f = _ssd_chunk(pre, dtr_ref[rows, :], dtb_ref[...], a_ref[...], dk_ref[...], ex_ref[...], tril_ref[...], h_in)
            dx = (f["decx"] * f["xdt"]).astype(BF16)
            st = [_mm_tn(f["bm"][:, g * NST:(g + 1) * NST].astype(BF16), dx[:, g * 256:(g + 1) * 256]) for g in range(2)]
            h_in = h_in * f["ecl"] + jnp.concatenate(st, axis=1)
            zv = z_ref[rows, :]
            tg = f["y"] * (zv * _sig(zv))
            r = lax.rsqrt(_group_mean(tg * tg) + EPS)
            y_ref[rows, :] = (tg * r * nw_ref[...]).astype(BF16)
        h_ref[...] = h_in

    cur = lambda w: pl.BlockSpec((tile, w), lambda t: (t, 0))
    return _pcall(
        body, [z, xbc, xbc, dtr, cw, cb, dtb, av, dk, nw, ex, tril], name="ssd_fwd", grid=(s // tile,),
        out_shape=[jax.ShapeDtypeStruct((s, SW), BF16), jax.ShapeDtypeStruct((nc, NST, SW), F32),
                   jax.ShapeDtypeStruct((s, XBCW), F32)],
        in_specs=[cur(SW), cur(XBCW), pl.BlockSpec((8, XBCW), lambda t: (jnp.maximum(t * (tile // 8) - 1, 0), 0)),
                  cur(128), _row((8, XBCW)), _row((1, XBCW)), _row((1, 128)),
                  _row((1, 128)), _row((1, SW)), _row((1, SW)), _row((128, SW)), _row((BLK, BLK))],
        out_specs=[cur(SW), pl.BlockSpec((SUBS, NST, SW), lambda t: (t, 0, 0)), cur(XBCW)],
        scratch_shapes=[pltpu.VMEM((NST, SW), F32)], carry=carry)


def _ssd_bwd(z, xbc, pre_all, dtr, dys, hs, cw, dtb, av, dk, nw, carry=None):
    s = z.shape[0]
    tile = SUBS * BLK
    nt = s // tile
    ex, ext_t, tril, triu = _ssd_consts()

    def body(z_ref, xc_ref, pre_ref, dtr_ref, dy_ref, hs_ref, cw_ref, dtb_ref, a_ref, dk_ref, nw_ref,
             ex_ref, ext_ref, tril_ref, triu_ref,
             dz_ref, dxbc_ref, ddt_ref, dcw_ref, dcb_ref, dnw_ref, dhd_ref, dh_ref, nxt_ref, dd_ref):
        i = pl.program_id(0)

        @pl.when(i == 0)
        def _():
            dh_ref[...] = jnp.zeros_like(dh_ref)
            nxt_ref[...] = jnp.zeros_like(nxt_ref)
            dd_ref[...] = jnp.zeros_like(dd_ref)
            dcw_ref[...] = jnp.zeros_like(dcw_ref)
            dcb_ref[...] = jnp.zeros_like(dcb_ref)
            dnw_ref[...] = jnp.zeros_like(dnw_ref)
            dhd_ref[...] = jnp.zeros_like(dhd_ref)

        gst, nxt = dh_ref[...], nxt_ref[...]
        for sub in reversed(range(SUBS)):
            rows = slice(sub * BLK, (sub + 1) * BLK)
            gst, nxt = chunk(sub, rows, gst, nxt, z_ref, xc_ref, pre_ref, dtr_ref, dy_ref, hs_ref, cw_ref, dtb_ref,
                             a_ref, dk_ref, nw_ref, ex_ref, ext_ref, tril_ref, triu_ref,
                             dz_ref, dxbc_ref, ddt_ref, dcw_ref, dcb_ref, dnw_ref, dhd_ref, dd_ref)
        dh_ref[...] = gst
        nxt_ref[...] = nxt

        @pl.when(i == nt - 1)
        def _():
            dhd_ref[2:3, :] = _sel_r(dd_ref[...], ext_ref[...])[0:1, :]

    def chunk(sub, rows, gst, nxt, z_ref, xc_ref, pre_ref, dtr_ref, dy_ref, hs_ref, cw_ref, dtb_ref,
              a_ref, dk_ref, nw_ref, ex_ref, ext_ref, tril_ref, triu_ref,
              dz_ref, dxbc_ref, ddt_ref, dcw_ref, dcb_ref, dnw_ref, dhd_ref, dd_ref):
        h_in = hs_ref[sub]
        f = _ssd_chunk(pre_ref[rows, :], dtr_ref[rows, :], dtb_ref[...], a_ref[...], dk_ref[...], ex_ref[...],
                       tril_ref[...], h_in)
        xs, xdt, decx, ecsx, ecl, dtx = f["xs"], f["xdt"], f["decx"], f["ecsx"], f["ecl"], f["dtx"]
        cs, cst, causal = f["cs"], f["cst"], f["causal"]
        causal_t = triu_ref[...].astype(F32) > 0.5

        zv = z_ref[rows, :]
        sz = _sig(zv)
        gz = zv * sz
        t = f["y"] * gz
        r = lax.rsqrt(_group_mean(t * t) + EPS)
        tn_ = t * r
        dyn = dy_ref[rows, :]
        dnw_ref[0:1, :] += jnp.sum(dyn * tn_, axis=0, keepdims=True)
        u = dyn * nw_ref[...]
        dt_ = r * u - tn_ * (r * _group_mean(u * tn_))
        dy = dt_ * gz
        dz_ref[rows, :] = (dt_ * f["y"] * (sz * (1.0 + zv * (1.0 - sz)))).astype(BF16)

        dd_ref[0:1, :] += jnp.sum(dy * xs, axis=0, keepdims=True)
        dxs = dk_ref[...] * dy

        edy = ecsx * dy
        dxdt, dbs, dcs_, dcsx_parts, dh_new = [], [], [], [], []
        lane = lax.broadcasted_iota(jnp.int32, (1, 128), 1)
        dcs_intra = jnp.zeros((BLK, 128), F32)
        for g in range(2):
            sl = slice(g * 256, (g + 1) * 256)
            bgf, cgf = f["bm"][:, g * NST:(g + 1) * NST], f["cm"][:, g * NST:(g + 1) * NST]
            bg, cg = bgf.astype(BF16), cgf.astype(BF16)
            gg = gst[:, sl].astype(BF16)
            hg = h_in[:, sl].astype(BF16)
            edyg = edy[:, sl].astype(BF16)
            dc = _mm_nt(edyg, hg)
            dh_new.append(gst[:, sl] * ecl[:, sl] + _mm_tn(cg, edyg))
            bgm = _mm(bg, gg)
            dxdt_g = decx[:, sl] * bgm
            dxg = (decx[:, sl] * xdt[:, sl]).astype(BF16)
            db = _mm_nt(dxg, gg)
            qd = bgm * xdt[:, sl] * decx[:, sl]
            last = jnp.sum(qd, axis=0, keepdims=True) + ecl[:, sl] * jnp.sum(gst[:, sl] * h_in[:, sl], axis=0, keepdims=True)
            rowid = lax.broadcasted_iota(jnp.int32, (BLK, 256), 0)
            dcsx_parts.append(f["yoff"][:, sl] * dy[:, sl] - qd + jnp.where(rowid == BLK - 1, last, 0.0))
            cb_ = f["cbs"][g]
            cbt = _mm_nt(bg, cg)
            dcb_ = jnp.zeros((BLK, BLK), F32)
            dcbt = jnp.zeros((BLK, BLK), F32)
            dxd = []
            for r_ in range(4):
                h = 4 * g + r_
                hl = slice(h * HD, (h + 1) * HD)
                lm = f["lms"][h]
                segt = cst[h:h + 1, :] - cs[:, h:h + 1]
                lmt = jnp.where(causal_t, jnp.exp(jnp.minimum(segt, 0.0)), 0.0)
                dyh = dy[:, hl].astype(BF16)
                xdh = xdt[:, hl].astype(BF16)
                dw = _mm_nt(dyh, xdh)
                dwt = _mm_nt(xdh, dyh)
                wt = cbt * lmt
                dxd.append(_mm(wt.astype(BF16), dyh))
                dcb_ = dcb_ + dw * lm
                dcbt = dcbt + dwt * lmt
                col = jnp.sum(dw * (cb_ * lm), axis=-1, keepdims=True) - jnp.sum(dwt * wt, axis=-1, keepdims=True)
                dcs_intra = dcs_intra + jnp.where(lane == h, col, 0.0)
            dxdt.append(dxdt_g + jnp.concatenate(dxd, axis=1))
            dcs_.append(dc + _mm(dcb_.astype(BF16), bg))
            dbs.append(db + _mm(dcbt.astype(BF16), cg))
        dxdt = jnp.concatenate(dxdt, axis=1)
        dxs = dxs + dxdt * dtx
        ext_t_ = ext_ref[...]
        dcs = dcs_intra + _sel_r(jnp.concatenate(dcsx_parts, axis=1), ext_t_)
        da = _sel_l(triu_ref[...], dcs)
        ddt = da * f["av"] + _sel_r(dxdt * xs, ext_t_)
        dhd_ref[1:2, :] += jnp.sum(da * f["dt"], axis=0, keepdims=True)
        ddtr = ddt * _sig(f["dtin"])
        dhd_ref[0:1, :] += jnp.sum(ddtr, axis=0, keepdims=True)
        ddt_ref[rows, :] = ddtr.astype(BF16)

        sp, pre = f["sp"], f["pre"]
        dact = jnp.concatenate([dxs] + dbs + dcs_, axis=1)
        dpre = dact * (sp * (1.0 + pre * (1.0 - sp)))
        dcb_ref[0:1, :] += jnp.sum(dpre, axis=0, keepdims=True)
        ext2 = jnp.concatenate([dpre, nxt], axis=0)
        shifted = [pltpu.roll(ext2, BLK + 8 - (CK - 1 - k), 0)[0:BLK] for k in range(CK - 1)] + [dpre]
        cw = cw_ref[...]
        xc = xc_ref[rows, :]
        dxr = cw[CK - 1:CK, :] * dpre
        for k in range(CK):
            dcw_ref[k:k + 1, :] += jnp.sum(shifted[k] * xc, axis=0, keepdims=True)
            if k < CK - 1:
                dxr = dxr + cw[k:k + 1, :] * shifted[k]
        dxbc_ref[rows, :] = dxr.astype(BF16)
        return jnp.concatenate(dh_new, axis=1), dpre[0:8]

    cur = lambda w: pl.BlockSpec((tile, w), lambda i: (nt - 1 - i, 0))
    return _pcall(
        body, [z, xbc, pre_all, dtr, dys, hs, cw, dtb, av, dk, nw, ex, ext_t, tril, triu], name="ssd_bwd", grid=(nt,),
        out_shape=[jax.ShapeDtypeStruct((s, SW), BF16), jax.ShapeDtypeStruct((s, XBCW), BF16),
                   jax.ShapeDtypeStruct((s, 128), BF16), jax.ShapeDtypeStruct((8, XBCW), F32),
                   jax.ShapeDtypeStruct((8, XBCW), F32), jax.ShapeDtypeStruct((8, SW), F32),
                   jax.ShapeDtypeStruct((8, 128), F32)],
        in_specs=[cur(SW), cur(XBCW), cur(XBCW), cur(128), cur(SW),
                  pl.BlockSpec((SUBS, NST, SW), lambda i: (nt - 1 - i, 0, 0)),
                  _row((8, XBCW)), _row((1, 128)), _row((1, 128)), _row((1, SW)), _row((1, SW)),
                  _row((128, SW)), _row((SW, 128)), _row((BLK, BLK)), _row((BLK, BLK))],
        out_specs=[cur(SW), cur(XBCW), cur(128), _row((8, XBCW)), _row((8, XBCW)), _row((8, SW)), _row((8, 128))],
        scratch_shapes=[pltpu.VMEM((NST, SW), F32), pltpu.VMEM((8, XBCW), F32), pltpu.VMEM((8, SW), F32)], carry=carry)


def _load_once(i, pairs, sem):
    @pl.when(i == 0)
    def _():
        cps = [pltpu.make_async_copy(src, dst, sem.at[k]) for k, (src, dst) in enumerate(pairs)]
        for cp in cps:
            cp.start()
        for cp in cps:
            cp.wait()


def _mlp_fwd(x, ya, ys, tgt, w_o, w_ga, w_gb, w_dn, gate1, a2, sh2, gate2, fn):
    s = x.shape[0]
    sub_m, subs = 256, 2
    tm = sub_m * subs

    def body(x_ref, ya_ref, ys_ref, t_ref, wo_hbm, wga_hbm, wgb_hbm, wdn_hbm, g1_ref, a2_ref, s2_ref, g2_ref, fn_ref,
             x1_ref, gu_ref, dx2_ref, loss_ref, dfn_ref, wo, wga, wgb, wdn, sem):
        i = pl.program_id(0)
        _load_once(i, [(wo_hbm, wo), (wga_hbm, wga), (wgb_hbm, wgb), (wdn_hbm, wdn)], sem)

        @pl.when(i == 0)
        def _():
            loss_ref[...] = jnp.zeros_like(loss_ref)
            dfn_ref[...] = jnp.zeros_like(dfn_ref)

        def proj(st):
            st["mix"] = _mm(ya_ref[st["rows"], :], wo[0:QW, :]) + _mm(ys_ref[st["rows"], :], wo[QW:D, :])

        def norm(st):
            x1 = x_ref[st["rows"], :] + g1_ref[...] * st.pop("mix")
            x1_ref[st["rows"], :] = x1
            r2 = lax.rsqrt(jnp.mean(x1 * x1, axis=-1, keepdims=True) + EPS)
            st["x1"] = x1
            st["h2"] = (x1 * r2 * a2_ref[...] + s2_ref[...]).astype(BF16)

        def gate_up(st):
            h2 = st.pop("h2")
            ha, hb = h2[:, 0:D // 2], h2[:, D // 2:D]
            gub = jnp.concatenate([(_mm(ha, wga[j]) + _mm(hb, wgb[j])).astype(BF16) for j in range(4)], axis=1)
            gu_ref[st["rows"], :] = gub
            st["gub"] = gub

        def activate(st):
            gub = st.pop("gub")
            gv, uv = gub[:, 0:DFF].astype(F32), gub[:, DFF:].astype(F32)
            st["act"] = (gv * _sig(gv) * uv).astype(BF16)

        def down(st):
            st["ff"] = _mm(st.pop("act"), wdn[...])

        def head(st):
            x2 = st.pop("x1") + g2_ref[...] * st.pop("ff")
            r3 = lax.rsqrt(jnp.mean(x2 * x2, axis=-1, keepdims=True) + EPS)
            xn = x2 * r3
            fnv = fn_ref[...]
            err = xn * fnv - t_ref[st["rows"], :]
            st["loss"] = jnp.sum(err * err) * (0.5 / D)
            dy = err * (1.0 / D)
            st["dfn"] = jnp.sum(dy * xn, axis=0, keepdims=True)
            u = dy * fnv
            dx2_ref[st["rows"], :] = r3 * u - xn * (r3 * jnp.mean(u * xn, axis=-1, keepdims=True))

        a, b = [dict(rows=slice(k * sub_m, (k + 1) * sub_m)) for k in range(subs)]
        for stage, st in [(proj, a), (norm, a), (proj, b), (gate_up, a), (norm, b), (activate, a), (gate_up, b),
                          (down, a), (activate, b), (head, a), (down, b), (head, b)]:
            stage(st)
        loss_ref[...] += a["loss"] + b["loss"]
        dfn_ref[0:1, :] += a["dfn"] + b["dfn"]

    def tok(w):
        return pl.BlockSpec((tm, w), lambda i: (i, 0))

    hbm = pl.BlockSpec(memory_space=pl.ANY)
    return pl.pallas_call(
        body, name="mlp_fwd", grid=(s // tm,),
        out_shape=[jax.ShapeDtypeStruct((s, D), F32), jax.ShapeDtypeStruct((s, 2 * DFF), BF16),
                   jax.ShapeDtypeStruct((s, D), F32), jax.ShapeDtypeStruct((8, 128), F32),
                   jax.ShapeDtypeStruct((8, D), F32)],
        in_specs=[tok(D), tok(QW), tok(SW), tok(D), hbm, hbm, hbm, hbm,
                  _row((1, D)), _row((1, D)), _row((1, D)), _row((1, D)), _row((1, D))],
        out_specs=[tok(D), tok(2 * DFF), tok(D), _row((8, 128)), _row((8, D))],
        scratch_shapes=[pltpu.VMEM((D, D), BF16), pltpu.VMEM(w_ga.shape, BF16), pltpu.VMEM(w_gb.shape, BF16),
                        pltpu.VMEM((DFF, D), BF16), pltpu.SemaphoreType.DMA((4,))],
        compiler_params=_cp(("arbitrary",)),
    )(x, ya, ys, tgt, w_o, w_ga, w_gb, w_dn, gate1, a2, sh2, gate2, fn)


def _mlp_bwd(x1, gu, dx2, w_o, w_ga, w_gb, w_dn, gate1, a2, sh2, gate2):
    s = x1.shape[0]
    tm = 256
    nj = 2 * DFF // 4

    def body(x1_ref, gu_ref, dx2_ref, wo_hbm, wga_hbm, wgb_hbm, wdn_hbm, g1_ref, a2_ref, s2_ref, g2_ref,
             dx1_ref, dya_ref, dys_ref, act_ref, dgu_ref, h2_ref, dsh_ref, p_ref, wo, wga, wgb, wdn, sem):
        i = pl.program_id(0)
        _load_once(i, [(wo_hbm, wo), (wga_hbm, wga), (wgb_hbm, wgb), (wdn_hbm, wdn)], sem)

        @pl.when(i == 0)
        def _():
            dsh_ref[...] = jnp.zeros_like(dsh_ref)
            p_ref[...] = jnp.zeros_like(p_ref)

        dx2 = dx2_ref[...]
        dact = _mm_nt((dx2 * g2_ref[...]).astype(BF16), wdn[...])
        gub = gu_ref[...]
        gv, uv = gub[:, 0:DFF].astype(F32), gub[:, DFF:].astype(F32)
        sg = _sig(gv)
        sl = gv * sg
        act_ref[...] = (sl * uv).astype(BF16)
        dgu = jnp.concatenate([dact * uv * (sg * (1.0 + gv * (1.0 - sg))), dact * sl], axis=1).astype(BF16)
        dgu_ref[...] = dgu
        dha = sum(_mm_nt(dgu[:, j * nj:(j + 1) * nj], wga[j]) for j in range(4))
        dhb = sum(_mm_nt(dgu[:, j * nj:(j + 1) * nj], wgb[j]) for j in range(4))
        dh = jnp.concatenate([dha, dhb], axis=1)
        x1 = x1_ref[...]
        r2 = lax.rsqrt(jnp.mean(x1 * x1, axis=-1, keepdims=True) + EPS)
        xn = x1 * r2
        a2 = a2_ref[...]
        h2_ref[...] = (xn * a2 + s2_ref[...]).astype(BF16)
        dsh_ref[0:1, :] += jnp.sum(dh, axis=0, keepdims=True)
        p_ref[0:1, :] += jnp.sum(dh * xn, axis=0, keepdims=True)
        u = dh * a2
        dx1 = dx2 + r2 * u - xn * (r2 * jnp.mean(u * xn, axis=-1, keepdims=True))
        dx1_ref[...] = dx1
        dcat = _mm_nt((dx1 * g1_ref[...]).astype(BF16), wo[...])
        dya_ref[...] = dcat[:, 0:QW]
        dys_ref[...] = dcat[:, QW:D]

    def tok(w):
        return pl.BlockSpec((tm, w), lambda i: (i, 0))

    hbm = pl.BlockSpec(memory_space=pl.ANY)
    return pl.pallas_call(
        body, name="mlp_bwd", grid=(s // tm,),
        out_shape=[jax.ShapeDtypeStruct((s, D), F32), jax.ShapeDtypeStruct((s, QW), F32),
                   jax.ShapeDtypeStruct((s, SW), F32), jax.ShapeDtypeStruct((s, DFF), BF16),
                   jax.ShapeDtypeStruct((s, 2 * DFF), BF16), jax.ShapeDtypeStruct((s, D), BF16),
                   jax.ShapeDtypeStruct((8, D), F32), jax.ShapeDtypeStruct((8, D), F32)],
        in_specs=[tok(D), tok(2 * DFF), tok(D), hbm, hbm, hbm, hbm, _row((1, D)), _row((1, D)), _row((1, D)), _row((1, D))],
        out_specs=[tok(D), tok(QW), tok(SW), tok(DFF), tok(2 * DFF), tok(D), _row((8, D)), _row((8, D))],
        scratch_shapes=[pltpu.VMEM((D, D), BF16), pltpu.VMEM(w_ga.shape, BF16), pltpu.VMEM(w_gb.shape, BF16),
                        pltpu.VMEM((DFF, D), BF16), pltpu.SemaphoreType.DMA((4,))],
        compiler_params=_cp(("arbitrary",)),
    )(x1, gu, dx2, w_o, w_ga, w_gb, w_dn, gate1, a2, sh2, gate2)


def _wgrad(name, a, b, gate, w, carry=None):
    s, m = a.shape
    n = b.shape[1]
    tk = min(1024, s)
    nk = s // tk

    def body(a_ref, b_ref, g_ref, w_ref, o_hbm, dg_ref, acc_ref, sem):
        k = pl.program_id(0)

        @pl.when(k == 0)
        def _():
            acc_ref[...] = jnp.zeros_like(acc_ref)

        acc_ref[...] += _mm_tn(a_ref[...], b_ref[...].astype(BF16))

        @pl.when(k == nk - 1)
        def _():
            acc = acc_ref[...]
            dg_ref[...] = jnp.zeros_like(dg_ref)
            dg_ref[0:1, :] = jnp.sum(acc * w_ref[...].astype(F32), axis=0, keepdims=True)
            acc_ref[...] = acc * g_ref[...]
            cp = pltpu.make_async_copy(acc_ref, o_hbm, sem)
            cp.start()
            cp.wait()

    return _pcall(body, [a, b, gate, w], name=name, grid=(nk,),
                  out_shape=[jax.ShapeDtypeStruct((m, n), F32), jax.ShapeDtypeStruct((8, n), F32)],
                  in_specs=[pl.BlockSpec((tk, m), lambda k: (k, 0)), pl.BlockSpec((tk, n), lambda k: (k, 0)),
                            _row((1, n)), _row((m, n))],
                  out_specs=[pl.BlockSpec(memory_space=pl.ANY), _row((8, n))],
                  scratch_shapes=[pltpu.VMEM((m, n), F32), pltpu.SemaphoreType.DMA], carry=carry)


def _wgrad_gate_up(h2, dgu, carry=None):
    s = h2.shape[0]
    tk = min(1024, s)
    nk = s // tk
    n = dgu.shape[1]
    nj = n // 4

    def body(a_ref, b_ref, o_hbm, acc_ref, sems):
        k = pl.program_id(0)

        @pl.when(k == 0)
        def _():
            acc_ref[...] = jnp.zeros_like(acc_ref)

        acc_ref[...] += _mm_tn(a_ref[...], b_ref[...])

        @pl.when(k == nk - 1)
        def _():
            cps = [pltpu.make_async_copy(acc_ref.at[:, pl.ds(j * nj, nj)], o_hbm.at[j], sems.at[j]) for j in range(4)]
            for cp in cps:
                cp.start()
            for cp in cps:
                cp.wait()

    return _pcall(body, [h2, dgu], name="wgrad_gate_up", grid=(nk,),
                  out_shape=[jax.ShapeDtypeStruct((4, D, nj), F32)],
                  in_specs=[pl.BlockSpec((tk, D), lambda k: (k, 0)), pl.BlockSpec((tk, n), lambda k: (k, 0))],
                  out_specs=[pl.BlockSpec(memory_space=pl.ANY)],
                  scratch_shapes=[pltpu.VMEM((D, n), F32), pltpu.SemaphoreType.DMA((4,))], carry=carry)


def _wgrad_in_t(h1, pieces, carry=None):
    s = h1.shape[0]
    rows = IN_W // 4
    tk = min(1024, s)
    nk = s // tk

    def body(a_ref, dq_ref, dkv_ref, dz_ref, dxbc_ref, ddt_ref, o_hbm, acc_ref, tr_ref, sl_ref, sem):
        k = pl.program_id(0)

        @pl.when(k == 0)
        def _():
            acc_ref[...] = jnp.zeros_like(acc_ref)

        dproj = jnp.concatenate([dq_ref[...], dkv_ref[...], dz_ref[...], dxbc_ref[...], ddt_ref[...]], axis=1)
        acc_ref[...] += _mm_tn(a_ref[...], dproj)

        @pl.when(k == nk - 1)
        def _():
            for j in range(PROJ_W // 128):
                tr_ref[j * 128:(j + 1) * 128, :] = acc_ref[:, j * 128:(j + 1) * 128].T
            for j in range(4):
                sl_ref[j] = tr_ref[j * rows:(j + 1) * rows, :]
            cp = pltpu.make_async_copy(sl_ref, o_hbm, sem)
            cp.start()
            cp.wait()

    return _pcall(body, [h1] + list(pieces), name="wgrad_in", grid=(nk,),
                  out_shape=[jax.ShapeDtypeStruct((4, rows, D), F32)],
                  in_specs=[pl.BlockSpec((tk, p.shape[1]), lambda k: (k, 0)) for p in [h1] + list(pieces)],
                  out_specs=[pl.BlockSpec(memory_space=pl.ANY)],
                  scratch_shapes=[pltpu.VMEM((D, PROJ_W), F32), pltpu.VMEM((PROJ_W, D), F32),
                                  pltpu.VMEM((4, rows, D), F32), pltpu.SemaphoreType.DMA],
                  carry=carry)


_SMALL = ["ada_b", "norm1", "conv_w", "conv_b", "dt_bias", "A_log", "D_skip", "sinks", "attn_out_norm",
          "ssm_out_norm", "norm2", "rel_bias", "final_norm"]


def _small_grad(name, gs, chip):
    if name == "ada_b":
        return jnp.concatenate([gs[j:j + 1, :] for j in range(6)], axis=1)
    if name == "conv_w":
        full = gs[7:11, :]
        out = full[:, 0:256]
        for j in range(1, 4):
            out = jnp.where(chip == j, full[:, j * 256:(j + 1) * 256], out)
        return out
    row, width = {"norm1": (6, D), "conv_b": (11, D), "norm2": (12, D), "final_norm": (13, D),
                  "attn_out_norm": (14, QW), "ssm_out_norm": (15, SW), "dt_bias": (16, NH), "A_log": (17, NH),
                  "D_skip": (18, NH), "sinks": (19, NH), "rel_bias": (24, NH)}[name]
    rows = NBUCKET if name == "rel_bias" else 1
    return gs[row:row + rows, 0:width]


def _small_update(small_all, where, ws, ms, vs):
    n = len(_SMALL)

    def body(where_ref, sa_ref, *refs):
        w_refs, m_refs, v_refs, outs = refs[:n], refs[n:2 * n], refs[2 * n:3 * n], refs[3 * n:]
        gs = sa_ref[0]
        for b in range(1, 8):
            gs = gs + sa_ref[b]
        chip = where_ref[1]
        for i, name in enumerate(_SMALL):
            g = _small_grad(name, gs, chip)
            lead = (0,) if name == "conv_w" else ()
            d, mo, vo = _adamw(w_refs[i][lead + (...,)], g, m_refs[i][lead + (...,)], v_refs[i][lead + (...,)])
            for k, val in enumerate((g, d, mo, vo)):
                outs[k * n + i][lead + (...,)] = val
        outs[4 * n][...] = gs[20:21, 0:128]

    shapes = [jax.ShapeDtypeStruct(w.shape, F32) for w in ws]
    vmem = pl.BlockSpec(memory_space=pltpu.VMEM)
    res = pl.pallas_call(
        body, name="small_update", out_shape=shapes * 4 + [jax.ShapeDtypeStruct((1, 128), F32)],
        in_specs=[pl.BlockSpec(memory_space=pltpu.SMEM)] + [vmem] * (1 + 3 * n), out_specs=[vmem] * (4 * n + 1),
    )(where, small_all, *ws, *ms, *vs)
    return [res[k * n:(k + 1) * n] for k in range(4)], res[4 * n][0, 0]


def _add_half(name, g, got, where, by_cols=False):
    rr, cc = got.shape[1:]
    if by_cols:
        mine = pl.BlockSpec((None, rr, cc), lambda i, w_ref: (i, 0, w_ref[0]))
    else:
        mine = pl.BlockSpec((None, None, rr, cc), lambda i, w_ref: (i, w_ref[0], 0, 0))

    def body(w_ref, g_ref, r_ref, o_ref, own_ref):
        s = g_ref[...] + r_ref[...]
        o_ref[...] = s.astype(BF16)

        @pl.when(pl.program_id(0) == w_ref[1])
        def _():
            own_ref[...] = s

    spec = pl.BlockSpec((None, rr, cc), lambda i, w_ref: (i, 0, 0))
    return _pcall(body, [where, g, got], name=name, grid=(4,), nprefetch=1,
                  out_shape=[jax.ShapeDtypeStruct(got.shape, BF16), jax.ShapeDtypeStruct((rr, cc), F32)],
                  in_specs=[mine, spec],
                  out_specs=[spec, pl.BlockSpec((rr, cc), lambda i, w_ref: (0, 0))])


def _add_chips(name, own, got):
    rr, cc = own.shape
    tr = rr // 2 if rr % 32 == 0 else rr

    def body(s_ref, r_ref, o_ref):
        o_ref[...] = ((s_ref[...] + r_ref[0].astype(F32)) + r_ref[1].astype(F32)) + r_ref[2].astype(F32)

    spec = pl.BlockSpec((tr, cc), lambda i: (i, 0))
    return _pcall(body, [own, got], name=name, grid=(rr // tr,), out_shape=[jax.ShapeDtypeStruct((rr, cc), F32)],
                  in_specs=[spec, pl.BlockSpec((3, tr, cc), lambda i: (0, i, 0))], out_specs=[spec])[0]


def _adamw_halves(name, mine, got, w, m, v, where, by_cols=False):
    rr, cc = mine.shape

    def body(w_ref_, t_ref, r_ref, w_ref, m_ref, v_ref, g_ref, d_ref, mo_ref, vo_ref):
        g = jnp.where(pl.program_id(0) == w_ref_[0], t_ref[...], r_ref[...])
        g_ref[...] = g
        d_ref[...], mo_ref[...], vo_ref[...] = _adamw(w_ref[...], g, m_ref[...], v_ref[...])

    if by_cols:
        grid = (2, 1)
        half = pl.BlockSpec((rr, cc), lambda h, i, w_ref_: (0, 0))
        full = pl.BlockSpec((rr, cc), lambda h, i, w_ref_: (0, h))
    else:
        tr = rr // 2
        grid = (2, 2)
        half = pl.BlockSpec((tr, cc), lambda h, i, w_ref_: (i, 0))
        full = pl.BlockSpec((None, tr, cc), lambda h, i, w_ref_: (0, 2 * h + i, 0))
    return _pcall(body, [where, mine, got, w, m, v], name=name, grid=grid, nprefetch=1,
                  out_shape=[jax.ShapeDtypeStruct(w.shape, F32)] * 4,
                  in_specs=[half, half, full, full, full], out_specs=[full] * 4)


def _bias_table(rel_bias, bucket, mask):
    def body(rb_ref, bk_ref, mk_ref, o_ref):
        bk = bk_ref[...]
        valid = mk_ref[...] > 0
        for h in range(NH):
            acc = jnp.zeros((BLK, 2 * BLK), F32)
            for b in range(NBUCKET):
                acc = jnp.where(bk == b, rb_ref[b, h], acc)
            o_ref[h] = jnp.where(valid, acc, NEG)

    vmem = pl.BlockSpec(memory_space=pltpu.VMEM)
    return pl.pallas_call(
        body, name="bias_table", out_shape=jax.ShapeDtypeStruct((NH, BLK, 2 * BLK), F32),
        in_specs=[pl.BlockSpec(memory_space=pltpu.SMEM), vmem, vmem], out_specs=vmem,
    )(rel_bias, bucket, mask)


def _pack_small(dsh1, p1, dsh2, p2, dg1a, dg1b, dg2, norm1, norm2, scale1, scale2, dcw, dcb, dfn,
                dnw_attn, dnw_ssm, dhd, av, dsink, drel, loss_acc):
    def body(dsh1_ref, p1_ref, dsh2_ref, p2_ref, dg1a_ref, dg1b_ref, dg2_ref, n1_ref, n2_ref, s1_ref, s2_ref,
             dcw_ref, dcb_ref, dfn_ref, da_ref, ds_ref, dhd_ref, av_ref, dsink_ref, drel_ref, loss_ref, o_ref):
        o_ref[...] = jnp.zeros_like(o_ref)
        p1v, p2v = p1_ref[0:1, :], p2_ref[0:1, :]
        o_ref[0:1, :] = dsh1_ref[0:1, :]
        o_ref[1:2, :] = p1v * n1_ref[...]
        o_ref[2:3, :] = dg1a_ref[0:1, :] + dg1b_ref[0:1, :]
        o_ref[3:4, :] = dsh2_ref[0:1, :]
        o_ref[4:5, :] = p2v * n2_ref[...]
        o_ref[5:6, :] = dg2_ref[0:1, :]
        o_ref[6:7, :] = p1v * (1.0 + s1_ref[...])
        o_ref[7:11, :] = dcw_ref[0:4, :]
        o_ref[11:12, :] = dcb_ref[0:1, :]
        o_ref[12:13, :] = p2v * (1.0 + s2_ref[...])
        o_ref[13:14, :] = dfn_ref[0:1, :]
        o_ref[14:15, 0:QW] = da_ref[0:1, :]
        o_ref[15:16, 0:SW] = ds_ref[0:1, :]
        o_ref[16:17, 0:128] = dhd_ref[0:1, :]
        o_ref[17:18, 0:128] = dhd_ref[1:2, :] * av_ref[...]
        o_ref[18:19, 0:128] = dhd_ref[2:3, :]
        o_ref[19:20, 0:128] = dsink_ref[0:1, :]
        o_ref[20:21, 0:128] = loss_ref[0:1, :]
        o_ref[24:56, 0:128] = drel_ref[...]

    return pl.pallas_call(body, name="pack_small", out_shape=jax.ShapeDtypeStruct((56, D), F32))(
        dsh1, p1, dsh2, p2, dg1a, dg1b, dg2, norm1, norm2, scale1, scale2, dcw, dcb, dfn,
        dnw_attn, dnw_ssm, dhd, av, dsink, drel, loss_acc)


def _pad_row(a, rows=1):
    return jnp.pad(a.reshape(rows, -1), ((0, 0), (0, D - a.size // rows)))


def kernel(x, c, ada_w, ada_b, norm1, w_in, conv_w, conv_b, dt_bias, A_log, D_skip, sinks, attn_out_norm, ssm_out_norm, w_o, norm2, w_gate_up, w_down, rel_bias, final_norm, loss_target, m_ada_w, m_ada_b, m_norm1, m_w_in, m_conv_w, m_conv_b, m_dt_bias, m_A_log, m_D_skip, m_sinks, m_attn_out_norm, m_ssm_out_norm, m_w_o, m_norm2, m_w_gate_up, m_w_down, m_rel_bias, m_final_norm, v_ada_w, v_ada_b, v_norm1, v_w_in, v_conv_w, v_conv_b, v_dt_bias, v_A_log, v_D_skip, v_sinks, v_attn_out_norm, v_ssm_out_norm, v_w_o, v_norm2, v_w_gate_up, v_w_down, v_rel_bias, v_final_norm):
    xi, yi, ci = lax.axis_index("x"), lax.axis_index("y"), lax.axis_index("c")
    chip = 2 * xi + yi
    me = 4 * xi + 2 * yi + ci
    where = jnp.stack([ci, chip]).astype(jnp.int32)
    xs2, tgt = x[0], loss_target[0]

    first = jnp.concatenate([c, _pad_row(conv_w[0], CK), jnp.zeros((3, D), F32)], axis=0)
    w_in_t, m_w_in_t, v_w_in_t = w_in[0].T, m_w_in[0].T, v_w_in[0].T
    w_in_b, w_o_b, w_dn_b = w_in_t.astype(BF16), w_o[0].astype(BF16), w_down[0].astype(BF16)
    w_gu_b = w_gate_up[0].astype(BF16)
    ncol = ada_w.shape[2]
    first_all, w_half, mod_all, w_other = _front(first, w_in_b, ada_w[0],
                                                 lax.dynamic_slice(ada_b, (0, chip * ncol), (1, ncol)))
    c_all = first_all[:, 0, :]
    cw_full = jnp.concatenate([first_all[2 * j, 1:1 + CK, 0:256] for j in range(4)], axis=1)
    w_lo = jnp.where(ci == 0, w_half, w_other)
    w_hi = jnp.where(ci == 0, w_other, w_half)
    w_in_f = jnp.pad(jnp.concatenate([w_lo, w_hi], axis=2).reshape(IN_W, D), ((0, PROJ_W - IN_W), (0, 0)))
    mod = lax.dynamic_slice(jnp.transpose(mod_all, (1, 0, 2)).reshape(8, 4 * ncol), (me, 0), (1, 4 * ncol))
    shift1, scale1, gate1, shift2, scale2, gate2 = [mod[:, j * D:(j + 1) * D] for j in range(6)]
    a1 = norm1 * (1.0 + scale1)
    a2 = norm2 * (1.0 + scale2)

    hdn = DFF // 8
    q, kv, z, xbc, dtr, w_o_g, w_dna_g = _in_proj_fwd(xs2, a1, shift1, w_in_f,
                                                      carry=_gather_chips_carry([w_o_b, w_dn_b[0:hdn]]))
    w_o_f = w_o_g.reshape(D, D)
    bucket, mask = _attn_geometry()
    bucket = jnp.asarray(bucket)
    bias = _bias_table(rel_bias, bucket, jnp.asarray(mask.astype(np.int32)))
    sinks1 = sinks[0]
    ya, w_ga_g = _attn_fwd(q, kv, bias, sinks1, attn_out_norm, carry=_gather_chips_carry([w_gu_b[0:D // 2]]))
    cw8 = jnp.concatenate([cw_full, jnp.zeros((4, XBCW), F32)], axis=0)
    dtb = _pad_row(dt_bias)[:, 0:128]
    av = _pad_row(-jnp.exp(A_log))[:, 0:128]
    dk = jnp.repeat(D_skip, HD, axis=1)
    ys, hs, pre, w_gb_g, w_dnb_g = _ssd_fwd(z, xbc, dtr, cw8, conv_b, dtb, av, dk, ssm_out_norm,
                                            carry=_gather_chips_carry([w_gu_b[D // 2:D], w_dn_b[hdn:2 * hdn]]))
    w_dn_f = jnp.stack([w_dna_g, w_dnb_g], axis=1).reshape(DFF, D)
    fn = final_norm[None, :]
    x1, gu, dx2, loss_acc, dfn = _mlp_fwd(xs2, ya, ys, tgt, w_o_f, w_ga_g, w_gb_g, w_dn_f, gate1, a2, shift2, gate2, fn)

    def to_sibling(p):
        return _Carry([p], [jax.ShapeDtypeStruct((4,) + p.shape[2:], F32)],
                      lambda x_, y_, c_: [(_SIBLING, 0, (j, 1 - c_), 0, j) for j in range(4)])

    def to_chips(s4):
        return _Carry([s4], [jax.ShapeDtypeStruct((3,) + s4.shape[1:], s4.dtype)],
                      lambda x_, y_, c_: [(f, 0, jnp.bitwise_xor(2 * x_ + y_, k + 1), 0, k) for k, f in enumerate(_CHIPS3)])

    def back(t):
        return _Carry([t[None]], [jax.ShapeDtypeStruct((1,) + t.shape, F32)], lambda x_, y_, c_: [(_SIBLING, 0, 0, 0, 0)])

    dx1, dya, dys, act, dgu, h2, dsh2, p2 = _mlp_bwd(x1, gu, dx2, w_o_f, w_ga_g, w_gb_g, w_dn_f, gate1, a2, shift2, gate2)
    p_gu = _wgrad_gate_up(h2, dgu)[0].reshape(4, 2, D // 2, 2 * DFF // 4)
    dq, dkv, dbias, dsink, dnw_attn, g_dn, dg2, got1_gu = _attn_bwd(
        q, kv, dya, bias, sinks1, attn_out_norm, act, dx2, gate2, w_dn_f, carry=to_sibling(p_gu))
    p_dn = g_dn.reshape(4, 2, DFF // 8, D)
    drel = _rel_bias_grad(dbias, bucket)
    s4_gu, own_gu = _add_half("rs_add_half_gu", p_gu, got1_gu, where)
    dz, dxbc, ddt, dcw, dcb, dnw_ssm, dhd, got2_gu, got1_dn = _ssd_bwd(
        z, xbc, pre, dtr, dys, hs, cw8, dtb, av, dk, ssm_out_norm, carry=_merge(to_chips(s4_gu), to_sibling(p_dn)))
    mine_gu = _add_chips("rs_add_chips_gu", own_gu, got2_gu)
    s4_dn, own_dn = _add_half("rs_add_half_dn", p_dn, got1_dn, where)
    grad_x, h1, dsh1, p1 = _in_proj_bwd(xs2, dx1, a1, shift1, w_in_f, dq, dkv, dz, dxbc, ddt)
    p_in, got2_dn, got3_gu = _wgrad_in_t(h1, [dq, dkv, dz, dxbc, ddt], carry=_merge(to_chips(s4_dn), back(mine_gu)))
    mine_dn = _add_chips("rs_add_chips_dn", own_dn, got2_dn)

    def to_sibling_cols(p):
        return _Carry([p], [jax.ShapeDtypeStruct(p.shape[:2] + (D // 2,), F32)],
                      lambda x_, y_, c_: [(_SIBLING, 0, (j, slice(None), pl.ds((1 - c_) * (D // 2), D // 2)), 0, j)
                                          for j in range(4)])

    no_dg1 = jnp.zeros((8, D), F32)
    small = _pack_small(dsh1, p1, dsh2, p2, no_dg1, no_dg1, dg2, norm1, norm2, scale1, scale2, dcw, dcb, dfn,
                        dnw_attn, dnw_ssm, dhd, av, dsink, drel, loss_acc)
    g_oa, dg1a, got1_in, got3_dn, small_all = _wgrad(
        "wgrad_o_attn", ya, dx1, gate1, w_o_f[0:QW],
        carry=_merge(to_sibling_cols(p_in), back(mine_dn), _gather8_carry(small)))
    s4_in, own_in = _add_half("rs_add_half_in", p_in, got1_in, where, by_cols=True)
    g_os, dg1b, got2_in = _wgrad("wgrad_o_ssm", ys, dx1, gate1, w_o_f[QW:D], carry=to_chips(s4_in))
    mine_in = _add_chips("rs_add_chips_in", own_in, got2_in)

    dg1_all, got3_in, mine_o, got3_o = _tail(g_oa, g_os, dg1a, dg1b, mine_in)
    small_all = small_all.at[:, 2, :].set(dg1_all[:, 0, :])
    small_res, loss = _small_update(
        small_all, where,
        [ada_b, norm1, conv_w, conv_b, dt_bias, A_log, D_skip, sinks, attn_out_norm, ssm_out_norm, norm2, rel_bias,
         final_norm[None, :]],
        [m_ada_b, m_norm1, m_conv_w, m_conv_b, m_dt_bias, m_A_log, m_D_skip, m_sinks, m_attn_out_norm,
         m_ssm_out_norm, m_norm2, m_rel_bias, m_final_norm[None, :]],
        [v_ada_b, v_norm1, v_conv_w, v_conv_b, v_dt_bias, v_A_log, v_D_skip, v_sinks, v_attn_out_norm,
         v_ssm_out_norm, v_norm2, v_rel_bias, v_final_norm[None, :]])
    small_out = [dict(zip(_SMALL, r)) for r in small_res]
    for r in small_out:
        r["final_norm"] = r["final_norm"][0]

    dmod_all = small_all[:, 0:6, :].reshape(8, 6 * D)
    dmod_loc = lax.dynamic_slice(dmod_all, (0, chip * ncol), (8, ncol))
    ada_out = _ada_bwd_adamw(c_all.T, dmod_loc, ada_w[0], m_ada_w[0], v_ada_w[0])

    big_gu = _adamw_halves("adamw_gate_up", mine_gu, got3_gu[0], w_gate_up, m_w_gate_up, v_w_gate_up, where)
    big_dn = _adamw_halves("adamw_down", mine_dn, got3_dn[0], w_down, m_w_down, v_w_down, where)
    big_o = _adamw_halves("adamw_o", mine_o, got3_o, w_o, m_w_o, v_w_o, where)
    big_in = [o.T[None] for o in _adamw_halves("adamw_in", mine_in, got3_in, w_in_t, m_w_in_t, v_w_in_t, where,
                                               by_cols=True)]
    big = [big_in, big_o, big_gu, big_dn]

    order = ["ada_w", "ada_b", "norm1", "w_in", "conv_w", "conv_b", "dt_bias", "A_log", "D_skip", "sinks",
             "attn_out_norm", "ssm_out_norm", "w_o", "norm2", "w_gate_up", "w_down", "rel_bias", "final_norm"]
    bigname = {"w_in": 0, "w_o": 1, "w_gate_up": 2, "w_down": 3}
    res = [loss, grad_x[None]]
    for kind in range(4):
        for nm in order:
            if nm == "ada_w":
                res.append(ada_out[kind][None])
            elif nm in bigname:
                res.append(big[bigname[nm]][kind])
            else:
                res.append(small_out[kind][nm])
    return tuple(res)
```

```python
import numpy as np
import jax
import jax.numpy as jnp
from jax import lax
from jax.experimental import pallas as pl
from jax.experimental.pallas import tpu as pltpu

F32, BF16 = jnp.float32, jnp.bfloat16
HI = lax.Precision.HIGHEST

D = 1024
QW, KVW = 512, 128
NH, HD, NKV = 8, 64, 2
SW = 512
NST = 128
XBCW = 1024
CK = 4
BLK = 128
DFF = 2816
IN_W = 2312
PROJ_W = 2432
EPS = 1e-6
NEG = -1e30
NBUCKET = 32

B1, B2, LR, AEPS, WD, STEP = 0.9, 0.999, 0.001, 1e-08, 0.01, 10

VMEM_LIMIT = 56 * 1024 * 1024

_NT = (((1,), (1,)), ((), ()))
_TN = (((0,), (0,)), ((), ()))


def _mm(a, b):
    return jnp.dot(a, b, preferred_element_type=F32)


def _mm_nt(a, b):
    return lax.dot_general(a, b, _NT, preferred_element_type=F32)


def _mm_tn(a, b):
    return lax.dot_general(a, b, _TN, preferred_element_type=F32)


def _mm_hi(a, b):
    return jnp.dot(a, b, preferred_element_type=F32, precision=HI)


def _split3(x):
    hi = x.astype(BF16)
    r = x - hi.astype(F32)
    mid = r.astype(BF16)
    lo = (r - mid.astype(F32)).astype(BF16)
    return hi, mid, lo


def _sel_r(x, e):
    hi, mid, lo = _split3(x)
    return (_mm(hi, e) + _mm(mid, e)) + _mm(lo, e)


def _sel_l(e, x):
    hi, mid, lo = _split3(x)
    return (_mm(e, hi) + _mm(e, mid)) + _mm(e, lo)


def _sig(x):
    return 1.0 / (1.0 + jnp.exp(-x))


def _cp(sem):
    return pltpu.CompilerParams(dimension_semantics=sem, vmem_limit_bytes=VMEM_LIMIT)


def _row(shape):
    nd = len(shape)
    return pl.BlockSpec(shape, lambda *_: (0,) * nd)


def _adamw(w, g, m, v):
    m = B1 * m + (1.0 - B1) * g
    v = B2 * v + (1.0 - B2) * (g * g)
    m_hat = m / (1.0 - B1 ** STEP)
    v_hat = v / (1.0 - B2 ** STEP)
    delta = -LR * (m_hat / (jnp.sqrt(v_hat) + AEPS) + WD * w)
    return delta, m, v


class _Carry:
    def __init__(self, inps, outs, copies):
        self.inps, self.outs, self.copies = list(inps), list(outs), copies
        self.n = len(copies(0, 0, 0))

    def descriptors(self, in_refs, out_refs, send_sems, recv_sems):
        x, y, c = lax.axis_index("x"), lax.axis_index("y"), lax.axis_index("c")
        out = []
        for j, (flip, a, si, o, di) in enumerate(self.copies(x, y, c)):
            if flip is None:
                out.append(pltpu.make_async_copy(in_refs[a].at[si], out_refs[o].at[di], send_sems.at[j]))
            else:
                fx, fy, fc = flip
                peer = (1 - x if fx else x, 1 - y if fy else y, 1 - c if fc else c)
                out.append(pltpu.make_async_remote_copy(
                    src_ref=in_refs[a].at[si], dst_ref=out_refs[o].at[di],
                    send_sem=send_sems.at[j], recv_sem=recv_sems.at[j],
                    device_id=peer, device_id_type=pl.DeviceIdType.MESH))
        return out


def _pcall(body, args, *, name, grid, in_specs, out_specs, out_shape, scratch_shapes=(), sem=None, nprefetch=0,
           carry=None):
    out_shape, out_specs = list(out_shape), list(out_specs)
    in_specs, scratch_shapes = list(in_specs), list(scratch_shapes)
    nin, nout, nscr = len(in_specs), len(out_shape), len(scratch_shapes)
    run = body
    if carry is not None:
        ncin, ncout = len(carry.inps), len(carry.outs)
        hbm = pl.BlockSpec(memory_space=pl.ANY)

        def run(*refs):
            pre, r = refs[:nprefetch], refs[nprefetch:]
            ins, cins = r[:nin], r[nin:nin + ncin]
            r = r[nin + ncin:]
            outs, couts = r[:nout], r[nout:nout + ncout]
            r = r[nout + ncout:]
            scr, (send_sems, recv_sems) = r[:nscr], r[nscr:]
            first = pl.program_id(0) == 0
            last = pl.program_id(0) == grid[0] - 1
            for ax in range(1, len(grid)):
                first = jnp.logical_and(first, pl.program_id(ax) == 0)
                last = jnp.logical_and(last, pl.program_id(ax) == grid[ax] - 1)

            @pl.when(first)
            def _():
                for d in carry.descriptors(cins, couts, send_sems, recv_sems):
                    d.start()

            body(*pre, *ins, *outs, *scr)

            @pl.when(last)
            def _():
                for d in carry.descriptors(cins, couts, send_sems, recv_sems):
                    d.wait()

        in_specs = in_specs + [hbm] * ncin
        out_specs = out_specs + [hbm] * ncout
        out_shape = out_shape + carry.outs
        scratch_shapes = scratch_shapes + [pltpu.SemaphoreType.DMA((carry.n,)), pltpu.SemaphoreType.DMA((carry.n,))]
        args = list(args) + carry.inps
    if sem is None:
        sem = ("arbitrary",) * len(grid)
    if nprefetch:
        kw = dict(grid_spec=pltpu.PrefetchScalarGridSpec(num_scalar_prefetch=nprefetch, grid=grid, in_specs=in_specs,
                                                         out_specs=out_specs, scratch_shapes=scratch_shapes))
    else:
        kw = dict(grid=grid, in_specs=in_specs, out_specs=out_specs, scratch_shapes=scratch_shapes)
    res = pl.pallas_call(run, name=name, out_shape=out_shape, compiler_params=_cp(sem), **kw)(*args)
    return list(res)


def _merge(*carries):
    inps, outs, offs = [], [], []
    for cr in carries:
        offs.append((len(inps), len(outs)))
        inps += cr.inps
        outs += cr.outs

    def copies(x, y, c):
        return [(f, a + io, si, o + oo, di) for cr, (io, oo) in zip(carries, offs) for f, a, si, o, di in cr.copies(x, y, c)]

    return _Carry(inps, outs, copies)


_ALL7 = [(f >> 2 & 1, f >> 1 & 1, f & 1) for f in range(1, 8)]
_CHIPS3 = [(0, 1, 0), (1, 0, 0), (1, 1, 0)]
_SIBLING = (0, 0, 1)


def _gather8_carry(blk):
    def copies(x, y, c):
        me = 4 * x + 2 * y + c
        return [(None, 0, 0, 0, me)] + [(f, 0, 0, 0, me) for f in _ALL7]

    return _Carry([blk[None]], [jax.ShapeDtypeStruct((8,) + blk.shape, blk.dtype)], copies)


def _gather_chips_carry(blks):
    def copies(x, y, c):
        chip = 2 * x + y
        return [(f, a, 0, a, chip) for a in range(len(blks)) for f in [None] + _CHIPS3]

    return _Carry([b[None] for b in blks], [jax.ShapeDtypeStruct((4,) + b.shape, b.dtype) for b in blks], copies)


def _front(first, w_in_t, w_loc, b_loc):
    n = w_loc.shape[1]
    rows = w_in_t.shape[0]
    hw = D // 2
    gather = _gather8_carry(first)
    fetch = _Carry([w_in_t], [jax.ShapeDtypeStruct((4, rows, hw), BF16)],
                   lambda x_, y_, c_: [(f, 0, (slice(None), pl.ds(c_ * hw, hw)), 0, 2 * x_ + y_) for f in [None] + _CHIPS3])
    phase_a = _merge(gather, fetch)
    send_mod = _Carry([None], [None], lambda x_, y_, c_: [(f, 0, slice(None), 0, 2 * x_ + y_) for f in [None] + _CHIPS3])
    swap = _Carry([None], [None], lambda x_, y_, c_: [(_SIBLING, 0, slice(None), 0, slice(None))])

    def body(first_hbm, w_hbm, wl_ref, bl_ref, first_all, w_half, mod_all, w_other,
             c_scr, mod_scr, sa, ra, sb, rb, sc, rc, sl):
        da = phase_a.descriptors([first_hbm, w_hbm], [first_all, w_half], sa, ra)
        for d in da:
            d.start()
        for d in da[:gather.n]:
            d.wait()
        cp = pltpu.make_async_copy(first_all, c_scr, sl)
        cp.start()
        cp.wait()
        cv = c_scr[:, 0, :]
        cond = cv * _sig(cv)
        for j in range(n // 512):
            cols = slice(j * 512, (j + 1) * 512)
            mod_scr[:, cols] = _mm_hi(cond, wl_ref[:, cols]) + bl_ref[:, cols]
        db = send_mod.descriptors([mod_scr], [mod_all], sb, rb)
        for d in db:
            d.start()
        for d in da[gather.n:]:
            d.wait()
        dc = swap.descriptors([w_half], [w_other], sc, rc)
        for d in dc:
            d.start()
        for d in db + dc:
            d.wait()

    hbm = pl.BlockSpec(memory_space=pl.ANY)
    vmem = pl.BlockSpec(memory_space=pltpu.VMEM)
    sems = pltpu.SemaphoreType.DMA
    return pl.pallas_call(
        body, name="front",
        out_shape=[jax.ShapeDtypeStruct((8,) + first.shape, F32), jax.ShapeDtypeStruct((4, rows, hw), BF16),
                   jax.ShapeDtypeStruct((4, 8, n), F32), jax.ShapeDtypeStruct((4, rows, hw), BF16)],
        in_specs=[hbm, hbm, vmem, vmem], out_specs=[hbm] * 4,
        scratch_shapes=[pltpu.VMEM((8,) + first.shape, F32), pltpu.VMEM((8, n), F32),
                        sems((phase_a.n,)), sems((phase_a.n,)), sems((4,)), sems((4,)), sems((1,)), sems((1,)), sems],
        compiler_params=pltpu.CompilerParams(vmem_limit_bytes=VMEM_LIMIT),
    )(first[None], w_in_t, w_loc, b_loc)


def _tail(g_a, g_b, row_a, row_b, mine_in):
    rr, cc = g_a.shape[0] // 4, g_a.shape[1]

    def copies_a(x_, y_, c_):
        me = 4 * x_ + 2 * y_ + c_
        out = [(None, 0, 0, 0, me)] + [(f, 0, 0, 0, me) for f in _ALL7]
        out += [(_SIBLING, 1 + j // 2, pl.ds((j % 2) * 2 * rr + (1 - c_) * rr, rr), 1, j) for j in range(4)]
        out += [(None, 1 + j // 2, pl.ds((j % 2) * 2 * rr + c_ * rr, rr), 2, j) for j in range(4)]
        return out + [(_SIBLING, 3, slice(None), 3, slice(None))]

    phase_a = _Carry([None] * 4, [None] * 4, copies_a)
    to_chips = _Carry([None], [None], lambda x_, y_, c_: [(f, 0, jnp.bitwise_xor(2 * x_ + y_, k + 1), 0, k)
                                                        for k, f in enumerate(_CHIPS3)])
    back = _Carry([None], [None] * 2, lambda x_, y_, c_: [(None, 0, slice(None), 0, slice(None)),
                                                        (_SIBLING, 0, slice(None), 1, slice(None))])

    def body(ga_hbm, gb_hbm, ra_ref, rb_ref, in_hbm, rows_all, got_in, mine_o, got_o,
             row_scr, got1_scr, mine_scr, s4_scr, got2_scr, red_scr, sa, ra, sb, rb, sc, rc):
        chip = 2 * lax.axis_index("x") + lax.axis_index("y")
        row_scr[0] = ra_ref[0:1, :] + rb_ref[0:1, :]
        da = phase_a.descriptors([row_scr, ga_hbm, gb_hbm, in_hbm], [rows_all, got1_scr, mine_scr, got_in], sa, ra)
        for d in da:
            d.start()
        for d in da[8:16]:
            d.wait()
        for j in range(4):
            s4_scr[j] = (mine_scr[j] + got1_scr[j]).astype(BF16)
        red_scr[...] = mine_scr[chip] + got1_scr[chip]
        db = to_chips.descriptors([s4_scr], [got2_scr], sb, rb)
        for d in db:
            d.start()
        for d in db:
            d.wait()
        red_scr[...] = ((red_scr[...] + got2_scr[0].astype(F32)) + got2_scr[1].astype(F32)) + got2_scr[2].astype(F32)
        dc = back.descriptors([red_scr], [mine_o, got_o], sc, rc)
        for d in dc:
            d.start()
        for d in da[:8] + da[16:] + dc:
            d.wait()

    hbm = pl.BlockSpec(memory_space=pl.ANY)
    vmem = pl.BlockSpec(memory_space=pltpu.VMEM)
    sems = pltpu.SemaphoreType.DMA
    half = jax.ShapeDtypeStruct((rr, cc), F32)
    return pl.pallas_call(
        body, name="tail",
        out_shape=[jax.ShapeDtypeStruct((8, 1, cc), F32), jax.ShapeDtypeStruct(mine_in.shape, F32), half, half],
        in_specs=[hbm, hbm, vmem, vmem, hbm], out_specs=[hbm] * 4,
        scratch_shapes=[pltpu.VMEM((1, 1, cc), F32), pltpu.VMEM((4, rr, cc), F32), pltpu.VMEM((4, rr, cc), F32),
                        pltpu.VMEM((4, rr, cc), BF16), pltpu.VMEM((3, rr, cc), BF16), pltpu.VMEM((rr, cc), F32),
                        sems((phase_a.n,)), sems((phase_a.n,)), sems((3,)), sems((3,)), sems((2,)), sems((2,))],
        compiler_params=pltpu.CompilerParams(vmem_limit_bytes=VMEM_LIMIT),
    )(g_a, g_b, row_a, row_b, mine_in)


def _ada_bwd_adamw(c_all_t, dmod_loc, w, m, v, carry=None):
    n = w.shape[1]
    tn = 128

    def body(ct_ref, dm_ref, w_ref, m_ref, v_ref, g_ref, d_ref, mo_ref, vo_ref):
        ct = ct_ref[...]
        cond = ct * _sig(ct)
        dm = dm_ref[...]
        g = cond[:, 0:1] * dm[0:1, :]
        for b in range(1, 8):
            g = g + cond[:, b:b + 1] * dm[b:b + 1, :]
        g_ref[...] = g
        d_ref[...], mo_ref[...], vo_ref[...] = _adamw(w_ref[...], g, m_ref[...], v_ref[...])

    wspec = pl.BlockSpec((D, tn), lambda j: (0, j))
    return _pcall(
        body, [c_all_t, dmod_loc, w, m, v], name="ada_bwd_adamw", grid=(n // tn,),
        out_shape=[jax.ShapeDtypeStruct((D, n), F32)] * 4,
        in_specs=[_row((D, 8)), pl.BlockSpec((8, tn), lambda j: (0, j)), wspec, wspec, wspec],
        out_specs=[wspec] * 4, carry=carry)


def _in_proj_fwd(x, a1, sh1, w_in, carry=None):
    s = x.shape[0]
    tm = 512

    def body(x_ref, a_ref, s_ref, w_ref, q_ref, kv_ref, z_ref, xbc_ref, dt_ref):
        def norm(rows):
            xv = x_ref[rows, :]
            r = lax.rsqrt(jnp.mean(xv * xv, axis=-1, keepdims=True) + EPS)
            return (xv * r * a_ref[...] + s_ref[...]).astype(BF16)

        def project(rows, h):
            p = _mm_nt(h, w_ref[...])
            q_ref[rows, :] = p[:, 0:512].astype(BF16)
            kv_ref[rows, :] = p[:, 512:768].astype(BF16)
            z_ref[rows, :] = p[:, 768:1280]
            xbc_ref[rows, :] = p[:, 1280:2304]
            dt_ref[rows, :] = p[:, 2304:2432]

        r0, r1 = slice(0, tm // 2), slice(tm // 2, tm)
        h0 = norm(r0)
        project(r0, h0)
        project(r1, norm(r1))

    def tok(w):
        return pl.BlockSpec((tm, w), lambda i: (i, 0))

    return _pcall(
        body, [x, a1, sh1, w_in], name="in_proj_fwd", grid=(s // tm,),
        out_shape=[jax.ShapeDtypeStruct((s, QW), BF16), jax.ShapeDtypeStruct((s, 2 * KVW), BF16),
                   jax.ShapeDtypeStruct((s, SW), F32), jax.ShapeDtypeStruct((s, XBCW), F32),
                   jax.ShapeDtypeStruct((s, 128), F32)],
        in_specs=[tok(D), _row((1, D)), _row((1, D)), _row((PROJ_W, D))],
        out_specs=[tok(QW), tok(2 * KVW), tok(SW), tok(XBCW), tok(128)], carry=carry)


def _in_proj_bwd(x, dx1, a1, sh1, w_in, dq, dkv, dz, dxbc, ddt, carry=None):
    s = x.shape[0]
    tm = 512

    def body(x_ref, dx1_ref, a_ref, s_ref, w_ref, dq_ref, dkv_ref, dz_ref, dxbc_ref, ddt_ref,
             gx_ref, h_ref, dsh_ref, p_ref):
        i = pl.program_id(0)

        @pl.when(i == 0)
        def _():
            dsh_ref[...] = jnp.zeros_like(dsh_ref)
            p_ref[...] = jnp.zeros_like(p_ref)

        def gather(st):
            rows = st["rows"]
            st["dproj"] = jnp.concatenate([dq_ref[rows, :], dkv_ref[rows, :], dz_ref[rows, :], dxbc_ref[rows, :],
                                           ddt_ref[rows, :]], axis=1)

        def back(st):
            st["dh"] = _mm(st.pop("dproj"), w_ref[...])

        def norm(st):
            rows, dh = st["rows"], st.pop("dh")
            xv = x_ref[rows, :]
            r = lax.rsqrt(jnp.mean(xv * xv, axis=-1, keepdims=True) + EPS)
            xn = xv * r
            a = a_ref[...]
            h_ref[rows, :] = (xn * a + s_ref[...]).astype(BF16)
            st["dsh"] = jnp.sum(dh, axis=0, keepdims=True)
            st["p"] = jnp.sum(dh * xn, axis=0, keepdims=True)
            u = dh * a
            gx_ref[rows, :] = dx1_ref[rows, :] + r * u - xn * (r * jnp.mean(u * xn, axis=-1, keepdims=True))

        g0, g1 = [dict(rows=slice(k * (tm // 2), (k + 1) * (tm // 2))) for k in range(2)]
        for stage, st in [(gather, g0), (back, g0), (gather, g1), (norm, g0), (back, g1), (norm, g1)]:
            stage(st)
        dsh_ref[0:1, :] += g0["dsh"] + g1["dsh"]
        p_ref[0:1, :] += g0["p"] + g1["p"]

    def tok(w):
        return pl.BlockSpec((tm, w), lambda i: (i, 0))

    return _pcall(
        body, [x, dx1, a1, sh1, w_in, dq, dkv, dz, dxbc, ddt], name="in_proj_bwd", grid=(s // tm,),
        out_shape=[jax.ShapeDtypeStruct((s, D), F32), jax.ShapeDtypeStruct((s, D), BF16),
                   jax.ShapeDtypeStruct((8, D), F32), jax.ShapeDtypeStruct((8, D), F32)],
        in_specs=[tok(D), tok(D), _row((1, D)), _row((1, D)), _row((PROJ_W, D)),
                  tok(QW), tok(2 * KVW), tok(SW), tok(XBCW), tok(128)],
        out_specs=[tok(D), tok(D), _row((8, D)), _row((8, D))], carry=carry)


def _attn_geometry():
    dist = np.arange(BLK)[:, None] + BLK - np.arange(2 * BLK)[None, :]
    n = np.maximum(dist, 0)
    max_exact = NBUCKET // 2
    large = max_exact + (np.log(np.maximum(n, 1) / max_exact) / np.log(128 / max_exact)
                         * (NBUCKET - max_exact)).astype(np.int32)
    large = np.minimum(large, NBUCKET - 1)
    bucket = np.where(n < max_exact, n, large).astype(np.int32)
    mask = (dist >= 0) & (dist < 128)
    return bucket, mask


def _attn_heads(is_first, q_blk, kvw, bias_ref, sinks_ref):
    qv = q_blk * 0.125
    col = lax.broadcasted_iota(jnp.int32, (BLK, 2 * BLK), 1)
    first = jnp.where(jnp.logical_and(is_first, col < BLK), NEG, 0.0)
    groups = []
    for g in range(NKV):
        qs = jnp.concatenate([qv[:, (4 * g + r) * HD:(4 * g + r + 1) * HD] for r in range(4)], axis=0)
        kw = kvw[:, g * HD:(g + 1) * HD]
        vw = kvw[:, KVW + g * HD:KVW + (g + 1) * HD]
        sc = _mm_nt(qs, kw)
        pn, ps = [], []
        for r in range(4):
            h = 4 * g + r
            sr = sc[r * BLK:(r + 1) * BLK] + bias_ref[h] + first
            sink = sinks_ref[h]
            m = jnp.maximum(jnp.max(sr, axis=-1, keepdims=True), sink)
            p = jnp.exp(sr - m)
            es = jnp.exp(sink - m)
            inv = 1.0 / (jnp.sum(p, axis=-1, keepdims=True) + es)
            pn.append(p * inv)
            ps.append(es * inv)
        pn = jnp.concatenate(pn, axis=0)
        ps = jnp.concatenate(ps, axis=0)
        o = _mm(pn.astype(BF16), vw)
        groups.append((qs, kw, vw, pn, ps, o))
    return groups


def _unstack_heads(parts):
    return jnp.concatenate([p[r * BLK:(r + 1) * BLK] for p in parts for r in range(4)], axis=1)


NB = 2


def _attn_fwd(q, kv, bias, sinks, nw, carry=None):
    s = q.shape[0]

    def body(q_ref, kvp_ref, kvc_ref, bias_ref, sinks_ref, nw_ref, y_ref):
        t = pl.program_id(0)
        kv3 = jnp.concatenate([kvp_ref[...], kvc_ref[...]], axis=0)
        for sub in range(NB):
            rows = slice(sub * BLK, (sub + 1) * BLK)
            groups = _attn_heads(jnp.logical_and(t == 0, sub == 0), q_ref[rows, :], kv3[sub * BLK:(sub + 2) * BLK],
                                 bias_ref, sinks_ref)
            o = _unstack_heads([g[5] for g in groups])
            r = lax.rsqrt(jnp.mean(o * o, axis=-1, keepdims=True) + EPS)
            y_ref[rows, :] = (o * r * nw_ref[...]).astype(BF16)

    return _pcall(
        body, [q, kv, kv, bias, sinks, nw], name="attn_fwd", grid=(s // (NB * BLK),),
        out_shape=[jax.ShapeDtypeStruct((s, QW), BF16)],
        in_specs=[pl.BlockSpec((NB * BLK, QW), lambda t: (t, 0)),
                  pl.BlockSpec((BLK, 2 * KVW), lambda t: (jnp.maximum(NB * t - 1, 0), 0)),
                  pl.BlockSpec((NB * BLK, 2 * KVW), lambda t: (t, 0)),
                  _row((NH, BLK, 2 * BLK)),
                  pl.BlockSpec(memory_space=pltpu.SMEM),
                  _row((1, QW))],
        out_specs=[pl.BlockSpec((NB * BLK, QW), lambda t: (t, 0))], carry=carry)


def _attn_bwd(q, kv, dya, bias, sinks, nw, act, dx2, gate2, w_dn, carry=None):
    s = q.shape[0]
    nt = s // (NB * BLK)
    npiece = DFF // NB

    def body(q_ref, kvp_ref, kvc_ref, dy_ref, bias_ref, sinks_ref, nw_ref, act_ref, dx2_ref, g2_ref, wdn_ref,
             dq_ref, dkv_ref, dbias_ref, dsink_ref, dnw_ref, gdn_hbm, dg2_ref, carry_ref, held_ref, acc_ref, sem):
        t = pl.program_id(0)

        @pl.when(t == 0)
        def _():
            carry_ref[...] = jnp.zeros_like(carry_ref)
            held_ref[...] = jnp.zeros_like(held_ref)
            dbias_ref[...] = jnp.zeros_like(dbias_ref)
            dsink_ref[...] = jnp.zeros_like(dsink_ref)
            dnw_ref[...] = jnp.zeros_like(dnw_ref)
            acc_ref[...] = jnp.zeros_like(acc_ref)

        def wgrad_piece(sub):
            rows = slice(sub * npiece, (sub + 1) * npiece)
            acc_ref[rows, :] += _mm_tn(act_ref[:, rows], dx2_ref[...].astype(BF16))

        def block(sub, kv3):
            rows = slice(sub * BLK, (sub + 1) * BLK)
            groups = _attn_heads(jnp.logical_and(t == 0, sub == 0), q_ref[rows, :], kv3[sub * BLK:(sub + 2) * BLK],
                                 bias_ref, sinks_ref)
            o = _unstack_heads([g[5] for g in groups])
            r = lax.rsqrt(jnp.mean(o * o, axis=-1, keepdims=True) + EPS)
            dy = dy_ref[rows, :]
            on = o * r
            dnw_ref[0:1, :] += jnp.sum(dy * on, axis=0, keepdims=True)
            u = dy * nw_ref[...]
            do = r * u - on * (r * jnp.mean(u * on, axis=-1, keepdims=True))
            dq_parts, dk_parts, dv_parts = [], [], []
            for g, (qs, kw, vw, pn, ps, og) in enumerate(groups):
                dos = jnp.concatenate([do[:, (4 * g + r_) * HD:(4 * g + r_ + 1) * HD] for r_ in range(4)], axis=0)
                delta = jnp.sum(dos * og, axis=-1, keepdims=True)
                dp = _mm_nt(dos.astype(BF16), vw)
                ds = pn * (dp - delta)
                dsk = ps * delta
                lane = lax.broadcasted_iota(jnp.int32, (1, 128), 1)
                for r_ in range(4):
                    h = 4 * g + r_
                    dbias_ref[h] += ds[r_ * BLK:(r_ + 1) * BLK]
                    dsink_ref[0:1, :] -= jnp.where(lane == h, jnp.sum(dsk[r_ * BLK:(r_ + 1) * BLK]), 0.0)
                dsb = ds.astype(BF16)
                dq_parts.append(_mm(dsb, kw) * 0.125)
                dk_parts.append(_mm_tn(dsb, qs))
                dv_parts.append(_mm_tn(pn.astype(BF16), dos.astype(BF16)))
            dq_ref[rows, :] = _unstack_heads(dq_parts).astype(BF16)
            return jnp.concatenate(dk_parts + dv_parts, axis=1)

        @pl.when(t < nt)
        def _():
            kv3 = jnp.concatenate([kvp_ref[...], kvc_ref[...]], axis=0)
            tail = carry_ref[...]
            for sub in range(NB):
                d = block(sub, kv3)
                done = tail + d[0:BLK]
                if sub == 0:
                    dkv_ref[0:(NB - 1) * BLK, :] = held_ref[...].astype(BF16)
                    dkv_ref[(NB - 1) * BLK:NB * BLK, :] = done.astype(BF16)
                else:
                    held_ref[(sub - 1) * BLK:sub * BLK, :] = done
                tail = d[BLK:2 * BLK]
                wgrad_piece(sub)
            carry_ref[...] = tail

        @pl.when(t == nt)
        def _():
            dkv_ref[0:(NB - 1) * BLK, :] = held_ref[...].astype(BF16)
            dkv_ref[(NB - 1) * BLK:NB * BLK, :] = carry_ref[...].astype(BF16)
            acc = acc_ref[...]
            dg2_ref[...] = jnp.zeros_like(dg2_ref)
            dg2_ref[0:1, :] = jnp.sum(acc * wdn_ref[...].astype(F32), axis=0, keepdims=True)
            acc_ref[...] = acc * g2_ref[...]
            cp = pltpu.make_async_copy(acc_ref, gdn_hbm, sem)
            cp.start()
            cp.wait()

    last = nt - 1
    tile = lambda w: pl.BlockSpec((NB * BLK, w), lambda t: (jnp.minimum(t, last), 0))
    return _pcall(
        body, [q, kv, kv, dya, bias, sinks, nw, act, dx2, gate2, w_dn], name="attn_bwd", grid=(nt + 1,),
        out_shape=[jax.ShapeDtypeStruct((s, QW), BF16), jax.ShapeDtypeStruct((s, 2 * KVW), BF16),
                   jax.ShapeDtypeStruct((NH, BLK, 2 * BLK), F32), jax.ShapeDtypeStruct((NH, 128), F32),
                   jax.ShapeDtypeStruct((8, QW), F32), jax.ShapeDtypeStruct((DFF, D), F32),
                   jax.ShapeDtypeStruct((8, D), F32)],
        in_specs=[tile(QW),
                  pl.BlockSpec((BLK, 2 * KVW), lambda t: (jnp.clip(NB * t - 1, 0, NB * nt - 1), 0)),
                  tile(2 * KVW), tile(QW),
                  _row((NH, BLK, 2 * BLK)),
                  pl.BlockSpec(memory_space=pltpu.SMEM),
                  _row((1, QW)), tile(DFF), tile(D), _row((1, D)), _row((DFF, D))],
        out_specs=[tile(QW),
                   pl.BlockSpec((NB * BLK, 2 * KVW), lambda t: (jnp.maximum(t - 1, 0), 0)),
                   _row((NH, BLK, 2 * BLK)), _row((NH, 128)), _row((8, QW)),
                   pl.BlockSpec(memory_space=pl.ANY), _row((8, D))],
        scratch_shapes=[pltpu.VMEM((BLK, 2 * KVW), F32), pltpu.VMEM(((NB - 1) * BLK, 2 * KVW), F32),
                        pltpu.VMEM((DFF, D), F32), pltpu.SemaphoreType.DMA], carry=carry)


def _rel_bias_grad(dbias, bucket):
    def body(db_ref, bk_ref, o_ref):
        bk = bk_ref[...]
        lane = lax.broadcasted_iota(jnp.int32, (1, 128), 1)
        for b in range(NBUCKET):
            sel = bk == b
            row = jnp.zeros((1, 128), F32)
            for h in range(NH):
                row = row + jnp.where(lane == h, jnp.sum(jnp.where(sel, db_ref[h], 0.0)), 0.0)
            o_ref[b:b + 1, :] = row

    return pl.pallas_call(
        body, name="rel_bias_grad",
        out_shape=jax.ShapeDtypeStruct((NBUCKET, 128), F32),
    )(dbias, bucket)


def _ssd_consts():
    head_of_lane = np.arange(SW) // HD
    expand = (np.arange(128)[:, None] == head_of_lane[None, :]).astype(np.float32)
    tril = np.tril(np.ones((BLK, BLK), np.float32))
    return (jnp.asarray(expand, BF16), jnp.asarray(expand.T.copy(), BF16), jnp.asarray(tril, BF16),
            jnp.asarray(tril.T.copy(), BF16))


def _conv_pre(xc, halo, cw, cb):
    ext = jnp.concatenate([halo, xc], axis=0)
    taps = [xc if k == CK - 1 else pltpu.roll(ext, CK - 1 - k, 0)[8:8 + BLK] for k in range(CK)]
    return cb + sum(cw[k:k + 1, :] * taps[k] for k in range(CK))


def _ssd_chunk(pre, dtr, dtb, av, dkv, ex, tril, h_in):
    sp = _sig(pre)
    xbc = pre * sp
    xs, bm, cm = xbc[:, 0:SW], xbc[:, SW:SW + 2 * NST], xbc[:, SW + 2 * NST:]
    dtin = dtr + dtb
    dt = jnp.maximum(dtin, 0.0) + jnp.log1p(jnp.exp(-jnp.abs(dtin)))
    cs = _sel_l(tril, dt * av)
    cst = cs.T
    dtx = _sel_r(dt, ex)
    csx = _sel_r(cs, ex)
    xdt = xs * dtx
    csl = csx[BLK - 1:BLK, :]
    decx = jnp.exp(csl - csx)
    ecsx = jnp.exp(csx)
    ecl = jnp.exp(csl)
    causal = tril.astype(F32) > 0.5
    ydiag, yoff, cbs, lms = [], [], [], []
    for g in range(2):
        bg = bm[:, g * NST:(g + 1) * NST].astype(BF16)
        cg = cm[:, g * NST:(g + 1) * NST].astype(BF16)
        cb = _mm_nt(cg, bg)
        cbs.append(cb)
        yoff.append(_mm(cg, h_in[:, g * 256:(g + 1) * 256].astype(BF16)))
        for r in range(4):
            h = 4 * g + r
            seg = cs[:, h:h + 1] - cst[h:h + 1, :]
            lm = jnp.where(causal, jnp.exp(jnp.minimum(seg, 0.0)), 0.0)
            lms.append(lm)
            ydiag.append(_mm((cb * lm).astype(BF16), xdt[:, h * HD:(h + 1) * HD].astype(BF16)))
    yoff = jnp.concatenate(yoff, axis=1) * ecsx
    y = jnp.concatenate(ydiag, axis=1) + yoff + dkv * xs
    return dict(pre=pre, sp=sp, xs=xs, bm=bm, cm=cm, dtin=dtin, dt=dt, av=av, cs=cs, cst=cst,
                dtx=dtx, csx=csx, xdt=xdt, decx=decx, ecsx=ecsx, ecl=ecl, causal=causal, cbs=cbs, lms=lms,
                yoff=yoff, y=y)


def _group_mean(t):
    m0 = jnp.mean(t[:, 0:256], axis=-1, keepdims=True)
    m1 = jnp.mean(t[:, 256:512], axis=-1, keepdims=True)
    return jnp.concatenate([jnp.broadcast_to(m0, (t.shape[0], 256)), jnp.broadcast_to(m1, (t.shape[0], 256))], axis=1)


SUBS = 4


def _ssd_fwd(z, xbc, dtr, cw, cb, dtb, av, dk, nw, carry=None):
    s = z.shape[0]
    nc = s // BLK
    tile = SUBS * BLK
    ex, _, tril, _ = _ssd_consts()

    def body(z_ref, xc_ref, xh_ref, dtr_ref, cw_ref, cb_ref, dtb_ref, a_ref, dk_ref, nw_ref, ex_ref, tril_ref,
             y_ref, hs_ref, pre_ref, h_ref):
        t = pl.program_id(0)

        @pl.when(t == 0)
        def _():
            h_ref[...] = jnp.zeros_like(h_ref)

        h_in = h_ref[...]
        for sub in range(SUBS):
            rows = slice(sub * BLK, (sub + 1) * BLK)
            xc = xc_ref[rows, :]
            halo = jnp.where(t == 0, 0.0, xh_ref[...]) if sub == 0 else xc_ref[sub * BLK - 8:sub * BLK, :]
            pre = _conv_pre(xc, halo, cw_ref[...], cb_ref[...])
            pre_ref[rows, :] = pre
            hs_ref[sub] = h_in
            f = _ssd_chunk(pre, dtr_ref[rows, :], dtb_ref[...], a_ref[...], dk_ref[...], ex_ref[...], tril_ref[...], h_in)
            dx = (f["decx"] * f["xdt"]).astype(BF16)
            st = [_mm_tn(f["bm"][:, g * NST:(g + 1) * NST].astype(BF16), dx[:, g * 256:(g + 1) * 256]) for g in range(2)]
            h_in = h_in * f["ecl"] + jnp.concatenate(st, axis=1)
            zv = z_ref[rows, :]
            tg = f["y"] * (zv * _sig(zv))
            r = lax.rsqrt(_group_mean(tg * tg) + EPS)
            y_ref[rows, :] = (tg * r * nw_ref[...]).astype(BF16)
        h_ref[...] = h_in

    cur = lambda w: pl.BlockSpec((tile, w), lambda t: (t, 0))
    return _pcall(
        body, [z, xbc, xbc, dtr, cw, cb, dtb, av, dk, nw, ex, tril], name="ssd_fwd", grid=(s // tile,),
        out_shape=[jax.ShapeDtypeStruct((s, SW), BF16), jax.ShapeDtypeStruct((nc, NST, SW), F32),
                   jax.ShapeDtypeStruct((s, XBCW), F32)],
        in_specs=[cur(SW), cur(XBCW), pl.BlockSpec((8, XBCW), lambda t: (jnp.maximum(t * (tile // 8) - 1, 0), 0)),
                  cur(128), _row((8, XBCW)), _row((1, XBCW)), _row((1, 128)),
                  _row((1, 128)), _row((1, SW)), _row((1, SW)), _row((128, SW)), _row((BLK, BLK))],
        out_specs=[cur(SW), pl.BlockSpec((SUBS, NST, SW), lambda t: (t, 0, 0)), cur(XBCW)],
        scratch_shapes=[pltpu.VMEM((NST, SW), F32)], carry=carry)


def _ssd_bwd(z, xbc, pre_all, dtr, dys, hs, cw, dtb, av, dk, nw, carry=None):
    s = z.shape[0]
    tile = SUBS * BLK
    nt = s // tile
    ex, ext_t, tril, triu = _ssd_consts()

    def body(z_ref, xc_ref, pre_ref, dtr_ref, dy_ref, hs_ref, cw_ref, dtb_ref, a_ref, dk_ref, nw_ref,
             ex_ref, ext_ref, tril_ref, triu_ref,
             dz_ref, dxbc_ref, ddt_ref, dcw_ref, dcb_ref, dnw_ref, dhd_ref, dh_ref, nxt_ref, dd_ref):
        i = pl.program_id(0)

        @pl.when(i == 0)
        def _():
            dh_ref[...] = jnp.zeros_like(dh_ref)
            nxt_ref[...] = jnp.zeros_like(nxt_ref)
            dd_ref[...] = jnp.zeros_like(dd_ref)
            dcw_ref[...] = jnp.zeros_like(dcw_ref)
            dcb_ref[...] = jnp.zeros_like(dcb_ref)
            dnw_ref[...] = jnp.zeros_like(dnw_ref)
            dhd_ref[...] = jnp.zeros_like(dhd_ref)

        gst, nxt = dh_ref[...], nxt_ref[...]
        for sub in reversed(range(SUBS)):
            rows = slice(sub * BLK, (sub + 1) * BLK)
            gst, nxt = chunk(sub, rows, gst, nxt, z_ref, xc_ref, pre_ref, dtr_ref, dy_ref, hs_ref, cw_ref, dtb_ref,
                             a_ref, dk_ref, nw_ref, ex_ref, ext_ref, tril_ref, triu_ref,
                             dz_ref, dxbc_ref, ddt_ref, dcw_ref, dcb_ref, dnw_ref, dhd_ref, dd_ref)
        dh_ref[...] = gst
        nxt_ref[...] = nxt

        @pl.when(i == nt - 1)
        def _():
            dhd_ref[2:3, :] = _sel_r(dd_ref[...], ext_ref[...])[0:1, :]

    def chunk(sub, rows, gst, nxt, z_ref, xc_ref, pre_ref, dtr_ref, dy_ref, hs_ref, cw_ref, dtb_ref,
              a_ref, dk_ref, nw_ref, ex_ref, ext_ref, tril_ref, triu_ref,
              dz_ref, dxbc_ref, ddt_ref, dcw_ref, dcb_ref, dnw_ref, dhd_ref, dd_ref):
        h_in = hs_ref[sub]
        f = _ssd_chunk(pre_ref[rows, :], dtr_ref[rows, :], dtb_ref[...], a_ref[...], dk_ref[...], ex_ref[...],
                       tril_ref[...], h_in)
        xs, xdt, decx, ecsx, ecl, dtx = f["xs"], f["xdt"], f["decx"], f["ecsx"], f["ecl"], f["dtx"]
        cs, cst, causal = f["cs"], f["cst"], f["causal"]
        causal_t = triu_ref[...].astype(F32) > 0.5

        zv = z_ref[rows, :]
        sz = _sig(zv)
        gz = zv * sz
        t = f["y"] * gz
        r = lax.rsqrt(_group_mean(t * t) + EPS)
        tn_ = t * r
        dyn = dy_ref[rows, :]
        dnw_ref[0:1, :] += jnp.sum(dyn * tn_, axis=0, keepdims=True)
        u = dyn * nw_ref[...]
        dt_ = r * u - tn_ * (r * _group_mean(u * tn_))
        dy = dt_ * gz
        dz_ref[rows, :] = (dt_ * f["y"] * (sz * (1.0 + zv * (1.0 - sz)))).astype(BF16)

        dd_ref[0:1, :] += jnp.sum(dy * xs, axis=0, keepdims=True)
        dxs = dk_ref[...] * dy

        edy = ecsx * dy
        dxdt, dbs, dcs_, dcsx_parts, dh_new = [], [], [], [], []
        lane = lax.broadcasted_iota(jnp.int32, (1, 128), 1)
        dcs_intra = jnp.zeros((BLK, 128), F32)
        for g in range(2):
            sl = slice(g * 256, (g + 1) * 256)
            bgf, cgf = f["bm"][:, g * NST:(g + 1) * NST], f["cm"][:, g * NST:(g + 1) * NST]
            bg, cg = bgf.astype(BF16), cgf.astype(BF16)
            gg = gst[:, sl].astype(BF16)
            hg = h_in[:, sl].astype(BF16)
            edyg = edy[:, sl].astype(BF16)
            dc = _mm_nt(edyg, hg)
            dh_new.append(gst[:, sl] * ecl[:, sl] + _mm_tn(cg, edyg))
            bgm = _mm(bg, gg)
            dxdt_g = decx[:, sl] * bgm
            dxg = (decx[:, sl] * xdt[:, sl]).astype(BF16)
            db = _mm_nt(dxg, gg)
            qd = bgm * xdt[:, sl] * decx[:, sl]
            last = jnp.sum(qd, axis=0, keepdims=True) + ecl[:, sl] * jnp.sum(gst[:, sl] * h_in[:, sl], axis=0, keepdims=True)
            rowid = lax.broadcasted_iota(jnp.int32, (BLK, 256), 0)
            dcsx_parts.append(f["yoff"][:, sl] * dy[:, sl] - qd + jnp.where(rowid == BLK - 1, last, 0.0))
            cb_ = f["cbs"][g]
            cbt = _mm_nt(bg, cg)
            dcb_ = jnp.zeros((BLK, BLK), F32)
            dcbt = jnp.zeros((BLK, BLK), F32)
            dxd = []
            for r_ in range(4):
                h = 4 * g + r_
                hl = slice(h * HD, (h + 1) * HD)
                lm = f["lms"][h]
                segt = cst[h:h + 1, :] - cs[:, h:h + 1]
                lmt = jnp.where(causal_t, jnp.exp(jnp.minimum(segt, 0.0)), 0.0)
                dyh = dy[:, hl].astype(BF16)
                xdh = xdt[:, hl].astype(BF16)
                dw = _mm_nt(dyh, xdh)
                dwt = _mm_nt(xdh, dyh)
                wt = cbt * lmt
                dxd.append(_mm(wt.astype(BF16), dyh))
                dcb_ = dcb_ + dw * lm
                dcbt = dcbt + dwt * lmt
                col = jnp.sum(dw * (cb_ * lm), axis=-1, keepdims=True) - jnp.sum(dwt * wt, axis=-1, keepdims=True)
                dcs_intra = dcs_intra + jnp.where(lane == h, col, 0.0)
            dxdt.append(dxdt_g + jnp.concatenate(dxd, axis=1))
            dcs_.append(dc + _mm(dcb_.astype(BF16), bg))
            dbs.append(db + _mm(dcbt.astype(BF16), cg))
        dxdt = jnp.concatenate(dxdt, axis=1)
        dxs = dxs + dxdt * dtx
        ext_t_ = ext_ref[...]
        dcs = dcs_intra + _sel_r(jnp.concatenate(dcsx_parts, axis=1), ext_t_)
        da = _sel_l(triu_ref[...], dcs)
        ddt = da * f["av"] + _sel_r(dxdt * xs, ext_t_)
        dhd_ref[1:2, :] += jnp.sum(da * f["dt"], axis=0, keepdims=True)
        ddtr = ddt * _sig(f["dtin"])
        dhd_ref[0:1, :] += jnp.sum(ddtr, axis=0, keepdims=True)
        ddt_ref[rows, :] = ddtr.astype(BF16)

        sp, pre = f["sp"], f["pre"]
        dact = jnp.concatenate([dxs] + dbs + dcs_, axis=1)
        dpre = dact * (sp * (1.0 + pre * (1.0 - sp)))
        dcb_ref[0:1, :] += jnp.sum(dpre, axis=0, keepdims=True)
        ext2 = jnp.concatenate([dpre, nxt], axis=0)
        shifted = [pltpu.roll(ext2, BLK + 8 - (CK - 1 - k), 0)[0:BLK] for k in range(CK - 1)] + [dpre]
        cw = cw_ref[...]
        xc = xc_ref[rows, :]
        dxr = cw[CK - 1:CK, :] * dpre
        for k in range(CK):
            dcw_ref[k:k + 1, :] += jnp.sum(shifted[k] * xc, axis=0, keepdims=True)
            if k < CK - 1:
                dxr = dxr + cw[k:k + 1, :] * shifted[k]
        dxbc_ref[rows, :] = dxr.astype(BF16)
        return jnp.concatenate(dh_new, axis=1), dpre[0:8]

    cur = lambda w: pl.BlockSpec((tile, w), lambda i: (nt - 1 - i, 0))
    return _pcall(
        body, [z, xbc, pre_all, dtr, dys, hs, cw, dtb, av, dk, nw, ex, ext_t, tril, triu], name="ssd_bwd", grid=(nt,),
        out_shape=[jax.ShapeDtypeStruct((s, SW), BF16), jax.ShapeDtypeStruct((s, XBCW), BF16),
                   jax.ShapeDtypeStruct((s, 128), BF16), jax.ShapeDtypeStruct((8, XBCW), F32),
                   jax.ShapeDtypeStruct((8, XBCW), F32), jax.ShapeDtypeStruct((8, SW), F32),
                   jax.ShapeDtypeStruct((8, 128), F32)],
        in_specs=[cur(SW), cur(XBCW), cur(XBCW), cur(128), cur(SW),
                  pl.BlockSpec((SUBS, NST, SW), lambda i: (nt - 1 - i, 0, 0)),
                  _row((8, XBCW)), _row((1, 128)), _row((1, 128)), _row((1, SW)), _row((1, SW)),
                  _row((128, SW)), _row((SW, 128)), _row((BLK, BLK)), _row((BLK, BLK))],
        out_specs=[cur(SW), cur(XBCW), cur(128), _row((8, XBCW)), _row((8, XBCW)), _row((8, SW)), _row((8, 128))],
        scratch_shapes=[pltpu.VMEM((NST, SW), F32), pltpu.VMEM((8, XBCW), F32), pltpu.VMEM((8, SW), F32)], carry=carry)


def _load_once(i, pairs, sem):
    @pl.when(i == 0)
    def _():
        cps = [pltpu.make_async_copy(src, dst, sem.at[k]) for k, (src, dst) in enumerate(pairs)]
        for cp in cps:
            cp.start()
        for cp in cps:
            cp.wait()


def _mlp_fwd(x, ya, ys, tgt, w_o, w_ga, w_gb, w_dn, gate1, a2, sh2, gate2, fn):
    s = x.shape[0]
    sub_m, subs = 256, 2
    tm = sub_m * subs

    def body(x_ref, ya_ref, ys_ref, t_ref, wo_hbm, wga_hbm, wgb_hbm, wdn_hbm, g1_ref, a2_ref, s2_ref, g2_ref, fn_ref,
             x1_ref, gu_ref, dx2_ref, loss_ref, dfn_ref, wo, wga, wgb, wdn, sem):
        i = pl.program_id(0)
        _load_once(i, [(wo_hbm, wo), (wga_hbm, wga), (wgb_hbm, wgb), (wdn_hbm, wdn)], sem)

        @pl.when(i == 0)
        def _():
            loss_ref[...] = jnp.zeros_like(loss_ref)
            dfn_ref[...] = jnp.zeros_like(dfn_ref)

        def proj(st):
            st["mix"] = _mm(ya_ref[st["rows"], :], wo[0:QW, :]) + _mm(ys_ref[st["rows"], :], wo[QW:D, :])

        def norm(st):
            x1 = x_ref[st["rows"], :] + g1_ref[...] * st.pop("mix")
            x1_ref[st["rows"], :] = x1
            r2 = lax.rsqrt(jnp.mean(x1 * x1, axis=-1, keepdims=True) + EPS)
            st["x1"] = x1
            st["h2"] = (x1 * r2 * a2_ref[...] + s2_ref[...]).astype(BF16)

        def gate_up(st):
            h2 = st.pop("h2")
            ha, hb = h2[:, 0:D // 2], h2[:, D // 2:D]
            gub = jnp.concatenate([(_mm(ha, wga[j]) + _mm(hb, wgb[j])).astype(BF16) for j in range(4)], axis=1)
            gu_ref[st["rows"], :] = gub
            st["gub"] = gub

        def activate(st):
            gub = st.pop("gub")
            gv, uv = gub[:, 0:DFF].astype(F32), gub[:, DFF:].astype(F32)
            st["act"] = (gv * _sig(gv) * uv).astype(BF16)

        def down(st):
            st["ff"] = _mm(st.pop("act"), wdn[...])

        def head(st):
            x2 = st.pop("x1") + g2_ref[...] * st.pop("ff")
            r3 = lax.rsqrt(jnp.mean(x2 * x2, axis=-1, keepdims=True) + EPS)
            xn = x2 * r3
            fnv = fn_ref[...]
            err = xn * fnv - t_ref[st["rows"], :]
            st["loss"] = jnp.sum(err * err) * (0.5 / D)
            dy = err * (1.0 / D)
            st["dfn"] = jnp.sum(dy * xn, axis=0, keepdims=True)
            u = dy * fnv
            dx2_ref[st["rows"], :] = r3 * u - xn * (r3 * jnp.mean(u * xn, axis=-1, keepdims=True))

        a, b = [dict(rows=slice(k * sub_m, (k + 1) * sub_m)) for k in range(subs)]
        for stage, st in [(proj, a), (norm, a), (proj, b), (gate_up, a), (norm, b), (activate, a), (gate_up, b),
                          (down, a), (activate, b), (head, a), (down, b), (head, b)]:
            stage(st)
        loss_ref[...] += a["loss"] + b["loss"]
        dfn_ref[0:1, :] += a["dfn"] + b["dfn"]

    def tok(w):
        return pl.BlockSpec((tm, w), lambda i: (i, 0))

    hbm = pl.BlockSpec(memory_space=pl.ANY)
    return pl.pallas_call(
        body, name="mlp_fwd", grid=(s // tm,),
        out_shape=[jax.ShapeDtypeStruct((s, D), F32), jax.ShapeDtypeStruct((s, 2 * DFF), BF16),
                   jax.ShapeDtypeStruct((s, D), F32), jax.ShapeDtypeStruct((8, 128), F32),
                   jax.ShapeDtypeStruct((8, D), F32)],
        in_specs=[tok(D), tok(QW), tok(SW), tok(D), hbm, hbm, hbm, hbm,
                  _row((1, D)), _row((1, D)), _row((1, D)), _row((1, D)), _row((1, D))],
        out_specs=[tok(D), tok(2 * DFF), tok(D), _row((8, 128)), _row((8, D))],
        scratch_shapes=[pltpu.VMEM((D, D), BF16), pltpu.VMEM(w_ga.shape, BF16), pltpu.VMEM(w_gb.shape, BF16),
                        pltpu.VMEM((DFF, D), BF16), pltpu.SemaphoreType.DMA((4,))],
        compiler_params=_cp(("arbitrary",)),
    )(x, ya, ys, tgt, w_o, w_ga, w_gb, w_dn, gate1, a2, sh2, gate2, fn)


def _mlp_bwd(x1, gu, dx2, w_o, w_ga, w_gb, w_dn, gate1, a2, sh2, gate2):
    s = x1.shape[0]
    tm = 256
    nj = 2 * DFF // 4

    def body(x1_ref, gu_ref, dx2_ref, wo_hbm, wga_hbm, wgb_hbm, wdn_hbm, g1_ref, a2_ref, s2_ref, g2_ref,
             dx1_ref, dya_ref, dys_ref, act_ref, dgu_ref, h2_ref, dsh_ref, p_ref, wo, wga, wgb, wdn, sem):
        i = pl.program_id(0)
        _load_once(i, [(wo_hbm, wo), (wga_hbm, wga), (wgb_hbm, wgb), (wdn_hbm, wdn)], sem)

        @pl.when(i == 0)
        def _():
            dsh_ref[...] = jnp.zeros_like(dsh_ref)
            p_ref[...] = jnp.zeros_like(p_ref)

        dx2 = dx2_ref[...]
        dact = _mm_nt((dx2 * g2_ref[...]).astype(BF16), wdn[...])
        gub = gu_ref[...]
        gv, uv = gub[:, 0:DFF].astype(F32), gub[:, DFF:].astype(F32)
        sg = _sig(gv)
        sl = gv * sg
        act_ref[...] = (sl * uv).astype(BF16)
        dgu = jnp.concatenate([dact * uv * (sg * (1.0 + gv * (1.0 - sg))), dact * sl], axis=1).astype(BF16)
        dgu_ref[...] = dgu
        dha = sum(_mm_nt(dgu[:, j * nj:(j + 1) * nj], wga[j]) for j in range(4))
        dhb = sum(_mm_nt(dgu[:, j * nj:(j + 1) * nj], wgb[j]) for j in range(4))
        dh = jnp.concatenate([dha, dhb], axis=1)
        x1 = x1_ref[...]
        r2 = lax.rsqrt(jnp.mean(x1 * x1, axis=-1, keepdims=True) + EPS)
        xn = x1 * r2
        a2 = a2_ref[...]
        h2_ref[...] = (xn * a2 + s2_ref[...]).astype(BF16)
        dsh_ref[0:1, :] += jnp.sum(dh, axis=0, keepdims=True)
        p_ref[0:1, :] += jnp.sum(dh * xn, axis=0, keepdims=True)
        u = dh * a2
        dx1 = dx2 + r2 * u - xn * (r2 * jnp.mean(u * xn, axis=-1, keepdims=True))
        dx1_ref[...] = dx1
        dcat = _mm_nt((dx1 * g1_ref[...]).astype(BF16), wo[...])
        dya_ref[...] = dcat[:, 0:QW]
        dys_ref[...] = dcat[:, QW:D]

    def tok(w):
        return pl.BlockSpec((tm, w), lambda i: (i, 0))

    hbm = pl.BlockSpec(memory_space=pl.ANY)
    return pl.pallas_call(
        body, name="mlp_bwd", grid=(s // tm,),
        out_shape=[jax.ShapeDtypeStruct((s, D), F32), jax.ShapeDtypeStruct((s, QW), F32),
                   jax.ShapeDtypeStruct((s, SW), F32), jax.ShapeDtypeStruct((s, DFF), BF16),
                   jax.ShapeDtypeStruct((s, 2 * DFF), BF16), jax.ShapeDtypeStruct((s, D), BF16),
                   jax.ShapeDtypeStruct((8, D), F32), jax.ShapeDtypeStruct((8, D), F32)],
        in_specs=[tok(D), tok(2 * DFF), tok(D), hbm, hbm, hbm, hbm, _row((1, D)), _row((1, D)), _row((1, D)), _row((1, D))],
        out_specs=[tok(D), tok(QW), tok(SW), tok(DFF), tok(2 * DFF), tok(D), _row((8, D)), _row((8, D))],
        scratch_shapes=[pltpu.VMEM((D, D), BF16), pltpu.VMEM(w_ga.shape, BF16), pltpu.VMEM(w_gb.shape, BF16),
                        pltpu.VMEM((DFF, D), BF16), pltpu.SemaphoreType.DMA((4,))],
        compiler_params=_cp(("arbitrary",)),
    )(x1, gu, dx2, w_o, w_ga, w_gb, w_dn, gate1, a2, sh2, gate2)


def _wgrad(name, a, b, gate, w, carry=None):
    s, m = a.shape
    n = b.shape[1]
    tk = min(1024, s)
    nk = s // tk

    def body(a_ref, b_ref, g_ref, w_ref, o_hbm, dg_ref, acc_ref, sem):
        k = pl.program_id(0)

        @pl.when(k == 0)
        def _():
            acc_ref[...] = jnp.zeros_like(acc_ref)

        acc_ref[...] += _mm_tn(a_ref[...], b_ref[...].astype(BF16))

        @pl.when(k == nk - 1)
        def _():
            acc = acc_ref[...]
            dg_ref[...] = jnp.zeros_like(dg_ref)
            dg_ref[0:1, :] = jnp.sum(acc * w_ref[...].astype(F32), axis=0, keepdims=True)
            acc_ref[...] = acc * g_ref[...]
            cp = pltpu.make_async_copy(acc_ref, o_hbm, sem)
            cp.start()
            cp.wait()

    return _pcall(body, [a, b, gate, w], name=name, grid=(nk,),
                  out_shape=[jax.ShapeDtypeStruct((m, n), F32), jax.ShapeDtypeStruct((8, n), F32)],
                  in_specs=[pl.BlockSpec((tk, m), lambda k: (k, 0)), pl.BlockSpec((tk, n), lambda k: (k, 0)),
                            _row((1, n)), _row((m, n))],
                  out_specs=[pl.BlockSpec(memory_space=pl.ANY), _row((8, n))],
                  scratch_shapes=[pltpu.VMEM((m, n), F32), pltpu.SemaphoreType.DMA], carry=carry)


def _wgrad_gate_up(h2, dgu, carry=None):
    s = h2.shape[0]
    tk = min(1024, s)
    nk = s // tk
    n = dgu.shape[1]
    nj = n // 4

    def body(a_ref, b_ref, o_hbm, acc_ref, sems):
        k = pl.program_id(0)

        @pl.when(k == 0)
        def _():
            acc_ref[...] = jnp.zeros_like(acc_ref)

        acc_ref[...] += _mm_tn(a_ref[...], b_ref[...])

        @pl.when(k == nk - 1)
        def _():
            cps = [pltpu.make_async_copy(acc_ref.at[:, pl.ds(j * nj, nj)], o_hbm.at[j], sems.at[j]) for j in range(4)]
            for cp in cps:
                cp.start()
            for cp in cps:
                cp.wait()

    return _pcall(body, [h2, dgu], name="wgrad_gate_up", grid=(nk,),
                  out_shape=[jax.ShapeDtypeStruct((4, D, nj), F32)],
                  in_specs=[pl.BlockSpec((tk, D), lambda k: (k, 0)), pl.BlockSpec((tk, n), lambda k: (k, 0))],
                  out_specs=[pl.BlockSpec(memory_space=pl.ANY)],
                  scratch_shapes=[pltpu.VMEM((D, n), F32), pltpu.SemaphoreType.DMA((4,))], carry=carry)


def _wgrad_in_t(h1, pieces, carry=None):
    s = h1.shape[0]
    rows = IN_W // 4
    tk = min(1024, s)
    nk = s // tk

    def body(a_ref, dq_ref, dkv_ref, dz_ref, dxbc_ref, ddt_ref, o_hbm, acc_ref, tr_ref, sl_ref, sem):
        k = pl.program_id(0)

        @pl.when(k == 0)
        def _():
            acc_ref[...] = jnp.zeros_like(acc_ref)

        dproj = jnp.concatenate([dq_ref[...], dkv_ref[...], dz_ref[...], dxbc_ref[...], ddt_ref[...]], axis=1)
        acc_ref[...] += _mm_tn(a_ref[...], dproj)

        @pl.when(k == nk - 1)
        def _():
            for j in range(PROJ_W // 128):
                tr_ref[j * 128:(j + 1) * 128, :] = acc_ref[:, j * 128:(j + 1) * 128].T
            for j in range(4):
                sl_ref[j] = tr_ref[j * rows:(j + 1) * rows, :]
            cp = pltpu.make_async_copy(sl_ref, o_hbm, sem)
            cp.start()
            cp.wait()

    return _pcall(body, [h1] + list(pieces), name="wgrad_in", grid=(nk,),
                  out_shape=[jax.ShapeDtypeStruct((4, rows, D), F32)],
                  in_specs=[pl.BlockSpec((tk, p.shape[1]), lambda k: (k, 0)) for p in [h1] + list(pieces)],
                  out_specs=[pl.BlockSpec(memory_space=pl.ANY)],
                  scratch_shapes=[pltpu.VMEM((D, PROJ_W), F32), pltpu.VMEM((PROJ_W, D), F32),
                                  pltpu.VMEM((4, rows, D), F32), pltpu.SemaphoreType.DMA],
                  carry=carry)


_SMALL = ["ada_b", "norm1", "conv_w", "conv_b", "dt_bias", "A_log", "D_skip", "sinks", "attn_out_norm",
          "ssm_out_norm", "norm2", "rel_bias", "final_norm"]


def _small_grad(name, gs, chip):
    if name == "ada_b":
        return jnp.concatenate([gs[j:j + 1, :] for j in range(6)], axis=1)
    if name == "conv_w":
        full = gs[7:11, :]
        out = full[:, 0:256]
        for j in range(1, 4):
            out = jnp.where(chip == j, full[:, j * 256:(j + 1) * 256], out)
        return out
    row, width = {"norm1": (6, D), "conv_b": (11, D), "norm2": (12, D), "final_norm": (13, D),
                  "attn_out_norm": (14, QW), "ssm_out_norm": (15, SW), "dt_bias": (16, NH), "A_log": (17, NH),
                  "D_skip": (18, NH), "sinks": (19, NH), "rel_bias": (24, NH)}[name]
    rows = NBUCKET if name == "rel_bias" else 1
    return gs[row:row + rows, 0:width]


def _small_update(small_all, where, ws, ms, vs):
    n = len(_SMALL)

    def body(where_ref, sa_ref, *refs):
        w_refs, m_refs, v_refs, outs = refs[:n], refs[n:2 * n], refs[2 * n:3 * n], refs[3 * n:]
        gs = sa_ref[0]
        for b in range(1, 8):
            gs = gs + sa_ref[b]
        chip = where_ref[1]
        for i, name in enumerate(_SMALL):
            g = _small_grad(name, gs, chip)
            lead = (0,) if name == "conv_w" else ()
            d, mo, vo = _adamw(w_refs[i][lead + (...,)], g, m_refs[i][lead + (...,)], v_refs[i][lead + (...,)])
            for k, val in enumerate((g, d, mo, vo)):
                outs[k * n + i][lead + (...,)] = val
        outs[4 * n][...] = gs[20:21, 0:128]

    shapes = [jax.ShapeDtypeStruct(w.shape, F32) for w in ws]
    vmem = pl.BlockSpec(memory_space=pltpu.VMEM)
    res = pl.pallas_call(
        body, name="small_update", out_shape=shapes * 4 + [jax.ShapeDtypeStruct((1, 128), F32)],
        in_specs=[pl.BlockSpec(memory_space=pltpu.SMEM)] + [vmem] * (1 + 3 * n), out_specs=[vmem] * (4 * n + 1),
    )(where, small_all, *ws, *ms, *vs)
    return [res[k * n:(k + 1) * n] for k in range(4)], res[4 * n][0, 0]


def _add_half(name, g, got, where, by_cols=False):
    rr, cc = got.shape[1:]
    if by_cols:
        mine = pl.BlockSpec((None, rr, cc), lambda i, w_ref: (i, 0, w_ref[0]))
    else:
        mine = pl.BlockSpec((None, None, rr, cc), lambda i, w_ref: (i, w_ref[0], 0, 0))

    def body(w_ref, g_ref, r_ref, o_ref, own_ref):
        s = g_ref[...] + r_ref[...]
        o_ref[...] = s.astype(BF16)

        @pl.when(pl.program_id(0) == w_ref[1])
        def _():
            own_ref[...] = s

    spec = pl.BlockSpec((None, rr, cc), lambda i, w_ref: (i, 0, 0))
    return _pcall(body, [where, g, got], name=name, grid=(4,), nprefetch=1,
                  out_shape=[jax.ShapeDtypeStruct(got.shape, BF16), jax.ShapeDtypeStruct((rr, cc), F32)],
                  in_specs=[mine, spec],
                  out_specs=[spec, pl.BlockSpec((rr, cc), lambda i, w_ref: (0, 0))])


def _add_chips(name, own, got):
    rr, cc = own.shape
    tr = rr // 2 if rr % 32 == 0 else rr

    def body(s_ref, r_ref, o_ref):
        o_ref[...] = ((s_ref[...] + r_ref[0].astype(F32)) + r_ref[1].astype(F32)) + r_ref[2].astype(F32)

    spec = pl.BlockSpec((tr, cc), lambda i: (i, 0))
    return _pcall(body, [own, got], name=name, grid=(rr // tr,), out_shape=[jax.ShapeDtypeStruct((rr, cc), F32)],
                  in_specs=[spec, pl.BlockSpec((3, tr, cc), lambda i: (0, i, 0))], out_specs=[spec])[0]


def _adamw_halves(name, mine, got, w, m, v, where, by_cols=False):
    rr, cc = mine.shape

    def body(w_ref_, t_ref, r_ref, w_ref, m_ref, v_ref, g_ref, d_ref, mo_ref, vo_ref):
        g = jnp.where(pl.program_id(0) == w_ref_[0], t_ref[...], r_ref[...])
        g_ref[...] = g
        d_ref[...], mo_ref[...], vo_ref[...] = _adamw(w_ref[...], g, m_ref[...], v_ref[...])

    if by_cols:
        nt = cc // 128
        grid = (2, nt)
        half = pl.BlockSpec((rr, 128), lambda h, i, w_ref_: (0, i))
        full = pl.BlockSpec((rr, 128), lambda h, i, w_ref_: (0, nt * h + i))
    else:
        nt = 8 if rr % 64 == 0 else 4
        tr = rr // nt
        grid = (2, nt)
        half = pl.BlockSpec((tr, cc), lambda h, i, w_ref_: (i, 0))
        full = pl.BlockSpec((None, tr, cc), lambda h, i, w_ref_: (0, nt * h + i, 0))
    return _pcall(body, [where, mine, got, w, m, v], name=name, grid=grid, nprefetch=1,
                  out_shape=[jax.ShapeDtypeStruct(w.shape, F32)] * 4,
                  in_specs=[half, half, full, full, full], out_specs=[full] * 4)


def _bias_table(rel_bias, bucket, mask):
    def body(rb_ref, bk_ref, mk_ref, o_ref):
        bk = bk_ref[...]
        valid = mk_ref[...] > 0
        for h in range(NH):
            acc = jnp.zeros((BLK, 2 * BLK), F32)
            for b in range(NBUCKET):
                acc = jnp.where(bk == b, rb_ref[b, h], acc)
            o_ref[h] = jnp.where(valid, acc, NEG)

    vmem = pl.BlockSpec(memory_space=pltpu.VMEM)
    return pl.pallas_call(
        body, name="bias_table", out_shape=jax.ShapeDtypeStruct((NH, BLK, 2 * BLK), F32),
        in_specs=[pl.BlockSpec(memory_space=pltpu.SMEM), vmem, vmem], out_specs=vmem,
    )(rel_bias, bucket, mask)


def _pack_small(dsh1, p1, dsh2, p2, dg1a, dg1b, dg2, norm1, norm2, scale1, scale2, dcw, dcb, dfn,
                dnw_attn, dnw_ssm, dhd, av, dsink, drel, loss_acc):
    def body(dsh1_ref, p1_ref, dsh2_ref, p2_ref, dg1a_ref, dg1b_ref, dg2_ref, n1_ref, n2_ref, s1_ref, s2_ref,
             dcw_ref, dcb_ref, dfn_ref, da_ref, ds_ref, dhd_ref, av_ref, dsink_ref, drel_ref, loss_ref, o_ref):
        o_ref[...] = jnp.zeros_like(o_ref)
        p1v, p2v = p1_ref[0:1, :], p2_ref[0:1, :]
        o_ref[0:1, :] = dsh1_ref[0:1, :]
        o_ref[1:2, :] = p1v * n1_ref[...]
        o_ref[2:3, :] = dg1a_ref[0:1, :] + dg1b_ref[0:1, :]
        o_ref[3:4, :] = dsh2_ref[0:1, :]
        o_ref[4:5, :] = p2v * n2_ref[...]
        o_ref[5:6, :] = dg2_ref[0:1, :]
        o_ref[6:7, :] = p1v * (1.0 + s1_ref[...])
        o_ref[7:11, :] = dcw_ref[0:4, :]
        o_ref[11:12, :] = dcb_ref[0:1, :]
        o_ref[12:13, :] = p2v * (1.0 + s2_ref[...])
        o_ref[13:14, :] = dfn_ref[0:1, :]
        o_ref[14:15, 0:QW] = da_ref[0:1, :]
        o_ref[15:16, 0:SW] = ds_ref[0:1, :]
        o_ref[16:17, 0:128] = dhd_ref[0:1, :]
        o_ref[17:18, 0:128] = dhd_ref[1:2, :] * av_ref[...]
        o_ref[18:19, 0:128] = dhd_ref[2:3, :]
        o_ref[19:20, 0:128] = dsink_ref[0:1, :]
        o_ref[20:21, 0:128] = loss_ref[0:1, :]
        o_ref[24:56, 0:128] = drel_ref[...]

    return pl.pallas_call(body, name="pack_small", out_shape=jax.ShapeDtypeStruct((56, D), F32))(
        dsh1, p1, dsh2, p2, dg1a, dg1b, dg2, norm1, norm2, scale1, scale2, dcw, dcb, dfn,
        dnw_attn, dnw_ssm, dhd, av, dsink, drel, loss_acc)


def _pad_row(a, rows=1):
    return jnp.pad(a.reshape(rows, -1), ((0, 0), (0, D - a.size // rows)))


def kernel(x, c, ada_w, ada_b, norm1, w_in, conv_w, conv_b, dt_bias, A_log, D_skip, sinks, attn_out_norm, ssm_out_norm, w_o, norm2, w_gate_up, w_down, rel_bias, final_norm, loss_target, m_ada_w, m_ada_b, m_norm1, m_w_in, m_conv_w, m_conv_b, m_dt_bias, m_A_log, m_D_skip, m_sinks, m_attn_out_norm, m_ssm_out_norm, m_w_o, m_norm2, m_w_gate_up, m_w_down, m_rel_bias, m_final_norm, v_ada_w, v_ada_b, v_norm1, v_w_in, v_conv_w, v_conv_b, v_dt_bias, v_A_log, v_D_skip, v_sinks, v_attn_out_norm, v_ssm_out_norm, v_w_o, v_norm2, v_w_gate_up, v_w_down, v_rel_bias, v_final_norm):
    xi, yi, ci = lax.axis_index("x"), lax.axis_index("y"), lax.axis_index("c")
    chip = 2 * xi + yi
    me = 4 * xi + 2 * yi + ci
    where = jnp.stack([ci, chip]).astype(jnp.int32)
    xs2, tgt = x[0], loss_target[0]

    first = jnp.concatenate([c, _pad_row(conv_w[0], CK), jnp.zeros((3, D), F32)], axis=0)
    w_in_t, m_w_in_t, v_w_in_t = w_in[0].T, m_w_in[0].T, v_w_in[0].T
    w_in_b, w_o_b, w_dn_b = w_in_t.astype(BF16), w_o[0].astype(BF16), w_down[0].astype(BF16)
    w_gu_b = w_gate_up[0].astype(BF16)
    ncol = ada_w.shape[2]
    first_all, w_half, mod_all, w_other = _front(first, w_in_b, ada_w[0],
                                                 lax.dynamic_slice(ada_b, (0, chip * ncol), (1, ncol)))
    c_all = first_all[:, 0, :]
    cw_full = jnp.concatenate([first_all[2 * j, 1:1 + CK, 0:256] for j in range(4)], axis=1)
    w_lo = jnp.where(ci == 0, w_half, w_other)
    w_hi = jnp.where(ci == 0, w_other, w_half)
    w_in_f = jnp.pad(jnp.concatenate([w_lo, w_hi], axis=2).reshape(IN_W, D), ((0, PROJ_W - IN_W), (0, 0)))
    mod = lax.dynamic_slice(jnp.transpose(mod_all, (1, 0, 2)).reshape(8, 4 * ncol), (me, 0), (1, 4 * ncol))
    shift1, scale1, gate1, shift2, scale2, gate2 = [mod[:, j * D:(j + 1) * D] for j in range(6)]
    a1 = norm1 * (1.0 + scale1)
    a2 = norm2 * (1.0 + scale2)

    hdn = DFF // 8
    q, kv, z, xbc, dtr, w_o_g, w_dna_g = _in_proj_fwd(xs2, a1, shift1, w_in_f,
                                                      carry=_gather_chips_carry([w_o_b, w_dn_b[0:hdn]]))
    w_o_f = w_o_g.reshape(D, D)
    bucket, mask = _attn_geometry()
    bucket = jnp.asarray(bucket)
    bias = _bias_table(rel_bias, bucket, jnp.asarray(mask.astype(np.int32)))
    sinks1 = sinks[0]
    ya, w_ga_g = _attn_fwd(q, kv, bias, sinks1, attn_out_norm, carry=_gather_chips_carry([w_gu_b[0:D // 2]]))
    cw8 = jnp.concatenate([cw_full, jnp.zeros((4, XBCW), F32)], axis=0)
    dtb = _pad_row(dt_bias)[:, 0:128]
    av = _pad_row(-jnp.exp(A_log))[:, 0:128]
    dk = jnp.repeat(D_skip, HD, axis=1)
    ys, hs, pre, w_gb_g, w_dnb_g = _ssd_fwd(z, xbc, dtr, cw8, conv_b, dtb, av, dk, ssm_out_norm,
                                            carry=_gather_chips_carry([w_gu_b[D // 2:D], w_dn_b[hdn:2 * hdn]]))
    w_dn_f = jnp.stack([w_dna_g, w_dnb_g], axis=1).reshape(DFF, D)
    fn = final_norm[None, :]
    x1, gu, dx2, loss_acc, dfn = _mlp_fwd(xs2, ya, ys, tgt, w_o_f, w_ga_g, w_gb_g, w_dn_f, gate1, a2, shift2, gate2, fn)

    def to_sibling(p):
        return _Carry([p], [jax.ShapeDtypeStruct((4,) + p.shape[2:], F32)],
                      lambda x_, y_, c_: [(_SIBLING, 0, (j, 1 - c_), 0, j) for j in range(4)])

    def to_chips(s4):
        return _Carry([s4], [jax.ShapeDtypeStruct((3,) + s4.shape[1:], s4.dtype)],
                      lambda x_, y_, c_: [(f, 0, jnp.bitwise_xor(2 * x_ + y_, k + 1), 0, k) for k, f in enumerate(_CHIPS3)])

    def back(t):
        return _Carry([t[None]], [jax.ShapeDtypeStruct((1,) + t.shape, F32)], lambda x_, y_, c_: [(_SIBLING, 0, 0, 0, 0)])

    dx1, dya, dys, act, dgu, h2, dsh2, p2 = _mlp_bwd(x1, gu, dx2, w_o_f, w_ga_g, w_gb_g, w_dn_f, gate1, a2, shift2, gate2)
    p_gu = _wgrad_gate_up(h2, dgu)[0].reshape(4, 2, D // 2, 2 * DFF // 4)
    dq, dkv, dbias, dsink, dnw_attn, g_dn, dg2, got1_gu = _attn_bwd(
        q, kv, dya, bias, sinks1, attn_out_norm, act, dx2, gate2, w_dn_f, carry=to_sibling(p_gu))
    p_dn = g_dn.reshape(4, 2, DFF // 8, D)
    drel = _rel_bias_grad(dbias, bucket)
    s4_gu, own_gu = _add_half("rs_add_half_gu", p_gu, got1_gu, where)
    dz, dxbc, ddt, dcw, dcb, dnw_ssm, dhd, got2_gu, got1_dn = _ssd_bwd(
        z, xbc, pre, dtr, dys, hs, cw8, dtb, av, dk, ssm_out_norm, carry=_merge(to_chips(s4_gu), to_sibling(p_dn)))
    mine_gu = _add_chips("rs_add_chips_gu", own_gu, got2_gu)
    s4_dn, own_dn = _add_half("rs_add_half_dn", p_dn, got1_dn, where)
    grad_x, h1, dsh1, p1 = _in_proj_bwd(xs2, dx1, a1, shift1, w_in_f, dq, dkv, dz, dxbc, ddt)
    p_in, got2_dn, got3_gu = _wgrad_in_t(h1, [dq, dkv, dz, dxbc, ddt], carry=_merge(to_chips(s4_dn), back(mine_gu)))
    mine_dn = _add_chips("rs_add_chips_dn", own_dn, got2_dn)

    def to_sibling_cols(p):
        return _Carry([p], [jax.ShapeDtypeStruct(p.shape[:2] + (D // 2,), F32)],
                      lambda x_, y_, c_: [(_SIBLING, 0, (j, slice(None), pl.ds((1 - c_) * (D // 2), D // 2)), 0, j)
                                          for j in range(4)])

    no_dg1 = jnp.zeros((8, D), F32)
    small = _pack_small(dsh1, p1, dsh2, p2, no_dg1, no_dg1, dg2, norm1, norm2, scale1, scale2, dcw, dcb, dfn,
                        dnw_attn, dnw_ssm, dhd, av, dsink, drel, loss_acc)
    g_oa, dg1a, got1_in, got3_dn, small_all = _wgrad(
        "wgrad_o_attn", ya, dx1, gate1, w_o_f[0:QW],
        carry=_merge(to_sibling_cols(p_in), back(mine_dn), _gather8_carry(small)))
    s4_in, own_in = _add_half("rs_add_half_in", p_in, got1_in, where, by_cols=True)
    g_os, dg1b, got2_in = _wgrad("wgrad_o_ssm", ys, dx1, gate1, w_o_f[QW:D], carry=to_chips(s4_in))
    mine_in = _add_chips("rs_add_chips_in", own_in, got2_in)

    dg1_all, got3_in, mine_o, got3_o = _tail(g_oa, g_os, dg1a, dg1b, mine_in)
    small_all = small_all.at[:, 2, :].set(dg1_all[:, 0, :])
    small_res, loss = _small_update(
        small_all, where,
        [ada_b, norm1, conv_w, conv_b, dt_bias, A_log, D_skip, sinks, attn_out_norm, ssm_out_norm, norm2, rel_bias,
         final_norm[None, :]],
        [m_ada_b, m_norm1, m_conv_w, m_conv_b, m_dt_bias, m_A_log, m_D_skip, m_sinks, m_attn_out_norm,
         m_ssm_out_norm, m_norm2, m_rel_bias, m_final_norm[None, :]],
        [v_ada_b, v_norm1, v_conv_w, v_conv_b, v_dt_bias, v_A_log, v_D_skip, v_sinks, v_attn_out_norm,
         v_ssm_out_norm, v_norm2, v_rel_bias, v_final_norm[None, :]])
    small_out = [dict(zip(_SMALL, r)) for r in small_res]
    for r in small_out:
        r["final_norm"] = r["final_norm"][0]

    dmod_all = small_all[:, 0:6, :].reshape(8, 6 * D)
    dmod_loc = lax.dynamic_slice(dmod_all, (0, chip * ncol), (8, ncol))
    ada_out = _ada_bwd_adamw(c_all.T, dmod_loc, ada_w[0], m_ada_w[0], v_ada_w[0])

    big_gu = _adamw_halves("adamw_gate_up", mine_gu, got3_gu[0], w_gate_up, m_w_gate_up, v_w_gate_up, where)
    big_dn = _adamw_halves("adamw_down", mine_dn, got3_dn[0], w_down, m_w_down, v_w_down, where)
    big_o = _adamw_halves("adamw_o", mine_o, got3_o, w_o, m_w_o, v_w_o, where)
    big_in = [o.T[None] for o in _adamw_halves("adamw_in", mine_in, got3_in, w_in_t, m_w_in_t, v_w_in_t, where,
                                               by_cols=True)]
    big = [big_in, big_o, big_gu, big_dn]

    order = ["ada_w", "ada_b", "norm1", "w_in", "conv_w", "conv_b", "dt_bias", "A_log", "D_skip", "sinks",
             "attn_out_norm", "ssm_out_norm", "w_o", "norm2", "w_gate_up", "w_down", "rel_bias", "final_norm"]
    bigname = {"w_in": 0, "w_o": 1, "w_gate_up": 2, "w_down": 3}
    res = [loss, grad_x[None]]
    for kind in range(4):
        for nm in order:
            if nm == "ada_w":
                res.append(ada_out[kind][None])
            elif nm in bigname:
                res.append(big[bigname[nm]][kind])
            else:
                res.append(small_out[kind][nm])
    return tuple(res)
```

```python
import numpy as np
import jax
import jax.numpy as jnp
from jax import lax
from jax.experimental import pallas as pl
from jax.experimental.pallas import tpu as pltpu

F32, BF16 = jnp.float32, jnp.bfloat16
HI = lax.Precision.HIGHEST

D = 1024
QW, KVW = 512, 128
NH, HD, NKV = 8, 64, 2
SW = 512
NST = 128
XBCW = 1024
CK = 4
BLK = 128
DFF = 2816
IN_W = 2312
PROJ_W = 2432
EPS = 1e-6
NEG = -1e30
NBUCKET = 32

B1, B2, LR, AEPS, WD, STEP = 0.9, 0.999, 0.001, 1e-08, 0.01, 10

VMEM_LIMIT = 56 * 1024 * 1024

_NT = (((1,), (1,)), ((), ()))
_TN = (((0,), (0,)), ((), ()))


def _mm(a, b):
    return jnp.dot(a, b, preferred_element_type=F32)


def _mm_nt(a, b):
    return lax.dot_general(a, b, _NT, preferred_element_type=F32)


def _mm_tn(a, b):
    return lax.dot_general(a, b, _TN, preferred_element_type=F32)


def _mm_hi(a, b):
    return jnp.dot(a, b, preferred_element_type=F32, precision=HI)


def _split3(x):
    hi = x.astype(BF16)
    r = x - hi.astype(F32)
    mid = r.astype(BF16)
    lo = (r - mid.astype(F32)).astype(BF16)
    return hi, mid, lo


def _sel_r(x, e):
    hi, mid, lo = _split3(x)
    return (_mm(hi, e) + _mm(mid, e)) + _mm(lo, e)


def _sel_l(e, x):
    hi, mid, lo = _split3(x)
    return (_mm(e, hi) + _mm(e, mid)) + _mm(e, lo)


def _sig(x):
    return 1.0 / (1.0 + jnp.exp(-x))


def _cp(sem):
    return pltpu.CompilerParams(dimension_semantics=sem, vmem_limit_bytes=VMEM_LIMIT)


def _row(shape):
    nd = len(shape)
    return pl.BlockSpec(shape, lambda *_: (0,) * nd)


def _adamw(w, g, m, v):
    m = B1 * m + (1.0 - B1) * g
    v = B2 * v + (1.0 - B2) * (g * g)
    m_hat = m / (1.0 - B1 ** STEP)
    v_hat = v / (1.0 - B2 ** STEP)
    delta = -LR * (m_hat / (jnp.sqrt(v_hat) + AEPS) + WD * w)
    return delta, m, v


class _Carry:
    def __init__(self, inps, outs, copies):
        self.inps, self.outs, self.copies = list(inps), list(outs), copies
        self.n = len(copies(0, 0, 0))

    def descriptors(self, in_refs, out_refs, send_sems, recv_sems):
        x, y, c = lax.axis_index("x"), lax.axis_index("y"), lax.axis_index("c")
        out = []
        for j, (flip, a, si, o, di) in enumerate(self.copies(x, y, c)):
            if flip is None:
                out.append(pltpu.make_async_copy(in_refs[a].at[si], out_refs[o].at[di], send_sems.at[j]))
            else:
                fx, fy, fc = flip
                peer = (1 - x if fx else x, 1 - y if fy else y, 1 - c if fc else c)
                out.append(pltpu.make_async_remote_copy(
                    src_ref=in_refs[a].at[si], dst_ref=out_refs[o].at[di],
                    send_sem=send_sems.at[j], recv_sem=recv_sems.at[j],
                    device_id=peer, device_id_type=pl.DeviceIdType.MESH))
        return out


def _pcall(body, args, *, name, grid, in_specs, out_specs, out_shape, scratch_shapes=(), sem=None, nprefetch=0,
           carry=None):
    out_shape, out_specs = list(out_shape), list(out_specs)
    in_specs, scratch_shapes = list(in_specs), list(scratch_shapes)
    nin, nout, nscr = len(in_specs), len(out_shape), len(scratch_shapes)
    run = body
    if carry is not None:
        ncin, ncout = len(carry.inps), len(carry.outs)
        hbm = pl.BlockSpec(memory_space=pl.ANY)

        def run(*refs):
            pre, r = refs[:nprefetch], refs[nprefetch:]
            ins, cins = r[:nin], r[nin:nin + ncin]
            r = r[nin + ncin:]
            outs, couts = r[:nout], r[nout:nout + ncout]
            r = r[nout + ncout:]
            scr, (send_sems, recv_sems) = r[:nscr], r[nscr:]
            first = pl.program_id(0) == 0
            last = pl.program_id(0) == grid[0] - 1
            for ax in range(1, len(grid)):
                first = jnp.logical_and(first, pl.program_id(ax) == 0)
                last = jnp.logical_and(last, pl.program_id(ax) == grid[ax] - 1)

            @pl.when(first)
            def _():
                for d in carry.descriptors(cins, couts, send_sems, recv_sems):
                    d.start()

            body(*pre, *ins, *outs, *scr)

            @pl.when(last)
            def _():
                for d in carry.descriptors(cins, couts, send_sems, recv_sems):
                    d.wait()

        in_specs = in_specs + [hbm] * ncin
        out_specs = out_specs + [hbm] * ncout
        out_shape = out_shape + carry.outs
        scratch_shapes = scratch_shapes + [pltpu.SemaphoreType.DMA((carry.n,)), pltpu.SemaphoreType.DMA((carry.n,))]
        args = list(args) + carry.inps
    if sem is None:
        sem = ("arbitrary",) * len(grid)
    if nprefetch:
        kw = dict(grid_spec=pltpu.PrefetchScalarGridSpec(num_scalar_prefetch=nprefetch, grid=grid, in_specs=in_specs,
                                                         out_specs=out_specs, scratch_shapes=scratch_shapes))
    else:
        kw = dict(grid=grid, in_specs=in_specs, out_specs=out_specs, scratch_shapes=scratch_shapes)
    res = pl.pallas_call(run, name=name, out_shape=out_shape, compiler_params=_cp(sem), **kw)(*args)
    return list(res)


def _merge(*carries):
    inps, outs, offs = [], [], []
    for cr in carries:
        offs.append((len(inps), len(outs)))
        inps += cr.inps
        outs += cr.outs

    def copies(x, y, c):
        return [(f, a + io, si, o + oo, di) for cr, (io, oo) in zip(carries, offs) for f, a, si, o, di in cr.copies(x, y, c)]

    return _Carry(inps, outs, copies)


_ALL7 = [(f >> 2 & 1, f >> 1 & 1, f & 1) for f in range(1, 8)]
_CHIPS3 = [(0, 1, 0), (1, 0, 0), (1, 1, 0)]
_SIBLING = (0, 0, 1)


def _gather8_carry(blk):
    def copies(x, y, c):
        me = 4 * x + 2 * y + c
        return [(None, 0, 0, 0, me)] + [(f, 0, 0, 0, me) for f in _ALL7]

    return _Carry([blk[None]], [jax.ShapeDtypeStruct((8,) + blk.shape, blk.dtype)], copies)


def _gather_chips_carry(blks):
    def copies(x, y, c):
        chip = 2 * x + y
        return [(f, a, 0, a, chip) for a in range(len(blks)) for f in [None] + _CHIPS3]

    return _Carry([b[None] for b in blks], [jax.ShapeDtypeStruct((4,) + b.shape, b.dtype) for b in blks], copies)


def _front(first, w_in_t, w_loc, b_loc):
    n = w_loc.shape[1]
    rows = w_in_t.shape[0]
    hw = D // 2
    gather = _gather8_carry(first)
    fetch = _Carry([w_in_t], [jax.ShapeDtypeStruct((4, rows, hw), BF16)],
                   lambda x_, y_, c_: [(f, 0, (slice(None), pl.ds(c_ * hw, hw)), 0, 2 * x_ + y_) for f in [None] + _CHIPS3])
    phase_a = _merge(gather, fetch)
    send_mod = _Carry([None], [None], lambda x_, y_, c_: [(f, 0, slice(None), 0, 2 * x_ + y_) for f in [None] + _CHIPS3])
    swap = _Carry([None], [None], lambda x_, y_, c_: [(_SIBLING, 0, slice(None), 0, slice(None))])

    def body(first_hbm, w_hbm, wl_ref, bl_ref, first_all, w_full, mod_all,
             c_scr, mod_scr, w_half, w_other, wf_scr, sa, ra, sb, rb, sc, rc, sl):
        da = phase_a.descriptors([first_hbm, w_hbm], [first_all, w_half], sa, ra)
        for d in da:
            d.start()
        for d in da[:gather.n]:
            d.wait()
        cp = pltpu.make_async_copy(first_all, c_scr, sl)
        cp.start()
        cp.wait()
        cv = c_scr[:, 0, :]
        cond = cv * _sig(cv)
        for j in range(n // 512):
            cols = slice(j * 512, (j + 1) * 512)
            mod_scr[:, cols] = _mm_hi(cond, wl_ref[:, cols]) + bl_ref[:, cols]
        db = send_mod.descriptors([mod_scr], [mod_all], sb, rb)
        for d in db:
            d.start()
        for d in da[gather.n:]:
            d.wait()
        dc = swap.descriptors([w_half], [w_other], sc, rc)
        for d in dc:
            d.start()
        for d in db + dc:
            d.wait()
        core = lax.axis_index("c")
        for mine_first in (True, False):
            @pl.when(core == (0 if mine_first else 1))
            def _(mine_first=mine_first):
                lo, hi = (w_half, w_other) if mine_first else (w_other, w_half)
                for j in range(4):
                    wf_scr[j * rows:(j + 1) * rows, 0:hw] = lo[j]
                    wf_scr[j * rows:(j + 1) * rows, hw:D] = hi[j]
        wf_scr[4 * rows:, :] = jnp.zeros((PROJ_W - 4 * rows, D), BF16)
        cp = pltpu.make_async_copy(wf_scr, w_full, sl)
        cp.start()
        cp.wait()

    hbm = pl.BlockSpec(memory_space=pl.ANY)
    vmem = pl.BlockSpec(memory_space=pltpu.VMEM)
    sems = pltpu.SemaphoreType.DMA
    return pl.pallas_call(
        body, name="front",
        out_shape=[jax.ShapeDtypeStruct((8,) + first.shape, F32), jax.ShapeDtypeStruct((PROJ_W, D), BF16),
                   jax.ShapeDtypeStruct((4, 8, n), F32)],
        in_specs=[hbm, hbm, vmem, vmem], out_specs=[hbm] * 3,
        scratch_shapes=[pltpu.VMEM((8,) + first.shape, F32), pltpu.VMEM((8, n), F32),
                        pltpu.VMEM((4, rows, hw), BF16), pltpu.VMEM((4, rows, hw), BF16), pltpu.VMEM((PROJ_W, D), BF16),
                        sems((phase_a.n,)), sems((phase_a.n,)), sems((4,)), sems((4,)), sems((1,)), sems((1,)), sems],
        compiler_params=pltpu.CompilerParams(vmem_limit_bytes=VMEM_LIMIT),
    )(first[None], w_in_t, w_loc, b_loc)


def _tail(g_a, g_b, row_a, row_b, mine_in):
    rr, cc = g_a.shape[0] // 4, g_a.shape[1]

    def copies_a(x_, y_, c_):
        me = 4 * x_ + 2 * y_ + c_
        out = [(None, 0, 0, 0, me)] + [(f, 0, 0, 0, me) for f in _ALL7]
        out += [(_SIBLING, 1 + j // 2, pl.ds((j % 2) * 2 * rr + (1 - c_) * rr, rr), 1, j) for j in range(4)]
        out += [(None, 1 + j // 2, pl.ds((j % 2) * 2 * rr + c_ * rr, rr), 2, j) for j in range(4)]
        return out + [(_SIBLING, 3, slice(None), 3, slice(None))]

    phase_a = _Carry([None] * 4, [None] * 4, copies_a)
    to_chips = _Carry([None], [None], lambda x_, y_, c_: [(f, 0, jnp.bitwise_xor(2 * x_ + y_, k + 1), 0, k)
                                                        for k, f in enumerate(_CHIPS3)])
    back = _Carry([None], [None] * 2, lambda x_, y_, c_: [(None, 0, slice(None), 0, slice(None)),
                                                        (_SIBLING, 0, slice(None), 1, slice(None))])

    def body(ga_hbm, gb_hbm, ra_ref, rb_ref, in_hbm, rows_all, got_in, mine_o, got_o,
             row_scr, got1_scr, mine_scr, s4_scr, got2_scr, red_scr, sa, ra, sb, rb, sc, rc):
        chip = 2 * lax.axis_index("x") + lax.axis_index("y")
        row_scr[0] = ra_ref[0:1, :] + rb_ref[0:1, :]
        da = phase_a.descriptors([row_scr, ga_hbm, gb_hbm, in_hbm], [rows_all, got1_scr, mine_scr, got_in], sa, ra)
        for d in da:
            d.start()
        for d in da[8:16]:
            d.wait()
        for j in range(4):
            s4_scr[j] = (mine_scr[j] + got1_scr[j]).astype(BF16)
        red_scr[...] = mine_scr[chip] + got1_scr[chip]
        db = to_chips.descriptors([s4_scr], [got2_scr], sb, rb)
        for d in db:
            d.start()
        for d in db:
            d.wait()
        red_scr[...] = ((red_scr[...] + got2_scr[0].astype(F32)) + got2_scr[1].astype(F32)) + got2_scr[2].astype(F32)
        dc = back.descriptors([red_scr], [mine_o, got_o], sc, rc)
        for d in dc:
            d.start()
        for d in da[:8] + da[16:] + dc:
            d.wait()

    hbm = pl.BlockSpec(memory_space=pl.ANY)
    vmem = pl.BlockSpec(memory_space=pltpu.VMEM)
    sems = pltpu.SemaphoreType.DMA
    half = jax.ShapeDtypeStruct((rr, cc), F32)
    return pl.pallas_call(
        body, name="tail",
        out_shape=[jax.ShapeDtypeStruct((8, 1, cc), F32), jax.ShapeDtypeStruct(mine_in.shape, F32), half, half],
        in_specs=[hbm, hbm, vmem, vmem, hbm], out_specs=[hbm] * 4,
        scratch_shapes=[pltpu.VMEM((1, 1, cc), F32), pltpu.VMEM((4, rr, cc), F32), pltpu.VMEM((4, rr, cc), F32),
                        pltpu.VMEM((4, rr, cc), BF16), pltpu.VMEM((3, rr, cc), BF16), pltpu.VMEM((rr, cc), F32),
                        sems((phase_a.n,)), sems((phase_a.n,)), sems((3,)), sems((3,)), sems((2,)), sems((2,))],
        compiler_params=pltpu.CompilerParams(vmem_limit_bytes=VMEM_LIMIT),
    )(g_a, g_b, row_a, row_b, mine_in)


def _ada_bwd_adamw(c_all_t, dmod_loc, w, m, v, carry=None):
    n = w.shape[1]
    tn = 512

    def body(ct_ref, dm_ref, w_ref, m_ref, v_ref, g_ref, d_ref, mo_ref, vo_ref):
        ct = ct_ref[...]
        cond = ct * _sig(ct)
        dm = dm_ref[...]
        g = cond[:, 0:1] * dm[0:1, :]
        for b in range(1, 8):
            g = g + cond[:, b:b + 1] * dm[b:b + 1, :]
        g_ref[...] = g
        d_ref[...], mo_ref[...], vo_ref[...] = _adamw(w_ref[...], g, m_ref[...], v_ref[...])

    wspec = pl.BlockSpec((D, tn), lambda j: (0, j))
    return _pcall(
        body, [c_all_t, dmod_loc, w, m, v], name="ada_bwd_adamw", grid=(n // tn,),
        out_shape=[jax.ShapeDtypeStruct((D, n), F32)] * 4,
        in_specs=[_row((D, 8)), pl.BlockSpec((8, tn), lambda j: (0, j)), wspec, wspec, wspec],
        out_specs=[wspec] * 4, carry=carry)


def _in_proj_fwd(x, a1, sh1, w_in, carry=None):
    s = x.shape[0]
    tm = 512

    def body(x_ref, a_ref, s_ref, w_ref, q_ref, kv_ref, z_ref, xbc_ref, dt_ref):
        def norm(rows):
            xv = x_ref[rows, :]
            r = lax.rsqrt(jnp.mean(xv * xv, axis=-1, keepdims=True) + EPS)
            return (xv * r * a_ref[...] + s_ref[...]).astype(BF16)

        def project(rows, h):
            p = _mm_nt(h, w_ref[...])
            q_ref[rows, :] = p[:, 0:512].astype(BF16)
            kv_ref[rows, :] = p[:, 512:768].astype(BF16)
            z_ref[rows, :] = p[:, 768:1280]
            xbc_ref[rows, :] = p[:, 1280:2304]
            dt_ref[rows, :] = p[:, 2304:2432]

        r0, r1 = slice(0, tm // 2), slice(tm // 2, tm)
        h0 = norm(r0)
        project(r0, h0)
        project(r1, norm(r1))

    def tok(w):
        return pl.BlockSpec((tm, w), lambda i: (i, 0))

    return _pcall(
        body, [x, a1, sh1, w_in], name="in_proj_fwd", grid=(s // tm,),
        out_shape=[jax.ShapeDtypeStruct((s, QW), BF16), jax.ShapeDtypeStruct((s, 2 * KVW), BF16),
                   jax.ShapeDtypeStruct((s, SW), F32), jax.ShapeDtypeStruct((s, XBCW), F32),
                   jax.ShapeDtypeStruct((s, 128), F32)],
        in_specs=[tok(D), _row((1, D)), _row((1, D)), _row((PROJ_W, D))],
        out_specs=[tok(QW), tok(2 * KVW), tok(SW), tok(XBCW), tok(128)], carry=carry)


def _in_proj_bwd(x, dx1, a1, sh1, w_in, dq, dkv, dz, dxbc, ddt, carry=None):
    s = x.shape[0]
    tm = 512

    def body(x_ref, dx1_ref, a_ref, s_ref, w_ref, dq_ref, dkv_ref, dz_ref, dxbc_ref, ddt_ref,
             gx_ref, h_ref, dsh_ref, p_ref):
        i = pl.program_id(0)

        @pl.when(i == 0)
        def _():
            dsh_ref[...] = jnp.zeros_like(dsh_ref)
            p_ref[...] = jnp.zeros_like(p_ref)

        def gather(st):
            rows = st["rows"]
            st["dproj"] = jnp.concatenate([dq_ref[rows, :], dkv_ref[rows, :], dz_ref[rows, :], dxbc_ref[rows, :],
                                           ddt_ref[rows, :]], axis=1)

        def back(st):
            st["dh"] = _mm(st.pop("dproj"), w_ref[...])

        def norm(st):
            rows, dh = st["rows"], st.pop("dh")
            xv = x_ref[rows, :]
            r = lax.rsqrt(jnp.mean(xv * xv, axis=-1, keepdims=True) + EPS)
            xn = xv * r
            a = a_ref[...]
            h_ref[rows, :] = (xn * a + s_ref[...]).astype(BF16)
            st["dsh"] = jnp.sum(dh, axis=0, keepdims=True)
            st["p"] = jnp.sum(dh * xn, axis=0, keepdims=True)
            u = dh * a
            gx_ref[rows, :] = dx1_ref[rows, :] + r * u - xn * (r * jnp.mean(u * xn, axis=-1, keepdims=True))

        g0, g1 = [dict(rows=slice(k * (tm // 2), (k + 1) * (tm // 2))) for k in range(2)]
        for stage, st in [(gather, g0), (back, g0), (gather, g1), (norm, g0), (back, g1), (norm, g1)]:
            stage(st)
        dsh_ref[0:1, :] += g0["dsh"] + g1["dsh"]
        p_ref[0:1, :] += g0["p"] + g1["p"]

    def tok(w):
        return pl.BlockSpec((tm, w), lambda i: (i, 0))

    return _pcall(
        body, [x, dx1, a1, sh1, w_in, dq, dkv, dz, dxbc, ddt], name="in_proj_bwd", grid=(s // tm,),
        out_shape=[jax.ShapeDtypeStruct((s, D), F32), jax.ShapeDtypeStruct((s, D), BF16),
                   jax.ShapeDtypeStruct((8, D), F32), jax.ShapeDtypeStruct((8, D), F32)],
        in_specs=[tok(D), tok(D), _row((1, D)), _row((1, D)), _row((PROJ_W, D)),
                  tok(QW), tok(2 * KVW), tok(SW), tok(XBCW), tok(128)],
        out_specs=[tok(D), tok(D), _row((8, D)), _row((8, D))], carry=carry)


def _attn_geometry():
    dist = np.arange(BLK)[:, None] + BLK - np.arange(2 * BLK)[None, :]
    n = np.maximum(dist, 0)
    max_exact = NBUCKET // 2
    large = max_exact + (np.log(np.maximum(n, 1) / max_exact) / np.log(128 / max_exact)
                         * (NBUCKET - max_exact)).astype(np.int32)
    large = np.minimum(large, NBUCKET - 1)
    bucket = np.where(n < max_exact, n, large).astype(np.int32)
    mask = (dist >= 0) & (dist < 128)
    return bucket, mask


def _attn_heads(is_first, q_blk, kvw, bias_ref, sinks_ref):
    qv = q_blk * 0.125
    col = lax.broadcasted_iota(jnp.int32, (BLK, 2 * BLK), 1)
    first = jnp.where(jnp.logical_and(is_first, col < BLK), NEG, 0.0)
    groups = []
    for g in range(NKV):
        qs = jnp.concatenate([qv[:, (4 * g + r) * HD:(4 * g + r + 1) * HD] for r in range(4)], axis=0)
        kw = kvw[:, g * HD:(g + 1) * HD]
        vw = kvw[:, KVW + g * HD:KVW + (g + 1) * HD]
        sc = _mm_nt(qs, kw)
        pn, ps = [], []
        for r in range(4):
            h = 4 * g + r
            sr = sc[r * BLK:(r + 1) * BLK] + bias_ref[h] + first
            sink = sinks_ref[h]
            m = jnp.maximum(jnp.max(sr, axis=-1, keepdims=True), sink)
            p = jnp.exp(sr - m)
            es = jnp.exp(sink - m)
            inv = 1.0 / (jnp.sum(p, axis=-1, keepdims=True) + es)
            pn.append(p * inv)
            ps.append(es * inv)
        pn = jnp.concatenate(pn, axis=0)
        ps = jnp.concatenate(ps, axis=0)
        o = _mm(pn.astype(BF16), vw)
        groups.append((qs, kw, vw, pn, ps, o))
    return groups


def _unstack_heads(parts):
    return jnp.concatenate([p[r * BLK:(r + 1) * BLK] for p in parts for r in range(4)], axis=1)


NB = 2


def _attn_fwd(q, kv, bias, sinks, nw, carry=None):
    s = q.shape[0]

    def body(q_ref, kvp_ref, kvc_ref, bias_ref, sinks_ref, nw_ref, y_ref):
        t = pl.program_id(0)
        kv3 = jnp.concatenate([kvp_ref[...], kvc_ref[...]], axis=0)
        for sub in range(NB):
            rows = slice(sub * BLK, (sub + 1) * BLK)
            groups = _attn_heads(jnp.logical_and(t == 0, sub == 0), q_ref[rows, :], kv3[sub * BLK:(sub + 2) * BLK],
                                 bias_ref, sinks_ref)
            o = _unstack_heads([g[5] for g in groups])
            r = lax.rsqrt(jnp.mean(o * o, axis=-1, keepdims=True) + EPS)
            y_ref[rows, :] = (o * r * nw_ref[...]).astype(BF16)

    return _pcall(
        body, [q, kv, kv, bias, sinks, nw], name="attn_fwd", grid=(s // (NB * BLK),),
        out_shape=[jax.ShapeDtypeStruct((s, QW), BF16)],
        in_specs=[pl.BlockSpec((NB * BLK, QW), lambda t: (t, 0)),
                  pl.BlockSpec((BLK, 2 * KVW), lambda t: (jnp.maximum(NB * t - 1, 0), 0)),
                  pl.BlockSpec((NB * BLK, 2 * KVW), lambda t: (t, 0)),
                  _row((NH, BLK, 2 * BLK)),
                  pl.BlockSpec(memory_space=pltpu.SMEM),
                  _row((1, QW))],
        out_specs=[pl.BlockSpec((NB * BLK, QW), lambda t: (t, 0))], carry=carry)


def _attn_bwd(q, kv, dya, bias, sinks, nw, act, dx2, gate2, w_dn, carry=None):
    s = q.shape[0]
    nt = s // (NB * BLK)
    npiece = DFF // NB

    def body(q_ref, kvp_ref, kvc_ref, dy_ref, bias_ref, sinks_ref, nw_ref, act_ref, dx2_ref, g2_ref, wdn_ref,
             dq_ref, dkv_ref, dbias_ref, dsink_ref, dnw_ref, gdn_hbm, dg2_ref, carry_ref, held_ref, acc_ref, sem):
        t = pl.program_id(0)

        @pl.when(t == 0)
        def _():
            carry_ref[...] = jnp.zeros_like(carry_ref)
            held_ref[...] = jnp.zeros_like(held_ref)
            dbias_ref[...] = jnp.zeros_like(dbias_ref)
            dsink_ref[...] = jnp.zeros_like(dsink_ref)
            dnw_ref[...] = jnp.zeros_like(dnw_ref)
            acc_ref[...] = jnp.zeros_like(acc_ref)

        def wgrad_piece(sub):
            rows = slice(sub * npiece, (sub + 1) * npiece)
            acc_ref[rows, :] += _mm_tn(act_ref[:, rows], dx2_ref[...].astype(BF16))

        def block(sub, kv3):
            rows = slice(sub * BLK, (sub + 1) * BLK)
            groups = _attn_heads(jnp.logical_and(t == 0, sub == 0), q_ref[rows, :], kv3[sub * BLK:(sub + 2) * BLK],
                                 bias_ref, sinks_ref)
            o = _unstack_heads([g[5] for g in groups])
            r = lax.rsqrt(jnp.mean(o * o, axis=-1, keepdims=True) + EPS)
            dy = dy_ref[rows, :]
            on = o * r
            dnw_ref[0:1, :] += jnp.sum(dy * on, axis=0, keepdims=True)
            u = dy * nw_ref[...]
            do = r * u - on * (r * jnp.mean(u * on, axis=-1, keepdims=True))
            dq_parts, dk_parts, dv_parts = [], [], []
            for g, (qs, kw, vw, pn, ps, og) in enumerate(groups):
                dos = jnp.concatenate([do[:, (4 * g + r_) * HD:(4 * g + r_ + 1) * HD] for r_ in range(4)], axis=0)
                delta = jnp.sum(dos * og, axis=-1, keepdims=True)
                dp = _mm_nt(dos.astype(BF16), vw)
                ds = pn * (dp - delta)
                dsk = ps * delta
                lane = lax.broadcasted_iota(jnp.int32, (1, 128), 1)
                for r_ in range(4):
                    h = 4 * g + r_
                    dbias_ref[h] += ds[r_ * BLK:(r_ + 1) * BLK]
                    dsink_ref[0:1, :] -= jnp.where(lane == h, jnp.sum(dsk[r_ * BLK:(r_ + 1) * BLK]), 0.0)
                dsb = ds.astype(BF16)
                dq_parts.append(_mm(dsb, kw) * 0.125)
                dk_parts.append(_mm_tn(dsb, qs))
                dv_parts.append(_mm_tn(pn.astype(BF16), dos.astype(BF16)))
            dq_ref[rows, :] = _unstack_heads(dq_parts).astype(BF16)
            return jnp.concatenate(dk_parts + dv_parts, axis=1)

        @pl.when(t < nt)
        def _():
            kv3 = jnp.concatenate([kvp_ref[...], kvc_ref[...]], axis=0)
            tail = carry_ref[...]
            for sub in range(NB):
                d = block(sub, kv3)
                done = tail + d[0:BLK]
                if sub == 0:
                    dkv_ref[0:(NB - 1) * BLK, :] = held_ref[...].astype(BF16)
                    dkv_ref[(NB - 1) * BLK:NB * BLK, :] = done.astype(BF16)
                else:
                    held_ref[(sub - 1) * BLK:sub * BLK, :] = done
                tail = d[BLK:2 * BLK]
                wgrad_piece(sub)
            carry_ref[...] = tail

        @pl.when(t == nt)
        def _():
            dkv_ref[0:(NB - 1) * BLK, :] = held_ref[...].astype(BF16)
            dkv_ref[(NB - 1) * BLK:NB * BLK, :] = carry_ref[...].astype(BF16)
            acc = acc_ref[...]
            dg2_ref[...] = jnp.zeros_like(dg2_ref)
            dg2_ref[0:1, :] = jnp.sum(acc * wdn_ref[...].astype(F32), axis=0, keepdims=True)
            acc_ref[...] = acc * g2_ref[...]
            cp = pltpu.make_async_copy(acc_ref, gdn_hbm, sem)
            cp.start()
            cp.wait()

    last = nt - 1
    tile = lambda w: pl.BlockSpec((NB * BLK, w), lambda t: (jnp.minimum(t, last), 0))
    return _pcall(
        body, [q, kv, kv, dya, bias, sinks, nw, act, dx2, gate2, w_dn], name="attn_bwd", grid=(nt + 1,),
        out_shape=[jax.ShapeDtypeStruct((s, QW), BF16), jax.ShapeDtypeStruct((s, 2 * KVW), BF16),
                   jax.ShapeDtypeStruct((NH, BLK, 2 * BLK), F32), jax.ShapeDtypeStruct((NH, 128), F32),
                   jax.ShapeDtypeStruct((8, QW), F32), jax.ShapeDtypeStruct((DFF, D), F32),
                   jax.ShapeDtypeStruct((8, D), F32)],
        in_specs=[tile(QW),
                  pl.BlockSpec((BLK, 2 * KVW), lambda t: (jnp.clip(NB * t - 1, 0, NB * nt - 1), 0)),
                  tile(2 * KVW), tile(QW),
                  _row((NH, BLK, 2 * BLK)),
                  pl.BlockSpec(memory_space=pltpu.SMEM),
                  _row((1, QW)), tile(DFF), tile(D), _row((1, D)), _row((DFF, D))],
        out_specs=[tile(QW),
                   pl.BlockSpec((NB * BLK, 2 * KVW), lambda t: (jnp.maximum(t - 1, 0), 0)),
                   _row((NH, BLK, 2 * BLK)), _row((NH, 128)), _row((8, QW)),
                   pl.BlockSpec(memory_space=pl.ANY), _row((8, D))],
        scratch_shapes=[pltpu.VMEM((BLK, 2 * KVW), F32), pltpu.VMEM(((NB - 1) * BLK, 2 * KVW), F32),
                        pltpu.VMEM((DFF, D), F32), pltpu.SemaphoreType.DMA], carry=carry)


def _rel_bias_grad(dbias, bucket):
    def body(db_ref, bk_ref, o_ref):
        bk = bk_ref[...]
        lane = lax.broadcasted_iota(jnp.int32, (1, 128), 1)
        for b in range(NBUCKET):
            sel = bk == b
            row = jnp.zeros((1, 128), F32)
            for h in range(NH):
                row = row + jnp.where(lane == h, jnp.sum(jnp.where(sel, db_ref[h], 0.0)), 0.0)
            o_ref[b:b + 1, :] = row

    return pl.pallas_call(
        body, name="rel_bias_grad",
        out_shape=jax.ShapeDtypeStruct((NBUCKET, 128), F32),
    )(dbias, bucket)


def _ssd_consts():
    head_of_lane = np.arange(SW) // HD
    expand = (np.arange(128)[:, None] == head_of_lane[None, :]).astype(np.float32)
    tril = np.tril(np.ones((BLK, BLK), np.float32))
    return (jnp.asarray(expand, BF16), jnp.asarray(expand.T.copy(), BF16), jnp.asarray(tril, BF16),
            jnp.asarray(tril.T.copy(), BF16))


def _conv_pre(xc, halo, cw, cb):
    ext = jnp.concatenate([halo, xc], axis=0)
    taps = [xc if k == CK - 1 else pltpu.roll(ext, CK - 1 - k, 0)[8:8 + BLK] for k in range(CK)]
    return cb + sum(cw[k:k + 1, :] * taps[k] for k in range(CK))


def _ssd_chunk(pre, dtr, dtb, av, dkv, ex, tril, h_in):
    sp = _sig(pre)
    xbc = pre * sp
    xs, bm, cm = xbc[:, 0:SW], xbc[:, SW:SW + 2 * NST], xbc[:, SW + 2 * NST:]
    dtin = dtr + dtb
    dt = jnp.maximum(dtin, 0.0) + jnp.log1p(jnp.exp(-jnp.abs(dtin)))
    cs = _sel_l(tril, dt * av)
    cst = cs.T
    dtx = _sel_r(dt, ex)
    csx = _sel_r(cs, ex)
    xdt = xs * dtx
    csl = csx[BLK - 1:BLK, :]
    decx = jnp.exp(csl - csx)
    ecsx = jnp.exp(csx)
    ecl = jnp.exp(csl)
    causal = tril.astype(F32) > 0.5
    ydiag, yoff, cbs, lms = [], [], [], []
    for g in range(2):
        bg = bm[:, g * NST:(g + 1) * NST].astype(BF16)
        cg = cm[:, g * NST:(g + 1) * NST].astype(BF16)
        cb = _mm_nt(cg, bg)
        cbs.append(cb)
        yoff.append(_mm(cg, h_in[:, g * 256:(g + 1) * 256].astype(BF16)))
        for r in range(4):
            h = 4 * g + r
            seg = cs[:, h:h + 1] - cst[h:h + 1, :]
            lm = jnp.where(causal, jnp.exp(jnp.minimum(seg, 0.0)), 0.0)
            lms.append(lm)
            ydiag.append(_mm((cb * lm).astype(BF16), xdt[:, h * HD:(h + 1) * HD].astype(BF16)))
    yoff = jnp.concatenate(yoff, axis=1) * ecsx
    y = jnp.concatenate(ydiag, axis=1) + yoff + dkv * xs
    return dict(pre=pre, sp=sp, xs=xs, bm=bm, cm=cm, dtin=dtin, dt=dt, av=av, cs=cs, cst=cst,
                dtx=dtx, csx=csx, xdt=xdt, decx=decx, ecsx=ecsx, ecl=ecl, causal=causal, cbs=cbs, lms=lms,
                yoff=yoff, y=y)


def _group_mean(t):
    m0 = jnp.mean(t[:, 0:256], axis=-1, keepdims=True)
    m1 = jnp.mean(t[:, 256:512], axis=-1, keepdims=True)
    return jnp.concatenate([jnp.broadcast_to(m0, (t.shape[0], 256)), jnp.broadcast_to(m1, (t.shape[0], 256))], axis=1)


SUBS = 4


def _ssd_fwd(z, xbc, dtr, cw, cb, dtb, av, dk, nw, carry=None):
    s = z.shape[0]
    nc = s // BLK
    tile = SUBS * BLK
    ex, _, tril, _ = _ssd_consts()

    def body(z_ref, xc_ref, xh_ref, dtr_ref, cw_ref, cb_ref, dtb_ref, a_ref, dk_ref, nw_ref, ex_ref, tril_ref,
             y_ref, hs_ref, pre_ref, h_ref):
        t = pl.program_id(0)

        @pl.when(t == 0)
        def _():
            h_ref[...] = jnp.zeros_like(h_ref)

        h_in = h_ref[...]
        for sub in range(SUBS):
            rows = slice(sub * BLK, (sub + 1) * BLK)
            xc = xc_ref[rows, :]
            halo = jnp.where(t == 0, 0.0, xh_ref[...]) if sub == 0 else xc_ref[sub * BLK - 8:sub * BLK, :]
            pre = _conv_pre(xc, halo, cw_ref[...], cb_ref[...])
            pre_ref[rows, :] = pre
            hs_ref[sub] = h_in
            f = _ssd_chunk(pre, dtr_ref[rows, :], dtb_ref[...], a_ref[...], dk_ref[...], ex_ref[...], tril_ref[...], h_in)
            dx = (f["decx"] * f["xdt"]).astype(BF16)
            st = [_mm_tn(f["bm"][:, g * NST:(g + 1) * NST].astype(BF16), dx[:, g * 256:(g + 1) * 256]) for g in range(2)]
            h_in = h_in * f["ecl"] + jnp.concatenate(st, axis=1)
            zv = z_ref[rows, :]
            tg = f["y"] * (zv * _sig(zv))
            r = lax.rsqrt(_group_mean(tg * tg) + EPS)
            y_ref[rows, :] = (tg * r * nw_ref[...]).astype(BF16)
        h_ref[...] = h_in

    cur = lambda w: pl.BlockSpec((tile, w), lambda t: (t, 0))
    return _pcall(
        body, [z, xbc, xbc, dtr, cw, cb, dtb, av, dk, nw, ex, tril], name="ssd_fwd", grid=(s // tile,),
        out_shape=[jax.ShapeDtypeStruct((s, SW), BF16), jax.ShapeDtypeStruct((nc, NST, SW), F32),
                   jax.ShapeDtypeStruct((s, XBCW), F32)],
        in_specs=[cur(SW), cur(XBCW), pl.BlockSpec((8, XBCW), lambda t: (jnp.maximum(t * (tile // 8) - 1, 0), 0)),
                  cur(128), _row((8, XBCW)), _row((1, XBCW)), _row((1, 128)),
                  _row((1, 128)), _row((1, SW)), _row((1, SW)), _row((128, SW)), _row((BLK, BLK))],
        out_specs=[cur(SW), pl.BlockSpec((SUBS, NST, SW), lambda t: (t, 0, 0)), cur(XBCW)],
        scratch_shapes=[pltpu.VMEM((NST, SW), F32)], carry=carry)


def _ssd_bwd(z, xbc, pre_all, dtr, dys, hs, cw, dtb, av, dk, nw, carry=None):
    s = z.shape[0]
    tile = SUBS * BLK
    nt = s // tile
    ex, ext_t, tril, triu = _ssd_consts()

    def body(z_ref, xc_ref, pre_ref, dtr_ref, dy_ref, hs_ref, cw_ref, dtb_ref, a_ref, dk_ref, nw_ref,
             ex_ref, ext_ref, tril_ref, triu_ref,
             dz_ref, dxbc_ref, ddt_ref, dcw_ref, dcb_ref, dnw_ref, dhd_ref, dh_ref, nxt_ref, dd_ref):
        i = pl.program_id(0)

        @pl.when(i == 0)
        def _():
            dh_ref[...] = jnp.zeros_like(dh_ref)
            nxt_ref[...] = jnp.zeros_like(nxt_ref)
            dd_ref[...] = jnp.zeros_like(dd_ref)
            dcw_ref[...] = jnp.zeros_like(dcw_ref)
            dcb_ref[...] = jnp.zeros_like(dcb_ref)
            dnw_ref[...] = jnp.zeros_like(dnw_ref)
            dhd_ref[...] = jnp.zeros_like(dhd_ref)

        gst, nxt = dh_ref[...], nxt_ref[...]
        for sub in reversed(range(SUBS)):
            rows = slice(sub * BLK, (sub + 1) * BLK)
            gst, nxt = chunk(sub, rows, gst, nxt, z_ref, xc_ref, pre_ref, dtr_ref, dy_ref, hs_ref, cw_ref, dtb_ref,
                             a_ref, dk_ref, nw_ref, ex_ref, ext_ref, tril_ref, triu_ref,
                             dz_ref, dxbc_ref, ddt_ref, dcw_ref, dcb_ref, dnw_ref, dhd_ref, dd_ref)
        dh_ref[...] = gst
        nxt_ref[...] = nxt

        @pl.when(i == nt - 1)
        def _():
            dhd_ref[2:3, :] = _sel_r(dd_ref[...], ext_ref[...])[0:1, :]

    def chunk(sub, rows, gst, nxt, z_ref, xc_ref, pre_ref, dtr_ref, dy_ref, hs_ref, cw_ref, dtb_ref,
              a_ref, dk_ref, nw_ref, ex_ref, ext_ref, tril_ref, triu_ref,
              dz_ref, dxbc_ref, ddt_ref, dcw_ref, dcb_ref, dnw_ref, dhd_ref, dd_ref):
        h_in = hs_ref[sub]
        f = _ssd_chunk(pre_ref[rows, :], dtr_ref[rows, :], dtb_ref[...], a_ref[...], dk_ref[...], ex_ref[...],
                       tril_ref[...], h_in)
        xs, xdt, decx, ecsx, ecl, dtx = f["xs"], f["xdt"], f["decx"], f["ecsx"], f["ecl"], f["dtx"]
        cs, cst, causal = f["cs"], f["cst"], f["causal"]
        causal_t = triu_ref[...].astype(F32) > 0.5

        zv = z_ref[rows, :]
        sz = _sig(zv)
        gz = zv * sz
        t = f["y"] * gz
        r = lax.rsqrt(_group_mean(t * t) + EPS)
        tn_ = t * r
        dyn = dy_ref[rows, :]
        dnw_ref[0:1, :] += jnp.sum(dyn * tn_, axis=0, keepdims=True)
        u = dyn * nw_ref[...]
        dt_ = r * u - tn_ * (r * _group_mean(u * tn_))
        dy = dt_ * gz
        dz_ref[rows, :] = (dt_ * f["y"] * (sz * (1.0 + zv * (1.0 - sz)))).astype(BF16)

        dd_ref[0:1, :] += jnp.sum(dy * xs, axis=0, keepdims=True)
        dxs = dk_ref[...] * dy

        edy = ecsx * dy
        dxdt, dbs, dcs_, dcsx_parts, dh_new = [], [], [], [], []
        lane = lax.broadcasted_iota(jnp.int32, (1, 128), 1)
        dcs_intra = jnp.zeros((BLK, 128), F32)
        for g in range(2):
            sl = slice(g * 256, (g + 1) * 256)
            bgf, cgf = f["bm"][:, g * NST:(g + 1) * NST], f["cm"][:, g * NST:(g + 1) * NST]
            bg, cg = bgf.astype(BF16), cgf.astype(BF16)
            gg = gst[:, sl].astype(BF16)
            hg = h_in[:, sl].astype(BF16)
            edyg = edy[:, sl].astype(BF16)
            dc = _mm_nt(edyg, hg)
            dh_new.append(gst[:, sl] * ecl[:, sl] + _mm_tn(cg, edyg))
            bgm = _mm(bg, gg)
            dxdt_g = decx[:, sl] * bgm
            dxg = (decx[:, sl] * xdt[:, sl]).astype(BF16)
            db = _mm_nt(dxg, gg)
            qd = bgm * xdt[:, sl] * decx[:, sl]
            last = jnp.sum(qd, axis=0, keepdims=True) + ecl[:, sl] * jnp.sum(gst[:, sl] * h_in[:, sl], axis=0, keepdims=True)
            rowid = lax.broadcasted_iota(jnp.int32, (BLK, 256), 0)
            dcsx_parts.append(f["yoff"][:, sl] * dy[:, sl] - qd + jnp.where(rowid == BLK - 1, last, 0.0))
            cb_ = f["cbs"][g]
            cbt = _mm_nt(bg, cg)
            dcb_ = jnp.zeros((BLK, BLK), F32)
            dcbt = jnp.zeros((BLK, BLK), F32)
            dxd = []
            for r_ in range(4):
                h = 4 * g + r_
                hl = slice(h * HD, (h + 1) * HD)
                lm = f["lms"][h]
                segt = cst[h:h + 1, :] - cs[:, h:h + 1]
                lmt = jnp.where(causal_t, jnp.exp(jnp.minimum(segt, 0.0)), 0.0)
                dyh = dy[:, hl].astype(BF16)
                xdh = xdt[:, hl].astype(BF16)
                dw = _mm_nt(dyh, xdh)
                dwt = _mm_nt(xdh, dyh)
                wt = cbt * lmt
                dxd.append(_mm(wt.astype(BF16), dyh))
                dcb_ = dcb_ + dw * lm
                dcbt = dcbt + dwt * lmt
                col = jnp.sum(dw * (cb_ * lm), axis=-1, keepdims=True) - jnp.sum(dwt * wt, axis=-1, keepdims=True)
                dcs_intra = dcs_intra + jnp.where(lane == h, col, 0.0)
            dxdt.append(dxdt_g + jnp.concatenate(dxd, axis=1))
            dcs_.append(dc + _mm(dcb_.astype(BF16), bg))
            dbs.append(db + _mm(dcbt.astype(BF16), cg))
        dxdt = jnp.concatenate(dxdt, axis=1)
        dxs = dxs + dxdt * dtx
        ext_t_ = ext_ref[...]
        dcs = dcs_intra + _sel_r(jnp.concatenate(dcsx_parts, axis=1), ext_t_)
        da = _sel_l(triu_ref[...], dcs)
        ddt = da * f["av"] + _sel_r(dxdt * xs, ext_t_)
        dhd_ref[1:2, :] += jnp.sum(da * f["dt"], axis=0, keepdims=True)
        ddtr = ddt * _sig(f["dtin"])
        dhd_ref[0:1, :] += jnp.sum(ddtr, axis=0, keepdims=True)
        ddt_ref[rows, :] = ddtr.astype(BF16)

        sp, pre = f["sp"], f["pre"]
        dact = jnp.concatenate([dxs] + dbs + dcs_, axis=1)
        dpre = dact * (sp * (1.0 + pre * (1.0 - sp)))
        dcb_ref[0:1, :] += jnp.sum(dpre, axis=0, keepdims=True)
        ext2 = jnp.concatenate([dpre, nxt], axis=0)
        shifted = [pltpu.roll(ext2, BLK + 8 - (CK - 1 - k), 0)[0:BLK] for k in range(CK - 1)] + [dpre]
        cw = cw_ref[...]
        xc = xc_ref[rows, :]
        dxr = cw[CK - 1:CK, :] * dpre
        for k in range(CK):
            dcw_ref[k:k + 1, :] += jnp.sum(shifted[k] * xc, axis=0, keepdims=True)
            if k < CK - 1:
                dxr = dxr + cw[k:k + 1, :] * shifted[k]
        dxbc_ref[rows, :] = dxr.astype(BF16)
        return jnp.concatenate(dh_new, axis=1), dpre[0:8]

    cur = lambda w: pl.BlockSpec((tile, w), lambda i: (nt - 1 - i, 0))
    return _pcall(
        body, [z, xbc, pre_all, dtr, dys, hs, cw, dtb, av, dk, nw, ex, ext_t, tril, triu], name="ssd_bwd", grid=(nt,),
        out_shape=[jax.ShapeDtypeStruct((s, SW), BF16), jax.ShapeDtypeStruct((s, XBCW), BF16),
                   jax.ShapeDtypeStruct((s, 128), BF16), jax.ShapeDtypeStruct((8, XBCW), F32),
                   jax.ShapeDtypeStruct((8, XBCW), F32), jax.ShapeDtypeStruct((8, SW), F32),
                   jax.ShapeDtypeStruct((8, 128), F32)],
        in_specs=[cur(SW), cur(XBCW), cur(XBCW), cur(128), cur(SW),
                  pl.BlockSpec((SUBS, NST, SW), lambda i: (nt - 1 - i, 0, 0)),
                  _row((8, XBCW)), _row((1, 128)), _row((1, 128)), _row((1, SW)), _row((1, SW)),
                  _row((128, SW)), _row((SW, 128)), _row((BLK, BLK)), _row((BLK, BLK))],
        out_specs=[cur(SW), cur(XBCW), cur(128), _row((8, XBCW)), _row((8, XBCW)), _row((8, SW)), _row((8, 128))],
        scratch_shapes=[pltpu.VMEM((NST, SW), F32), pltpu.VMEM((8, XBCW), F32), pltpu.VMEM((8, SW), F32)], carry=carry)


def _load_once(i, pairs, sem):
    @pl.when(i == 0)
    def _():
        cps = [pltpu.make_async_copy(src, dst, sem.at[k]) for k, (src, dst) in enumerate(pairs)]
        for cp in cps:
            cp.start()
        for cp in cps:
            cp.wait()


def _mlp_fwd(x, ya, ys, tgt, w_o, w_ga, w_gb, w_dn, gate1, a2, sh2, gate2, fn):
    s = x.shape[0]
    sub_m, subs = 256, 2
    tm = sub_m * subs

    def body(x_ref, ya_ref, ys_ref, t_ref, wo_hbm, wga_hbm, wgb_hbm, wdn_hbm, g1_ref, a2_ref, s2_ref, g2_ref, fn_ref,
             x1_ref, gu_ref, dx2_ref, loss_ref, dfn_ref, wo, wga, wgb, wdn, sem):
        i = pl.program_id(0)
        _load_once(i, [(wo_hbm, wo), (wga_hbm, wga), (wgb_hbm, wgb), (wdn_hbm, wdn)], sem)

        @pl.when(i == 0)
        def _():
            loss_ref[...] = jnp.zeros_like(loss_ref)
            dfn_ref[...] = jnp.zeros_like(dfn_ref)

        def proj(st):
            st["mix"] = _mm(ya_ref[st["rows"], :], wo[0:QW, :]) + _mm(ys_ref[st["rows"], :], wo[QW:D, :])

        def norm(st):
            x1 = x_ref[st["rows"], :] + g1_ref[...] * st.pop("mix")
            x1_ref[st["rows"], :] = x1
            r2 = lax.rsqrt(jnp.mean(x1 * x1, axis=-1, keepdims=True) + EPS)
            st["x1"] = x1
            st["h2"] = (x1 * r2 * a2_ref[...] + s2_ref[...]).astype(BF16)

        def gate_up(st):
            h2 = st.pop("h2")
            ha, hb = h2[:, 0:D // 2], h2[:, D // 2:D]
            gub = jnp.concatenate([(_mm(ha, wga[j]) + _mm(hb, wgb[j])).astype(BF16) for j in range(4)], axis=1)
            gu_ref[st["rows"], :] = gub
            st["gub"] = gub

        def activate(st):
            gub = st.pop("gub")
            gv, uv = gub[:, 0:DFF].astype(F32), gub[:, DFF:].astype(F32)
            st["act"] = (gv * _sig(gv) * uv).astype(BF16)

        def down(st):
            st["ff"] = _mm(st.pop("act"), wdn[...])

        def head(st):
            x2 = st.pop("x1") + g2_ref[...] * st.pop("ff")
            r3 = lax.rsqrt(jnp.mean(x2 * x2, axis=-1, keepdims=True) + EPS)
            xn = x2 * r3
            fnv = fn_ref[...]
            err = xn * fnv - t_ref[st["rows"], :]
            st["loss"] = jnp.sum(err * err) * (0.5 / D)
            dy = err * (1.0 / D)
            st["dfn"] = jnp.sum(dy * xn, axis=0, keepdims=True)
            u = dy * fnv
            dx2_ref[st["rows"], :] = r3 * u - xn * (r3 * jnp.mean(u * xn, axis=-1, keepdims=True))

        a, b = [dict(rows=slice(k * sub_m, (k + 1) * sub_m)) for k in range(subs)]
        for stage, st in [(proj, a), (norm, a), (proj, b), (gate_up, a), (norm, b), (activate, a), (gate_up, b),
                          (down, a), (activate, b), (head, a), (down, b), (head, b)]:
            stage(st)
        loss_ref[...] += a["loss"] + b["loss"]
        dfn_ref[0:1, :] += a["dfn"] + b["dfn"]

    def tok(w):
        return pl.BlockSpec((tm, w), lambda i: (i, 0))

    hbm = pl.BlockSpec(memory_space=pl.ANY)
    return pl.pallas_call(
        body, name="mlp_fwd", grid=(s // tm,),
        out_shape=[jax.ShapeDtypeStruct((s, D), F32), jax.ShapeDtypeStruct((s, 2 * DFF), BF16),
                   jax.ShapeDtypeStruct((s, D), F32), jax.ShapeDtypeStruct((8, 128), F32),
                   jax.ShapeDtypeStruct((8, D), F32)],
        in_specs=[tok(D), tok(QW), tok(SW), tok(D), hbm, hbm, hbm, hbm,
                  _row((1, D)), _row((1, D)), _row((1, D)), _row((1, D)), _row((1, D))],
        out_specs=[tok(D), tok(2 * DFF), tok(D), _row((8, 128)), _row((8, D))],
        scratch_shapes=[pltpu.VMEM((D, D), BF16), pltpu.VMEM(w_ga.shape, BF16), pltpu.VMEM(w_gb.shape, BF16),
                        pltpu.VMEM((DFF, D), BF16), pltpu.SemaphoreType.DMA((4,))],
        compiler_params=_cp(("arbitrary",)),
    )(x, ya, ys, tgt, w_o, w_ga, w_gb, w_dn, gate1, a2, sh2, gate2, fn)


def _mlp_bwd(x1, gu, dx2, w_o, w_ga, w_gb, w_dn, gate1, a2, sh2, gate2):
    s = x1.shape[0]
    tm = 256
    nj = 2 * DFF // 4

    def body(x1_ref, gu_ref, dx2_ref, wo_hbm, wga_hbm, wgb_hbm, wdn_hbm, g1_ref, a2_ref, s2_ref, g2_ref,
             dx1_ref, dya_ref, dys_ref, act_ref, dgu_ref, h2_ref, dsh_ref, p_ref, wo, wga, wgb, wdn, sem):
        i = pl.program_id(0)
        _load_once(i, [(wo_hbm, wo), (wga_hbm, wga), (wgb_hbm, wgb), (wdn_hbm, wdn)], sem)

        @pl.when(i == 0)
        def _():
            dsh_ref[...] = jnp.zeros_like(dsh_ref)
            p_ref[...] = jnp.zeros_like(p_ref)

        dx2 = dx2_ref[...]
        dact = _mm_nt((dx2 * g2_ref[...]).astype(BF16), wdn[...])
        gub = gu_ref[...]
        gv, uv = gub[:, 0:DFF].astype(F32), gub[:, DFF:].astype(F32)
        sg = _sig(gv)
        sl = gv * sg
        act_ref[...] = (sl * uv).astype(BF16)
        dgu = jnp.concatenate([dact * uv * (sg * (1.0 + gv * (1.0 - sg))), dact * sl], axis=1).astype(BF16)
        dgu_ref[...] = dgu
        dha = sum(_mm_nt(dgu[:, j * nj:(j + 1) * nj], wga[j]) for j in range(4))
        dhb = sum(_mm_nt(dgu[:, j * nj:(j + 1) * nj], wgb[j]) for j in range(4))
        dh = jnp.concatenate([dha, dhb], axis=1)
        x1 = x1_ref[...]
        r2 = lax.rsqrt(jnp.mean(x1 * x1, axis=-1, keepdims=True) + EPS)
        xn = x1 * r2
        a2 = a2_ref[...]
        h2_ref[...] = (xn * a2 + s2_ref[...]).astype(BF16)
        dsh_ref[0:1, :] += jnp.sum(dh, axis=0, keepdims=True)
        p_ref[0:1, :] += jnp.sum(dh * xn, axis=0, keepdims=True)
        u = dh * a2
        dx1 = dx2 + r2 * u - xn * (r2 * jnp.mean(u * xn, axis=-1, keepdims=True))
        dx1_ref[...] = dx1
        dcat = _mm_nt((dx1 * g1_ref[...]).astype(BF16), wo[...])
        dya_ref[...] = dcat[:, 0:QW]
        dys_ref[...] = dcat[:, QW:D]

    def tok(w):
        return pl.BlockSpec((tm, w), lambda i: (i, 0))

    hbm = pl.BlockSpec(memory_space=pl.ANY)
    return pl.pallas_call(
        body, name="mlp_bwd", grid=(s // tm,),
        out_shape=[jax.ShapeDtypeStruct((s, D), F32), jax.ShapeDtypeStruct((s, QW), F32),
                   jax.ShapeDtypeStruct((s, SW), F32), jax.ShapeDtypeStruct((s, DFF), BF16),
                   jax.ShapeDtypeStruct((s, 2 * DFF), BF16), jax.ShapeDtypeStruct((s, D), BF16),
                   jax.ShapeDtypeStruct((8, D), F32), jax.ShapeDtypeStruct((8, D), F32)],
        in_specs=[tok(D), tok(2 * DFF), tok(D), hbm, hbm, hbm, hbm, _row((1, D)), _row((1, D)), _row((1, D)), _row((1, D))],
        out_specs=[tok(D), tok(QW), tok(SW), tok(DFF), tok(2 * DFF), tok(D), _row((8, D)), _row((8, D))],
        scratch_shapes=[pltpu.VMEM((D, D), BF16), pltpu.VMEM(w_ga.shape, BF16), pltpu.VMEM(w_gb.shape, BF16),
                        pltpu.VMEM((DFF, D), BF16), pltpu.SemaphoreType.DMA((4,))],
        compiler_params=_cp(("arbitrary",)),
    )(x1, gu, dx2, w_o, w_ga, w_gb, w_dn, gate1, a2, sh2, gate2)


def _wgrad(name, a, b, gate, w, carry=None):
    s, m = a.shape
    n = b.shape[1]
    tk = min(1024, s)
    nk = s // tk

    def body(a_ref, b_ref, g_ref, w_ref, o_hbm, dg_ref, acc_ref, sem):
        k = pl.program_id(0)

        @pl.when(k == 0)
        def _():
            acc_ref[...] = jnp.zeros_like(acc_ref)

        acc_ref[...] += _mm_tn(a_ref[...], b_ref[...].astype(BF16))

        @pl.when(k == nk - 1)
        def _():
            acc = acc_ref[...]
            dg_ref[...] = jnp.zeros_like(dg_ref)
            dg_ref[0:1, :] = jnp.sum(acc * w_ref[...].astype(F32), axis=0, keepdims=True)
            acc_ref[...] = acc * g_ref[...]
            cp = pltpu.make_async_copy(acc_ref, o_hbm, sem)
            cp.start()
            cp.wait()

    return _pcall(body, [a, b, gate, w], name=name, grid=(nk,),
                  out_shape=[jax.ShapeDtypeStruct((m, n), F32), jax.ShapeDtypeStruct((8, n), F32)],
                  in_specs=[pl.BlockSpec((tk, m), lambda k: (k, 0)), pl.BlockSpec((tk, n), lambda k: (k, 0)),
                            _row((1, n)), _row((m, n))],
                  out_specs=[pl.BlockSpec(memory_space=pl.ANY), _row((8, n))],
                  scratch_shapes=[pltpu.VMEM((m, n), F32), pltpu.SemaphoreType.DMA], carry=carry)


def _wgrad_gate_up(h2, dgu, carry=None):
    s = h2.shape[0]
    tk = min(1024, s)
    nk = s // tk
    n = dgu.shape[1]
    nj = n // 4

    def body(a_ref, b_ref, o_hbm, acc_ref, sems):
        k = pl.program_id(0)

        @pl.when(k == 0)
        def _():
            acc_ref[...] = jnp.zeros_like(acc_ref)

        acc_ref[...] += _mm_tn(a_ref[...], b_ref[...])

        @pl.when(k == nk - 1)
        def _():
            cps = [pltpu.make_async_copy(acc_ref.at[:, pl.ds(j * nj, nj)], o_hbm.at[j], sems.at[j]) for j in range(4)]
            for cp in cps:
                cp.start()
            for cp in cps:
                cp.wait()

    return _pcall(body, [h2, dgu], name="wgrad_gate_up", grid=(nk,),
                  out_shape=[jax.ShapeDtypeStruct((4, D, nj), F32)],
                  in_specs=[pl.BlockSpec((tk, D), lambda k: (k, 0)), pl.BlockSpec((tk, n), lambda k: (k, 0))],
                  out_specs=[pl.BlockSpec(memory_space=pl.ANY)],
                  scratch_shapes=[pltpu.VMEM((D, n), F32), pltpu.SemaphoreType.DMA((4,))], carry=carry)


def _wgrad_in_t(h1, pieces, carry=None):
    s = h1.shape[0]
    rows = IN_W // 4
    tk = min(1024, s)
    nk = s // tk

    def body(a_ref, dq_ref, dkv_ref, dz_ref, dxbc_ref, ddt_ref, o_hbm, acc_ref, tr_ref, sl_ref, sem):
        k = pl.program_id(0)

        @pl.when(k == 0)
        def _():
            acc_ref[...] = jnp.zeros_like(acc_ref)

        dproj = jnp.concatenate([dq_ref[...], dkv_ref[...], dz_ref[...], dxbc_ref[...], ddt_ref[...]], axis=1)
        acc_ref[...] += _mm_tn(a_ref[...], dproj)

        @pl.when(k == nk - 1)
        def _():
            for j in range(PROJ_W // 128):
                tr_ref[j * 128:(j + 1) * 128, :] = acc_ref[:, j * 128:(j + 1) * 128].T
            for j in range(4):
                sl_ref[j] = tr_ref[j * rows:(j + 1) * rows, :]
            cp = pltpu.make_async_copy(sl_ref, o_hbm, sem)
            cp.start()
            cp.wait()

    return _pcall(body, [h1] + list(pieces), name="wgrad_in", grid=(nk,),
                  out_shape=[jax.ShapeDtypeStruct((4, rows, D), F32)],
                  in_specs=[pl.BlockSpec((tk, p.shape[1]), lambda k: (k, 0)) for p in [h1] + list(pieces)],
                  out_specs=[pl.BlockSpec(memory_space=pl.ANY)],
                  scratch_shapes=[pltpu.VMEM((D, PROJ_W), F32), pltpu.VMEM((PROJ_W, D), F32),
                                  pltpu.VMEM((4, rows, D), F32), pltpu.SemaphoreType.DMA],
                  carry=carry)


_SMALL = ["ada_b", "norm1", "conv_w", "conv_b", "dt_bias", "A_log", "D_skip", "sinks", "attn_out_norm",
          "ssm_out_norm", "norm2", "rel_bias", "final_norm"]


def _small_grad(name, gs, chip):
    if name == "ada_b":
        return jnp.concatenate([gs[j:j + 1, :] for j in range(6)], axis=1)
    if name == "conv_w":
        full = gs[7:11, :]
        out = full[:, 0:256]
        for j in range(1, 4):
            out = jnp.where(chip == j, full[:, j * 256:(j + 1) * 256], out)
        return out
    row, width = {"norm1": (6, D), "conv_b": (11, D), "norm2": (12, D), "final_norm": (13, D),
                  "attn_out_norm": (14, QW), "ssm_out_norm": (15, SW), "dt_bias": (16, NH), "A_log": (17, NH),
                  "D_skip": (18, NH), "sinks": (19, NH), "rel_bias": (24, NH)}[name]
    rows = NBUCKET if name == "rel_bias" else 1
    return gs[row:row + rows, 0:width]


def _small_update(small_all, where, ws, ms, vs):
    n = len(_SMALL)

    def body(where_ref, sa_ref, *refs):
        w_refs, m_refs, v_refs, outs = refs[:n], refs[n:2 * n], refs[2 * n:3 * n], refs[3 * n:]
        gs = sa_ref[0]
        for b in range(1, 8):
            gs = gs + sa_ref[b]
        chip = where_ref[1]
        for i, name in enumerate(_SMALL):
            g = _small_grad(name, gs, chip)
            lead = (0,) if name == "conv_w" else ()
            d, mo, vo = _adamw(w_refs[i][lead + (...,)], g, m_refs[i][lead + (...,)], v_refs[i][lead + (...,)])
            for k, val in enumerate((g, d, mo, vo)):
                outs[k * n + i][lead + (...,)] = val
        outs[4 * n][...] = gs[20:21, 0:128]

    shapes = [jax.ShapeDtypeStruct(w.shape, F32) for w in ws]
    vmem = pl.BlockSpec(memory_space=pltpu.VMEM)
    res = pl.pallas_call(
        body, name="small_update", out_shape=shapes * 4 + [jax.ShapeDtypeStruct((1, 128), F32)],
        in_specs=[pl.BlockSpec(memory_space=pltpu.SMEM)] + [vmem] * (1 + 3 * n), out_specs=[vmem] * (4 * n + 1),
    )(where, small_all, *ws, *ms, *vs)
    return [res[k * n:(k + 1) * n] for k in range(4)], res[4 * n][0, 0]


def _add_half(name, g, got, where, by_cols=False):
    rr, cc = got.shape[1:]
    if by_cols:
        mine = pl.BlockSpec((None, rr, cc), lambda i, w_ref: (i, 0, w_ref[0]))
    else:
        mine = pl.BlockSpec((None, None, rr, cc), lambda i, w_ref: (i, w_ref[0], 0, 0))

    def body(w_ref, g_ref, r_ref, o_ref, own_ref):
        s = g_ref[...] + r_ref[...]
        o_ref[...] = s.astype(BF16)

        @pl.when(pl.program_id(0) == w_ref[1])
        def _():
            own_ref[...] = s

    spec = pl.BlockSpec((None, rr, cc), lambda i, w_ref: (i, 0, 0))
    return _pcall(body, [where, g, got], name=name, grid=(4,), nprefetch=1,
                  out_shape=[jax.ShapeDtypeStruct(got.shape, BF16), jax.ShapeDtypeStruct((rr, cc), F32)],
                  in_specs=[mine, spec],
                  out_specs=[spec, pl.BlockSpec((rr, cc), lambda i, w_ref: (0, 0))])


def _add_chips(name, own, got):
    rr, cc = own.shape
    tr = rr // 2 if rr % 32 == 0 else rr

    def body(s_ref, r_ref, o_ref):
        o_ref[...] = ((s_ref[...] + r_ref[0].astype(F32)) + r_ref[1].astype(F32)) + r_ref[2].astype(F32)

    spec = pl.BlockSpec((tr, cc), lambda i: (i, 0))
    return _pcall(body, [own, got], name=name, grid=(rr // tr,), out_shape=[jax.ShapeDtypeStruct((rr, cc), F32)],
                  in_specs=[spec, pl.BlockSpec((3, tr, cc), lambda i: (0, i, 0))], out_specs=[spec])[0]


def _adamw_halves(name, mine, got, w, m, v, where, by_cols=False):
    rr, cc = mine.shape

    def body(w_ref_, t_ref, r_ref, w_ref, m_ref, v_ref, g_ref, d_ref, mo_ref, vo_ref):
        g = jnp.where(pl.program_id(0) == w_ref_[0], t_ref[...], r_ref[...])
        g_ref[...] = g
        d_ref[...], mo_ref[...], vo_ref[...] = _adamw(w_ref[...], g, m_ref[...], v_ref[...])

    if by_cols:
        grid = (2, 1)
        half = pl.BlockSpec((rr, cc), lambda h, i, w_ref_: (0, 0))
        full = pl.BlockSpec((rr, cc), lambda h, i, w_ref_: (0, h))
    else:
        tr = rr // 2
        grid = (2, 2)
        half = pl.BlockSpec((tr, cc), lambda h, i, w_ref_: (i, 0))
        full = pl.BlockSpec((None, tr, cc), lambda h, i, w_ref_: (0, 2 * h + i, 0))
    return _pcall(body, [where, mine, got, w, m, v], name=name, grid=grid, nprefetch=1,
                  out_shape=[jax.ShapeDtypeStruct(w.shape, F32)] * 4,
                  in_specs=[half, half, full, full, full], out_specs=[full] * 4)


def _bias_table(rel_bias, bucket, mask):
    def body(rb_ref, bk_ref, mk_ref, o_ref):
        bk = bk_ref[...]
        valid = mk_ref[...] > 0
        for h in range(NH):
            acc = jnp.zeros((BLK, 2 * BLK), F32)
            for b in range(NBUCKET):
                acc = jnp.where(bk == b, rb_ref[b, h], acc)
            o_ref[h] = jnp.where(valid, acc, NEG)

    vmem = pl.BlockSpec(memory_space=pltpu.VMEM)
    return pl.pallas_call(
        body, name="bias_table", out_shape=jax.ShapeDtypeStruct((NH, BLK, 2 * BLK), F32),
        in_specs=[pl.BlockSpec(memory_space=pltpu.SMEM), vmem, vmem], out_specs=vmem,
    )(rel_bias, bucket, mask)


def _pack_small(dsh1, p1, dsh2, p2, dg1a, dg1b, dg2, norm1, norm2, scale1, scale2, dcw, dcb, dfn,
                dnw_attn, dnw_ssm, dhd, av, dsink, drel, loss_acc):
    def body(dsh1_ref, p1_ref, dsh2_ref, p2_ref, dg1a_ref, dg1b_ref, dg2_ref, n1_ref, n2_ref, s1_ref, s2_ref,
             dcw_ref, dcb_ref, dfn_ref, da_ref, ds_ref, dhd_ref, av_ref, dsink_ref, drel_ref, loss_ref, o_ref):
        o_ref[...] = jnp.zeros_like(o_ref)
        p1v, p2v = p1_ref[0:1, :], p2_ref[0:1, :]
        o_ref[0:1, :] = dsh1_ref[0:1, :]
        o_ref[1:2, :] = p1v * n1_ref[...]
        o_ref[2:3, :] = dg1a_ref[0:1, :] + dg1b_ref[0:1, :]
        o_ref[3:4, :] = dsh2_ref[0:1, :]
        o_ref[4:5, :] = p2v * n2_ref[...]
        o_ref[5:6, :] = dg2_ref[0:1, :]
        o_ref[6:7, :] = p1v * (1.0 + s1_ref[...])
        o_ref[7:11, :] = dcw_ref[0:4, :]
        o_ref[11:12, :] = dcb_ref[0:1, :]
        o_ref[12:13, :] = p2v * (1.0 + s2_ref[...])
        o_ref[13:14, :] = dfn_ref[0:1, :]
        o_ref[14:15, 0:QW] = da_ref[0:1, :]
        o_ref[15:16, 0:SW] = ds_ref[0:1, :]
        o_ref[16:17, 0:128] = dhd_ref[0:1, :]
        o_ref[17:18, 0:128] = dhd_ref[1:2, :] * av_ref[...]
        o_ref[18:19, 0:128] = dhd_ref[2:3, :]
        o_ref[19:20, 0:128] = dsink_ref[0:1, :]
        o_ref[20:21, 0:128] = loss_ref[0:1, :]
        o_ref[24:56, 0:128] = drel_ref[...]

    return pl.pallas_call(body, name="pack_small", out_shape=jax.ShapeDtypeStruct((56, D), F32))(
        dsh1, p1, dsh2, p2, dg1a, dg1b, dg2, norm1, norm2, scale1, scale2, dcw, dcb, dfn,
        dnw_attn, dnw_ssm, dhd, av, dsink, drel, loss_acc)


def _pad_row(a, rows=1):
    return jnp.pad(a.reshape(rows, -1), ((0, 0), (0, D - a.size // rows)))


def kernel(x, c, ada_w, ada_b, norm1, w_in, conv_w, conv_b, dt_bias, A_log, D_skip, sinks, attn_out_norm, ssm_out_norm, w_o, norm2, w_gate_up, w_down, rel_bias, final_norm, loss_target, m_ada_w, m_ada_b, m_norm1, m_w_in, m_conv_w, m_conv_b, m_dt_bias, m_A_log, m_D_skip, m_sinks, m_attn_out_norm, m_ssm_out_norm, m_w_o, m_norm2, m_w_gate_up, m_w_down, m_rel_bias, m_final_norm, v_ada_w, v_ada_b, v_norm1, v_w_in, v_conv_w, v_conv_b, v_dt_bias, v_A_log, v_D_skip, v_sinks, v_attn_out_norm, v_ssm_out_norm, v_w_o, v_norm2, v_w_gate_up, v_w_down, v_rel_bias, v_final_norm):
    xi, yi, ci = lax.axis_index("x"), lax.axis_index("y"), lax.axis_index("c")
    chip = 2 * xi + yi
    me = 4 * xi + 2 * yi + ci
    where = jnp.stack([ci, chip]).astype(jnp.int32)
    xs2, tgt = x[0], loss_target[0]

    first = jnp.concatenate([c, _pad_row(conv_w[0], CK), jnp.zeros((3, D), F32)], axis=0)
    w_in_t, m_w_in_t, v_w_in_t = w_in[0].T, m_w_in[0].T, v_w_in[0].T
    w_in_b, w_o_b, w_dn_b = w_in_t.astype(BF16), w_o[0].astype(BF16), w_down[0].astype(BF16)
    w_gu_b = w_gate_up[0].astype(BF16)
    ncol = ada_w.shape[2]
    first_all, w_in_f, mod_all = _front(first, w_in_b, ada_w[0], lax.dynamic_slice(ada_b, (0, chip * ncol), (1, ncol)))
    c_all = first_all[:, 0, :]
    cw_full = jnp.concatenate([first_all[2 * j, 1:1 + CK, 0:256] for j in range(4)], axis=1)
    mod = lax.dynamic_slice(jnp.transpose(mod_all, (1, 0, 2)).reshape(8, 4 * ncol), (me, 0), (1, 4 * ncol))
    shift1, scale1, gate1, shift2, scale2, gate2 = [mod[:, j * D:(j + 1) * D] for j in range(6)]
    a1 = norm1 * (1.0 + scale1)
    a2 = norm2 * (1.0 + scale2)

    hdn = DFF // 8
    q, kv, z, xbc, dtr, w_o_g, w_dna_g = _in_proj_fwd(xs2, a1, shift1, w_in_f,
                                                      carry=_gather_chips_carry([w_o_b, w_dn_b[0:hdn]]))
    w_o_f = w_o_g.reshape(D, D)
    bucket, mask = _attn_geometry()
    bucket = jnp.asarray(bucket)
    bias = _bias_table(rel_bias, bucket, jnp.asarray(mask.astype(np.int32)))
    sinks1 = sinks[0]
    ya, w_ga_g = _attn_fwd(q, kv, bias, sinks1, attn_out_norm, carry=_gather_chips_carry([w_gu_b[0:D // 2]]))
    cw8 = jnp.concatenate([cw_full, jnp.zeros((4, XBCW), F32)], axis=0)
    dtb = _pad_row(dt_bias)[:, 0:128]
    av = _pad_row(-jnp.exp(A_log))[:, 0:128]
    dk = jnp.repeat(D_skip, HD, axis=1)
    ys, hs, pre, w_gb_g, w_dnb_g = _ssd_fwd(z, xbc, dtr, cw8, conv_b, dtb, av, dk, ssm_out_norm,
                                            carry=_gather_chips_carry([w_gu_b[D // 2:D], w_dn_b[hdn:2 * hdn]]))
    w_dn_f = jnp.stack([w_dna_g, w_dnb_g], axis=1).reshape(DFF, D)
    fn = final_norm[None, :]
    x1, gu, dx2, loss_acc, dfn = _mlp_fwd(xs2, ya, ys, tgt, w_o_f, w_ga_g, w_gb_g, w_dn_f, gate1, a2, shift2, gate2, fn)

    def to_sibling(p):
        return _Carry([p], [jax.ShapeDtypeStruct((4,) + p.shape[2:], F32)],
                      lambda x_, y_, c_: [(_SIBLING, 0, (j, 1 - c_), 0, j) for j in range(4)])

    def to_chips(s4):
        return _Carry([s4], [jax.ShapeDtypeStruct((3,) + s4.shape[1:], s4.dtype)],
                      lambda x_, y_, c_: [(f, 0, jnp.bitwise_xor(2 * x_ + y_, k + 1), 0, k) for k, f in enumerate(_CHIPS3)])

    def back(t):
        return _Carry([t[None]], [jax.ShapeDtypeStruct((1,) + t.shape, F32)], lambda x_, y_, c_: [(_SIBLING, 0, 0, 0, 0)])

    dx1, dya, dys, act, dgu, h2, dsh2, p2 = _mlp_bwd(x1, gu, dx2, w_o_f, w_ga_g, w_gb_g, w_dn_f, gate1, a2, shift2, gate2)
    p_gu = _wgrad_gate_up(h2, dgu)[0].reshape(4, 2, D // 2, 2 * DFF // 4)
    dq, dkv, dbias, dsink, dnw_attn, g_dn, dg2, got1_gu = _attn_bwd(
        q, kv, dya, bias, sinks1, attn_out_norm, act, dx2, gate2, w_dn_f, carry=to_sibling(p_gu))
    p_dn = g_dn.reshape(4, 2, DFF // 8, D)
    drel = _rel_bias_grad(dbias, bucket)
    s4_gu, own_gu = _add_half("rs_add_half_gu", p_gu, got1_gu, where)
    dz, dxbc, ddt, dcw, dcb, dnw_ssm, dhd, got2_gu, got1_dn = _ssd_bwd(
        z, xbc, pre, dtr, dys, hs, cw8, dtb, av, dk, ssm_out_norm, carry=_merge(to_chips(s4_gu), to_sibling(p_dn)))
    mine_gu = _add_chips("rs_add_chips_gu", own_gu, got2_gu)
    s4_dn, own_dn = _add_half("rs_add_half_dn", p_dn, got1_dn, where)
    grad_x, h1, dsh1, p1 = _in_proj_bwd(xs2, dx1, a1, shift1, w_in_f, dq, dkv, dz, dxbc, ddt)
    p_in, got2_dn, got3_gu = _wgrad_in_t(h1, [dq, dkv, dz, dxbc, ddt], carry=_merge(to_chips(s4_dn), back(mine_gu)))
    mine_dn = _add_chips("rs_add_chips_dn", own_dn, got2_dn)

    def to_sibling_cols(p):
        return _Carry([p], [jax.ShapeDtypeStruct(p.shape[:2] + (D // 2,), F32)],
                      lambda x_, y_, c_: [(_SIBLING, 0, (j, slice(None), pl.ds((1 - c_) * (D // 2), D // 2)), 0, j)
                                          for j in range(4)])

    no_dg1 = jnp.zeros((8, D), F32)
    small = _pack_small(dsh1, p1, dsh2, p2, no_dg1, no_dg1, dg2, norm1, norm2, scale1, scale2, dcw, dcb, dfn,
                        dnw_attn, dnw_ssm, dhd, av, dsink, drel, loss_acc)
    g_oa, dg1a, got1_in, got3_dn, small_all = _wgrad(
        "wgrad_o_attn", ya, dx1, gate1, w_o_f[0:QW],
        carry=_merge(to_sibling_cols(p_in), back(mine_dn), _gather8_carry(small)))
    s4_in, own_in = _add_half("rs_add_half_in", p_in, got1_in, where, by_cols=True)
    g_os, dg1b, got2_in = _wgrad("wgrad_o_ssm", ys, dx1, gate1, w_o_f[QW:D], carry=to_chips(s4_in))
    mine_in = _add_chips("rs_add_chips_in", own_in, got2_in)

    dg1_all, got3_in, mine_o, got3_o = _tail(g_oa, g_os, dg1a, dg1b, mine_in)
    small_all = small_all.at[:, 2, :].set(dg1_all[:, 0, :])
    small_res, loss = _small_update(
        small_all, where,
        [ada_b, norm1, conv_w, conv_b, dt_bias, A_log, D_skip, sinks, attn_out_norm, ssm_out_norm, norm2, rel_bias,
         final_norm[None, :]],
        [m_ada_b, m_norm1, m_conv_w, m_conv_b, m_dt_bias, m_A_log, m_D_skip, m_sinks, m_attn_out_norm,
         m_ssm_out_norm, m_norm2, m_rel_bias, m_final_norm[None, :]],
        [v_ada_b, v_norm1, v_conv_w, v_conv_b, v_dt_bias, v_A_log, v_D_skip, v_sinks, v_attn_out_norm,
         v_ssm_out_norm, v_norm2, v_rel_bias, v_final_norm[None, :]])
    small_out = [dict(zip(_SMALL, r)) for r in small_res]
    for r in small_out:
        r["final_norm"] = r["final_norm"][0]

    dmod_all = small_all[:, 0:6, :].reshape(8, 6 * D)
    dmod_loc = lax.dynamic_slice(dmod_all, (0, chip * ncol), (8, ncol))
    ada_out = _ada_bwd_adamw(c_all.T, dmod_loc, ada_w[0], m_ada_w[0], v_ada_w[0])

    big_gu = _adamw_halves("adamw_gate_up", mine_gu, got3_gu[0], w_gate_up, m_w_gate_up, v_w_gate_up, where)
    big_dn = _adamw_halves("adamw_down", mine_dn, got3_dn[0], w_down, m_w_down, v_w_down, where)
    big_o = _adamw_halves("adamw_o", mine_o, got3_o, w_o, m_w_o, v_w_o, where)
    big_in = [o.T[None] for o in _adamw_halves("adamw_in", mine_in, got3_in, w_in_t, m_w_in_t, v_w_in_t, where,
                                               by_cols=True)]
    big = [big_in, big_o, big_gu, big_dn]

    order = ["ada_w", "ada_b", "norm1", "w_in", "conv_w", "conv_b", "dt_bias", "A_log", "D_skip", "sinks",
             "attn_out_norm", "ssm_out_norm", "w_o", "norm2", "w_gate_up", "w_down", "rel_bias", "final_norm"]
    bigname = {"w_in": 0, "w_o": 1, "w_gate_up": 2, "w_down": 3}
    res = [loss, grad_x[None]]
    for kind in range(4):
        for nm in order:
            if nm == "ada_w":
                res.append(ada_out[kind][None])
            elif nm in bigname:
                res.append(big[bigname[nm]][kind])
            else:
                res.append(small_out[kind][nm])
    return tuple(res)
```

```python
import numpy as np
import jax
import jax.numpy as jnp
from jax import lax
from jax.experimental import pallas as pl
from jax.experimental.pallas import tpu as pltpu

F32, BF16 = jnp.float32, jnp.bfloat16
HI = lax.Precision.HIGHEST

D = 1024
QW, KVW = 512, 128
NH, HD, NKV = 8, 64, 2
SW = 512
NST = 128
XBCW = 1024
CK = 4
BLK = 128
DFF = 2816
IN_W = 2312
PROJ_W = 2432
EPS = 1e-6
NEG = -1e30
NBUCKET = 32

B1, B2, LR, AEPS, WD, STEP = 0.9, 0.999, 0.001, 1e-08, 0.01, 10

VMEM_LIMIT = 56 * 1024 * 1024

_NT = (((1,), (1,)), ((), ()))
_TN = (((0,), (0,)), ((), ()))


def _mm(a, b):
    return jnp.dot(a, b, preferred_element_type=F32)


def _mm_nt(a, b):
    return lax.dot_general(a, b, _NT, preferred_element_type=F32)


def _mm_tn(a, b):
    return lax.dot_general(a, b, _TN, preferred_element_type=F32)


def _mm_hi(a, b):
    return jnp.dot(a, b, preferred_element_type=F32, precision=HI)


def _split3(x):
    hi = x.astype(BF16)
    r = x - hi.astype(F32)
    mid = r.astype(BF16)
    lo = (r - mid.astype(F32)).astype(BF16)
    return hi, mid, lo


def _sel_r(x, e):
    hi, mid, lo = _split3(x)
    return (_mm(hi, e) + _mm(mid, e)) + _mm(lo, e)


def _sel_l(e, x):
    hi, mid, lo = _split3(x)
    return (_mm(e, hi) + _mm(e, mid)) + _mm(e, lo)


def _sig(x):
    return 1.0 / (1.0 + jnp.exp(-x))


def _cp(sem):
    return pltpu.CompilerParams(dimension_semantics=sem, vmem_limit_bytes=VMEM_LIMIT)


def _row(shape):
    nd = len(shape)
    return pl.BlockSpec(shape, lambda *_: (0,) * nd)


def _adamw(w, g, m, v):
    m = B1 * m + (1.0 - B1) * g
    v = B2 * v + (1.0 - B2) * (g * g)
    m_hat = m / (1.0 - B1 ** STEP)
    v_hat = v / (1.0 - B2 ** STEP)
    delta = -LR * (m_hat / (jnp.sqrt(v_hat) + AEPS) + WD * w)
    return delta, m, v


class _Carry:
    def __init__(self, inps, outs, copies):
        self.inps, self.outs, self.copies = list(inps), list(outs), copies
        self.n = len(copies(0, 0, 0))

    def descriptors(self, in_refs, out_refs, send_sems, recv_sems):
        x, y, c = lax.axis_index("x"), lax.axis_index("y"), lax.axis_index("c")
        out = []
        for j, (flip, a, si, o, di) in enumerate(self.copies(x, y, c)):
            if flip is None:
                out.append(pltpu.make_async_copy(in_refs[a].at[si], out_refs[o].at[di], send_sems.at[j]))
            else:
                fx, fy, fc = flip
                peer = (1 - x if fx else x, 1 - y if fy else y, 1 - c if fc else c)
                out.append(pltpu.make_async_remote_copy(
                    src_ref=in_refs[a].at[si], dst_ref=out_refs[o].at[di],
                    send_sem=send_sems.at[j], recv_sem=recv_sems.at[j],
                    device_id=peer, device_id_type=pl.DeviceIdType.MESH))
        return out


def _pcall(body, args, *, name, grid, in_specs, out_specs, out_shape, scratch_shapes=(), sem=None, nprefetch=0,
           carry=None):
    out_shape, out_specs = list(out_shape), list(out_specs)
    in_specs, scratch_shapes = list(in_specs), list(scratch_shapes)
    nin, nout, nscr = len(in_specs), len(out_shape), len(scratch_shapes)
    run = body
    if carry is not None:
        ncin, ncout = len(carry.inps), len(carry.outs)
        hbm = pl.BlockSpec(memory_space=pl.ANY)

        def run(*refs):
            pre, r = refs[:nprefetch], refs[nprefetch:]
            ins, cins = r[:nin], r[nin:nin + ncin]
            r = r[nin + ncin:]
            outs, couts = r[:nout], r[nout:nout + ncout]
            r = r[nout + ncout:]
            scr, (send_sems, recv_sems) = r[:nscr], r[nscr:]
            first = pl.program_id(0) == 0
            last = pl.program_id(0) == grid[0] - 1
            for ax in range(1, len(grid)):
                first = jnp.logical_and(first, pl.program_id(ax) == 0)
                last = jnp.logical_and(last, pl.program_id(ax) == grid[ax] - 1)

            @pl.when(first)
            def _():
                for d in carry.descriptors(cins, couts, send_sems, recv_sems):
                    d.start()

            body(*pre, *ins, *outs, *scr)

            @pl.when(last)
            def _():
                for d in carry.descriptors(cins, couts, send_sems, recv_sems):
                    d.wait()

        in_specs = in_specs + [hbm] * ncin
        out_specs = out_specs + [hbm] * ncout
        out_shape = out_shape + carry.outs
        scratch_shapes = scratch_shapes + [pltpu.SemaphoreType.DMA((carry.n,)), pltpu.SemaphoreType.DMA((carry.n,))]
        args = list(args) + carry.inps
    if sem is None:
        sem = ("arbitrary",) * len(grid)
    if nprefetch:
        kw = dict(grid_spec=pltpu.PrefetchScalarGridSpec(num_scalar_prefetch=nprefetch, grid=grid, in_specs=in_specs,
                                                         out_specs=out_specs, scratch_shapes=scratch_shapes))
    else:
        kw = dict(grid=grid, in_specs=in_specs, out_specs=out_specs, scratch_shapes=scratch_shapes)
    res = pl.pallas_call(run, name=name, out_shape=out_shape, compiler_params=_cp(sem), **kw)(*args)
    return list(res)


def _merge(*carries):
    inps, outs, offs = [], [], []
    for cr in carries:
        offs.append((len(inps), len(outs)))
        inps += cr.inps
        outs += cr.outs

    def copies(x, y, c):
        return [(f, a + io, si, o + oo, di) for cr, (io, oo) in zip(carries, offs) for f, a, si, o, di in cr.copies(x, y, c)]

    return _Carry(inps, outs, copies)


_ALL7 = [(f >> 2 & 1, f >> 1 & 1, f & 1) for f in range(1, 8)]
_CHIPS3 = [(0, 1, 0), (1, 0, 0), (1, 1, 0)]
_SIBLING = (0, 0, 1)


def _gather8_carry(blk):
    def copies(x, y, c):
        me = 4 * x + 2 * y + c
        return [(None, 0, 0, 0, me)] + [(f, 0, 0, 0, me) for f in _ALL7]

    return _Carry([blk[None]], [jax.ShapeDtypeStruct((8,) + blk.shape, blk.dtype)], copies)


def _gather_chips_carry(blks):
    def copies(x, y, c):
        chip = 2 * x + y
        return [(f, a, 0, a, chip) for a in range(len(blks)) for f in [None] + _CHIPS3]

    return _Carry([b[None] for b in blks], [jax.ShapeDtypeStruct((4,) + b.shape, b.dtype) for b in blks], copies)


def _front(first, w_in_t, w_loc, b_loc):
    n = w_loc.shape[1]
    rows = w_in_t.shape[0]
    hw = D // 2
    gather = _gather8_carry(first)
    fetch = _Carry([w_in_t], [jax.ShapeDtypeStruct((4, rows, hw), BF16)],
                   lambda x_, y_, c_: [(f, 0, (slice(None), pl.ds(c_ * hw, hw)), 0, 2 * x_ + y_) for f in [None] + _CHIPS3])
    phase_a = _merge(gather, fetch)
    send_mod = _Carry([None], [None], lambda x_, y_, c_: [(f, 0, slice(None), 0, 2 * x_ + y_) for f in [None] + _CHIPS3])
    swap = _Carry([None], [None], lambda x_, y_, c_: [(_SIBLING, 0, slice(None), 0, slice(None))])

    def body(first_hbm, w_hbm, wl_hbm, bl_ref, first_all, w_full, mod_all,
             c_scr, mod_scr, w_half, w_other, wf_scr, wl_ref, sa, ra, sb, rb, sc, rc, sl, sw):
        loads = [pltpu.make_async_copy(wl_hbm.at[:, pl.ds(j * 512, 512)], wl_ref.at[:, pl.ds(j * 512, 512)], sw.at[j])
                 for j in range(n // 512)]
        da = phase_a.descriptors([first_hbm, w_hbm], [first_all, w_half], sa, ra)
        for d in da:
            d.start()
        for j, ld in enumerate(loads):
            ld.start(priority=j % 2)
        for d in da[:gather.n]:
            d.wait()
        cp = pltpu.make_async_copy(first_all, c_scr, sl)
        cp.start()
        cp.wait()
        cv = c_scr[:, 0, :]
        cond = cv * _sig(cv)
        for j in range(n // 512):
            cols = slice(j * 512, (j + 1) * 512)
            loads[j].wait()
            mod_scr[:, cols] = _mm_hi(cond, wl_ref[:, cols]) + bl_ref[:, cols]
        db = send_mod.descriptors([mod_scr], [mod_all], sb, rb)
        for d in db:
            d.start()
        for d in da[gather.n:]:
            d.wait()
        dc = swap.descriptors([w_half], [w_other], sc, rc)
        for d in dc:
            d.start()
        for d in db + dc:
            d.wait()
        core = lax.axis_index("c")
        for mine_first in (True, False):
            @pl.when(core == (0 if mine_first else 1))
            def _(mine_first=mine_first):
                lo, hi = (w_half, w_other) if mine_first else (w_other, w_half)
                for j in range(4):
                    wf_scr[j * rows:(j + 1) * rows, 0:hw] = lo[j]
                    wf_scr[j * rows:(j + 1) * rows, hw:D] = hi[j]
        wf_scr[4 * rows:, :] = jnp.zeros((PROJ_W - 4 * rows, D), BF16)
        cp = pltpu.make_async_copy(wf_scr, w_full, sl)
        cp.start()
        cp.wait()

    hbm = pl.BlockSpec(memory_space=pl.ANY)
    vmem = pl.BlockSpec(memory_space=pltpu.VMEM)
    sems = pltpu.SemaphoreType.DMA
    return pl.pallas_call(
        body, name="front",
        out_shape=[jax.ShapeDtypeStruct((8,) + first.shape, F32), jax.ShapeDtypeStruct((PROJ_W, D), BF16),
                   jax.ShapeDtypeStruct((4, 8, n), F32)],
        in_specs=[hbm, hbm, hbm, vmem], out_specs=[hbm] * 3,
        scratch_shapes=[pltpu.VMEM((8,) + first.shape, F32), pltpu.VMEM((8, n), F32),
                        pltpu.VMEM((4, rows, hw), BF16), pltpu.VMEM((4, rows, hw), BF16), pltpu.VMEM((PROJ_W, D), BF16),
                        pltpu.VMEM((D, n), F32),
                        sems((phase_a.n,)), sems((phase_a.n,)), sems((4,)), sems((4,)), sems((1,)), sems((1,)), sems,
                        sems((n // 512,))],
        compiler_params=pltpu.CompilerParams(vmem_limit_bytes=VMEM_LIMIT),
    )(first[None], w_in_t, w_loc, b_loc)


def _tail(g_a, g_b, row_a, row_b, mine_in):
    rr, cc = g_a.shape[0] // 4, g_a.shape[1]

    def copies_a(x_, y_, c_):
        me = 4 * x_ + 2 * y_ + c_
        out = [(None, 0, 0, 0, me)] + [(f, 0, 0, 0, me) for f in _ALL7]
        out += [(_SIBLING, 1 + j // 2, pl.ds((j % 2) * 2 * rr + (1 - c_) * rr, rr), 1, j) for j in range(4)]
        out += [(None, 1 + j // 2, pl.ds((j % 2) * 2 * rr + c_ * rr, rr), 2, j) for j in range(4)]
        return out + [(_SIBLING, 3, slice(None), 3, slice(None))]

    phase_a = _Carry([None] * 4, [None] * 4, copies_a)
    to_chips = _Carry([None], [None], lambda x_, y_, c_: [(f, 0, jnp.bitwise_xor(2 * x_ + y_, k + 1), 0, k)
                                                        for k, f in enumerate(_CHIPS3)])
    back = _Carry([None], [None] * 2, lambda x_, y_, c_: [(None, 0, slice(None), 0, slice(None)),
                                                        (_SIBLING, 0, slice(None), 1, slice(None))])

    def body(ga_hbm, gb_hbm, ra_ref, rb_ref, in_hbm, rows_all, got_in, mine_o, got_o,
             row_scr, got1_scr, mine_scr, s4_scr, got2_scr, red_scr, sa, ra, sb, rb, sc, rc):
        chip = 2 * lax.axis_index("x") + lax.axis_index("y")
        row_scr[0] = ra_ref[0:1, :] + rb_ref[0:1, :]
        da = phase_a.descriptors([row_scr, ga_hbm, gb_hbm, in_hbm], [rows_all, got1_scr, mine_scr, got_in], sa, ra)
        for d in da:
            d.start()
        for d in da[8:16]:
            d.wait()
        for j in range(4):
            s4_scr[j] = (mine_scr[j] + got1_scr[j]).astype(BF16)
        red_scr[...] = mine_scr[chip] + got1_scr[chip]
        db = to_chips.descriptors([s4_scr], [got2_scr], sb, rb)
        for d in db:
            d.start()
        for d in db:
            d.wait()
        red_scr[...] = ((red_scr[...] + got2_scr[0].astype(F32)) + got2_scr[1].astype(F32)) + got2_scr[2].astype(F32)
        dc = back.descriptors([red_scr], [mine_o, got_o], sc, rc)
        for d in dc:
            d.start()
        for d in da[:8] + da[16:] + dc:
            d.wait()

    hbm = pl.BlockSpec(memory_space=pl.ANY)
    vmem = pl.BlockSpec(memory_space=pltpu.VMEM)
    sems = pltpu.SemaphoreType.DMA
    half = jax.ShapeDtypeStruct((rr, cc), F32)
    return pl.pallas_call(
        body, name="tail",
        out_shape=[jax.ShapeDtypeStruct((8, 1, cc), F32), jax.ShapeDtypeStruct(mine_in.shape, F32), half, half],
        in_specs=[hbm, hbm, vmem, vmem, hbm], out_specs=[hbm] * 4,
        scratch_shapes=[pltpu.VMEM((1, 1, cc), F32), pltpu.VMEM((4, rr, cc), F32), pltpu.VMEM((4, rr, cc), F32),
                        pltpu.VMEM((4, rr, cc), BF16), pltpu.VMEM((3, rr, cc), BF16), pltpu.VMEM((rr, cc), F32),
                        sems((phase_a.n,)), sems((phase_a.n,)), sems((3,)), sems((3,)), sems((2,)), sems((2,))],
        compiler_params=pltpu.CompilerParams(vmem_limit_bytes=VMEM_LIMIT),
    )(g_a, g_b, row_a, row_b, mine_in)


def _ada_bwd_adamw(c_all_t, dmod_loc, w, m, v, carry=None):
    n = w.shape[1]
    tn = 512

    def body(ct_ref, dm_ref, w_ref, m_ref, v_ref, g_ref, d_ref, mo_ref, vo_ref):
        ct = ct_ref[...]
        cond = ct * _sig(ct)
        dm = dm_ref[...]
        g = cond[:, 0:1] * dm[0:1, :]
        for b in range(1, 8):
            g = g + cond[:, b:b + 1] * dm[b:b + 1, :]
        g_ref[...] = g
        d_ref[...], mo_ref[...], vo_ref[...] = _adamw(w_ref[...], g, m_ref[...], v_ref[...])

    wspec = pl.BlockSpec((D, tn), lambda j: (0, j))
    return _pcall(
        body, [c_all_t, dmod_loc, w, m, v], name="ada_bwd_adamw", grid=(n // tn,),
        out_shape=[jax.ShapeDtypeStruct((D, n), F32)] * 4,
        in_specs=[_row((D, 8)), pl.BlockSpec((8, tn), lambda j: (0, j)), wspec, wspec, wspec],
        out_specs=[wspec] * 4, carry=carry)


def _in_proj_fwd(x, a1, sh1, w_in, carry=None):
    s = x.shape[0]
    tm = 512

    def body(x_ref, a_ref, s_ref, w_ref, q_ref, kv_ref, z_ref, xbc_ref, dt_ref):
        def norm(rows):
            xv = x_ref[rows, :]
            r = lax.rsqrt(jnp.mean(xv * xv, axis=-1, keepdims=True) + EPS)
            return (xv * r * a_ref[...] + s_ref[...]).astype(BF16)

        def project(rows, h):
            p = _mm_nt(h, w_ref[...])
            q_ref[rows, :] = p[:, 0:512].astype(BF16)
            kv_ref[rows, :] = p[:, 512:768].astype(BF16)
            z_ref[rows, :] = p[:, 768:1280]
            xbc_ref[rows, :] = p[:, 1280:2304]
            dt_ref[rows, :] = p[:, 2304:2432]

        r0, r1 = slice(0, tm // 2), slice(tm // 2, tm)
        h0 = norm(r0)
        project(r0, h0)
        project(r1, norm(r1))

    def tok(w):
        return pl.BlockSpec((tm, w), lambda i: (i, 0))

    return _pcall(
        body, [x, a1, sh1, w_in], name="in_proj_fwd", grid=(s // tm,),
        out_shape=[jax.ShapeDtypeStruct((s, QW), BF16), jax.ShapeDtypeStruct((s, 2 * KVW), BF16),
                   jax.ShapeDtypeStruct((s, SW), F32), jax.ShapeDtypeStruct((s, XBCW), F32),
                   jax.ShapeDtypeStruct((s, 128), F32)],
        in_specs=[tok(D), _row((1, D)), _row((1, D)), _row((PROJ_W, D))],
        out_specs=[tok(QW), tok(2 * KVW), tok(SW), tok(XBCW), tok(128)], carry=carry)


def _in_proj_bwd(x, dx1, a1, sh1, w_in, dq, dkv, dz, dxbc, ddt, carry=None):
    s = x.shape[0]
    tm = 512

    def body(x_ref, dx1_ref, a_ref, s_ref, w_ref, dq_ref, dkv_ref, dz_ref, dxbc_ref, ddt_ref,
             gx_ref, h_ref, dsh_ref, p_ref):
        i = pl.program_id(0)

        @pl.when(i == 0)
        def _():
            dsh_ref[...] = jnp.zeros_like(dsh_ref)
            p_ref[...] = jnp.zeros_like(p_ref)

        def gather(st):
            rows = st["rows"]
            st["dproj"] = jnp.concatenate([dq_ref[rows, :], dkv_ref[rows, :], dz_ref[rows, :], dxbc_ref[rows, :],
                                           ddt_ref[rows, :]], axis=1)

        def back(st):
            st["dh"] = _mm(st.pop("dproj"), w_ref[...])

        def norm(st):
            rows, dh = st["rows"], st.pop("dh")
            xv = x_ref[rows, :]
            r = lax.rsqrt(jnp.mean(xv * xv, axis=-1, keepdims=True) + EPS)
            xn = xv * r
            a = a_ref[...]
            h_ref[rows, :] = (xn * a + s_ref[...]).astype(BF16)
            st["dsh"] = jnp.sum(dh, axis=0, keepdims=True)
            st["p"] = jnp.sum(dh * xn, axis=0, keepdims=True)
            u = dh * a
            gx_ref[rows, :] = dx1_ref[rows, :] + r * u - xn * (r * jnp.mean(u * xn, axis=-1, keepdims=True))

        g0, g1 = [dict(rows=slice(k * (tm // 2), (k + 1) * (tm // 2))) for k in range(2)]
        for stage, st in [(gather, g0), (back, g0), (gather, g1), (norm, g0), (back, g1), (norm, g1)]:
            stage(st)
        dsh_ref[0:1, :] += g0["dsh"] + g1["dsh"]
        p_ref[0:1, :] += g0["p"] + g1["p"]

    def tok(w):
        return pl.BlockSpec((tm, w), lambda i: (i, 0))

    return _pcall(
        body, [x, dx1, a1, sh1, w_in, dq, dkv, dz, dxbc, ddt], name="in_proj_bwd", grid=(s // tm,),
        out_shape=[jax.ShapeDtypeStruct((s, D), F32), jax.ShapeDtypeStruct((s, D), BF16),
                   jax.ShapeDtypeStruct((8, D), F32), jax.ShapeDtypeStruct((8, D), F32)],
        in_specs=[tok(D), tok(D), _row((1, D)), _row((1, D)), _row((PROJ_W, D)),
                  tok(QW), tok(2 * KVW), tok(SW), tok(XBCW), tok(128)],
        out_specs=[tok(D), tok(D), _row((8, D)), _row((8, D))], carry=carry)


def _attn_geometry():
    dist = np.arange(BLK)[:, None] + BLK - np.arange(2 * BLK)[None, :]
    n = np.maximum(dist, 0)
    max_exact = NBUCKET // 2
    large = max_exact + (np.log(np.maximum(n, 1) / max_exact) / np.log(128 / max_exact)
                         * (NBUCKET - max_exact)).astype(np.int32)
    large = np.minimum(large, NBUCKET - 1)
    bucket = np.where(n < max_exact, n, large).astype(np.int32)
    mask = (dist >= 0) & (dist < 128)
    return bucket, mask


def _attn_heads(is_first, q_blk, kvw, bias_ref, sinks_ref):
    qv = q_blk * 0.125
    col = lax.broadcasted_iota(jnp.int32, (BLK, 2 * BLK), 1)
    first = jnp.where(jnp.logical_and(is_first, col < BLK), NEG, 0.0)
    groups = []
    for g in range(NKV):
        qs = jnp.concatenate([qv[:, (4 * g + r) * HD:(4 * g + r + 1) * HD] for r in range(4)], axis=0)
        kw = kvw[:, g * HD:(g + 1) * HD]
        vw = kvw[:, KVW + g * HD:KVW + (g + 1) * HD]
        sc = _mm_nt(qs, kw)
        pn, ps = [], []
        for r in range(4):
            h = 4 * g + r
            sr = sc[r * BLK:(r + 1) * BLK] + bias_ref[h] + first
            sink = sinks_ref[h]
            m = jnp.maximum(jnp.max(sr, axis=-1, keepdims=True), sink)
            p = jnp.exp(sr - m)
            es = jnp.exp(sink - m)
            inv = 1.0 / (jnp.sum(p, axis=-1, keepdims=True) + es)
            pn.append(p * inv)
            ps.append(es * inv)
        pn = jnp.concatenate(pn, axis=0)
        ps = jnp.concatenate(ps, axis=0)
        o = _mm(pn.astype(BF16), vw)
        groups.append((qs, kw, vw, pn, ps, o))
    return groups


def _unstack_heads(parts):
    return jnp.concatenate([p[r * BLK:(r + 1) * BLK] for p in parts for r in range(4)], axis=1)


NB = 2


def _attn_fwd(q, kv, bias, sinks, nw, carry=None):
    s = q.shape[0]

    def body(q_ref, kvp_ref, kvc_ref, bias_ref, sinks_ref, nw_ref, y_ref):
        t = pl.program_id(0)
        kv3 = jnp.concatenate([kvp_ref[...], kvc_ref[...]], axis=0)
        for sub in range(NB):
            rows = slice(sub * BLK, (sub + 1) * BLK)
            groups = _attn_heads(jnp.logical_and(t == 0, sub == 0), q_ref[rows, :], kv3[sub * BLK:(sub + 2) * BLK],
                                 bias_ref, sinks_ref)
            o = _unstack_heads([g[5] for g in groups])
            r = lax.rsqrt(jnp.mean(o * o, axis=-1, keepdims=True) + EPS)
            y_ref[rows, :] = (o * r * nw_ref[...]).astype(BF16)

    return _pcall(
        body, [q, kv, kv, bias, sinks, nw], name="attn_fwd", grid=(s // (NB * BLK),),
        out_shape=[jax.ShapeDtypeStruct((s, QW), BF16)],
        in_specs=[pl.BlockSpec((NB * BLK, QW), lambda t: (t, 0)),
                  pl.BlockSpec((BLK, 2 * KVW), lambda t: (jnp.maximum(NB * t - 1, 0), 0)),
                  pl.BlockSpec((NB * BLK, 2 * KVW), lambda t: (t, 0)),
                  _row((NH, BLK, 2 * BLK)),
                  pl.BlockSpec(memory_space=pltpu.SMEM),
                  _row((1, QW))],
        out_specs=[pl.BlockSpec((NB * BLK, QW), lambda t: (t, 0))], carry=carry)


def _attn_bwd(q, kv, dya, bias, sinks, nw, act, dx2, gate2, w_dn, carry=None):
    s = q.shape[0]
    nt = s // (NB * BLK)
    npiece = DFF // NB

    def body(q_ref, kvp_ref, kvc_ref, dy_ref, bias_ref, sinks_ref, nw_ref, act_ref, dx2_ref, g2_ref, wdn_ref,
             dq_ref, dkv_ref, dbias_ref, dsink_ref, dnw_ref, gdn_hbm, dg2_ref, carry_ref, held_ref, acc_ref, sem):
        t = pl.program_id(0)

        @pl.when(t == 0)
        def _():
            carry_ref[...] = jnp.zeros_like(carry_ref)
            held_ref[...] = jnp.zeros_like(held_ref)
            dbias_ref[...] = jnp.zeros_like(dbias_ref)
            dsink_ref[...] = jnp.zeros_like(dsink_ref)
            dnw_ref[...] = jnp.zeros_like(dnw_ref)
            acc_ref[...] = jnp.zeros_like(acc_ref)

        def wgrad_piece(sub):
            rows = slice(sub * npiece, (sub + 1) * npiece)
            acc_ref[rows, :] += _mm_tn(act_ref[:, rows], dx2_ref[...].astype(BF16))

        def block(sub, kv3):
            rows = slice(sub * BLK, (sub + 1) * BLK)
            groups = _attn_heads(jnp.logical_and(t == 0, sub == 0), q_ref[rows, :], kv3[sub * BLK:(sub + 2) * BLK],
                                 bias_ref, sinks_ref)
            o = _unstack_heads([g[5] for g in groups])
            r = lax.rsqrt(jnp.mean(o * o, axis=-1, keepdims=True) + EPS)
            dy = dy_ref[rows, :]
            on = o * r
            dnw_ref[0:1, :] += jnp.sum(dy * on, axis=0, keepdims=True)
            u = dy * nw_ref[...]
            do = r * u - on * (r * jnp.mean(u * on, axis=-1, keepdims=True))
            dq_parts, dk_parts, dv_parts = [], [], []
            for g, (qs, kw, vw, pn, ps, og) in enumerate(groups):
                dos = jnp.concatenate([do[:, (4 * g + r_) * HD:(4 * g + r_ + 1) * HD] for r_ in range(4)], axis=0)
                delta = jnp.sum(dos * og, axis=-1, keepdims=True)
                dp = _mm_nt(dos.astype(BF16), vw)
                ds = pn * (dp - delta)
                dsk = ps * delta
                lane = lax.broadcasted_iota(jnp.int32, (1, 128), 1)
                for r_ in range(4):
                    h = 4 * g + r_
                    dbias_ref[h] += ds[r_ * BLK:(r_ + 1) * BLK]
                    dsink_ref[0:1, :] -= jnp.where(lane == h, jnp.sum(dsk[r_ * BLK:(r_ + 1) * BLK]), 0.0)
                dsb = ds.astype(BF16)
                dq_parts.append(_mm(dsb, kw) * 0.125)
                dk_parts.append(_mm_tn(dsb, qs))
                dv_parts.append(_mm_tn(pn.astype(BF16), dos.astype(BF16)))
            dq_ref[rows, :] = _unstack_heads(dq_parts).astype(BF16)
            return jnp.concatenate(dk_parts + dv_parts, axis=1)

        @pl.when(t < nt)
        def _():
            kv3 = jnp.concatenate([kvp_ref[...], kvc_ref[...]], axis=0)
            tail = carry_ref[...]
            for sub in range(NB):
                d = block(sub, kv3)
                done = tail + d[0:BLK]
                if sub == 0:
                    dkv_ref[0:(NB - 1) * BLK, :] = held_ref[...].astype(BF16)
                    dkv_ref[(NB - 1) * BLK:NB * BLK, :] = done.astype(BF16)
                else:
                    held_ref[(sub - 1) * BLK:sub * BLK, :] = done
                tail = d[BLK:2 * BLK]
                wgrad_piece(sub)
            carry_ref[...] = tail

        @pl.when(t == nt)
        def _():
            dkv_ref[0:(NB - 1) * BLK, :] = held_ref[...].astype(BF16)
            dkv_ref[(NB - 1) * BLK:NB * BLK, :] = carry_ref[...].astype(BF16)
            acc = acc_ref[...]
            dg2_ref[...] = jnp.zeros_like(dg2_ref)
            dg2_ref[0:1, :] = jnp.sum(acc * wdn_ref[...].astype(F32), axis=0, keepdims=True)
            acc_ref[...] = acc * g2_ref[...]
            cp = pltpu.make_async_copy(acc_ref, gdn_hbm, sem)
            cp.start()
            cp.wait()

    last = nt - 1
    tile = lambda w: pl.BlockSpec((NB * BLK, w), lambda t: (jnp.minimum(t, last), 0))
    return _pcall(
        body, [q, kv, kv, dya, bias, sinks, nw, act, dx2, gate2, w_dn], name="attn_bwd", grid=(nt + 1,),
        out_shape=[jax.ShapeDtypeStruct((s, QW), BF16), jax.ShapeDtypeStruct((s, 2 * KVW), BF16),
                   jax.ShapeDtypeStruct((NH, BLK, 2 * BLK), F32), jax.ShapeDtypeStruct((NH, 128), F32),
                   jax.ShapeDtypeStruct((8, QW), F32), jax.ShapeDtypeStruct((DFF, D), F32),
                   jax.ShapeDtypeStruct((8, D), F32)],
        in_specs=[tile(QW),
                  pl.BlockSpec((BLK, 2 * KVW), lambda t: (jnp.clip(NB * t - 1, 0, NB * nt - 1), 0)),
                  tile(2 * KVW), tile(QW),
                  _row((NH, BLK, 2 * BLK)),
                  pl.BlockSpec(memory_space=pltpu.SMEM),
                  _row((1, QW)), tile(DFF), tile(D), _row((1, D)), _row((DFF, D))],
        out_specs=[tile(QW),
                   pl.BlockSpec((NB * BLK, 2 * KVW), lambda t: (jnp.maximum(t - 1, 0), 0)),
                   _row((NH, BLK, 2 * BLK)), _row((NH, 128)), _row((8, QW)),
                   pl.BlockSpec(memory_space=pl.ANY), _row((8, D))],
        scratch_shapes=[pltpu.VMEM((BLK, 2 * KVW), F32), pltpu.VMEM(((NB - 1) * BLK, 2 * KVW), F32),
                        pltpu.VMEM((DFF, D), F32), pltpu.SemaphoreType.DMA], carry=carry)


def _rel_bias_grad(dbias, bucket):
    def body(db_ref, bk_ref, o_ref):
        bk = bk_ref[...]
        lane = lax.broadcasted_iota(jnp.int32, (1, 128), 1)
        for b in range(NBUCKET):
            sel = bk == b
            row = jnp.zeros((1, 128), F32)
            for h in range(NH):
                row = row + jnp.where(lane == h, jnp.sum(jnp.where(sel, db_ref[h], 0.0)), 0.0)
            o_ref[b:b + 1, :] = row

    return pl.pallas_call(
        body, name="rel_bias_grad",
        out_shape=jax.ShapeDtypeStruct((NBUCKET, 128), F32),
    )(dbias, bucket)


def _ssd_consts():
    head_of_lane = np.arange(SW) // HD
    expand = (np.arange(128)[:, None] == head_of_lane[None, :]).astype(np.float32)
    tril = np.tril(np.ones((BLK, BLK), np.float32))
    return (jnp.asarray(expand, BF16), jnp.asarray(expand.T.copy(), BF16), jnp.asarray(tril, BF16),
            jnp.asarray(tril.T.copy(), BF16))


def _conv_pre(xc, halo, cw, cb):
    ext = jnp.concatenate([halo, xc], axis=0)
    taps = [xc if k == CK - 1 else pltpu.roll(ext, CK - 1 - k, 0)[8:8 + BLK] for k in range(CK)]
    return cb + sum(cw[k:k + 1, :] * taps[k] for k in range(CK))


def _ssd_chunk(pre, dtr, dtb, av, dkv, ex, tril, h_in):
    sp = _sig(pre)
    xbc = pre * sp
    xs, bm, cm = xbc[:, 0:SW], xbc[:, SW:SW + 2 * NST], xbc[:, SW + 2 * NST:]
    dtin = dtr + dtb
    dt = jnp.maximum(dtin, 0.0) + jnp.log1p(jnp.exp(-jnp.abs(dtin)))
    cs = _sel_l(tril, dt * av)
    cst = cs.T
    dtx = _sel_r(dt, ex)
    csx = _sel_r(cs, ex)
    xdt = xs * dtx
    csl = csx[BLK - 1:BLK, :]
    decx = jnp.exp(csl - csx)
    ecsx = jnp.exp(csx)
    ecl = jnp.exp(csl)
    causal = tril.astype(F32) > 0.5
    ydiag, yoff, cbs, lms = [], [], [], []
    for g in range(2):
        bg = bm[:, g * NST:(g + 1) * NST].astype(BF16)
        cg = cm[:, g * NST:(g + 1) * NST].astype(BF16)
        cb = _mm_nt(cg, bg)
        cbs.append(cb)
        yoff.append(_mm(cg, h_in[:, g * 256:(g + 1) * 256].astype(BF16)))
        for r in range(4):
            h = 4 * g + r
            seg = cs[:, h:h + 1] - cst[h:h + 1, :]
            lm = jnp.where(causal, jnp.exp(jnp.minimum(seg, 0.0)), 0.0)
            lms.append(lm)
            ydiag.append(_mm((cb * lm).astype(BF16), xdt[:, h * HD:(h + 1) * HD].astype(BF16)))
    yoff = jnp.concatenate(yoff, axis=1) * ecsx
    y = jnp.concatenate(ydiag, axis=1) + yoff + dkv * xs
    return dict(pre=pre, sp=sp, xs=xs, bm=bm, cm=cm, dtin=dtin, dt=dt, av=av, cs=cs, cst=cst,
                dtx=dtx, csx=csx, xdt=xdt, decx=decx, ecsx=ecsx, ecl=ecl, causal=causal, cbs=cbs, lms=lms,
                yoff=yoff, y=y)


def _group_mean(t):
    m0 = jnp.mean(t[:, 0:256], axis=-1, keepdims=True)
    m1 = jnp.mean(t[:, 256:512], axis=-1, keepdims=True)
    return jnp.concatenate([jnp.broadcast_to(m0, (t.shape[0], 256)), jnp.broadcast_to(m1, (t.shape[0], 256))], axis=1)


SUBS = 4


def _ssd_fwd(z, xbc, dtr, cw, cb, dtb, av, dk, nw, carry=None):
    s = z.shape[0]
    nc = s // BLK
    tile = SUBS * BLK
    ex, _, tril, _ = _ssd_consts()

    def body(z_ref, xc_ref, xh_ref, dtr_ref, cw_ref, cb_ref, dtb_ref, a_ref, dk_ref, nw_ref, ex_ref, tril_ref,
             y_ref, hs_ref, pre_ref, h_ref):
        t = pl.program_id(0)

        @pl.when(t == 0)
        def _():
            h_ref[...] = jnp.zeros_like(h_ref)

        h_in = h_ref[...]
        for sub in range(SUBS):
            rows = slice(sub * BLK, (sub + 1) * BLK)
            xc = xc_ref[rows, :]
            halo = jnp.where(t == 0, 0.0, xh_ref[...]) if sub == 0 else xc_ref[sub * BLK - 8:sub * BLK, :]
            pre = _conv_pre(xc, halo, cw_ref[...], cb_ref[...])
            pre_ref[rows, :] = pre
            hs_ref[sub] = h_in
            f = _ssd_chunk(pre, dtr_ref[rows, :], dtb_ref[...], a_ref[...], dk_ref[...], ex_ref[...], tril_ref[...], h_in)
            dx = (f["decx"] * f["xdt"]).astype(BF16)
            st = [_mm_tn(f["bm"][:, g * NST:(g + 1) * NST].astype(BF16), dx[:, g * 256:(g + 1) * 256]) for g in range(2)]
            h_in = h_in * f["ecl"] + jnp.concatenate(st, axis=1)
            zv = z_ref[rows, :]
            tg = f["y"] * (zv * _sig(zv))
            r = lax.rsqrt(_group_mean(tg * tg) + EPS)
            y_ref[rows, :] = (tg * r * nw_ref[...]).astype(BF16)
        h_ref[...] = h_in

    cur = lambda w: pl.BlockSpec((tile, w), lambda t: (t, 0))
    return _pcall(
        body, [z, xbc, xbc, dtr, cw, cb, dtb, av, dk, nw, ex, tril], name="ssd_fwd", grid=(s // tile,),
        out_shape=[jax.ShapeDtypeStruct((s, SW), BF16), jax.ShapeDtypeStruct((nc, NST, SW), F32),
                   jax.ShapeDtypeStruct((s, XBCW), F32)],
        in_specs=[cur(SW), cur(XBCW), pl.BlockSpec((8, XBCW), lambda t: (jnp.maximum(t * (tile // 8) - 1, 0), 0)),
                  cur(128), _row((8, XBCW)), _row((1, XBCW)), _row((1, 128)),
                  _row((1, 128)), _row((1, SW)), _row((1, SW)), _row((128, SW)), _row((BLK, BLK))],
        out_specs=[cur(SW), pl.BlockSpec((SUBS, NST, SW), lambda t: (t, 0, 0)), cur(XBCW)],
        scratch_shapes=[pltpu.VMEM((NST, SW), F32)], carry=carry)


def _ssd_bwd(z, xbc, pre_all, dtr, dys, hs, cw, dtb, av, dk, nw, carry=None):
    s = z.shape[0]
    tile = SUBS * BLK
    nt = s // tile
    ex, ext_t, tril, triu = _ssd_consts()

    def body(z_ref, xc_ref, pre_ref, dtr_ref, dy_ref, hs_ref, cw_ref, dtb_ref, a_ref, dk_ref, nw_ref,
             ex_ref, ext_ref, tril_ref, triu_ref,
             dz_ref, dxbc_ref, ddt_ref, dcw_ref, dcb_ref, dnw_ref, dhd_ref, dh_ref, nxt_ref, dd_ref):
        i = pl.program_id(0)

        @pl.when(i == 0)
        def _():
            dh_ref[...] = jnp.zeros_like(dh_ref)
            nxt_ref[...] = jnp.zeros_like(nxt_ref)
            dd_ref[...] = jnp.zeros_like(dd_ref)
            dcw_ref[...] = jnp.zeros_like(dcw_ref)
            dcb_ref[...] = jnp.zeros_like(dcb_ref)
            dnw_ref[...] = jnp.zeros_like(dnw_ref)
            dhd_ref[...] = jnp.zeros_like(dhd_ref)

        gst, nxt = dh_ref[...], nxt_ref[...]
        for sub in reversed(range(SUBS)):
            rows = slice(sub * BLK, (sub + 1) * BLK)
            gst, nxt = chunk(sub, rows, gst, nxt, z_ref, xc_ref, pre_ref, dtr_ref, dy_ref, hs_ref, cw_ref, dtb_ref,
                             a_ref, dk_ref, nw_ref, ex_ref, ext_ref, tril_ref, triu_ref,
                             dz_ref, dxbc_ref, ddt_ref, dcw_ref, dcb_ref, dnw_ref, dhd_ref, dd_ref)
        dh_ref[...] = gst
        nxt_ref[...] = nxt

        @pl.when(i == nt - 1)
        def _():
            dhd_ref[2:3, :] = _sel_r(dd_ref[...], ext_ref[...])[0:1, :]

    def chunk(sub, rows, gst, nxt, z_ref, xc_ref, pre_ref, dtr_ref, dy_ref, hs_ref, cw_ref, dtb_ref,
              a_ref, dk_ref, nw_ref, ex_ref, ext_ref, tril_ref, triu_ref,
              dz_ref, dxbc_ref, ddt_ref, dcw_ref, dcb_ref, dnw_ref, dhd_ref, dd_ref):
        h_in = hs_ref[sub]
        f = _ssd_chunk(pre_ref[rows, :], dtr_ref[rows, :], dtb_ref[...], a_ref[...], dk_ref[...], ex_ref[...],
                       tril_ref[...], h_in)
        xs, xdt, decx, ecsx, ecl, dtx = f["xs"], f["xdt"], f["decx"], f["ecsx"], f["ecl"], f["dtx"]
        cs, cst, causal = f["cs"], f["cst"], f["causal"]
        causal_t = triu_ref[...].astype(F32) > 0.5

        zv = z_ref[rows, :]
        sz = _sig(zv)
        gz = zv * sz
        t = f["y"] * gz
        r = lax.rsqrt(_group_mean(t * t) + EPS)
        tn_ = t * r
        dyn = dy_ref[rows, :]
        dnw_ref[0:1, :] += jnp.sum(dyn * tn_, axis=0, keepdims=True)
        u = dyn * nw_ref[...]
        dt_ = r * u - tn_ * (r * _group_mean(u * tn_))
        dy = dt_ * gz
        dz_ref[rows, :] = (dt_ * f["y"] * (sz * (1.0 + zv * (1.0 - sz)))).astype(BF16)

        dd_ref[0:1, :] += jnp.sum(dy * xs, axis=0, keepdims=True)
        dxs = dk_ref[...] * dy

        edy = ecsx * dy
        dxdt, dbs, dcs_, dcsx_parts, dh_new = [], [], [], [], []
        lane = lax.broadcasted_iota(jnp.int32, (1, 128), 1)
        dcs_intra = jnp.zeros((BLK, 128), F32)
        for g in range(2):
            sl = slice(g * 256, (g + 1) * 256)
            bgf, cgf = f["bm"][:, g * NST:(g + 1) * NST], f["cm"][:, g * NST:(g + 1) * NST]
            bg, cg = bgf.astype(BF16), cgf.astype(BF16)
            gg = gst[:, sl].astype(BF16)
            hg = h_in[:, sl].astype(BF16)
            edyg = edy[:, sl].astype(BF16)
            dc = _mm_nt(edyg, hg)
            dh_new.append(gst[:, sl] * ecl[:, sl] + _mm_tn(cg, edyg))
            bgm = _mm(bg, gg)
            dxdt_g = decx[:, sl] * bgm
            dxg = (decx[:, sl] * xdt[:, sl]).astype(BF16)
            db = _mm_nt(dxg, gg)
            qd = bgm * xdt[:, sl] * decx[:, sl]
            last = jnp.sum(qd, axis=0, keepdims=True) + ecl[:, sl] * jnp.sum(gst[:, sl] * h_in[:, sl], axis=0, keepdims=True)
            rowid = lax.broadcasted_iota(jnp.int32, (BLK, 256), 0)
            dcsx_parts.append(f["yoff"][:, sl] * dy[:, sl] - qd + jnp.where(rowid == BLK - 1, last, 0.0))
            cb_ = f["cbs"][g]
            cbt = _mm_nt(bg, cg)
            dcb_ = jnp.zeros((BLK, BLK), F32)
            dcbt = jnp.zeros((BLK, BLK), F32)
            dxd = []
            for r_ in range(4):
                h = 4 * g + r_
                hl = slice(h * HD, (h + 1) * HD)
                lm = f["lms"][h]
                segt = cst[h:h + 1, :] - cs[:, h:h + 1]
                lmt = jnp.where(causal_t, jnp.exp(jnp.minimum(segt, 0.0)), 0.0)
                dyh = dy[:, hl].astype(BF16)
                xdh = xdt[:, hl].astype(BF16)
                dw = _mm_nt(dyh, xdh)
                dwt = _mm_nt(xdh, dyh)
                wt = cbt * lmt
                dxd.append(_mm(wt.astype(BF16), dyh))
                dcb_ = dcb_ + dw * lm
                dcbt = dcbt + dwt * lmt
                col = jnp.sum(dw * (cb_ * lm), axis=-1, keepdims=True) - jnp.sum(dwt * wt, axis=-1, keepdims=True)
                dcs_intra = dcs_intra + jnp.where(lane == h, col, 0.0)
            dxdt.append(dxdt_g + jnp.concatenate(dxd, axis=1))
            dcs_.append(dc + _mm(dcb_.astype(BF16), bg))
            dbs.append(db + _mm(dcbt.astype(BF16), cg))
        dxdt = jnp.concatenate(dxdt, axis=1)
        dxs = dxs + dxdt * dtx
        ext_t_ = ext_ref[...]
        dcs = dcs_intra + _sel_r(jnp.concatenate(dcsx_parts, axis=1), ext_t_)
        da = _sel_l(triu_ref[...], dcs)
        ddt = da * f["av"] + _sel_r(dxdt * xs, ext_t_)
        dhd_ref[1:2, :] += jnp.sum(da * f["dt"], axis=0, keepdims=True)
        ddtr = ddt * _sig(f["dtin"])
        dhd_ref[0:1, :] += jnp.sum(ddtr, axis=0, keepdims=True)
        ddt_ref[rows, :] = ddtr.astype(BF16)

        sp, pre = f["sp"], f["pre"]
        dact = jnp.concatenate([dxs] + dbs + dcs_, axis=1)
        dpre = dact * (sp * (1.0 + pre * (1.0 - sp)))
        dcb_ref[0:1, :] += jnp.sum(dpre, axis=0, keepdims=True)
        ext2 = jnp.concatenate([dpre, nxt], axis=0)
        shifted = [pltpu.roll(ext2, BLK + 8 - (CK - 1 - k), 0)[0:BLK] for k in range(CK - 1)] + [dpre]
        cw = cw_ref[...]
        xc = xc_ref[rows, :]
        dxr = cw[CK - 1:CK, :] * dpre
        for k in range(CK):
            dcw_ref[k:k + 1, :] += jnp.sum(shifted[k] * xc, axis=0, keepdims=True)
            if k < CK - 1:
                dxr = dxr + cw[k:k + 1, :] * shifted[k]
        dxbc_ref[rows, :] = dxr.astype(BF16)
        return jnp.concatenate(dh_new, axis=1), dpre[0:8]

    cur = lambda w: pl.BlockSpec((tile, w), lambda i: (nt - 1 - i, 0))
    return _pcall(
        body, [z, xbc, pre_all, dtr, dys, hs, cw, dtb, av, dk, nw, ex, ext_t, tril, triu], name="ssd_bwd", grid=(nt,),
        out_shape=[jax.ShapeDtypeStruct((s, SW), BF16), jax.ShapeDtypeStruct((s, XBCW), BF16),
                   jax.ShapeDtypeStruct((s, 128), BF16), jax.ShapeDtypeStruct((8, XBCW), F32),
                   jax.ShapeDtypeStruct((8, XBCW), F32), jax.ShapeDtypeStruct((8, SW), F32),
                   jax.ShapeDtypeStruct((8, 128), F32)],
        in_specs=[cur(SW), cur(XBCW), cur(XBCW), cur(128), cur(SW),
                  pl.BlockSpec((SUBS, NST, SW), lambda i: (nt - 1 - i, 0, 0)),
                  _row((8, XBCW)), _row((1, 128)), _row((1, 128)), _row((1, SW)), _row((1, SW)),
                  _row((128, SW)), _row((SW, 128)), _row((BLK, BLK)), _row((BLK, BLK))],
        out_specs=[cur(SW), cur(XBCW), cur(128), _row((8, XBCW)), _row((8, XBCW)), _row((8, SW)), _row((8, 128))],
        scratch_shapes=[pltpu.VMEM((NST, SW), F32), pltpu.VMEM((8, XBCW), F32), pltpu.VMEM((8, SW), F32)], carry=carry)


def _load_once(i, pairs, sem):
    @pl.when(i == 0)
    def _():
        cps = [pltpu.make_async_copy(src, dst, sem.at[k]) for k, (src, dst) in enumerate(pairs)]
        for cp in cps:
            cp.start()
        for cp in cps:
            cp.wait()


def _mlp_fwd(x, ya, ys, tgt, w_o, w_ga, w_gb, w_dn, gate1, a2, sh2, gate2, fn):
    s = x.shape[0]
    sub_m, subs = 256, 2
    tm = sub_m * subs

    def body(x_ref, ya_ref, ys_ref, t_ref, wo_hbm, wga_hbm, wgb_hbm, wdn_hbm, g1_ref, a2_ref, s2_ref, g2_ref, fn_ref,
             x1_ref, gu_ref, dx2_ref, loss_ref, dfn_ref, wo, wga, wgb, wdn, sem):
        i = pl.program_id(0)
        _load_once(i, [(wo_hbm, wo), (wga_hbm, wga), (wgb_hbm, wgb), (wdn_hbm, wdn)], sem)

        @pl.when(i == 0)
        def _():
            loss_ref[...] = jnp.zeros_like(loss_ref)
            dfn_ref[...] = jnp.zeros_like(dfn_ref)

        def proj(st):
            st["mix"] = _mm(ya_ref[st["rows"], :], wo[0:QW, :]) + _mm(ys_ref[st["rows"], :], wo[QW:D, :])

        def norm(st):
            x1 = x_ref[st["rows"], :] + g1_ref[...] * st.pop("mix")
            x1_ref[st["rows"], :] = x1
            r2 = lax.rsqrt(jnp.mean(x1 * x1, axis=-1, keepdims=True) + EPS)
            st["x1"] = x1
            st["h2"] = (x1 * r2 * a2_ref[...] + s2_ref[...]).astype(BF16)

        def gate_up(st):
            h2 = st.pop("h2")
            ha, hb = h2[:, 0:D // 2], h2[:, D // 2:D]
            gub = jnp.concatenate([(_mm(ha, wga[j]) + _mm(hb, wgb[j])).astype(BF16) for j in range(4)], axis=1)
            gu_ref[st["rows"], :] = gub
            st["gub"] = gub

        def activate(st):
            gub = st.pop("gub")
            gv, uv = gub[:, 0:DFF].astype(F32), gub[:, DFF:].astype(F32)
            st["act"] = (gv * _sig(gv) * uv).astype(BF16)

        def down(st):
            st["ff"] = _mm(st.pop("act"), wdn[...])

        def head(st):
            x2 = st.pop("x1") + g2_ref[...] * st.pop("ff")
            r3 = lax.rsqrt(jnp.mean(x2 * x2, axis=-1, keepdims=True) + EPS)
            xn = x2 * r3
            fnv = fn_ref[...]
            err = xn * fnv - t_ref[st["rows"], :]
            st["loss"] = jnp.sum(err * err) * (0.5 / D)
            dy = err * (1.0 / D)
            st["dfn"] = jnp.sum(dy * xn, axis=0, keepdims=True)
            u = dy * fnv
            dx2_ref[st["rows"], :] = r3 * u - xn * (r3 * jnp.mean(u * xn, axis=-1, keepdims=True))

        a, b = [dict(rows=slice(k * sub_m, (k + 1) * sub_m)) for k in range(subs)]
        for stage, st in [(proj, a), (norm, a), (proj, b), (gate_up, a), (norm, b), (activate, a), (gate_up, b),
                          (down, a), (activate, b), (head, a), (down, b), (head, b)]:
            stage(st)
        loss_ref[...] += a["loss"] + b["loss"]
        dfn_ref[0:1, :] += a["dfn"] + b["dfn"]

    def tok(w):
        return pl.BlockSpec((tm, w), lambda i: (i, 0))

    hbm = pl.BlockSpec(memory_space=pl.ANY)
    return pl.pallas_call(
        body, name="mlp_fwd", grid=(s // tm,),
        out_shape=[jax.ShapeDtypeStruct((s, D), F32), jax.ShapeDtypeStruct((s, 2 * DFF), BF16),
                   jax.ShapeDtypeStruct((s, D), F32), jax.ShapeDtypeStruct((8, 128), F32),
                   jax.ShapeDtypeStruct((8, D), F32)],
        in_specs=[tok(D), tok(QW), tok(SW), tok(D), hbm, hbm, hbm, hbm,
                  _row((1, D)), _row((1, D)), _row((1, D)), _row((1, D)), _row((1, D))],
        out_specs=[tok(D), tok(2 * DFF), tok(D), _row((8, 128)), _row((8, D))],
        scratch_shapes=[pltpu.VMEM((D, D), BF16), pltpu.VMEM(w_ga.shape, BF16), pltpu.VMEM(w_gb.shape, BF16),
                        pltpu.VMEM((DFF, D), BF16), pltpu.SemaphoreType.DMA((4,))],
        compiler_params=_cp(("arbitrary",)),
    )(x, ya, ys, tgt, w_o, w_ga, w_gb, w_dn, gate1, a2, sh2, gate2, fn)


def _mlp_bwd(x1, gu, dx2, w_o, w_ga, w_gb, w_dn, gate1, a2, sh2, gate2):
    s = x1.shape[0]
    tm = 256
    nj = 2 * DFF // 4

    def body(x1_ref, gu_ref, dx2_ref, wo_hbm, wga_hbm, wgb_hbm, wdn_hbm, g1_ref, a2_ref, s2_ref, g2_ref,
             dx1_ref, dya_ref, dys_ref, act_ref, dgu_ref, h2_ref, dsh_ref, p_ref, wo, wga, wgb, wdn, sem):
        i = pl.program_id(0)
        _load_once(i, [(wo_hbm, wo), (wga_hbm, wga), (wgb_hbm, wgb), (wdn_hbm, wdn)], sem)

        @pl.when(i == 0)
        def _():
            dsh_ref[...] = jnp.zeros_like(dsh_ref)
            p_ref[...] = jnp.zeros_like(p_ref)

        dx2 = dx2_ref[...]
        dact = _mm_nt((dx2 * g2_ref[...]).astype(BF16), wdn[...])
        gub = gu_ref[...]
        gv, uv = gub[:, 0:DFF].astype(F32), gub[:, DFF:].astype(F32)
        sg = _sig(gv)
        sl = gv * sg
        act_ref[...] = (sl * uv).astype(BF16)
        dgu = jnp.concatenate([dact * uv * (sg * (1.0 + gv * (1.0 - sg))), dact * sl], axis=1).astype(BF16)
        dgu_ref[...] = dgu
        dha = sum(_mm_nt(dgu[:, j * nj:(j + 1) * nj], wga[j]) for j in range(4))
        dhb = sum(_mm_nt(dgu[:, j * nj:(j + 1) * nj], wgb[j]) for j in range(4))
        dh = jnp.concatenate([dha, dhb], axis=1)
        x1 = x1_ref[...]
        r2 = lax.rsqrt(jnp.mean(x1 * x1, axis=-1, keepdims=True) + EPS)
        xn = x1 * r2
        a2 = a2_ref[...]
        h2_ref[...] = (xn * a2 + s2_ref[...]).astype(BF16)
        dsh_ref[0:1, :] += jnp.sum(dh, axis=0, keepdims=True)
        p_ref[0:1, :] += jnp.sum(dh * xn, axis=0, keepdims=True)
        u = dh * a2
        dx1 = dx2 + r2 * u - xn * (r2 * jnp.mean(u * xn, axis=-1, keepdims=True))
        dx1_ref[...] = dx1
        dcat = _mm_nt((dx1 * g1_ref[...]).astype(BF16), wo[...])
        dya_ref[...] = dcat[:, 0:QW]
        dys_ref[...] = dcat[:, QW:D]

    def tok(w):
        return pl.BlockSpec((tm, w), lambda i: (i, 0))

    hbm = pl.BlockSpec(memory_space=pl.ANY)
    return pl.pallas_call(
        body, name="mlp_bwd", grid=(s // tm,),
        out_shape=[jax.ShapeDtypeStruct((s, D), F32), jax.ShapeDtypeStruct((s, QW), F32),
                   jax.ShapeDtypeStruct((s, SW), F32), jax.ShapeDtypeStruct((s, DFF), BF16),
                   jax.ShapeDtypeStruct((s, 2 * DFF), BF16), jax.ShapeDtypeStruct((s, D), BF16),
                   jax.ShapeDtypeStruct((8, D), F32), jax.ShapeDtypeStruct((8, D), F32)],
        in_specs=[tok(D), tok(2 * DFF), tok(D), hbm, hbm, hbm, hbm, _row((1, D)), _row((1, D)), _row((1, D)), _row((1, D))],
        out_specs=[tok(D), tok(QW), tok(SW), tok(DFF), tok(2 * DFF), tok(D), _row((8, D)), _row((8, D))],
        scratch_shapes=[pltpu.VMEM((D, D), BF16), pltpu.VMEM(w_ga.shape, BF16), pltpu.VMEM(w_gb.shape, BF16),
                        pltpu.VMEM((DFF, D), BF16), pltpu.SemaphoreType.DMA((4,))],
        compiler_params=_cp(("arbitrary",)),
    )(x1, gu, dx2, w_o, w_ga, w_gb, w_dn, gate1, a2, sh2, gate2)


def _wgrad(name, a, b, gate, w, carry=None):
    s, m = a.shape
    n = b.shape[1]
    tk = min(1024, s)
    nk = s // tk

    def body(a_ref, b_ref, g_ref, w_ref, o_hbm, dg_ref, acc_ref, sem):
        k = pl.program_id(0)

        @pl.when(k == 0)
        def _():
            acc_ref[...] = jnp.zeros_like(acc_ref)

        acc_ref[...] += _mm_tn(a_ref[...], b_ref[...].astype(BF16))

        @pl.when(k == nk - 1)
        def _():
            acc = acc_ref[...]
            dg_ref[...] = jnp.zeros_like(dg_ref)
            dg_ref[0:1, :] = jnp.sum(acc * w_ref[...].astype(F32), axis=0, keepdims=True)
            acc_ref[...] = acc * g_ref[...]
            cp = pltpu.make_async_copy(acc_ref, o_hbm, sem)
            cp.start()
            cp.wait()

    return _pcall(body, [a, b, gate, w], name=name, grid=(nk,),
                  out_shape=[jax.ShapeDtypeStruct((m, n), F32), jax.ShapeDtypeStruct((8, n), F32)],
                  in_specs=[pl.BlockSpec((tk, m), lambda k: (k, 0)), pl.BlockSpec((tk, n), lambda k: (k, 0)),
                            _row((1, n)), _row((m, n))],
                  out_specs=[pl.BlockSpec(memory_space=pl.ANY), _row((8, n))],
                  scratch_shapes=[pltpu.VMEM((m, n), F32), pltpu.SemaphoreType.DMA], carry=carry)


def _wgrad_gate_up(h2, dgu, carry=None):
    s = h2.shape[0]
    tk = min(1024, s)
    nk = s // tk
    n = dgu.shape[1]
    nj = n // 4

    def body(a_ref, b_ref, o_hbm, acc_ref, sems):
        k = pl.program_id(0)

        @pl.when(k == 0)
        def _():
            acc_ref[...] = jnp.zeros_like(acc_ref)

        acc_ref[...] += _mm_tn(a_ref[...], b_ref[...])

        @pl.when(k == nk - 1)
        def _():
            cps = [pltpu.make_async_copy(acc_ref.at[:, pl.ds(j * nj, nj)], o_hbm.at[j], sems.at[j]) for j in range(4)]
            for cp in cps:
                cp.start()
            for cp in cps:
                cp.wait()

    return _pcall(body, [h2, dgu], name="wgrad_gate_up", grid=(nk,),
                  out_shape=[jax.ShapeDtypeStruct((4, D, nj), F32)],
                  in_specs=[pl.BlockSpec((tk, D), lambda k: (k, 0)), pl.BlockSpec((tk, n), lambda k: (k, 0))],
                  out_specs=[pl.BlockSpec(memory_space=pl.ANY)],
                  scratch_shapes=[pltpu.VMEM((D, n), F32), pltpu.SemaphoreType.DMA((4,))], carry=carry)


def _wgrad_in_t(h1, pieces, carry=None):
    s = h1.shape[0]
    rows = IN_W // 4
    tk = min(1024, s)
    nk = s // tk

    def body(a_ref, dq_ref, dkv_ref, dz_ref, dxbc_ref, ddt_ref, o_hbm, acc_ref, tr_ref, sl_ref, sem):
        k = pl.program_id(0)

        @pl.when(k == 0)
        def _():
            acc_ref[...] = jnp.zeros_like(acc_ref)

        dproj = jnp.concatenate([dq_ref[...], dkv_ref[...], dz_ref[...], dxbc_ref[...], ddt_ref[...]], axis=1)
        acc_ref[...] += _mm_tn(a_ref[...], dproj)

        @pl.when(k == nk - 1)
        def _():
            for j in range(PROJ_W // 128):
                tr_ref[j * 128:(j + 1) * 128, :] = acc_ref[:, j * 128:(j + 1) * 128].T
            for j in range(4):
                sl_ref[j] = tr_ref[j * rows:(j + 1) * rows, :]
            cp = pltpu.make_async_copy(sl_ref, o_hbm, sem)
            cp.start()
            cp.wait()

    return _pcall(body, [h1] + list(pieces), name="wgrad_in", grid=(nk,),
                  out_shape=[jax.ShapeDtypeStruct((4, rows, D), F32)],
                  in_specs=[pl.BlockSpec((tk, p.shape[1]), lambda k: (k, 0)) for p in [h1] + list(pieces)],
                  out_specs=[pl.BlockSpec(memory_space=pl.ANY)],
                  scratch_shapes=[pltpu.VMEM((D, PROJ_W), F32), pltpu.VMEM((PROJ_W, D), F32),
                                  pltpu.VMEM((4, rows, D), F32), pltpu.SemaphoreType.DMA],
                  carry=carry)


_SMALL = ["ada_b", "norm1", "conv_w", "conv_b", "dt_bias", "A_log", "D_skip", "sinks", "attn_out_norm",
          "ssm_out_norm", "norm2", "rel_bias", "final_norm"]


def _small_grad(name, gs, chip):
    if name == "ada_b":
        return jnp.concatenate([gs[j:j + 1, :] for j in range(6)], axis=1)
    if name == "conv_w":
        full = gs[7:11, :]
        out = full[:, 0:256]
        for j in range(1, 4):
            out = jnp.where(chip == j, full[:, j * 256:(j + 1) * 256], out)
        return out
    row, width = {"norm1": (6, D), "conv_b": (11, D), "norm2": (12, D), "final_norm": (13, D),
                  "attn_out_norm": (14, QW), "ssm_out_norm": (15, SW), "dt_bias": (16, NH), "A_log": (17, NH),
                  "D_skip": (18, NH), "sinks": (19, NH), "rel_bias": (24, NH)}[name]
    rows = NBUCKET if name == "rel_bias" else 1
    return gs[row:row + rows, 0:width]


def _small_update(small_all, where, ws, ms, vs):
    n = len(_SMALL)

    def body(where_ref, sa_ref, *refs):
        w_refs, m_refs, v_refs, outs = refs[:n], refs[n:2 * n], refs[2 * n:3 * n], refs[3 * n:]
        gs = sa_ref[0]
        for b in range(1, 8):
            gs = gs + sa_ref[b]
        chip = where_ref[1]
        for i, name in enumerate(_SMALL):
            g = _small_grad(name, gs, chip)
            lead = (0,) if name == "conv_w" else ()
            d, mo, vo = _adamw(w_refs[i][lead + (...,)], g, m_refs[i][lead + (...,)], v_refs[i][lead + (...,)])
            for k, val in enumerate((g, d, mo, vo)):
                outs[k * n + i][lead + (...,)] = val
        outs[4 * n][...] = gs[20:21, 0:128]

    shapes = [jax.ShapeDtypeStruct(w.shape, F32) for w in ws]
    vmem = pl.BlockSpec(memory_space=pltpu.VMEM)
    res = pl.pallas_call(
        body, name="small_update", out_shape=shapes * 4 + [jax.ShapeDtypeStruct((1, 128), F32)],
        in_specs=[pl.BlockSpec(memory_space=pltpu.SMEM)] + [vmem] * (1 + 3 * n), out_specs=[vmem] * (4 * n + 1),
    )(where, small_all, *ws, *ms, *vs)
    return [res[k * n:(k + 1) * n] for k in range(4)], res[4 * n][0, 0]


def _add_half(name, g, got, where, by_cols=False):
    rr, cc = got.shape[1:]
    if by_cols:
        mine = pl.BlockSpec((None, rr, cc), lambda i, w_ref: (i, 0, w_ref[0]))
    else:
        mine = pl.BlockSpec((None, None, rr, cc), lambda i, w_ref: (i, w_ref[0], 0, 0))

    def body(w_ref, g_ref, r_ref, o_ref, own_ref):
        s = g_ref[...] + r_ref[...]
        o_ref[...] = s.astype(BF16)

        @pl.when(pl.program_id(0) == w_ref[1])
        def _():
            own_ref[...] = s

    spec = pl.BlockSpec((None, rr, cc), lambda i, w_ref: (i, 0, 0))
    return _pcall(body, [where, g, got], name=name, grid=(4,), nprefetch=1,
                  out_shape=[jax.ShapeDtypeStruct(got.shape, BF16), jax.ShapeDtypeStruct((rr, cc), F32)],
                  in_specs=[mine, spec],
                  out_specs=[spec, pl.BlockSpec((rr, cc), lambda i, w_ref: (0, 0))])


def _add_chips(name, own, got):
    rr, cc = own.shape
    tr = rr // 2 if rr % 32 == 0 else rr

    def body(s_ref, r_ref, o_ref):
        o_ref[...] = ((s_ref[...] + r_ref[0].astype(F32)) + r_ref[1].astype(F32)) + r_ref[2].astype(F32)

    spec = pl.BlockSpec((tr, cc), lambda i: (i, 0))
    return _pcall(body, [own, got], name=name, grid=(rr // tr,), out_shape=[jax.ShapeDtypeStruct((rr, cc), F32)],
                  in_specs=[spec, pl.BlockSpec((3, tr, cc), lambda i: (0, i, 0))], out_specs=[spec])[0]


def _adamw_halves(name, mine, got, w, m, v, where, by_cols=False):
    rr, cc = mine.shape

    def body(w_ref_, t_ref, r_ref, w_ref, m_ref, v_ref, g_ref, d_ref, mo_ref, vo_ref):
        g = jnp.where(pl.program_id(0) == w_ref_[0], t_ref[...], r_ref[...])
        g_ref[...] = g
        d_ref[...], mo_ref[...], vo_ref[...] = _adamw(w_ref[...], g, m_ref[...], v_ref[...])

    if by_cols:
        grid = (2, 1)
        half = pl.BlockSpec((rr, cc), lambda h, i, w_ref_: (0, 0))
        full = pl.BlockSpec((rr, cc), lambda h, i, w_ref_: (0, h))
    else:
        tr = rr // 2
        grid = (2, 2)
        half = pl.BlockSpec((tr, cc), lambda h, i, w_ref_: (i, 0))
        full = pl.BlockSpec((None, tr, cc), lambda h, i, w_ref_: (0, 2 * h + i, 0))
    return _pcall(body, [where, mine, got, w, m, v], name=name, grid=grid, nprefetch=1,
                  out_shape=[jax.ShapeDtypeStruct(w.shape, F32)] * 4,
                  in_specs=[half, half, full, full, full], out_specs=[full] * 4)


def _bias_table(rel_bias, bucket, mask):
    def body(rb_ref, bk_ref, mk_ref, o_ref):
        bk = bk_ref[...]
        valid = mk_ref[...] > 0
        for h in range(NH):
            acc = jnp.zeros((BLK, 2 * BLK), F32)
            for b in range(NBUCKET):
                acc = jnp.where(bk == b, rb_ref[b, h], acc)
            o_ref[h] = jnp.where(valid, acc, NEG)

    vmem = pl.BlockSpec(memory_space=pltpu.VMEM)
    return pl.pallas_call(
        body, name="bias_table", out_shape=jax.ShapeDtypeStruct((NH, BLK, 2 * BLK), F32),
        in_specs=[pl.BlockSpec(memory_space=pltpu.SMEM), vmem, vmem], out_specs=vmem,
    )(rel_bias, bucket, mask)


def _pack_small(dsh1, p1, dsh2, p2, dg1a, dg1b, dg2, norm1, norm2, scale1, scale2, dcw, dcb, dfn,
                dnw_attn, dnw_ssm, dhd, av, dsink, drel, loss_acc):
    def body(dsh1_ref, p1_ref, dsh2_ref, p2_ref, dg1a_ref, dg1b_ref, dg2_ref, n1_ref, n2_ref, s1_ref, s2_ref,
             dcw_ref, dcb_ref, dfn_ref, da_ref, ds_ref, dhd_ref, av_ref, dsink_ref, drel_ref, loss_ref, o_ref):
        o_ref[...] = jnp.zeros_like(o_ref)
        p1v, p2v = p1_ref[0:1, :], p2_ref[0:1, :]
        o_ref[0:1, :] = dsh1_ref[0:1, :]
        o_ref[1:2, :] = p1v * n1_ref[...]
        o_ref[2:3, :] = dg1a_ref[0:1, :] + dg1b_ref[0:1, :]
        o_ref[3:4, :] = dsh2_ref[0:1, :]
        o_ref[4:5, :] = p2v * n2_ref[...]
        o_ref[5:6, :] = dg2_ref[0:1, :]
        o_ref[6:7, :] = p1v * (1.0 + s1_ref[...])
        o_ref[7:11, :] = dcw_ref[0:4, :]
        o_ref[11:12, :] = dcb_ref[0:1, :]
        o_ref[12:13, :] = p2v * (1.0 + s2_ref[...])
        o_ref[13:14, :] = dfn_ref[0:1, :]
        o_ref[14:15, 0:QW] = da_ref[0:1, :]
        o_ref[15:16, 0:SW] = ds_ref[0:1, :]
        o_ref[16:17, 0:128] = dhd_ref[0:1, :]
        o_ref[17:18, 0:128] = dhd_ref[1:2, :] * av_ref[...]
        o_ref[18:19, 0:128] = dhd_ref[2:3, :]
        o_ref[19:20, 0:128] = dsink_ref[0:1, :]
        o_ref[20:21, 0:128] = loss_ref[0:1, :]
        o_ref[24:56, 0:128] = drel_ref[...]

    return pl.pallas_call(body, name="pack_small", out_shape=jax.ShapeDtypeStruct((56, D), F32))(
        dsh1, p1, dsh2, p2, dg1a, dg1b, dg2, norm1, norm2, scale1, scale2, dcw, dcb, dfn,
        dnw_attn, dnw_ssm, dhd, av, dsink, drel, loss_acc)


def _pad_row(a, rows=1):
    return jnp.pad(a.reshape(rows, -1), ((0, 0), (0, D - a.size // rows)))


def kernel(x, c, ada_w, ada_b, norm1, w_in, conv_w, conv_b, dt_bias, A_log, D_skip, sinks, attn_out_norm, ssm_out_norm, w_o, norm2, w_gate_up, w_down, rel_bias, final_norm, loss_target, m_ada_w, m_ada_b, m_norm1, m_w_in, m_conv_w, m_conv_b, m_dt_bias, m_A_log, m_D_skip, m_sinks, m_attn_out_norm, m_ssm_out_norm, m_w_o, m_norm2, m_w_gate_up, m_w_down, m_rel_bias, m_final_norm, v_ada_w, v_ada_b, v_norm1, v_w_in, v_conv_w, v_conv_b, v_dt_bias, v_A_log, v_D_skip, v_sinks, v_attn_out_norm, v_ssm_out_norm, v_w_o, v_norm2, v_w_gate_up, v_w_down, v_rel_bias, v_final_norm):
    xi, yi, ci = lax.axis_index("x"), lax.axis_index("y"), lax.axis_index("c")
    chip = 2 * xi + yi
    me = 4 * xi + 2 * yi + ci
    where = jnp.stack([ci, chip]).astype(jnp.int32)
    xs2, tgt = x[0], loss_target[0]

    first = jnp.concatenate([c, _pad_row(conv_w[0], CK), jnp.zeros((3, D), F32)], axis=0)
    w_in_t, m_w_in_t, v_w_in_t = w_in[0].T, m_w_in[0].T, v_w_in[0].T
    w_in_b, w_o_b, w_dn_b = w_in_t.astype(BF16), w_o[0].astype(BF16), w_down[0].astype(BF16)
    w_gu_b = w_gate_up[0].astype(BF16)
    ncol = ada_w.shape[2]
    first_all, w_in_f, mod_all = _front(first, w_in_b, ada_w[0], lax.dynamic_slice(ada_b, (0, chip * ncol), (1, ncol)))
    c_all = first_all[:, 0, :]
    cw_full = jnp.concatenate([first_all[2 * j, 1:1 + CK, 0:256] for j in range(4)], axis=1)
    mod = lax.dynamic_slice(jnp.transpose(mod_all, (1, 0, 2)).reshape(8, 4 * ncol), (me, 0), (1, 4 * ncol))
    shift1, scale1, gate1, shift2, scale2, gate2 = [mod[:, j * D:(j + 1) * D] for j in range(6)]
    a1 = norm1 * (1.0 + scale1)
    a2 = norm2 * (1.0 + scale2)

    hdn = DFF // 8
    q, kv, z, xbc, dtr, w_o_g, w_dna_g = _in_proj_fwd(xs2, a1, shift1, w_in_f,
                                                      carry=_gather_chips_carry([w_o_b, w_dn_b[0:hdn]]))
    w_o_f = w_o_g.reshape(D, D)
    bucket, mask = _attn_geometry()
    bucket = jnp.asarray(bucket)
    bias = _bias_table(rel_bias, bucket, jnp.asarray(mask.astype(np.int32)))
    sinks1 = sinks[0]
    ya, w_ga_g = _attn_fwd(q, kv, bias, sinks1, attn_out_norm, carry=_gather_chips_carry([w_gu_b[0:D // 2]]))
    cw8 = jnp.concatenate([cw_full, jnp.zeros((4, XBCW), F32)], axis=0)
    dtb = _pad_row(dt_bias)[:, 0:128]
    av = _pad_row(-jnp.exp(A_log))[:, 0:128]
    dk = jnp.repeat(D_skip, HD, axis=1)
    ys, hs, pre, w_gb_g, w_dnb_g = _ssd_fwd(z, xbc, dtr, cw8, conv_b, dtb, av, dk, ssm_out_norm,
                                            carry=_gather_chips_carry([w_gu_b[D // 2:D], w_dn_b[hdn:2 * hdn]]))
    w_dn_f = jnp.stack([w_dna_g, w_dnb_g], axis=1).reshape(DFF, D)
    fn = final_norm[None, :]
    x1, gu, dx2, loss_acc, dfn = _mlp_fwd(xs2, ya, ys, tgt, w_o_f, w_ga_g, w_gb_g, w_dn_f, gate1, a2, shift2, gate2, fn)

    def to_sibling(p):
        return _Carry([p], [jax.ShapeDtypeStruct((4,) + p.shape[2:], F32)],
                      lambda x_, y_, c_: [(_SIBLING, 0, (j, 1 - c_), 0, j) for j in range(4)])

    def to_chips(s4):
        return _Carry([s4], [jax.ShapeDtypeStruct((3,) + s4.shape[1:], s4.dtype)],
                      lambda x_, y_, c_: [(f, 0, jnp.bitwise_xor(2 * x_ + y_, k + 1), 0, k) for k, f in enumerate(_CHIPS3)])

    def back(t):
        return _Carry([t[None]], [jax.ShapeDtypeStruct((1,) + t.shape, F32)], lambda x_, y_, c_: [(_SIBLING, 0, 0, 0, 0)])

    dx1, dya, dys, act, dgu, h2, dsh2, p2 = _mlp_bwd(x1, gu, dx2, w_o_f, w_ga_g, w_gb_g, w_dn_f, gate1, a2, shift2, gate2)
    p_gu = _wgrad_gate_up(h2, dgu)[0].reshape(4, 2, D // 2, 2 * DFF // 4)
    dq, dkv, dbias, dsink, dnw_attn, g_dn, dg2, got1_gu = _attn_bwd(
        q, kv, dya, bias, sinks1, attn_out_norm, act, dx2, gate2, w_dn_f, carry=to_sibling(p_gu))
    p_dn = g_dn.reshape(4, 2, DFF // 8, D)
    drel = _rel_bias_grad(dbias, bucket)
    s4_gu, own_gu = _add_half("rs_add_half_gu", p_gu, got1_gu, where)
    dz, dxbc, ddt, dcw, dcb, dnw_ssm, dhd, got2_gu, got1_dn = _ssd_bwd(
        z, xbc, pre, dtr, dys, hs, cw8, dtb, av, dk, ssm_out_norm, carry=_merge(to_chips(s4_gu), to_sibling(p_dn)))
    mine_gu = _add_chips("rs_add_chips_gu", own_gu, got2_gu)
    s4_dn, own_dn = _add_half("rs_add_half_dn", p_dn, got1_dn, where)
    grad_x, h1, dsh1, p1 = _in_proj_bwd(xs2, dx1, a1, shift1, w_in_f, dq, dkv, dz, dxbc, ddt)
    p_in, got2_dn, got3_gu = _wgrad_in_t(h1, [dq, dkv, dz, dxbc, ddt], carry=_merge(to_chips(s4_dn), back(mine_gu)))
    mine_dn = _add_chips("rs_add_chips_dn", own_dn, got2_dn)

    def to_sibling_cols(p):
        return _Carry([p], [jax.ShapeDtypeStruct(p.shape[:2] + (D // 2,), F32)],
                      lambda x_, y_, c_: [(_SIBLING, 0, (j, slice(None), pl.ds((1 - c_) * (D // 2), D // 2)), 0, j)
                                          for j in range(4)])

    no_dg1 = jnp.zeros((8, D), F32)
    small = _pack_small(dsh1, p1, dsh2, p2, no_dg1, no_dg1, dg2, norm1, norm2, scale1, scale2, dcw, dcb, dfn,
                        dnw_attn, dnw_ssm, dhd, av, dsink, drel, loss_acc)
    g_oa, dg1a, got1_in, got3_dn, small_all = _wgrad(
        "wgrad_o_attn", ya, dx1, gate1, w_o_f[0:QW],
        carry=_merge(to_sibling_cols(p_in), back(mine_dn), _gather8_carry(small)))
    s4_in, own_in = _add_half("rs_add_half_in", p_in, got1_in, where, by_cols=True)
    g_os, dg1b, got2_in = _wgrad("wgrad_o_ssm", ys, dx1, gate1, w_o_f[QW:D], carry=to_chips(s4_in))
    mine_in = _add_chips("rs_add_chips_in", own_in, got2_in)

    dg1_all, got3_in, mine_o, got3_o = _tail(g_oa, g_os, dg1a, dg1b, mine_in)
    small_all = small_all.at[:, 2, :].set(dg1_all[:, 0, :])
    small_res, loss = _small_update(
        small_all, where,
        [ada_b, norm1, conv_w, conv_b, dt_bias, A_log, D_skip, sinks, attn_out_norm, ssm_out_norm, norm2, rel_bias,
         final_norm[None, :]],
        [m_ada_b, m_norm1, m_conv_w, m_conv_b, m_dt_bias, m_A_log, m_D_skip, m_sinks, m_attn_out_norm,
         m_ssm_out_norm, m_norm2, m_rel_bias, m_final_norm[None, :]],
        [v_ada_b, v_norm1, v_conv_w, v_conv_b, v_dt_bias, v_A_log, v_D_skip, v_sinks, v_attn_out_norm,
         v_ssm_out_norm, v_norm2, v_rel_bias, v_final_norm[None, :]])
    small_out = [dict(zip(_SMALL, r)) for r in small_res]
    for r in small_out:
        r["final_norm"] = r["final_norm"][0]

    dmod_all = small_all[:, 0:6, :].reshape(8, 6 * D)
    dmod_loc = lax.dynamic_slice(dmod_all, (0, chip * ncol), (8, ncol))
    ada_out = _ada_bwd_adamw(c_all.T, dmod_loc, ada_w[0], m_ada_w[0], v_ada_w[0])

    big_gu = _adamw_halves("adamw_gate_up", mine_gu, got3_gu[0], w_gate_up, m_w_gate_up, v_w_gate_up, where)
    big_dn = _adamw_halves("adamw_down", mine_dn, got3_dn[0], w_down, m_w_down, v_w_down, where)
    big_o = _adamw_halves("adamw_o", mine_o, got3_o, w_o, m_w_o, v_w_o, where)
    big_in = [o.T[None] for o in _adamw_halves("adamw_in", mine_in, got3_in, w_in_t, m_w_in_t, v_w_in_t, where,
                                               by_cols=True)]
    big = [big_in, big_o, big_gu, big_dn]

    order = ["ada_w", "ada_b", "norm1", "w_in", "conv_w", "conv_b", "dt_bias", "A_log", "D_skip", "sinks",
             "attn_out_norm", "ssm_out_norm", "w_o", "norm2", "w_gate_up", "w_down", "rel_bias", "final_norm"]
    bigname = {"w_in": 0, "w_o": 1, "w_gate_up": 2, "w_down": 3}
    res = [loss, grad_x[None]]
    for kind in range(4):
        for nm in order:
            if nm == "ada_w":
                res.append(ada_out[kind][None])
            elif nm in bigname:
                res.append(big[bigname[nm]][kind])
            else:
                res.append(small_out[kind][nm])
    return tuple(res)
```

```python
import numpy as np
import jax
import jax.numpy as jnp
from jax import lax
from jax.experimental import pallas as pl
from jax.experimental.pallas import tpu as pltpu

F32, BF16 = jnp.float32, jnp.bfloat16
HI = lax.Precision.HIGHEST

D = 1024
QW, KVW = 512, 128
NH, HD, NKV = 8, 64, 2
SW = 512
NST = 128
XBCW = 1024
CK = 4
BLK = 128
DFF = 2816
IN_W = 2312
PROJ_W = 2432
EPS = 1e-6
NEG = -1e30
NBUCKET = 32

B1, B2, LR, AEPS, WD, STEP = 0.9, 0.999, 0.001, 1e-08, 0.01, 10

VMEM_LIMIT = 56 * 1024 * 1024

_NT = (((1,), (1,)), ((), ()))
_TN = (((0,), (0,)), ((), ()))


def _mm(a, b):
    return jnp.dot(a, b, preferred_element_type=F32)


def _mm_nt(a, b):
    return lax.dot_general(a, b, _NT, preferred_element_type=F32)


def _mm_tn(a, b):
    return lax.dot_general(a, b, _TN, preferred_element_type=F32)


def _mm_hi(a, b):
    return jnp.dot(a, b, preferred_element_type=F32, precision=HI)


def _split3(x):
    hi = x.astype(BF16)
    r = x - hi.astype(F32)
    mid = r.astype(BF16)
    lo = (r - mid.astype(F32)).astype(BF16)
    return hi, mid, lo


def _sel_r(x, e):
    hi, mid, lo = _split3(x)
    return (_mm(hi, e) + _mm(mid, e)) + _mm(lo, e)


def _sel_l(e, x):
    hi, mid, lo = _split3(x)
    return (_mm(e, hi) + _mm(e, mid)) + _mm(e, lo)


def _sig(x):
    return 1.0 / (1.0 + jnp.exp(-x))


def _cp(sem):
    return pltpu.CompilerParams(dimension_semantics=sem, vmem_limit_bytes=VMEM_LIMIT)


def _row(shape):
    nd = len(shape)
    return pl.BlockSpec(shape, lambda *_: (0,) * nd)


def _adamw(w, g, m, v):
    m = B1 * m + (1.0 - B1) * g
    v = B2 * v + (1.0 - B2) * (g * g)
    m_hat = m / (1.0 - B1 ** STEP)
    v_hat = v / (1.0 - B2 ** STEP)
    delta = -LR * (m_hat / (jnp.sqrt(v_hat) + AEPS) + WD * w)
    return delta, m, v


class _Carry:
    def __init__(self, inps, outs, copies):
        self.inps, self.outs, self.copies = list(inps), list(outs), copies
        self.n = len(copies(0, 0, 0))

    def descriptors(self, in_refs, out_refs, send_sems, recv_sems):
        x, y, c = lax.axis_index("x"), lax.axis_index("y"), lax.axis_index("c")
        out = []
        for j, (flip, a, si, o, di) in enumerate(self.copies(x, y, c)):
            if flip is None:
                out.append(pltpu.make_async_copy(in_refs[a].at[si], out_refs[o].at[di], send_sems.at[j]))
            else:
                fx, fy, fc = flip
                peer = (1 - x if fx else x, 1 - y if fy else y, 1 - c if fc else c)
                out.append(pltpu.make_async_remote_copy(
                    src_ref=in_refs[a].at[si], dst_ref=out_refs[o].at[di],
                    send_sem=send_sems.at[j], recv_sem=recv_sems.at[j],
                    device_id=peer, device_id_type=pl.DeviceIdType.MESH))
        return out


def _pcall(body, args, *, name, grid, in_specs, out_specs, out_shape, scratch_shapes=(), sem=None, nprefetch=0,
           carry=None):
    out_shape, out_specs = list(out_shape), list(out_specs)
    in_specs, scratch_shapes = list(in_specs), list(scratch_shapes)
    nin, nout, nscr = len(in_specs), len(out_shape), len(scratch_shapes)
    run = body
    if carry is not None:
        ncin, ncout = len(carry.inps), len(carry.outs)
        hbm = pl.BlockSpec(memory_space=pl.ANY)

        def run(*refs):
            pre, r = refs[:nprefetch], refs[nprefetch:]
            ins, cins = r[:nin], r[nin:nin + ncin]
            r = r[nin + ncin:]
            outs, couts = r[:nout], r[nout:nout + ncout]
            r = r[nout + ncout:]
            scr, (send_sems, recv_sems) = r[:nscr], r[nscr:]
            first = pl.program_id(0) == 0
            last = pl.program_id(0) == grid[0] - 1
            for ax in range(1, len(grid)):
                first = jnp.logical_and(first, pl.program_id(ax) == 0)
                last = jnp.logical_and(last, pl.program_id(ax) == grid[ax] - 1)

            @pl.when(first)
            def _():
                for d in carry.descriptors(cins, couts, send_sems, recv_sems):
                    d.start()

            body(*pre, *ins, *outs, *scr)

            @pl.when(last)
            def _():
                for d in carry.descriptors(cins, couts, send_sems, recv_sems):
                    d.wait()

        in_specs = in_specs + [hbm] * ncin
        out_specs = out_specs + [hbm] * ncout
        out_shape = out_shape + carry.outs
        scratch_shapes = scratch_shapes + [pltpu.SemaphoreType.DMA((carry.n,)), pltpu.SemaphoreType.DMA((carry.n,))]
        args = list(args) + carry.inps
    if sem is None:
        sem = ("arbitrary",) * len(grid)
    if nprefetch:
        kw = dict(grid_spec=pltpu.PrefetchScalarGridSpec(num_scalar_prefetch=nprefetch, grid=grid, in_specs=in_specs,
                                                         out_specs=out_specs, scratch_shapes=scratch_shapes))
    else:
        kw = dict(grid=grid, in_specs=in_specs, out_specs=out_specs, scratch_shapes=scratch_shapes)
    res = pl.pallas_call(run, name=name, out_shape=out_shape, compiler_params=_cp(sem), **kw)(*args)
    return list(res)


def _merge(*carries):
    inps, outs, offs = [], [], []
    for cr in carries:
        offs.append((len(inps), len(outs)))
        inps += cr.inps
        outs += cr.outs

    def copies(x, y, c):
        return [(f, a + io, si, o + oo, di) for cr, (io, oo) in zip(carries, offs) for f, a, si, o, di in cr.copies(x, y, c)]

    return _Carry(inps, outs, copies)


_ALL7 = [(f >> 2 & 1, f >> 1 & 1, f & 1) for f in range(1, 8)]
_CHIPS3 = [(0, 1, 0), (1, 0, 0), (1, 1, 0)]
_SIBLING = (0, 0, 1)


def _gather8_carry(blk):
    def copies(x, y, c):
        me = 4 * x + 2 * y + c
        return [(None, 0, 0, 0, me)] + [(f, 0, 0, 0, me) for f in _ALL7]

    return _Carry([blk[None]], [jax.ShapeDtypeStruct((8,) + blk.shape, blk.dtype)], copies)


def _gather_chips_carry(blks):
    def copies(x, y, c):
        chip = 2 * x + y
        return [(f, a, 0, a, chip) for a in range(len(blks)) for f in [None] + _CHIPS3]

    return _Carry([b[None] for b in blks], [jax.ShapeDtypeStruct((4,) + b.shape, b.dtype) for b in blks], copies)


def _front(first, w_in_t, w_loc, b_loc):
    n = w_loc.shape[1]
    rows = w_in_t.shape[0]
    hw = D // 2
    gather = _gather8_carry(first)
    fetch = _Carry([w_in_t], [jax.ShapeDtypeStruct((4, rows, hw), BF16)],
                   lambda x_, y_, c_: [(f, 0, (slice(None), pl.ds(c_ * hw, hw)), 0, 2 * x_ + y_) for f in [None] + _CHIPS3])
    phase_a = _merge(gather, fetch)
    send_mod = _Carry([None], [None], lambda x_, y_, c_: [(f, 0, slice(None), 0, 2 * x_ + y_) for f in [None] + _CHIPS3])
    swap = _Carry([None], [None], lambda x_, y_, c_: [(_SIBLING, 0, slice(None), 0, slice(None))])

    def body(first_hbm, w_hbm, wl_ref, bl_ref, first_all, w_full, mod_all,
             c_scr, mod_scr, w_half, w_other, wf_scr, sa, ra, sb, rb, sc, rc, sl):
        da = phase_a.descriptors([first_hbm, w_hbm], [first_all, w_half], sa, ra)
        for d in da:
            d.start()
        for d in da[:gather.n]:
            d.wait()
        cp = pltpu.make_async_copy(first_all, c_scr, sl)
        cp.start()
        cp.wait()
        cv = c_scr[:, 0, :]
        cond = cv * _sig(cv)
        for j in range(n // 512):
            cols = slice(j * 512, (j + 1) * 512)
            mod_scr[:, cols] = _mm_hi(cond, wl_ref[:, cols]) + bl_ref[:, cols]
        db = send_mod.descriptors([mod_scr], [mod_all], sb, rb)
        for d in db:
            d.start()
        for d in da[gather.n:]:
            d.wait()
        dc = swap.descriptors([w_half], [w_other], sc, rc)
        for d in dc:
            d.start()
        for d in db + dc:
            d.wait()
        core = lax.axis_index("c")
        for mine_first in (True, False):
            @pl.when(core == (0 if mine_first else 1))
            def _(mine_first=mine_first):
                lo, hi = (w_half, w_other) if mine_first else (w_other, w_half)
                for j in range(4):
                    wf_scr[j * rows:(j + 1) * rows, 0:hw] = lo[j]
                    wf_scr[j * rows:(j + 1) * rows, hw:D] = hi[j]
        wf_scr[4 * rows:, :] = jnp.zeros((PROJ_W - 4 * rows, D), BF16)
        cp = pltpu.make_async_copy(wf_scr, w_full, sl)
        cp.start()
        cp.wait()

    hbm = pl.BlockSpec(memory_space=pl.ANY)
    vmem = pl.BlockSpec(memory_space=pltpu.VMEM)
    sems = pltpu.SemaphoreType.DMA
    return pl.pallas_call(
        body, name="front",
        out_shape=[jax.ShapeDtypeStruct((8,) + first.shape, F32), jax.ShapeDtypeStruct((PROJ_W, D), BF16),
                   jax.ShapeDtypeStruct((4, 8, n), F32)],
        in_specs=[hbm, hbm, vmem, vmem], out_specs=[hbm] * 3,
        scratch_shapes=[pltpu.VMEM((8,) + first.shape, F32), pltpu.VMEM((8, n), F32),
                        pltpu.VMEM((4, rows, hw), BF16), pltpu.VMEM((4, rows, hw), BF16), pltpu.VMEM((PROJ_W, D), BF16),
                        sems((phase_a.n,)), sems((phase_a.n,)), sems((4,)), sems((4,)), sems((1,)), sems((1,)), sems],
        compiler_params=pltpu.CompilerParams(vmem_limit_bytes=VMEM_LIMIT),
    )(first[None], w_in_t, w_loc, b_loc)


def _tail(g_a, g_b, row_a, row_b, mine_in):
    rr, cc = g_a.shape[0] // 4, g_a.shape[1]

    def copies_a(x_, y_, c_):
        me = 4 * x_ + 2 * y_ + c_
        out = [(None, 0, 0, 0, me)] + [(f, 0, 0, 0, me) for f in _ALL7]
        out += [(_SIBLING, 1 + j // 2, pl.ds((j % 2) * 2 * rr + (1 - c_) * rr, rr), 1, j) for j in range(4)]
        out += [(None, 1 + j // 2, pl.ds((j % 2) * 2 * rr + c_ * rr, rr), 2, j) for j in range(4)]
        return out + [(_SIBLING, 3, slice(None), 3, slice(None))]

    phase_a = _Carry([None] * 4, [None] * 4, copies_a)
    to_chips = _Carry([None], [None], lambda x_, y_, c_: [(f, 0, jnp.bitwise_xor(2 * x_ + y_, k + 1), 0, k)
                                                        for k, f in enumerate(_CHIPS3)])
    back = _Carry([None], [None] * 2, lambda x_, y_, c_: [(None, 0, slice(None), 0, slice(None)),
                                                        (_SIBLING, 0, slice(None), 1, slice(None))])

    def body(ga_hbm, gb_hbm, ra_ref, rb_ref, in_hbm, rows_all, got_in, mine_o, got_o,
             row_scr, got1_scr, mine_scr, s4_scr, got2_scr, red_scr, sa, ra, sb, rb, sc, rc):
        chip = 2 * lax.axis_index("x") + lax.axis_index("y")
        row_scr[0] = ra_ref[0:1, :] + rb_ref[0:1, :]
        da = phase_a.descriptors([row_scr, ga_hbm, gb_hbm, in_hbm], [rows_all, got1_scr, mine_scr, got_in], sa, ra)
        for d in da:
            d.start()
        for d in da[8:16]:
            d.wait()
        for j in range(4):
            s4_scr[j] = (mine_scr[j] + got1_scr[j]).astype(BF16)
        red_scr[...] = mine_scr[chip] + got1_scr[chip]
        db = to_chips.descriptors([s4_scr], [got2_scr], sb, rb)
        for d in db:
            d.start()
        for d in db:
            d.wait()
        red_scr[...] = ((red_scr[...] + got2_scr[0].astype(F32)) + got2_scr[1].astype(F32)) + got2_scr[2].astype(F32)
        dc = back.descriptors([red_scr], [mine_o, got_o], sc, rc)
        for d in dc:
            d.start()
        for d in da[:8] + da[16:] + dc:
            d.wait()

    hbm = pl.BlockSpec(memory_space=pl.ANY)
    vmem = pl.BlockSpec(memory_space=pltpu.VMEM)
    sems = pltpu.SemaphoreType.DMA
    half = jax.ShapeDtypeStruct((rr, cc), F32)
    return pl.pallas_call(
        body, name="tail",
        out_shape=[jax.ShapeDtypeStruct((8, 1, cc), F32), jax.ShapeDtypeStruct(mine_in.shape, F32), half, half],
        in_specs=[hbm, hbm, vmem, vmem, hbm], out_specs=[hbm] * 4,
        scratch_shapes=[pltpu.VMEM((1, 1, cc), F32), pltpu.VMEM((4, rr, cc), F32), pltpu.VMEM((4, rr, cc), F32),
                        pltpu.VMEM((4, rr, cc), BF16), pltpu.VMEM((3, rr, cc), BF16), pltpu.VMEM((rr, cc), F32),
                        sems((phase_a.n,)), sems((phase_a.n,)), sems((3,)), sems((3,)), sems((2,)), sems((2,))],
        compiler_params=pltpu.CompilerParams(vmem_limit_bytes=VMEM_LIMIT),
    )(g_a, g_b, row_a, row_b, mine_in)


def _ada_bwd_adamw(c_all_t, dmod_loc, w, m, v, carry=None):
    n = w.shape[1]
    tn = 512

    def body(ct_ref, dm_ref, w_ref, m_ref, v_ref, g_ref, d_ref, mo_ref, vo_ref):
        ct = ct_ref[...]
        cond = ct * _sig(ct)
        dm = dm_ref[...]
        g = cond[:, 0:1] * dm[0:1, :]
        for b in range(1, 8):
            g = g + cond[:, b:b + 1] * dm[b:b + 1, :]
        g_ref[...] = g
        d_ref[...], mo_ref[...], vo_ref[...] = _adamw(w_ref[...], g, m_ref[...], v_ref[...])

    wspec = pl.BlockSpec((D, tn), lambda j: (0, j))
    return _pcall(
        body, [c_all_t, dmod_loc, w, m, v], name="ada_bwd_adamw", grid=(n // tn,),
        out_shape=[jax.ShapeDtypeStruct((D, n), F32)] * 4,
        in_specs=[_row((D, 8)), pl.BlockSpec((8, tn), lambda j: (0, j)), wspec, wspec, wspec],
        out_specs=[wspec] * 4, carry=carry)


def _in_proj_fwd(x, a1, sh1, w_in, carry=None):
    s = x.shape[0]
    tm = 512

    def body(x_ref, a_ref, s_ref, w_ref, q_ref, kv_ref, z_ref, xbc_ref, dt_ref):
        def norm(rows):
            xv = x_ref[rows, :]
            r = lax.rsqrt(jnp.mean(xv * xv, axis=-1, keepdims=True) + EPS)
            return (xv * r * a_ref[...] + s_ref[...]).astype(BF16)

        def project(rows, h):
            p = _mm_nt(h, w_ref[...])
            q_ref[rows, :] = p[:, 0:512].astype(BF16)
            kv_ref[rows, :] = p[:, 512:768].astype(BF16)
            z_ref[rows, :] = p[:, 768:1280]
            xbc_ref[rows, :] = p[:, 1280:2304]
            dt_ref[rows, :] = p[:, 2304:2432]

        r0, r1 = slice(0, tm // 2), slice(tm // 2, tm)
        h0 = norm(r0)
        project(r0, h0)
        project(r1, norm(r1))

    def tok(w):
        return pl.BlockSpec((tm, w), lambda i: (i, 0))

    return _pcall(
        body, [x, a1, sh1, w_in], name="in_proj_fwd", grid=(s // tm,),
        out_shape=[jax.ShapeDtypeStruct((s, QW), BF16), jax.ShapeDtypeStruct((s, 2 * KVW), BF16),
                   jax.ShapeDtypeStruct((s, SW), F32), jax.ShapeDtypeStruct((s, XBCW), F32),
                   jax.ShapeDtypeStruct((s, 128), F32)],
        in_specs=[tok(D), _row((1, D)), _row((1, D)), _row((PROJ_W, D))],
        out_specs=[tok(QW), tok(2 * KVW), tok(SW), tok(XBCW), tok(128)], carry=carry)


def _in_proj_bwd(x, dx1, a1, sh1, w_in, dq, dkv, dz, dxbc, ddt, carry=None):
    s = x.shape[0]
    tm = 512

    def body(x_ref, dx1_ref, a_ref, s_ref, w_ref, dq_ref, dkv_ref, dz_ref, dxbc_ref, ddt_ref,
             gx_ref, h_ref, dsh_ref, p_ref):
        i = pl.program_id(0)

        @pl.when(i == 0)
        def _():
            dsh_ref[...] = jnp.zeros_like(dsh_ref)
            p_ref[...] = jnp.zeros_like(p_ref)

        def gather(st):
            rows = st["rows"]
            st["dproj"] = jnp.concatenate([dq_ref[rows, :], dkv_ref[rows, :], dz_ref[rows, :], dxbc_ref[rows, :],
                                           ddt_ref[rows, :]], axis=1)

        def back(st):
            st["dh"] = _mm(st.pop("dproj"), w_ref[...])

        def norm(st):
            rows, dh = st["rows"], st.pop("dh")
            xv = x_ref[rows, :]
            r = lax.rsqrt(jnp.mean(xv * xv, axis=-1, keepdims=True) + EPS)
            xn = xv * r
            a = a_ref[...]
            h_ref[rows, :] = (xn * a + s_ref[...]).astype(BF16)
            st["dsh"] = jnp.sum(dh, axis=0, keepdims=True)
            st["p"] = jnp.sum(dh * xn, axis=0, keepdims=True)
            u = dh * a
            gx_ref[rows, :] = dx1_ref[rows, :] + r * u - xn * (r * jnp.mean(u * xn, axis=-1, keepdims=True))

        g0, g1 = [dict(rows=slice(k * (tm // 2), (k + 1) * (tm // 2))) for k in range(2)]
        for stage, st in [(gather, g0), (back, g0), (gather, g1), (norm, g0), (back, g1), (norm, g1)]:
            stage(st)
        dsh_ref[0:1, :] += g0["dsh"] + g1["dsh"]
        p_ref[0:1, :] += g0["p"] + g1["p"]

    def tok(w):
        return pl.BlockSpec((tm, w), lambda i: (i, 0))

    return _pcall(
        body, [x, dx1, a1, sh1, w_in, dq, dkv, dz, dxbc, ddt], name="in_proj_bwd", grid=(s // tm,),
        out_shape=[jax.ShapeDtypeStruct((s, D), F32), jax.ShapeDtypeStruct((s, D), BF16),
                   jax.ShapeDtypeStruct((8, D), F32), jax.ShapeDtypeStruct((8, D), F32)],
        in_specs=[tok(D), tok(D), _row((1, D)), _row((1, D)), _row((PROJ_W, D)),
                  tok(QW), tok(2 * KVW), tok(SW), tok(XBCW), tok(128)],
        out_specs=[tok(D), tok(D), _row((8, D)), _row((8, D))], carry=carry)


def _attn_geometry():
    dist = np.arange(BLK)[:, None] + BLK - np.arange(2 * BLK)[None, :]
    n = np.maximum(dist, 0)
    max_exact = NBUCKET // 2
    large = max_exact + (np.log(np.maximum(n, 1) / max_exact) / np.log(128 / max_exact)
                         * (NBUCKET - max_exact)).astype(np.int32)
    large = np.minimum(large, NBUCKET - 1)
    bucket = np.where(n < max_exact, n, large).astype(np.int32)
    mask = (dist >= 0) & (dist < 128)
    return bucket, mask


def _attn_heads(is_first, q_blk, kvw, bias_ref, sinks_ref):
    qv = q_blk * 0.125
    col = lax.broadcasted_iota(jnp.int32, (BLK, 2 * BLK), 1)
    first = jnp.where(jnp.logical_and(is_first, col < BLK), NEG, 0.0)
    groups = []
    for g in range(NKV):
        qs = jnp.concatenate([qv[:, (4 * g + r) * HD:(4 * g + r + 1) * HD] for r in range(4)], axis=0)
        kw = kvw[:, g * HD:(g + 1) * HD]
        vw = kvw[:, KVW + g * HD:KVW + (g + 1) * HD]
        sc = _mm_nt(qs, kw)
        pn, ps = [], []
        for r in range(4):
            h = 4 * g + r
            sr = sc[r * BLK:(r + 1) * BLK] + bias_ref[h] + first
            sink = sinks_ref[h]
            m = jnp.maximum(jnp.max(sr, axis=-1, keepdims=True), sink)
            p = jnp.exp(sr - m)
            es = jnp.exp(sink - m)
            inv = 1.0 / (jnp.sum(p, axis=-1, keepdims=True) + es)
            pn.append(p * inv)
            ps.append(es * inv)
        pn = jnp.concatenate(pn, axis=0)
        ps = jnp.concatenate(ps, axis=0)
        o = _mm(pn.astype(BF16), vw)
        groups.append((qs, kw, vw, pn, ps, o))
    return groups


def _unstack_heads(parts):
    return jnp.concatenate([p[r * BLK:(r + 1) * BLK] for p in parts for r in range(4)], axis=1)


NB = 2


def _attn_fwd(q, kv, bias, sinks, nw, carry=None):
    s = q.shape[0]

    def body(q_ref, kvp_ref, kvc_ref, bias_ref, sinks_ref, nw_ref, y_ref):
        t = pl.program_id(0)
        kv3 = jnp.concatenate([kvp_ref[...], kvc_ref[...]], axis=0)
        for sub in range(NB):
            rows = slice(sub * BLK, (sub + 1) * BLK)
            groups = _attn_heads(jnp.logical_and(t == 0, sub == 0), q_ref[rows, :], kv3[sub * BLK:(sub + 2) * BLK],
                                 bias_ref, sinks_ref)
            o = _unstack_heads([g[5] for g in groups])
            r = lax.rsqrt(jnp.mean(o * o, axis=-1, keepdims=True) + EPS)
            y_ref[rows, :] = (o * r * nw_ref[...]).astype(BF16)

    return _pcall(
        body, [q, kv, kv, bias, sinks, nw], name="attn_fwd", grid=(s // (NB * BLK),),
        out_shape=[jax.ShapeDtypeStruct((s, QW), BF16)],
        in_specs=[pl.BlockSpec((NB * BLK, QW), lambda t: (t, 0)),
                  pl.BlockSpec((BLK, 2 * KVW), lambda t: (jnp.maximum(NB * t - 1, 0), 0)),
                  pl.BlockSpec((NB * BLK, 2 * KVW), lambda t: (t, 0)),
                  _row((NH, BLK, 2 * BLK)),
                  pl.BlockSpec(memory_space=pltpu.SMEM),
                  _row((1, QW))],
        out_specs=[pl.BlockSpec((NB * BLK, QW), lambda t: (t, 0))], carry=carry)


def _attn_bwd(q, kv, dya, bias, sinks, nw, act, dx2, gate2, w_dn, carry=None):
    s = q.shape[0]
    nt = s // (NB * BLK)
    npiece = DFF // NB

    def body(q_ref, kvp_ref, kvc_ref, dy_ref, bias_ref, sinks_ref, nw_ref, act_ref, dx2_ref, g2_ref, wdn_ref,
             dq_ref, dkv_ref, dbias_ref, dsink_ref, dnw_ref, gdn_hbm, dg2_ref, carry_ref, held_ref, acc_ref, sem):
        t = pl.program_id(0)

        @pl.when(t == 0)
        def _():
            carry_ref[...] = jnp.zeros_like(carry_ref)
            held_ref[...] = jnp.zeros_like(held_ref)
            dbias_ref[...] = jnp.zeros_like(dbias_ref)
            dsink_ref[...] = jnp.zeros_like(dsink_ref)
            dnw_ref[...] = jnp.zeros_like(dnw_ref)
            acc_ref[...] = jnp.zeros_like(acc_ref)

        def wgrad_piece(sub):
            rows = slice(sub * npiece, (sub + 1) * npiece)
            acc_ref[rows, :] += _mm_tn(act_ref[:, rows], dx2_ref[...].astype(BF16))

        def block(sub, kv3):
            rows = slice(sub * BLK, (sub + 1) * BLK)
            groups = _attn_heads(jnp.logical_and(t == 0, sub == 0), q_ref[rows, :], kv3[sub * BLK:(sub + 2) * BLK],
                                 bias_ref, sinks_ref)
            o = _unstack_heads([g[5] for g in groups])
            r = lax.rsqrt(jnp.mean(o * o, axis=-1, keepdims=True) + EPS)
            dy = dy_ref[rows, :]
            on = o * r
            dnw_ref[0:1, :] += jnp.sum(dy * on, axis=0, keepdims=True)
            u = dy * nw_ref[...]
            do = r * u - on * (r * jnp.mean(u * on, axis=-1, keepdims=True))
            dq_parts, dk_parts, dv_parts = [], [], []
            for g, (qs, kw, vw, pn, ps, og) in enumerate(groups):
                dos = jnp.concatenate([do[:, (4 * g + r_) * HD:(4 * g + r_ + 1) * HD] for r_ in range(4)], axis=0)
                delta = jnp.sum(dos * og, axis=-1, keepdims=True)
                dp = _mm_nt(dos.astype(BF16), vw)
                ds = pn * (dp - delta)
                dsk = ps * delta
                lane = lax.broadcasted_iota(jnp.int32, (1, 128), 1)
                for r_ in range(4):
                    h = 4 * g + r_
                    dbias_ref[h] += ds[r_ * BLK:(r_ + 1) * BLK]
                    dsink_ref[0:1, :] -= jnp.where(lane == h, jnp.sum(dsk[r_ * BLK:(r_ + 1) * BLK]), 0.0)
                dsb = ds.astype(BF16)
                dq_parts.append(_mm(dsb, kw) * 0.125)
                dk_parts.append(_mm_tn(dsb, qs))
                dv_parts.append(_mm_tn(pn.astype(BF16), dos.astype(BF16)))
            dq_ref[rows, :] = _unstack_heads(dq_parts).astype(BF16)
            return jnp.concatenate(dk_parts + dv_parts, axis=1)

        @pl.when(t < nt)
        def _():
            kv3 = jnp.concatenate([kvp_ref[...], kvc_ref[...]], axis=0)
            tail = carry_ref[...]
            for sub in range(NB):
                d = block(sub, kv3)
                done = tail + d[0:BLK]
                if sub == 0:
                    dkv_ref[0:(NB - 1) * BLK, :] = held_ref[...].astype(BF16)
                    dkv_ref[(NB - 1) * BLK:NB * BLK, :] = done.astype(BF16)
                else:
                    held_ref[(sub - 1) * BLK:sub * BLK, :] = done
                tail = d[BLK:2 * BLK]
                wgrad_piece(sub)
            carry_ref[...] = tail

        @pl.when(t == nt)
        def _():
            dkv_ref[0:(NB - 1) * BLK, :] = held_ref[...].astype(BF16)
            dkv_ref[(NB - 1) * BLK:NB * BLK, :] = carry_ref[...].astype(BF16)
            acc = acc_ref[...]
            dg2_ref[...] = jnp.zeros_like(dg2_ref)
            dg2_ref[0:1, :] = jnp.sum(acc * wdn_ref[...].astype(F32), axis=0, keepdims=True)
            acc_ref[...] = acc * g2_ref[...]
            cp = pltpu.make_async_copy(acc_ref, gdn_hbm, sem)
            cp.start()
            cp.wait()

    last = nt - 1
    tile = lambda w: pl.BlockSpec((NB * BLK, w), lambda t: (jnp.minimum(t, last), 0))
    return _pcall(
        body, [q, kv, kv, dya, bias, sinks, nw, act, dx2, gate2, w_dn], name="attn_bwd", grid=(nt + 1,),
        out_shape=[jax.ShapeDtypeStruct((s, QW), BF16), jax.ShapeDtypeStruct((s, 2 * KVW), BF16),
                   jax.ShapeDtypeStruct((NH, BLK, 2 * BLK), F32), jax.ShapeDtypeStruct((NH, 128), F32),
                   jax.ShapeDtypeStruct((8, QW), F32), jax.ShapeDtypeStruct((DFF, D), F32),
                   jax.ShapeDtypeStruct((8, D), F32)],
        in_specs=[tile(QW),
                  pl.BlockSpec((BLK, 2 * KVW), lambda t: (jnp.clip(NB * t - 1, 0, NB * nt - 1), 0)),
                  tile(2 * KVW), tile(QW),
                  _row((NH, BLK, 2 * BLK)),
                  pl.BlockSpec(memory_space=pltpu.SMEM),
                  _row((1, QW)), tile(DFF), tile(D), _row((1, D)), _row((DFF, D))],
        out_specs=[tile(QW),
                   pl.BlockSpec((NB * BLK, 2 * KVW), lambda t: (jnp.maximum(t - 1, 0), 0)),
                   _row((NH, BLK, 2 * BLK)), _row((NH, 128)), _row((8, QW)),
                   pl.BlockSpec(memory_space=pl.ANY), _row((8, D))],
        scratch_shapes=[pltpu.VMEM((BLK, 2 * KVW), F32), pltpu.VMEM(((NB - 1) * BLK, 2 * KVW), F32),
                        pltpu.VMEM((DFF, D), F32), pltpu.SemaphoreType.DMA], carry=carry)


def _rel_bias_grad(dbias, bucket):
    def body(db_ref, bk_ref, o_ref):
        bk = bk_ref[...]
        lane = lax.broadcasted_iota(jnp.int32, (1, 128), 1)
        for b in range(NBUCKET):
            sel = bk == b
            row = jnp.zeros((1, 128), F32)
            for h in range(NH):
                row = row + jnp.where(lane == h, jnp.sum(jnp.where(sel, db_ref[h], 0.0)), 0.0)
            o_ref[b:b + 1, :] = row

    return pl.pallas_call(
        body, name="rel_bias_grad",
        out_shape=jax.ShapeDtypeStruct((NBUCKET, 128), F32),
    )(dbias, bucket)


def _ssd_consts():
    head_of_lane = np.arange(SW) // HD
    expand = (np.arange(128)[:, None] == head_of_lane[None, :]).astype(np.float32)
    tril = np.tril(np.ones((BLK, BLK), np.float32))
    return (jnp.asarray(expand, BF16), jnp.asarray(expand.T.copy(), BF16), jnp.asarray(tril, BF16),
            jnp.asarray(tril.T.copy(), BF16))


def _conv_pre(xc, halo, cw, cb):
    ext = jnp.concatenate([halo, xc], axis=0)
    taps = [xc if k == CK - 1 else pltpu.roll(ext, CK - 1 - k, 0)[8:8 + BLK] for k in range(CK)]
    return cb + sum(cw[k:k + 1, :] * taps[k] for k in range(CK))


def _ssd_chunk(pre, dtr, dtb, av, dkv, ex, tril, h_in):
    sp = _sig(pre)
    xbc = pre * sp
    xs, bm, cm = xbc[:, 0:SW], xbc[:, SW:SW + 2 * NST], xbc[:, SW + 2 * NST:]
    dtin = dtr + dtb
    dt = jnp.maximum(dtin, 0.0) + jnp.log1p(jnp.exp(-jnp.abs(dtin)))
    cs = _sel_l(tril, dt * av)
    cst = cs.T
    dtx = _sel_r(dt, ex)
    csx = _sel_r(cs, ex)
    xdt = xs * dtx
    csl = csx[BLK - 1:BLK, :]
    decx = jnp.exp(csl - csx)
    ecsx = jnp.exp(csx)
    ecl = jnp.exp(csl)
    causal = tril.astype(F32) > 0.5
    ydiag, yoff, cbs, lms = [], [], [], []
    for g in range(2):
        bg = bm[:, g * NST:(g + 1) * NST].astype(BF16)
        cg = cm[:, g * NST:(g + 1) * NST].astype(BF16)
        cb = _mm_nt(cg, bg)
        cbs.append(cb)
        yoff.append(_mm(cg, h_in[:, g * 256:(g + 1) * 256].astype(BF16)))
        for r in range(4):
            h = 4 * g + r
            seg = cs[:, h:h + 1] - cst[h:h + 1, :]
            lm = jnp.where(causal, jnp.exp(jnp.minimum(seg, 0.0)), 0.0)
            lms.append(lm)
            ydiag.append(_mm((cb * lm).astype(BF16), xdt[:, h * HD:(h + 1) * HD].astype(BF16)))
    yoff = jnp.concatenate(yoff, axis=1) * ecsx
    y = jnp.concatenate(ydiag, axis=1) + yoff + dkv * xs
    return dict(pre=pre, sp=sp, xs=xs, bm=bm, cm=cm, dtin=dtin, dt=dt, av=av, cs=cs, cst=cst,
                dtx=dtx, csx=csx, xdt=xdt, decx=decx, ecsx=ecsx, ecl=ecl, causal=causal, cbs=cbs, lms=lms,
                yoff=yoff, y=y)


def _group_mean(t):
    m0 = jnp.mean(t[:, 0:256], axis=-1, keepdims=True)
    m1 = jnp.mean(t[:, 256:512], axis=-1, keepdims=True)
    return jnp.concatenate([jnp.broadcast_to(m0, (t.shape[0], 256)), jnp.broadcast_to(m1, (t.shape[0], 256))], axis=1)


SUBS = 4


def _ssd_fwd(z, xbc, dtr, cw, cb, dtb, av, dk, nw, carry=None):
    s = z.shape[0]
    nc = s // BLK
    tile = SUBS * BLK
    ex, _, tril, _ = _ssd_consts()

    def body(z_ref, xc_ref, xh_ref, dtr_ref, cw_ref, cb_ref, dtb_ref, a_ref, dk_ref, nw_ref, ex_ref, tril_ref,
             y_ref, hs_ref, pre_ref, h_ref):
        t = pl.program_id(0)

        @pl.when(t == 0)
        def _():
            h_ref[...] = jnp.zeros_like(h_ref)

        h_in = h_ref[...]
        for sub in range(SUBS):
            rows = slice(sub * BLK, (sub + 1) * BLK)
            xc = xc_ref[rows, :]
            halo = jnp.where(t == 0, 0.0, xh_ref[...]) if sub == 0 else xc_ref[sub * BLK - 8:sub * BLK, :]
            pre = _conv_pre(xc, halo, cw_ref[...], cb_ref[...])
            pre_ref[rows, :] = pre
            hs_ref[sub] = h_in
            f = _ssd_chunk(pre, dtr_ref[rows, :], dtb_ref[...], a_ref[...], dk_ref[...], ex_ref[...], tril_ref[...], h_in)
            dx = (f["decx"] * f["xdt"]).astype(BF16)
            st = [_mm_tn(f["bm"][:, g * NST:(g + 1) * NST].astype(BF16), dx[:, g * 256:(g + 1) * 256]) for g in range(2)]
            h_in = h_in * f["ecl"] + jnp.concatenate(st, axis=1)
            zv = z_ref[rows, :]
            tg = f["y"] * (zv * _sig(zv))
            r = lax.rsqrt(_group_mean(tg * tg) + EPS)
            y_ref[rows, :] = (tg * r * nw_ref[...]).astype(BF16)
        h_ref[...] = h_in

    cur = lambda w: pl.BlockSpec((tile, w), lambda t: (t, 0))
    return _pcall(
        body, [z, xbc, xbc, dtr, cw, cb, dtb, av, dk, nw, ex, tril], name="ssd_fwd", grid=(s // tile,),
        out_shape=[jax.ShapeDtypeStruct((s, SW), BF16), jax.ShapeDtypeStruct((nc, NST, SW), F32),
                   jax.ShapeDtypeStruct((s, XBCW), F32)],
        in_specs=[cur(SW), cur(XBCW), pl.BlockSpec((8, XBCW), lambda t: (jnp.maximum(t * (tile // 8) - 1, 0), 0)),
                  cur(128), _row((8, XBCW)), _row((1, XBCW)), _row((1, 128)),
                  _row((1, 128)), _row((1, SW)), _row((1, SW)), _row((128, SW)), _row((BLK, BLK))],
        out_specs=[cur(SW), pl.BlockSpec((SUBS, NST, SW), lambda t: (t, 0, 0)), cur(XBCW)],
        scratch_shapes=[pltpu.VMEM((NST, SW), F32)], carry=carry)


def _ssd_bwd(z, xbc, pre_all, dtr, dys, hs, cw, dtb, av, dk, nw, carry=None):
    s = z.shape[0]
    tile = SUBS * BLK
    nt = s // tile
    ex, ext_t, tril, triu = _ssd_consts()

    def body(z_ref, xc_ref, pre_ref, dtr_ref, dy_ref, hs_ref, cw_ref, dtb_ref, a_ref, dk_ref, nw_ref,
             ex_ref, ext_ref, tril_ref, triu_ref,
             dz_ref, dxbc_ref, ddt_ref, dcw_ref, dcb_ref, dnw_ref, dhd_ref, dh_ref, nxt_ref, dd_ref):
        i = pl.program_id(0)

        @pl.when(i == 0)
        def _():
            dh_ref[...] = jnp.zeros_like(dh_ref)
            nxt_ref[...] = jnp.zeros_like(nxt_ref)
            dd_ref[...] = jnp.zeros_like(dd_ref)
            dcw_ref[...] = jnp.zeros_like(dcw_ref)
            dcb_ref[...] = jnp.zeros_like(dcb_ref)
            dnw_ref[...] = jnp.zeros_like(dnw_ref)
            dhd_ref[...] = jnp.zeros_like(dhd_ref)

        gst, nxt = dh_ref[...], nxt_ref[...]
        for sub in reversed(range(SUBS)):
            rows = slice(sub * BLK, (sub + 1) * BLK)
            gst, nxt = chunk(sub, rows, gst, nxt, z_ref, xc_ref, pre_ref, dtr_ref, dy_ref, hs_ref, cw_ref, dtb_ref,
                             a_ref, dk_ref, nw_ref, ex_ref, ext_ref, tril_ref, triu_ref,
                             dz_ref, dxbc_ref, ddt_ref, dcw_ref, dcb_ref, dnw_ref, dhd_ref, dd_ref)
        dh_ref[...] = gst
        nxt_ref[...] = nxt

        @pl.when(i == nt - 1)
        def _():
            dhd_ref[2:3, :] = _sel_r(dd_ref[...], ext_ref[...])[0:1, :]

    def chunk(sub, rows, gst, nxt, z_ref, xc_ref, pre_ref, dtr_ref, dy_ref, hs_ref, cw_ref, dtb_ref,
              a_ref, dk_ref, nw_ref, ex_ref, ext_ref, tril_ref, triu_ref,
              dz_ref, dxbc_ref, ddt_ref, dcw_ref, dcb_ref, dnw_ref, dhd_ref, dd_ref):
        h_in = hs_ref[sub]
        f = _ssd_chunk(pre_ref[rows, :], dtr_ref[rows, :], dtb_ref[...], a_ref[...], dk_ref[...], ex_ref[...],
                       tril_ref[...], h_in)
        xs, xdt, decx, ecsx, ecl, dtx = f["xs"], f["xdt"], f["decx"], f["ecsx"], f["ecl"], f["dtx"]
        cs, cst, causal = f["cs"], f["cst"], f["causal"]
        causal_t = triu_ref[...].astype(F32) > 0.5

        zv = z_ref[rows, :]
        sz = _sig(zv)
        gz = zv * sz
        t = f["y"] * gz
        r = lax.rsqrt(_group_mean(t * t) + EPS)
        tn_ = t * r
        dyn = dy_ref[rows, :]
        dnw_ref[0:1, :] += jnp.sum(dyn * tn_, axis=0, keepdims=True)
        u = dyn * nw_ref[...]
        dt_ = r * u - tn_ * (r * _group_mean(u * tn_))
        dy = dt_ * gz
        dz_ref[rows, :] = (dt_ * f["y"] * (sz * (1.0 + zv * (1.0 - sz)))).astype(BF16)

        dd_ref[0:1, :] += jnp.sum(dy * xs, axis=0, keepdims=True)
        dxs = dk_ref[...] * dy

        edy = ecsx * dy
        dxdt, dbs, dcs_, dcsx_parts, dh_new = [], [], [], [], []
        lane = lax.broadcasted_iota(jnp.int32, (1, 128), 1)
        dcs_intra = jnp.zeros((BLK, 128), F32)
        for g in range(2):
            sl = slice(g * 256, (g + 1) * 256)
            bgf, cgf = f["bm"][:, g * NST:(g + 1) * NST], f["cm"][:, g * NST:(g + 1) * NST]
            bg, cg = bgf.astype(BF16), cgf.astype(BF16)
            gg = gst[:, sl].astype(BF16)
            hg = h_in[:, sl].astype(BF16)
            edyg = edy[:, sl].astype(BF16)
            dc = _mm_nt(edyg, hg)
            dh_new.append(gst[:, sl] * ecl[:, sl] + _mm_tn(cg, edyg))
            bgm = _mm(bg, gg)
            dxdt_g = decx[:, sl] * bgm
            dxg = (decx[:, sl] * xdt[:, sl]).astype(BF16)
            db = _mm_nt(dxg, gg)
            qd = bgm * xdt[:, sl] * decx[:, sl]
            last = jnp.sum(qd, axis=0, keepdims=True) + ecl[:, sl] * jnp.sum(gst[:, sl] * h_in[:, sl], axis=0, keepdims=True)
            rowid = lax.broadcasted_iota(jnp.int32, (BLK, 256), 0)
            dcsx_parts.append(f["yoff"][:, sl] * dy[:, sl] - qd + jnp.where(rowid == BLK - 1, last, 0.0))
            cb_ = f["cbs"][g]
            cbt = _mm_nt(bg, cg)
            dcb_ = jnp.zeros((BLK, BLK), F32)
            dcbt = jnp.zeros((BLK, BLK), F32)
            dxd = []
            for r_ in range(4):
                h = 4 * g + r_
                hl = slice(h * HD, (h + 1) * HD)
                lm = f["lms"][h]
                segt = cst[h:h + 1, :] - cs[:, h:h + 1]
                lmt = jnp.where(causal_t, jnp.exp(jnp.minimum(segt, 0.0)), 0.0)
                dyh = dy[:, hl].astype(BF16)
                xdh = xdt[:, hl].astype(BF16)
                dw = _mm_nt(dyh, xdh)
                dwt = _mm_nt(xdh, dyh)
                wt = cbt * lmt
                dxd.append(_mm(wt.astype(BF16), dyh))
                dcb_ = dcb_ + dw * lm
                dcbt = dcbt + dwt * lmt
                col = jnp.sum(dw * (cb_ * lm), axis=-1, keepdims=True) - jnp.sum(dwt * wt, axis=-1, keepdims=True)
                dcs_intra = dcs_intra + jnp.where(lane == h, col, 0.0)
            dxdt.append(dxdt_g + jnp.concatenate(dxd, axis=1))
            dcs_.append(dc + _mm(dcb_.astype(BF16), bg))
            dbs.append(db + _mm(dcbt.astype(BF16), cg))
        dxdt = jnp.concatenate(dxdt, axis=1)
        dxs = dxs + dxdt * dtx
        ext_t_ = ext_ref[...]
        dcs = dcs_intra + _sel_r(jnp.concatenate(dcsx_parts, axis=1), ext_t_)
        da = _sel_l(triu_ref[...], dcs)
        ddt = da * f["av"] + _sel_r(dxdt * xs, ext_t_)
        dhd_ref[1:2, :] += jnp.sum(da * f["dt"], axis=0, keepdims=True)
        ddtr = ddt * _sig(f["dtin"])
        dhd_ref[0:1, :] += jnp.sum(ddtr, axis=0, keepdims=True)
        ddt_ref[rows, :] = ddtr.astype(BF16)

        sp, pre = f["sp"], f["pre"]
        dact = jnp.concatenate([dxs] + dbs + dcs_, axis=1)
        dpre = dact * (sp * (1.0 + pre * (1.0 - sp)))
        dcb_ref[0:1, :] += jnp.sum(dpre, axis=0, keepdims=True)
        ext2 = jnp.concatenate([dpre, nxt], axis=0)
        shifted = [pltpu.roll(ext2, BLK + 8 - (CK - 1 - k), 0)[0:BLK] for k in range(CK - 1)] + [dpre]
        cw = cw_ref[...]
        xc = xc_ref[rows, :]
        dxr = cw[CK - 1:CK, :] * dpre
        for k in range(CK):
            dcw_ref[k:k + 1, :] += jnp.sum(shifted[k] * xc, axis=0, keepdims=True)
            if k < CK - 1:
                dxr = dxr + cw[k:k + 1, :] * shifted[k]
        dxbc_ref[rows, :] = dxr.astype(BF16)
        return jnp.concatenate(dh_new, axis=1), dpre[0:8]

    cur = lambda w: pl.BlockSpec((tile, w), lambda i: (nt - 1 - i, 0))
    return _pcall(
        body, [z, xbc, pre_all, dtr, dys, hs, cw, dtb, av, dk, nw, ex, ext_t, tril, triu], name="ssd_bwd", grid=(nt,),
        out_shape=[jax.ShapeDtypeStruct((s, SW), BF16), jax.ShapeDtypeStruct((s, XBCW), BF16),
                   jax.ShapeDtypeStruct((s, 128), BF16), jax.ShapeDtypeStruct((8, XBCW), F32),
                   jax.ShapeDtypeStruct((8, XBCW), F32), jax.ShapeDtypeStruct((8, SW), F32),
                   jax.ShapeDtypeStruct((8, 128), F32)],
        in_specs=[cur(SW), cur(XBCW), cur(XBCW), cur(128), cur(SW),
                  pl.BlockSpec((SUBS, NST, SW), lambda i: (nt - 1 - i, 0, 0)),
                  _row((8, XBCW)), _row((1, 128)), _row((1, 128)), _row((1, SW)), _row((1, SW)),
                  _row((128, SW)), _row((SW, 128)), _row((BLK, BLK)), _row((BLK, BLK))],
        out_specs=[cur(SW), cur(XBCW), cur(128), _row((8, XBCW)), _row((8, XBCW)), _row((8, SW)), _row((8, 128))],
        scratch_shapes=[pltpu.VMEM((NST, SW), F32), pltpu.VMEM((8, XBCW), F32), pltpu.VMEM((8, SW), F32)], carry=carry)


def _load_once(i, pairs, sem):
    @pl.when(i == 0)
    def _():
        cps = [pltpu.make_async_copy(src, dst, sem.at[k]) for k, (src, dst) in enumerate(pairs)]
        for cp in cps:
            cp.start()
        for cp in cps:
            cp.wait()


def _mlp_fwd(x, ya, ys, tgt, w_o, w_ga, w_gb, w_dn, gate1, a2, sh2, gate2, fn):
    s = x.shape[0]
    sub_m, subs = 256, 2
    tm = sub_m * subs

    def body(x_ref, ya_ref, ys_ref, t_ref, wo_hbm, wga_hbm, wgb_hbm, wdn_hbm, g1_ref, a2_ref, s2_ref, g2_ref, fn_ref,
             x1_ref, gu_ref, dx2_ref, loss_ref, dfn_ref, wo, wga, wgb, wdn, sem):
        i = pl.program_id(0)
        _load_once(i, [(wo_hbm, wo), (wga_hbm, wga), (wgb_hbm, wgb), (wdn_hbm, wdn)], sem)

        @pl.when(i == 0)
        def _():
            loss_ref[...] = jnp.zeros_like(loss_ref)
            dfn_ref[...] = jnp.zeros_like(dfn_ref)

        def proj(st):
            st["mix"] = _mm(ya_ref[st["rows"], :], wo[0:QW, :]) + _mm(ys_ref[st["rows"], :], wo[QW:D, :])

        def norm(st):
            x1 = x_ref[st["rows"], :] + g1_ref[...] * st.pop("mix")
            x1_ref[st["rows"], :] = x1
            r2 = lax.rsqrt(jnp.mean(x1 * x1, axis=-1, keepdims=True) + EPS)
            st["x1"] = x1
            st["h2"] = (x1 * r2 * a2_ref[...] + s2_ref[...]).astype(BF16)

        def gate_up(st):
            h2 = st.pop("h2")
            ha, hb = h2[:, 0:D // 2], h2[:, D // 2:D]
            gub = jnp.concatenate([(_mm(ha, wga[j]) + _mm(hb, wgb[j])).astype(BF16) for j in range(4)], axis=1)
            gu_ref[st["rows"], :] = gub
            st["gub"] = gub

        def activate(st):
            gub = st.pop("gub")
            gv, uv = gub[:, 0:DFF].astype(F32), gub[:, DFF:].astype(F32)
            st["act"] = (gv * _sig(gv) * uv).astype(BF16)

        def down(st):
            st["ff"] = _mm(st.pop("act"), wdn[...])

        def head(st):
            x2 = st.pop("x1") + g2_ref[...] * st.pop("ff")
            r3 = lax.rsqrt(jnp.mean(x2 * x2, axis=-1, keepdims=True) + EPS)
            xn = x2 * r3
            fnv = fn_ref[...]
            err = xn * fnv - t_ref[st["rows"], :]
            st["loss"] = jnp.sum(err * err) * (0.5 / D)
            dy = err * (1.0 / D)
            st["dfn"] = jnp.sum(dy * xn, axis=0, keepdims=True)
            u = dy * fnv
            dx2_ref[st["rows"], :] = r3 * u - xn * (r3 * jnp.mean(u * xn, axis=-1, keepdims=True))

        a, b = [dict(rows=slice(k * sub_m, (k + 1) * sub_m)) for k in range(subs)]
        for stage, st in [(proj, a), (norm, a), (proj, b), (gate_up, a), (norm, b), (activate, a), (gate_up, b),
                          (down, a), (activate, b), (head, a), (down, b), (head, b)]:
            stage(st)
        loss_ref[...] += a["loss"] + b["loss"]
        dfn_ref[0:1, :] += a["dfn"] + b["dfn"]

    def tok(w):
        return pl.BlockSpec((tm, w), lambda i: (i, 0))

    hbm = pl.BlockSpec(memory_space=pl.ANY)
    return pl.pallas_call(
        body, name="mlp_fwd", grid=(s // tm,),
        out_shape=[jax.ShapeDtypeStruct((s, D), F32), jax.ShapeDtypeStruct((s, 2 * DFF), BF16),
                   jax.ShapeDtypeStruct((s, D), F32), jax.ShapeDtypeStruct((8, 128), F32),
                   jax.ShapeDtypeStruct((8, D), F32)],
        in_specs=[tok(D), tok(QW), tok(SW), tok(D), hbm, hbm, hbm, hbm,
                  _row((1, D)), _row((1, D)), _row((1, D)), _row((1, D)), _row((1, D))],
        out_specs=[tok(D), tok(2 * DFF), tok(D), _row((8, 128)), _row((8, D))],
        scratch_shapes=[pltpu.VMEM((D, D), BF16), pltpu.VMEM(w_ga.shape, BF16), pltpu.VMEM(w_gb.shape, BF16),
                        pltpu.VMEM((DFF, D), BF16), pltpu.SemaphoreType.DMA((4,))],
        compiler_params=_cp(("arbitrary",)),
    )(x, ya, ys, tgt, w_o, w_ga, w_gb, w_dn, gate1, a2, sh2, gate2, fn)


def _mlp_bwd(x1, gu, dx2, w_o, w_ga, w_gb, w_dn, gate1, a2, sh2, gate2):
    s = x1.shape[0]
    tm = 256
    nj = 2 * DFF // 4

    def body(x1_ref, gu_ref, dx2_ref, wo_hbm, wga_hbm, wgb_hbm, wdn_hbm, g1_ref, a2_ref, s2_ref, g2_ref,
             dx1_ref, dya_ref, dys_ref, act_ref, dgu_ref, h2_ref, dsh_ref, p_ref, wo, wga, wgb, wdn, sem):
        i = pl.program_id(0)
        _load_once(i, [(wo_hbm, wo), (wga_hbm, wga), (wgb_hbm, wgb), (wdn_hbm, wdn)], sem)

        @pl.when(i == 0)
        def _():
            dsh_ref[...] = jnp.zeros_like(dsh_ref)
            p_ref[...] = jnp.zeros_like(p_ref)

        dx2 = dx2_ref[...]
        dact = _mm_nt((dx2 * g2_ref[...]).astype(BF16), wdn[...])
        gub = gu_ref[...]
        gv, uv = gub[:, 0:DFF].astype(F32), gub[:, DFF:].astype(F32)
        sg = _sig(gv)
        sl = gv * sg
        act_ref[...] = (sl * uv).astype(BF16)
        dgu = jnp.concatenate([dact * uv * (sg * (1.0 + gv * (1.0 - sg))), dact * sl], axis=1).astype(BF16)
        dgu_ref[...] = dgu
        dha = sum(_mm_nt(dgu[:, j * nj:(j + 1) * nj], wga[j]) for j in range(4))
        dhb = sum(_mm_nt(dgu[:, j * nj:(j + 1) * nj], wgb[j]) for j in range(4))
        dh = jnp.concatenate([dha, dhb], axis=1)
        x1 = x1_ref[...]
        r2 = lax.rsqrt(jnp.mean(x1 * x1, axis=-1, keepdims=True) + EPS)
        xn = x1 * r2
        a2 = a2_ref[...]
        h2_ref[...] = (xn * a2 + s2_ref[...]).astype(BF16)
        dsh_ref[0:1, :] += jnp.sum(dh, axis=0, keepdims=True)
        p_ref[0:1, :] += jnp.sum(dh * xn, axis=0, keepdims=True)
        u = dh * a2
        dx1 = dx2 + r2 * u - xn * (r2 * jnp.mean(u * xn, axis=-1, keepdims=True))
        dx1_ref[...] = dx1
        dcat = _mm_nt((dx1 * g1_ref[...]).astype(BF16), wo[...])
        dya_ref[...] = dcat[:, 0:QW]
        dys_ref[...] = dcat[:, QW:D]

    def tok(w):
        return pl.BlockSpec((tm, w), lambda i: (i, 0))

    hbm = pl.BlockSpec(memory_space=pl.ANY)
    return pl.pallas_call(
        body, name="mlp_bwd", grid=(s // tm,),
        out_shape=[jax.ShapeDtypeStruct((s, D), F32), jax.ShapeDtypeStruct((s, QW), F32),
                   jax.ShapeDtypeStruct((s, SW), F32), jax.ShapeDtypeStruct((s, DFF), BF16),
                   jax.ShapeDtypeStruct((s, 2 * DFF), BF16), jax.ShapeDtypeStruct((s, D), BF16),
                   jax.ShapeDtypeStruct((8, D), F32), jax.ShapeDtypeStruct((8, D), F32)],
        in_specs=[tok(D), tok(2 * DFF), tok(D), hbm, hbm, hbm, hbm, _row((1, D)), _row((1, D)), _row((1, D)), _row((1, D))],
        out_specs=[tok(D), tok(QW), tok(SW), tok(DFF), tok(2 * DFF), tok(D), _row((8, D)), _row((8, D))],
        scratch_shapes=[pltpu.VMEM((D, D), BF16), pltpu.VMEM(w_ga.shape, BF16), pltpu.VMEM(w_gb.shape, BF16),
                        pltpu.VMEM((DFF, D), BF16), pltpu.SemaphoreType.DMA((4,))],
        compiler_params=_cp(("arbitrary",)),
    )(x1, gu, dx2, w_o, w_ga, w_gb, w_dn, gate1, a2, sh2, gate2)


def _wgrad(name, a, b, gate, w, carry=None):
    s, m = a.shape
    n = b.shape[1]
    tk = min(1024, s)
    nk = s // tk

    def body(a_ref, b_ref, g_ref, w_ref, o_hbm, dg_ref, acc_ref, sem):
        k = pl.program_id(0)

        @pl.when(k == 0)
        def _():
            acc_ref[...] = _mm_tn(a_ref[...], b_ref[...].astype(BF16))

        @pl.when(k > 0)
        def _():
            acc_ref[...] += _mm_tn(a_ref[...], b_ref[...].astype(BF16))

        @pl.when(k == nk - 1)
        def _():
            acc = acc_ref[...]
            dg_ref[...] = jnp.zeros_like(dg_ref)
            dg_ref[0:1, :] = jnp.sum(acc * w_ref[...].astype(F32), axis=0, keepdims=True)
            acc_ref[...] = acc * g_ref[...]
            cp = pltpu.make_async_copy(acc_ref, o_hbm, sem)
            cp.start()
            cp.wait()

    return _pcall(body, [a, b, gate, w], name=name, grid=(nk,),
                  out_shape=[jax.ShapeDtypeStruct((m, n), F32), jax.ShapeDtypeStruct((8, n), F32)],
                  in_specs=[pl.BlockSpec((tk, m), lambda k: (k, 0)), pl.BlockSpec((tk, n), lambda k: (k, 0)),
                            _row((1, n)), _row((m, n))],
                  out_specs=[pl.BlockSpec(memory_space=pl.ANY), _row((8, n))],
                  scratch_shapes=[pltpu.VMEM((m, n), F32), pltpu.SemaphoreType.DMA], carry=carry)


def _wgrad_gate_up(h2, dgu, carry=None):
    s = h2.shape[0]
    tk = min(1024, s)
    nk = s // tk
    n = dgu.shape[1]
    nj = n // 4

    def body(a_ref, b_ref, o_hbm, acc_ref, sems):
        k = pl.program_id(0)

        @pl.when(k == 0)
        def _():
            acc_ref[...] = _mm_tn(a_ref[...], b_ref[...])

        @pl.when(k > 0)
        def _():
            acc_ref[...] += _mm_tn(a_ref[...], b_ref[...])

        @pl.when(k == nk - 1)
        def _():
            cps = [pltpu.make_async_copy(acc_ref.at[:, pl.ds(j * nj, nj)], o_hbm.at[j], sems.at[j]) for j in range(4)]
            for cp in cps:
                cp.start()
            for cp in cps:
                cp.wait()

    return _pcall(body, [h2, dgu], name="wgrad_gate_up", grid=(nk,),
                  out_shape=[jax.ShapeDtypeStruct((4, D, nj), F32)],
                  in_specs=[pl.BlockSpec((tk, D), lambda k: (k, 0)), pl.BlockSpec((tk, n), lambda k: (k, 0))],
                  out_specs=[pl.BlockSpec(memory_space=pl.ANY)],
                  scratch_shapes=[pltpu.VMEM((D, n), F32), pltpu.SemaphoreType.DMA((4,))], carry=carry)


def _wgrad_in_t(h1, pieces, carry=None):
    s = h1.shape[0]
    rows = IN_W // 4
    tk = min(1024, s)
    nk = s // tk

    def body(a_ref, dq_ref, dkv_ref, dz_ref, dxbc_ref, ddt_ref, o_hbm, acc_ref, tr_ref, sl_ref, sem):
        k = pl.program_id(0)

        dproj = jnp.concatenate([dq_ref[...], dkv_ref[...], dz_ref[...], dxbc_ref[...], ddt_ref[...]], axis=1)

        @pl.when(k == 0)
        def _():
            acc_ref[...] = _mm_tn(a_ref[...], dproj)

        @pl.when(k > 0)
        def _():
            acc_ref[...] += _mm_tn(a_ref[...], dproj)

        @pl.when(k == nk - 1)
        def _():
            for j in range(PROJ_W // 128):
                tr_ref[j * 128:(j + 1) * 128, :] = acc_ref[:, j * 128:(j + 1) * 128].T
            for j in range(4):
                sl_ref[j] = tr_ref[j * rows:(j + 1) * rows, :]
            cp = pltpu.make_async_copy(sl_ref, o_hbm, sem)
            cp.start()
            cp.wait()

    return _pcall(body, [h1] + list(pieces), name="wgrad_in", grid=(nk,),
                  out_shape=[jax.ShapeDtypeStruct((4, rows, D), F32)],
                  in_specs=[pl.BlockSpec((tk, p.shape[1]), lambda k: (k, 0)) for p in [h1] + list(pieces)],
                  out_specs=[pl.BlockSpec(memory_space=pl.ANY)],
                  scratch_shapes=[pltpu.VMEM((D, PROJ_W), F32), pltpu.VMEM((PROJ_W, D), F32),
                                  pltpu.VMEM((4, rows, D), F32), pltpu.SemaphoreType.DMA],
                  carry=carry)


_SMALL = ["ada_b", "norm1", "conv_w", "conv_b", "dt_bias", "A_log", "D_skip", "sinks", "attn_out_norm",
          "ssm_out_norm", "norm2", "rel_bias", "final_norm"]


def _small_grad(name, gs, chip):
    if name == "ada_b":
        return jnp.concatenate([gs[j:j + 1, :] for j in range(6)], axis=1)
    if name == "conv_w":
        full = gs[7:11, :]
        out = full[:, 0:256]
        for j in range(1, 4):
            out = jnp.where(chip == j, full[:, j * 256:(j + 1) * 256], out)
        return out
    row, width = {"norm1": (6, D), "conv_b": (11, D), "norm2": (12, D), "final_norm": (13, D),
                  "attn_out_norm": (14, QW), "ssm_out_norm": (15, SW), "dt_bias": (16, NH), "A_log": (17, NH),
                  "D_skip": (18, NH), "sinks": (19, NH), "rel_bias": (24, NH)}[name]
    rows = NBUCKET if name == "rel_bias" else 1
    return gs[row:row + rows, 0:width]


def _small_update(small_all, where, ws, ms, vs):
    n = len(_SMALL)

    def body(where_ref, sa_ref, *refs):
        w_refs, m_refs, v_refs, outs = refs[:n], refs[n:2 * n], refs[2 * n:3 * n], refs[3 * n:]
        gs = sa_ref[0]
        for b in range(1, 8):
            gs = gs + sa_ref[b]
        chip = where_ref[1]
        for i, name in enumerate(_SMALL):
            g = _small_grad(name, gs, chip)
            lead = (0,) if name == "conv_w" else ()
            d, mo, vo = _adamw(w_refs[i][lead + (...,)], g, m_refs[i][lead + (...,)], v_refs[i][lead + (...,)])
            for k, val in enumerate((g, d, mo, vo)):
                outs[k * n + i][lead + (...,)] = val
        outs[4 * n][...] = gs[20:21, 0:128]

    shapes = [jax.ShapeDtypeStruct(w.shape, F32) for w in ws]
    vmem = pl.BlockSpec(memory_space=pltpu.VMEM)
    res = pl.pallas_call(
        body, name="small_update", out_shape=shapes * 4 + [jax.ShapeDtypeStruct((1, 128), F32)],
        in_specs=[pl.BlockSpec(memory_space=pltpu.SMEM)] + [vmem] * (1 + 3 * n), out_specs=[vmem] * (4 * n + 1),
    )(where, small_all, *ws, *ms, *vs)
    return [res[k * n:(k + 1) * n] for k in range(4)], res[4 * n][0, 0]


def _add_half(name, g, got, where, by_cols=False):
    rr, cc = got.shape[1:]
    if by_cols:
        mine = pl.BlockSpec((None, rr, cc), lambda i, w_ref: (i, 0, w_ref[0]))
    else:
        mine = pl.BlockSpec((None, None, rr, cc), lambda i, w_ref: (i, w_ref[0], 0, 0))

    def body(w_ref, g_ref, r_ref, o_ref, own_ref):
        s = g_ref[...] + r_ref[...]
        o_ref[...] = s.astype(BF16)

        @pl.when(pl.program_id(0) == w_ref[1])
        def _():
            own_ref[...] = s

    spec = pl.BlockSpec((None, rr, cc), lambda i, w_ref: (i, 0, 0))
    return _pcall(body, [where, g, got], name=name, grid=(4,), nprefetch=1,
                  out_shape=[jax.ShapeDtypeStruct(got.shape, BF16), jax.ShapeDtypeStruct((rr, cc), F32)],
                  in_specs=[mine, spec],
                  out_specs=[spec, pl.BlockSpec((rr, cc), lambda i, w_ref: (0, 0))])


def _add_chips(name, own, got):
    rr, cc = own.shape
    tr = rr // 2 if rr % 32 == 0 else rr

    def body(s_ref, r_ref, o_ref):
        o_ref[...] = ((s_ref[...] + r_ref[0].astype(F32)) + r_ref[1].astype(F32)) + r_ref[2].astype(F32)

    spec = pl.BlockSpec((tr, cc), lambda i: (i, 0))
    return _pcall(body, [own, got], name=name, grid=(rr // tr,), out_shape=[jax.ShapeDtypeStruct((rr, cc), F32)],
                  in_specs=[spec, pl.BlockSpec((3, tr, cc), lambda i: (0, i, 0))], out_specs=[spec])[0]


def _adamw_halves(name, mine, got, w, m, v, where, by_cols=False):
    rr, cc = mine.shape

    def body(w_ref_, t_ref, r_ref, w_ref, m_ref, v_ref, g_ref, d_ref, mo_ref, vo_ref):
        g = jnp.where(pl.program_id(0) == w_ref_[0], t_ref[...], r_ref[...])
        g_ref[...] = g
        d_ref[...], mo_ref[...], vo_ref[...] = _adamw(w_ref[...], g, m_ref[...], v_ref[...])

    if by_cols:
        grid = (2, 1)
        half = pl.BlockSpec((rr, cc), lambda h, i, w_ref_: (0, 0))
        full = pl.BlockSpec((rr, cc), lambda h, i, w_ref_: (0, h))
    else:
        tr = rr // 2
        grid = (2, 2)
        half = pl.BlockSpec((tr, cc), lambda h, i, w_ref_: (i, 0))
        full = pl.BlockSpec((None, tr, cc), lambda h, i, w_ref_: (0, 2 * h + i, 0))
    return _pcall(body, [where, mine, got, w, m, v], name=name, grid=grid, nprefetch=1,
                  out_shape=[jax.ShapeDtypeStruct(w.shape, F32)] * 4,
                  in_specs=[half, half, full, full, full], out_specs=[full] * 4)


def _bias_table(rel_bias, bucket, mask):
    def body(rb_ref, bk_ref, mk_ref, o_ref):
        bk = bk_ref[...]
        valid = mk_ref[...] > 0
        for h in range(NH):
            acc = jnp.zeros((BLK, 2 * BLK), F32)
            for b in range(NBUCKET):
                acc = jnp.where(bk == b, rb_ref[b, h], acc)
            o_ref[h] = jnp.where(valid, acc, NEG)

    vmem = pl.BlockSpec(memory_space=pltpu.VMEM)
    return pl.pallas_call(
        body, name="bias_table", out_shape=jax.ShapeDtypeStruct((NH, BLK, 2 * BLK), F32),
        in_specs=[pl.BlockSpec(memory_space=pltpu.SMEM), vmem, vmem], out_specs=vmem,
    )(rel_bias, bucket, mask)


def _pack_small(dsh1, p1, dsh2, p2, dg1a, dg1b, dg2, norm1, norm2, scale1, scale2, dcw, dcb, dfn,
                dnw_attn, dnw_ssm, dhd, av, dsink, drel, loss_acc):
    def body(dsh1_ref, p1_ref, dsh2_ref, p2_ref, dg1a_ref, dg1b_ref, dg2_ref, n1_ref, n2_ref, s1_ref, s2_ref,
             dcw_ref, dcb_ref, dfn_ref, da_ref, ds_ref, dhd_ref, av_ref, dsink_ref, drel_ref, loss_ref, o_ref):
        o_ref[...] = jnp.zeros_like(o_ref)
        p1v, p2v = p1_ref[0:1, :], p2_ref[0:1, :]
        o_ref[0:1, :] = dsh1_ref[0:1, :]
        o_ref[1:2, :] = p1v * n1_ref[...]
        o_ref[2:3, :] = dg1a_ref[0:1, :] + dg1b_ref[0:1, :]
        o_ref[3:4, :] = dsh2_ref[0:1, :]
        o_ref[4:5, :] = p2v * n2_ref[...]
        o_ref[5:6, :] = dg2_ref[0:1, :]
        o_ref[6:7, :] = p1v * (1.0 + s1_ref[...])
        o_ref[7:11, :] = dcw_ref[0:4, :]
        o_ref[11:12, :] = dcb_ref[0:1, :]
        o_ref[12:13, :] = p2v * (1.0 + s2_ref[...])
        o_ref[13:14, :] = dfn_ref[0:1, :]
        o_ref[14:15, 0:QW] = da_ref[0:1, :]
        o_ref[15:16, 0:SW] = ds_ref[0:1, :]
        o_ref[16:17, 0:128] = dhd_ref[0:1, :]
        o_ref[17:18, 0:128] = dhd_ref[1:2, :] * av_ref[...]
        o_ref[18:19, 0:128] = dhd_ref[2:3, :]
        o_ref[19:20, 0:128] = dsink_ref[0:1, :]
        o_ref[20:21, 0:128] = loss_ref[0:1, :]
        o_ref[24:56, 0:128] = drel_ref[...]

    return pl.pallas_call(body, name="pack_small", out_shape=jax.ShapeDtypeStruct((56, D), F32))(
        dsh1, p1, dsh2, p2, dg1a, dg1b, dg2, norm1, norm2, scale1, scale2, dcw, dcb, dfn,
        dnw_attn, dnw_ssm, dhd, av, dsink, drel, loss_acc)


def _pad_row(a, rows=1):
    return jnp.pad(a.reshape(rows, -1), ((0, 0), (0, D - a.size // rows)))


def kernel(x, c, ada_w, ada_b, norm1, w_in, conv_w, conv_b, dt_bias, A_log, D_skip, sinks, attn_out_norm, ssm_out_norm, w_o, norm2, w_gate_up, w_down, rel_bias, final_norm, loss_target, m_ada_w, m_ada_b, m_norm1, m_w_in, m_conv_w, m_conv_b, m_dt_bias, m_A_log, m_D_skip, m_sinks, m_attn_out_norm, m_ssm_out_norm, m_w_o, m_norm2, m_w_gate_up, m_w_down, m_rel_bias, m_final_norm, v_ada_w, v_ada_b, v_norm1, v_w_in, v_conv_w, v_conv_b, v_dt_bias, v_A_log, v_D_skip, v_sinks, v_attn_out_norm, v_ssm_out_norm, v_w_o, v_norm2, v_w_gate_up, v_w_down, v_rel_bias, v_final_norm):
    xi, yi, ci = lax.axis_index("x"), lax.axis_index("y"), lax.axis_index("c")
    chip = 2 * xi + yi
    me = 4 * xi + 2 * yi + ci
    where = jnp.stack([ci, chip]).astype(jnp.int32)
    xs2, tgt = x[0], loss_target[0]

    first = jnp.concatenate([c, _pad_row(conv_w[0], CK), jnp.zeros((3, D), F32)], axis=0)
    w_in_t, m_w_in_t, v_w_in_t = w_in[0].T, m_w_in[0].T, v_w_in[0].T
    w_in_b, w_o_b, w_dn_b = w_in_t.astype(BF16), w_o[0].astype(BF16), w_down[0].astype(BF16)
    w_gu_b = w_gate_up[0].astype(BF16)
    ncol = ada_w.shape[2]
    first_all, w_in_f, mod_all = _front(first, w_in_b, ada_w[0], lax.dynamic_slice(ada_b, (0, chip * ncol), (1, ncol)))
    c_all = first_all[:, 0, :]
    cw_full = jnp.concatenate([first_all[2 * j, 1:1 + CK, 0:256] for j in range(4)], axis=1)
    mod = lax.dynamic_slice(jnp.transpose(mod_all, (1, 0, 2)).reshape(8, 4 * ncol), (me, 0), (1, 4 * ncol))
    shift1, scale1, gate1, shift2, scale2, gate2 = [mod[:, j * D:(j + 1) * D] for j in range(6)]
    a1 = norm1 * (1.0 + scale1)
    a2 = norm2 * (1.0 + scale2)

    hdn = DFF // 8
    q, kv, z, xbc, dtr, w_o_g, w_dna_g = _in_proj_fwd(xs2, a1, shift1, w_in_f,
                                                      carry=_gather_chips_carry([w_o_b, w_dn_b[0:hdn]]))
    w_o_f = w_o_g.reshape(D, D)
    bucket, mask = _attn_geometry()
    bucket = jnp.asarray(bucket)
    bias = _bias_table(rel_bias, bucket, jnp.asarray(mask.astype(np.int32)))
    sinks1 = sinks[0]
    ya, w_ga_g = _attn_fwd(q, kv, bias, sinks1, attn_out_norm, carry=_gather_chips_carry([w_gu_b[0:D // 2]]))
    cw8 = jnp.concatenate([cw_full, jnp.zeros((4, XBCW), F32)], axis=0)
    dtb = _pad_row(dt_bias)[:, 0:128]
    av = _pad_row(-jnp.exp(A_log))[:, 0:128]
    dk = jnp.repeat(D_skip, HD, axis=1)
    ys, hs, pre, w_gb_g, w_dnb_g = _ssd_fwd(z, xbc, dtr, cw8, conv_b, dtb, av, dk, ssm_out_norm,
                                            carry=_gather_chips_carry([w_gu_b[D // 2:D], w_dn_b[hdn:2 * hdn]]))
    w_dn_f = jnp.stack([w_dna_g, w_dnb_g], axis=1).reshape(DFF, D)
    fn = final_norm[None, :]
    x1, gu, dx2, loss_acc, dfn = _mlp_fwd(xs2, ya, ys, tgt, w_o_f, w_ga_g, w_gb_g, w_dn_f, gate1, a2, shift2, gate2, fn)

    def to_sibling(p):
        return _Carry([p], [jax.ShapeDtypeStruct((4,) + p.shape[2:], F32)],
                      lambda x_, y_, c_: [(_SIBLING, 0, (j, 1 - c_), 0, j) for j in range(4)])

    def to_chips(s4):
        return _Carry([s4], [jax.ShapeDtypeStruct((3,) + s4.shape[1:], s4.dtype)],
                      lambda x_, y_, c_: [(f, 0, jnp.bitwise_xor(2 * x_ + y_, k + 1), 0, k) for k, f in enumerate(_CHIPS3)])

    def back(t):
        return _Carry([t[None]], [jax.ShapeDtypeStruct((1,) + t.shape, F32)], lambda x_, y_, c_: [(_SIBLING, 0, 0, 0, 0)])

    dx1, dya, dys, act, dgu, h2, dsh2, p2 = _mlp_bwd(x1, gu, dx2, w_o_f, w_ga_g, w_gb_g, w_dn_f, gate1, a2, shift2, gate2)
    p_gu = _wgrad_gate_up(h2, dgu)[0].reshape(4, 2, D // 2, 2 * DFF // 4)
    dq, dkv, dbias, dsink, dnw_attn, g_dn, dg2, got1_gu = _attn_bwd(
        q, kv, dya, bias, sinks1, attn_out_norm, act, dx2, gate2, w_dn_f, carry=to_sibling(p_gu))
    p_dn = g_dn.reshape(4, 2, DFF // 8, D)
    drel = _rel_bias_grad(dbias, bucket)
    s4_gu, own_gu = _add_half("rs_add_half_gu", p_gu, got1_gu, where)
    dz, dxbc, ddt, dcw, dcb, dnw_ssm, dhd, got2_gu, got1_dn = _ssd_bwd(
        z, xbc, pre, dtr, dys, hs, cw8, dtb, av, dk, ssm_out_norm, carry=_merge(to_chips(s4_gu), to_sibling(p_dn)))
    mine_gu = _add_chips("rs_add_chips_gu", own_gu, got2_gu)
    s4_dn, own_dn = _add_half("rs_add_half_dn", p_dn, got1_dn, where)
    grad_x, h1, dsh1, p1 = _in_proj_bwd(xs2, dx1, a1, shift1, w_in_f, dq, dkv, dz, dxbc, ddt)
    p_in, got2_dn, got3_gu = _wgrad_in_t(h1, [dq, dkv, dz, dxbc, ddt], carry=_merge(to_chips(s4_dn), back(mine_gu)))
    mine_dn = _add_chips("rs_add_chips_dn", own_dn, got2_dn)

    def to_sibling_cols(p):
        return _Carry([p], [jax.ShapeDtypeStruct(p.shape[:2] + (D // 2,), F32)],
                      lambda x_, y_, c_: [(_SIBLING, 0, (j, slice(None), pl.ds((1 - c_) * (D // 2), D // 2)), 0, j)
                                          for j in range(4)])

    no_dg1 = jnp.zeros((8, D), F32)
    small = _pack_small(dsh1, p1, dsh2, p2, no_dg1, no_dg1, dg2, norm1, norm2, scale1, scale2, dcw, dcb, dfn,
                        dnw_attn, dnw_ssm, dhd, av, dsink, drel, loss_acc)
    g_oa, dg1a, got1_in, got3_dn, small_all = _wgrad(
        "wgrad_o_attn", ya, dx1, gate1, w_o_f[0:QW],
        carry=_merge(to_sibling_cols(p_in), back(mine_dn), _gather8_carry(small)))
    s4_in, own_in = _add_half("rs_add_half_in", p_in, got1_in, where, by_cols=True)
    g_os, dg1b, got2_in = _wgrad("wgrad_o_ssm", ys, dx1, gate1, w_o_f[QW:D], carry=to_chips(s4_in))
    mine_in = _add_chips("rs_add_chips_in", own_in, got2_in)

    dg1_all, got3_in, mine_o, got3_o = _tail(g_oa, g_os, dg1a, dg1b, mine_in)
    small_all = small_all.at[:, 2, :].set(dg1_all[:, 0, :])
    small_res, loss = _small_update(
        small_all, where,
        [ada_b, norm1, conv_w, conv_b, dt_bias, A_log, D_skip, sinks, attn_out_norm, ssm_out_norm, norm2, rel_bias,
         final_norm[None, :]],
        [m_ada_b, m_norm1, m_conv_w, m_conv_b, m_dt_bias, m_A_log, m_D_skip, m_sinks, m_attn_out_norm,
         m_ssm_out_norm, m_norm2, m_rel_bias, m_final_norm[None, :]],
        [v_ada_b, v_norm1, v_conv_w, v_conv_b, v_dt_bias, v_A_log, v_D_skip, v_sinks, v_attn_out_norm,
         v_ssm_out_norm, v_norm2, v_rel_bias, v_final_norm[None, :]])
    small_out = [dict(zip(_SMALL, r)) for r in small_res]
    for r in small_out:
        r["final_norm"] = r["final_norm"][0]

    dmod_all = small_all[:, 0:6, :].reshape(8, 6 * D)
    dmod_loc = lax.dynamic_slice(dmod_all, (0, chip * ncol), (8, ncol))
    ada_out = _ada_bwd_adamw(c_all.T, dmod_loc, ada_w[0], m_ada_w[0], v_ada_w[0])

    big_gu = _adamw_halves("adamw_gate_up", mine_gu, got3_gu[0], w_gate_up, m_w_gate_up, v_w_gate_up, where)
    big_dn = _adamw_halves("adamw_down", mine_dn, got3_dn[0], w_down, m_w_down, v_w_down, where)
    big_o = _adamw_halves("adamw_o", mine_o, got3_o, w_o, m_w_o, v_w_o, where)
    big_in = [o.T[None] for o in _adamw_halves("adamw_in", mine_in, got3_in, w_in_t, m_w_in_t, v_w_in_t, where,
                                               by_cols=True)]
    big = [big_in, big_o, big_gu, big_dn]

    order = ["ada_w", "ada_b", "norm1", "w_in", "conv_w", "conv_b", "dt_bias", "A_log", "D_skip", "sinks",
             "attn_out_norm", "ssm_out_norm", "w_o", "norm2", "w_gate_up", "w_down", "rel_bias", "final_norm"]
    bigname = {"w_in": 0, "w_o": 1, "w_gate_up": 2, "w_down": 3}
    res = [loss, grad_x[None]]
    for kind in range(4):
        for nm in order:
            if nm == "ada_w":
                res.append(ada_out[kind][None])
            elif nm in bigname:
                res.append(big[bigname[nm]][kind])
            else:
                res.append(small_out[kind][nm])
    return tuple(res)
```
